```python
import jax, jax.numpy as jnp
from jax import lax
import numpy as np

D_MODEL = 1024
BATCH = 8
SEQ = 4096
DEPTH = 1

N_META = 16
ATTN_HEADS = 8
HEAD_DIM = D_MODEL // 16
D_ATTN = ATTN_HEADS * HEAD_DIM
CONV_GROUPS = 8
D_CONV = D_MODEL // 2
CONV_WIDTH = 3
D_MIX = D_ATTN + D_CONV
Q_BLOCK = 128
EPS = 1e-6
SPLIT_SIZES = (D_ATTN, D_ATTN, D_ATTN, ATTN_HEADS, D_ATTN, D_CONV, D_CONV, D_CONV, D_CONV)
D_IN = sum(SPLIT_SIZES)
SPLIT_POINTS = tuple(int(s) for s in np.cumsum(SPLIT_SIZES)[:-1])

kernel_name = "hymba_fox_shortconv_hybrid"


def _rmsnorm(x, g):
    xf = x.astype(jnp.float32)
    y = xf * lax.rsqrt(jnp.mean(xf * xf, axis=-1, keepdims=True) + EPS)
    return (y * g.astype(jnp.float32)).astype(x.dtype)


def _group_rmsnorm(y, g, n_groups):
    lead = y.shape[:-1]
    c = y.shape[-1]
    yf = y.astype(jnp.float32).reshape(lead + (n_groups, c // n_groups))
    yf = yf * lax.rsqrt(jnp.mean(yf * yf, axis=-1, keepdims=True) + EPS)
    return (yf.reshape(lead + (c,)) * g.astype(jnp.float32)).astype(y.dtype)


def _fox_attention(q, k, v, cum_logf):
    b, l, h, dh = q.shape
    scale = dh ** -0.5
    key_pos = jnp.arange(l)
    c_keys = jnp.transpose(cum_logf, (0, 2, 1))

    def block(args):
        q_blk, c_blk, t_blk = args
        s = jnp.einsum('bqhd,bkhd->bhqk', q_blk, k,
                       preferred_element_type=jnp.float32) * scale
        s = s + jnp.transpose(c_blk, (0, 2, 1))[..., :, None] - c_keys[:, :, None, :]
        s = jnp.where(key_pos[None, :] <= t_blk[:, None], s, -jnp.inf)
        p = jax.nn.softmax(s, axis=-1)
        return jnp.einsum('bhqk,bkhd->bqhd', p.astype(v.dtype), v)

    out_meta = block((q[:, :N_META], cum_logf[:, :N_META], key_pos[:N_META]))
    n_blk = (l - N_META) // Q_BLOCK
    q_r = jnp.transpose(q[:, N_META:].reshape(b, n_blk, Q_BLOCK, h, dh), (1, 0, 2, 3, 4))
    c_r = jnp.transpose(cum_logf[:, N_META:].reshape(b, n_blk, Q_BLOCK, h), (1, 0, 2, 3))
    t_r = key_pos[N_META:].reshape(n_blk, Q_BLOCK)
    out_r = lax.map(block, (q_r, c_r, t_r))
    out_r = jnp.transpose(out_r, (1, 0, 2, 3, 4)).reshape(b, l - N_META, h, dh)
    return jnp.concatenate([out_meta, out_r], axis=1)


def _causal_depthwise_conv(x, w):
    c = x.shape[-1]
    return lax.conv_general_dilated(
        x, w.reshape(CONV_WIDTH, 1, c).astype(x.dtype),
        window_strides=(1,), padding=[(CONV_WIDTH - 1, 0)],
        dimension_numbers=('NWC', 'WIO', 'NWC'), feature_group_count=c)


def _hybrid_layer(h, norm_g, w_in, b_f, conv_w, attn_norm_g, conv_norm_g, w_out):
    b, l, _ = h.shape
    u = _rmsnorm(h, norm_g)
    proj = jnp.einsum('bld,de->ble', u, w_in)
    q, k, v, f_logit, z_attn, gate_b, gate_c, xc, z_conv = jnp.split(proj, SPLIT_POINTS, axis=-1)

    log_f = jax.nn.log_sigmoid(f_logit.astype(jnp.float32) + b_f.astype(jnp.float32))
    cum_logf = jnp.cumsum(log_f, axis=1)
    shp = (b, l, ATTN_HEADS, HEAD_DIM)
    attn = _fox_attention(q.reshape(shp), k.reshape(shp), v.reshape(shp), cum_logf)
    y_attn = _group_rmsnorm(attn.reshape(b, l, D_ATTN), attn_norm_g, ATTN_HEADS) * jax.nn.silu(z_attn)

    conv = _causal_depthwise_conv(gate_c * xc, conv_w)
    y_conv = _group_rmsnorm(gate_b * conv, conv_norm_g, CONV_GROUPS) * jax.nn.silu(z_conv)

    mix = jnp.concatenate([y_attn, y_conv], axis=-1)
    return h + jnp.einsum('ble,ed->bld', mix, w_out)


def _fwd_setup_inputs(seed: int = 0) -> dict:
    key = jax.random.key(seed)
    ks = jax.random.split(key, 10)
    f32 = jnp.float32
    x = jax.random.normal(ks[0], (BATCH, SEQ, D_MODEL), f32)
    meta = jax.random.normal(ks[1], (N_META, D_MODEL), f32)
    norm_g = 1.0 + 0.02 * jax.random.normal(ks[2], (DEPTH, D_MODEL), f32)
    w_in = jax.random.normal(ks[3], (DEPTH, D_MODEL, D_IN), f32) * D_MODEL ** -0.5
    b_f = jax.random.uniform(ks[4], (DEPTH, ATTN_HEADS), f32, minval=1.0, maxval=5.0)
    conv_w = jax.random.normal(ks[5], (DEPTH, CONV_WIDTH, D_CONV), f32) * CONV_WIDTH ** -0.5
    attn_norm_g = 1.0 + 0.02 * jax.random.normal(ks[6], (DEPTH, D_ATTN), f32)
    conv_norm_g = 1.0 + 0.02 * jax.random.normal(ks[7], (DEPTH, D_CONV), f32)
    w_out = jax.random.normal(ks[8], (DEPTH, D_MIX, D_MODEL), f32) * D_MIX ** -0.5
    final_norm_g = 1.0 + 0.02 * jax.random.normal(ks[9], (D_MODEL,), f32)
    return {"x": x, "meta": meta, "norm_g": norm_g, "w_in": w_in, "b_f": b_f,
            "conv_w": conv_w, "attn_norm_g": attn_norm_g, "conv_norm_g": conv_norm_g,
            "w_out": w_out, "final_norm_g": final_norm_g}


def _fwd_reference(x, meta, norm_g, w_in, b_f, conv_w, attn_norm_g, conv_norm_g, w_out, final_norm_g):
    b = x.shape[0]
    meta_b = jnp.broadcast_to(meta.astype(x.dtype)[None], (b, N_META, x.shape[-1]))
    h = jnp.concatenate([meta_b, x], axis=1)
    for layer in range(DEPTH):
        h = _hybrid_layer(h, norm_g[layer], w_in[layer], b_f[layer], conv_w[layer],
                          attn_norm_g[layer], conv_norm_g[layer], w_out[layer])
    return _rmsnorm(h[:, N_META:], final_norm_g)


import jax as _jax
import jax.numpy as _jnp

TWIN_FORMAT = 'train_step'
FWD_PARAMS = ['x', 'meta', 'norm_g', 'w_in', 'b_f', 'conv_w', 'attn_norm_g', 'conv_norm_g', 'w_out', 'final_norm_g']
TWIN_WEIGHTS = ['meta', 'norm_g', 'w_in', 'b_f', 'conv_w', 'attn_norm_g', 'conv_norm_g', 'w_out', 'final_norm_g']
TWIN_DIFF_INPUT = 'x'
TWIN_INPUTS = ['x', 'meta', 'norm_g', 'w_in', 'b_f', 'conv_w', 'attn_norm_g', 'conv_norm_g', 'w_out', 'final_norm_g', 'loss_target', 'm_meta', 'm_norm_g', 'm_w_in', 'm_b_f', 'm_conv_w', 'm_attn_norm_g', 'm_conv_norm_g', 'm_w_out', 'm_final_norm_g', 'v_meta', 'v_norm_g', 'v_w_in', 'v_b_f', 'v_conv_w', 'v_attn_norm_g', 'v_conv_norm_g', 'v_w_out', 'v_final_norm_g']
TWIN_OUTPUTS = ['loss', 'grad_x', 'grad_meta', 'grad_norm_g', 'grad_w_in', 'grad_b_f', 'grad_conv_w', 'grad_attn_norm_g', 'grad_conv_norm_g', 'grad_w_out', 'grad_final_norm_g', 'delta_meta', 'delta_norm_g', 'delta_w_in', 'delta_b_f', 'delta_conv_w', 'delta_attn_norm_g', 'delta_conv_norm_g', 'delta_w_out', 'delta_final_norm_g', 'new_m_meta', 'new_m_norm_g', 'new_m_w_in', 'new_m_b_f', 'new_m_conv_w', 'new_m_attn_norm_g', 'new_m_conv_norm_g', 'new_m_w_out', 'new_m_final_norm_g', 'new_v_meta', 'new_v_norm_g', 'new_v_w_in', 'new_v_b_f', 'new_v_conv_w', 'new_v_attn_norm_g', 'new_v_conv_norm_g', 'new_v_w_out', 'new_v_final_norm_g']
TWIN_LEAF_KINDS = {'loss': 'loss', 'grad_x': 'grad_x', 'grad_meta': 'grad_w', 'grad_norm_g': 'grad_w', 'grad_w_in': 'grad_w', 'grad_b_f': 'grad_w', 'grad_conv_w': 'grad_w', 'grad_attn_norm_g': 'grad_w', 'grad_conv_norm_g': 'grad_w', 'grad_w_out': 'grad_w', 'grad_final_norm_g': 'grad_w', 'delta_meta': 'delta_w', 'delta_norm_g': 'delta_w', 'delta_w_in': 'delta_w', 'delta_b_f': 'delta_w', 'delta_conv_w': 'delta_w', 'delta_attn_norm_g': 'delta_w', 'delta_conv_norm_g': 'delta_w', 'delta_w_out': 'delta_w', 'delta_final_norm_g': 'delta_w', 'new_m_meta': 'new_m', 'new_m_norm_g': 'new_m', 'new_m_w_in': 'new_m', 'new_m_b_f': 'new_m', 'new_m_conv_w': 'new_m', 'new_m_attn_norm_g': 'new_m', 'new_m_conv_norm_g': 'new_m', 'new_m_w_out': 'new_m', 'new_m_final_norm_g': 'new_m', 'new_v_meta': 'new_v', 'new_v_norm_g': 'new_v', 'new_v_w_in': 'new_v', 'new_v_b_f': 'new_v', 'new_v_conv_w': 'new_v', 'new_v_attn_norm_g': 'new_v', 'new_v_conv_norm_g': 'new_v', 'new_v_w_out': 'new_v', 'new_v_final_norm_g': 'new_v'}


def _forward(args):
    return _fwd_reference(*[args[k] for k in FWD_PARAMS])


def _output_shape():
    out = _jax.eval_shape(lambda: _forward(_fwd_setup_inputs(0)))
    return out.shape, out.dtype

N_MICROBATCH = 1
ADAM_LR = 0.001
ADAM_B1 = 0.9
ADAM_B2 = 0.999
ADAM_EPS = 1e-08
ADAM_WD = 0.01
ADAM_STEP = 10
PER_EXAMPLE_BATCH_AXIS = {'x': 0, 'loss_target': 0}
SHARED_INPUTS = []
_WEIGHT_DTYPES = {'meta': _jnp.float32, 'norm_g': _jnp.float32, 'w_in': _jnp.float32, 'b_f': _jnp.float32, 'conv_w': _jnp.float32, 'attn_norm_g': _jnp.float32, 'conv_norm_g': _jnp.float32, 'w_out': _jnp.float32, 'final_norm_g': _jnp.float32}
MOMENT_SCALE = {'meta': 3.935765e-03, 'norm_g': 1.841572e-01, 'w_in': 8.984132e-02, 'b_f': 3.425286e-01, 'conv_w': 9.440184e-02, 'attn_norm_g': 9.346652e-02, 'conv_norm_g': 1.085157e-01, 'w_out': 9.055560e-02, 'final_norm_g': 3.195487e+01}


def _to_microbatches(a, axis):
    t = _jnp.moveaxis(a, axis, 0)
    t = t.reshape((N_MICROBATCH, t.shape[0] // N_MICROBATCH) + t.shape[1:])
    return _jnp.moveaxis(t, 1, axis + 1)


def setup_inputs(seed: int = 0) -> dict:
    inp = _fwd_setup_inputs(seed)
    key = _jax.random.fold_in(_jax.random.key(seed), 7919)
    shape, _ = _output_shape()
    out = dict(inp)
    out["loss_target"] = _jax.random.normal(_jax.random.fold_in(key, 0), shape, _jnp.float32)
    for i, name in enumerate(TWIN_WEIGHTS):
        w = inp[name].astype(_jnp.float32)
        if MOMENT_SCALE is None:
            s = _jnp.sqrt(_jnp.mean(_jnp.square(w)) + 1e-30)
        else:
            s = MOMENT_SCALE[name]
        km, kv = _jax.random.split(_jax.random.fold_in(key, i + 1))
        out[name] = w
        out["m_" + name] = s * _jax.random.normal(km, w.shape, _jnp.float32)
        out["v_" + name] = (s * s) * _jax.random.uniform(kv, w.shape, _jnp.float32, 0.5, 1.5)
    if N_MICROBATCH > 1:
        for name, axis in PER_EXAMPLE_BATCH_AXIS.items():
            out[name] = _to_microbatches(out[name], axis)
    return {'x': out['x'], 'meta': out['meta'], 'norm_g': out['norm_g'], 'w_in': out['w_in'], 'b_f': out['b_f'], 'conv_w': out['conv_w'], 'attn_norm_g': out['attn_norm_g'], 'conv_norm_g': out['conv_norm_g'], 'w_out': out['w_out'], 'final_norm_g': out['final_norm_g'], 'loss_target': out['loss_target'], 'm_meta': out['m_meta'], 'm_norm_g': out['m_norm_g'], 'm_w_in': out['m_w_in'], 'm_b_f': out['m_b_f'], 'm_conv_w': out['m_conv_w'], 'm_attn_norm_g': out['m_attn_norm_g'], 'm_conv_norm_g': out['m_conv_norm_g'], 'm_w_out': out['m_w_out'], 'm_final_norm_g': out['m_final_norm_g'], 'v_meta': out['v_meta'], 'v_norm_g': out['v_norm_g'], 'v_w_in': out['v_w_in'], 'v_b_f': out['v_b_f'], 'v_conv_w': out['v_conv_w'], 'v_attn_norm_g': out['v_attn_norm_g'], 'v_conv_norm_g': out['v_conv_norm_g'], 'v_w_out': out['v_w_out'], 'v_final_norm_g': out['v_final_norm_g']}


def _loss(weights, diff, rest, loss_target):
    with _jax.named_scope("forward"):
        args = {**rest, TWIN_DIFF_INPUT: diff, **{k: w.astype(_WEIGHT_DTYPES[k]) for k, w in weights.items()}}
        y = _forward(args)
    with _jax.named_scope("loss_head"):
        err = _jnp.square(y.astype(_jnp.float32) - loss_target)
        return 0.5 * _jnp.sum(_jnp.mean(err, axis=-1)) if err.ndim else 0.5 * err


def _adamw(w, g, m, v):
    m = ADAM_B1 * m + (1.0 - ADAM_B1) * g
    v = ADAM_B2 * v + (1.0 - ADAM_B2) * _jnp.square(g)
    m_hat = m / (1.0 - ADAM_B1 ** ADAM_STEP)
    v_hat = v / (1.0 - ADAM_B2 ** ADAM_STEP)
    delta = -ADAM_LR * (m_hat / (_jnp.sqrt(v_hat) + ADAM_EPS) + ADAM_WD * w)
    return delta, m, v


def reference(x, meta, norm_g, w_in, b_f, conv_w, attn_norm_g, conv_norm_g, w_out, final_norm_g, loss_target, m_meta, m_norm_g, m_w_in, m_b_f, m_conv_w, m_attn_norm_g, m_conv_norm_g, m_w_out, m_final_norm_g, v_meta, v_norm_g, v_w_in, v_b_f, v_conv_w, v_attn_norm_g, v_conv_norm_g, v_w_out, v_final_norm_g):
    given = dict(x=x, meta=meta, norm_g=norm_g, w_in=w_in, b_f=b_f, conv_w=conv_w, attn_norm_g=attn_norm_g, conv_norm_g=conv_norm_g, w_out=w_out, final_norm_g=final_norm_g, loss_target=loss_target, m_meta=m_meta, m_norm_g=m_norm_g, m_w_in=m_w_in, m_b_f=m_b_f, m_conv_w=m_conv_w, m_attn_norm_g=m_attn_norm_g, m_conv_norm_g=m_conv_norm_g, m_w_out=m_w_out, m_final_norm_g=m_final_norm_g, v_meta=v_meta, v_norm_g=v_norm_g, v_w_in=v_w_in, v_b_f=v_b_f, v_conv_w=v_conv_w, v_attn_norm_g=v_attn_norm_g, v_conv_norm_g=v_conv_norm_g, v_w_out=v_w_out, v_final_norm_g=v_final_norm_g)
    weights = {n: given[n] for n in TWIN_WEIGHTS}
    shared = {n: given[n] for n in SHARED_INPUTS}
    per_example = {n: given[n] for n in ['x']}
    grad_fn = _jax.value_and_grad(_loss, argnums=(0, 1))

    def one_microbatch(ex, loss_target):
        ex = dict(ex)
        diff = ex.pop(TWIN_DIFF_INPUT)
        return grad_fn(weights, diff, {**shared, **ex}, loss_target)

    if N_MICROBATCH == 1:
        loss, (grad_w, grad_x) = one_microbatch(per_example, given["loss_target"])
    else:
        def body(carry, xs):
            loss_sum, grad_sum = carry
            l_k, (gw_k, gx_k) = one_microbatch(xs[0], xs[1])
            with _jax.named_scope("update"):
                return (loss_sum + l_k, _jax.tree.map(_jnp.add, grad_sum, gw_k)), gx_k

        init = (_jnp.zeros((), _jnp.float32), _jax.tree.map(_jnp.zeros_like, weights))
        (loss, grad_w), grad_x = _jax.lax.scan(body, init, (per_example, given["loss_target"]))
    with _jax.named_scope("update"):
        delta_w, new_m, new_v = {}, {}, {}
        for n in TWIN_WEIGHTS:
            delta_w[n], new_m[n], new_v[n] = _adamw(weights[n], grad_w[n], given["m_" + n], given["v_" + n])
    return (loss, grad_x, *[grad_w[n] for n in TWIN_WEIGHTS], *[delta_w[n] for n in TWIN_WEIGHTS],
            *[new_m[n] for n in TWIN_WEIGHTS], *[new_v[n] for n in TWIN_WEIGHTS])
```

```python
import functools

import jax
import jax.numpy as jnp
from jax import lax
from jax.experimental import pallas as pl
from jax.experimental.pallas import tpu as pltpu

F32 = jnp.float32
MXU_DTYPE = jnp.bfloat16

D_MODEL = 1024
N_META = 16
HEADS = 8
HEAD_DIM = 64
D_ATTN = HEADS * HEAD_DIM
D_CONV = 512
EPS = 1e-6
LANE = 128
SUBLANE = 8
ROW_TILE = 384
FRONT = LANE
PAD_ROWS = FRONT - N_META
NEG = -1e30
N_CHIPS = 4
N_DEV = 8
VMEM_LIMIT_BYTES = 60 * 1024 * 1024

SEG_Q, SEG_K, SEG_V, SEG_F, SEG_ZA, SEG_GB, SEG_GC, SEG_XC, SEG_ZC = (
    0, 512, 1024, 1536, 1664, 2176, 2688, 3200, 3712)
D_IN = 4104
D_IN_PAD = 4224
F_END = 1544
GW_COL_TILE = 1408

ADAM_LR = 0.001
ADAM_B1 = 0.9
ADAM_B2 = 0.999
ADAM_EPS = 1e-08
ADAM_WD = 0.01
ADAM_STEP = 10

MESH = pl.DeviceIdType.MESH
ANY = pl.BlockSpec(memory_space=pl.ANY)

PACK_ROWS = 32
SLOT_NORM = (0, 1, 0, 1024)
SLOT_FINAL = (1, 2, 0, 1024)
SLOT_ATTN = (2, 3, 0, 512)
SLOT_CONVG = (2, 3, 512, 1024)
SLOT_BF = (3, 4, 0, 8)
SLOT_META = (8, 24, 0, 256)
SLOT_CONVW = (24, 27, 0, 128)
LOSS_ROW = 4


def _params(sem=None):
    return pltpu.CompilerParams(dimension_semantics=sem, vmem_limit_bytes=VMEM_LIMIT_BYTES)


def _sigmoid(z):
    return 1.0 / (1.0 + jnp.exp(-z))


def _dot(a, b):
    return jnp.dot(a, b, preferred_element_type=F32)


def _dot_nt(a, b):
    return lax.dot_general(a, b, (((1,), (1,)), ((), ())), preferred_element_type=F32)


def _dot_exact(a, b):
    return jnp.dot(a, b, preferred_element_type=F32, precision=lax.Precision.HIGHEST)


def _group_matrix():
    r = lax.broadcasted_iota(jnp.int32, (D_ATTN, D_ATTN), 0) >> 6
    c = lax.broadcasted_iota(jnp.int32, (D_ATTN, D_ATTN), 1) >> 6
    return jnp.where(r == c, 1.0, 0.0).astype(MXU_DTYPE)


def _group_sum(x, gmat):
    hi = x.astype(MXU_DTYPE)
    lo = (x - hi.astype(F32)).astype(MXU_DTYPE)
    return _dot(hi, gmat) + _dot(lo, gmat)


def _x_block_specs(n_sub, rows):
    specs = [pl.BlockSpec((rows, D_MODEL), lambda i: (jnp.maximum(n_sub * i - 1, 0), 0))]
    for b in range(1, n_sub):
        specs.append(pl.BlockSpec((rows, D_MODEL), functools.partial(lambda i, b: (n_sub * i - 1 + b, 0), b=b)))
    return specs


def _position():
    return lax.axis_index("x"), lax.axis_index("y"), lax.axis_index("c")


def _gather_weights(wi, wo, small):
    def body(wi_ref, wo_ref, sm_ref, gwi_ref, gwo_ref, gsm_ref, send_sems, recv_sems, local_sems):
        x, y, c = _position()
        me = 2 * x + y
        sibling = (x, y, 1 - c)
        chips = [(1 - x, y), (x, 1 - y), (1 - x, 1 - y)]
        locals_ = [
            pltpu.make_async_copy(wi_ref, gwi_ref.at[me], local_sems.at[0]),
            pltpu.make_async_copy(wo_ref, gwo_ref.at[me], local_sems.at[1]),
            pltpu.make_async_copy(sm_ref, gsm_ref.at[me], local_sems.at[2]),
        ]
        for cp in locals_:
            cp.start()

        def remote(k, src, dst, to):
            return pltpu.make_async_remote_copy(src_ref=src, dst_ref=dst, send_sem=send_sems.at[k],
                                                recv_sem=recv_sems.at[k], device_id=to, device_id_type=MESH)

        first, passed, smalls = [], [], []
        for a, (src_ref, g_ref) in enumerate(((wi_ref, gwi_ref), (wo_ref, gwo_ref))):
            for j, (cx, cy) in enumerate(chips):
                first.append(remote(6 * a + j, src_ref.at[c], g_ref.at[me, c], (cx, cy, c)))
                slot = g_ref.at[2 * cx + cy, c]
                passed.append(remote(6 * a + 3 + j, slot, slot, sibling))
        for j, (cx, cy) in enumerate(chips):
            smalls.append(remote(12 + j, sm_ref, gsm_ref.at[me], (cx, cy, c)))
        for cp in first + smalls:
            cp.start()
        n = 0
        for a, g_ref in enumerate((gwi_ref, gwo_ref)):
            for j, (cx, cy) in enumerate(chips):
                slot = g_ref.at[2 * cx + cy, c]
                remote(6 * a + j, slot, slot, sibling).wait_recv()
                passed[n].start()
                n += 1
        for a, g_ref in enumerate((gwi_ref, gwo_ref)):
            for j, (cx, cy) in enumerate(chips):
                slot = g_ref.at[2 * cx + cy, 1 - c]
                remote(6 * a + 3 + j, slot, slot, sibling).wait_recv()
        for j, (cx, cy) in enumerate(chips):
            remote(12 + j, sm_ref, gsm_ref.at[2 * cx + cy], sibling).wait_recv()
        for cp in first + smalls + passed:
            cp.wait_send()
        for cp in locals_:
            cp.wait()

    return pl.pallas_call(
        body, name="gather_weights",
        out_shape=(jax.ShapeDtypeStruct((N_CHIPS,) + wi.shape, wi.dtype),
                   jax.ShapeDtypeStruct((N_CHIPS,) + wo.shape, wo.dtype),
                   jax.ShapeDtypeStruct((N_CHIPS,) + small.shape, small.dtype)),
        in_specs=[ANY, ANY, ANY], out_specs=(ANY, ANY, ANY),
        scratch_shapes=[pltpu.SemaphoreType.DMA((15,)), pltpu.SemaphoreType.DMA((15,)), pltpu.SemaphoreType.DMA((3,))],
    )(wi, wo, small)


def _pair_exchange(ga, gb):
    def body(ga_ref, gb_ref, ra_ref, rb_ref, send_sems, recv_sems):
        x, y, c = _position()
        sibling = (x, y, 1 - c)
        copies = []
        for k, (src, dst) in enumerate(((ga_ref, ra_ref), (gb_ref, rb_ref))):
            copies.append(pltpu.make_async_remote_copy(
                src_ref=src.at[:, 1 - c], dst_ref=dst, send_sem=send_sems.at[k], recv_sem=recv_sems.at[k],
                device_id=sibling, device_id_type=MESH))
        for cp in copies:
            cp.start()
        for cp in copies:
            cp.wait()

    return pl.pallas_call(
        body, name="grad_pair_exchange",
        out_shape=(jax.ShapeDtypeStruct((N_CHIPS,) + ga.shape[2:], ga.dtype),
                   jax.ShapeDtypeStruct((N_CHIPS,) + gb.shape[2:], gb.dtype)),
        in_specs=[ANY, ANY], out_specs=(ANY, ANY),
        scratch_shapes=[pltpu.SemaphoreType.DMA((2,)), pltpu.SemaphoreType.DMA((2,))],
    )(ga, gb)


def _chip_exchange(pa, pb):
    def body(pa_ref, pb_ref, ra_ref, rb_ref, send_sems, recv_sems):
        x, y, c = _position()
        chips = [(1 - x, y), (x, 1 - y), (1 - x, 1 - y)]
        copies = []
        for a, (src, dst) in enumerate(((pa_ref, ra_ref), (pb_ref, rb_ref))):
            for j, (cx, cy) in enumerate(chips):
                copies.append(pltpu.make_async_remote_copy(
                    src_ref=src.at[2 * cx + cy], dst_ref=dst.at[j], send_sem=send_sems.at[3 * a + j],
                    recv_sem=recv_sems.at[3 * a + j], device_id=(cx, cy, c), device_id_type=MESH))
        for cp in copies:
            cp.start()
        for cp in copies:
            cp.wait()

    return pl.pallas_call(
        body, name="grad_chip_exchange",
        out_shape=(jax.ShapeDtypeStruct((3,) + pa.shape[1:], pa.dtype),
                   jax.ShapeDtypeStruct((3,) + pb.shape[1:], pb.dtype)),
        in_specs=[ANY, ANY], out_specs=(ANY, ANY),
        scratch_shapes=[pltpu.SemaphoreType.DMA((6,)), pltpu.SemaphoreType.DMA((6,))],
    )(pa, pb)


def _pair_share(ha, hb):
    def body(ha_ref, hb_ref, oa_ref, ob_ref, send_sems, recv_sems, local_sems):
        x, y, c = _position()
        sibling = (x, y, 1 - c)
        locals_, copies = [], []
        for k, (src, dst) in enumerate(((ha_ref, oa_ref), (hb_ref, ob_ref))):
            locals_.append(pltpu.make_async_copy(src, dst.at[c], local_sems.at[k]))
            copies.append(pltpu.make_async_remote_copy(
                src_ref=src, dst_ref=dst.at[c], send_sem=send_sems.at[k], recv_sem=recv_sems.at[k],
                device_id=sibling, device_id_type=MESH))
        for cp in locals_ + copies:
            cp.start()
        for k, (src, dst) in enumerate(((ha_ref, oa_ref), (hb_ref, ob_ref))):
            pltpu.make_async_remote_copy(
                src_ref=src, dst_ref=dst.at[1 - c], send_sem=send_sems.at[k], recv_sem=recv_sems.at[k],
                device_id=sibling, device_id_type=MESH).wait_recv()
        for cp in copies:
            cp.wait_send()
        for cp in locals_:
            cp.wait()

    return pl.pallas_call(
        body, name="grad_pair_share",
        out_shape=(jax.ShapeDtypeStruct((2,) + ha.shape, ha.dtype), jax.ShapeDtypeStruct((2,) + hb.shape, hb.dtype)),
        in_specs=[ANY, ANY], out_specs=(ANY, ANY),
        scratch_shapes=[pltpu.SemaphoreType.DMA((2,)), pltpu.SemaphoreType.DMA((2,)), pltpu.SemaphoreType.DMA((2,))],
    )(ha, hb)


def _gather_small(pack):
    def body(p_ref, o_ref, send_sems, recv_sems, local_sem):
        x, y, c = _position()
        me = 4 * x + 2 * y + c
        mine = pltpu.make_async_copy(p_ref, o_ref.at[me], local_sem)
        mine.start()
        sends = []
        for mask in range(1, N_DEV):
            px = 1 - x if mask & 4 else x
            py = 1 - y if mask & 2 else y
            pc = 1 - c if mask & 1 else c
            sends.append((pltpu.make_async_remote_copy(
                src_ref=p_ref, dst_ref=o_ref.at[me], send_sem=send_sems.at[mask - 1], recv_sem=recv_sems.at[mask - 1],
                device_id=(px, py, pc), device_id_type=MESH), 4 * px + 2 * py + pc, mask))
        for cp, _, _ in sends:
            cp.start()
        for _, peer, mask in sends:
            pltpu.make_async_remote_copy(
                src_ref=p_ref, dst_ref=o_ref.at[peer], send_sem=send_sems.at[mask - 1], recv_sem=recv_sems.at[mask - 1],
                device_id=(x, y, c), device_id_type=MESH).wait_recv()
        for cp, _, _ in sends:
            cp.wait_send()
        mine.wait()

    return pl.pallas_call(
        body, name="gather_small",
        out_shape=jax.ShapeDtypeStruct((N_DEV,) + pack.shape, pack.dtype),
        in_specs=[ANY], out_specs=ANY,
        scratch_shapes=[pltpu.SemaphoreType.DMA((N_DEV - 1,)), pltpu.SemaphoreType.DMA((N_DEV - 1,)),
                        pltpu.SemaphoreType.DMA],
    )(pack)


def _pair_sum(mine, recv, c_idx):
    rows, cols = mine.shape[2:]

    def body(c_ref, a_ref, b_ref, o_ref):
        o_ref[...] = a_ref[...] + b_ref[...]

    return pl.pallas_call(
        body, name="grad_pair_sum",
        grid_spec=pltpu.PrefetchScalarGridSpec(
            num_scalar_prefetch=1, grid=(N_CHIPS,),
            in_specs=[pl.BlockSpec((None, None, rows, cols), lambda s, c_ref: (s, c_ref[0], 0, 0)),
                      pl.BlockSpec((None, rows, cols), lambda s, c_ref: (s, 0, 0))],
            out_specs=pl.BlockSpec((None, rows, cols), lambda s, c_ref: (s, 0, 0))),
        out_shape=jax.ShapeDtypeStruct(recv.shape, recv.dtype),
        compiler_params=_params(("parallel",)),
    )(c_idx, mine, recv)


def _chip_sum(psum, recv3, chip_idx):
    rows, cols = psum.shape[1:]
    tr = rows // 2

    def body(s_ref, p_ref, r0, r1, r2, o_ref):
        o_ref[...] = ((p_ref[...] + r0[...]) + r1[...]) + r2[...]

    return pl.pallas_call(
        body, name="grad_chip_sum",
        grid_spec=pltpu.PrefetchScalarGridSpec(
            num_scalar_prefetch=1, grid=(2,),
            in_specs=[pl.BlockSpec((None, tr, cols), lambda i, s_ref: (s_ref[0], i, 0))] +
                     [pl.BlockSpec((None, tr, cols), functools.partial(lambda i, s_ref, j: (j, i, 0), j=j))
                      for j in range(3)],
            out_specs=pl.BlockSpec((tr, cols), lambda i, s_ref: (i, 0))),
        out_shape=jax.ShapeDtypeStruct((rows, cols), psum.dtype),
        compiler_params=_params(("parallel",)),
    )(chip_idx, psum, recv3, recv3, recv3)


def _adamw_math(w, g, m, v):
    m = ADAM_B1 * m + (1.0 - ADAM_B1) * g
    v = ADAM_B2 * v + (1.0 - ADAM_B2) * (g * g)
    m_hat = m / (1.0 - ADAM_B1 ** ADAM_STEP)
    v_hat = v / (1.0 - ADAM_B2 ** ADAM_STEP)
    delta = -ADAM_LR * (m_hat / (jnp.sqrt(v_hat) + ADAM_EPS) + ADAM_WD * w)
    return delta, m, v


def _adamw_big(w, g, m, v, tr):
    rows, cols = w.shape

    def body(w_ref, g_ref, m_ref, v_ref, d_out, m_out, v_out):
        d, m2, v2 = _adamw_math(w_ref[...], g_ref[...], m_ref[...], v_ref[...])
        d_out[...] = d
        m_out[...] = m2
        v_out[...] = v2

    spec = pl.BlockSpec((tr, cols), lambda i: (i, 0))
    sds = jax.ShapeDtypeStruct((rows, cols), F32)
    return pl.pallas_call(
        body, name="adamw_big", grid=(rows // tr,), in_specs=[spec] * 4, out_specs=(spec,) * 3,
        out_shape=(sds,) * 3, compiler_params=_params(("parallel",)),
    )(w, g, m, v)


def _small_update(gpack, params, ms, vs):
    slots = (SLOT_NORM, SLOT_FINAL, SLOT_ATTN, SLOT_CONVG, SLOT_BF, SLOT_META, SLOT_CONVW)
    n = len(slots)

    def body(*refs):
        gp_ref = refs[0]
        w_refs, m_refs, v_refs = refs[1:1 + n], refs[1 + n:1 + 2 * n], refs[1 + 2 * n:1 + 3 * n]
        outs = refs[1 + 3 * n:2 + 7 * n]
        loss_ref = outs[0]
        g_outs, d_outs, m_outs, v_outs = (outs[1 + k * n:1 + (k + 1) * n] for k in range(4))
        g_scr, w_scr, m_scr, v_scr = refs[2 + 7 * n:]
        x, y, _ = _position()
        shard = 2 * x + y
        tot = gp_ref[0]
        for d in range(1, N_DEV):
            tot = tot + gp_ref[d]
        r0, r1, _, _ = SLOT_META
        meta_sel = tot[r0:r1, 0:256]
        cw_sel = tot[24:32, 0:128]
        for k in range(1, N_CHIPS):
            meta_sel = jnp.where(shard == k, tot[r0:r1, 256 * k:256 * (k + 1)], meta_sel)
            cw_sel = jnp.where(shard == k, tot[24:32, 128 * k:128 * (k + 1)], cw_sel)
        zeros = jnp.zeros((PACK_ROWS, D_MODEL), F32)
        for scr in (g_scr, w_scr, m_scr, v_scr):
            scr[...] = zeros
        g_scr[0:8, :] = tot[0:8, :]
        g_scr[r0:r1, 0:256] = meta_sel
        g_scr[24:32, 0:128] = cw_sel
        for (a, b, c0, c1), w_ref, m_ref, v_ref in zip(slots, w_refs, m_refs, v_refs):
            w_scr[a:b, c0:c1] = w_ref[...]
            m_scr[a:b, c0:c1] = m_ref[...]
            v_scr[a:b, c0:c1] = v_ref[...]
        loss_ref[...] = g_scr[LOSS_ROW:LOSS_ROW + 1, 0:1]
        d, m2, v2 = _adamw_math(w_scr[...], g_scr[...], m_scr[...], v_scr[...])
        w_scr[...] = d
        m_scr[...] = m2
        v_scr[...] = v2
        for (a, b, c0, c1), g_o, d_o, m_o, v_o in zip(slots, g_outs, d_outs, m_outs, v_outs):
            g_o[...] = g_scr[a:b, c0:c1]
            d_o[...] = w_scr[a:b, c0:c1]
            m_o[...] = m_scr[a:b, c0:c1]
            v_o[...] = v_scr[a:b, c0:c1]

    shapes = [jax.ShapeDtypeStruct(p.shape, F32) for p in params]
    out = pl.pallas_call(
        body, name="small_update",
        out_shape=[jax.ShapeDtypeStruct((1, 1), F32)] + shapes * 4,
        scratch_shapes=[pltpu.VMEM((PACK_ROWS, D_MODEL), F32)] * 4,
        compiler_params=_params(),
    )(gpack, *params, *ms, *vs)
    return out[0], out[1:1 + n], out[1 + n:1 + 2 * n], out[1 + 2 * n:1 + 3 * n], out[1 + 3 * n:1 + 4 * n]


def _in_proj(x2, meta_blk, norm_g, w_pad, bf_pad):
    seq = x2.shape[0]
    lp = seq + FRONT
    t = ROW_TILE
    nt = lp // t
    n_sub = t // LANE

    def body(*refs):
        x_refs = refs[:n_sub]
        mb, g_ref, w_ref, bf_ref = refs[n_sub:n_sub + 4]
        q_ref, k_ref, v_ref, rest_ref, fl_ref, cexp_ref, ct_ref, ut_ref, carry = refs[n_sub + 4:]
        i = pl.program_id(0)

        @pl.when(i == 0)
        def _():
            carry[...] = jnp.zeros_like(carry)

        first = jnp.where(i == 0, mb[...], x_refs[0][...])
        h = jnp.concatenate([first] + [r[...] for r in x_refs[1:]], axis=0)
        ms = jnp.mean(h * h, axis=-1, keepdims=True)
        u32 = (h * lax.rsqrt(ms + EPS)) * g_ref[...]
        u = u32.astype(MXU_DTYPE)
        ut_ref[...] = u32.T.astype(MXU_DTYPE)

        def seg(a, width):
            return _dot(u, w_ref[:, a:a + width])

        q_ref[...] = (seg(SEG_Q, D_ATTN) * (HEAD_DIM ** -0.5)).astype(MXU_DTYPE)
        k_ref[...] = seg(SEG_K, D_ATTN).astype(MXU_DTYPE)
        v_ref[...] = seg(SEG_V, D_ATTN).astype(MXU_DTYPE)
        for s in range(5):
            rest_ref[:, 512 * s:512 * (s + 1)] = seg(SEG_ZA + 512 * s, 512)
        fl = seg(SEG_F, LANE)
        fl_ref[...] = fl
        z = fl + bf_ref[...]
        logf = jnp.minimum(z, 0.0) - jnp.log(1.0 + jnp.exp(-jnp.abs(z)))
        row = i * t + lax.broadcasted_iota(jnp.int32, (t, LANE), 0)
        logf = jnp.where(row >= PAD_ROWS, logf, 0.0)
        tri = jnp.where(lax.broadcasted_iota(jnp.int32, (t, t), 0) >= lax.broadcasted_iota(jnp.int32, (t, t), 1),
                        1.0, 0.0)
        cs = _dot_exact(tri, logf) + carry[...]
        carry[...] = carry[...] + jnp.sum(logf, axis=0, keepdims=True)
        ct_ref[...] = cs.T[0:SUBLANE, :]
        expand = jnp.where(lax.broadcasted_iota(jnp.int32, (LANE, D_ATTN), 0)
                           == (lax.broadcasted_iota(jnp.int32, (LANE, D_ATTN), 1) >> 6), 1.0, 0.0)
        cexp_ref[...] = _dot_exact(cs, expand)

    row_blk = lambda cols: pl.BlockSpec((t, cols), lambda i: (i, 0))
    const = lambda shape: pl.BlockSpec(shape, lambda i: (0, 0))
    return pl.pallas_call(
        body, name="in_proj", grid=(nt,),
        in_specs=_x_block_specs(n_sub, LANE) + [const((LANE, D_MODEL)), const((1, D_MODEL)),
                                                pl.BlockSpec((D_MODEL, D_IN_PAD), lambda i: (0, 0),
                                                             pipeline_mode=pl.Buffered(1)),
                                                const((1, LANE))],
        out_specs=(row_blk(D_ATTN), row_blk(D_ATTN), row_blk(D_ATTN), row_blk(5 * 512), row_blk(LANE),
                   row_blk(D_ATTN), pl.BlockSpec((SUBLANE, t), lambda i: (0, i)),
                   pl.BlockSpec((D_MODEL, t), lambda i: (0, i))),
        out_shape=(jax.ShapeDtypeStruct((lp, D_ATTN), MXU_DTYPE), jax.ShapeDtypeStruct((lp, D_ATTN), MXU_DTYPE),
                   jax.ShapeDtypeStruct((lp, D_ATTN), MXU_DTYPE), jax.ShapeDtypeStruct((lp, 5 * 512), F32),
                   jax.ShapeDtypeStruct((lp, LANE), F32), jax.ShapeDtypeStruct((lp, D_ATTN), F32),
                   jax.ShapeDtypeStruct((SUBLANE, lp), F32), jax.ShapeDtypeStruct((D_MODEL, lp), MXU_DTYPE)),
        scratch_shapes=[pltpu.VMEM((1, LANE), F32)],
        compiler_params=_params(("arbitrary",)),
    )(*([x2] * n_sub), meta_blk, norm_g, w_pad, bf_pad)


def _head_masks():
    lane = lax.broadcasted_iota(jnp.int32, (1, LANE), 1)
    return lane < HEAD_DIM, lane >= HEAD_DIM


def _causal_mask(r0, c0, t):
    qpos = r0 + lax.broadcasted_iota(jnp.int32, (t, t), 0)
    kpos = c0 + lax.broadcasted_iota(jnp.int32, (t, t), 1)
    return (kpos <= qpos) & (kpos >= PAD_ROWS)


def _pair_specs(lp, nt, t):
    blk = pl.BlockSpec((lp, LANE), lambda g: (0, g))
    ct_a = pl.BlockSpec((None, nt, 1, t), lambda g: (2 * g, 0, 0, 0))
    ct_b = pl.BlockSpec((None, nt, 1, t), lambda g: (2 * g + 1, 0, 0, 0))
    return blk, ct_a, ct_b


def _attn_fwd(q, k, v, cexp, ct4):
    lp = q.shape[0]
    t = ROW_TILE
    nt = lp // t

    def body(q_ref, k_ref, v_ref, ce_ref, cta_ref, ctb_ref, o_ref, lse_ref):
        masks = _head_masks()

        def q_block(i, _):
            r0 = pl.multiple_of(i * t, t)
            qi = q_ref[pl.ds(r0, t), :]
            ci = ce_ref[pl.ds(r0, t), :]
            o_pair = jnp.zeros((t, LANE), F32)
            lse_pair = jnp.zeros((t, LANE), F32)
            for hh in range(2):
                hm = masks[hh]
                ct_ref = (cta_ref, ctb_ref)[hh]
                cq = jnp.max(jnp.where(hm, ci, -jnp.inf), axis=-1, keepdims=True)

                def k_block(j, carry, hm=hm, ct_ref=ct_ref, cq=cq):
                    m, l, acc = carry
                    c0 = pl.multiple_of(j * t, t)
                    kjh = jnp.where(hm, k_ref[pl.ds(c0, t), :], 0).astype(MXU_DTYPE)
                    vjh = jnp.where(hm, v_ref[pl.ds(c0, t), :], 0).astype(MXU_DTYPE)
                    s = _dot_nt(qi, kjh) + cq - ct_ref[j]
                    s = jnp.where(_causal_mask(r0, c0, t), s, NEG)
                    m_new = jnp.maximum(m, jnp.max(s, axis=-1, keepdims=True))
                    alpha = jnp.exp(m - m_new)
                    p = jnp.exp(s - m_new)
                    l = alpha * l + jnp.sum(p, axis=-1, keepdims=True)
                    p_hi = p.astype(MXU_DTYPE)
                    p_lo = (p - p_hi.astype(F32)).astype(MXU_DTYPE)
                    acc = alpha * acc + _dot(jnp.concatenate([p_hi, p_lo], axis=1),
                                             jnp.concatenate([vjh, vjh], axis=0))
                    return m_new, l, acc

                m, l, acc = lax.fori_loop(
                    0, i + 1, k_block,
                    (jnp.full((t, 1), NEG, F32), jnp.zeros((t, 1), F32), jnp.zeros((t, LANE), F32)))
                o_pair = o_pair + acc * (1.0 / l)
                lse_pair = jnp.where(hm, m + jnp.log(l), lse_pair)
            o_ref[pl.ds(r0, t), :] = o_pair
            lse_ref[pl.ds(r0, t), :] = lse_pair
            return 0

        lax.fori_loop(0, nt, q_block, 0)

    blk, ct_a, ct_b = _pair_specs(lp, nt, t)
    return pl.pallas_call(
        body, name="attn_fwd", grid=(HEADS // 2,),
        in_specs=[blk, blk, blk, blk, ct_a, ct_b], out_specs=(blk, blk),
        out_shape=(jax.ShapeDtypeStruct((lp, D_ATTN), F32), jax.ShapeDtypeStruct((lp, D_ATTN), F32)),
        compiler_params=_params(("parallel",)),
    )(q, k, v, cexp, ct4, ct4)


def _attn_bwd(q, k, v, do, lse, delta, cexp, ct4):
    lp = q.shape[0]
    t = ROW_TILE
    nt = lp // t

    def body(q_ref, k_ref, v_ref, do_ref, lse_ref, dl_ref, ce_ref, cta_ref, ctb_ref,
             dq_ref, dk_ref, dv_ref, dc_ref, dq_acc, dk_acc, dv_acc):
        masks = _head_masks()
        dq_acc[...] = jnp.zeros_like(dq_acc)

        def k_block(j, _):
            c0 = pl.multiple_of(j * t, t)
            kj = k_ref[pl.ds(c0, t), :]
            vj = v_ref[pl.ds(c0, t), :]
            for hh in range(2):
                hm = masks[hh]
                ct_ref = (cta_ref, ctb_ref)[hh]
                kjh = jnp.where(hm, kj, 0).astype(MXU_DTYPE)
                vjh = jnp.where(hm, vj, 0).astype(MXU_DTYPE)
                ck = ct_ref[j]
                dk_acc[hh] = jnp.zeros((t, LANE), F32)
                dv_acc[hh] = jnp.zeros((t, LANE), F32)

                def q_block(i, colsum, hm=hm, kjh=kjh, vjh=vjh, ck=ck, hh=hh):
                    r0 = pl.multiple_of(i * t, t)
                    qi = q_ref[pl.ds(r0, t), :]
                    doi = do_ref[pl.ds(r0, t), :]
                    cq = jnp.max(jnp.where(hm, ce_ref[pl.ds(r0, t), :], -jnp.inf), axis=-1, keepdims=True)
                    lse_i = jnp.max(jnp.where(hm, lse_ref[pl.ds(r0, t), :], -jnp.inf), axis=-1, keepdims=True)
                    dl_i = jnp.max(jnp.where(hm, dl_ref[pl.ds(r0, t), :], -jnp.inf), axis=-1, keepdims=True)
                    s = _dot_nt(qi, kjh) + cq - ck
                    s = jnp.where(_causal_mask(r0, c0, t), s, NEG)
                    p = jnp.exp(s - lse_i)
                    dp = _dot_nt(doi, vjh)
                    ds32 = p * (dp - dl_i)
                    ds = ds32.astype(MXU_DTYPE)
                    pb = p.astype(MXU_DTYPE)
                    dv_acc[hh] = dv_acc[hh] + lax.dot_general(pb, doi, (((0,), (0,)), ((), ())),
                                                              preferred_element_type=F32)
                    dk_acc[hh] = dk_acc[hh] + lax.dot_general(ds, qi, (((0,), (0,)), ((), ())),
                                                              preferred_element_type=F32)
                    dq_acc[pl.ds(r0, t), :] = dq_acc[pl.ds(r0, t), :] + _dot(ds, kjh)
                    return colsum + jnp.sum(ds32, axis=0, keepdims=True)

                colsum = lax.fori_loop(j, nt, q_block, jnp.zeros((1, t), F32))
                dc_ref[hh, j] = -colsum
            dk_ref[pl.ds(c0, t), :] = jnp.where(masks[0], dk_acc[0], dk_acc[1]).astype(dk_ref.dtype)
            dv_ref[pl.ds(c0, t), :] = jnp.where(masks[0], dv_acc[0], dv_acc[1]).astype(dv_ref.dtype)
            return 0

        lax.fori_loop(0, nt, k_block, 0)
        dq_ref[...] = (dq_acc[...] * (HEAD_DIM ** -0.5)).astype(dq_ref.dtype)

    blk, ct_a, ct_b = _pair_specs(lp, nt, t)
    return pl.pallas_call(
        body, name="attn_bwd", grid=(HEADS // 2,),
        in_specs=[blk] * 7 + [ct_a, ct_b],
        out_specs=(blk, blk, blk, pl.BlockSpec((2, nt, 1, t), lambda g: (g, 0, 0, 0))),
        out_shape=(jax.ShapeDtypeStruct((lp, D_ATTN), MXU_DTYPE),) * 3
                  + (jax.ShapeDtypeStruct((HEADS, nt, 1, t), F32),),
        scratch_shapes=[pltpu.VMEM((lp, LANE), F32), pltpu.VMEM((2, t, LANE), F32), pltpu.VMEM((2, t, LANE), F32)],
        compiler_params=_params(("parallel",)),
    )(q, k, v, do, lse, delta, cexp, ct4, ct4)


def _shift_down(prev8, cur, k):
    ext = jnp.concatenate([prev8, cur], axis=0)
    return pltpu.roll(ext, k, 0)[SUBLANE:, :]


def _shift_up(cur, next8, k):
    ext = jnp.concatenate([cur, next8], axis=0)
    n = ext.shape[0]
    return pltpu.roll(ext, n - k, 0)[:cur.shape[0], :]


def _post(o, rest, x2, meta_blk, tgt2, w_out, attn_g, conv_g, final_g, conv_w8):
    lp = o.shape[0]
    t = ROW_TILE
    nt = lp // t
    n_sub = t // LANE
    hb = t // SUBLANE

    def body(*refs):
        o_ref, za_ref, gb_ref, gc_ref, xc_ref, zc_ref, gch_ref, xch_ref = refs[:8]
        x_refs = refs[8:8 + n_sub]
        mb = refs[8 + n_sub]
        t_refs = refs[9 + n_sub:9 + 2 * n_sub]
        wo_ref, ag_ref, cg_ref, fg_ref, cw_ref = refs[9 + 2 * n_sub:14 + 2 * n_sub]
        (dout_ref, do_ref, dl_ref, dza_ref, dgb_ref, dzc_ref, dcv_ref,
         loss_ref, gf_ref, gag_ref, gcg_ref, gwo_ref) = refs[14 + 2 * n_sub:]
        i = pl.program_id(0)

        @pl.when(i == 0)
        def _():
            for r in (loss_ref, gf_ref, gag_ref, gcg_ref, gwo_ref):
                r[...] = jnp.zeros_like(r)

        gmat = _group_matrix()
        inv_g = 1.0 / HEAD_DIM
        o_v = o_ref[...]
        ra = lax.rsqrt(_group_sum(o_v * o_v, gmat) * inv_g + EPS)
        n_a = o_v * ra
        a_n = n_a * ag_ref[...]
        za = za_ref[...]
        sig_a = _sigmoid(za)
        sz_a = za * sig_a
        y_a = a_n * sz_a
        gb = gb_ref[...]
        gc = gc_ref[...]
        xc = xc_ref[...]
        cx = gc * xc
        cx_prev = jnp.where(i == 0, 0.0, gch_ref[...] * xch_ref[...])
        conv = (cw_ref[0:1, :] * _shift_down(cx_prev, cx, 2) + cw_ref[1:2, :] * _shift_down(cx_prev, cx, 1)
                + cw_ref[2:3, :] * cx)
        e = gb * conv
        re = lax.rsqrt(_group_sum(e * e, gmat) * inv_g + EPS)
        n_e = e * re
        e_n = n_e * cg_ref[...]
        zc = zc_ref[...]
        sig_c = _sigmoid(zc)
        sz_c = zc * sig_c
        y_c = e_n * sz_c
        mix = jnp.concatenate([y_a, y_c], axis=-1)
        mix_b = mix.astype(MXU_DTYPE)
        first = jnp.where(i == 0, mb[...], x_refs[0][...])
        h = jnp.concatenate([first] + [r[...] for r in x_refs[1:]], axis=0)
        out = h + _dot(mix_b, wo_ref[...])
        r2 = lax.rsqrt(jnp.mean(out * out, axis=-1, keepdims=True) + EPS)
        n_f = out * r2
        y = n_f * fg_ref[...]
        tgt = jnp.concatenate([r[...] for r in t_refs], axis=0)
        valid = (i * t + lax.broadcasted_iota(jnp.int32, (t, 1), 0)) >= FRONT
        diff = jnp.where(valid, y - tgt, 0.0)
        loss_ref[...] = loss_ref[...] + 0.5 * jnp.sum(jnp.sum(diff * diff, axis=-1, keepdims=True) * (1.0 / D_MODEL))
        dy = diff * (1.0 / D_MODEL)
        gf_ref[...] = gf_ref[...] + jnp.sum(dy * n_f, axis=0, keepdims=True)
        dn = dy * fg_ref[...]
        d_out = r2 * (dn - n_f * jnp.mean(dn * n_f, axis=-1, keepdims=True))
        dout_ref[...] = d_out
        d_out_b = d_out.astype(MXU_DTYPE)
        d_mix = _dot_nt(d_out_b, wo_ref[...])
        gwo_ref[...] = gwo_ref[...] + _dot(mix.T.astype(MXU_DTYPE), d_out_b)
        d_ya = d_mix[:, :D_ATTN]
        d_yc = d_mix[:, D_ATTN:]
        d_an = d_ya * sz_a
        dza_ref[...] = (d_ya * a_n * (sig_a * (1.0 + za * (1.0 - sig_a)))).astype(dza_ref.dtype)
        gag_ref[...] = gag_ref[...] + jnp.sum(d_an * n_a, axis=0, keepdims=True)
        dn_a = d_an * ag_ref[...]
        d_o = ra * (dn_a - n_a * (_group_sum(dn_a * n_a, gmat) * inv_g))
        d_o_b = d_o.astype(do_ref.dtype)
        do_ref[...] = d_o_b
        dl_ref[...] = _group_sum(d_o_b.astype(F32) * o_v, gmat)
        d_en = d_yc * sz_c
        dzc_ref[...] = (d_yc * e_n * (sig_c * (1.0 + zc * (1.0 - sig_c)))).astype(dzc_ref.dtype)
        gcg_ref[...] = gcg_ref[...] + jnp.sum(d_en * n_e, axis=0, keepdims=True)
        dn_e = d_en * cg_ref[...]
        d_e = re * (dn_e - n_e * (_group_sum(dn_e * n_e, gmat) * inv_g))
        dgb_ref[...] = (d_e * conv).astype(dgb_ref.dtype)
        dcv_ref[...] = d_e * gb

    row_blk = lambda cols: pl.BlockSpec((t, cols), lambda i: (i, 0))
    rest_blk = lambda s: pl.BlockSpec((t, 512), functools.partial(lambda i, s: (i, s), s=s))
    halo = lambda s: pl.BlockSpec((SUBLANE, 512), functools.partial(lambda i, s: (jnp.maximum(i * hb - 1, 0), s), s=s))
    const = lambda shape: pl.BlockSpec(shape, lambda i: (0, 0))
    acc = lambda shape: pl.BlockSpec(shape, lambda i: (0, 0))
    return pl.pallas_call(
        body, name="post_fwd_bwd", grid=(nt,),
        in_specs=[row_blk(D_ATTN)] + [rest_blk(s) for s in range(5)] + [halo(2), halo(3)]
                 + _x_block_specs(n_sub, LANE) + [const((LANE, D_MODEL))] + _x_block_specs(n_sub, LANE)
                 + [const((D_MODEL, D_MODEL)), const((1, D_ATTN)), const((1, D_CONV)), const((1, D_MODEL)),
                    const((SUBLANE, D_CONV))],
        out_specs=(row_blk(D_MODEL), row_blk(D_ATTN), row_blk(D_ATTN), row_blk(D_ATTN), row_blk(D_CONV),
                   row_blk(D_CONV), row_blk(D_CONV),
                   acc((1, LANE)), acc((1, D_MODEL)), acc((1, D_ATTN)), acc((1, D_CONV)), acc((D_MODEL, D_MODEL))),
        out_shape=(jax.ShapeDtypeStruct((lp, D_MODEL), F32), jax.ShapeDtypeStruct((lp, D_ATTN), MXU_DTYPE),
                   jax.ShapeDtypeStruct((lp, D_ATTN), F32), jax.ShapeDtypeStruct((lp, D_ATTN), MXU_DTYPE),
                   jax.ShapeDtypeStruct((lp, D_CONV), MXU_DTYPE), jax.ShapeDtypeStruct((lp, D_CONV), MXU_DTYPE),
                   jax.ShapeDtypeStruct((lp, D_CONV), F32),
                   jax.ShapeDtypeStruct((1, LANE), F32), jax.ShapeDtypeStruct((1, D_MODEL), F32),
                   jax.ShapeDtypeStruct((1, D_ATTN), F32), jax.ShapeDtypeStruct((1, D_CONV), F32),
                   jax.ShapeDtypeStruct((D_MODEL, D_MODEL), F32)),
        compiler_params=_params(("arbitrary",)),
    )(o, *([rest] * 5), rest, rest, *([x2] * n_sub), meta_blk, *([tgt2] * n_sub),
      w_out, attn_g, conv_g, final_g, conv_w8)


def _bwd_in(x2, meta_blk, norm_g, w_pad, bf_pad, fl, dc, dq, dk, dv, dza, dgb, dzc, dconv, rest, d_out, conv_w8):
    lp = fl.shape[0]
    t = ROW_TILE
    nt = lp // t
    n_sub = t // LANE
    hb = t // SUBLANE
    rev = lambda i: nt - 1 - i

    def body(*refs):
        x_refs = refs[:n_sub]
        (mb, g_ref, w_ref, bf_ref, fl_ref, dc_ref, dq_ref, dk_ref, dv_ref, dza_ref, dgb_ref, dzc_ref,
         dcv_ref, dcvn_ref, gc_ref, xc_ref, gch_ref, xch_ref, dout_ref, cw_ref) = refs[n_sub:n_sub + 20]
        dp_ref, dh_ref, gn_ref, gbf_ref, gcw_ref, carry = refs[n_sub + 20:]
        step = pl.program_id(0)
        i = rev(step)

        @pl.when(step == 0)
        def _():
            for r in (gn_ref, gbf_ref, gcw_ref, carry):
                r[...] = jnp.zeros_like(r)

        dc8 = jnp.concatenate([dc_ref[...], jnp.zeros((LANE - HEADS, t), F32)], axis=0).T
        triu = jnp.where(lax.broadcasted_iota(jnp.int32, (t, t), 1) >= lax.broadcasted_iota(jnp.int32, (t, t), 0),
                         1.0, 0.0)
        dlogf = _dot_exact(triu, dc8) + carry[...]
        carry[...] = carry[...] + jnp.sum(dc8, axis=0, keepdims=True)
        z = fl_ref[...] + bf_ref[...]
        row = i * t + lax.broadcasted_iota(jnp.int32, (t, LANE), 0)
        d_f = jnp.where(row >= PAD_ROWS, dlogf * (1.0 / (1.0 + jnp.exp(z))), 0.0)
        gbf_ref[...] = gbf_ref[...] + jnp.sum(d_f, axis=0, keepdims=True)
        dcv = dcv_ref[...]
        dcv_next = jnp.where(i == nt - 1, 0.0, dcvn_ref[...])
        d_cx = (cw_ref[2:3, :] * dcv + cw_ref[1:2, :] * _shift_up(dcv, dcv_next, 1)
                + cw_ref[0:1, :] * _shift_up(dcv, dcv_next, 2))
        gc = gc_ref[...]
        xc = xc_ref[...]
        cx = gc * xc
        cx_prev = jnp.where(i == 0, 0.0, gch_ref[...] * xch_ref[...])
        rowi = lax.broadcasted_iota(jnp.int32, (SUBLANE, 1), 0)
        gcw = (jnp.where(rowi == 0, jnp.sum(dcv * _shift_down(cx_prev, cx, 2), axis=0, keepdims=True), 0.0)
               + jnp.where(rowi == 1, jnp.sum(dcv * _shift_down(cx_prev, cx, 1), axis=0, keepdims=True), 0.0)
               + jnp.where(rowi == 2, jnp.sum(dcv * cx, axis=0, keepdims=True), 0.0))
        gcw_ref[...] = gcw_ref[...] + gcw
        dp_ref[:, SEG_Q:SEG_Q + 512] = dq_ref[...]
        dp_ref[:, SEG_K:SEG_K + 512] = dk_ref[...]
        dp_ref[:, SEG_V:SEG_V + 512] = dv_ref[...]
        dp_ref[:, SEG_F:SEG_F + LANE] = d_f.astype(dp_ref.dtype)
        dp_ref[:, SEG_ZA:SEG_ZA + 512] = dza_ref[...]
        dp_ref[:, SEG_GB:SEG_GB + 512] = dgb_ref[...]
        dp_ref[:, SEG_GC:SEG_GC + 512] = (d_cx * xc).astype(dp_ref.dtype)
        dp_ref[:, SEG_XC:SEG_XC + 512] = (d_cx * gc).astype(dp_ref.dtype)
        dp_ref[:, SEG_ZC:SEG_ZC + 512] = dzc_ref[...]
        d_u = _dot_nt(dp_ref[...], w_ref[...])
        first = jnp.where(i == 0, mb[...], x_refs[0][...])
        h = jnp.concatenate([first] + [r[...] for r in x_refs[1:]], axis=0)
        r1 = lax.rsqrt(jnp.mean(h * h, axis=-1, keepdims=True) + EPS)
        n_h = h * r1
        gn_ref[...] = gn_ref[...] + jnp.sum(d_u * n_h, axis=0, keepdims=True)
        dn = d_u * g_ref[...]
        dh_ref[...] = dout_ref[...] + r1 * (dn - n_h * jnp.mean(dn * n_h, axis=-1, keepdims=True))

    def x_specs():
        specs = [pl.BlockSpec((LANE, D_MODEL), lambda s: (jnp.maximum(n_sub * rev(s) - 1, 0), 0))]
        for b in range(1, n_sub):
            specs.append(pl.BlockSpec((LANE, D_MODEL), functools.partial(lambda s, b: (n_sub * rev(s) - 1 + b, 0), b=b)))
        return specs

    row_blk = lambda cols: pl.BlockSpec((t, cols), lambda s: (rev(s), 0))
    rest_blk = lambda k: pl.BlockSpec((t, 512), functools.partial(lambda s, k: (rev(s), k), k=k))
    halo_prev = lambda k: pl.BlockSpec(
        (SUBLANE, 512), functools.partial(lambda s, k: (jnp.maximum(rev(s) * hb - 1, 0), k), k=k))
    halo_next = pl.BlockSpec((SUBLANE, 512), lambda s: (jnp.minimum((rev(s) + 1) * hb, lp // SUBLANE - 1), 0))
    const = lambda shape: pl.BlockSpec(shape, lambda s: (0, 0))
    return pl.pallas_call(
        body, name="bwd_in", grid=(nt,),
        in_specs=x_specs() + [const((LANE, D_MODEL)), const((1, D_MODEL)),
                              pl.BlockSpec((D_MODEL, D_IN_PAD), lambda s: (0, 0), pipeline_mode=pl.Buffered(1)),
                              const((1, LANE)), row_blk(LANE),
                              pl.BlockSpec((HEADS, t), lambda s: (0, rev(s))),
                              row_blk(512), row_blk(512), row_blk(512), row_blk(512), row_blk(512), row_blk(512),
                              row_blk(512), halo_next, rest_blk(2), rest_blk(3), halo_prev(2), halo_prev(3),
                              row_blk(D_MODEL), const((SUBLANE, D_CONV))],
        out_specs=(row_blk(D_IN_PAD), row_blk(D_MODEL), const((1, D_MODEL)), const((1, LANE)),
                   const((SUBLANE, D_CONV))),
        out_shape=(jax.ShapeDtypeStruct((lp, D_IN_PAD), MXU_DTYPE), jax.ShapeDtypeStruct((lp, D_MODEL), F32),
                   jax.ShapeDtypeStruct((1, D_MODEL), F32), jax.ShapeDtypeStruct((1, LANE), F32),
                   jax.ShapeDtypeStruct((SUBLANE, D_CONV), F32)),
        scratch_shapes=[pltpu.VMEM((1, LANE), F32)],
        compiler_params=_params(("arbitrary",)),
    )(*([x2] * n_sub), meta_blk, norm_g, w_pad, bf_pad, fl, dc, dq, dk, dv, dza, dgb, dzc, dconv, dconv,
      rest, rest, rest, rest, d_out, conv_w8)


def _grad_w_in(ut, dproj):
    lp = ut.shape[1]
    tk = ROW_TILE
    tn = GW_COL_TILE

    def body(u_ref, d_ref, o_ref):
        @pl.when(pl.program_id(1) == 0)
        def _():
            o_ref[...] = jnp.zeros_like(o_ref)

        o_ref[...] = o_ref[...] + _dot(u_ref[...], d_ref[...])

    return pl.pallas_call(
        body, name="grad_w_in", grid=(D_IN_PAD // tn, lp // tk),
        in_specs=[pl.BlockSpec((D_MODEL, tk), lambda n, k: (0, k)), pl.BlockSpec((tk, tn), lambda n, k: (k, n))],
        out_specs=pl.BlockSpec((D_MODEL, tn), lambda n, k: (0, n)),
        out_shape=jax.ShapeDtypeStruct((D_MODEL, D_IN_PAD), F32),
        compiler_params=_params(("parallel", "arbitrary")),
    )(ut, dproj)


def _pad_cols(w):
    return jnp.concatenate([w[:, :F_END], jnp.zeros((w.shape[0], D_IN_PAD - D_IN), w.dtype), w[:, F_END:]], axis=1)


def _unpad_cols(g):
    return jnp.concatenate([g[:, :F_END], g[:, SEG_ZA:]], axis=1)


def _local_step(x2, tgt2, meta_full, norm_g, w_pad, b_f, conv_w_full, attn_g, conv_g, w_out_full, final_g):
    lp = x2.shape[0] + FRONT
    nt = lp // ROW_TILE
    meta_blk = jnp.concatenate([jnp.zeros((PAD_ROWS, D_MODEL), F32), meta_full], axis=0)
    bf_pad = jnp.pad(b_f, ((0, 0), (0, LANE - HEADS)))
    conv_w8 = jnp.pad(conv_w_full, ((0, SUBLANE - conv_w_full.shape[0]), (0, 0)))
    q, k, v, rest, fl, cexp, ct, ut = _in_proj(x2, meta_blk, norm_g, w_pad, bf_pad)
    ct4 = ct.reshape(SUBLANE, nt, 1, ROW_TILE)
    o, lse = _attn_fwd(q, k, v, cexp, ct4)
    (d_out, d_o, delta, dza, dgb, dzc, dconv, loss, g_final, g_attn, g_convg, gw_out) = _post(
        o, rest, x2, meta_blk, tgt2, w_out_full, attn_g, conv_g, final_g, conv_w8)
    dq, dk, dv, dc = _attn_bwd(q, k, v, d_o, lse, delta, cexp, ct4)
    dproj, d_h, g_norm, g_bf, g_cw = _bwd_in(x2, meta_blk, norm_g, w_pad, bf_pad, fl, dc.reshape(HEADS, lp), dq, dk, dv,
                                             dza, dgb, dzc, dconv, rest, d_out, conv_w8)
    gw_in = _grad_w_in(ut, dproj)
    return dict(loss=loss, d_h=d_h, g_norm=g_norm, g_final=g_final, g_attn=g_attn, g_convg=g_convg, g_bf=g_bf,
                g_cw=g_cw, gw_out=gw_out, gw_in=gw_in)


def kernel(x, meta, norm_g, w_in, b_f, conv_w, attn_norm_g, conv_norm_g, w_out, final_norm_g, loss_target, m_meta, m_norm_g, m_w_in, m_b_f, m_conv_w, m_attn_norm_g, m_conv_norm_g, m_w_out, m_final_norm_g, v_meta, v_norm_g, v_w_in, v_b_f, v_conv_w, v_attn_norm_g, v_conv_norm_g, v_w_out, v_final_norm_g):
    cx_, cy_, cc_ = _position()
    shard_rows_in = w_in.shape[1] // 2
    shard_cols_in = w_in.shape[2]
    shard_rows_out = w_out.shape[1] // 2
    wi = w_in[0].astype(MXU_DTYPE).reshape(2, shard_rows_in, shard_cols_in)
    wo = w_out[0].astype(MXU_DTYPE).reshape(2, shard_rows_out, D_MODEL)
    small = jnp.concatenate([meta, jnp.pad(conv_w[0], ((0, 8 - conv_w.shape[1]), (0, meta.shape[1] - conv_w.shape[2])))],
                            axis=0)
    gwi, gwo, gsm = _gather_weights(wi, wo, small)
    w_full = jnp.transpose(gwi.reshape(N_CHIPS, D_MODEL, shard_cols_in), (1, 0, 2)).reshape(D_MODEL, D_IN)
    w_pad = _pad_cols(w_full)
    w_out_full = gwo.reshape(D_MODEL, D_MODEL)
    meta_full = jnp.transpose(gsm[:, :N_META, :], (1, 0, 2)).reshape(N_META, D_MODEL)
    conv_w_full = jnp.transpose(gsm[:, N_META:N_META + 3, :LANE], (1, 0, 2)).reshape(3, D_CONV)
    final_g2 = final_norm_g.reshape(1, D_MODEL)
    r = _local_step(x[0], loss_target[0], meta_full, norm_g, w_pad, b_f, conv_w_full, attn_norm_g, conv_norm_g,
                    w_out_full, final_g2)
    grad_x = r["d_h"][FRONT:][None]
    ga = jnp.transpose(_unpad_cols(r["gw_in"]).reshape(D_MODEL, N_CHIPS, shard_cols_in), (1, 0, 2)).reshape(
        N_CHIPS, 2, shard_rows_in, shard_cols_in)
    gb = r["gw_out"].reshape(N_CHIPS, 2, shard_rows_out, D_MODEL)
    ra, rb = _pair_exchange(ga, gb)
    c_idx = jnp.reshape(cc_, (1,)).astype(jnp.int32)
    chip_idx = jnp.reshape(2 * cx_ + cy_, (1,)).astype(jnp.int32)
    pa = _pair_sum(ga, ra, c_idx)
    pb = _pair_sum(gb, rb, c_idx)
    xa, xb = _chip_exchange(pa, pb)
    ha = _chip_sum(pa, xa, chip_idx)
    hb = _chip_sum(pb, xb, chip_idx)
    fa, fb = _pair_share(ha, hb)
    g_w_in = fa.reshape(D_MODEL, shard_cols_in)
    g_w_out = fb.reshape(2 * shard_rows_out, D_MODEL)
    d_w_in, nm_w_in, nv_w_in = _adamw_big(w_in[0], g_w_in, m_w_in[0], v_w_in[0], LANE)
    d_w_out, nm_w_out, nv_w_out = _adamw_big(w_out[0], g_w_out, m_w_out[0], v_w_out[0], LANE)
    wide = lambda a: jnp.pad(a, ((0, 0), (0, D_MODEL - a.shape[1])))
    pack = jnp.concatenate([
        r["g_norm"], r["g_final"], jnp.concatenate([r["g_attn"], r["g_convg"]], axis=1), wide(r["g_bf"]),
        wide(r["loss"]), jnp.zeros((3, D_MODEL), F32), r["d_h"][PAD_ROWS:FRONT], wide(r["g_cw"])], axis=0)
    gpack = _gather_small(pack)
    params = (norm_g, final_g2, attn_norm_g, conv_norm_g, b_f, meta, conv_w[0])
    ms = (m_norm_g, m_final_norm_g.reshape(1, D_MODEL), m_attn_norm_g, m_conv_norm_g, m_b_f, m_meta, m_conv_w[0])
    vs = (v_norm_g, v_final_norm_g.reshape(1, D_MODEL), v_attn_norm_g, v_conv_norm_g, v_b_f, v_meta, v_conv_w[0])
    loss, g_s, d_s, m_s, v_s = _small_update(gpack, params, ms, vs)

    def ordered(small_list, big_in, big_out):
        s_norm, s_final, s_attn, s_convg, s_bf, s_meta, s_cw = small_list
        return (s_meta, s_norm, big_in[None], s_bf, s_cw[None], s_attn, s_convg, big_out[None], s_final.reshape(D_MODEL))

    return (loss.reshape(()), grad_x,
            *ordered(g_s, g_w_in, g_w_out), *ordered(d_s, d_w_in, d_w_out),
            *ordered(m_s, nm_w_in, nm_w_out), *ordered(v_s, nv_w_in, nv_w_out))
```

```python
import functools

import jax
import jax.numpy as jnp
from jax import lax
from jax.experimental import pallas as pl
from jax.experimental.pallas import tpu as pltpu

F32 = jnp.float32
MXU_DTYPE = jnp.bfloat16

D_MODEL = 1024
N_META = 16
HEADS = 8
HEAD_DIM = 64
D_ATTN = HEADS * HEAD_DIM
D_CONV = 512
EPS = 1e-6
LANE = 128
SUBLANE = 8
ROW_TILE = 384
FRONT = LANE
PAD_ROWS = FRONT - N_META
NEG = -1e30
N_CHIPS = 4
N_DEV = 8
VMEM_LIMIT_BYTES = 60 * 1024 * 1024

SEG_Q, SEG_K, SEG_V, SEG_F, SEG_ZA, SEG_GB, SEG_GC, SEG_XC, SEG_ZC = (
    0, 512, 1024, 1536, 1664, 2176, 2688, 3200, 3712)
D_IN = 4104
D_IN_PAD = 4224
F_END = 1544
GW_COL_TILE = 1408

ADAM_LR = 0.001
ADAM_B1 = 0.9
ADAM_B2 = 0.999
ADAM_EPS = 1e-08
ADAM_WD = 0.01
ADAM_STEP = 10

MESH = pl.DeviceIdType.MESH
ANY = pl.BlockSpec(memory_space=pl.ANY)

PACK_ROWS = 32
SLOT_NORM = (0, 1, 0, 1024)
SLOT_FINAL = (1, 2, 0, 1024)
SLOT_ATTN = (2, 3, 0, 512)
SLOT_CONVG = (2, 3, 512, 1024)
SLOT_BF = (3, 4, 0, 8)
SLOT_META = (8, 24, 0, 256)
SLOT_CONVW = (24, 27, 0, 128)
LOSS_ROW = 4


def _params(sem=None):
    return pltpu.CompilerParams(dimension_semantics=sem, vmem_limit_bytes=VMEM_LIMIT_BYTES)


def _sigmoid(z):
    return 1.0 / (1.0 + jnp.exp(-z))


def _dot(a, b):
    return jnp.dot(a, b, preferred_element_type=F32)


def _dot_nt(a, b):
    return lax.dot_general(a, b, (((1,), (1,)), ((), ())), preferred_element_type=F32)


def _dot_exact(a, b):
    return jnp.dot(a, b, preferred_element_type=F32, precision=lax.Precision.HIGHEST)


def _group_matrix():
    r = lax.broadcasted_iota(jnp.int32, (D_ATTN, D_ATTN), 0) >> 6
    c = lax.broadcasted_iota(jnp.int32, (D_ATTN, D_ATTN), 1) >> 6
    return jnp.where(r == c, 1.0, 0.0).astype(MXU_DTYPE)


def _group_sum(x, gmat):
    hi = x.astype(MXU_DTYPE)
    lo = (x - hi.astype(F32)).astype(MXU_DTYPE)
    return _dot(hi, gmat) + _dot(lo, gmat)


def _x_block_specs(n_sub, rows):
    specs = [pl.BlockSpec((rows, D_MODEL), lambda i: (jnp.maximum(n_sub * i - 1, 0), 0))]
    for b in range(1, n_sub):
        specs.append(pl.BlockSpec((rows, D_MODEL), functools.partial(lambda i, b: (n_sub * i - 1 + b, 0), b=b)))
    return specs


def _position():
    return lax.axis_index("x"), lax.axis_index("y"), lax.axis_index("c")


def _gather_weights(wi, wo, small):
    def body(wi_ref, wo_ref, sm_ref, gwi_ref, gwo_ref, gsm_ref, send_sems, recv_sems, local_sems):
        x, y, c = _position()
        me = 2 * x + y
        sibling = (x, y, 1 - c)
        chips = [(1 - x, y), (x, 1 - y), (1 - x, 1 - y)]
        locals_ = [
            pltpu.make_async_copy(wi_ref, gwi_ref.at[me], local_sems.at[0]),
            pltpu.make_async_copy(wo_ref, gwo_ref.at[me], local_sems.at[1]),
            pltpu.make_async_copy(sm_ref, gsm_ref.at[me], local_sems.at[2]),
        ]
        for cp in locals_:
            cp.start()

        def remote(k, src, dst, to):
            return pltpu.make_async_remote_copy(src_ref=src, dst_ref=dst, send_sem=send_sems.at[k],
                                                recv_sem=recv_sems.at[k], device_id=to, device_id_type=MESH)

        first, passed, smalls = [], [], []
        for a, (src_ref, g_ref) in enumerate(((wi_ref, gwi_ref), (wo_ref, gwo_ref))):
            for j, (cx, cy) in enumerate(chips):
                first.append(remote(6 * a + j, src_ref.at[c], g_ref.at[me, c], (cx, cy, c)))
                slot = g_ref.at[2 * cx + cy, c]
                passed.append(remote(6 * a + 3 + j, slot, slot, sibling))
        for j, (cx, cy) in enumerate(chips):
            smalls.append(remote(12 + j, sm_ref, gsm_ref.at[me], (cx, cy, c)))
        for cp in first + smalls:
            cp.start()
        n = 0
        for a, g_ref in enumerate((gwi_ref, gwo_ref)):
            for j, (cx, cy) in enumerate(chips):
                slot = g_ref.at[2 * cx + cy, c]
                remote(6 * a + j, slot, slot, sibling).wait_recv()
                passed[n].start()
                n += 1
        for a, g_ref in enumerate((gwi_ref, gwo_ref)):
            for j, (cx, cy) in enumerate(chips):
                slot = g_ref.at[2 * cx + cy, 1 - c]
                remote(6 * a + 3 + j, slot, slot, sibling).wait_recv()
        for j, (cx, cy) in enumerate(chips):
            remote(12 + j, sm_ref, gsm_ref.at[2 * cx + cy], sibling).wait_recv()
        for cp in first + smalls + passed:
            cp.wait_send()
        for cp in locals_:
            cp.wait()

    return pl.pallas_call(
        body, name="gather_weights",
        out_shape=(jax.ShapeDtypeStruct((N_CHIPS,) + wi.shape, wi.dtype),
                   jax.ShapeDtypeStruct((N_CHIPS,) + wo.shape, wo.dtype),
                   jax.ShapeDtypeStruct((N_CHIPS,) + small.shape, small.dtype)),
        in_specs=[ANY, ANY, ANY], out_specs=(ANY, ANY, ANY),
        scratch_shapes=[pltpu.SemaphoreType.DMA((15,)), pltpu.SemaphoreType.DMA((15,)), pltpu.SemaphoreType.DMA((3,))],
    )(wi, wo, small)


def _pair_exchange(ga, gb):
    def body(ga_ref, gb_ref, ra_ref, rb_ref, send_sems, recv_sems):
        x, y, c = _position()
        sibling = (x, y, 1 - c)
        copies = []
        for k, (src, dst) in enumerate(((ga_ref, ra_ref), (gb_ref, rb_ref))):
            copies.append(pltpu.make_async_remote_copy(
                src_ref=src.at[:, 1 - c], dst_ref=dst, send_sem=send_sems.at[k], recv_sem=recv_sems.at[k],
                device_id=sibling, device_id_type=MESH))
        for cp in copies:
            cp.start()
        for cp in copies:
            cp.wait()

    return pl.pallas_call(
        body, name="grad_pair_exchange",
        out_shape=(jax.ShapeDtypeStruct((N_CHIPS,) + ga.shape[2:], ga.dtype),
                   jax.ShapeDtypeStruct((N_CHIPS,) + gb.shape[2:], gb.dtype)),
        in_specs=[ANY, ANY], out_specs=(ANY, ANY),
        scratch_shapes=[pltpu.SemaphoreType.DMA((2,)), pltpu.SemaphoreType.DMA((2,))],
    )(ga, gb)


def _chip_exchange(pa, pb):
    def body(pa_ref, pb_ref, ra_ref, rb_ref, send_sems, recv_sems):
        x, y, c = _position()
        chips = [(1 - x, y), (x, 1 - y), (1 - x, 1 - y)]
        copies = []
        for a, (src, dst) in enumerate(((pa_ref, ra_ref), (pb_ref, rb_ref))):
            for j, (cx, cy) in enumerate(chips):
                copies.append(pltpu.make_async_remote_copy(
                    src_ref=src.at[2 * cx + cy], dst_ref=dst.at[j], send_sem=send_sems.at[3 * a + j],
                    recv_sem=recv_sems.at[3 * a + j], device_id=(cx, cy, c), device_id_type=MESH))
        for cp in copies:
            cp.start()
        for cp in copies:
            cp.wait()

    return pl.pallas_call(
        body, name="grad_chip_exchange",
        out_shape=(jax.ShapeDtypeStruct((3,) + pa.shape[1:], pa.dtype),
                   jax.ShapeDtypeStruct((3,) + pb.shape[1:], pb.dtype)),
        in_specs=[ANY, ANY], out_specs=(ANY, ANY),
        scratch_shapes=[pltpu.SemaphoreType.DMA((6,)), pltpu.SemaphoreType.DMA((6,))],
    )(pa, pb)


def _pair_share(ha, hb):
    def body(ha_ref, hb_ref, oa_ref, ob_ref, send_sems, recv_sems, local_sems):
        x, y, c = _position()
        sibling = (x, y, 1 - c)
        locals_, copies = [], []
        for k, (src, dst) in enumerate(((ha_ref, oa_ref), (hb_ref, ob_ref))):
            locals_.append(pltpu.make_async_copy(src, dst.at[c], local_sems.at[k]))
            copies.append(pltpu.make_async_remote_copy(
                src_ref=src, dst_ref=dst.at[c], send_sem=send_sems.at[k], recv_sem=recv_sems.at[k],
                device_id=sibling, device_id_type=MESH))
        for cp in locals_ + copies:
            cp.start()
        for k, (src, dst) in enumerate(((ha_ref, oa_ref), (hb_ref, ob_ref))):
            pltpu.make_async_remote_copy(
                src_ref=src, dst_ref=dst.at[1 - c], send_sem=send_sems.at[k], recv_sem=recv_sems.at[k],
                device_id=sibling, device_id_type=MESH).wait_recv()
        for cp in copies:
            cp.wait_send()
        for cp in locals_:
            cp.wait()

    return pl.pallas_call(
        body, name="grad_pair_share",
        out_shape=(jax.ShapeDtypeStruct((2,) + ha.shape, ha.dtype), jax.ShapeDtypeStruct((2,) + hb.shape, hb.dtype)),
        in_specs=[ANY, ANY], out_specs=(ANY, ANY),
        scratch_shapes=[pltpu.SemaphoreType.DMA((2,)), pltpu.SemaphoreType.DMA((2,)), pltpu.SemaphoreType.DMA((2,))],
    )(ha, hb)


def _gather_small(pack):
    def body(p_ref, o_ref, send_sems, recv_sems, local_sem):
        x, y, c = _position()
        me = 4 * x + 2 * y + c
        mine = pltpu.make_async_copy(p_ref, o_ref.at[me], local_sem)
        mine.start()
        sends = []
        for mask in range(1, N_DEV):
            px = 1 - x if mask & 4 else x
            py = 1 - y if mask & 2 else y
            pc = 1 - c if mask & 1 else c
            sends.append((pltpu.make_async_remote_copy(
                src_ref=p_ref, dst_ref=o_ref.at[me], send_sem=send_sems.at[mask - 1], recv_sem=recv_sems.at[mask - 1],
                device_id=(px, py, pc), device_id_type=MESH), 4 * px + 2 * py + pc, mask))
        for cp, _, _ in sends:
            cp.start()
        for _, peer, mask in sends:
            pltpu.make_async_remote_copy(
                src_ref=p_ref, dst_ref=o_ref.at[peer], send_sem=send_sems.at[mask - 1], recv_sem=recv_sems.at[mask - 1],
                device_id=(x, y, c), device_id_type=MESH).wait_recv()
        for cp, _, _ in sends:
            cp.wait_send()
        mine.wait()

    return pl.pallas_call(
        body, name="gather_small",
        out_shape=jax.ShapeDtypeStruct((N_DEV,) + pack.shape, pack.dtype),
        in_specs=[ANY], out_specs=ANY,
        scratch_shapes=[pltpu.SemaphoreType.DMA((N_DEV - 1,)), pltpu.SemaphoreType.DMA((N_DEV - 1,)),
                        pltpu.SemaphoreType.DMA],
    )(pack)


def _pair_sum(mine, recv, c_idx):
    rows, cols = mine.shape[2:]

    def body(c_ref, a_ref, b_ref, o_ref):
        o_ref[...] = a_ref[...] + b_ref[...]

    return pl.pallas_call(
        body, name="grad_pair_sum",
        grid_spec=pltpu.PrefetchScalarGridSpec(
            num_scalar_prefetch=1, grid=(N_CHIPS,),
            in_specs=[pl.BlockSpec((None, None, rows, cols), lambda s, c_ref: (s, c_ref[0], 0, 0)),
                      pl.BlockSpec((None, rows, cols), lambda s, c_ref: (s, 0, 0))],
            out_specs=pl.BlockSpec((None, rows, cols), lambda s, c_ref: (s, 0, 0))),
        out_shape=jax.ShapeDtypeStruct(recv.shape, recv.dtype),
        compiler_params=_params(("parallel",)),
    )(c_idx, mine, recv)


def _chip_sum(psum, recv3, chip_idx):
    rows, cols = psum.shape[1:]
    tr = rows // 2

    def body(s_ref, p_ref, r0, r1, r2, o_ref):
        o_ref[...] = ((p_ref[...] + r0[...]) + r1[...]) + r2[...]

    return pl.pallas_call(
        body, name="grad_chip_sum",
        grid_spec=pltpu.PrefetchScalarGridSpec(
            num_scalar_prefetch=1, grid=(2,),
            in_specs=[pl.BlockSpec((None, tr, cols), lambda i, s_ref: (s_ref[0], i, 0))] +
                     [pl.BlockSpec((None, tr, cols), functools.partial(lambda i, s_ref, j: (j, i, 0), j=j))
                      for j in range(3)],
            out_specs=pl.BlockSpec((tr, cols), lambda i, s_ref: (i, 0))),
        out_shape=jax.ShapeDtypeStruct((rows, cols), psum.dtype),
        compiler_params=_params(("parallel",)),
    )(chip_idx, psum, recv3, recv3, recv3)


def _adamw_math(w, g, m, v):
    m = ADAM_B1 * m + (1.0 - ADAM_B1) * g
    v = ADAM_B2 * v + (1.0 - ADAM_B2) * (g * g)
    m_hat = m / (1.0 - ADAM_B1 ** ADAM_STEP)
    v_hat = v / (1.0 - ADAM_B2 ** ADAM_STEP)
    delta = -ADAM_LR * (m_hat / (jnp.sqrt(v_hat) + ADAM_EPS) + ADAM_WD * w)
    return delta, m, v


def _adamw_big(w, g, m, v, tr):
    rows, cols = w.shape

    def body(w_ref, g_ref, m_ref, v_ref, d_out, m_out, v_out):
        d, m2, v2 = _adamw_math(w_ref[...], g_ref[...], m_ref[...], v_ref[...])
        d_out[...] = d
        m_out[...] = m2
        v_out[...] = v2

    spec = pl.BlockSpec((tr, cols), lambda i: (i, 0))
    sds = jax.ShapeDtypeStruct((rows, cols), F32)
    return pl.pallas_call(
        body, name="adamw_big", grid=(rows // tr,), in_specs=[spec] * 4, out_specs=(spec,) * 3,
        out_shape=(sds,) * 3, compiler_params=_params(("parallel",)),
    )(w, g, m, v)


def _small_update(gpack, params, ms, vs):
    slots = (SLOT_NORM, SLOT_FINAL, SLOT_ATTN, SLOT_CONVG, SLOT_BF, SLOT_META, SLOT_CONVW)
    n = len(slots)

    def body(*refs):
        gp_ref = refs[0]
        w_refs, m_refs, v_refs = refs[1:1 + n], refs[1 + n:1 + 2 * n], refs[1 + 2 * n:1 + 3 * n]
        outs = refs[1 + 3 * n:2 + 7 * n]
        loss_ref = outs[0]
        g_outs, d_outs, m_outs, v_outs = (outs[1 + k * n:1 + (k + 1) * n] for k in range(4))
        g_scr, w_scr, m_scr, v_scr = refs[2 + 7 * n:]
        x, y, _ = _position()
        shard = 2 * x + y
        tot = gp_ref[0]
        for d in range(1, N_DEV):
            tot = tot + gp_ref[d]
        r0, r1, _, _ = SLOT_META
        meta_sel = tot[r0:r1, 0:256]
        cw_sel = tot[24:32, 0:128]
        for k in range(1, N_CHIPS):
            meta_sel = jnp.where(shard == k, tot[r0:r1, 256 * k:256 * (k + 1)], meta_sel)
            cw_sel = jnp.where(shard == k, tot[24:32, 128 * k:128 * (k + 1)], cw_sel)
        zeros = jnp.zeros((PACK_ROWS, D_MODEL), F32)
        for scr in (g_scr, w_scr, m_scr, v_scr):
            scr[...] = zeros
        g_scr[0:8, :] = tot[0:8, :]
        g_scr[r0:r1, 0:256] = meta_sel
        g_scr[24:32, 0:128] = cw_sel
        for (a, b, c0, c1), w_ref, m_ref, v_ref in zip(slots, w_refs, m_refs, v_refs):
            w_scr[a:b, c0:c1] = w_ref[...]
            m_scr[a:b, c0:c1] = m_ref[...]
            v_scr[a:b, c0:c1] = v_ref[...]
        loss_ref[...] = g_scr[LOSS_ROW:LOSS_ROW + 1, 0:1]
        d, m2, v2 = _adamw_math(w_scr[...], g_scr[...], m_scr[...], v_scr[...])
        w_scr[...] = d
        m_scr[...] = m2
        v_scr[...] = v2
        for (a, b, c0, c1), g_o, d_o, m_o, v_o in zip(slots, g_outs, d_outs, m_outs, v_outs):
            g_o[...] = g_scr[a:b, c0:c1]
            d_o[...] = w_scr[a:b, c0:c1]
            m_o[...] = m_scr[a:b, c0:c1]
            v_o[...] = v_scr[a:b, c0:c1]

    shapes = [jax.ShapeDtypeStruct(p.shape, F32) for p in params]
    out = pl.pallas_call(
        body, name="small_update",
        out_shape=[jax.ShapeDtypeStruct((1, 1), F32)] + shapes * 4,
        scratch_shapes=[pltpu.VMEM((PACK_ROWS, D_MODEL), F32)] * 4,
        compiler_params=_params(),
    )(gpack, *params, *ms, *vs)
    return out[0], out[1:1 + n], out[1 + n:1 + 2 * n], out[1 + 2 * n:1 + 3 * n], out[1 + 3 * n:1 + 4 * n]


def _in_proj(x2, meta_blk, norm_g, w_pad, bf_pad):
    seq = x2.shape[0]
    lp = seq + FRONT
    t = ROW_TILE
    nt = lp // t
    n_sub = t // LANE

    def body(*refs):
        x_refs = refs[:n_sub]
        mb, g_ref, w_ref, bf_ref = refs[n_sub:n_sub + 4]
        q_ref, k_ref, v_ref, rest_ref, fl_ref, ct_ref, ut_ref, carry = refs[n_sub + 4:]
        i = pl.program_id(0)

        @pl.when(i == 0)
        def _():
            carry[...] = jnp.zeros_like(carry)

        first = jnp.where(i == 0, mb[...], x_refs[0][...])
        h = jnp.concatenate([first] + [r[...] for r in x_refs[1:]], axis=0)
        ms = jnp.mean(h * h, axis=-1, keepdims=True)
        u32 = (h * lax.rsqrt(ms + EPS)) * g_ref[...]
        u = u32.astype(MXU_DTYPE)
        ut_ref[...] = u32.T.astype(MXU_DTYPE)

        def seg(a, width):
            return _dot(u, w_ref[:, a:a + width])

        q_ref[...] = (seg(SEG_Q, D_ATTN) * (HEAD_DIM ** -0.5)).astype(MXU_DTYPE)
        k_ref[...] = seg(SEG_K, D_ATTN).astype(MXU_DTYPE)
        v_ref[...] = seg(SEG_V, D_ATTN).astype(MXU_DTYPE)
        for s in range(5):
            rest_ref[:, 512 * s:512 * (s + 1)] = seg(SEG_ZA + 512 * s, 512)
        fl = seg(SEG_F, LANE)
        fl_ref[...] = fl
        z = fl + bf_ref[...]
        logf = jnp.minimum(z, 0.0) - jnp.log(1.0 + jnp.exp(-jnp.abs(z)))
        row = i * t + lax.broadcasted_iota(jnp.int32, (t, LANE), 0)
        logf = jnp.where(row >= PAD_ROWS, logf, 0.0)
        tri = jnp.where(lax.broadcasted_iota(jnp.int32, (t, t), 0) >= lax.broadcasted_iota(jnp.int32, (t, t), 1),
                        1.0, 0.0)
        cs = _dot_exact(tri, logf) + carry[...]
        carry[...] = carry[...] + jnp.sum(logf, axis=0, keepdims=True)
        col = i * t + lax.broadcasted_iota(jnp.int32, (SUBLANE, t), 1)
        ct_ref[...] = jnp.where(col >= PAD_ROWS, cs.T[0:SUBLANE, :], -NEG)

    row_blk = lambda cols: pl.BlockSpec((t, cols), lambda i: (i, 0))
    const = lambda shape: pl.BlockSpec(shape, lambda i: (0, 0))
    return pl.pallas_call(
        body, name="in_proj", grid=(nt,),
        in_specs=_x_block_specs(n_sub, LANE) + [const((LANE, D_MODEL)), const((1, D_MODEL)),
                                                pl.BlockSpec((D_MODEL, D_IN_PAD), lambda i: (0, 0),
                                                             pipeline_mode=pl.Buffered(1)),
                                                const((1, LANE))],
        out_specs=(row_blk(D_ATTN), row_blk(D_ATTN), row_blk(D_ATTN), row_blk(5 * 512), row_blk(LANE),
                   pl.BlockSpec((SUBLANE, t), lambda i: (0, i)),
                   pl.BlockSpec((D_MODEL, t), lambda i: (0, i))),
        out_shape=(jax.ShapeDtypeStruct((lp, D_ATTN), MXU_DTYPE), jax.ShapeDtypeStruct((lp, D_ATTN), MXU_DTYPE),
                   jax.ShapeDtypeStruct((lp, D_ATTN), MXU_DTYPE), jax.ShapeDtypeStruct((lp, 5 * 512), F32),
                   jax.ShapeDtypeStruct((lp, LANE), F32),
                   jax.ShapeDtypeStruct((SUBLANE, lp), F32), jax.ShapeDtypeStruct((D_MODEL, lp), MXU_DTYPE)),
        scratch_shapes=[pltpu.VMEM((1, LANE), F32)],
        compiler_params=_params(("arbitrary",)),
    )(*([x2] * n_sub), meta_blk, norm_g, w_pad, bf_pad)


def _head_masks():
    lane = lax.broadcasted_iota(jnp.int32, (1, LANE), 1)
    return lane < HEAD_DIM, lane >= HEAD_DIM


def _pair_specs(lp, nt, t):
    blk = pl.BlockSpec((lp, LANE), lambda g: (0, g))
    ct_a = pl.BlockSpec((None, nt, 1, t), lambda g: (2 * g, 0, 0, 0))
    ct_b = pl.BlockSpec((None, nt, 1, t), lambda g: (2 * g + 1, 0, 0, 0))
    return blk, ct_a, ct_b


def _sub_rows(s, col):
    return jnp.concatenate([s[:, a * LANE:(a + 1) * LANE] - col for a in range(s.shape[1] // LANE)], axis=1)


def _loop_by_two(lo, hi, step, init):
    def pair(jj, carry):
        return step(lo + 2 * jj + 1, step(lo + 2 * jj, carry))

    pairs = (hi - lo) // 2
    carry = lax.fori_loop(0, pairs, pair, init)
    return lax.fori_loop(lo + 2 * pairs, hi, step, carry)


def _lane_chunks(s):
    return [s[:, a * LANE:(a + 1) * LANE] for a in range(s.shape[1] // LANE)]


def _attn_fwd(q, k, v, ct4):
    lp = q.shape[0]
    t = ROW_TILE
    nt = lp // t

    def body(q_ref, k_ref, v_ref, cta_ref, ctb_ref, o_ref, l_ref, m_ref):
        masks = _head_masks()
        ct_refs = (cta_ref, ctb_ref)
        below = lax.broadcasted_iota(jnp.int32, (t, t), 1) <= lax.broadcasted_iota(jnp.int32, (t, t), 0)
        lane = lax.broadcasted_iota(jnp.int32, (1, LANE), 1)
        head_of_row = lax.broadcasted_iota(jnp.int32, (2 * t, LANE), 0) >= t
        ones_cols = jnp.where(lax.broadcasted_iota(jnp.int32, (2 * t, LANE), 1) == head_of_row.astype(jnp.int32),
                              1.0, 0.0).astype(MXU_DTYPE)

        def q_block(i, _):
            r0 = pl.multiple_of(i * t, t)
            qi = q_ref[pl.ds(r0, t), :]

            def scores(j):
                kj = k_ref[pl.ds(pl.multiple_of(j * t, t), t), :]
                return _dot_nt(qi, jnp.concatenate([jnp.where(hm, kj, 0).astype(MXU_DTYPE) for hm in masks], axis=0))

            def biased(j, hh, s2, diagonal):
                s = s2[:, hh * t:(hh + 1) * t] - ct_refs[hh][j]
                return jnp.where(below, s, NEG) if diagonal else s

            def max_step(j, carry, diagonal):
                s2 = scores(j)
                out = []
                for hh, m in enumerate(carry):
                    for c in _lane_chunks(biased(j, hh, s2, diagonal)):
                        m = jnp.maximum(m, c)
                    out.append(m)
                return tuple(out)

            lanes_neg = jnp.full((t, LANE), NEG, F32)
            carry = _loop_by_two(0, i, functools.partial(max_step, diagonal=False), (lanes_neg, lanes_neg))
            ms = [jnp.max(m, axis=-1, keepdims=True) for m in max_step(i, carry, True)]

            def sum_step(j, acc, diagonal):
                s2 = scores(j)
                vj = v_ref[pl.ds(pl.multiple_of(j * t, t), t), :]
                v2 = jnp.concatenate([jnp.where(hm, vj, 0).astype(MXU_DTYPE) for hm in masks], axis=0)
                parts = [jnp.exp(biased(j, hh, s2, diagonal) - ms[hh]).astype(MXU_DTYPE) for hh in range(2)]
                return acc + _dot(jnp.concatenate(parts, axis=1), jnp.concatenate([v2, ones_cols], axis=1))

            acc = _loop_by_two(0, i, functools.partial(sum_step, diagonal=False), jnp.zeros((t, 2 * LANE), F32))
            acc = sum_step(i, acc, True)
            sums = acc[:, LANE:]
            l_pair = jnp.where(masks[0], jnp.sum(jnp.where(lane == 0, sums, 0.0), axis=-1, keepdims=True),
                               jnp.sum(jnp.where(lane == 1, sums, 0.0), axis=-1, keepdims=True))
            o_ref[pl.ds(r0, t), :] = acc[:, :LANE] / l_pair
            l_ref[pl.ds(r0, t), :] = l_pair
            m_ref[pl.ds(r0, t), 0:LANE] = jnp.broadcast_to(ms[0], (t, LANE))
            m_ref[pl.ds(r0, t), LANE:2 * LANE] = jnp.broadcast_to(ms[1], (t, LANE))
            return 0

        lax.fori_loop(0, nt, q_block, 0)

    blk, ct_a, ct_b = _pair_specs(lp, nt, t)
    return pl.pallas_call(
        body, name="attn_fwd", grid=(HEADS // 2,),
        in_specs=[blk, blk, blk, ct_a, ct_b], out_specs=(blk, blk, pl.BlockSpec((lp, 2 * LANE), lambda g: (0, g))),
        out_shape=(jax.ShapeDtypeStruct((lp, D_ATTN), F32), jax.ShapeDtypeStruct((lp, D_ATTN), F32),
                   jax.ShapeDtypeStruct((lp, HEADS * LANE), F32)),
        compiler_params=_params(("parallel",)),
    )(q, k, v, ct4, ct4)


def _attn_bwd(q, k, v, do, m, delta, ct4):
    lp = q.shape[0]
    t = ROW_TILE
    nt = lp // t

    def body(q_ref, k_ref, v_ref, do_ref, ma_ref, mb_ref, dla_ref, dlb_ref, cta_ref, ctb_ref,
             dq_ref, dk_ref, dv_ref, dc_ref, dq_acc, dk_acc, dv_acc):
        masks = _head_masks()
        ct_refs, m_refs, dl_refs = (cta_ref, ctb_ref), (ma_ref, mb_ref), (dla_ref, dlb_ref)
        below = lax.broadcasted_iota(jnp.int32, (t, t), 1) <= lax.broadcasted_iota(jnp.int32, (t, t), 0)
        tn = (((0,), (0,)), ((), ()))
        dq_acc[...] = jnp.zeros_like(dq_acc)

        def k_block(j, _):
            c0 = pl.multiple_of(j * t, t)
            kj = k_ref[pl.ds(c0, t), :]
            vj = v_ref[pl.ds(c0, t), :]
            k2 = jnp.concatenate([jnp.where(hm, kj, 0).astype(MXU_DTYPE) for hm in masks], axis=0)
            v2 = jnp.concatenate([jnp.where(hm, vj, 0).astype(MXU_DTYPE) for hm in masks], axis=0)
            ck = [r[j] for r in ct_refs]
            dk_acc[...] = jnp.zeros_like(dk_acc)
            dv_acc[...] = jnp.zeros_like(dv_acc)

            def q_block(i, colsums, diagonal):
                r0 = pl.multiple_of(i * t, t)
                qi = q_ref[pl.ds(r0, t), :]
                doi = do_ref[pl.ds(r0, t), :]
                q2 = jnp.concatenate([jnp.where(hm, qi, 0).astype(MXU_DTYPE) for hm in masks], axis=0)
                do2 = jnp.concatenate([jnp.where(hm, doi, 0).astype(MXU_DTYPE) for hm in masks], axis=0)
                s2 = _dot_nt(qi, k2)
                dp2 = _dot_nt(doi, v2)
                out, ps, dss = [], [], []
                for hh in range(2):
                    s = s2[:, hh * t:(hh + 1) * t] - ck[hh]
                    if diagonal:
                        s = jnp.where(below, s, NEG)
                    p = jnp.exp(_sub_rows(s, m_refs[hh][pl.ds(r0, t), :])).astype(MXU_DTYPE)
                    ds32 = p.astype(F32) * _sub_rows(dp2[:, hh * t:(hh + 1) * t], dl_refs[hh][pl.ds(r0, t), :])
                    ps.append(p)
                    dss.append(ds32.astype(MXU_DTYPE))
                    out.append(colsums[hh] + jnp.sum(ds32, axis=0, keepdims=True))
                dv_acc[...] = dv_acc[...] + lax.dot_general(jnp.concatenate(ps, axis=0), do2, tn,
                                                            preferred_element_type=F32)
                dk_acc[...] = dk_acc[...] + lax.dot_general(jnp.concatenate(dss, axis=0), q2, tn,
                                                            preferred_element_type=F32)
                dq_acc[pl.ds(r0, t), :] = dq_acc[pl.ds(r0, t), :] + _dot(jnp.concatenate(dss, axis=1), k2)
                return tuple(out)

            colsums = q_block(j, (jnp.zeros((1, t), F32), jnp.zeros((1, t), F32)), True)
            colsums = lax.fori_loop(j + 1, nt, functools.partial(q_block, diagonal=False), colsums)
            for hh in range(2):
                dc_ref[hh, j] = -colsums[hh]
            dk_ref[pl.ds(c0, t), :] = dk_acc[...].astype(dk_ref.dtype)
            dv_ref[pl.ds(c0, t), :] = dv_acc[...].astype(dv_ref.dtype)
            return 0

        lax.fori_loop(0, nt, k_block, 0)
        dq_ref[...] = (dq_acc[...] * (HEAD_DIM ** -0.5)).astype(dq_ref.dtype)

    blk, ct_a, ct_b = _pair_specs(lp, nt, t)
    rep_a = pl.BlockSpec((lp, LANE), lambda g: (0, 2 * g))
    rep_b = pl.BlockSpec((lp, LANE), lambda g: (0, 2 * g + 1))
    return pl.pallas_call(
        body, name="attn_bwd", grid=(HEADS // 2,),
        in_specs=[blk] * 4 + [rep_a, rep_b, rep_a, rep_b, ct_a, ct_b],
        out_specs=(blk, blk, blk, pl.BlockSpec((2, nt, 1, t), lambda g: (g, 0, 0, 0))),
        out_shape=(jax.ShapeDtypeStruct((lp, D_ATTN), MXU_DTYPE),) * 3
                  + (jax.ShapeDtypeStruct((HEADS, nt, 1, t), F32),),
        scratch_shapes=[pltpu.VMEM((lp, LANE), F32), pltpu.VMEM((t, LANE), F32), pltpu.VMEM((t, LANE), F32)],
        compiler_params=_params(("parallel",)),
    )(q, k, v, do, m, m, delta, delta, ct4, ct4)


def _shift_down(prev8, cur, k):
    ext = jnp.concatenate([prev8, cur], axis=0)
    return pltpu.roll(ext, k, 0)[SUBLANE:, :]


def _shift_up(cur, next8, k):
    ext = jnp.concatenate([cur, next8], axis=0)
    n = ext.shape[0]
    return pltpu.roll(ext, n - k, 0)[:cur.shape[0], :]


def _post(o, l_sum, rest, x2, meta_blk, tgt2, w_out, attn_g, conv_g, final_g, conv_w8):
    lp = o.shape[0]
    t = ROW_TILE
    nt = lp // t
    n_sub = t // LANE
    hb = t // SUBLANE

    def body(*refs):
        o_ref, l_ref, za_ref, gb_ref, gc_ref, xc_ref, zc_ref, gch_ref, xch_ref = refs[:9]
        refs = refs[1:]
        x_refs = refs[8:8 + n_sub]
        mb = refs[8 + n_sub]
        t_refs = refs[9 + n_sub:9 + 2 * n_sub]
        wo_ref, ag_ref, cg_ref, fg_ref, cw_ref = refs[9 + 2 * n_sub:14 + 2 * n_sub]
        (dout_ref, do_ref, dl_ref, dza_ref, dgb_ref, dzc_ref, dcv_ref,
         loss_ref, gf_ref, gag_ref, gcg_ref, gwo_ref) = refs[14 + 2 * n_sub:]
        i = pl.program_id(0)

        @pl.when(i == 0)
        def _():
            for r in (loss_ref, gf_ref, gag_ref, gcg_ref, gwo_ref):
                r[...] = jnp.zeros_like(r)

        gmat = _group_matrix()
        inv_g = 1.0 / HEAD_DIM
        o_v = o_ref[...]
        ra = lax.rsqrt(_group_sum(o_v * o_v, gmat) * inv_g + EPS)
        n_a = o_v * ra
        a_n = n_a * ag_ref[...]
        za = za_ref[...]
        sig_a = _sigmoid(za)
        sz_a = za * sig_a
        y_a = a_n * sz_a
        gb = gb_ref[...]
        gc = gc_ref[...]
        xc = xc_ref[...]
        cx = gc * xc
        cx_prev = jnp.where(i == 0, 0.0, gch_ref[...] * xch_ref[...])
        conv = (cw_ref[0:1, :] * _shift_down(cx_prev, cx, 2) + cw_ref[1:2, :] * _shift_down(cx_prev, cx, 1)
                + cw_ref[2:3, :] * cx)
        e = gb * conv
        re = lax.rsqrt(_group_sum(e * e, gmat) * inv_g + EPS)
        n_e = e * re
        e_n = n_e * cg_ref[...]
        zc = zc_ref[...]
        sig_c = _sigmoid(zc)
        sz_c = zc * sig_c
        y_c = e_n * sz_c
        mix = jnp.concatenate([y_a, y_c], axis=-1)
        mix_b = mix.astype(MXU_DTYPE)
        first = jnp.where(i == 0, mb[...], x_refs[0][...])
        h = jnp.concatenate([first] + [r[...] for r in x_refs[1:]], axis=0)
        out = h + _dot(mix_b, wo_ref[...])
        r2 = lax.rsqrt(jnp.mean(out * out, axis=-1, keepdims=True) + EPS)
        n_f = out * r2
        y = n_f * fg_ref[...]
        tgt = jnp.concatenate([r[...] for r in t_refs], axis=0)
        valid = (i * t + lax.broadcasted_iota(jnp.int32, (t, 1), 0)) >= FRONT
        diff = jnp.where(valid, y - tgt, 0.0)
        loss_ref[...] = loss_ref[...] + 0.5 * jnp.sum(jnp.sum(diff * diff, axis=-1, keepdims=True) * (1.0 / D_MODEL))
        dy = diff * (1.0 / D_MODEL)
        gf_ref[...] = gf_ref[...] + jnp.sum(dy * n_f, axis=0, keepdims=True)
        dn = dy * fg_ref[...]
        d_out = r2 * (dn - n_f * jnp.mean(dn * n_f, axis=-1, keepdims=True))
        dout_ref[...] = d_out
        d_out_b = d_out.astype(MXU_DTYPE)
        d_mix = _dot_nt(d_out_b, wo_ref[...])
        gwo_ref[...] = gwo_ref[...] + _dot(mix.T.astype(MXU_DTYPE), d_out_b)
        d_ya = d_mix[:, :D_ATTN]
        d_yc = d_mix[:, D_ATTN:]
        d_an = d_ya * sz_a
        dza_ref[...] = (d_ya * a_n * (sig_a * (1.0 + za * (1.0 - sig_a)))).astype(dza_ref.dtype)
        gag_ref[...] = gag_ref[...] + jnp.sum(d_an * n_a, axis=0, keepdims=True)
        dn_a = d_an * ag_ref[...]
        d_o = ra * (dn_a - n_a * (_group_sum(dn_a * n_a, gmat) * inv_g))
        d_o_b = (d_o / l_ref[...]).astype(do_ref.dtype)
        do_ref[...] = d_o_b
        head_rep = jnp.where((lax.broadcasted_iota(jnp.int32, (D_ATTN, HEADS * LANE), 0) >> 6)
                             == (lax.broadcasted_iota(jnp.int32, (D_ATTN, HEADS * LANE), 1) >> 7), 1.0, 0.0)
        dl_ref[...] = _group_sum(d_o_b.astype(F32) * o_v, head_rep.astype(MXU_DTYPE))
        d_en = d_yc * sz_c
        dzc_ref[...] = (d_yc * e_n * (sig_c * (1.0 + zc * (1.0 - sig_c)))).astype(dzc_ref.dtype)
        gcg_ref[...] = gcg_ref[...] + jnp.sum(d_en * n_e, axis=0, keepdims=True)
        dn_e = d_en * cg_ref[...]
        d_e = re * (dn_e - n_e * (_group_sum(dn_e * n_e, gmat) * inv_g))
        dgb_ref[...] = (d_e * conv).astype(dgb_ref.dtype)
        dcv_ref[...] = d_e * gb

    row_blk = lambda cols: pl.BlockSpec((t, cols), lambda i: (i, 0))
    rest_blk = lambda s: pl.BlockSpec((t, 512), functools.partial(lambda i, s: (i, s), s=s))
    halo = lambda s: pl.BlockSpec((SUBLANE, 512), functools.partial(lambda i, s: (jnp.maximum(i * hb - 1, 0), s), s=s))
    const = lambda shape: pl.BlockSpec(shape, lambda i: (0, 0))
    acc = lambda shape: pl.BlockSpec(shape, lambda i: (0, 0))
    return pl.pallas_call(
        body, name="post_fwd_bwd", grid=(nt,),
        in_specs=[row_blk(D_ATTN), row_blk(D_ATTN)] + [rest_blk(s) for s in range(5)] + [halo(2), halo(3)]
                 + _x_block_specs(n_sub, LANE) + [const((LANE, D_MODEL))] + _x_block_specs(n_sub, LANE)
                 + [const((D_MODEL, D_MODEL)), const((1, D_ATTN)), const((1, D_CONV)), const((1, D_MODEL)),
                    const((SUBLANE, D_CONV))],
        out_specs=(row_blk(D_MODEL), row_blk(D_ATTN), row_blk(HEADS * LANE), row_blk(D_ATTN), row_blk(D_CONV),
                   row_blk(D_CONV), row_blk(D_CONV),
                   acc((1, LANE)), acc((1, D_MODEL)), acc((1, D_ATTN)), acc((1, D_CONV)), acc((D_MODEL, D_MODEL))),
        out_shape=(jax.ShapeDtypeStruct((lp, D_MODEL), F32), jax.ShapeDtypeStruct((lp, D_ATTN), MXU_DTYPE),
                   jax.ShapeDtypeStruct((lp, HEADS * LANE), F32), jax.ShapeDtypeStruct((lp, D_ATTN), MXU_DTYPE),
                   jax.ShapeDtypeStruct((lp, D_CONV), MXU_DTYPE), jax.ShapeDtypeStruct((lp, D_CONV), MXU_DTYPE),
                   jax.ShapeDtypeStruct((lp, D_CONV), F32),
                   jax.ShapeDtypeStruct((1, LANE), F32), jax.ShapeDtypeStruct((1, D_MODEL), F32),
                   jax.ShapeDtypeStruct((1, D_ATTN), F32), jax.ShapeDtypeStruct((1, D_CONV), F32),
                   jax.ShapeDtypeStruct((D_MODEL, D_MODEL), F32)),
        compiler_params=_params(("arbitrary",)),
    )(o, l_sum, *([rest] * 5), rest, rest, *([x2] * n_sub), meta_blk, *([tgt2] * n_sub),
      w_out, attn_g, conv_g, final_g, conv_w8)


def _bwd_in(x2, meta_blk, norm_g, w_pad, bf_pad, fl, dc, dq, dk, dv, dza, dgb, dzc, dconv, rest, d_out, conv_w8):
    lp = fl.shape[0]
    t = ROW_TILE
    nt = lp // t
    n_sub = t // LANE
    hb = t // SUBLANE
    rev = lambda i: nt - 1 - i

    def body(*refs):
        x_refs = refs[:n_sub]
        (mb, g_ref, w_ref, bf_ref, fl_ref, dc_ref, dq_ref, dk_ref, dv_ref, dza_ref, dgb_ref, dzc_ref,
         dcv_ref, dcvn_ref, gc_ref, xc_ref, gch_ref, xch_ref, dout_ref, cw_ref) = refs[n_sub:n_sub + 20]
        dp_ref, dh_ref, gn_ref, gbf_ref, gcw_ref, carry = refs[n_sub + 20:]
        step = pl.program_id(0)
        i = rev(step)

        @pl.when(step == 0)
        def _():
            for r in (gn_ref, gbf_ref, gcw_ref, carry):
                r[...] = jnp.zeros_like(r)

        dc8 = jnp.concatenate([dc_ref[...], jnp.zeros((LANE - HEADS, t), F32)], axis=0).T
        triu = jnp.where(lax.broadcasted_iota(jnp.int32, (t, t), 1) >= lax.broadcasted_iota(jnp.int32, (t, t), 0),
                         1.0, 0.0)
        dlogf = _dot_exact(triu, dc8) + carry[...]
        carry[...] = carry[...] + jnp.sum(dc8, axis=0, keepdims=True)
        z = fl_ref[...] + bf_ref[...]
        row = i * t + lax.broadcasted_iota(jnp.int32, (t, LANE), 0)
        d_f = jnp.where(row >= PAD_ROWS, dlogf * (1.0 / (1.0 + jnp.exp(z))), 0.0)
        gbf_ref[...] = gbf_ref[...] + jnp.sum(d_f, axis=0, keepdims=True)
        dcv = dcv_ref[...]
        dcv_next = jnp.where(i == nt - 1, 0.0, dcvn_ref[...])
        d_cx = (cw_ref[2:3, :] * dcv + cw_ref[1:2, :] * _shift_up(dcv, dcv_next, 1)
                + cw_ref[0:1, :] * _shift_up(dcv, dcv_next, 2))
        gc = gc_ref[...]
        xc = xc_ref[...]
        cx = gc * xc
        cx_prev = jnp.where(i == 0, 0.0, gch_ref[...] * xch_ref[...])
        rowi = lax.broadcasted_iota(jnp.int32, (SUBLANE, 1), 0)
        gcw = (jnp.where(rowi == 0, jnp.sum(dcv * _shift_down(cx_prev, cx, 2), axis=0, keepdims=True), 0.0)
               + jnp.where(rowi == 1, jnp.sum(dcv * _shift_down(cx_prev, cx, 1), axis=0, keepdims=True), 0.0)
               + jnp.where(rowi == 2, jnp.sum(dcv * cx, axis=0, keepdims=True), 0.0))
        gcw_ref[...] = gcw_ref[...] + gcw
        dp_ref[:, SEG_Q:SEG_Q + 512] = dq_ref[...]
        dp_ref[:, SEG_K:SEG_K + 512] = dk_ref[...]
        dp_ref[:, SEG_V:SEG_V + 512] = dv_ref[...]
        dp_ref[:, SEG_F:SEG_F + LANE] = d_f.astype(dp_ref.dtype)
        dp_ref[:, SEG_ZA:SEG_ZA + 512] = dza_ref[...]
        dp_ref[:, SEG_GB:SEG_GB + 512] = dgb_ref[...]
        dp_ref[:, SEG_GC:SEG_GC + 512] = (d_cx * xc).astype(dp_ref.dtype)
        dp_ref[:, SEG_XC:SEG_XC + 512] = (d_cx * gc).astype(dp_ref.dtype)
        dp_ref[:, SEG_ZC:SEG_ZC + 512] = dzc_ref[...]
        d_u = _dot_nt(dp_ref[...], w_ref[...])
        first = jnp.where(i == 0, mb[...], x_refs[0][...])
        h = jnp.concatenate([first] + [r[...] for r in x_refs[1:]], axis=0)
        r1 = lax.rsqrt(jnp.mean(h * h, axis=-1, keepdims=True) + EPS)
        n_h = h * r1
        gn_ref[...] = gn_ref[...] + jnp.sum(d_u * n_h, axis=0, keepdims=True)
        dn = d_u * g_ref[...]
        dh_ref[...] = dout_ref[...] + r1 * (dn - n_h * jnp.mean(dn * n_h, axis=-1, keepdims=True))

    def x_specs():
        specs = [pl.BlockSpec((LANE, D_MODEL), lambda s: (jnp.maximum(n_sub * rev(s) - 1, 0), 0))]
        for b in range(1, n_sub):
            specs.append(pl.BlockSpec((LANE, D_MODEL), functools.partial(lambda s, b: (n_sub * rev(s) - 1 + b, 0), b=b)))
        return specs

    row_blk = lambda cols: pl.BlockSpec((t, cols), lambda s: (rev(s), 0))
    rest_blk = lambda k: pl.BlockSpec((t, 512), functools.partial(lambda s, k: (rev(s), k), k=k))
    halo_prev = lambda k: pl.BlockSpec(
        (SUBLANE, 512), functools.partial(lambda s, k: (jnp.maximum(rev(s) * hb - 1, 0), k), k=k))
    halo_next = pl.BlockSpec((SUBLANE, 512), lambda s: (jnp.minimum((rev(s) + 1) * hb, lp // SUBLANE - 1), 0))
    const = lambda shape: pl.BlockSpec(shape, lambda s: (0, 0))
    return pl.pallas_call(
        body, name="bwd_in", grid=(nt,),
        in_specs=x_specs() + [const((LANE, D_MODEL)), const((1, D_MODEL)),
                              pl.BlockSpec((D_MODEL, D_IN_PAD), lambda s: (0, 0), pipeline_mode=pl.Buffered(1)),
                              const((1, LANE)), row_blk(LANE),
                              pl.BlockSpec((HEADS, t), lambda s: (0, rev(s))),
                              row_blk(512), row_blk(512), row_blk(512), row_blk(512), row_blk(512), row_blk(512),
                              row_blk(512), halo_next, rest_blk(2), rest_blk(3), halo_prev(2), halo_prev(3),
                              row_blk(D_MODEL), const((SUBLANE, D_CONV))],
        out_specs=(row_blk(D_IN_PAD), row_blk(D_MODEL), const((1, D_MODEL)), const((1, LANE)),
                   const((SUBLANE, D_CONV))),
        out_shape=(jax.ShapeDtypeStruct((lp, D_IN_PAD), MXU_DTYPE), jax.ShapeDtypeStruct((lp, D_MODEL), F32),
                   jax.ShapeDtypeStruct((1, D_MODEL), F32), jax.ShapeDtypeStruct((1, LANE), F32),
                   jax.ShapeDtypeStruct((SUBLANE, D_CONV), F32)),
        scratch_shapes=[pltpu.VMEM((1, LANE), F32)],
        compiler_params=_params(("arbitrary",)),
    )(*([x2] * n_sub), meta_blk, norm_g, w_pad, bf_pad, fl, dc, dq, dk, dv, dza, dgb, dzc, dconv, dconv,
      rest, rest, rest, rest, d_out, conv_w8)


def _grad_w_in(ut, dproj):
    lp = ut.shape[1]
    tk = ROW_TILE
    tn = GW_COL_TILE

    def body(u_ref, d_ref, o_ref):
        @pl.when(pl.program_id(1) == 0)
        def _():
            o_ref[...] = jnp.zeros_like(o_ref)

        o_ref[...] = o_ref[...] + _dot(u_ref[...], d_ref[...])

    return pl.pallas_call(
        body, name="grad_w_in", grid=(D_IN_PAD // tn, lp // tk),
        in_specs=[pl.BlockSpec((D_MODEL, tk), lambda n, k: (0, k)), pl.BlockSpec((tk, tn), lambda n, k: (k, n))],
        out_specs=pl.BlockSpec((D_MODEL, tn), lambda n, k: (0, n)),
        out_shape=jax.ShapeDtypeStruct((D_MODEL, D_IN_PAD), F32),
        compiler_params=_params(("parallel", "arbitrary")),
    )(ut, dproj)


def _pad_cols(w):
    return jnp.concatenate([w[:, :F_END], jnp.zeros((w.shape[0], D_IN_PAD - D_IN), w.dtype), w[:, F_END:]], axis=1)


def _unpad_cols(g):
    return jnp.concatenate([g[:, :F_END], g[:, SEG_ZA:]], axis=1)


def _local_step(x2, tgt2, meta_full, norm_g, w_pad, b_f, conv_w_full, attn_g, conv_g, w_out_full, final_g):
    lp = x2.shape[0] + FRONT
    nt = lp // ROW_TILE
    meta_blk = jnp.concatenate([jnp.zeros((PAD_ROWS, D_MODEL), F32), meta_full], axis=0)
    bf_pad = jnp.pad(b_f, ((0, 0), (0, LANE - HEADS)))
    conv_w8 = jnp.pad(conv_w_full, ((0, SUBLANE - conv_w_full.shape[0]), (0, 0)))
    q, k, v, rest, fl, ct, ut = _in_proj(x2, meta_blk, norm_g, w_pad, bf_pad)
    ct4 = ct.reshape(SUBLANE, nt, 1, ROW_TILE)
    o, l_sum, m_max = _attn_fwd(q, k, v, ct4)
    (d_out, d_o, delta, dza, dgb, dzc, dconv, loss, g_final, g_attn, g_convg, gw_out) = _post(
        o, l_sum, rest, x2, meta_blk, tgt2, w_out_full, attn_g, conv_g, final_g, conv_w8)
    dq, dk, dv, dc = _attn_bwd(q, k, v, d_o, m_max, delta, ct4)
    dproj, d_h, g_norm, g_bf, g_cw = _bwd_in(x2, meta_blk, norm_g, w_pad, bf_pad, fl, dc.reshape(HEADS, lp), dq, dk, dv,
                                             dza, dgb, dzc, dconv, rest, d_out, conv_w8)
    gw_in = _grad_w_in(ut, dproj)
    return dict(loss=loss, d_h=d_h, g_norm=g_norm, g_final=g_final, g_attn=g_attn, g_convg=g_convg, g_bf=g_bf,
                g_cw=g_cw, gw_out=gw_out, gw_in=gw_in)


def kernel(x, meta, norm_g, w_in, b_f, conv_w, attn_norm_g, conv_norm_g, w_out, final_norm_g, loss_target, m_meta, m_norm_g, m_w_in, m_b_f, m_conv_w, m_attn_norm_g, m_conv_norm_g, m_w_out, m_final_norm_g, v_meta, v_norm_g, v_w_in, v_b_f, v_conv_w, v_attn_norm_g, v_conv_norm_g, v_w_out, v_final_norm_g):
    cx_, cy_, cc_ = _position()
    shard_rows_in = w_in.shape[1] // 2
    shard_cols_in = w_in.shape[2]
    shard_rows_out = w_out.shape[1] // 2
    wi = w_in[0].astype(MXU_DTYPE).reshape(2, shard_rows_in, shard_cols_in)
    wo = w_out[0].astype(MXU_DTYPE).reshape(2, shard_rows_out, D_MODEL)
    small = jnp.concatenate([meta, jnp.pad(conv_w[0], ((0, 8 - conv_w.shape[1]), (0, meta.shape[1] - conv_w.shape[2])))],
                            axis=0)
    gwi, gwo, gsm = _gather_weights(wi, wo, small)
    w_full = jnp.transpose(gwi.reshape(N_CHIPS, D_MODEL, shard_cols_in), (1, 0, 2)).reshape(D_MODEL, D_IN)
    w_pad = _pad_cols(w_full)
    w_out_full = gwo.reshape(D_MODEL, D_MODEL)
    meta_full = jnp.transpose(gsm[:, :N_META, :], (1, 0, 2)).reshape(N_META, D_MODEL)
    conv_w_full = jnp.transpose(gsm[:, N_META:N_META + 3, :LANE], (1, 0, 2)).reshape(3, D_CONV)
    final_g2 = final_norm_g.reshape(1, D_MODEL)
    r = _local_step(x[0], loss_target[0], meta_full, norm_g, w_pad, b_f, conv_w_full, attn_norm_g, conv_norm_g,
                    w_out_full, final_g2)
    grad_x = r["d_h"][FRONT:][None]
    ga = jnp.transpose(_unpad_cols(r["gw_in"]).reshape(D_MODEL, N_CHIPS, shard_cols_in), (1, 0, 2)).reshape(
        N_CHIPS, 2, shard_rows_in, shard_cols_in)
    gb = r["gw_out"].reshape(N_CHIPS, 2, shard_rows_out, D_MODEL)
    ra, rb = _pair_exchange(ga, gb)
    c_idx = jnp.reshape(cc_, (1,)).astype(jnp.int32)
    chip_idx = jnp.reshape(2 * cx_ + cy_, (1,)).astype(jnp.int32)
    pa = _pair_sum(ga, ra, c_idx)
    pb = _pair_sum(gb, rb, c_idx)
    xa, xb = _chip_exchange(pa, pb)
    ha = _chip_sum(pa, xa, chip_idx)
    hb = _chip_sum(pb, xb, chip_idx)
    fa, fb = _pair_share(ha, hb)
    g_w_in = fa.reshape(D_MODEL, shard_cols_in)
    g_w_out = fb.reshape(2 * shard_rows_out, D_MODEL)
    d_w_in, nm_w_in, nv_w_in = _adamw_big(w_in[0], g_w_in, m_w_in[0], v_w_in[0], LANE)
    d_w_out, nm_w_out, nv_w_out = _adamw_big(w_out[0], g_w_out, m_w_out[0], v_w_out[0], LANE)
    wide = lambda a: jnp.pad(a, ((0, 0), (0, D_MODEL - a.shape[1])))
    pack = jnp.concatenate([
        r["g_norm"], r["g_final"], jnp.concatenate([r["g_attn"], r["g_convg"]], axis=1), wide(r["g_bf"]),
        wide(r["loss"]), jnp.zeros((3, D_MODEL), F32), r["d_h"][PAD_ROWS:FRONT], wide(r["g_cw"])], axis=0)
    gpack = _gather_small(pack)
    params = (norm_g, final_g2, attn_norm_g, conv_norm_g, b_f, meta, conv_w[0])
    ms = (m_norm_g, m_final_norm_g.reshape(1, D_MODEL), m_attn_norm_g, m_conv_norm_g, m_b_f, m_meta, m_conv_w[0])
    vs = (v_norm_g, v_final_norm_g.reshape(1, D_MODEL), v_attn_norm_g, v_conv_norm_g, v_b_f, v_meta, v_conv_w[0])
    loss, g_s, d_s, m_s, v_s = _small_update(gpack, params, ms, vs)

    def ordered(small_list, big_in, big_out):
        s_norm, s_final, s_attn, s_convg, s_bf, s_meta, s_cw = small_list
        return (s_meta, s_norm, big_in[None], s_bf, s_cw[None], s_attn, s_convg, big_out[None], s_final.reshape(D_MODEL))

    return (loss.reshape(()), grad_x,
            *ordered(g_s, g_w_in, g_w_out), *ordered(d_s, d_w_in, d_w_out),
            *ordered(m_s, nm_w_in, nm_w_out), *ordered(v_s, nv_w_in, nv_w_out))
```

```python
import functools

import jax
import jax.numpy as jnp
from jax import lax
from jax.experimental import pallas as pl
from jax.experimental.pallas import tpu as pltpu

F32 = jnp.float32
MXU_DTYPE = jnp.bfloat16

D_MODEL = 1024
N_META = 16
HEADS = 8
HEAD_DIM = 64
D_ATTN = HEADS * HEAD_DIM
D_CONV = 512
EPS = 1e-6
LANE = 128
SUBLANE = 8
ROW_TILE = 384
FRONT = LANE
PAD_ROWS = FRONT - N_META
NEG = -1e30
N_CHIPS = 4
N_DEV = 8
VMEM_LIMIT_BYTES = 60 * 1024 * 1024

SEG_Q, SEG_K, SEG_V, SEG_F, SEG_ZA, SEG_GB, SEG_GC, SEG_XC, SEG_ZC = (
    0, 512, 1024, 1536, 1664, 2176, 2688, 3200, 3712)
D_IN = 4104
D_IN_PAD = 4224
F_END = 1544
GW_COL_TILE = 1408
W_IN_HALF = 528
ADAM_ROWS = 1032

ADAM_LR = 0.001
ADAM_B1 = 0.9
ADAM_B2 = 0.999
ADAM_EPS = 1e-08
ADAM_WD = 0.01
ADAM_STEP = 10

MESH = pl.DeviceIdType.MESH
ANY = pl.BlockSpec(memory_space=pl.ANY)

PACK_ROWS = 32
SLOT_NORM = (0, 1, 0, 1024)
SLOT_FINAL = (1, 2, 0, 1024)
SLOT_ATTN = (2, 3, 0, 512)
SLOT_CONVG = (2, 3, 512, 1024)
SLOT_BF = (3, 4, 0, 8)
SLOT_META = (8, 24, 0, 256)
SLOT_CONVW = (24, 27, 0, 128)
LOSS_ROW = 4


def _params(sem=None):
    return pltpu.CompilerParams(dimension_semantics=sem, vmem_limit_bytes=VMEM_LIMIT_BYTES)


def _sigmoid(z):
    return 1.0 / (1.0 + jnp.exp(-z))


def _dot(a, b):
    return jnp.dot(a, b, preferred_element_type=F32)


def _dot_nt(a, b):
    return lax.dot_general(a, b, (((1,), (1,)), ((), ())), preferred_element_type=F32)


def _dot_exact(a, b):
    return jnp.dot(a, b, preferred_element_type=F32, precision=lax.Precision.HIGHEST)


def _group_matrix():
    r = lax.broadcasted_iota(jnp.int32, (D_ATTN, D_ATTN), 0) >> 6
    c = lax.broadcasted_iota(jnp.int32, (D_ATTN, D_ATTN), 1) >> 6
    return jnp.where(r == c, 1.0, 0.0).astype(MXU_DTYPE)


def _group_sum(x, gmat):
    hi = x.astype(MXU_DTYPE)
    lo = (x - hi.astype(F32)).astype(MXU_DTYPE)
    return _dot(hi, gmat) + _dot(lo, gmat)


def _x_block_specs(n_sub, rows):
    specs = [pl.BlockSpec((rows, D_MODEL), lambda i: (jnp.maximum(n_sub * i - 1, 0), 0))]
    for b in range(1, n_sub):
        specs.append(pl.BlockSpec((rows, D_MODEL), functools.partial(lambda i, b: (n_sub * i - 1 + b, 0), b=b)))
    return specs


def _position():
    return lax.axis_index("x"), lax.axis_index("y"), lax.axis_index("c")


def _gather_weights(wi, wo, small):
    def body(wi_ref, wo_ref, sm_ref, gwi_ref, gwo_ref, gsm_ref, send_sems, recv_sems):
        x, y, c = _position()
        sibling = (x, y, 1 - c)
        chips = [(1 - x, y), (x, 1 - y), (1 - x, 1 - y)]

        def remote(k, src, dst, to):
            return pltpu.make_async_remote_copy(src_ref=src, dst_ref=dst, send_sem=send_sems.at[k],
                                                recv_sem=recv_sems.at[k], device_id=to, device_id_type=MESH)

        first, passed, landed = [], [], []
        for a, (src_ref, g_ref) in enumerate(((wi_ref, gwi_ref), (wo_ref, gwo_ref))):
            for j, (cx, cy) in enumerate(chips):
                slot = g_ref.at[j, c]
                first.append(remote(6 * a + j, src_ref.at[c], slot, (cx, cy, c)))
                landed.append(remote(6 * a + j, slot, slot, sibling))
                passed.append(remote(6 * a + 3 + j, slot, slot, sibling))
        for j, (cx, cy) in enumerate(chips):
            first.append(remote(12 + j, sm_ref, gsm_ref.at[j], (cx, cy, c)))
        for cp in first:
            cp.start()
        for arrived, onward in zip(landed, passed):
            arrived.wait_recv()
            onward.start()
        for a, g_ref in enumerate((gwi_ref, gwo_ref)):
            for j in range(3):
                remote(6 * a + 3 + j, g_ref.at[j, 1 - c], g_ref.at[j, 1 - c], sibling).wait_recv()
        for j in range(3):
            remote(12 + j, sm_ref, gsm_ref.at[j], sibling).wait_recv()
        for cp in first + passed:
            cp.wait_send()

    return pl.pallas_call(
        body, name="gather_weights",
        out_shape=(jax.ShapeDtypeStruct((3,) + wi.shape, wi.dtype), jax.ShapeDtypeStruct((3,) + wo.shape, wo.dtype),
                   jax.ShapeDtypeStruct((3,) + small.shape, small.dtype)),
        in_specs=[ANY, ANY, ANY], out_specs=(ANY, ANY, ANY),
        scratch_shapes=[pltpu.SemaphoreType.DMA((15,)), pltpu.SemaphoreType.DMA((15,))],
    )(wi, wo, small)


def _pair_exchange(ga, gb):
    def body(ga_ref, gb_ref, ra_ref, rb_ref, send_sems, recv_sems):
        x, y, c = _position()
        sibling = (x, y, 1 - c)
        copies = []
        for k, (src, dst) in enumerate(((ga_ref, ra_ref), (gb_ref, rb_ref))):
            copies.append(pltpu.make_async_remote_copy(
                src_ref=src.at[:, 1 - c], dst_ref=dst, send_sem=send_sems.at[k], recv_sem=recv_sems.at[k],
                device_id=sibling, device_id_type=MESH))
        for cp in copies:
            cp.start()
        for cp in copies:
            cp.wait()

    return pl.pallas_call(
        body, name="grad_pair_exchange",
        out_shape=(jax.ShapeDtypeStruct((N_CHIPS,) + ga.shape[2:], ga.dtype),
                   jax.ShapeDtypeStruct((N_CHIPS,) + gb.shape[2:], gb.dtype)),
        in_specs=[ANY, ANY], out_specs=(ANY, ANY),
        scratch_shapes=[pltpu.SemaphoreType.DMA((2,)), pltpu.SemaphoreType.DMA((2,))],
    )(ga, gb)


def _chip_exchange(pa, pb):
    def body(pa_ref, pb_ref, ra_ref, rb_ref, send_sems, recv_sems):
        x, y, c = _position()
        chips = [(1 - x, y), (x, 1 - y), (1 - x, 1 - y)]
        copies = []
        for a, (src, dst) in enumerate(((pa_ref, ra_ref), (pb_ref, rb_ref))):
            for j, (cx, cy) in enumerate(chips):
                copies.append(pltpu.make_async_remote_copy(
                    src_ref=src.at[2 * cx + cy], dst_ref=dst.at[j], send_sem=send_sems.at[3 * a + j],
                    recv_sem=recv_sems.at[3 * a + j], device_id=(cx, cy, c), device_id_type=MESH))
        for cp in copies:
            cp.start()
        for cp in copies:
            cp.wait()

    return pl.pallas_call(
        body, name="grad_chip_exchange",
        out_shape=(jax.ShapeDtypeStruct((3,) + pa.shape[1:], pa.dtype),
                   jax.ShapeDtypeStruct((3,) + pb.shape[1:], pb.dtype)),
        in_specs=[ANY, ANY], out_specs=(ANY, ANY),
        scratch_shapes=[pltpu.SemaphoreType.DMA((6,)), pltpu.SemaphoreType.DMA((6,))],
    )(pa, pb)


def _pair_share(ha, hb):
    def body(ha_ref, hb_ref, oa_ref, ob_ref, send_sems, recv_sems):
        x, y, c = _position()
        copies = [pltpu.make_async_remote_copy(
            src_ref=src, dst_ref=dst, send_sem=send_sems.at[k], recv_sem=recv_sems.at[k],
            device_id=(x, y, 1 - c), device_id_type=MESH)
            for k, (src, dst) in enumerate(((ha_ref, oa_ref), (hb_ref, ob_ref)))]
        for cp in copies:
            cp.start()
        for cp in copies:
            cp.wait()

    return pl.pallas_call(
        body, name="grad_pair_share",
        out_shape=(jax.ShapeDtypeStruct(ha.shape, ha.dtype), jax.ShapeDtypeStruct(hb.shape, hb.dtype)),
        in_specs=[ANY, ANY], out_specs=(ANY, ANY),
        scratch_shapes=[pltpu.SemaphoreType.DMA((2,)), pltpu.SemaphoreType.DMA((2,))],
    )(ha, hb)


def _gather_small(pack):
    def body(p_ref, o_ref, send_sems, recv_sems):
        x, y, c = _position()
        copies = []
        for mask in range(1, N_DEV):
            peer = (1 - x if mask & 4 else x, 1 - y if mask & 2 else y, 1 - c if mask & 1 else c)
            copies.append(pltpu.make_async_remote_copy(
                src_ref=p_ref, dst_ref=o_ref.at[mask - 1], send_sem=send_sems.at[mask - 1],
                recv_sem=recv_sems.at[mask - 1], device_id=peer, device_id_type=MESH))
        for cp in copies:
            cp.start()
        for cp in copies:
            cp.wait()

    return pl.pallas_call(
        body, name="gather_small",
        out_shape=jax.ShapeDtypeStruct((N_DEV - 1,) + pack.shape, pack.dtype),
        in_specs=[ANY], out_specs=ANY,
        scratch_shapes=[pltpu.SemaphoreType.DMA((N_DEV - 1,)), pltpu.SemaphoreType.DMA((N_DEV - 1,))],
    )(pack)


def _pair_sum(mine, recv, c_idx):
    rows, cols = mine.shape[2:]

    def body(c_ref, a_ref, b_ref, o_ref):
        o_ref[...] = a_ref[...] + b_ref[...]

    return pl.pallas_call(
        body, name="grad_pair_sum",
        grid_spec=pltpu.PrefetchScalarGridSpec(
            num_scalar_prefetch=1, grid=(N_CHIPS,),
            in_specs=[pl.BlockSpec((None, None, rows, cols), lambda s, c_ref: (s, c_ref[0], 0, 0)),
                      pl.BlockSpec((None, rows, cols), lambda s, c_ref: (s, 0, 0))],
            out_specs=pl.BlockSpec((None, rows, cols), lambda s, c_ref: (s, 0, 0))),
        out_shape=jax.ShapeDtypeStruct(recv.shape, recv.dtype),
        compiler_params=_params(("parallel",)),
    )(c_idx, mine, recv)


def _chip_sum(psum, recv3, chip_idx):
    rows, cols = psum.shape[1:]
    tr = rows // 2

    def body(s_ref, p_ref, r0, r1, r2, o_ref):
        o_ref[...] = ((p_ref[...] + r0[...]) + r1[...]) + r2[...]

    return pl.pallas_call(
        body, name="grad_chip_sum",
        grid_spec=pltpu.PrefetchScalarGridSpec(
            num_scalar_prefetch=1, grid=(2,),
            in_specs=[pl.BlockSpec((None, tr, cols), lambda i, s_ref: (s_ref[0], i, 0))] +
                     [pl.BlockSpec((None, tr, cols), functools.partial(lambda i, s_ref, j: (j, i, 0), j=j))
                      for j in range(3)],
            out_specs=pl.BlockSpec((tr, cols), lambda i, s_ref: (i, 0))),
        out_shape=jax.ShapeDtypeStruct((rows, cols), psum.dtype),
        compiler_params=_params(("parallel",)),
    )(chip_idx, psum, recv3, recv3, recv3)


def _adamw_math(w, g, m, v):
    m = ADAM_B1 * m + (1.0 - ADAM_B1) * g
    v = ADAM_B2 * v + (1.0 - ADAM_B2) * (g * g)
    m_hat = m / (1.0 - ADAM_B1 ** ADAM_STEP)
    v_hat = v / (1.0 - ADAM_B2 ** ADAM_STEP)
    delta = -ADAM_LR * (m_hat / (jnp.sqrt(v_hat) + ADAM_EPS) + ADAM_WD * w)
    return delta, m, v


def _adamw_big(w, g, m, v, tr):
    rows, cols = w.shape
    assert rows % tr == 0 and g.shape[0] >= rows

    def body(w_ref, g_ref, m_ref, v_ref, d_out, m_out, v_out):
        d, m2, v2 = _adamw_math(w_ref[...], g_ref[...], m_ref[...], v_ref[...])
        d_out[...] = d
        m_out[...] = m2
        v_out[...] = v2

    spec = pl.BlockSpec((tr, cols), lambda i: (i, 0))
    sds = jax.ShapeDtypeStruct((rows, cols), F32)
    return pl.pallas_call(
        body, name="adamw_big", grid=(rows // tr,), in_specs=[spec] * 4, out_specs=(spec,) * 3,
        out_shape=(sds,) * 3, compiler_params=_params(("parallel",)),
    )(w, g, m, v)


def _small_update(own, others, params, ms, vs):
    slots = (SLOT_NORM, SLOT_FINAL, SLOT_ATTN, SLOT_CONVG, SLOT_BF, SLOT_META, SLOT_CONVW)
    n = len(slots)

    def body(*refs):
        own_ref, gp_ref = refs[:2]
        refs = refs[1:]
        w_refs, m_refs, v_refs = refs[1:1 + n], refs[1 + n:1 + 2 * n], refs[1 + 2 * n:1 + 3 * n]
        outs = refs[1 + 3 * n:2 + 7 * n]
        loss_ref = outs[0]
        g_outs, d_outs, m_outs, v_outs = (outs[1 + k * n:1 + (k + 1) * n] for k in range(4))
        g_scr, w_scr, m_scr, v_scr = refs[2 + 7 * n:]
        x, y, c = _position()
        shard = 2 * x + y
        me = 4 * x + 2 * y + c
        tot = None
        for d in range(N_DEV):
            rel = jnp.bitwise_xor(me, d)
            term = jnp.where(rel == 0, own_ref[...], gp_ref[jnp.maximum(rel, 1) - 1])
            tot = term if tot is None else tot + term
        r0, r1, _, _ = SLOT_META
        meta_sel = tot[r0:r1, 0:256]
        cw_sel = tot[24:32, 0:128]
        for k in range(1, N_CHIPS):
            meta_sel = jnp.where(shard == k, tot[r0:r1, 256 * k:256 * (k + 1)], meta_sel)
            cw_sel = jnp.where(shard == k, tot[24:32, 128 * k:128 * (k + 1)], cw_sel)
        zeros = jnp.zeros((PACK_ROWS, D_MODEL), F32)
        for scr in (g_scr, w_scr, m_scr, v_scr):
            scr[...] = zeros
        g_scr[0:8, :] = tot[0:8, :]
        g_scr[r0:r1, 0:256] = meta_sel
        g_scr[24:32, 0:128] = cw_sel
        for (a, b, c0, c1), w_ref, m_ref, v_ref in zip(slots, w_refs, m_refs, v_refs):
            w_scr[a:b, c0:c1] = w_ref[...]
            m_scr[a:b, c0:c1] = m_ref[...]
            v_scr[a:b, c0:c1] = v_ref[...]
        loss_ref[...] = g_scr[LOSS_ROW:LOSS_ROW + 1, 0:1]
        d, m2, v2 = _adamw_math(w_scr[...], g_scr[...], m_scr[...], v_scr[...])
        w_scr[...] = d
        m_scr[...] = m2
        v_scr[...] = v2
        for (a, b, c0, c1), g_o, d_o, m_o, v_o in zip(slots, g_outs, d_outs, m_outs, v_outs):
            g_o[...] = g_scr[a:b, c0:c1]
            d_o[...] = w_scr[a:b, c0:c1]
            m_o[...] = m_scr[a:b, c0:c1]
            v_o[...] = v_scr[a:b, c0:c1]

    shapes = [jax.ShapeDtypeStruct(p.shape, F32) for p in params]
    out = pl.pallas_call(
        body, name="small_update",
        out_shape=[jax.ShapeDtypeStruct((1, 1), F32)] + shapes * 4,
        scratch_shapes=[pltpu.VMEM((PACK_ROWS, D_MODEL), F32)] * 4,
        compiler_params=_params(),
    )(own, others, *params, *ms, *vs)
    return out[0], out[1:1 + n], out[1 + n:1 + 2 * n], out[1 + 2 * n:1 + 3 * n], out[1 + 3 * n:1 + 4 * n]


def _in_proj(x2, meta_blk, norm_g, w_pad, bf_pad):
    seq = x2.shape[0]
    lp = seq + FRONT
    t = ROW_TILE
    nt = lp // t
    n_sub = t // LANE

    def body(*refs):
        x_refs = refs[:n_sub]
        mb, g_ref, w_ref, bf_ref = refs[n_sub:n_sub + 4]
        q_ref, k_ref, v_ref, rest_ref, fl_ref, ct_ref, u_ref, carry = refs[n_sub + 4:]
        i = pl.program_id(0)

        @pl.when(i == 0)
        def _():
            carry[...] = jnp.zeros_like(carry)

        first = jnp.where(i == 0, mb[...], x_refs[0][...])
        h = jnp.concatenate([first] + [r[...] for r in x_refs[1:]], axis=0)
        ms = jnp.mean(h * h, axis=-1, keepdims=True)
        u = ((h * lax.rsqrt(ms + EPS)) * g_ref[...]).astype(MXU_DTYPE)
        u_ref[...] = u

        def seg(a, width):
            return _dot_nt(u, w_ref[a:a + width, :])

        q_ref[...] = (seg(SEG_Q, D_ATTN) * (HEAD_DIM ** -0.5)).astype(MXU_DTYPE)
        k_ref[...] = seg(SEG_K, D_ATTN).astype(MXU_DTYPE)
        v_ref[...] = seg(SEG_V, D_ATTN).astype(MXU_DTYPE)
        for s in range(5):
            rest_ref[:, 512 * s:512 * (s + 1)] = seg(SEG_ZA + 512 * s, 512)
        fl = seg(SEG_F, LANE)
        fl_ref[...] = fl
        z = fl + bf_ref[...]
        logf = jnp.minimum(z, 0.0) - jnp.log(1.0 + jnp.exp(-jnp.abs(z)))
        row = i * t + lax.broadcasted_iota(jnp.int32, (t, LANE), 0)
        logf = jnp.where(row >= PAD_ROWS, logf, 0.0)
        tri = jnp.where(lax.broadcasted_iota(jnp.int32, (t, t), 0) >= lax.broadcasted_iota(jnp.int32, (t, t), 1),
                        1.0, 0.0)
        cs = _dot_exact(tri, logf) + carry[...]
        carry[...] = carry[...] + jnp.sum(logf, axis=0, keepdims=True)
        col = i * t + lax.broadcasted_iota(jnp.int32, (SUBLANE, t), 1)
        ct_ref[...] = jnp.where(col >= PAD_ROWS, cs.T[0:SUBLANE, :], -NEG)

    row_blk = lambda cols: pl.BlockSpec((t, cols), lambda i: (i, 0))
    const = lambda shape: pl.BlockSpec(shape, lambda i: (0, 0))
    return pl.pallas_call(
        body, name="in_proj", grid=(nt,),
        in_specs=_x_block_specs(n_sub, LANE) + [const((LANE, D_MODEL)), const((1, D_MODEL)),
                                                pl.BlockSpec((D_IN_PAD, D_MODEL), lambda i: (0, 0),
                                                             pipeline_mode=pl.Buffered(1)),
                                                const((1, LANE))],
        out_specs=(row_blk(D_ATTN), row_blk(D_ATTN), row_blk(D_ATTN), row_blk(5 * 512), row_blk(LANE),
                   pl.BlockSpec((SUBLANE, t), lambda i: (0, i)), row_blk(D_MODEL)),
        out_shape=(jax.ShapeDtypeStruct((lp, D_ATTN), MXU_DTYPE), jax.ShapeDtypeStruct((lp, D_ATTN), MXU_DTYPE),
                   jax.ShapeDtypeStruct((lp, D_ATTN), MXU_DTYPE), jax.ShapeDtypeStruct((lp, 5 * 512), F32),
                   jax.ShapeDtypeStruct((lp, LANE), F32),
                   jax.ShapeDtypeStruct((SUBLANE, lp), F32), jax.ShapeDtypeStruct((lp, D_MODEL), MXU_DTYPE)),
        scratch_shapes=[pltpu.VMEM((1, LANE), F32)],
        compiler_params=_params(("arbitrary",)),
    )(*([x2] * n_sub), meta_blk, norm_g, w_pad, bf_pad)


def _head_masks():
    lane = lax.broadcasted_iota(jnp.int32, (1, LANE), 1)
    return lane < HEAD_DIM, lane >= HEAD_DIM


def _pair_specs(lp, nt, t):
    blk = pl.BlockSpec((lp, LANE), lambda g: (0, g))
    ct_a = pl.BlockSpec((None, nt, 1, t), lambda g: (2 * g, 0, 0, 0))
    ct_b = pl.BlockSpec((None, nt, 1, t), lambda g: (2 * g + 1, 0, 0, 0))
    return blk, ct_a, ct_b


def _sub_rows(s, col):
    return jnp.concatenate([s[:, a * LANE:(a + 1) * LANE] - col for a in range(s.shape[1] // LANE)], axis=1)


def _loop_by_two(lo, hi, step, init):
    def pair(jj, carry):
        return step(lo + 2 * jj + 1, step(lo + 2 * jj, carry))

    pairs = (hi - lo) // 2
    carry = lax.fori_loop(0, pairs, pair, init)
    return lax.fori_loop(lo + 2 * pairs, hi, step, carry)


def _lane_chunks(s):
    return [s[:, a * LANE:(a + 1) * LANE] for a in range(s.shape[1] // LANE)]


def _attn_fwd(q, k, v, ct4):
    lp = q.shape[0]
    t = ROW_TILE
    nt = lp // t

    def body(q_ref, k_ref, v_ref, cta_ref, ctb_ref, o_ref, l_ref, m_ref):
        masks = _head_masks()
        ct_refs = (cta_ref, ctb_ref)
        below = lax.broadcasted_iota(jnp.int32, (t, t), 1) <= lax.broadcasted_iota(jnp.int32, (t, t), 0)
        lane = lax.broadcasted_iota(jnp.int32, (1, LANE), 1)
        head_of_row = lax.broadcasted_iota(jnp.int32, (2 * t, LANE), 0) >= t
        ones_cols = jnp.where(lax.broadcasted_iota(jnp.int32, (2 * t, LANE), 1) == head_of_row.astype(jnp.int32),
                              1.0, 0.0).astype(MXU_DTYPE)

        def q_block(i, _):
            r0 = pl.multiple_of(i * t, t)
            qi = q_ref[pl.ds(r0, t), :]

            def scores(j):
                kj = k_ref[pl.ds(pl.multiple_of(j * t, t), t), :]
                return _dot_nt(qi, jnp.concatenate([jnp.where(hm, kj, 0).astype(MXU_DTYPE) for hm in masks], axis=0))

            def biased(j, hh, s2, diagonal):
                s = s2[:, hh * t:(hh + 1) * t] - ct_refs[hh][j]
                return jnp.where(below, s, NEG) if diagonal else s

            def max_step(j, carry, diagonal):
                s2 = scores(j)
                out = []
                for hh, m in enumerate(carry):
                    for c in _lane_chunks(biased(j, hh, s2, diagonal)):
                        m = jnp.maximum(m, c)
                    out.append(m)
                return tuple(out)

            lanes_neg = jnp.full((t, LANE), NEG, F32)
            carry = _loop_by_two(0, i, functools.partial(max_step, diagonal=False), (lanes_neg, lanes_neg))
            ms = [jnp.max(m, axis=-1, keepdims=True) for m in max_step(i, carry, True)]

            def sum_step(j, acc, diagonal):
                s2 = scores(j)
                vj = v_ref[pl.ds(pl.multiple_of(j * t, t), t), :]
                v2 = jnp.concatenate([jnp.where(hm, vj, 0).astype(MXU_DTYPE) for hm in masks], axis=0)
                parts = [jnp.exp(biased(j, hh, s2, diagonal) - ms[hh]).astype(MXU_DTYPE) for hh in range(2)]
                return acc + _dot(jnp.concatenate(parts, axis=1), jnp.concatenate([v2, ones_cols], axis=1))

            acc = _loop_by_two(0, i, functools.partial(sum_step, diagonal=False), jnp.zeros((t, 2 * LANE), F32))
            acc = sum_step(i, acc, True)
            sums = acc[:, LANE:]
            l_pair = jnp.where(masks[0], jnp.sum(jnp.where(lane == 0, sums, 0.0), axis=-1, keepdims=True),
                               jnp.sum(jnp.where(lane == 1, sums, 0.0), axis=-1, keepdims=True))
            o_ref[pl.ds(r0, t), :] = acc[:, :LANE] / l_pair
            l_ref[pl.ds(r0, t), :] = l_pair
            m_ref[pl.ds(r0, t), 0:LANE] = jnp.broadcast_to(ms[0], (t, LANE))
            m_ref[pl.ds(r0, t), LANE:2 * LANE] = jnp.broadcast_to(ms[1], (t, LANE))
            return 0

        lax.fori_loop(0, nt, q_block, 0)

    blk, ct_a, ct_b = _pair_specs(lp, nt, t)
    return pl.pallas_call(
        body, name="attn_fwd", grid=(HEADS // 2,),
        in_specs=[blk, blk, blk, ct_a, ct_b], out_specs=(blk, blk, pl.BlockSpec((lp, 2 * LANE), lambda g: (0, g))),
        out_shape=(jax.ShapeDtypeStruct((lp, D_ATTN), F32), jax.ShapeDtypeStruct((lp, D_ATTN), F32),
                   jax.ShapeDtypeStruct((lp, HEADS * LANE), F32)),
        compiler_params=_params(("parallel",)),
    )(q, k, v, ct4, ct4)


def _attn_bwd(q, k, v, do, m, delta, ct4):
    lp = q.shape[0]
    t = ROW_TILE
    nt = lp // t

    def body(q_ref, k_ref, v_ref, do_ref, ma_ref, mb_ref, dla_ref, dlb_ref, cta_ref, ctb_ref,
             dq_ref, dk_ref, dv_ref, dc_ref, dq_acc, dk_acc, dv_acc):
        masks = _head_masks()
        ct_refs, m_refs, dl_refs = (cta_ref, ctb_ref), (ma_ref, mb_ref), (dla_ref, dlb_ref)
        below = lax.broadcasted_iota(jnp.int32, (t, t), 1) <= lax.broadcasted_iota(jnp.int32, (t, t), 0)
        tn = (((0,), (0,)), ((), ()))
        dq_acc[...] = jnp.zeros_like(dq_acc)

        def k_block(j, _):
            c0 = pl.multiple_of(j * t, t)
            kj = k_ref[pl.ds(c0, t), :]
            vj = v_ref[pl.ds(c0, t), :]
            k2 = jnp.concatenate([jnp.where(hm, kj, 0).astype(MXU_DTYPE) for hm in masks], axis=0)
            v2 = jnp.concatenate([jnp.where(hm, vj, 0).astype(MXU_DTYPE) for hm in masks], axis=0)
            ck = [r[j] for r in ct_refs]
            dk_acc[...] = jnp.zeros_like(dk_acc)
            dv_acc[...] = jnp.zeros_like(dv_acc)

            def q_block(i, colsums, diagonal):
                r0 = pl.multiple_of(i * t, t)
                qi = q_ref[pl.ds(r0, t), :]
                doi = do_ref[pl.ds(r0, t), :]
                q2 = jnp.concatenate([jnp.where(hm, qi, 0).astype(MXU_DTYPE) for hm in masks], axis=0)
                do2 = jnp.concatenate([jnp.where(hm, doi, 0).astype(MXU_DTYPE) for hm in masks], axis=0)
                s2 = _dot_nt(qi, k2)
                dp2 = _dot_nt(doi, v2)
                out, ps, dss = [], [], []
                for hh in range(2):
                    s = s2[:, hh * t:(hh + 1) * t] - ck[hh]
                    if diagonal:
                        s = jnp.where(below, s, NEG)
                    p = jnp.exp(_sub_rows(s, m_refs[hh][pl.ds(r0, t), :])).astype(MXU_DTYPE)
                    ds32 = p.astype(F32) * _sub_rows(dp2[:, hh * t:(hh + 1) * t], dl_refs[hh][pl.ds(r0, t), :])
                    ps.append(p)
                    dss.append(ds32.astype(MXU_DTYPE))
                    out.append(colsums[hh] + jnp.sum(ds32, axis=0, keepdims=True))
                dv_acc[...] = dv_acc[...] + lax.dot_general(jnp.concatenate(ps, axis=0), do2, tn,
                                                            preferred_element_type=F32)
                dk_acc[...] = dk_acc[...] + lax.dot_general(jnp.concatenate(dss, axis=0), q2, tn,
                                                            preferred_element_type=F32)
                dq_acc[pl.ds(r0, t), :] = dq_acc[pl.ds(r0, t), :] + _dot(jnp.concatenate(dss, axis=1), k2)
                return tuple(out)

            colsums = q_block(j, (jnp.zeros((1, t), F32), jnp.zeros((1, t), F32)), True)
            colsums = lax.fori_loop(j + 1, nt, functools.partial(q_block, diagonal=False), colsums)
            for hh in range(2):
                dc_ref[hh, j] = -colsums[hh]
            dk_ref[pl.ds(c0, t), :] = dk_acc[...].astype(dk_ref.dtype)
            dv_ref[pl.ds(c0, t), :] = dv_acc[...].astype(dv_ref.dtype)
            return 0

        lax.fori_loop(0, nt, k_block, 0)
        dq_ref[...] = (dq_acc[...] * (HEAD_DIM ** -0.5)).astype(dq_ref.dtype)

    blk, ct_a, ct_b = _pair_specs(lp, nt, t)
    rep_a = pl.BlockSpec((lp, LANE), lambda g: (0, 2 * g))
    rep_b = pl.BlockSpec((lp, LANE), lambda g: (0, 2 * g + 1))
    return pl.pallas_call(
        body, name="attn_bwd", grid=(HEADS // 2,),
        in_specs=[blk] * 4 + [rep_a, rep_b, rep_a, rep_b, ct_a, ct_b],
        out_specs=(blk, blk, blk, pl.BlockSpec((2, nt, 1, t), lambda g: (g, 0, 0, 0))),
        out_shape=(jax.ShapeDtypeStruct((lp, D_ATTN), MXU_DTYPE),) * 3
                  + (jax.ShapeDtypeStruct((HEADS, nt, 1, t), F32),),
        scratch_shapes=[pltpu.VMEM((lp, LANE), F32), pltpu.VMEM((t, LANE), F32), pltpu.VMEM((t, LANE), F32)],
        compiler_params=_params(("parallel",)),
    )(q, k, v, do, m, m, delta, delta, ct4, ct4)


def _shift_down(prev8, cur, k):
    ext = jnp.concatenate([prev8, cur], axis=0)
    return pltpu.roll(ext, k, 0)[SUBLANE:, :]


def _shift_up(cur, next8, k):
    ext = jnp.concatenate([cur, next8], axis=0)
    n = ext.shape[0]
    return pltpu.roll(ext, n - k, 0)[:cur.shape[0], :]


def _post(o, l_sum, rest, x2, meta_blk, tgt2, w_out, attn_g, conv_g, final_g, conv_w8):
    lp = o.shape[0]
    t = ROW_TILE
    nt = lp // t
    n_sub = t // LANE
    hb = t // SUBLANE

    def body(*refs):
        o_ref, l_ref, za_ref, gb_ref, gc_ref, xc_ref, zc_ref, gch_ref, xch_ref = refs[:9]
        refs = refs[1:]
        x_refs = refs[8:8 + n_sub]
        mb = refs[8 + n_sub]
        t_refs = refs[9 + n_sub:9 + 2 * n_sub]
        wo_ref, ag_ref, cg_ref, fg_ref, cw_ref = refs[9 + 2 * n_sub:14 + 2 * n_sub]
        (dout_ref, do_ref, dl_ref, dza_ref, dgb_ref, dzc_ref, dcv_ref,
         loss_ref, gf_ref, gag_ref, gcg_ref, gwo_ref) = refs[14 + 2 * n_sub:]
        i = pl.program_id(0)

        @pl.when(i == 0)
        def _():
            for r in (loss_ref, gf_ref, gag_ref, gcg_ref, gwo_ref):
                r[...] = jnp.zeros_like(r)

        gmat = _group_matrix()
        inv_g = 1.0 / HEAD_DIM
        o_v = o_ref[...]
        ra = lax.rsqrt(_group_sum(o_v * o_v, gmat) * inv_g + EPS)
        n_a = o_v * ra
        a_n = n_a * ag_ref[...]
        za = za_ref[...]
        sig_a = _sigmoid(za)
        sz_a = za * sig_a
        y_a = a_n * sz_a
        gb = gb_ref[...]
        gc = gc_ref[...]
        xc = xc_ref[...]
        cx = gc * xc
        cx_prev = jnp.where(i == 0, 0.0, gch_ref[...] * xch_ref[...])
        conv = (cw_ref[0:1, :] * _shift_down(cx_prev, cx, 2) + cw_ref[1:2, :] * _shift_down(cx_prev, cx, 1)
                + cw_ref[2:3, :] * cx)
        e = gb * conv
        re = lax.rsqrt(_group_sum(e * e, gmat) * inv_g + EPS)
        n_e = e * re
        e_n = n_e * cg_ref[...]
        zc = zc_ref[...]
        sig_c = _sigmoid(zc)
        sz_c = zc * sig_c
        y_c = e_n * sz_c
        mix = jnp.concatenate([y_a, y_c], axis=-1)
        mix_b = mix.astype(MXU_DTYPE)
        first = jnp.where(i == 0, mb[...], x_refs[0][...])
        h = jnp.concatenate([first] + [r[...] for r in x_refs[1:]], axis=0)
        out = h + _dot(mix_b, wo_ref[...])
        r2 = lax.rsqrt(jnp.mean(out * out, axis=-1, keepdims=True) + EPS)
        n_f = out * r2
        y = n_f * fg_ref[...]
        tgt = jnp.concatenate([r[...] for r in t_refs], axis=0)
        valid = (i * t + lax.broadcasted_iota(jnp.int32, (t, 1), 0)) >= FRONT
        diff = jnp.where(valid, y - tgt, 0.0)
        loss_ref[...] = loss_ref[...] + 0.5 * jnp.sum(jnp.sum(diff * diff, axis=-1, keepdims=True) * (1.0 / D_MODEL))
        dy = diff * (1.0 / D_MODEL)
        gf_ref[...] = gf_ref[...] + jnp.sum(dy * n_f, axis=0, keepdims=True)
        dn = dy * fg_ref[...]
        d_out = r2 * (dn - n_f * jnp.mean(dn * n_f, axis=-1, keepdims=True))
        dout_ref[...] = d_out
        d_out_b = d_out.astype(MXU_DTYPE)
        d_mix = _dot_nt(d_out_b, wo_ref[...])
        gwo_ref[...] = gwo_ref[...] + _dot(mix.T.astype(MXU_DTYPE), d_out_b)
        d_ya = d_mix[:, :D_ATTN]
        d_yc = d_mix[:, D_ATTN:]
        d_an = d_ya * sz_a
        dza_ref[...] = (d_ya * a_n * (sig_a * (1.0 + za * (1.0 - sig_a)))).astype(dza_ref.dtype)
        gag_ref[...] = gag_ref[...] + jnp.sum(d_an * n_a, axis=0, keepdims=True)
        dn_a = d_an * ag_ref[...]
        d_o = ra * (dn_a - n_a * (_group_sum(dn_a * n_a, gmat) * inv_g))
        d_o_b = (d_o / l_ref[...]).astype(do_ref.dtype)
        do_ref[...] = d_o_b
        head_rep = jnp.where((lax.broadcasted_iota(jnp.int32, (D_ATTN, HEADS * LANE), 0) >> 6)
                             == (lax.broadcasted_iota(jnp.int32, (D_ATTN, HEADS * LANE), 1) >> 7), 1.0, 0.0)
        dl_ref[...] = _group_sum(d_o_b.astype(F32) * o_v, head_rep.astype(MXU_DTYPE))
        d_en = d_yc * sz_c
        dzc_ref[...] = (d_yc * e_n * (sig_c * (1.0 + zc * (1.0 - sig_c)))).astype(dzc_ref.dtype)
        gcg_ref[...] = gcg_ref[...] + jnp.sum(d_en * n_e, axis=0, keepdims=True)
        dn_e = d_en * cg_ref[...]
        d_e = re * (dn_e - n_e * (_group_sum(dn_e * n_e, gmat) * inv_g))
        dgb_ref[...] = (d_e * conv).astype(dgb_ref.dtype)
        dcv_ref[...] = d_e * gb

    row_blk = lambda cols: pl.BlockSpec((t, cols), lambda i: (i, 0))
    rest_blk = lambda s: pl.BlockSpec((t, 512), functools.partial(lambda i, s: (i, s), s=s))
    halo = lambda s: pl.BlockSpec((SUBLANE, 512), functools.partial(lambda i, s: (jnp.maximum(i * hb - 1, 0), s), s=s))
    const = lambda shape: pl.BlockSpec(shape, lambda i: (0, 0))
    acc = lambda shape: pl.BlockSpec(shape, lambda i: (0, 0))
    return pl.pallas_call(
        body, name="post_fwd_bwd", grid=(nt,),
        in_specs=[row_blk(D_ATTN), row_blk(D_ATTN)] + [rest_blk(s) for s in range(5)] + [halo(2), halo(3)]
                 + _x_block_specs(n_sub, LANE) + [const((LANE, D_MODEL))] + _x_block_specs(n_sub, LANE)
                 + [const((D_MODEL, D_MODEL)), const((1, D_ATTN)), const((1, D_CONV)), const((1, D_MODEL)),
                    const((SUBLANE, D_CONV))],
        out_specs=(row_blk(D_MODEL), row_blk(D_ATTN), row_blk(HEADS * LANE), row_blk(D_ATTN), row_blk(D_CONV),
                   row_blk(D_CONV), row_blk(D_CONV),
                   acc((1, LANE)), acc((1, D_MODEL)), acc((1, D_ATTN)), acc((1, D_CONV)), acc((D_MODEL, D_MODEL))),
        out_shape=(jax.ShapeDtypeStruct((lp, D_MODEL), F32), jax.ShapeDtypeStruct((lp, D_ATTN), MXU_DTYPE),
                   jax.ShapeDtypeStruct((lp, HEADS * LANE), F32), jax.ShapeDtypeStruct((lp, D_ATTN), MXU_DTYPE),
                   jax.ShapeDtypeStruct((lp, D_CONV), MXU_DTYPE), jax.ShapeDtypeStruct((lp, D_CONV), MXU_DTYPE),
                   jax.ShapeDtypeStruct((lp, D_CONV), F32),
                   jax.ShapeDtypeStruct((1, LANE), F32), jax.ShapeDtypeStruct((1, D_MODEL), F32),
                   jax.ShapeDtypeStruct((1, D_ATTN), F32), jax.ShapeDtypeStruct((1, D_CONV), F32),
                   jax.ShapeDtypeStruct((D_MODEL, D_MODEL), F32)),
        compiler_params=_params(("arbitrary",)),
    )(o, l_sum, *([rest] * 5), rest, rest, *([x2] * n_sub), meta_blk, *([tgt2] * n_sub),
      w_out, attn_g, conv_g, final_g, conv_w8)


def _bwd_in(x2, meta_blk, norm_g, w_pad, bf_pad, fl, dc, dq, dk, dv, dza, dgb, dzc, dconv, rest, d_out, conv_w8):
    lp = fl.shape[0]
    t = ROW_TILE
    nt = lp // t
    n_sub = t // LANE
    hb = t // SUBLANE
    rev = lambda i: nt - 1 - i

    def body(*refs):
        x_refs = refs[:n_sub]
        (mb, g_ref, w_ref, bf_ref, fl_ref, dc_ref, dq_ref, dk_ref, dv_ref, dza_ref, dgb_ref, dzc_ref,
         dcv_ref, dcvn_ref, gc_ref, xc_ref, gch_ref, xch_ref, dout_ref, cw_ref) = refs[n_sub:n_sub + 20]
        dp_ref, dh_ref, gn_ref, gbf_ref, gcw_ref, carry = refs[n_sub + 20:]
        step = pl.program_id(0)
        i = rev(step)

        @pl.when(step == 0)
        def _():
            for r in (gn_ref, gbf_ref, gcw_ref, carry):
                r[...] = jnp.zeros_like(r)

        dc8 = jnp.concatenate([dc_ref[...], jnp.zeros((LANE - HEADS, t), F32)], axis=0).T
        triu = jnp.where(lax.broadcasted_iota(jnp.int32, (t, t), 1) >= lax.broadcasted_iota(jnp.int32, (t, t), 0),
                         1.0, 0.0)
        dlogf = _dot_exact(triu, dc8) + carry[...]
        carry[...] = carry[...] + jnp.sum(dc8, axis=0, keepdims=True)
        z = fl_ref[...] + bf_ref[...]
        row = i * t + lax.broadcasted_iota(jnp.int32, (t, LANE), 0)
        d_f = jnp.where(row >= PAD_ROWS, dlogf * (1.0 / (1.0 + jnp.exp(z))), 0.0)
        gbf_ref[...] = gbf_ref[...] + jnp.sum(d_f, axis=0, keepdims=True)
        dcv = dcv_ref[...]
        dcv_next = jnp.where(i == nt - 1, 0.0, dcvn_ref[...])
        d_cx = (cw_ref[2:3, :] * dcv + cw_ref[1:2, :] * _shift_up(dcv, dcv_next, 1)
                + cw_ref[0:1, :] * _shift_up(dcv, dcv_next, 2))
        gc = gc_ref[...]
        xc = xc_ref[...]
        cx = gc * xc
        cx_prev = jnp.where(i == 0, 0.0, gch_ref[...] * xch_ref[...])
        rowi = lax.broadcasted_iota(jnp.int32, (SUBLANE, 1), 0)
        gcw = (jnp.where(rowi == 0, jnp.sum(dcv * _shift_down(cx_prev, cx, 2), axis=0, keepdims=True), 0.0)
               + jnp.where(rowi == 1, jnp.sum(dcv * _shift_down(cx_prev, cx, 1), axis=0, keepdims=True), 0.0)
               + jnp.where(rowi == 2, jnp.sum(dcv * cx, axis=0, keepdims=True), 0.0))
        gcw_ref[...] = gcw_ref[...] + gcw
        dp_ref[:, SEG_Q:SEG_Q + 512] = dq_ref[...]
        dp_ref[:, SEG_K:SEG_K + 512] = dk_ref[...]
        dp_ref[:, SEG_V:SEG_V + 512] = dv_ref[...]
        dp_ref[:, SEG_F:SEG_F + LANE] = d_f.astype(dp_ref.dtype)
        dp_ref[:, SEG_ZA:SEG_ZA + 512] = dza_ref[...]
        dp_ref[:, SEG_GB:SEG_GB + 512] = dgb_ref[...]
        dp_ref[:, SEG_GC:SEG_GC + 512] = (d_cx * xc).astype(dp_ref.dtype)
        dp_ref[:, SEG_XC:SEG_XC + 512] = (d_cx * gc).astype(dp_ref.dtype)
        dp_ref[:, SEG_ZC:SEG_ZC + 512] = dzc_ref[...]
        d_u = _dot(dp_ref[...], w_ref[...])
        first = jnp.where(i == 0, mb[...], x_refs[0][...])
        h = jnp.concatenate([first] + [r[...] for r in x_refs[1:]], axis=0)
        r1 = lax.rsqrt(jnp.mean(h * h, axis=-1, keepdims=True) + EPS)
        n_h = h * r1
        gn_ref[...] = gn_ref[...] + jnp.sum(d_u * n_h, axis=0, keepdims=True)
        dn = d_u * g_ref[...]
        dh_ref[...] = dout_ref[...] + r1 * (dn - n_h * jnp.mean(dn * n_h, axis=-1, keepdims=True))

    def x_specs():
        specs = [pl.BlockSpec((LANE, D_MODEL), lambda s: (jnp.maximum(n_sub * rev(s) - 1, 0), 0))]
        for b in range(1, n_sub):
            specs.append(pl.BlockSpec((LANE, D_MODEL), functools.partial(lambda s, b: (n_sub * rev(s) - 1 + b, 0), b=b)))
        return specs

    row_blk = lambda cols: pl.BlockSpec((t, cols), lambda s: (rev(s), 0))
    rest_blk = lambda k: pl.BlockSpec((t, 512), functools.partial(lambda s, k: (rev(s), k), k=k))
    halo_prev = lambda k: pl.BlockSpec(
        (SUBLANE, 512), functools.partial(lambda s, k: (jnp.maximum(rev(s) * hb - 1, 0), k), k=k))
    halo_next = pl.BlockSpec((SUBLANE, 512), lambda s: (jnp.minimum((rev(s) + 1) * hb, lp // SUBLANE - 1), 0))
    const = lambda shape: pl.BlockSpec(shape, lambda s: (0, 0))
    return pl.pallas_call(
        body, name="bwd_in", grid=(nt,),
        in_specs=x_specs() + [const((LANE, D_MODEL)), const((1, D_MODEL)),
                              pl.BlockSpec((D_IN_PAD, D_MODEL), lambda s: (0, 0), pipeline_mode=pl.Buffered(1)),
                              const((1, LANE)), row_blk(LANE),
                              pl.BlockSpec((HEADS, t), lambda s: (0, rev(s))),
                              row_blk(512), row_blk(512), row_blk(512), row_blk(512), row_blk(512), row_blk(512),
                              row_blk(512), halo_next, rest_blk(2), rest_blk(3), halo_prev(2), halo_prev(3),
                              row_blk(D_MODEL), const((SUBLANE, D_CONV))],
        out_specs=(row_blk(D_IN_PAD), row_blk(D_MODEL), const((1, D_MODEL)), const((1, LANE)),
                   const((SUBLANE, D_CONV))),
        out_shape=(jax.ShapeDtypeStruct((lp, D_IN_PAD), MXU_DTYPE), jax.ShapeDtypeStruct((lp, D_MODEL), F32),
                   jax.ShapeDtypeStruct((1, D_MODEL), F32), jax.ShapeDtypeStruct((1, LANE), F32),
                   jax.ShapeDtypeStruct((SUBLANE, D_CONV), F32)),
        scratch_shapes=[pltpu.VMEM((1, LANE), F32)],
        compiler_params=_params(("arbitrary",)),
    )(*([x2] * n_sub), meta_blk, norm_g, w_pad, bf_pad, fl, dc, dq, dk, dv, dza, dgb, dzc, dconv, dconv,
      rest, rest, rest, rest, d_out, conv_w8)


def _grad_w_in(u, dproj):
    lp = u.shape[0]
    tk = ROW_TILE
    tn = GW_COL_TILE

    def body(d_ref, u_ref, o_ref):
        @pl.when(pl.program_id(1) == 0)
        def _():
            o_ref[...] = jnp.zeros_like(o_ref)

        o_ref[...] = o_ref[...] + lax.dot_general(d_ref[...], u_ref[...], (((0,), (0,)), ((), ())),
                                                  preferred_element_type=F32)

    return pl.pallas_call(
        body, name="grad_w_in", grid=(D_IN_PAD // tn, lp // tk),
        in_specs=[pl.BlockSpec((tk, tn), lambda n, k: (k, n)), pl.BlockSpec((tk, D_MODEL), lambda n, k: (k, 0))],
        out_specs=pl.BlockSpec((tn, D_MODEL), lambda n, k: (n, 0)),
        out_shape=jax.ShapeDtypeStruct((D_IN_PAD, D_MODEL), F32),
        compiler_params=_params(("parallel", "arbitrary")),
    )(dproj, u)


def _pad_rows(wt):
    return jnp.concatenate([wt[:F_END], jnp.zeros((D_IN_PAD - D_IN, wt.shape[1]), wt.dtype), wt[F_END:]], axis=0)


def _unpad_rows(gt):
    return jnp.concatenate([gt[:F_END], gt[SEG_ZA:]], axis=0)


def _by_chip(own, others, me):
    by_mask = jnp.stack([own, others[1], others[0], others[2]])
    return [lax.dynamic_index_in_dim(by_mask, jnp.bitwise_xor(me, s), 0, keepdims=False) for s in range(N_CHIPS)]


def _both_halves(mine, other, c):
    return jnp.where(c == 0, jnp.concatenate([mine, other], axis=0), jnp.concatenate([other, mine], axis=0))


def _local_step(x2, tgt2, meta_full, norm_g, w_pad, b_f, conv_w_full, attn_g, conv_g, w_out_full, final_g):
    lp = x2.shape[0] + FRONT
    nt = lp // ROW_TILE
    meta_blk = jnp.concatenate([jnp.zeros((PAD_ROWS, D_MODEL), F32), meta_full], axis=0)
    bf_pad = jnp.pad(b_f, ((0, 0), (0, LANE - HEADS)))
    conv_w8 = jnp.pad(conv_w_full, ((0, SUBLANE - conv_w_full.shape[0]), (0, 0)))
    q, k, v, rest, fl, ct, u = _in_proj(x2, meta_blk, norm_g, w_pad, bf_pad)
    ct4 = ct.reshape(SUBLANE, nt, 1, ROW_TILE)
    o, l_sum, m_max = _attn_fwd(q, k, v, ct4)
    (d_out, d_o, delta, dza, dgb, dzc, dconv, loss, g_final, g_attn, g_convg, gw_out) = _post(
        o, l_sum, rest, x2, meta_blk, tgt2, w_out_full, attn_g, conv_g, final_g, conv_w8)
    dq, dk, dv, dc = _attn_bwd(q, k, v, d_o, m_max, delta, ct4)
    dproj, d_h, g_norm, g_bf, g_cw = _bwd_in(x2, meta_blk, norm_g, w_pad, bf_pad, fl, dc.reshape(HEADS, lp), dq, dk, dv,
                                             dza, dgb, dzc, dconv, rest, d_out, conv_w8)
    gw_in = _grad_w_in(u, dproj)
    return dict(loss=loss, d_h=d_h, g_norm=g_norm, g_final=g_final, g_attn=g_attn, g_convg=g_convg, g_bf=g_bf,
                g_cw=g_cw, gw_out=gw_out, gw_in=gw_in)


def kernel(x, meta, norm_g, w_in, b_f, conv_w, attn_norm_g, conv_norm_g, w_out, final_norm_g, loss_target, m_meta, m_norm_g, m_w_in, m_b_f, m_conv_w, m_attn_norm_g, m_conv_norm_g, m_w_out, m_final_norm_g, v_meta, v_norm_g, v_w_in, v_b_f, v_conv_w, v_attn_norm_g, v_conv_norm_g, v_w_out, v_final_norm_g):
    cx_, cy_, cc_ = _position()
    chip = 2 * cx_ + cy_
    shard = w_in.shape[2]
    out_half = w_out.shape[1] // 2
    rows_t = lambda a, rows: jnp.pad(jnp.transpose(a[0]), ((0, rows - shard), (0, 0)))
    from_t = lambda a: jnp.transpose(a[:shard])[None]
    wi = rows_t(w_in, 2 * W_IN_HALF).astype(MXU_DTYPE)
    wo = w_out[0].astype(MXU_DTYPE)
    small = jnp.concatenate([meta, jnp.pad(conv_w[0], ((0, 8 - conv_w.shape[1]), (0, meta.shape[1] - conv_w.shape[2])))],
                            axis=0)
    gwi, gwo, gsm = _gather_weights(wi.reshape(2, W_IN_HALF, D_MODEL), wo.reshape(2, out_half, D_MODEL), small)
    w_pad = _pad_rows(jnp.concatenate(
        [s[:shard] for s in _by_chip(wi, gwi.reshape(3, 2 * W_IN_HALF, D_MODEL), chip)], axis=0))
    w_out_full = jnp.concatenate(_by_chip(wo, gwo.reshape(3, 2 * out_half, D_MODEL), chip), axis=0)
    small_full = jnp.concatenate(_by_chip(small, gsm, chip), axis=1)
    meta_full = small_full[:N_META]
    conv_w_full = jnp.concatenate([small_full[N_META:N_META + 3, 256 * s:256 * s + LANE] for s in range(N_CHIPS)], axis=1)
    final_g2 = final_norm_g.reshape(1, D_MODEL)
    r = _local_step(x[0], loss_target[0], meta_full, norm_g, w_pad, b_f, conv_w_full, attn_norm_g, conv_norm_g,
                    w_out_full, final_g2)
    grad_x = r["d_h"][FRONT:][None]
    ga = jnp.pad(_unpad_rows(r["gw_in"]).reshape(N_CHIPS, shard, D_MODEL),
                 ((0, 0), (0, 2 * W_IN_HALF - shard), (0, 0))).reshape(N_CHIPS, 2, W_IN_HALF, D_MODEL)
    gb = r["gw_out"].reshape(N_CHIPS, 2, out_half, D_MODEL)
    ra, rb = _pair_exchange(ga, gb)
    c_idx = jnp.reshape(cc_, (1,)).astype(jnp.int32)
    chip_idx = jnp.reshape(chip, (1,)).astype(jnp.int32)
    pa = _pair_sum(ga, ra, c_idx)
    pb = _pair_sum(gb, rb, c_idx)
    xa, xb = _chip_exchange(pa, pb)
    ha = _chip_sum(pa, xa, chip_idx)
    hb = _chip_sum(pb, xb, chip_idx)
    oa, ob = _pair_share(ha, hb)
    g_w_in_t = _both_halves(ha, oa, cc_)
    g_w_out = _both_halves(hb, ob, cc_)
    d_w_in, nm_w_in, nv_w_in = (from_t(a) for a in _adamw_big(
        rows_t(w_in, ADAM_ROWS), g_w_in_t, rows_t(m_w_in, ADAM_ROWS), rows_t(v_w_in, ADAM_ROWS), ADAM_ROWS // 3))
    g_w_in = from_t(g_w_in_t)
    d_w_out, nm_w_out, nv_w_out = (a[None] for a in _adamw_big(w_out[0], g_w_out, m_w_out[0], v_w_out[0], LANE))
    wide = lambda a: jnp.pad(a, ((0, 0), (0, D_MODEL - a.shape[1])))
    pack = jnp.concatenate([
        r["g_norm"], r["g_final"], jnp.concatenate([r["g_attn"], r["g_convg"]], axis=1), wide(r["g_bf"]),
        wide(r["loss"]), jnp.zeros((3, D_MODEL), F32), r["d_h"][PAD_ROWS:FRONT], wide(r["g_cw"])], axis=0)
    params = (norm_g, final_g2, attn_norm_g, conv_norm_g, b_f, meta, conv_w[0])
    ms = (m_norm_g, m_final_norm_g.reshape(1, D_MODEL), m_attn_norm_g, m_conv_norm_g, m_b_f, m_meta, m_conv_w[0])
    vs = (v_norm_g, v_final_norm_g.reshape(1, D_MODEL), v_attn_norm_g, v_conv_norm_g, v_b_f, v_meta, v_conv_w[0])
    loss, g_s, d_s, m_s, v_s = _small_update(pack, _gather_small(pack), params, ms, vs)

    def ordered(small_list, big_in, big_out):
        s_norm, s_final, s_attn, s_convg, s_bf, s_meta, s_cw = small_list
        return (s_meta, s_norm, big_in, s_bf, s_cw[None], s_attn, s_convg, big_out, s_final.reshape(D_MODEL))

    return (loss.reshape(()), grad_x,
            *ordered(g_s, g_w_in, g_w_out[None]), *ordered(d_s, d_w_in, d_w_out),
            *ordered(m_s, nm_w_in, nm_w_out), *ordered(v_s, nv_w_in, nv_w_out))
```

```python
import functools

import jax
import jax.numpy as jnp
from jax import lax
from jax.experimental import pallas as pl
from jax.experimental.pallas import tpu as pltpu

F32 = jnp.float32
MXU_DTYPE = jnp.bfloat16

D_MODEL = 1024
N_META = 16
HEADS = 8
HEAD_DIM = 64
D_ATTN = HEADS * HEAD_DIM
D_CONV = 512
EPS = 1e-6
LANE = 128
SUBLANE = 8
ROW_TILE = 384
FRONT = LANE
PAD_ROWS = FRONT - N_META
NEG = -1e30
N_CHIPS = 4
N_DEV = 8
VMEM_LIMIT_BYTES = 60 * 1024 * 1024

SEG_Q, SEG_K, SEG_V, SEG_F, SEG_ZA, SEG_GB, SEG_GC, SEG_XC, SEG_ZC = (
    0, 512, 1024, 1536, 1664, 2176, 2688, 3200, 3712)
D_IN = 4104
D_IN_PAD = 4224
F_END = 1544
GW_COL_TILE = 1408
WIN_ROWS = 1152
WIN_HALF = WIN_ROWS // 2
WIN_START = (0, 1024, 2160, 3072)
PIECE_A = 518
A_OFF = (0, 2, 12, 126)
B_OFF = (518, 640, 530, 644)
ADAM_LR = 0.001
ADAM_B1 = 0.9
ADAM_B2 = 0.999
ADAM_EPS = 1e-08
ADAM_WD = 0.01
ADAM_STEP = 10

MESH = pl.DeviceIdType.MESH
ANY = pl.BlockSpec(memory_space=pl.ANY)

PACK_ROWS = 32
SLOT_NORM = (0, 1, 0, 1024)
SLOT_FINAL = (1, 2, 0, 1024)
SLOT_ATTN = (2, 3, 0, 512)
SLOT_CONVG = (2, 3, 512, 1024)
SLOT_BF = (3, 4, 0, 8)
SLOT_META = (8, 24, 0, 256)
SLOT_CONVW = (24, 27, 0, 128)
LOSS_ROW = 4


def _params(sem=None):
    return pltpu.CompilerParams(dimension_semantics=sem, vmem_limit_bytes=VMEM_LIMIT_BYTES)


def _sigmoid(z):
    return 1.0 / (1.0 + jnp.exp(-z))


def _dot(a, b):
    return jnp.dot(a, b, preferred_element_type=F32)


def _dot_nt(a, b):
    return lax.dot_general(a, b, (((1,), (1,)), ((), ())), preferred_element_type=F32)


def _dot_exact(a, b):
    return jnp.dot(a, b, preferred_element_type=F32, precision=lax.Precision.HIGHEST)


def _group_matrix():
    r = lax.broadcasted_iota(jnp.int32, (D_ATTN, D_ATTN), 0) >> 6
    c = lax.broadcasted_iota(jnp.int32, (D_ATTN, D_ATTN), 1) >> 6
    return jnp.where(r == c, 1.0, 0.0).astype(MXU_DTYPE)


def _group_sum(x, gmat):
    hi = x.astype(MXU_DTYPE)
    lo = (x - hi.astype(F32)).astype(MXU_DTYPE)
    return _dot(hi, gmat) + _dot(lo, gmat)


def _x_block_specs(n_sub, rows):
    specs = [pl.BlockSpec((rows, D_MODEL), lambda i: (jnp.maximum(n_sub * i - 1, 0), 0))]
    for b in range(1, n_sub):
        specs.append(pl.BlockSpec((rows, D_MODEL), functools.partial(lambda i, b: (n_sub * i - 1 + b, 0), b=b)))
    return specs


def _position():
    return lax.axis_index("x"), lax.axis_index("y"), lax.axis_index("c")


def _gather_weights(wi, wo, small):
    def body(wi_ref, wo_ref, sm_ref, gwi_ref, gwo_ref, gsm_ref, send_sems, recv_sems):
        x, y, c = _position()
        sibling = (x, y, 1 - c)
        chips = [(1 - x, y), (x, 1 - y), (1 - x, 1 - y)]

        def remote(k, src, dst, to):
            return pltpu.make_async_remote_copy(src_ref=src, dst_ref=dst, send_sem=send_sems.at[k],
                                                recv_sem=recv_sems.at[k], device_id=to, device_id_type=MESH)

        first, passed, landed = [], [], []
        for a, (src_ref, g_ref) in enumerate(((wi_ref, gwi_ref), (wo_ref, gwo_ref))):
            for j, (cx, cy) in enumerate(chips):
                slot = g_ref.at[j, c]
                first.append(remote(6 * a + j, src_ref.at[c], slot, (cx, cy, c)))
                landed.append(remote(6 * a + j, slot, slot, sibling))
                passed.append(remote(6 * a + 3 + j, slot, slot, sibling))
        for j, (cx, cy) in enumerate(chips):
            first.append(remote(12 + j, sm_ref, gsm_ref.at[j], (cx, cy, c)))
        for cp in first:
            cp.start()
        for arrived, onward in zip(landed, passed):
            arrived.wait_recv()
            onward.start()
        for a, g_ref in enumerate((gwi_ref, gwo_ref)):
            for j in range(3):
                remote(6 * a + 3 + j, g_ref.at[j, 1 - c], g_ref.at[j, 1 - c], sibling).wait_recv()
        for j in range(3):
            remote(12 + j, sm_ref, gsm_ref.at[j], sibling).wait_recv()
        for cp in first + passed:
            cp.wait_send()

    return pl.pallas_call(
        body, name="gather_weights",
        out_shape=(jax.ShapeDtypeStruct((3,) + wi.shape, wi.dtype), jax.ShapeDtypeStruct((3,) + wo.shape, wo.dtype),
                   jax.ShapeDtypeStruct((3,) + small.shape, small.dtype)),
        in_specs=[ANY, ANY, ANY], out_specs=(ANY, ANY, ANY),
        scratch_shapes=[pltpu.SemaphoreType.DMA((15,)), pltpu.SemaphoreType.DMA((15,))],
    )(wi, wo, small)


def _pair_exchange(gw, gb):
    def body(gw_ref, gb_ref, ra_ref, rb_ref, send_sems, recv_sems):
        x, y, c = _position()
        sibling = (x, y, 1 - c)
        copies = [pltpu.make_async_remote_copy(
            src_ref=gb_ref.at[:, 1 - c], dst_ref=rb_ref, send_sem=send_sems.at[N_CHIPS], recv_sem=recv_sems.at[N_CHIPS],
            device_id=sibling, device_id_type=MESH)]
        for s, start in enumerate(WIN_START):
            rows = pl.ds(pl.multiple_of(start + WIN_HALF * (1 - c), SUBLANE), WIN_HALF)
            copies.append(pltpu.make_async_remote_copy(
                src_ref=gw_ref.at[rows], dst_ref=ra_ref.at[s], send_sem=send_sems.at[s], recv_sem=recv_sems.at[s],
                device_id=sibling, device_id_type=MESH))
        for cp in copies:
            cp.start()
        for cp in copies:
            cp.wait()

    return pl.pallas_call(
        body, name="grad_pair_exchange",
        out_shape=(jax.ShapeDtypeStruct((N_CHIPS, WIN_HALF, D_MODEL), gw.dtype),
                   jax.ShapeDtypeStruct((N_CHIPS,) + gb.shape[2:], gb.dtype)),
        in_specs=[ANY, ANY], out_specs=(ANY, ANY),
        scratch_shapes=[pltpu.SemaphoreType.DMA((N_CHIPS + 1,)), pltpu.SemaphoreType.DMA((N_CHIPS + 1,))],
    )(gw, gb)


def _chip_exchange(pa, pb):
    def body(pa_ref, pb_ref, ra_ref, rb_ref, send_sems, recv_sems):
        x, y, c = _position()
        chips = [(1 - x, y), (x, 1 - y), (1 - x, 1 - y)]
        copies = []
        for a, (src, dst) in enumerate(((pa_ref, ra_ref), (pb_ref, rb_ref))):
            for j, (cx, cy) in enumerate(chips):
                copies.append(pltpu.make_async_remote_copy(
                    src_ref=src.at[2 * cx + cy], dst_ref=dst.at[j], send_sem=send_sems.at[3 * a + j],
                    recv_sem=recv_sems.at[3 * a + j], device_id=(cx, cy, c), device_id_type=MESH))
        for cp in copies:
            cp.start()
        for cp in copies:
            cp.wait()

    return pl.pallas_call(
        body, name="grad_chip_exchange",
        out_shape=(jax.ShapeDtypeStruct((3,) + pa.shape[1:], pa.dtype),
                   jax.ShapeDtypeStruct((3,) + pb.shape[1:], pb.dtype)),
        in_specs=[ANY, ANY], out_specs=(ANY, ANY),
        scratch_shapes=[pltpu.SemaphoreType.DMA((6,)), pltpu.SemaphoreType.DMA((6,))],
    )(pa, pb)


def _pair_share(ha, hb):
    def body(ha_ref, hb_ref, oa_ref, ob_ref, send_sems, recv_sems):
        x, y, c = _position()
        copies = [pltpu.make_async_remote_copy(
            src_ref=src, dst_ref=dst, send_sem=send_sems.at[k], recv_sem=recv_sems.at[k],
            device_id=(x, y, 1 - c), device_id_type=MESH)
            for k, (src, dst) in enumerate(((ha_ref, oa_ref), (hb_ref, ob_ref)))]
        for cp in copies:
            cp.start()
        for cp in copies:
            cp.wait()

    return pl.pallas_call(
        body, name="grad_pair_share",
        out_shape=(jax.ShapeDtypeStruct(ha.shape, ha.dtype), jax.ShapeDtypeStruct(hb.shape, hb.dtype)),
        in_specs=[ANY, ANY], out_specs=(ANY, ANY),
        scratch_shapes=[pltpu.SemaphoreType.DMA((2,)), pltpu.SemaphoreType.DMA((2,))],
    )(ha, hb)


def _gather_small(pack):
    def body(p_ref, o_ref, send_sems, recv_sems):
        x, y, c = _position()
        copies = []
        for mask in range(1, N_DEV):
            peer = (1 - x if mask & 4 else x, 1 - y if mask & 2 else y, 1 - c if mask & 1 else c)
            copies.append(pltpu.make_async_remote_copy(
                src_ref=p_ref, dst_ref=o_ref.at[mask - 1], send_sem=send_sems.at[mask - 1],
                recv_sem=recv_sems.at[mask - 1], device_id=peer, device_id_type=MESH))
        for cp in copies:
            cp.start()
        for cp in copies:
            cp.wait()

    return pl.pallas_call(
        body, name="gather_small",
        out_shape=jax.ShapeDtypeStruct((N_DEV - 1,) + pack.shape, pack.dtype),
        in_specs=[ANY], out_specs=ANY,
        scratch_shapes=[pltpu.SemaphoreType.DMA((N_DEV - 1,)), pltpu.SemaphoreType.DMA((N_DEV - 1,))],
    )(pack)


def _pair_sum(mine, recv, c_idx):
    rows, cols = mine.shape[2:]

    def body(c_ref, a_ref, b_ref, o_ref):
        o_ref[...] = a_ref[...] + b_ref[...]

    return pl.pallas_call(
        body, name="grad_pair_sum",
        grid_spec=pltpu.PrefetchScalarGridSpec(
            num_scalar_prefetch=1, grid=(N_CHIPS,),
            in_specs=[pl.BlockSpec((None, None, rows, cols), lambda s, c_ref: (s, c_ref[0], 0, 0)),
                      pl.BlockSpec((None, rows, cols), lambda s, c_ref: (s, 0, 0))],
            out_specs=pl.BlockSpec((None, rows, cols), lambda s, c_ref: (s, 0, 0))),
        out_shape=jax.ShapeDtypeStruct(recv.shape, recv.dtype),
        compiler_params=_params(("parallel",)),
    )(c_idx, mine, recv)


def _window_start(s):
    return jnp.where(s == 0, WIN_START[0], jnp.where(s == 1, WIN_START[1], jnp.where(s == 2, WIN_START[2], WIN_START[3])))


def _pair_sum_windows(gw, recv, c_idx):
    tr = WIN_HALF // 3

    def body(c_ref, a_ref, b_ref, o_ref):
        o_ref[...] = a_ref[...] + b_ref[...]

    return pl.pallas_call(
        body, name="grad_pair_sum_windows",
        grid_spec=pltpu.PrefetchScalarGridSpec(
            num_scalar_prefetch=1, grid=(N_CHIPS, WIN_HALF // tr),
            in_specs=[pl.BlockSpec((pl.Element(tr), pl.Element(D_MODEL)),
                                   lambda s, i, c_ref: (pl.multiple_of(
                                       _window_start(s) + WIN_HALF * c_ref[0] + tr * i, SUBLANE), 0)),
                      pl.BlockSpec((None, tr, D_MODEL), lambda s, i, c_ref: (s, i, 0))],
            out_specs=pl.BlockSpec((None, tr, D_MODEL), lambda s, i, c_ref: (s, i, 0))),
        out_shape=jax.ShapeDtypeStruct(recv.shape, recv.dtype),
        compiler_params=_params(("parallel", "parallel")),
    )(c_idx, gw, recv)


def _assemble_w(own, others, starts):
    def body(starts_ref, own_ref, oth_ref, o_ref):
        o_ref[...] = jnp.zeros_like(o_ref)
        for k in range(N_CHIPS):
            rows = pl.ds(pl.multiple_of(starts_ref[k], 2 * SUBLANE), WIN_ROWS)
            o_ref[rows, :] = o_ref[rows, :] + (own_ref[...] if k == 0 else oth_ref[k - 1])

    return pl.pallas_call(
        body, name="assemble_w",
        in_specs=[pl.BlockSpec(memory_space=pltpu.SMEM), pl.BlockSpec(memory_space=pltpu.VMEM),
                  pl.BlockSpec(memory_space=pltpu.VMEM)],
        out_specs=pl.BlockSpec(memory_space=pltpu.VMEM),
        out_shape=jax.ShapeDtypeStruct((D_IN_PAD, D_MODEL), own.dtype),
        compiler_params=_params(),
    )(starts, own, others)


def _chip_sum(psum, recv3, chip_idx):
    rows, cols = psum.shape[1:]
    tr = rows // 2

    def body(s_ref, p_ref, r0, r1, r2, o_ref):
        o_ref[...] = ((p_ref[...] + r0[...]) + r1[...]) + r2[...]

    return pl.pallas_call(
        body, name="grad_chip_sum",
        grid_spec=pltpu.PrefetchScalarGridSpec(
            num_scalar_prefetch=1, grid=(2,),
            in_specs=[pl.BlockSpec((None, tr, cols), lambda i, s_ref: (s_ref[0], i, 0))] +
                     [pl.BlockSpec((None, tr, cols), functools.partial(lambda i, s_ref, j: (j, i, 0), j=j))
                      for j in range(3)],
            out_specs=pl.BlockSpec((tr, cols), lambda i, s_ref: (i, 0))),
        out_shape=jax.ShapeDtypeStruct((rows, cols), psum.dtype),
        compiler_params=_params(("parallel",)),
    )(chip_idx, psum, recv3, recv3, recv3)


def _adamw_math(w, g, m, v):
    m = ADAM_B1 * m + (1.0 - ADAM_B1) * g
    v = ADAM_B2 * v + (1.0 - ADAM_B2) * (g * g)
    m_hat = m / (1.0 - ADAM_B1 ** ADAM_STEP)
    v_hat = v / (1.0 - ADAM_B2 ** ADAM_STEP)
    delta = -ADAM_LR * (m_hat / (jnp.sqrt(v_hat) + ADAM_EPS) + ADAM_WD * w)
    return delta, m, v


def _adamw_big(w, g, m, v, tr):
    rows, cols = w.shape
    assert rows % tr == 0 and g.shape[0] >= rows

    def body(w_ref, g_ref, m_ref, v_ref, d_out, m_out, v_out):
        d, m2, v2 = _adamw_math(w_ref[...], g_ref[...], m_ref[...], v_ref[...])
        d_out[...] = d
        m_out[...] = m2
        v_out[...] = v2

    spec = pl.BlockSpec((tr, cols), lambda i: (i, 0))
    sds = jax.ShapeDtypeStruct((rows, cols), F32)
    return pl.pallas_call(
        body, name="adamw_big", grid=(rows // tr,), in_specs=[spec] * 4, out_specs=(spec,) * 3,
        out_shape=(sds,) * 3, compiler_params=_params(("parallel",)),
    )(w, g, m, v)


def _adamw_rows(w3, g, m3, v3):
    rows, _, cols = w3.shape
    tc = 2 * LANE

    def body(w_ref, g_ref, m_ref, v_ref, g_out, d_out, m_out, v_out):
        g = g_ref[...]
        d, m2, v2 = _adamw_math(w_ref[:, 0, :], g, m_ref[:, 0, :], v_ref[:, 0, :])
        g_out[:, 0, :] = g
        d_out[:, 0, :] = d
        m_out[:, 0, :] = m2
        v_out[:, 0, :] = v2

    spec3 = pl.BlockSpec((rows, 1, tc), lambda i: (0, 0, i))
    sds = jax.ShapeDtypeStruct((rows, 1, cols), F32)
    return pl.pallas_call(
        body, name="adamw_rows", grid=(cols // tc,),
        in_specs=[spec3, pl.BlockSpec((rows, tc), lambda i: (0, i)), spec3, spec3], out_specs=(spec3,) * 4,
        out_shape=(sds,) * 4, compiler_params=_params(("parallel",)),
    )(w3, g, m3, v3)


def _small_update(own, others, params, ms, vs):
    slots = (SLOT_NORM, SLOT_FINAL, SLOT_ATTN, SLOT_CONVG, SLOT_BF, SLOT_META, SLOT_CONVW)
    n = len(slots)

    def body(*refs):
        own_ref, gp_ref = refs[:2]
        refs = refs[1:]
        w_refs, m_refs, v_refs = refs[1:1 + n], refs[1 + n:1 + 2 * n], refs[1 + 2 * n:1 + 3 * n]
        outs = refs[1 + 3 * n:2 + 7 * n]
        loss_ref = outs[0]
        g_outs, d_outs, m_outs, v_outs = (outs[1 + k * n:1 + (k + 1) * n] for k in range(4))
        g_scr, w_scr, m_scr, v_scr = refs[2 + 7 * n:]
        x, y, c = _position()
        shard = 2 * x + y
        me = 4 * x + 2 * y + c
        tot = None
        for d in range(N_DEV):
            rel = jnp.bitwise_xor(me, d)
            term = jnp.where(rel == 0, own_ref[...], gp_ref[jnp.maximum(rel, 1) - 1])
            tot = term if tot is None else tot + term
        r0, r1, _, _ = SLOT_META
        meta_sel = tot[r0:r1, 0:256]
        cw_sel = tot[24:32, 0:128]
        for k in range(1, N_CHIPS):
            meta_sel = jnp.where(shard == k, tot[r0:r1, 256 * k:256 * (k + 1)], meta_sel)
            cw_sel = jnp.where(shard == k, tot[24:32, 128 * k:128 * (k + 1)], cw_sel)
        zeros = jnp.zeros((PACK_ROWS, D_MODEL), F32)
        for scr in (g_scr, w_scr, m_scr, v_scr):
            scr[...] = zeros
        g_scr[0:8, :] = tot[0:8, :]
        g_scr[r0:r1, 0:256] = meta_sel
        g_scr[24:32, 0:128] = cw_sel
        for (a, b, c0, c1), w_ref, m_ref, v_ref in zip(slots, w_refs, m_refs, v_refs):
            w_scr[a:b, c0:c1] = w_ref[...]
            m_scr[a:b, c0:c1] = m_ref[...]
            v_scr[a:b, c0:c1] = v_ref[...]
        loss_ref[...] = g_scr[LOSS_ROW:LOSS_ROW + 1, 0:1]
        d, m2, v2 = _adamw_math(w_scr[...], g_scr[...], m_scr[...], v_scr[...])
        w_scr[...] = d
        m_scr[...] = m2
        v_scr[...] = v2
        for (a, b, c0, c1), g_o, d_o, m_o, v_o in zip(slots, g_outs, d_outs, m_outs, v_outs):
            g_o[...] = g_scr[a:b, c0:c1]
            d_o[...] = w_scr[a:b, c0:c1]
            m_o[...] = m_scr[a:b, c0:c1]
            v_o[...] = v_scr[a:b, c0:c1]

    shapes = [jax.ShapeDtypeStruct(p.shape, F32) for p in params]
    out = pl.pallas_call(
        body, name="small_update",
        out_shape=[jax.ShapeDtypeStruct((1, 1), F32)] + shapes * 4,
        scratch_shapes=[pltpu.VMEM((PACK_ROWS, D_MODEL), F32)] * 4,
        compiler_params=_params(),
    )(own, others, *params, *ms, *vs)
    return out[0], out[1:1 + n], out[1 + n:1 + 2 * n], out[1 + 2 * n:1 + 3 * n], out[1 + 3 * n:1 + 4 * n]


def _in_proj(x2, meta_blk, norm_g, w_pad, bf_pad):
    seq = x2.shape[0]
    lp = seq + FRONT
    t = ROW_TILE
    nt = lp // t
    n_sub = t // LANE

    def body(*refs):
        x_refs = refs[:n_sub]
        mb, g_ref, w_ref, bf_ref = refs[n_sub:n_sub + 4]
        q_ref, k_ref, v_ref, rest_ref, fl_ref, ct_ref, u_ref, carry = refs[n_sub + 4:]
        i = pl.program_id(0)

        @pl.when(i == 0)
        def _():
            carry[...] = jnp.zeros_like(carry)

        first = jnp.where(i == 0, mb[...], x_refs[0][...])
        h = jnp.concatenate([first] + [r[...] for r in x_refs[1:]], axis=0)
        ms = jnp.mean(h * h, axis=-1, keepdims=True)
        u = ((h * lax.rsqrt(ms + EPS)) * g_ref[...]).astype(MXU_DTYPE)
        u_ref[...] = u

        def seg(a, width):
            return _dot_nt(u, w_ref[a:a + width, :])

        q_ref[...] = (seg(SEG_Q, D_ATTN) * (HEAD_DIM ** -0.5)).astype(MXU_DTYPE)
        k_ref[...] = seg(SEG_K, D_ATTN).astype(MXU_DTYPE)
        v_ref[...] = seg(SEG_V, D_ATTN).astype(MXU_DTYPE)
        for s in range(5):
            rest_ref[:, 512 * s:512 * (s + 1)] = seg(SEG_ZA + 512 * s, 512)
        fl = seg(SEG_F, LANE)
        fl_ref[...] = fl
        z = fl + bf_ref[...]
        logf = jnp.minimum(z, 0.0) - jnp.log(1.0 + jnp.exp(-jnp.abs(z)))
        row = i * t + lax.broadcasted_iota(jnp.int32, (t, LANE), 0)
        logf = jnp.where(row >= PAD_ROWS, logf, 0.0)
        tri = jnp.where(lax.broadcasted_iota(jnp.int32, (t, t), 0) >= lax.broadcasted_iota(jnp.int32, (t, t), 1),
                        1.0, 0.0)
        cs = _dot_exact(tri, logf) + carry[...]
        carry[...] = carry[...] + jnp.sum(logf, axis=0, keepdims=True)
        col = i * t + lax.broadcasted_iota(jnp.int32, (SUBLANE, t), 1)
        ct_ref[...] = jnp.where(col >= PAD_ROWS, cs.T[0:SUBLANE, :], -NEG)

    row_blk = lambda cols: pl.BlockSpec((t, cols), lambda i: (i, 0))
    const = lambda shape: pl.BlockSpec(shape, lambda i: (0, 0))
    return pl.pallas_call(
        body, name="in_proj", grid=(nt,),
        in_specs=_x_block_specs(n_sub, LANE) + [const((LANE, D_MODEL)), const((1, D_MODEL)),
                                                pl.BlockSpec((D_IN_PAD, D_MODEL), lambda i: (0, 0),
                                                             pipeline_mode=pl.Buffered(1)),
                                                const((1, LANE))],
        out_specs=(row_blk(D_ATTN), row_blk(D_ATTN), row_blk(D_ATTN), row_blk(5 * 512), row_blk(LANE),
                   pl.BlockSpec((SUBLANE, t), lambda i: (0, i)), row_blk(D_MODEL)),
        out_shape=(jax.ShapeDtypeStruct((lp, D_ATTN), MXU_DTYPE), jax.ShapeDtypeStruct((lp, D_ATTN), MXU_DTYPE),
                   jax.ShapeDtypeStruct((lp, D_ATTN), MXU_DTYPE), jax.ShapeDtypeStruct((lp, 5 * 512), F32),
                   jax.ShapeDtypeStruct((lp, LANE), F32),
                   jax.ShapeDtypeStruct((SUBLANE, lp), F32), jax.ShapeDtypeStruct((lp, D_MODEL), MXU_DTYPE)),
        scratch_shapes=[pltpu.VMEM((1, LANE), F32)],
        compiler_params=_params(("arbitrary",)),
    )(*([x2] * n_sub), meta_blk, norm_g, w_pad, bf_pad)


def _head_masks():
    lane = lax.broadcasted_iota(jnp.int32, (1, LANE), 1)
    return lane < HEAD_DIM, lane >= HEAD_DIM


def _pair_specs(lp, nt, t):
    blk = pl.BlockSpec((lp, LANE), lambda g: (0, g))
    ct_a = pl.BlockSpec((None, nt, 1, t), lambda g: (2 * g, 0, 0, 0))
    ct_b = pl.BlockSpec((None, nt, 1, t), lambda g: (2 * g + 1, 0, 0, 0))
    return blk, ct_a, ct_b


def _sub_rows(s, col):
    return jnp.concatenate([s[:, a * LANE:(a + 1) * LANE] - col for a in range(s.shape[1] // LANE)], axis=1)


def _loop_by_two(lo, hi, step, init):
    def pair(jj, carry):
        return step(lo + 2 * jj + 1, step(lo + 2 * jj, carry))

    pairs = (hi - lo) // 2
    carry = lax.fori_loop(0, pairs, pair, init)
    return lax.fori_loop(lo + 2 * pairs, hi, step, carry)


def _lane_chunks(s):
    return [s[:, a * LANE:(a + 1) * LANE] for a in range(s.shape[1] // LANE)]


def _attn_fwd(q, k, v, ct4):
    lp = q.shape[0]
    t = ROW_TILE
    nt = lp // t

    def body(q_ref, k_ref, v_ref, cta_ref, ctb_ref, o_ref, l_ref, m_ref):
        masks = _head_masks()
        ct_refs = (cta_ref, ctb_ref)
        below = lax.broadcasted_iota(jnp.int32, (t, t), 1) <= lax.broadcasted_iota(jnp.int32, (t, t), 0)
        lane = lax.broadcasted_iota(jnp.int32, (1, LANE), 1)
        head_of_row = lax.broadcasted_iota(jnp.int32, (2 * t, LANE), 0) >= t
        ones_cols = jnp.where(lax.broadcasted_iota(jnp.int32, (2 * t, LANE), 1) == head_of_row.astype(jnp.int32),
                              1.0, 0.0).astype(MXU_DTYPE)

        def q_block(i, _):
            r0 = pl.multiple_of(i * t, t)
            qi = q_ref[pl.ds(r0, t), :]

            def scores(j):
                kj = k_ref[pl.ds(pl.multiple_of(j * t, t), t), :]
                return _dot_nt(qi, jnp.concatenate([jnp.where(hm, kj, 0).astype(MXU_DTYPE) for hm in masks], axis=0))

            def biased(j, hh, s2, diagonal):
                s = s2[:, hh * t:(hh + 1) * t] - ct_refs[hh][j]
                return jnp.where(below, s, NEG) if diagonal else s

            def max_step(j, carry, diagonal):
                s2 = scores(j)
                out = []
                for hh, m in enumerate(carry):
                    for c in _lane_chunks(biased(j, hh, s2, diagonal)):
                        m = jnp.maximum(m, c)
                    out.append(m)
                return tuple(out)

            lanes_neg = jnp.full((t, LANE), NEG, F32)
            carry = _loop_by_two(0, i, functools.partial(max_step, diagonal=False), (lanes_neg, lanes_neg))
            ms = [jnp.max(m, axis=-1, keepdims=True) for m in max_step(i, carry, True)]

            def sum_step(j, acc, diagonal):
                s2 = scores(j)
                vj = v_ref[pl.ds(pl.multiple_of(j * t, t), t), :]
                v2 = jnp.concatenate([jnp.where(hm, vj, 0).astype(MXU_DTYPE) for hm in masks], axis=0)
                parts = [jnp.exp(biased(j, hh, s2, diagonal) - ms[hh]).astype(MXU_DTYPE) for hh in range(2)]
                return acc + _dot(jnp.concatenate(parts, axis=1), jnp.concatenate([v2, ones_cols], axis=1))

            acc = _loop_by_two(0, i, functools.partial(sum_step, diagonal=False), jnp.zeros((t, 2 * LANE), F32))
            acc = sum_step(i, acc, True)
            sums = acc[:, LANE:]
            l_pair = jnp.where(masks[0], jnp.sum(jnp.where(lane == 0, sums, 0.0), axis=-1, keepdims=True),
                               jnp.sum(jnp.where(lane == 1, sums, 0.0), axis=-1, keepdims=True))
            o_ref[pl.ds(r0, t), :] = acc[:, :LANE] / l_pair
            l_ref[pl.ds(r0, t), :] = l_pair
            m_ref[pl.ds(r0, t), 0:LANE] = jnp.broadcast_to(ms[0], (t, LANE))
            m_ref[pl.ds(r0, t), LANE:2 * LANE] = jnp.broadcast_to(ms[1], (t, LANE))
            return 0

        lax.fori_loop(0, nt, q_block, 0)

    blk, ct_a, ct_b = _pair_specs(lp, nt, t)
    return pl.pallas_call(
        body, name="attn_fwd", grid=(HEADS // 2,),
        in_specs=[blk, blk, blk, ct_a, ct_b], out_specs=(blk, blk, pl.BlockSpec((lp, 2 * LANE), lambda g: (0, g))),
        out_shape=(jax.ShapeDtypeStruct((lp, D_ATTN), F32), jax.ShapeDtypeStruct((lp, D_ATTN), F32),
                   jax.ShapeDtypeStruct((lp, HEADS * LANE), F32)),
        compiler_params=_params(("parallel",)),
    )(q, k, v, ct4, ct4)


def _attn_bwd(q, k, v, do, m, delta, ct4):
    lp = q.shape[0]
    t = ROW_TILE
    nt = lp // t

    def body(q_ref, k_ref, v_ref, do_ref, ma_ref, mb_ref, dla_ref, dlb_ref, cta_ref, ctb_ref,
             dq_ref, dk_ref, dv_ref, dc_ref, dq_acc, dk_acc, dv_acc):
        masks = _head_masks()
        ct_refs, m_refs, dl_refs = (cta_ref, ctb_ref), (ma_ref, mb_ref), (dla_ref, dlb_ref)
        below = lax.broadcasted_iota(jnp.int32, (t, t), 1) <= lax.broadcasted_iota(jnp.int32, (t, t), 0)
        tn = (((0,), (0,)), ((), ()))
        dq_acc[...] = jnp.zeros_like(dq_acc)

        def k_block(j, _):
            c0 = pl.multiple_of(j * t, t)
            kj = k_ref[pl.ds(c0, t), :]
            vj = v_ref[pl.ds(c0, t), :]
            k2 = jnp.concatenate([jnp.where(hm, kj, 0).astype(MXU_DTYPE) for hm in masks], axis=0)
            v2 = jnp.concatenate([jnp.where(hm, vj, 0).astype(MXU_DTYPE) for hm in masks], axis=0)
            ck = [r[j] for r in ct_refs]
            dk_acc[...] = jnp.zeros_like(dk_acc)
            dv_acc[...] = jnp.zeros_like(dv_acc)

            def q_block(i, colsums, diagonal):
                r0 = pl.multiple_of(i * t, t)
                qi = q_ref[pl.ds(r0, t), :]
                doi = do_ref[pl.ds(r0, t), :]
                q2 = jnp.concatenate([jnp.where(hm, qi, 0).astype(MXU_DTYPE) for hm in masks], axis=0)
                do2 = jnp.concatenate([jnp.where(hm, doi, 0).astype(MXU_DTYPE) for hm in masks], axis=0)
                s2 = _dot_nt(qi, k2)
                dp2 = _dot_nt(doi, v2)
                out, ps, dss = [], [], []
                for hh in range(2):
                    s = s2[:, hh * t:(hh + 1) * t] - ck[hh]
                    if diagonal:
                        s = jnp.where(below, s, NEG)
                    p = jnp.exp(_sub_rows(s, m_refs[hh][pl.ds(r0, t), :])).astype(MXU_DTYPE)
                    ds32 = p.astype(F32) * _sub_rows(dp2[:, hh * t:(hh + 1) * t], dl_refs[hh][pl.ds(r0, t), :])
                    ps.append(p)
                    dss.append(ds32.astype(MXU_DTYPE))
                    out.append(colsums[hh] + jnp.sum(ds32, axis=0, keepdims=True))
                dv_acc[...] = dv_acc[...] + lax.dot_general(jnp.concatenate(ps, axis=0), do2, tn,
                                                            preferred_element_type=F32)
                dk_acc[...] = dk_acc[...] + lax.dot_general(jnp.concatenate(dss, axis=0), q2, tn,
                                                            preferred_element_type=F32)
                dq_acc[pl.ds(r0, t), :] = dq_acc[pl.ds(r0, t), :] + _dot(jnp.concatenate(dss, axis=1), k2)
                return tuple(out)

            colsums = q_block(j, (jnp.zeros((1, t), F32), jnp.zeros((1, t), F32)), True)
            colsums = lax.fori_loop(j + 1, nt, functools.partial(q_block, diagonal=False), colsums)
            for hh in range(2):
                dc_ref[hh, j] = -colsums[hh]
            dk_ref[pl.ds(c0, t), :] = dk_acc[...].astype(dk_ref.dtype)
            dv_ref[pl.ds(c0, t), :] = dv_acc[...].astype(dv_ref.dtype)
            return 0

        lax.fori_loop(0, nt, k_block, 0)
        dq_ref[...] = (dq_acc[...] * (HEAD_DIM ** -0.5)).astype(dq_ref.dtype)

    blk, ct_a, ct_b = _pair_specs(lp, nt, t)
    rep_a = pl.BlockSpec((lp, LANE), lambda g: (0, 2 * g))
    rep_b = pl.BlockSpec((lp, LANE), lambda g: (0, 2 * g + 1))
    return pl.pallas_call(
        body, name="attn_bwd", grid=(HEADS // 2,),
        in_specs=[blk] * 4 + [rep_a, rep_b, rep_a, rep_b, ct_a, ct_b],
        out_specs=(blk, blk, blk, pl.BlockSpec((2, nt, 1, t), lambda g: (g, 0, 0, 0))),
        out_shape=(jax.ShapeDtypeStruct((lp, D_ATTN), MXU_DTYPE),) * 3
                  + (jax.ShapeDtypeStruct((HEADS, nt, 1, t), F32),),
        scratch_shapes=[pltpu.VMEM((lp, LANE), F32), pltpu.VMEM((t, LANE), F32), pltpu.VMEM((t, LANE), F32)],
        compiler_params=_params(("parallel",)),
    )(q, k, v, do, m, m, delta, delta, ct4, ct4)


def _shift_down(prev8, cur, k):
    ext = jnp.concatenate([prev8, cur], axis=0)
    return pltpu.roll(ext, k, 0)[SUBLANE:, :]


def _shift_up(cur, next8, k):
    ext = jnp.concatenate([cur, next8], axis=0)
    n = ext.shape[0]
    return pltpu.roll(ext, n - k, 0)[:cur.shape[0], :]


def _post(o, l_sum, rest, x2, meta_blk, tgt2, w_out, attn_g, conv_g, final_g, conv_w8):
    lp = o.shape[0]
    t = ROW_TILE
    nt = lp // t
    n_sub = t // LANE
    hb = t // SUBLANE

    def body(*refs):
        o_ref, l_ref, za_ref, gb_ref, gc_ref, xc_ref, zc_ref, gch_ref, xch_ref = refs[:9]
        refs = refs[1:]
        x_refs = refs[8:8 + n_sub]
        mb = refs[8 + n_sub]
        t_refs = refs[9 + n_sub:9 + 2 * n_sub]
        wo_ref, ag_ref, cg_ref, fg_ref, cw_ref = refs[9 + 2 * n_sub:14 + 2 * n_sub]
        (dout_ref, do_ref, dl_ref, dza_ref, dgb_ref, dzc_ref, dcv_ref,
         loss_ref, gf_ref, gag_ref, gcg_ref, gwo_ref) = refs[14 + 2 * n_sub:]
        i = pl.program_id(0)

        @pl.when(i == 0)
        def _():
            for r in (loss_ref, gf_ref, gag_ref, gcg_ref, gwo_ref):
                r[...] = jnp.zeros_like(r)

        gmat = _group_matrix()
        inv_g = 1.0 / HEAD_DIM
        o_v = o_ref[...]
        ra = lax.rsqrt(_group_sum(o_v * o_v, gmat) * inv_g + EPS)
        n_a = o_v * ra
        a_n = n_a * ag_ref[...]
        za = za_ref[...]
        sig_a = _sigmoid(za)
        sz_a = za * sig_a
        y_a = a_n * sz_a
        gb = gb_ref[...]
        gc = gc_ref[...]
        xc = xc_ref[...]
        cx = gc * xc
        cx_prev = jnp.where(i == 0, 0.0, gch_ref[...] * xch_ref[...])
        conv = (cw_ref[0:1, :] * _shift_down(cx_prev, cx, 2) + cw_ref[1:2, :] * _shift_down(cx_prev, cx, 1)
                + cw_ref[2:3, :] * cx)
        e = gb * conv
        re = lax.rsqrt(_group_sum(e * e, gmat) * inv_g + EPS)
        n_e = e * re
        e_n = n_e * cg_ref[...]
        zc = zc_ref[...]
        sig_c = _sigmoid(zc)
        sz_c = zc * sig_c
        y_c = e_n * sz_c
        mix = jnp.concatenate([y_a, y_c], axis=-1)
        mix_b = mix.astype(MXU_DTYPE)
        first = jnp.where(i == 0, mb[...], x_refs[0][...])
        h = jnp.concatenate([first] + [r[...] for r in x_refs[1:]], axis=0)
        out = h + _dot(mix_b, wo_ref[...])
        r2 = lax.rsqrt(jnp.mean(out * out, axis=-1, keepdims=True) + EPS)
        n_f = out * r2
        y = n_f * fg_ref[...]
        tgt = jnp.concatenate([r[...] for r in t_refs], axis=0)
        valid = (i * t + lax.broadcasted_iota(jnp.int32, (t, 1), 0)) >= FRONT
        diff = jnp.where(valid, y - tgt, 0.0)
        loss_ref[...] = loss_ref[...] + 0.5 * jnp.sum(jnp.sum(diff * diff, axis=-1, keepdims=True) * (1.0 / D_MODEL))
        dy = diff * (1.0 / D_MODEL)
        gf_ref[...] = gf_ref[...] + jnp.sum(dy * n_f, axis=0, keepdims=True)
        dn = dy * fg_ref[...]
        d_out = r2 * (dn - n_f * jnp.mean(dn * n_f, axis=-1, keepdims=True))
        dout_ref[...] = d_out
        d_out_b = d_out.astype(MXU_DTYPE)
        d_mix = _dot_nt(d_out_b, wo_ref[...])
        gwo_ref[...] = gwo_ref[...] + _dot(mix.T.astype(MXU_DTYPE), d_out_b)
        d_ya = d_mix[:, :D_ATTN]
        d_yc = d_mix[:, D_ATTN:]
        d_an = d_ya * sz_a
        dza_ref[...] = (d_ya * a_n * (sig_a * (1.0 + za * (1.0 - sig_a)))).astype(dza_ref.dtype)
        gag_ref[...] = gag_ref[...] + jnp.sum(d_an * n_a, axis=0, keepdims=True)
        dn_a = d_an * ag_ref[...]
        d_o = ra * (dn_a - n_a * (_group_sum(dn_a * n_a, gmat) * inv_g))
        d_o_b = (d_o / l_ref[...]).astype(do_ref.dtype)
        do_ref[...] = d_o_b
        head_rep = jnp.where((lax.broadcasted_iota(jnp.int32, (D_ATTN, HEADS * LANE), 0) >> 6)
                             == (lax.broadcasted_iota(jnp.int32, (D_ATTN, HEADS * LANE), 1) >> 7), 1.0, 0.0)
        dl_ref[...] = _group_sum(d_o_b.astype(F32) * o_v, head_rep.astype(MXU_DTYPE))
        d_en = d_yc * sz_c
        dzc_ref[...] = (d_yc * e_n * (sig_c * (1.0 + zc * (1.0 - sig_c)))).astype(dzc_ref.dtype)
        gcg_ref[...] = gcg_ref[...] + jnp.sum(d_en * n_e, axis=0, keepdims=True)
        dn_e = d_en * cg_ref[...]
        d_e = re * (dn_e - n_e * (_group_sum(dn_e * n_e, gmat) * inv_g))
        dgb_ref[...] = (d_e * conv).astype(dgb_ref.dtype)
        dcv_ref[...] = d_e * gb

    row_blk = lambda cols: pl.BlockSpec((t, cols), lambda i: (i, 0))
    rest_blk = lambda s: pl.BlockSpec((t, 512), functools.partial(lambda i, s: (i, s), s=s))
    halo = lambda s: pl.BlockSpec((SUBLANE, 512), functools.partial(lambda i, s: (jnp.maximum(i * hb - 1, 0), s), s=s))
    const = lambda shape: pl.BlockSpec(shape, lambda i: (0, 0))
    acc = lambda shape: pl.BlockSpec(shape, lambda i: (0, 0))
    return pl.pallas_call(
        body, name="post_fwd_bwd", grid=(nt,),
        in_specs=[row_blk(D_ATTN), row_blk(D_ATTN)] + [rest_blk(s) for s in range(5)] + [halo(2), halo(3)]
                 + _x_block_specs(n_sub, LANE) + [const((LANE, D_MODEL))] + _x_block_specs(n_sub, LANE)
                 + [const((D_MODEL, D_MODEL)), const((1, D_ATTN)), const((1, D_CONV)), const((1, D_MODEL)),
                    const((SUBLANE, D_CONV))],
        out_specs=(row_blk(D_MODEL), row_blk(D_ATTN), row_blk(HEADS * LANE), row_blk(D_ATTN), row_blk(D_CONV),
                   row_blk(D_CONV), row_blk(D_CONV),
                   acc((1, LANE)), acc((1, D_MODEL)), acc((1, D_ATTN)), acc((1, D_CONV)), acc((D_MODEL, D_MODEL))),
        out_shape=(jax.ShapeDtypeStruct((lp, D_MODEL), F32), jax.ShapeDtypeStruct((lp, D_ATTN), MXU_DTYPE),
                   jax.ShapeDtypeStruct((lp, HEADS * LANE), F32), jax.ShapeDtypeStruct((lp, D_ATTN), MXU_DTYPE),
                   jax.ShapeDtypeStruct((lp, D_CONV), MXU_DTYPE), jax.ShapeDtypeStruct((lp, D_CONV), MXU_DTYPE),
                   jax.ShapeDtypeStruct((lp, D_CONV), F32),
                   jax.ShapeDtypeStruct((1, LANE), F32), jax.ShapeDtypeStruct((1, D_MODEL), F32),
                   jax.ShapeDtypeStruct((1, D_ATTN), F32), jax.ShapeDtypeStruct((1, D_CONV), F32),
                   jax.ShapeDtypeStruct((D_MODEL, D_MODEL), F32)),
        compiler_params=_params(("arbitrary",)),
    )(o, l_sum, *([rest] * 5), rest, rest, *([x2] * n_sub), meta_blk, *([tgt2] * n_sub),
      w_out, attn_g, conv_g, final_g, conv_w8)


def _bwd_in(x2, meta_blk, norm_g, w_pad, bf_pad, fl, dc, dq, dk, dv, dza, dgb, dzc, dconv, rest, d_out, conv_w8):
    lp = fl.shape[0]
    t = ROW_TILE
    nt = lp // t
    n_sub = t // LANE
    hb = t // SUBLANE
    rev = lambda i: nt - 1 - i

    def body(*refs):
        x_refs = refs[:n_sub]
        (mb, g_ref, w_ref, bf_ref, fl_ref, dc_ref, dq_ref, dk_ref, dv_ref, dza_ref, dgb_ref, dzc_ref,
         dcv_ref, dcvn_ref, gc_ref, xc_ref, gch_ref, xch_ref, dout_ref, cw_ref) = refs[n_sub:n_sub + 20]
        dp_ref, dh_ref, gn_ref, gbf_ref, gcw_ref, carry = refs[n_sub + 20:]
        step = pl.program_id(0)
        i = rev(step)

        @pl.when(step == 0)
        def _():
            for r in (gn_ref, gbf_ref, gcw_ref, carry):
                r[...] = jnp.zeros_like(r)

        dc8 = jnp.concatenate([dc_ref[...], jnp.zeros((LANE - HEADS, t), F32)], axis=0).T
        triu = jnp.where(lax.broadcasted_iota(jnp.int32, (t, t), 1) >= lax.broadcasted_iota(jnp.int32, (t, t), 0),
                         1.0, 0.0)
        dlogf = _dot_exact(triu, dc8) + carry[...]
        carry[...] = carry[...] + jnp.sum(dc8, axis=0, keepdims=True)
        z = fl_ref[...] + bf_ref[...]
        row = i * t + lax.broadcasted_iota(jnp.int32, (t, LANE), 0)
        d_f = jnp.where(row >= PAD_ROWS, dlogf * (1.0 / (1.0 + jnp.exp(z))), 0.0)
        gbf_ref[...] = gbf_ref[...] + jnp.sum(d_f, axis=0, keepdims=True)
        dcv = dcv_ref[...]
        dcv_next = jnp.where(i == nt - 1, 0.0, dcvn_ref[...])
        d_cx = (cw_ref[2:3, :] * dcv + cw_ref[1:2, :] * _shift_up(dcv, dcv_next, 1)
                + cw_ref[0:1, :] * _shift_up(dcv, dcv_next, 2))
        gc = gc_ref[...]
        xc = xc_ref[...]
        cx = gc * xc
        cx_prev = jnp.where(i == 0, 0.0, gch_ref[...] * xch_ref[...])
        rowi = lax.broadcasted_iota(jnp.int32, (SUBLANE, 1), 0)
        gcw = (jnp.where(rowi == 0, jnp.sum(dcv * _shift_down(cx_prev, cx, 2), axis=0, keepdims=True), 0.0)
               + jnp.where(rowi == 1, jnp.sum(dcv * _shift_down(cx_prev, cx, 1), axis=0, keepdims=True), 0.0)
               + jnp.where(rowi == 2, jnp.sum(dcv * cx, axis=0, keepdims=True), 0.0))
        gcw_ref[...] = gcw_ref[...] + gcw
        dp_ref[:, SEG_Q:SEG_Q + 512] = dq_ref[...]
        dp_ref[:, SEG_K:SEG_K + 512] = dk_ref[...]
        dp_ref[:, SEG_V:SEG_V + 512] = dv_ref[...]
        dp_ref[:, SEG_F:SEG_F + LANE] = d_f.astype(dp_ref.dtype)
        dp_ref[:, SEG_ZA:SEG_ZA + 512] = dza_ref[...]
        dp_ref[:, SEG_GB:SEG_GB + 512] = dgb_ref[...]
        dp_ref[:, SEG_GC:SEG_GC + 512] = (d_cx * xc).astype(dp_ref.dtype)
        dp_ref[:, SEG_XC:SEG_XC + 512] = (d_cx * gc).astype(dp_ref.dtype)
        dp_ref[:, SEG_ZC:SEG_ZC + 512] = dzc_ref[...]
        d_u = _dot(dp_ref[...], w_ref[...])
        first = jnp.where(i == 0, mb[...], x_refs[0][...])
        h = jnp.concatenate([first] + [r[...] for r in x_refs[1:]], axis=0)
        r1 = lax.rsqrt(jnp.mean(h * h, axis=-1, keepdims=True) + EPS)
        n_h = h * r1
        gn_ref[...] = gn_ref[...] + jnp.sum(d_u * n_h, axis=0, keepdims=True)
        dn = d_u * g_ref[...]
        dh_ref[...] = dout_ref[...] + r1 * (dn - n_h * jnp.mean(dn * n_h, axis=-1, keepdims=True))

    def x_specs():
        specs = [pl.BlockSpec((LANE, D_MODEL), lambda s: (jnp.maximum(n_sub * rev(s) - 1, 0), 0))]
        for b in range(1, n_sub):
            specs.append(pl.BlockSpec((LANE, D_MODEL), functools.partial(lambda s, b: (n_sub * rev(s) - 1 + b, 0), b=b)))
        return specs

    row_blk = lambda cols: pl.BlockSpec((t, cols), lambda s: (rev(s), 0))
    rest_blk = lambda k: pl.BlockSpec((t, 512), functools.partial(lambda s, k: (rev(s), k), k=k))
    halo_prev = lambda k: pl.BlockSpec(
        (SUBLANE, 512), functools.partial(lambda s, k: (jnp.maximum(rev(s) * hb - 1, 0), k), k=k))
    halo_next = pl.BlockSpec((SUBLANE, 512), lambda s: (jnp.minimum((rev(s) + 1) * hb, lp // SUBLANE - 1), 0))
    const = lambda shape: pl.BlockSpec(shape, lambda s: (0, 0))
    return pl.pallas_call(
        body, name="bwd_in", grid=(nt,),
        in_specs=x_specs() + [const((LANE, D_MODEL)), const((1, D_MODEL)),
                              pl.BlockSpec((D_IN_PAD, D_MODEL), lambda s: (0, 0), pipeline_mode=pl.Buffered(1)),
                              const((1, LANE)), row_blk(LANE),
                              pl.BlockSpec((HEADS, t), lambda s: (0, rev(s))),
                              row_blk(512), row_blk(512), row_blk(512), row_blk(512), row_blk(512), row_blk(512),
                              row_blk(512), halo_next, rest_blk(2), rest_blk(3), halo_prev(2), halo_prev(3),
                              row_blk(D_MODEL), const((SUBLANE, D_CONV))],
        out_specs=(row_blk(D_IN_PAD), row_blk(D_MODEL), const((1, D_MODEL)), const((1, LANE)),
                   const((SUBLANE, D_CONV))),
        out_shape=(jax.ShapeDtypeStruct((lp, D_IN_PAD), MXU_DTYPE), jax.ShapeDtypeStruct((lp, D_MODEL), F32),
                   jax.ShapeDtypeStruct((1, D_MODEL), F32), jax.ShapeDtypeStruct((1, LANE), F32),
                   jax.ShapeDtypeStruct((SUBLANE, D_CONV), F32)),
        scratch_shapes=[pltpu.VMEM((1, LANE), F32)],
        compiler_params=_params(("arbitrary",)),
    )(*([x2] * n_sub), meta_blk, norm_g, w_pad, bf_pad, fl, dc, dq, dk, dv, dza, dgb, dzc, dconv, dconv,
      rest, rest, rest, rest, d_out, conv_w8)


def _grad_w_in(u, dproj):
    lp = u.shape[0]
    tk = ROW_TILE
    tn = GW_COL_TILE

    def body(d_ref, u_ref, o_ref):
        @pl.when(pl.program_id(1) == 0)
        def _():
            o_ref[...] = jnp.zeros_like(o_ref)

        o_ref[...] = o_ref[...] + lax.dot_general(d_ref[...], u_ref[...], (((0,), (0,)), ((), ())),
                                                  preferred_element_type=F32)

    return pl.pallas_call(
        body, name="grad_w_in", grid=(D_IN_PAD // tn, lp // tk),
        in_specs=[pl.BlockSpec((tk, tn), lambda n, k: (k, n)), pl.BlockSpec((tk, D_MODEL), lambda n, k: (k, 0))],
        out_specs=pl.BlockSpec((tn, D_MODEL), lambda n, k: (n, 0)),
        out_shape=jax.ShapeDtypeStruct((D_IN_PAD, D_MODEL), F32),
        compiler_params=_params(("parallel", "arbitrary")),
    )(dproj, u)


def _by_chip(own, others, me):
    by_mask = jnp.stack([own, others[1], others[0], others[2]])
    return [lax.dynamic_index_in_dim(by_mask, jnp.bitwise_xor(me, s), 0, keepdims=False) for s in range(N_CHIPS)]


def _both_halves(mine, other, c):
    return jnp.where(c == 0, jnp.concatenate([mine, other], axis=0), jnp.concatenate([other, mine], axis=0))


def _local_step(x2, tgt2, meta_full, norm_g, w_pad, b_f, conv_w_full, attn_g, conv_g, w_out_full, final_g):
    lp = x2.shape[0] + FRONT
    nt = lp // ROW_TILE
    meta_blk = jnp.concatenate([jnp.zeros((PAD_ROWS, D_MODEL), F32), meta_full], axis=0)
    bf_pad = jnp.pad(b_f, ((0, 0), (0, LANE - HEADS)))
    conv_w8 = jnp.pad(conv_w_full, ((0, SUBLANE - conv_w_full.shape[0]), (0, 0)))
    q, k, v, rest, fl, ct, u = _in_proj(x2, meta_blk, norm_g, w_pad, bf_pad)
    ct4 = ct.reshape(SUBLANE, nt, 1, ROW_TILE)
    o, l_sum, m_max = _attn_fwd(q, k, v, ct4)
    (d_out, d_o, delta, dza, dgb, dzc, dconv, loss, g_final, g_attn, g_convg, gw_out) = _post(
        o, l_sum, rest, x2, meta_blk, tgt2, w_out_full, attn_g, conv_g, final_g, conv_w8)
    dq, dk, dv, dc = _attn_bwd(q, k, v, d_o, m_max, delta, ct4)
    dproj, d_h, g_norm, g_bf, g_cw = _bwd_in(x2, meta_blk, norm_g, w_pad, bf_pad, fl, dc.reshape(HEADS, lp), dq, dk, dv,
                                             dza, dgb, dzc, dconv, rest, d_out, conv_w8)
    gw_in = _grad_w_in(u, dproj)
    return dict(loss=loss, d_h=d_h, g_norm=g_norm, g_final=g_final, g_attn=g_attn, g_convg=g_convg, g_bf=g_bf,
                g_cw=g_cw, gw_out=gw_out, gw_in=gw_in)


def kernel(x, meta, norm_g, w_in, b_f, conv_w, attn_norm_g, conv_norm_g, w_out, final_norm_g, loss_target, m_meta, m_norm_g, m_w_in, m_b_f, m_conv_w, m_attn_norm_g, m_conv_norm_g, m_w_out, m_final_norm_g, v_meta, v_norm_g, v_w_in, v_b_f, v_conv_w, v_attn_norm_g, v_conv_norm_g, v_w_out, v_final_norm_g):
    cx_, cy_, cc_ = _position()
    chip = 2 * cx_ + cy_
    shard = w_in.shape[2]
    out_half = w_out.shape[1] // 2
    pick = lambda vals: jnp.where(chip == 0, vals[0], jnp.where(chip == 1, vals[1], jnp.where(chip == 2, vals[2], vals[3])))
    a_off, b_off = pick(A_OFF), pick(B_OFF)
    wt = jnp.transpose(w_in[0]).astype(MXU_DTYPE)
    wi = lax.dynamic_update_slice_in_dim(
        lax.dynamic_update_slice_in_dim(jnp.zeros((WIN_ROWS, D_MODEL), MXU_DTYPE), wt[:PIECE_A], a_off, 0),
        wt[PIECE_A:], b_off, 0)
    wo = w_out[0].astype(MXU_DTYPE)
    small = jnp.concatenate([meta, jnp.pad(conv_w[0], ((0, 8 - conv_w.shape[1]), (0, meta.shape[1] - conv_w.shape[2])))],
                            axis=0)
    gwi, gwo, gsm = _gather_weights(wi.reshape(2, WIN_HALF, D_MODEL), wo.reshape(2, out_half, D_MODEL), small)
    starts = jnp.stack([_window_start(jnp.bitwise_xor(chip, mask)) for mask in (0, 2, 1, 3)]).astype(jnp.int32)
    w_pad = _assemble_w(wi, gwi.reshape(3, WIN_ROWS, D_MODEL), starts)
    w_out_full = jnp.concatenate(_by_chip(wo, gwo.reshape(3, 2 * out_half, D_MODEL), chip), axis=0)
    small_full = jnp.concatenate(_by_chip(small, gsm, chip), axis=1)
    meta_full = small_full[:N_META]
    conv_w_full = jnp.concatenate([small_full[N_META:N_META + 3, 256 * s:256 * s + LANE] for s in range(N_CHIPS)], axis=1)
    final_g2 = final_norm_g.reshape(1, D_MODEL)
    r = _local_step(x[0], loss_target[0], meta_full, norm_g, w_pad, b_f, conv_w_full, attn_norm_g, conv_norm_g,
                    w_out_full, final_g2)
    grad_x = r["d_h"][FRONT:][None]
    gb = r["gw_out"].reshape(N_CHIPS, 2, out_half, D_MODEL)
    ra, rb = _pair_exchange(r["gw_in"], gb)
    c_idx = jnp.reshape(cc_, (1,)).astype(jnp.int32)
    chip_idx = jnp.reshape(chip, (1,)).astype(jnp.int32)
    pa = _pair_sum_windows(r["gw_in"], ra, c_idx)
    pb = _pair_sum(gb, rb, c_idx)
    xa, xb = _chip_exchange(pa, pb)
    ha = _chip_sum(pa, xa, chip_idx)
    hb = _chip_sum(pb, xb, chip_idx)
    oa, ob = _pair_share(ha, hb)
    g_window = _both_halves(ha, oa, cc_)
    g_w_in_t = jnp.concatenate([lax.dynamic_slice_in_dim(g_window, a_off, PIECE_A, 0),
                                lax.dynamic_slice_in_dim(g_window, b_off, shard - PIECE_A, 0)], axis=0)
    g_w_out = _both_halves(hb, ob, cc_)
    as_rows = lambda a: jnp.transpose(a, (2, 0, 1))
    g_w_in, d_w_in, nm_w_in, nv_w_in = (jnp.transpose(a, (1, 2, 0)) for a in _adamw_rows(
        as_rows(w_in), g_w_in_t, as_rows(m_w_in), as_rows(v_w_in)))
    d_w_out, nm_w_out, nv_w_out = (a[None] for a in _adamw_big(w_out[0], g_w_out, m_w_out[0], v_w_out[0], LANE))
    wide = lambda a: jnp.pad(a, ((0, 0), (0, D_MODEL - a.shape[1])))
    pack = jnp.concatenate([
        r["g_norm"], r["g_final"], jnp.concatenate([r["g_attn"], r["g_convg"]], axis=1), wide(r["g_bf"]),
        wide(r["loss"]), jnp.zeros((3, D_MODEL), F32), r["d_h"][PAD_ROWS:FRONT], wide(r["g_cw"])], axis=0)
    params = (norm_g, final_g2, attn_norm_g, conv_norm_g, b_f, meta, conv_w[0])
    ms = (m_norm_g, m_final_norm_g.reshape(1, D_MODEL), m_attn_norm_g, m_conv_norm_g, m_b_f, m_meta, m_conv_w[0])
    vs = (v_norm_g, v_final_norm_g.reshape(1, D_MODEL), v_attn_norm_g, v_conv_norm_g, v_b_f, v_meta, v_conv_w[0])
    loss, g_s, d_s, m_s, v_s = _small_update(pack, _gather_small(pack), params, ms, vs)

    def ordered(small_list, big_in, big_out):
        s_norm, s_final, s_attn, s_convg, s_bf, s_meta, s_cw = small_list
        return (s_meta, s_norm, big_in, s_bf, s_cw[None], s_attn, s_convg, big_out, s_final.reshape(D_MODEL))

    return (loss.reshape(()), grad_x,
            *ordered(g_s, g_w_in, g_w_out[None]), *ordered(d_s, d_w_in, d_w_out),
            *ordered(m_s, nm_w_in, nm_w_out), *ordered(v_s, nv_w_in, nv_w_out))
```

```python
import functools

import jax
import jax.numpy as jnp
from jax import lax
from jax.experimental import pallas as pl
from jax.experimental.pallas import tpu as pltpu

F32 = jnp.float32
MXU_DTYPE = jnp.bfloat16
WIRE_DTYPE = jnp.bfloat16

D_MODEL = 1024
N_META = 16
HEADS = 8
HEAD_DIM = 64
D_ATTN = HEADS * HEAD_DIM
D_CONV = 512
EPS = 1e-6
LANE = 128
SUBLANE = 8
ROW_TILE = 384
FRONT = LANE
PAD_ROWS = FRONT - N_META
NEG = -1e30
N_CHIPS = 4
N_DEV = 8
VMEM_LIMIT_BYTES = 60 * 1024 * 1024

SEG_Q, SEG_K, SEG_V, SEG_F, SEG_ZA, SEG_GB, SEG_GC, SEG_XC, SEG_ZC = (
    0, 512, 1024, 1536, 1664, 2176, 2688, 3200, 3712)
D_IN = 4104
D_IN_PAD = 4224
F_END = 1544
GW_COL_TILE = 1408
WIN_ROWS = 1152
WIN_HALF = WIN_ROWS // 2
WIN_START = (0, 1024, 2160, 3072)
PIECE_A = 518
A_OFF = (0, 2, 12, 126)
B_OFF = (518, 640, 530, 644)
ADAM_LR = 0.001
ADAM_B1 = 0.9
ADAM_B2 = 0.999
ADAM_EPS = 1e-08
ADAM_WD = 0.01
ADAM_STEP = 10

MESH = pl.DeviceIdType.MESH
ANY = pl.BlockSpec(memory_space=pl.ANY)

PACK_ROWS = 32
SLOT_NORM = (0, 1, 0, 1024)
SLOT_FINAL = (1, 2, 0, 1024)
SLOT_ATTN = (2, 3, 0, 512)
SLOT_CONVG = (2, 3, 512, 1024)
SLOT_BF = (3, 4, 0, 8)
SLOT_META = (8, 24, 0, 256)
SLOT_CONVW = (24, 27, 0, 128)
LOSS_ROW = 4


def _params(sem=None):
    return pltpu.CompilerParams(dimension_semantics=sem, vmem_limit_bytes=VMEM_LIMIT_BYTES)


def _sigmoid(z):
    return 1.0 / (1.0 + jnp.exp(-z))


def _dot(a, b):
    return jnp.dot(a, b, preferred_element_type=F32)


def _dot_nt(a, b):
    return lax.dot_general(a, b, (((1,), (1,)), ((), ())), preferred_element_type=F32)


def _dot_exact(a, b):
    return jnp.dot(a, b, preferred_element_type=F32, precision=lax.Precision.HIGHEST)


def _group_matrix():
    r = lax.broadcasted_iota(jnp.int32, (D_ATTN, D_ATTN), 0) >> 6
    c = lax.broadcasted_iota(jnp.int32, (D_ATTN, D_ATTN), 1) >> 6
    return jnp.where(r == c, 1.0, 0.0).astype(MXU_DTYPE)


def _group_sum(x, gmat):
    hi = x.astype(MXU_DTYPE)
    lo = (x - hi.astype(F32)).astype(MXU_DTYPE)
    return _dot(hi, gmat) + _dot(lo, gmat)


def _x_block_specs(n_sub, rows):
    specs = [pl.BlockSpec((rows, D_MODEL), lambda i: (jnp.maximum(n_sub * i - 1, 0), 0))]
    for b in range(1, n_sub):
        specs.append(pl.BlockSpec((rows, D_MODEL), functools.partial(lambda i, b: (n_sub * i - 1 + b, 0), b=b)))
    return specs


def _position():
    return lax.axis_index("x"), lax.axis_index("y"), lax.axis_index("c")


def _gather_weights(wi, wo, small):
    def body(wi_ref, wo_ref, sm_ref, gwi_ref, gwo_ref, gsm_ref, send_sems, recv_sems):
        x, y, c = _position()
        sibling = (x, y, 1 - c)
        chips = [(1 - x, y), (x, 1 - y), (1 - x, 1 - y)]

        def remote(k, src, dst, to):
            return pltpu.make_async_remote_copy(src_ref=src, dst_ref=dst, send_sem=send_sems.at[k],
                                                recv_sem=recv_sems.at[k], device_id=to, device_id_type=MESH)

        first, passed, landed = [], [], []
        for a, (src_ref, g_ref) in enumerate(((wi_ref, gwi_ref), (wo_ref, gwo_ref))):
            for j, (cx, cy) in enumerate(chips):
                slot = g_ref.at[j, c]
                first.append(remote(6 * a + j, src_ref.at[c], slot, (cx, cy, c)))
                landed.append(remote(6 * a + j, slot, slot, sibling))
                passed.append(remote(6 * a + 3 + j, slot, slot, sibling))
        for j, (cx, cy) in enumerate(chips):
            first.append(remote(12 + j, sm_ref, gsm_ref.at[j], (cx, cy, c)))
        for cp in first:
            cp.start()
        for arrived, onward in zip(landed, passed):
            arrived.wait_recv()
            onward.start()
        for a, g_ref in enumerate((gwi_ref, gwo_ref)):
            for j in range(3):
                remote(6 * a + 3 + j, g_ref.at[j, 1 - c], g_ref.at[j, 1 - c], sibling).wait_recv()
        for j in range(3):
            remote(12 + j, sm_ref, gsm_ref.at[j], sibling).wait_recv()
        for cp in first + passed:
            cp.wait_send()

    return pl.pallas_call(
        body, name="gather_weights",
        out_shape=(jax.ShapeDtypeStruct((3,) + wi.shape, wi.dtype), jax.ShapeDtypeStruct((3,) + wo.shape, wo.dtype),
                   jax.ShapeDtypeStruct((3,) + small.shape, small.dtype)),
        in_specs=[ANY, ANY, ANY], out_specs=(ANY, ANY, ANY),
        scratch_shapes=[pltpu.SemaphoreType.DMA((15,)), pltpu.SemaphoreType.DMA((15,))],
    )(wi, wo, small)


def _pair_exchange(gw, gb):
    def body(gw_ref, gb_ref, ra_ref, rb_ref, send_sems, recv_sems):
        x, y, c = _position()
        sibling = (x, y, 1 - c)
        copies = [pltpu.make_async_remote_copy(
            src_ref=gb_ref.at[:, 1 - c], dst_ref=rb_ref, send_sem=send_sems.at[N_CHIPS], recv_sem=recv_sems.at[N_CHIPS],
            device_id=sibling, device_id_type=MESH)]
        for s, start in enumerate(WIN_START):
            rows = pl.ds(pl.multiple_of(start + WIN_HALF * (1 - c), SUBLANE), WIN_HALF)
            copies.append(pltpu.make_async_remote_copy(
                src_ref=gw_ref.at[rows], dst_ref=ra_ref.at[s], send_sem=send_sems.at[s], recv_sem=recv_sems.at[s],
                device_id=sibling, device_id_type=MESH))
        for cp in copies:
            cp.start()
        for cp in copies:
            cp.wait()

    return pl.pallas_call(
        body, name="grad_pair_exchange",
        out_shape=(jax.ShapeDtypeStruct((N_CHIPS, WIN_HALF, D_MODEL), gw.dtype),
                   jax.ShapeDtypeStruct((N_CHIPS,) + gb.shape[2:], gb.dtype)),
        in_specs=[ANY, ANY], out_specs=(ANY, ANY),
        scratch_shapes=[pltpu.SemaphoreType.DMA((N_CHIPS + 1,)), pltpu.SemaphoreType.DMA((N_CHIPS + 1,))],
    )(gw, gb)


def _chip_exchange(pa, pb):
    def body(pa_ref, pb_ref, ra_ref, rb_ref, send_sems, recv_sems):
        x, y, c = _position()
        chips = [(1 - x, y), (x, 1 - y), (1 - x, 1 - y)]
        copies = []
        for a, (src, dst) in enumerate(((pa_ref, ra_ref), (pb_ref, rb_ref))):
            for j, (cx, cy) in enumerate(chips):
                copies.append(pltpu.make_async_remote_copy(
                    src_ref=src.at[2 * cx + cy], dst_ref=dst.at[j], send_sem=send_sems.at[3 * a + j],
                    recv_sem=recv_sems.at[3 * a + j], device_id=(cx, cy, c), device_id_type=MESH))
        for cp in copies:
            cp.start()
        for cp in copies:
            cp.wait()

    return pl.pallas_call(
        body, name="grad_chip_exchange",
        out_shape=(jax.ShapeDtypeStruct((3,) + pa.shape[1:], pa.dtype),
                   jax.ShapeDtypeStruct((3,) + pb.shape[1:], pb.dtype)),
        in_specs=[ANY, ANY], out_specs=(ANY, ANY),
        scratch_shapes=[pltpu.SemaphoreType.DMA((6,)), pltpu.SemaphoreType.DMA((6,))],
    )(pa, pb)


def _pair_share(ha, hb):
    def body(ha_ref, hb_ref, oa_ref, ob_ref, send_sems, recv_sems):
        x, y, c = _position()
        copies = [pltpu.make_async_remote_copy(
            src_ref=src, dst_ref=dst, send_sem=send_sems.at[k], recv_sem=recv_sems.at[k],
            device_id=(x, y, 1 - c), device_id_type=MESH)
            for k, (src, dst) in enumerate(((ha_ref, oa_ref), (hb_ref, ob_ref)))]
        for cp in copies:
            cp.start()
        for cp in copies:
            cp.wait()

    return pl.pallas_call(
        body, name="grad_pair_share",
        out_shape=(jax.ShapeDtypeStruct(ha.shape, ha.dtype), jax.ShapeDtypeStruct(hb.shape, hb.dtype)),
        in_specs=[ANY, ANY], out_specs=(ANY, ANY),
        scratch_shapes=[pltpu.SemaphoreType.DMA((2,)), pltpu.SemaphoreType.DMA((2,))],
    )(ha, hb)


def _gather_small(pack):
    def body(p_ref, o_ref, send_sems, recv_sems):
        x, y, c = _position()
        copies = []
        for mask in range(1, N_DEV):
            peer = (1 - x if mask & 4 else x, 1 - y if mask & 2 else y, 1 - c if mask & 1 else c)
            copies.append(pltpu.make_async_remote_copy(
                src_ref=p_ref, dst_ref=o_ref.at[mask - 1], send_sem=send_sems.at[mask - 1],
                recv_sem=recv_sems.at[mask - 1], device_id=peer, device_id_type=MESH))
        for cp in copies:
            cp.start()
        for cp in copies:
            cp.wait()

    return pl.pallas_call(
        body, name="gather_small",
        out_shape=jax.ShapeDtypeStruct((N_DEV - 1,) + pack.shape, pack.dtype),
        in_specs=[ANY], out_specs=ANY,
        scratch_shapes=[pltpu.SemaphoreType.DMA((N_DEV - 1,)), pltpu.SemaphoreType.DMA((N_DEV - 1,))],
    )(pack)


def _pair_sum(mine, recv, c_idx):
    rows, cols = mine.shape[2:]

    def body(c_ref, a_ref, b_ref, o_ref, send_ref):
        total = a_ref[...] + b_ref[...]
        o_ref[...] = total
        send_ref[...] = total.astype(send_ref.dtype)

    out_spec = pl.BlockSpec((None, rows, cols), lambda s, c_ref: (s, 0, 0))
    return pl.pallas_call(
        body, name="grad_pair_sum",
        grid_spec=pltpu.PrefetchScalarGridSpec(
            num_scalar_prefetch=1, grid=(N_CHIPS,),
            in_specs=[pl.BlockSpec((None, None, rows, cols), lambda s, c_ref: (s, c_ref[0], 0, 0)),
                      pl.BlockSpec((None, rows, cols), lambda s, c_ref: (s, 0, 0))],
            out_specs=(out_spec, out_spec)),
        out_shape=(jax.ShapeDtypeStruct(recv.shape, recv.dtype), jax.ShapeDtypeStruct(recv.shape, WIRE_DTYPE)),
        compiler_params=_params(("parallel",)),
    )(c_idx, mine, recv)


def _window_start(s):
    return jnp.where(s == 0, WIN_START[0], jnp.where(s == 1, WIN_START[1], jnp.where(s == 2, WIN_START[2], WIN_START[3])))


def _pair_sum_windows(gw, recv, c_idx):
    tr = WIN_HALF // 3

    def body(c_ref, a_ref, b_ref, o_ref, send_ref):
        total = a_ref[...] + b_ref[...]
        o_ref[...] = total
        send_ref[...] = total.astype(send_ref.dtype)

    out_spec = pl.BlockSpec((None, tr, D_MODEL), lambda s, i, c_ref: (s, i, 0))
    return pl.pallas_call(
        body, name="grad_pair_sum_windows",
        grid_spec=pltpu.PrefetchScalarGridSpec(
            num_scalar_prefetch=1, grid=(N_CHIPS, WIN_HALF // tr),
            in_specs=[pl.BlockSpec((pl.Element(tr), pl.Element(D_MODEL)),
                                   lambda s, i, c_ref: (pl.multiple_of(
                                       _window_start(s) + WIN_HALF * c_ref[0] + tr * i, SUBLANE), 0)),
                      pl.BlockSpec((None, tr, D_MODEL), lambda s, i, c_ref: (s, i, 0))],
            out_specs=(out_spec, out_spec)),
        out_shape=(jax.ShapeDtypeStruct(recv.shape, recv.dtype), jax.ShapeDtypeStruct(recv.shape, WIRE_DTYPE)),
        compiler_params=_params(("parallel", "parallel")),
    )(c_idx, gw, recv)


def _assemble_w(own, others, starts):
    def body(starts_ref, own_ref, oth_ref, o_ref):
        o_ref[...] = jnp.zeros_like(o_ref)
        for k in range(N_CHIPS):
            rows = pl.ds(pl.multiple_of(starts_ref[k], 2 * SUBLANE), WIN_ROWS)
            o_ref[rows, :] = o_ref[rows, :] + (own_ref[...] if k == 0 else oth_ref[k - 1])

    return pl.pallas_call(
        body, name="assemble_w",
        in_specs=[pl.BlockSpec(memory_space=pltpu.SMEM), pl.BlockSpec(memory_space=pltpu.VMEM),
                  pl.BlockSpec(memory_space=pltpu.VMEM)],
        out_specs=pl.BlockSpec(memory_space=pltpu.VMEM),
        out_shape=jax.ShapeDtypeStruct((D_IN_PAD, D_MODEL), own.dtype),
        compiler_params=_params(),
    )(starts, own, others)


def _chip_sum(psum, recv3, chip_idx):
    rows, cols = psum.shape[1:]
    tr = rows // 2

    def body(s_ref, p_ref, r0, r1, r2, o_ref):
        o_ref[...] = ((p_ref[...] + r0[...].astype(F32)) + r1[...].astype(F32)) + r2[...].astype(F32)

    return pl.pallas_call(
        body, name="grad_chip_sum",
        grid_spec=pltpu.PrefetchScalarGridSpec(
            num_scalar_prefetch=1, grid=(2,),
            in_specs=[pl.BlockSpec((None, tr, cols), lambda i, s_ref: (s_ref[0], i, 0))] +
                     [pl.BlockSpec((None, tr, cols), functools.partial(lambda i, s_ref, j: (j, i, 0), j=j))
                      for j in range(3)],
            out_specs=pl.BlockSpec((tr, cols), lambda i, s_ref: (i, 0))),
        out_shape=jax.ShapeDtypeStruct((rows, cols), psum.dtype),
        compiler_params=_params(("parallel",)),
    )(chip_idx, psum, recv3, recv3, recv3)


def _adamw_math(w, g, m, v):
    m = ADAM_B1 * m + (1.0 - ADAM_B1) * g
    v = ADAM_B2 * v + (1.0 - ADAM_B2) * (g * g)
    m_hat = m / (1.0 - ADAM_B1 ** ADAM_STEP)
    v_hat = v / (1.0 - ADAM_B2 ** ADAM_STEP)
    delta = -ADAM_LR * (m_hat / (jnp.sqrt(v_hat) + ADAM_EPS) + ADAM_WD * w)
    return delta, m, v


def _adamw_big(w, g, m, v, tr):
    rows, cols = w.shape
    assert rows % tr == 0 and g.shape[0] >= rows

    def body(w_ref, g_ref, m_ref, v_ref, d_out, m_out, v_out):
        d, m2, v2 = _adamw_math(w_ref[...], g_ref[...], m_ref[...], v_ref[...])
        d_out[...] = d
        m_out[...] = m2
        v_out[...] = v2

    spec = pl.BlockSpec((tr, cols), lambda i: (i, 0))
    sds = jax.ShapeDtypeStruct((rows, cols), F32)
    return pl.pallas_call(
        body, name="adamw_big", grid=(rows // tr,), in_specs=[spec] * 4, out_specs=(spec,) * 3,
        out_shape=(sds,) * 3, compiler_params=_params(("parallel",)),
    )(w, g, m, v)


def _adamw_rows(w3, g, m3, v3):
    rows, _, cols = w3.shape
    tc = 2 * LANE

    def body(w_ref, g_ref, m_ref, v_ref, g_out, d_out, m_out, v_out):
        g = g_ref[...]
        d, m2, v2 = _adamw_math(w_ref[:, 0, :], g, m_ref[:, 0, :], v_ref[:, 0, :])
        g_out[:, 0, :] = g
        d_out[:, 0, :] = d
        m_out[:, 0, :] = m2
        v_out[:, 0, :] = v2

    spec3 = pl.BlockSpec((rows, 1, tc), lambda i: (0, 0, i))
    sds = jax.ShapeDtypeStruct((rows, 1, cols), F32)
    return pl.pallas_call(
        body, name="adamw_rows", grid=(cols // tc,),
        in_specs=[spec3, pl.BlockSpec((rows, tc), lambda i: (0, i)), spec3, spec3], out_specs=(spec3,) * 4,
        out_shape=(sds,) * 4, compiler_params=_params(("parallel",)),
    )(w3, g, m3, v3)


def _small_update(own, others, params, ms, vs):
    slots = (SLOT_NORM, SLOT_FINAL, SLOT_ATTN, SLOT_CONVG, SLOT_BF, SLOT_META, SLOT_CONVW)
    n = len(slots)

    def body(*refs):
        own_ref, gp_ref = refs[:2]
        refs = refs[1:]
        w_refs, m_refs, v_refs = refs[1:1 + n], refs[1 + n:1 + 2 * n], refs[1 + 2 * n:1 + 3 * n]
        outs = refs[1 + 3 * n:2 + 7 * n]
        loss_ref = outs[0]
        g_outs, d_outs, m_outs, v_outs = (outs[1 + k * n:1 + (k + 1) * n] for k in range(4))
        g_scr, w_scr, m_scr, v_scr = refs[2 + 7 * n:]
        x, y, c = _position()
        shard = 2 * x + y
        me = 4 * x + 2 * y + c
        tot = None
        for d in range(N_DEV):
            rel = jnp.bitwise_xor(me, d)
            term = jnp.where(rel == 0, own_ref[...], gp_ref[jnp.maximum(rel, 1) - 1])
            tot = term if tot is None else tot + term
        r0, r1, _, _ = SLOT_META
        meta_sel = tot[r0:r1, 0:256]
        cw_sel = tot[24:32, 0:128]
        for k in range(1, N_CHIPS):
            meta_sel = jnp.where(shard == k, tot[r0:r1, 256 * k:256 * (k + 1)], meta_sel)
            cw_sel = jnp.where(shard == k, tot[24:32, 128 * k:128 * (k + 1)], cw_sel)
        zeros = jnp.zeros((PACK_ROWS, D_MODEL), F32)
        for scr in (g_scr, w_scr, m_scr, v_scr):
            scr[...] = zeros
        g_scr[0:8, :] = tot[0:8, :]
        g_scr[r0:r1, 0:256] = meta_sel
        g_scr[24:32, 0:128] = cw_sel
        for (a, b, c0, c1), w_ref, m_ref, v_ref in zip(slots, w_refs, m_refs, v_refs):
            w_scr[a:b, c0:c1] = w_ref[...]
            m_scr[a:b, c0:c1] = m_ref[...]
            v_scr[a:b, c0:c1] = v_ref[...]
        loss_ref[...] = g_scr[LOSS_ROW:LOSS_ROW + 1, 0:1]
        d, m2, v2 = _adamw_math(w_scr[...], g_scr[...], m_scr[...], v_scr[...])
        w_scr[...] = d
        m_scr[...] = m2
        v_scr[...] = v2
        for (a, b, c0, c1), g_o, d_o, m_o, v_o in zip(slots, g_outs, d_outs, m_outs, v_outs):
            g_o[...] = g_scr[a:b, c0:c1]
            d_o[...] = w_scr[a:b, c0:c1]
            m_o[...] = m_scr[a:b, c0:c1]
            v_o[...] = v_scr[a:b, c0:c1]

    shapes = [jax.ShapeDtypeStruct(p.shape, F32) for p in params]
    out = pl.pallas_call(
        body, name="small_update",
        out_shape=[jax.ShapeDtypeStruct((1, 1), F32)] + shapes * 4,
        scratch_shapes=[pltpu.VMEM((PACK_ROWS, D_MODEL), F32)] * 4,
        compiler_params=_params(),
    )(own, others, *params, *ms, *vs)
    return out[0], out[1:1 + n], out[1 + n:1 + 2 * n], out[1 + 2 * n:1 + 3 * n], out[1 + 3 * n:1 + 4 * n]


def _in_proj(x2, meta_blk, norm_g, w_pad, bf_pad):
    seq = x2.shape[0]
    lp = seq + FRONT
    t = ROW_TILE
    nt = lp // t
    n_sub = t // LANE

    def body(*refs):
        x_refs = refs[:n_sub]
        mb, g_ref, w_ref, bf_ref = refs[n_sub:n_sub + 4]
        q_ref, k_ref, v_ref, rest_ref, fl_ref, ct_ref, u_ref, carry = refs[n_sub + 4:]
        i = pl.program_id(0)

        @pl.when(i == 0)
        def _():
            carry[...] = jnp.zeros_like(carry)

        first = jnp.where(i == 0, mb[...], x_refs[0][...])
        h = jnp.concatenate([first] + [r[...] for r in x_refs[1:]], axis=0)
        ms = jnp.mean(h * h, axis=-1, keepdims=True)
        u = ((h * lax.rsqrt(ms + EPS)) * g_ref[...]).astype(MXU_DTYPE)
        u_ref[...] = u

        def seg(a, width):
            return _dot_nt(u, w_ref[a:a + width, :])

        q_ref[...] = (seg(SEG_Q, D_ATTN) * (HEAD_DIM ** -0.5)).astype(MXU_DTYPE)
        k_ref[...] = seg(SEG_K, D_ATTN).astype(MXU_DTYPE)
        v_ref[...] = seg(SEG_V, D_ATTN).astype(MXU_DTYPE)
        for s in range(5):
            rest_ref[:, 512 * s:512 * (s + 1)] = seg(SEG_ZA + 512 * s, 512)
        fl = seg(SEG_F, LANE)
        fl_ref[...] = fl
        z = fl + bf_ref[...]
        logf = jnp.minimum(z, 0.0) - jnp.log(1.0 + jnp.exp(-jnp.abs(z)))
        row = i * t + lax.broadcasted_iota(jnp.int32, (t, LANE), 0)
        logf = jnp.where(row >= PAD_ROWS, logf, 0.0)
        tri = jnp.where(lax.broadcasted_iota(jnp.int32, (t, t), 0) >= lax.broadcasted_iota(jnp.int32, (t, t), 1),
                        1.0, 0.0)
        cs = _dot_exact(tri, logf) + carry[...]
        carry[...] = carry[...] + jnp.sum(logf, axis=0, keepdims=True)
        col = i * t + lax.broadcasted_iota(jnp.int32, (SUBLANE, t), 1)
        ct_ref[...] = jnp.where(col >= PAD_ROWS, cs.T[0:SUBLANE, :], -NEG)

    row_blk = lambda cols: pl.BlockSpec((t, cols), lambda i: (i, 0))
    const = lambda shape: pl.BlockSpec(shape, lambda i: (0, 0))
    return pl.pallas_call(
        body, name="in_proj", grid=(nt,),
        in_specs=_x_block_specs(n_sub, LANE) + [const((LANE, D_MODEL)), const((1, D_MODEL)),
                                                pl.BlockSpec((D_IN_PAD, D_MODEL), lambda i: (0, 0),
                                                             pipeline_mode=pl.Buffered(1)),
                                                const((1, LANE))],
        out_specs=(row_blk(D_ATTN), row_blk(D_ATTN), row_blk(D_ATTN), row_blk(5 * 512), row_blk(LANE),
                   pl.BlockSpec((SUBLANE, t), lambda i: (0, i)), row_blk(D_MODEL)),
        out_shape=(jax.ShapeDtypeStruct((lp, D_ATTN), MXU_DTYPE), jax.ShapeDtypeStruct((lp, D_ATTN), MXU_DTYPE),
                   jax.ShapeDtypeStruct((lp, D_ATTN), MXU_DTYPE), jax.ShapeDtypeStruct((lp, 5 * 512), F32),
                   jax.ShapeDtypeStruct((lp, LANE), F32),
                   jax.ShapeDtypeStruct((SUBLANE, lp), F32), jax.ShapeDtypeStruct((lp, D_MODEL), MXU_DTYPE)),
        scratch_shapes=[pltpu.VMEM((1, LANE), F32)],
        compiler_params=_params(("arbitrary",)),
    )(*([x2] * n_sub), meta_blk, norm_g, w_pad, bf_pad)


def _head_masks():
    lane = lax.broadcasted_iota(jnp.int32, (1, LANE), 1)
    return lane < HEAD_DIM, lane >= HEAD_DIM


def _pair_specs(lp, nt, t):
    blk = pl.BlockSpec((lp, LANE), lambda g: (0, g))
    ct_a = pl.BlockSpec((None, nt, 1, t), lambda g: (2 * g, 0, 0, 0))
    ct_b = pl.BlockSpec((None, nt, 1, t), lambda g: (2 * g + 1, 0, 0, 0))
    return blk, ct_a, ct_b


def _sub_rows(s, col):
    return jnp.concatenate([s[:, a * LANE:(a + 1) * LANE] - col for a in range(s.shape[1] // LANE)], axis=1)


def _loop_by_two(lo, hi, step, init):
    def pair(jj, carry):
        return step(lo + 2 * jj + 1, step(lo + 2 * jj, carry))

    pairs = (hi - lo) // 2
    carry = lax.fori_loop(0, pairs, pair, init)
    return lax.fori_loop(lo + 2 * pairs, hi, step, carry)


def _lane_chunks(s):
    return [s[:, a * LANE:(a + 1) * LANE] for a in range(s.shape[1] // LANE)]


def _attn_fwd(q, k, v, ct4):
    lp = q.shape[0]
    t = ROW_TILE
    nt = lp // t

    def body(q_ref, k_ref, v_ref, cta_ref, ctb_ref, o_ref, l_ref, m_ref):
        masks = _head_masks()
        ct_refs = (cta_ref, ctb_ref)
        below = lax.broadcasted_iota(jnp.int32, (t, t), 1) <= lax.broadcasted_iota(jnp.int32, (t, t), 0)
        lane = lax.broadcasted_iota(jnp.int32, (1, LANE), 1)
        head_of_row = lax.broadcasted_iota(jnp.int32, (2 * t, LANE), 0) >= t
        ones_cols = jnp.where(lax.broadcasted_iota(jnp.int32, (2 * t, LANE), 1) == head_of_row.astype(jnp.int32),
                              1.0, 0.0).astype(MXU_DTYPE)

        def q_block(i, _):
            r0 = pl.multiple_of(i * t, t)
            qi = q_ref[pl.ds(r0, t), :]

            def scores(j):
                kj = k_ref[pl.ds(pl.multiple_of(j * t, t), t), :]
                return _dot_nt(qi, jnp.concatenate([jnp.where(hm, kj, 0).astype(MXU_DTYPE) for hm in masks], axis=0))

            def biased(j, hh, s2, diagonal):
                s = s2[:, hh * t:(hh + 1) * t] - ct_refs[hh][j]
                return jnp.where(below, s, NEG) if diagonal else s

            def max_step(j, carry, diagonal):
                s2 = scores(j)
                out = []
                for hh, m in enumerate(carry):
                    for c in _lane_chunks(biased(j, hh, s2, diagonal)):
                        m = jnp.maximum(m, c)
                    out.append(m)
                return tuple(out)

            lanes_neg = jnp.full((t, LANE), NEG, F32)
            carry = _loop_by_two(0, i, functools.partial(max_step, diagonal=False), (lanes_neg, lanes_neg))
            ms = [jnp.max(m, axis=-1, keepdims=True) for m in max_step(i, carry, True)]

            def sum_step(j, acc, diagonal):
                s2 = scores(j)
                vj = v_ref[pl.ds(pl.multiple_of(j * t, t), t), :]
                v2 = jnp.concatenate([jnp.where(hm, vj, 0).astype(MXU_DTYPE) for hm in masks], axis=0)
                parts = [jnp.exp(biased(j, hh, s2, diagonal) - ms[hh]).astype(MXU_DTYPE) for hh in range(2)]
                return acc + _dot(jnp.concatenate(parts, axis=1), jnp.concatenate([v2, ones_cols], axis=1))

            acc = _loop_by_two(0, i, functools.partial(sum_step, diagonal=False), jnp.zeros((t, 2 * LANE), F32))
            acc = sum_step(i, acc, True)
            sums = acc[:, LANE:]
            l_pair = jnp.where(masks[0], jnp.sum(jnp.where(lane == 0, sums, 0.0), axis=-1, keepdims=True),
                               jnp.sum(jnp.where(lane == 1, sums, 0.0), axis=-1, keepdims=True))
            o_ref[pl.ds(r0, t), :] = acc[:, :LANE] / l_pair
            l_ref[pl.ds(r0, t), :] = l_pair
            m_ref[pl.ds(r0, t), 0:LANE] = jnp.broadcast_to(ms[0], (t, LANE))
            m_ref[pl.ds(r0, t), LANE:2 * LANE] = jnp.broadcast_to(ms[1], (t, LANE))
            return 0

        lax.fori_loop(0, nt, q_block, 0)

    blk, ct_a, ct_b = _pair_specs(lp, nt, t)
    return pl.pallas_call(
        body, name="attn_fwd", grid=(HEADS // 2,),
        in_specs=[blk, blk, blk, ct_a, ct_b], out_specs=(blk, blk, pl.BlockSpec((lp, 2 * LANE), lambda g: (0, g))),
        out_shape=(jax.ShapeDtypeStruct((lp, D_ATTN), F32), jax.ShapeDtypeStruct((lp, D_ATTN), F32),
                   jax.ShapeDtypeStruct((lp, HEADS * LANE), F32)),
        compiler_params=_params(("parallel",)),
    )(q, k, v, ct4, ct4)


def _attn_bwd(q, k, v, do, m, delta, ct4):
    lp = q.shape[0]
    t = ROW_TILE
    nt = lp // t

    def body(q_ref, k_ref, v_ref, do_ref, ma_ref, mb_ref, dla_ref, dlb_ref, cta_ref, ctb_ref,
             dq_ref, dk_ref, dv_ref, dc_ref, dq_acc, dk_acc, dv_acc):
        masks = _head_masks()
        ct_refs, m_refs, dl_refs = (cta_ref, ctb_ref), (ma_ref, mb_ref), (dla_ref, dlb_ref)
        below = lax.broadcasted_iota(jnp.int32, (t, t), 1) <= lax.broadcasted_iota(jnp.int32, (t, t), 0)
        tn = (((0,), (0,)), ((), ()))
        dq_acc[...] = jnp.zeros_like(dq_acc)

        def k_block(j, _):
            c0 = pl.multiple_of(j * t, t)
            kj = k_ref[pl.ds(c0, t), :]
            vj = v_ref[pl.ds(c0, t), :]
            k2 = jnp.concatenate([jnp.where(hm, kj, 0).astype(MXU_DTYPE) for hm in masks], axis=0)
            v2 = jnp.concatenate([jnp.where(hm, vj, 0).astype(MXU_DTYPE) for hm in masks], axis=0)
            ck = [r[j] for r in ct_refs]
            dk_acc[...] = jnp.zeros_like(dk_acc)
            dv_acc[...] = jnp.zeros_like(dv_acc)

            def q_block(i, colsums, diagonal):
                r0 = pl.multiple_of(i * t, t)
                qi = q_ref[pl.ds(r0, t), :]
                doi = do_ref[pl.ds(r0, t), :]
                q2 = jnp.concatenate([jnp.where(hm, qi, 0).astype(MXU_DTYPE) for hm in masks], axis=0)
                do2 = jnp.concatenate([jnp.where(hm, doi, 0).astype(MXU_DTYPE) for hm in masks], axis=0)
                s2 = _dot_nt(qi, k2)
                dp2 = _dot_nt(doi, v2)
                out, ps, dss = [], [], []
                for hh in range(2):
                    s = s2[:, hh * t:(hh + 1) * t] - ck[hh]
                    if diagonal:
                        s = jnp.where(below, s, NEG)
                    p = jnp.exp(_sub_rows(s, m_refs[hh][pl.ds(r0, t), :])).astype(MXU_DTYPE)
                    ds32 = p.astype(F32) * _sub_rows(dp2[:, hh * t:(hh + 1) * t], dl_refs[hh][pl.ds(r0, t), :])
                    ps.append(p)
                    dss.append(ds32.astype(MXU_DTYPE))
                    out.append(colsums[hh] + jnp.sum(ds32, axis=0, keepdims=True))
                dv_acc[...] = dv_acc[...] + lax.dot_general(jnp.concatenate(ps, axis=0), do2, tn,
                                                            preferred_element_type=F32)
                dk_acc[...] = dk_acc[...] + lax.dot_general(jnp.concatenate(dss, axis=0), q2, tn,
                                                            preferred_element_type=F32)
                dq_acc[pl.ds(r0, t), :] = dq_acc[pl.ds(r0, t), :] + _dot(jnp.concatenate(dss, axis=1), k2)
                return tuple(out)

            colsums = q_block(j, (jnp.zeros((1, t), F32), jnp.zeros((1, t), F32)), True)
            colsums = lax.fori_loop(j + 1, nt, functools.partial(q_block, diagonal=False), colsums)
            for hh in range(2):
                dc_ref[hh, j] = -colsums[hh]
            dk_ref[pl.ds(c0, t), :] = dk_acc[...].astype(dk_ref.dtype)
            dv_ref[pl.ds(c0, t), :] = dv_acc[...].astype(dv_ref.dtype)
            return 0

        lax.fori_loop(0, nt, k_block, 0)
        dq_ref[...] = (dq_acc[...] * (HEAD_DIM ** -0.5)).astype(dq_ref.dtype)

    blk, ct_a, ct_b = _pair_specs(lp, nt, t)
    rep_a = pl.BlockSpec((lp, LANE), lambda g: (0, 2 * g))
    rep_b = pl.BlockSpec((lp, LANE), lambda g: (0, 2 * g + 1))
    return pl.pallas_call(
        body, name="attn_bwd", grid=(HEADS // 2,),
        in_specs=[blk] * 4 + [rep_a, rep_b, rep_a, rep_b, ct_a, ct_b],
        out_specs=(blk, blk, blk, pl.BlockSpec((2, nt, 1, t), lambda g: (g, 0, 0, 0))),
        out_shape=(jax.ShapeDtypeStruct((lp, D_ATTN), MXU_DTYPE),) * 3
                  + (jax.ShapeDtypeStruct((HEADS, nt, 1, t), F32),),
        scratch_shapes=[pltpu.VMEM((lp, LANE), F32), pltpu.VMEM((t, LANE), F32), pltpu.VMEM((t, LANE), F32)],
        compiler_params=_params(("parallel",)),
    )(q, k, v, do, m, m, delta, delta, ct4, ct4)


def _shift_down(prev8, cur, k):
    ext = jnp.concatenate([prev8, cur], axis=0)
    return pltpu.roll(ext, k, 0)[SUBLANE:, :]


def _shift_up(cur, next8, k):
    ext = jnp.concatenate([cur, next8], axis=0)
    n = ext.shape[0]
    return pltpu.roll(ext, n - k, 0)[:cur.shape[0], :]


def _post(o, l_sum, rest, x2, meta_blk, tgt2, w_out, attn_g, conv_g, final_g, conv_w8):
    lp = o.shape[0]
    t = ROW_TILE
    nt = lp // t
    n_sub = t // LANE
    hb = t // SUBLANE

    def body(*refs):
        o_ref, l_ref, za_ref, gb_ref, gc_ref, xc_ref, zc_ref, gch_ref, xch_ref = refs[:9]
        refs = refs[1:]
        x_refs = refs[8:8 + n_sub]
        mb = refs[8 + n_sub]
        t_refs = refs[9 + n_sub:9 + 2 * n_sub]
        wo_ref, ag_ref, cg_ref, fg_ref, cw_ref = refs[9 + 2 * n_sub:14 + 2 * n_sub]
        (dout_ref, do_ref, dl_ref, dza_ref, dgb_ref, dzc_ref, dcv_ref,
         loss_ref, gf_ref, gag_ref, gcg_ref, gwo_ref) = refs[14 + 2 * n_sub:]
        i = pl.program_id(0)

        @pl.when(i == 0)
        def _():
            for r in (loss_ref, gf_ref, gag_ref, gcg_ref, gwo_ref):
                r[...] = jnp.zeros_like(r)

        gmat = _group_matrix()
        inv_g = 1.0 / HEAD_DIM
        o_v = o_ref[...]
        ra = lax.rsqrt(_group_sum(o_v * o_v, gmat) * inv_g + EPS)
        n_a = o_v * ra
        a_n = n_a * ag_ref[...]
        za = za_ref[...]
        sig_a = _sigmoid(za)
        sz_a = za * sig_a
        y_a = a_n * sz_a
        gb = gb_ref[...]
        gc = gc_ref[...]
        xc = xc_ref[...]
        cx = gc * xc
        cx_prev = jnp.where(i == 0, 0.0, gch_ref[...] * xch_ref[...])
        conv = (cw_ref[0:1, :] * _shift_down(cx_prev, cx, 2) + cw_ref[1:2, :] * _shift_down(cx_prev, cx, 1)
                + cw_ref[2:3, :] * cx)
        e = gb * conv
        re = lax.rsqrt(_group_sum(e * e, gmat) * inv_g + EPS)
        n_e = e * re
        e_n = n_e * cg_ref[...]
        zc = zc_ref[...]
        sig_c = _sigmoid(zc)
        sz_c = zc * sig_c
        y_c = e_n * sz_c
        mix = jnp.concatenate([y_a, y_c], axis=-1)
        mix_b = mix.astype(MXU_DTYPE)
        first = jnp.where(i == 0, mb[...], x_refs[0][...])
        h = jnp.concatenate([first] + [r[...] for r in x_refs[1:]], axis=0)
        out = h + _dot(mix_b, wo_ref[...])
        r2 = lax.rsqrt(jnp.mean(out * out, axis=-1, keepdims=True) + EPS)
        n_f = out * r2
        y = n_f * fg_ref[...]
        tgt = jnp.concatenate([r[...] for r in t_refs], axis=0)
        valid = (i * t + lax.broadcasted_iota(jnp.int32, (t, 1), 0)) >= FRONT
        diff = jnp.where(valid, y - tgt, 0.0)
        loss_ref[...] = loss_ref[...] + 0.5 * jnp.sum(jnp.sum(diff * diff, axis=-1, keepdims=True) * (1.0 / D_MODEL))
        dy = diff * (1.0 / D_MODEL)
        gf_ref[...] = gf_ref[...] + jnp.sum(dy * n_f, axis=0, keepdims=True)
        dn = dy * fg_ref[...]
        d_out = r2 * (dn - n_f * jnp.mean(dn * n_f, axis=-1, keepdims=True))
        dout_ref[...] = d_out
        d_out_b = d_out.astype(MXU_DTYPE)
        d_mix = _dot_nt(d_out_b, wo_ref[...])
        gwo_ref[...] = gwo_ref[...] + _dot(mix.T.astype(MXU_DTYPE), d_out_b)
        d_ya = d_mix[:, :D_ATTN]
        d_yc = d_mix[:, D_ATTN:]
        d_an = d_ya * sz_a
        dza_ref[...] = (d_ya * a_n * (sig_a * (1.0 + za * (1.0 - sig_a)))).astype(dza_ref.dtype)
        gag_ref[...] = gag_ref[...] + jnp.sum(d_an * n_a, axis=0, keepdims=True)
        dn_a = d_an * ag_ref[...]
        d_o = ra * (dn_a - n_a * (_group_sum(dn_a * n_a, gmat) * inv_g))
        d_o_b = (d_o / l_ref[...]).astype(do_ref.dtype)
        do_ref[...] = d_o_b
        head_rep = jnp.where((lax.broadcasted_iota(jnp.int32, (D_ATTN, HEADS * LANE), 0) >> 6)
                             == (lax.broadcasted_iota(jnp.int32, (D_ATTN, HEADS * LANE), 1) >> 7), 1.0, 0.0)
        dl_ref[...] = _group_sum(d_o_b.astype(F32) * o_v, head_rep.astype(MXU_DTYPE))
        d_en = d_yc * sz_c
        dzc_ref[...] = (d_yc * e_n * (sig_c * (1.0 + zc * (1.0 - sig_c)))).astype(dzc_ref.dtype)
        gcg_ref[...] = gcg_ref[...] + jnp.sum(d_en * n_e, axis=0, keepdims=True)
        dn_e = d_en * cg_ref[...]
        d_e = re * (dn_e - n_e * (_group_sum(dn_e * n_e, gmat) * inv_g))
        dgb_ref[...] = (d_e * conv).astype(dgb_ref.dtype)
        dcv_ref[...] = d_e * gb

    row_blk = lambda cols: pl.BlockSpec((t, cols), lambda i: (i, 0))
    rest_blk = lambda s: pl.BlockSpec((t, 512), functools.partial(lambda i, s: (i, s), s=s))
    halo = lambda s: pl.BlockSpec((SUBLANE, 512), functools.partial(lambda i, s: (jnp.maximum(i * hb - 1, 0), s), s=s))
    const = lambda shape: pl.BlockSpec(shape, lambda i: (0, 0))
    acc = lambda shape: pl.BlockSpec(shape, lambda i: (0, 0))
    return pl.pallas_call(
        body, name="post_fwd_bwd", grid=(nt,),
        in_specs=[row_blk(D_ATTN), row_blk(D_ATTN)] + [rest_blk(s) for s in range(5)] + [halo(2), halo(3)]
                 + _x_block_specs(n_sub, LANE) + [const((LANE, D_MODEL))] + _x_block_specs(n_sub, LANE)
                 + [const((D_MODEL, D_MODEL)), const((1, D_ATTN)), const((1, D_CONV)), const((1, D_MODEL)),
                    const((SUBLANE, D_CONV))],
        out_specs=(row_blk(D_MODEL), row_blk(D_ATTN), row_blk(HEADS * LANE), row_blk(D_ATTN), row_blk(D_CONV),
                   row_blk(D_CONV), row_blk(D_CONV),
                   acc((1, LANE)), acc((1, D_MODEL)), acc((1, D_ATTN)), acc((1, D_CONV)), acc((D_MODEL, D_MODEL))),
        out_shape=(jax.ShapeDtypeStruct((lp, D_MODEL), F32), jax.ShapeDtypeStruct((lp, D_ATTN), MXU_DTYPE),
                   jax.ShapeDtypeStruct((lp, HEADS * LANE), F32), jax.ShapeDtypeStruct((lp, D_ATTN), MXU_DTYPE),
                   jax.ShapeDtypeStruct((lp, D_CONV), MXU_DTYPE), jax.ShapeDtypeStruct((lp, D_CONV), MXU_DTYPE),
                   jax.ShapeDtypeStruct((lp, D_CONV), F32),
                   jax.ShapeDtypeStruct((1, LANE), F32), jax.ShapeDtypeStruct((1, D_MODEL), F32),
                   jax.ShapeDtypeStruct((1, D_ATTN), F32), jax.ShapeDtypeStruct((1, D_CONV), F32),
                   jax.ShapeDtypeStruct((D_MODEL, D_MODEL), F32)),
        compiler_params=_params(("arbitrary",)),
    )(o, l_sum, *([rest] * 5), rest, rest, *([x2] * n_sub), meta_blk, *([tgt2] * n_sub),
      w_out, attn_g, conv_g, final_g, conv_w8)


def _bwd_in(x2, meta_blk, norm_g, w_pad, bf_pad, fl, dc, dq, dk, dv, dza, dgb, dzc, dconv, rest, d_out, conv_w8):
    lp = fl.shape[0]
    t = ROW_TILE
    nt = lp // t
    n_sub = t // LANE
    hb = t // SUBLANE
    rev = lambda i: nt - 1 - i

    def body(*refs):
        x_refs = refs[:n_sub]
        (mb, g_ref, w_ref, bf_ref, fl_ref, dc_ref, dq_ref, dk_ref, dv_ref, dza_ref, dgb_ref, dzc_ref,
         dcv_ref, dcvn_ref, gc_ref, xc_ref, gch_ref, xch_ref, dout_ref, cw_ref) = refs[n_sub:n_sub + 20]
        dp_ref, dh_ref, gn_ref, gbf_ref, gcw_ref, carry = refs[n_sub + 20:]
        step = pl.program_id(0)
        i = rev(step)

        @pl.when(step == 0)
        def _():
            for r in (gn_ref, gbf_ref, gcw_ref, carry):
                r[...] = jnp.zeros_like(r)

        dc8 = jnp.concatenate([dc_ref[...], jnp.zeros((LANE - HEADS, t), F32)], axis=0).T
        triu = jnp.where(lax.broadcasted_iota(jnp.int32, (t, t), 1) >= lax.broadcasted_iota(jnp.int32, (t, t), 0),
                         1.0, 0.0)
        dlogf = _dot_exact(triu, dc8) + carry[...]
        carry[...] = carry[...] + jnp.sum(dc8, axis=0, keepdims=True)
        z = fl_ref[...] + bf_ref[...]
        row = i * t + lax.broadcasted_iota(jnp.int32, (t, LANE), 0)
        d_f = jnp.where(row >= PAD_ROWS, dlogf * (1.0 / (1.0 + jnp.exp(z))), 0.0)
        gbf_ref[...] = gbf_ref[...] + jnp.sum(d_f, axis=0, keepdims=True)
        dcv = dcv_ref[...]
        dcv_next = jnp.where(i == nt - 1, 0.0, dcvn_ref[...])
        d_cx = (cw_ref[2:3, :] * dcv + cw_ref[1:2, :] * _shift_up(dcv, dcv_next, 1)
                + cw_ref[0:1, :] * _shift_up(dcv, dcv_next, 2))
        gc = gc_ref[...]
        xc = xc_ref[...]
        cx = gc * xc
        cx_prev = jnp.where(i == 0, 0.0, gch_ref[...] * xch_ref[...])
        rowi = lax.broadcasted_iota(jnp.int32, (SUBLANE, 1), 0)
        gcw = (jnp.where(rowi == 0, jnp.sum(dcv * _shift_down(cx_prev, cx, 2), axis=0, keepdims=True), 0.0)
               + jnp.where(rowi == 1, jnp.sum(dcv * _shift_down(cx_prev, cx, 1), axis=0, keepdims=True), 0.0)
               + jnp.where(rowi == 2, jnp.sum(dcv * cx, axis=0, keepdims=True), 0.0))
        gcw_ref[...] = gcw_ref[...] + gcw
        dp_ref[:, SEG_Q:SEG_Q + 512] = dq_ref[...]
        dp_ref[:, SEG_K:SEG_K + 512] = dk_ref[...]
        dp_ref[:, SEG_V:SEG_V + 512] = dv_ref[...]
        dp_ref[:, SEG_F:SEG_F + LANE] = d_f.astype(dp_ref.dtype)
        dp_ref[:, SEG_ZA:SEG_ZA + 512] = dza_ref[...]
        dp_ref[:, SEG_GB:SEG_GB + 512] = dgb_ref[...]
        dp_ref[:, SEG_GC:SEG_GC + 512] = (d_cx * xc).astype(dp_ref.dtype)
        dp_ref[:, SEG_XC:SEG_XC + 512] = (d_cx * gc).astype(dp_ref.dtype)
        dp_ref[:, SEG_ZC:SEG_ZC + 512] = dzc_ref[...]
        d_u = _dot(dp_ref[...], w_ref[...])
        first = jnp.where(i == 0, mb[...], x_refs[0][...])
        h = jnp.concatenate([first] + [r[...] for r in x_refs[1:]], axis=0)
        r1 = lax.rsqrt(jnp.mean(h * h, axis=-1, keepdims=True) + EPS)
        n_h = h * r1
        gn_ref[...] = gn_ref[...] + jnp.sum(d_u * n_h, axis=0, keepdims=True)
        dn = d_u * g_ref[...]
        dh_ref[...] = dout_ref[...] + r1 * (dn - n_h * jnp.mean(dn * n_h, axis=-1, keepdims=True))

    def x_specs():
        specs = [pl.BlockSpec((LANE, D_MODEL), lambda s: (jnp.maximum(n_sub * rev(s) - 1, 0), 0))]
        for b in range(1, n_sub):
            specs.append(pl.BlockSpec((LANE, D_MODEL), functools.partial(lambda s, b: (n_sub * rev(s) - 1 + b, 0), b=b)))
        return specs

    row_blk = lambda cols: pl.BlockSpec((t, cols), lambda s: (rev(s), 0))
    rest_blk = lambda k: pl.BlockSpec((t, 512), functools.partial(lambda s, k: (rev(s), k), k=k))
    halo_prev = lambda k: pl.BlockSpec(
        (SUBLANE, 512), functools.partial(lambda s, k: (jnp.maximum(rev(s) * hb - 1, 0), k), k=k))
    halo_next = pl.BlockSpec((SUBLANE, 512), lambda s: (jnp.minimum((rev(s) + 1) * hb, lp // SUBLANE - 1), 0))
    const = lambda shape: pl.BlockSpec(shape, lambda s: (0, 0))
    return pl.pallas_call(
        body, name="bwd_in", grid=(nt,),
        in_specs=x_specs() + [const((LANE, D_MODEL)), const((1, D_MODEL)),
                              pl.BlockSpec((D_IN_PAD, D_MODEL), lambda s: (0, 0), pipeline_mode=pl.Buffered(1)),
                              const((1, LANE)), row_blk(LANE),
                              pl.BlockSpec((HEADS, t), lambda s: (0, rev(s))),
                              row_blk(512), row_blk(512), row_blk(512), row_blk(512), row_blk(512), row_blk(512),
                              row_blk(512), halo_next, rest_blk(2), rest_blk(3), halo_prev(2), halo_prev(3),
                              row_blk(D_MODEL), const((SUBLANE, D_CONV))],
        out_specs=(row_blk(D_IN_PAD), row_blk(D_MODEL), const((1, D_MODEL)), const((1, LANE)),
                   const((SUBLANE, D_CONV))),
        out_shape=(jax.ShapeDtypeStruct((lp, D_IN_PAD), MXU_DTYPE), jax.ShapeDtypeStruct((lp, D_MODEL), F32),
                   jax.ShapeDtypeStruct((1, D_MODEL), F32), jax.ShapeDtypeStruct((1, LANE), F32),
                   jax.ShapeDtypeStruct((SUBLANE, D_CONV), F32)),
        scratch_shapes=[pltpu.VMEM((1, LANE), F32)],
        compiler_params=_params(("arbitrary",)),
    )(*([x2] * n_sub), meta_blk, norm_g, w_pad, bf_pad, fl, dc, dq, dk, dv, dza, dgb, dzc, dconv, dconv,
      rest, rest, rest, rest, d_out, conv_w8)


def _grad_w_in(u, dproj):
    lp = u.shape[0]
    tk = ROW_TILE
    tn = GW_COL_TILE

    def body(d_ref, u_ref, o_ref):
        @pl.when(pl.program_id(1) == 0)
        def _():
            o_ref[...] = jnp.zeros_like(o_ref)

        o_ref[...] = o_ref[...] + lax.dot_general(d_ref[...], u_ref[...], (((0,), (0,)), ((), ())),
                                                  preferred_element_type=F32)

    return pl.pallas_call(
        body, name="grad_w_in", grid=(D_IN_PAD // tn, lp // tk),
        in_specs=[pl.BlockSpec((tk, tn), lambda n, k: (k, n)), pl.BlockSpec((tk, D_MODEL), lambda n, k: (k, 0))],
        out_specs=pl.BlockSpec((tn, D_MODEL), lambda n, k: (n, 0)),
        out_shape=jax.ShapeDtypeStruct((D_IN_PAD, D_MODEL), F32),
        compiler_params=_params(("parallel", "arbitrary")),
    )(dproj, u)


def _by_chip(own, others, me):
    by_mask = jnp.stack([own, others[1], others[0], others[2]])
    return [lax.dynamic_index_in_dim(by_mask, jnp.bitwise_xor(me, s), 0, keepdims=False) for s in range(N_CHIPS)]


def _both_halves(mine, other, c):
    return jnp.where(c == 0, jnp.concatenate([mine, other], axis=0), jnp.concatenate([other, mine], axis=0))


def _local_step(x2, tgt2, meta_full, norm_g, w_pad, b_f, conv_w_full, attn_g, conv_g, w_out_full, final_g):
    lp = x2.shape[0] + FRONT
    nt = lp // ROW_TILE
    meta_blk = jnp.concatenate([jnp.zeros((PAD_ROWS, D_MODEL), F32), meta_full], axis=0)
    bf_pad = jnp.pad(b_f, ((0, 0), (0, LANE - HEADS)))
    conv_w8 = jnp.pad(conv_w_full, ((0, SUBLANE - conv_w_full.shape[0]), (0, 0)))
    q, k, v, rest, fl, ct, u = _in_proj(x2, meta_blk, norm_g, w_pad, bf_pad)
    ct4 = ct.reshape(SUBLANE, nt, 1, ROW_TILE)
    o, l_sum, m_max = _attn_fwd(q, k, v, ct4)
    (d_out, d_o, delta, dza, dgb, dzc, dconv, loss, g_final, g_attn, g_convg, gw_out) = _post(
        o, l_sum, rest, x2, meta_blk, tgt2, w_out_full, attn_g, conv_g, final_g, conv_w8)
    dq, dk, dv, dc = _attn_bwd(q, k, v, d_o, m_max, delta, ct4)
    dproj, d_h, g_norm, g_bf, g_cw = _bwd_in(x2, meta_blk, norm_g, w_pad, bf_pad, fl, dc.reshape(HEADS, lp), dq, dk, dv,
                                             dza, dgb, dzc, dconv, rest, d_out, conv_w8)
    gw_in = _grad_w_in(u, dproj)
    return dict(loss=loss, d_h=d_h, g_norm=g_norm, g_final=g_final, g_attn=g_attn, g_convg=g_convg, g_bf=g_bf,
                g_cw=g_cw, gw_out=gw_out, gw_in=gw_in)


def kernel(x, meta, norm_g, w_in, b_f, conv_w, attn_norm_g, conv_norm_g, w_out, final_norm_g, loss_target, m_meta, m_norm_g, m_w_in, m_b_f, m_conv_w, m_attn_norm_g, m_conv_norm_g, m_w_out, m_final_norm_g, v_meta, v_norm_g, v_w_in, v_b_f, v_conv_w, v_attn_norm_g, v_conv_norm_g, v_w_out, v_final_norm_g):
    cx_, cy_, cc_ = _position()
    chip = 2 * cx_ + cy_
    shard = w_in.shape[2]
    out_half = w_out.shape[1] // 2
    pick = lambda vals: jnp.where(chip == 0, vals[0], jnp.where(chip == 1, vals[1], jnp.where(chip == 2, vals[2], vals[3])))
    a_off, b_off = pick(A_OFF), pick(B_OFF)
    wt = jnp.transpose(w_in[0]).astype(MXU_DTYPE)
    wi = lax.dynamic_update_slice_in_dim(
        lax.dynamic_update_slice_in_dim(jnp.zeros((WIN_ROWS, D_MODEL), MXU_DTYPE), wt[:PIECE_A], a_off, 0),
        wt[PIECE_A:], b_off, 0)
    wo = w_out[0].astype(MXU_DTYPE)
    small = jnp.concatenate([meta, jnp.pad(conv_w[0], ((0, 8 - conv_w.shape[1]), (0, meta.shape[1] - conv_w.shape[2])))],
                            axis=0)
    gwi, gwo, gsm = _gather_weights(wi.reshape(2, WIN_HALF, D_MODEL), wo.reshape(2, out_half, D_MODEL), small)
    starts = jnp.stack([_window_start(jnp.bitwise_xor(chip, mask)) for mask in (0, 2, 1, 3)]).astype(jnp.int32)
    w_pad = _assemble_w(wi, gwi.reshape(3, WIN_ROWS, D_MODEL), starts)
    w_out_full = jnp.concatenate(_by_chip(wo, gwo.reshape(3, 2 * out_half, D_MODEL), chip), axis=0)
    small_full = jnp.concatenate(_by_chip(small, gsm, chip), axis=1)
    meta_full = small_full[:N_META]
    conv_w_full = jnp.concatenate([small_full[N_META:N_META + 3, 256 * s:256 * s + LANE] for s in range(N_CHIPS)], axis=1)
    final_g2 = final_norm_g.reshape(1, D_MODEL)
    r = _local_step(x[0], loss_target[0], meta_full, norm_g, w_pad, b_f, conv_w_full, attn_norm_g, conv_norm_g,
                    w_out_full, final_g2)
    grad_x = r["d_h"][FRONT:][None]
    gb = r["gw_out"].reshape(N_CHIPS, 2, out_half, D_MODEL)
    ra, rb = _pair_exchange(r["gw_in"], gb)
    c_idx = jnp.reshape(cc_, (1,)).astype(jnp.int32)
    chip_idx = jnp.reshape(chip, (1,)).astype(jnp.int32)
    pa, pa_wire = _pair_sum_windows(r["gw_in"], ra, c_idx)
    pb, pb_wire = _pair_sum(gb, rb, c_idx)
    xa, xb = _chip_exchange(pa_wire, pb_wire)
    ha = _chip_sum(pa, xa, chip_idx)
    hb = _chip_sum(pb, xb, chip_idx)
    oa, ob = _pair_share(ha, hb)
    g_window = _both_halves(ha, oa, cc_)
    g_w_in_t = jnp.concatenate([lax.dynamic_slice_in_dim(g_window, a_off, PIECE_A, 0),
                                lax.dynamic_slice_in_dim(g_window, b_off, shard - PIECE_A, 0)], axis=0)
    g_w_out = _both_halves(hb, ob, cc_)
    as_rows = lambda a: jnp.transpose(a, (2, 0, 1))
    g_w_in, d_w_in, nm_w_in, nv_w_in = (jnp.transpose(a, (1, 2, 0)) for a in _adamw_rows(
        as_rows(w_in), g_w_in_t, as_rows(m_w_in), as_rows(v_w_in)))
    d_w_out, nm_w_out, nv_w_out = (a[None] for a in _adamw_big(w_out[0], g_w_out, m_w_out[0], v_w_out[0], LANE))
    wide = lambda a: jnp.pad(a, ((0, 0), (0, D_MODEL - a.shape[1])))
    pack = jnp.concatenate([
        r["g_norm"], r["g_final"], jnp.concatenate([r["g_attn"], r["g_convg"]], axis=1), wide(r["g_bf"]),
        wide(r["loss"]), jnp.zeros((3, D_MODEL), F32), r["d_h"][PAD_ROWS:FRONT], wide(r["g_cw"])], axis=0)
    params = (norm_g, final_g2, attn_norm_g, conv_norm_g, b_f, meta, conv_w[0])
    ms = (m_norm_g, m_final_norm_g.reshape(1, D_MODEL), m_attn_norm_g, m_conv_norm_g, m_b_f, m_meta, m_conv_w[0])
    vs = (v_norm_g, v_final_norm_g.reshape(1, D_MODEL), v_attn_norm_g, v_conv_norm_g, v_b_f, v_meta, v_conv_w[0])
    loss, g_s, d_s, m_s, v_s = _small_update(pack, _gather_small(pack), params, ms, vs)

    def ordered(small_list, big_in, big_out):
        s_norm, s_final, s_attn, s_convg, s_bf, s_meta, s_cw = small_list
        return (s_meta, s_norm, big_in, s_bf, s_cw[None], s_attn, s_convg, big_out, s_final.reshape(D_MODEL))

    return (loss.reshape(()), grad_x,
            *ordered(g_s, g_w_in, g_w_out[None]), *ordered(d_s, d_w_in, d_w_out),
            *ordered(m_s, nm_w_in, nm_w_out), *ordered(v_s, nv_w_in, nv_w_out))
```

```python
import functools

import jax
import jax.numpy as jnp
from jax import lax
from jax.experimental import pallas as pl
from jax.experimental.pallas import tpu as pltpu

F32 = jnp.float32
MXU_DTYPE = jnp.bfloat16
WIRE_DTYPE = jnp.bfloat16

D_MODEL = 1024
N_META = 16
HEADS = 8
HEAD_DIM = 64
D_ATTN = HEADS * HEAD_DIM
D_CONV = 512
EPS = 1e-6
LANE = 128
SUBLANE = 8
ROW_TILE = 384
ATTN_UNROLL = 3
FRONT = LANE
PAD_ROWS = FRONT - N_META
NEG = -1e30
LOG2E = 1.4426950408889634
N_CHIPS = 4
N_DEV = 8
VMEM_LIMIT_BYTES = 60 * 1024 * 1024

SEG_Q, SEG_K, SEG_V, SEG_F, SEG_ZA, SEG_GB, SEG_GC, SEG_XC, SEG_ZC = (
    0, 512, 1024, 1536, 1664, 2176, 2688, 3200, 3712)
D_IN = 4104
D_IN_PAD = 4224
F_END = 1544
GW_COL_TILE = 1408
WIN_ROWS = 1152
WIN_HALF = WIN_ROWS // 2
WIN_START = (0, 1024, 2160, 3072)
PIECE_A = 518
A_OFF = (0, 2, 12, 126)
B_OFF = (518, 640, 530, 644)
ADAM_LR = 0.001
ADAM_B1 = 0.9
ADAM_B2 = 0.999
ADAM_EPS = 1e-08
ADAM_WD = 0.01
ADAM_STEP = 10

MESH = pl.DeviceIdType.MESH
ANY = pl.BlockSpec(memory_space=pl.ANY)

PACK_ROWS = 32
SLOT_NORM = (0, 1, 0, 1024)
SLOT_FINAL = (1, 2, 0, 1024)
SLOT_ATTN = (2, 3, 0, 512)
SLOT_CONVG = (2, 3, 512, 1024)
SLOT_BF = (3, 4, 0, 8)
SLOT_META = (8, 24, 0, 256)
SLOT_CONVW = (24, 27, 0, 128)
LOSS_ROW = 4


def _params(sem=None):
    return pltpu.CompilerParams(dimension_semantics=sem, vmem_limit_bytes=VMEM_LIMIT_BYTES)


def _sigmoid(z):
    return 1.0 / (1.0 + jnp.exp(-z))


def _dot(a, b):
    return jnp.dot(a, b, preferred_element_type=F32)


def _dot_nt(a, b):
    return lax.dot_general(a, b, (((1,), (1,)), ((), ())), preferred_element_type=F32)


def _dot_exact(a, b):
    return jnp.dot(a, b, preferred_element_type=F32, precision=lax.Precision.HIGHEST)


def _group_matrix():
    r = lax.broadcasted_iota(jnp.int32, (D_ATTN, D_ATTN), 0) >> 6
    c = lax.broadcasted_iota(jnp.int32, (D_ATTN, D_ATTN), 1) >> 6
    return jnp.where(r == c, 1.0, 0.0).astype(MXU_DTYPE)


def _group_sum(x, gmat):
    hi = x.astype(MXU_DTYPE)
    lo = (x - hi.astype(F32)).astype(MXU_DTYPE)
    return _dot(hi, gmat) + _dot(lo, gmat)


def _x_block_specs(n_sub, rows):
    specs = [pl.BlockSpec((rows, D_MODEL), lambda i: (jnp.maximum(n_sub * i - 1, 0), 0))]
    for b in range(1, n_sub):
        specs.append(pl.BlockSpec((rows, D_MODEL), functools.partial(lambda i, b: (n_sub * i - 1 + b, 0), b=b)))
    return specs


def _position():
    return lax.axis_index("x"), lax.axis_index("y"), lax.axis_index("c")


def _gather_weights(wi, wo, small):
    def body(wi_ref, wo_ref, sm_ref, gwi_ref, gwo_ref, gsm_ref, send_sems, recv_sems):
        x, y, c = _position()
        sibling = (x, y, 1 - c)
        chips = [(1 - x, y), (x, 1 - y), (1 - x, 1 - y)]

        def remote(k, src, dst, to):
            return pltpu.make_async_remote_copy(src_ref=src, dst_ref=dst, send_sem=send_sems.at[k],
                                                recv_sem=recv_sems.at[k], device_id=to, device_id_type=MESH)

        first, passed, landed = [], [], []
        for a, (src_ref, g_ref) in enumerate(((wi_ref, gwi_ref), (wo_ref, gwo_ref))):
            for j, (cx, cy) in enumerate(chips):
                slot = g_ref.at[j, c]
                first.append(remote(6 * a + j, src_ref.at[c], slot, (cx, cy, c)))
                landed.append(remote(6 * a + j, slot, slot, sibling))
                passed.append(remote(6 * a + 3 + j, slot, slot, sibling))
        for j, (cx, cy) in enumerate(chips):
            first.append(remote(12 + j, sm_ref, gsm_ref.at[j], (cx, cy, c)))
        for cp in first:
            cp.start()
        for arrived, onward in zip(landed, passed):
            arrived.wait_recv()
            onward.start()
        for a, g_ref in enumerate((gwi_ref, gwo_ref)):
            for j in range(3):
                remote(6 * a + 3 + j, g_ref.at[j, 1 - c], g_ref.at[j, 1 - c], sibling).wait_recv()
        for j in range(3):
            remote(12 + j, sm_ref, gsm_ref.at[j], sibling).wait_recv()
        for cp in first + passed:
            cp.wait_send()

    return pl.pallas_call(
        body, name="gather_weights",
        out_shape=(jax.ShapeDtypeStruct((3,) + wi.shape, wi.dtype), jax.ShapeDtypeStruct((3,) + wo.shape, wo.dtype),
                   jax.ShapeDtypeStruct((3,) + small.shape, small.dtype)),
        in_specs=[ANY, ANY, ANY], out_specs=(ANY, ANY, ANY),
        scratch_shapes=[pltpu.SemaphoreType.DMA((15,)), pltpu.SemaphoreType.DMA((15,))],
    )(wi, wo, small)


def _pair_exchange(gw, gb):
    def body(gw_ref, gb_ref, ra_ref, rb_ref, send_sems, recv_sems):
        x, y, c = _position()
        sibling = (x, y, 1 - c)
        copies = [pltpu.make_async_remote_copy(
            src_ref=gb_ref.at[:, 1 - c], dst_ref=rb_ref, send_sem=send_sems.at[N_CHIPS], recv_sem=recv_sems.at[N_CHIPS],
            device_id=sibling, device_id_type=MESH)]
        for s, start in enumerate(WIN_START):
            rows = pl.ds(pl.multiple_of(start + WIN_HALF * (1 - c), SUBLANE), WIN_HALF)
            copies.append(pltpu.make_async_remote_copy(
                src_ref=gw_ref.at[rows], dst_ref=ra_ref.at[s], send_sem=send_sems.at[s], recv_sem=recv_sems.at[s],
                device_id=sibling, device_id_type=MESH))
        for cp in copies:
            cp.start()
        for cp in copies:
            cp.wait()

    return pl.pallas_call(
        body, name="grad_pair_exchange",
        out_shape=(jax.ShapeDtypeStruct((N_CHIPS, WIN_HALF, D_MODEL), gw.dtype),
                   jax.ShapeDtypeStruct((N_CHIPS,) + gb.shape[2:], gb.dtype)),
        in_specs=[ANY, ANY], out_specs=(ANY, ANY),
        scratch_shapes=[pltpu.SemaphoreType.DMA((N_CHIPS + 1,)), pltpu.SemaphoreType.DMA((N_CHIPS + 1,))],
    )(gw, gb)


def _chip_exchange(pa, pb):
    def body(pa_ref, pb_ref, ra_ref, rb_ref, send_sems, recv_sems):
        x, y, c = _position()
        chips = [(1 - x, y), (x, 1 - y), (1 - x, 1 - y)]
        copies = []
        for a, (src, dst) in enumerate(((pa_ref, ra_ref), (pb_ref, rb_ref))):
            for j, (cx, cy) in enumerate(chips):
                copies.append(pltpu.make_async_remote_copy(
                    src_ref=src.at[2 * cx + cy], dst_ref=dst.at[j], send_sem=send_sems.at[3 * a + j],
                    recv_sem=recv_sems.at[3 * a + j], device_id=(cx, cy, c), device_id_type=MESH))
        for cp in copies:
            cp.start()
        for cp in copies:
            cp.wait()

    return pl.pallas_call(
        body, name="grad_chip_exchange",
        out_shape=(jax.ShapeDtypeStruct((3,) + pa.shape[1:], pa.dtype),
                   jax.ShapeDtypeStruct((3,) + pb.shape[1:], pb.dtype)),
        in_specs=[ANY, ANY], out_specs=(ANY, ANY),
        scratch_shapes=[pltpu.SemaphoreType.DMA((6,)), pltpu.SemaphoreType.DMA((6,))],
    )(pa, pb)


def _pair_share(ha, hb):
    def body(ha_ref, hb_ref, oa_ref, ob_ref, send_sems, recv_sems):
        x, y, c = _position()
        copies = [pltpu.make_async_remote_copy(
            src_ref=src, dst_ref=dst, send_sem=send_sems.at[k], recv_sem=recv_sems.at[k],
            device_id=(x, y, 1 - c), device_id_type=MESH)
            for k, (src, dst) in enumerate(((ha_ref, oa_ref), (hb_ref, ob_ref)))]
        for cp in copies:
            cp.start()
        for cp in copies:
            cp.wait()

    return pl.pallas_call(
        body, name="grad_pair_share",
        out_shape=(jax.ShapeDtypeStruct(ha.shape, ha.dtype), jax.ShapeDtypeStruct(hb.shape, hb.dtype)),
        in_specs=[ANY, ANY], out_specs=(ANY, ANY),
        scratch_shapes=[pltpu.SemaphoreType.DMA((2,)), pltpu.SemaphoreType.DMA((2,))],
    )(ha, hb)


def _gather_small(pack):
    def body(p_ref, o_ref, send_sems, recv_sems):
        x, y, c = _position()
        copies = []
        for mask in range(1, N_DEV):
            peer = (1 - x if mask & 4 else x, 1 - y if mask & 2 else y, 1 - c if mask & 1 else c)
            copies.append(pltpu.make_async_remote_copy(
                src_ref=p_ref, dst_ref=o_ref.at[mask - 1], send_sem=send_sems.at[mask - 1],
                recv_sem=recv_sems.at[mask - 1], device_id=peer, device_id_type=MESH))
        for cp in copies:
            cp.start()
        for cp in copies:
            cp.wait()

    return pl.pallas_call(
        body, name="gather_small",
        out_shape=jax.ShapeDtypeStruct((N_DEV - 1,) + pack.shape, pack.dtype),
        in_specs=[ANY], out_specs=ANY,
        scratch_shapes=[pltpu.SemaphoreType.DMA((N_DEV - 1,)), pltpu.SemaphoreType.DMA((N_DEV - 1,))],
    )(pack)


def _pair_sum(mine, recv, c_idx):
    rows, cols = mine.shape[2:]

    def body(c_ref, a_ref, b_ref, o_ref, send_ref):
        total = a_ref[...] + b_ref[...]
        o_ref[...] = total
        send_ref[...] = total.astype(send_ref.dtype)

    out_spec = pl.BlockSpec((None, rows, cols), lambda s, c_ref: (s, 0, 0))
    return pl.pallas_call(
        body, name="grad_pair_sum",
        grid_spec=pltpu.PrefetchScalarGridSpec(
            num_scalar_prefetch=1, grid=(N_CHIPS,),
            in_specs=[pl.BlockSpec((None, None, rows, cols), lambda s, c_ref: (s, c_ref[0], 0, 0)),
                      pl.BlockSpec((None, rows, cols), lambda s, c_ref: (s, 0, 0))],
            out_specs=(out_spec, out_spec)),
        out_shape=(jax.ShapeDtypeStruct(recv.shape, recv.dtype), jax.ShapeDtypeStruct(recv.shape, WIRE_DTYPE)),
        compiler_params=_params(("parallel",)),
    )(c_idx, mine, recv)


def _window_start(s):
    return jnp.where(s == 0, WIN_START[0], jnp.where(s == 1, WIN_START[1], jnp.where(s == 2, WIN_START[2], WIN_START[3])))


def _pair_sum_windows(gw, recv, c_idx):
    tr = WIN_HALF // 3

    def body(c_ref, a_ref, b_ref, o_ref, send_ref):
        total = a_ref[...] + b_ref[...]
        o_ref[...] = total
        send_ref[...] = total.astype(send_ref.dtype)

    out_spec = pl.BlockSpec((None, tr, D_MODEL), lambda s, i, c_ref: (s, i, 0))
    return pl.pallas_call(
        body, name="grad_pair_sum_windows",
        grid_spec=pltpu.PrefetchScalarGridSpec(
            num_scalar_prefetch=1, grid=(N_CHIPS, WIN_HALF // tr),
            in_specs=[pl.BlockSpec((pl.Element(tr), pl.Element(D_MODEL)),
                                   lambda s, i, c_ref: (pl.multiple_of(
                                       _window_start(s) + WIN_HALF * c_ref[0] + tr * i, SUBLANE), 0)),
                      pl.BlockSpec((None, tr, D_MODEL), lambda s, i, c_ref: (s, i, 0))],
            out_specs=(out_spec, out_spec)),
        out_shape=(jax.ShapeDtypeStruct(recv.shape, recv.dtype), jax.ShapeDtypeStruct(recv.shape, WIRE_DTYPE)),
        compiler_params=_params(("parallel", "parallel")),
    )(c_idx, gw, recv)


def _assemble_w(own, others, starts):
    def body(starts_ref, own_ref, oth_ref, o_ref):
        o_ref[...] = jnp.zeros_like(o_ref)
        for k in range(N_CHIPS):
            rows = pl.ds(pl.multiple_of(starts_ref[k], 2 * SUBLANE), WIN_ROWS)
            o_ref[rows, :] = o_ref[rows, :] + (own_ref[...] if k == 0 else oth_ref[k - 1])

    return pl.pallas_call(
        body, name="assemble_w",
        in_specs=[pl.BlockSpec(memory_space=pltpu.SMEM), pl.BlockSpec(memory_space=pltpu.VMEM),
                  pl.BlockSpec(memory_space=pltpu.VMEM)],
        out_specs=pl.BlockSpec(memory_space=pltpu.VMEM),
        out_shape=jax.ShapeDtypeStruct((D_IN_PAD, D_MODEL), own.dtype),
        compiler_params=_params(),
    )(starts, own, others)


def _chip_sum(psum, recv3, chip_idx):
    rows, cols = psum.shape[1:]
    tr = rows // 2

    def body(s_ref, p_ref, r0, r1, r2, o_ref):
        o_ref[...] = ((p_ref[...] + r0[...].astype(F32)) + r1[...].astype(F32)) + r2[...].astype(F32)

    return pl.pallas_call(
        body, name="grad_chip_sum",
        grid_spec=pltpu.PrefetchScalarGridSpec(
            num_scalar_prefetch=1, grid=(2,),
            in_specs=[pl.BlockSpec((None, tr, cols), lambda i, s_ref: (s_ref[0], i, 0))] +
                     [pl.BlockSpec((None, tr, cols), functools.partial(lambda i, s_ref, j: (j, i, 0), j=j))
                      for j in range(3)],
            out_specs=pl.BlockSpec((tr, cols), lambda i, s_ref: (i, 0))),
        out_shape=jax.ShapeDtypeStruct((rows, cols), psum.dtype),
        compiler_params=_params(("parallel",)),
    )(chip_idx, psum, recv3, recv3, recv3)


def _adamw_math(w, g, m, v):
    m = ADAM_B1 * m + (1.0 - ADAM_B1) * g
    v = ADAM_B2 * v + (1.0 - ADAM_B2) * (g * g)
    m_hat = m / (1.0 - ADAM_B1 ** ADAM_STEP)
    v_hat = v / (1.0 - ADAM_B2 ** ADAM_STEP)
    delta = -ADAM_LR * (m_hat / (jnp.sqrt(v_hat) + ADAM_EPS) + ADAM_WD * w)
    return delta, m, v


def _adamw_big(w, g, m, v, tr):
    rows, cols = w.shape
    assert rows % tr == 0 and g.shape[0] >= rows

    def body(w_ref, g_ref, m_ref, v_ref, d_out, m_out, v_out):
        d, m2, v2 = _adamw_math(w_ref[...], g_ref[...], m_ref[...], v_ref[...])
        d_out[...] = d
        m_out[...] = m2
        v_out[...] = v2

    spec = pl.BlockSpec((tr, cols), lambda i: (i, 0))
    sds = jax.ShapeDtypeStruct((rows, cols), F32)
    return pl.pallas_call(
        body, name="adamw_big", grid=(rows // tr,), in_specs=[spec] * 4, out_specs=(spec,) * 3,
        out_shape=(sds,) * 3, compiler_params=_params(("parallel",)),
    )(w, g, m, v)


def _adamw_rows(w3, g, m3, v3):
    rows, _, cols = w3.shape
    tc = 2 * LANE

    def body(w_ref, g_ref, m_ref, v_ref, g_out, d_out, m_out, v_out):
        g = g_ref[...]
        d, m2, v2 = _adamw_math(w_ref[:, 0, :], g, m_ref[:, 0, :], v_ref[:, 0, :])
        g_out[:, 0, :] = g
        d_out[:, 0, :] = d
        m_out[:, 0, :] = m2
        v_out[:, 0, :] = v2

    spec3 = pl.BlockSpec((rows, 1, tc), lambda i: (0, 0, i))
    sds = jax.ShapeDtypeStruct((rows, 1, cols), F32)
    return pl.pallas_call(
        body, name="adamw_rows", grid=(cols // tc,),
        in_specs=[spec3, pl.BlockSpec((rows, tc), lambda i: (0, i)), spec3, spec3], out_specs=(spec3,) * 4,
        out_shape=(sds,) * 4, compiler_params=_params(("parallel",)),
    )(w3, g, m3, v3)


def _small_update(own, others, params, ms, vs):
    slots = (SLOT_NORM, SLOT_FINAL, SLOT_ATTN, SLOT_CONVG, SLOT_BF, SLOT_META, SLOT_CONVW)
    n = len(slots)

    def body(*refs):
        own_ref, gp_ref = refs[:2]
        refs = refs[1:]
        w_refs, m_refs, v_refs = refs[1:1 + n], refs[1 + n:1 + 2 * n], refs[1 + 2 * n:1 + 3 * n]
        outs = refs[1 + 3 * n:2 + 7 * n]
        loss_ref = outs[0]
        g_outs, d_outs, m_outs, v_outs = (outs[1 + k * n:1 + (k + 1) * n] for k in range(4))
        g_scr, w_scr, m_scr, v_scr = refs[2 + 7 * n:]
        x, y, c = _position()
        shard = 2 * x + y
        me = 4 * x + 2 * y + c
        tot = None
        for d in range(N_DEV):
            rel = jnp.bitwise_xor(me, d)
            term = jnp.where(rel == 0, own_ref[...], gp_ref[jnp.maximum(rel, 1) - 1])
            tot = term if tot is None else tot + term
        r0, r1, _, _ = SLOT_META
        meta_sel = tot[r0:r1, 0:256]
        cw_sel = tot[24:32, 0:128]
        for k in range(1, N_CHIPS):
            meta_sel = jnp.where(shard == k, tot[r0:r1, 256 * k:256 * (k + 1)], meta_sel)
            cw_sel = jnp.where(shard == k, tot[24:32, 128 * k:128 * (k + 1)], cw_sel)
        zeros = jnp.zeros((PACK_ROWS, D_MODEL), F32)
        for scr in (g_scr, w_scr, m_scr, v_scr):
            scr[...] = zeros
        g_scr[0:8, :] = tot[0:8, :]
        g_scr[r0:r1, 0:256] = meta_sel
        g_scr[24:32, 0:128] = cw_sel
        for (a, b, c0, c1), w_ref, m_ref, v_ref in zip(slots, w_refs, m_refs, v_refs):
            w_scr[a:b, c0:c1] = w_ref[...]
            m_scr[a:b, c0:c1] = m_ref[...]
            v_scr[a:b, c0:c1] = v_ref[...]
        loss_ref[...] = g_scr[LOSS_ROW:LOSS_ROW + 1, 0:1]
        d, m2, v2 = _adamw_math(w_scr[...], g_scr[...], m_scr[...], v_scr[...])
        w_scr[...] = d
        m_scr[...] = m2
        v_scr[...] = v2
        for (a, b, c0, c1), g_o, d_o, m_o, v_o in zip(slots, g_outs, d_outs, m_outs, v_outs):
            g_o[...] = g_scr[a:b, c0:c1]
            d_o[...] = w_scr[a:b, c0:c1]
            m_o[...] = m_scr[a:b, c0:c1]
            v_o[...] = v_scr[a:b, c0:c1]

    shapes = [jax.ShapeDtypeStruct(p.shape, F32) for p in params]
    out = pl.pallas_call(
        body, name="small_update",
        out_shape=[jax.ShapeDtypeStruct((1, 1), F32)] + shapes * 4,
        scratch_shapes=[pltpu.VMEM((PACK_ROWS, D_MODEL), F32)] * 4,
        compiler_params=_params(),
    )(own, others, *params, *ms, *vs)
    return out[0], out[1:1 + n], out[1 + n:1 + 2 * n], out[1 + 2 * n:1 + 3 * n], out[1 + 3 * n:1 + 4 * n]


def _in_proj(x2, meta_blk, norm_g, w_pad, bf_pad):
    seq = x2.shape[0]
    lp = seq + FRONT
    t = ROW_TILE
    nt = lp // t
    n_sub = t // LANE

    def body(*refs):
        x_refs = refs[:n_sub]
        mb, g_ref, w_ref, bf_ref = refs[n_sub:n_sub + 4]
        q_ref, k_ref, v_ref, rest_ref, fl_ref, ct_ref, u_ref, carry = refs[n_sub + 4:]
        i = pl.program_id(0)

        @pl.when(i == 0)
        def _():
            carry[...] = jnp.zeros_like(carry)

        first = jnp.where(i == 0, mb[...], x_refs[0][...])
        h = jnp.concatenate([first] + [r[...] for r in x_refs[1:]], axis=0)
        ms = jnp.mean(h * h, axis=-1, keepdims=True)
        u = ((h * lax.rsqrt(ms + EPS)) * g_ref[...]).astype(MXU_DTYPE)
        u_ref[...] = u

        def seg(a, width):
            return _dot_nt(u, w_ref[a:a + width, :])

        q_ref[...] = (seg(SEG_Q, D_ATTN) * (HEAD_DIM ** -0.5)).astype(MXU_DTYPE)
        k_ref[...] = seg(SEG_K, D_ATTN).astype(MXU_DTYPE)
        v_ref[...] = seg(SEG_V, D_ATTN).astype(MXU_DTYPE)
        for s in range(5):
            rest_ref[:, 512 * s:512 * (s + 1)] = seg(SEG_ZA + 512 * s, 512)
        fl = seg(SEG_F, LANE)
        fl_ref[...] = fl
        z = fl + bf_ref[...]
        logf = jnp.minimum(z, 0.0) - jnp.log(1.0 + jnp.exp(-jnp.abs(z)))
        row = i * t + lax.broadcasted_iota(jnp.int32, (t, LANE), 0)
        logf = jnp.where(row >= PAD_ROWS, logf, 0.0)
        tri = jnp.where(lax.broadcasted_iota(jnp.int32, (t, t), 0) >= lax.broadcasted_iota(jnp.int32, (t, t), 1),
                        1.0, 0.0)
        cs = _dot_exact(tri, logf) + carry[...]
        carry[...] = carry[...] + jnp.sum(logf, axis=0, keepdims=True)
        col = i * t + lax.broadcasted_iota(jnp.int32, (SUBLANE, t), 1)
        ct_ref[...] = jnp.where(col >= PAD_ROWS, cs.T[0:SUBLANE, :], -NEG)

    row_blk = lambda cols: pl.BlockSpec((t, cols), lambda i: (i, 0))
    const = lambda shape: pl.BlockSpec(shape, lambda i: (0, 0))
    return pl.pallas_call(
        body, name="in_proj", grid=(nt,),
        in_specs=_x_block_specs(n_sub, LANE) + [const((LANE, D_MODEL)), const((1, D_MODEL)),
                                                pl.BlockSpec((D_IN_PAD, D_MODEL), lambda i: (0, 0),
                                                             pipeline_mode=pl.Buffered(1)),
                                                const((1, LANE))],
        out_specs=(row_blk(D_ATTN), row_blk(D_ATTN), row_blk(D_ATTN), row_blk(5 * 512), row_blk(LANE),
                   pl.BlockSpec((SUBLANE, t), lambda i: (0, i)), row_blk(D_MODEL)),
        out_shape=(jax.ShapeDtypeStruct((lp, D_ATTN), MXU_DTYPE), jax.ShapeDtypeStruct((lp, D_ATTN), MXU_DTYPE),
                   jax.ShapeDtypeStruct((lp, D_ATTN), MXU_DTYPE), jax.ShapeDtypeStruct((lp, 5 * 512), F32),
                   jax.ShapeDtypeStruct((lp, LANE), F32),
                   jax.ShapeDtypeStruct((SUBLANE, lp), F32), jax.ShapeDtypeStruct((lp, D_MODEL), MXU_DTYPE)),
        scratch_shapes=[pltpu.VMEM((1, LANE), F32)],
        compiler_params=_params(("arbitrary",)),
    )(*([x2] * n_sub), meta_blk, norm_g, w_pad, bf_pad)


def _head_masks():
    lane = lax.broadcasted_iota(jnp.int32, (1, LANE), 1)
    return lane < HEAD_DIM, lane >= HEAD_DIM


def _pair_specs(lp, nt, t):
    blk = pl.BlockSpec((lp, LANE), lambda g: (0, g))
    ct_a = pl.BlockSpec((None, nt, 1, t), lambda g: (2 * g, 0, 0, 0))
    ct_b = pl.BlockSpec((None, nt, 1, t), lambda g: (2 * g + 1, 0, 0, 0))
    return blk, ct_a, ct_b


def _sub_rows(s, col):
    return jnp.concatenate([s[:, a * LANE:(a + 1) * LANE] - col for a in range(s.shape[1] // LANE)], axis=1)


def _loop_unrolled(lo, hi, step, init, n):
    def group(jj, carry):
        for k in range(n):
            carry = step(lo + n * jj + k, carry)
        return carry

    groups = (hi - lo) // n
    carry = lax.fori_loop(0, groups, group, init)
    return lax.fori_loop(lo + n * groups, hi, step, carry)


def _lane_chunks(s):
    return [s[:, a * LANE:(a + 1) * LANE] for a in range(s.shape[1] // LANE)]


def _attn_fwd(q, k, v, ct4):
    lp = q.shape[0]
    t = ROW_TILE
    nt = lp // t

    def body(q_ref, k_ref, v_ref, cta_ref, ctb_ref, o_ref, l_ref, m_ref, s_scr):
        masks = _head_masks()
        ct_refs = (cta_ref, ctb_ref)
        below = lax.broadcasted_iota(jnp.int32, (t, t), 1) <= lax.broadcasted_iota(jnp.int32, (t, t), 0)
        lane = lax.broadcasted_iota(jnp.int32, (1, LANE), 1)
        head_of_row = lax.broadcasted_iota(jnp.int32, (2 * t, LANE), 0) >= t
        ones_cols = jnp.where(lax.broadcasted_iota(jnp.int32, (2 * t, LANE), 1) == head_of_row.astype(jnp.int32),
                              1.0, 0.0).astype(MXU_DTYPE)

        def q_block(i, _):
            r0 = pl.multiple_of(i * t, t)
            qi = q_ref[pl.ds(r0, t), :]

            def scores(j):
                kj = k_ref[pl.ds(pl.multiple_of(j * t, t), t), :]
                return _dot_nt(qi, jnp.concatenate([jnp.where(hm, kj, 0).astype(MXU_DTYPE) for hm in masks], axis=0))

            def biased(j, hh, s2, diagonal):
                s = (s2[:, hh * t:(hh + 1) * t] - ct_refs[hh][j]) * LOG2E
                return jnp.where(below, s, NEG) if diagonal else s

            def max_step(j, carry, diagonal):
                s2 = scores(j)
                out = []
                for hh, m in enumerate(carry):
                    s = biased(j, hh, s2, diagonal)
                    s_scr[j, :, hh * t:(hh + 1) * t] = s
                    for c in _lane_chunks(s):
                        m = jnp.maximum(m, c)
                    out.append(m)
                return tuple(out)

            lanes_neg = jnp.full((t, LANE), NEG, F32)
            carry = _loop_unrolled(0, i, functools.partial(max_step, diagonal=False), (lanes_neg, lanes_neg),
                                   ATTN_UNROLL)
            ms = [jnp.max(m, axis=-1, keepdims=True) for m in max_step(i, carry, True)]

            def sum_step(j, acc):
                vj = v_ref[pl.ds(pl.multiple_of(j * t, t), t), :]
                v2 = jnp.concatenate([jnp.where(hm, vj, 0).astype(MXU_DTYPE) for hm in masks], axis=0)
                parts = [jnp.exp2(s_scr[j, :, hh * t:(hh + 1) * t] - ms[hh]).astype(MXU_DTYPE) for hh in range(2)]
                return acc + _dot(jnp.concatenate(parts, axis=1), jnp.concatenate([v2, ones_cols], axis=1))

            acc = _loop_unrolled(0, i + 1, sum_step, jnp.zeros((t, 2 * LANE), F32), ATTN_UNROLL)
            sums = acc[:, LANE:]
            l_pair = jnp.where(masks[0], jnp.sum(jnp.where(lane == 0, sums, 0.0), axis=-1, keepdims=True),
                               jnp.sum(jnp.where(lane == 1, sums, 0.0), axis=-1, keepdims=True))
            o_ref[pl.ds(r0, t), :] = acc[:, :LANE] / l_pair
            l_ref[pl.ds(r0, t), :] = l_pair
            m_ref[pl.ds(r0, t), 0:LANE] = jnp.broadcast_to(ms[0], (t, LANE))
            m_ref[pl.ds(r0, t), LANE:2 * LANE] = jnp.broadcast_to(ms[1], (t, LANE))
            return 0

        lax.fori_loop(0, nt, q_block, 0)

    blk, ct_a, ct_b = _pair_specs(lp, nt, t)
    return pl.pallas_call(
        body, name="attn_fwd", grid=(HEADS // 2,),
        in_specs=[blk, blk, blk, ct_a, ct_b], out_specs=(blk, blk, pl.BlockSpec((lp, 2 * LANE), lambda g: (0, g))),
        out_shape=(jax.ShapeDtypeStruct((lp, D_ATTN), F32), jax.ShapeDtypeStruct((lp, D_ATTN), F32),
                   jax.ShapeDtypeStruct((lp, HEADS * LANE), F32)),
        scratch_shapes=[pltpu.VMEM((nt, t, 2 * t), F32)],
        compiler_params=_params(("parallel",)),
    )(q, k, v, ct4, ct4)


def _attn_bwd(q, k, v, do, m, delta, ct4):
    lp = q.shape[0]
    t = ROW_TILE
    nt = lp // t

    def body(q_ref, k_ref, v_ref, do_ref, ma_ref, mb_ref, dla_ref, dlb_ref, cta_ref, ctb_ref,
             dq_ref, dk_ref, dv_ref, dc_ref, dq_acc, dk_acc, dv_acc):
        masks = _head_masks()
        ct_refs, m_refs, dl_refs = (cta_ref, ctb_ref), (ma_ref, mb_ref), (dla_ref, dlb_ref)
        below = lax.broadcasted_iota(jnp.int32, (t, t), 1) <= lax.broadcasted_iota(jnp.int32, (t, t), 0)
        tn = (((0,), (0,)), ((), ()))
        dq_acc[...] = jnp.zeros_like(dq_acc)

        def k_block(j, _):
            c0 = pl.multiple_of(j * t, t)
            kj = k_ref[pl.ds(c0, t), :]
            vj = v_ref[pl.ds(c0, t), :]
            k2 = jnp.concatenate([jnp.where(hm, kj, 0).astype(MXU_DTYPE) for hm in masks], axis=0)
            v2 = jnp.concatenate([jnp.where(hm, vj, 0).astype(MXU_DTYPE) for hm in masks], axis=0)
            ck = [r[j] for r in ct_refs]
            dk_acc[...] = jnp.zeros_like(dk_acc)
            dv_acc[...] = jnp.zeros_like(dv_acc)

            def q_block(i, colsums, diagonal):
                r0 = pl.multiple_of(i * t, t)
                qi = q_ref[pl.ds(r0, t), :]
                doi = do_ref[pl.ds(r0, t), :]
                q2 = jnp.concatenate([jnp.where(hm, qi, 0).astype(MXU_DTYPE) for hm in masks], axis=0)
                do2 = jnp.concatenate([jnp.where(hm, doi, 0).astype(MXU_DTYPE) for hm in masks], axis=0)
                s2 = _dot_nt(qi, k2)
                dp2 = _dot_nt(doi, v2)
                out, ps, dss = [], [], []
                for hh in range(2):
                    s = (s2[:, hh * t:(hh + 1) * t] - ck[hh]) * LOG2E
                    if diagonal:
                        s = jnp.where(below, s, NEG)
                    p = jnp.exp2(_sub_rows(s, m_refs[hh][pl.ds(r0, t), :])).astype(MXU_DTYPE)
                    ds32 = p.astype(F32) * _sub_rows(dp2[:, hh * t:(hh + 1) * t], dl_refs[hh][pl.ds(r0, t), :])
                    ps.append(p)
                    dss.append(ds32.astype(MXU_DTYPE))
                    out.append(colsums[hh] + jnp.sum(ds32, axis=0, keepdims=True))
                dv_acc[...] = dv_acc[...] + lax.dot_general(jnp.concatenate(ps, axis=0), do2, tn,
                                                            preferred_element_type=F32)
                dk_acc[...] = dk_acc[...] + lax.dot_general(jnp.concatenate(dss, axis=0), q2, tn,
                                                            preferred_element_type=F32)
                dq_acc[pl.ds(r0, t), :] = dq_acc[pl.ds(r0, t), :] + _dot(jnp.concatenate(dss, axis=1), k2)
                return tuple(out)

            colsums = q_block(j, (jnp.zeros((1, t), F32), jnp.zeros((1, t), F32)), True)
            colsums = lax.fori_loop(j + 1, nt, functools.partial(q_block, diagonal=False), colsums)
            for hh in range(2):
                dc_ref[hh, j] = -colsums[hh]
            dk_ref[pl.ds(c0, t), :] = dk_acc[...].astype(dk_ref.dtype)
            dv_ref[pl.ds(c0, t), :] = dv_acc[...].astype(dv_ref.dtype)
            return 0

        lax.fori_loop(0, nt, k_block, 0)
        dq_ref[...] = (dq_acc[...] * (HEAD_DIM ** -0.5)).astype(dq_ref.dtype)

    blk, ct_a, ct_b = _pair_specs(lp, nt, t)
    rep_a = pl.BlockSpec((lp, LANE), lambda g: (0, 2 * g))
    rep_b = pl.BlockSpec((lp, LANE), lambda g: (0, 2 * g + 1))
    return pl.pallas_call(
        body, name="attn_bwd", grid=(HEADS // 2,),
        in_specs=[blk] * 4 + [rep_a, rep_b, rep_a, rep_b, ct_a, ct_b],
        out_specs=(blk, blk, blk, pl.BlockSpec((2, nt, 1, t), lambda g: (g, 0, 0, 0))),
        out_shape=(jax.ShapeDtypeStruct((lp, D_ATTN), MXU_DTYPE),) * 3
                  + (jax.ShapeDtypeStruct((HEADS, nt, 1, t), F32),),
        scratch_shapes=[pltpu.VMEM((lp, LANE), F32), pltpu.VMEM((t, LANE), F32), pltpu.VMEM((t, LANE), F32)],
        compiler_params=_params(("parallel",)),
    )(q, k, v, do, m, m, delta, delta, ct4, ct4)


def _shift_down(prev8, cur, k):
    ext = jnp.concatenate([prev8, cur], axis=0)
    return pltpu.roll(ext, k, 0)[SUBLANE:, :]


def _shift_up(cur, next8, k):
    ext = jnp.concatenate([cur, next8], axis=0)
    n = ext.shape[0]
    return pltpu.roll(ext, n - k, 0)[:cur.shape[0], :]


def _post(o, l_sum, rest, x2, meta_blk, tgt2, w_out, attn_g, conv_g, final_g, conv_w8):
    lp = o.shape[0]
    t = ROW_TILE
    nt = lp // t
    n_sub = t // LANE
    hb = t // SUBLANE

    def body(*refs):
        o_ref, l_ref, za_ref, gb_ref, gc_ref, xc_ref, zc_ref, gch_ref, xch_ref = refs[:9]
        refs = refs[1:]
        x_refs = refs[8:8 + n_sub]
        mb = refs[8 + n_sub]
        t_refs = refs[9 + n_sub:9 + 2 * n_sub]
        wo_ref, ag_ref, cg_ref, fg_ref, cw_ref = refs[9 + 2 * n_sub:14 + 2 * n_sub]
        (dout_ref, do_ref, dl_ref, dza_ref, dgb_ref, dzc_ref, dcv_ref,
         loss_ref, gf_ref, gag_ref, gcg_ref, gwo_ref) = refs[14 + 2 * n_sub:]
        i = pl.program_id(0)

        @pl.when(i == 0)
        def _():
            for r in (loss_ref, gf_ref, gag_ref, gcg_ref, gwo_ref):
                r[...] = jnp.zeros_like(r)

        gmat = _group_matrix()
        inv_g = 1.0 / HEAD_DIM
        o_v = o_ref[...]
        ra = lax.rsqrt(_group_sum(o_v * o_v, gmat) * inv_g + EPS)
        n_a = o_v * ra
        a_n = n_a * ag_ref[...]
        za = za_ref[...]
        sig_a = _sigmoid(za)
        sz_a = za * sig_a
        y_a = a_n * sz_a
        gb = gb_ref[...]
        gc = gc_ref[...]
        xc = xc_ref[...]
        cx = gc * xc
        cx_prev = jnp.where(i == 0, 0.0, gch_ref[...] * xch_ref[...])
        conv = (cw_ref[0:1, :] * _shift_down(cx_prev, cx, 2) + cw_ref[1:2, :] * _shift_down(cx_prev, cx, 1)
                + cw_ref[2:3, :] * cx)
        e = gb * conv
        re = lax.rsqrt(_group_sum(e * e, gmat) * inv_g + EPS)
        n_e = e * re
        e_n = n_e * cg_ref[...]
        zc = zc_ref[...]
        sig_c = _sigmoid(zc)
        sz_c = zc * sig_c
        y_c = e_n * sz_c
        mix = jnp.concatenate([y_a, y_c], axis=-1)
        mix_b = mix.astype(MXU_DTYPE)
        first = jnp.where(i == 0, mb[...], x_refs[0][...])
        h = jnp.concatenate([first] + [r[...] for r in x_refs[1:]], axis=0)
        out = h + _dot(mix_b, wo_ref[...])
        r2 = lax.rsqrt(jnp.mean(out * out, axis=-1, keepdims=True) + EPS)
        n_f = out * r2
        y = n_f * fg_ref[...]
        tgt = jnp.concatenate([r[...] for r in t_refs], axis=0)
        valid = (i * t + lax.broadcasted_iota(jnp.int32, (t, 1), 0)) >= FRONT
        diff = jnp.where(valid, y - tgt, 0.0)
        loss_ref[...] = loss_ref[...] + 0.5 * jnp.sum(jnp.sum(diff * diff, axis=-1, keepdims=True) * (1.0 / D_MODEL))
        dy = diff * (1.0 / D_MODEL)
        gf_ref[...] = gf_ref[...] + jnp.sum(dy * n_f, axis=0, keepdims=True)
        dn = dy * fg_ref[...]
        d_out = r2 * (dn - n_f * jnp.mean(dn * n_f, axis=-1, keepdims=True))
        dout_ref[...] = d_out
        d_out_b = d_out.astype(MXU_DTYPE)
        d_mix = _dot_nt(d_out_b, wo_ref[...])
        gwo_ref[...] = gwo_ref[...] + _dot(mix.T.astype(MXU_DTYPE), d_out_b)
        d_ya = d_mix[:, :D_ATTN]
        d_yc = d_mix[:, D_ATTN:]
        d_an = d_ya * sz_a
        dza_ref[...] = (d_ya * a_n * (sig_a * (1.0 + za * (1.0 - sig_a)))).astype(dza_ref.dtype)
        gag_ref[...] = gag_ref[...] + jnp.sum(d_an * n_a, axis=0, keepdims=True)
        dn_a = d_an * ag_ref[...]
        d_o = ra * (dn_a - n_a * (_group_sum(dn_a * n_a, gmat) * inv_g))
        d_o_b = (d_o / l_ref[...]).astype(do_ref.dtype)
        do_ref[...] = d_o_b
        head_rep = jnp.where((lax.broadcasted_iota(jnp.int32, (D_ATTN, HEADS * LANE), 0) >> 6)
                             == (lax.broadcasted_iota(jnp.int32, (D_ATTN, HEADS * LANE), 1) >> 7), 1.0, 0.0)
        dl_ref[...] = _group_sum(d_o_b.astype(F32) * o_v, head_rep.astype(MXU_DTYPE))
        d_en = d_yc * sz_c
        dzc_ref[...] = (d_yc * e_n * (sig_c * (1.0 + zc * (1.0 - sig_c)))).astype(dzc_ref.dtype)
        gcg_ref[...] = gcg_ref[...] + jnp.sum(d_en * n_e, axis=0, keepdims=True)
        dn_e = d_en * cg_ref[...]
        d_e = re * (dn_e - n_e * (_group_sum(dn_e * n_e, gmat) * inv_g))
        dgb_ref[...] = (d_e * conv).astype(dgb_ref.dtype)
        dcv_ref[...] = d_e * gb

    row_blk = lambda cols: pl.BlockSpec((t, cols), lambda i: (i, 0))
    rest_blk = lambda s: pl.BlockSpec((t, 512), functools.partial(lambda i, s: (i, s), s=s))
    halo = lambda s: pl.BlockSpec((SUBLANE, 512), functools.partial(lambda i, s: (jnp.maximum(i * hb - 1, 0), s), s=s))
    const = lambda shape: pl.BlockSpec(shape, lambda i: (0, 0))
    acc = lambda shape: pl.BlockSpec(shape, lambda i: (0, 0))
    return pl.pallas_call(
        body, name="post_fwd_bwd", grid=(nt,),
        in_specs=[row_blk(D_ATTN), row_blk(D_ATTN)] + [rest_blk(s) for s in range(5)] + [halo(2), halo(3)]
                 + _x_block_specs(n_sub, LANE) + [const((LANE, D_MODEL))] + _x_block_specs(n_sub, LANE)
                 + [const((D_MODEL, D_MODEL)), const((1, D_ATTN)), const((1, D_CONV)), const((1, D_MODEL)),
                    const((SUBLANE, D_CONV))],
        out_specs=(row_blk(D_MODEL), row_blk(D_ATTN), row_blk(HEADS * LANE), row_blk(D_ATTN), row_blk(D_CONV),
                   row_blk(D_CONV), row_blk(D_CONV),
                   acc((1, LANE)), acc((1, D_MODEL)), acc((1, D_ATTN)), acc((1, D_CONV)), acc((D_MODEL, D_MODEL))),
        out_shape=(jax.ShapeDtypeStruct((lp, D_MODEL), F32), jax.ShapeDtypeStruct((lp, D_ATTN), MXU_DTYPE),
                   jax.ShapeDtypeStruct((lp, HEADS * LANE), F32), jax.ShapeDtypeStruct((lp, D_ATTN), MXU_DTYPE),
                   jax.ShapeDtypeStruct((lp, D_CONV), MXU_DTYPE), jax.ShapeDtypeStruct((lp, D_CONV), MXU_DTYPE),
                   jax.ShapeDtypeStruct((lp, D_CONV), F32),
                   jax.ShapeDtypeStruct((1, LANE), F32), jax.ShapeDtypeStruct((1, D_MODEL), F32),
                   jax.ShapeDtypeStruct((1, D_ATTN), F32), jax.ShapeDtypeStruct((1, D_CONV), F32),
                   jax.ShapeDtypeStruct((D_MODEL, D_MODEL), F32)),
        compiler_params=_params(("arbitrary",)),
    )(o, l_sum, *([rest] * 5), rest, rest, *([x2] * n_sub), meta_blk, *([tgt2] * n_sub),
      w_out, attn_g, conv_g, final_g, conv_w8)


def _bwd_in(x2, meta_blk, norm_g, w_pad, bf_pad, fl, dc, dq, dk, dv, dza, dgb, dzc, dconv, rest, d_out, conv_w8):
    lp = fl.shape[0]
    t = ROW_TILE
    nt = lp // t
    n_sub = t // LANE
    hb = t // SUBLANE
    rev = lambda i: nt - 1 - i

    def body(*refs):
        x_refs = refs[:n_sub]
        (mb, g_ref, w_ref, bf_ref, fl_ref, dc_ref, dq_ref, dk_ref, dv_ref, dza_ref, dgb_ref, dzc_ref,
         dcv_ref, dcvn_ref, gc_ref, xc_ref, gch_ref, xch_ref, dout_ref, cw_ref) = refs[n_sub:n_sub + 20]
        dp_ref, dh_ref, gn_ref, gbf_ref, gcw_ref, carry = refs[n_sub + 20:]
        step = pl.program_id(0)
        i = rev(step)

        @pl.when(step == 0)
        def _():
            for r in (gn_ref, gbf_ref, gcw_ref, carry):
                r[...] = jnp.zeros_like(r)

        dc8 = jnp.concatenate([dc_ref[...], jnp.zeros((LANE - HEADS, t), F32)], axis=0).T
        triu = jnp.where(lax.broadcasted_iota(jnp.int32, (t, t), 1) >= lax.broadcasted_iota(jnp.int32, (t, t), 0),
                         1.0, 0.0)
        dlogf = _dot_exact(triu, dc8) + carry[...]
        carry[...] = carry[...] + jnp.sum(dc8, axis=0, keepdims=True)
        z = fl_ref[...] + bf_ref[...]
        row = i * t + lax.broadcasted_iota(jnp.int32, (t, LANE), 0)
        d_f = jnp.where(row >= PAD_ROWS, dlogf * (1.0 / (1.0 + jnp.exp(z))), 0.0)
        gbf_ref[...] = gbf_ref[...] + jnp.sum(d_f, axis=0, keepdims=True)
        dcv = dcv_ref[...]
        dcv_next = jnp.where(i == nt - 1, 0.0, dcvn_ref[...])
        d_cx = (cw_ref[2:3, :] * dcv + cw_ref[1:2, :] * _shift_up(dcv, dcv_next, 1)
                + cw_ref[0:1, :] * _shift_up(dcv, dcv_next, 2))
        gc = gc_ref[...]
        xc = xc_ref[...]
        cx = gc * xc
        cx_prev = jnp.where(i == 0, 0.0, gch_ref[...] * xch_ref[...])
        rowi = lax.broadcasted_iota(jnp.int32, (SUBLANE, 1), 0)
        gcw = (jnp.where(rowi == 0, jnp.sum(dcv * _shift_down(cx_prev, cx, 2), axis=0, keepdims=True), 0.0)
               + jnp.where(rowi == 1, jnp.sum(dcv * _shift_down(cx_prev, cx, 1), axis=0, keepdims=True), 0.0)
               + jnp.where(rowi == 2, jnp.sum(dcv * cx, axis=0, keepdims=True), 0.0))
        gcw_ref[...] = gcw_ref[...] + gcw
        dp_ref[:, SEG_Q:SEG_Q + 512] = dq_ref[...]
        dp_ref[:, SEG_K:SEG_K + 512] = dk_ref[...]
        dp_ref[:, SEG_V:SEG_V + 512] = dv_ref[...]
        dp_ref[:, SEG_F:SEG_F + LANE] = d_f.astype(dp_ref.dtype)
        dp_ref[:, SEG_ZA:SEG_ZA + 512] = dza_ref[...]
        dp_ref[:, SEG_GB:SEG_GB + 512] = dgb_ref[...]
        dp_ref[:, SEG_GC:SEG_GC + 512] = (d_cx * xc).astype(dp_ref.dtype)
        dp_ref[:, SEG_XC:SEG_XC + 512] = (d_cx * gc).astype(dp_ref.dtype)
        dp_ref[:, SEG_ZC:SEG_ZC + 512] = dzc_ref[...]
        d_u = _dot(dp_ref[...], w_ref[...])
        first = jnp.where(i == 0, mb[...], x_refs[0][...])
        h = jnp.concatenate([first] + [r[...] for r in x_refs[1:]], axis=0)
        r1 = lax.rsqrt(jnp.mean(h * h, axis=-1, keepdims=True) + EPS)
        n_h = h * r1
        gn_ref[...] = gn_ref[...] + jnp.sum(d_u * n_h, axis=0, keepdims=True)
        dn = d_u * g_ref[...]
        dh_ref[...] = dout_ref[...] + r1 * (dn - n_h * jnp.mean(dn * n_h, axis=-1, keepdims=True))

    def x_specs():
        specs = [pl.BlockSpec((LANE, D_MODEL), lambda s: (jnp.maximum(n_sub * rev(s) - 1, 0), 0))]
        for b in range(1, n_sub):
            specs.append(pl.BlockSpec((LANE, D_MODEL), functools.partial(lambda s, b: (n_sub * rev(s) - 1 + b, 0), b=b)))
        return specs

    row_blk = lambda cols: pl.BlockSpec((t, cols), lambda s: (rev(s), 0))
    rest_blk = lambda k: pl.BlockSpec((t, 512), functools.partial(lambda s, k: (rev(s), k), k=k))
    halo_prev = lambda k: pl.BlockSpec(
        (SUBLANE, 512), functools.partial(lambda s, k: (jnp.maximum(rev(s) * hb - 1, 0), k), k=k))
    halo_next = pl.BlockSpec((SUBLANE, 512), lambda s: (jnp.minimum((rev(s) + 1) * hb, lp // SUBLANE - 1), 0))
    const = lambda shape: pl.BlockSpec(shape, lambda s: (0, 0))
    return pl.pallas_call(
        body, name="bwd_in", grid=(nt,),
        in_specs=x_specs() + [const((LANE, D_MODEL)), const((1, D_MODEL)),
                              pl.BlockSpec((D_IN_PAD, D_MODEL), lambda s: (0, 0), pipeline_mode=pl.Buffered(1)),
                              const((1, LANE)), row_blk(LANE),
                              pl.BlockSpec((HEADS, t), lambda s: (0, rev(s))),
                              row_blk(512), row_blk(512), row_blk(512), row_blk(512), row_blk(512), row_blk(512),
                              row_blk(512), halo_next, rest_blk(2), rest_blk(3), halo_prev(2), halo_prev(3),
                              row_blk(D_MODEL), const((SUBLANE, D_CONV))],
        out_specs=(row_blk(D_IN_PAD), row_blk(D_MODEL), const((1, D_MODEL)), const((1, LANE)),
                   const((SUBLANE, D_CONV))),
        out_shape=(jax.ShapeDtypeStruct((lp, D_IN_PAD), MXU_DTYPE), jax.ShapeDtypeStruct((lp, D_MODEL), F32),
                   jax.ShapeDtypeStruct((1, D_MODEL), F32), jax.ShapeDtypeStruct((1, LANE), F32),
                   jax.ShapeDtypeStruct((SUBLANE, D_CONV), F32)),
        scratch_shapes=[pltpu.VMEM((1, LANE), F32)],
        compiler_params=_params(("arbitrary",)),
    )(*([x2] * n_sub), meta_blk, norm_g, w_pad, bf_pad, fl, dc, dq, dk, dv, dza, dgb, dzc, dconv, dconv,
      rest, rest, rest, rest, d_out, conv_w8)


def _grad_w_in(u, dproj):
    lp = u.shape[0]
    tk = ROW_TILE
    tn = GW_COL_TILE

    def body(d_ref, u_ref, o_ref):
        @pl.when(pl.program_id(1) == 0)
        def _():
            o_ref[...] = jnp.zeros_like(o_ref)

        o_ref[...] = o_ref[...] + lax.dot_general(d_ref[...], u_ref[...], (((0,), (0,)), ((), ())),
                                                  preferred_element_type=F32)

    return pl.pallas_call(
        body, name="grad_w_in", grid=(D_IN_PAD // tn, lp // tk),
        in_specs=[pl.BlockSpec((tk, tn), lambda n, k: (k, n)), pl.BlockSpec((tk, D_MODEL), lambda n, k: (k, 0))],
        out_specs=pl.BlockSpec((tn, D_MODEL), lambda n, k: (n, 0)),
        out_shape=jax.ShapeDtypeStruct((D_IN_PAD, D_MODEL), F32),
        compiler_params=_params(("parallel", "arbitrary")),
    )(dproj, u)


def _by_chip(own, others, me):
    by_mask = jnp.stack([own, others[1], others[0], others[2]])
    return [lax.dynamic_index_in_dim(by_mask, jnp.bitwise_xor(me, s), 0, keepdims=False) for s in range(N_CHIPS)]


def _both_halves(mine, other, c):
    return jnp.where(c == 0, jnp.concatenate([mine, other], axis=0), jnp.concatenate([other, mine], axis=0))


def _local_step(x2, tgt2, meta_full, norm_g, w_pad, b_f, conv_w_full, attn_g, conv_g, w_out_full, final_g):
    lp = x2.shape[0] + FRONT
    nt = lp // ROW_TILE
    meta_blk = jnp.concatenate([jnp.zeros((PAD_ROWS, D_MODEL), F32), meta_full], axis=0)
    bf_pad = jnp.pad(b_f, ((0, 0), (0, LANE - HEADS)))
    conv_w8 = jnp.pad(conv_w_full, ((0, SUBLANE - conv_w_full.shape[0]), (0, 0)))
    q, k, v, rest, fl, ct, u = _in_proj(x2, meta_blk, norm_g, w_pad, bf_pad)
    ct4 = ct.reshape(SUBLANE, nt, 1, ROW_TILE)
    o, l_sum, m_max = _attn_fwd(q, k, v, ct4)
    (d_out, d_o, delta, dza, dgb, dzc, dconv, loss, g_final, g_attn, g_convg, gw_out) = _post(
        o, l_sum, rest, x2, meta_blk, tgt2, w_out_full, attn_g, conv_g, final_g, conv_w8)
    dq, dk, dv, dc = _attn_bwd(q, k, v, d_o, m_max, delta, ct4)
    dproj, d_h, g_norm, g_bf, g_cw = _bwd_in(x2, meta_blk, norm_g, w_pad, bf_pad, fl, dc.reshape(HEADS, lp), dq, dk, dv,
                                             dza, dgb, dzc, dconv, rest, d_out, conv_w8)
    gw_in = _grad_w_in(u, dproj)
    return dict(loss=loss, d_h=d_h, g_norm=g_norm, g_final=g_final, g_attn=g_attn, g_convg=g_convg, g_bf=g_bf,
                g_cw=g_cw, gw_out=gw_out, gw_in=gw_in)


def kernel(x, meta, norm_g, w_in, b_f, conv_w, attn_norm_g, conv_norm_g, w_out, final_norm_g, loss_target, m_meta, m_norm_g, m_w_in, m_b_f, m_conv_w, m_attn_norm_g, m_conv_norm_g, m_w_out, m_final_norm_g, v_meta, v_norm_g, v_w_in, v_b_f, v_conv_w, v_attn_norm_g, v_conv_norm_g, v_w_out, v_final_norm_g):
    cx_, cy_, cc_ = _position()
    chip = 2 * cx_ + cy_
    shard = w_in.shape[2]
    out_half = w_out.shape[1] // 2
    pick = lambda vals: jnp.where(chip == 0, vals[0], jnp.where(chip == 1, vals[1], jnp.where(chip == 2, vals[2], vals[3])))
    a_off, b_off = pick(A_OFF), pick(B_OFF)
    wt = jnp.transpose(w_in[0]).astype(MXU_DTYPE)
    wi = lax.dynamic_update_slice_in_dim(
        lax.dynamic_update_slice_in_dim(jnp.zeros((WIN_ROWS, D_MODEL), MXU_DTYPE), wt[:PIECE_A], a_off, 0),
        wt[PIECE_A:], b_off, 0)
    wo = w_out[0].astype(MXU_DTYPE)
    small = jnp.concatenate([meta, jnp.pad(conv_w[0], ((0, 8 - conv_w.shape[1]), (0, meta.shape[1] - conv_w.shape[2])))],
                            axis=0)
    gwi, gwo, gsm = _gather_weights(wi.reshape(2, WIN_HALF, D_MODEL), wo.reshape(2, out_half, D_MODEL), small)
    starts = jnp.stack([_window_start(jnp.bitwise_xor(chip, mask)) for mask in (0, 2, 1, 3)]).astype(jnp.int32)
    w_pad = _assemble_w(wi, gwi.reshape(3, WIN_ROWS, D_MODEL), starts)
    w_out_full = jnp.concatenate(_by_chip(wo, gwo.reshape(3, 2 * out_half, D_MODEL), chip), axis=0)
    small_full = jnp.concatenate(_by_chip(small, gsm, chip), axis=1)
    meta_full = small_full[:N_META]
    conv_w_full = jnp.concatenate([small_full[N_META:N_META + 3, 256 * s:256 * s + LANE] for s in range(N_CHIPS)], axis=1)
    final_g2 = final_norm_g.reshape(1, D_MODEL)
    r = _local_step(x[0], loss_target[0], meta_full, norm_g, w_pad, b_f, conv_w_full, attn_norm_g, conv_norm_g,
                    w_out_full, final_g2)
    grad_x = r["d_h"][FRONT:][None]
    gb = r["gw_out"].reshape(N_CHIPS, 2, out_half, D_MODEL)
    ra, rb = _pair_exchange(r["gw_in"], gb)
    c_idx = jnp.reshape(cc_, (1,)).astype(jnp.int32)
    chip_idx = jnp.reshape(chip, (1,)).astype(jnp.int32)
    pa, pa_wire = _pair_sum_windows(r["gw_in"], ra, c_idx)
    pb, pb_wire = _pair_sum(gb, rb, c_idx)
    xa, xb = _chip_exchange(pa_wire, pb_wire)
    ha = _chip_sum(pa, xa, chip_idx)
    hb = _chip_sum(pb, xb, chip_idx)
    oa, ob = _pair_share(ha, hb)
    g_window = _both_halves(ha, oa, cc_)
    g_w_in_t = jnp.concatenate([lax.dynamic_slice_in_dim(g_window, a_off, PIECE_A, 0),
                                lax.dynamic_slice_in_dim(g_window, b_off, shard - PIECE_A, 0)], axis=0)
    g_w_out = _both_halves(hb, ob, cc_)
    as_rows = lambda a: jnp.transpose(a, (2, 0, 1))
    g_w_in, d_w_in, nm_w_in, nv_w_in = (jnp.transpose(a, (1, 2, 0)) for a in _adamw_rows(
        as_rows(w_in), g_w_in_t, as_rows(m_w_in), as_rows(v_w_in)))
    d_w_out, nm_w_out, nv_w_out = (a[None] for a in _adamw_big(w_out[0], g_w_out, m_w_out[0], v_w_out[0], LANE))
    wide = lambda a: jnp.pad(a, ((0, 0), (0, D_MODEL - a.shape[1])))
    pack = jnp.concatenate([
        r["g_norm"], r["g_final"], jnp.concatenate([r["g_attn"], r["g_convg"]], axis=1), wide(r["g_bf"]),
        wide(r["loss"]), jnp.zeros((3, D_MODEL), F32), r["d_h"][PAD_ROWS:FRONT], wide(r["g_cw"])], axis=0)
    params = (norm_g, final_g2, attn_norm_g, conv_norm_g, b_f, meta, conv_w[0])
    ms = (m_norm_g, m_final_norm_g.reshape(1, D_MODEL), m_attn_norm_g, m_conv_norm_g, m_b_f, m_meta, m_conv_w[0])
    vs = (v_norm_g, v_final_norm_g.reshape(1, D_MODEL), v_attn_norm_g, v_conv_norm_g, v_b_f, v_meta, v_conv_w[0])
    loss, g_s, d_s, m_s, v_s = _small_update(pack, _gather_small(pack), params, ms, vs)

    def ordered(small_list, big_in, big_out):
        s_norm, s_final, s_attn, s_convg, s_bf, s_meta, s_cw = small_list
        return (s_meta, s_norm, big_in, s_bf, s_cw[None], s_attn, s_convg, big_out, s_final.reshape(D_MODEL))

    return (loss.reshape(()), grad_x,
            *ordered(g_s, g_w_in, g_w_out[None]), *ordered(d_s, d_w_in, d_w_out),
            *ordered(m_s, nm_w_in, nm_w_out), *ordered(v_s, nv_w_in, nv_w_out))
```

```python
import functools

import jax
import jax.numpy as jnp
from jax import lax
from jax.experimental import pallas as pl
from jax.experimental.pallas import tpu as pltpu

F32 = jnp.float32
MXU_DTYPE = jnp.bfloat16
WIRE_DTYPE = jnp.bfloat16

D_MODEL = 1024
N_META = 16
HEADS = 8
HEAD_DIM = 64
D_ATTN = HEADS * HEAD_DIM
D_CONV = 512
EPS = 1e-6
LANE = 128
SUBLANE = 8
ROW_TILE = 384
ATTN_UNROLL = 3
FRONT = LANE
PAD_ROWS = FRONT - N_META
NEG = -1e30
LOG2E = 1.4426950408889634
N_CHIPS = 4
N_DEV = 8
VMEM_LIMIT_BYTES = 60 * 1024 * 1024

SEG_Q, SEG_K, SEG_V, SEG_F, SEG_ZA, SEG_GB, SEG_GC, SEG_XC, SEG_ZC = (
    0, 512, 1024, 1536, 1664, 2176, 2688, 3200, 3712)
D_IN = 4104
D_IN_PAD = 4224
F_END = 1544
GW_COL_TILE = 1408
WIN_ROWS = 1152
WIN_HALF = WIN_ROWS // 2
WIN_START = (0, 1024, 2160, 3072)
PIECE_A = 518
A_OFF = (0, 2, 12, 126)
B_OFF = (518, 640, 530, 644)
ADAM_LR = 0.001
ADAM_B1 = 0.9
ADAM_B2 = 0.999
ADAM_EPS = 1e-08
ADAM_WD = 0.01
ADAM_STEP = 10

MESH = pl.DeviceIdType.MESH
ANY = pl.BlockSpec(memory_space=pl.ANY)

PACK_ROWS = 32
SLOT_NORM = (0, 1, 0, 1024)
SLOT_FINAL = (1, 2, 0, 1024)
SLOT_ATTN = (2, 3, 0, 512)
SLOT_CONVG = (2, 3, 512, 1024)
SLOT_BF = (3, 4, 0, 8)
SLOT_META = (8, 24, 0, 256)
SLOT_CONVW = (24, 27, 0, 128)
LOSS_ROW = 4


def _params(sem=None):
    return pltpu.CompilerParams(dimension_semantics=sem, vmem_limit_bytes=VMEM_LIMIT_BYTES)


def _sigmoid(z):
    return 1.0 / (1.0 + jnp.exp(-z))


def _dot(a, b):
    return jnp.dot(a, b, preferred_element_type=F32)


def _dot_nt(a, b):
    return lax.dot_general(a, b, (((1,), (1,)), ((), ())), preferred_element_type=F32)


def _dot_exact(ones, x):
    ones = ones.astype(MXU_DTYPE)
    total = None
    for _ in range(3):
        term = x.astype(MXU_DTYPE)
        x = x - term.astype(F32)
        total = _dot(ones, term) if total is None else total + _dot(ones, term)
    return total


def _group_matrix():
    r = lax.broadcasted_iota(jnp.int32, (D_ATTN, D_ATTN), 0) >> 6
    c = lax.broadcasted_iota(jnp.int32, (D_ATTN, D_ATTN), 1) >> 6
    return jnp.where(r == c, 1.0, 0.0).astype(MXU_DTYPE)


def _group_sum(x, gmat):
    hi = x.astype(MXU_DTYPE)
    lo = (x - hi.astype(F32)).astype(MXU_DTYPE)
    return _dot(hi, gmat) + _dot(lo, gmat)


def _x_block_specs(n_sub, rows):
    specs = [pl.BlockSpec((rows, D_MODEL), lambda i: (jnp.maximum(n_sub * i - 1, 0), 0))]
    for b in range(1, n_sub):
        specs.append(pl.BlockSpec((rows, D_MODEL), functools.partial(lambda i, b: (n_sub * i - 1 + b, 0), b=b)))
    return specs


def _position():
    return lax.axis_index("x"), lax.axis_index("y"), lax.axis_index("c")


def _gather_weights(wi, wo, small):
    def body(wi_ref, wo_ref, sm_ref, gwi_ref, gwo_ref, gsm_ref, send_sems, recv_sems):
        x, y, c = _position()
        sibling = (x, y, 1 - c)
        chips = [(1 - x, y), (x, 1 - y), (1 - x, 1 - y)]

        def remote(k, src, dst, to):
            return pltpu.make_async_remote_copy(src_ref=src, dst_ref=dst, send_sem=send_sems.at[k],
                                                recv_sem=recv_sems.at[k], device_id=to, device_id_type=MESH)

        first, passed, landed = [], [], []
        for a, (src_ref, g_ref) in enumerate(((wi_ref, gwi_ref), (wo_ref, gwo_ref))):
            for j, (cx, cy) in enumerate(chips):
                slot = g_ref.at[j, c]
                first.append(remote(6 * a + j, src_ref.at[c], slot, (cx, cy, c)))
                landed.append(remote(6 * a + j, slot, slot, sibling))
                passed.append(remote(6 * a + 3 + j, slot, slot, sibling))
        for j, (cx, cy) in enumerate(chips):
            first.append(remote(12 + j, sm_ref, gsm_ref.at[j], (cx, cy, c)))
        for cp in first:
            cp.start()
        for arrived, onward in zip(landed, passed):
            arrived.wait_recv()
            onward.start()
        for a, g_ref in enumerate((gwi_ref, gwo_ref)):
            for j in range(3):
                remote(6 * a + 3 + j, g_ref.at[j, 1 - c], g_ref.at[j, 1 - c], sibling).wait_recv()
        for j in range(3):
            remote(12 + j, sm_ref, gsm_ref.at[j], sibling).wait_recv()
        for cp in first + passed:
            cp.wait_send()

    return pl.pallas_call(
        body, name="gather_weights",
        out_shape=(jax.ShapeDtypeStruct((3,) + wi.shape, wi.dtype), jax.ShapeDtypeStruct((3,) + wo.shape, wo.dtype),
                   jax.ShapeDtypeStruct((3,) + small.shape, small.dtype)),
        in_specs=[ANY, ANY, ANY], out_specs=(ANY, ANY, ANY),
        scratch_shapes=[pltpu.SemaphoreType.DMA((15,)), pltpu.SemaphoreType.DMA((15,))],
    )(wi, wo, small)


def _pair_exchange(gw, gb):
    def body(gw_ref, gb_ref, ra_ref, rb_ref, send_sems, recv_sems):
        x, y, c = _position()
        sibling = (x, y, 1 - c)
        copies = [pltpu.make_async_remote_copy(
            src_ref=gb_ref.at[:, 1 - c], dst_ref=rb_ref, send_sem=send_sems.at[N_CHIPS], recv_sem=recv_sems.at[N_CHIPS],
            device_id=sibling, device_id_type=MESH)]
        for s, start in enumerate(WIN_START):
            rows = pl.ds(pl.multiple_of(start + WIN_HALF * (1 - c), SUBLANE), WIN_HALF)
            copies.append(pltpu.make_async_remote_copy(
                src_ref=gw_ref.at[rows], dst_ref=ra_ref.at[s], send_sem=send_sems.at[s], recv_sem=recv_sems.at[s],
                device_id=sibling, device_id_type=MESH))
        for cp in copies:
            cp.start()
        for cp in copies:
            cp.wait()

    return pl.pallas_call(
        body, name="grad_pair_exchange",
        out_shape=(jax.ShapeDtypeStruct((N_CHIPS, WIN_HALF, D_MODEL), gw.dtype),
                   jax.ShapeDtypeStruct((N_CHIPS,) + gb.shape[2:], gb.dtype)),
        in_specs=[ANY, ANY], out_specs=(ANY, ANY),
        scratch_shapes=[pltpu.SemaphoreType.DMA((N_CHIPS + 1,)), pltpu.SemaphoreType.DMA((N_CHIPS + 1,))],
    )(gw, gb)


def _chip_exchange(pa, pb):
    def body(pa_ref, pb_ref, ra_ref, rb_ref, send_sems, recv_sems):
        x, y, c = _position()
        chips = [(1 - x, y), (x, 1 - y), (1 - x, 1 - y)]
        copies = []
        for a, (src, dst) in enumerate(((pa_ref, ra_ref), (pb_ref, rb_ref))):
            for j, (cx, cy) in enumerate(chips):
                copies.append(pltpu.make_async_remote_copy(
                    src_ref=src.at[2 * cx + cy], dst_ref=dst.at[j], send_sem=send_sems.at[3 * a + j],
                    recv_sem=recv_sems.at[3 * a + j], device_id=(cx, cy, c), device_id_type=MESH))
        for cp in copies:
            cp.start()
        for cp in copies:
            cp.wait()

    return pl.pallas_call(
        body, name="grad_chip_exchange",
        out_shape=(jax.ShapeDtypeStruct((3,) + pa.shape[1:], pa.dtype),
                   jax.ShapeDtypeStruct((3,) + pb.shape[1:], pb.dtype)),
        in_specs=[ANY, ANY], out_specs=(ANY, ANY),
        scratch_shapes=[pltpu.SemaphoreType.DMA((6,)), pltpu.SemaphoreType.DMA((6,))],
    )(pa, pb)


def _pair_share(ha, hb):
    def body(ha_ref, hb_ref, oa_ref, ob_ref, send_sems, recv_sems):
        x, y, c = _position()
        copies = [pltpu.make_async_remote_copy(
            src_ref=src, dst_ref=dst, send_sem=send_sems.at[k], recv_sem=recv_sems.at[k],
            device_id=(x, y, 1 - c), device_id_type=MESH)
            for k, (src, dst) in enumerate(((ha_ref, oa_ref), (hb_ref, ob_ref)))]
        for cp in copies:
            cp.start()
        for cp in copies:
            cp.wait()

    return pl.pallas_call(
        body, name="grad_pair_share",
        out_shape=(jax.ShapeDtypeStruct(ha.shape, ha.dtype), jax.ShapeDtypeStruct(hb.shape, hb.dtype)),
        in_specs=[ANY, ANY], out_specs=(ANY, ANY),
        scratch_shapes=[pltpu.SemaphoreType.DMA((2,)), pltpu.SemaphoreType.DMA((2,))],
    )(ha, hb)


def _gather_small(pack):
    def body(p_ref, o_ref, send_sems, recv_sems):
        x, y, c = _position()
        copies = []
        for mask in range(1, N_DEV):
            peer = (1 - x if mask & 4 else x, 1 - y if mask & 2 else y, 1 - c if mask & 1 else c)
            copies.append(pltpu.make_async_remote_copy(
                src_ref=p_ref, dst_ref=o_ref.at[mask - 1], send_sem=send_sems.at[mask - 1],
                recv_sem=recv_sems.at[mask - 1], device_id=peer, device_id_type=MESH))
        for cp in copies:
            cp.start()
        for cp in copies:
            cp.wait()

    return pl.pallas_call(
        body, name="gather_small",
        out_shape=jax.ShapeDtypeStruct((N_DEV - 1,) + pack.shape, pack.dtype),
        in_specs=[ANY], out_specs=ANY,
        scratch_shapes=[pltpu.SemaphoreType.DMA((N_DEV - 1,)), pltpu.SemaphoreType.DMA((N_DEV - 1,))],
    )(pack)


def _pair_sum(mine, recv, c_idx):
    rows, cols = mine.shape[2:]

    def body(c_ref, a_ref, b_ref, o_ref, send_ref):
        total = a_ref[...] + b_ref[...]
        o_ref[...] = total
        send_ref[...] = total.astype(send_ref.dtype)

    out_spec = pl.BlockSpec((None, rows, cols), lambda s, c_ref: (s, 0, 0))
    return pl.pallas_call(
        body, name="grad_pair_sum",
        grid_spec=pltpu.PrefetchScalarGridSpec(
            num_scalar_prefetch=1, grid=(N_CHIPS,),
            in_specs=[pl.BlockSpec((None, None, rows, cols), lambda s, c_ref: (s, c_ref[0], 0, 0)),
                      pl.BlockSpec((None, rows, cols), lambda s, c_ref: (s, 0, 0))],
            out_specs=(out_spec, out_spec)),
        out_shape=(jax.ShapeDtypeStruct(recv.shape, recv.dtype), jax.ShapeDtypeStruct(recv.shape, WIRE_DTYPE)),
        compiler_params=_params(("parallel",)),
    )(c_idx, mine, recv)


def _window_start(s):
    return jnp.where(s == 0, WIN_START[0], jnp.where(s == 1, WIN_START[1], jnp.where(s == 2, WIN_START[2], WIN_START[3])))


def _pair_sum_windows(gw, recv, c_idx):
    tr = WIN_HALF // 3

    def body(c_ref, a_ref, b_ref, o_ref, send_ref):
        total = a_ref[...] + b_ref[...]
        o_ref[...] = total
        send_ref[...] = total.astype(send_ref.dtype)

    out_spec = pl.BlockSpec((None, tr, D_MODEL), lambda s, i, c_ref: (s, i, 0))
    return pl.pallas_call(
        body, name="grad_pair_sum_windows",
        grid_spec=pltpu.PrefetchScalarGridSpec(
            num_scalar_prefetch=1, grid=(N_CHIPS, WIN_HALF // tr),
            in_specs=[pl.BlockSpec((pl.Element(tr), pl.Element(D_MODEL)),
                                   lambda s, i, c_ref: (pl.multiple_of(
                                       _window_start(s) + WIN_HALF * c_ref[0] + tr * i, SUBLANE), 0)),
                      pl.BlockSpec((None, tr, D_MODEL), lambda s, i, c_ref: (s, i, 0))],
            out_specs=(out_spec, out_spec)),
        out_shape=(jax.ShapeDtypeStruct(recv.shape, recv.dtype), jax.ShapeDtypeStruct(recv.shape, WIRE_DTYPE)),
        compiler_params=_params(("parallel", "parallel")),
    )(c_idx, gw, recv)


def _assemble_w(own, others, starts):
    def body(starts_ref, own_ref, oth_ref, o_ref):
        o_ref[...] = jnp.zeros_like(o_ref)
        for k in range(N_CHIPS):
            rows = pl.ds(pl.multiple_of(starts_ref[k], 2 * SUBLANE), WIN_ROWS)
            o_ref[rows, :] = o_ref[rows, :] + (own_ref[...] if k == 0 else oth_ref[k - 1])

    return pl.pallas_call(
        body, name="assemble_w",
        in_specs=[pl.BlockSpec(memory_space=pltpu.SMEM), pl.BlockSpec(memory_space=pltpu.VMEM),
                  pl.BlockSpec(memory_space=pltpu.VMEM)],
        out_specs=pl.BlockSpec(memory_space=pltpu.VMEM),
        out_shape=jax.ShapeDtypeStruct((D_IN_PAD, D_MODEL), own.dtype),
        compiler_params=_params(),
    )(starts, own, others)


def _chip_sum(psum, recv3, chip_idx):
    rows, cols = psum.shape[1:]
    tr = rows // 2

    def body(s_ref, p_ref, r0, r1, r2, o_ref):
        o_ref[...] = ((p_ref[...] + r0[...].astype(F32)) + r1[...].astype(F32)) + r2[...].astype(F32)

    return pl.pallas_call(
        body, name="grad_chip_sum",
        grid_spec=pltpu.PrefetchScalarGridSpec(
            num_scalar_prefetch=1, grid=(2,),
            in_specs=[pl.BlockSpec((None, tr, cols), lambda i, s_ref: (s_ref[0], i, 0))] +
                     [pl.BlockSpec((None, tr, cols), functools.partial(lambda i, s_ref, j: (j, i, 0), j=j))
                      for j in range(3)],
            out_specs=pl.BlockSpec((tr, cols), lambda i, s_ref: (i, 0))),
        out_shape=jax.ShapeDtypeStruct((rows, cols), psum.dtype),
        compiler_params=_params(("parallel",)),
    )(chip_idx, psum, recv3, recv3, recv3)


def _adamw_math(w, g, m, v):
    m = ADAM_B1 * m + (1.0 - ADAM_B1) * g
    v = ADAM_B2 * v + (1.0 - ADAM_B2) * (g * g)
    m_hat = m / (1.0 - ADAM_B1 ** ADAM_STEP)
    v_hat = v / (1.0 - ADAM_B2 ** ADAM_STEP)
    delta = -ADAM_LR * (m_hat / (jnp.sqrt(v_hat) + ADAM_EPS) + ADAM_WD * w)
    return delta, m, v


def _adamw_big(w, g, m, v, tr):
    rows, cols = w.shape
    assert rows % tr == 0 and g.shape[0] >= rows

    def body(w_ref, g_ref, m_ref, v_ref, d_out, m_out, v_out):
        d, m2, v2 = _adamw_math(w_ref[...], g_ref[...], m_ref[...], v_ref[...])
        d_out[...] = d
        m_out[...] = m2
        v_out[...] = v2

    spec = pl.BlockSpec((tr, cols), lambda i: (i, 0))
    sds = jax.ShapeDtypeStruct((rows, cols), F32)
    return pl.pallas_call(
        body, name="adamw_big", grid=(rows // tr,), in_specs=[spec] * 4, out_specs=(spec,) * 3,
        out_shape=(sds,) * 3, compiler_params=_params(("parallel",)),
    )(w, g, m, v)


def _adamw_rows(w3, g, m3, v3):
    rows, _, cols = w3.shape
    tc = 2 * LANE

    def body(w_ref, g_ref, m_ref, v_ref, g_out, d_out, m_out, v_out):
        g = g_ref[...]
        d, m2, v2 = _adamw_math(w_ref[:, 0, :], g, m_ref[:, 0, :], v_ref[:, 0, :])
        g_out[:, 0, :] = g
        d_out[:, 0, :] = d
        m_out[:, 0, :] = m2
        v_out[:, 0, :] = v2

    spec3 = pl.BlockSpec((rows, 1, tc), lambda i: (0, 0, i))
    sds = jax.ShapeDtypeStruct((rows, 1, cols), F32)
    return pl.pallas_call(
        body, name="adamw_rows", grid=(cols // tc,),
        in_specs=[spec3, pl.BlockSpec((rows, tc), lambda i: (0, i)), spec3, spec3], out_specs=(spec3,) * 4,
        out_shape=(sds,) * 4, compiler_params=_params(("parallel",)),
    )(w3, g, m3, v3)


def _small_update(own, others, params, ms, vs):
    slots = (SLOT_NORM, SLOT_FINAL, SLOT_ATTN, SLOT_CONVG, SLOT_BF, SLOT_META, SLOT_CONVW)
    n = len(slots)

    def body(*refs):
        own_ref, gp_ref = refs[:2]
        refs = refs[1:]
        w_refs, m_refs, v_refs = refs[1:1 + n], refs[1 + n:1 + 2 * n], refs[1 + 2 * n:1 + 3 * n]
        outs = refs[1 + 3 * n:2 + 7 * n]
        loss_ref = outs[0]
        g_outs, d_outs, m_outs, v_outs = (outs[1 + k * n:1 + (k + 1) * n] for k in range(4))
        g_scr, w_scr, m_scr, v_scr = refs[2 + 7 * n:]
        x, y, c = _position()
        shard = 2 * x + y
        me = 4 * x + 2 * y + c
        tot = None
        for d in range(N_DEV):
            rel = jnp.bitwise_xor(me, d)
            term = jnp.where(rel == 0, own_ref[...], gp_ref[jnp.maximum(rel, 1) - 1])
            tot = term if tot is None else tot + term
        r0, r1, _, _ = SLOT_META
        meta_sel = tot[r0:r1, 0:256]
        cw_sel = tot[24:32, 0:128]
        for k in range(1, N_CHIPS):
            meta_sel = jnp.where(shard == k, tot[r0:r1, 256 * k:256 * (k + 1)], meta_sel)
            cw_sel = jnp.where(shard == k, tot[24:32, 128 * k:128 * (k + 1)], cw_sel)
        zeros = jnp.zeros((PACK_ROWS, D_MODEL), F32)
        for scr in (g_scr, w_scr, m_scr, v_scr):
            scr[...] = zeros
        g_scr[0:8, :] = tot[0:8, :]
        g_scr[r0:r1, 0:256] = meta_sel
        g_scr[24:32, 0:128] = cw_sel
        for (a, b, c0, c1), w_ref, m_ref, v_ref in zip(slots, w_refs, m_refs, v_refs):
            w_scr[a:b, c0:c1] = w_ref[...]
            m_scr[a:b, c0:c1] = m_ref[...]
            v_scr[a:b, c0:c1] = v_ref[...]
        loss_ref[...] = g_scr[LOSS_ROW:LOSS_ROW + 1, 0:1]
        d, m2, v2 = _adamw_math(w_scr[...], g_scr[...], m_scr[...], v_scr[...])
        w_scr[...] = d
        m_scr[...] = m2
        v_scr[...] = v2
        for (a, b, c0, c1), g_o, d_o, m_o, v_o in zip(slots, g_outs, d_outs, m_outs, v_outs):
            g_o[...] = g_scr[a:b, c0:c1]
            d_o[...] = w_scr[a:b, c0:c1]
            m_o[...] = m_scr[a:b, c0:c1]
            v_o[...] = v_scr[a:b, c0:c1]

    shapes = [jax.ShapeDtypeStruct(p.shape, F32) for p in params]
    out = pl.pallas_call(
        body, name="small_update",
        out_shape=[jax.ShapeDtypeStruct((1, 1), F32)] + shapes * 4,
        scratch_shapes=[pltpu.VMEM((PACK_ROWS, D_MODEL), F32)] * 4,
        compiler_params=_params(),
    )(own, others, *params, *ms, *vs)
    return out[0], out[1:1 + n], out[1 + n:1 + 2 * n], out[1 + 2 * n:1 + 3 * n], out[1 + 3 * n:1 + 4 * n]


def _in_proj(x2, meta_blk, norm_g, w_pad, bf_pad):
    seq = x2.shape[0]
    lp = seq + FRONT
    t = ROW_TILE
    nt = lp // t
    n_sub = t // LANE

    def body(*refs):
        x_refs = refs[:n_sub]
        mb, g_ref, w_ref, bf_ref = refs[n_sub:n_sub + 4]
        q_ref, k_ref, v_ref, rest_ref, fl_ref, ct_ref, u_ref, carry = refs[n_sub + 4:]
        i = pl.program_id(0)

        @pl.when(i == 0)
        def _():
            carry[...] = jnp.zeros_like(carry)

        first = jnp.where(i == 0, mb[...], x_refs[0][...])
        h = jnp.concatenate([first] + [r[...] for r in x_refs[1:]], axis=0)
        ms = jnp.mean(h * h, axis=-1, keepdims=True)
        u = ((h * lax.rsqrt(ms + EPS)) * g_ref[...]).astype(MXU_DTYPE)
        u_ref[...] = u

        def seg(a, width):
            return _dot_nt(u, w_ref[a:a + width, :])

        q_ref[...] = (seg(SEG_Q, D_ATTN) * (HEAD_DIM ** -0.5)).astype(MXU_DTYPE)
        k_ref[...] = seg(SEG_K, D_ATTN).astype(MXU_DTYPE)
        v_ref[...] = seg(SEG_V, D_ATTN).astype(MXU_DTYPE)
        for s in range(5):
            rest_ref[:, 512 * s:512 * (s + 1)] = seg(SEG_ZA + 512 * s, 512)
        fl = seg(SEG_F, LANE)
        fl_ref[...] = fl
        z = fl + bf_ref[...]
        logf = jnp.minimum(z, 0.0) - jnp.log(1.0 + jnp.exp(-jnp.abs(z)))
        row = i * t + lax.broadcasted_iota(jnp.int32, (t, LANE), 0)
        logf = jnp.where(row >= PAD_ROWS, logf, 0.0)
        tri = jnp.where(lax.broadcasted_iota(jnp.int32, (t, t), 0) >= lax.broadcasted_iota(jnp.int32, (t, t), 1),
                        1.0, 0.0)
        cs = _dot_exact(tri, logf) + carry[...]
        carry[...] = carry[...] + jnp.sum(logf, axis=0, keepdims=True)
        col = i * t + lax.broadcasted_iota(jnp.int32, (SUBLANE, t), 1)
        ct_ref[...] = jnp.where(col >= PAD_ROWS, cs.T[0:SUBLANE, :], -NEG)

    row_blk = lambda cols: pl.BlockSpec((t, cols), lambda i: (i, 0))
    const = lambda shape: pl.BlockSpec(shape, lambda i: (0, 0))
    return pl.pallas_call(
        body, name="in_proj", grid=(nt,),
        in_specs=_x_block_specs(n_sub, LANE) + [const((LANE, D_MODEL)), const((1, D_MODEL)),
                                                pl.BlockSpec((D_IN_PAD, D_MODEL), lambda i: (0, 0),
                                                             pipeline_mode=pl.Buffered(1)),
                                                const((1, LANE))],
        out_specs=(row_blk(D_ATTN), row_blk(D_ATTN), row_blk(D_ATTN), row_blk(5 * 512), row_blk(LANE),
                   pl.BlockSpec((SUBLANE, t), lambda i: (0, i)), row_blk(D_MODEL)),
        out_shape=(jax.ShapeDtypeStruct((lp, D_ATTN), MXU_DTYPE), jax.ShapeDtypeStruct((lp, D_ATTN), MXU_DTYPE),
                   jax.ShapeDtypeStruct((lp, D_ATTN), MXU_DTYPE), jax.ShapeDtypeStruct((lp, 5 * 512), F32),
                   jax.ShapeDtypeStruct((lp, LANE), F32),
                   jax.ShapeDtypeStruct((SUBLANE, lp), F32), jax.ShapeDtypeStruct((lp, D_MODEL), MXU_DTYPE)),
        scratch_shapes=[pltpu.VMEM((1, LANE), F32)],
        compiler_params=_params(("arbitrary",)),
    )(*([x2] * n_sub), meta_blk, norm_g, w_pad, bf_pad)


def _head_masks():
    lane = lax.broadcasted_iota(jnp.int32, (1, LANE), 1)
    return lane < HEAD_DIM, lane >= HEAD_DIM


def _pair_specs(lp, nt, t):
    blk = pl.BlockSpec((lp, LANE), lambda g: (0, g))
    ct_a = pl.BlockSpec((None, nt, 1, t), lambda g: (2 * g, 0, 0, 0))
    ct_b = pl.BlockSpec((None, nt, 1, t), lambda g: (2 * g + 1, 0, 0, 0))
    return blk, ct_a, ct_b


def _sub_rows(s, col):
    return jnp.concatenate([s[:, a * LANE:(a + 1) * LANE] - col for a in range(s.shape[1] // LANE)], axis=1)


def _loop_unrolled(lo, hi, step, init, n):
    def group(jj, carry):
        for k in range(n):
            carry = step(lo + n * jj + k, carry)
        return carry

    groups = (hi - lo) // n
    carry = lax.fori_loop(0, groups, group, init)
    return lax.fori_loop(lo + n * groups, hi, step, carry)


def _lane_chunks(s):
    return [s[:, a * LANE:(a + 1) * LANE] for a in range(s.shape[1] // LANE)]


def _attn_fwd(q, k, v, ct4):
    lp = q.shape[0]
    t = ROW_TILE
    nt = lp // t

    def body(q_ref, k_ref, v_ref, cta_ref, ctb_ref, o_ref, l_ref, m_ref, s_scr):
        masks = _head_masks()
        ct_refs = (cta_ref, ctb_ref)
        below = lax.broadcasted_iota(jnp.int32, (t, t), 1) <= lax.broadcasted_iota(jnp.int32, (t, t), 0)
        lane = lax.broadcasted_iota(jnp.int32, (1, LANE), 1)
        head_of_row = lax.broadcasted_iota(jnp.int32, (2 * t, LANE), 0) >= t
        ones_cols = jnp.where(lax.broadcasted_iota(jnp.int32, (2 * t, LANE), 1) == head_of_row.astype(jnp.int32),
                              1.0, 0.0).astype(MXU_DTYPE)

        def q_block(i, _):
            r0 = pl.multiple_of(i * t, t)
            qi = q_ref[pl.ds(r0, t), :]

            def scores(j):
                kj = k_ref[pl.ds(pl.multiple_of(j * t, t), t), :]
                return _dot_nt(qi, jnp.concatenate([jnp.where(hm, kj, 0).astype(MXU_DTYPE) for hm in masks], axis=0))

            def biased(j, hh, s2, diagonal):
                s = (s2[:, hh * t:(hh + 1) * t] - ct_refs[hh][j]) * LOG2E
                return jnp.where(below, s, NEG) if diagonal else s

            def max_step(j, carry, diagonal):
                s2 = scores(j)
                out = []
                for hh, m in enumerate(carry):
                    s = biased(j, hh, s2, diagonal)
                    s_scr[j, :, hh * t:(hh + 1) * t] = s
                    for c in _lane_chunks(s):
                        m = jnp.maximum(m, c)
                    out.append(m)
                return tuple(out)

            lanes_neg = jnp.full((t, LANE), NEG, F32)
            carry = _loop_unrolled(0, i, functools.partial(max_step, diagonal=False), (lanes_neg, lanes_neg),
                                   ATTN_UNROLL)
            ms = [jnp.max(m, axis=-1, keepdims=True) for m in max_step(i, carry, True)]

            def sum_step(j, acc):
                vj = v_ref[pl.ds(pl.multiple_of(j * t, t), t), :]
                v2 = jnp.concatenate([jnp.where(hm, vj, 0).astype(MXU_DTYPE) for hm in masks], axis=0)
                parts = [jnp.exp2(s_scr[j, :, hh * t:(hh + 1) * t] - ms[hh]).astype(MXU_DTYPE) for hh in range(2)]
                return acc + _dot(jnp.concatenate(parts, axis=1), jnp.concatenate([v2, ones_cols], axis=1))

            acc = _loop_unrolled(0, i + 1, sum_step, jnp.zeros((t, 2 * LANE), F32), ATTN_UNROLL)
            sums = acc[:, LANE:]
            l_pair = jnp.where(masks[0], jnp.sum(jnp.where(lane == 0, sums, 0.0), axis=-1, keepdims=True),
                               jnp.sum(jnp.where(lane == 1, sums, 0.0), axis=-1, keepdims=True))
            o_ref[pl.ds(r0, t), :] = acc[:, :LANE] / l_pair
            l_ref[pl.ds(r0, t), :] = l_pair
            m_ref[pl.ds(r0, t), 0:LANE] = jnp.broadcast_to(ms[0], (t, LANE))
            m_ref[pl.ds(r0, t), LANE:2 * LANE] = jnp.broadcast_to(ms[1], (t, LANE))
            return 0

        lax.fori_loop(0, nt, q_block, 0)

    blk, ct_a, ct_b = _pair_specs(lp, nt, t)
    return pl.pallas_call(
        body, name="attn_fwd", grid=(HEADS // 2,),
        in_specs=[blk, blk, blk, ct_a, ct_b], out_specs=(blk, blk, pl.BlockSpec((lp, 2 * LANE), lambda g: (0, g))),
        out_shape=(jax.ShapeDtypeStruct((lp, D_ATTN), F32), jax.ShapeDtypeStruct((lp, D_ATTN), F32),
                   jax.ShapeDtypeStruct((lp, HEADS * LANE), F32)),
        scratch_shapes=[pltpu.VMEM((nt, t, 2 * t), F32)],
        compiler_params=_params(("parallel",)),
    )(q, k, v, ct4, ct4)


def _attn_bwd(q, k, v, do, m, delta, ct4):
    lp = q.shape[0]
    t = ROW_TILE
    nt = lp // t

    def body(q_ref, k_ref, v_ref, do_ref, ma_ref, mb_ref, dla_ref, dlb_ref, cta_ref, ctb_ref,
             dq_ref, dk_ref, dv_ref, dc_ref, dq_acc, dk_acc, dv_acc):
        masks = _head_masks()
        ct_refs, m_refs, dl_refs = (cta_ref, ctb_ref), (ma_ref, mb_ref), (dla_ref, dlb_ref)
        below = lax.broadcasted_iota(jnp.int32, (t, t), 1) <= lax.broadcasted_iota(jnp.int32, (t, t), 0)
        tn = (((0,), (0,)), ((), ()))
        dq_acc[...] = jnp.zeros_like(dq_acc)

        def k_block(j, _):
            c0 = pl.multiple_of(j * t, t)
            kj = k_ref[pl.ds(c0, t), :]
            vj = v_ref[pl.ds(c0, t), :]
            k2 = jnp.concatenate([jnp.where(hm, kj, 0).astype(MXU_DTYPE) for hm in masks], axis=0)
            v2 = jnp.concatenate([jnp.where(hm, vj, 0).astype(MXU_DTYPE) for hm in masks], axis=0)
            ck = [r[j] for r in ct_refs]
            dk_acc[...] = jnp.zeros_like(dk_acc)
            dv_acc[...] = jnp.zeros_like(dv_acc)

            def q_block(i, colsums, diagonal):
                r0 = pl.multiple_of(i * t, t)
                qi = q_ref[pl.ds(r0, t), :]
                doi = do_ref[pl.ds(r0, t), :]
                q2 = jnp.concatenate([jnp.where(hm, qi, 0).astype(MXU_DTYPE) for hm in masks], axis=0)
                do2 = jnp.concatenate([jnp.where(hm, doi, 0).astype(MXU_DTYPE) for hm in masks], axis=0)
                s2 = _dot_nt(qi, k2)
                dp2 = _dot_nt(doi, v2)
                out, ps, dss = [], [], []
                for hh in range(2):
                    s = (s2[:, hh * t:(hh + 1) * t] - ck[hh]) * LOG2E
                    if diagonal:
                        s = jnp.where(below, s, NEG)
                    p = jnp.exp2(_sub_rows(s, m_refs[hh][pl.ds(r0, t), :])).astype(MXU_DTYPE)
                    ds32 = p.astype(F32) * _sub_rows(dp2[:, hh * t:(hh + 1) * t], dl_refs[hh][pl.ds(r0, t), :])
                    ps.append(p)
                    dss.append(ds32.astype(MXU_DTYPE))
                    out.append(colsums[hh] + jnp.sum(ds32, axis=0, keepdims=True))
                dv_acc[...] = dv_acc[...] + lax.dot_general(jnp.concatenate(ps, axis=0), do2, tn,
                                                            preferred_element_type=F32)
                dk_acc[...] = dk_acc[...] + lax.dot_general(jnp.concatenate(dss, axis=0), q2, tn,
                                                            preferred_element_type=F32)
                dq_acc[pl.ds(r0, t), :] = dq_acc[pl.ds(r0, t), :] + _dot(jnp.concatenate(dss, axis=1), k2)
                return tuple(out)

            colsums = q_block(j, (jnp.zeros((1, t), F32), jnp.zeros((1, t), F32)), True)
            colsums = lax.fori_loop(j + 1, nt, functools.partial(q_block, diagonal=False), colsums)
            for hh in range(2):
                dc_ref[hh, j] = -colsums[hh]
            dk_ref[pl.ds(c0, t), :] = dk_acc[...].astype(dk_ref.dtype)
            dv_ref[pl.ds(c0, t), :] = dv_acc[...].astype(dv_ref.dtype)
            return 0

        lax.fori_loop(0, nt, k_block, 0)
        dq_ref[...] = (dq_acc[...] * (HEAD_DIM ** -0.5)).astype(dq_ref.dtype)

    blk, ct_a, ct_b = _pair_specs(lp, nt, t)
    rep_a = pl.BlockSpec((lp, LANE), lambda g: (0, 2 * g))
    rep_b = pl.BlockSpec((lp, LANE), lambda g: (0, 2 * g + 1))
    return pl.pallas_call(
        body, name="attn_bwd", grid=(HEADS // 2,),
        in_specs=[blk] * 4 + [rep_a, rep_b, rep_a, rep_b, ct_a, ct_b],
        out_specs=(blk, blk, blk, pl.BlockSpec((2, nt, 1, t), lambda g: (g, 0, 0, 0))),
        out_shape=(jax.ShapeDtypeStruct((lp, D_ATTN), MXU_DTYPE),) * 3
                  + (jax.ShapeDtypeStruct((HEADS, nt, 1, t), F32),),
        scratch_shapes=[pltpu.VMEM((lp, LANE), F32), pltpu.VMEM((t, LANE), F32), pltpu.VMEM((t, LANE), F32)],
        compiler_params=_params(("parallel",)),
    )(q, k, v, do, m, m, delta, delta, ct4, ct4)


def _shift_down(prev8, cur, k):
    ext = jnp.concatenate([prev8, cur], axis=0)
    return pltpu.roll(ext, k, 0)[SUBLANE:, :]


def _shift_up(cur, next8, k):
    ext = jnp.concatenate([cur, next8], axis=0)
    n = ext.shape[0]
    return pltpu.roll(ext, n - k, 0)[:cur.shape[0], :]


def _post(o, l_sum, rest, x2, meta_blk, tgt2, w_out, attn_g, conv_g, final_g, conv_w8):
    lp = o.shape[0]
    t = ROW_TILE
    nt = lp // t
    n_sub = t // LANE
    hb = t // SUBLANE

    def body(*refs):
        o_ref, l_ref, za_ref, gb_ref, gc_ref, xc_ref, zc_ref, gch_ref, xch_ref = refs[:9]
        refs = refs[1:]
        x_refs = refs[8:8 + n_sub]
        mb = refs[8 + n_sub]
        t_refs = refs[9 + n_sub:9 + 2 * n_sub]
        wo_ref, ag_ref, cg_ref, fg_ref, cw_ref = refs[9 + 2 * n_sub:14 + 2 * n_sub]
        (dout_ref, do_ref, dl_ref, dza_ref, dgb_ref, dzc_ref, dcv_ref,
         loss_ref, gf_ref, gag_ref, gcg_ref, gwo_ref) = refs[14 + 2 * n_sub:]
        i = pl.program_id(0)

        @pl.when(i == 0)
        def _():
            for r in (loss_ref, gf_ref, gag_ref, gcg_ref, gwo_ref):
                r[...] = jnp.zeros_like(r)

        gmat = _group_matrix()
        inv_g = 1.0 / HEAD_DIM
        o_v = o_ref[...]
        ra = lax.rsqrt(_group_sum(o_v * o_v, gmat) * inv_g + EPS)
        n_a = o_v * ra
        a_n = n_a * ag_ref[...]
        za = za_ref[...]
        sig_a = _sigmoid(za)
        sz_a = za * sig_a
        y_a = a_n * sz_a
        gb = gb_ref[...]
        gc = gc_ref[...]
        xc = xc_ref[...]
        cx = gc * xc
        cx_prev = jnp.where(i == 0, 0.0, gch_ref[...] * xch_ref[...])
        conv = (cw_ref[0:1, :] * _shift_down(cx_prev, cx, 2) + cw_ref[1:2, :] * _shift_down(cx_prev, cx, 1)
                + cw_ref[2:3, :] * cx)
        e = gb * conv
        re = lax.rsqrt(_group_sum(e * e, gmat) * inv_g + EPS)
        n_e = e * re
        e_n = n_e * cg_ref[...]
        zc = zc_ref[...]
        sig_c = _sigmoid(zc)
        sz_c = zc * sig_c
        y_c = e_n * sz_c
        mix = jnp.concatenate([y_a, y_c], axis=-1)
        mix_b = mix.astype(MXU_DTYPE)
        first = jnp.where(i == 0, mb[...], x_refs[0][...])
        h = jnp.concatenate([first] + [r[...] for r in x_refs[1:]], axis=0)
        out = h + _dot(mix_b, wo_ref[...])
        r2 = lax.rsqrt(jnp.mean(out * out, axis=-1, keepdims=True) + EPS)
        n_f = out * r2
        y = n_f * fg_ref[...]
        tgt = jnp.concatenate([r[...] for r in t_refs], axis=0)
        valid = (i * t + lax.broadcasted_iota(jnp.int32, (t, 1), 0)) >= FRONT
        diff = jnp.where(valid, y - tgt, 0.0)
        loss_ref[...] = loss_ref[...] + 0.5 * jnp.sum(jnp.sum(diff * diff, axis=-1, keepdims=True) * (1.0 / D_MODEL))
        dy = diff * (1.0 / D_MODEL)
        gf_ref[...] = gf_ref[...] + jnp.sum(dy * n_f, axis=0, keepdims=True)
        dn = dy * fg_ref[...]
        d_out = r2 * (dn - n_f * jnp.mean(dn * n_f, axis=-1, keepdims=True))
        dout_ref[...] = d_out
        d_out_b = d_out.astype(MXU_DTYPE)
        d_mix = _dot_nt(d_out_b, wo_ref[...])
        gwo_ref[...] = gwo_ref[...] + _dot(mix.T.astype(MXU_DTYPE), d_out_b)
        d_ya = d_mix[:, :D_ATTN]
        d_yc = d_mix[:, D_ATTN:]
        d_an = d_ya * sz_a
        dza_ref[...] = (d_ya * a_n * (sig_a * (1.0 + za * (1.0 - sig_a)))).astype(dza_ref.dtype)
        gag_ref[...] = gag_ref[...] + jnp.sum(d_an * n_a, axis=0, keepdims=True)
        dn_a = d_an * ag_ref[...]
        d_o = ra * (dn_a - n_a * (_group_sum(dn_a * n_a, gmat) * inv_g))
        d_o_b = (d_o / l_ref[...]).astype(do_ref.dtype)
        do_ref[...] = d_o_b
        head_rep = jnp.where((lax.broadcasted_iota(jnp.int32, (D_ATTN, HEADS * LANE), 0) >> 6)
                             == (lax.broadcasted_iota(jnp.int32, (D_ATTN, HEADS * LANE), 1) >> 7), 1.0, 0.0)
        dl_ref[...] = _group_sum(d_o_b.astype(F32) * o_v, head_rep.astype(MXU_DTYPE))
        d_en = d_yc * sz_c
        dzc_ref[...] = (d_yc * e_n * (sig_c * (1.0 + zc * (1.0 - sig_c)))).astype(dzc_ref.dtype)
        gcg_ref[...] = gcg_ref[...] + jnp.sum(d_en * n_e, axis=0, keepdims=True)
        dn_e = d_en * cg_ref[...]
        d_e = re * (dn_e - n_e * (_group_sum(dn_e * n_e, gmat) * inv_g))
        dgb_ref[...] = (d_e * conv).astype(dgb_ref.dtype)
        dcv_ref[...] = d_e * gb

    row_blk = lambda cols: pl.BlockSpec((t, cols), lambda i: (i, 0))
    rest_blk = lambda s: pl.BlockSpec((t, 512), functools.partial(lambda i, s: (i, s), s=s))
    halo = lambda s: pl.BlockSpec((SUBLANE, 512), functools.partial(lambda i, s: (jnp.maximum(i * hb - 1, 0), s), s=s))
    const = lambda shape: pl.BlockSpec(shape, lambda i: (0, 0))
    acc = lambda shape: pl.BlockSpec(shape, lambda i: (0, 0))
    return pl.pallas_call(
        body, name="post_fwd_bwd", grid=(nt,),
        in_specs=[row_blk(D_ATTN), row_blk(D_ATTN)] + [rest_blk(s) for s in range(5)] + [halo(2), halo(3)]
                 + _x_block_specs(n_sub, LANE) + [const((LANE, D_MODEL))] + _x_block_specs(n_sub, LANE)
                 + [const((D_MODEL, D_MODEL)), const((1, D_ATTN)), const((1, D_CONV)), const((1, D_MODEL)),
                    const((SUBLANE, D_CONV))],
        out_specs=(row_blk(D_MODEL), row_blk(D_ATTN), row_blk(HEADS * LANE), row_blk(D_ATTN), row_blk(D_CONV),
                   row_blk(D_CONV), row_blk(D_CONV),
                   acc((1, LANE)), acc((1, D_MODEL)), acc((1, D_ATTN)), acc((1, D_CONV)), acc((D_MODEL, D_MODEL))),
        out_shape=(jax.ShapeDtypeStruct((lp, D_MODEL), F32), jax.ShapeDtypeStruct((lp, D_ATTN), MXU_DTYPE),
                   jax.ShapeDtypeStruct((lp, HEADS * LANE), F32), jax.ShapeDtypeStruct((lp, D_ATTN), MXU_DTYPE),
                   jax.ShapeDtypeStruct((lp, D_CONV), MXU_DTYPE), jax.ShapeDtypeStruct((lp, D_CONV), MXU_DTYPE),
                   jax.ShapeDtypeStruct((lp, D_CONV), F32),
                   jax.ShapeDtypeStruct((1, LANE), F32), jax.ShapeDtypeStruct((1, D_MODEL), F32),
                   jax.ShapeDtypeStruct((1, D_ATTN), F32), jax.ShapeDtypeStruct((1, D_CONV), F32),
                   jax.ShapeDtypeStruct((D_MODEL, D_MODEL), F32)),
        compiler_params=_params(("arbitrary",)),
    )(o, l_sum, *([rest] * 5), rest, rest, *([x2] * n_sub), meta_blk, *([tgt2] * n_sub),
      w_out, attn_g, conv_g, final_g, conv_w8)


def _bwd_in(x2, meta_blk, norm_g, w_pad, bf_pad, fl, dc, dq, dk, dv, dza, dgb, dzc, dconv, rest, d_out, conv_w8):
    lp = fl.shape[0]
    t = ROW_TILE
    nt = lp // t
    n_sub = t // LANE
    hb = t // SUBLANE
    rev = lambda i: nt - 1 - i

    def body(*refs):
        x_refs = refs[:n_sub]
        (mb, g_ref, w_ref, bf_ref, fl_ref, dc_ref, dq_ref, dk_ref, dv_ref, dza_ref, dgb_ref, dzc_ref,
         dcv_ref, dcvn_ref, gc_ref, xc_ref, gch_ref, xch_ref, dout_ref, cw_ref) = refs[n_sub:n_sub + 20]
        dp_ref, gx_ref, front_ref, gn_ref, gbf_ref, gcw_ref, carry, dh_scr, gx_sems = refs[n_sub + 20:]
        step = pl.program_id(0)
        i = rev(step)

        @pl.when(step == 0)
        def _():
            for r in (gn_ref, gbf_ref, gcw_ref, carry):
                r[...] = jnp.zeros_like(r)

        dc8 = jnp.concatenate([dc_ref[...], jnp.zeros((LANE - HEADS, t), F32)], axis=0).T
        triu = jnp.where(lax.broadcasted_iota(jnp.int32, (t, t), 1) >= lax.broadcasted_iota(jnp.int32, (t, t), 0),
                         1.0, 0.0)
        dlogf = _dot_exact(triu, dc8) + carry[...]
        carry[...] = carry[...] + jnp.sum(dc8, axis=0, keepdims=True)
        z = fl_ref[...] + bf_ref[...]
        row = i * t + lax.broadcasted_iota(jnp.int32, (t, LANE), 0)
        d_f = jnp.where(row >= PAD_ROWS, dlogf * (1.0 / (1.0 + jnp.exp(z))), 0.0)
        gbf_ref[...] = gbf_ref[...] + jnp.sum(d_f, axis=0, keepdims=True)
        dcv = dcv_ref[...]
        dcv_next = jnp.where(i == nt - 1, 0.0, dcvn_ref[...])
        d_cx = (cw_ref[2:3, :] * dcv + cw_ref[1:2, :] * _shift_up(dcv, dcv_next, 1)
                + cw_ref[0:1, :] * _shift_up(dcv, dcv_next, 2))
        gc = gc_ref[...]
        xc = xc_ref[...]
        cx = gc * xc
        cx_prev = jnp.where(i == 0, 0.0, gch_ref[...] * xch_ref[...])
        rowi = lax.broadcasted_iota(jnp.int32, (SUBLANE, 1), 0)
        gcw = (jnp.where(rowi == 0, jnp.sum(dcv * _shift_down(cx_prev, cx, 2), axis=0, keepdims=True), 0.0)
               + jnp.where(rowi == 1, jnp.sum(dcv * _shift_down(cx_prev, cx, 1), axis=0, keepdims=True), 0.0)
               + jnp.where(rowi == 2, jnp.sum(dcv * cx, axis=0, keepdims=True), 0.0))
        gcw_ref[...] = gcw_ref[...] + gcw
        dp_ref[:, SEG_Q:SEG_Q + 512] = dq_ref[...]
        dp_ref[:, SEG_K:SEG_K + 512] = dk_ref[...]
        dp_ref[:, SEG_V:SEG_V + 512] = dv_ref[...]
        dp_ref[:, SEG_F:SEG_F + LANE] = d_f.astype(dp_ref.dtype)
        dp_ref[:, SEG_ZA:SEG_ZA + 512] = dza_ref[...]
        dp_ref[:, SEG_GB:SEG_GB + 512] = dgb_ref[...]
        dp_ref[:, SEG_GC:SEG_GC + 512] = (d_cx * xc).astype(dp_ref.dtype)
        dp_ref[:, SEG_XC:SEG_XC + 512] = (d_cx * gc).astype(dp_ref.dtype)
        dp_ref[:, SEG_ZC:SEG_ZC + 512] = dzc_ref[...]
        d_u = _dot(dp_ref[...], w_ref[...])
        first = jnp.where(i == 0, mb[...], x_refs[0][...])
        h = jnp.concatenate([first] + [r[...] for r in x_refs[1:]], axis=0)
        r1 = lax.rsqrt(jnp.mean(h * h, axis=-1, keepdims=True) + EPS)
        n_h = h * r1
        gn_ref[...] = gn_ref[...] + jnp.sum(d_u * n_h, axis=0, keepdims=True)
        dn = d_u * g_ref[...]
        d_h = dout_ref[...] + r1 * (dn - n_h * jnp.mean(dn * n_h, axis=-1, keepdims=True))
        slot = step % 2

        def to_grad_x(slot_, tile):
            return pltpu.make_async_copy(dh_scr.at[slot_], gx_ref.at[pl.ds(pl.multiple_of(tile * t - FRONT, SUBLANE), t)],
                                         gx_sems.at[slot_])

        @pl.when(step >= 2)
        def _():
            to_grad_x(slot, 1).wait()

        dh_scr[slot] = d_h

        @pl.when(i > 0)
        def _():
            to_grad_x(slot, i).start()

        @pl.when(i == 0)
        def _():
            front_ref[...] = d_h[:FRONT]
            rest_rows = pltpu.make_async_copy(dh_scr.at[slot, pl.ds(FRONT, t - FRONT)], gx_ref.at[pl.ds(0, t - FRONT)],
                                              gx_sems.at[slot])
            rest_rows.start()
            rest_rows.wait()
            if nt >= 2:
                to_grad_x(1 - slot, 1).wait()

    def x_specs():
        specs = [pl.BlockSpec((LANE, D_MODEL), lambda s: (jnp.maximum(n_sub * rev(s) - 1, 0), 0))]
        for b in range(1, n_sub):
            specs.append(pl.BlockSpec((LANE, D_MODEL), functools.partial(lambda s, b: (n_sub * rev(s) - 1 + b, 0), b=b)))
        return specs

    row_blk = lambda cols: pl.BlockSpec((t, cols), lambda s: (rev(s), 0))
    rest_blk = lambda k: pl.BlockSpec((t, 512), functools.partial(lambda s, k: (rev(s), k), k=k))
    halo_prev = lambda k: pl.BlockSpec(
        (SUBLANE, 512), functools.partial(lambda s, k: (jnp.maximum(rev(s) * hb - 1, 0), k), k=k))
    halo_next = pl.BlockSpec((SUBLANE, 512), lambda s: (jnp.minimum((rev(s) + 1) * hb, lp // SUBLANE - 1), 0))
    const = lambda shape: pl.BlockSpec(shape, lambda s: (0, 0))
    return pl.pallas_call(
        body, name="bwd_in", grid=(nt,),
        in_specs=x_specs() + [const((LANE, D_MODEL)), const((1, D_MODEL)),
                              pl.BlockSpec((D_IN_PAD, D_MODEL), lambda s: (0, 0), pipeline_mode=pl.Buffered(1)),
                              const((1, LANE)), row_blk(LANE),
                              pl.BlockSpec((HEADS, t), lambda s: (0, rev(s))),
                              row_blk(512), row_blk(512), row_blk(512), row_blk(512), row_blk(512), row_blk(512),
                              row_blk(512), halo_next, rest_blk(2), rest_blk(3), halo_prev(2), halo_prev(3),
                              row_blk(D_MODEL), const((SUBLANE, D_CONV))],
        out_specs=(row_blk(D_IN_PAD), ANY, const((FRONT, D_MODEL)), const((1, D_MODEL)), const((1, LANE)),
                   const((SUBLANE, D_CONV))),
        out_shape=(jax.ShapeDtypeStruct((lp, D_IN_PAD), MXU_DTYPE), jax.ShapeDtypeStruct((lp - FRONT, D_MODEL), F32),
                   jax.ShapeDtypeStruct((FRONT, D_MODEL), F32),
                   jax.ShapeDtypeStruct((1, D_MODEL), F32), jax.ShapeDtypeStruct((1, LANE), F32),
                   jax.ShapeDtypeStruct((SUBLANE, D_CONV), F32)),
        scratch_shapes=[pltpu.VMEM((1, LANE), F32), pltpu.VMEM((2, t, D_MODEL), F32), pltpu.SemaphoreType.DMA((2,))],
        compiler_params=_params(("arbitrary",)),
    )(*([x2] * n_sub), meta_blk, norm_g, w_pad, bf_pad, fl, dc, dq, dk, dv, dza, dgb, dzc, dconv, dconv,
      rest, rest, rest, rest, d_out, conv_w8)


def _grad_w_in(u, dproj):
    lp = u.shape[0]
    tn = GW_COL_TILE
    tk = tn if lp % tn == 0 else ROW_TILE

    def body(d_ref, u_ref, o_ref):
        @pl.when(pl.program_id(1) == 0)
        def _():
            o_ref[...] = jnp.zeros_like(o_ref)

        o_ref[...] = o_ref[...] + lax.dot_general(d_ref[...], u_ref[...], (((0,), (0,)), ((), ())),
                                                  preferred_element_type=F32)

    return pl.pallas_call(
        body, name="grad_w_in", grid=(D_IN_PAD // tn, lp // tk),
        in_specs=[pl.BlockSpec((tk, tn), lambda n, k: (k, n)), pl.BlockSpec((tk, D_MODEL), lambda n, k: (k, 0))],
        out_specs=pl.BlockSpec((tn, D_MODEL), lambda n, k: (n, 0)),
        out_shape=jax.ShapeDtypeStruct((D_IN_PAD, D_MODEL), F32),
        compiler_params=_params(("parallel", "arbitrary")),
    )(dproj, u)


def _by_chip(own, others, me):
    by_mask = jnp.stack([own, others[1], others[0], others[2]])
    return [lax.dynamic_index_in_dim(by_mask, jnp.bitwise_xor(me, s), 0, keepdims=False) for s in range(N_CHIPS)]


def _both_halves(mine, other, c):
    return jnp.where(c == 0, jnp.concatenate([mine, other], axis=0), jnp.concatenate([other, mine], axis=0))


def _local_step(x2, tgt2, meta_full, norm_g, w_pad, b_f, conv_w_full, attn_g, conv_g, w_out_full, final_g):
    lp = x2.shape[0] + FRONT
    nt = lp // ROW_TILE
    meta_blk = jnp.concatenate([jnp.zeros((PAD_ROWS, D_MODEL), F32), meta_full], axis=0)
    bf_pad = jnp.pad(b_f, ((0, 0), (0, LANE - HEADS)))
    conv_w8 = jnp.pad(conv_w_full, ((0, SUBLANE - conv_w_full.shape[0]), (0, 0)))
    q, k, v, rest, fl, ct, u = _in_proj(x2, meta_blk, norm_g, w_pad, bf_pad)
    ct4 = ct.reshape(SUBLANE, nt, 1, ROW_TILE)
    o, l_sum, m_max = _attn_fwd(q, k, v, ct4)
    (d_out, d_o, delta, dza, dgb, dzc, dconv, loss, g_final, g_attn, g_convg, gw_out) = _post(
        o, l_sum, rest, x2, meta_blk, tgt2, w_out_full, attn_g, conv_g, final_g, conv_w8)
    dq, dk, dv, dc = _attn_bwd(q, k, v, d_o, m_max, delta, ct4)
    dproj, grad_x, d_front, g_norm, g_bf, g_cw = _bwd_in(x2, meta_blk, norm_g, w_pad, bf_pad, fl, dc.reshape(HEADS, lp), dq, dk, dv,
                                             dza, dgb, dzc, dconv, rest, d_out, conv_w8)
    gw_in = _grad_w_in(u, dproj)
    return dict(loss=loss, grad_x=grad_x, d_front=d_front, g_norm=g_norm, g_final=g_final, g_attn=g_attn, g_convg=g_convg, g_bf=g_bf,
                g_cw=g_cw, gw_out=gw_out, gw_in=gw_in)


def kernel(x, meta, norm_g, w_in, b_f, conv_w, attn_norm_g, conv_norm_g, w_out, final_norm_g, loss_target, m_meta, m_norm_g, m_w_in, m_b_f, m_conv_w, m_attn_norm_g, m_conv_norm_g, m_w_out, m_final_norm_g, v_meta, v_norm_g, v_w_in, v_b_f, v_conv_w, v_attn_norm_g, v_conv_norm_g, v_w_out, v_final_norm_g):
    cx_, cy_, cc_ = _position()
    chip = 2 * cx_ + cy_
    shard = w_in.shape[2]
    out_half = w_out.shape[1] // 2
    pick = lambda vals: jnp.where(chip == 0, vals[0], jnp.where(chip == 1, vals[1], jnp.where(chip == 2, vals[2], vals[3])))
    a_off, b_off = pick(A_OFF), pick(B_OFF)
    wt = jnp.transpose(w_in[0]).astype(MXU_DTYPE)
    wi = lax.dynamic_update_slice_in_dim(
        lax.dynamic_update_slice_in_dim(jnp.zeros((WIN_ROWS, D_MODEL), MXU_DTYPE), wt[:PIECE_A], a_off, 0),
        wt[PIECE_A:], b_off, 0)
    wo = w_out[0].astype(MXU_DTYPE)
    small = jnp.concatenate([meta, jnp.pad(conv_w[0], ((0, 8 - conv_w.shape[1]), (0, meta.shape[1] - conv_w.shape[2])))],
                            axis=0)
    gwi, gwo, gsm = _gather_weights(wi.reshape(2, WIN_HALF, D_MODEL), wo.reshape(2, out_half, D_MODEL), small)
    starts = jnp.stack([_window_start(jnp.bitwise_xor(chip, mask)) for mask in (0, 2, 1, 3)]).astype(jnp.int32)
    w_pad = _assemble_w(wi, gwi.reshape(3, WIN_ROWS, D_MODEL), starts)
    w_out_full = jnp.concatenate(_by_chip(wo, gwo.reshape(3, 2 * out_half, D_MODEL), chip), axis=0)
    small_full = jnp.concatenate(_by_chip(small, gsm, chip), axis=1)
    meta_full = small_full[:N_META]
    conv_w_full = jnp.concatenate([small_full[N_META:N_META + 3, 256 * s:256 * s + LANE] for s in range(N_CHIPS)], axis=1)
    final_g2 = final_norm_g.reshape(1, D_MODEL)
    r = _local_step(x[0], loss_target[0], meta_full, norm_g, w_pad, b_f, conv_w_full, attn_norm_g, conv_norm_g,
                    w_out_full, final_g2)
    grad_x = r["grad_x"][None]
    gb = r["gw_out"].reshape(N_CHIPS, 2, out_half, D_MODEL)
    ra, rb = _pair_exchange(r["gw_in"], gb)
    c_idx = jnp.reshape(cc_, (1,)).astype(jnp.int32)
    chip_idx = jnp.reshape(chip, (1,)).astype(jnp.int32)
    pa, pa_wire = _pair_sum_windows(r["gw_in"], ra, c_idx)
    pb, pb_wire = _pair_sum(gb, rb, c_idx)
    xa, xb = _chip_exchange(pa_wire, pb_wire)
    ha = _chip_sum(pa, xa, chip_idx)
    hb = _chip_sum(pb, xb, chip_idx)
    oa, ob = _pair_share(ha, hb)
    g_window = _both_halves(ha, oa, cc_)
    g_w_in_t = jnp.concatenate([lax.dynamic_slice_in_dim(g_window, a_off, PIECE_A, 0),
                                lax.dynamic_slice_in_dim(g_window, b_off, shard - PIECE_A, 0)], axis=0)
    g_w_out = _both_halves(hb, ob, cc_)
    as_rows = lambda a: jnp.transpose(a, (2, 0, 1))
    g_w_in, d_w_in, nm_w_in, nv_w_in = (jnp.transpose(a, (1, 2, 0)) for a in _adamw_rows(
        as_rows(w_in), g_w_in_t, as_rows(m_w_in), as_rows(v_w_in)))
    d_w_out, nm_w_out, nv_w_out = (a[None] for a in _adamw_big(w_out[0], g_w_out, m_w_out[0], v_w_out[0], LANE))
    wide = lambda a: jnp.pad(a, ((0, 0), (0, D_MODEL - a.shape[1])))
    pack = jnp.concatenate([
        r["g_norm"], r["g_final"], jnp.concatenate([r["g_attn"], r["g_convg"]], axis=1), wide(r["g_bf"]),
        wide(r["loss"]), jnp.zeros((3, D_MODEL), F32), r["d_front"][PAD_ROWS:], wide(r["g_cw"])], axis=0)
    params = (norm_g, final_g2, attn_norm_g, conv_norm_g, b_f, meta, conv_w[0])
    ms = (m_norm_g, m_final_norm_g.reshape(1, D_MODEL), m_attn_norm_g, m_conv_norm_g, m_b_f, m_meta, m_conv_w[0])
    vs = (v_norm_g, v_final_norm_g.reshape(1, D_MODEL), v_attn_norm_g, v_conv_norm_g, v_b_f, v_meta, v_conv_w[0])
    loss, g_s, d_s, m_s, v_s = _small_update(pack, _gather_small(pack), params, ms, vs)

    def ordered(small_list, big_in, big_out):
        s_norm, s_final, s_attn, s_convg, s_bf, s_meta, s_cw = small_list
        return (s_meta, s_norm, big_in, s_bf, s_cw[None], s_attn, s_convg, big_out, s_final.reshape(D_MODEL))

    return (loss.reshape(()), grad_x,
            *ordered(g_s, g_w_in, g_w_out[None]), *ordered(d_s, d_w_in, d_w_out),
            *ordered(m_s, nm_w_in, nm_w_out), *ordered(v_s, nv_w_in, nv_w_out))
```

```python
import functools

import jax
import jax.numpy as jnp
from jax import lax
from jax.experimental import pallas as pl
from jax.experimental.pallas import tpu as pltpu

F32 = jnp.float32
MXU_DTYPE = jnp.bfloat16
WIRE_DTYPE = jnp.bfloat16

D_MODEL = 1024
N_META = 16
HEADS = 8
HEAD_DIM = 64
D_ATTN = HEADS * HEAD_DIM
D_CONV = 512
EPS = 1e-6
LANE = 128
SUBLANE = 8
ROW_TILE = 384
ATTN_UNROLL = 3
STAT_TERMS = 1
FRONT = LANE
PAD_ROWS = FRONT - N_META
NEG = -1e30
LOG2E = 1.4426950408889634
N_CHIPS = 4
N_DEV = 8
VMEM_LIMIT_BYTES = 60 * 1024 * 1024

SEG_Q, SEG_K, SEG_V, SEG_F, SEG_ZA, SEG_GB, SEG_GC, SEG_XC, SEG_ZC = (
    0, 512, 1024, 1536, 1664, 2176, 2688, 3200, 3712)
D_IN = 4104
D_IN_PAD = 4224
F_END = 1544
GW_COL_TILE = 1408
WIN_ROWS = 1152
WIN_HALF = WIN_ROWS // 2
WIN_START = (0, 1024, 2160, 3072)
PIECE_A = 518
A_OFF = (0, 2, 12, 126)
B_OFF = (518, 640, 530, 644)
ADAM_LR = 0.001
ADAM_B1 = 0.9
ADAM_B2 = 0.999
ADAM_EPS = 1e-08
ADAM_WD = 0.01
ADAM_STEP = 10

MESH = pl.DeviceIdType.MESH
ANY = pl.BlockSpec(memory_space=pl.ANY)

PACK_ROWS = 32
SLOT_NORM = (0, 1, 0, 1024)
SLOT_FINAL = (1, 2, 0, 1024)
SLOT_ATTN = (2, 3, 0, 512)
SLOT_CONVG = (2, 3, 512, 1024)
SLOT_BF = (3, 4, 0, 8)
SLOT_META = (8, 24, 0, 256)
SLOT_CONVW = (24, 27, 0, 128)
LOSS_ROW = 4


def _params(sem=None):
    return pltpu.CompilerParams(dimension_semantics=sem, vmem_limit_bytes=VMEM_LIMIT_BYTES)


def _sigmoid(z):
    return 1.0 / (1.0 + jnp.exp(-z))


def _dot(a, b):
    return jnp.dot(a, b, preferred_element_type=F32)


def _dot_nt(a, b):
    return lax.dot_general(a, b, (((1,), (1,)), ((), ())), preferred_element_type=F32)


def _dot_exact(ones, x):
    ones = ones.astype(MXU_DTYPE)
    total = None
    for _ in range(3):
        term = x.astype(MXU_DTYPE)
        x = x - term.astype(F32)
        total = _dot(ones, term) if total is None else total + _dot(ones, term)
    return total


def _group_matrix():
    r = lax.broadcasted_iota(jnp.int32, (D_ATTN, D_ATTN), 0) >> 6
    c = lax.broadcasted_iota(jnp.int32, (D_ATTN, D_ATTN), 1) >> 6
    return jnp.where(r == c, 1.0, 0.0).astype(MXU_DTYPE)


def _triangle(n, lower):
    r = lax.broadcasted_iota(jnp.int32, (n, n), 0)
    c = lax.broadcasted_iota(jnp.int32, (n, n), 1)
    return jnp.where((r >= c) if lower else (c >= r), 1.0, 0.0).astype(MXU_DTYPE)


def _group_sum(x, gmat, terms=2):
    hi = x.astype(MXU_DTYPE)
    if terms == 1:
        return _dot(hi, gmat)
    lo = (x - hi.astype(F32)).astype(MXU_DTYPE)
    return _dot(hi, gmat) + _dot(lo, gmat)


def _x_block_specs(n_sub, rows):
    specs = [pl.BlockSpec((rows, D_MODEL), lambda i: (jnp.maximum(n_sub * i - 1, 0), 0))]
    for b in range(1, n_sub):
        specs.append(pl.BlockSpec((rows, D_MODEL), functools.partial(lambda i, b: (n_sub * i - 1 + b, 0), b=b)))
    return specs


def _position():
    return lax.axis_index("x"), lax.axis_index("y"), lax.axis_index("c")


def _gather_weights(wi, wo, small):
    def body(wi_ref, wo_ref, sm_ref, gwi_ref, gwo_ref, gsm_ref, send_sems, recv_sems):
        x, y, c = _position()
        sibling = (x, y, 1 - c)
        chips = [(1 - x, y), (x, 1 - y), (1 - x, 1 - y)]

        def remote(k, src, dst, to):
            return pltpu.make_async_remote_copy(src_ref=src, dst_ref=dst, send_sem=send_sems.at[k],
                                                recv_sem=recv_sems.at[k], device_id=to, device_id_type=MESH)

        first, passed, landed = [], [], []
        for a, (src_ref, g_ref) in enumerate(((wi_ref, gwi_ref), (wo_ref, gwo_ref))):
            for j, (cx, cy) in enumerate(chips):
                slot = g_ref.at[j, c]
                first.append(remote(6 * a + j, src_ref.at[c], slot, (cx, cy, c)))
                landed.append(remote(6 * a + j, slot, slot, sibling))
                passed.append(remote(6 * a + 3 + j, slot, slot, sibling))
        for j, (cx, cy) in enumerate(chips):
            first.append(remote(12 + j, sm_ref, gsm_ref.at[j], (cx, cy, c)))
        for cp in first:
            cp.start()
        for arrived, onward in zip(landed, passed):
            arrived.wait_recv()
            onward.start()
        for a, g_ref in enumerate((gwi_ref, gwo_ref)):
            for j in range(3):
                remote(6 * a + 3 + j, g_ref.at[j, 1 - c], g_ref.at[j, 1 - c], sibling).wait_recv()
        for j in range(3):
            remote(12 + j, sm_ref, gsm_ref.at[j], sibling).wait_recv()
        for cp in first + passed:
            cp.wait_send()

    return pl.pallas_call(
        body, name="gather_weights",
        out_shape=(jax.ShapeDtypeStruct((3,) + wi.shape, wi.dtype), jax.ShapeDtypeStruct((3,) + wo.shape, wo.dtype),
                   jax.ShapeDtypeStruct((3,) + small.shape, small.dtype)),
        in_specs=[ANY, ANY, ANY], out_specs=(ANY, ANY, ANY),
        scratch_shapes=[pltpu.SemaphoreType.DMA((15,)), pltpu.SemaphoreType.DMA((15,))],
    )(wi, wo, small)


def _pair_exchange(gw, gb):
    def body(gw_ref, gb_ref, ra_ref, rb_ref, send_sems, recv_sems):
        x, y, c = _position()
        sibling = (x, y, 1 - c)
        copies = [pltpu.make_async_remote_copy(
            src_ref=gb_ref.at[:, 1 - c], dst_ref=rb_ref, send_sem=send_sems.at[N_CHIPS], recv_sem=recv_sems.at[N_CHIPS],
            device_id=sibling, device_id_type=MESH)]
        for s, start in enumerate(WIN_START):
            rows = pl.ds(pl.multiple_of(start + WIN_HALF * (1 - c), SUBLANE), WIN_HALF)
            copies.append(pltpu.make_async_remote_copy(
                src_ref=gw_ref.at[rows], dst_ref=ra_ref.at[s], send_sem=send_sems.at[s], recv_sem=recv_sems.at[s],
                device_id=sibling, device_id_type=MESH))
        for cp in copies:
            cp.start()
        for cp in copies:
            cp.wait()

    return pl.pallas_call(
        body, name="grad_pair_exchange",
        out_shape=(jax.ShapeDtypeStruct((N_CHIPS, WIN_HALF, D_MODEL), gw.dtype),
                   jax.ShapeDtypeStruct((N_CHIPS,) + gb.shape[2:], gb.dtype)),
        in_specs=[ANY, ANY], out_specs=(ANY, ANY),
        scratch_shapes=[pltpu.SemaphoreType.DMA((N_CHIPS + 1,)), pltpu.SemaphoreType.DMA((N_CHIPS + 1,))],
    )(gw, gb)


def _chip_exchange(pa, pb):
    def body(pa_ref, pb_ref, ra_ref, rb_ref, send_sems, recv_sems):
        x, y, c = _position()
        chips = [(1 - x, y), (x, 1 - y), (1 - x, 1 - y)]
        copies = []
        for a, (src, dst) in enumerate(((pa_ref, ra_ref), (pb_ref, rb_ref))):
            for j, (cx, cy) in enumerate(chips):
                copies.append(pltpu.make_async_remote_copy(
                    src_ref=src.at[2 * cx + cy], dst_ref=dst.at[j], send_sem=send_sems.at[3 * a + j],
                    recv_sem=recv_sems.at[3 * a + j], device_id=(cx, cy, c), device_id_type=MESH))
        for cp in copies:
            cp.start()
        for cp in copies:
            cp.wait()

    return pl.pallas_call(
        body, name="grad_chip_exchange",
        out_shape=(jax.ShapeDtypeStruct((3,) + pa.shape[1:], pa.dtype),
                   jax.ShapeDtypeStruct((3,) + pb.shape[1:], pb.dtype)),
        in_specs=[ANY, ANY], out_specs=(ANY, ANY),
        scratch_shapes=[pltpu.SemaphoreType.DMA((6,)), pltpu.SemaphoreType.DMA((6,))],
    )(pa, pb)


def _pair_share(ha, hb):
    def body(ha_ref, hb_ref, oa_ref, ob_ref, send_sems, recv_sems):
        x, y, c = _position()
        copies = [pltpu.make_async_remote_copy(
            src_ref=src, dst_ref=dst, send_sem=send_sems.at[k], recv_sem=recv_sems.at[k],
            device_id=(x, y, 1 - c), device_id_type=MESH)
            for k, (src, dst) in enumerate(((ha_ref, oa_ref), (hb_ref, ob_ref)))]
        for cp in copies:
            cp.start()
        for cp in copies:
            cp.wait()

    return pl.pallas_call(
        body, name="grad_pair_share",
        out_shape=(jax.ShapeDtypeStruct(ha.shape, ha.dtype), jax.ShapeDtypeStruct(hb.shape, hb.dtype)),
        in_specs=[ANY, ANY], out_specs=(ANY, ANY),
        scratch_shapes=[pltpu.SemaphoreType.DMA((2,)), pltpu.SemaphoreType.DMA((2,))],
    )(ha, hb)


def _gather_small(pack):
    def body(p_ref, o_ref, send_sems, recv_sems):
        x, y, c = _position()
        copies = []
        for mask in range(1, N_DEV):
            peer = (1 - x if mask & 4 else x, 1 - y if mask & 2 else y, 1 - c if mask & 1 else c)
            copies.append(pltpu.make_async_remote_copy(
                src_ref=p_ref, dst_ref=o_ref.at[mask - 1], send_sem=send_sems.at[mask - 1],
                recv_sem=recv_sems.at[mask - 1], device_id=peer, device_id_type=MESH))
        for cp in copies:
            cp.start()
        for cp in copies:
            cp.wait()

    return pl.pallas_call(
        body, name="gather_small",
        out_shape=jax.ShapeDtypeStruct((N_DEV - 1,) + pack.shape, pack.dtype),
        in_specs=[ANY], out_specs=ANY,
        scratch_shapes=[pltpu.SemaphoreType.DMA((N_DEV - 1,)), pltpu.SemaphoreType.DMA((N_DEV - 1,))],
    )(pack)


def _pair_sum(mine, recv, c_idx):
    rows, cols = mine.shape[2:]

    def body(c_ref, a_ref, b_ref, o_ref, send_ref):
        total = a_ref[...] + b_ref[...]
        o_ref[...] = total
        send_ref[...] = total.astype(send_ref.dtype)

    out_spec = pl.BlockSpec((None, rows, cols), lambda s, c_ref: (s, 0, 0))
    return pl.pallas_call(
        body, name="grad_pair_sum",
        grid_spec=pltpu.PrefetchScalarGridSpec(
            num_scalar_prefetch=1, grid=(N_CHIPS,),
            in_specs=[pl.BlockSpec((None, None, rows, cols), lambda s, c_ref: (s, c_ref[0], 0, 0)),
                      pl.BlockSpec((None, rows, cols), lambda s, c_ref: (s, 0, 0))],
            out_specs=(out_spec, out_spec)),
        out_shape=(jax.ShapeDtypeStruct(recv.shape, recv.dtype), jax.ShapeDtypeStruct(recv.shape, WIRE_DTYPE)),
        compiler_params=_params(("parallel",)),
    )(c_idx, mine, recv)


def _window_start(s):
    return jnp.where(s == 0, WIN_START[0], jnp.where(s == 1, WIN_START[1], jnp.where(s == 2, WIN_START[2], WIN_START[3])))


def _pair_sum_windows(gw, recv, c_idx):
    tr = WIN_HALF // 3

    def body(c_ref, a_ref, b_ref, o_ref, send_ref):
        total = a_ref[...] + b_ref[...]
        o_ref[...] = total
        send_ref[...] = total.astype(send_ref.dtype)

    out_spec = pl.BlockSpec((None, tr, D_MODEL), lambda s, i, c_ref: (s, i, 0))
    return pl.pallas_call(
        body, name="grad_pair_sum_windows",
        grid_spec=pltpu.PrefetchScalarGridSpec(
            num_scalar_prefetch=1, grid=(N_CHIPS, WIN_HALF // tr),
            in_specs=[pl.BlockSpec((pl.Element(tr), pl.Element(D_MODEL)),
                                   lambda s, i, c_ref: (pl.multiple_of(
                                       _window_start(s) + WIN_HALF * c_ref[0] + tr * i, SUBLANE), 0)),
                      pl.BlockSpec((None, tr, D_MODEL), lambda s, i, c_ref: (s, i, 0))],
            out_specs=(out_spec, out_spec)),
        out_shape=(jax.ShapeDtypeStruct(recv.shape, recv.dtype), jax.ShapeDtypeStruct(recv.shape, WIRE_DTYPE)),
        compiler_params=_params(("parallel", "parallel")),
    )(c_idx, gw, recv)


def _assemble_w(own, others, starts):
    def body(starts_ref, own_ref, oth_ref, o_ref):
        o_ref[...] = jnp.zeros_like(o_ref)
        for k in range(N_CHIPS):
            rows = pl.ds(pl.multiple_of(starts_ref[k], 2 * SUBLANE), WIN_ROWS)
            o_ref[rows, :] = o_ref[rows, :] + (own_ref[...] if k == 0 else oth_ref[k - 1])

    return pl.pallas_call(
        body, name="assemble_w",
        in_specs=[pl.BlockSpec(memory_space=pltpu.SMEM), pl.BlockSpec(memory_space=pltpu.VMEM),
                  pl.BlockSpec(memory_space=pltpu.VMEM)],
        out_specs=pl.BlockSpec(memory_space=pltpu.VMEM),
        out_shape=jax.ShapeDtypeStruct((D_IN_PAD, D_MODEL), own.dtype),
        compiler_params=_params(),
    )(starts, own, others)


def _chip_sum(psum, recv3, chip_idx):
    rows, cols = psum.shape[1:]
    tr = rows // 2

    def body(s_ref, p_ref, r0, r1, r2, o_ref):
        o_ref[...] = ((p_ref[...] + r0[...].astype(F32)) + r1[...].astype(F32)) + r2[...].astype(F32)

    return pl.pallas_call(
        body, name="grad_chip_sum",
        grid_spec=pltpu.PrefetchScalarGridSpec(
            num_scalar_prefetch=1, grid=(2,),
            in_specs=[pl.BlockSpec((None, tr, cols), lambda i, s_ref: (s_ref[0], i, 0))] +
                     [pl.BlockSpec((None, tr, cols), functools.partial(lambda i, s_ref, j: (j, i, 0), j=j))
                      for j in range(3)],
            out_specs=pl.BlockSpec((tr, cols), lambda i, s_ref: (i, 0))),
        out_shape=jax.ShapeDtypeStruct((rows, cols), psum.dtype),
        compiler_params=_params(("parallel",)),
    )(chip_idx, psum, recv3, recv3, recv3)


def _adamw_math(w, g, m, v):
    m = ADAM_B1 * m + (1.0 - ADAM_B1) * g
    v = ADAM_B2 * v + (1.0 - ADAM_B2) * (g * g)
    m_hat = m * (1.0 / (1.0 - ADAM_B1 ** ADAM_STEP))
    v_hat = v * (1.0 / (1.0 - ADAM_B2 ** ADAM_STEP))
    delta = -ADAM_LR * (m_hat / (jnp.sqrt(v_hat) + ADAM_EPS) + ADAM_WD * w)
    return delta, m, v


def _adamw_big(w, g, m, v, tr):
    rows, cols = w.shape
    assert rows % tr == 0 and g.shape[0] >= rows

    def body(w_ref, g_ref, m_ref, v_ref, d_out, m_out, v_out):
        d, m2, v2 = _adamw_math(w_ref[...], g_ref[...], m_ref[...], v_ref[...])
        d_out[...] = d
        m_out[...] = m2
        v_out[...] = v2

    spec = pl.BlockSpec((tr, cols), lambda i: (i, 0))
    sds = jax.ShapeDtypeStruct((rows, cols), F32)
    return pl.pallas_call(
        body, name="adamw_big", grid=(rows // tr,), in_specs=[spec] * 4, out_specs=(spec,) * 3,
        out_shape=(sds,) * 3, compiler_params=_params(("parallel",)),
    )(w, g, m, v)


def _adamw_rows(w3, g, m3, v3):
    rows, _, cols = w3.shape
    tc = 2 * LANE

    def body(w_ref, g_ref, m_ref, v_ref, g_out, d_out, m_out, v_out):
        g = g_ref[...]
        d, m2, v2 = _adamw_math(w_ref[:, 0, :], g, m_ref[:, 0, :], v_ref[:, 0, :])
        g_out[:, 0, :] = g
        d_out[:, 0, :] = d
        m_out[:, 0, :] = m2
        v_out[:, 0, :] = v2

    spec3 = pl.BlockSpec((rows, 1, tc), lambda i: (0, 0, i))
    sds = jax.ShapeDtypeStruct((rows, 1, cols), F32)
    return pl.pallas_call(
        body, name="adamw_rows", grid=(cols // tc,),
        in_specs=[spec3, pl.BlockSpec((rows, tc), lambda i: (0, i)), spec3, spec3], out_specs=(spec3,) * 4,
        out_shape=(sds,) * 4, compiler_params=_params(("parallel",)),
    )(w3, g, m3, v3)


def _small_update(own, others, params, ms, vs):
    slots = (SLOT_NORM, SLOT_FINAL, SLOT_ATTN, SLOT_CONVG, SLOT_BF, SLOT_META, SLOT_CONVW)
    n = len(slots)

    def body(*refs):
        own_ref, gp_ref = refs[:2]
        refs = refs[1:]
        w_refs, m_refs, v_refs = refs[1:1 + n], refs[1 + n:1 + 2 * n], refs[1 + 2 * n:1 + 3 * n]
        outs = refs[1 + 3 * n:2 + 7 * n]
        loss_ref = outs[0]
        g_outs, d_outs, m_outs, v_outs = (outs[1 + k * n:1 + (k + 1) * n] for k in range(4))
        g_scr, w_scr, m_scr, v_scr = refs[2 + 7 * n:]
        x, y, c = _position()
        shard = 2 * x + y
        me = 4 * x + 2 * y + c
        tot = None
        for d in range(N_DEV):
            rel = jnp.bitwise_xor(me, d)
            term = jnp.where(rel == 0, own_ref[...], gp_ref[jnp.maximum(rel, 1) - 1])
            tot = term if tot is None else tot + term
        r0, r1, _, _ = SLOT_META
        meta_sel = tot[r0:r1, 0:256]
        cw_sel = tot[24:32, 0:128]
        for k in range(1, N_CHIPS):
            meta_sel = jnp.where(shard == k, tot[r0:r1, 256 * k:256 * (k + 1)], meta_sel)
            cw_sel = jnp.where(shard == k, tot[24:32, 128 * k:128 * (k + 1)], cw_sel)
        zeros = jnp.zeros((PACK_ROWS, D_MODEL), F32)
        for scr in (g_scr, w_scr, m_scr, v_scr):
            scr[...] = zeros
        g_scr[0:8, :] = tot[0:8, :]
        g_scr[r0:r1, 0:256] = meta_sel
        g_scr[24:32, 0:128] = cw_sel
        for (a, b, c0, c1), w_ref, m_ref, v_ref in zip(slots, w_refs, m_refs, v_refs):
            w_scr[a:b, c0:c1] = w_ref[...]
            m_scr[a:b, c0:c1] = m_ref[...]
            v_scr[a:b, c0:c1] = v_ref[...]
        loss_ref[...] = g_scr[LOSS_ROW:LOSS_ROW + 1, 0:1]
        d, m2, v2 = _adamw_math(w_scr[...], g_scr[...], m_scr[...], v_scr[...])
        w_scr[...] = d
        m_scr[...] = m2
        v_scr[...] = v2
        for (a, b, c0, c1), g_o, d_o, m_o, v_o in zip(slots, g_outs, d_outs, m_outs, v_outs):
            g_o[...] = g_scr[a:b, c0:c1]
            d_o[...] = w_scr[a:b, c0:c1]
            m_o[...] = m_scr[a:b, c0:c1]
            v_o[...] = v_scr[a:b, c0:c1]

    shapes = [jax.ShapeDtypeStruct(p.shape, F32) for p in params]
    out = pl.pallas_call(
        body, name="small_update",
        out_shape=[jax.ShapeDtypeStruct((1, 1), F32)] + shapes * 4,
        scratch_shapes=[pltpu.VMEM((PACK_ROWS, D_MODEL), F32)] * 4,
        compiler_params=_params(),
    )(own, others, *params, *ms, *vs)
    return out[0], out[1:1 + n], out[1 + n:1 + 2 * n], out[1 + 2 * n:1 + 3 * n], out[1 + 3 * n:1 + 4 * n]


def _in_proj(x2, meta_blk, norm_g, w_pad, bf_pad):
    seq = x2.shape[0]
    lp = seq + FRONT
    t = ROW_TILE
    nt = lp // t
    n_sub = t // LANE

    def body(*refs):
        x_refs = refs[:n_sub]
        mb, g_ref, w_ref, bf_ref, tri_ref = refs[n_sub:n_sub + 5]
        q_ref, k_ref, v_ref, rest_ref, fl_ref, ct_ref, u_ref, carry = refs[n_sub + 5:]
        i = pl.program_id(0)

        @pl.when(i == 0)
        def _():
            carry[...] = jnp.zeros_like(carry)

        first = jnp.where(i == 0, mb[...], x_refs[0][...])
        h = jnp.concatenate([first] + [r[...] for r in x_refs[1:]], axis=0)
        ms = jnp.mean(h * h, axis=-1, keepdims=True)
        u = ((h * lax.rsqrt(ms + EPS)) * g_ref[...]).astype(MXU_DTYPE)
        u_ref[...] = u

        def seg(a, width):
            return _dot_nt(u, w_ref[a:a + width, :])

        q_ref[...] = (seg(SEG_Q, D_ATTN) * (HEAD_DIM ** -0.5)).astype(MXU_DTYPE)
        k_ref[...] = seg(SEG_K, D_ATTN).astype(MXU_DTYPE)
        v_ref[...] = seg(SEG_V, D_ATTN).astype(MXU_DTYPE)
        for s in range(5):
            rest_ref[:, 512 * s:512 * (s + 1)] = seg(SEG_ZA + 512 * s, 512)
        fl = seg(SEG_F, LANE)
        fl_ref[...] = fl
        z = fl + bf_ref[...]
        logf = jnp.minimum(z, 0.0) - jnp.log(1.0 + jnp.exp(-jnp.abs(z)))
        row = i * t + lax.broadcasted_iota(jnp.int32, (t, LANE), 0)
        logf = jnp.where(row >= PAD_ROWS, logf, 0.0)
        cs = _dot_exact(tri_ref[...], logf) + carry[...]
        carry[...] = carry[...] + jnp.sum(logf, axis=0, keepdims=True)
        col = i * t + lax.broadcasted_iota(jnp.int32, (SUBLANE, t), 1)
        ct_ref[...] = jnp.where(col >= PAD_ROWS, cs.T[0:SUBLANE, :], -NEG)

    row_blk = lambda cols: pl.BlockSpec((t, cols), lambda i: (i, 0))
    const = lambda shape: pl.BlockSpec(shape, lambda i: (0, 0))
    return pl.pallas_call(
        body, name="in_proj", grid=(nt,),
        in_specs=_x_block_specs(n_sub, LANE) + [const((LANE, D_MODEL)), const((1, D_MODEL)),
                                                pl.BlockSpec((D_IN_PAD, D_MODEL), lambda i: (0, 0),
                                                             pipeline_mode=pl.Buffered(1)),
                                                const((1, LANE)), const((t, t))],
        out_specs=(row_blk(D_ATTN), row_blk(D_ATTN), row_blk(D_ATTN), row_blk(5 * 512), row_blk(LANE),
                   pl.BlockSpec((SUBLANE, t), lambda i: (0, i)), row_blk(D_MODEL)),
        out_shape=(jax.ShapeDtypeStruct((lp, D_ATTN), MXU_DTYPE), jax.ShapeDtypeStruct((lp, D_ATTN), MXU_DTYPE),
                   jax.ShapeDtypeStruct((lp, D_ATTN), MXU_DTYPE), jax.ShapeDtypeStruct((lp, 5 * 512), F32),
                   jax.ShapeDtypeStruct((lp, LANE), F32),
                   jax.ShapeDtypeStruct((SUBLANE, lp), F32), jax.ShapeDtypeStruct((lp, D_MODEL), MXU_DTYPE)),
        scratch_shapes=[pltpu.VMEM((1, LANE), F32)],
        compiler_params=_params(("arbitrary",)),
    )(*([x2] * n_sub), meta_blk, norm_g, w_pad, bf_pad, _triangle(t, lower=True))


def _head_masks():
    lane = lax.broadcasted_iota(jnp.int32, (1, LANE), 1)
    return lane < HEAD_DIM, lane >= HEAD_DIM


def _pair_specs(lp, nt, t):
    blk = pl.BlockSpec((lp, LANE), lambda g: (0, g))
    ct_a = pl.BlockSpec((None, nt, 1, t), lambda g: (2 * g, 0, 0, 0))
    ct_b = pl.BlockSpec((None, nt, 1, t), lambda g: (2 * g + 1, 0, 0, 0))
    return blk, ct_a, ct_b


def _sub_rows(s, col):
    return jnp.concatenate([s[:, a * LANE:(a + 1) * LANE] - col for a in range(s.shape[1] // LANE)], axis=1)


def _loop_unrolled(lo, hi, step, init, n):
    def group(jj, carry):
        for k in range(n):
            carry = step(lo + n * jj + k, carry)
        return carry

    groups = (hi - lo) // n
    carry = lax.fori_loop(0, groups, group, init)
    return lax.fori_loop(lo + n * groups, hi, step, carry)


def _lane_chunks(s):
    return [s[:, a * LANE:(a + 1) * LANE] for a in range(s.shape[1] // LANE)]


def _attn_fwd(q, k, v, ct4):
    lp = q.shape[0]
    t = ROW_TILE
    nt = lp // t

    def body(q_ref, k_ref, v_ref, cta_ref, ctb_ref, o_ref, l_ref, m_ref, s_scr):
        masks = _head_masks()
        ct_refs = (cta_ref, ctb_ref)
        below = lax.broadcasted_iota(jnp.int32, (t, t), 1) <= lax.broadcasted_iota(jnp.int32, (t, t), 0)
        lane = lax.broadcasted_iota(jnp.int32, (1, LANE), 1)
        head_of_row = lax.broadcasted_iota(jnp.int32, (2 * t, LANE), 0) >= t
        ones_cols = jnp.where(lax.broadcasted_iota(jnp.int32, (2 * t, LANE), 1) == head_of_row.astype(jnp.int32),
                              1.0, 0.0).astype(MXU_DTYPE)

        def q_block(i, _):
            r0 = pl.multiple_of(i * t, t)
            qi = q_ref[pl.ds(r0, t), :]

            def scores(j):
                kj = k_ref[pl.ds(pl.multiple_of(j * t, t), t), :]
                return _dot_nt(qi, jnp.concatenate([jnp.where(hm, kj, 0).astype(MXU_DTYPE) for hm in masks], axis=0))

            def biased(j, hh, s2, diagonal):
                s = (s2[:, hh * t:(hh + 1) * t] - ct_refs[hh][j]) * LOG2E
                return jnp.where(below, s, NEG) if diagonal else s

            def max_step(j, carry, diagonal):
                s2 = scores(j)
                out = []
                for hh, m in enumerate(carry):
                    s = biased(j, hh, s2, diagonal)
                    s_scr[j, :, hh * t:(hh + 1) * t] = s
                    for c in _lane_chunks(s):
                        m = jnp.maximum(m, c)
                    out.append(m)
                return tuple(out)

            lanes_neg = jnp.full((t, LANE), NEG, F32)
            carry = _loop_unrolled(0, i, functools.partial(max_step, diagonal=False), (lanes_neg, lanes_neg),
                                   ATTN_UNROLL)
            ms = [jnp.max(m, axis=-1, keepdims=True) for m in max_step(i, carry, True)]

            def sum_step(j, acc):
                vj = v_ref[pl.ds(pl.multiple_of(j * t, t), t), :]
                v2 = jnp.concatenate([jnp.where(hm, vj, 0).astype(MXU_DTYPE) for hm in masks], axis=0)
                parts = [jnp.exp2(s_scr[j, :, hh * t:(hh + 1) * t] - ms[hh]).astype(MXU_DTYPE) for hh in range(2)]
                return acc + _dot(jnp.concatenate(parts, axis=1), jnp.concatenate([v2, ones_cols], axis=1))

            acc = _loop_unrolled(0, i + 1, sum_step, jnp.zeros((t, 2 * LANE), F32), ATTN_UNROLL)
            sums = acc[:, LANE:]
            l_pair = jnp.where(masks[0], jnp.sum(jnp.where(lane == 0, sums, 0.0), axis=-1, keepdims=True),
                               jnp.sum(jnp.where(lane == 1, sums, 0.0), axis=-1, keepdims=True))
            o_ref[pl.ds(r0, t), :] = acc[:, :LANE] / l_pair
            l_ref[pl.ds(r0, t), :] = l_pair
            m_ref[pl.ds(r0, t), 0:LANE] = jnp.broadcast_to(ms[0], (t, LANE))
            m_ref[pl.ds(r0, t), LANE:2 * LANE] = jnp.broadcast_to(ms[1], (t, LANE))
            return 0

        lax.fori_loop(0, nt, q_block, 0)

    blk, ct_a, ct_b = _pair_specs(lp, nt, t)
    return pl.pallas_call(
        body, name="attn_fwd", grid=(HEADS // 2,),
        in_specs=[blk, blk, blk, ct_a, ct_b], out_specs=(blk, blk, pl.BlockSpec((lp, 2 * LANE), lambda g: (0, g))),
        out_shape=(jax.ShapeDtypeStruct((lp, D_ATTN), F32), jax.ShapeDtypeStruct((lp, D_ATTN), F32),
                   jax.ShapeDtypeStruct((lp, HEADS * LANE), F32)),
        scratch_shapes=[pltpu.VMEM((nt, t, 2 * t), F32)],
        compiler_params=_params(("parallel",)),
    )(q, k, v, ct4, ct4)


def _attn_bwd(q, k, v, do, m, delta, ct4):
    lp = q.shape[0]
    t = ROW_TILE
    nt = lp // t

    def body(q_ref, k_ref, v_ref, do_ref, ma_ref, mb_ref, dla_ref, dlb_ref, cta_ref, ctb_ref,
             dq_ref, dk_ref, dv_ref, dc_ref, dq_acc, dk_acc, dv_acc):
        masks = _head_masks()
        ct_refs, m_refs, dl_refs = (cta_ref, ctb_ref), (ma_ref, mb_ref), (dla_ref, dlb_ref)
        below = lax.broadcasted_iota(jnp.int32, (t, t), 1) <= lax.broadcasted_iota(jnp.int32, (t, t), 0)
        tn = (((0,), (0,)), ((), ()))
        dq_acc[...] = jnp.zeros_like(dq_acc)

        def k_block(j, _):
            c0 = pl.multiple_of(j * t, t)
            kj = k_ref[pl.ds(c0, t), :]
            vj = v_ref[pl.ds(c0, t), :]
            k2 = jnp.concatenate([jnp.where(hm, kj, 0).astype(MXU_DTYPE) for hm in masks], axis=0)
            v2 = jnp.concatenate([jnp.where(hm, vj, 0).astype(MXU_DTYPE) for hm in masks], axis=0)
            ck = [r[j] for r in ct_refs]
            dk_acc[...] = jnp.zeros_like(dk_acc)
            dv_acc[...] = jnp.zeros_like(dv_acc)

            def q_block(i, colsums, diagonal):
                r0 = pl.multiple_of(i * t, t)
                qi = q_ref[pl.ds(r0, t), :]
                doi = do_ref[pl.ds(r0, t), :]
                q2 = jnp.concatenate([jnp.where(hm, qi, 0).astype(MXU_DTYPE) for hm in masks], axis=0)
                do2 = jnp.concatenate([jnp.where(hm, doi, 0).astype(MXU_DTYPE) for hm in masks], axis=0)
                s2 = _dot_nt(qi, k2)
                dp2 = _dot_nt(doi, v2)
                out, ps, dss = [], [], []
                for hh in range(2):
                    s = (s2[:, hh * t:(hh + 1) * t] - ck[hh]) * LOG2E
                    if diagonal:
                        s = jnp.where(below, s, NEG)
                    p = jnp.exp2(_sub_rows(s, m_refs[hh][pl.ds(r0, t), :])).astype(MXU_DTYPE)
                    ds32 = p.astype(F32) * _sub_rows(dp2[:, hh * t:(hh + 1) * t], dl_refs[hh][pl.ds(r0, t), :])
                    ps.append(p)
                    dss.append(ds32.astype(MXU_DTYPE))
                    out.append(colsums[hh] + jnp.sum(ds32, axis=0, keepdims=True))
                dv_acc[...] = dv_acc[...] + lax.dot_general(jnp.concatenate(ps, axis=0), do2, tn,
                                                            preferred_element_type=F32)
                dk_acc[...] = dk_acc[...] + lax.dot_general(jnp.concatenate(dss, axis=0), q2, tn,
                                                            preferred_element_type=F32)
                dq_acc[pl.ds(r0, t), :] = dq_acc[pl.ds(r0, t), :] + _dot(jnp.concatenate(dss, axis=1), k2)
                return tuple(out)

            colsums = q_block(j, (jnp.zeros((1, t), F32), jnp.zeros((1, t), F32)), True)
            colsums = _loop_unrolled(j + 1, nt, functools.partial(q_block, diagonal=False), colsums, 2)
            for hh in range(2):
                dc_ref[hh, j] = -colsums[hh]
            dk_ref[pl.ds(c0, t), :] = dk_acc[...].astype(dk_ref.dtype)
            dv_ref[pl.ds(c0, t), :] = dv_acc[...].astype(dv_ref.dtype)
            return 0

        lax.fori_loop(0, nt, k_block, 0)
        dq_ref[...] = (dq_acc[...] * (HEAD_DIM ** -0.5)).astype(dq_ref.dtype)

    blk, ct_a, ct_b = _pair_specs(lp, nt, t)
    rep_a = pl.BlockSpec((lp, LANE), lambda g: (0, 2 * g))
    rep_b = pl.BlockSpec((lp, LANE), lambda g: (0, 2 * g + 1))
    return pl.pallas_call(
        body, name="attn_bwd", grid=(HEADS // 2,),
        in_specs=[blk] * 4 + [rep_a, rep_b, rep_a, rep_b, ct_a, ct_b],
        out_specs=(blk, blk, blk, pl.BlockSpec((2, nt, 1, t), lambda g: (g, 0, 0, 0))),
        out_shape=(jax.ShapeDtypeStruct((lp, D_ATTN), MXU_DTYPE),) * 3
                  + (jax.ShapeDtypeStruct((HEADS, nt, 1, t), F32),),
        scratch_shapes=[pltpu.VMEM((lp, LANE), F32), pltpu.VMEM((t, LANE), F32), pltpu.VMEM((t, LANE), F32)],
        compiler_params=_params(("parallel",)),
    )(q, k, v, do, m, m, delta, delta, ct4, ct4)


def _shift_down(prev8, cur, k):
    ext = jnp.concatenate([prev8, cur], axis=0)
    return pltpu.roll(ext, k, 0)[SUBLANE:, :]


def _shift_up(cur, next8, k):
    ext = jnp.concatenate([cur, next8], axis=0)
    n = ext.shape[0]
    return pltpu.roll(ext, n - k, 0)[:cur.shape[0], :]


def _post(o, l_sum, rest, x2, meta_blk, tgt2, w_out, attn_g, conv_g, final_g, conv_w8):
    lp = o.shape[0]
    t = ROW_TILE
    nt = lp // t
    n_sub = t // LANE
    hb = t // SUBLANE

    def body(*refs):
        o_ref, l_ref, za_ref, gb_ref, gc_ref, xc_ref, zc_ref, gch_ref, xch_ref = refs[:9]
        refs = refs[1:]
        x_refs = refs[8:8 + n_sub]
        mb = refs[8 + n_sub]
        t_refs = refs[9 + n_sub:9 + 2 * n_sub]
        wo_ref, ag_ref, cg_ref, fg_ref, cw_ref, gm_ref, hr_ref = refs[9 + 2 * n_sub:16 + 2 * n_sub]
        (dout_ref, do_ref, dl_ref, dza_ref, dgb_ref, dzc_ref, dcv_ref,
         loss_ref, gf_ref, gag_ref, gcg_ref, gwo_ref) = refs[16 + 2 * n_sub:]
        i = pl.program_id(0)

        @pl.when(i == 0)
        def _():
            for r in (loss_ref, gf_ref, gag_ref, gcg_ref, gwo_ref):
                r[...] = jnp.zeros_like(r)

        gmat = gm_ref[...]
        inv_g = 1.0 / HEAD_DIM
        o_v = o_ref[...]
        ra = lax.rsqrt(_group_sum(o_v * o_v, gmat, STAT_TERMS) * inv_g + EPS)
        n_a = o_v * ra
        a_n = n_a * ag_ref[...]
        za = za_ref[...]
        sig_a = _sigmoid(za)
        sz_a = za * sig_a
        y_a = a_n * sz_a
        gb = gb_ref[...]
        gc = gc_ref[...]
        xc = xc_ref[...]
        cx = gc * xc
        cx_prev = jnp.where(i == 0, 0.0, gch_ref[...] * xch_ref[...])
        conv = (cw_ref[0:1, :] * _shift_down(cx_prev, cx, 2) + cw_ref[1:2, :] * _shift_down(cx_prev, cx, 1)
                + cw_ref[2:3, :] * cx)
        e = gb * conv
        re = lax.rsqrt(_group_sum(e * e, gmat, STAT_TERMS) * inv_g + EPS)
        n_e = e * re
        e_n = n_e * cg_ref[...]
        zc = zc_ref[...]
        sig_c = _sigmoid(zc)
        sz_c = zc * sig_c
        y_c = e_n * sz_c
        mix = jnp.concatenate([y_a, y_c], axis=-1)
        mix_b = mix.astype(MXU_DTYPE)
        first = jnp.where(i == 0, mb[...], x_refs[0][...])
        h = jnp.concatenate([first] + [r[...] for r in x_refs[1:]], axis=0)
        out = h + _dot(mix_b, wo_ref[...])
        r2 = lax.rsqrt(jnp.mean(out * out, axis=-1, keepdims=True) + EPS)
        n_f = out * r2
        y = n_f * fg_ref[...]
        tgt = jnp.concatenate([r[...] for r in t_refs], axis=0)
        valid = (i * t + lax.broadcasted_iota(jnp.int32, (t, 1), 0)) >= FRONT
        diff = jnp.where(valid, y - tgt, 0.0)
        loss_ref[...] = loss_ref[...] + 0.5 * jnp.sum(jnp.sum(diff * diff, axis=-1, keepdims=True) * (1.0 / D_MODEL))
        dy = diff * (1.0 / D_MODEL)
        gf_ref[...] = gf_ref[...] + jnp.sum(dy * n_f, axis=0, keepdims=True)
        dn = dy * fg_ref[...]
        d_out = r2 * (dn - n_f * jnp.mean(dn * n_f, axis=-1, keepdims=True))
        dout_ref[...] = d_out
        d_out_b = d_out.astype(MXU_DTYPE)
        d_mix = _dot_nt(d_out_b, wo_ref[...])
        gwo_ref[...] = gwo_ref[...] + _dot(mix.T.astype(MXU_DTYPE), d_out_b)
        d_ya = d_mix[:, :D_ATTN]
        d_yc = d_mix[:, D_ATTN:]
        d_an = d_ya * sz_a
        dza_ref[...] = (d_ya * a_n * (sig_a * (1.0 + za * (1.0 - sig_a)))).astype(dza_ref.dtype)
        gag_ref[...] = gag_ref[...] + jnp.sum(d_an * n_a, axis=0, keepdims=True)
        dn_a = d_an * ag_ref[...]
        d_o = ra * (dn_a - n_a * (_group_sum(dn_a * n_a, gmat, STAT_TERMS) * inv_g))
        d_o_b = (d_o / l_ref[...]).astype(do_ref.dtype)
        do_ref[...] = d_o_b
        dl_ref[...] = _group_sum(d_o_b.astype(F32) * o_v, hr_ref[...])
        d_en = d_yc * sz_c
        dzc_ref[...] = (d_yc * e_n * (sig_c * (1.0 + zc * (1.0 - sig_c)))).astype(dzc_ref.dtype)
        gcg_ref[...] = gcg_ref[...] + jnp.sum(d_en * n_e, axis=0, keepdims=True)
        dn_e = d_en * cg_ref[...]
        d_e = re * (dn_e - n_e * (_group_sum(dn_e * n_e, gmat, STAT_TERMS) * inv_g))
        dgb_ref[...] = (d_e * conv).astype(dgb_ref.dtype)
        dcv_ref[...] = d_e * gb

    head_rep = jnp.where((lax.broadcasted_iota(jnp.int32, (D_ATTN, HEADS * LANE), 0) >> 6)
                         == (lax.broadcasted_iota(jnp.int32, (D_ATTN, HEADS * LANE), 1) >> 7), 1.0, 0.0).astype(MXU_DTYPE)
    row_blk = lambda cols: pl.BlockSpec((t, cols), lambda i: (i, 0))
    rest_blk = lambda s: pl.BlockSpec((t, 512), functools.partial(lambda i, s: (i, s), s=s))
    halo = lambda s: pl.BlockSpec((SUBLANE, 512), functools.partial(lambda i, s: (jnp.maximum(i * hb - 1, 0), s), s=s))
    const = lambda shape: pl.BlockSpec(shape, lambda i: (0, 0))
    acc = lambda shape: pl.BlockSpec(shape, lambda i: (0, 0))
    return pl.pallas_call(
        body, name="post_fwd_bwd", grid=(nt,),
        in_specs=[row_blk(D_ATTN), row_blk(D_ATTN)] + [rest_blk(s) for s in range(5)] + [halo(2), halo(3)]
                 + _x_block_specs(n_sub, LANE) + [const((LANE, D_MODEL))] + _x_block_specs(n_sub, LANE)
                 + [const((D_MODEL, D_MODEL)), const((1, D_ATTN)), const((1, D_CONV)), const((1, D_MODEL)),
                    const((SUBLANE, D_CONV)), const((D_ATTN, D_ATTN)), const((D_ATTN, HEADS * LANE))],
        out_specs=(row_blk(D_MODEL), row_blk(D_ATTN), row_blk(HEADS * LANE), row_blk(D_ATTN), row_blk(D_CONV),
                   row_blk(D_CONV), row_blk(D_CONV),
                   acc((1, LANE)), acc((1, D_MODEL)), acc((1, D_ATTN)), acc((1, D_CONV)), acc((D_MODEL, D_MODEL))),
        out_shape=(jax.ShapeDtypeStruct((lp, D_MODEL), F32), jax.ShapeDtypeStruct((lp, D_ATTN), MXU_DTYPE),
                   jax.ShapeDtypeStruct((lp, HEADS * LANE), F32), jax.ShapeDtypeStruct((lp, D_ATTN), MXU_DTYPE),
                   jax.ShapeDtypeStruct((lp, D_CONV), MXU_DTYPE), jax.ShapeDtypeStruct((lp, D_CONV), MXU_DTYPE),
                   jax.ShapeDtypeStruct((lp, D_CONV), F32),
                   jax.ShapeDtypeStruct((1, LANE), F32), jax.ShapeDtypeStruct((1, D_MODEL), F32),
                   jax.ShapeDtypeStruct((1, D_ATTN), F32), jax.ShapeDtypeStruct((1, D_CONV), F32),
                   jax.ShapeDtypeStruct((D_MODEL, D_MODEL), F32)),
        compiler_params=_params(("arbitrary",)),
    )(o, l_sum, *([rest] * 5), rest, rest, *([x2] * n_sub), meta_blk, *([tgt2] * n_sub),
      w_out, attn_g, conv_g, final_g, conv_w8, _group_matrix(), head_rep)


def _bwd_in(x2, meta_blk, norm_g, w_pad, bf_pad, fl, dc, dq, dk, dv, dza, dgb, dzc, dconv, rest, d_out, conv_w8):
    lp = fl.shape[0]
    t = ROW_TILE
    nt = lp // t
    n_sub = t // LANE
    hb = t // SUBLANE
    rev = lambda i: nt - 1 - i

    def body(*refs):
        x_refs = refs[:n_sub]
        (mb, g_ref, w_ref, bf_ref, fl_ref, dc_ref, dq_ref, dk_ref, dv_ref, dza_ref, dgb_ref, dzc_ref,
         dcv_ref, dcvn_ref, gc_ref, xc_ref, gch_ref, xch_ref, dout_ref, cw_ref, tri_ref) = refs[n_sub:n_sub + 21]
        dp_ref, gx_ref, front_ref, gn_ref, gbf_ref, gcw_ref, carry, dh_scr, gx_sems = refs[n_sub + 21:]
        step = pl.program_id(0)
        i = rev(step)

        @pl.when(step == 0)
        def _():
            for r in (gn_ref, gbf_ref, gcw_ref, carry):
                r[...] = jnp.zeros_like(r)

        dc8 = jnp.concatenate([dc_ref[...], jnp.zeros((LANE - HEADS, t), F32)], axis=0).T
        dlogf = _dot_exact(tri_ref[...], dc8) + carry[...]
        carry[...] = carry[...] + jnp.sum(dc8, axis=0, keepdims=True)
        z = fl_ref[...] + bf_ref[...]
        row = i * t + lax.broadcasted_iota(jnp.int32, (t, LANE), 0)
        d_f = jnp.where(row >= PAD_ROWS, dlogf * (1.0 / (1.0 + jnp.exp(z))), 0.0)
        gbf_ref[...] = gbf_ref[...] + jnp.sum(d_f, axis=0, keepdims=True)
        dcv = dcv_ref[...]
        dcv_next = jnp.where(i == nt - 1, 0.0, dcvn_ref[...])
        d_cx = (cw_ref[2:3, :] * dcv + cw_ref[1:2, :] * _shift_up(dcv, dcv_next, 1)
                + cw_ref[0:1, :] * _shift_up(dcv, dcv_next, 2))
        gc = gc_ref[...]
        xc = xc_ref[...]
        cx = gc * xc
        cx_prev = jnp.where(i == 0, 0.0, gch_ref[...] * xch_ref[...])
        rowi = lax.broadcasted_iota(jnp.int32, (SUBLANE, 1), 0)
        gcw = (jnp.where(rowi == 0, jnp.sum(dcv * _shift_down(cx_prev, cx, 2), axis=0, keepdims=True), 0.0)
               + jnp.where(rowi == 1, jnp.sum(dcv * _shift_down(cx_prev, cx, 1), axis=0, keepdims=True), 0.0)
               + jnp.where(rowi == 2, jnp.sum(dcv * cx, axis=0, keepdims=True), 0.0))
        gcw_ref[...] = gcw_ref[...] + gcw
        dp_ref[:, SEG_Q:SEG_Q + 512] = dq_ref[...]
        dp_ref[:, SEG_K:SEG_K + 512] = dk_ref[...]
        dp_ref[:, SEG_V:SEG_V + 512] = dv_ref[...]
        dp_ref[:, SEG_F:SEG_F + LANE] = d_f.astype(dp_ref.dtype)
        dp_ref[:, SEG_ZA:SEG_ZA + 512] = dza_ref[...]
        dp_ref[:, SEG_GB:SEG_GB + 512] = dgb_ref[...]
        dp_ref[:, SEG_GC:SEG_GC + 512] = (d_cx * xc).astype(dp_ref.dtype)
        dp_ref[:, SEG_XC:SEG_XC + 512] = (d_cx * gc).astype(dp_ref.dtype)
        dp_ref[:, SEG_ZC:SEG_ZC + 512] = dzc_ref[...]
        d_u = _dot(dp_ref[...], w_ref[...])
        first = jnp.where(i == 0, mb[...], x_refs[0][...])
        h = jnp.concatenate([first] + [r[...] for r in x_refs[1:]], axis=0)
        r1 = lax.rsqrt(jnp.mean(h * h, axis=-1, keepdims=True) + EPS)
        n_h = h * r1
        gn_ref[...] = gn_ref[...] + jnp.sum(d_u * n_h, axis=0, keepdims=True)
        dn = d_u * g_ref[...]
        d_h = dout_ref[...] + r1 * (dn - n_h * jnp.mean(dn * n_h, axis=-1, keepdims=True))
        slot = step % 2

        def to_grad_x(slot_, tile):
            return pltpu.make_async_copy(dh_scr.at[slot_], gx_ref.at[pl.ds(pl.multiple_of(tile * t - FRONT, SUBLANE), t)],
                                         gx_sems.at[slot_])

        @pl.when(step >= 2)
        def _():
            to_grad_x(slot, 1).wait()

        dh_scr[slot] = d_h

        @pl.when(i > 0)
        def _():
            to_grad_x(slot, i).start()

        @pl.when(i == 0)
        def _():
            front_ref[...] = d_h[:FRONT]
            rest_rows = pltpu.make_async_copy(dh_scr.at[slot, pl.ds(FRONT, t - FRONT)], gx_ref.at[pl.ds(0, t - FRONT)],
                                              gx_sems.at[slot])
            rest_rows.start()
            rest_rows.wait()
            if nt >= 2:
                to_grad_x(1 - slot, 1).wait()

    def x_specs():
        specs = [pl.BlockSpec((LANE, D_MODEL), lambda s: (jnp.maximum(n_sub * rev(s) - 1, 0), 0))]
        for b in range(1, n_sub):
            specs.append(pl.BlockSpec((LANE, D_MODEL), functools.partial(lambda s, b: (n_sub * rev(s) - 1 + b, 0), b=b)))
        return specs

    row_blk = lambda cols: pl.BlockSpec((t, cols), lambda s: (rev(s), 0))
    rest_blk = lambda k: pl.BlockSpec((t, 512), functools.partial(lambda s, k: (rev(s), k), k=k))
    halo_prev = lambda k: pl.BlockSpec(
        (SUBLANE, 512), functools.partial(lambda s, k: (jnp.maximum(rev(s) * hb - 1, 0), k), k=k))
    halo_next = pl.BlockSpec((SUBLANE, 512), lambda s: (jnp.minimum((rev(s) + 1) * hb, lp // SUBLANE - 1), 0))
    const = lambda shape: pl.BlockSpec(shape, lambda s: (0, 0))
    return pl.pallas_call(
        body, name="bwd_in", grid=(nt,),
        in_specs=x_specs() + [const((LANE, D_MODEL)), const((1, D_MODEL)),
                              pl.BlockSpec((D_IN_PAD, D_MODEL), lambda s: (0, 0), pipeline_mode=pl.Buffered(1)),
                              const((1, LANE)), row_blk(LANE),
                              pl.BlockSpec((HEADS, t), lambda s: (0, rev(s))),
                              row_blk(512), row_blk(512), row_blk(512), row_blk(512), row_blk(512), row_blk(512),
                              row_blk(512), halo_next, rest_blk(2), rest_blk(3), halo_prev(2), halo_prev(3),
                              row_blk(D_MODEL), const((SUBLANE, D_CONV)), const((t, t))],
        out_specs=(row_blk(D_IN_PAD), ANY, const((FRONT, D_MODEL)), const((1, D_MODEL)), const((1, LANE)),
                   const((SUBLANE, D_CONV))),
        out_shape=(jax.ShapeDtypeStruct((lp, D_IN_PAD), MXU_DTYPE), jax.ShapeDtypeStruct((lp - FRONT, D_MODEL), F32),
                   jax.ShapeDtypeStruct((FRONT, D_MODEL), F32),
                   jax.ShapeDtypeStruct((1, D_MODEL), F32), jax.ShapeDtypeStruct((1, LANE), F32),
                   jax.ShapeDtypeStruct((SUBLANE, D_CONV), F32)),
        scratch_shapes=[pltpu.VMEM((1, LANE), F32), pltpu.VMEM((2, t, D_MODEL), F32), pltpu.SemaphoreType.DMA((2,))],
        compiler_params=_params(("arbitrary",)),
    )(*([x2] * n_sub), meta_blk, norm_g, w_pad, bf_pad, fl, dc, dq, dk, dv, dza, dgb, dzc, dconv, dconv,
      rest, rest, rest, rest, d_out, conv_w8, _triangle(t, lower=False))


def _grad_w_in(u, dproj):
    lp = u.shape[0]
    tn = GW_COL_TILE
    tk = tn if lp % tn == 0 else ROW_TILE

    def body(d_ref, u_ref, o_ref):
        @pl.when(pl.program_id(1) == 0)
        def _():
            o_ref[...] = jnp.zeros_like(o_ref)

        o_ref[...] = o_ref[...] + lax.dot_general(d_ref[...], u_ref[...], (((0,), (0,)), ((), ())),
                                                  preferred_element_type=F32)

    return pl.pallas_call(
        body, name="grad_w_in", grid=(D_IN_PAD // tn, lp // tk),
        in_specs=[pl.BlockSpec((tk, tn), lambda n, k: (k, n)), pl.BlockSpec((tk, D_MODEL), lambda n, k: (k, 0))],
        out_specs=pl.BlockSpec((tn, D_MODEL), lambda n, k: (n, 0)),
        out_shape=jax.ShapeDtypeStruct((D_IN_PAD, D_MODEL), F32),
        compiler_params=_params(("parallel", "arbitrary")),
    )(dproj, u)


def _by_chip(own, others, me):
    by_mask = jnp.stack([own, others[1], others[0], others[2]])
    return [lax.dynamic_index_in_dim(by_mask, jnp.bitwise_xor(me, s), 0, keepdims=False) for s in range(N_CHIPS)]


def _both_halves(mine, other, c):
    return jnp.where(c == 0, jnp.concatenate([mine, other], axis=0), jnp.concatenate([other, mine], axis=0))


def _local_step(x2, tgt2, meta_full, norm_g, w_pad, b_f, conv_w_full, attn_g, conv_g, w_out_full, final_g):
    lp = x2.shape[0] + FRONT
    nt = lp // ROW_TILE
    meta_blk = jnp.concatenate([jnp.zeros((PAD_ROWS, D_MODEL), F32), meta_full], axis=0)
    bf_pad = jnp.pad(b_f, ((0, 0), (0, LANE - HEADS)))
    conv_w8 = jnp.pad(conv_w_full, ((0, SUBLANE - conv_w_full.shape[0]), (0, 0)))
    q, k, v, rest, fl, ct, u = _in_proj(x2, meta_blk, norm_g, w_pad, bf_pad)
    ct4 = ct.reshape(SUBLANE, nt, 1, ROW_TILE)
    o, l_sum, m_max = _attn_fwd(q, k, v, ct4)
    (d_out, d_o, delta, dza, dgb, dzc, dconv, loss, g_final, g_attn, g_convg, gw_out) = _post(
        o, l_sum, rest, x2, meta_blk, tgt2, w_out_full, attn_g, conv_g, final_g, conv_w8)
    dq, dk, dv, dc = _attn_bwd(q, k, v, d_o, m_max, delta, ct4)
    dproj, grad_x, d_front, g_norm, g_bf, g_cw = _bwd_in(x2, meta_blk, norm_g, w_pad, bf_pad, fl, dc.reshape(HEADS, lp), dq, dk, dv,
                                             dza, dgb, dzc, dconv, rest, d_out, conv_w8)
    gw_in = _grad_w_in(u, dproj)
    return dict(loss=loss, grad_x=grad_x, d_front=d_front, g_norm=g_norm, g_final=g_final, g_attn=g_attn, g_convg=g_convg, g_bf=g_bf,
                g_cw=g_cw, gw_out=gw_out, gw_in=gw_in)


def kernel(x, meta, norm_g, w_in, b_f, conv_w, attn_norm_g, conv_norm_g, w_out, final_norm_g, loss_target, m_meta, m_norm_g, m_w_in, m_b_f, m_conv_w, m_attn_norm_g, m_conv_norm_g, m_w_out, m_final_norm_g, v_meta, v_norm_g, v_w_in, v_b_f, v_conv_w, v_attn_norm_g, v_conv_norm_g, v_w_out, v_final_norm_g):
    cx_, cy_, cc_ = _position()
    chip = 2 * cx_ + cy_
    shard = w_in.shape[2]
    out_half = w_out.shape[1] // 2
    pick = lambda vals: jnp.where(chip == 0, vals[0], jnp.where(chip == 1, vals[1], jnp.where(chip == 2, vals[2], vals[3])))
    a_off, b_off = pick(A_OFF), pick(B_OFF)
    wt = jnp.transpose(w_in[0]).astype(MXU_DTYPE)
    wi = lax.dynamic_update_slice_in_dim(
        lax.dynamic_update_slice_in_dim(jnp.zeros((WIN_ROWS, D_MODEL), MXU_DTYPE), wt[:PIECE_A], a_off, 0),
        wt[PIECE_A:], b_off, 0)
    wo = w_out[0].astype(MXU_DTYPE)
    small = jnp.concatenate([meta, jnp.pad(conv_w[0], ((0, 8 - conv_w.shape[1]), (0, meta.shape[1] - conv_w.shape[2])))],
                            axis=0)
    gwi, gwo, gsm = _gather_weights(wi.reshape(2, WIN_HALF, D_MODEL), wo.reshape(2, out_half, D_MODEL), small)
    starts = jnp.stack([_window_start(jnp.bitwise_xor(chip, mask)) for mask in (0, 2, 1, 3)]).astype(jnp.int32)
    w_pad = _assemble_w(wi, gwi.reshape(3, WIN_ROWS, D_MODEL), starts)
    w_out_full = jnp.concatenate(_by_chip(wo, gwo.reshape(3, 2 * out_half, D_MODEL), chip), axis=0)
    small_full = jnp.concatenate(_by_chip(small, gsm, chip), axis=1)
    meta_full = small_full[:N_META]
    conv_w_full = jnp.concatenate([small_full[N_META:N_META + 3, 256 * s:256 * s + LANE] for s in range(N_CHIPS)], axis=1)
    final_g2 = final_norm_g.reshape(1, D_MODEL)
    r = _local_step(x[0], loss_target[0], meta_full, norm_g, w_pad, b_f, conv_w_full, attn_norm_g, conv_norm_g,
                    w_out_full, final_g2)
    grad_x = r["grad_x"][None]
    gb = r["gw_out"].reshape(N_CHIPS, 2, out_half, D_MODEL)
    ra, rb = _pair_exchange(r["gw_in"], gb)
    c_idx = jnp.reshape(cc_, (1,)).astype(jnp.int32)
    chip_idx = jnp.reshape(chip, (1,)).astype(jnp.int32)
    pa, pa_wire = _pair_sum_windows(r["gw_in"], ra, c_idx)
    pb, pb_wire = _pair_sum(gb, rb, c_idx)
    xa, xb = _chip_exchange(pa_wire, pb_wire)
    ha = _chip_sum(pa, xa, chip_idx)
    hb = _chip_sum(pb, xb, chip_idx)
    oa, ob = _pair_share(ha, hb)
    g_window = _both_halves(ha, oa, cc_)
    g_w_in_t = jnp.concatenate([lax.dynamic_slice_in_dim(g_window, a_off, PIECE_A, 0),
                                lax.dynamic_slice_in_dim(g_window, b_off, shard - PIECE_A, 0)], axis=0)
    g_w_out = _both_halves(hb, ob, cc_)
    as_rows = lambda a: jnp.transpose(a, (2, 0, 1))
    g_w_in, d_w_in, nm_w_in, nv_w_in = (jnp.transpose(a, (1, 2, 0)) for a in _adamw_rows(
        as_rows(w_in), g_w_in_t, as_rows(m_w_in), as_rows(v_w_in)))
    d_w_out, nm_w_out, nv_w_out = (a[None] for a in _adamw_big(w_out[0], g_w_out, m_w_out[0], v_w_out[0], LANE))
    wide = lambda a: jnp.pad(a, ((0, 0), (0, D_MODEL - a.shape[1])))
    pack = jnp.concatenate([
        r["g_norm"], r["g_final"], jnp.concatenate([r["g_attn"], r["g_convg"]], axis=1), wide(r["g_bf"]),
        wide(r["loss"]), jnp.zeros((3, D_MODEL), F32), r["d_front"][PAD_ROWS:], wide(r["g_cw"])], axis=0)
    params = (norm_g, final_g2, attn_norm_g, conv_norm_g, b_f, meta, conv_w[0])
    ms = (m_norm_g, m_final_norm_g.reshape(1, D_MODEL), m_attn_norm_g, m_conv_norm_g, m_b_f, m_meta, m_conv_w[0])
    vs = (v_norm_g, v_final_norm_g.reshape(1, D_MODEL), v_attn_norm_g, v_conv_norm_g, v_b_f, v_meta, v_conv_w[0])
    loss, g_s, d_s, m_s, v_s = _small_update(pack, _gather_small(pack), params, ms, vs)

    def ordered(small_list, big_in, big_out):
        s_norm, s_final, s_attn, s_convg, s_bf, s_meta, s_cw = small_list
        return (s_meta, s_norm, big_in, s_bf, s_cw[None], s_attn, s_convg, big_out, s_final.reshape(D_MODEL))

    return (loss.reshape(()), grad_x,
            *ordered(g_s, g_w_in, g_w_out[None]), *ordered(d_s, d_w_in, d_w_out),
            *ordered(m_s, nm_w_in, nm_w_out), *ordered(v_s, nv_w_in, nv_w_out))
```

```python
import functools

import jax
import jax.numpy as jnp
from jax import lax
from jax.experimental import pallas as pl
from jax.experimental.pallas import tpu as pltpu

F32 = jnp.float32
MXU_DTYPE = jnp.bfloat16
WIRE_DTYPE = jnp.bfloat16

D_MODEL = 1024
N_META = 16
HEADS = 8
HEAD_DIM = 64
D_ATTN = HEADS * HEAD_DIM
D_CONV = 512
EPS = 1e-6
LANE = 128
SUBLANE = 8
ROW_TILE = 384
ATTN_UNROLL = 3
STAT_TERMS = 1
FRONT = LANE
PAD_ROWS = FRONT - N_META
NEG = -1e30
LOG2E = 1.4426950408889634
N_CHIPS = 4
N_DEV = 8
VMEM_LIMIT_BYTES = 60 * 1024 * 1024

SEG_Q, SEG_K, SEG_V, SEG_F, SEG_ZA, SEG_GB, SEG_GC, SEG_XC, SEG_ZC = (
    0, 512, 1024, 1536, 1664, 2176, 2688, 3200, 3712)
D_IN = 4104
D_IN_PAD = 4224
F_END = 1544
GW_COL_TILE = 1408
WIN_ROWS = 1152
WIN_HALF = WIN_ROWS // 2
WIN_START = (0, 1024, 2160, 3072)
PIECE_A = 518
A_OFF = (0, 2, 12, 126)
B_OFF = (518, 640, 530, 644)
ADAM_LR = 0.001
ADAM_B1 = 0.9
ADAM_B2 = 0.999
ADAM_EPS = 1e-08
ADAM_WD = 0.01
ADAM_STEP = 10

MESH = pl.DeviceIdType.MESH
ANY = pl.BlockSpec(memory_space=pl.ANY)

PACK_ROWS = 32
SLOT_NORM = (0, 1, 0, 1024)
SLOT_FINAL = (1, 2, 0, 1024)
SLOT_ATTN = (2, 3, 0, 512)
SLOT_CONVG = (2, 3, 512, 1024)
SLOT_BF = (3, 4, 0, 8)
SLOT_META = (8, 24, 0, 256)
SLOT_CONVW = (24, 27, 0, 128)
LOSS_ROW = 4


def _params(sem=None):
    return pltpu.CompilerParams(dimension_semantics=sem, vmem_limit_bytes=VMEM_LIMIT_BYTES)


def _sigmoid(z):
    return 1.0 / (1.0 + jnp.exp(-z))


def _dot(a, b):
    return jnp.dot(a, b, preferred_element_type=F32)


def _dot_nt(a, b):
    return lax.dot_general(a, b, (((1,), (1,)), ((), ())), preferred_element_type=F32)


def _dot_exact(ones, x):
    ones = ones.astype(MXU_DTYPE)
    total = None
    for _ in range(3):
        term = x.astype(MXU_DTYPE)
        x = x - term.astype(F32)
        total = _dot(ones, term) if total is None else total + _dot(ones, term)
    return total


def _group_matrix():
    r = lax.broadcasted_iota(jnp.int32, (D_ATTN, D_ATTN), 0) >> 6
    c = lax.broadcasted_iota(jnp.int32, (D_ATTN, D_ATTN), 1) >> 6
    return jnp.where(r == c, 1.0, 0.0).astype(MXU_DTYPE)


def _triangle(n, lower):
    r = lax.broadcasted_iota(jnp.int32, (n, n), 0)
    c = lax.broadcasted_iota(jnp.int32, (n, n), 1)
    return jnp.where((r >= c) if lower else (c >= r), 1.0, 0.0).astype(MXU_DTYPE)


def _group_sum(x, gmat, terms=2):
    hi = x.astype(MXU_DTYPE)
    if terms == 1:
        return _dot(hi, gmat)
    lo = (x - hi.astype(F32)).astype(MXU_DTYPE)
    return _dot(hi, gmat) + _dot(lo, gmat)


def _x_block_specs(n_sub, rows):
    specs = [pl.BlockSpec((rows, D_MODEL), lambda i: (jnp.maximum(n_sub * i - 1, 0), 0))]
    for b in range(1, n_sub):
        specs.append(pl.BlockSpec((rows, D_MODEL), functools.partial(lambda i, b: (n_sub * i - 1 + b, 0), b=b)))
    return specs


def _position():
    return lax.axis_index("x"), lax.axis_index("y"), lax.axis_index("c")


def _gather_weights(wi, wo, small):
    def body(wi_ref, wo_ref, sm_ref, gwi_ref, gwo_ref, gsm_ref, send_sems, recv_sems):
        x, y, c = _position()
        sibling = (x, y, 1 - c)
        chips = [(1 - x, y), (x, 1 - y), (1 - x, 1 - y)]

        def remote(k, src, dst, to):
            return pltpu.make_async_remote_copy(src_ref=src, dst_ref=dst, send_sem=send_sems.at[k],
                                                recv_sem=recv_sems.at[k], device_id=to, device_id_type=MESH)

        first, passed, landed = [], [], []
        for a, (src_ref, g_ref) in enumerate(((wi_ref, gwi_ref), (wo_ref, gwo_ref))):
            for j, (cx, cy) in enumerate(chips):
                slot = g_ref.at[j, c]
                first.append(remote(6 * a + j, src_ref.at[c], slot, (cx, cy, c)))
                landed.append(remote(6 * a + j, slot, slot, sibling))
                passed.append(remote(6 * a + 3 + j, slot, slot, sibling))
        for j, (cx, cy) in enumerate(chips):
            first.append(remote(12 + j, sm_ref, gsm_ref.at[j], (cx, cy, c)))
        for cp in first:
            cp.start()
        for arrived, onward in zip(landed, passed):
            arrived.wait_recv()
            onward.start()
        for a, g_ref in enumerate((gwi_ref, gwo_ref)):
            for j in range(3):
                remote(6 * a + 3 + j, g_ref.at[j, 1 - c], g_ref.at[j, 1 - c], sibling).wait_recv()
        for j in range(3):
            remote(12 + j, sm_ref, gsm_ref.at[j], sibling).wait_recv()
        for cp in first + passed:
            cp.wait_send()

    return pl.pallas_call(
        body, name="gather_weights",
        out_shape=(jax.ShapeDtypeStruct((3,) + wi.shape, wi.dtype), jax.ShapeDtypeStruct((3,) + wo.shape, wo.dtype),
                   jax.ShapeDtypeStruct((3,) + small.shape, small.dtype)),
        in_specs=[ANY, ANY, ANY], out_specs=(ANY, ANY, ANY),
        scratch_shapes=[pltpu.SemaphoreType.DMA((15,)), pltpu.SemaphoreType.DMA((15,))],
    )(wi, wo, small)


def _pair_exchange(gw, gb):
    def body(gw_ref, gb_ref, ra_ref, rb_ref, send_sems, recv_sems):
        x, y, c = _position()
        sibling = (x, y, 1 - c)
        copies = [pltpu.make_async_remote_copy(
            src_ref=gb_ref.at[:, 1 - c], dst_ref=rb_ref, send_sem=send_sems.at[N_CHIPS], recv_sem=recv_sems.at[N_CHIPS],
            device_id=sibling, device_id_type=MESH)]
        for s, start in enumerate(WIN_START):
            rows = pl.ds(pl.multiple_of(start + WIN_HALF * (1 - c), 2 * SUBLANE), WIN_HALF)
            copies.append(pltpu.make_async_remote_copy(
                src_ref=gw_ref.at[rows], dst_ref=ra_ref.at[s], send_sem=send_sems.at[s], recv_sem=recv_sems.at[s],
                device_id=sibling, device_id_type=MESH))
        for cp in copies:
            cp.start()
        for cp in copies:
            cp.wait()

    return pl.pallas_call(
        body, name="grad_pair_exchange",
        out_shape=(jax.ShapeDtypeStruct((N_CHIPS, WIN_HALF, D_MODEL), gw.dtype),
                   jax.ShapeDtypeStruct((N_CHIPS,) + gb.shape[2:], gb.dtype)),
        in_specs=[ANY, ANY], out_specs=(ANY, ANY),
        scratch_shapes=[pltpu.SemaphoreType.DMA((N_CHIPS + 1,)), pltpu.SemaphoreType.DMA((N_CHIPS + 1,))],
    )(gw, gb)


def _chip_exchange(pa, pb):
    def body(pa_ref, pb_ref, ra_ref, rb_ref, send_sems, recv_sems):
        x, y, c = _position()
        chips = [(1 - x, y), (x, 1 - y), (1 - x, 1 - y)]
        copies = []
        for a, (src, dst) in enumerate(((pa_ref, ra_ref), (pb_ref, rb_ref))):
            for j, (cx, cy) in enumerate(chips):
                copies.append(pltpu.make_async_remote_copy(
                    src_ref=src.at[2 * cx + cy], dst_ref=dst.at[j], send_sem=send_sems.at[3 * a + j],
                    recv_sem=recv_sems.at[3 * a + j], device_id=(cx, cy, c), device_id_type=MESH))
        for cp in copies:
            cp.start()
        for cp in copies:
            cp.wait()

    return pl.pallas_call(
        body, name="grad_chip_exchange",
        out_shape=(jax.ShapeDtypeStruct((3,) + pa.shape[1:], pa.dtype),
                   jax.ShapeDtypeStruct((3,) + pb.shape[1:], pb.dtype)),
        in_specs=[ANY, ANY], out_specs=(ANY, ANY),
        scratch_shapes=[pltpu.SemaphoreType.DMA((6,)), pltpu.SemaphoreType.DMA((6,))],
    )(pa, pb)


def _pair_share(ha, hb):
    def body(ha_ref, hb_ref, oa_ref, ob_ref, send_sems, recv_sems):
        x, y, c = _position()
        copies = [pltpu.make_async_remote_copy(
            src_ref=src, dst_ref=dst, send_sem=send_sems.at[k], recv_sem=recv_sems.at[k],
            device_id=(x, y, 1 - c), device_id_type=MESH)
            for k, (src, dst) in enumerate(((ha_ref, oa_ref), (hb_ref, ob_ref)))]
        for cp in copies:
            cp.start()
        for cp in copies:
            cp.wait()

    return pl.pallas_call(
        body, name="grad_pair_share",
        out_shape=(jax.ShapeDtypeStruct(ha.shape, ha.dtype), jax.ShapeDtypeStruct(hb.shape, hb.dtype)),
        in_specs=[ANY, ANY], out_specs=(ANY, ANY),
        scratch_shapes=[pltpu.SemaphoreType.DMA((2,)), pltpu.SemaphoreType.DMA((2,))],
    )(ha, hb)


def _gather_small(pack):
    def body(p_ref, o_ref, send_sems, recv_sems):
        x, y, c = _position()
        copies = []
        for mask in range(1, N_DEV):
            peer = (1 - x if mask & 4 else x, 1 - y if mask & 2 else y, 1 - c if mask & 1 else c)
            copies.append(pltpu.make_async_remote_copy(
                src_ref=p_ref, dst_ref=o_ref.at[mask - 1], send_sem=send_sems.at[mask - 1],
                recv_sem=recv_sems.at[mask - 1], device_id=peer, device_id_type=MESH))
        for cp in copies:
            cp.start()
        for cp in copies:
            cp.wait()

    return pl.pallas_call(
        body, name="gather_small",
        out_shape=jax.ShapeDtypeStruct((N_DEV - 1,) + pack.shape, pack.dtype),
        in_specs=[ANY], out_specs=ANY,
        scratch_shapes=[pltpu.SemaphoreType.DMA((N_DEV - 1,)), pltpu.SemaphoreType.DMA((N_DEV - 1,))],
    )(pack)


def _pair_sum(mine, recv, c_idx):
    rows, cols = mine.shape[2:]

    def body(c_ref, a_ref, b_ref, o_ref, send_ref):
        total = a_ref[...] + b_ref[...]
        o_ref[...] = total
        send_ref[...] = total.astype(send_ref.dtype)

    out_spec = pl.BlockSpec((None, rows, cols), lambda s, c_ref: (s, 0, 0))
    return pl.pallas_call(
        body, name="grad_pair_sum",
        grid_spec=pltpu.PrefetchScalarGridSpec(
            num_scalar_prefetch=1, grid=(N_CHIPS,),
            in_specs=[pl.BlockSpec((None, None, rows, cols), lambda s, c_ref: (s, c_ref[0], 0, 0)),
                      pl.BlockSpec((None, rows, cols), lambda s, c_ref: (s, 0, 0))],
            out_specs=(out_spec, out_spec)),
        out_shape=(jax.ShapeDtypeStruct(recv.shape, recv.dtype), jax.ShapeDtypeStruct(recv.shape, WIRE_DTYPE)),
        compiler_params=_params(("parallel",)),
    )(c_idx, mine, recv)


def _window_start(s):
    return jnp.where(s == 0, WIN_START[0], jnp.where(s == 1, WIN_START[1], jnp.where(s == 2, WIN_START[2], WIN_START[3])))


def _pair_sum_windows(gw, recv, c_idx):
    tr = WIN_HALF // 3

    def body(c_ref, a_ref, b_ref, o_ref, send_ref):
        total = a_ref[...] + b_ref[...].astype(F32)
        o_ref[...] = total
        send_ref[...] = total.astype(send_ref.dtype)

    out_spec = pl.BlockSpec((None, tr, D_MODEL), lambda s, i, c_ref: (s, i, 0))
    return pl.pallas_call(
        body, name="grad_pair_sum_windows",
        grid_spec=pltpu.PrefetchScalarGridSpec(
            num_scalar_prefetch=1, grid=(N_CHIPS, WIN_HALF // tr),
            in_specs=[pl.BlockSpec((pl.Element(tr), pl.Element(D_MODEL)),
                                   lambda s, i, c_ref: (pl.multiple_of(
                                       _window_start(s) + WIN_HALF * c_ref[0] + tr * i, SUBLANE), 0)),
                      pl.BlockSpec((None, tr, D_MODEL), lambda s, i, c_ref: (s, i, 0))],
            out_specs=(out_spec, out_spec)),
        out_shape=(jax.ShapeDtypeStruct(recv.shape, F32), jax.ShapeDtypeStruct(recv.shape, WIRE_DTYPE)),
        compiler_params=_params(("parallel", "parallel")),
    )(c_idx, gw, recv)


def _assemble_w(own, others, starts):
    def body(starts_ref, own_ref, oth_ref, o_ref):
        o_ref[...] = jnp.zeros_like(o_ref)
        for k in range(N_CHIPS):
            rows = pl.ds(pl.multiple_of(starts_ref[k], 2 * SUBLANE), WIN_ROWS)
            o_ref[rows, :] = o_ref[rows, :] + (own_ref[...] if k == 0 else oth_ref[k - 1])

    return pl.pallas_call(
        body, name="assemble_w",
        in_specs=[pl.BlockSpec(memory_space=pltpu.SMEM), pl.BlockSpec(memory_space=pltpu.VMEM),
                  pl.BlockSpec(memory_space=pltpu.VMEM)],
        out_specs=pl.BlockSpec(memory_space=pltpu.VMEM),
        out_shape=jax.ShapeDtypeStruct((D_IN_PAD, D_MODEL), own.dtype),
        compiler_params=_params(),
    )(starts, own, others)


def _chip_sum(psum, recv3, chip_idx):
    rows, cols = psum.shape[1:]
    tr = rows // 2

    def body(s_ref, p_ref, r0, r1, r2, o_ref):
        o_ref[...] = ((p_ref[...] + r0[...].astype(F32)) + r1[...].astype(F32)) + r2[...].astype(F32)

    return pl.pallas_call(
        body, name="grad_chip_sum",
        grid_spec=pltpu.PrefetchScalarGridSpec(
            num_scalar_prefetch=1, grid=(2,),
            in_specs=[pl.BlockSpec((None, tr, cols), lambda i, s_ref: (s_ref[0], i, 0))] +
                     [pl.BlockSpec((None, tr, cols), functools.partial(lambda i, s_ref, j: (j, i, 0), j=j))
                      for j in range(3)],
            out_specs=pl.BlockSpec((tr, cols), lambda i, s_ref: (i, 0))),
        out_shape=jax.ShapeDtypeStruct((rows, cols), psum.dtype),
        compiler_params=_params(("parallel",)),
    )(chip_idx, psum, recv3, recv3, recv3)


def _adamw_math(w, g, m, v):
    m = ADAM_B1 * m + (1.0 - ADAM_B1) * g
    v = ADAM_B2 * v + (1.0 - ADAM_B2) * (g * g)
    m_hat = m * (1.0 / (1.0 - ADAM_B1 ** ADAM_STEP))
    v_hat = v * (1.0 / (1.0 - ADAM_B2 ** ADAM_STEP))
    delta = -ADAM_LR * (m_hat / (jnp.sqrt(v_hat) + ADAM_EPS) + ADAM_WD * w)
    return delta, m, v


def _adamw_big(w, g, m, v, tr):
    rows, cols = w.shape
    assert rows % tr == 0 and g.shape[0] >= rows

    def body(w_ref, g_ref, m_ref, v_ref, d_out, m_out, v_out):
        d, m2, v2 = _adamw_math(w_ref[...], g_ref[...], m_ref[...], v_ref[...])
        d_out[...] = d
        m_out[...] = m2
        v_out[...] = v2

    spec = pl.BlockSpec((tr, cols), lambda i: (i, 0))
    sds = jax.ShapeDtypeStruct((rows, cols), F32)
    return pl.pallas_call(
        body, name="adamw_big", grid=(rows // tr,), in_specs=[spec] * 4, out_specs=(spec,) * 3,
        out_shape=(sds,) * 3, compiler_params=_params(("parallel",)),
    )(w, g, m, v)


def _adamw_rows(w3, g, m3, v3):
    rows, _, cols = w3.shape
    tc = 2 * LANE

    def body(w_ref, g_ref, m_ref, v_ref, g_out, d_out, m_out, v_out):
        g = g_ref[...]
        d, m2, v2 = _adamw_math(w_ref[:, 0, :], g, m_ref[:, 0, :], v_ref[:, 0, :])
        g_out[:, 0, :] = g
        d_out[:, 0, :] = d
        m_out[:, 0, :] = m2
        v_out[:, 0, :] = v2

    spec3 = pl.BlockSpec((rows, 1, tc), lambda i: (0, 0, i))
    sds = jax.ShapeDtypeStruct((rows, 1, cols), F32)
    return pl.pallas_call(
        body, name="adamw_rows", grid=(cols // tc,),
        in_specs=[spec3, pl.BlockSpec((rows, tc), lambda i: (0, i)), spec3, spec3], out_specs=(spec3,) * 4,
        out_shape=(sds,) * 4, compiler_params=_params(("parallel",)),
    )(w3, g, m3, v3)


def _small_update(own, others, params, ms, vs):
    slots = (SLOT_NORM, SLOT_FINAL, SLOT_ATTN, SLOT_CONVG, SLOT_BF, SLOT_META, SLOT_CONVW)
    n = len(slots)

    def body(*refs):
        own_ref, gp_ref = refs[:2]
        refs = refs[1:]
        w_refs, m_refs, v_refs = refs[1:1 + n], refs[1 + n:1 + 2 * n], refs[1 + 2 * n:1 + 3 * n]
        outs = refs[1 + 3 * n:2 + 7 * n]
        loss_ref = outs[0]
        g_outs, d_outs, m_outs, v_outs = (outs[1 + k * n:1 + (k + 1) * n] for k in range(4))
        g_scr, w_scr, m_scr, v_scr = refs[2 + 7 * n:]
        x, y, c = _position()
        shard = 2 * x + y
        me = 4 * x + 2 * y + c
        tot = None
        for d in range(N_DEV):
            rel = jnp.bitwise_xor(me, d)
            term = jnp.where(rel == 0, own_ref[...], gp_ref[jnp.maximum(rel, 1) - 1])
            tot = term if tot is None else tot + term
        r0, r1, _, _ = SLOT_META
        meta_sel = tot[r0:r1, 0:256]
        cw_sel = tot[24:32, 0:128]
        for k in range(1, N_CHIPS):
            meta_sel = jnp.where(shard == k, tot[r0:r1, 256 * k:256 * (k + 1)], meta_sel)
            cw_sel = jnp.where(shard == k, tot[24:32, 128 * k:128 * (k + 1)], cw_sel)
        zeros = jnp.zeros((PACK_ROWS, D_MODEL), F32)
        for scr in (g_scr, w_scr, m_scr, v_scr):
            scr[...] = zeros
        g_scr[0:8, :] = tot[0:8, :]
        g_scr[r0:r1, 0:256] = meta_sel
        g_scr[24:32, 0:128] = cw_sel
        for (a, b, c0, c1), w_ref, m_ref, v_ref in zip(slots, w_refs, m_refs, v_refs):
            w_scr[a:b, c0:c1] = w_ref[...]
            m_scr[a:b, c0:c1] = m_ref[...]
            v_scr[a:b, c0:c1] = v_ref[...]
        loss_ref[...] = g_scr[LOSS_ROW:LOSS_ROW + 1, 0:1]
        d, m2, v2 = _adamw_math(w_scr[...], g_scr[...], m_scr[...], v_scr[...])
        w_scr[...] = d
        m_scr[...] = m2
        v_scr[...] = v2
        for (a, b, c0, c1), g_o, d_o, m_o, v_o in zip(slots, g_outs, d_outs, m_outs, v_outs):
            g_o[...] = g_scr[a:b, c0:c1]
            d_o[...] = w_scr[a:b, c0:c1]
            m_o[...] = m_scr[a:b, c0:c1]
            v_o[...] = v_scr[a:b, c0:c1]

    shapes = [jax.ShapeDtypeStruct(p.shape, F32) for p in params]
    out = pl.pallas_call(
        body, name="small_update",
        out_shape=[jax.ShapeDtypeStruct((1, 1), F32)] + shapes * 4,
        scratch_shapes=[pltpu.VMEM((PACK_ROWS, D_MODEL), F32)] * 4,
        compiler_params=_params(),
    )(own, others, *params, *ms, *vs)
    return out[0], out[1:1 + n], out[1 + n:1 + 2 * n], out[1 + 2 * n:1 + 3 * n], out[1 + 3 * n:1 + 4 * n]


def _in_proj(x2, meta_blk, norm_g, w_pad, bf_pad):
    seq = x2.shape[0]
    lp = seq + FRONT
    t = ROW_TILE
    nt = lp // t
    n_sub = t // LANE

    def body(*refs):
        x_refs = refs[:n_sub]
        mb, g_ref, w_ref, bf_ref, tri_ref = refs[n_sub:n_sub + 5]
        q_ref, k_ref, v_ref, rest_ref, fl_ref, ct_ref, u_ref, carry = refs[n_sub + 5:]
        i = pl.program_id(0)

        @pl.when(i == 0)
        def _():
            carry[...] = jnp.zeros_like(carry)

        first = jnp.where(i == 0, mb[...], x_refs[0][...])
        h = jnp.concatenate([first] + [r[...] for r in x_refs[1:]], axis=0)
        ms = jnp.mean(h * h, axis=-1, keepdims=True)
        u = ((h * lax.rsqrt(ms + EPS)) * g_ref[...]).astype(MXU_DTYPE)
        u_ref[...] = u

        def seg(a, width):
            return _dot_nt(u, w_ref[a:a + width, :])

        q_ref[...] = (seg(SEG_Q, D_ATTN) * (HEAD_DIM ** -0.5)).astype(MXU_DTYPE)
        k_ref[...] = seg(SEG_K, D_ATTN).astype(MXU_DTYPE)
        v_ref[...] = seg(SEG_V, D_ATTN).astype(MXU_DTYPE)
        for s in range(5):
            rest_ref[:, 512 * s:512 * (s + 1)] = seg(SEG_ZA + 512 * s, 512)
        fl = seg(SEG_F, LANE)
        fl_ref[...] = fl
        z = fl + bf_ref[...]
        logf = jnp.minimum(z, 0.0) - jnp.log(1.0 + jnp.exp(-jnp.abs(z)))
        row = i * t + lax.broadcasted_iota(jnp.int32, (t, LANE), 0)
        logf = jnp.where(row >= PAD_ROWS, logf, 0.0)
        cs = _dot_exact(tri_ref[...], logf) + carry[...]
        carry[...] = carry[...] + jnp.sum(logf, axis=0, keepdims=True)
        col = i * t + lax.broadcasted_iota(jnp.int32, (SUBLANE, t), 1)
        ct_ref[...] = jnp.where(col >= PAD_ROWS, cs.T[0:SUBLANE, :], -NEG)

    row_blk = lambda cols: pl.BlockSpec((t, cols), lambda i: (i, 0))
    const = lambda shape: pl.BlockSpec(shape, lambda i: (0, 0))
    return pl.pallas_call(
        body, name="in_proj", grid=(nt,),
        in_specs=_x_block_specs(n_sub, LANE) + [const((LANE, D_MODEL)), const((1, D_MODEL)),
                                                pl.BlockSpec((D_IN_PAD, D_MODEL), lambda i: (0, 0),
                                                             pipeline_mode=pl.Buffered(1)),
                                                const((1, LANE)), const((t, t))],
        out_specs=(row_blk(D_ATTN), row_blk(D_ATTN), row_blk(D_ATTN), row_blk(5 * 512), row_blk(LANE),
                   pl.BlockSpec((SUBLANE, t), lambda i: (0, i)), row_blk(D_MODEL)),
        out_shape=(jax.ShapeDtypeStruct((lp, D_ATTN), MXU_DTYPE), jax.ShapeDtypeStruct((lp, D_ATTN), MXU_DTYPE),
                   jax.ShapeDtypeStruct((lp, D_ATTN), MXU_DTYPE), jax.ShapeDtypeStruct((lp, 5 * 512), F32),
                   jax.ShapeDtypeStruct((lp, LANE), F32),
                   jax.ShapeDtypeStruct((SUBLANE, lp), F32), jax.ShapeDtypeStruct((lp, D_MODEL), MXU_DTYPE)),
        scratch_shapes=[pltpu.VMEM((1, LANE), F32)],
        compiler_params=_params(("arbitrary",)),
    )(*([x2] * n_sub), meta_blk, norm_g, w_pad, bf_pad, _triangle(t, lower=True))


def _head_masks():
    lane = lax.broadcasted_iota(jnp.int32, (1, LANE), 1)
    return lane < HEAD_DIM, lane >= HEAD_DIM


def _pair_specs(lp, nt, t):
    blk = pl.BlockSpec((lp, LANE), lambda g: (0, g))
    ct_a = pl.BlockSpec((None, nt, 1, t), lambda g: (2 * g, 0, 0, 0))
    ct_b = pl.BlockSpec((None, nt, 1, t), lambda g: (2 * g + 1, 0, 0, 0))
    return blk, ct_a, ct_b


def _sub_rows(s, col):
    return jnp.concatenate([s[:, a * LANE:(a + 1) * LANE] - col for a in range(s.shape[1] // LANE)], axis=1)


def _loop_unrolled(lo, hi, step, init, n):
    def group(jj, carry):
        for k in range(n):
            carry = step(lo + n * jj + k, carry)
        return carry

    groups = (hi - lo) // n
    carry = lax.fori_loop(0, groups, group, init)
    return lax.fori_loop(lo + n * groups, hi, step, carry)


def _lane_chunks(s):
    return [s[:, a * LANE:(a + 1) * LANE] for a in range(s.shape[1] // LANE)]


def _attn_fwd(q, k, v, ct4):
    lp = q.shape[0]
    t = ROW_TILE
    nt = lp // t

    def body(q_ref, k_ref, v_ref, cta_ref, ctb_ref, o_ref, l_ref, m_ref, s_scr):
        masks = _head_masks()
        ct_refs = (cta_ref, ctb_ref)
        below = lax.broadcasted_iota(jnp.int32, (t, t), 1) <= lax.broadcasted_iota(jnp.int32, (t, t), 0)
        lane = lax.broadcasted_iota(jnp.int32, (1, LANE), 1)
        head_of_row = lax.broadcasted_iota(jnp.int32, (2 * t, LANE), 0) >= t
        ones_cols = jnp.where(lax.broadcasted_iota(jnp.int32, (2 * t, LANE), 1) == head_of_row.astype(jnp.int32),
                              1.0, 0.0).astype(MXU_DTYPE)

        def q_block(i, _):
            r0 = pl.multiple_of(i * t, t)
            qi = q_ref[pl.ds(r0, t), :]

            def scores(j):
                kj = k_ref[pl.ds(pl.multiple_of(j * t, t), t), :]
                return _dot_nt(qi, jnp.concatenate([jnp.where(hm, kj, 0).astype(MXU_DTYPE) for hm in masks], axis=0))

            def biased(j, hh, s2, diagonal):
                s = (s2[:, hh * t:(hh + 1) * t] - ct_refs[hh][j]) * LOG2E
                return jnp.where(below, s, NEG) if diagonal else s

            def max_step(j, carry, diagonal):
                s2 = scores(j)
                out = []
                for hh, m in enumerate(carry):
                    s = biased(j, hh, s2, diagonal)
                    s_scr[j, :, hh * t:(hh + 1) * t] = s
                    for c in _lane_chunks(s):
                        m = jnp.maximum(m, c)
                    out.append(m)
                return tuple(out)

            lanes_neg = jnp.full((t, LANE), NEG, F32)
            carry = _loop_unrolled(0, i, functools.partial(max_step, diagonal=False), (lanes_neg, lanes_neg),
                                   ATTN_UNROLL)
            ms = [jnp.max(m, axis=-1, keepdims=True) for m in max_step(i, carry, True)]

            def sum_step(j, acc):
                vj = v_ref[pl.ds(pl.multiple_of(j * t, t), t), :]
                v2 = jnp.concatenate([jnp.where(hm, vj, 0).astype(MXU_DTYPE) for hm in masks], axis=0)
                parts = [jnp.exp2(s_scr[j, :, hh * t:(hh + 1) * t] - ms[hh]).astype(MXU_DTYPE) for hh in range(2)]
                return acc + _dot(jnp.concatenate(parts, axis=1), jnp.concatenate([v2, ones_cols], axis=1))

            acc = _loop_unrolled(0, i + 1, sum_step, jnp.zeros((t, 2 * LANE), F32), ATTN_UNROLL)
            sums = acc[:, LANE:]
            l_pair = jnp.where(masks[0], jnp.sum(jnp.where(lane == 0, sums, 0.0), axis=-1, keepdims=True),
                               jnp.sum(jnp.where(lane == 1, sums, 0.0), axis=-1, keepdims=True))
            o_ref[pl.ds(r0, t), :] = acc[:, :LANE] / l_pair
            l_ref[pl.ds(r0, t), :] = l_pair
            m_ref[pl.ds(r0, t), 0:LANE] = jnp.broadcast_to(ms[0], (t, LANE))
            m_ref[pl.ds(r0, t), LANE:2 * LANE] = jnp.broadcast_to(ms[1], (t, LANE))
            return 0

        lax.fori_loop(0, nt, q_block, 0)

    blk, ct_a, ct_b = _pair_specs(lp, nt, t)
    return pl.pallas_call(
        body, name="attn_fwd", grid=(HEADS // 2,),
        in_specs=[blk, blk, blk, ct_a, ct_b], out_specs=(blk, blk, pl.BlockSpec((lp, 2 * LANE), lambda g: (0, g))),
        out_shape=(jax.ShapeDtypeStruct((lp, D_ATTN), F32), jax.ShapeDtypeStruct((lp, D_ATTN), F32),
                   jax.ShapeDtypeStruct((lp, HEADS * LANE), F32)),
        scratch_shapes=[pltpu.VMEM((nt, t, 2 * t), F32)],
        compiler_params=_params(("parallel",)),
    )(q, k, v, ct4, ct4)


def _attn_bwd(q, k, v, do, m, delta, ct4):
    lp = q.shape[0]
    t = ROW_TILE
    nt = lp // t

    def body(q_ref, k_ref, v_ref, do_ref, ma_ref, mb_ref, dla_ref, dlb_ref, cta_ref, ctb_ref,
             dq_ref, dk_ref, dv_ref, dc_ref, dq_acc, dk_acc, dv_acc):
        masks = _head_masks()
        ct_refs, m_refs, dl_refs = (cta_ref, ctb_ref), (ma_ref, mb_ref), (dla_ref, dlb_ref)
        below = lax.broadcasted_iota(jnp.int32, (t, t), 1) <= lax.broadcasted_iota(jnp.int32, (t, t), 0)
        tn = (((0,), (0,)), ((), ()))
        dq_acc[...] = jnp.zeros_like(dq_acc)

        def k_block(j, _):
            c0 = pl.multiple_of(j * t, t)
            kj = k_ref[pl.ds(c0, t), :]
            vj = v_ref[pl.ds(c0, t), :]
            k2 = jnp.concatenate([jnp.where(hm, kj, 0).astype(MXU_DTYPE) for hm in masks], axis=0)
            v2 = jnp.concatenate([jnp.where(hm, vj, 0).astype(MXU_DTYPE) for hm in masks], axis=0)
            ck = [r[j] for r in ct_refs]
            dk_acc[...] = jnp.zeros_like(dk_acc)
            dv_acc[...] = jnp.zeros_like(dv_acc)

            def q_block(i, colsums, diagonal):
                r0 = pl.multiple_of(i * t, t)
                qi = q_ref[pl.ds(r0, t), :]
                doi = do_ref[pl.ds(r0, t), :]
                q2 = jnp.concatenate([jnp.where(hm, qi, 0).astype(MXU_DTYPE) for hm in masks], axis=0)
                do2 = jnp.concatenate([jnp.where(hm, doi, 0).astype(MXU_DTYPE) for hm in masks], axis=0)
                s2 = _dot_nt(qi, k2)
                dp2 = _dot_nt(doi, v2)
                out, ps, dss = [], [], []
                for hh in range(2):
                    s = (s2[:, hh * t:(hh + 1) * t] - ck[hh]) * LOG2E
                    if diagonal:
                        s = jnp.where(below, s, NEG)
                    p = jnp.exp2(_sub_rows(s, m_refs[hh][pl.ds(r0, t), :])).astype(MXU_DTYPE)
                    ds32 = p.astype(F32) * _sub_rows(dp2[:, hh * t:(hh + 1) * t], dl_refs[hh][pl.ds(r0, t), :])
                    ps.append(p)
                    dss.append(ds32.astype(MXU_DTYPE))
                    out.append(colsums[hh] + jnp.sum(ds32, axis=0, keepdims=True))
                dv_acc[...] = dv_acc[...] + lax.dot_general(jnp.concatenate(ps, axis=0), do2, tn,
                                                            preferred_element_type=F32)
                dk_acc[...] = dk_acc[...] + lax.dot_general(jnp.concatenate(dss, axis=0), q2, tn,
                                                            preferred_element_type=F32)
                dq_acc[pl.ds(r0, t), :] = dq_acc[pl.ds(r0, t), :] + _dot(jnp.concatenate(dss, axis=1), k2)
                return tuple(out)

            colsums = q_block(j, (jnp.zeros((1, t), F32), jnp.zeros((1, t), F32)), True)
            colsums = _loop_unrolled(j + 1, nt, functools.partial(q_block, diagonal=False), colsums, 2)
            for hh in range(2):
                dc_ref[hh, j] = -colsums[hh]
            dk_ref[pl.ds(c0, t), :] = dk_acc[...].astype(dk_ref.dtype)
            dv_ref[pl.ds(c0, t), :] = dv_acc[...].astype(dv_ref.dtype)
            return 0

        lax.fori_loop(0, nt, k_block, 0)
        dq_ref[...] = (dq_acc[...] * (HEAD_DIM ** -0.5)).astype(dq_ref.dtype)

    blk, ct_a, ct_b = _pair_specs(lp, nt, t)
    rep_a = pl.BlockSpec((lp, LANE), lambda g: (0, 2 * g))
    rep_b = pl.BlockSpec((lp, LANE), lambda g: (0, 2 * g + 1))
    return pl.pallas_call(
        body, name="attn_bwd", grid=(HEADS // 2,),
        in_specs=[blk] * 4 + [rep_a, rep_b, rep_a, rep_b, ct_a, ct_b],
        out_specs=(blk, blk, blk, pl.BlockSpec((2, nt, 1, t), lambda g: (g, 0, 0, 0))),
        out_shape=(jax.ShapeDtypeStruct((lp, D_ATTN), MXU_DTYPE),) * 3
                  + (jax.ShapeDtypeStruct((HEADS, nt, 1, t), F32),),
        scratch_shapes=[pltpu.VMEM((lp, LANE), F32), pltpu.VMEM((t, LANE), F32), pltpu.VMEM((t, LANE), F32)],
        compiler_params=_params(("parallel",)),
    )(q, k, v, do, m, m, delta, delta, ct4, ct4)


def _shift_down(prev8, cur, k):
    ext = jnp.concatenate([prev8, cur], axis=0)
    return pltpu.roll(ext, k, 0)[SUBLANE:, :]


def _shift_up(cur, next8, k):
    ext = jnp.concatenate([cur, next8], axis=0)
    n = ext.shape[0]
    return pltpu.roll(ext, n - k, 0)[:cur.shape[0], :]


def _post(o, l_sum, rest, x2, meta_blk, tgt2, w_out, attn_g, conv_g, final_g, conv_w8):
    lp = o.shape[0]
    t = ROW_TILE
    nt = lp // t
    n_sub = t // LANE
    hb = t // SUBLANE

    def body(*refs):
        o_ref, l_ref, za_ref, gb_ref, gc_ref, xc_ref, zc_ref, gch_ref, xch_ref = refs[:9]
        refs = refs[1:]
        x_refs = refs[8:8 + n_sub]
        mb = refs[8 + n_sub]
        t_refs = refs[9 + n_sub:9 + 2 * n_sub]
        wo_ref, ag_ref, cg_ref, fg_ref, cw_ref, gm_ref, hr_ref = refs[9 + 2 * n_sub:16 + 2 * n_sub]
        (dout_ref, do_ref, dl_ref, dza_ref, dgb_ref, dzc_ref, dcv_ref,
         loss_ref, gf_ref, gag_ref, gcg_ref, gwo_ref) = refs[16 + 2 * n_sub:]
        i = pl.program_id(0)

        @pl.when(i == 0)
        def _():
            for r in (loss_ref, gf_ref, gag_ref, gcg_ref, gwo_ref):
                r[...] = jnp.zeros_like(r)

        gmat = gm_ref[...]
        inv_g = 1.0 / HEAD_DIM
        o_v = o_ref[...]
        ra = lax.rsqrt(_group_sum(o_v * o_v, gmat, STAT_TERMS) * inv_g + EPS)
        n_a = o_v * ra
        a_n = n_a * ag_ref[...]
        za = za_ref[...]
        sig_a = _sigmoid(za)
        sz_a = za * sig_a
        y_a = a_n * sz_a
        gb = gb_ref[...]
        gc = gc_ref[...]
        xc = xc_ref[...]
        cx = gc * xc
        cx_prev = jnp.where(i == 0, 0.0, gch_ref[...] * xch_ref[...])
        conv = (cw_ref[0:1, :] * _shift_down(cx_prev, cx, 2) + cw_ref[1:2, :] * _shift_down(cx_prev, cx, 1)
                + cw_ref[2:3, :] * cx)
        e = gb * conv
        re = lax.rsqrt(_group_sum(e * e, gmat, STAT_TERMS) * inv_g + EPS)
        n_e = e * re
        e_n = n_e * cg_ref[...]
        zc = zc_ref[...]
        sig_c = _sigmoid(zc)
        sz_c = zc * sig_c
        y_c = e_n * sz_c
        mix = jnp.concatenate([y_a, y_c], axis=-1)
        mix_b = mix.astype(MXU_DTYPE)
        first = jnp.where(i == 0, mb[...], x_refs[0][...])
        h = jnp.concatenate([first] + [r[...] for r in x_refs[1:]], axis=0)
        out = h + _dot(mix_b, wo_ref[...])
        r2 = lax.rsqrt(jnp.mean(out * out, axis=-1, keepdims=True) + EPS)
        n_f = out * r2
        y = n_f * fg_ref[...]
        tgt = jnp.concatenate([r[...] for r in t_refs], axis=0)
        valid = (i * t + lax.broadcasted_iota(jnp.int32, (t, 1), 0)) >= FRONT
        diff = jnp.where(valid, y - tgt, 0.0)
        loss_ref[...] = loss_ref[...] + 0.5 * jnp.sum(jnp.sum(diff * diff, axis=-1, keepdims=True) * (1.0 / D_MODEL))
        dy = diff * (1.0 / D_MODEL)
        gf_ref[...] = gf_ref[...] + jnp.sum(dy * n_f, axis=0, keepdims=True)
        dn = dy * fg_ref[...]
        d_out = r2 * (dn - n_f * jnp.mean(dn * n_f, axis=-1, keepdims=True))
        dout_ref[...] = d_out
        d_out_b = d_out.astype(MXU_DTYPE)
        d_mix = _dot_nt(d_out_b, wo_ref[...])
        gwo_ref[...] = gwo_ref[...] + _dot(mix.T.astype(MXU_DTYPE), d_out_b)
        d_ya = d_mix[:, :D_ATTN]
        d_yc = d_mix[:, D_ATTN:]
        d_an = d_ya * sz_a
        dza_ref[...] = (d_ya * a_n * (sig_a * (1.0 + za * (1.0 - sig_a)))).astype(dza_ref.dtype)
        gag_ref[...] = gag_ref[...] + jnp.sum(d_an * n_a, axis=0, keepdims=True)
        dn_a = d_an * ag_ref[...]
        d_o = ra * (dn_a - n_a * (_group_sum(dn_a * n_a, gmat, STAT_TERMS) * inv_g))
        d_o_b = (d_o / l_ref[...]).astype(do_ref.dtype)
        do_ref[...] = d_o_b
        dl_ref[...] = _group_sum(d_o_b.astype(F32) * o_v, hr_ref[...])
        d_en = d_yc * sz_c
        dzc_ref[...] = (d_yc * e_n * (sig_c * (1.0 + zc * (1.0 - sig_c)))).astype(dzc_ref.dtype)
        gcg_ref[...] = gcg_ref[...] + jnp.sum(d_en * n_e, axis=0, keepdims=True)
        dn_e = d_en * cg_ref[...]
        d_e = re * (dn_e - n_e * (_group_sum(dn_e * n_e, gmat, STAT_TERMS) * inv_g))
        dgb_ref[...] = (d_e * conv).astype(dgb_ref.dtype)
        dcv_ref[...] = d_e * gb

    head_rep = jnp.where((lax.broadcasted_iota(jnp.int32, (D_ATTN, HEADS * LANE), 0) >> 6)
                         == (lax.broadcasted_iota(jnp.int32, (D_ATTN, HEADS * LANE), 1) >> 7), 1.0, 0.0).astype(MXU_DTYPE)
    row_blk = lambda cols: pl.BlockSpec((t, cols), lambda i: (i, 0))
    rest_blk = lambda s: pl.BlockSpec((t, 512), functools.partial(lambda i, s: (i, s), s=s))
    halo = lambda s: pl.BlockSpec((SUBLANE, 512), functools.partial(lambda i, s: (jnp.maximum(i * hb - 1, 0), s), s=s))
    const = lambda shape: pl.BlockSpec(shape, lambda i: (0, 0))
    acc = lambda shape: pl.BlockSpec(shape, lambda i: (0, 0))
    return pl.pallas_call(
        body, name="post_fwd_bwd", grid=(nt,),
        in_specs=[row_blk(D_ATTN), row_blk(D_ATTN)] + [rest_blk(s) for s in range(5)] + [halo(2), halo(3)]
                 + _x_block_specs(n_sub, LANE) + [const((LANE, D_MODEL))] + _x_block_specs(n_sub, LANE)
                 + [const((D_MODEL, D_MODEL)), const((1, D_ATTN)), const((1, D_CONV)), const((1, D_MODEL)),
                    const((SUBLANE, D_CONV)), const((D_ATTN, D_ATTN)), const((D_ATTN, HEADS * LANE))],
        out_specs=(row_blk(D_MODEL), row_blk(D_ATTN), row_blk(HEADS * LANE), row_blk(D_ATTN), row_blk(D_CONV),
                   row_blk(D_CONV), row_blk(D_CONV),
                   acc((1, LANE)), acc((1, D_MODEL)), acc((1, D_ATTN)), acc((1, D_CONV)), acc((D_MODEL, D_MODEL))),
        out_shape=(jax.ShapeDtypeStruct((lp, D_MODEL), F32), jax.ShapeDtypeStruct((lp, D_ATTN), MXU_DTYPE),
                   jax.ShapeDtypeStruct((lp, HEADS * LANE), F32), jax.ShapeDtypeStruct((lp, D_ATTN), MXU_DTYPE),
                   jax.ShapeDtypeStruct((lp, D_CONV), MXU_DTYPE), jax.ShapeDtypeStruct((lp, D_CONV), MXU_DTYPE),
                   jax.ShapeDtypeStruct((lp, D_CONV), F32),
                   jax.ShapeDtypeStruct((1, LANE), F32), jax.ShapeDtypeStruct((1, D_MODEL), F32),
                   jax.ShapeDtypeStruct((1, D_ATTN), F32), jax.ShapeDtypeStruct((1, D_CONV), F32),
                   jax.ShapeDtypeStruct((D_MODEL, D_MODEL), F32)),
        compiler_params=_params(("arbitrary",)),
    )(o, l_sum, *([rest] * 5), rest, rest, *([x2] * n_sub), meta_blk, *([tgt2] * n_sub),
      w_out, attn_g, conv_g, final_g, conv_w8, _group_matrix(), head_rep)


def _bwd_in(x2, meta_blk, norm_g, w_pad, bf_pad, fl, dc, dq, dk, dv, dza, dgb, dzc, dconv, rest, d_out, conv_w8):
    lp = fl.shape[0]
    t = ROW_TILE
    nt = lp // t
    n_sub = t // LANE
    hb = t // SUBLANE
    rev = lambda i: nt - 1 - i

    def body(*refs):
        x_refs = refs[:n_sub]
        (mb, g_ref, w_ref, bf_ref, fl_ref, dc_ref, dq_ref, dk_ref, dv_ref, dza_ref, dgb_ref, dzc_ref,
         dcv_ref, dcvn_ref, gc_ref, xc_ref, gch_ref, xch_ref, dout_ref, cw_ref, tri_ref) = refs[n_sub:n_sub + 21]
        dp_ref, gx_ref, front_ref, gn_ref, gbf_ref, gcw_ref, carry, dh_scr, gx_sems = refs[n_sub + 21:]
        step = pl.program_id(0)
        i = rev(step)

        @pl.when(step == 0)
        def _():
            for r in (gn_ref, gbf_ref, gcw_ref, carry):
                r[...] = jnp.zeros_like(r)

        dc8 = jnp.concatenate([dc_ref[...], jnp.zeros((LANE - HEADS, t), F32)], axis=0).T
        dlogf = _dot_exact(tri_ref[...], dc8) + carry[...]
        carry[...] = carry[...] + jnp.sum(dc8, axis=0, keepdims=True)
        z = fl_ref[...] + bf_ref[...]
        row = i * t + lax.broadcasted_iota(jnp.int32, (t, LANE), 0)
        d_f = jnp.where(row >= PAD_ROWS, dlogf * (1.0 / (1.0 + jnp.exp(z))), 0.0)
        gbf_ref[...] = gbf_ref[...] + jnp.sum(d_f, axis=0, keepdims=True)
        dcv = dcv_ref[...]
        dcv_next = jnp.where(i == nt - 1, 0.0, dcvn_ref[...])
        d_cx = (cw_ref[2:3, :] * dcv + cw_ref[1:2, :] * _shift_up(dcv, dcv_next, 1)
                + cw_ref[0:1, :] * _shift_up(dcv, dcv_next, 2))
        gc = gc_ref[...]
        xc = xc_ref[...]
        cx = gc * xc
        cx_prev = jnp.where(i == 0, 0.0, gch_ref[...] * xch_ref[...])
        rowi = lax.broadcasted_iota(jnp.int32, (SUBLANE, 1), 0)
        gcw = (jnp.where(rowi == 0, jnp.sum(dcv * _shift_down(cx_prev, cx, 2), axis=0, keepdims=True), 0.0)
               + jnp.where(rowi == 1, jnp.sum(dcv * _shift_down(cx_prev, cx, 1), axis=0, keepdims=True), 0.0)
               + jnp.where(rowi == 2, jnp.sum(dcv * cx, axis=0, keepdims=True), 0.0))
        gcw_ref[...] = gcw_ref[...] + gcw
        dp_ref[:, SEG_Q:SEG_Q + 512] = dq_ref[...]
        dp_ref[:, SEG_K:SEG_K + 512] = dk_ref[...]
        dp_ref[:, SEG_V:SEG_V + 512] = dv_ref[...]
        dp_ref[:, SEG_F:SEG_F + LANE] = d_f.astype(dp_ref.dtype)
        dp_ref[:, SEG_ZA:SEG_ZA + 512] = dza_ref[...]
        dp_ref[:, SEG_GB:SEG_GB + 512] = dgb_ref[...]
        dp_ref[:, SEG_GC:SEG_GC + 512] = (d_cx * xc).astype(dp_ref.dtype)
        dp_ref[:, SEG_XC:SEG_XC + 512] = (d_cx * gc).astype(dp_ref.dtype)
        dp_ref[:, SEG_ZC:SEG_ZC + 512] = dzc_ref[...]
        d_u = _dot(dp_ref[...], w_ref[...])
        first = jnp.where(i == 0, mb[...], x_refs[0][...])
        h = jnp.concatenate([first] + [r[...] for r in x_refs[1:]], axis=0)
        r1 = lax.rsqrt(jnp.mean(h * h, axis=-1, keepdims=True) + EPS)
        n_h = h * r1
        gn_ref[...] = gn_ref[...] + jnp.sum(d_u * n_h, axis=0, keepdims=True)
        dn = d_u * g_ref[...]
        d_h = dout_ref[...] + r1 * (dn - n_h * jnp.mean(dn * n_h, axis=-1, keepdims=True))
        slot = step % 2

        def to_grad_x(slot_, tile):
            return pltpu.make_async_copy(dh_scr.at[slot_], gx_ref.at[pl.ds(pl.multiple_of(tile * t - FRONT, SUBLANE), t)],
                                         gx_sems.at[slot_])

        @pl.when(step >= 2)
        def _():
            to_grad_x(slot, 1).wait()

        dh_scr[slot] = d_h

        @pl.when(i > 0)
        def _():
            to_grad_x(slot, i).start()

        @pl.when(i == 0)
        def _():
            front_ref[...] = d_h[:FRONT]
            rest_rows = pltpu.make_async_copy(dh_scr.at[slot, pl.ds(FRONT, t - FRONT)], gx_ref.at[pl.ds(0, t - FRONT)],
                                              gx_sems.at[slot])
            rest_rows.start()
            rest_rows.wait()
            if nt >= 2:
                to_grad_x(1 - slot, 1).wait()

    def x_specs():
        specs = [pl.BlockSpec((LANE, D_MODEL), lambda s: (jnp.maximum(n_sub * rev(s) - 1, 0), 0))]
        for b in range(1, n_sub):
            specs.append(pl.BlockSpec((LANE, D_MODEL), functools.partial(lambda s, b: (n_sub * rev(s) - 1 + b, 0), b=b)))
        return specs

    row_blk = lambda cols: pl.BlockSpec((t, cols), lambda s: (rev(s), 0))
    rest_blk = lambda k: pl.BlockSpec((t, 512), functools.partial(lambda s, k: (rev(s), k), k=k))
    halo_prev = lambda k: pl.BlockSpec(
        (SUBLANE, 512), functools.partial(lambda s, k: (jnp.maximum(rev(s) * hb - 1, 0), k), k=k))
    halo_next = pl.BlockSpec((SUBLANE, 512), lambda s: (jnp.minimum((rev(s) + 1) * hb, lp // SUBLANE - 1), 0))
    const = lambda shape: pl.BlockSpec(shape, lambda s: (0, 0))
    return pl.pallas_call(
        body, name="bwd_in", grid=(nt,),
        in_specs=x_specs() + [const((LANE, D_MODEL)), const((1, D_MODEL)),
                              pl.BlockSpec((D_IN_PAD, D_MODEL), lambda s: (0, 0), pipeline_mode=pl.Buffered(1)),
                              const((1, LANE)), row_blk(LANE),
                              pl.BlockSpec((HEADS, t), lambda s: (0, rev(s))),
                              row_blk(512), row_blk(512), row_blk(512), row_blk(512), row_blk(512), row_blk(512),
                              row_blk(512), halo_next, rest_blk(2), rest_blk(3), halo_prev(2), halo_prev(3),
                              row_blk(D_MODEL), const((SUBLANE, D_CONV)), const((t, t))],
        out_specs=(row_blk(D_IN_PAD), ANY, const((FRONT, D_MODEL)), const((1, D_MODEL)), const((1, LANE)),
                   const((SUBLANE, D_CONV))),
        out_shape=(jax.ShapeDtypeStruct((lp, D_IN_PAD), MXU_DTYPE), jax.ShapeDtypeStruct((lp - FRONT, D_MODEL), F32),
                   jax.ShapeDtypeStruct((FRONT, D_MODEL), F32),
                   jax.ShapeDtypeStruct((1, D_MODEL), F32), jax.ShapeDtypeStruct((1, LANE), F32),
                   jax.ShapeDtypeStruct((SUBLANE, D_CONV), F32)),
        scratch_shapes=[pltpu.VMEM((1, LANE), F32), pltpu.VMEM((2, t, D_MODEL), F32), pltpu.SemaphoreType.DMA((2,))],
        compiler_params=_params(("arbitrary",)),
    )(*([x2] * n_sub), meta_blk, norm_g, w_pad, bf_pad, fl, dc, dq, dk, dv, dza, dgb, dzc, dconv, dconv,
      rest, rest, rest, rest, d_out, conv_w8, _triangle(t, lower=False))


def _grad_w_in(u, dproj):
    lp = u.shape[0]
    tn = GW_COL_TILE
    tk = tn if lp % tn == 0 else ROW_TILE

    def body(d_ref, u_ref, o_ref, wire_ref):
        k = pl.program_id(1)

        @pl.when(k == 0)
        def _():
            o_ref[...] = jnp.zeros_like(o_ref)

        o_ref[...] = o_ref[...] + lax.dot_general(d_ref[...], u_ref[...], (((0,), (0,)), ((), ())),
                                                  preferred_element_type=F32)

        @pl.when(k == pl.num_programs(1) - 1)
        def _():
            wire_ref[...] = o_ref[...].astype(wire_ref.dtype)

    out_spec = pl.BlockSpec((tn, D_MODEL), lambda n, k: (n, 0))
    return pl.pallas_call(
        body, name="grad_w_in", grid=(D_IN_PAD // tn, lp // tk),
        in_specs=[pl.BlockSpec((tk, tn), lambda n, k: (k, n)), pl.BlockSpec((tk, D_MODEL), lambda n, k: (k, 0))],
        out_specs=(out_spec, out_spec),
        out_shape=(jax.ShapeDtypeStruct((D_IN_PAD, D_MODEL), F32), jax.ShapeDtypeStruct((D_IN_PAD, D_MODEL), WIRE_DTYPE)),
        compiler_params=_params(("parallel", "arbitrary")),
    )(dproj, u)


def _by_chip(own, others, me):
    by_mask = jnp.stack([own, others[1], others[0], others[2]])
    return [lax.dynamic_index_in_dim(by_mask, jnp.bitwise_xor(me, s), 0, keepdims=False) for s in range(N_CHIPS)]


def _both_halves(mine, other, c):
    return jnp.where(c == 0, jnp.concatenate([mine, other], axis=0), jnp.concatenate([other, mine], axis=0))


def _local_step(x2, tgt2, meta_full, norm_g, w_pad, b_f, conv_w_full, attn_g, conv_g, w_out_full, final_g):
    lp = x2.shape[0] + FRONT
    nt = lp // ROW_TILE
    meta_blk = jnp.concatenate([jnp.zeros((PAD_ROWS, D_MODEL), F32), meta_full], axis=0)
    bf_pad = jnp.pad(b_f, ((0, 0), (0, LANE - HEADS)))
    conv_w8 = jnp.pad(conv_w_full, ((0, SUBLANE - conv_w_full.shape[0]), (0, 0)))
    q, k, v, rest, fl, ct, u = _in_proj(x2, meta_blk, norm_g, w_pad, bf_pad)
    ct4 = ct.reshape(SUBLANE, nt, 1, ROW_TILE)
    o, l_sum, m_max = _attn_fwd(q, k, v, ct4)
    (d_out, d_o, delta, dza, dgb, dzc, dconv, loss, g_final, g_attn, g_convg, gw_out) = _post(
        o, l_sum, rest, x2, meta_blk, tgt2, w_out_full, attn_g, conv_g, final_g, conv_w8)
    dq, dk, dv, dc = _attn_bwd(q, k, v, d_o, m_max, delta, ct4)
    dproj, grad_x, d_front, g_norm, g_bf, g_cw = _bwd_in(x2, meta_blk, norm_g, w_pad, bf_pad, fl, dc.reshape(HEADS, lp), dq, dk, dv,
                                             dza, dgb, dzc, dconv, rest, d_out, conv_w8)
    gw_in, gw_in_wire = _grad_w_in(u, dproj)
    return dict(loss=loss, grad_x=grad_x, d_front=d_front, g_norm=g_norm, g_final=g_final, g_attn=g_attn, g_convg=g_convg, g_bf=g_bf,
                g_cw=g_cw, gw_out=gw_out, gw_in=gw_in, gw_in_wire=gw_in_wire)


def kernel(x, meta, norm_g, w_in, b_f, conv_w, attn_norm_g, conv_norm_g, w_out, final_norm_g, loss_target, m_meta, m_norm_g, m_w_in, m_b_f, m_conv_w, m_attn_norm_g, m_conv_norm_g, m_w_out, m_final_norm_g, v_meta, v_norm_g, v_w_in, v_b_f, v_conv_w, v_attn_norm_g, v_conv_norm_g, v_w_out, v_final_norm_g):
    cx_, cy_, cc_ = _position()
    chip = 2 * cx_ + cy_
    shard = w_in.shape[2]
    out_half = w_out.shape[1] // 2
    pick = lambda vals: jnp.where(chip == 0, vals[0], jnp.where(chip == 1, vals[1], jnp.where(chip == 2, vals[2], vals[3])))
    a_off, b_off = pick(A_OFF), pick(B_OFF)
    wt = jnp.transpose(w_in[0]).astype(MXU_DTYPE)
    wi = lax.dynamic_update_slice_in_dim(
        lax.dynamic_update_slice_in_dim(jnp.zeros((WIN_ROWS, D_MODEL), MXU_DTYPE), wt[:PIECE_A], a_off, 0),
        wt[PIECE_A:], b_off, 0)
    wo = w_out[0].astype(MXU_DTYPE)
    small = jnp.concatenate([meta, jnp.pad(conv_w[0], ((0, 8 - conv_w.shape[1]), (0, meta.shape[1] - conv_w.shape[2])))],
                            axis=0)
    gwi, gwo, gsm = _gather_weights(wi.reshape(2, WIN_HALF, D_MODEL), wo.reshape(2, out_half, D_MODEL), small)
    starts = jnp.stack([_window_start(jnp.bitwise_xor(chip, mask)) for mask in (0, 2, 1, 3)]).astype(jnp.int32)
    w_pad = _assemble_w(wi, gwi.reshape(3, WIN_ROWS, D_MODEL), starts)
    w_out_full = jnp.concatenate(_by_chip(wo, gwo.reshape(3, 2 * out_half, D_MODEL), chip), axis=0)
    small_full = jnp.concatenate(_by_chip(small, gsm, chip), axis=1)
    meta_full = small_full[:N_META]
    conv_w_full = jnp.concatenate([small_full[N_META:N_META + 3, 256 * s:256 * s + LANE] for s in range(N_CHIPS)], axis=1)
    final_g2 = final_norm_g.reshape(1, D_MODEL)
    r = _local_step(x[0], loss_target[0], meta_full, norm_g, w_pad, b_f, conv_w_full, attn_norm_g, conv_norm_g,
                    w_out_full, final_g2)
    grad_x = r["grad_x"][None]
    gb = r["gw_out"].reshape(N_CHIPS, 2, out_half, D_MODEL)
    ra, rb = _pair_exchange(r["gw_in_wire"], gb)
    c_idx = jnp.reshape(cc_, (1,)).astype(jnp.int32)
    chip_idx = jnp.reshape(chip, (1,)).astype(jnp.int32)
    pa, pa_wire = _pair_sum_windows(r["gw_in"], ra, c_idx)
    pb, pb_wire = _pair_sum(gb, rb, c_idx)
    xa, xb = _chip_exchange(pa_wire, pb_wire)
    ha = _chip_sum(pa, xa, chip_idx)
    hb = _chip_sum(pb, xb, chip_idx)
    oa, ob = _pair_share(ha, hb)
    g_window = _both_halves(ha, oa, cc_)
    g_w_in_t = jnp.concatenate([lax.dynamic_slice_in_dim(g_window, a_off, PIECE_A, 0),
                                lax.dynamic_slice_in_dim(g_window, b_off, shard - PIECE_A, 0)], axis=0)
    g_w_out = _both_halves(hb, ob, cc_)
    as_rows = lambda a: jnp.transpose(a, (2, 0, 1))
    g_w_in, d_w_in, nm_w_in, nv_w_in = (jnp.transpose(a, (1, 2, 0)) for a in _adamw_rows(
        as_rows(w_in), g_w_in_t, as_rows(m_w_in), as_rows(v_w_in)))
    d_w_out, nm_w_out, nv_w_out = (a[None] for a in _adamw_big(w_out[0], g_w_out, m_w_out[0], v_w_out[0], LANE))
    wide = lambda a: jnp.pad(a, ((0, 0), (0, D_MODEL - a.shape[1])))
    pack = jnp.concatenate([
        r["g_norm"], r["g_final"], jnp.concatenate([r["g_attn"], r["g_convg"]], axis=1), wide(r["g_bf"]),
        wide(r["loss"]), jnp.zeros((3, D_MODEL), F32), r["d_front"][PAD_ROWS:], wide(r["g_cw"])], axis=0)
    params = (norm_g, final_g2, attn_norm_g, conv_norm_g, b_f, meta, conv_w[0])
    ms = (m_norm_g, m_final_norm_g.reshape(1, D_MODEL), m_attn_norm_g, m_conv_norm_g, m_b_f, m_meta, m_conv_w[0])
    vs = (v_norm_g, v_final_norm_g.reshape(1, D_MODEL), v_attn_norm_g, v_conv_norm_g, v_b_f, v_meta, v_conv_w[0])
    loss, g_s, d_s, m_s, v_s = _small_update(pack, _gather_small(pack), params, ms, vs)

    def ordered(small_list, big_in, big_out):
        s_norm, s_final, s_attn, s_convg, s_bf, s_meta, s_cw = small_list
        return (s_meta, s_norm, big_in, s_bf, s_cw[None], s_attn, s_convg, big_out, s_final.reshape(D_MODEL))

    return (loss.reshape(()), grad_x,
            *ordered(g_s, g_w_in, g_w_out[None]), *ordered(d_s, d_w_in, d_w_out),
            *ordered(m_s, nm_w_in, nm_w_out), *ordered(v_s, nv_w_in, nv_w_out))
```

```python
import functools

import jax
import jax.numpy as jnp
from jax import lax
from jax.experimental import pallas as pl
from jax.experimental.pallas import tpu as pltpu

F32 = jnp.float32
MXU_DTYPE = jnp.bfloat16
WIRE_DTYPE = jnp.bfloat16

D_MODEL = 1024
N_META = 16
HEADS = 8
HEAD_DIM = 64
D_ATTN = HEADS * HEAD_DIM
D_CONV = 512
EPS = 1e-6
LANE = 128
SUBLANE = 8
ROW_TILE = 384
ATTN_UNROLL = 3
STAT_TERMS = 1
FRONT = LANE
PAD_ROWS = FRONT - N_META
NEG = -1e30
LOG2E = 1.4426950408889634
N_CHIPS = 4
N_DEV = 8
VMEM_LIMIT_BYTES = 60 * 1024 * 1024

SEG_Q, SEG_K, SEG_V, SEG_F, SEG_ZA, SEG_GB, SEG_GC, SEG_XC, SEG_ZC = (
    0, 512, 1024, 1536, 1664, 2176, 2688, 3200, 3712)
D_IN = 4104
D_IN_PAD = 4224
F_END = 1544
GW_COL_TILE = 1408
WIN_ROWS = 1152
WIN_HALF = WIN_ROWS // 2
GATHER_CHUNKS = 4
WIN_START = (0, 1024, 2160, 3072)
PIECE_A = 518
A_OFF = (0, 2, 12, 126)
B_OFF = (518, 640, 530, 644)
ADAM_LR = 0.001
ADAM_B1 = 0.9
ADAM_B2 = 0.999
ADAM_EPS = 1e-08
ADAM_WD = 0.01
ADAM_STEP = 10

MESH = pl.DeviceIdType.MESH
ANY = pl.BlockSpec(memory_space=pl.ANY)

PACK_ROWS = 32
SLOT_NORM = (0, 1, 0, 1024)
SLOT_FINAL = (1, 2, 0, 1024)
SLOT_ATTN = (2, 3, 0, 512)
SLOT_CONVG = (2, 3, 512, 1024)
SLOT_BF = (3, 4, 0, 8)
SLOT_META = (8, 24, 0, 256)
SLOT_CONVW = (24, 27, 0, 128)
LOSS_ROW = 4


def _params(sem=None):
    return pltpu.CompilerParams(dimension_semantics=sem, vmem_limit_bytes=VMEM_LIMIT_BYTES)


def _sigmoid(z):
    return 1.0 / (1.0 + jnp.exp(-z))


def _dot(a, b):
    return jnp.dot(a, b, preferred_element_type=F32)


def _dot_nt(a, b):
    return lax.dot_general(a, b, (((1,), (1,)), ((), ())), preferred_element_type=F32)


def _dot_exact(ones, x):
    ones = ones.astype(MXU_DTYPE)
    total = None
    for _ in range(3):
        term = x.astype(MXU_DTYPE)
        x = x - term.astype(F32)
        total = _dot(ones, term) if total is None else total + _dot(ones, term)
    return total


def _group_matrix():
    r = lax.broadcasted_iota(jnp.int32, (D_ATTN, D_ATTN), 0) >> 6
    c = lax.broadcasted_iota(jnp.int32, (D_ATTN, D_ATTN), 1) >> 6
    return jnp.where(r == c, 1.0, 0.0).astype(MXU_DTYPE)


def _triangle(n, lower):
    r = lax.broadcasted_iota(jnp.int32, (n, n), 0)
    c = lax.broadcasted_iota(jnp.int32, (n, n), 1)
    return jnp.where((r >= c) if lower else (c >= r), 1.0, 0.0).astype(MXU_DTYPE)


def _group_sum(x, gmat, terms=2):
    hi = x.astype(MXU_DTYPE)
    if terms == 1:
        return _dot(hi, gmat)
    lo = (x - hi.astype(F32)).astype(MXU_DTYPE)
    return _dot(hi, gmat) + _dot(lo, gmat)


def _x_block_specs(n_sub, rows):
    specs = [pl.BlockSpec((rows, D_MODEL), lambda i: (jnp.maximum(n_sub * i - 1, 0), 0))]
    for b in range(1, n_sub):
        specs.append(pl.BlockSpec((rows, D_MODEL), functools.partial(lambda i, b: (n_sub * i - 1 + b, 0), b=b)))
    return specs


def _position():
    return lax.axis_index("x"), lax.axis_index("y"), lax.axis_index("c")


def _gather_weights(wi, wo, small):
    n_ch = GATHER_CHUNKS
    rows_ch = wi.shape[1] // n_ch
    n_pieces = 3 * n_ch + 3
    n_sems = 2 * n_pieces + 3

    def body(wi_ref, wo_ref, sm_ref, gwi_ref, gwo_ref, gsm_ref, send_sems, recv_sems):
        x, y, c = _position()
        sibling = (x, y, 1 - c)
        chips = [(1 - x, y), (x, 1 - y), (1 - x, 1 - y)]

        def remote(k, src, dst, to):
            return pltpu.make_async_remote_copy(src_ref=src, dst_ref=dst, send_sem=send_sems.at[k],
                                                recv_sem=recv_sems.at[k], device_id=to, device_id_type=MESH)

        pieces = []
        for ch in range(n_ch):
            rows = pl.ds(ch * rows_ch, rows_ch)
            for j in range(3):
                pieces.append((j, wi_ref.at[c, rows], functools.partial(lambda h, j, rows: gwi_ref.at[j, h, rows], j=j, rows=rows)))
        for j in range(3):
            pieces.append((j, wo_ref.at[c], functools.partial(lambda h, j: gwo_ref.at[j, h], j=j)))
        first, landed, passed, from_sibling = [], [], [], []
        for k, (j, src, land) in enumerate(pieces):
            first.append(remote(k, src, land(c), (*chips[j], c)))
            landed.append(remote(k, land(c), land(c), sibling))
            passed.append(remote(n_pieces + k, land(c), land(c), sibling))
            from_sibling.append(remote(n_pieces + k, land(1 - c), land(1 - c), sibling))
        smalls = [remote(2 * n_pieces + j, sm_ref, gsm_ref.at[j], (*chips[j], c)) for j in range(3)]
        for cp in first + smalls:
            cp.start()
        for arrived, onward in zip(landed, passed):
            arrived.wait_recv()
            onward.start()
        for cp in from_sibling:
            cp.wait_recv()
        for cp in smalls:
            cp.wait_recv()
        for cp in first + passed + smalls:
            cp.wait_send()

    return pl.pallas_call(
        body, name="gather_weights",
        out_shape=(jax.ShapeDtypeStruct((3,) + wi.shape, wi.dtype), jax.ShapeDtypeStruct((3,) + wo.shape, wo.dtype),
                   jax.ShapeDtypeStruct((3,) + small.shape, small.dtype)),
        in_specs=[ANY, ANY, ANY], out_specs=(ANY, ANY, ANY),
        scratch_shapes=[pltpu.SemaphoreType.DMA((n_sems,)), pltpu.SemaphoreType.DMA((n_sems,))],
    )(wi, wo, small)


def _pair_exchange(gw, gb):
    def body(gw_ref, gb_ref, ra_ref, rb_ref, send_sems, recv_sems):
        x, y, c = _position()
        sibling = (x, y, 1 - c)
        copies = [pltpu.make_async_remote_copy(
            src_ref=gb_ref.at[:, 1 - c], dst_ref=rb_ref, send_sem=send_sems.at[N_CHIPS], recv_sem=recv_sems.at[N_CHIPS],
            device_id=sibling, device_id_type=MESH)]
        for s, start in enumerate(WIN_START):
            rows = pl.ds(pl.multiple_of(start + WIN_HALF * (1 - c), 2 * SUBLANE), WIN_HALF)
            copies.append(pltpu.make_async_remote_copy(
                src_ref=gw_ref.at[rows], dst_ref=ra_ref.at[s], send_sem=send_sems.at[s], recv_sem=recv_sems.at[s],
                device_id=sibling, device_id_type=MESH))
        for cp in copies:
            cp.start()
        for cp in copies:
            cp.wait()

    return pl.pallas_call(
        body, name="grad_pair_exchange",
        out_shape=(jax.ShapeDtypeStruct((N_CHIPS, WIN_HALF, D_MODEL), gw.dtype),
                   jax.ShapeDtypeStruct((N_CHIPS,) + gb.shape[2:], gb.dtype)),
        in_specs=[ANY, ANY], out_specs=(ANY, ANY),
        scratch_shapes=[pltpu.SemaphoreType.DMA((N_CHIPS + 1,)), pltpu.SemaphoreType.DMA((N_CHIPS + 1,))],
    )(gw, gb)


def _chip_exchange(pa, pb):
    def body(pa_ref, pb_ref, ra_ref, rb_ref, send_sems, recv_sems):
        x, y, c = _position()
        chips = [(1 - x, y), (x, 1 - y), (1 - x, 1 - y)]
        copies = []
        for a, (src, dst) in enumerate(((pa_ref, ra_ref), (pb_ref, rb_ref))):
            for j, (cx, cy) in enumerate(chips):
                copies.append(pltpu.make_async_remote_copy(
                    src_ref=src.at[2 * cx + cy], dst_ref=dst.at[j], send_sem=send_sems.at[3 * a + j],
                    recv_sem=recv_sems.at[3 * a + j], device_id=(cx, cy, c), device_id_type=MESH))
        for cp in copies:
            cp.start()
        for cp in copies:
            cp.wait()

    return pl.pallas_call(
        body, name="grad_chip_exchange",
        out_shape=(jax.ShapeDtypeStruct((3,) + pa.shape[1:], pa.dtype),
                   jax.ShapeDtypeStruct((3,) + pb.shape[1:], pb.dtype)),
        in_specs=[ANY, ANY], out_specs=(ANY, ANY),
        scratch_shapes=[pltpu.SemaphoreType.DMA((6,)), pltpu.SemaphoreType.DMA((6,))],
    )(pa, pb)


def _pair_share(ha, hb):
    def body(ha_ref, hb_ref, oa_ref, ob_ref, send_sems, recv_sems):
        x, y, c = _position()
        copies = [pltpu.make_async_remote_copy(
            src_ref=src, dst_ref=dst, send_sem=send_sems.at[k], recv_sem=recv_sems.at[k],
            device_id=(x, y, 1 - c), device_id_type=MESH)
            for k, (src, dst) in enumerate(((ha_ref, oa_ref), (hb_ref, ob_ref)))]
        for cp in copies:
            cp.start()
        for cp in copies:
            cp.wait()

    return pl.pallas_call(
        body, name="grad_pair_share",
        out_shape=(jax.ShapeDtypeStruct(ha.shape, ha.dtype), jax.ShapeDtypeStruct(hb.shape, hb.dtype)),
        in_specs=[ANY, ANY], out_specs=(ANY, ANY),
        scratch_shapes=[pltpu.SemaphoreType.DMA((2,)), pltpu.SemaphoreType.DMA((2,))],
    )(ha, hb)


def _gather_small(pack):
    def body(p_ref, o_ref, send_sems, recv_sems):
        x, y, c = _position()
        copies = []
        for mask in range(1, N_DEV):
            peer = (1 - x if mask & 4 else x, 1 - y if mask & 2 else y, 1 - c if mask & 1 else c)
            copies.append(pltpu.make_async_remote_copy(
                src_ref=p_ref, dst_ref=o_ref.at[mask - 1], send_sem=send_sems.at[mask - 1],
                recv_sem=recv_sems.at[mask - 1], device_id=peer, device_id_type=MESH))
        for cp in copies:
            cp.start()
        for cp in copies:
            cp.wait()

    return pl.pallas_call(
        body, name="gather_small",
        out_shape=jax.ShapeDtypeStruct((N_DEV - 1,) + pack.shape, pack.dtype),
        in_specs=[ANY], out_specs=ANY,
        scratch_shapes=[pltpu.SemaphoreType.DMA((N_DEV - 1,)), pltpu.SemaphoreType.DMA((N_DEV - 1,))],
    )(pack)


def _pair_sum(mine, recv, c_idx):
    rows, cols = mine.shape[2:]

    def body(c_ref, a_ref, b_ref, o_ref, send_ref):
        total = a_ref[...] + b_ref[...]
        o_ref[...] = total
        send_ref[...] = total.astype(send_ref.dtype)

    out_spec = pl.BlockSpec((None, rows, cols), lambda s, c_ref: (s, 0, 0))
    return pl.pallas_call(
        body, name="grad_pair_sum",
        grid_spec=pltpu.PrefetchScalarGridSpec(
            num_scalar_prefetch=1, grid=(N_CHIPS,),
            in_specs=[pl.BlockSpec((None, None, rows, cols), lambda s, c_ref: (s, c_ref[0], 0, 0)),
                      pl.BlockSpec((None, rows, cols), lambda s, c_ref: (s, 0, 0))],
            out_specs=(out_spec, out_spec)),
        out_shape=(jax.ShapeDtypeStruct(recv.shape, recv.dtype), jax.ShapeDtypeStruct(recv.shape, WIRE_DTYPE)),
        compiler_params=_params(("parallel",)),
    )(c_idx, mine, recv)


def _window_start(s):
    return jnp.where(s == 0, WIN_START[0], jnp.where(s == 1, WIN_START[1], jnp.where(s == 2, WIN_START[2], WIN_START[3])))


def _pair_sum_windows(gw, recv, c_idx):
    tr = WIN_HALF // 3

    def body(c_ref, a_ref, b_ref, o_ref, send_ref):
        total = a_ref[...] + b_ref[...].astype(F32)
        o_ref[...] = total
        send_ref[...] = total.astype(send_ref.dtype)

    out_spec = pl.BlockSpec((None, tr, D_MODEL), lambda s, i, c_ref: (s, i, 0))
    return pl.pallas_call(
        body, name="grad_pair_sum_windows",
        grid_spec=pltpu.PrefetchScalarGridSpec(
            num_scalar_prefetch=1, grid=(N_CHIPS, WIN_HALF // tr),
            in_specs=[pl.BlockSpec((pl.Element(tr), pl.Element(D_MODEL)),
                                   lambda s, i, c_ref: (pl.multiple_of(
                                       _window_start(s) + WIN_HALF * c_ref[0] + tr * i, SUBLANE), 0)),
                      pl.BlockSpec((None, tr, D_MODEL), lambda s, i, c_ref: (s, i, 0))],
            out_specs=(out_spec, out_spec)),
        out_shape=(jax.ShapeDtypeStruct(recv.shape, F32), jax.ShapeDtypeStruct(recv.shape, WIRE_DTYPE)),
        compiler_params=_params(("parallel", "parallel")),
    )(c_idx, gw, recv)


def _assemble_w(own, others, starts):
    def body(starts_ref, own_ref, oth_ref, o_ref):
        o_ref[...] = jnp.zeros_like(o_ref)
        for k in range(N_CHIPS):
            rows = pl.ds(pl.multiple_of(starts_ref[k], 2 * SUBLANE), WIN_ROWS)
            o_ref[rows, :] = o_ref[rows, :] + (own_ref[...] if k == 0 else oth_ref[k - 1])

    return pl.pallas_call(
        body, name="assemble_w",
        in_specs=[pl.BlockSpec(memory_space=pltpu.SMEM), pl.BlockSpec(memory_space=pltpu.VMEM),
                  pl.BlockSpec(memory_space=pltpu.VMEM)],
        out_specs=pl.BlockSpec(memory_space=pltpu.VMEM),
        out_shape=jax.ShapeDtypeStruct((D_IN_PAD, D_MODEL), own.dtype),
        compiler_params=_params(),
    )(starts, own, others)


def _chip_sum(psum, recv3, chip_idx):
    rows, cols = psum.shape[1:]
    tr = rows // 2

    def body(s_ref, p_ref, r0, r1, r2, o_ref):
        o_ref[...] = ((p_ref[...] + r0[...].astype(F32)) + r1[...].astype(F32)) + r2[...].astype(F32)

    return pl.pallas_call(
        body, name="grad_chip_sum",
        grid_spec=pltpu.PrefetchScalarGridSpec(
            num_scalar_prefetch=1, grid=(2,),
            in_specs=[pl.BlockSpec((None, tr, cols), lambda i, s_ref: (s_ref[0], i, 0))] +
                     [pl.BlockSpec((None, tr, cols), functools.partial(lambda i, s_ref, j: (j, i, 0), j=j))
                      for j in range(3)],
            out_specs=pl.BlockSpec((tr, cols), lambda i, s_ref: (i, 0))),
        out_shape=jax.ShapeDtypeStruct((rows, cols), psum.dtype),
        compiler_params=_params(("parallel",)),
    )(chip_idx, psum, recv3, recv3, recv3)


def _adamw_math(w, g, m, v):
    m = ADAM_B1 * m + (1.0 - ADAM_B1) * g
    v = ADAM_B2 * v + (1.0 - ADAM_B2) * (g * g)
    m_hat = m * (1.0 / (1.0 - ADAM_B1 ** ADAM_STEP))
    v_hat = v * (1.0 / (1.0 - ADAM_B2 ** ADAM_STEP))
    delta = -ADAM_LR * (m_hat / (jnp.sqrt(v_hat) + ADAM_EPS) + ADAM_WD * w)
    return delta, m, v


def _adamw_big(w, g, m, v, tr):
    rows, cols = w.shape
    assert rows % tr == 0 and g.shape[0] >= rows

    def body(w_ref, g_ref, m_ref, v_ref, d_out, m_out, v_out):
        d, m2, v2 = _adamw_math(w_ref[...], g_ref[...], m_ref[...], v_ref[...])
        d_out[...] = d
        m_out[...] = m2
        v_out[...] = v2

    spec = pl.BlockSpec((tr, cols), lambda i: (i, 0))
    sds = jax.ShapeDtypeStruct((rows, cols), F32)
    return pl.pallas_call(
        body, name="adamw_big", grid=(rows // tr,), in_specs=[spec] * 4, out_specs=(spec,) * 3,
        out_shape=(sds,) * 3, compiler_params=_params(("parallel",)),
    )(w, g, m, v)


def _adamw_rows(w3, g, m3, v3):
    rows, _, cols = w3.shape
    tc = 2 * LANE

    def body(w_ref, g_ref, m_ref, v_ref, g_out, d_out, m_out, v_out):
        g = g_ref[...]
        d, m2, v2 = _adamw_math(w_ref[:, 0, :], g, m_ref[:, 0, :], v_ref[:, 0, :])
        g_out[:, 0, :] = g
        d_out[:, 0, :] = d
        m_out[:, 0, :] = m2
        v_out[:, 0, :] = v2

    spec3 = pl.BlockSpec((rows, 1, tc), lambda i: (0, 0, i))
    sds = jax.ShapeDtypeStruct((rows, 1, cols), F32)
    return pl.pallas_call(
        body, name="adamw_rows", grid=(cols // tc,),
        in_specs=[spec3, pl.BlockSpec((rows, tc), lambda i: (0, i)), spec3, spec3], out_specs=(spec3,) * 4,
        out_shape=(sds,) * 4, compiler_params=_params(("parallel",)),
    )(w3, g, m3, v3)


def _small_update(own, others, params, ms, vs):
    slots = (SLOT_NORM, SLOT_FINAL, SLOT_ATTN, SLOT_CONVG, SLOT_BF, SLOT_META, SLOT_CONVW)
    n = len(slots)

    def body(*refs):
        own_ref, gp_ref = refs[:2]
        w_refs, m_refs, v_refs = refs[2:2 + n], refs[2 + n:2 + 2 * n], refs[2 + 2 * n:2 + 3 * n]
        outs = refs[2 + 3 * n:3 + 7 * n]
        loss_ref = outs[0]
        g_outs, d_outs, m_outs, v_outs = (outs[1 + k * n:1 + (k + 1) * n] for k in range(4))
        g_scr, w_scr, m_scr, v_scr = refs[3 + 7 * n:]
        x, y, c = _position()
        shard = 2 * x + y
        me = 4 * x + 2 * y + c
        tot = None
        for d in range(N_DEV):
            rel = jnp.bitwise_xor(me, d)
            term = jnp.where(rel == 0, own_ref[...], gp_ref[jnp.maximum(rel, 1) - 1])
            tot = term if tot is None else tot + term
        r0, r1, _, _ = SLOT_META
        meta_sel = tot[r0:r1, 0:256]
        cw_sel = tot[24:32, 0:128]
        for k in range(1, N_CHIPS):
            meta_sel = jnp.where(shard == k, tot[r0:r1, 256 * k:256 * (k + 1)], meta_sel)
            cw_sel = jnp.where(shard == k, tot[24:32, 128 * k:128 * (k + 1)], cw_sel)
        zeros = jnp.zeros((PACK_ROWS, D_MODEL), F32)
        for scr in (g_scr, w_scr, m_scr, v_scr):
            scr[...] = zeros
        g_scr[0:8, :] = tot[0:8, :]
        g_scr[r0:r1, 0:256] = meta_sel
        g_scr[24:32, 0:128] = cw_sel
        for (a, b, c0, c1), w_ref, m_ref, v_ref in zip(slots, w_refs, m_refs, v_refs):
            w_scr[a:b, c0:c1] = w_ref[...]
            m_scr[a:b, c0:c1] = m_ref[...]
            v_scr[a:b, c0:c1] = v_ref[...]
        loss_ref[...] = g_scr[LOSS_ROW:LOSS_ROW + 1, 0:1]
        d, m2, v2 = _adamw_math(w_scr[...], g_scr[...], m_scr[...], v_scr[...])
        w_scr[...] = d
        m_scr[...] = m2
        v_scr[...] = v2
        for (a, b, c0, c1), g_o, d_o, m_o, v_o in zip(slots, g_outs, d_outs, m_outs, v_outs):
            g_o[...] = g_scr[a:b, c0:c1]
            d_o[...] = w_scr[a:b, c0:c1]
            m_o[...] = m_scr[a:b, c0:c1]
            v_o[...] = v_scr[a:b, c0:c1]

    shapes = [jax.ShapeDtypeStruct(p.shape, F32) for p in params]
    out = pl.pallas_call(
        body, name="small_update",
        out_shape=[jax.ShapeDtypeStruct((1, 1), F32)] + shapes * 4,
        scratch_shapes=[pltpu.VMEM((PACK_ROWS, D_MODEL), F32)] * 4,
        compiler_params=_params(),
    )(own, others, *params, *ms, *vs)
    return out[0], out[1:1 + n], out[1 + n:1 + 2 * n], out[1 + 2 * n:1 + 3 * n], out[1 + 3 * n:1 + 4 * n]


def _in_proj(x2, meta_blk, norm_g, w_pad, bf_pad):
    seq = x2.shape[0]
    lp = seq + FRONT
    t = ROW_TILE
    nt = lp // t
    n_sub = t // LANE

    def body(*refs):
        x_refs = refs[:n_sub]
        mb, g_ref, w_ref, bf_ref, tri_ref = refs[n_sub:n_sub + 5]
        q_ref, k_ref, v_ref, rest_ref, fl_ref, ct_ref, u_ref, carry = refs[n_sub + 5:]
        i = pl.program_id(0)

        @pl.when(i == 0)
        def _():
            carry[...] = jnp.zeros_like(carry)

        first = jnp.where(i == 0, mb[...], x_refs[0][...])
        h = jnp.concatenate([first] + [r[...] for r in x_refs[1:]], axis=0)
        ms = jnp.mean(h * h, axis=-1, keepdims=True)
        u = ((h * lax.rsqrt(ms + EPS)) * g_ref[...]).astype(MXU_DTYPE)
        u_ref[...] = u

        def seg(a, width):
            return _dot_nt(u, w_ref[a:a + width, :])

        q_ref[...] = (seg(SEG_Q, D_ATTN) * (HEAD_DIM ** -0.5)).astype(MXU_DTYPE)
        k_ref[...] = seg(SEG_K, D_ATTN).astype(MXU_DTYPE)
        v_ref[...] = seg(SEG_V, D_ATTN).astype(MXU_DTYPE)
        for s in range(5):
            rest_ref[:, 512 * s:512 * (s + 1)] = seg(SEG_ZA + 512 * s, 512)
        fl = seg(SEG_F, LANE)
        fl_ref[...] = fl
        z = fl + bf_ref[...]
        logf = jnp.minimum(z, 0.0) - jnp.log(1.0 + jnp.exp(-jnp.abs(z)))
        row = i * t + lax.broadcasted_iota(jnp.int32, (t, LANE), 0)
        logf = jnp.where(row >= PAD_ROWS, logf, 0.0)
        cs = _dot_exact(tri_ref[...], logf) + carry[...]
        carry[...] = carry[...] + jnp.sum(logf, axis=0, keepdims=True)
        col = i * t + lax.broadcasted_iota(jnp.int32, (SUBLANE, t), 1)
        ct_ref[...] = jnp.where(col >= PAD_ROWS, cs.T[0:SUBLANE, :], -NEG)

    row_blk = lambda cols: pl.BlockSpec((t, cols), lambda i: (i, 0))
    const = lambda shape: pl.BlockSpec(shape, lambda i: (0, 0))
    return pl.pallas_call(
        body, name="in_proj", grid=(nt,),
        in_specs=_x_block_specs(n_sub, LANE) + [const((LANE, D_MODEL)), const((1, D_MODEL)),
                                                pl.BlockSpec((D_IN_PAD, D_MODEL), lambda i: (0, 0),
                                                             pipeline_mode=pl.Buffered(1)),
                                                const((1, LANE)), const((t, t))],
        out_specs=(row_blk(D_ATTN), row_blk(D_ATTN), row_blk(D_ATTN), row_blk(5 * 512), row_blk(LANE),
                   pl.BlockSpec((SUBLANE, t), lambda i: (0, i)), row_blk(D_MODEL)),
        out_shape=(jax.ShapeDtypeStruct((lp, D_ATTN), MXU_DTYPE), jax.ShapeDtypeStruct((lp, D_ATTN), MXU_DTYPE),
                   jax.ShapeDtypeStruct((lp, D_ATTN), MXU_DTYPE), jax.ShapeDtypeStruct((lp, 5 * 512), F32),
                   jax.ShapeDtypeStruct((lp, LANE), F32),
                   jax.ShapeDtypeStruct((SUBLANE, lp), F32), jax.ShapeDtypeStruct((lp, D_MODEL), MXU_DTYPE)),
        scratch_shapes=[pltpu.VMEM((1, LANE), F32)],
        compiler_params=_params(("arbitrary",)),
    )(*([x2] * n_sub), meta_blk, norm_g, w_pad, bf_pad, _triangle(t, lower=True))


def _head_masks():
    lane = lax.broadcasted_iota(jnp.int32, (1, LANE), 1)
    return lane < HEAD_DIM, lane >= HEAD_DIM


def _pair_specs(lp, nt, t):
    blk = pl.BlockSpec((lp, LANE), lambda g: (0, g))
    ct_a = pl.BlockSpec((None, nt, 1, t), lambda g: (2 * g, 0, 0, 0))
    ct_b = pl.BlockSpec((None, nt, 1, t), lambda g: (2 * g + 1, 0, 0, 0))
    return blk, ct_a, ct_b


def _sub_rows(s, col):
    return jnp.concatenate([s[:, a * LANE:(a + 1) * LANE] - col for a in range(s.shape[1] // LANE)], axis=1)


def _loop_unrolled(lo, hi, step, init, n):
    def group(jj, carry):
        for k in range(n):
            carry = step(lo + n * jj + k, carry)
        return carry

    groups = (hi - lo) // n
    carry = lax.fori_loop(0, groups, group, init)
    return lax.fori_loop(lo + n * groups, hi, step, carry)


def _lane_chunks(s):
    return [s[:, a * LANE:(a + 1) * LANE] for a in range(s.shape[1] // LANE)]


def _attn_fwd(q, k, v, ct4):
    lp = q.shape[0]
    t = ROW_TILE
    nt = lp // t

    def body(q_ref, k_ref, v_ref, cta_ref, ctb_ref, o_ref, l_ref, m_ref, s_scr):
        masks = _head_masks()
        ct_refs = (cta_ref, ctb_ref)
        below = lax.broadcasted_iota(jnp.int32, (t, t), 1) <= lax.broadcasted_iota(jnp.int32, (t, t), 0)
        lane = lax.broadcasted_iota(jnp.int32, (1, LANE), 1)
        head_of_row = lax.broadcasted_iota(jnp.int32, (2 * t, LANE), 0) >= t
        ones_cols = jnp.where(lax.broadcasted_iota(jnp.int32, (2 * t, LANE), 1) == head_of_row.astype(jnp.int32),
                              1.0, 0.0).astype(MXU_DTYPE)

        def q_block(i, _):
            r0 = pl.multiple_of(i * t, t)
            qi = q_ref[pl.ds(r0, t), :]

            def scores(j):
                kj = k_ref[pl.ds(pl.multiple_of(j * t, t), t), :]
                return _dot_nt(qi, jnp.concatenate([jnp.where(hm, kj, 0).astype(MXU_DTYPE) for hm in masks], axis=0))

            def biased(j, hh, s2, diagonal):
                s = (s2[:, hh * t:(hh + 1) * t] - ct_refs[hh][j]) * LOG2E
                return jnp.where(below, s, NEG) if diagonal else s

            def max_step(j, carry, diagonal):
                s2 = scores(j)
                out = []
                for hh, m in enumerate(carry):
                    s = biased(j, hh, s2, diagonal)
                    s_scr[j, :, hh * t:(hh + 1) * t] = s
                    for c in _lane_chunks(s):
                        m = jnp.maximum(m, c)
                    out.append(m)
                return tuple(out)

            lanes_neg = jnp.full((t, LANE), NEG, F32)
            carry = _loop_unrolled(0, i, functools.partial(max_step, diagonal=False), (lanes_neg, lanes_neg),
                                   ATTN_UNROLL)
            ms = [jnp.max(m, axis=-1, keepdims=True) for m in max_step(i, carry, True)]

            def sum_step(j, acc):
                vj = v_ref[pl.ds(pl.multiple_of(j * t, t), t), :]
                v2 = jnp.concatenate([jnp.where(hm, vj, 0).astype(MXU_DTYPE) for hm in masks], axis=0)
                parts = [jnp.exp2(s_scr[j, :, hh * t:(hh + 1) * t] - ms[hh]).astype(MXU_DTYPE) for hh in range(2)]
                return acc + _dot(jnp.concatenate(parts, axis=1), jnp.concatenate([v2, ones_cols], axis=1))

            acc = _loop_unrolled(0, i + 1, sum_step, jnp.zeros((t, 2 * LANE), F32), ATTN_UNROLL)
            sums = acc[:, LANE:]
            l_pair = jnp.where(masks[0], jnp.sum(jnp.where(lane == 0, sums, 0.0), axis=-1, keepdims=True),
                               jnp.sum(jnp.where(lane == 1, sums, 0.0), axis=-1, keepdims=True))
            o_ref[pl.ds(r0, t), :] = acc[:, :LANE] / l_pair
            l_ref[pl.ds(r0, t), :] = l_pair
            m_ref[pl.ds(r0, t), 0:LANE] = jnp.broadcast_to(ms[0], (t, LANE))
            m_ref[pl.ds(r0, t), LANE:2 * LANE] = jnp.broadcast_to(ms[1], (t, LANE))
            return 0

        lax.fori_loop(0, nt, q_block, 0)

    blk, ct_a, ct_b = _pair_specs(lp, nt, t)
    return pl.pallas_call(
        body, name="attn_fwd", grid=(HEADS // 2,),
        in_specs=[blk, blk, blk, ct_a, ct_b], out_specs=(blk, blk, pl.BlockSpec((lp, 2 * LANE), lambda g: (0, g))),
        out_shape=(jax.ShapeDtypeStruct((lp, D_ATTN), F32), jax.ShapeDtypeStruct((lp, D_ATTN), F32),
                   jax.ShapeDtypeStruct((lp, HEADS * LANE), F32)),
        scratch_shapes=[pltpu.VMEM((nt, t, 2 * t), F32)],
        compiler_params=_params(("parallel",)),
    )(q, k, v, ct4, ct4)


def _attn_bwd(q, k, v, do, m, delta, ct4):
    lp = q.shape[0]
    t = ROW_TILE
    nt = lp // t

    def body(q_ref, k_ref, v_ref, do_ref, ma_ref, mb_ref, dla_ref, dlb_ref, cta_ref, ctb_ref,
             dq_ref, dk_ref, dv_ref, dc_ref, dq_acc, dk_acc, dv_acc):
        masks = _head_masks()
        ct_refs, m_refs, dl_refs = (cta_ref, ctb_ref), (ma_ref, mb_ref), (dla_ref, dlb_ref)
        below = lax.broadcasted_iota(jnp.int32, (t, t), 1) <= lax.broadcasted_iota(jnp.int32, (t, t), 0)
        tn = (((0,), (0,)), ((), ()))
        dq_acc[...] = jnp.zeros_like(dq_acc)

        def k_block(j, _):
            c0 = pl.multiple_of(j * t, t)
            kj = k_ref[pl.ds(c0, t), :]
            vj = v_ref[pl.ds(c0, t), :]
            k2 = jnp.concatenate([jnp.where(hm, kj, 0).astype(MXU_DTYPE) for hm in masks], axis=0)
            v2 = jnp.concatenate([jnp.where(hm, vj, 0).astype(MXU_DTYPE) for hm in masks], axis=0)
            ck = [r[j] for r in ct_refs]
            dk_acc[...] = jnp.zeros_like(dk_acc)
            dv_acc[...] = jnp.zeros_like(dv_acc)

            def q_block(i, colsums, diagonal):
                r0 = pl.multiple_of(i * t, t)
                qi = q_ref[pl.ds(r0, t), :]
                doi = do_ref[pl.ds(r0, t), :]
                q2 = jnp.concatenate([jnp.where(hm, qi, 0).astype(MXU_DTYPE) for hm in masks], axis=0)
                do2 = jnp.concatenate([jnp.where(hm, doi, 0).astype(MXU_DTYPE) for hm in masks], axis=0)
                s2 = _dot_nt(qi, k2)
                dp2 = _dot_nt(doi, v2)
                out, ps, dss = [], [], []
                for hh in range(2):
                    s = (s2[:, hh * t:(hh + 1) * t] - ck[hh]) * LOG2E
                    if diagonal:
                        s = jnp.where(below, s, NEG)
                    p = jnp.exp2(_sub_rows(s, m_refs[hh][pl.ds(r0, t), :])).astype(MXU_DTYPE)
                    ds32 = p.astype(F32) * _sub_rows(dp2[:, hh * t:(hh + 1) * t], dl_refs[hh][pl.ds(r0, t), :])
                    ps.append(p)
                    dss.append(ds32.astype(MXU_DTYPE))
                    out.append(colsums[hh] + jnp.sum(ds32, axis=0, keepdims=True))
                dv_acc[...] = dv_acc[...] + lax.dot_general(jnp.concatenate(ps, axis=0), do2, tn,
                                                            preferred_element_type=F32)
                dk_acc[...] = dk_acc[...] + lax.dot_general(jnp.concatenate(dss, axis=0), q2, tn,
                                                            preferred_element_type=F32)
                dq_acc[pl.ds(r0, t), :] = dq_acc[pl.ds(r0, t), :] + _dot(jnp.concatenate(dss, axis=1), k2)
                return tuple(out)

            colsums = q_block(j, (jnp.zeros((1, t), F32), jnp.zeros((1, t), F32)), True)
            colsums = _loop_unrolled(j + 1, nt, functools.partial(q_block, diagonal=False), colsums, 2)
            for hh in range(2):
                dc_ref[hh, j] = -colsums[hh]
            dk_ref[pl.ds(c0, t), :] = dk_acc[...].astype(dk_ref.dtype)
            dv_ref[pl.ds(c0, t), :] = dv_acc[...].astype(dv_ref.dtype)
            return 0

        lax.fori_loop(0, nt, k_block, 0)
        dq_ref[...] = (dq_acc[...] * (HEAD_DIM ** -0.5)).astype(dq_ref.dtype)

    blk, ct_a, ct_b = _pair_specs(lp, nt, t)
    rep_a = pl.BlockSpec((lp, LANE), lambda g: (0, 2 * g))
    rep_b = pl.BlockSpec((lp, LANE), lambda g: (0, 2 * g + 1))
    return pl.pallas_call(
        body, name="attn_bwd", grid=(HEADS // 2,),
        in_specs=[blk] * 4 + [rep_a, rep_b, rep_a, rep_b, ct_a, ct_b],
        out_specs=(blk, blk, blk, pl.BlockSpec((2, nt, 1, t), lambda g: (g, 0, 0, 0))),
        out_shape=(jax.ShapeDtypeStruct((lp, D_ATTN), MXU_DTYPE),) * 3
                  + (jax.ShapeDtypeStruct((HEADS, nt, 1, t), F32),),
        scratch_shapes=[pltpu.VMEM((lp, LANE), F32), pltpu.VMEM((t, LANE), F32), pltpu.VMEM((t, LANE), F32)],
        compiler_params=_params(("parallel",)),
    )(q, k, v, do, m, m, delta, delta, ct4, ct4)


def _shift_down(prev8, cur, k):
    ext = jnp.concatenate([prev8, cur], axis=0)
    return pltpu.roll(ext, k, 0)[SUBLANE:, :]


def _shift_up(cur, next8, k):
    ext = jnp.concatenate([cur, next8], axis=0)
    n = ext.shape[0]
    return pltpu.roll(ext, n - k, 0)[:cur.shape[0], :]


def _post(o, l_sum, rest, x2, meta_blk, tgt2, w_out, attn_g, conv_g, final_g, conv_w8):
    lp = o.shape[0]
    t = ROW_TILE
    nt = lp // t
    n_sub = t // LANE
    hb = t // SUBLANE

    def body(*refs):
        o_ref, l_ref, za_ref, gb_ref, gc_ref, xc_ref, zc_ref, gch_ref, xch_ref = refs[:9]
        x_refs = refs[9:9 + n_sub]
        mb = refs[9 + n_sub]
        t_refs = refs[10 + n_sub:10 + 2 * n_sub]
        wo_ref, ag_ref, cg_ref, fg_ref, cw_ref, gm_ref, hr_ref = refs[10 + 2 * n_sub:17 + 2 * n_sub]
        (dout_ref, do_ref, dl_ref, dza_ref, dgb_ref, dzc_ref, dcv_ref,
         loss_ref, gf_ref, gag_ref, gcg_ref, gwo_ref) = refs[17 + 2 * n_sub:]
        i = pl.program_id(0)

        @pl.when(i == 0)
        def _():
            for r in (loss_ref, gf_ref, gag_ref, gcg_ref, gwo_ref):
                r[...] = jnp.zeros_like(r)

        gmat = gm_ref[...]
        inv_g = 1.0 / HEAD_DIM
        o_v = o_ref[...]
        ra = lax.rsqrt(_group_sum(o_v * o_v, gmat, STAT_TERMS) * inv_g + EPS)
        n_a = o_v * ra
        a_n = n_a * ag_ref[...]
        za = za_ref[...]
        sig_a = _sigmoid(za)
        sz_a = za * sig_a
        y_a = a_n * sz_a
        gb = gb_ref[...]
        gc = gc_ref[...]
        xc = xc_ref[...]
        cx = gc * xc
        cx_prev = jnp.where(i == 0, 0.0, gch_ref[...] * xch_ref[...])
        conv = (cw_ref[0:1, :] * _shift_down(cx_prev, cx, 2) + cw_ref[1:2, :] * _shift_down(cx_prev, cx, 1)
                + cw_ref[2:3, :] * cx)
        e = gb * conv
        re = lax.rsqrt(_group_sum(e * e, gmat, STAT_TERMS) * inv_g + EPS)
        n_e = e * re
        e_n = n_e * cg_ref[...]
        zc = zc_ref[...]
        sig_c = _sigmoid(zc)
        sz_c = zc * sig_c
        y_c = e_n * sz_c
        mix = jnp.concatenate([y_a, y_c], axis=-1)
        mix_b = mix.astype(MXU_DTYPE)
        first = jnp.where(i == 0, mb[...], x_refs[0][...])
        h = jnp.concatenate([first] + [r[...] for r in x_refs[1:]], axis=0)
        out = h + _dot(mix_b, wo_ref[...])
        r2 = lax.rsqrt(jnp.mean(out * out, axis=-1, keepdims=True) + EPS)
        n_f = out * r2
        y = n_f * fg_ref[...]
        tgt = jnp.concatenate([r[...] for r in t_refs], axis=0)
        valid = (i * t + lax.broadcasted_iota(jnp.int32, (t, 1), 0)) >= FRONT
        diff = jnp.where(valid, y - tgt, 0.0)
        loss_ref[...] = loss_ref[...] + 0.5 * jnp.sum(jnp.sum(diff * diff, axis=-1, keepdims=True) * (1.0 / D_MODEL))
        dy = diff * (1.0 / D_MODEL)
        gf_ref[...] = gf_ref[...] + jnp.sum(dy * n_f, axis=0, keepdims=True)
        dn = dy * fg_ref[...]
        d_out = r2 * (dn - n_f * jnp.mean(dn * n_f, axis=-1, keepdims=True))
        dout_ref[...] = d_out
        d_out_b = d_out.astype(MXU_DTYPE)
        d_mix = _dot_nt(d_out_b, wo_ref[...])
        gwo_ref[...] = gwo_ref[...] + _dot(mix.T.astype(MXU_DTYPE), d_out_b)
        d_ya = d_mix[:, :D_ATTN]
        d_yc = d_mix[:, D_ATTN:]
        d_an = d_ya * sz_a
        dza_ref[...] = (d_ya * a_n * (sig_a * (1.0 + za * (1.0 - sig_a)))).astype(dza_ref.dtype)
        gag_ref[...] = gag_ref[...] + jnp.sum(d_an * n_a, axis=0, keepdims=True)
        dn_a = d_an * ag_ref[...]
        d_o = ra * (dn_a - n_a * (_group_sum(dn_a * n_a, gmat, STAT_TERMS) * inv_g))
        d_o_b = (d_o / l_ref[...]).astype(do_ref.dtype)
        do_ref[...] = d_o_b
        dl_ref[...] = _group_sum(d_o_b.astype(F32) * o_v, hr_ref[...])
        d_en = d_yc * sz_c
        dzc_ref[...] = (d_yc * e_n * (sig_c * (1.0 + zc * (1.0 - sig_c)))).astype(dzc_ref.dtype)
        gcg_ref[...] = gcg_ref[...] + jnp.sum(d_en * n_e, axis=0, keepdims=True)
        dn_e = d_en * cg_ref[...]
        d_e = re * (dn_e - n_e * (_group_sum(dn_e * n_e, gmat, STAT_TERMS) * inv_g))
        dgb_ref[...] = (d_e * conv).astype(dgb_ref.dtype)
        dcv_ref[...] = d_e * gb

    head_rep = jnp.where((lax.broadcasted_iota(jnp.int32, (D_ATTN, HEADS * LANE), 0) >> 6)
                         == (lax.broadcasted_iota(jnp.int32, (D_ATTN, HEADS * LANE), 1) >> 7), 1.0, 0.0).astype(MXU_DTYPE)
    row_blk = lambda cols: pl.BlockSpec((t, cols), lambda i: (i, 0))
    rest_blk = lambda s: pl.BlockSpec((t, 512), functools.partial(lambda i, s: (i, s), s=s))
    halo = lambda s: pl.BlockSpec((SUBLANE, 512), functools.partial(lambda i, s: (jnp.maximum(i * hb - 1, 0), s), s=s))
    const = lambda shape: pl.BlockSpec(shape, lambda i: (0, 0))
    acc = lambda shape: pl.BlockSpec(shape, lambda i: (0, 0))
    return pl.pallas_call(
        body, name="post_fwd_bwd", grid=(nt,),
        in_specs=[row_blk(D_ATTN), row_blk(D_ATTN)] + [rest_blk(s) for s in range(5)] + [halo(2), halo(3)]
                 + _x_block_specs(n_sub, LANE) + [const((LANE, D_MODEL))] + _x_block_specs(n_sub, LANE)
                 + [const((D_MODEL, D_MODEL)), const((1, D_ATTN)), const((1, D_CONV)), const((1, D_MODEL)),
                    const((SUBLANE, D_CONV)), const((D_ATTN, D_ATTN)), const((D_ATTN, HEADS * LANE))],
        out_specs=(row_blk(D_MODEL), row_blk(D_ATTN), row_blk(HEADS * LANE), row_blk(D_ATTN), row_blk(D_CONV),
                   row_blk(D_CONV), row_blk(D_CONV),
                   acc((1, LANE)), acc((1, D_MODEL)), acc((1, D_ATTN)), acc((1, D_CONV)), acc((D_MODEL, D_MODEL))),
        out_shape=(jax.ShapeDtypeStruct((lp, D_MODEL), F32), jax.ShapeDtypeStruct((lp, D_ATTN), MXU_DTYPE),
                   jax.ShapeDtypeStruct((lp, HEADS * LANE), F32), jax.ShapeDtypeStruct((lp, D_ATTN), MXU_DTYPE),
                   jax.ShapeDtypeStruct((lp, D_CONV), MXU_DTYPE), jax.ShapeDtypeStruct((lp, D_CONV), MXU_DTYPE),
                   jax.ShapeDtypeStruct((lp, D_CONV), F32),
                   jax.ShapeDtypeStruct((1, LANE), F32), jax.ShapeDtypeStruct((1, D_MODEL), F32),
                   jax.ShapeDtypeStruct((1, D_ATTN), F32), jax.ShapeDtypeStruct((1, D_CONV), F32),
                   jax.ShapeDtypeStruct((D_MODEL, D_MODEL), F32)),
        compiler_params=_params(("arbitrary",)),
    )(o, l_sum, *([rest] * 5), rest, rest, *([x2] * n_sub), meta_blk, *([tgt2] * n_sub),
      w_out, attn_g, conv_g, final_g, conv_w8, _group_matrix(), head_rep)


def _bwd_in(x2, meta_blk, norm_g, w_pad, bf_pad, fl, dc, dq, dk, dv, dza, dgb, dzc, dconv, rest, d_out, conv_w8):
    lp = fl.shape[0]
    t = ROW_TILE
    nt = lp // t
    n_sub = t // LANE
    hb = t // SUBLANE
    rev = lambda i: nt - 1 - i

    def body(*refs):
        x_refs = refs[:n_sub]
        (mb, g_ref, w_ref, bf_ref, fl_ref, dc_ref, dq_ref, dk_ref, dv_ref, dza_ref, dgb_ref, dzc_ref,
         dcv_ref, dcvn_ref, gc_ref, xc_ref, gch_ref, xch_ref, dout_ref, cw_ref, tri_ref) = refs[n_sub:n_sub + 21]
        dp_ref, gx_ref, front_ref, gn_ref, gbf_ref, gcw_ref, carry, dh_scr, gx_sems = refs[n_sub + 21:]
        step = pl.program_id(0)
        i = rev(step)

        @pl.when(step == 0)
        def _():
            for r in (gn_ref, gbf_ref, gcw_ref, carry):
                r[...] = jnp.zeros_like(r)

        dc8 = jnp.concatenate([dc_ref[...], jnp.zeros((LANE - HEADS, t), F32)], axis=0).T
        dlogf = _dot_exact(tri_ref[...], dc8) + carry[...]
        carry[...] = carry[...] + jnp.sum(dc8, axis=0, keepdims=True)
        z = fl_ref[...] + bf_ref[...]
        row = i * t + lax.broadcasted_iota(jnp.int32, (t, LANE), 0)
        d_f = jnp.where(row >= PAD_ROWS, dlogf * (1.0 / (1.0 + jnp.exp(z))), 0.0)
        gbf_ref[...] = gbf_ref[...] + jnp.sum(d_f, axis=0, keepdims=True)
        dcv = dcv_ref[...]
        dcv_next = jnp.where(i == nt - 1, 0.0, dcvn_ref[...])
        d_cx = (cw_ref[2:3, :] * dcv + cw_ref[1:2, :] * _shift_up(dcv, dcv_next, 1)
                + cw_ref[0:1, :] * _shift_up(dcv, dcv_next, 2))
        gc = gc_ref[...]
        xc = xc_ref[...]
        cx = gc * xc
        cx_prev = jnp.where(i == 0, 0.0, gch_ref[...] * xch_ref[...])
        rowi = lax.broadcasted_iota(jnp.int32, (SUBLANE, 1), 0)
        gcw = (jnp.where(rowi == 0, jnp.sum(dcv * _shift_down(cx_prev, cx, 2), axis=0, keepdims=True), 0.0)
               + jnp.where(rowi == 1, jnp.sum(dcv * _shift_down(cx_prev, cx, 1), axis=0, keepdims=True), 0.0)
               + jnp.where(rowi == 2, jnp.sum(dcv * cx, axis=0, keepdims=True), 0.0))
        gcw_ref[...] = gcw_ref[...] + gcw
        dp_ref[:, SEG_Q:SEG_Q + 512] = dq_ref[...]
        dp_ref[:, SEG_K:SEG_K + 512] = dk_ref[...]
        dp_ref[:, SEG_V:SEG_V + 512] = dv_ref[...]
        dp_ref[:, SEG_F:SEG_F + LANE] = d_f.astype(dp_ref.dtype)
        dp_ref[:, SEG_ZA:SEG_ZA + 512] = dza_ref[...]
        dp_ref[:, SEG_GB:SEG_GB + 512] = dgb_ref[...]
        dp_ref[:, SEG_GC:SEG_GC + 512] = (d_cx * xc).astype(dp_ref.dtype)
        dp_ref[:, SEG_XC:SEG_XC + 512] = (d_cx * gc).astype(dp_ref.dtype)
        dp_ref[:, SEG_ZC:SEG_ZC + 512] = dzc_ref[...]
        d_u = _dot(dp_ref[...], w_ref[...])
        first = jnp.where(i == 0, mb[...], x_refs[0][...])
        h = jnp.concatenate([first] + [r[...] for r in x_refs[1:]], axis=0)
        r1 = lax.rsqrt(jnp.mean(h * h, axis=-1, keepdims=True) + EPS)
        n_h = h * r1
        gn_ref[...] = gn_ref[...] + jnp.sum(d_u * n_h, axis=0, keepdims=True)
        dn = d_u * g_ref[...]
        d_h = dout_ref[...] + r1 * (dn - n_h * jnp.mean(dn * n_h, axis=-1, keepdims=True))
        slot = step % 2

        def to_grad_x(slot_, tile):
            return pltpu.make_async_copy(dh_scr.at[slot_], gx_ref.at[pl.ds(pl.multiple_of(tile * t - FRONT, SUBLANE), t)],
                                         gx_sems.at[slot_])

        @pl.when(step >= 2)
        def _():
            to_grad_x(slot, 1).wait()

        dh_scr[slot] = d_h

        @pl.when(i > 0)
        def _():
            to_grad_x(slot, i).start()

        @pl.when(i == 0)
        def _():
            front_ref[...] = d_h[:FRONT]
            rest_rows = pltpu.make_async_copy(dh_scr.at[slot, pl.ds(FRONT, t - FRONT)], gx_ref.at[pl.ds(0, t - FRONT)],
                                              gx_sems.at[slot])
            rest_rows.start()
            rest_rows.wait()
            if nt >= 2:
                to_grad_x(1 - slot, 1).wait()

    def x_specs():
        specs = [pl.BlockSpec((LANE, D_MODEL), lambda s: (jnp.maximum(n_sub * rev(s) - 1, 0), 0))]
        for b in range(1, n_sub):
            specs.append(pl.BlockSpec((LANE, D_MODEL), functools.partial(lambda s, b: (n_sub * rev(s) - 1 + b, 0), b=b)))
        return specs

    row_blk = lambda cols: pl.BlockSpec((t, cols), lambda s: (rev(s), 0))
    rest_blk = lambda k: pl.BlockSpec((t, 512), functools.partial(lambda s, k: (rev(s), k), k=k))
    halo_prev = lambda k: pl.BlockSpec(
        (SUBLANE, 512), functools.partial(lambda s, k: (jnp.maximum(rev(s) * hb - 1, 0), k), k=k))
    halo_next = pl.BlockSpec((SUBLANE, 512), lambda s: (jnp.minimum((rev(s) + 1) * hb, lp // SUBLANE - 1), 0))
    const = lambda shape: pl.BlockSpec(shape, lambda s: (0, 0))
    return pl.pallas_call(
        body, name="bwd_in", grid=(nt,),
        in_specs=x_specs() + [const((LANE, D_MODEL)), const((1, D_MODEL)),
                              pl.BlockSpec((D_IN_PAD, D_MODEL), lambda s: (0, 0), pipeline_mode=pl.Buffered(1)),
                              const((1, LANE)), row_blk(LANE),
                              pl.BlockSpec((HEADS, t), lambda s: (0, rev(s))),
                              row_blk(512), row_blk(512), row_blk(512), row_blk(512), row_blk(512), row_blk(512),
                              row_blk(512), halo_next, rest_blk(2), rest_blk(3), halo_prev(2), halo_prev(3),
                              row_blk(D_MODEL), const((SUBLANE, D_CONV)), const((t, t))],
        out_specs=(row_blk(D_IN_PAD), ANY, const((FRONT, D_MODEL)), const((1, D_MODEL)), const((1, LANE)),
                   const((SUBLANE, D_CONV))),
        out_shape=(jax.ShapeDtypeStruct((lp, D_IN_PAD), MXU_DTYPE), jax.ShapeDtypeStruct((lp - FRONT, D_MODEL), F32),
                   jax.ShapeDtypeStruct((FRONT, D_MODEL), F32),
                   jax.ShapeDtypeStruct((1, D_MODEL), F32), jax.ShapeDtypeStruct((1, LANE), F32),
                   jax.ShapeDtypeStruct((SUBLANE, D_CONV), F32)),
        scratch_shapes=[pltpu.VMEM((1, LANE), F32), pltpu.VMEM((2, t, D_MODEL), F32), pltpu.SemaphoreType.DMA((2,))],
        compiler_params=_params(("arbitrary",)),
    )(*([x2] * n_sub), meta_blk, norm_g, w_pad, bf_pad, fl, dc, dq, dk, dv, dza, dgb, dzc, dconv, dconv,
      rest, rest, rest, rest, d_out, conv_w8, _triangle(t, lower=False))


def _grad_w_in(u, dproj):
    lp = u.shape[0]
    tn = GW_COL_TILE
    tk = tn if lp % tn == 0 else ROW_TILE

    def body(d_ref, u_ref, o_ref, wire_ref):
        k = pl.program_id(1)

        @pl.when(k == 0)
        def _():
            o_ref[...] = jnp.zeros_like(o_ref)

        o_ref[...] = o_ref[...] + lax.dot_general(d_ref[...], u_ref[...], (((0,), (0,)), ((), ())),
                                                  preferred_element_type=F32)

        @pl.when(k == pl.num_programs(1) - 1)
        def _():
            wire_ref[...] = o_ref[...].astype(wire_ref.dtype)

    out_spec = pl.BlockSpec((tn, D_MODEL), lambda n, k: (n, 0))
    return pl.pallas_call(
        body, name="grad_w_in", grid=(D_IN_PAD // tn, lp // tk),
        in_specs=[pl.BlockSpec((tk, tn), lambda n, k: (k, n)), pl.BlockSpec((tk, D_MODEL), lambda n, k: (k, 0))],
        out_specs=(out_spec, out_spec),
        out_shape=(jax.ShapeDtypeStruct((D_IN_PAD, D_MODEL), F32), jax.ShapeDtypeStruct((D_IN_PAD, D_MODEL), WIRE_DTYPE)),
        compiler_params=_params(("parallel", "arbitrary")),
    )(dproj, u)


def _by_chip(own, others, me):
    by_mask = jnp.stack([own, others[1], others[0], others[2]])
    return [lax.dynamic_index_in_dim(by_mask, jnp.bitwise_xor(me, s), 0, keepdims=False) for s in range(N_CHIPS)]


def _both_halves(mine, other, c):
    return jnp.where(c == 0, jnp.concatenate([mine, other], axis=0), jnp.concatenate([other, mine], axis=0))


def _local_step(x2, tgt2, meta_full, norm_g, w_pad, b_f, conv_w_full, attn_g, conv_g, w_out_full, final_g):
    lp = x2.shape[0] + FRONT
    nt = lp // ROW_TILE
    meta_blk = jnp.concatenate([jnp.zeros((PAD_ROWS, D_MODEL), F32), meta_full], axis=0)
    bf_pad = jnp.pad(b_f, ((0, 0), (0, LANE - HEADS)))
    conv_w8 = jnp.pad(conv_w_full, ((0, SUBLANE - conv_w_full.shape[0]), (0, 0)))
    q, k, v, rest, fl, ct, u = _in_proj(x2, meta_blk, norm_g, w_pad, bf_pad)
    ct4 = ct.reshape(SUBLANE, nt, 1, ROW_TILE)
    o, l_sum, m_max = _attn_fwd(q, k, v, ct4)
    (d_out, d_o, delta, dza, dgb, dzc, dconv, loss, g_final, g_attn, g_convg, gw_out) = _post(
        o, l_sum, rest, x2, meta_blk, tgt2, w_out_full, attn_g, conv_g, final_g, conv_w8)
    dq, dk, dv, dc = _attn_bwd(q, k, v, d_o, m_max, delta, ct4)
    dproj, grad_x, d_front, g_norm, g_bf, g_cw = _bwd_in(x2, meta_blk, norm_g, w_pad, bf_pad, fl, dc.reshape(HEADS, lp), dq, dk, dv,
                                             dza, dgb, dzc, dconv, rest, d_out, conv_w8)
    gw_in, gw_in_wire = _grad_w_in(u, dproj)
    return dict(loss=loss, grad_x=grad_x, d_front=d_front, g_norm=g_norm, g_final=g_final, g_attn=g_attn, g_convg=g_convg, g_bf=g_bf,
                g_cw=g_cw, gw_out=gw_out, gw_in=gw_in, gw_in_wire=gw_in_wire)


def kernel(x, meta, norm_g, w_in, b_f, conv_w, attn_norm_g, conv_norm_g, w_out, final_norm_g, loss_target, m_meta, m_norm_g, m_w_in, m_b_f, m_conv_w, m_attn_norm_g, m_conv_norm_g, m_w_out, m_final_norm_g, v_meta, v_norm_g, v_w_in, v_b_f, v_conv_w, v_attn_norm_g, v_conv_norm_g, v_w_out, v_final_norm_g):
    cx_, cy_, cc_ = _position()
    chip = 2 * cx_ + cy_
    shard = w_in.shape[2]
    out_half = w_out.shape[1] // 2
    pick = lambda vals: jnp.where(chip == 0, vals[0], jnp.where(chip == 1, vals[1], jnp.where(chip == 2, vals[2], vals[3])))
    a_off, b_off = pick(A_OFF), pick(B_OFF)
    wt = jnp.transpose(w_in[0]).astype(MXU_DTYPE)
    wi = lax.dynamic_update_slice_in_dim(
        lax.dynamic_update_slice_in_dim(jnp.zeros((WIN_ROWS, D_MODEL), MXU_DTYPE), wt[:PIECE_A], a_off, 0),
        wt[PIECE_A:], b_off, 0)
    wo = w_out[0].astype(MXU_DTYPE)
    small = jnp.concatenate([meta, jnp.pad(conv_w[0], ((0, 8 - conv_w.shape[1]), (0, meta.shape[1] - conv_w.shape[2])))],
                            axis=0)
    gwi, gwo, gsm = _gather_weights(wi.reshape(2, WIN_HALF, D_MODEL), wo.reshape(2, out_half, D_MODEL), small)
    starts = jnp.stack([_window_start(jnp.bitwise_xor(chip, mask)) for mask in (0, 2, 1, 3)]).astype(jnp.int32)
    w_pad = _assemble_w(wi, gwi.reshape(3, WIN_ROWS, D_MODEL), starts)
    w_out_full = jnp.concatenate(_by_chip(wo, gwo.reshape(3, 2 * out_half, D_MODEL), chip), axis=0)
    small_full = jnp.concatenate(_by_chip(small, gsm, chip), axis=1)
    meta_full = small_full[:N_META]
    conv_w_full = jnp.concatenate([small_full[N_META:N_META + 3, 256 * s:256 * s + LANE] for s in range(N_CHIPS)], axis=1)
    final_g2 = final_norm_g.reshape(1, D_MODEL)
    r = _local_step(x[0], loss_target[0], meta_full, norm_g, w_pad, b_f, conv_w_full, attn_norm_g, conv_norm_g,
                    w_out_full, final_g2)
    grad_x = r["grad_x"][None]
    gb = r["gw_out"].reshape(N_CHIPS, 2, out_half, D_MODEL)
    ra, rb = _pair_exchange(r["gw_in_wire"], gb)
    c_idx = jnp.reshape(cc_, (1,)).astype(jnp.int32)
    chip_idx = jnp.reshape(chip, (1,)).astype(jnp.int32)
    pa, pa_wire = _pair_sum_windows(r["gw_in"], ra, c_idx)
    pb, pb_wire = _pair_sum(gb, rb, c_idx)
    xa, xb = _chip_exchange(pa_wire, pb_wire)
    ha = _chip_sum(pa, xa, chip_idx)
    hb = _chip_sum(pb, xb, chip_idx)
    oa, ob = _pair_share(ha, hb)
    g_window = _both_halves(ha, oa, cc_)
    g_w_in_t = jnp.concatenate([lax.dynamic_slice_in_dim(g_window, a_off, PIECE_A, 0),
                                lax.dynamic_slice_in_dim(g_window, b_off, shard - PIECE_A, 0)], axis=0)
    g_w_out = _both_halves(hb, ob, cc_)
    as_rows = lambda a: jnp.transpose(a, (2, 0, 1))
    g_w_in, d_w_in, nm_w_in, nv_w_in = (jnp.transpose(a, (1, 2, 0)) for a in _adamw_rows(
        as_rows(w_in), g_w_in_t, as_rows(m_w_in), as_rows(v_w_in)))
    d_w_out, nm_w_out, nv_w_out = (a[None] for a in _adamw_big(w_out[0], g_w_out, m_w_out[0], v_w_out[0], LANE))
    wide = lambda a: jnp.pad(a, ((0, 0), (0, D_MODEL - a.shape[1])))
    pack = jnp.concatenate([
        r["g_norm"], r["g_final"], jnp.concatenate([r["g_attn"], r["g_convg"]], axis=1), wide(r["g_bf"]),
        wide(r["loss"]), jnp.zeros((3, D_MODEL), F32), r["d_front"][PAD_ROWS:], wide(r["g_cw"])], axis=0)
    params = (norm_g, final_g2, attn_norm_g, conv_norm_g, b_f, meta, conv_w[0])
    ms = (m_norm_g, m_final_norm_g.reshape(1, D_MODEL), m_attn_norm_g, m_conv_norm_g, m_b_f, m_meta, m_conv_w[0])
    vs = (v_norm_g, v_final_norm_g.reshape(1, D_MODEL), v_attn_norm_g, v_conv_norm_g, v_b_f, v_meta, v_conv_w[0])
    loss, g_s, d_s, m_s, v_s = _small_update(pack, _gather_small(pack), params, ms, vs)

    def ordered(small_list, big_in, big_out):
        s_norm, s_final, s_attn, s_convg, s_bf, s_meta, s_cw = small_list
        return (s_meta, s_norm, big_in, s_bf, s_cw[None], s_attn, s_convg, big_out, s_final.reshape(D_MODEL))

    return (loss.reshape(()), grad_x,
            *ordered(g_s, g_w_in, g_w_out[None]), *ordered(d_s, d_w_in, d_w_out),
            *ordered(m_s, nm_w_in, nm_w_out), *ordered(v_s, nv_w_in, nv_w_out))
```

```python
import functools

import jax
import jax.numpy as jnp
from jax import lax
from jax.experimental import pallas as pl
from jax.experimental.pallas import tpu as pltpu

F32 = jnp.float32
MXU_DTYPE = jnp.bfloat16
WIRE_DTYPE = jnp.bfloat16

D_MODEL = 1024
N_META = 16
HEADS = 8
HEAD_DIM = 64
D_ATTN = HEADS * HEAD_DIM
D_CONV = 512
EPS = 1e-6
LANE = 128
SUBLANE = 8
ROW_TILE = 384
ATTN_UNROLL = 3
STAT_TERMS = 1
FRONT = LANE
PAD_ROWS = FRONT - N_META
NEG = -1e30
LOG2E = 1.4426950408889634
N_CHIPS = 4
N_DEV = 8
VMEM_LIMIT_BYTES = 60 * 1024 * 1024

SEG_Q, SEG_K, SEG_V, SEG_F, SEG_ZA, SEG_GB, SEG_GC, SEG_XC, SEG_ZC = (
    0, 512, 1024, 1536, 1664, 2176, 2688, 3200, 3712)
D_IN = 4104
D_IN_PAD = 4224
F_END = 1544
GW_COL_TILE = 1408
WIN_ROWS = 1152
WIN_HALF = WIN_ROWS // 2
WIN_START = (0, 1024, 2160, 3072)
PIECE_A = 518
A_OFF = (0, 2, 12, 126)
B_OFF = (518, 640, 530, 644)
ADAM_LR = 0.001
ADAM_B1 = 0.9
ADAM_B2 = 0.999
ADAM_EPS = 1e-08
ADAM_WD = 0.01
ADAM_STEP = 10

MESH = pl.DeviceIdType.MESH
ANY = pl.BlockSpec(memory_space=pl.ANY)

PACK_ROWS = 32
SLOT_NORM = (0, 1, 0, 1024)
SLOT_FINAL = (1, 2, 0, 1024)
SLOT_ATTN = (2, 3, 0, 512)
SLOT_CONVG = (2, 3, 512, 1024)
SLOT_BF = (3, 4, 0, 8)
SLOT_META = (8, 24, 0, 256)
SLOT_CONVW = (24, 27, 0, 128)
LOSS_ROW = 4


def _params(sem=None):
    return pltpu.CompilerParams(dimension_semantics=sem, vmem_limit_bytes=VMEM_LIMIT_BYTES)


def _sigmoid(z):
    return 1.0 / (1.0 + jnp.exp(-z))


def _dot(a, b):
    return jnp.dot(a, b, preferred_element_type=F32)


def _dot_nt(a, b):
    return lax.dot_general(a, b, (((1,), (1,)), ((), ())), preferred_element_type=F32)


def _dot_exact(ones, x):
    ones = ones.astype(MXU_DTYPE)
    total = None
    for _ in range(3):
        term = x.astype(MXU_DTYPE)
        x = x - term.astype(F32)
        total = _dot(ones, term) if total is None else total + _dot(ones, term)
    return total


def _group_matrix():
    r = lax.broadcasted_iota(jnp.int32, (D_ATTN, D_ATTN), 0) >> 6
    c = lax.broadcasted_iota(jnp.int32, (D_ATTN, D_ATTN), 1) >> 6
    return jnp.where(r == c, 1.0, 0.0).astype(MXU_DTYPE)


def _triangle(n, lower):
    r = lax.broadcasted_iota(jnp.int32, (n, n), 0)
    c = lax.broadcasted_iota(jnp.int32, (n, n), 1)
    return jnp.where((r >= c) if lower else (c >= r), 1.0, 0.0).astype(MXU_DTYPE)


def _group_sum(x, gmat, terms=2):
    hi = x.astype(MXU_DTYPE)
    if terms == 1:
        return _dot(hi, gmat)
    lo = (x - hi.astype(F32)).astype(MXU_DTYPE)
    return _dot(hi, gmat) + _dot(lo, gmat)


def _x_block_specs(n_sub, rows):
    specs = [pl.BlockSpec((rows, D_MODEL), lambda i: (jnp.maximum(n_sub * i - 1, 0), 0))]
    for b in range(1, n_sub):
        specs.append(pl.BlockSpec((rows, D_MODEL), functools.partial(lambda i, b: (n_sub * i - 1 + b, 0), b=b)))
    return specs


def _position():
    return lax.axis_index("x"), lax.axis_index("y"), lax.axis_index("c")


def _gather_weights(wi, wo, small):
    def body(wi_ref, wo_ref, sm_ref, gwi_ref, gwo_ref, gsm_ref, send_sems, recv_sems):
        x, y, c = _position()
        sibling = (x, y, 1 - c)
        chips = [(1 - x, y), (x, 1 - y), (1 - x, 1 - y)]

        def remote(k, src, dst, to):
            return pltpu.make_async_remote_copy(src_ref=src, dst_ref=dst, send_sem=send_sems.at[k],
                                                recv_sem=recv_sems.at[k], device_id=to, device_id_type=MESH)

        first, passed, landed = [], [], []
        for a, (src_ref, g_ref) in enumerate(((wi_ref, gwi_ref), (wo_ref, gwo_ref))):
            for j, (cx, cy) in enumerate(chips):
                slot = g_ref.at[j, c]
                first.append(remote(6 * a + j, src_ref.at[c], slot, (cx, cy, c)))
                landed.append(remote(6 * a + j, slot, slot, sibling))
                passed.append(remote(6 * a + 3 + j, slot, slot, sibling))
        for j, (cx, cy) in enumerate(chips):
            first.append(remote(12 + j, sm_ref, gsm_ref.at[j], (cx, cy, c)))
        for cp in first:
            cp.start()
        for arrived, onward in zip(landed, passed):
            arrived.wait_recv()
            onward.start()
        for a, g_ref in enumerate((gwi_ref, gwo_ref)):
            for j in range(3):
                remote(6 * a + 3 + j, g_ref.at[j, 1 - c], g_ref.at[j, 1 - c], sibling).wait_recv()
        for j in range(3):
            remote(12 + j, sm_ref, gsm_ref.at[j], sibling).wait_recv()
        for cp in first + passed:
            cp.wait_send()

    return pl.pallas_call(
        body, name="gather_weights",
        out_shape=(jax.ShapeDtypeStruct((3,) + wi.shape, wi.dtype), jax.ShapeDtypeStruct((3,) + wo.shape, wo.dtype),
                   jax.ShapeDtypeStruct((3,) + small.shape, small.dtype)),
        in_specs=[ANY, ANY, ANY], out_specs=(ANY, ANY, ANY),
        scratch_shapes=[pltpu.SemaphoreType.DMA((15,)), pltpu.SemaphoreType.DMA((15,))],
    )(wi, wo, small)


def _pair_exchange(gw, gb):
    def body(gw_ref, gb_ref, ra_ref, rb_ref, send_sems, recv_sems):
        x, y, c = _position()
        sibling = (x, y, 1 - c)
        copies = [pltpu.make_async_remote_copy(
            src_ref=gb_ref.at[:, 1 - c], dst_ref=rb_ref, send_sem=send_sems.at[N_CHIPS], recv_sem=recv_sems.at[N_CHIPS],
            device_id=sibling, device_id_type=MESH)]
        for s, start in enumerate(WIN_START):
            rows = pl.ds(pl.multiple_of(start + WIN_HALF * (1 - c), 2 * SUBLANE), WIN_HALF)
            copies.append(pltpu.make_async_remote_copy(
                src_ref=gw_ref.at[rows], dst_ref=ra_ref.at[s], send_sem=send_sems.at[s], recv_sem=recv_sems.at[s],
                device_id=sibling, device_id_type=MESH))
        for cp in copies:
            cp.start()
        for cp in copies:
            cp.wait()

    return pl.pallas_call(
        body, name="grad_pair_exchange",
        out_shape=(jax.ShapeDtypeStruct((N_CHIPS, WIN_HALF, D_MODEL), gw.dtype),
                   jax.ShapeDtypeStruct((N_CHIPS,) + gb.shape[2:], gb.dtype)),
        in_specs=[ANY, ANY], out_specs=(ANY, ANY),
        scratch_shapes=[pltpu.SemaphoreType.DMA((N_CHIPS + 1,)), pltpu.SemaphoreType.DMA((N_CHIPS + 1,))],
    )(gw, gb)


def _chip_exchange(pa, pb):
    def body(pa_ref, pb_ref, ra_ref, rb_ref, send_sems, recv_sems):
        x, y, c = _position()
        chips = [(1 - x, y), (x, 1 - y), (1 - x, 1 - y)]
        copies = []
        for a, (src, dst) in enumerate(((pa_ref, ra_ref), (pb_ref, rb_ref))):
            for j, (cx, cy) in enumerate(chips):
                copies.append(pltpu.make_async_remote_copy(
                    src_ref=src.at[2 * cx + cy], dst_ref=dst.at[j], send_sem=send_sems.at[3 * a + j],
                    recv_sem=recv_sems.at[3 * a + j], device_id=(cx, cy, c), device_id_type=MESH))
        for cp in copies:
            cp.start()
        for cp in copies:
            cp.wait()

    return pl.pallas_call(
        body, name="grad_chip_exchange",
        out_shape=(jax.ShapeDtypeStruct((3,) + pa.shape[1:], pa.dtype),
                   jax.ShapeDtypeStruct((3,) + pb.shape[1:], pb.dtype)),
        in_specs=[ANY, ANY], out_specs=(ANY, ANY),
        scratch_shapes=[pltpu.SemaphoreType.DMA((6,)), pltpu.SemaphoreType.DMA((6,))],
    )(pa, pb)


def _pair_share(ha, hb):
    def body(ha_ref, hb_ref, oa_ref, ob_ref, send_sems, recv_sems):
        x, y, c = _position()
        copies = [pltpu.make_async_remote_copy(
            src_ref=src, dst_ref=dst, send_sem=send_sems.at[k], recv_sem=recv_sems.at[k],
            device_id=(x, y, 1 - c), device_id_type=MESH)
            for k, (src, dst) in enumerate(((ha_ref, oa_ref), (hb_ref, ob_ref)))]
        for cp in copies:
            cp.start()
        for cp in copies:
            cp.wait()

    return pl.pallas_call(
        body, name="grad_pair_share",
        out_shape=(jax.ShapeDtypeStruct(ha.shape, ha.dtype), jax.ShapeDtypeStruct(hb.shape, hb.dtype)),
        in_specs=[ANY, ANY], out_specs=(ANY, ANY),
        scratch_shapes=[pltpu.SemaphoreType.DMA((2,)), pltpu.SemaphoreType.DMA((2,))],
    )(ha, hb)


def _gather_small(pack):
    def body(p_ref, o_ref, send_sems, recv_sems):
        x, y, c = _position()
        copies = []
        for mask in range(1, N_DEV):
            peer = (1 - x if mask & 4 else x, 1 - y if mask & 2 else y, 1 - c if mask & 1 else c)
            copies.append(pltpu.make_async_remote_copy(
                src_ref=p_ref, dst_ref=o_ref.at[mask - 1], send_sem=send_sems.at[mask - 1],
                recv_sem=recv_sems.at[mask - 1], device_id=peer, device_id_type=MESH))
        for cp in copies:
            cp.start()
        for cp in copies:
            cp.wait()

    return pl.pallas_call(
        body, name="gather_small",
        out_shape=jax.ShapeDtypeStruct((N_DEV - 1,) + pack.shape, pack.dtype),
        in_specs=[ANY], out_specs=ANY,
        scratch_shapes=[pltpu.SemaphoreType.DMA((N_DEV - 1,)), pltpu.SemaphoreType.DMA((N_DEV - 1,))],
    )(pack)


def _pair_sum(mine, recv, c_idx):
    rows, cols = mine.shape[2:]

    def body(c_ref, a_ref, b_ref, o_ref, send_ref):
        total = a_ref[...] + b_ref[...]
        o_ref[...] = total
        send_ref[...] = total.astype(send_ref.dtype)

    out_spec = pl.BlockSpec((None, rows, cols), lambda s, c_ref: (s, 0, 0))
    return pl.pallas_call(
        body, name="grad_pair_sum",
        grid_spec=pltpu.PrefetchScalarGridSpec(
            num_scalar_prefetch=1, grid=(N_CHIPS,),
            in_specs=[pl.BlockSpec((None, None, rows, cols), lambda s, c_ref: (s, c_ref[0], 0, 0)),
                      pl.BlockSpec((None, rows, cols), lambda s, c_ref: (s, 0, 0))],
            out_specs=(out_spec, out_spec)),
        out_shape=(jax.ShapeDtypeStruct(recv.shape, recv.dtype), jax.ShapeDtypeStruct(recv.shape, WIRE_DTYPE)),
        compiler_params=_params(("parallel",)),
    )(c_idx, mine, recv)


def _window_start(s):
    return jnp.where(s == 0, WIN_START[0], jnp.where(s == 1, WIN_START[1], jnp.where(s == 2, WIN_START[2], WIN_START[3])))


def _pair_sum_windows(gw, recv, c_idx):
    tr = WIN_HALF // 3

    def body(c_ref, a_ref, b_ref, o_ref, send_ref):
        total = a_ref[...] + b_ref[...].astype(F32)
        o_ref[...] = total
        send_ref[...] = total.astype(send_ref.dtype)

    out_spec = pl.BlockSpec((None, tr, D_MODEL), lambda s, i, c_ref: (s, i, 0))
    return pl.pallas_call(
        body, name="grad_pair_sum_windows",
        grid_spec=pltpu.PrefetchScalarGridSpec(
            num_scalar_prefetch=1, grid=(N_CHIPS, WIN_HALF // tr),
            in_specs=[pl.BlockSpec((pl.Element(tr), pl.Element(D_MODEL)),
                                   lambda s, i, c_ref: (pl.multiple_of(
                                       _window_start(s) + WIN_HALF * c_ref[0] + tr * i, SUBLANE), 0)),
                      pl.BlockSpec((None, tr, D_MODEL), lambda s, i, c_ref: (s, i, 0))],
            out_specs=(out_spec, out_spec)),
        out_shape=(jax.ShapeDtypeStruct(recv.shape, F32), jax.ShapeDtypeStruct(recv.shape, WIRE_DTYPE)),
        compiler_params=_params(("parallel", "parallel")),
    )(c_idx, gw, recv)


def _assemble_w(own, others, starts):
    def body(starts_ref, own_ref, oth_ref, o_ref):
        o_ref[...] = jnp.zeros_like(o_ref)
        for k in range(N_CHIPS):
            rows = pl.ds(pl.multiple_of(starts_ref[k], 2 * SUBLANE), WIN_ROWS)
            o_ref[rows, :] = o_ref[rows, :] + (own_ref[...] if k == 0 else oth_ref[k - 1])

    return pl.pallas_call(
        body, name="assemble_w",
        in_specs=[pl.BlockSpec(memory_space=pltpu.SMEM), pl.BlockSpec(memory_space=pltpu.VMEM),
                  pl.BlockSpec(memory_space=pltpu.VMEM)],
        out_specs=pl.BlockSpec(memory_space=pltpu.VMEM),
        out_shape=jax.ShapeDtypeStruct((D_IN_PAD, D_MODEL), own.dtype),
        compiler_params=_params(),
    )(starts, own, others)


def _chip_sum(psum, recv3, chip_idx):
    rows, cols = psum.shape[1:]
    tr = rows // 2

    def body(s_ref, p_ref, r0, r1, r2, o_ref):
        o_ref[...] = ((p_ref[...] + r0[...].astype(F32)) + r1[...].astype(F32)) + r2[...].astype(F32)

    return pl.pallas_call(
        body, name="grad_chip_sum",
        grid_spec=pltpu.PrefetchScalarGridSpec(
            num_scalar_prefetch=1, grid=(2,),
            in_specs=[pl.BlockSpec((None, tr, cols), lambda i, s_ref: (s_ref[0], i, 0))] +
                     [pl.BlockSpec((None, tr, cols), functools.partial(lambda i, s_ref, j: (j, i, 0), j=j))
                      for j in range(3)],
            out_specs=pl.BlockSpec((tr, cols), lambda i, s_ref: (i, 0))),
        out_shape=jax.ShapeDtypeStruct((rows, cols), psum.dtype),
        compiler_params=_params(("parallel",)),
    )(chip_idx, psum, recv3, recv3, recv3)


def _adamw_math(w, g, m, v):
    m = ADAM_B1 * m + (1.0 - ADAM_B1) * g
    v = ADAM_B2 * v + (1.0 - ADAM_B2) * (g * g)
    m_hat = m * (1.0 / (1.0 - ADAM_B1 ** ADAM_STEP))
    v_hat = v * (1.0 / (1.0 - ADAM_B2 ** ADAM_STEP))
    delta = -ADAM_LR * (m_hat / (jnp.sqrt(v_hat) + ADAM_EPS) + ADAM_WD * w)
    return delta, m, v


def _adamw_big(w, g, m, v, tr):
    rows, cols = w.shape
    assert rows % tr == 0 and g.shape[0] >= rows

    def body(w_ref, g_ref, m_ref, v_ref, d_out, m_out, v_out):
        d, m2, v2 = _adamw_math(w_ref[...], g_ref[...], m_ref[...], v_ref[...])
        d_out[...] = d
        m_out[...] = m2
        v_out[...] = v2

    spec = pl.BlockSpec((tr, cols), lambda i: (i, 0))
    sds = jax.ShapeDtypeStruct((rows, cols), F32)
    return pl.pallas_call(
        body, name="adamw_big", grid=(rows // tr,), in_specs=[spec] * 4, out_specs=(spec,) * 3,
        out_shape=(sds,) * 3, compiler_params=_params(("parallel",)),
    )(w, g, m, v)


def _adamw_rows(w3, g, m3, v3):
    rows, _, cols = w3.shape
    tc = 2 * LANE

    def body(w_ref, g_ref, m_ref, v_ref, g_out, d_out, m_out, v_out):
        g = g_ref[...]
        d, m2, v2 = _adamw_math(w_ref[:, 0, :], g, m_ref[:, 0, :], v_ref[:, 0, :])
        g_out[:, 0, :] = g
        d_out[:, 0, :] = d
        m_out[:, 0, :] = m2
        v_out[:, 0, :] = v2

    spec3 = pl.BlockSpec((rows, 1, tc), lambda i: (0, 0, i))
    sds = jax.ShapeDtypeStruct((rows, 1, cols), F32)
    return pl.pallas_call(
        body, name="adamw_rows", grid=(cols // tc,),
        in_specs=[spec3, pl.BlockSpec((rows, tc), lambda i: (0, i)), spec3, spec3], out_specs=(spec3,) * 4,
        out_shape=(sds,) * 4, compiler_params=_params(("parallel",)),
    )(w3, g, m3, v3)


def _small_update(own, others, params, ms, vs):
    slots = (SLOT_NORM, SLOT_FINAL, SLOT_ATTN, SLOT_CONVG, SLOT_BF, SLOT_META, SLOT_CONVW)
    n = len(slots)

    def body(*refs):
        own_ref, gp_ref = refs[:2]
        w_refs, m_refs, v_refs = refs[2:2 + n], refs[2 + n:2 + 2 * n], refs[2 + 2 * n:2 + 3 * n]
        outs = refs[2 + 3 * n:3 + 7 * n]
        loss_ref = outs[0]
        g_outs, d_outs, m_outs, v_outs = (outs[1 + k * n:1 + (k + 1) * n] for k in range(4))
        g_scr, w_scr, m_scr, v_scr = refs[3 + 7 * n:]
        x, y, c = _position()
        shard = 2 * x + y
        me = 4 * x + 2 * y + c
        tot = None
        for d in range(N_DEV):
            rel = jnp.bitwise_xor(me, d)
            term = jnp.where(rel == 0, own_ref[...], gp_ref[jnp.maximum(rel, 1) - 1])
            tot = term if tot is None else tot + term
        r0, r1, _, _ = SLOT_META
        meta_sel = tot[r0:r1, 0:256]
        cw_sel = tot[24:32, 0:128]
        for k in range(1, N_CHIPS):
            meta_sel = jnp.where(shard == k, tot[r0:r1, 256 * k:256 * (k + 1)], meta_sel)
            cw_sel = jnp.where(shard == k, tot[24:32, 128 * k:128 * (k + 1)], cw_sel)
        zeros = jnp.zeros((PACK_ROWS, D_MODEL), F32)
        for scr in (g_scr, w_scr, m_scr, v_scr):
            scr[...] = zeros
        g_scr[0:8, :] = tot[0:8, :]
        g_scr[r0:r1, 0:256] = meta_sel
        g_scr[24:32, 0:128] = cw_sel
        for (a, b, c0, c1), w_ref, m_ref, v_ref in zip(slots, w_refs, m_refs, v_refs):
            w_scr[a:b, c0:c1] = w_ref[...]
            m_scr[a:b, c0:c1] = m_ref[...]
            v_scr[a:b, c0:c1] = v_ref[...]
        loss_ref[...] = g_scr[LOSS_ROW:LOSS_ROW + 1, 0:1]
        d, m2, v2 = _adamw_math(w_scr[...], g_scr[...], m_scr[...], v_scr[...])
        w_scr[...] = d
        m_scr[...] = m2
        v_scr[...] = v2
        for (a, b, c0, c1), g_o, d_o, m_o, v_o in zip(slots, g_outs, d_outs, m_outs, v_outs):
            g_o[...] = g_scr[a:b, c0:c1]
            d_o[...] = w_scr[a:b, c0:c1]
            m_o[...] = m_scr[a:b, c0:c1]
            v_o[...] = v_scr[a:b, c0:c1]

    shapes = [jax.ShapeDtypeStruct(p.shape, F32) for p in params]
    out = pl.pallas_call(
        body, name="small_update",
        out_shape=[jax.ShapeDtypeStruct((1, 1), F32)] + shapes * 4,
        scratch_shapes=[pltpu.VMEM((PACK_ROWS, D_MODEL), F32)] * 4,
        compiler_params=_params(),
    )(own, others, *params, *ms, *vs)
    return out[0], out[1:1 + n], out[1 + n:1 + 2 * n], out[1 + 2 * n:1 + 3 * n], out[1 + 3 * n:1 + 4 * n]


def _in_proj(x2, meta_blk, norm_g, w_pad, bf_pad):
    seq = x2.shape[0]
    lp = seq + FRONT
    t = ROW_TILE
    nt = lp // t
    n_sub = t // LANE

    def body(*refs):
        x_refs = refs[:n_sub]
        mb, g_ref, w_ref, bf_ref, tri_ref = refs[n_sub:n_sub + 5]
        q_ref, k_ref, v_ref, rest_ref, fl_ref, ct_ref, u_ref, carry = refs[n_sub + 5:]
        i = pl.program_id(0)

        @pl.when(i == 0)
        def _():
            carry[...] = jnp.zeros_like(carry)

        first = jnp.where(i == 0, mb[...], x_refs[0][...])
        h = jnp.concatenate([first] + [r[...] for r in x_refs[1:]], axis=0)
        ms = jnp.mean(h * h, axis=-1, keepdims=True)
        u = ((h * lax.rsqrt(ms + EPS)) * g_ref[...]).astype(MXU_DTYPE)
        u_ref[...] = u

        def seg(a, width):
            return _dot_nt(u, w_ref[a:a + width, :])

        q_ref[...] = (seg(SEG_Q, D_ATTN) * (HEAD_DIM ** -0.5)).astype(MXU_DTYPE)
        k_ref[...] = seg(SEG_K, D_ATTN).astype(MXU_DTYPE)
        v_ref[...] = seg(SEG_V, D_ATTN).astype(MXU_DTYPE)
        for s in range(5):
            rest_ref[:, 512 * s:512 * (s + 1)] = seg(SEG_ZA + 512 * s, 512)
        fl = seg(SEG_F, LANE)
        fl_ref[...] = fl
        z = fl + bf_ref[...]
        logf = jnp.minimum(z, 0.0) - jnp.log(1.0 + jnp.exp(-jnp.abs(z)))
        row = i * t + lax.broadcasted_iota(jnp.int32, (t, LANE), 0)
        logf = jnp.where(row >= PAD_ROWS, logf, 0.0)
        cs = _dot_exact(tri_ref[...], logf) + carry[...]
        carry[...] = carry[...] + jnp.sum(logf, axis=0, keepdims=True)
        col = i * t + lax.broadcasted_iota(jnp.int32, (SUBLANE, t), 1)
        ct_ref[...] = jnp.where(col >= PAD_ROWS, cs.T[0:SUBLANE, :], -NEG)

    row_blk = lambda cols: pl.BlockSpec((t, cols), lambda i: (i, 0))
    const = lambda shape: pl.BlockSpec(shape, lambda i: (0, 0))
    return pl.pallas_call(
        body, name="in_proj", grid=(nt,),
        in_specs=_x_block_specs(n_sub, LANE) + [const((LANE, D_MODEL)), const((1, D_MODEL)),
                                                pl.BlockSpec((D_IN_PAD, D_MODEL), lambda i: (0, 0),
                                                             pipeline_mode=pl.Buffered(1)),
                                                const((1, LANE)), const((t, t))],
        out_specs=(row_blk(D_ATTN), row_blk(D_ATTN), row_blk(D_ATTN), row_blk(5 * 512), row_blk(LANE),
                   pl.BlockSpec((SUBLANE, t), lambda i: (0, i)), row_blk(D_MODEL)),
        out_shape=(jax.ShapeDtypeStruct((lp, D_ATTN), MXU_DTYPE), jax.ShapeDtypeStruct((lp, D_ATTN), MXU_DTYPE),
                   jax.ShapeDtypeStruct((lp, D_ATTN), MXU_DTYPE), jax.ShapeDtypeStruct((lp, 5 * 512), F32),
                   jax.ShapeDtypeStruct((lp, LANE), F32),
                   jax.ShapeDtypeStruct((SUBLANE, lp), F32), jax.ShapeDtypeStruct((lp, D_MODEL), MXU_DTYPE)),
        scratch_shapes=[pltpu.VMEM((1, LANE), F32)],
        compiler_params=_params(("arbitrary",)),
    )(*([x2] * n_sub), meta_blk, norm_g, w_pad, bf_pad, _triangle(t, lower=True))


def _head_masks():
    lane = lax.broadcasted_iota(jnp.int32, (1, LANE), 1)
    return lane < HEAD_DIM, lane >= HEAD_DIM


def _pair_specs(lp, nt, t):
    blk = pl.BlockSpec((lp, LANE), lambda g: (0, g))
    ct_a = pl.BlockSpec((None, nt, 1, t), lambda g: (2 * g, 0, 0, 0))
    ct_b = pl.BlockSpec((None, nt, 1, t), lambda g: (2 * g + 1, 0, 0, 0))
    return blk, ct_a, ct_b


def _sub_rows(s, col):
    return jnp.concatenate([s[:, a * LANE:(a + 1) * LANE] - col for a in range(s.shape[1] // LANE)], axis=1)


def _loop_unrolled(lo, hi, step, init, n):
    def group(jj, carry):
        for k in range(n):
            carry = step(lo + n * jj + k, carry)
        return carry

    groups = (hi - lo) // n
    carry = lax.fori_loop(0, groups, group, init)
    return lax.fori_loop(lo + n * groups, hi, step, carry)


def _lane_chunks(s):
    return [s[:, a * LANE:(a + 1) * LANE] for a in range(s.shape[1] // LANE)]


def _attn_fwd(q, k, v, ct4):
    lp = q.shape[0]
    t = ROW_TILE
    nt = lp // t

    def body(q_ref, k_ref, v_ref, cta_ref, ctb_ref, o_ref, l_ref, m_ref, s_scr):
        masks = _head_masks()
        ct_refs = (cta_ref, ctb_ref)
        below = lax.broadcasted_iota(jnp.int32, (t, t), 1) <= lax.broadcasted_iota(jnp.int32, (t, t), 0)
        lane = lax.broadcasted_iota(jnp.int32, (1, LANE), 1)
        head_of_row = lax.broadcasted_iota(jnp.int32, (2 * t, LANE), 0) >= t
        ones_cols = jnp.where(lax.broadcasted_iota(jnp.int32, (2 * t, LANE), 1) == head_of_row.astype(jnp.int32),
                              1.0, 0.0).astype(MXU_DTYPE)

        def q_block(i, _):
            r0 = pl.multiple_of(i * t, t)
            qi = q_ref[pl.ds(r0, t), :]

            def scores(j):
                kj = k_ref[pl.ds(pl.multiple_of(j * t, t), t), :]
                return _dot_nt(qi, jnp.concatenate([jnp.where(hm, kj, 0).astype(MXU_DTYPE) for hm in masks], axis=0))

            def biased(j, hh, s2, diagonal):
                s = (s2[:, hh * t:(hh + 1) * t] - ct_refs[hh][j]) * LOG2E
                return jnp.where(below, s, NEG) if diagonal else s

            def max_step(j, carry, diagonal):
                s2 = scores(j)
                out = []
                for hh, m in enumerate(carry):
                    s = biased(j, hh, s2, diagonal)
                    s_scr[j, :, hh * t:(hh + 1) * t] = s
                    for c in _lane_chunks(s):
                        m = jnp.maximum(m, c)
                    out.append(m)
                return tuple(out)

            lanes_neg = jnp.full((t, LANE), NEG, F32)
            carry = _loop_unrolled(0, i, functools.partial(max_step, diagonal=False), (lanes_neg, lanes_neg),
                                   ATTN_UNROLL)
            ms = [jnp.max(m, axis=-1, keepdims=True) for m in max_step(i, carry, True)]

            def sum_step(j, acc):
                vj = v_ref[pl.ds(pl.multiple_of(j * t, t), t), :]
                v2 = jnp.concatenate([jnp.where(hm, vj, 0).astype(MXU_DTYPE) for hm in masks], axis=0)
                parts = [jnp.exp2(s_scr[j, :, hh * t:(hh + 1) * t] - ms[hh]).astype(MXU_DTYPE) for hh in range(2)]
                return acc + _dot(jnp.concatenate(parts, axis=1), jnp.concatenate([v2, ones_cols], axis=1))

            acc = _loop_unrolled(0, i + 1, sum_step, jnp.zeros((t, 2 * LANE), F32), ATTN_UNROLL)
            sums = acc[:, LANE:]
            l_pair = jnp.where(masks[0], jnp.sum(jnp.where(lane == 0, sums, 0.0), axis=-1, keepdims=True),
                               jnp.sum(jnp.where(lane == 1, sums, 0.0), axis=-1, keepdims=True))
            o_ref[pl.ds(r0, t), :] = acc[:, :LANE] / l_pair
            l_ref[pl.ds(r0, t), :] = l_pair
            m_ref[pl.ds(r0, t), 0:LANE] = jnp.broadcast_to(ms[0], (t, LANE))
            m_ref[pl.ds(r0, t), LANE:2 * LANE] = jnp.broadcast_to(ms[1], (t, LANE))
            return 0

        lax.fori_loop(0, nt, q_block, 0)

    blk, ct_a, ct_b = _pair_specs(lp, nt, t)
    return pl.pallas_call(
        body, name="attn_fwd", grid=(HEADS // 2,),
        in_specs=[blk, blk, blk, ct_a, ct_b], out_specs=(blk, blk, pl.BlockSpec((lp, 2 * LANE), lambda g: (0, g))),
        out_shape=(jax.ShapeDtypeStruct((lp, D_ATTN), F32), jax.ShapeDtypeStruct((lp, D_ATTN), F32),
                   jax.ShapeDtypeStruct((lp, HEADS * LANE), F32)),
        scratch_shapes=[pltpu.VMEM((nt, t, 2 * t), F32)],
        compiler_params=_params(("parallel",)),
    )(q, k, v, ct4, ct4)


def _attn_bwd(q, k, v, do, m, delta, ct4):
    lp = q.shape[0]
    t = ROW_TILE
    nt = lp // t

    def body(q_ref, k_ref, v_ref, do_ref, ma_ref, mb_ref, dla_ref, dlb_ref, cta_ref, ctb_ref,
             dq_ref, dk_ref, dv_ref, dc_ref, dq_acc, dk_acc, dv_acc):
        masks = _head_masks()
        ct_refs, m_refs, dl_refs = (cta_ref, ctb_ref), (ma_ref, mb_ref), (dla_ref, dlb_ref)
        below = lax.broadcasted_iota(jnp.int32, (t, t), 1) <= lax.broadcasted_iota(jnp.int32, (t, t), 0)
        tn = (((0,), (0,)), ((), ()))
        dq_acc[...] = jnp.zeros_like(dq_acc)

        def k_block(j, _):
            c0 = pl.multiple_of(j * t, t)
            kj = k_ref[pl.ds(c0, t), :]
            vj = v_ref[pl.ds(c0, t), :]
            k2 = jnp.concatenate([jnp.where(hm, kj, 0).astype(MXU_DTYPE) for hm in masks], axis=0)
            v2 = jnp.concatenate([jnp.where(hm, vj, 0).astype(MXU_DTYPE) for hm in masks], axis=0)
            ck = [r[j] for r in ct_refs]
            dk_acc[...] = jnp.zeros_like(dk_acc)
            dv_acc[...] = jnp.zeros_like(dv_acc)

            def q_block(i, colsums, diagonal, rows=t):
                r0 = pl.multiple_of(i * t, t)
                qi = q_ref[pl.ds(r0, rows), :]
                doi = do_ref[pl.ds(r0, rows), :]
                q2 = jnp.concatenate([jnp.where(hm, qi, 0).astype(MXU_DTYPE) for hm in masks], axis=0)
                do2 = jnp.concatenate([jnp.where(hm, doi, 0).astype(MXU_DTYPE) for hm in masks], axis=0)
                s2 = _dot_nt(qi, k2)
                dp2 = _dot_nt(doi, v2)
                out, ps, dss = [], [], []
                for hh in range(2):
                    s = (s2[:, hh * t:(hh + 1) * t] - ck[hh]) * LOG2E
                    if diagonal:
                        s = jnp.where(below, s, NEG)
                    p = jnp.exp2(_sub_rows(s, m_refs[hh][pl.ds(r0, rows), :])).astype(MXU_DTYPE)
                    ds32 = p.astype(F32) * _sub_rows(dp2[:, hh * t:(hh + 1) * t], dl_refs[hh][pl.ds(r0, rows), :])
                    ps.append(p)
                    dss.append(ds32.astype(MXU_DTYPE))
                    out.append(colsums[hh] + jnp.sum(ds32, axis=0, keepdims=True))
                dv_acc[...] = dv_acc[...] + lax.dot_general(jnp.concatenate(ps, axis=0), do2, tn,
                                                            preferred_element_type=F32)
                dk_acc[...] = dk_acc[...] + lax.dot_general(jnp.concatenate(dss, axis=0), q2, tn,
                                                            preferred_element_type=F32)
                dq_acc[pl.ds(r0, rows), :] = dq_acc[pl.ds(r0, rows), :] + _dot(jnp.concatenate(dss, axis=1), k2)
                return tuple(out)

            colsums = q_block(j, (jnp.zeros((1, t), F32), jnp.zeros((1, t), F32)), True)
            pairs = (nt - 1 - j) // 2
            colsums = lax.fori_loop(0, pairs, lambda p, c: q_block(j + 1 + 2 * p, c, False, 2 * t), colsums)
            colsums = lax.fori_loop(j + 1 + 2 * pairs, nt, functools.partial(q_block, diagonal=False), colsums)
            for hh in range(2):
                dc_ref[hh, j] = -colsums[hh]
            dk_ref[pl.ds(c0, t), :] = dk_acc[...].astype(dk_ref.dtype)
            dv_ref[pl.ds(c0, t), :] = dv_acc[...].astype(dv_ref.dtype)
            return 0

        lax.fori_loop(0, nt, k_block, 0)
        dq_ref[...] = (dq_acc[...] * (HEAD_DIM ** -0.5)).astype(dq_ref.dtype)

    blk, ct_a, ct_b = _pair_specs(lp, nt, t)
    rep_a = pl.BlockSpec((lp, LANE), lambda g: (0, 2 * g))
    rep_b = pl.BlockSpec((lp, LANE), lambda g: (0, 2 * g + 1))
    return pl.pallas_call(
        body, name="attn_bwd", grid=(HEADS // 2,),
        in_specs=[blk] * 4 + [rep_a, rep_b, rep_a, rep_b, ct_a, ct_b],
        out_specs=(blk, blk, blk, pl.BlockSpec((2, nt, 1, t), lambda g: (g, 0, 0, 0))),
        out_shape=(jax.ShapeDtypeStruct((lp, D_ATTN), MXU_DTYPE),) * 3
                  + (jax.ShapeDtypeStruct((HEADS, nt, 1, t), F32),),
        scratch_shapes=[pltpu.VMEM((lp, LANE), F32), pltpu.VMEM((t, LANE), F32), pltpu.VMEM((t, LANE), F32)],
        compiler_params=_params(("parallel",)),
    )(q, k, v, do, m, m, delta, delta, ct4, ct4)


def _shift_down(prev8, cur, k):
    ext = jnp.concatenate([prev8, cur], axis=0)
    return pltpu.roll(ext, k, 0)[SUBLANE:, :]


def _shift_up(cur, next8, k):
    ext = jnp.concatenate([cur, next8], axis=0)
    n = ext.shape[0]
    return pltpu.roll(ext, n - k, 0)[:cur.shape[0], :]


def _post(o, l_sum, rest, x2, meta_blk, tgt2, w_out, attn_g, conv_g, final_g, conv_w8):
    lp = o.shape[0]
    t = ROW_TILE
    nt = lp // t
    n_sub = t // LANE
    hb = t // SUBLANE

    def body(*refs):
        o_ref, l_ref, za_ref, gb_ref, gc_ref, xc_ref, zc_ref, gch_ref, xch_ref = refs[:9]
        x_refs = refs[9:9 + n_sub]
        mb = refs[9 + n_sub]
        t_refs = refs[10 + n_sub:10 + 2 * n_sub]
        wo_ref, ag_ref, cg_ref, fg_ref, cw_ref, gm_ref, hr_ref = refs[10 + 2 * n_sub:17 + 2 * n_sub]
        (dout_ref, do_ref, dl_ref, dza_ref, dgb_ref, dzc_ref, dcv_ref,
         loss_ref, gf_ref, gag_ref, gcg_ref, gwo_ref) = refs[17 + 2 * n_sub:]
        i = pl.program_id(0)

        @pl.when(i == 0)
        def _():
            for r in (loss_ref, gf_ref, gag_ref, gcg_ref, gwo_ref):
                r[...] = jnp.zeros_like(r)

        gmat = gm_ref[...]
        inv_g = 1.0 / HEAD_DIM
        o_v = o_ref[...]
        ra = lax.rsqrt(_group_sum(o_v * o_v, gmat, STAT_TERMS) * inv_g + EPS)
        n_a = o_v * ra
        a_n = n_a * ag_ref[...]
        za = za_ref[...]
        sig_a = _sigmoid(za)
        sz_a = za * sig_a
        y_a = a_n * sz_a
        gb = gb_ref[...]
        gc = gc_ref[...]
        xc = xc_ref[...]
        cx = gc * xc
        cx_prev = jnp.where(i == 0, 0.0, gch_ref[...] * xch_ref[...])
        conv = (cw_ref[0:1, :] * _shift_down(cx_prev, cx, 2) + cw_ref[1:2, :] * _shift_down(cx_prev, cx, 1)
                + cw_ref[2:3, :] * cx)
        e = gb * conv
        re = lax.rsqrt(_group_sum(e * e, gmat, STAT_TERMS) * inv_g + EPS)
        n_e = e * re
        e_n = n_e * cg_ref[...]
        zc = zc_ref[...]
        sig_c = _sigmoid(zc)
        sz_c = zc * sig_c
        y_c = e_n * sz_c
        mix = jnp.concatenate([y_a, y_c], axis=-1)
        mix_b = mix.astype(MXU_DTYPE)
        first = jnp.where(i == 0, mb[...], x_refs[0][...])
        h = jnp.concatenate([first] + [r[...] for r in x_refs[1:]], axis=0)
        out = h + _dot(mix_b, wo_ref[...])
        r2 = lax.rsqrt(jnp.mean(out * out, axis=-1, keepdims=True) + EPS)
        n_f = out * r2
        y = n_f * fg_ref[...]
        tgt = jnp.concatenate([r[...] for r in t_refs], axis=0)
        valid = (i * t + lax.broadcasted_iota(jnp.int32, (t, 1), 0)) >= FRONT
        diff = jnp.where(valid, y - tgt, 0.0)
        loss_ref[...] = loss_ref[...] + 0.5 * jnp.sum(jnp.sum(diff * diff, axis=-1, keepdims=True) * (1.0 / D_MODEL))
        dy = diff * (1.0 / D_MODEL)
        gf_ref[...] = gf_ref[...] + jnp.sum(dy * n_f, axis=0, keepdims=True)
        dn = dy * fg_ref[...]
        d_out = r2 * (dn - n_f * jnp.mean(dn * n_f, axis=-1, keepdims=True))
        dout_ref[...] = d_out
        d_out_b = d_out.astype(MXU_DTYPE)
        d_mix = _dot_nt(d_out_b, wo_ref[...])
        gwo_ref[...] = gwo_ref[...] + _dot(mix.T.astype(MXU_DTYPE), d_out_b)
        d_ya = d_mix[:, :D_ATTN]
        d_yc = d_mix[:, D_ATTN:]
        d_an = d_ya * sz_a
        dza_ref[...] = (d_ya * a_n * (sig_a * (1.0 + za * (1.0 - sig_a)))).astype(dza_ref.dtype)
        gag_ref[...] = gag_ref[...] + jnp.sum(d_an * n_a, axis=0, keepdims=True)
        dn_a = d_an * ag_ref[...]
        d_o = ra * (dn_a - n_a * (_group_sum(dn_a * n_a, gmat, STAT_TERMS) * inv_g))
        d_o_b = (d_o / l_ref[...]).astype(do_ref.dtype)
        do_ref[...] = d_o_b
        dl_ref[...] = _group_sum(d_o_b.astype(F32) * o_v, hr_ref[...])
        d_en = d_yc * sz_c
        dzc_ref[...] = (d_yc * e_n * (sig_c * (1.0 + zc * (1.0 - sig_c)))).astype(dzc_ref.dtype)
        gcg_ref[...] = gcg_ref[...] + jnp.sum(d_en * n_e, axis=0, keepdims=True)
        dn_e = d_en * cg_ref[...]
        d_e = re * (dn_e - n_e * (_group_sum(dn_e * n_e, gmat, STAT_TERMS) * inv_g))
        dgb_ref[...] = (d_e * conv).astype(dgb_ref.dtype)
        dcv_ref[...] = d_e * gb

    head_rep = jnp.where((lax.broadcasted_iota(jnp.int32, (D_ATTN, HEADS * LANE), 0) >> 6)
                         == (lax.broadcasted_iota(jnp.int32, (D_ATTN, HEADS * LANE), 1) >> 7), 1.0, 0.0).astype(MXU_DTYPE)
    row_blk = lambda cols: pl.BlockSpec((t, cols), lambda i: (i, 0))
    rest_blk = lambda s: pl.BlockSpec((t, 512), functools.partial(lambda i, s: (i, s), s=s))
    halo = lambda s: pl.BlockSpec((SUBLANE, 512), functools.partial(lambda i, s: (jnp.maximum(i * hb - 1, 0), s), s=s))
    const = lambda shape: pl.BlockSpec(shape, lambda i: (0, 0))
    acc = lambda shape: pl.BlockSpec(shape, lambda i: (0, 0))
    return pl.pallas_call(
        body, name="post_fwd_bwd", grid=(nt,),
        in_specs=[row_blk(D_ATTN), row_blk(D_ATTN)] + [rest_blk(s) for s in range(5)] + [halo(2), halo(3)]
                 + _x_block_specs(n_sub, LANE) + [const((LANE, D_MODEL))] + _x_block_specs(n_sub, LANE)
                 + [const((D_MODEL, D_MODEL)), const((1, D_ATTN)), const((1, D_CONV)), const((1, D_MODEL)),
                    const((SUBLANE, D_CONV)), const((D_ATTN, D_ATTN)), const((D_ATTN, HEADS * LANE))],
        out_specs=(row_blk(D_MODEL), row_blk(D_ATTN), row_blk(HEADS * LANE), row_blk(D_ATTN), row_blk(D_CONV),
                   row_blk(D_CONV), row_blk(D_CONV),
                   acc((1, LANE)), acc((1, D_MODEL)), acc((1, D_ATTN)), acc((1, D_CONV)), acc((D_MODEL, D_MODEL))),
        out_shape=(jax.ShapeDtypeStruct((lp, D_MODEL), F32), jax.ShapeDtypeStruct((lp, D_ATTN), MXU_DTYPE),
                   jax.ShapeDtypeStruct((lp, HEADS * LANE), F32), jax.ShapeDtypeStruct((lp, D_ATTN), MXU_DTYPE),
                   jax.ShapeDtypeStruct((lp, D_CONV), MXU_DTYPE), jax.ShapeDtypeStruct((lp, D_CONV), MXU_DTYPE),
                   jax.ShapeDtypeStruct((lp, D_CONV), F32),
                   jax.ShapeDtypeStruct((1, LANE), F32), jax.ShapeDtypeStruct((1, D_MODEL), F32),
                   jax.ShapeDtypeStruct((1, D_ATTN), F32), jax.ShapeDtypeStruct((1, D_CONV), F32),
                   jax.ShapeDtypeStruct((D_MODEL, D_MODEL), F32)),
        compiler_params=_params(("arbitrary",)),
    )(o, l_sum, *([rest] * 5), rest, rest, *([x2] * n_sub), meta_blk, *([tgt2] * n_sub),
      w_out, attn_g, conv_g, final_g, conv_w8, _group_matrix(), head_rep)


def _bwd_in(x2, meta_blk, norm_g, w_pad, bf_pad, fl, dc, dq, dk, dv, dza, dgb, dzc, dconv, rest, d_out, conv_w8):
    lp = fl.shape[0]
    t = ROW_TILE
    nt = lp // t
    n_sub = t // LANE
    hb = t // SUBLANE
    rev = lambda i: nt - 1 - i

    def body(*refs):
        x_refs = refs[:n_sub]
        (mb, g_ref, w_ref, bf_ref, fl_ref, dc_ref, dq_ref, dk_ref, dv_ref, dza_ref, dgb_ref, dzc_ref,
         dcv_ref, dcvn_ref, gc_ref, xc_ref, gch_ref, xch_ref, dout_ref, cw_ref, tri_ref) = refs[n_sub:n_sub + 21]
        dp_ref, gx_ref, front_ref, gn_ref, gbf_ref, gcw_ref, carry, dh_scr, gx_sems = refs[n_sub + 21:]
        step = pl.program_id(0)
        i = rev(step)

        @pl.when(step == 0)
        def _():
            for r in (gn_ref, gbf_ref, gcw_ref, carry):
                r[...] = jnp.zeros_like(r)

        dc8 = jnp.concatenate([dc_ref[...], jnp.zeros((LANE - HEADS, t), F32)], axis=0).T
        dlogf = _dot_exact(tri_ref[...], dc8) + carry[...]
        carry[...] = carry[...] + jnp.sum(dc8, axis=0, keepdims=True)
        z = fl_ref[...] + bf_ref[...]
        row = i * t + lax.broadcasted_iota(jnp.int32, (t, LANE), 0)
        d_f = jnp.where(row >= PAD_ROWS, dlogf * (1.0 / (1.0 + jnp.exp(z))), 0.0)
        gbf_ref[...] = gbf_ref[...] + jnp.sum(d_f, axis=0, keepdims=True)
        dcv = dcv_ref[...]
        dcv_next = jnp.where(i == nt - 1, 0.0, dcvn_ref[...])
        d_cx = (cw_ref[2:3, :] * dcv + cw_ref[1:2, :] * _shift_up(dcv, dcv_next, 1)
                + cw_ref[0:1, :] * _shift_up(dcv, dcv_next, 2))
        gc = gc_ref[...]
        xc = xc_ref[...]
        cx = gc * xc
        cx_prev = jnp.where(i == 0, 0.0, gch_ref[...] * xch_ref[...])
        rowi = lax.broadcasted_iota(jnp.int32, (SUBLANE, 1), 0)
        gcw = (jnp.where(rowi == 0, jnp.sum(dcv * _shift_down(cx_prev, cx, 2), axis=0, keepdims=True), 0.0)
               + jnp.where(rowi == 1, jnp.sum(dcv * _shift_down(cx_prev, cx, 1), axis=0, keepdims=True), 0.0)
               + jnp.where(rowi == 2, jnp.sum(dcv * cx, axis=0, keepdims=True), 0.0))
        gcw_ref[...] = gcw_ref[...] + gcw
        dp_ref[:, SEG_Q:SEG_Q + 512] = dq_ref[...]
        dp_ref[:, SEG_K:SEG_K + 512] = dk_ref[...]
        dp_ref[:, SEG_V:SEG_V + 512] = dv_ref[...]
        dp_ref[:, SEG_F:SEG_F + LANE] = d_f.astype(dp_ref.dtype)
        dp_ref[:, SEG_ZA:SEG_ZA + 512] = dza_ref[...]
        dp_ref[:, SEG_GB:SEG_GB + 512] = dgb_ref[...]
        dp_ref[:, SEG_GC:SEG_GC + 512] = (d_cx * xc).astype(dp_ref.dtype)
        dp_ref[:, SEG_XC:SEG_XC + 512] = (d_cx * gc).astype(dp_ref.dtype)
        dp_ref[:, SEG_ZC:SEG_ZC + 512] = dzc_ref[...]
        d_u = _dot(dp_ref[...], w_ref[...])
        first = jnp.where(i == 0, mb[...], x_refs[0][...])
        h = jnp.concatenate([first] + [r[...] for r in x_refs[1:]], axis=0)
        r1 = lax.rsqrt(jnp.mean(h * h, axis=-1, keepdims=True) + EPS)
        n_h = h * r1
        gn_ref[...] = gn_ref[...] + jnp.sum(d_u * n_h, axis=0, keepdims=True)
        dn = d_u * g_ref[...]
        d_h = dout_ref[...] + r1 * (dn - n_h * jnp.mean(dn * n_h, axis=-1, keepdims=True))
        slot = step % 2

        def to_grad_x(slot_, tile):
            return pltpu.make_async_copy(dh_scr.at[slot_], gx_ref.at[pl.ds(pl.multiple_of(tile * t - FRONT, SUBLANE), t)],
                                         gx_sems.at[slot_])

        @pl.when(step >= 2)
        def _():
            to_grad_x(slot, 1).wait()

        dh_scr[slot] = d_h

        @pl.when(i > 0)
        def _():
            to_grad_x(slot, i).start()

        @pl.when(i == 0)
        def _():
            front_ref[...] = d_h[:FRONT]
            rest_rows = pltpu.make_async_copy(dh_scr.at[slot, pl.ds(FRONT, t - FRONT)], gx_ref.at[pl.ds(0, t - FRONT)],
                                              gx_sems.at[slot])
            rest_rows.start()
            rest_rows.wait()
            if nt >= 2:
                to_grad_x(1 - slot, 1).wait()

    def x_specs():
        specs = [pl.BlockSpec((LANE, D_MODEL), lambda s: (jnp.maximum(n_sub * rev(s) - 1, 0), 0))]
        for b in range(1, n_sub):
            specs.append(pl.BlockSpec((LANE, D_MODEL), functools.partial(lambda s, b: (n_sub * rev(s) - 1 + b, 0), b=b)))
        return specs

    row_blk = lambda cols: pl.BlockSpec((t, cols), lambda s: (rev(s), 0))
    rest_blk = lambda k: pl.BlockSpec((t, 512), functools.partial(lambda s, k: (rev(s), k), k=k))
    halo_prev = lambda k: pl.BlockSpec(
        (SUBLANE, 512), functools.partial(lambda s, k: (jnp.maximum(rev(s) * hb - 1, 0), k), k=k))
    halo_next = pl.BlockSpec((SUBLANE, 512), lambda s: (jnp.minimum((rev(s) + 1) * hb, lp // SUBLANE - 1), 0))
    const = lambda shape: pl.BlockSpec(shape, lambda s: (0, 0))
    return pl.pallas_call(
        body, name="bwd_in", grid=(nt,),
        in_specs=x_specs() + [const((LANE, D_MODEL)), const((1, D_MODEL)),
                              pl.BlockSpec((D_IN_PAD, D_MODEL), lambda s: (0, 0), pipeline_mode=pl.Buffered(1)),
                              const((1, LANE)), row_blk(LANE),
                              pl.BlockSpec((HEADS, t), lambda s: (0, rev(s))),
                              row_blk(512), row_blk(512), row_blk(512), row_blk(512), row_blk(512), row_blk(512),
                              row_blk(512), halo_next, rest_blk(2), rest_blk(3), halo_prev(2), halo_prev(3),
                              row_blk(D_MODEL), const((SUBLANE, D_CONV)), const((t, t))],
        out_specs=(row_blk(D_IN_PAD), ANY, const((FRONT, D_MODEL)), const((1, D_MODEL)), const((1, LANE)),
                   const((SUBLANE, D_CONV))),
        out_shape=(jax.ShapeDtypeStruct((lp, D_IN_PAD), MXU_DTYPE), jax.ShapeDtypeStruct((lp - FRONT, D_MODEL), F32),
                   jax.ShapeDtypeStruct((FRONT, D_MODEL), F32),
                   jax.ShapeDtypeStruct((1, D_MODEL), F32), jax.ShapeDtypeStruct((1, LANE), F32),
                   jax.ShapeDtypeStruct((SUBLANE, D_CONV), F32)),
        scratch_shapes=[pltpu.VMEM((1, LANE), F32), pltpu.VMEM((2, t, D_MODEL), F32), pltpu.SemaphoreType.DMA((2,))],
        compiler_params=_params(("arbitrary",)),
    )(*([x2] * n_sub), meta_blk, norm_g, w_pad, bf_pad, fl, dc, dq, dk, dv, dza, dgb, dzc, dconv, dconv,
      rest, rest, rest, rest, d_out, conv_w8, _triangle(t, lower=False))


def _grad_w_in(u, dproj):
    lp = u.shape[0]
    tn = GW_COL_TILE
    tk = tn if lp % tn == 0 else ROW_TILE

    def body(d_ref, u_ref, o_ref, wire_ref):
        k = pl.program_id(1)

        @pl.when(k == 0)
        def _():
            o_ref[...] = jnp.zeros_like(o_ref)

        o_ref[...] = o_ref[...] + lax.dot_general(d_ref[...], u_ref[...], (((0,), (0,)), ((), ())),
                                                  preferred_element_type=F32)

        @pl.when(k == pl.num_programs(1) - 1)
        def _():
            wire_ref[...] = o_ref[...].astype(wire_ref.dtype)

    out_spec = pl.BlockSpec((tn, D_MODEL), lambda n, k: (n, 0))
    return pl.pallas_call(
        body, name="grad_w_in", grid=(D_IN_PAD // tn, lp // tk),
        in_specs=[pl.BlockSpec((tk, tn), lambda n, k: (k, n)), pl.BlockSpec((tk, D_MODEL), lambda n, k: (k, 0))],
        out_specs=(out_spec, out_spec),
        out_shape=(jax.ShapeDtypeStruct((D_IN_PAD, D_MODEL), F32), jax.ShapeDtypeStruct((D_IN_PAD, D_MODEL), WIRE_DTYPE)),
        compiler_params=_params(("parallel", "arbitrary")),
    )(dproj, u)


def _by_chip(own, others, me):
    by_mask = jnp.stack([own, others[1], others[0], others[2]])
    return [lax.dynamic_index_in_dim(by_mask, jnp.bitwise_xor(me, s), 0, keepdims=False) for s in range(N_CHIPS)]


def _both_halves(mine, other, c):
    return jnp.where(c == 0, jnp.concatenate([mine, other], axis=0), jnp.concatenate([other, mine], axis=0))


def _local_step(x2, tgt2, meta_full, norm_g, w_pad, b_f, conv_w_full, attn_g, conv_g, w_out_full, final_g):
    lp = x2.shape[0] + FRONT
    nt = lp // ROW_TILE
    meta_blk = jnp.concatenate([jnp.zeros((PAD_ROWS, D_MODEL), F32), meta_full], axis=0)
    bf_pad = jnp.pad(b_f, ((0, 0), (0, LANE - HEADS)))
    conv_w8 = jnp.pad(conv_w_full, ((0, SUBLANE - conv_w_full.shape[0]), (0, 0)))
    q, k, v, rest, fl, ct, u = _in_proj(x2, meta_blk, norm_g, w_pad, bf_pad)
    ct4 = ct.reshape(SUBLANE, nt, 1, ROW_TILE)
    o, l_sum, m_max = _attn_fwd(q, k, v, ct4)
    (d_out, d_o, delta, dza, dgb, dzc, dconv, loss, g_final, g_attn, g_convg, gw_out) = _post(
        o, l_sum, rest, x2, meta_blk, tgt2, w_out_full, attn_g, conv_g, final_g, conv_w8)
    dq, dk, dv, dc = _attn_bwd(q, k, v, d_o, m_max, delta, ct4)
    dproj, grad_x, d_front, g_norm, g_bf, g_cw = _bwd_in(x2, meta_blk, norm_g, w_pad, bf_pad, fl, dc.reshape(HEADS, lp), dq, dk, dv,
                                             dza, dgb, dzc, dconv, rest, d_out, conv_w8)
    gw_in, gw_in_wire = _grad_w_in(u, dproj)
    return dict(loss=loss, grad_x=grad_x, d_front=d_front, g_norm=g_norm, g_final=g_final, g_attn=g_attn, g_convg=g_convg, g_bf=g_bf,
                g_cw=g_cw, gw_out=gw_out, gw_in=gw_in, gw_in_wire=gw_in_wire)


def kernel(x, meta, norm_g, w_in, b_f, conv_w, attn_norm_g, conv_norm_g, w_out, final_norm_g, loss_target, m_meta, m_norm_g, m_w_in, m_b_f, m_conv_w, m_attn_norm_g, m_conv_norm_g, m_w_out, m_final_norm_g, v_meta, v_norm_g, v_w_in, v_b_f, v_conv_w, v_attn_norm_g, v_conv_norm_g, v_w_out, v_final_norm_g):
    cx_, cy_, cc_ = _position()
    chip = 2 * cx_ + cy_
    shard = w_in.shape[2]
    out_half = w_out.shape[1] // 2
    pick = lambda vals: jnp.where(chip == 0, vals[0], jnp.where(chip == 1, vals[1], jnp.where(chip == 2, vals[2], vals[3])))
    a_off, b_off = pick(A_OFF), pick(B_OFF)
    wt = jnp.transpose(w_in[0]).astype(MXU_DTYPE)
    wi = lax.dynamic_update_slice_in_dim(
        lax.dynamic_update_slice_in_dim(jnp.zeros((WIN_ROWS, D_MODEL), MXU_DTYPE), wt[:PIECE_A], a_off, 0),
        wt[PIECE_A:], b_off, 0)
    wo = w_out[0].astype(MXU_DTYPE)
    small = jnp.concatenate([meta, jnp.pad(conv_w[0], ((0, 8 - conv_w.shape[1]), (0, meta.shape[1] - conv_w.shape[2])))],
                            axis=0)
    gwi, gwo, gsm = _gather_weights(wi.reshape(2, WIN_HALF, D_MODEL), wo.reshape(2, out_half, D_MODEL), small)
    starts = jnp.stack([_window_start(jnp.bitwise_xor(chip, mask)) for mask in (0, 2, 1, 3)]).astype(jnp.int32)
    w_pad = _assemble_w(wi, gwi.reshape(3, WIN_ROWS, D_MODEL), starts)
    w_out_full = jnp.concatenate(_by_chip(wo, gwo.reshape(3, 2 * out_half, D_MODEL), chip), axis=0)
    small_full = jnp.concatenate(_by_chip(small, gsm, chip), axis=1)
    meta_full = small_full[:N_META]
    conv_w_full = jnp.concatenate([small_full[N_META:N_META + 3, 256 * s:256 * s + LANE] for s in range(N_CHIPS)], axis=1)
    final_g2 = final_norm_g.reshape(1, D_MODEL)
    r = _local_step(x[0], loss_target[0], meta_full, norm_g, w_pad, b_f, conv_w_full, attn_norm_g, conv_norm_g,
                    w_out_full, final_g2)
    grad_x = r["grad_x"][None]
    gb = r["gw_out"].reshape(N_CHIPS, 2, out_half, D_MODEL)
    ra, rb = _pair_exchange(r["gw_in_wire"], gb)
    c_idx = jnp.reshape(cc_, (1,)).astype(jnp.int32)
    chip_idx = jnp.reshape(chip, (1,)).astype(jnp.int32)
    pa, pa_wire = _pair_sum_windows(r["gw_in"], ra, c_idx)
    pb, pb_wire = _pair_sum(gb, rb, c_idx)
    xa, xb = _chip_exchange(pa_wire, pb_wire)
    ha = _chip_sum(pa, xa, chip_idx)
    hb = _chip_sum(pb, xb, chip_idx)
    oa, ob = _pair_share(ha, hb)
    g_window = _both_halves(ha, oa, cc_)
    g_w_in_t = jnp.concatenate([lax.dynamic_slice_in_dim(g_window, a_off, PIECE_A, 0),
                                lax.dynamic_slice_in_dim(g_window, b_off, shard - PIECE_A, 0)], axis=0)
    g_w_out = _both_halves(hb, ob, cc_)
    as_rows = lambda a: jnp.transpose(a, (2, 0, 1))
    g_w_in, d_w_in, nm_w_in, nv_w_in = (jnp.transpose(a, (1, 2, 0)) for a in _adamw_rows(
        as_rows(w_in), g_w_in_t, as_rows(m_w_in), as_rows(v_w_in)))
    d_w_out, nm_w_out, nv_w_out = (a[None] for a in _adamw_big(w_out[0], g_w_out, m_w_out[0], v_w_out[0], LANE))
    wide = lambda a: jnp.pad(a, ((0, 0), (0, D_MODEL - a.shape[1])))
    pack = jnp.concatenate([
        r["g_norm"], r["g_final"], jnp.concatenate([r["g_attn"], r["g_convg"]], axis=1), wide(r["g_bf"]),
        wide(r["loss"]), jnp.zeros((3, D_MODEL), F32), r["d_front"][PAD_ROWS:], wide(r["g_cw"])], axis=0)
    params = (norm_g, final_g2, attn_norm_g, conv_norm_g, b_f, meta, conv_w[0])
    ms = (m_norm_g, m_final_norm_g.reshape(1, D_MODEL), m_attn_norm_g, m_conv_norm_g, m_b_f, m_meta, m_conv_w[0])
    vs = (v_norm_g, v_final_norm_g.reshape(1, D_MODEL), v_attn_norm_g, v_conv_norm_g, v_b_f, v_meta, v_conv_w[0])
    loss, g_s, d_s, m_s, v_s = _small_update(pack, _gather_small(pack), params, ms, vs)

    def ordered(small_list, big_in, big_out):
        s_norm, s_final, s_attn, s_convg, s_bf, s_meta, s_cw = small_list
        return (s_meta, s_norm, big_in, s_bf, s_cw[None], s_attn, s_convg, big_out, s_final.reshape(D_MODEL))

    return (loss.reshape(()), grad_x,
            *ordered(g_s, g_w_in, g_w_out[None]), *ordered(d_s, d_w_in, d_w_out),
            *ordered(m_s, nm_w_in, nm_w_out), *ordered(v_s, nv_w_in, nv_w_out))
```

```python
import functools

import jax
import jax.numpy as jnp
from jax import lax
from jax.experimental import pallas as pl
from jax.experimental.pallas import tpu as pltpu

F32 = jnp.float32
MXU_DTYPE = jnp.bfloat16
WIRE_DTYPE = jnp.bfloat16

D_MODEL = 1024
N_META = 16
HEADS = 8
HEAD_DIM = 64
D_ATTN = HEADS * HEAD_DIM
D_CONV = 512
EPS = 1e-6
LANE = 128
SUBLANE = 8
ROW_TILE = 384
ATTN_UNROLL = 3
ATTN_BWD_QBLOCKS = 2
STAT_TERMS = 1
FRONT = LANE
PAD_ROWS = FRONT - N_META
NEG = -1e30
LOG2E = 1.4426950408889634
N_CHIPS = 4
N_DEV = 8
VMEM_LIMIT_BYTES = 60 * 1024 * 1024

SEG_Q, SEG_K, SEG_V, SEG_F, SEG_ZA, SEG_GB, SEG_GC, SEG_XC, SEG_ZC = (
    0, 512, 1024, 1536, 1664, 2176, 2688, 3200, 3712)
D_IN = 4104
D_IN_PAD = 4224
F_END = 1544
GW_COL_TILE = 1408
WIN_ROWS = 1152
WIN_HALF = WIN_ROWS // 2
WIN_START = (0, 1024, 2160, 3072)
PIECE_A = 518
A_OFF = (0, 2, 12, 126)
B_OFF = (518, 640, 530, 644)
ADAM_LR = 0.001
ADAM_B1 = 0.9
ADAM_B2 = 0.999
ADAM_EPS = 1e-08
ADAM_WD = 0.01
ADAM_STEP = 10

MESH = pl.DeviceIdType.MESH
ANY = pl.BlockSpec(memory_space=pl.ANY)

PACK_ROWS = 32
SLOT_NORM = (0, 1, 0, 1024)
SLOT_FINAL = (1, 2, 0, 1024)
SLOT_ATTN = (2, 3, 0, 512)
SLOT_CONVG = (2, 3, 512, 1024)
SLOT_BF = (3, 4, 0, 8)
SLOT_META = (8, 24, 0, 256)
SLOT_CONVW = (24, 27, 0, 128)
LOSS_ROW = 4


def _params(sem=None):
    return pltpu.CompilerParams(dimension_semantics=sem, vmem_limit_bytes=VMEM_LIMIT_BYTES)


def _sigmoid(z):
    return 1.0 / (1.0 + jnp.exp(-z))


def _dot(a, b):
    return jnp.dot(a, b, preferred_element_type=F32)


def _dot_nt(a, b):
    return lax.dot_general(a, b, (((1,), (1,)), ((), ())), preferred_element_type=F32)


def _dot_exact(ones, x):
    ones = ones.astype(MXU_DTYPE)
    total = None
    for _ in range(3):
        term = x.astype(MXU_DTYPE)
        x = x - term.astype(F32)
        total = _dot(ones, term) if total is None else total + _dot(ones, term)
    return total


def _group_matrix():
    r = lax.broadcasted_iota(jnp.int32, (D_ATTN, D_ATTN), 0) >> 6
    c = lax.broadcasted_iota(jnp.int32, (D_ATTN, D_ATTN), 1) >> 6
    return jnp.where(r == c, 1.0, 0.0).astype(MXU_DTYPE)


def _triangle(n, lower):
    r = lax.broadcasted_iota(jnp.int32, (n, n), 0)
    c = lax.broadcasted_iota(jnp.int32, (n, n), 1)
    return jnp.where((r >= c) if lower else (c >= r), 1.0, 0.0).astype(MXU_DTYPE)


def _group_sum(x, gmat, terms=2):
    hi = x.astype(MXU_DTYPE)
    if terms == 1:
        return _dot(hi, gmat)
    lo = (x - hi.astype(F32)).astype(MXU_DTYPE)
    return _dot(hi, gmat) + _dot(lo, gmat)


def _x_block_specs(n_sub, rows):
    specs = [pl.BlockSpec((rows, D_MODEL), lambda i: (jnp.maximum(n_sub * i - 1, 0), 0))]
    for b in range(1, n_sub):
        specs.append(pl.BlockSpec((rows, D_MODEL), functools.partial(lambda i, b: (n_sub * i - 1 + b, 0), b=b)))
    return specs


def _position():
    return lax.axis_index("x"), lax.axis_index("y"), lax.axis_index("c")


def _gather_weights(wi, wo, small):
    def body(wi_ref, wo_ref, sm_ref, gwi_ref, gwo_ref, gsm_ref, send_sems, recv_sems):
        x, y, c = _position()
        sibling = (x, y, 1 - c)
        chips = [(1 - x, y), (x, 1 - y), (1 - x, 1 - y)]

        def remote(k, src, dst, to):
            return pltpu.make_async_remote_copy(src_ref=src, dst_ref=dst, send_sem=send_sems.at[k],
                                                recv_sem=recv_sems.at[k], device_id=to, device_id_type=MESH)

        first, passed, landed = [], [], []
        for a, (src_ref, g_ref) in enumerate(((wi_ref, gwi_ref), (wo_ref, gwo_ref))):
            for j, (cx, cy) in enumerate(chips):
                slot = g_ref.at[j, c]
                first.append(remote(6 * a + j, src_ref.at[c], slot, (cx, cy, c)))
                landed.append(remote(6 * a + j, slot, slot, sibling))
                passed.append(remote(6 * a + 3 + j, slot, slot, sibling))
        for j, (cx, cy) in enumerate(chips):
            first.append(remote(12 + j, sm_ref, gsm_ref.at[j], (cx, cy, c)))
        for cp in first:
            cp.start()
        for arrived, onward in zip(landed, passed):
            arrived.wait_recv()
            onward.start()
        for a, g_ref in enumerate((gwi_ref, gwo_ref)):
            for j in range(3):
                remote(6 * a + 3 + j, g_ref.at[j, 1 - c], g_ref.at[j, 1 - c], sibling).wait_recv()
        for j in range(3):
            remote(12 + j, sm_ref, gsm_ref.at[j], sibling).wait_recv()
        for cp in first + passed:
            cp.wait_send()

    return pl.pallas_call(
        body, name="gather_weights",
        out_shape=(jax.ShapeDtypeStruct((3,) + wi.shape, wi.dtype), jax.ShapeDtypeStruct((3,) + wo.shape, wo.dtype),
                   jax.ShapeDtypeStruct((3,) + small.shape, small.dtype)),
        in_specs=[ANY, ANY, ANY], out_specs=(ANY, ANY, ANY),
        scratch_shapes=[pltpu.SemaphoreType.DMA((15,)), pltpu.SemaphoreType.DMA((15,))],
    )(wi, wo, small)


def _pair_exchange(gw, gb):
    def body(gw_ref, gb_ref, ra_ref, rb_ref, send_sems, recv_sems):
        x, y, c = _position()
        sibling = (x, y, 1 - c)
        copies = [pltpu.make_async_remote_copy(
            src_ref=gb_ref.at[:, 1 - c], dst_ref=rb_ref, send_sem=send_sems.at[N_CHIPS], recv_sem=recv_sems.at[N_CHIPS],
            device_id=sibling, device_id_type=MESH)]
        for s, start in enumerate(WIN_START):
            rows = pl.ds(pl.multiple_of(start + WIN_HALF * (1 - c), 2 * SUBLANE), WIN_HALF)
            copies.append(pltpu.make_async_remote_copy(
                src_ref=gw_ref.at[rows], dst_ref=ra_ref.at[s], send_sem=send_sems.at[s], recv_sem=recv_sems.at[s],
                device_id=sibling, device_id_type=MESH))
        for cp in copies:
            cp.start()
        for cp in copies:
            cp.wait()

    return pl.pallas_call(
        body, name="grad_pair_exchange",
        out_shape=(jax.ShapeDtypeStruct((N_CHIPS, WIN_HALF, D_MODEL), gw.dtype),
                   jax.ShapeDtypeStruct((N_CHIPS,) + gb.shape[2:], gb.dtype)),
        in_specs=[ANY, ANY], out_specs=(ANY, ANY),
        scratch_shapes=[pltpu.SemaphoreType.DMA((N_CHIPS + 1,)), pltpu.SemaphoreType.DMA((N_CHIPS + 1,))],
    )(gw, gb)


def _chip_exchange(pa, pb):
    def body(pa_ref, pb_ref, ra_ref, rb_ref, send_sems, recv_sems):
        x, y, c = _position()
        chips = [(1 - x, y), (x, 1 - y), (1 - x, 1 - y)]
        copies = []
        for a, (src, dst) in enumerate(((pa_ref, ra_ref), (pb_ref, rb_ref))):
            for j, (cx, cy) in enumerate(chips):
                copies.append(pltpu.make_async_remote_copy(
                    src_ref=src.at[2 * cx + cy], dst_ref=dst.at[j], send_sem=send_sems.at[3 * a + j],
                    recv_sem=recv_sems.at[3 * a + j], device_id=(cx, cy, c), device_id_type=MESH))
        for cp in copies:
            cp.start()
        for cp in copies:
            cp.wait()

    return pl.pallas_call(
        body, name="grad_chip_exchange",
        out_shape=(jax.ShapeDtypeStruct((3,) + pa.shape[1:], pa.dtype),
                   jax.ShapeDtypeStruct((3,) + pb.shape[1:], pb.dtype)),
        in_specs=[ANY, ANY], out_specs=(ANY, ANY),
        scratch_shapes=[pltpu.SemaphoreType.DMA((6,)), pltpu.SemaphoreType.DMA((6,))],
    )(pa, pb)


def _pair_share(ha, hb):
    def body(ha_ref, hb_ref, oa_ref, ob_ref, send_sems, recv_sems):
        x, y, c = _position()
        copies = [pltpu.make_async_remote_copy(
            src_ref=src, dst_ref=dst, send_sem=send_sems.at[k], recv_sem=recv_sems.at[k],
            device_id=(x, y, 1 - c), device_id_type=MESH)
            for k, (src, dst) in enumerate(((ha_ref, oa_ref), (hb_ref, ob_ref)))]
        for cp in copies:
            cp.start()
        for cp in copies:
            cp.wait()

    return pl.pallas_call(
        body, name="grad_pair_share",
        out_shape=(jax.ShapeDtypeStruct(ha.shape, ha.dtype), jax.ShapeDtypeStruct(hb.shape, hb.dtype)),
        in_specs=[ANY, ANY], out_specs=(ANY, ANY),
        scratch_shapes=[pltpu.SemaphoreType.DMA((2,)), pltpu.SemaphoreType.DMA((2,))],
    )(ha, hb)


def _gather_small(pack):
    def body(p_ref, o_ref, send_sems, recv_sems):
        x, y, c = _position()
        copies = []
        for mask in range(1, N_DEV):
            peer = (1 - x if mask & 4 else x, 1 - y if mask & 2 else y, 1 - c if mask & 1 else c)
            copies.append(pltpu.make_async_remote_copy(
                src_ref=p_ref, dst_ref=o_ref.at[mask - 1], send_sem=send_sems.at[mask - 1],
                recv_sem=recv_sems.at[mask - 1], device_id=peer, device_id_type=MESH))
        for cp in copies:
            cp.start()
        for cp in copies:
            cp.wait()

    return pl.pallas_call(
        body, name="gather_small",
        out_shape=jax.ShapeDtypeStruct((N_DEV - 1,) + pack.shape, pack.dtype),
        in_specs=[ANY], out_specs=ANY,
        scratch_shapes=[pltpu.SemaphoreType.DMA((N_DEV - 1,)), pltpu.SemaphoreType.DMA((N_DEV - 1,))],
    )(pack)


def _pair_sum(mine, recv, c_idx):
    rows, cols = mine.shape[2:]

    def body(c_ref, a_ref, b_ref, o_ref, send_ref):
        total = a_ref[...] + b_ref[...]
        o_ref[...] = total
        send_ref[...] = total.astype(send_ref.dtype)

    out_spec = pl.BlockSpec((None, rows, cols), lambda s, c_ref: (s, 0, 0))
    return pl.pallas_call(
        body, name="grad_pair_sum",
        grid_spec=pltpu.PrefetchScalarGridSpec(
            num_scalar_prefetch=1, grid=(N_CHIPS,),
            in_specs=[pl.BlockSpec((None, None, rows, cols), lambda s, c_ref: (s, c_ref[0], 0, 0)),
                      pl.BlockSpec((None, rows, cols), lambda s, c_ref: (s, 0, 0))],
            out_specs=(out_spec, out_spec)),
        out_shape=(jax.ShapeDtypeStruct(recv.shape, recv.dtype), jax.ShapeDtypeStruct(recv.shape, WIRE_DTYPE)),
        compiler_params=_params(("parallel",)),
    )(c_idx, mine, recv)


def _window_start(s):
    return jnp.where(s == 0, WIN_START[0], jnp.where(s == 1, WIN_START[1], jnp.where(s == 2, WIN_START[2], WIN_START[3])))


def _pair_sum_windows(gw, recv, c_idx):
    tr = WIN_HALF // 3

    def body(c_ref, a_ref, b_ref, o_ref, send_ref):
        total = a_ref[...] + b_ref[...].astype(F32)
        o_ref[...] = total
        send_ref[...] = total.astype(send_ref.dtype)

    out_spec = pl.BlockSpec((None, tr, D_MODEL), lambda s, i, c_ref: (s, i, 0))
    return pl.pallas_call(
        body, name="grad_pair_sum_windows",
        grid_spec=pltpu.PrefetchScalarGridSpec(
            num_scalar_prefetch=1, grid=(N_CHIPS, WIN_HALF // tr),
            in_specs=[pl.BlockSpec((pl.Element(tr), pl.Element(D_MODEL)),
                                   lambda s, i, c_ref: (pl.multiple_of(
                                       _window_start(s) + WIN_HALF * c_ref[0] + tr * i, SUBLANE), 0)),
                      pl.BlockSpec((None, tr, D_MODEL), lambda s, i, c_ref: (s, i, 0))],
            out_specs=(out_spec, out_spec)),
        out_shape=(jax.ShapeDtypeStruct(recv.shape, F32), jax.ShapeDtypeStruct(recv.shape, WIRE_DTYPE)),
        compiler_params=_params(("parallel", "parallel")),
    )(c_idx, gw, recv)


def _assemble_w(own, others, starts):
    def body(starts_ref, own_ref, oth_ref, o_ref):
        o_ref[...] = jnp.zeros_like(o_ref)
        for k in range(N_CHIPS):
            rows = pl.ds(pl.multiple_of(starts_ref[k], 2 * SUBLANE), WIN_ROWS)
            o_ref[rows, :] = o_ref[rows, :] + (own_ref[...] if k == 0 else oth_ref[k - 1])

    return pl.pallas_call(
        body, name="assemble_w",
        in_specs=[pl.BlockSpec(memory_space=pltpu.SMEM), pl.BlockSpec(memory_space=pltpu.VMEM),
                  pl.BlockSpec(memory_space=pltpu.VMEM)],
        out_specs=pl.BlockSpec(memory_space=pltpu.VMEM),
        out_shape=jax.ShapeDtypeStruct((D_IN_PAD, D_MODEL), own.dtype),
        compiler_params=_params(),
    )(starts, own, others)


def _chip_sum(psum, recv3, chip_idx):
    rows, cols = psum.shape[1:]
    tr = rows // 2

    def body(s_ref, p_ref, r0, r1, r2, o_ref):
        o_ref[...] = ((p_ref[...] + r0[...].astype(F32)) + r1[...].astype(F32)) + r2[...].astype(F32)

    return pl.pallas_call(
        body, name="grad_chip_sum",
        grid_spec=pltpu.PrefetchScalarGridSpec(
            num_scalar_prefetch=1, grid=(2,),
            in_specs=[pl.BlockSpec((None, tr, cols), lambda i, s_ref: (s_ref[0], i, 0))] +
                     [pl.BlockSpec((None, tr, cols), functools.partial(lambda i, s_ref, j: (j, i, 0), j=j))
                      for j in range(3)],
            out_specs=pl.BlockSpec((tr, cols), lambda i, s_ref: (i, 0))),
        out_shape=jax.ShapeDtypeStruct((rows, cols), psum.dtype),
        compiler_params=_params(("parallel",)),
    )(chip_idx, psum, recv3, recv3, recv3)


def _adamw_math(w, g, m, v):
    m = ADAM_B1 * m + (1.0 - ADAM_B1) * g
    v = ADAM_B2 * v + (1.0 - ADAM_B2) * (g * g)
    m_hat = m * (1.0 / (1.0 - ADAM_B1 ** ADAM_STEP))
    v_hat = v * (1.0 / (1.0 - ADAM_B2 ** ADAM_STEP))
    delta = -ADAM_LR * (m_hat / (jnp.sqrt(v_hat) + ADAM_EPS) + ADAM_WD * w)
    return delta, m, v


def _adamw_big(w, g, m, v, tr):
    rows, cols = w.shape
    assert rows % tr == 0 and g.shape[0] >= rows

    def body(w_ref, g_ref, m_ref, v_ref, d_out, m_out, v_out):
        d, m2, v2 = _adamw_math(w_ref[...], g_ref[...], m_ref[...], v_ref[...])
        d_out[...] = d
        m_out[...] = m2
        v_out[...] = v2

    spec = pl.BlockSpec((tr, cols), lambda i: (i, 0))
    sds = jax.ShapeDtypeStruct((rows, cols), F32)
    return pl.pallas_call(
        body, name="adamw_big", grid=(rows // tr,), in_specs=[spec] * 4, out_specs=(spec,) * 3,
        out_shape=(sds,) * 3, compiler_params=_params(("parallel",)),
    )(w, g, m, v)


def _adamw_rows(w3, g, m3, v3):
    rows, _, cols = w3.shape
    tc = 2 * LANE

    def body(w_ref, g_ref, m_ref, v_ref, g_out, d_out, m_out, v_out):
        g = g_ref[...]
        d, m2, v2 = _adamw_math(w_ref[:, 0, :], g, m_ref[:, 0, :], v_ref[:, 0, :])
        g_out[:, 0, :] = g
        d_out[:, 0, :] = d
        m_out[:, 0, :] = m2
        v_out[:, 0, :] = v2

    spec3 = pl.BlockSpec((rows, 1, tc), lambda i: (0, 0, i))
    sds = jax.ShapeDtypeStruct((rows, 1, cols), F32)
    return pl.pallas_call(
        body, name="adamw_rows", grid=(cols // tc,),
        in_specs=[spec3, pl.BlockSpec((rows, tc), lambda i: (0, i)), spec3, spec3], out_specs=(spec3,) * 4,
        out_shape=(sds,) * 4, compiler_params=_params(("parallel",)),
    )(w3, g, m3, v3)


def _small_update(own, others, params, ms, vs):
    slots = (SLOT_NORM, SLOT_FINAL, SLOT_ATTN, SLOT_CONVG, SLOT_BF, SLOT_META, SLOT_CONVW)
    n = len(slots)

    def body(*refs):
        own_ref, gp_ref = refs[:2]
        w_refs, m_refs, v_refs = refs[2:2 + n], refs[2 + n:2 + 2 * n], refs[2 + 2 * n:2 + 3 * n]
        outs = refs[2 + 3 * n:3 + 7 * n]
        loss_ref = outs[0]
        g_outs, d_outs, m_outs, v_outs = (outs[1 + k * n:1 + (k + 1) * n] for k in range(4))
        g_scr, w_scr, m_scr, v_scr = refs[3 + 7 * n:]
        x, y, c = _position()
        shard = 2 * x + y
        me = 4 * x + 2 * y + c
        tot = None
        for d in range(N_DEV):
            rel = jnp.bitwise_xor(me, d)
            term = jnp.where(rel == 0, own_ref[...], gp_ref[jnp.maximum(rel, 1) - 1])
            tot = term if tot is None else tot + term
        r0, r1, _, _ = SLOT_META
        meta_sel = tot[r0:r1, 0:256]
        cw_sel = tot[24:32, 0:128]
        for k in range(1, N_CHIPS):
            meta_sel = jnp.where(shard == k, tot[r0:r1, 256 * k:256 * (k + 1)], meta_sel)
            cw_sel = jnp.where(shard == k, tot[24:32, 128 * k:128 * (k + 1)], cw_sel)
        zeros = jnp.zeros((PACK_ROWS, D_MODEL), F32)
        for scr in (g_scr, w_scr, m_scr, v_scr):
            scr[...] = zeros
        g_scr[0:8, :] = tot[0:8, :]
        g_scr[r0:r1, 0:256] = meta_sel
        g_scr[24:32, 0:128] = cw_sel
        for (a, b, c0, c1), w_ref, m_ref, v_ref in zip(slots, w_refs, m_refs, v_refs):
            w_scr[a:b, c0:c1] = w_ref[...]
            m_scr[a:b, c0:c1] = m_ref[...]
            v_scr[a:b, c0:c1] = v_ref[...]
        loss_ref[...] = g_scr[LOSS_ROW:LOSS_ROW + 1, 0:1]
        d, m2, v2 = _adamw_math(w_scr[...], g_scr[...], m_scr[...], v_scr[...])
        w_scr[...] = d
        m_scr[...] = m2
        v_scr[...] = v2
        for (a, b, c0, c1), g_o, d_o, m_o, v_o in zip(slots, g_outs, d_outs, m_outs, v_outs):
            g_o[...] = g_scr[a:b, c0:c1]
            d_o[...] = w_scr[a:b, c0:c1]
            m_o[...] = m_scr[a:b, c0:c1]
            v_o[...] = v_scr[a:b, c0:c1]

    shapes = [jax.ShapeDtypeStruct(p.shape, F32) for p in params]
    out = pl.pallas_call(
        body, name="small_update",
        out_shape=[jax.ShapeDtypeStruct((1, 1), F32)] + shapes * 4,
        scratch_shapes=[pltpu.VMEM((PACK_ROWS, D_MODEL), F32)] * 4,
        compiler_params=_params(),
    )(own, others, *params, *ms, *vs)
    return out[0], out[1:1 + n], out[1 + n:1 + 2 * n], out[1 + 2 * n:1 + 3 * n], out[1 + 3 * n:1 + 4 * n]


def _in_proj(x2, meta_blk, norm_g, w_pad, bf_pad):
    seq = x2.shape[0]
    lp = seq + FRONT
    t = ROW_TILE
    nt = lp // t
    n_sub = t // LANE

    def body(*refs):
        x_refs = refs[:n_sub]
        mb, g_ref, w_ref, bf_ref, tri_ref = refs[n_sub:n_sub + 5]
        q_ref, k_ref, v_ref, rest_ref, fl_ref, ct_ref, u_ref, carry = refs[n_sub + 5:]
        i = pl.program_id(0)

        @pl.when(i == 0)
        def _():
            carry[...] = jnp.zeros_like(carry)

        first = jnp.where(i == 0, mb[...], x_refs[0][...])
        h = jnp.concatenate([first] + [r[...] for r in x_refs[1:]], axis=0)
        ms = jnp.mean(h * h, axis=-1, keepdims=True)
        u = ((h * lax.rsqrt(ms + EPS)) * g_ref[...]).astype(MXU_DTYPE)
        u_ref[...] = u

        def seg(a, width):
            return _dot_nt(u, w_ref[a:a + width, :])

        q_ref[...] = (seg(SEG_Q, D_ATTN) * (HEAD_DIM ** -0.5)).astype(MXU_DTYPE)
        k_ref[...] = seg(SEG_K, D_ATTN).astype(MXU_DTYPE)
        v_ref[...] = seg(SEG_V, D_ATTN).astype(MXU_DTYPE)
        for s in range(5):
            rest_ref[:, 512 * s:512 * (s + 1)] = seg(SEG_ZA + 512 * s, 512)
        fl = seg(SEG_F, LANE)
        fl_ref[...] = fl
        z = fl + bf_ref[...]
        logf = jnp.minimum(z, 0.0) - jnp.log(1.0 + jnp.exp(-jnp.abs(z)))
        row = i * t + lax.broadcasted_iota(jnp.int32, (t, LANE), 0)
        logf = jnp.where(row >= PAD_ROWS, logf, 0.0)
        cs = _dot_exact(tri_ref[...], logf) + carry[...]
        carry[...] = carry[...] + jnp.sum(logf, axis=0, keepdims=True)
        col = i * t + lax.broadcasted_iota(jnp.int32, (SUBLANE, t), 1)
        ct_ref[...] = jnp.where(col >= PAD_ROWS, cs.T[0:SUBLANE, :], -NEG)

    row_blk = lambda cols: pl.BlockSpec((t, cols), lambda i: (i, 0))
    const = lambda shape: pl.BlockSpec(shape, lambda i: (0, 0))
    return pl.pallas_call(
        body, name="in_proj", grid=(nt,),
        in_specs=_x_block_specs(n_sub, LANE) + [const((LANE, D_MODEL)), const((1, D_MODEL)),
                                                pl.BlockSpec((D_IN_PAD, D_MODEL), lambda i: (0, 0),
                                                             pipeline_mode=pl.Buffered(1)),
                                                const((1, LANE)), const((t, t))],
        out_specs=(row_blk(D_ATTN), row_blk(D_ATTN), row_blk(D_ATTN), row_blk(5 * 512), row_blk(LANE),
                   pl.BlockSpec((SUBLANE, t), lambda i: (0, i)), row_blk(D_MODEL)),
        out_shape=(jax.ShapeDtypeStruct((lp, D_ATTN), MXU_DTYPE), jax.ShapeDtypeStruct((lp, D_ATTN), MXU_DTYPE),
                   jax.ShapeDtypeStruct((lp, D_ATTN), MXU_DTYPE), jax.ShapeDtypeStruct((lp, 5 * 512), F32),
                   jax.ShapeDtypeStruct((lp, LANE), F32),
                   jax.ShapeDtypeStruct((SUBLANE, lp), F32), jax.ShapeDtypeStruct((lp, D_MODEL), MXU_DTYPE)),
        scratch_shapes=[pltpu.VMEM((1, LANE), F32)],
        compiler_params=_params(("arbitrary",)),
    )(*([x2] * n_sub), meta_blk, norm_g, w_pad, bf_pad, _triangle(t, lower=True))


def _head_masks():
    lane = lax.broadcasted_iota(jnp.int32, (1, LANE), 1)
    return lane < HEAD_DIM, lane >= HEAD_DIM


def _pair_specs(lp, nt, t):
    blk = pl.BlockSpec((lp, LANE), lambda g: (0, g))
    ct_a = pl.BlockSpec((None, nt, 1, t), lambda g: (2 * g, 0, 0, 0))
    ct_b = pl.BlockSpec((None, nt, 1, t), lambda g: (2 * g + 1, 0, 0, 0))
    return blk, ct_a, ct_b


def _sub_rows(s, col):
    return jnp.concatenate([s[:, a * LANE:(a + 1) * LANE] - col for a in range(s.shape[1] // LANE)], axis=1)


def _loop_unrolled(lo, hi, step, init, n):
    def group(jj, carry):
        for k in range(n):
            carry = step(lo + n * jj + k, carry)
        return carry

    groups = (hi - lo) // n
    carry = lax.fori_loop(0, groups, group, init)
    return lax.fori_loop(lo + n * groups, hi, step, carry)


def _lane_chunks(s):
    return [s[:, a * LANE:(a + 1) * LANE] for a in range(s.shape[1] // LANE)]


def _attn_fwd(q, k, v, ct4):
    lp = q.shape[0]
    t = ROW_TILE
    nt = lp // t

    def body(q_ref, k_ref, v_ref, cta_ref, ctb_ref, o_ref, l_ref, m_ref, s_scr, m_scr, acc_scr):
        masks = _head_masks()
        ct_refs = (cta_ref, ctb_ref)
        below = lax.broadcasted_iota(jnp.int32, (t, t), 1) <= lax.broadcasted_iota(jnp.int32, (t, t), 0)
        lane = lax.broadcasted_iota(jnp.int32, (1, LANE), 1)
        head_of_row = lax.broadcasted_iota(jnp.int32, (2 * t, LANE), 0) >= t
        ones_cols = jnp.where(lax.broadcasted_iota(jnp.int32, (2 * t, LANE), 1) == head_of_row.astype(jnp.int32),
                              1.0, 0.0).astype(MXU_DTYPE)

        def q_block(i, _):
            r0 = pl.multiple_of(i * t, t)
            qi = q_ref[pl.ds(r0, t), :]

            def scores(j):
                kj = k_ref[pl.ds(pl.multiple_of(j * t, t), t), :]
                return _dot_nt(qi, jnp.concatenate([jnp.where(hm, kj, 0).astype(MXU_DTYPE) for hm in masks], axis=0))

            def biased(j, hh, s2, diagonal):
                s = (s2[:, hh * t:(hh + 1) * t] - ct_refs[hh][j]) * LOG2E
                return jnp.where(below, s, NEG) if diagonal else s

            def max_step(j, carry, diagonal):
                s2 = scores(j)
                for hh in range(2):
                    s = biased(j, hh, s2, diagonal)
                    s_scr[j, :, hh * t:(hh + 1) * t] = s
                    m = m_scr[hh]
                    for c in _lane_chunks(s):
                        m = jnp.maximum(m, c)
                    m_scr[hh] = m
                return carry

            m_scr[...] = jnp.full((2, t, LANE), NEG, F32)
            _loop_unrolled(0, i, functools.partial(max_step, diagonal=False), 0, ATTN_UNROLL)
            max_step(i, 0, True)
            ms = [jnp.max(m_scr[hh], axis=-1, keepdims=True) for hh in range(2)]

            def sum_step(j, carry):
                vj = v_ref[pl.ds(pl.multiple_of(j * t, t), t), :]
                v2 = jnp.concatenate([jnp.where(hm, vj, 0).astype(MXU_DTYPE) for hm in masks], axis=0)
                parts = [jnp.exp2(s_scr[j, :, hh * t:(hh + 1) * t] - ms[hh]).astype(MXU_DTYPE) for hh in range(2)]
                acc_scr[...] = acc_scr[...] + _dot(jnp.concatenate(parts, axis=1), jnp.concatenate([v2, ones_cols], axis=1))
                return carry

            acc_scr[...] = jnp.zeros((t, 2 * LANE), F32)
            _loop_unrolled(0, i + 1, sum_step, 0, ATTN_UNROLL)
            acc = acc_scr[...]
            sums = acc[:, LANE:]
            l_pair = jnp.where(masks[0], jnp.sum(jnp.where(lane == 0, sums, 0.0), axis=-1, keepdims=True),
                               jnp.sum(jnp.where(lane == 1, sums, 0.0), axis=-1, keepdims=True))
            o_ref[pl.ds(r0, t), :] = acc[:, :LANE] / l_pair
            l_ref[pl.ds(r0, t), :] = l_pair
            m_ref[pl.ds(r0, t), 0:LANE] = jnp.broadcast_to(ms[0], (t, LANE))
            m_ref[pl.ds(r0, t), LANE:2 * LANE] = jnp.broadcast_to(ms[1], (t, LANE))
            return 0

        lax.fori_loop(0, nt, q_block, 0)

    blk, ct_a, ct_b = _pair_specs(lp, nt, t)
    return pl.pallas_call(
        body, name="attn_fwd", grid=(HEADS // 2,),
        in_specs=[blk, blk, blk, ct_a, ct_b], out_specs=(blk, blk, pl.BlockSpec((lp, 2 * LANE), lambda g: (0, g))),
        out_shape=(jax.ShapeDtypeStruct((lp, D_ATTN), F32), jax.ShapeDtypeStruct((lp, D_ATTN), F32),
                   jax.ShapeDtypeStruct((lp, HEADS * LANE), F32)),
        scratch_shapes=[pltpu.VMEM((nt, t, 2 * t), F32), pltpu.VMEM((2, t, LANE), F32), pltpu.VMEM((t, 2 * LANE), F32)],
        compiler_params=_params(("parallel",)),
    )(q, k, v, ct4, ct4)


def _attn_bwd(q, k, v, do, m, delta, ct4):
    lp = q.shape[0]
    t = ROW_TILE
    nt = lp // t

    def body(q_ref, k_ref, v_ref, do_ref, ma_ref, mb_ref, dla_ref, dlb_ref, cta_ref, ctb_ref,
             dq_ref, dk_ref, dv_ref, dc_ref, dq_acc, dk_acc, dv_acc):
        masks = _head_masks()
        ct_refs, m_refs, dl_refs = (cta_ref, ctb_ref), (ma_ref, mb_ref), (dla_ref, dlb_ref)
        below = lax.broadcasted_iota(jnp.int32, (t, t), 1) <= lax.broadcasted_iota(jnp.int32, (t, t), 0)
        tn = (((0,), (0,)), ((), ()))
        dq_acc[...] = jnp.zeros_like(dq_acc)

        def k_block(j, _):
            c0 = pl.multiple_of(j * t, t)
            kj = k_ref[pl.ds(c0, t), :]
            vj = v_ref[pl.ds(c0, t), :]
            k2 = jnp.concatenate([jnp.where(hm, kj, 0).astype(MXU_DTYPE) for hm in masks], axis=0)
            v2 = jnp.concatenate([jnp.where(hm, vj, 0).astype(MXU_DTYPE) for hm in masks], axis=0)
            ck = [r[j] for r in ct_refs]
            dk_acc[...] = jnp.zeros_like(dk_acc)
            dv_acc[...] = jnp.zeros_like(dv_acc)

            def q_block(i, colsums, diagonal, rows=t):
                r0 = pl.multiple_of(i * t, t)
                qi = q_ref[pl.ds(r0, rows), :]
                doi = do_ref[pl.ds(r0, rows), :]
                q2 = jnp.concatenate([jnp.where(hm, qi, 0).astype(MXU_DTYPE) for hm in masks], axis=0)
                do2 = jnp.concatenate([jnp.where(hm, doi, 0).astype(MXU_DTYPE) for hm in masks], axis=0)
                s2 = _dot_nt(qi, k2)
                dp2 = _dot_nt(doi, v2)
                out, ps, dss = [], [], []
                for hh in range(2):
                    s = (s2[:, hh * t:(hh + 1) * t] - ck[hh]) * LOG2E
                    if diagonal:
                        s = jnp.where(below, s, NEG)
                    p = jnp.exp2(_sub_rows(s, m_refs[hh][pl.ds(r0, rows), :])).astype(MXU_DTYPE)
                    ds32 = p.astype(F32) * _sub_rows(dp2[:, hh * t:(hh + 1) * t], dl_refs[hh][pl.ds(r0, rows), :])
                    ps.append(p)
                    dss.append(ds32.astype(MXU_DTYPE))
                    out.append(colsums[hh] + jnp.sum(ds32, axis=0, keepdims=True))
                dv_acc[...] = dv_acc[...] + lax.dot_general(jnp.concatenate(ps, axis=0), do2, tn,
                                                            preferred_element_type=F32)
                dk_acc[...] = dk_acc[...] + lax.dot_general(jnp.concatenate(dss, axis=0), q2, tn,
                                                            preferred_element_type=F32)
                dq_acc[pl.ds(r0, rows), :] = dq_acc[pl.ds(r0, rows), :] + _dot(jnp.concatenate(dss, axis=1), k2)
                return tuple(out)

            colsums = q_block(j, (jnp.zeros((1, t), F32), jnp.zeros((1, t), F32)), True)
            nq = ATTN_BWD_QBLOCKS
            groups = (nt - 1 - j) // nq
            colsums = lax.fori_loop(0, groups, lambda p, c: q_block(j + 1 + nq * p, c, False, nq * t), colsums)
            colsums = lax.fori_loop(j + 1 + nq * groups, nt, functools.partial(q_block, diagonal=False), colsums)
            for hh in range(2):
                dc_ref[hh, j] = -colsums[hh]
            dk_ref[pl.ds(c0, t), :] = dk_acc[...].astype(dk_ref.dtype)
            dv_ref[pl.ds(c0, t), :] = dv_acc[...].astype(dv_ref.dtype)
            return 0

        lax.fori_loop(0, nt, k_block, 0)
        dq_ref[...] = (dq_acc[...] * (HEAD_DIM ** -0.5)).astype(dq_ref.dtype)

    blk, ct_a, ct_b = _pair_specs(lp, nt, t)
    rep_a = pl.BlockSpec((lp, LANE), lambda g: (0, 2 * g))
    rep_b = pl.BlockSpec((lp, LANE), lambda g: (0, 2 * g + 1))
    return pl.pallas_call(
        body, name="attn_bwd", grid=(HEADS // 2,),
        in_specs=[blk] * 4 + [rep_a, rep_b, rep_a, rep_b, ct_a, ct_b],
        out_specs=(blk, blk, blk, pl.BlockSpec((2, nt, 1, t), lambda g: (g, 0, 0, 0))),
        out_shape=(jax.ShapeDtypeStruct((lp, D_ATTN), MXU_DTYPE),) * 3
                  + (jax.ShapeDtypeStruct((HEADS, nt, 1, t), F32),),
        scratch_shapes=[pltpu.VMEM((lp, LANE), F32), pltpu.VMEM((t, LANE), F32), pltpu.VMEM((t, LANE), F32)],
        compiler_params=_params(("parallel",)),
    )(q, k, v, do, m, m, delta, delta, ct4, ct4)


def _shift_down(prev8, cur, k):
    ext = jnp.concatenate([prev8, cur], axis=0)
    return pltpu.roll(ext, k, 0)[SUBLANE:, :]


def _shift_up(cur, next8, k):
    ext = jnp.concatenate([cur, next8], axis=0)
    n = ext.shape[0]
    return pltpu.roll(ext, n - k, 0)[:cur.shape[0], :]


def _post(o, l_sum, rest, x2, meta_blk, tgt2, w_out, attn_g, conv_g, final_g, conv_w8):
    lp = o.shape[0]
    t = ROW_TILE
    nt = lp // t
    n_sub = t // LANE
    hb = t // SUBLANE

    def body(*refs):
        o_ref, l_ref, za_ref, gb_ref, gc_ref, xc_ref, zc_ref, gch_ref, xch_ref = refs[:9]
        x_refs = refs[9:9 + n_sub]
        mb = refs[9 + n_sub]
        t_refs = refs[10 + n_sub:10 + 2 * n_sub]
        wo_ref, ag_ref, cg_ref, fg_ref, cw_ref, gm_ref, hr_ref = refs[10 + 2 * n_sub:17 + 2 * n_sub]
        (dout_ref, do_ref, dl_ref, dza_ref, dgb_ref, dzc_ref, dcv_ref,
         loss_ref, gf_ref, gag_ref, gcg_ref, gwo_ref) = refs[17 + 2 * n_sub:]
        i = pl.program_id(0)

        @pl.when(i == 0)
        def _():
            for r in (loss_ref, gf_ref, gag_ref, gcg_ref, gwo_ref):
                r[...] = jnp.zeros_like(r)

        gmat = gm_ref[...]
        inv_g = 1.0 / HEAD_DIM
        o_v = o_ref[...]
        ra = lax.rsqrt(_group_sum(o_v * o_v, gmat, STAT_TERMS) * inv_g + EPS)
        n_a = o_v * ra
        a_n = n_a * ag_ref[...]
        za = za_ref[...]
        sig_a = _sigmoid(za)
        sz_a = za * sig_a
        y_a = a_n * sz_a
        gb = gb_ref[...]
        gc = gc_ref[...]
        xc = xc_ref[...]
        cx = gc * xc
        cx_prev = jnp.where(i == 0, 0.0, gch_ref[...] * xch_ref[...])
        conv = (cw_ref[0:1, :] * _shift_down(cx_prev, cx, 2) + cw_ref[1:2, :] * _shift_down(cx_prev, cx, 1)
                + cw_ref[2:3, :] * cx)
        e = gb * conv
        re = lax.rsqrt(_group_sum(e * e, gmat, STAT_TERMS) * inv_g + EPS)
        n_e = e * re
        e_n = n_e * cg_ref[...]
        zc = zc_ref[...]
        sig_c = _sigmoid(zc)
        sz_c = zc * sig_c
        y_c = e_n * sz_c
        mix = jnp.concatenate([y_a, y_c], axis=-1)
        mix_b = mix.astype(MXU_DTYPE)
        first = jnp.where(i == 0, mb[...], x_refs[0][...])
        h = jnp.concatenate([first] + [r[...] for r in x_refs[1:]], axis=0)
        out = h + _dot(mix_b, wo_ref[...])
        r2 = lax.rsqrt(jnp.mean(out * out, axis=-1, keepdims=True) + EPS)
        n_f = out * r2
        y = n_f * fg_ref[...]
        tgt = jnp.concatenate([r[...] for r in t_refs], axis=0)
        valid = (i * t + lax.broadcasted_iota(jnp.int32, (t, 1), 0)) >= FRONT
        diff = jnp.where(valid, y - tgt, 0.0)
        loss_ref[...] = loss_ref[...] + 0.5 * jnp.sum(jnp.sum(diff * diff, axis=-1, keepdims=True) * (1.0 / D_MODEL))
        dy = diff * (1.0 / D_MODEL)
        gf_ref[...] = gf_ref[...] + jnp.sum(dy * n_f, axis=0, keepdims=True)
        dn = dy * fg_ref[...]
        d_out = r2 * (dn - n_f * jnp.mean(dn * n_f, axis=-1, keepdims=True))
        dout_ref[...] = d_out
        d_out_b = d_out.astype(MXU_DTYPE)
        d_mix = _dot_nt(d_out_b, wo_ref[...])
        gwo_ref[...] = gwo_ref[...] + _dot(mix.T.astype(MXU_DTYPE), d_out_b)
        d_ya = d_mix[:, :D_ATTN]
        d_yc = d_mix[:, D_ATTN:]
        d_an = d_ya * sz_a
        dza_ref[...] = (d_ya * a_n * (sig_a * (1.0 + za * (1.0 - sig_a)))).astype(dza_ref.dtype)
        gag_ref[...] = gag_ref[...] + jnp.sum(d_an * n_a, axis=0, keepdims=True)
        dn_a = d_an * ag_ref[...]
        d_o = ra * (dn_a - n_a * (_group_sum(dn_a * n_a, gmat, STAT_TERMS) * inv_g))
        d_o_b = (d_o / l_ref[...]).astype(do_ref.dtype)
        do_ref[...] = d_o_b
        dl_ref[...] = _group_sum(d_o_b.astype(F32) * o_v, hr_ref[...])
        d_en = d_yc * sz_c
        dzc_ref[...] = (d_yc * e_n * (sig_c * (1.0 + zc * (1.0 - sig_c)))).astype(dzc_ref.dtype)
        gcg_ref[...] = gcg_ref[...] + jnp.sum(d_en * n_e, axis=0, keepdims=True)
        dn_e = d_en * cg_ref[...]
        d_e = re * (dn_e - n_e * (_group_sum(dn_e * n_e, gmat, STAT_TERMS) * inv_g))
        dgb_ref[...] = (d_e * conv).astype(dgb_ref.dtype)
        dcv_ref[...] = d_e * gb

    head_rep = jnp.where((lax.broadcasted_iota(jnp.int32, (D_ATTN, HEADS * LANE), 0) >> 6)
                         == (lax.broadcasted_iota(jnp.int32, (D_ATTN, HEADS * LANE), 1) >> 7), 1.0, 0.0).astype(MXU_DTYPE)
    row_blk = lambda cols: pl.BlockSpec((t, cols), lambda i: (i, 0))
    rest_blk = lambda s: pl.BlockSpec((t, 512), functools.partial(lambda i, s: (i, s), s=s))
    halo = lambda s: pl.BlockSpec((SUBLANE, 512), functools.partial(lambda i, s: (jnp.maximum(i * hb - 1, 0), s), s=s))
    const = lambda shape: pl.BlockSpec(shape, lambda i: (0, 0))
    acc = lambda shape: pl.BlockSpec(shape, lambda i: (0, 0))
    return pl.pallas_call(
        body, name="post_fwd_bwd", grid=(nt,),
        in_specs=[row_blk(D_ATTN), row_blk(D_ATTN)] + [rest_blk(s) for s in range(5)] + [halo(2), halo(3)]
                 + _x_block_specs(n_sub, LANE) + [const((LANE, D_MODEL))] + _x_block_specs(n_sub, LANE)
                 + [const((D_MODEL, D_MODEL)), const((1, D_ATTN)), const((1, D_CONV)), const((1, D_MODEL)),
                    const((SUBLANE, D_CONV)), const((D_ATTN, D_ATTN)), const((D_ATTN, HEADS * LANE))],
        out_specs=(row_blk(D_MODEL), row_blk(D_ATTN), row_blk(HEADS * LANE), row_blk(D_ATTN), row_blk(D_CONV),
                   row_blk(D_CONV), row_blk(D_CONV),
                   acc((1, LANE)), acc((1, D_MODEL)), acc((1, D_ATTN)), acc((1, D_CONV)), acc((D_MODEL, D_MODEL))),
        out_shape=(jax.ShapeDtypeStruct((lp, D_MODEL), F32), jax.ShapeDtypeStruct((lp, D_ATTN), MXU_DTYPE),
                   jax.ShapeDtypeStruct((lp, HEADS * LANE), F32), jax.ShapeDtypeStruct((lp, D_ATTN), MXU_DTYPE),
                   jax.ShapeDtypeStruct((lp, D_CONV), MXU_DTYPE), jax.ShapeDtypeStruct((lp, D_CONV), MXU_DTYPE),
                   jax.ShapeDtypeStruct((lp, D_CONV), F32),
                   jax.ShapeDtypeStruct((1, LANE), F32), jax.ShapeDtypeStruct((1, D_MODEL), F32),
                   jax.ShapeDtypeStruct((1, D_ATTN), F32), jax.ShapeDtypeStruct((1, D_CONV), F32),
                   jax.ShapeDtypeStruct((D_MODEL, D_MODEL), F32)),
        compiler_params=_params(("arbitrary",)),
    )(o, l_sum, *([rest] * 5), rest, rest, *([x2] * n_sub), meta_blk, *([tgt2] * n_sub),
      w_out, attn_g, conv_g, final_g, conv_w8, _group_matrix(), head_rep)


def _bwd_in(x2, meta_blk, norm_g, w_pad, bf_pad, fl, dc, dq, dk, dv, dza, dgb, dzc, dconv, rest, d_out, conv_w8):
    lp = fl.shape[0]
    t = ROW_TILE
    nt = lp // t
    n_sub = t // LANE
    hb = t // SUBLANE
    rev = lambda i: nt - 1 - i

    def body(*refs):
        x_refs = refs[:n_sub]
        (mb, g_ref, w_ref, bf_ref, fl_ref, dc_ref, dq_ref, dk_ref, dv_ref, dza_ref, dgb_ref, dzc_ref,
         dcv_ref, dcvn_ref, gc_ref, xc_ref, gch_ref, xch_ref, dout_ref, cw_ref, tri_ref) = refs[n_sub:n_sub + 21]
        dp_ref, gx_ref, front_ref, gn_ref, gbf_ref, gcw_ref, carry, dh_scr, gx_sems = refs[n_sub + 21:]
        step = pl.program_id(0)
        i = rev(step)

        @pl.when(step == 0)
        def _():
            for r in (gn_ref, gbf_ref, gcw_ref, carry):
                r[...] = jnp.zeros_like(r)

        dc8 = jnp.concatenate([dc_ref[...], jnp.zeros((LANE - HEADS, t), F32)], axis=0).T
        dlogf = _dot_exact(tri_ref[...], dc8) + carry[...]
        carry[...] = carry[...] + jnp.sum(dc8, axis=0, keepdims=True)
        z = fl_ref[...] + bf_ref[...]
        row = i * t + lax.broadcasted_iota(jnp.int32, (t, LANE), 0)
        d_f = jnp.where(row >= PAD_ROWS, dlogf * (1.0 / (1.0 + jnp.exp(z))), 0.0)
        gbf_ref[...] = gbf_ref[...] + jnp.sum(d_f, axis=0, keepdims=True)
        dcv = dcv_ref[...]
        dcv_next = jnp.where(i == nt - 1, 0.0, dcvn_ref[...])
        d_cx = (cw_ref[2:3, :] * dcv + cw_ref[1:2, :] * _shift_up(dcv, dcv_next, 1)
                + cw_ref[0:1, :] * _shift_up(dcv, dcv_next, 2))
        gc = gc_ref[...]
        xc = xc_ref[...]
        cx = gc * xc
        cx_prev = jnp.where(i == 0, 0.0, gch_ref[...] * xch_ref[...])
        rowi = lax.broadcasted_iota(jnp.int32, (SUBLANE, 1), 0)
        gcw = (jnp.where(rowi == 0, jnp.sum(dcv * _shift_down(cx_prev, cx, 2), axis=0, keepdims=True), 0.0)
               + jnp.where(rowi == 1, jnp.sum(dcv * _shift_down(cx_prev, cx, 1), axis=0, keepdims=True), 0.0)
               + jnp.where(rowi == 2, jnp.sum(dcv * cx, axis=0, keepdims=True), 0.0))
        gcw_ref[...] = gcw_ref[...] + gcw
        dp_ref[:, SEG_Q:SEG_Q + 512] = dq_ref[...]
        dp_ref[:, SEG_K:SEG_K + 512] = dk_ref[...]
        dp_ref[:, SEG_V:SEG_V + 512] = dv_ref[...]
        dp_ref[:, SEG_F:SEG_F + LANE] = d_f.astype(dp_ref.dtype)
        dp_ref[:, SEG_ZA:SEG_ZA + 512] = dza_ref[...]
        dp_ref[:, SEG_GB:SEG_GB + 512] = dgb_ref[...]
        dp_ref[:, SEG_GC:SEG_GC + 512] = (d_cx * xc).astype(dp_ref.dtype)
        dp_ref[:, SEG_XC:SEG_XC + 512] = (d_cx * gc).astype(dp_ref.dtype)
        dp_ref[:, SEG_ZC:SEG_ZC + 512] = dzc_ref[...]
        d_u = _dot(dp_ref[...], w_ref[...])
        first = jnp.where(i == 0, mb[...], x_refs[0][...])
        h = jnp.concatenate([first] + [r[...] for r in x_refs[1:]], axis=0)
        r1 = lax.rsqrt(jnp.mean(h * h, axis=-1, keepdims=True) + EPS)
        n_h = h * r1
        gn_ref[...] = gn_ref[...] + jnp.sum(d_u * n_h, axis=0, keepdims=True)
        dn = d_u * g_ref[...]
        d_h = dout_ref[...] + r1 * (dn - n_h * jnp.mean(dn * n_h, axis=-1, keepdims=True))
        slot = step % 2

        def to_grad_x(slot_, tile):
            return pltpu.make_async_copy(dh_scr.at[slot_], gx_ref.at[pl.ds(pl.multiple_of(tile * t - FRONT, SUBLANE), t)],
                                         gx_sems.at[slot_])

        @pl.when(step >= 2)
        def _():
            to_grad_x(slot, 1).wait()

        dh_scr[slot] = d_h

        @pl.when(i > 0)
        def _():
            to_grad_x(slot, i).start()

        @pl.when(i == 0)
        def _():
            front_ref[...] = d_h[:FRONT]
            rest_rows = pltpu.make_async_copy(dh_scr.at[slot, pl.ds(FRONT, t - FRONT)], gx_ref.at[pl.ds(0, t - FRONT)],
                                              gx_sems.at[slot])
            rest_rows.start()
            rest_rows.wait()
            if nt >= 2:
                to_grad_x(1 - slot, 1).wait()

    def x_specs():
        specs = [pl.BlockSpec((LANE, D_MODEL), lambda s: (jnp.maximum(n_sub * rev(s) - 1, 0), 0))]
        for b in range(1, n_sub):
            specs.append(pl.BlockSpec((LANE, D_MODEL), functools.partial(lambda s, b: (n_sub * rev(s) - 1 + b, 0), b=b)))
        return specs

    row_blk = lambda cols: pl.BlockSpec((t, cols), lambda s: (rev(s), 0))
    rest_blk = lambda k: pl.BlockSpec((t, 512), functools.partial(lambda s, k: (rev(s), k), k=k))
    halo_prev = lambda k: pl.BlockSpec(
        (SUBLANE, 512), functools.partial(lambda s, k: (jnp.maximum(rev(s) * hb - 1, 0), k), k=k))
    halo_next = pl.BlockSpec((SUBLANE, 512), lambda s: (jnp.minimum((rev(s) + 1) * hb, lp // SUBLANE - 1), 0))
    const = lambda shape: pl.BlockSpec(shape, lambda s: (0, 0))
    return pl.pallas_call(
        body, name="bwd_in", grid=(nt,),
        in_specs=x_specs() + [const((LANE, D_MODEL)), const((1, D_MODEL)),
                              pl.BlockSpec((D_IN_PAD, D_MODEL), lambda s: (0, 0), pipeline_mode=pl.Buffered(1)),
                              const((1, LANE)), row_blk(LANE),
                              pl.BlockSpec((HEADS, t), lambda s: (0, rev(s))),
                              row_blk(512), row_blk(512), row_blk(512), row_blk(512), row_blk(512), row_blk(512),
                              row_blk(512), halo_next, rest_blk(2), rest_blk(3), halo_prev(2), halo_prev(3),
                              row_blk(D_MODEL), const((SUBLANE, D_CONV)), const((t, t))],
        out_specs=(row_blk(D_IN_PAD), ANY, const((FRONT, D_MODEL)), const((1, D_MODEL)), const((1, LANE)),
                   const((SUBLANE, D_CONV))),
        out_shape=(jax.ShapeDtypeStruct((lp, D_IN_PAD), MXU_DTYPE), jax.ShapeDtypeStruct((lp - FRONT, D_MODEL), F32),
                   jax.ShapeDtypeStruct((FRONT, D_MODEL), F32),
                   jax.ShapeDtypeStruct((1, D_MODEL), F32), jax.ShapeDtypeStruct((1, LANE), F32),
                   jax.ShapeDtypeStruct((SUBLANE, D_CONV), F32)),
        scratch_shapes=[pltpu.VMEM((1, LANE), F32), pltpu.VMEM((2, t, D_MODEL), F32), pltpu.SemaphoreType.DMA((2,))],
        compiler_params=_params(("arbitrary",)),
    )(*([x2] * n_sub), meta_blk, norm_g, w_pad, bf_pad, fl, dc, dq, dk, dv, dza, dgb, dzc, dconv, dconv,
      rest, rest, rest, rest, d_out, conv_w8, _triangle(t, lower=False))


def _grad_w_in(u, dproj):
    lp = u.shape[0]
    tn = GW_COL_TILE
    tk = tn if lp % tn == 0 else ROW_TILE

    def body(d_ref, u_ref, o_ref, wire_ref):
        k = pl.program_id(1)

        @pl.when(k == 0)
        def _():
            o_ref[...] = jnp.zeros_like(o_ref)

        o_ref[...] = o_ref[...] + lax.dot_general(d_ref[...], u_ref[...], (((0,), (0,)), ((), ())),
                                                  preferred_element_type=F32)

        @pl.when(k == pl.num_programs(1) - 1)
        def _():
            wire_ref[...] = o_ref[...].astype(wire_ref.dtype)

    out_spec = pl.BlockSpec((tn, D_MODEL), lambda n, k: (n, 0))
    return pl.pallas_call(
        body, name="grad_w_in", grid=(D_IN_PAD // tn, lp // tk),
        in_specs=[pl.BlockSpec((tk, tn), lambda n, k: (k, n)), pl.BlockSpec((tk, D_MODEL), lambda n, k: (k, 0))],
        out_specs=(out_spec, out_spec),
        out_shape=(jax.ShapeDtypeStruct((D_IN_PAD, D_MODEL), F32), jax.ShapeDtypeStruct((D_IN_PAD, D_MODEL), WIRE_DTYPE)),
        compiler_params=_params(("parallel", "arbitrary")),
    )(dproj, u)


def _by_chip(own, others, me):
    by_mask = jnp.stack([own, others[1], others[0], others[2]])
    return [lax.dynamic_index_in_dim(by_mask, jnp.bitwise_xor(me, s), 0, keepdims=False) for s in range(N_CHIPS)]


def _both_halves(mine, other, c):
    return jnp.where(c == 0, jnp.concatenate([mine, other], axis=0), jnp.concatenate([other, mine], axis=0))


def _local_step(x2, tgt2, meta_full, norm_g, w_pad, b_f, conv_w_full, attn_g, conv_g, w_out_full, final_g):
    lp = x2.shape[0] + FRONT
    nt = lp // ROW_TILE
    meta_blk = jnp.concatenate([jnp.zeros((PAD_ROWS, D_MODEL), F32), meta_full], axis=0)
    bf_pad = jnp.pad(b_f, ((0, 0), (0, LANE - HEADS)))
    conv_w8 = jnp.pad(conv_w_full, ((0, SUBLANE - conv_w_full.shape[0]), (0, 0)))
    q, k, v, rest, fl, ct, u = _in_proj(x2, meta_blk, norm_g, w_pad, bf_pad)
    ct4 = ct.reshape(SUBLANE, nt, 1, ROW_TILE)
    o, l_sum, m_max = _attn_fwd(q, k, v, ct4)
    (d_out, d_o, delta, dza, dgb, dzc, dconv, loss, g_final, g_attn, g_convg, gw_out) = _post(
        o, l_sum, rest, x2, meta_blk, tgt2, w_out_full, attn_g, conv_g, final_g, conv_w8)
    dq, dk, dv, dc = _attn_bwd(q, k, v, d_o, m_max, delta, ct4)
    dproj, grad_x, d_front, g_norm, g_bf, g_cw = _bwd_in(x2, meta_blk, norm_g, w_pad, bf_pad, fl, dc.reshape(HEADS, lp), dq, dk, dv,
                                             dza, dgb, dzc, dconv, rest, d_out, conv_w8)
    gw_in, gw_in_wire = _grad_w_in(u, dproj)
    return dict(loss=loss, grad_x=grad_x, d_front=d_front, g_norm=g_norm, g_final=g_final, g_attn=g_attn, g_convg=g_convg, g_bf=g_bf,
                g_cw=g_cw, gw_out=gw_out, gw_in=gw_in, gw_in_wire=gw_in_wire)


def kernel(x, meta, norm_g, w_in, b_f, conv_w, attn_norm_g, conv_norm_g, w_out, final_norm_g, loss_target, m_meta, m_norm_g, m_w_in, m_b_f, m_conv_w, m_attn_norm_g, m_conv_norm_g, m_w_out, m_final_norm_g, v_meta, v_norm_g, v_w_in, v_b_f, v_conv_w, v_attn_norm_g, v_conv_norm_g, v_w_out, v_final_norm_g):
    cx_, cy_, cc_ = _position()
    chip = 2 * cx_ + cy_
    shard = w_in.shape[2]
    out_half = w_out.shape[1] // 2
    pick = lambda vals: jnp.where(chip == 0, vals[0], jnp.where(chip == 1, vals[1], jnp.where(chip == 2, vals[2], vals[3])))
    a_off, b_off = pick(A_OFF), pick(B_OFF)
    wt = jnp.transpose(w_in[0]).astype(MXU_DTYPE)
    wi = lax.dynamic_update_slice_in_dim(
        lax.dynamic_update_slice_in_dim(jnp.zeros((WIN_ROWS, D_MODEL), MXU_DTYPE), wt[:PIECE_A], a_off, 0),
        wt[PIECE_A:], b_off, 0)
    wo = w_out[0].astype(MXU_DTYPE)
    small = jnp.concatenate([meta, jnp.pad(conv_w[0], ((0, 8 - conv_w.shape[1]), (0, meta.shape[1] - conv_w.shape[2])))],
                            axis=0)
    gwi, gwo, gsm = _gather_weights(wi.reshape(2, WIN_HALF, D_MODEL), wo.reshape(2, out_half, D_MODEL), small)
    starts = jnp.stack([_window_start(jnp.bitwise_xor(chip, mask)) for mask in (0, 2, 1, 3)]).astype(jnp.int32)
    w_pad = _assemble_w(wi, gwi.reshape(3, WIN_ROWS, D_MODEL), starts)
    w_out_full = jnp.concatenate(_by_chip(wo, gwo.reshape(3, 2 * out_half, D_MODEL), chip), axis=0)
    small_full = jnp.concatenate(_by_chip(small, gsm, chip), axis=1)
    meta_full = small_full[:N_META]
    conv_w_full = jnp.concatenate([small_full[N_META:N_META + 3, 256 * s:256 * s + LANE] for s in range(N_CHIPS)], axis=1)
    final_g2 = final_norm_g.reshape(1, D_MODEL)
    r = _local_step(x[0], loss_target[0], meta_full, norm_g, w_pad, b_f, conv_w_full, attn_norm_g, conv_norm_g,
                    w_out_full, final_g2)
    grad_x = r["grad_x"][None]
    gb = r["gw_out"].reshape(N_CHIPS, 2, out_half, D_MODEL)
    ra, rb = _pair_exchange(r["gw_in_wire"], gb)
    c_idx = jnp.reshape(cc_, (1,)).astype(jnp.int32)
    chip_idx = jnp.reshape(chip, (1,)).astype(jnp.int32)
    pa, pa_wire = _pair_sum_windows(r["gw_in"], ra, c_idx)
    pb, pb_wire = _pair_sum(gb, rb, c_idx)
    xa, xb = _chip_exchange(pa_wire, pb_wire)
    ha = _chip_sum(pa, xa, chip_idx)
    hb = _chip_sum(pb, xb, chip_idx)
    oa, ob = _pair_share(ha, hb)
    g_window = _both_halves(ha, oa, cc_)
    g_w_in_t = jnp.concatenate([lax.dynamic_slice_in_dim(g_window, a_off, PIECE_A, 0),
                                lax.dynamic_slice_in_dim(g_window, b_off, shard - PIECE_A, 0)], axis=0)
    g_w_out = _both_halves(hb, ob, cc_)
    as_rows = lambda a: jnp.transpose(a, (2, 0, 1))
    g_w_in, d_w_in, nm_w_in, nv_w_in = (jnp.transpose(a, (1, 2, 0)) for a in _adamw_rows(
        as_rows(w_in), g_w_in_t, as_rows(m_w_in), as_rows(v_w_in)))
    d_w_out, nm_w_out, nv_w_out = (a[None] for a in _adamw_big(w_out[0], g_w_out, m_w_out[0], v_w_out[0], LANE))
    wide = lambda a: jnp.pad(a, ((0, 0), (0, D_MODEL - a.shape[1])))
    pack = jnp.concatenate([
        r["g_norm"], r["g_final"], jnp.concatenate([r["g_attn"], r["g_convg"]], axis=1), wide(r["g_bf"]),
        wide(r["loss"]), jnp.zeros((3, D_MODEL), F32), r["d_front"][PAD_ROWS:], wide(r["g_cw"])], axis=0)
    params = (norm_g, final_g2, attn_norm_g, conv_norm_g, b_f, meta, conv_w[0])
    ms = (m_norm_g, m_final_norm_g.reshape(1, D_MODEL), m_attn_norm_g, m_conv_norm_g, m_b_f, m_meta, m_conv_w[0])
    vs = (v_norm_g, v_final_norm_g.reshape(1, D_MODEL), v_attn_norm_g, v_conv_norm_g, v_b_f, v_meta, v_conv_w[0])
    loss, g_s, d_s, m_s, v_s = _small_update(pack, _gather_small(pack), params, ms, vs)

    def ordered(small_list, big_in, big_out):
        s_norm, s_final, s_attn, s_convg, s_bf, s_meta, s_cw = small_list
        return (s_meta, s_norm, big_in, s_bf, s_cw[None], s_attn, s_convg, big_out, s_final.reshape(D_MODEL))

    return (loss.reshape(()), grad_x,
            *ordered(g_s, g_w_in, g_w_out[None]), *ordered(d_s, d_w_in, d_w_out),
            *ordered(m_s, nm_w_in, nm_w_out), *ordered(v_s, nv_w_in, nv_w_out))
```

```python
import functools

import jax
import jax.numpy as jnp
from jax import lax
from jax.experimental import pallas as pl
from jax.experimental.pallas import tpu as pltpu

F32 = jnp.float32
MXU_DTYPE = jnp.bfloat16
WIRE_DTYPE = jnp.bfloat16

D_MODEL = 1024
N_META = 16
HEADS = 8
HEAD_DIM = 64
D_ATTN = HEADS * HEAD_DIM
D_CONV = 512
EPS = 1e-6
LANE = 128
SUBLANE = 8
ROW_TILE = 384
ATTN_UNROLL = 3
ATTN_BWD_QBLOCKS = 2
STAT_TERMS = 1
FRONT = LANE
PAD_ROWS = FRONT - N_META
NEG = -1e30
LOG2E = 1.4426950408889634
N_CHIPS = 4
N_DEV = 8
VMEM_LIMIT_BYTES = 60 * 1024 * 1024

SEG_Q, SEG_K, SEG_V, SEG_F, SEG_ZA, SEG_GB, SEG_GC, SEG_XC, SEG_ZC = (
    0, 512, 1024, 1536, 1664, 2176, 2688, 3200, 3712)
D_IN = 4104
D_IN_PAD = 4224
F_END = 1544
GW_COL_TILE = 1408
WIN_ROWS = 1152
WIN_HALF = WIN_ROWS // 2
WIN_START = (0, 1024, 2160, 3072)
PIECE_A = 518
A_OFF = (0, 2, 12, 126)
B_OFF = (518, 640, 530, 644)
ADAM_LR = 0.001
ADAM_B1 = 0.9
ADAM_B2 = 0.999
ADAM_EPS = 1e-08
ADAM_WD = 0.01
ADAM_STEP = 10

MESH = pl.DeviceIdType.MESH
ANY = pl.BlockSpec(memory_space=pl.ANY)

PACK_ROWS = 32
SLOT_NORM = (0, 1, 0, 1024)
SLOT_FINAL = (1, 2, 0, 1024)
SLOT_ATTN = (2, 3, 0, 512)
SLOT_CONVG = (2, 3, 512, 1024)
SLOT_BF = (3, 4, 0, 8)
SLOT_META = (8, 24, 0, 256)
SLOT_CONVW = (24, 27, 0, 128)
LOSS_ROW = 4


def _params(sem=None):
    return pltpu.CompilerParams(dimension_semantics=sem, vmem_limit_bytes=VMEM_LIMIT_BYTES)


def _sigmoid(z):
    return 1.0 / (1.0 + jnp.exp(-z))


def _dot(a, b):
    return jnp.dot(a, b, preferred_element_type=F32)


def _dot_nt(a, b):
    return lax.dot_general(a, b, (((1,), (1,)), ((), ())), preferred_element_type=F32)


def _dot_exact(ones, x):
    ones = ones.astype(MXU_DTYPE)
    total = None
    for _ in range(3):
        term = x.astype(MXU_DTYPE)
        x = x - term.astype(F32)
        total = _dot(ones, term) if total is None else total + _dot(ones, term)
    return total


def _group_matrix():
    r = lax.broadcasted_iota(jnp.int32, (D_ATTN, D_ATTN), 0) >> 6
    c = lax.broadcasted_iota(jnp.int32, (D_ATTN, D_ATTN), 1) >> 6
    return jnp.where(r == c, 1.0, 0.0).astype(MXU_DTYPE)


def _triangle(n, lower):
    r = lax.broadcasted_iota(jnp.int32, (n, n), 0)
    c = lax.broadcasted_iota(jnp.int32, (n, n), 1)
    return jnp.where((r >= c) if lower else (c >= r), 1.0, 0.0).astype(MXU_DTYPE)


def _group_sum(x, gmat, terms=2):
    hi = x.astype(MXU_DTYPE)
    if terms == 1:
        return _dot(hi, gmat)
    lo = (x - hi.astype(F32)).astype(MXU_DTYPE)
    return _dot(hi, gmat) + _dot(lo, gmat)


def _x_block_specs(n_sub, rows):
    specs = [pl.BlockSpec((rows, D_MODEL), lambda i: (jnp.maximum(n_sub * i - 1, 0), 0))]
    for b in range(1, n_sub):
        specs.append(pl.BlockSpec((rows, D_MODEL), functools.partial(lambda i, b: (n_sub * i - 1 + b, 0), b=b)))
    return specs


def _position():
    return lax.axis_index("x"), lax.axis_index("y"), lax.axis_index("c")


def _gather_weights(wi, wo, small):
    def body(wi_ref, wo_ref, sm_ref, gwi_ref, gwo_ref, gsm_ref, send_sems, recv_sems):
        x, y, c = _position()
        sibling = (x, y, 1 - c)
        chips = [(1 - x, y), (x, 1 - y), (1 - x, 1 - y)]

        def remote(k, src, dst, to):
            return pltpu.make_async_remote_copy(src_ref=src, dst_ref=dst, send_sem=send_sems.at[k],
                                                recv_sem=recv_sems.at[k], device_id=to, device_id_type=MESH)

        first, passed, landed = [], [], []
        for a, (src_ref, g_ref) in enumerate(((wi_ref, gwi_ref), (wo_ref, gwo_ref))):
            for j, (cx, cy) in enumerate(chips):
                slot = g_ref.at[j, c]
                first.append(remote(6 * a + j, src_ref.at[c], slot, (cx, cy, c)))
                landed.append(remote(6 * a + j, slot, slot, sibling))
                passed.append(remote(6 * a + 3 + j, slot, slot, sibling))
        for j, (cx, cy) in enumerate(chips):
            first.append(remote(12 + j, sm_ref, gsm_ref.at[j], (cx, cy, c)))
        for cp in first:
            cp.start()
        for arrived, onward in zip(landed, passed):
            arrived.wait_recv()
            onward.start()
        for a, g_ref in enumerate((gwi_ref, gwo_ref)):
            for j in range(3):
                remote(6 * a + 3 + j, g_ref.at[j, 1 - c], g_ref.at[j, 1 - c], sibling).wait_recv()
        for j in range(3):
            remote(12 + j, sm_ref, gsm_ref.at[j], sibling).wait_recv()
        for cp in first + passed:
            cp.wait_send()

    return pl.pallas_call(
        body, name="gather_weights",
        out_shape=(jax.ShapeDtypeStruct((3,) + wi.shape, wi.dtype), jax.ShapeDtypeStruct((3,) + wo.shape, wo.dtype),
                   jax.ShapeDtypeStruct((3,) + small.shape, small.dtype)),
        in_specs=[ANY, ANY, ANY], out_specs=(ANY, ANY, ANY),
        scratch_shapes=[pltpu.SemaphoreType.DMA((15,)), pltpu.SemaphoreType.DMA((15,))],
    )(wi, wo, small)


def _pair_exchange(gw, gb, pack):
    n_big = N_CHIPS + 1

    def body(gw_ref, gb_ref, p_ref, ra_ref, rb_ref, o_ref, send_sems, recv_sems):
        x, y, c = _position()
        sibling = (x, y, 1 - c)

        def remote(k, src, dst, to):
            return pltpu.make_async_remote_copy(src_ref=src, dst_ref=dst, send_sem=send_sems.at[k],
                                                recv_sem=recv_sems.at[k], device_id=to, device_id_type=MESH)

        copies = [remote(N_CHIPS, gb_ref.at[:, 1 - c], rb_ref, sibling)]
        for s, start in enumerate(WIN_START):
            rows = pl.ds(pl.multiple_of(start + WIN_HALF * (1 - c), 2 * SUBLANE), WIN_HALF)
            copies.append(remote(s, gw_ref.at[rows], ra_ref.at[s], sibling))
        for mask in range(1, N_DEV):
            peer = (1 - x if mask & 4 else x, 1 - y if mask & 2 else y, 1 - c if mask & 1 else c)
            copies.append(remote(n_big + mask - 1, p_ref, o_ref.at[mask - 1], peer))
        for cp in copies:
            cp.start()
        for cp in copies:
            cp.wait()

    n_sems = n_big + N_DEV - 1
    return pl.pallas_call(
        body, name="grad_pair_exchange",
        out_shape=(jax.ShapeDtypeStruct((N_CHIPS, WIN_HALF, D_MODEL), gw.dtype),
                   jax.ShapeDtypeStruct((N_CHIPS,) + gb.shape[2:], gb.dtype),
                   jax.ShapeDtypeStruct((N_DEV - 1,) + pack.shape, pack.dtype)),
        in_specs=[ANY, ANY, ANY], out_specs=(ANY, ANY, ANY),
        scratch_shapes=[pltpu.SemaphoreType.DMA((n_sems,)), pltpu.SemaphoreType.DMA((n_sems,))],
    )(gw, gb, pack)


def _chip_exchange(pa, pb):
    def body(pa_ref, pb_ref, ra_ref, rb_ref, send_sems, recv_sems):
        x, y, c = _position()
        chips = [(1 - x, y), (x, 1 - y), (1 - x, 1 - y)]
        copies = []
        for a, (src, dst) in enumerate(((pa_ref, ra_ref), (pb_ref, rb_ref))):
            for j, (cx, cy) in enumerate(chips):
                copies.append(pltpu.make_async_remote_copy(
                    src_ref=src.at[2 * cx + cy], dst_ref=dst.at[j], send_sem=send_sems.at[3 * a + j],
                    recv_sem=recv_sems.at[3 * a + j], device_id=(cx, cy, c), device_id_type=MESH))
        for cp in copies:
            cp.start()
        for cp in copies:
            cp.wait()

    return pl.pallas_call(
        body, name="grad_chip_exchange",
        out_shape=(jax.ShapeDtypeStruct((3,) + pa.shape[1:], pa.dtype),
                   jax.ShapeDtypeStruct((3,) + pb.shape[1:], pb.dtype)),
        in_specs=[ANY, ANY], out_specs=(ANY, ANY),
        scratch_shapes=[pltpu.SemaphoreType.DMA((6,)), pltpu.SemaphoreType.DMA((6,))],
    )(pa, pb)


def _pair_share(ha, hb):
    def body(ha_ref, hb_ref, oa_ref, ob_ref, send_sems, recv_sems):
        x, y, c = _position()
        copies = [pltpu.make_async_remote_copy(
            src_ref=src, dst_ref=dst, send_sem=send_sems.at[k], recv_sem=recv_sems.at[k],
            device_id=(x, y, 1 - c), device_id_type=MESH)
            for k, (src, dst) in enumerate(((ha_ref, oa_ref), (hb_ref, ob_ref)))]
        for cp in copies:
            cp.start()
        for cp in copies:
            cp.wait()

    return pl.pallas_call(
        body, name="grad_pair_share",
        out_shape=(jax.ShapeDtypeStruct(ha.shape, ha.dtype), jax.ShapeDtypeStruct(hb.shape, hb.dtype)),
        in_specs=[ANY, ANY], out_specs=(ANY, ANY),
        scratch_shapes=[pltpu.SemaphoreType.DMA((2,)), pltpu.SemaphoreType.DMA((2,))],
    )(ha, hb)


def _pair_sum(mine, recv, c_idx):
    rows, cols = mine.shape[2:]

    def body(c_ref, a_ref, b_ref, o_ref, send_ref):
        total = a_ref[...] + b_ref[...]
        o_ref[...] = total
        send_ref[...] = total.astype(send_ref.dtype)

    out_spec = pl.BlockSpec((None, rows, cols), lambda s, c_ref: (s, 0, 0))
    return pl.pallas_call(
        body, name="grad_pair_sum",
        grid_spec=pltpu.PrefetchScalarGridSpec(
            num_scalar_prefetch=1, grid=(N_CHIPS,),
            in_specs=[pl.BlockSpec((None, None, rows, cols), lambda s, c_ref: (s, c_ref[0], 0, 0)),
                      pl.BlockSpec((None, rows, cols), lambda s, c_ref: (s, 0, 0))],
            out_specs=(out_spec, out_spec)),
        out_shape=(jax.ShapeDtypeStruct(recv.shape, recv.dtype), jax.ShapeDtypeStruct(recv.shape, WIRE_DTYPE)),
        compiler_params=_params(("parallel",)),
    )(c_idx, mine, recv)


def _window_start(s):
    return jnp.where(s == 0, WIN_START[0], jnp.where(s == 1, WIN_START[1], jnp.where(s == 2, WIN_START[2], WIN_START[3])))


def _pair_sum_windows(gw, recv, c_idx):
    tr = WIN_HALF // 3

    def body(c_ref, a_ref, b_ref, o_ref, send_ref):
        total = a_ref[...] + b_ref[...].astype(F32)
        o_ref[...] = total
        send_ref[...] = total.astype(send_ref.dtype)

    out_spec = pl.BlockSpec((None, tr, D_MODEL), lambda s, i, c_ref: (s, i, 0))
    return pl.pallas_call(
        body, name="grad_pair_sum_windows",
        grid_spec=pltpu.PrefetchScalarGridSpec(
            num_scalar_prefetch=1, grid=(N_CHIPS, WIN_HALF // tr),
            in_specs=[pl.BlockSpec((pl.Element(tr), pl.Element(D_MODEL)),
                                   lambda s, i, c_ref: (pl.multiple_of(
                                       _window_start(s) + WIN_HALF * c_ref[0] + tr * i, SUBLANE), 0)),
                      pl.BlockSpec((None, tr, D_MODEL), lambda s, i, c_ref: (s, i, 0))],
            out_specs=(out_spec, out_spec)),
        out_shape=(jax.ShapeDtypeStruct(recv.shape, F32), jax.ShapeDtypeStruct(recv.shape, WIRE_DTYPE)),
        compiler_params=_params(("parallel", "parallel")),
    )(c_idx, gw, recv)


def _assemble_w(own, others, starts):
    def body(starts_ref, own_ref, oth_ref, o_ref):
        o_ref[...] = jnp.zeros_like(o_ref)
        for k in range(N_CHIPS):
            rows = pl.ds(pl.multiple_of(starts_ref[k], 2 * SUBLANE), WIN_ROWS)
            o_ref[rows, :] = o_ref[rows, :] + (own_ref[...] if k == 0 else oth_ref[k - 1])

    return pl.pallas_call(
        body, name="assemble_w",
        in_specs=[pl.BlockSpec(memory_space=pltpu.SMEM), pl.BlockSpec(memory_space=pltpu.VMEM),
                  pl.BlockSpec(memory_space=pltpu.VMEM)],
        out_specs=pl.BlockSpec(memory_space=pltpu.VMEM),
        out_shape=jax.ShapeDtypeStruct((D_IN_PAD, D_MODEL), own.dtype),
        compiler_params=_params(),
    )(starts, own, others)


def _chip_sum(psum, recv3, chip_idx):
    rows, cols = psum.shape[1:]
    tr = rows // 2

    def body(s_ref, p_ref, r0, r1, r2, o_ref):
        o_ref[...] = ((p_ref[...] + r0[...].astype(F32)) + r1[...].astype(F32)) + r2[...].astype(F32)

    return pl.pallas_call(
        body, name="grad_chip_sum",
        grid_spec=pltpu.PrefetchScalarGridSpec(
            num_scalar_prefetch=1, grid=(2,),
            in_specs=[pl.BlockSpec((None, tr, cols), lambda i, s_ref: (s_ref[0], i, 0))] +
                     [pl.BlockSpec((None, tr, cols), functools.partial(lambda i, s_ref, j: (j, i, 0), j=j))
                      for j in range(3)],
            out_specs=pl.BlockSpec((tr, cols), lambda i, s_ref: (i, 0))),
        out_shape=jax.ShapeDtypeStruct((rows, cols), psum.dtype),
        compiler_params=_params(("parallel",)),
    )(chip_idx, psum, recv3, recv3, recv3)


def _adamw_math(w, g, m, v):
    m = ADAM_B1 * m + (1.0 - ADAM_B1) * g
    v = ADAM_B2 * v + (1.0 - ADAM_B2) * (g * g)
    m_hat = m * (1.0 / (1.0 - ADAM_B1 ** ADAM_STEP))
    v_hat = v * (1.0 / (1.0 - ADAM_B2 ** ADAM_STEP))
    delta = -ADAM_LR * (m_hat / (jnp.sqrt(v_hat) + ADAM_EPS) + ADAM_WD * w)
    return delta, m, v


def _adamw_big(w, g, m, v, tr):
    rows, cols = w.shape
    assert rows % tr == 0 and g.shape[0] >= rows

    def body(w_ref, g_ref, m_ref, v_ref, d_out, m_out, v_out):
        d, m2, v2 = _adamw_math(w_ref[...], g_ref[...], m_ref[...], v_ref[...])
        d_out[...] = d
        m_out[...] = m2
        v_out[...] = v2

    spec = pl.BlockSpec((tr, cols), lambda i: (i, 0))
    sds = jax.ShapeDtypeStruct((rows, cols), F32)
    return pl.pallas_call(
        body, name="adamw_big", grid=(rows // tr,), in_specs=[spec] * 4, out_specs=(spec,) * 3,
        out_shape=(sds,) * 3, compiler_params=_params(("parallel",)),
    )(w, g, m, v)


def _adamw_rows(w3, g, m3, v3):
    rows, _, cols = w3.shape
    tc = 2 * LANE

    def body(w_ref, g_ref, m_ref, v_ref, g_out, d_out, m_out, v_out):
        g = g_ref[...]
        d, m2, v2 = _adamw_math(w_ref[:, 0, :], g, m_ref[:, 0, :], v_ref[:, 0, :])
        g_out[:, 0, :] = g
        d_out[:, 0, :] = d
        m_out[:, 0, :] = m2
        v_out[:, 0, :] = v2

    spec3 = pl.BlockSpec((rows, 1, tc), lambda i: (0, 0, i))
    sds = jax.ShapeDtypeStruct((rows, 1, cols), F32)
    return pl.pallas_call(
        body, name="adamw_rows", grid=(cols // tc,),
        in_specs=[spec3, pl.BlockSpec((rows, tc), lambda i: (0, i)), spec3, spec3], out_specs=(spec3,) * 4,
        out_shape=(sds,) * 4, compiler_params=_params(("parallel",)),
    )(w3, g, m3, v3)


def _small_update(own, others, params, ms, vs):
    slots = (SLOT_NORM, SLOT_FINAL, SLOT_ATTN, SLOT_CONVG, SLOT_BF, SLOT_META, SLOT_CONVW)
    n = len(slots)

    def body(*refs):
        own_ref, gp_ref = refs[:2]
        w_refs, m_refs, v_refs = refs[2:2 + n], refs[2 + n:2 + 2 * n], refs[2 + 2 * n:2 + 3 * n]
        outs = refs[2 + 3 * n:3 + 7 * n]
        loss_ref = outs[0]
        g_outs, d_outs, m_outs, v_outs = (outs[1 + k * n:1 + (k + 1) * n] for k in range(4))
        g_scr, w_scr, m_scr, v_scr = refs[3 + 7 * n:]
        x, y, c = _position()
        shard = 2 * x + y
        me = 4 * x + 2 * y + c
        tot = None
        for d in range(N_DEV):
            rel = jnp.bitwise_xor(me, d)
            term = jnp.where(rel == 0, own_ref[...], gp_ref[jnp.maximum(rel, 1) - 1])
            tot = term if tot is None else tot + term
        r0, r1, _, _ = SLOT_META
        meta_sel = tot[r0:r1, 0:256]
        cw_sel = tot[24:32, 0:128]
        for k in range(1, N_CHIPS):
            meta_sel = jnp.where(shard == k, tot[r0:r1, 256 * k:256 * (k + 1)], meta_sel)
            cw_sel = jnp.where(shard == k, tot[24:32, 128 * k:128 * (k + 1)], cw_sel)
        zeros = jnp.zeros((PACK_ROWS, D_MODEL), F32)
        for scr in (g_scr, w_scr, m_scr, v_scr):
            scr[...] = zeros
        g_scr[0:8, :] = tot[0:8, :]
        g_scr[r0:r1, 0:256] = meta_sel
        g_scr[24:32, 0:128] = cw_sel
        for (a, b, c0, c1), w_ref, m_ref, v_ref in zip(slots, w_refs, m_refs, v_refs):
            w_scr[a:b, c0:c1] = w_ref[...]
            m_scr[a:b, c0:c1] = m_ref[...]
            v_scr[a:b, c0:c1] = v_ref[...]
        loss_ref[...] = g_scr[LOSS_ROW:LOSS_ROW + 1, 0:1]
        d, m2, v2 = _adamw_math(w_scr[...], g_scr[...], m_scr[...], v_scr[...])
        w_scr[...] = d
        m_scr[...] = m2
        v_scr[...] = v2
        for (a, b, c0, c1), g_o, d_o, m_o, v_o in zip(slots, g_outs, d_outs, m_outs, v_outs):
            g_o[...] = g_scr[a:b, c0:c1]
            d_o[...] = w_scr[a:b, c0:c1]
            m_o[...] = m_scr[a:b, c0:c1]
            v_o[...] = v_scr[a:b, c0:c1]

    shapes = [jax.ShapeDtypeStruct(p.shape, F32) for p in params]
    out = pl.pallas_call(
        body, name="small_update",
        out_shape=[jax.ShapeDtypeStruct((1, 1), F32)] + shapes * 4,
        scratch_shapes=[pltpu.VMEM((PACK_ROWS, D_MODEL), F32)] * 4,
        compiler_params=_params(),
    )(own, others, *params, *ms, *vs)
    return out[0], out[1:1 + n], out[1 + n:1 + 2 * n], out[1 + 2 * n:1 + 3 * n], out[1 + 3 * n:1 + 4 * n]


def _in_proj(x2, meta_blk, norm_g, w_pad, bf_pad):
    seq = x2.shape[0]
    lp = seq + FRONT
    t = ROW_TILE
    nt = lp // t
    n_sub = t // LANE

    def body(*refs):
        x_refs = refs[:n_sub]
        mb, g_ref, w_ref, bf_ref, tri_ref = refs[n_sub:n_sub + 5]
        q_ref, k_ref, v_ref, rest_ref, fl_ref, ct_ref, u_ref, carry = refs[n_sub + 5:]
        i = pl.program_id(0)

        @pl.when(i == 0)
        def _():
            carry[...] = jnp.zeros_like(carry)

        first = jnp.where(i == 0, mb[...], x_refs[0][...])
        h = jnp.concatenate([first] + [r[...] for r in x_refs[1:]], axis=0)
        ms = jnp.mean(h * h, axis=-1, keepdims=True)
        u = ((h * lax.rsqrt(ms + EPS)) * g_ref[...]).astype(MXU_DTYPE)
        u_ref[...] = u

        def seg(a, width):
            return _dot_nt(u, w_ref[a:a + width, :])

        q_ref[...] = (seg(SEG_Q, D_ATTN) * (HEAD_DIM ** -0.5)).astype(MXU_DTYPE)
        k_ref[...] = seg(SEG_K, D_ATTN).astype(MXU_DTYPE)
        v_ref[...] = seg(SEG_V, D_ATTN).astype(MXU_DTYPE)
        for s in range(5):
            rest_ref[:, 512 * s:512 * (s + 1)] = seg(SEG_ZA + 512 * s, 512)
        fl = seg(SEG_F, LANE)
        fl_ref[...] = fl
        z = fl + bf_ref[...]
        logf = jnp.minimum(z, 0.0) - jnp.log(1.0 + jnp.exp(-jnp.abs(z)))
        row = i * t + lax.broadcasted_iota(jnp.int32, (t, LANE), 0)
        logf = jnp.where(row >= PAD_ROWS, logf, 0.0)
        cs = _dot_exact(tri_ref[...], logf) + carry[...]
        carry[...] = carry[...] + jnp.sum(logf, axis=0, keepdims=True)
        col = i * t + lax.broadcasted_iota(jnp.int32, (SUBLANE, t), 1)
        ct_ref[...] = jnp.where(col >= PAD_ROWS, cs.T[0:SUBLANE, :], -NEG)

    row_blk = lambda cols: pl.BlockSpec((t, cols), lambda i: (i, 0))
    const = lambda shape: pl.BlockSpec(shape, lambda i: (0, 0))
    return pl.pallas_call(
        body, name="in_proj", grid=(nt,),
        in_specs=_x_block_specs(n_sub, LANE) + [const((LANE, D_MODEL)), const((1, D_MODEL)),
                                                pl.BlockSpec((D_IN_PAD, D_MODEL), lambda i: (0, 0),
                                                             pipeline_mode=pl.Buffered(1)),
                                                const((1, LANE)), const((t, t))],
        out_specs=(row_blk(D_ATTN), row_blk(D_ATTN), row_blk(D_ATTN), row_blk(5 * 512), row_blk(LANE),
                   pl.BlockSpec((SUBLANE, t), lambda i: (0, i)), row_blk(D_MODEL)),
        out_shape=(jax.ShapeDtypeStruct((lp, D_ATTN), MXU_DTYPE), jax.ShapeDtypeStruct((lp, D_ATTN), MXU_DTYPE),
                   jax.ShapeDtypeStruct((lp, D_ATTN), MXU_DTYPE), jax.ShapeDtypeStruct((lp, 5 * 512), F32),
                   jax.ShapeDtypeStruct((lp, LANE), F32),
                   jax.ShapeDtypeStruct((SUBLANE, lp), F32), jax.ShapeDtypeStruct((lp, D_MODEL), MXU_DTYPE)),
        scratch_shapes=[pltpu.VMEM((1, LANE), F32)],
        compiler_params=_params(("arbitrary",)),
    )(*([x2] * n_sub), meta_blk, norm_g, w_pad, bf_pad, _triangle(t, lower=True))


def _head_masks():
    lane = lax.broadcasted_iota(jnp.int32, (1, LANE), 1)
    return lane < HEAD_DIM, lane >= HEAD_DIM


def _pair_specs(lp, nt, t):
    blk = pl.BlockSpec((lp, LANE), lambda g: (0, g))
    ct_a = pl.BlockSpec((None, nt, 1, t), lambda g: (2 * g, 0, 0, 0))
    ct_b = pl.BlockSpec((None, nt, 1, t), lambda g: (2 * g + 1, 0, 0, 0))
    return blk, ct_a, ct_b


def _sub_rows(s, col):
    return jnp.concatenate([s[:, a * LANE:(a + 1) * LANE] - col for a in range(s.shape[1] // LANE)], axis=1)


def _loop_unrolled(lo, hi, step, init, n):
    def group(jj, carry):
        for k in range(n):
            carry = step(lo + n * jj + k, carry)
        return carry

    groups = (hi - lo) // n
    carry = lax.fori_loop(0, groups, group, init)
    return lax.fori_loop(lo + n * groups, hi, step, carry)


def _lane_chunks(s):
    return [s[:, a * LANE:(a + 1) * LANE] for a in range(s.shape[1] // LANE)]


def _attn_fwd(q, k, v, ct4):
    lp = q.shape[0]
    t = ROW_TILE
    nt = lp // t

    def body(q_ref, k_ref, v_ref, cta_ref, ctb_ref, o_ref, l_ref, m_ref, s_scr, m_scr, acc_scr):
        masks = _head_masks()
        ct_refs = (cta_ref, ctb_ref)
        below = lax.broadcasted_iota(jnp.int32, (t, t), 1) <= lax.broadcasted_iota(jnp.int32, (t, t), 0)
        lane = lax.broadcasted_iota(jnp.int32, (1, LANE), 1)
        head_of_row = lax.broadcasted_iota(jnp.int32, (2 * t, LANE), 0) >= t
        ones_cols = jnp.where(lax.broadcasted_iota(jnp.int32, (2 * t, LANE), 1) == head_of_row.astype(jnp.int32),
                              1.0, 0.0).astype(MXU_DTYPE)

        def q_block(i, _):
            r0 = pl.multiple_of(i * t, t)
            qi = q_ref[pl.ds(r0, t), :]

            def scores(j):
                kj = k_ref[pl.ds(pl.multiple_of(j * t, t), t), :]
                return _dot_nt(qi, jnp.concatenate([jnp.where(hm, kj, 0).astype(MXU_DTYPE) for hm in masks], axis=0))

            def biased(j, hh, s2, diagonal):
                s = (s2[:, hh * t:(hh + 1) * t] - ct_refs[hh][j]) * LOG2E
                return jnp.where(below, s, NEG) if diagonal else s

            def max_step(j, carry, diagonal):
                s2 = scores(j)
                for hh in range(2):
                    s = biased(j, hh, s2, diagonal)
                    s_scr[j, :, hh * t:(hh + 1) * t] = s
                    m = m_scr[hh]
                    for c in _lane_chunks(s):
                        m = jnp.maximum(m, c)
                    m_scr[hh] = m
                return carry

            m_scr[...] = jnp.full((2, t, LANE), NEG, F32)
            _loop_unrolled(0, i, functools.partial(max_step, diagonal=False), 0, ATTN_UNROLL)
            max_step(i, 0, True)
            ms = [jnp.max(m_scr[hh], axis=-1, keepdims=True) for hh in range(2)]

            def sum_step(j, carry):
                vj = v_ref[pl.ds(pl.multiple_of(j * t, t), t), :]
                v2 = jnp.concatenate([jnp.where(hm, vj, 0).astype(MXU_DTYPE) for hm in masks], axis=0)
                parts = [jnp.exp2(s_scr[j, :, hh * t:(hh + 1) * t] - ms[hh]).astype(MXU_DTYPE) for hh in range(2)]
                acc_scr[...] = acc_scr[...] + _dot(jnp.concatenate(parts, axis=1), jnp.concatenate([v2, ones_cols], axis=1))
                return carry

            acc_scr[...] = jnp.zeros((t, 2 * LANE), F32)
            _loop_unrolled(0, i + 1, sum_step, 0, ATTN_UNROLL)
            acc = acc_scr[...]
            sums = acc[:, LANE:]
            l_pair = jnp.where(masks[0], jnp.sum(jnp.where(lane == 0, sums, 0.0), axis=-1, keepdims=True),
                               jnp.sum(jnp.where(lane == 1, sums, 0.0), axis=-1, keepdims=True))
            o_ref[pl.ds(r0, t), :] = acc[:, :LANE] / l_pair
            l_ref[pl.ds(r0, t), :] = l_pair
            m_ref[pl.ds(r0, t), 0:LANE] = jnp.broadcast_to(ms[0], (t, LANE))
            m_ref[pl.ds(r0, t), LANE:2 * LANE] = jnp.broadcast_to(ms[1], (t, LANE))
            return 0

        lax.fori_loop(0, nt, q_block, 0)

    blk, ct_a, ct_b = _pair_specs(lp, nt, t)
    return pl.pallas_call(
        body, name="attn_fwd", grid=(HEADS // 2,),
        in_specs=[blk, blk, blk, ct_a, ct_b], out_specs=(blk, blk, pl.BlockSpec((lp, 2 * LANE), lambda g: (0, g))),
        out_shape=(jax.ShapeDtypeStruct((lp, D_ATTN), F32), jax.ShapeDtypeStruct((lp, D_ATTN), F32),
                   jax.ShapeDtypeStruct((lp, HEADS * LANE), F32)),
        scratch_shapes=[pltpu.VMEM((nt, t, 2 * t), F32), pltpu.VMEM((2, t, LANE), F32), pltpu.VMEM((t, 2 * LANE), F32)],
        compiler_params=_params(("parallel",)),
    )(q, k, v, ct4, ct4)


def _attn_bwd(q, k, v, do, m, delta, ct4):
    lp = q.shape[0]
    t = ROW_TILE
    nt = lp // t

    def body(q_ref, k_ref, v_ref, do_ref, ma_ref, mb_ref, dla_ref, dlb_ref, cta_ref, ctb_ref,
             dq_ref, dk_ref, dv_ref, dc_ref, dq_acc, dk_acc, dv_acc):
        masks = _head_masks()
        ct_refs, m_refs, dl_refs = (cta_ref, ctb_ref), (ma_ref, mb_ref), (dla_ref, dlb_ref)
        below = lax.broadcasted_iota(jnp.int32, (t, t), 1) <= lax.broadcasted_iota(jnp.int32, (t, t), 0)
        tn = (((0,), (0,)), ((), ()))
        dq_acc[...] = jnp.zeros_like(dq_acc)

        def k_block(j, _):
            c0 = pl.multiple_of(j * t, t)
            kj = k_ref[pl.ds(c0, t), :]
            vj = v_ref[pl.ds(c0, t), :]
            k2 = jnp.concatenate([jnp.where(hm, kj, 0).astype(MXU_DTYPE) for hm in masks], axis=0)
            v2 = jnp.concatenate([jnp.where(hm, vj, 0).astype(MXU_DTYPE) for hm in masks], axis=0)
            ck = [r[j] for r in ct_refs]
            dk_acc[...] = jnp.zeros_like(dk_acc)
            dv_acc[...] = jnp.zeros_like(dv_acc)

            def q_block(i, colsums, diagonal, rows=t):
                r0 = pl.multiple_of(i * t, t)
                qi = q_ref[pl.ds(r0, rows), :]
                doi = do_ref[pl.ds(r0, rows), :]
                q2 = jnp.concatenate([jnp.where(hm, qi, 0).astype(MXU_DTYPE) for hm in masks], axis=0)
                do2 = jnp.concatenate([jnp.where(hm, doi, 0).astype(MXU_DTYPE) for hm in masks], axis=0)
                s2 = _dot_nt(qi, k2)
                dp2 = _dot_nt(doi, v2)
                out, ps, dss = [], [], []
                for hh in range(2):
                    s = (s2[:, hh * t:(hh + 1) * t] - ck[hh]) * LOG2E
                    if diagonal:
                        s = jnp.where(below, s, NEG)
                    p = jnp.exp2(_sub_rows(s, m_refs[hh][pl.ds(r0, rows), :])).astype(MXU_DTYPE)
                    ds32 = p.astype(F32) * _sub_rows(dp2[:, hh * t:(hh + 1) * t], dl_refs[hh][pl.ds(r0, rows), :])
                    ps.append(p)
                    dss.append(ds32.astype(MXU_DTYPE))
                    out.append(colsums[hh] + jnp.sum(ds32, axis=0, keepdims=True))
                dv_acc[...] = dv_acc[...] + lax.dot_general(jnp.concatenate(ps, axis=0), do2, tn,
                                                            preferred_element_type=F32)
                dk_acc[...] = dk_acc[...] + lax.dot_general(jnp.concatenate(dss, axis=0), q2, tn,
                                                            preferred_element_type=F32)
                dq_acc[pl.ds(r0, rows), :] = dq_acc[pl.ds(r0, rows), :] + _dot(jnp.concatenate(dss, axis=1), k2)
                return tuple(out)

            colsums = q_block(j, (jnp.zeros((1, t), F32), jnp.zeros((1, t), F32)), True)
            nq = ATTN_BWD_QBLOCKS
            groups = (nt - 1 - j) // nq
            colsums = lax.fori_loop(0, groups, lambda p, c: q_block(j + 1 + nq * p, c, False, nq * t), colsums)
            colsums = lax.fori_loop(j + 1 + nq * groups, nt, functools.partial(q_block, diagonal=False), colsums)
            for hh in range(2):
                dc_ref[hh, j] = -colsums[hh]
            dk_ref[pl.ds(c0, t), :] = dk_acc[...].astype(dk_ref.dtype)
            dv_ref[pl.ds(c0, t), :] = dv_acc[...].astype(dv_ref.dtype)
            return 0

        lax.fori_loop(0, nt, k_block, 0)
        dq_ref[...] = (dq_acc[...] * (HEAD_DIM ** -0.5)).astype(dq_ref.dtype)

    blk, ct_a, ct_b = _pair_specs(lp, nt, t)
    rep_a = pl.BlockSpec((lp, LANE), lambda g: (0, 2 * g))
    rep_b = pl.BlockSpec((lp, LANE), lambda g: (0, 2 * g + 1))
    return pl.pallas_call(
        body, name="attn_bwd", grid=(HEADS // 2,),
        in_specs=[blk] * 4 + [rep_a, rep_b, rep_a, rep_b, ct_a, ct_b],
        out_specs=(blk, blk, blk, pl.BlockSpec((2, nt, 1, t), lambda g: (g, 0, 0, 0))),
        out_shape=(jax.ShapeDtypeStruct((lp, D_ATTN), MXU_DTYPE),) * 3
                  + (jax.ShapeDtypeStruct((HEADS, nt, 1, t), F32),),
        scratch_shapes=[pltpu.VMEM((lp, LANE), F32), pltpu.VMEM((t, LANE), F32), pltpu.VMEM((t, LANE), F32)],
        compiler_params=_params(("parallel",)),
    )(q, k, v, do, m, m, delta, delta, ct4, ct4)


def _shift_down(prev8, cur, k):
    ext = jnp.concatenate([prev8, cur], axis=0)
    return pltpu.roll(ext, k, 0)[SUBLANE:, :]


def _shift_up(cur, next8, k):
    ext = jnp.concatenate([cur, next8], axis=0)
    n = ext.shape[0]
    return pltpu.roll(ext, n - k, 0)[:cur.shape[0], :]


def _post(o, l_sum, rest, x2, meta_blk, tgt2, w_out, attn_g, conv_g, final_g, conv_w8):
    lp = o.shape[0]
    t = ROW_TILE
    nt = lp // t
    n_sub = t // LANE
    hb = t // SUBLANE

    def body(*refs):
        o_ref, l_ref, za_ref, gb_ref, gc_ref, xc_ref, zc_ref, gch_ref, xch_ref = refs[:9]
        x_refs = refs[9:9 + n_sub]
        mb = refs[9 + n_sub]
        t_refs = refs[10 + n_sub:10 + 2 * n_sub]
        wo_ref, ag_ref, cg_ref, fg_ref, cw_ref, gm_ref, hr_ref = refs[10 + 2 * n_sub:17 + 2 * n_sub]
        (dout_ref, do_ref, dl_ref, dza_ref, dgb_ref, dzc_ref, dcv_ref,
         loss_ref, gf_ref, gag_ref, gcg_ref, gwo_ref) = refs[17 + 2 * n_sub:]
        i = pl.program_id(0)

        @pl.when(i == 0)
        def _():
            for r in (loss_ref, gf_ref, gag_ref, gcg_ref, gwo_ref):
                r[...] = jnp.zeros_like(r)

        gmat = gm_ref[...]
        inv_g = 1.0 / HEAD_DIM
        o_v = o_ref[...]
        ra = lax.rsqrt(_group_sum(o_v * o_v, gmat, STAT_TERMS) * inv_g + EPS)
        n_a = o_v * ra
        a_n = n_a * ag_ref[...]
        za = za_ref[...]
        sig_a = _sigmoid(za)
        sz_a = za * sig_a
        y_a = a_n * sz_a
        gb = gb_ref[...]
        gc = gc_ref[...]
        xc = xc_ref[...]
        cx = gc * xc
        cx_prev = jnp.where(i == 0, 0.0, gch_ref[...] * xch_ref[...])
        conv = (cw_ref[0:1, :] * _shift_down(cx_prev, cx, 2) + cw_ref[1:2, :] * _shift_down(cx_prev, cx, 1)
                + cw_ref[2:3, :] * cx)
        e = gb * conv
        re = lax.rsqrt(_group_sum(e * e, gmat, STAT_TERMS) * inv_g + EPS)
        n_e = e * re
        e_n = n_e * cg_ref[...]
        zc = zc_ref[...]
        sig_c = _sigmoid(zc)
        sz_c = zc * sig_c
        y_c = e_n * sz_c
        mix = jnp.concatenate([y_a, y_c], axis=-1)
        mix_b = mix.astype(MXU_DTYPE)
        first = jnp.where(i == 0, mb[...], x_refs[0][...])
        h = jnp.concatenate([first] + [r[...] for r in x_refs[1:]], axis=0)
        out = h + _dot(mix_b, wo_ref[...])
        r2 = lax.rsqrt(jnp.mean(out * out, axis=-1, keepdims=True) + EPS)
        n_f = out * r2
        y = n_f * fg_ref[...]
        tgt = jnp.concatenate([r[...] for r in t_refs], axis=0)
        valid = (i * t + lax.broadcasted_iota(jnp.int32, (t, 1), 0)) >= FRONT
        diff = jnp.where(valid, y - tgt, 0.0)
        loss_ref[...] = loss_ref[...] + 0.5 * jnp.sum(jnp.sum(diff * diff, axis=-1, keepdims=True) * (1.0 / D_MODEL))
        dy = diff * (1.0 / D_MODEL)
        gf_ref[...] = gf_ref[...] + jnp.sum(dy * n_f, axis=0, keepdims=True)
        dn = dy * fg_ref[...]
        d_out = r2 * (dn - n_f * jnp.mean(dn * n_f, axis=-1, keepdims=True))
        dout_ref[...] = d_out
        d_out_b = d_out.astype(MXU_DTYPE)
        d_mix = _dot_nt(d_out_b, wo_ref[...])
        gwo_ref[...] = gwo_ref[...] + _dot(mix.T.astype(MXU_DTYPE), d_out_b)
        d_ya = d_mix[:, :D_ATTN]
        d_yc = d_mix[:, D_ATTN:]
        d_an = d_ya * sz_a
        dza_ref[...] = (d_ya * a_n * (sig_a * (1.0 + za * (1.0 - sig_a)))).astype(dza_ref.dtype)
        gag_ref[...] = gag_ref[...] + jnp.sum(d_an * n_a, axis=0, keepdims=True)
        dn_a = d_an * ag_ref[...]
        d_o = ra * (dn_a - n_a * (_group_sum(dn_a * n_a, gmat, STAT_TERMS) * inv_g))
        d_o_b = (d_o / l_ref[...]).astype(do_ref.dtype)
        do_ref[...] = d_o_b
        dl_ref[...] = _group_sum(d_o_b.astype(F32) * o_v, hr_ref[...])
        d_en = d_yc * sz_c
        dzc_ref[...] = (d_yc * e_n * (sig_c * (1.0 + zc * (1.0 - sig_c)))).astype(dzc_ref.dtype)
        gcg_ref[...] = gcg_ref[...] + jnp.sum(d_en * n_e, axis=0, keepdims=True)
        dn_e = d_en * cg_ref[...]
        d_e = re * (dn_e - n_e * (_group_sum(dn_e * n_e, gmat, STAT_TERMS) * inv_g))
        dgb_ref[...] = (d_e * conv).astype(dgb_ref.dtype)
        dcv_ref[...] = d_e * gb

    head_rep = jnp.where((lax.broadcasted_iota(jnp.int32, (D_ATTN, HEADS * LANE), 0) >> 6)
                         == (lax.broadcasted_iota(jnp.int32, (D_ATTN, HEADS * LANE), 1) >> 7), 1.0, 0.0).astype(MXU_DTYPE)
    row_blk = lambda cols: pl.BlockSpec((t, cols), lambda i: (i, 0))
    rest_blk = lambda s: pl.BlockSpec((t, 512), functools.partial(lambda i, s: (i, s), s=s))
    halo = lambda s: pl.BlockSpec((SUBLANE, 512), functools.partial(lambda i, s: (jnp.maximum(i * hb - 1, 0), s), s=s))
    const = lambda shape: pl.BlockSpec(shape, lambda i: (0, 0))
    acc = lambda shape: pl.BlockSpec(shape, lambda i: (0, 0))
    return pl.pallas_call(
        body, name="post_fwd_bwd", grid=(nt,),
        in_specs=[row_blk(D_ATTN), row_blk(D_ATTN)] + [rest_blk(s) for s in range(5)] + [halo(2), halo(3)]
                 + _x_block_specs(n_sub, LANE) + [const((LANE, D_MODEL))] + _x_block_specs(n_sub, LANE)
                 + [const((D_MODEL, D_MODEL)), const((1, D_ATTN)), const((1, D_CONV)), const((1, D_MODEL)),
                    const((SUBLANE, D_CONV)), const((D_ATTN, D_ATTN)), const((D_ATTN, HEADS * LANE))],
        out_specs=(row_blk(D_MODEL), row_blk(D_ATTN), row_blk(HEADS * LANE), row_blk(D_ATTN), row_blk(D_CONV),
                   row_blk(D_CONV), row_blk(D_CONV),
                   acc((1, LANE)), acc((1, D_MODEL)), acc((1, D_ATTN)), acc((1, D_CONV)), acc((D_MODEL, D_MODEL))),
        out_shape=(jax.ShapeDtypeStruct((lp, D_MODEL), F32), jax.ShapeDtypeStruct((lp, D_ATTN), MXU_DTYPE),
                   jax.ShapeDtypeStruct((lp, HEADS * LANE), F32), jax.ShapeDtypeStruct((lp, D_ATTN), MXU_DTYPE),
                   jax.ShapeDtypeStruct((lp, D_CONV), MXU_DTYPE), jax.ShapeDtypeStruct((lp, D_CONV), MXU_DTYPE),
                   jax.ShapeDtypeStruct((lp, D_CONV), F32),
                   jax.ShapeDtypeStruct((1, LANE), F32), jax.ShapeDtypeStruct((1, D_MODEL), F32),
                   jax.ShapeDtypeStruct((1, D_ATTN), F32), jax.ShapeDtypeStruct((1, D_CONV), F32),
                   jax.ShapeDtypeStruct((D_MODEL, D_MODEL), F32)),
        compiler_params=_params(("arbitrary",)),
    )(o, l_sum, *([rest] * 5), rest, rest, *([x2] * n_sub), meta_blk, *([tgt2] * n_sub),
      w_out, attn_g, conv_g, final_g, conv_w8, _group_matrix(), head_rep)


def _bwd_in(x2, meta_blk, norm_g, w_pad, bf_pad, fl, dc, dq, dk, dv, dza, dgb, dzc, dconv, rest, d_out, conv_w8):
    lp = fl.shape[0]
    t = ROW_TILE
    nt = lp // t
    n_sub = t // LANE
    hb = t // SUBLANE
    rev = lambda i: nt - 1 - i

    def body(*refs):
        x_refs = refs[:n_sub]
        (mb, g_ref, w_ref, bf_ref, fl_ref, dc_ref, dq_ref, dk_ref, dv_ref, dza_ref, dgb_ref, dzc_ref,
         dcv_ref, dcvn_ref, gc_ref, xc_ref, gch_ref, xch_ref, dout_ref, cw_ref, tri_ref) = refs[n_sub:n_sub + 21]
        dp_ref, gx_ref, front_ref, gn_ref, gbf_ref, gcw_ref, carry, dh_scr, gx_sems = refs[n_sub + 21:]
        step = pl.program_id(0)
        i = rev(step)

        @pl.when(step == 0)
        def _():
            for r in (gn_ref, gbf_ref, gcw_ref, carry):
                r[...] = jnp.zeros_like(r)

        dc8 = jnp.concatenate([dc_ref[...], jnp.zeros((LANE - HEADS, t), F32)], axis=0).T
        dlogf = _dot_exact(tri_ref[...], dc8) + carry[...]
        carry[...] = carry[...] + jnp.sum(dc8, axis=0, keepdims=True)
        z = fl_ref[...] + bf_ref[...]
        row = i * t + lax.broadcasted_iota(jnp.int32, (t, LANE), 0)
        d_f = jnp.where(row >= PAD_ROWS, dlogf * (1.0 / (1.0 + jnp.exp(z))), 0.0)
        gbf_ref[...] = gbf_ref[...] + jnp.sum(d_f, axis=0, keepdims=True)
        dcv = dcv_ref[...]
        dcv_next = jnp.where(i == nt - 1, 0.0, dcvn_ref[...])
        d_cx = (cw_ref[2:3, :] * dcv + cw_ref[1:2, :] * _shift_up(dcv, dcv_next, 1)
                + cw_ref[0:1, :] * _shift_up(dcv, dcv_next, 2))
        gc = gc_ref[...]
        xc = xc_ref[...]
        cx = gc * xc
        cx_prev = jnp.where(i == 0, 0.0, gch_ref[...] * xch_ref[...])
        rowi = lax.broadcasted_iota(jnp.int32, (SUBLANE, 1), 0)
        gcw = (jnp.where(rowi == 0, jnp.sum(dcv * _shift_down(cx_prev, cx, 2), axis=0, keepdims=True), 0.0)
               + jnp.where(rowi == 1, jnp.sum(dcv * _shift_down(cx_prev, cx, 1), axis=0, keepdims=True), 0.0)
               + jnp.where(rowi == 2, jnp.sum(dcv * cx, axis=0, keepdims=True), 0.0))
        gcw_ref[...] = gcw_ref[...] + gcw
        dp_ref[:, SEG_Q:SEG_Q + 512] = dq_ref[...]
        dp_ref[:, SEG_K:SEG_K + 512] = dk_ref[...]
        dp_ref[:, SEG_V:SEG_V + 512] = dv_ref[...]
        dp_ref[:, SEG_F:SEG_F + LANE] = d_f.astype(dp_ref.dtype)
        dp_ref[:, SEG_ZA:SEG_ZA + 512] = dza_ref[...]
        dp_ref[:, SEG_GB:SEG_GB + 512] = dgb_ref[...]
        dp_ref[:, SEG_GC:SEG_GC + 512] = (d_cx * xc).astype(dp_ref.dtype)
        dp_ref[:, SEG_XC:SEG_XC + 512] = (d_cx * gc).astype(dp_ref.dtype)
        dp_ref[:, SEG_ZC:SEG_ZC + 512] = dzc_ref[...]
        d_u = _dot(dp_ref[...], w_ref[...])
        first = jnp.where(i == 0, mb[...], x_refs[0][...])
        h = jnp.concatenate([first] + [r[...] for r in x_refs[1:]], axis=0)
        r1 = lax.rsqrt(jnp.mean(h * h, axis=-1, keepdims=True) + EPS)
        n_h = h * r1
        gn_ref[...] = gn_ref[...] + jnp.sum(d_u * n_h, axis=0, keepdims=True)
        dn = d_u * g_ref[...]
        d_h = dout_ref[...] + r1 * (dn - n_h * jnp.mean(dn * n_h, axis=-1, keepdims=True))
        slot = step % 2

        def to_grad_x(slot_, tile):
            return pltpu.make_async_copy(dh_scr.at[slot_], gx_ref.at[pl.ds(pl.multiple_of(tile * t - FRONT, SUBLANE), t)],
                                         gx_sems.at[slot_])

        @pl.when(step >= 2)
        def _():
            to_grad_x(slot, 1).wait()

        dh_scr[slot] = d_h

        @pl.when(i > 0)
        def _():
            to_grad_x(slot, i).start()

        @pl.when(i == 0)
        def _():
            front_ref[...] = d_h[:FRONT]
            rest_rows = pltpu.make_async_copy(dh_scr.at[slot, pl.ds(FRONT, t - FRONT)], gx_ref.at[pl.ds(0, t - FRONT)],
                                              gx_sems.at[slot])
            rest_rows.start()
            rest_rows.wait()
            if nt >= 2:
                to_grad_x(1 - slot, 1).wait()

    def x_specs():
        specs = [pl.BlockSpec((LANE, D_MODEL), lambda s: (jnp.maximum(n_sub * rev(s) - 1, 0), 0))]
        for b in range(1, n_sub):
            specs.append(pl.BlockSpec((LANE, D_MODEL), functools.partial(lambda s, b: (n_sub * rev(s) - 1 + b, 0), b=b)))
        return specs

    row_blk = lambda cols: pl.BlockSpec((t, cols), lambda s: (rev(s), 0))
    rest_blk = lambda k: pl.BlockSpec((t, 512), functools.partial(lambda s, k: (rev(s), k), k=k))
    halo_prev = lambda k: pl.BlockSpec(
        (SUBLANE, 512), functools.partial(lambda s, k: (jnp.maximum(rev(s) * hb - 1, 0), k), k=k))
    halo_next = pl.BlockSpec((SUBLANE, 512), lambda s: (jnp.minimum((rev(s) + 1) * hb, lp // SUBLANE - 1), 0))
    const = lambda shape: pl.BlockSpec(shape, lambda s: (0, 0))
    return pl.pallas_call(
        body, name="bwd_in", grid=(nt,),
        in_specs=x_specs() + [const((LANE, D_MODEL)), const((1, D_MODEL)),
                              pl.BlockSpec((D_IN_PAD, D_MODEL), lambda s: (0, 0), pipeline_mode=pl.Buffered(1)),
                              const((1, LANE)), row_blk(LANE),
                              pl.BlockSpec((HEADS, t), lambda s: (0, rev(s))),
                              row_blk(512), row_blk(512), row_blk(512), row_blk(512), row_blk(512), row_blk(512),
                              row_blk(512), halo_next, rest_blk(2), rest_blk(3), halo_prev(2), halo_prev(3),
                              row_blk(D_MODEL), const((SUBLANE, D_CONV)), const((t, t))],
        out_specs=(row_blk(D_IN_PAD), ANY, const((FRONT, D_MODEL)), const((1, D_MODEL)), const((1, LANE)),
                   const((SUBLANE, D_CONV))),
        out_shape=(jax.ShapeDtypeStruct((lp, D_IN_PAD), MXU_DTYPE), jax.ShapeDtypeStruct((lp - FRONT, D_MODEL), F32),
                   jax.ShapeDtypeStruct((FRONT, D_MODEL), F32),
                   jax.ShapeDtypeStruct((1, D_MODEL), F32), jax.ShapeDtypeStruct((1, LANE), F32),
                   jax.ShapeDtypeStruct((SUBLANE, D_CONV), F32)),
        scratch_shapes=[pltpu.VMEM((1, LANE), F32), pltpu.VMEM((2, t, D_MODEL), F32), pltpu.SemaphoreType.DMA((2,))],
        compiler_params=_params(("arbitrary",)),
    )(*([x2] * n_sub), meta_blk, norm_g, w_pad, bf_pad, fl, dc, dq, dk, dv, dza, dgb, dzc, dconv, dconv,
      rest, rest, rest, rest, d_out, conv_w8, _triangle(t, lower=False))


def _grad_w_in(u, dproj):
    lp = u.shape[0]
    tn = GW_COL_TILE
    tk = tn if lp % tn == 0 else ROW_TILE

    def body(d_ref, u_ref, o_ref, wire_ref):
        k = pl.program_id(1)

        @pl.when(k == 0)
        def _():
            o_ref[...] = jnp.zeros_like(o_ref)

        o_ref[...] = o_ref[...] + lax.dot_general(d_ref[...], u_ref[...], (((0,), (0,)), ((), ())),
                                                  preferred_element_type=F32)

        @pl.when(k == pl.num_programs(1) - 1)
        def _():
            wire_ref[...] = o_ref[...].astype(wire_ref.dtype)

    out_spec = pl.BlockSpec((tn, D_MODEL), lambda n, k: (n, 0))
    return pl.pallas_call(
        body, name="grad_w_in", grid=(D_IN_PAD // tn, lp // tk),
        in_specs=[pl.BlockSpec((tk, tn), lambda n, k: (k, n)), pl.BlockSpec((tk, D_MODEL), lambda n, k: (k, 0))],
        out_specs=(out_spec, out_spec),
        out_shape=(jax.ShapeDtypeStruct((D_IN_PAD, D_MODEL), F32), jax.ShapeDtypeStruct((D_IN_PAD, D_MODEL), WIRE_DTYPE)),
        compiler_params=_params(("parallel", "arbitrary")),
    )(dproj, u)


def _by_chip(own, others, me):
    by_mask = jnp.stack([own, others[1], others[0], others[2]])
    return [lax.dynamic_index_in_dim(by_mask, jnp.bitwise_xor(me, s), 0, keepdims=False) for s in range(N_CHIPS)]


def _both_halves(mine, other, c):
    return jnp.where(c == 0, jnp.concatenate([mine, other], axis=0), jnp.concatenate([other, mine], axis=0))


def _local_step(x2, tgt2, meta_full, norm_g, w_pad, b_f, conv_w_full, attn_g, conv_g, w_out_full, final_g):
    lp = x2.shape[0] + FRONT
    nt = lp // ROW_TILE
    meta_blk = jnp.concatenate([jnp.zeros((PAD_ROWS, D_MODEL), F32), meta_full], axis=0)
    bf_pad = jnp.pad(b_f, ((0, 0), (0, LANE - HEADS)))
    conv_w8 = jnp.pad(conv_w_full, ((0, SUBLANE - conv_w_full.shape[0]), (0, 0)))
    q, k, v, rest, fl, ct, u = _in_proj(x2, meta_blk, norm_g, w_pad, bf_pad)
    ct4 = ct.reshape(SUBLANE, nt, 1, ROW_TILE)
    o, l_sum, m_max = _attn_fwd(q, k, v, ct4)
    (d_out, d_o, delta, dza, dgb, dzc, dconv, loss, g_final, g_attn, g_convg, gw_out) = _post(
        o, l_sum, rest, x2, meta_blk, tgt2, w_out_full, attn_g, conv_g, final_g, conv_w8)
    dq, dk, dv, dc = _attn_bwd(q, k, v, d_o, m_max, delta, ct4)
    dproj, grad_x, d_front, g_norm, g_bf, g_cw = _bwd_in(x2, meta_blk, norm_g, w_pad, bf_pad, fl, dc.reshape(HEADS, lp), dq, dk, dv,
                                             dza, dgb, dzc, dconv, rest, d_out, conv_w8)
    gw_in, gw_in_wire = _grad_w_in(u, dproj)
    return dict(loss=loss, grad_x=grad_x, d_front=d_front, g_norm=g_norm, g_final=g_final, g_attn=g_attn, g_convg=g_convg, g_bf=g_bf,
                g_cw=g_cw, gw_out=gw_out, gw_in=gw_in, gw_in_wire=gw_in_wire)


def kernel(x, meta, norm_g, w_in, b_f, conv_w, attn_norm_g, conv_norm_g, w_out, final_norm_g, loss_target, m_meta, m_norm_g, m_w_in, m_b_f, m_conv_w, m_attn_norm_g, m_conv_norm_g, m_w_out, m_final_norm_g, v_meta, v_norm_g, v_w_in, v_b_f, v_conv_w, v_attn_norm_g, v_conv_norm_g, v_w_out, v_final_norm_g):
    cx_, cy_, cc_ = _position()
    chip = 2 * cx_ + cy_
    shard = w_in.shape[2]
    out_half = w_out.shape[1] // 2
    pick = lambda vals: jnp.where(chip == 0, vals[0], jnp.where(chip == 1, vals[1], jnp.where(chip == 2, vals[2], vals[3])))
    a_off, b_off = pick(A_OFF), pick(B_OFF)
    wt = jnp.transpose(w_in[0]).astype(MXU_DTYPE)
    wi = lax.dynamic_update_slice_in_dim(
        lax.dynamic_update_slice_in_dim(jnp.zeros((WIN_ROWS, D_MODEL), MXU_DTYPE), wt[:PIECE_A], a_off, 0),
        wt[PIECE_A:], b_off, 0)
    wo = w_out[0].astype(MXU_DTYPE)
    small = jnp.concatenate([meta, jnp.pad(conv_w[0], ((0, 8 - conv_w.shape[1]), (0, meta.shape[1] - conv_w.shape[2])))],
                            axis=0)
    gwi, gwo, gsm = _gather_weights(wi.reshape(2, WIN_HALF, D_MODEL), wo.reshape(2, out_half, D_MODEL), small)
    starts = jnp.stack([_window_start(jnp.bitwise_xor(chip, mask)) for mask in (0, 2, 1, 3)]).astype(jnp.int32)
    w_pad = _assemble_w(wi, gwi.reshape(3, WIN_ROWS, D_MODEL), starts)
    w_out_full = jnp.concatenate(_by_chip(wo, gwo.reshape(3, 2 * out_half, D_MODEL), chip), axis=0)
    small_full = jnp.concatenate(_by_chip(small, gsm, chip), axis=1)
    meta_full = small_full[:N_META]
    conv_w_full = jnp.concatenate([small_full[N_META:N_META + 3, 256 * s:256 * s + LANE] for s in range(N_CHIPS)], axis=1)
    final_g2 = final_norm_g.reshape(1, D_MODEL)
    r = _local_step(x[0], loss_target[0], meta_full, norm_g, w_pad, b_f, conv_w_full, attn_norm_g, conv_norm_g,
                    w_out_full, final_g2)
    grad_x = r["grad_x"][None]
    gb = r["gw_out"].reshape(N_CHIPS, 2, out_half, D_MODEL)
    wide = lambda a: jnp.pad(a, ((0, 0), (0, D_MODEL - a.shape[1])))
    pack = jnp.concatenate([
        r["g_norm"], r["g_final"], jnp.concatenate([r["g_attn"], r["g_convg"]], axis=1), wide(r["g_bf"]),
        wide(r["loss"]), jnp.zeros((3, D_MODEL), F32), r["d_front"][PAD_ROWS:], wide(r["g_cw"])], axis=0)
    ra, rb, packs = _pair_exchange(r["gw_in_wire"], gb, pack)
    c_idx = jnp.reshape(cc_, (1,)).astype(jnp.int32)
    chip_idx = jnp.reshape(chip, (1,)).astype(jnp.int32)
    pa, pa_wire = _pair_sum_windows(r["gw_in"], ra, c_idx)
    pb, pb_wire = _pair_sum(gb, rb, c_idx)
    xa, xb = _chip_exchange(pa_wire, pb_wire)
    ha = _chip_sum(pa, xa, chip_idx)
    hb = _chip_sum(pb, xb, chip_idx)
    oa, ob = _pair_share(ha, hb)
    g_window = _both_halves(ha, oa, cc_)
    g_w_in_t = jnp.concatenate([lax.dynamic_slice_in_dim(g_window, a_off, PIECE_A, 0),
                                lax.dynamic_slice_in_dim(g_window, b_off, shard - PIECE_A, 0)], axis=0)
    g_w_out = _both_halves(hb, ob, cc_)
    as_rows = lambda a: jnp.transpose(a, (2, 0, 1))
    g_w_in, d_w_in, nm_w_in, nv_w_in = (jnp.transpose(a, (1, 2, 0)) for a in _adamw_rows(
        as_rows(w_in), g_w_in_t, as_rows(m_w_in), as_rows(v_w_in)))
    d_w_out, nm_w_out, nv_w_out = (a[None] for a in _adamw_big(w_out[0], g_w_out, m_w_out[0], v_w_out[0], LANE))
    params = (norm_g, final_g2, attn_norm_g, conv_norm_g, b_f, meta, conv_w[0])
    ms = (m_norm_g, m_final_norm_g.reshape(1, D_MODEL), m_attn_norm_g, m_conv_norm_g, m_b_f, m_meta, m_conv_w[0])
    vs = (v_norm_g, v_final_norm_g.reshape(1, D_MODEL), v_attn_norm_g, v_conv_norm_g, v_b_f, v_meta, v_conv_w[0])
    loss, g_s, d_s, m_s, v_s = _small_update(pack, packs, params, ms, vs)

    def ordered(small_list, big_in, big_out):
        s_norm, s_final, s_attn, s_convg, s_bf, s_meta, s_cw = small_list
        return (s_meta, s_norm, big_in, s_bf, s_cw[None], s_attn, s_convg, big_out, s_final.reshape(D_MODEL))

    return (loss.reshape(()), grad_x,
            *ordered(g_s, g_w_in, g_w_out[None]), *ordered(d_s, d_w_in, d_w_out),
            *ordered(m_s, nm_w_in, nm_w_out), *ordered(v_s, nv_w_in, nv_w_out))
```

```python
import functools

import jax
import jax.numpy as jnp
from jax import lax
from jax.experimental import pallas as pl
from jax.experimental.pallas import tpu as pltpu

F32 = jnp.float32
MXU_DTYPE = jnp.bfloat16
WIRE_DTYPE = jnp.bfloat16

D_MODEL = 1024
N_META = 16
HEADS = 8
HEAD_DIM = 64
D_ATTN = HEADS * HEAD_DIM
D_CONV = 512
EPS = 1e-6
LANE = 128
SUBLANE = 8
ROW_TILE = 384
ATTN_UNROLL = 3
ATTN_BWD_QBLOCKS = 2
STAT_TERMS = 1
FRONT = LANE
PAD_ROWS = FRONT - N_META
NEG = -1e30
LOG2E = 1.4426950408889634
N_CHIPS = 4
N_DEV = 8
VMEM_LIMIT_BYTES = 60 * 1024 * 1024

SEG_Q, SEG_K, SEG_V, SEG_F, SEG_ZA, SEG_GB, SEG_GC, SEG_XC, SEG_ZC = (
    0, 512, 1024, 1536, 1664, 2176, 2688, 3200, 3712)
D_IN = 4104
D_IN_PAD = 4224
F_END = 1544
GW_COL_TILE = 1408
WIN_ROWS = 1152
WIN_HALF = WIN_ROWS // 2
WIN_START = (0, 1024, 2160, 3072)
PIECE_A = 518
A_OFF = (0, 2, 12, 126)
B_OFF = (518, 640, 530, 644)
ADAM_LR = 0.001
ADAM_B1 = 0.9
ADAM_B2 = 0.999
ADAM_EPS = 1e-08
ADAM_WD = 0.01
ADAM_STEP = 10

MESH = pl.DeviceIdType.MESH
ANY = pl.BlockSpec(memory_space=pl.ANY)

PACK_ROWS = 32
SLOT_NORM = (0, 1, 0, 1024)
SLOT_FINAL = (1, 2, 0, 1024)
SLOT_ATTN = (2, 3, 0, 512)
SLOT_CONVG = (2, 3, 512, 1024)
SLOT_BF = (3, 4, 0, 8)
SLOT_META = (8, 24, 0, 256)
SLOT_CONVW = (24, 27, 0, 128)
LOSS_ROW = 4


def _params(sem=None):
    return pltpu.CompilerParams(dimension_semantics=sem, vmem_limit_bytes=VMEM_LIMIT_BYTES)


def _sigmoid(z):
    return 1.0 / (1.0 + jnp.exp(-z))


def _dot(a, b):
    return jnp.dot(a, b, preferred_element_type=F32)


def _dot_nt(a, b):
    return lax.dot_general(a, b, (((1,), (1,)), ((), ())), preferred_element_type=F32)


def _dot_exact(ones, x):
    ones = ones.astype(MXU_DTYPE)
    total = None
    for _ in range(3):
        term = x.astype(MXU_DTYPE)
        x = x - term.astype(F32)
        total = _dot(ones, term) if total is None else total + _dot(ones, term)
    return total


def _group_matrix():
    r = lax.broadcasted_iota(jnp.int32, (D_ATTN, D_ATTN), 0) >> 6
    c = lax.broadcasted_iota(jnp.int32, (D_ATTN, D_ATTN), 1) >> 6
    return jnp.where(r == c, 1.0, 0.0).astype(MXU_DTYPE)


def _triangle(n, lower):
    r = lax.broadcasted_iota(jnp.int32, (n, n), 0)
    c = lax.broadcasted_iota(jnp.int32, (n, n), 1)
    return jnp.where((r >= c) if lower else (c >= r), 1.0, 0.0).astype(MXU_DTYPE)


def _group_sum(x, gmat, terms=2):
    hi = x.astype(MXU_DTYPE)
    if terms == 1:
        return _dot(hi, gmat)
    lo = (x - hi.astype(F32)).astype(MXU_DTYPE)
    return _dot(hi, gmat) + _dot(lo, gmat)


def _x_block_specs(n_sub, rows):
    specs = [pl.BlockSpec((rows, D_MODEL), lambda i: (jnp.maximum(n_sub * i - 1, 0), 0))]
    for b in range(1, n_sub):
        specs.append(pl.BlockSpec((rows, D_MODEL), functools.partial(lambda i, b: (n_sub * i - 1 + b, 0), b=b)))
    return specs


def _position():
    return lax.axis_index("x"), lax.axis_index("y"), lax.axis_index("c")


def _gather_weights(wi, wo, small):
    def body(wi_ref, wo_ref, sm_ref, gwi_ref, gwo_ref, gsm_ref, send_sems, recv_sems):
        x, y, c = _position()
        sibling = (x, y, 1 - c)
        chips = [(1 - x, y), (x, 1 - y), (1 - x, 1 - y)]

        def remote(k, src, dst, to):
            return pltpu.make_async_remote_copy(src_ref=src, dst_ref=dst, send_sem=send_sems.at[k],
                                                recv_sem=recv_sems.at[k], device_id=to, device_id_type=MESH)

        first, passed, landed = [], [], []
        for a, (src_ref, g_ref) in enumerate(((wi_ref, gwi_ref), (wo_ref, gwo_ref))):
            for j, (cx, cy) in enumerate(chips):
                slot = g_ref.at[j, c]
                first.append(remote(6 * a + j, src_ref.at[c], slot, (cx, cy, c)))
                landed.append(remote(6 * a + j, slot, slot, sibling))
                passed.append(remote(6 * a + 3 + j, slot, slot, sibling))
        for j, (cx, cy) in enumerate(chips):
            first.append(remote(12 + j, sm_ref, gsm_ref.at[j], (cx, cy, c)))
        for cp in first:
            cp.start()
        for arrived, onward in zip(landed, passed):
            arrived.wait_recv()
            onward.start()
        for a, g_ref in enumerate((gwi_ref, gwo_ref)):
            for j in range(3):
                remote(6 * a + 3 + j, g_ref.at[j, 1 - c], g_ref.at[j, 1 - c], sibling).wait_recv()
        for j in range(3):
            remote(12 + j, sm_ref, gsm_ref.at[j], sibling).wait_recv()
        for cp in first + passed:
            cp.wait_send()

    return pl.pallas_call(
        body, name="gather_weights",
        out_shape=(jax.ShapeDtypeStruct((3,) + wi.shape, wi.dtype), jax.ShapeDtypeStruct((3,) + wo.shape, wo.dtype),
                   jax.ShapeDtypeStruct((3,) + small.shape, small.dtype)),
        in_specs=[ANY, ANY, ANY], out_specs=(ANY, ANY, ANY),
        scratch_shapes=[pltpu.SemaphoreType.DMA((15,)), pltpu.SemaphoreType.DMA((15,))],
    )(wi, wo, small)


def _pair_exchange(gw, gb, pack):
    n_big = N_CHIPS + 1

    def body(gw_ref, gb_ref, p_ref, ra_ref, rb_ref, o_ref, send_sems, recv_sems):
        x, y, c = _position()
        sibling = (x, y, 1 - c)

        def remote(k, src, dst, to):
            return pltpu.make_async_remote_copy(src_ref=src, dst_ref=dst, send_sem=send_sems.at[k],
                                                recv_sem=recv_sems.at[k], device_id=to, device_id_type=MESH)

        copies = [remote(N_CHIPS, gb_ref.at[:, 1 - c], rb_ref, sibling)]
        for s, start in enumerate(WIN_START):
            rows = pl.ds(pl.multiple_of(start + WIN_HALF * (1 - c), 2 * SUBLANE), WIN_HALF)
            copies.append(remote(s, gw_ref.at[rows], ra_ref.at[s], sibling))
        for mask in range(1, N_DEV):
            peer = (1 - x if mask & 4 else x, 1 - y if mask & 2 else y, 1 - c if mask & 1 else c)
            copies.append(remote(n_big + mask - 1, p_ref, o_ref.at[mask - 1], peer))
        for cp in copies:
            cp.start()
        for cp in copies:
            cp.wait()

    n_sems = n_big + N_DEV - 1
    return pl.pallas_call(
        body, name="grad_pair_exchange",
        out_shape=(jax.ShapeDtypeStruct((N_CHIPS, WIN_HALF, D_MODEL), gw.dtype),
                   jax.ShapeDtypeStruct((N_CHIPS,) + gb.shape[2:], gb.dtype),
                   jax.ShapeDtypeStruct((N_DEV - 1,) + pack.shape, pack.dtype)),
        in_specs=[ANY, ANY, ANY], out_specs=(ANY, ANY, ANY),
        scratch_shapes=[pltpu.SemaphoreType.DMA((n_sems,)), pltpu.SemaphoreType.DMA((n_sems,))],
    )(gw, gb, pack)


def _chip_exchange(pa, pb):
    def body(pa_ref, pb_ref, ra_ref, rb_ref, send_sems, recv_sems):
        x, y, c = _position()
        chips = [(1 - x, y), (x, 1 - y), (1 - x, 1 - y)]
        copies = []
        for a, (src, dst) in enumerate(((pa_ref, ra_ref), (pb_ref, rb_ref))):
            for j, (cx, cy) in enumerate(chips):
                copies.append(pltpu.make_async_remote_copy(
                    src_ref=src.at[2 * cx + cy], dst_ref=dst.at[j], send_sem=send_sems.at[3 * a + j],
                    recv_sem=recv_sems.at[3 * a + j], device_id=(cx, cy, c), device_id_type=MESH))
        for cp in copies:
            cp.start()
        for cp in copies:
            cp.wait()

    return pl.pallas_call(
        body, name="grad_chip_exchange",
        out_shape=(jax.ShapeDtypeStruct((3,) + pa.shape[1:], pa.dtype),
                   jax.ShapeDtypeStruct((3,) + pb.shape[1:], pb.dtype)),
        in_specs=[ANY, ANY], out_specs=(ANY, ANY),
        scratch_shapes=[pltpu.SemaphoreType.DMA((6,)), pltpu.SemaphoreType.DMA((6,))],
    )(pa, pb)


def _pair_share(ha, hb):
    def body(ha_ref, hb_ref, oa_ref, ob_ref, send_sems, recv_sems):
        x, y, c = _position()
        copies = [pltpu.make_async_remote_copy(
            src_ref=src, dst_ref=dst, send_sem=send_sems.at[k], recv_sem=recv_sems.at[k],
            device_id=(x, y, 1 - c), device_id_type=MESH)
            for k, (src, dst) in enumerate(((ha_ref, oa_ref), (hb_ref, ob_ref)))]
        for cp in copies:
            cp.start()
        for cp in copies:
            cp.wait()

    return pl.pallas_call(
        body, name="grad_pair_share",
        out_shape=(jax.ShapeDtypeStruct(ha.shape, ha.dtype), jax.ShapeDtypeStruct(hb.shape, hb.dtype)),
        in_specs=[ANY, ANY], out_specs=(ANY, ANY),
        scratch_shapes=[pltpu.SemaphoreType.DMA((2,)), pltpu.SemaphoreType.DMA((2,))],
    )(ha, hb)


def _pair_sum(mine, recv, c_idx):
    rows, cols = mine.shape[2:]

    def body(c_ref, a_ref, b_ref, o_ref, send_ref):
        total = a_ref[...] + b_ref[...]
        o_ref[...] = total
        send_ref[...] = total.astype(send_ref.dtype)

    out_spec = pl.BlockSpec((None, rows, cols), lambda s, c_ref: (s, 0, 0))
    return pl.pallas_call(
        body, name="grad_pair_sum",
        grid_spec=pltpu.PrefetchScalarGridSpec(
            num_scalar_prefetch=1, grid=(N_CHIPS,),
            in_specs=[pl.BlockSpec((None, None, rows, cols), lambda s, c_ref: (s, c_ref[0], 0, 0)),
                      pl.BlockSpec((None, rows, cols), lambda s, c_ref: (s, 0, 0))],
            out_specs=(out_spec, out_spec)),
        out_shape=(jax.ShapeDtypeStruct(recv.shape, recv.dtype), jax.ShapeDtypeStruct(recv.shape, WIRE_DTYPE)),
        compiler_params=_params(("parallel",)),
    )(c_idx, mine, recv)


def _window_start(s):
    return jnp.where(s == 0, WIN_START[0], jnp.where(s == 1, WIN_START[1], jnp.where(s == 2, WIN_START[2], WIN_START[3])))


def _pair_sum_windows(gw, recv, c_idx):
    tr = WIN_HALF // 3

    def body(c_ref, a_ref, b_ref, o_ref, send_ref):
        total = a_ref[...] + b_ref[...].astype(F32)
        o_ref[...] = total
        send_ref[...] = total.astype(send_ref.dtype)

    out_spec = pl.BlockSpec((None, tr, D_MODEL), lambda s, i, c_ref: (s, i, 0))
    return pl.pallas_call(
        body, name="grad_pair_sum_windows",
        grid_spec=pltpu.PrefetchScalarGridSpec(
            num_scalar_prefetch=1, grid=(N_CHIPS, WIN_HALF // tr),
            in_specs=[pl.BlockSpec((pl.Element(tr), pl.Element(D_MODEL)),
                                   lambda s, i, c_ref: (pl.multiple_of(
                                       _window_start(s) + WIN_HALF * c_ref[0] + tr * i, SUBLANE), 0)),
                      pl.BlockSpec((None, tr, D_MODEL), lambda s, i, c_ref: (s, i, 0))],
            out_specs=(out_spec, out_spec)),
        out_shape=(jax.ShapeDtypeStruct(recv.shape, F32), jax.ShapeDtypeStruct(recv.shape, WIRE_DTYPE)),
        compiler_params=_params(("parallel", "parallel")),
    )(c_idx, gw, recv)


def _assemble_w(own, others, starts):
    def body(starts_ref, own_ref, oth_ref, o_ref):
        o_ref[...] = jnp.zeros_like(o_ref)
        for k in range(N_CHIPS):
            rows = pl.ds(pl.multiple_of(starts_ref[k], 2 * SUBLANE), WIN_ROWS)
            o_ref[rows, :] = o_ref[rows, :] + (own_ref[...] if k == 0 else oth_ref[k - 1])

    return pl.pallas_call(
        body, name="assemble_w",
        in_specs=[pl.BlockSpec(memory_space=pltpu.SMEM), pl.BlockSpec(memory_space=pltpu.VMEM),
                  pl.BlockSpec(memory_space=pltpu.VMEM)],
        out_specs=pl.BlockSpec(memory_space=pltpu.VMEM),
        out_shape=jax.ShapeDtypeStruct((D_IN_PAD, D_MODEL), own.dtype),
        compiler_params=_params(),
    )(starts, own, others)


def _chip_sum(psum, recv3, chip_idx):
    rows, cols = psum.shape[1:]
    tr = rows // 2

    def body(s_ref, p_ref, r0, r1, r2, o_ref):
        o_ref[...] = ((p_ref[...] + r0[...].astype(F32)) + r1[...].astype(F32)) + r2[...].astype(F32)

    return pl.pallas_call(
        body, name="grad_chip_sum",
        grid_spec=pltpu.PrefetchScalarGridSpec(
            num_scalar_prefetch=1, grid=(2,),
            in_specs=[pl.BlockSpec((None, tr, cols), lambda i, s_ref: (s_ref[0], i, 0))] +
                     [pl.BlockSpec((None, tr, cols), functools.partial(lambda i, s_ref, j: (j, i, 0), j=j))
                      for j in range(3)],
            out_specs=pl.BlockSpec((tr, cols), lambda i, s_ref: (i, 0))),
        out_shape=jax.ShapeDtypeStruct((rows, cols), psum.dtype),
        compiler_params=_params(("parallel",)),
    )(chip_idx, psum, recv3, recv3, recv3)


def _adamw_math(w, g, m, v):
    m = ADAM_B1 * m + (1.0 - ADAM_B1) * g
    v = ADAM_B2 * v + (1.0 - ADAM_B2) * (g * g)
    m_hat = m * (1.0 / (1.0 - ADAM_B1 ** ADAM_STEP))
    v_hat = v * (1.0 / (1.0 - ADAM_B2 ** ADAM_STEP))
    delta = -ADAM_LR * (m_hat / (jnp.sqrt(v_hat) + ADAM_EPS) + ADAM_WD * w)
    return delta, m, v


def _adamw_big(w, g, m, v, tr):
    rows, cols = w.shape
    assert rows % tr == 0 and g.shape[0] >= rows

    def body(w_ref, g_ref, m_ref, v_ref, d_out, m_out, v_out):
        d, m2, v2 = _adamw_math(w_ref[...], g_ref[...], m_ref[...], v_ref[...])
        d_out[...] = d
        m_out[...] = m2
        v_out[...] = v2

    spec = pl.BlockSpec((tr, cols), lambda i: (i, 0))
    sds = jax.ShapeDtypeStruct((rows, cols), F32)
    return pl.pallas_call(
        body, name="adamw_big", grid=(rows // tr,), in_specs=[spec] * 4, out_specs=(spec,) * 3,
        out_shape=(sds,) * 3, compiler_params=_params(("parallel",)),
    )(w, g, m, v)


def _adamw_rows(w3, g, m3, v3):
    rows, _, cols = w3.shape
    tc = 2 * LANE

    def body(w_ref, g_ref, m_ref, v_ref, g_out, d_out, m_out, v_out):
        g = g_ref[...]
        d, m2, v2 = _adamw_math(w_ref[:, 0, :], g, m_ref[:, 0, :], v_ref[:, 0, :])
        g_out[:, 0, :] = g
        d_out[:, 0, :] = d
        m_out[:, 0, :] = m2
        v_out[:, 0, :] = v2

    spec3 = pl.BlockSpec((rows, 1, tc), lambda i: (0, 0, i))
    sds = jax.ShapeDtypeStruct((rows, 1, cols), F32)
    return pl.pallas_call(
        body, name="adamw_rows", grid=(cols // tc,),
        in_specs=[spec3, pl.BlockSpec((rows, tc), lambda i: (0, i)), spec3, spec3], out_specs=(spec3,) * 4,
        out_shape=(sds,) * 4, compiler_params=_params(("parallel",)),
    )(w3, g, m3, v3)


def _small_update(own, others, params, ms, vs):
    slots = (SLOT_NORM, SLOT_FINAL, SLOT_ATTN, SLOT_CONVG, SLOT_BF, SLOT_META, SLOT_CONVW)
    n = len(slots)

    def body(*refs):
        own_ref, gp_ref = refs[:2]
        w_refs, m_refs, v_refs = refs[2:2 + n], refs[2 + n:2 + 2 * n], refs[2 + 2 * n:2 + 3 * n]
        outs = refs[2 + 3 * n:3 + 7 * n]
        loss_ref = outs[0]
        g_outs, d_outs, m_outs, v_outs = (outs[1 + k * n:1 + (k + 1) * n] for k in range(4))
        g_scr, w_scr, m_scr, v_scr = refs[3 + 7 * n:]
        x, y, c = _position()
        shard = 2 * x + y
        me = 4 * x + 2 * y + c
        tot = None
        for d in range(N_DEV):
            rel = jnp.bitwise_xor(me, d)
            term = jnp.where(rel == 0, own_ref[...], gp_ref[jnp.maximum(rel, 1) - 1])
            tot = term if tot is None else tot + term
        r0, r1, _, _ = SLOT_META
        meta_sel = tot[r0:r1, 0:256]
        cw_sel = tot[24:32, 0:128]
        for k in range(1, N_CHIPS):
            meta_sel = jnp.where(shard == k, tot[r0:r1, 256 * k:256 * (k + 1)], meta_sel)
            cw_sel = jnp.where(shard == k, tot[24:32, 128 * k:128 * (k + 1)], cw_sel)
        zeros = jnp.zeros((PACK_ROWS, D_MODEL), F32)
        for scr in (g_scr, w_scr, m_scr, v_scr):
            scr[...] = zeros
        g_scr[0:8, :] = tot[0:8, :]
        g_scr[r0:r1, 0:256] = meta_sel
        g_scr[24:32, 0:128] = cw_sel
        for (a, b, c0, c1), w_ref, m_ref, v_ref in zip(slots, w_refs, m_refs, v_refs):
            w_scr[a:b, c0:c1] = w_ref[...]
            m_scr[a:b, c0:c1] = m_ref[...]
            v_scr[a:b, c0:c1] = v_ref[...]
        loss_ref[...] = g_scr[LOSS_ROW:LOSS_ROW + 1, 0:1]
        d, m2, v2 = _adamw_math(w_scr[...], g_scr[...], m_scr[...], v_scr[...])
        w_scr[...] = d
        m_scr[...] = m2
        v_scr[...] = v2
        for (a, b, c0, c1), g_o, d_o, m_o, v_o in zip(slots, g_outs, d_outs, m_outs, v_outs):
            g_o[...] = g_scr[a:b, c0:c1]
            d_o[...] = w_scr[a:b, c0:c1]
            m_o[...] = m_scr[a:b, c0:c1]
            v_o[...] = v_scr[a:b, c0:c1]

    shapes = [jax.ShapeDtypeStruct(p.shape, F32) for p in params]
    out = pl.pallas_call(
        body, name="small_update",
        out_shape=[jax.ShapeDtypeStruct((1, 1), F32)] + shapes * 4,
        scratch_shapes=[pltpu.VMEM((PACK_ROWS, D_MODEL), F32)] * 4,
        compiler_params=_params(),
    )(own, others, *params, *ms, *vs)
    return out[0], out[1:1 + n], out[1 + n:1 + 2 * n], out[1 + 2 * n:1 + 3 * n], out[1 + 3 * n:1 + 4 * n]


def _in_proj(x2, meta_blk, norm_g, w_pad, bf_pad):
    seq = x2.shape[0]
    lp = seq + FRONT
    t = ROW_TILE
    nt = lp // t
    n_sub = t // LANE

    def body(*refs):
        x_refs = refs[:n_sub]
        mb, g_ref, w_ref, bf_ref, tri_ref = refs[n_sub:n_sub + 5]
        q_ref, k_ref, v_ref, rest_ref, fl_ref, ct_ref, u_ref, carry = refs[n_sub + 5:]
        i = pl.program_id(0)

        @pl.when(i == 0)
        def _():
            carry[...] = jnp.zeros_like(carry)

        first = jnp.where(i == 0, mb[...], x_refs[0][...])
        h = jnp.concatenate([first] + [r[...] for r in x_refs[1:]], axis=0)
        ms = jnp.mean(h * h, axis=-1, keepdims=True)
        u = ((h * lax.rsqrt(ms + EPS)) * g_ref[...]).astype(MXU_DTYPE)
        u_ref[...] = u

        def seg(a, width):
            return _dot_nt(u, w_ref[a:a + width, :])

        q_ref[...] = (seg(SEG_Q, D_ATTN) * (HEAD_DIM ** -0.5)).astype(MXU_DTYPE)
        k_ref[...] = seg(SEG_K, D_ATTN).astype(MXU_DTYPE)
        v_ref[...] = seg(SEG_V, D_ATTN).astype(MXU_DTYPE)
        for s in range(5):
            rest_ref[:, 512 * s:512 * (s + 1)] = seg(SEG_ZA + 512 * s, 512)
        fl = seg(SEG_F, LANE)
        fl_ref[...] = fl
        z = fl + bf_ref[...]
        logf = jnp.minimum(z, 0.0) - jnp.log(1.0 + jnp.exp(-jnp.abs(z)))
        row = i * t + lax.broadcasted_iota(jnp.int32, (t, LANE), 0)
        logf = jnp.where(row >= PAD_ROWS, logf, 0.0)
        cs = _dot_exact(tri_ref[...], logf) + carry[...]
        carry[...] = carry[...] + jnp.sum(logf, axis=0, keepdims=True)
        col = i * t + lax.broadcasted_iota(jnp.int32, (SUBLANE, t), 1)
        ct_ref[...] = jnp.where(col >= PAD_ROWS, cs.T[0:SUBLANE, :], -NEG)

    row_blk = lambda cols: pl.BlockSpec((t, cols), lambda i: (i, 0))
    const = lambda shape: pl.BlockSpec(shape, lambda i: (0, 0))
    return pl.pallas_call(
        body, name="in_proj", grid=(nt,),
        in_specs=_x_block_specs(n_sub, LANE) + [const((LANE, D_MODEL)), const((1, D_MODEL)),
                                                pl.BlockSpec((D_IN_PAD, D_MODEL), lambda i: (0, 0),
                                                             pipeline_mode=pl.Buffered(1)),
                                                const((1, LANE)), const((t, t))],
        out_specs=(row_blk(D_ATTN), row_blk(D_ATTN), row_blk(D_ATTN), row_blk(5 * 512), row_blk(LANE),
                   pl.BlockSpec((SUBLANE, t), lambda i: (0, i)), row_blk(D_MODEL)),
        out_shape=(jax.ShapeDtypeStruct((lp, D_ATTN), MXU_DTYPE), jax.ShapeDtypeStruct((lp, D_ATTN), MXU_DTYPE),
                   jax.ShapeDtypeStruct((lp, D_ATTN), MXU_DTYPE), jax.ShapeDtypeStruct((lp, 5 * 512), F32),
                   jax.ShapeDtypeStruct((lp, LANE), F32),
                   jax.ShapeDtypeStruct((SUBLANE, lp), F32), jax.ShapeDtypeStruct((lp, D_MODEL), MXU_DTYPE)),
        scratch_shapes=[pltpu.VMEM((1, LANE), F32)],
        compiler_params=_params(("arbitrary",)),
    )(*([x2] * n_sub), meta_blk, norm_g, w_pad, bf_pad, _triangle(t, lower=True))


def _head_masks():
    lane = lax.broadcasted_iota(jnp.int32, (1, LANE), 1)
    return lane < HEAD_DIM, lane >= HEAD_DIM


def _pair_specs(lp, nt, t):
    blk = pl.BlockSpec((lp, LANE), lambda g: (0, g))
    ct_a = pl.BlockSpec((None, nt, 1, t), lambda g: (2 * g, 0, 0, 0))
    ct_b = pl.BlockSpec((None, nt, 1, t), lambda g: (2 * g + 1, 0, 0, 0))
    return blk, ct_a, ct_b


def _sub_rows(s, col):
    return jnp.concatenate([s[:, a * LANE:(a + 1) * LANE] - col for a in range(s.shape[1] // LANE)], axis=1)


def _loop_unrolled(lo, hi, step, init, n):
    def group(jj, carry):
        for k in range(n):
            carry = step(lo + n * jj + k, carry)
        return carry

    groups = (hi - lo) // n
    carry = lax.fori_loop(0, groups, group, init)
    return lax.fori_loop(lo + n * groups, hi, step, carry)


def _lane_chunks(s):
    return [s[:, a * LANE:(a + 1) * LANE] for a in range(s.shape[1] // LANE)]


def _attn_fwd(q, k, v, ct4):
    lp = q.shape[0]
    t = ROW_TILE
    nt = lp // t

    def body(q_ref, k_ref, v_ref, cta_ref, ctb_ref, o_ref, l_ref, m_ref, s_scr, last_scr, m_scr, acc_scr):
        masks = _head_masks()
        ct_refs = (cta_ref, ctb_ref)
        below = lax.broadcasted_iota(jnp.int32, (t, t), 1) <= lax.broadcasted_iota(jnp.int32, (t, t), 0)
        lane = lax.broadcasted_iota(jnp.int32, (1, LANE), 1)
        head_of_row = lax.broadcasted_iota(jnp.int32, (2 * t, LANE), 0) >= t
        ones_cols = jnp.where(lax.broadcasted_iota(jnp.int32, (2 * t, LANE), 1) == head_of_row.astype(jnp.int32),
                              1.0, 0.0).astype(MXU_DTYPE)

        def q_rows(i, rows):
            r0 = pl.multiple_of(i * t, t)
            qi = q_ref[pl.ds(r0, rows), :]
            two = rows == 2 * t
            if two:
                on_first_diagonal = jnp.concatenate([below, jnp.ones((t, t), jnp.bool_)], axis=0)

            def scores(j, lhs):
                kj = k_ref[pl.ds(pl.multiple_of(j * t, t), t), :]
                return _dot_nt(lhs, jnp.concatenate([jnp.where(hm, kj, 0).astype(MXU_DTYPE) for hm in masks], axis=0))

            def max_step(j, carry, mask=None):
                s2 = scores(j, qi)
                for hh in range(2):
                    s = (s2[:, hh * t:(hh + 1) * t] - ct_refs[hh][j]) * LOG2E
                    if mask is not None:
                        s = jnp.where(mask, s, NEG)
                    s_scr[j, 0:rows, hh * t:(hh + 1) * t] = s
                    m = m_scr[hh, 0:rows]
                    for c in _lane_chunks(s):
                        m = jnp.maximum(m, c)
                    m_scr[hh, 0:rows] = m
                return carry

            m_scr[...] = jnp.full(m_scr.shape, NEG, F32)
            _loop_unrolled(0, i, max_step, 0, ATTN_UNROLL)
            max_step(i, 0, on_first_diagonal if two else below)
            if two:
                s2 = scores(i + 1, qi[t:])
                for hh in range(2):
                    s = jnp.where(below, (s2[:, hh * t:(hh + 1) * t] - ct_refs[hh][i + 1]) * LOG2E, NEG)
                    last_scr[:, hh * t:(hh + 1) * t] = s
                    m = m_scr[hh, t:rows]
                    for c in _lane_chunks(s):
                        m = jnp.maximum(m, c)
                    m_scr[hh, t:rows] = m
            ms = [jnp.max(m_scr[hh, 0:rows], axis=-1, keepdims=True) for hh in range(2)]

            def probabilities(scores_of, ms_rows):
                return jnp.concatenate([jnp.exp2(scores_of(hh) - ms_rows[hh]).astype(MXU_DTYPE) for hh in range(2)], axis=1)

            def values(j):
                vj = v_ref[pl.ds(pl.multiple_of(j * t, t), t), :]
                v2 = jnp.concatenate([jnp.where(hm, vj, 0).astype(MXU_DTYPE) for hm in masks], axis=0)
                return jnp.concatenate([v2, ones_cols], axis=1)

            def sum_step(j, carry):
                p = probabilities(lambda hh: s_scr[j, 0:rows, hh * t:(hh + 1) * t], ms)
                acc_scr[0:rows] = acc_scr[0:rows] + _dot(p, values(j))
                return carry

            acc_scr[...] = jnp.zeros(acc_scr.shape, F32)
            _loop_unrolled(0, i + 1, sum_step, 0, ATTN_UNROLL)
            if two:
                p = probabilities(lambda hh: last_scr[:, hh * t:(hh + 1) * t], [m[t:] for m in ms])
                acc_scr[t:rows] = acc_scr[t:rows] + _dot(p, values(i + 1))
            acc = acc_scr[0:rows]
            sums = acc[:, LANE:]
            l_pair = jnp.where(masks[0], jnp.sum(jnp.where(lane == 0, sums, 0.0), axis=-1, keepdims=True),
                               jnp.sum(jnp.where(lane == 1, sums, 0.0), axis=-1, keepdims=True))
            o_ref[pl.ds(r0, rows), :] = acc[:, :LANE] / l_pair
            l_ref[pl.ds(r0, rows), :] = l_pair
            m_ref[pl.ds(r0, rows), 0:LANE] = jnp.broadcast_to(ms[0], (rows, LANE))
            m_ref[pl.ds(r0, rows), LANE:2 * LANE] = jnp.broadcast_to(ms[1], (rows, LANE))

        def two_blocks(p, _):
            q_rows(2 * p, 2 * t)
            return 0

        lax.fori_loop(0, nt // 2, two_blocks, 0)
        if nt % 2:
            q_rows(nt - 1, t)

    blk, ct_a, ct_b = _pair_specs(lp, nt, t)
    return pl.pallas_call(
        body, name="attn_fwd", grid=(HEADS // 2,),
        in_specs=[blk, blk, blk, ct_a, ct_b], out_specs=(blk, blk, pl.BlockSpec((lp, 2 * LANE), lambda g: (0, g))),
        out_shape=(jax.ShapeDtypeStruct((lp, D_ATTN), F32), jax.ShapeDtypeStruct((lp, D_ATTN), F32),
                   jax.ShapeDtypeStruct((lp, HEADS * LANE), F32)),
        scratch_shapes=[pltpu.VMEM((nt, 2 * t, 2 * t), F32), pltpu.VMEM((t, 2 * t), F32),
                        pltpu.VMEM((2, 2 * t, LANE), F32), pltpu.VMEM((2 * t, 2 * LANE), F32)],
        compiler_params=_params(("parallel",)),
    )(q, k, v, ct4, ct4)


def _attn_bwd(q, k, v, do, m, delta, ct4):
    lp = q.shape[0]
    t = ROW_TILE
    nt = lp // t

    def body(q_ref, k_ref, v_ref, do_ref, ma_ref, mb_ref, dla_ref, dlb_ref, cta_ref, ctb_ref,
             dq_ref, dk_ref, dv_ref, dc_ref, dq_acc, dk_acc, dv_acc):
        masks = _head_masks()
        ct_refs, m_refs, dl_refs = (cta_ref, ctb_ref), (ma_ref, mb_ref), (dla_ref, dlb_ref)
        below = lax.broadcasted_iota(jnp.int32, (t, t), 1) <= lax.broadcasted_iota(jnp.int32, (t, t), 0)
        tn = (((0,), (0,)), ((), ()))
        dq_acc[...] = jnp.zeros_like(dq_acc)

        def k_block(j, _):
            c0 = pl.multiple_of(j * t, t)
            kj = k_ref[pl.ds(c0, t), :]
            vj = v_ref[pl.ds(c0, t), :]
            k2 = jnp.concatenate([jnp.where(hm, kj, 0).astype(MXU_DTYPE) for hm in masks], axis=0)
            v2 = jnp.concatenate([jnp.where(hm, vj, 0).astype(MXU_DTYPE) for hm in masks], axis=0)
            ck = [r[j] for r in ct_refs]
            dk_acc[...] = jnp.zeros_like(dk_acc)
            dv_acc[...] = jnp.zeros_like(dv_acc)

            def q_block(i, colsums, diagonal, rows=t):
                r0 = pl.multiple_of(i * t, t)
                qi = q_ref[pl.ds(r0, rows), :]
                doi = do_ref[pl.ds(r0, rows), :]
                q2 = jnp.concatenate([jnp.where(hm, qi, 0).astype(MXU_DTYPE) for hm in masks], axis=0)
                do2 = jnp.concatenate([jnp.where(hm, doi, 0).astype(MXU_DTYPE) for hm in masks], axis=0)
                s2 = _dot_nt(qi, k2)
                dp2 = _dot_nt(doi, v2)
                out, ps, dss = [], [], []
                for hh in range(2):
                    s = (s2[:, hh * t:(hh + 1) * t] - ck[hh]) * LOG2E
                    if diagonal:
                        s = jnp.where(below, s, NEG)
                    p = jnp.exp2(_sub_rows(s, m_refs[hh][pl.ds(r0, rows), :])).astype(MXU_DTYPE)
                    ds32 = p.astype(F32) * _sub_rows(dp2[:, hh * t:(hh + 1) * t], dl_refs[hh][pl.ds(r0, rows), :])
                    ps.append(p)
                    dss.append(ds32.astype(MXU_DTYPE))
                    out.append(colsums[hh] + jnp.sum(ds32, axis=0, keepdims=True))
                dv_acc[...] = dv_acc[...] + lax.dot_general(jnp.concatenate(ps, axis=0), do2, tn,
                                                            preferred_element_type=F32)
                dk_acc[...] = dk_acc[...] + lax.dot_general(jnp.concatenate(dss, axis=0), q2, tn,
                                                            preferred_element_type=F32)
                dq_acc[pl.ds(r0, rows), :] = dq_acc[pl.ds(r0, rows), :] + _dot(jnp.concatenate(dss, axis=1), k2)
                return tuple(out)

            colsums = q_block(j, (jnp.zeros((1, t), F32), jnp.zeros((1, t), F32)), True)
            nq = ATTN_BWD_QBLOCKS
            groups = (nt - 1 - j) // nq
            colsums = lax.fori_loop(0, groups, lambda p, c: q_block(j + 1 + nq * p, c, False, nq * t), colsums)
            colsums = lax.fori_loop(j + 1 + nq * groups, nt, functools.partial(q_block, diagonal=False), colsums)
            for hh in range(2):
                dc_ref[hh, j] = -colsums[hh]
            dk_ref[pl.ds(c0, t), :] = dk_acc[...].astype(dk_ref.dtype)
            dv_ref[pl.ds(c0, t), :] = dv_acc[...].astype(dv_ref.dtype)
            return 0

        lax.fori_loop(0, nt, k_block, 0)
        dq_ref[...] = (dq_acc[...] * (HEAD_DIM ** -0.5)).astype(dq_ref.dtype)

    blk, ct_a, ct_b = _pair_specs(lp, nt, t)
    rep_a = pl.BlockSpec((lp, LANE), lambda g: (0, 2 * g))
    rep_b = pl.BlockSpec((lp, LANE), lambda g: (0, 2 * g + 1))
    return pl.pallas_call(
        body, name="attn_bwd", grid=(HEADS // 2,),
        in_specs=[blk] * 4 + [rep_a, rep_b, rep_a, rep_b, ct_a, ct_b],
        out_specs=(blk, blk, blk, pl.BlockSpec((2, nt, 1, t), lambda g: (g, 0, 0, 0))),
        out_shape=(jax.ShapeDtypeStruct((lp, D_ATTN), MXU_DTYPE),) * 3
                  + (jax.ShapeDtypeStruct((HEADS, nt, 1, t), F32),),
        scratch_shapes=[pltpu.VMEM((lp, LANE), F32), pltpu.VMEM((t, LANE), F32), pltpu.VMEM((t, LANE), F32)],
        compiler_params=_params(("parallel",)),
    )(q, k, v, do, m, m, delta, delta, ct4, ct4)


def _shift_down(prev8, cur, k):
    ext = jnp.concatenate([prev8, cur], axis=0)
    return pltpu.roll(ext, k, 0)[SUBLANE:, :]


def _shift_up(cur, next8, k):
    ext = jnp.concatenate([cur, next8], axis=0)
    n = ext.shape[0]
    return pltpu.roll(ext, n - k, 0)[:cur.shape[0], :]


def _post(o, l_sum, rest, x2, meta_blk, tgt2, w_out, attn_g, conv_g, final_g, conv_w8):
    lp = o.shape[0]
    t = ROW_TILE
    nt = lp // t
    n_sub = t // LANE
    hb = t // SUBLANE

    def body(*refs):
        o_ref, l_ref, za_ref, gb_ref, gc_ref, xc_ref, zc_ref, gch_ref, xch_ref = refs[:9]
        x_refs = refs[9:9 + n_sub]
        mb = refs[9 + n_sub]
        t_refs = refs[10 + n_sub:10 + 2 * n_sub]
        wo_ref, ag_ref, cg_ref, fg_ref, cw_ref, gm_ref, hr_ref = refs[10 + 2 * n_sub:17 + 2 * n_sub]
        (dout_ref, do_ref, dl_ref, dza_ref, dgb_ref, dzc_ref, dcv_ref,
         loss_ref, gf_ref, gag_ref, gcg_ref, gwo_ref) = refs[17 + 2 * n_sub:]
        i = pl.program_id(0)

        @pl.when(i == 0)
        def _():
            for r in (loss_ref, gf_ref, gag_ref, gcg_ref, gwo_ref):
                r[...] = jnp.zeros_like(r)

        gmat = gm_ref[...]
        inv_g = 1.0 / HEAD_DIM
        o_v = o_ref[...]
        ra = lax.rsqrt(_group_sum(o_v * o_v, gmat, STAT_TERMS) * inv_g + EPS)
        n_a = o_v * ra
        a_n = n_a * ag_ref[...]
        za = za_ref[...]
        sig_a = _sigmoid(za)
        sz_a = za * sig_a
        y_a = a_n * sz_a
        gb = gb_ref[...]
        gc = gc_ref[...]
        xc = xc_ref[...]
        cx = gc * xc
        cx_prev = jnp.where(i == 0, 0.0, gch_ref[...] * xch_ref[...])
        conv = (cw_ref[0:1, :] * _shift_down(cx_prev, cx, 2) + cw_ref[1:2, :] * _shift_down(cx_prev, cx, 1)
                + cw_ref[2:3, :] * cx)
        e = gb * conv
        re = lax.rsqrt(_group_sum(e * e, gmat, STAT_TERMS) * inv_g + EPS)
        n_e = e * re
        e_n = n_e * cg_ref[...]
        zc = zc_ref[...]
        sig_c = _sigmoid(zc)
        sz_c = zc * sig_c
        y_c = e_n * sz_c
        mix = jnp.concatenate([y_a, y_c], axis=-1)
        mix_b = mix.astype(MXU_DTYPE)
        first = jnp.where(i == 0, mb[...], x_refs[0][...])
        h = jnp.concatenate([first] + [r[...] for r in x_refs[1:]], axis=0)
        out = h + _dot(mix_b, wo_ref[...])
        r2 = lax.rsqrt(jnp.mean(out * out, axis=-1, keepdims=True) + EPS)
        n_f = out * r2
        y = n_f * fg_ref[...]
        tgt = jnp.concatenate([r[...] for r in t_refs], axis=0)
        valid = (i * t + lax.broadcasted_iota(jnp.int32, (t, 1), 0)) >= FRONT
        diff = jnp.where(valid, y - tgt, 0.0)
        loss_ref[...] = loss_ref[...] + 0.5 * jnp.sum(jnp.sum(diff * diff, axis=-1, keepdims=True) * (1.0 / D_MODEL))
        dy = diff * (1.0 / D_MODEL)
        gf_ref[...] = gf_ref[...] + jnp.sum(dy * n_f, axis=0, keepdims=True)
        dn = dy * fg_ref[...]
        d_out = r2 * (dn - n_f * jnp.mean(dn * n_f, axis=-1, keepdims=True))
        dout_ref[...] = d_out
        d_out_b = d_out.astype(MXU_DTYPE)
        d_mix = _dot_nt(d_out_b, wo_ref[...])
        gwo_ref[...] = gwo_ref[...] + _dot(mix.T.astype(MXU_DTYPE), d_out_b)
        d_ya = d_mix[:, :D_ATTN]
        d_yc = d_mix[:, D_ATTN:]
        d_an = d_ya * sz_a
        dza_ref[...] = (d_ya * a_n * (sig_a * (1.0 + za * (1.0 - sig_a)))).astype(dza_ref.dtype)
        gag_ref[...] = gag_ref[...] + jnp.sum(d_an * n_a, axis=0, keepdims=True)
        dn_a = d_an * ag_ref[...]
        d_o = ra * (dn_a - n_a * (_group_sum(dn_a * n_a, gmat, STAT_TERMS) * inv_g))
        d_o_b = (d_o / l_ref[...]).astype(do_ref.dtype)
        do_ref[...] = d_o_b
        dl_ref[...] = _group_sum(d_o_b.astype(F32) * o_v, hr_ref[...])
        d_en = d_yc * sz_c
        dzc_ref[...] = (d_yc * e_n * (sig_c * (1.0 + zc * (1.0 - sig_c)))).astype(dzc_ref.dtype)
        gcg_ref[...] = gcg_ref[...] + jnp.sum(d_en * n_e, axis=0, keepdims=True)
        dn_e = d_en * cg_ref[...]
        d_e = re * (dn_e - n_e * (_group_sum(dn_e * n_e, gmat, STAT_TERMS) * inv_g))
        dgb_ref[...] = (d_e * conv).astype(dgb_ref.dtype)
        dcv_ref[...] = d_e * gb

    head_rep = jnp.where((lax.broadcasted_iota(jnp.int32, (D_ATTN, HEADS * LANE), 0) >> 6)
                         == (lax.broadcasted_iota(jnp.int32, (D_ATTN, HEADS * LANE), 1) >> 7), 1.0, 0.0).astype(MXU_DTYPE)
    row_blk = lambda cols: pl.BlockSpec((t, cols), lambda i: (i, 0))
    rest_blk = lambda s: pl.BlockSpec((t, 512), functools.partial(lambda i, s: (i, s), s=s))
    halo = lambda s: pl.BlockSpec((SUBLANE, 512), functools.partial(lambda i, s: (jnp.maximum(i * hb - 1, 0), s), s=s))
    const = lambda shape: pl.BlockSpec(shape, lambda i: (0, 0))
    acc = lambda shape: pl.BlockSpec(shape, lambda i: (0, 0))
    return pl.pallas_call(
        body, name="post_fwd_bwd", grid=(nt,),
        in_specs=[row_blk(D_ATTN), row_blk(D_ATTN)] + [rest_blk(s) for s in range(5)] + [halo(2), halo(3)]
                 + _x_block_specs(n_sub, LANE) + [const((LANE, D_MODEL))] + _x_block_specs(n_sub, LANE)
                 + [const((D_MODEL, D_MODEL)), const((1, D_ATTN)), const((1, D_CONV)), const((1, D_MODEL)),
                    const((SUBLANE, D_CONV)), const((D_ATTN, D_ATTN)), const((D_ATTN, HEADS * LANE))],
        out_specs=(row_blk(D_MODEL), row_blk(D_ATTN), row_blk(HEADS * LANE), row_blk(D_ATTN), row_blk(D_CONV),
                   row_blk(D_CONV), row_blk(D_CONV),
                   acc((1, LANE)), acc((1, D_MODEL)), acc((1, D_ATTN)), acc((1, D_CONV)), acc((D_MODEL, D_MODEL))),
        out_shape=(jax.ShapeDtypeStruct((lp, D_MODEL), F32), jax.ShapeDtypeStruct((lp, D_ATTN), MXU_DTYPE),
                   jax.ShapeDtypeStruct((lp, HEADS * LANE), F32), jax.ShapeDtypeStruct((lp, D_ATTN), MXU_DTYPE),
                   jax.ShapeDtypeStruct((lp, D_CONV), MXU_DTYPE), jax.ShapeDtypeStruct((lp, D_CONV), MXU_DTYPE),
                   jax.ShapeDtypeStruct((lp, D_CONV), F32),
                   jax.ShapeDtypeStruct((1, LANE), F32), jax.ShapeDtypeStruct((1, D_MODEL), F32),
                   jax.ShapeDtypeStruct((1, D_ATTN), F32), jax.ShapeDtypeStruct((1, D_CONV), F32),
                   jax.ShapeDtypeStruct((D_MODEL, D_MODEL), F32)),
        compiler_params=_params(("arbitrary",)),
    )(o, l_sum, *([rest] * 5), rest, rest, *([x2] * n_sub), meta_blk, *([tgt2] * n_sub),
      w_out, attn_g, conv_g, final_g, conv_w8, _group_matrix(), head_rep)


def _bwd_in(x2, meta_blk, norm_g, w_pad, bf_pad, fl, dc, dq, dk, dv, dza, dgb, dzc, dconv, rest, d_out, conv_w8):
    lp = fl.shape[0]
    t = ROW_TILE
    nt = lp // t
    n_sub = t // LANE
    hb = t // SUBLANE
    rev = lambda i: nt - 1 - i

    def body(*refs):
        x_refs = refs[:n_sub]
        (mb, g_ref, w_ref, bf_ref, fl_ref, dc_ref, dq_ref, dk_ref, dv_ref, dza_ref, dgb_ref, dzc_ref,
         dcv_ref, dcvn_ref, gc_ref, xc_ref, gch_ref, xch_ref, dout_ref, cw_ref, tri_ref) = refs[n_sub:n_sub + 21]
        dp_ref, gx_ref, front_ref, gn_ref, gbf_ref, gcw_ref, carry, dh_scr, gx_sems = refs[n_sub + 21:]
        step = pl.program_id(0)
        i = rev(step)

        @pl.when(step == 0)
        def _():
            for r in (gn_ref, gbf_ref, gcw_ref, carry):
                r[...] = jnp.zeros_like(r)

        dc8 = jnp.concatenate([dc_ref[...], jnp.zeros((LANE - HEADS, t), F32)], axis=0).T
        dlogf = _dot_exact(tri_ref[...], dc8) + carry[...]
        carry[...] = carry[...] + jnp.sum(dc8, axis=0, keepdims=True)
        z = fl_ref[...] + bf_ref[...]
        row = i * t + lax.broadcasted_iota(jnp.int32, (t, LANE), 0)
        d_f = jnp.where(row >= PAD_ROWS, dlogf * (1.0 / (1.0 + jnp.exp(z))), 0.0)
        gbf_ref[...] = gbf_ref[...] + jnp.sum(d_f, axis=0, keepdims=True)
        dcv = dcv_ref[...]
        dcv_next = jnp.where(i == nt - 1, 0.0, dcvn_ref[...])
        d_cx = (cw_ref[2:3, :] * dcv + cw_ref[1:2, :] * _shift_up(dcv, dcv_next, 1)
                + cw_ref[0:1, :] * _shift_up(dcv, dcv_next, 2))
        gc = gc_ref[...]
        xc = xc_ref[...]
        cx = gc * xc
        cx_prev = jnp.where(i == 0, 0.0, gch_ref[...] * xch_ref[...])
        rowi = lax.broadcasted_iota(jnp.int32, (SUBLANE, 1), 0)
        gcw = (jnp.where(rowi == 0, jnp.sum(dcv * _shift_down(cx_prev, cx, 2), axis=0, keepdims=True), 0.0)
               + jnp.where(rowi == 1, jnp.sum(dcv * _shift_down(cx_prev, cx, 1), axis=0, keepdims=True), 0.0)
               + jnp.where(rowi == 2, jnp.sum(dcv * cx, axis=0, keepdims=True), 0.0))
        gcw_ref[...] = gcw_ref[...] + gcw
        dp_ref[:, SEG_Q:SEG_Q + 512] = dq_ref[...]
        dp_ref[:, SEG_K:SEG_K + 512] = dk_ref[...]
        dp_ref[:, SEG_V:SEG_V + 512] = dv_ref[...]
        dp_ref[:, SEG_F:SEG_F + LANE] = d_f.astype(dp_ref.dtype)
        dp_ref[:, SEG_ZA:SEG_ZA + 512] = dza_ref[...]
        dp_ref[:, SEG_GB:SEG_GB + 512] = dgb_ref[...]
        dp_ref[:, SEG_GC:SEG_GC + 512] = (d_cx * xc).astype(dp_ref.dtype)
        dp_ref[:, SEG_XC:SEG_XC + 512] = (d_cx * gc).astype(dp_ref.dtype)
        dp_ref[:, SEG_ZC:SEG_ZC + 512] = dzc_ref[...]
        d_u = _dot(dp_ref[...], w_ref[...])
        first = jnp.where(i == 0, mb[...], x_refs[0][...])
        h = jnp.concatenate([first] + [r[...] for r in x_refs[1:]], axis=0)
        r1 = lax.rsqrt(jnp.mean(h * h, axis=-1, keepdims=True) + EPS)
        n_h = h * r1
        gn_ref[...] = gn_ref[...] + jnp.sum(d_u * n_h, axis=0, keepdims=True)
        dn = d_u * g_ref[...]
        d_h = dout_ref[...] + r1 * (dn - n_h * jnp.mean(dn * n_h, axis=-1, keepdims=True))
        slot = step % 2

        def to_grad_x(slot_, tile):
            return pltpu.make_async_copy(dh_scr.at[slot_], gx_ref.at[pl.ds(pl.multiple_of(tile * t - FRONT, SUBLANE), t)],
                                         gx_sems.at[slot_])

        @pl.when(step >= 2)
        def _():
            to_grad_x(slot, 1).wait()

        dh_scr[slot] = d_h

        @pl.when(i > 0)
        def _():
            to_grad_x(slot, i).start()

        @pl.when(i == 0)
        def _():
            front_ref[...] = d_h[:FRONT]
            rest_rows = pltpu.make_async_copy(dh_scr.at[slot, pl.ds(FRONT, t - FRONT)], gx_ref.at[pl.ds(0, t - FRONT)],
                                              gx_sems.at[slot])
            rest_rows.start()
            rest_rows.wait()
            if nt >= 2:
                to_grad_x(1 - slot, 1).wait()

    def x_specs():
        specs = [pl.BlockSpec((LANE, D_MODEL), lambda s: (jnp.maximum(n_sub * rev(s) - 1, 0), 0))]
        for b in range(1, n_sub):
            specs.append(pl.BlockSpec((LANE, D_MODEL), functools.partial(lambda s, b: (n_sub * rev(s) - 1 + b, 0), b=b)))
        return specs

    row_blk = lambda cols: pl.BlockSpec((t, cols), lambda s: (rev(s), 0))
    rest_blk = lambda k: pl.BlockSpec((t, 512), functools.partial(lambda s, k: (rev(s), k), k=k))
    halo_prev = lambda k: pl.BlockSpec(
        (SUBLANE, 512), functools.partial(lambda s, k: (jnp.maximum(rev(s) * hb - 1, 0), k), k=k))
    halo_next = pl.BlockSpec((SUBLANE, 512), lambda s: (jnp.minimum((rev(s) + 1) * hb, lp // SUBLANE - 1), 0))
    const = lambda shape: pl.BlockSpec(shape, lambda s: (0, 0))
    return pl.pallas_call(
        body, name="bwd_in", grid=(nt,),
        in_specs=x_specs() + [const((LANE, D_MODEL)), const((1, D_MODEL)),
                              pl.BlockSpec((D_IN_PAD, D_MODEL), lambda s: (0, 0), pipeline_mode=pl.Buffered(1)),
                              const((1, LANE)), row_blk(LANE),
                              pl.BlockSpec((HEADS, t), lambda s: (0, rev(s))),
                              row_blk(512), row_blk(512), row_blk(512), row_blk(512), row_blk(512), row_blk(512),
                              row_blk(512), halo_next, rest_blk(2), rest_blk(3), halo_prev(2), halo_prev(3),
                              row_blk(D_MODEL), const((SUBLANE, D_CONV)), const((t, t))],
        out_specs=(row_blk(D_IN_PAD), ANY, const((FRONT, D_MODEL)), const((1, D_MODEL)), const((1, LANE)),
                   const((SUBLANE, D_CONV))),
        out_shape=(jax.ShapeDtypeStruct((lp, D_IN_PAD), MXU_DTYPE), jax.ShapeDtypeStruct((lp - FRONT, D_MODEL), F32),
                   jax.ShapeDtypeStruct((FRONT, D_MODEL), F32),
                   jax.ShapeDtypeStruct((1, D_MODEL), F32), jax.ShapeDtypeStruct((1, LANE), F32),
                   jax.ShapeDtypeStruct((SUBLANE, D_CONV), F32)),
        scratch_shapes=[pltpu.VMEM((1, LANE), F32), pltpu.VMEM((2, t, D_MODEL), F32), pltpu.SemaphoreType.DMA((2,))],
        compiler_params=_params(("arbitrary",)),
    )(*([x2] * n_sub), meta_blk, norm_g, w_pad, bf_pad, fl, dc, dq, dk, dv, dza, dgb, dzc, dconv, dconv,
      rest, rest, rest, rest, d_out, conv_w8, _triangle(t, lower=False))


def _grad_w_in(u, dproj):
    lp = u.shape[0]
    tn = GW_COL_TILE
    tk = tn if lp % tn == 0 else ROW_TILE

    def body(d_ref, u_ref, o_ref, wire_ref):
        k = pl.program_id(1)

        @pl.when(k == 0)
        def _():
            o_ref[...] = jnp.zeros_like(o_ref)

        o_ref[...] = o_ref[...] + lax.dot_general(d_ref[...], u_ref[...], (((0,), (0,)), ((), ())),
                                                  preferred_element_type=F32)

        @pl.when(k == pl.num_programs(1) - 1)
        def _():
            wire_ref[...] = o_ref[...].astype(wire_ref.dtype)

    out_spec = pl.BlockSpec((tn, D_MODEL), lambda n, k: (n, 0))
    return pl.pallas_call(
        body, name="grad_w_in", grid=(D_IN_PAD // tn, lp // tk),
        in_specs=[pl.BlockSpec((tk, tn), lambda n, k: (k, n)), pl.BlockSpec((tk, D_MODEL), lambda n, k: (k, 0))],
        out_specs=(out_spec, out_spec),
        out_shape=(jax.ShapeDtypeStruct((D_IN_PAD, D_MODEL), F32), jax.ShapeDtypeStruct((D_IN_PAD, D_MODEL), WIRE_DTYPE)),
        compiler_params=_params(("parallel", "arbitrary")),
    )(dproj, u)


def _by_chip(own, others, me):
    by_mask = jnp.stack([own, others[1], others[0], others[2]])
    return [lax.dynamic_index_in_dim(by_mask, jnp.bitwise_xor(me, s), 0, keepdims=False) for s in range(N_CHIPS)]


def _both_halves(mine, other, c):
    return jnp.where(c == 0, jnp.concatenate([mine, other], axis=0), jnp.concatenate([other, mine], axis=0))


def _local_step(x2, tgt2, meta_full, norm_g, w_pad, b_f, conv_w_full, attn_g, conv_g, w_out_full, final_g):
    lp = x2.shape[0] + FRONT
    nt = lp // ROW_TILE
    meta_blk = jnp.concatenate([jnp.zeros((PAD_ROWS, D_MODEL), F32), meta_full], axis=0)
    bf_pad = jnp.pad(b_f, ((0, 0), (0, LANE - HEADS)))
    conv_w8 = jnp.pad(conv_w_full, ((0, SUBLANE - conv_w_full.shape[0]), (0, 0)))
    q, k, v, rest, fl, ct, u = _in_proj(x2, meta_blk, norm_g, w_pad, bf_pad)
    ct4 = ct.reshape(SUBLANE, nt, 1, ROW_TILE)
    o, l_sum, m_max = _attn_fwd(q, k, v, ct4)
    (d_out, d_o, delta, dza, dgb, dzc, dconv, loss, g_final, g_attn, g_convg, gw_out) = _post(
        o, l_sum, rest, x2, meta_blk, tgt2, w_out_full, attn_g, conv_g, final_g, conv_w8)
    dq, dk, dv, dc = _attn_bwd(q, k, v, d_o, m_max, delta, ct4)
    dproj, grad_x, d_front, g_norm, g_bf, g_cw = _bwd_in(x2, meta_blk, norm_g, w_pad, bf_pad, fl, dc.reshape(HEADS, lp), dq, dk, dv,
                                             dza, dgb, dzc, dconv, rest, d_out, conv_w8)
    gw_in, gw_in_wire = _grad_w_in(u, dproj)
    return dict(loss=loss, grad_x=grad_x, d_front=d_front, g_norm=g_norm, g_final=g_final, g_attn=g_attn, g_convg=g_convg, g_bf=g_bf,
                g_cw=g_cw, gw_out=gw_out, gw_in=gw_in, gw_in_wire=gw_in_wire)


def kernel(x, meta, norm_g, w_in, b_f, conv_w, attn_norm_g, conv_norm_g, w_out, final_norm_g, loss_target, m_meta, m_norm_g, m_w_in, m_b_f, m_conv_w, m_attn_norm_g, m_conv_norm_g, m_w_out, m_final_norm_g, v_meta, v_norm_g, v_w_in, v_b_f, v_conv_w, v_attn_norm_g, v_conv_norm_g, v_w_out, v_final_norm_g):
    cx_, cy_, cc_ = _position()
    chip = 2 * cx_ + cy_
    shard = w_in.shape[2]
    out_half = w_out.shape[1] // 2
    pick = lambda vals: jnp.where(chip == 0, vals[0], jnp.where(chip == 1, vals[1], jnp.where(chip == 2, vals[2], vals[3])))
    a_off, b_off = pick(A_OFF), pick(B_OFF)
    wt = jnp.transpose(w_in[0]).astype(MXU_DTYPE)
    wi = lax.dynamic_update_slice_in_dim(
        lax.dynamic_update_slice_in_dim(jnp.zeros((WIN_ROWS, D_MODEL), MXU_DTYPE), wt[:PIECE_A], a_off, 0),
        wt[PIECE_A:], b_off, 0)
    wo = w_out[0].astype(MXU_DTYPE)
    small = jnp.concatenate([meta, jnp.pad(conv_w[0], ((0, 8 - conv_w.shape[1]), (0, meta.shape[1] - conv_w.shape[2])))],
                            axis=0)
    gwi, gwo, gsm = _gather_weights(wi.reshape(2, WIN_HALF, D_MODEL), wo.reshape(2, out_half, D_MODEL), small)
    starts = jnp.stack([_window_start(jnp.bitwise_xor(chip, mask)) for mask in (0, 2, 1, 3)]).astype(jnp.int32)
    w_pad = _assemble_w(wi, gwi.reshape(3, WIN_ROWS, D_MODEL), starts)
    w_out_full = jnp.concatenate(_by_chip(wo, gwo.reshape(3, 2 * out_half, D_MODEL), chip), axis=0)
    small_full = jnp.concatenate(_by_chip(small, gsm, chip), axis=1)
    meta_full = small_full[:N_META]
    conv_w_full = jnp.concatenate([small_full[N_META:N_META + 3, 256 * s:256 * s + LANE] for s in range(N_CHIPS)], axis=1)
    final_g2 = final_norm_g.reshape(1, D_MODEL)
    r = _local_step(x[0], loss_target[0], meta_full, norm_g, w_pad, b_f, conv_w_full, attn_norm_g, conv_norm_g,
                    w_out_full, final_g2)
    grad_x = r["grad_x"][None]
    gb = r["gw_out"].reshape(N_CHIPS, 2, out_half, D_MODEL)
    wide = lambda a: jnp.pad(a, ((0, 0), (0, D_MODEL - a.shape[1])))
    pack = jnp.concatenate([
        r["g_norm"], r["g_final"], jnp.concatenate([r["g_attn"], r["g_convg"]], axis=1), wide(r["g_bf"]),
        wide(r["loss"]), jnp.zeros((3, D_MODEL), F32), r["d_front"][PAD_ROWS:], wide(r["g_cw"])], axis=0)
    ra, rb, packs = _pair_exchange(r["gw_in_wire"], gb, pack)
    c_idx = jnp.reshape(cc_, (1,)).astype(jnp.int32)
    chip_idx = jnp.reshape(chip, (1,)).astype(jnp.int32)
    pa, pa_wire = _pair_sum_windows(r["gw_in"], ra, c_idx)
    pb, pb_wire = _pair_sum(gb, rb, c_idx)
    xa, xb = _chip_exchange(pa_wire, pb_wire)
    ha = _chip_sum(pa, xa, chip_idx)
    hb = _chip_sum(pb, xb, chip_idx)
    oa, ob = _pair_share(ha, hb)
    g_window = _both_halves(ha, oa, cc_)
    g_w_in_t = jnp.concatenate([lax.dynamic_slice_in_dim(g_window, a_off, PIECE_A, 0),
                                lax.dynamic_slice_in_dim(g_window, b_off, shard - PIECE_A, 0)], axis=0)
    g_w_out = _both_halves(hb, ob, cc_)
    as_rows = lambda a: jnp.transpose(a, (2, 0, 1))
    g_w_in, d_w_in, nm_w_in, nv_w_in = (jnp.transpose(a, (1, 2, 0)) for a in _adamw_rows(
        as_rows(w_in), g_w_in_t, as_rows(m_w_in), as_rows(v_w_in)))
    d_w_out, nm_w_out, nv_w_out = (a[None] for a in _adamw_big(w_out[0], g_w_out, m_w_out[0], v_w_out[0], LANE))
    params = (norm_g, final_g2, attn_norm_g, conv_norm_g, b_f, meta, conv_w[0])
    ms = (m_norm_g, m_final_norm_g.reshape(1, D_MODEL), m_attn_norm_g, m_conv_norm_g, m_b_f, m_meta, m_conv_w[0])
    vs = (v_norm_g, v_final_norm_g.reshape(1, D_MODEL), v_attn_norm_g, v_conv_norm_g, v_b_f, v_meta, v_conv_w[0])
    loss, g_s, d_s, m_s, v_s = _small_update(pack, packs, params, ms, vs)

    def ordered(small_list, big_in, big_out):
        s_norm, s_final, s_attn, s_convg, s_bf, s_meta, s_cw = small_list
        return (s_meta, s_norm, big_in, s_bf, s_cw[None], s_attn, s_convg, big_out, s_final.reshape(D_MODEL))

    return (loss.reshape(()), grad_x,
            *ordered(g_s, g_w_in, g_w_out[None]), *ordered(d_s, d_w_in, d_w_out),
            *ordered(m_s, nm_w_in, nm_w_out), *ordered(v_s, nv_w_in, nv_w_out))
```

```python
import functools

import jax
import jax.numpy as jnp
from jax import lax
from jax.experimental import pallas as pl
from jax.experimental.pallas import tpu as pltpu

F32 = jnp.float32
MXU_DTYPE = jnp.bfloat16
WIRE_DTYPE = jnp.bfloat16

D_MODEL = 1024
N_META = 16
HEADS = 8
HEAD_DIM = 64
D_ATTN = HEADS * HEAD_DIM
D_CONV = 512
EPS = 1e-6
LANE = 128
SUBLANE = 8
ROW_TILE = 384
ATTN_UNROLL = 3
ATTN_BWD_QBLOCKS = 2
STAT_TERMS = 1
FRONT = LANE
PAD_ROWS = FRONT - N_META
NEG = -1e30
LOG2E = 1.4426950408889634
N_CHIPS = 4
N_DEV = 8
VMEM_LIMIT_BYTES = 60 * 1024 * 1024

SEG_Q, SEG_K, SEG_V, SEG_F, SEG_ZA, SEG_GB, SEG_GC, SEG_XC, SEG_ZC = (
    0, 512, 1024, 1536, 1664, 2176, 2688, 3200, 3712)
D_IN = 4104
D_IN_PAD = 4224
F_END = 1544
GW_COL_TILE = 1408
WIN_ROWS = 1152
WIN_HALF = WIN_ROWS // 2
WIN_START = (0, 1024, 2160, 3072)
PIECE_A = 518
A_OFF = (0, 2, 12, 126)
B_OFF = (518, 640, 530, 644)
ADAM_LR = 0.001
ADAM_B1 = 0.9
ADAM_B2 = 0.999
ADAM_EPS = 1e-08
ADAM_WD = 0.01
ADAM_STEP = 10

MESH = pl.DeviceIdType.MESH
ANY = pl.BlockSpec(memory_space=pl.ANY)

PACK_ROWS = 32
SLOT_NORM = (0, 1, 0, 1024)
SLOT_FINAL = (1, 2, 0, 1024)
SLOT_ATTN = (2, 3, 0, 512)
SLOT_CONVG = (2, 3, 512, 1024)
SLOT_BF = (3, 4, 0, 8)
SLOT_META = (8, 24, 0, 256)
SLOT_CONVW = (24, 27, 0, 128)
LOSS_ROW = 4


def _params(sem=None):
    return pltpu.CompilerParams(dimension_semantics=sem, vmem_limit_bytes=VMEM_LIMIT_BYTES)


def _sigmoid(z):
    return 1.0 / (1.0 + jnp.exp(-z))


def _dot(a, b):
    return jnp.dot(a, b, preferred_element_type=F32)


def _dot_nt(a, b):
    return lax.dot_general(a, b, (((1,), (1,)), ((), ())), preferred_element_type=F32)


def _dot_exact(ones, x):
    ones = ones.astype(MXU_DTYPE)
    total = None
    for _ in range(3):
        term = x.astype(MXU_DTYPE)
        x = x - term.astype(F32)
        total = _dot(ones, term) if total is None else total + _dot(ones, term)
    return total


def _group_matrix():
    r = lax.broadcasted_iota(jnp.int32, (D_ATTN, D_ATTN), 0) >> 6
    c = lax.broadcasted_iota(jnp.int32, (D_ATTN, D_ATTN), 1) >> 6
    return jnp.where(r == c, 1.0, 0.0).astype(MXU_DTYPE)


def _triangle(n, lower):
    r = lax.broadcasted_iota(jnp.int32, (n, n), 0)
    c = lax.broadcasted_iota(jnp.int32, (n, n), 1)
    return jnp.where((r >= c) if lower else (c >= r), 1.0, 0.0).astype(MXU_DTYPE)


def _group_sum(x, gmat, terms=2):
    hi = x.astype(MXU_DTYPE)
    if terms == 1:
        return _dot(hi, gmat)
    lo = (x - hi.astype(F32)).astype(MXU_DTYPE)
    return _dot(hi, gmat) + _dot(lo, gmat)


def _x_block_specs(n_sub, rows):
    specs = [pl.BlockSpec((rows, D_MODEL), lambda i: (jnp.maximum(n_sub * i - 1, 0), 0))]
    for b in range(1, n_sub):
        specs.append(pl.BlockSpec((rows, D_MODEL), functools.partial(lambda i, b: (n_sub * i - 1 + b, 0), b=b)))
    return specs


def _position():
    return lax.axis_index("x"), lax.axis_index("y"), lax.axis_index("c")


def _gather_weights(wi, wo, small):
    def body(wi_ref, wo_ref, sm_ref, gwi_ref, gwo_ref, gsm_ref, send_sems, recv_sems):
        x, y, c = _position()
        sibling = (x, y, 1 - c)
        chips = [(1 - x, y), (x, 1 - y), (1 - x, 1 - y)]

        def remote(k, src, dst, to):
            return pltpu.make_async_remote_copy(src_ref=src, dst_ref=dst, send_sem=send_sems.at[k],
                                                recv_sem=recv_sems.at[k], device_id=to, device_id_type=MESH)

        first, passed, landed = [], [], []
        for a, (src_ref, g_ref) in enumerate(((wi_ref, gwi_ref), (wo_ref, gwo_ref))):
            for j, (cx, cy) in enumerate(chips):
                slot = g_ref.at[j, c]
                first.append(remote(6 * a + j, src_ref.at[c], slot, (cx, cy, c)))
                landed.append(remote(6 * a + j, slot, slot, sibling))
                passed.append(remote(6 * a + 3 + j, slot, slot, sibling))
        for j, (cx, cy) in enumerate(chips):
            first.append(remote(12 + j, sm_ref, gsm_ref.at[j], (cx, cy, c)))
        for cp in first:
            cp.start()
        for arrived, onward in zip(landed, passed):
            arrived.wait_recv()
            onward.start()
        for a, g_ref in enumerate((gwi_ref, gwo_ref)):
            for j in range(3):
                remote(6 * a + 3 + j, g_ref.at[j, 1 - c], g_ref.at[j, 1 - c], sibling).wait_recv()
        for j in range(3):
            remote(12 + j, sm_ref, gsm_ref.at[j], sibling).wait_recv()
        for cp in first + passed:
            cp.wait_send()

    return pl.pallas_call(
        body, name="gather_weights",
        out_shape=(jax.ShapeDtypeStruct((3,) + wi.shape, wi.dtype), jax.ShapeDtypeStruct((3,) + wo.shape, wo.dtype),
                   jax.ShapeDtypeStruct((3,) + small.shape, small.dtype)),
        in_specs=[ANY, ANY, ANY], out_specs=(ANY, ANY, ANY),
        scratch_shapes=[pltpu.SemaphoreType.DMA((15,)), pltpu.SemaphoreType.DMA((15,))],
    )(wi, wo, small)


def _pair_exchange(gw, gb, pack):
    n_big = N_CHIPS + 1

    def body(gw_ref, gb_ref, p_ref, ra_ref, rb_ref, o_ref, send_sems, recv_sems):
        x, y, c = _position()
        sibling = (x, y, 1 - c)

        def remote(k, src, dst, to):
            return pltpu.make_async_remote_copy(src_ref=src, dst_ref=dst, send_sem=send_sems.at[k],
                                                recv_sem=recv_sems.at[k], device_id=to, device_id_type=MESH)

        copies = [remote(N_CHIPS, gb_ref.at[:, 1 - c], rb_ref, sibling)]
        for s, start in enumerate(WIN_START):
            rows = pl.ds(pl.multiple_of(start + WIN_HALF * (1 - c), 2 * SUBLANE), WIN_HALF)
            copies.append(remote(s, gw_ref.at[rows], ra_ref.at[s], sibling))
        for mask in range(1, N_DEV):
            peer = (1 - x if mask & 4 else x, 1 - y if mask & 2 else y, 1 - c if mask & 1 else c)
            copies.append(remote(n_big + mask - 1, p_ref, o_ref.at[mask - 1], peer))
        for cp in copies:
            cp.start()
        for cp in copies:
            cp.wait()

    n_sems = n_big + N_DEV - 1
    return pl.pallas_call(
        body, name="grad_pair_exchange",
        out_shape=(jax.ShapeDtypeStruct((N_CHIPS, WIN_HALF, D_MODEL), gw.dtype),
                   jax.ShapeDtypeStruct((N_CHIPS,) + gb.shape[2:], gb.dtype),
                   jax.ShapeDtypeStruct((N_DEV - 1,) + pack.shape, pack.dtype)),
        in_specs=[ANY, ANY, ANY], out_specs=(ANY, ANY, ANY),
        scratch_shapes=[pltpu.SemaphoreType.DMA((n_sems,)), pltpu.SemaphoreType.DMA((n_sems,))],
    )(gw, gb, pack)


def _chip_exchange(pa, pb):
    def body(pa_ref, pb_ref, ra_ref, rb_ref, send_sems, recv_sems):
        x, y, c = _position()
        chips = [(1 - x, y), (x, 1 - y), (1 - x, 1 - y)]
        copies = []
        for a, (src, dst) in enumerate(((pa_ref, ra_ref), (pb_ref, rb_ref))):
            for j, (cx, cy) in enumerate(chips):
                copies.append(pltpu.make_async_remote_copy(
                    src_ref=src.at[2 * cx + cy], dst_ref=dst.at[j], send_sem=send_sems.at[3 * a + j],
                    recv_sem=recv_sems.at[3 * a + j], device_id=(cx, cy, c), device_id_type=MESH))
        for cp in copies:
            cp.start()
        for cp in copies:
            cp.wait()

    return pl.pallas_call(
        body, name="grad_chip_exchange",
        out_shape=(jax.ShapeDtypeStruct((3,) + pa.shape[1:], pa.dtype),
                   jax.ShapeDtypeStruct((3,) + pb.shape[1:], pb.dtype)),
        in_specs=[ANY, ANY], out_specs=(ANY, ANY),
        scratch_shapes=[pltpu.SemaphoreType.DMA((6,)), pltpu.SemaphoreType.DMA((6,))],
    )(pa, pb)


def _pair_share(ha, hb):
    def body(ha_ref, hb_ref, oa_ref, ob_ref, send_sems, recv_sems):
        x, y, c = _position()
        copies = [pltpu.make_async_remote_copy(
            src_ref=src, dst_ref=dst, send_sem=send_sems.at[k], recv_sem=recv_sems.at[k],
            device_id=(x, y, 1 - c), device_id_type=MESH)
            for k, (src, dst) in enumerate(((ha_ref, oa_ref), (hb_ref, ob_ref)))]
        for cp in copies:
            cp.start()
        for cp in copies:
            cp.wait()

    return pl.pallas_call(
        body, name="grad_pair_share",
        out_shape=(jax.ShapeDtypeStruct(ha.shape, ha.dtype), jax.ShapeDtypeStruct(hb.shape, hb.dtype)),
        in_specs=[ANY, ANY], out_specs=(ANY, ANY),
        scratch_shapes=[pltpu.SemaphoreType.DMA((2,)), pltpu.SemaphoreType.DMA((2,))],
    )(ha, hb)


def _pair_sum(mine, recv, c_idx):
    rows, cols = mine.shape[2:]

    def body(c_ref, a_ref, b_ref, o_ref, send_ref):
        total = a_ref[...] + b_ref[...]
        o_ref[...] = total
        send_ref[...] = total.astype(send_ref.dtype)

    out_spec = pl.BlockSpec((None, rows, cols), lambda s, c_ref: (s, 0, 0))
    return pl.pallas_call(
        body, name="grad_pair_sum",
        grid_spec=pltpu.PrefetchScalarGridSpec(
            num_scalar_prefetch=1, grid=(N_CHIPS,),
            in_specs=[pl.BlockSpec((None, None, rows, cols), lambda s, c_ref: (s, c_ref[0], 0, 0)),
                      pl.BlockSpec((None, rows, cols), lambda s, c_ref: (s, 0, 0))],
            out_specs=(out_spec, out_spec)),
        out_shape=(jax.ShapeDtypeStruct(recv.shape, recv.dtype), jax.ShapeDtypeStruct(recv.shape, WIRE_DTYPE)),
        compiler_params=_params(("parallel",)),
    )(c_idx, mine, recv)


def _window_start(s):
    return jnp.where(s == 0, WIN_START[0], jnp.where(s == 1, WIN_START[1], jnp.where(s == 2, WIN_START[2], WIN_START[3])))


def _pair_sum_windows(gw, recv, c_idx):
    tr = WIN_HALF // 3

    def body(c_ref, a_ref, b_ref, o_ref, send_ref):
        total = a_ref[...] + b_ref[...].astype(F32)
        o_ref[...] = total
        send_ref[...] = total.astype(send_ref.dtype)

    out_spec = pl.BlockSpec((None, tr, D_MODEL), lambda s, i, c_ref: (s, i, 0))
    return pl.pallas_call(
        body, name="grad_pair_sum_windows",
        grid_spec=pltpu.PrefetchScalarGridSpec(
            num_scalar_prefetch=1, grid=(N_CHIPS, WIN_HALF // tr),
            in_specs=[pl.BlockSpec((pl.Element(tr), pl.Element(D_MODEL)),
                                   lambda s, i, c_ref: (pl.multiple_of(
                                       _window_start(s) + WIN_HALF * c_ref[0] + tr * i, SUBLANE), 0)),
                      pl.BlockSpec((None, tr, D_MODEL), lambda s, i, c_ref: (s, i, 0))],
            out_specs=(out_spec, out_spec)),
        out_shape=(jax.ShapeDtypeStruct(recv.shape, F32), jax.ShapeDtypeStruct(recv.shape, WIRE_DTYPE)),
        compiler_params=_params(("parallel", "parallel")),
    )(c_idx, gw, recv)


def _assemble_w(own, others, starts):
    def body(starts_ref, own_ref, oth_ref, o_ref):
        o_ref[...] = jnp.zeros_like(o_ref)
        for k in range(N_CHIPS):
            rows = pl.ds(pl.multiple_of(starts_ref[k], 2 * SUBLANE), WIN_ROWS)
            o_ref[rows, :] = o_ref[rows, :] + (own_ref[...] if k == 0 else oth_ref[k - 1])

    return pl.pallas_call(
        body, name="assemble_w",
        in_specs=[pl.BlockSpec(memory_space=pltpu.SMEM), pl.BlockSpec(memory_space=pltpu.VMEM),
                  pl.BlockSpec(memory_space=pltpu.VMEM)],
        out_specs=pl.BlockSpec(memory_space=pltpu.VMEM),
        out_shape=jax.ShapeDtypeStruct((D_IN_PAD, D_MODEL), own.dtype),
        compiler_params=_params(),
    )(starts, own, others)


def _chip_sum(psum, recv3, chip_idx):
    rows, cols = psum.shape[1:]
    tr = rows // 2

    def body(s_ref, p_ref, r0, r1, r2, o_ref):
        o_ref[...] = ((p_ref[...] + r0[...].astype(F32)) + r1[...].astype(F32)) + r2[...].astype(F32)

    return pl.pallas_call(
        body, name="grad_chip_sum",
        grid_spec=pltpu.PrefetchScalarGridSpec(
            num_scalar_prefetch=1, grid=(2,),
            in_specs=[pl.BlockSpec((None, tr, cols), lambda i, s_ref: (s_ref[0], i, 0))] +
                     [pl.BlockSpec((None, tr, cols), functools.partial(lambda i, s_ref, j: (j, i, 0), j=j))
                      for j in range(3)],
            out_specs=pl.BlockSpec((tr, cols), lambda i, s_ref: (i, 0))),
        out_shape=jax.ShapeDtypeStruct((rows, cols), psum.dtype),
        compiler_params=_params(("parallel",)),
    )(chip_idx, psum, recv3, recv3, recv3)


def _adamw_math(w, g, m, v):
    m = ADAM_B1 * m + (1.0 - ADAM_B1) * g
    v = ADAM_B2 * v + (1.0 - ADAM_B2) * (g * g)
    m_hat = m * (1.0 / (1.0 - ADAM_B1 ** ADAM_STEP))
    v_hat = v * (1.0 / (1.0 - ADAM_B2 ** ADAM_STEP))
    delta = -ADAM_LR * (m_hat / (jnp.sqrt(v_hat) + ADAM_EPS) + ADAM_WD * w)
    return delta, m, v


def _adamw_big(w, g, m, v, tr):
    rows, cols = w.shape
    assert rows % tr == 0 and g.shape[0] >= rows

    def body(w_ref, g_ref, m_ref, v_ref, d_out, m_out, v_out):
        d, m2, v2 = _adamw_math(w_ref[...], g_ref[...], m_ref[...], v_ref[...])
        d_out[...] = d
        m_out[...] = m2
        v_out[...] = v2

    spec = pl.BlockSpec((tr, cols), lambda i: (i, 0))
    sds = jax.ShapeDtypeStruct((rows, cols), F32)
    return pl.pallas_call(
        body, name="adamw_big", grid=(rows // tr,), in_specs=[spec] * 4, out_specs=(spec,) * 3,
        out_shape=(sds,) * 3, compiler_params=_params(("parallel",)),
    )(w, g, m, v)


def _adamw_rows(w3, g, m3, v3):
    rows, _, cols = w3.shape
    tc = 2 * LANE

    def body(w_ref, g_ref, m_ref, v_ref, g_out, d_out, m_out, v_out):
        g = g_ref[...]
        d, m2, v2 = _adamw_math(w_ref[:, 0, :], g, m_ref[:, 0, :], v_ref[:, 0, :])
        g_out[:, 0, :] = g
        d_out[:, 0, :] = d
        m_out[:, 0, :] = m2
        v_out[:, 0, :] = v2

    spec3 = pl.BlockSpec((rows, 1, tc), lambda i: (0, 0, i))
    sds = jax.ShapeDtypeStruct((rows, 1, cols), F32)
    return pl.pallas_call(
        body, name="adamw_rows", grid=(cols // tc,),
        in_specs=[spec3, pl.BlockSpec((rows, tc), lambda i: (0, i)), spec3, spec3], out_specs=(spec3,) * 4,
        out_shape=(sds,) * 4, compiler_params=_params(("parallel",)),
    )(w3, g, m3, v3)


def _small_update(own, others, params, ms, vs):
    slots = (SLOT_NORM, SLOT_FINAL, SLOT_ATTN, SLOT_CONVG, SLOT_BF, SLOT_META, SLOT_CONVW)
    n = len(slots)

    def body(*refs):
        own_ref, gp_ref = refs[:2]
        w_refs, m_refs, v_refs = refs[2:2 + n], refs[2 + n:2 + 2 * n], refs[2 + 2 * n:2 + 3 * n]
        outs = refs[2 + 3 * n:3 + 7 * n]
        loss_ref = outs[0]
        g_outs, d_outs, m_outs, v_outs = (outs[1 + k * n:1 + (k + 1) * n] for k in range(4))
        g_scr, w_scr, m_scr, v_scr = refs[3 + 7 * n:]
        x, y, c = _position()
        shard = 2 * x + y
        me = 4 * x + 2 * y + c
        tot = None
        for d in range(N_DEV):
            rel = jnp.bitwise_xor(me, d)
            term = jnp.where(rel == 0, own_ref[...], gp_ref[jnp.maximum(rel, 1) - 1])
            tot = term if tot is None else tot + term
        r0, r1, _, _ = SLOT_META
        meta_sel = tot[r0:r1, 0:256]
        cw_sel = tot[24:32, 0:128]
        for k in range(1, N_CHIPS):
            meta_sel = jnp.where(shard == k, tot[r0:r1, 256 * k:256 * (k + 1)], meta_sel)
            cw_sel = jnp.where(shard == k, tot[24:32, 128 * k:128 * (k + 1)], cw_sel)
        zeros = jnp.zeros((PACK_ROWS, D_MODEL), F32)
        for scr in (g_scr, w_scr, m_scr, v_scr):
            scr[...] = zeros
        g_scr[0:8, :] = tot[0:8, :]
        g_scr[r0:r1, 0:256] = meta_sel
        g_scr[24:32, 0:128] = cw_sel
        for (a, b, c0, c1), w_ref, m_ref, v_ref in zip(slots, w_refs, m_refs, v_refs):
            w_scr[a:b, c0:c1] = w_ref[...]
            m_scr[a:b, c0:c1] = m_ref[...]
            v_scr[a:b, c0:c1] = v_ref[...]
        loss_ref[...] = g_scr[LOSS_ROW:LOSS_ROW + 1, 0:1]
        d, m2, v2 = _adamw_math(w_scr[...], g_scr[...], m_scr[...], v_scr[...])
        w_scr[...] = d
        m_scr[...] = m2
        v_scr[...] = v2
        for (a, b, c0, c1), g_o, d_o, m_o, v_o in zip(slots, g_outs, d_outs, m_outs, v_outs):
            g_o[...] = g_scr[a:b, c0:c1]
            d_o[...] = w_scr[a:b, c0:c1]
            m_o[...] = m_scr[a:b, c0:c1]
            v_o[...] = v_scr[a:b, c0:c1]

    shapes = [jax.ShapeDtypeStruct(p.shape, F32) for p in params]
    out = pl.pallas_call(
        body, name="small_update",
        out_shape=[jax.ShapeDtypeStruct((1, 1), F32)] + shapes * 4,
        scratch_shapes=[pltpu.VMEM((PACK_ROWS, D_MODEL), F32)] * 4,
        compiler_params=_params(),
    )(own, others, *params, *ms, *vs)
    return out[0], out[1:1 + n], out[1 + n:1 + 2 * n], out[1 + 2 * n:1 + 3 * n], out[1 + 3 * n:1 + 4 * n]


def _in_proj(x2, meta_blk, norm_g, w_pad, bf_pad):
    seq = x2.shape[0]
    lp = seq + FRONT
    t = ROW_TILE
    nt = lp // t
    n_sub = t // LANE

    def body(*refs):
        x_refs = refs[:n_sub]
        mb, g_ref, w_ref, bf_ref, tri_ref = refs[n_sub:n_sub + 5]
        q_ref, k_ref, v_ref, rest_ref, fl_ref, ct_ref, u_ref, carry = refs[n_sub + 5:]
        i = pl.program_id(0)

        @pl.when(i == 0)
        def _():
            carry[...] = jnp.zeros_like(carry)

        first = jnp.where(i == 0, mb[...], x_refs[0][...])
        h = jnp.concatenate([first] + [r[...] for r in x_refs[1:]], axis=0)
        ms = jnp.mean(h * h, axis=-1, keepdims=True)
        u = ((h * lax.rsqrt(ms + EPS)) * g_ref[...]).astype(MXU_DTYPE)
        u_ref[...] = u

        def seg(a, width):
            return _dot_nt(u, w_ref[a:a + width, :])

        q_ref[...] = (seg(SEG_Q, D_ATTN) * (HEAD_DIM ** -0.5)).astype(MXU_DTYPE)
        k_ref[...] = seg(SEG_K, D_ATTN).astype(MXU_DTYPE)
        v_ref[...] = seg(SEG_V, D_ATTN).astype(MXU_DTYPE)
        for s in range(5):
            rest_ref[:, 512 * s:512 * (s + 1)] = seg(SEG_ZA + 512 * s, 512)
        fl = seg(SEG_F, LANE)
        fl_ref[...] = fl
        z = fl + bf_ref[...]
        logf = jnp.minimum(z, 0.0) - jnp.log(1.0 + jnp.exp(-jnp.abs(z)))
        row = i * t + lax.broadcasted_iota(jnp.int32, (t, LANE), 0)
        logf = jnp.where(row >= PAD_ROWS, logf, 0.0)
        cs = _dot_exact(tri_ref[...], logf) + carry[...]
        carry[...] = carry[...] + jnp.sum(logf, axis=0, keepdims=True)
        col = i * t + lax.broadcasted_iota(jnp.int32, (SUBLANE, t), 1)
        ct_ref[...] = jnp.where(col >= PAD_ROWS, cs.T[0:SUBLANE, :], -NEG)

    row_blk = lambda cols: pl.BlockSpec((t, cols), lambda i: (i, 0))
    const = lambda shape: pl.BlockSpec(shape, lambda i: (0, 0))
    return pl.pallas_call(
        body, name="in_proj", grid=(nt,),
        in_specs=_x_block_specs(n_sub, LANE) + [const((LANE, D_MODEL)), const((1, D_MODEL)),
                                                pl.BlockSpec((D_IN_PAD, D_MODEL), lambda i: (0, 0),
                                                             pipeline_mode=pl.Buffered(1)),
                                                const((1, LANE)), const((t, t))],
        out_specs=(row_blk(D_ATTN), row_blk(D_ATTN), row_blk(D_ATTN), row_blk(5 * 512), row_blk(LANE),
                   pl.BlockSpec((SUBLANE, t), lambda i: (0, i)), row_blk(D_MODEL)),
        out_shape=(jax.ShapeDtypeStruct((lp, D_ATTN), MXU_DTYPE), jax.ShapeDtypeStruct((lp, D_ATTN), MXU_DTYPE),
                   jax.ShapeDtypeStruct((lp, D_ATTN), MXU_DTYPE), jax.ShapeDtypeStruct((lp, 5 * 512), F32),
                   jax.ShapeDtypeStruct((lp, LANE), F32),
                   jax.ShapeDtypeStruct((SUBLANE, lp), F32), jax.ShapeDtypeStruct((lp, D_MODEL), MXU_DTYPE)),
        scratch_shapes=[pltpu.VMEM((1, LANE), F32)],
        compiler_params=_params(("arbitrary",)),
    )(*([x2] * n_sub), meta_blk, norm_g, w_pad, bf_pad, _triangle(t, lower=True))


def _head_masks():
    lane = lax.broadcasted_iota(jnp.int32, (1, LANE), 1)
    return lane < HEAD_DIM, lane >= HEAD_DIM


def _pair_specs(lp, nt, t):
    blk = pl.BlockSpec((lp, LANE), lambda g: (0, g))
    ct_a = pl.BlockSpec((None, nt, 1, t), lambda g: (2 * g, 0, 0, 0))
    ct_b = pl.BlockSpec((None, nt, 1, t), lambda g: (2 * g + 1, 0, 0, 0))
    return blk, ct_a, ct_b


def _sub_rows(s, col):
    return jnp.concatenate([s[:, a * LANE:(a + 1) * LANE] - col for a in range(s.shape[1] // LANE)], axis=1)


def _loop_unrolled(lo, hi, step, init, n):
    def group(jj, carry):
        for k in range(n):
            carry = step(lo + n * jj + k, carry)
        return carry

    groups = (hi - lo) // n
    carry = lax.fori_loop(0, groups, group, init)
    return lax.fori_loop(lo + n * groups, hi, step, carry)


def _lane_chunks(s):
    return [s[:, a * LANE:(a + 1) * LANE] for a in range(s.shape[1] // LANE)]


def _attn_fwd(q, k, v, ct4):
    lp = q.shape[0]
    t = ROW_TILE
    nt = lp // t

    def body(q_ref, k_ref, v_ref, cta_ref, ctb_ref, o_ref, l_ref, m_ref, s_scr, last_scr, m_scr, acc_scr):
        masks = _head_masks()
        ct_refs = (cta_ref, ctb_ref)
        below = lax.broadcasted_iota(jnp.int32, (t, t), 1) <= lax.broadcasted_iota(jnp.int32, (t, t), 0)
        lane = lax.broadcasted_iota(jnp.int32, (1, LANE), 1)
        head_of_row = lax.broadcasted_iota(jnp.int32, (2 * t, LANE), 0) >= t
        ones_cols = jnp.where(lax.broadcasted_iota(jnp.int32, (2 * t, LANE), 1) == head_of_row.astype(jnp.int32),
                              1.0, 0.0).astype(MXU_DTYPE)

        def q_rows(i, rows):
            r0 = pl.multiple_of(i * t, t)
            qi = q_ref[pl.ds(r0, rows), :]
            two = rows == 2 * t
            if two:
                on_first_diagonal = jnp.concatenate([below, jnp.ones((t, t), jnp.bool_)], axis=0)

            def scores(j, lhs):
                kj = k_ref[pl.ds(pl.multiple_of(j * t, t), t), :]
                return _dot_nt(lhs, jnp.concatenate([jnp.where(hm, kj, 0).astype(MXU_DTYPE) for hm in masks], axis=0))

            def max_step(j, carry, mask=None):
                s2 = scores(j, qi)
                for hh in range(2):
                    s = (s2[:, hh * t:(hh + 1) * t] - ct_refs[hh][j]) * LOG2E
                    if mask is not None:
                        s = jnp.where(mask, s, NEG)
                    s_scr[j, 0:rows, hh * t:(hh + 1) * t] = s
                    m = m_scr[hh, 0:rows]
                    for c in _lane_chunks(s):
                        m = jnp.maximum(m, c)
                    m_scr[hh, 0:rows] = m
                return carry

            m_scr[...] = jnp.full(m_scr.shape, NEG, F32)
            _loop_unrolled(0, i, max_step, 0, ATTN_UNROLL)
            max_step(i, 0, on_first_diagonal if two else below)
            if two:
                s2 = scores(i + 1, qi[t:])
                for hh in range(2):
                    s = jnp.where(below, (s2[:, hh * t:(hh + 1) * t] - ct_refs[hh][i + 1]) * LOG2E, NEG)
                    last_scr[:, hh * t:(hh + 1) * t] = s
                    m = m_scr[hh, t:rows]
                    for c in _lane_chunks(s):
                        m = jnp.maximum(m, c)
                    m_scr[hh, t:rows] = m
            ms = [jnp.max(m_scr[hh, 0:rows], axis=-1, keepdims=True) for hh in range(2)]

            def probabilities(scores_of, ms_rows):
                return jnp.concatenate([jnp.exp2(scores_of(hh) - ms_rows[hh]).astype(MXU_DTYPE) for hh in range(2)], axis=1)

            def values(j):
                vj = v_ref[pl.ds(pl.multiple_of(j * t, t), t), :]
                v2 = jnp.concatenate([jnp.where(hm, vj, 0).astype(MXU_DTYPE) for hm in masks], axis=0)
                return jnp.concatenate([v2, ones_cols], axis=1)

            def sum_step(j, carry):
                p = probabilities(lambda hh: s_scr[j, 0:rows, hh * t:(hh + 1) * t], ms)
                acc_scr[0:rows] = acc_scr[0:rows] + _dot(p, values(j))
                return carry

            acc_scr[...] = jnp.zeros(acc_scr.shape, F32)
            _loop_unrolled(0, i + 1, sum_step, 0, ATTN_UNROLL)
            if two:
                p = probabilities(lambda hh: last_scr[:, hh * t:(hh + 1) * t], [m[t:] for m in ms])
                acc_scr[t:rows] = acc_scr[t:rows] + _dot(p, values(i + 1))
            acc = acc_scr[0:rows]
            sums = acc[:, LANE:]
            l_pair = jnp.where(masks[0], jnp.sum(jnp.where(lane == 0, sums, 0.0), axis=-1, keepdims=True),
                               jnp.sum(jnp.where(lane == 1, sums, 0.0), axis=-1, keepdims=True))
            o_ref[pl.ds(r0, rows), :] = acc[:, :LANE] / l_pair
            l_ref[pl.ds(r0, rows), :] = l_pair
            m_ref[pl.ds(r0, rows), 0:LANE] = jnp.broadcast_to(ms[0], (rows, LANE))
            m_ref[pl.ds(r0, rows), LANE:2 * LANE] = jnp.broadcast_to(ms[1], (rows, LANE))

        def two_blocks(p, _):
            q_rows(2 * p, 2 * t)
            return 0

        lax.fori_loop(0, nt // 2, two_blocks, 0)
        if nt % 2:
            q_rows(nt - 1, t)

    blk, ct_a, ct_b = _pair_specs(lp, nt, t)
    return pl.pallas_call(
        body, name="attn_fwd", grid=(HEADS // 2,),
        in_specs=[blk, blk, blk, ct_a, ct_b], out_specs=(blk, blk, pl.BlockSpec((lp, 2 * LANE), lambda g: (0, g))),
        out_shape=(jax.ShapeDtypeStruct((lp, D_ATTN), F32), jax.ShapeDtypeStruct((lp, D_ATTN), F32),
                   jax.ShapeDtypeStruct((lp, HEADS * LANE), F32)),
        scratch_shapes=[pltpu.VMEM((nt, 2 * t, 2 * t), F32), pltpu.VMEM((t, 2 * t), F32),
                        pltpu.VMEM((2, 2 * t, LANE), F32), pltpu.VMEM((2 * t, 2 * LANE), F32)],
        compiler_params=_params(("parallel",)),
    )(q, k, v, ct4, ct4)


def _attn_bwd(q, k, v, do, q_t, do_t, m, delta, ct4):
    lp = q.shape[0]
    t = ROW_TILE
    nt = lp // t

    def body(q_ref, k_ref, v_ref, do_ref, qt_ref, dot_ref, ma_ref, mb_ref, dla_ref, dlb_ref, cta_ref, ctb_ref,
             dq_ref, dk_ref, dv_ref, dc_ref, dq_acc, dk_acc, dv_acc):
        masks = _head_masks()
        ct_refs, m_refs, dl_refs = (cta_ref, ctb_ref), (ma_ref, mb_ref), (dla_ref, dlb_ref)
        below = lax.broadcasted_iota(jnp.int32, (t, t), 1) <= lax.broadcasted_iota(jnp.int32, (t, t), 0)
        dq_acc[...] = jnp.zeros_like(dq_acc)

        def k_block(j, _):
            c0 = pl.multiple_of(j * t, t)
            kj = k_ref[pl.ds(c0, t), :]
            vj = v_ref[pl.ds(c0, t), :]
            k2 = jnp.concatenate([jnp.where(hm, kj, 0).astype(MXU_DTYPE) for hm in masks], axis=0)
            v2 = jnp.concatenate([jnp.where(hm, vj, 0).astype(MXU_DTYPE) for hm in masks], axis=0)
            ck = [r[j] for r in ct_refs]
            dk_acc[...] = jnp.zeros_like(dk_acc)
            dv_acc[...] = jnp.zeros_like(dv_acc)

            def q_block(i, colsums, diagonal, rows=t):
                r0 = pl.multiple_of(i * t, t)
                qi = q_ref[pl.ds(r0, rows), :]
                doi = do_ref[pl.ds(r0, rows), :]
                qti = jnp.concatenate([qt_ref[i + b] for b in range(rows // t)], axis=1)
                doti = jnp.concatenate([dot_ref[i + b] for b in range(rows // t)], axis=1)
                s2 = _dot_nt(qi, k2)
                dp2 = _dot_nt(doi, v2)
                out, ps, dss = [], [], []
                for hh in range(2):
                    s = (s2[:, hh * t:(hh + 1) * t] - ck[hh]) * LOG2E
                    if diagonal:
                        s = jnp.where(below, s, NEG)
                    p = jnp.exp2(_sub_rows(s, m_refs[hh][pl.ds(r0, rows), :])).astype(MXU_DTYPE)
                    ds32 = p.astype(F32) * _sub_rows(dp2[:, hh * t:(hh + 1) * t], dl_refs[hh][pl.ds(r0, rows), :])
                    ps.append(p)
                    dss.append(ds32.astype(MXU_DTYPE))
                    out.append(colsums[hh] + jnp.sum(ds32, axis=0, keepdims=True))
                ds_cat = jnp.concatenate(dss, axis=1)
                dv_acc[...] = dv_acc[...] + _dot(doti, jnp.concatenate(ps, axis=1))
                dk_acc[...] = dk_acc[...] + _dot(qti, ds_cat)
                dq_acc[pl.ds(r0, rows), :] = dq_acc[pl.ds(r0, rows), :] + _dot(ds_cat, k2)
                return tuple(out)

            colsums = q_block(j, (jnp.zeros((1, t), F32), jnp.zeros((1, t), F32)), True)
            nq = ATTN_BWD_QBLOCKS
            groups = (nt - 1 - j) // nq
            colsums = lax.fori_loop(0, groups, lambda p, c: q_block(j + 1 + nq * p, c, False, nq * t), colsums)
            colsums = lax.fori_loop(j + 1 + nq * groups, nt, functools.partial(q_block, diagonal=False), colsums)
            for hh in range(2):
                dc_ref[hh, j] = -colsums[hh]
            own = lambda acc: jnp.concatenate([acc[:HEAD_DIM, :t], acc[HEAD_DIM:, t:]], axis=0).T
            dk_ref[pl.ds(c0, t), :] = own(dk_acc[...]).astype(dk_ref.dtype)
            dv_ref[pl.ds(c0, t), :] = own(dv_acc[...]).astype(dv_ref.dtype)
            return 0

        lax.fori_loop(0, nt, k_block, 0)
        dq_ref[...] = (dq_acc[...] * (HEAD_DIM ** -0.5)).astype(dq_ref.dtype)

    blk, ct_a, ct_b = _pair_specs(lp, nt, t)
    rep_a = pl.BlockSpec((lp, LANE), lambda g: (0, 2 * g))
    rep_b = pl.BlockSpec((lp, LANE), lambda g: (0, 2 * g + 1))
    tr_blk = pl.BlockSpec((nt, LANE, t), lambda g: (0, g, 0))
    return pl.pallas_call(
        body, name="attn_bwd", grid=(HEADS // 2,),
        in_specs=[blk] * 4 + [tr_blk, tr_blk, rep_a, rep_b, rep_a, rep_b, ct_a, ct_b],
        out_specs=(blk, blk, blk, pl.BlockSpec((2, nt, 1, t), lambda g: (g, 0, 0, 0))),
        out_shape=(jax.ShapeDtypeStruct((lp, D_ATTN), MXU_DTYPE),) * 3
                  + (jax.ShapeDtypeStruct((HEADS, nt, 1, t), F32),),
        scratch_shapes=[pltpu.VMEM((lp, LANE), F32), pltpu.VMEM((LANE, 2 * t), F32), pltpu.VMEM((LANE, 2 * t), F32)],
        compiler_params=_params(("parallel",)),
    )(q, k, v, do, q_t, do_t, m, m, delta, delta, ct4, ct4)


def _shift_down(prev8, cur, k):
    ext = jnp.concatenate([prev8, cur], axis=0)
    return pltpu.roll(ext, k, 0)[SUBLANE:, :]


def _shift_up(cur, next8, k):
    ext = jnp.concatenate([cur, next8], axis=0)
    n = ext.shape[0]
    return pltpu.roll(ext, n - k, 0)[:cur.shape[0], :]


def _post(o, l_sum, rest, x2, meta_blk, tgt2, w_out, attn_g, conv_g, final_g, conv_w8):
    lp = o.shape[0]
    t = ROW_TILE
    nt = lp // t
    n_sub = t // LANE
    hb = t // SUBLANE

    def body(*refs):
        o_ref, l_ref, za_ref, gb_ref, gc_ref, xc_ref, zc_ref, gch_ref, xch_ref = refs[:9]
        x_refs = refs[9:9 + n_sub]
        mb = refs[9 + n_sub]
        t_refs = refs[10 + n_sub:10 + 2 * n_sub]
        wo_ref, ag_ref, cg_ref, fg_ref, cw_ref, gm_ref, hr_ref = refs[10 + 2 * n_sub:17 + 2 * n_sub]
        (dout_ref, do_ref, dl_ref, dza_ref, dgb_ref, dzc_ref, dcv_ref,
         loss_ref, gf_ref, gag_ref, gcg_ref, gwo_ref) = refs[17 + 2 * n_sub:]
        i = pl.program_id(0)

        @pl.when(i == 0)
        def _():
            for r in (loss_ref, gf_ref, gag_ref, gcg_ref, gwo_ref):
                r[...] = jnp.zeros_like(r)

        gmat = gm_ref[...]
        inv_g = 1.0 / HEAD_DIM
        o_v = o_ref[...]
        ra = lax.rsqrt(_group_sum(o_v * o_v, gmat, STAT_TERMS) * inv_g + EPS)
        n_a = o_v * ra
        a_n = n_a * ag_ref[...]
        za = za_ref[...]
        sig_a = _sigmoid(za)
        sz_a = za * sig_a
        y_a = a_n * sz_a
        gb = gb_ref[...]
        gc = gc_ref[...]
        xc = xc_ref[...]
        cx = gc * xc
        cx_prev = jnp.where(i == 0, 0.0, gch_ref[...] * xch_ref[...])
        conv = (cw_ref[0:1, :] * _shift_down(cx_prev, cx, 2) + cw_ref[1:2, :] * _shift_down(cx_prev, cx, 1)
                + cw_ref[2:3, :] * cx)
        e = gb * conv
        re = lax.rsqrt(_group_sum(e * e, gmat, STAT_TERMS) * inv_g + EPS)
        n_e = e * re
        e_n = n_e * cg_ref[...]
        zc = zc_ref[...]
        sig_c = _sigmoid(zc)
        sz_c = zc * sig_c
        y_c = e_n * sz_c
        mix = jnp.concatenate([y_a, y_c], axis=-1)
        mix_b = mix.astype(MXU_DTYPE)
        first = jnp.where(i == 0, mb[...], x_refs[0][...])
        h = jnp.concatenate([first] + [r[...] for r in x_refs[1:]], axis=0)
        out = h + _dot(mix_b, wo_ref[...])
        r2 = lax.rsqrt(jnp.mean(out * out, axis=-1, keepdims=True) + EPS)
        n_f = out * r2
        y = n_f * fg_ref[...]
        tgt = jnp.concatenate([r[...] for r in t_refs], axis=0)
        valid = (i * t + lax.broadcasted_iota(jnp.int32, (t, 1), 0)) >= FRONT
        diff = jnp.where(valid, y - tgt, 0.0)
        loss_ref[...] = loss_ref[...] + 0.5 * jnp.sum(jnp.sum(diff * diff, axis=-1, keepdims=True) * (1.0 / D_MODEL))
        dy = diff * (1.0 / D_MODEL)
        gf_ref[...] = gf_ref[...] + jnp.sum(dy * n_f, axis=0, keepdims=True)
        dn = dy * fg_ref[...]
        d_out = r2 * (dn - n_f * jnp.mean(dn * n_f, axis=-1, keepdims=True))
        dout_ref[...] = d_out
        d_out_b = d_out.astype(MXU_DTYPE)
        d_mix = _dot_nt(d_out_b, wo_ref[...])
        gwo_ref[...] = gwo_ref[...] + _dot(mix.T.astype(MXU_DTYPE), d_out_b)
        d_ya = d_mix[:, :D_ATTN]
        d_yc = d_mix[:, D_ATTN:]
        d_an = d_ya * sz_a
        dza_ref[...] = (d_ya * a_n * (sig_a * (1.0 + za * (1.0 - sig_a)))).astype(dza_ref.dtype)
        gag_ref[...] = gag_ref[...] + jnp.sum(d_an * n_a, axis=0, keepdims=True)
        dn_a = d_an * ag_ref[...]
        d_o = ra * (dn_a - n_a * (_group_sum(dn_a * n_a, gmat, STAT_TERMS) * inv_g))
        d_o_b = (d_o / l_ref[...]).astype(do_ref.dtype)
        do_ref[...] = d_o_b
        dl_ref[...] = _group_sum(d_o_b.astype(F32) * o_v, hr_ref[...])
        d_en = d_yc * sz_c
        dzc_ref[...] = (d_yc * e_n * (sig_c * (1.0 + zc * (1.0 - sig_c)))).astype(dzc_ref.dtype)
        gcg_ref[...] = gcg_ref[...] + jnp.sum(d_en * n_e, axis=0, keepdims=True)
        dn_e = d_en * cg_ref[...]
        d_e = re * (dn_e - n_e * (_group_sum(dn_e * n_e, gmat, STAT_TERMS) * inv_g))
        dgb_ref[...] = (d_e * conv).astype(dgb_ref.dtype)
        dcv_ref[...] = d_e * gb

    head_rep = jnp.where((lax.broadcasted_iota(jnp.int32, (D_ATTN, HEADS * LANE), 0) >> 6)
                         == (lax.broadcasted_iota(jnp.int32, (D_ATTN, HEADS * LANE), 1) >> 7), 1.0, 0.0).astype(MXU_DTYPE)
    row_blk = lambda cols: pl.BlockSpec((t, cols), lambda i: (i, 0))
    rest_blk = lambda s: pl.BlockSpec((t, 512), functools.partial(lambda i, s: (i, s), s=s))
    halo = lambda s: pl.BlockSpec((SUBLANE, 512), functools.partial(lambda i, s: (jnp.maximum(i * hb - 1, 0), s), s=s))
    const = lambda shape: pl.BlockSpec(shape, lambda i: (0, 0))
    acc = lambda shape: pl.BlockSpec(shape, lambda i: (0, 0))
    return pl.pallas_call(
        body, name="post_fwd_bwd", grid=(nt,),
        in_specs=[row_blk(D_ATTN), row_blk(D_ATTN)] + [rest_blk(s) for s in range(5)] + [halo(2), halo(3)]
                 + _x_block_specs(n_sub, LANE) + [const((LANE, D_MODEL))] + _x_block_specs(n_sub, LANE)
                 + [const((D_MODEL, D_MODEL)), const((1, D_ATTN)), const((1, D_CONV)), const((1, D_MODEL)),
                    const((SUBLANE, D_CONV)), const((D_ATTN, D_ATTN)), const((D_ATTN, HEADS * LANE))],
        out_specs=(row_blk(D_MODEL), row_blk(D_ATTN), row_blk(HEADS * LANE), row_blk(D_ATTN), row_blk(D_CONV),
                   row_blk(D_CONV), row_blk(D_CONV),
                   acc((1, LANE)), acc((1, D_MODEL)), acc((1, D_ATTN)), acc((1, D_CONV)), acc((D_MODEL, D_MODEL))),
        out_shape=(jax.ShapeDtypeStruct((lp, D_MODEL), F32), jax.ShapeDtypeStruct((lp, D_ATTN), MXU_DTYPE),
                   jax.ShapeDtypeStruct((lp, HEADS * LANE), F32), jax.ShapeDtypeStruct((lp, D_ATTN), MXU_DTYPE),
                   jax.ShapeDtypeStruct((lp, D_CONV), MXU_DTYPE), jax.ShapeDtypeStruct((lp, D_CONV), MXU_DTYPE),
                   jax.ShapeDtypeStruct((lp, D_CONV), F32),
                   jax.ShapeDtypeStruct((1, LANE), F32), jax.ShapeDtypeStruct((1, D_MODEL), F32),
                   jax.ShapeDtypeStruct((1, D_ATTN), F32), jax.ShapeDtypeStruct((1, D_CONV), F32),
                   jax.ShapeDtypeStruct((D_MODEL, D_MODEL), F32)),
        compiler_params=_params(("arbitrary",)),
    )(o, l_sum, *([rest] * 5), rest, rest, *([x2] * n_sub), meta_blk, *([tgt2] * n_sub),
      w_out, attn_g, conv_g, final_g, conv_w8, _group_matrix(), head_rep)


def _bwd_in(x2, meta_blk, norm_g, w_pad, bf_pad, fl, dc, dq, dk, dv, dza, dgb, dzc, dconv, rest, d_out, conv_w8):
    lp = fl.shape[0]
    t = ROW_TILE
    nt = lp // t
    n_sub = t // LANE
    hb = t // SUBLANE
    rev = lambda i: nt - 1 - i

    def body(*refs):
        x_refs = refs[:n_sub]
        (mb, g_ref, w_ref, bf_ref, fl_ref, dc_ref, dq_ref, dk_ref, dv_ref, dza_ref, dgb_ref, dzc_ref,
         dcv_ref, dcvn_ref, gc_ref, xc_ref, gch_ref, xch_ref, dout_ref, cw_ref, tri_ref) = refs[n_sub:n_sub + 21]
        dp_ref, gx_ref, front_ref, gn_ref, gbf_ref, gcw_ref, carry, dh_scr, gx_sems = refs[n_sub + 21:]
        step = pl.program_id(0)
        i = rev(step)

        @pl.when(step == 0)
        def _():
            for r in (gn_ref, gbf_ref, gcw_ref, carry):
                r[...] = jnp.zeros_like(r)

        dc8 = jnp.concatenate([dc_ref[...], jnp.zeros((LANE - HEADS, t), F32)], axis=0).T
        dlogf = _dot_exact(tri_ref[...], dc8) + carry[...]
        carry[...] = carry[...] + jnp.sum(dc8, axis=0, keepdims=True)
        z = fl_ref[...] + bf_ref[...]
        row = i * t + lax.broadcasted_iota(jnp.int32, (t, LANE), 0)
        d_f = jnp.where(row >= PAD_ROWS, dlogf * (1.0 / (1.0 + jnp.exp(z))), 0.0)
        gbf_ref[...] = gbf_ref[...] + jnp.sum(d_f, axis=0, keepdims=True)
        dcv = dcv_ref[...]
        dcv_next = jnp.where(i == nt - 1, 0.0, dcvn_ref[...])
        d_cx = (cw_ref[2:3, :] * dcv + cw_ref[1:2, :] * _shift_up(dcv, dcv_next, 1)
                + cw_ref[0:1, :] * _shift_up(dcv, dcv_next, 2))
        gc = gc_ref[...]
        xc = xc_ref[...]
        cx = gc * xc
        cx_prev = jnp.where(i == 0, 0.0, gch_ref[...] * xch_ref[...])
        rowi = lax.broadcasted_iota(jnp.int32, (SUBLANE, 1), 0)
        gcw = (jnp.where(rowi == 0, jnp.sum(dcv * _shift_down(cx_prev, cx, 2), axis=0, keepdims=True), 0.0)
               + jnp.where(rowi == 1, jnp.sum(dcv * _shift_down(cx_prev, cx, 1), axis=0, keepdims=True), 0.0)
               + jnp.where(rowi == 2, jnp.sum(dcv * cx, axis=0, keepdims=True), 0.0))
        gcw_ref[...] = gcw_ref[...] + gcw
        dp_ref[:, SEG_Q:SEG_Q + 512] = dq_ref[...]
        dp_ref[:, SEG_K:SEG_K + 512] = dk_ref[...]
        dp_ref[:, SEG_V:SEG_V + 512] = dv_ref[...]
        dp_ref[:, SEG_F:SEG_F + LANE] = d_f.astype(dp_ref.dtype)
        dp_ref[:, SEG_ZA:SEG_ZA + 512] = dza_ref[...]
        dp_ref[:, SEG_GB:SEG_GB + 512] = dgb_ref[...]
        dp_ref[:, SEG_GC:SEG_GC + 512] = (d_cx * xc).astype(dp_ref.dtype)
        dp_ref[:, SEG_XC:SEG_XC + 512] = (d_cx * gc).astype(dp_ref.dtype)
        dp_ref[:, SEG_ZC:SEG_ZC + 512] = dzc_ref[...]
        d_u = _dot(dp_ref[...], w_ref[...])
        first = jnp.where(i == 0, mb[...], x_refs[0][...])
        h = jnp.concatenate([first] + [r[...] for r in x_refs[1:]], axis=0)
        r1 = lax.rsqrt(jnp.mean(h * h, axis=-1, keepdims=True) + EPS)
        n_h = h * r1
        gn_ref[...] = gn_ref[...] + jnp.sum(d_u * n_h, axis=0, keepdims=True)
        dn = d_u * g_ref[...]
        d_h = dout_ref[...] + r1 * (dn - n_h * jnp.mean(dn * n_h, axis=-1, keepdims=True))
        slot = step % 2

        def to_grad_x(slot_, tile):
            return pltpu.make_async_copy(dh_scr.at[slot_], gx_ref.at[pl.ds(pl.multiple_of(tile * t - FRONT, SUBLANE), t)],
                                         gx_sems.at[slot_])

        @pl.when(step >= 2)
        def _():
            to_grad_x(slot, 1).wait()

        dh_scr[slot] = d_h

        @pl.when(i > 0)
        def _():
            to_grad_x(slot, i).start()

        @pl.when(i == 0)
        def _():
            front_ref[...] = d_h[:FRONT]
            rest_rows = pltpu.make_async_copy(dh_scr.at[slot, pl.ds(FRONT, t - FRONT)], gx_ref.at[pl.ds(0, t - FRONT)],
                                              gx_sems.at[slot])
            rest_rows.start()
            rest_rows.wait()
            if nt >= 2:
                to_grad_x(1 - slot, 1).wait()

    def x_specs():
        specs = [pl.BlockSpec((LANE, D_MODEL), lambda s: (jnp.maximum(n_sub * rev(s) - 1, 0), 0))]
        for b in range(1, n_sub):
            specs.append(pl.BlockSpec((LANE, D_MODEL), functools.partial(lambda s, b: (n_sub * rev(s) - 1 + b, 0), b=b)))
        return specs

    row_blk = lambda cols: pl.BlockSpec((t, cols), lambda s: (rev(s), 0))
    rest_blk = lambda k: pl.BlockSpec((t, 512), functools.partial(lambda s, k: (rev(s), k), k=k))
    halo_prev = lambda k: pl.BlockSpec(
        (SUBLANE, 512), functools.partial(lambda s, k: (jnp.maximum(rev(s) * hb - 1, 0), k), k=k))
    halo_next = pl.BlockSpec((SUBLANE, 512), lambda s: (jnp.minimum((rev(s) + 1) * hb, lp // SUBLANE - 1), 0))
    const = lambda shape: pl.BlockSpec(shape, lambda s: (0, 0))
    return pl.pallas_call(
        body, name="bwd_in", grid=(nt,),
        in_specs=x_specs() + [const((LANE, D_MODEL)), const((1, D_MODEL)),
                              pl.BlockSpec((D_IN_PAD, D_MODEL), lambda s: (0, 0), pipeline_mode=pl.Buffered(1)),
                              const((1, LANE)), row_blk(LANE),
                              pl.BlockSpec((HEADS, t), lambda s: (0, rev(s))),
                              row_blk(512), row_blk(512), row_blk(512), row_blk(512), row_blk(512), row_blk(512),
                              row_blk(512), halo_next, rest_blk(2), rest_blk(3), halo_prev(2), halo_prev(3),
                              row_blk(D_MODEL), const((SUBLANE, D_CONV)), const((t, t))],
        out_specs=(row_blk(D_IN_PAD), ANY, const((FRONT, D_MODEL)), const((1, D_MODEL)), const((1, LANE)),
                   const((SUBLANE, D_CONV))),
        out_shape=(jax.ShapeDtypeStruct((lp, D_IN_PAD), MXU_DTYPE), jax.ShapeDtypeStruct((lp - FRONT, D_MODEL), F32),
                   jax.ShapeDtypeStruct((FRONT, D_MODEL), F32),
                   jax.ShapeDtypeStruct((1, D_MODEL), F32), jax.ShapeDtypeStruct((1, LANE), F32),
                   jax.ShapeDtypeStruct((SUBLANE, D_CONV), F32)),
        scratch_shapes=[pltpu.VMEM((1, LANE), F32), pltpu.VMEM((2, t, D_MODEL), F32), pltpu.SemaphoreType.DMA((2,))],
        compiler_params=_params(("arbitrary",)),
    )(*([x2] * n_sub), meta_blk, norm_g, w_pad, bf_pad, fl, dc, dq, dk, dv, dza, dgb, dzc, dconv, dconv,
      rest, rest, rest, rest, d_out, conv_w8, _triangle(t, lower=False))


def _grad_w_in(u, dproj):
    lp = u.shape[0]
    tn = GW_COL_TILE
    tk = tn if lp % tn == 0 else ROW_TILE

    def body(d_ref, u_ref, o_ref, wire_ref):
        k = pl.program_id(1)

        @pl.when(k == 0)
        def _():
            o_ref[...] = jnp.zeros_like(o_ref)

        o_ref[...] = o_ref[...] + lax.dot_general(d_ref[...], u_ref[...], (((0,), (0,)), ((), ())),
                                                  preferred_element_type=F32)

        @pl.when(k == pl.num_programs(1) - 1)
        def _():
            wire_ref[...] = o_ref[...].astype(wire_ref.dtype)

    out_spec = pl.BlockSpec((tn, D_MODEL), lambda n, k: (n, 0))
    return pl.pallas_call(
        body, name="grad_w_in", grid=(D_IN_PAD // tn, lp // tk),
        in_specs=[pl.BlockSpec((tk, tn), lambda n, k: (k, n)), pl.BlockSpec((tk, D_MODEL), lambda n, k: (k, 0))],
        out_specs=(out_spec, out_spec),
        out_shape=(jax.ShapeDtypeStruct((D_IN_PAD, D_MODEL), F32), jax.ShapeDtypeStruct((D_IN_PAD, D_MODEL), WIRE_DTYPE)),
        compiler_params=_params(("parallel", "arbitrary")),
    )(dproj, u)


def _by_chip(own, others, me):
    by_mask = jnp.stack([own, others[1], others[0], others[2]])
    return [lax.dynamic_index_in_dim(by_mask, jnp.bitwise_xor(me, s), 0, keepdims=False) for s in range(N_CHIPS)]


def _both_halves(mine, other, c):
    return jnp.where(c == 0, jnp.concatenate([mine, other], axis=0), jnp.concatenate([other, mine], axis=0))


def _local_step(x2, tgt2, meta_full, norm_g, w_pad, b_f, conv_w_full, attn_g, conv_g, w_out_full, final_g):
    lp = x2.shape[0] + FRONT
    nt = lp // ROW_TILE
    meta_blk = jnp.concatenate([jnp.zeros((PAD_ROWS, D_MODEL), F32), meta_full], axis=0)
    bf_pad = jnp.pad(b_f, ((0, 0), (0, LANE - HEADS)))
    conv_w8 = jnp.pad(conv_w_full, ((0, SUBLANE - conv_w_full.shape[0]), (0, 0)))
    q, k, v, rest, fl, ct, u = _in_proj(x2, meta_blk, norm_g, w_pad, bf_pad)
    ct4 = ct.reshape(SUBLANE, nt, 1, ROW_TILE)
    o, l_sum, m_max = _attn_fwd(q, k, v, ct4)
    (d_out, d_o, delta, dza, dgb, dzc, dconv, loss, g_final, g_attn, g_convg, gw_out) = _post(
        o, l_sum, rest, x2, meta_blk, tgt2, w_out_full, attn_g, conv_g, final_g, conv_w8)
    by_tile_t = lambda a: a.reshape(nt, ROW_TILE, D_ATTN).transpose(0, 2, 1)
    dq, dk, dv, dc = _attn_bwd(q, k, v, d_o, by_tile_t(q), by_tile_t(d_o), m_max, delta, ct4)
    dproj, grad_x, d_front, g_norm, g_bf, g_cw = _bwd_in(x2, meta_blk, norm_g, w_pad, bf_pad, fl, dc.reshape(HEADS, lp), dq, dk, dv,
                                             dza, dgb, dzc, dconv, rest, d_out, conv_w8)
    gw_in, gw_in_wire = _grad_w_in(u, dproj)
    return dict(loss=loss, grad_x=grad_x, d_front=d_front, g_norm=g_norm, g_final=g_final, g_attn=g_attn, g_convg=g_convg, g_bf=g_bf,
                g_cw=g_cw, gw_out=gw_out, gw_in=gw_in, gw_in_wire=gw_in_wire)


def kernel(x, meta, norm_g, w_in, b_f, conv_w, attn_norm_g, conv_norm_g, w_out, final_norm_g, loss_target, m_meta, m_norm_g, m_w_in, m_b_f, m_conv_w, m_attn_norm_g, m_conv_norm_g, m_w_out, m_final_norm_g, v_meta, v_norm_g, v_w_in, v_b_f, v_conv_w, v_attn_norm_g, v_conv_norm_g, v_w_out, v_final_norm_g):
    cx_, cy_, cc_ = _position()
    chip = 2 * cx_ + cy_
    shard = w_in.shape[2]
    out_half = w_out.shape[1] // 2
    pick = lambda vals: jnp.where(chip == 0, vals[0], jnp.where(chip == 1, vals[1], jnp.where(chip == 2, vals[2], vals[3])))
    a_off, b_off = pick(A_OFF), pick(B_OFF)
    wt = jnp.transpose(w_in[0]).astype(MXU_DTYPE)
    wi = lax.dynamic_update_slice_in_dim(
        lax.dynamic_update_slice_in_dim(jnp.zeros((WIN_ROWS, D_MODEL), MXU_DTYPE), wt[:PIECE_A], a_off, 0),
        wt[PIECE_A:], b_off, 0)
    wo = w_out[0].astype(MXU_DTYPE)
    small = jnp.concatenate([meta, jnp.pad(conv_w[0], ((0, 8 - conv_w.shape[1]), (0, meta.shape[1] - conv_w.shape[2])))],
                            axis=0)
    gwi, gwo, gsm = _gather_weights(wi.reshape(2, WIN_HALF, D_MODEL), wo.reshape(2, out_half, D_MODEL), small)
    starts = jnp.stack([_window_start(jnp.bitwise_xor(chip, mask)) for mask in (0, 2, 1, 3)]).astype(jnp.int32)
    w_pad = _assemble_w(wi, gwi.reshape(3, WIN_ROWS, D_MODEL), starts)
    w_out_full = jnp.concatenate(_by_chip(wo, gwo.reshape(3, 2 * out_half, D_MODEL), chip), axis=0)
    small_full = jnp.concatenate(_by_chip(small, gsm, chip), axis=1)
    meta_full = small_full[:N_META]
    conv_w_full = jnp.concatenate([small_full[N_META:N_META + 3, 256 * s:256 * s + LANE] for s in range(N_CHIPS)], axis=1)
    final_g2 = final_norm_g.reshape(1, D_MODEL)
    r = _local_step(x[0], loss_target[0], meta_full, norm_g, w_pad, b_f, conv_w_full, attn_norm_g, conv_norm_g,
                    w_out_full, final_g2)
    grad_x = r["grad_x"][None]
    gb = r["gw_out"].reshape(N_CHIPS, 2, out_half, D_MODEL)
    wide = lambda a: jnp.pad(a, ((0, 0), (0, D_MODEL - a.shape[1])))
    pack = jnp.concatenate([
        r["g_norm"], r["g_final"], jnp.concatenate([r["g_attn"], r["g_convg"]], axis=1), wide(r["g_bf"]),
        wide(r["loss"]), jnp.zeros((3, D_MODEL), F32), r["d_front"][PAD_ROWS:], wide(r["g_cw"])], axis=0)
    ra, rb, packs = _pair_exchange(r["gw_in_wire"], gb, pack)
    c_idx = jnp.reshape(cc_, (1,)).astype(jnp.int32)
    chip_idx = jnp.reshape(chip, (1,)).astype(jnp.int32)
    pa, pa_wire = _pair_sum_windows(r["gw_in"], ra, c_idx)
    pb, pb_wire = _pair_sum(gb, rb, c_idx)
    xa, xb = _chip_exchange(pa_wire, pb_wire)
    ha = _chip_sum(pa, xa, chip_idx)
    hb = _chip_sum(pb, xb, chip_idx)
    oa, ob = _pair_share(ha, hb)
    g_window = _both_halves(ha, oa, cc_)
    g_w_in_t = jnp.concatenate([lax.dynamic_slice_in_dim(g_window, a_off, PIECE_A, 0),
                                lax.dynamic_slice_in_dim(g_window, b_off, shard - PIECE_A, 0)], axis=0)
    g_w_out = _both_halves(hb, ob, cc_)
    as_rows = lambda a: jnp.transpose(a, (2, 0, 1))
    g_w_in, d_w_in, nm_w_in, nv_w_in = (jnp.transpose(a, (1, 2, 0)) for a in _adamw_rows(
        as_rows(w_in), g_w_in_t, as_rows(m_w_in), as_rows(v_w_in)))
    d_w_out, nm_w_out, nv_w_out = (a[None] for a in _adamw_big(w_out[0], g_w_out, m_w_out[0], v_w_out[0], LANE))
    params = (norm_g, final_g2, attn_norm_g, conv_norm_g, b_f, meta, conv_w[0])
    ms = (m_norm_g, m_final_norm_g.reshape(1, D_MODEL), m_attn_norm_g, m_conv_norm_g, m_b_f, m_meta, m_conv_w[0])
    vs = (v_norm_g, v_final_norm_g.reshape(1, D_MODEL), v_attn_norm_g, v_conv_norm_g, v_b_f, v_meta, v_conv_w[0])
    loss, g_s, d_s, m_s, v_s = _small_update(pack, packs, params, ms, vs)

    def ordered(small_list, big_in, big_out):
        s_norm, s_final, s_attn, s_convg, s_bf, s_meta, s_cw = small_list
        return (s_meta, s_norm, big_in, s_bf, s_cw[None], s_attn, s_convg, big_out, s_final.reshape(D_MODEL))

    return (loss.reshape(()), grad_x,
            *ordered(g_s, g_w_in, g_w_out[None]), *ordered(d_s, d_w_in, d_w_out),
            *ordered(m_s, nm_w_in, nm_w_out), *ordered(v_s, nv_w_in, nv_w_out))
```

```python
import functools

import jax
import jax.numpy as jnp
from jax import lax
from jax.experimental import pallas as pl
from jax.experimental.pallas import tpu as pltpu

F32 = jnp.float32
MXU_DTYPE = jnp.bfloat16
WIRE_DTYPE = jnp.bfloat16

D_MODEL = 1024
N_META = 16
HEADS = 8
HEAD_DIM = 64
D_ATTN = HEADS * HEAD_DIM
D_CONV = 512
EPS = 1e-6
LANE = 128
SUBLANE = 8
ROW_TILE = 384
ATTN_UNROLL = 3
ATTN_BWD_QBLOCKS = 2
STAT_TERMS = 1
FRONT = LANE
PAD_ROWS = FRONT - N_META
NEG = -1e30
LOG2E = 1.4426950408889634
N_CHIPS = 4
N_DEV = 8
VMEM_LIMIT_BYTES = 60 * 1024 * 1024

SEG_Q, SEG_K, SEG_V, SEG_F, SEG_ZA, SEG_GB, SEG_GC, SEG_XC, SEG_ZC = (
    0, 512, 1024, 1536, 1664, 2176, 2688, 3200, 3712)
D_IN = 4104
D_IN_PAD = 4224
F_END = 1544
GW_COL_TILE = 1408
WIN_ROWS = 1152
WIN_HALF = WIN_ROWS // 2
WIN_START = (0, 1024, 2160, 3072)
PIECE_A = 518
A_OFF = (0, 2, 12, 126)
B_OFF = (518, 640, 530, 644)
ADAM_LR = 0.001
ADAM_B1 = 0.9
ADAM_B2 = 0.999
ADAM_EPS = 1e-08
ADAM_WD = 0.01
ADAM_STEP = 10

MESH = pl.DeviceIdType.MESH
ANY = pl.BlockSpec(memory_space=pl.ANY)

PACK_ROWS = 32
SLOT_NORM = (0, 1, 0, 1024)
SLOT_FINAL = (1, 2, 0, 1024)
SLOT_ATTN = (2, 3, 0, 512)
SLOT_CONVG = (2, 3, 512, 1024)
SLOT_BF = (3, 4, 0, 8)
SLOT_META = (8, 24, 0, 256)
SLOT_CONVW = (24, 27, 0, 128)
LOSS_ROW = 4


def _params(sem=None):
    return pltpu.CompilerParams(dimension_semantics=sem, vmem_limit_bytes=VMEM_LIMIT_BYTES)


def _sigmoid(z):
    return 1.0 / (1.0 + jnp.exp(-z))


def _dot(a, b):
    return jnp.dot(a, b, preferred_element_type=F32)


def _dot_nt(a, b):
    return lax.dot_general(a, b, (((1,), (1,)), ((), ())), preferred_element_type=F32)


def _dot_exact(ones, x):
    ones = ones.astype(MXU_DTYPE)
    total = None
    for _ in range(3):
        term = x.astype(MXU_DTYPE)
        x = x - term.astype(F32)
        total = _dot(ones, term) if total is None else total + _dot(ones, term)
    return total


def _group_matrix():
    r = lax.broadcasted_iota(jnp.int32, (D_ATTN, D_ATTN), 0) >> 6
    c = lax.broadcasted_iota(jnp.int32, (D_ATTN, D_ATTN), 1) >> 6
    return jnp.where(r == c, 1.0, 0.0).astype(MXU_DTYPE)


def _triangle(n, lower):
    r = lax.broadcasted_iota(jnp.int32, (n, n), 0)
    c = lax.broadcasted_iota(jnp.int32, (n, n), 1)
    return jnp.where((r >= c) if lower else (c >= r), 1.0, 0.0).astype(MXU_DTYPE)


def _group_sum(x, gmat, terms=2):
    hi = x.astype(MXU_DTYPE)
    if terms == 1:
        return _dot(hi, gmat)
    lo = (x - hi.astype(F32)).astype(MXU_DTYPE)
    return _dot(hi, gmat) + _dot(lo, gmat)


def _x_block_specs(n_sub, rows):
    specs = [pl.BlockSpec((rows, D_MODEL), lambda i: (jnp.maximum(n_sub * i - 1, 0), 0))]
    for b in range(1, n_sub):
        specs.append(pl.BlockSpec((rows, D_MODEL), functools.partial(lambda i, b: (n_sub * i - 1 + b, 0), b=b)))
    return specs


def _position():
    return lax.axis_index("x"), lax.axis_index("y"), lax.axis_index("c")


def _gather_weights(wi, wo, small):
    def body(wi_ref, wo_ref, sm_ref, gwi_ref, gwo_ref, gsm_ref, send_sems, recv_sems):
        x, y, c = _position()
        sibling = (x, y, 1 - c)
        chips = [(1 - x, y), (x, 1 - y), (1 - x, 1 - y)]

        def remote(k, src, dst, to):
            return pltpu.make_async_remote_copy(src_ref=src, dst_ref=dst, send_sem=send_sems.at[k],
                                                recv_sem=recv_sems.at[k], device_id=to, device_id_type=MESH)

        first, passed, landed = [], [], []
        for a, (src_ref, g_ref) in enumerate(((wi_ref, gwi_ref), (wo_ref, gwo_ref))):
            for j, (cx, cy) in enumerate(chips):
                slot = g_ref.at[j, c]
                first.append(remote(6 * a + j, src_ref.at[c], slot, (cx, cy, c)))
                landed.append(remote(6 * a + j, slot, slot, sibling))
                passed.append(remote(6 * a + 3 + j, slot, slot, sibling))
        for j, (cx, cy) in enumerate(chips):
            first.append(remote(12 + j, sm_ref, gsm_ref.at[j], (cx, cy, c)))
        for cp in first:
            cp.start()
        for arrived, onward in zip(landed, passed):
            arrived.wait_recv()
            onward.start()
        for a, g_ref in enumerate((gwi_ref, gwo_ref)):
            for j in range(3):
                remote(6 * a + 3 + j, g_ref.at[j, 1 - c], g_ref.at[j, 1 - c], sibling).wait_recv()
        for j in range(3):
            remote(12 + j, sm_ref, gsm_ref.at[j], sibling).wait_recv()
        for cp in first + passed:
            cp.wait_send()

    return pl.pallas_call(
        body, name="gather_weights",
        out_shape=(jax.ShapeDtypeStruct((3,) + wi.shape, wi.dtype), jax.ShapeDtypeStruct((3,) + wo.shape, wo.dtype),
                   jax.ShapeDtypeStruct((3,) + small.shape, small.dtype)),
        in_specs=[ANY, ANY, ANY], out_specs=(ANY, ANY, ANY),
        scratch_shapes=[pltpu.SemaphoreType.DMA((15,)), pltpu.SemaphoreType.DMA((15,))],
    )(wi, wo, small)


def _pair_exchange(gw, gb, pack):
    n_big = N_CHIPS + 1

    def body(gw_ref, gb_ref, p_ref, ra_ref, rb_ref, o_ref, send_sems, recv_sems):
        x, y, c = _position()
        sibling = (x, y, 1 - c)

        def remote(k, src, dst, to):
            return pltpu.make_async_remote_copy(src_ref=src, dst_ref=dst, send_sem=send_sems.at[k],
                                                recv_sem=recv_sems.at[k], device_id=to, device_id_type=MESH)

        copies = [remote(N_CHIPS, gb_ref.at[:, 1 - c], rb_ref, sibling)]
        for s, start in enumerate(WIN_START):
            rows = pl.ds(pl.multiple_of(start + WIN_HALF * (1 - c), 2 * SUBLANE), WIN_HALF)
            copies.append(remote(s, gw_ref.at[rows], ra_ref.at[s], sibling))
        for mask in range(1, N_DEV):
            peer = (1 - x if mask & 4 else x, 1 - y if mask & 2 else y, 1 - c if mask & 1 else c)
            copies.append(remote(n_big + mask - 1, p_ref, o_ref.at[mask - 1], peer))
        for cp in copies:
            cp.start()
        for cp in copies:
            cp.wait()

    n_sems = n_big + N_DEV - 1
    return pl.pallas_call(
        body, name="grad_pair_exchange",
        out_shape=(jax.ShapeDtypeStruct((N_CHIPS, WIN_HALF, D_MODEL), gw.dtype),
                   jax.ShapeDtypeStruct((N_CHIPS,) + gb.shape[2:], gb.dtype),
                   jax.ShapeDtypeStruct((N_DEV - 1,) + pack.shape, pack.dtype)),
        in_specs=[ANY, ANY, ANY], out_specs=(ANY, ANY, ANY),
        scratch_shapes=[pltpu.SemaphoreType.DMA((n_sems,)), pltpu.SemaphoreType.DMA((n_sems,))],
    )(gw, gb, pack)


def _chip_exchange(pa, pb):
    def body(pa_ref, pb_ref, ra_ref, rb_ref, send_sems, recv_sems):
        x, y, c = _position()
        chips = [(1 - x, y), (x, 1 - y), (1 - x, 1 - y)]
        copies = []
        for a, (src, dst) in enumerate(((pa_ref, ra_ref), (pb_ref, rb_ref))):
            for j, (cx, cy) in enumerate(chips):
                copies.append(pltpu.make_async_remote_copy(
                    src_ref=src.at[2 * cx + cy], dst_ref=dst.at[j], send_sem=send_sems.at[3 * a + j],
                    recv_sem=recv_sems.at[3 * a + j], device_id=(cx, cy, c), device_id_type=MESH))
        for cp in copies:
            cp.start()
        for cp in copies:
            cp.wait()

    return pl.pallas_call(
        body, name="grad_chip_exchange",
        out_shape=(jax.ShapeDtypeStruct((3,) + pa.shape[1:], pa.dtype),
                   jax.ShapeDtypeStruct((3,) + pb.shape[1:], pb.dtype)),
        in_specs=[ANY, ANY], out_specs=(ANY, ANY),
        scratch_shapes=[pltpu.SemaphoreType.DMA((6,)), pltpu.SemaphoreType.DMA((6,))],
    )(pa, pb)


def _pair_share(ha, hb):
    def body(ha_ref, hb_ref, oa_ref, ob_ref, send_sems, recv_sems):
        x, y, c = _position()
        copies = [pltpu.make_async_remote_copy(
            src_ref=src, dst_ref=dst, send_sem=send_sems.at[k], recv_sem=recv_sems.at[k],
            device_id=(x, y, 1 - c), device_id_type=MESH)
            for k, (src, dst) in enumerate(((ha_ref, oa_ref), (hb_ref, ob_ref)))]
        for cp in copies:
            cp.start()
        for cp in copies:
            cp.wait()

    return pl.pallas_call(
        body, name="grad_pair_share",
        out_shape=(jax.ShapeDtypeStruct(ha.shape, ha.dtype), jax.ShapeDtypeStruct(hb.shape, hb.dtype)),
        in_specs=[ANY, ANY], out_specs=(ANY, ANY),
        scratch_shapes=[pltpu.SemaphoreType.DMA((2,)), pltpu.SemaphoreType.DMA((2,))],
    )(ha, hb)


def _pair_sum(mine, recv, c_idx):
    rows, cols = mine.shape[2:]

    def body(c_ref, a_ref, b_ref, o_ref, send_ref):
        total = a_ref[...] + b_ref[...]
        o_ref[...] = total
        send_ref[...] = total.astype(send_ref.dtype)

    out_spec = pl.BlockSpec((None, rows, cols), lambda s, c_ref: (s, 0, 0))
    return pl.pallas_call(
        body, name="grad_pair_sum",
        grid_spec=pltpu.PrefetchScalarGridSpec(
            num_scalar_prefetch=1, grid=(N_CHIPS,),
            in_specs=[pl.BlockSpec((None, None, rows, cols), lambda s, c_ref: (s, c_ref[0], 0, 0)),
                      pl.BlockSpec((None, rows, cols), lambda s, c_ref: (s, 0, 0))],
            out_specs=(out_spec, out_spec)),
        out_shape=(jax.ShapeDtypeStruct(recv.shape, recv.dtype), jax.ShapeDtypeStruct(recv.shape, WIRE_DTYPE)),
        compiler_params=_params(("parallel",)),
    )(c_idx, mine, recv)


def _window_start(s):
    return jnp.where(s == 0, WIN_START[0], jnp.where(s == 1, WIN_START[1], jnp.where(s == 2, WIN_START[2], WIN_START[3])))


def _pair_sum_windows(gw, recv, c_idx):
    tr = WIN_HALF // 3

    def body(c_ref, a_ref, b_ref, o_ref, send_ref):
        total = a_ref[...] + b_ref[...].astype(F32)
        o_ref[...] = total
        send_ref[...] = total.astype(send_ref.dtype)

    out_spec = pl.BlockSpec((None, tr, D_MODEL), lambda s, i, c_ref: (s, i, 0))
    return pl.pallas_call(
        body, name="grad_pair_sum_windows",
        grid_spec=pltpu.PrefetchScalarGridSpec(
            num_scalar_prefetch=1, grid=(N_CHIPS, WIN_HALF // tr),
            in_specs=[pl.BlockSpec((pl.Element(tr), pl.Element(D_MODEL)),
                                   lambda s, i, c_ref: (pl.multiple_of(
                                       _window_start(s) + WIN_HALF * c_ref[0] + tr * i, SUBLANE), 0)),
                      pl.BlockSpec((None, tr, D_MODEL), lambda s, i, c_ref: (s, i, 0))],
            out_specs=(out_spec, out_spec)),
        out_shape=(jax.ShapeDtypeStruct(recv.shape, F32), jax.ShapeDtypeStruct(recv.shape, WIRE_DTYPE)),
        compiler_params=_params(("parallel", "parallel")),
    )(c_idx, gw, recv)


def _assemble_w(own, others, starts):
    def body(starts_ref, own_ref, oth_ref, o_ref):
        o_ref[...] = jnp.zeros_like(o_ref)
        for k in range(N_CHIPS):
            rows = pl.ds(pl.multiple_of(starts_ref[k], 2 * SUBLANE), WIN_ROWS)
            o_ref[rows, :] = o_ref[rows, :] + (own_ref[...] if k == 0 else oth_ref[k - 1])

    return pl.pallas_call(
        body, name="assemble_w",
        in_specs=[pl.BlockSpec(memory_space=pltpu.SMEM), pl.BlockSpec(memory_space=pltpu.VMEM),
                  pl.BlockSpec(memory_space=pltpu.VMEM)],
        out_specs=pl.BlockSpec(memory_space=pltpu.VMEM),
        out_shape=jax.ShapeDtypeStruct((D_IN_PAD, D_MODEL), own.dtype),
        compiler_params=_params(),
    )(starts, own, others)


def _chip_sum(psum, recv3, chip_idx):
    rows, cols = psum.shape[1:]
    tr = rows // 2

    def body(s_ref, p_ref, r0, r1, r2, o_ref):
        o_ref[...] = ((p_ref[...] + r0[...].astype(F32)) + r1[...].astype(F32)) + r2[...].astype(F32)

    return pl.pallas_call(
        body, name="grad_chip_sum",
        grid_spec=pltpu.PrefetchScalarGridSpec(
            num_scalar_prefetch=1, grid=(2,),
            in_specs=[pl.BlockSpec((None, tr, cols), lambda i, s_ref: (s_ref[0], i, 0))] +
                     [pl.BlockSpec((None, tr, cols), functools.partial(lambda i, s_ref, j: (j, i, 0), j=j))
                      for j in range(3)],
            out_specs=pl.BlockSpec((tr, cols), lambda i, s_ref: (i, 0))),
        out_shape=jax.ShapeDtypeStruct((rows, cols), psum.dtype),
        compiler_params=_params(("parallel",)),
    )(chip_idx, psum, recv3, recv3, recv3)


def _adamw_math(w, g, m, v):
    m = ADAM_B1 * m + (1.0 - ADAM_B1) * g
    v = ADAM_B2 * v + (1.0 - ADAM_B2) * (g * g)
    m_hat = m * (1.0 / (1.0 - ADAM_B1 ** ADAM_STEP))
    v_hat = v * (1.0 / (1.0 - ADAM_B2 ** ADAM_STEP))
    delta = -ADAM_LR * (m_hat / (jnp.sqrt(v_hat) + ADAM_EPS) + ADAM_WD * w)
    return delta, m, v


def _adamw_big(w, g, m, v, tr):
    rows, cols = w.shape
    assert rows % tr == 0 and g.shape[0] >= rows

    def body(w_ref, g_ref, m_ref, v_ref, d_out, m_out, v_out):
        d, m2, v2 = _adamw_math(w_ref[...], g_ref[...], m_ref[...], v_ref[...])
        d_out[...] = d
        m_out[...] = m2
        v_out[...] = v2

    spec = pl.BlockSpec((tr, cols), lambda i: (i, 0))
    sds = jax.ShapeDtypeStruct((rows, cols), F32)
    return pl.pallas_call(
        body, name="adamw_big", grid=(rows // tr,), in_specs=[spec] * 4, out_specs=(spec,) * 3,
        out_shape=(sds,) * 3, compiler_params=_params(("parallel",)),
    )(w, g, m, v)


def _adamw_rows(w3, g, m3, v3):
    rows, _, cols = w3.shape
    tc = 2 * LANE

    def body(w_ref, g_ref, m_ref, v_ref, g_out, d_out, m_out, v_out):
        g = g_ref[...]
        d, m2, v2 = _adamw_math(w_ref[:, 0, :], g, m_ref[:, 0, :], v_ref[:, 0, :])
        g_out[:, 0, :] = g
        d_out[:, 0, :] = d
        m_out[:, 0, :] = m2
        v_out[:, 0, :] = v2

    spec3 = pl.BlockSpec((rows, 1, tc), lambda i: (0, 0, i))
    sds = jax.ShapeDtypeStruct((rows, 1, cols), F32)
    return pl.pallas_call(
        body, name="adamw_rows", grid=(cols // tc,),
        in_specs=[spec3, pl.BlockSpec((rows, tc), lambda i: (0, i)), spec3, spec3], out_specs=(spec3,) * 4,
        out_shape=(sds,) * 4, compiler_params=_params(("parallel",)),
    )(w3, g, m3, v3)


def _small_update(own, others, params, ms, vs):
    slots = (SLOT_NORM, SLOT_FINAL, SLOT_ATTN, SLOT_CONVG, SLOT_BF, SLOT_META, SLOT_CONVW)
    n = len(slots)

    def body(*refs):
        own_ref, gp_ref = refs[:2]
        w_refs, m_refs, v_refs = refs[2:2 + n], refs[2 + n:2 + 2 * n], refs[2 + 2 * n:2 + 3 * n]
        outs = refs[2 + 3 * n:3 + 7 * n]
        loss_ref = outs[0]
        g_outs, d_outs, m_outs, v_outs = (outs[1 + k * n:1 + (k + 1) * n] for k in range(4))
        g_scr, w_scr, m_scr, v_scr = refs[3 + 7 * n:]
        x, y, c = _position()
        shard = 2 * x + y
        me = 4 * x + 2 * y + c
        tot = None
        for d in range(N_DEV):
            rel = jnp.bitwise_xor(me, d)
            term = jnp.where(rel == 0, own_ref[...], gp_ref[jnp.maximum(rel, 1) - 1])
            tot = term if tot is None else tot + term
        r0, r1, _, _ = SLOT_META
        meta_sel = tot[r0:r1, 0:256]
        cw_sel = tot[24:32, 0:128]
        for k in range(1, N_CHIPS):
            meta_sel = jnp.where(shard == k, tot[r0:r1, 256 * k:256 * (k + 1)], meta_sel)
            cw_sel = jnp.where(shard == k, tot[24:32, 128 * k:128 * (k + 1)], cw_sel)
        zeros = jnp.zeros((PACK_ROWS, D_MODEL), F32)
        for scr in (g_scr, w_scr, m_scr, v_scr):
            scr[...] = zeros
        g_scr[0:8, :] = tot[0:8, :]
        g_scr[r0:r1, 0:256] = meta_sel
        g_scr[24:32, 0:128] = cw_sel
        for (a, b, c0, c1), w_ref, m_ref, v_ref in zip(slots, w_refs, m_refs, v_refs):
            w_scr[a:b, c0:c1] = w_ref[...]
            m_scr[a:b, c0:c1] = m_ref[...]
            v_scr[a:b, c0:c1] = v_ref[...]
        loss_ref[...] = g_scr[LOSS_ROW:LOSS_ROW + 1, 0:1]
        d, m2, v2 = _adamw_math(w_scr[...], g_scr[...], m_scr[...], v_scr[...])
        w_scr[...] = d
        m_scr[...] = m2
        v_scr[...] = v2
        for (a, b, c0, c1), g_o, d_o, m_o, v_o in zip(slots, g_outs, d_outs, m_outs, v_outs):
            g_o[...] = g_scr[a:b, c0:c1]
            d_o[...] = w_scr[a:b, c0:c1]
            m_o[...] = m_scr[a:b, c0:c1]
            v_o[...] = v_scr[a:b, c0:c1]

    shapes = [jax.ShapeDtypeStruct(p.shape, F32) for p in params]
    out = pl.pallas_call(
        body, name="small_update",
        out_shape=[jax.ShapeDtypeStruct((1, 1), F32)] + shapes * 4,
        scratch_shapes=[pltpu.VMEM((PACK_ROWS, D_MODEL), F32)] * 4,
        compiler_params=_params(),
    )(own, others, *params, *ms, *vs)
    return out[0], out[1:1 + n], out[1 + n:1 + 2 * n], out[1 + 2 * n:1 + 3 * n], out[1 + 3 * n:1 + 4 * n]


def _in_proj(x2, meta_blk, norm_g, w_pad, bf_pad):
    seq = x2.shape[0]
    lp = seq + FRONT
    t = ROW_TILE
    nt = lp // t
    n_sub = t // LANE

    def body(*refs):
        x_refs = refs[:n_sub]
        mb, g_ref, w_ref, bf_ref, tri_ref = refs[n_sub:n_sub + 5]
        q_ref, k_ref, v_ref, rest_ref, fl_ref, ct_ref, u_ref, qt_ref, vt_ref, cc_ref, carry = refs[n_sub + 5:]
        i = pl.program_id(0)

        @pl.when(i == 0)
        def _():
            carry[...] = jnp.zeros_like(carry)

        first = jnp.where(i == 0, mb[...], x_refs[0][...])
        h = jnp.concatenate([first] + [r[...] for r in x_refs[1:]], axis=0)
        ms = jnp.mean(h * h, axis=-1, keepdims=True)
        u = ((h * lax.rsqrt(ms + EPS)) * g_ref[...]).astype(MXU_DTYPE)
        u_ref[...] = u

        def seg(a, width):
            return _dot_nt(u, w_ref[a:a + width, :])

        q_tile = seg(SEG_Q, D_ATTN) * (HEAD_DIM ** -0.5)
        q_ref[...] = q_tile.astype(MXU_DTYPE)
        qt_ref[...] = q_tile.T.astype(MXU_DTYPE)
        k_ref[...] = seg(SEG_K, D_ATTN).astype(MXU_DTYPE)
        v_tile = seg(SEG_V, D_ATTN)
        v_ref[...] = v_tile.astype(MXU_DTYPE)
        vt_ref[...] = v_tile.T.astype(MXU_DTYPE)
        for s in range(5):
            rest_ref[:, 512 * s:512 * (s + 1)] = seg(SEG_ZA + 512 * s, 512)
        fl = seg(SEG_F, LANE)
        fl_ref[...] = fl
        z = fl + bf_ref[...]
        logf = jnp.minimum(z, 0.0) - jnp.log(1.0 + jnp.exp(-jnp.abs(z)))
        row = i * t + lax.broadcasted_iota(jnp.int32, (t, LANE), 0)
        logf = jnp.where(row >= PAD_ROWS, logf, 0.0)
        cs = _dot_exact(tri_ref[...], logf) + carry[...]
        carry[...] = carry[...] + jnp.sum(logf, axis=0, keepdims=True)
        col = i * t + lax.broadcasted_iota(jnp.int32, (SUBLANE, t), 1)
        ct_ref[...] = jnp.where(col >= PAD_ROWS, cs.T[0:SUBLANE, :], -NEG)
        cc_ref[...] = jnp.where(row >= PAD_ROWS, cs, -NEG)

    row_blk = lambda cols: pl.BlockSpec((t, cols), lambda i: (i, 0))
    tr_blk = pl.BlockSpec((None, D_ATTN, t), lambda i: (i, 0, 0))
    const = lambda shape: pl.BlockSpec(shape, lambda i: (0, 0))
    return pl.pallas_call(
        body, name="in_proj", grid=(nt,),
        in_specs=_x_block_specs(n_sub, LANE) + [const((LANE, D_MODEL)), const((1, D_MODEL)),
                                                pl.BlockSpec((D_IN_PAD, D_MODEL), lambda i: (0, 0),
                                                             pipeline_mode=pl.Buffered(1)),
                                                const((1, LANE)), const((t, t))],
        out_specs=(row_blk(D_ATTN), row_blk(D_ATTN), row_blk(D_ATTN), row_blk(5 * 512), row_blk(LANE),
                   pl.BlockSpec((SUBLANE, t), lambda i: (0, i)), row_blk(D_MODEL), tr_blk, tr_blk, row_blk(LANE)),
        out_shape=(jax.ShapeDtypeStruct((lp, D_ATTN), MXU_DTYPE), jax.ShapeDtypeStruct((lp, D_ATTN), MXU_DTYPE),
                   jax.ShapeDtypeStruct((lp, D_ATTN), MXU_DTYPE), jax.ShapeDtypeStruct((lp, 5 * 512), F32),
                   jax.ShapeDtypeStruct((lp, LANE), F32),
                   jax.ShapeDtypeStruct((SUBLANE, lp), F32), jax.ShapeDtypeStruct((lp, D_MODEL), MXU_DTYPE),
                   jax.ShapeDtypeStruct((nt, D_ATTN, t), MXU_DTYPE), jax.ShapeDtypeStruct((nt, D_ATTN, t), MXU_DTYPE),
                   jax.ShapeDtypeStruct((lp, LANE), F32)),
        scratch_shapes=[pltpu.VMEM((1, LANE), F32)],
        compiler_params=_params(("arbitrary",)),
    )(*([x2] * n_sub), meta_blk, norm_g, w_pad, bf_pad, _triangle(t, lower=True))


def _head_masks():
    lane = lax.broadcasted_iota(jnp.int32, (1, LANE), 1)
    return lane < HEAD_DIM, lane >= HEAD_DIM


def _pair_specs(lp, nt, t):
    blk = pl.BlockSpec((lp, LANE), lambda g: (0, g))
    ct_a = pl.BlockSpec((None, nt, 1, t), lambda g: (2 * g, 0, 0, 0))
    ct_b = pl.BlockSpec((None, nt, 1, t), lambda g: (2 * g + 1, 0, 0, 0))
    return blk, ct_a, ct_b


def _sub_rows(s, col):
    return jnp.concatenate([s[:, a * LANE:(a + 1) * LANE] - col for a in range(s.shape[1] // LANE)], axis=1)


def _loop_unrolled(lo, hi, step, init, n):
    def group(jj, carry):
        for k in range(n):
            carry = step(lo + n * jj + k, carry)
        return carry

    groups = (hi - lo) // n
    carry = lax.fori_loop(0, groups, group, init)
    return lax.fori_loop(lo + n * groups, hi, step, carry)


def _attn_fwd(q, k, v_t, cc):
    lp = q.shape[0]
    t = ROW_TILE
    nt = lp // t
    ext = LANE + 2 * SUBLANE

    def body(q_ref, k_ref, vt_ref, cc_ref, o_ref, l_ref, m_ref, s_scr, last_scr, m_scr, mfin_scr, acc_scr, c_scr):
        masks = _head_masks()
        lane = lax.broadcasted_iota(jnp.int32, (1, LANE), 1)
        for hh in range(2):
            picked = jnp.where(lane == 2 * pl.program_id(0) + hh, cc_ref[...], 0.0)
            c_scr[hh] = jnp.broadcast_to(jnp.sum(picked, axis=-1, keepdims=True), (lp, LANE))
        visible = lax.broadcasted_iota(jnp.int32, (t, t), 0) <= lax.broadcasted_iota(jnp.int32, (t, t), 1)
        top = lax.broadcasted_iota(jnp.int32, (LANE, 1), 0) < HEAD_DIM
        second_head = (lax.broadcasted_iota(jnp.int32, (2 * SUBLANE, 2 * t), 1) >= t).astype(jnp.int32)
        ones_rows = jnp.where(lax.broadcasted_iota(jnp.int32, (2 * SUBLANE, 2 * t), 0) == second_head,
                              1.0, 0.0).astype(MXU_DTYPE)

        on_first_diagonal = jnp.concatenate([visible, jnp.ones((t, t), jnp.bool_)], axis=1)

        def scores(j, queries):
            kj = k_ref[pl.ds(pl.multiple_of(j * t, t), t), :]
            return _dot_nt(jnp.concatenate([jnp.where(hm, kj, 0).astype(MXU_DTYPE) for hm in masks], axis=0), queries)

        def biased(s2, j, hh):
            return _sub_rows(s2[hh * t:(hh + 1) * t, :], c_scr[hh, pl.ds(pl.multiple_of(j * t, t), t), :]) * LOG2E

        def track_max(hh, s, lo, hi):
            m = m_scr[hh, :, lo:hi]
            for a in range(t // SUBLANE):
                m = jnp.maximum(m, s[a * SUBLANE:(a + 1) * SUBLANE, :])
            m_scr[hh, :, lo:hi] = m

        def probabilities(scores_of, ms_cols):
            return jnp.concatenate([jnp.exp2(scores_of(hh) - ms_cols[hh]).astype(MXU_DTYPE) for hh in range(2)], axis=0)

        def values(j):
            vtj = vt_ref[j]
            v2 = jnp.concatenate([jnp.where(top, vtj, 0).astype(MXU_DTYPE),
                                  jnp.where(top, 0, vtj).astype(MXU_DTYPE)], axis=1)
            return jnp.concatenate([v2, ones_rows], axis=0)

        def stage(done, ahead):
            if ahead is not None:
                i_a, rows_a = ahead
                qa = q_ref[pl.ds(pl.multiple_of(i_a * t, t), rows_a), :]
                m_scr[...] = jnp.full(m_scr.shape, NEG, F32)

                def max_step(j, mask=None):
                    s2 = scores(j, qa)
                    for hh in range(2):
                        s = biased(s2, j, hh)
                        if mask is not None:
                            s = jnp.where(mask, s, NEG)
                        s_scr[j, hh * t:(hh + 1) * t, 0:rows_a] = s
                        track_max(hh, s, 0, rows_a)

            if done is not None:
                i_d, rows_d = done
                r0 = pl.multiple_of(i_d * t, t)
                ms = [mfin_scr[hh, 0:1, 0:rows_d] for hh in range(2)]
                acc_scr[...] = jnp.zeros(acc_scr.shape, F32)

                def key_step(j, carry):
                    p = probabilities(lambda hh: s_scr[j, hh * t:(hh + 1) * t, 0:rows_d], ms)
                    acc_scr[:, 0:rows_d] = acc_scr[:, 0:rows_d] + _dot(values(j), p)
                    if ahead is not None:
                        max_step(j)
                    return carry

                _loop_unrolled(0, i_d + 1, key_step, 0, ATTN_UNROLL)
                if rows_d == 2 * t:
                    p = probabilities(lambda hh: last_scr[hh * t:(hh + 1) * t, :], [m[:, t:] for m in ms])
                    acc_scr[:, t:rows_d] = acc_scr[:, t:rows_d] + _dot(values(i_d + 1), p)
                acc = acc_scr[:, 0:rows_d]
                l_pair = jnp.where(top, acc[LANE:LANE + 1], acc[LANE + 1:LANE + 2])
                o_ref[pl.ds(r0, rows_d), :] = (acc[:LANE] / l_pair).T
                l_ref[pl.ds(r0, rows_d), :] = l_pair.T
                for hh in range(2):
                    m_ref[pl.ds(r0, rows_d), hh * LANE:(hh + 1) * LANE] = jnp.broadcast_to(ms[hh], (LANE, rows_d)).T

            if ahead is not None:
                if done is not None:
                    max_step(i_a - 1)
                max_step(i_a, on_first_diagonal if rows_a == 2 * t else visible)
                if rows_a == 2 * t:
                    s2 = scores(i_a + 1, qa[t:])
                    for hh in range(2):
                        s = jnp.where(visible, biased(s2, i_a + 1, hh), NEG)
                        last_scr[hh * t:(hh + 1) * t, :] = s
                        track_max(hh, s, t, rows_a)
                for hh in range(2):
                    mfin_scr[hh, :, 0:rows_a] = jnp.broadcast_to(jnp.max(m_scr[hh, :, 0:rows_a], axis=0, keepdims=True),
                                                                 (SUBLANE, rows_a))

        pairs = nt // 2
        stage(None, (0, 2 * t))

        def pair_to_pair(u, _):
            stage((2 * u, 2 * t), (2 * u + 2, 2 * t))
            return 0

        lax.fori_loop(0, pairs - 1, pair_to_pair, 0)
        if nt % 2:
            stage((2 * pairs - 2, 2 * t), (nt - 1, t))
            stage((nt - 1, t), None)
        else:
            stage((2 * pairs - 2, 2 * t), None)

    blk = pl.BlockSpec((lp, LANE), lambda g: (0, g))
    return pl.pallas_call(
        body, name="attn_fwd", grid=(HEADS // 2,),
        in_specs=[blk, blk, pl.BlockSpec((nt, LANE, t), lambda g: (0, g, 0)),
                  pl.BlockSpec((lp, LANE), lambda g: (0, 0), pipeline_mode=pl.Buffered(1))],
        out_specs=(blk, blk, pl.BlockSpec((lp, 2 * LANE), lambda g: (0, g))),
        out_shape=(jax.ShapeDtypeStruct((lp, D_ATTN), F32), jax.ShapeDtypeStruct((lp, D_ATTN), F32),
                   jax.ShapeDtypeStruct((lp, HEADS * LANE), F32)),
        scratch_shapes=[pltpu.VMEM((nt, 2 * t, 2 * t), F32), pltpu.VMEM((2 * t, t), F32),
                        pltpu.VMEM((2, SUBLANE, 2 * t), F32), pltpu.VMEM((2, SUBLANE, 2 * t), F32),
                        pltpu.VMEM((ext, 2 * t), F32), pltpu.VMEM((2, lp, LANE), F32)],
        compiler_params=_params(("parallel",)),
    )(q, k, v_t, cc)


def _attn_bwd(q, k, v, do, q_t, do_t, m, delta, ct4):
    lp = q.shape[0]
    t = ROW_TILE
    nt = lp // t

    def body(q_ref, k_ref, v_ref, do_ref, qt_ref, dot_ref, ma_ref, mb_ref, dla_ref, dlb_ref, cta_ref, ctb_ref,
             dq_ref, dk_ref, dv_ref, dc_ref, dq_acc, dk_acc, dv_acc):
        masks = _head_masks()
        ct_refs, m_refs, dl_refs = (cta_ref, ctb_ref), (ma_ref, mb_ref), (dla_ref, dlb_ref)
        below = lax.broadcasted_iota(jnp.int32, (t, t), 1) <= lax.broadcasted_iota(jnp.int32, (t, t), 0)
        dq_acc[...] = jnp.zeros_like(dq_acc)

        def k_block(j, _):
            c0 = pl.multiple_of(j * t, t)
            kj = k_ref[pl.ds(c0, t), :]
            vj = v_ref[pl.ds(c0, t), :]
            k2 = jnp.concatenate([jnp.where(hm, kj, 0).astype(MXU_DTYPE) for hm in masks], axis=0)
            v2 = jnp.concatenate([jnp.where(hm, vj, 0).astype(MXU_DTYPE) for hm in masks], axis=0)
            ck = [r[j] for r in ct_refs]
            dk_acc[...] = jnp.zeros_like(dk_acc)
            dv_acc[...] = jnp.zeros_like(dv_acc)

            def q_block(i, colsums, diagonal, rows=t):
                r0 = pl.multiple_of(i * t, t)
                qi = q_ref[pl.ds(r0, rows), :]
                doi = do_ref[pl.ds(r0, rows), :]
                qti = jnp.concatenate([qt_ref[i + b] for b in range(rows // t)], axis=1)
                doti = jnp.concatenate([dot_ref[i + b] for b in range(rows // t)], axis=1)
                s2 = _dot_nt(qi, k2)
                dp2 = _dot_nt(doi, v2)
                out, ps, dss = [], [], []
                for hh in range(2):
                    s = (s2[:, hh * t:(hh + 1) * t] - ck[hh]) * LOG2E
                    if diagonal:
                        s = jnp.where(below, s, NEG)
                    p = jnp.exp2(_sub_rows(s, m_refs[hh][pl.ds(r0, rows), :])).astype(MXU_DTYPE)
                    ds32 = p.astype(F32) * _sub_rows(dp2[:, hh * t:(hh + 1) * t], dl_refs[hh][pl.ds(r0, rows), :])
                    ps.append(p)
                    dss.append(ds32.astype(MXU_DTYPE))
                    out.append(colsums[hh] + jnp.sum(ds32, axis=0, keepdims=True))
                ds_cat = jnp.concatenate(dss, axis=1)
                dv_acc[...] = dv_acc[...] + _dot(doti, jnp.concatenate(ps, axis=1))
                dk_acc[...] = dk_acc[...] + _dot(qti, ds_cat)
                dq_acc[pl.ds(r0, rows), :] = dq_acc[pl.ds(r0, rows), :] + _dot(ds_cat, k2)
                return tuple(out)

            colsums = q_block(j, (jnp.zeros((1, t), F32), jnp.zeros((1, t), F32)), True)
            nq = ATTN_BWD_QBLOCKS
            groups = (nt - 1 - j) // nq
            colsums = lax.fori_loop(0, groups, lambda p, c: q_block(j + 1 + nq * p, c, False, nq * t), colsums)
            colsums = lax.fori_loop(j + 1 + nq * groups, nt, functools.partial(q_block, diagonal=False), colsums)
            for hh in range(2):
                dc_ref[hh, j] = -colsums[hh]
            own = lambda acc: jnp.concatenate([acc[:HEAD_DIM, :t], acc[HEAD_DIM:, t:]], axis=0).T
            dk_ref[pl.ds(c0, t), :] = own(dk_acc[...]).astype(dk_ref.dtype)
            dv_ref[pl.ds(c0, t), :] = own(dv_acc[...]).astype(dv_ref.dtype)
            return 0

        lax.fori_loop(0, nt, k_block, 0)
        dq_ref[...] = (dq_acc[...] * (HEAD_DIM ** -0.5)).astype(dq_ref.dtype)

    blk, ct_a, ct_b = _pair_specs(lp, nt, t)
    rep_a = pl.BlockSpec((lp, LANE), lambda g: (0, 2 * g))
    rep_b = pl.BlockSpec((lp, LANE), lambda g: (0, 2 * g + 1))
    tr_blk = pl.BlockSpec((nt, LANE, t), lambda g: (0, g, 0))
    return pl.pallas_call(
        body, name="attn_bwd", grid=(HEADS // 2,),
        in_specs=[blk] * 4 + [tr_blk, tr_blk, rep_a, rep_b, rep_a, rep_b, ct_a, ct_b],
        out_specs=(blk, blk, blk, pl.BlockSpec((2, nt, 1, t), lambda g: (g, 0, 0, 0))),
        out_shape=(jax.ShapeDtypeStruct((lp, D_ATTN), MXU_DTYPE),) * 3
                  + (jax.ShapeDtypeStruct((HEADS, nt, 1, t), F32),),
        scratch_shapes=[pltpu.VMEM((lp, LANE), F32), pltpu.VMEM((LANE, 2 * t), F32), pltpu.VMEM((LANE, 2 * t), F32)],
        compiler_params=_params(("parallel",)),
    )(q, k, v, do, q_t, do_t, m, m, delta, delta, ct4, ct4)


def _shift_down(prev8, cur, k):
    ext = jnp.concatenate([prev8, cur], axis=0)
    return pltpu.roll(ext, k, 0)[SUBLANE:, :]


def _shift_up(cur, next8, k):
    ext = jnp.concatenate([cur, next8], axis=0)
    n = ext.shape[0]
    return pltpu.roll(ext, n - k, 0)[:cur.shape[0], :]


def _post(o, l_sum, rest, x2, meta_blk, tgt2, w_out, attn_g, conv_g, final_g, conv_w8):
    lp = o.shape[0]
    t = ROW_TILE
    nt = lp // t
    n_sub = t // LANE
    hb = t // SUBLANE

    def body(*refs):
        o_ref, l_ref, za_ref, gb_ref, gc_ref, xc_ref, zc_ref, gch_ref, xch_ref = refs[:9]
        x_refs = refs[9:9 + n_sub]
        mb = refs[9 + n_sub]
        t_refs = refs[10 + n_sub:10 + 2 * n_sub]
        wo_ref, ag_ref, cg_ref, fg_ref, cw_ref, gm_ref, hr_ref = refs[10 + 2 * n_sub:17 + 2 * n_sub]
        (dout_ref, do_ref, dot_ref, dl_ref, dza_ref, dgb_ref, dzc_ref, dcv_ref,
         loss_ref, gf_ref, gag_ref, gcg_ref, gwo_ref) = refs[17 + 2 * n_sub:]
        i = pl.program_id(0)

        @pl.when(i == 0)
        def _():
            for r in (loss_ref, gf_ref, gag_ref, gcg_ref, gwo_ref):
                r[...] = jnp.zeros_like(r)

        gmat = gm_ref[...]
        inv_g = 1.0 / HEAD_DIM
        o_v = o_ref[...]
        ra = lax.rsqrt(_group_sum(o_v * o_v, gmat, STAT_TERMS) * inv_g + EPS)
        n_a = o_v * ra
        a_n = n_a * ag_ref[...]
        za = za_ref[...]
        sig_a = _sigmoid(za)
        sz_a = za * sig_a
        y_a = a_n * sz_a
        gb = gb_ref[...]
        gc = gc_ref[...]
        xc = xc_ref[...]
        cx = gc * xc
        cx_prev = jnp.where(i == 0, 0.0, gch_ref[...] * xch_ref[...])
        conv = (cw_ref[0:1, :] * _shift_down(cx_prev, cx, 2) + cw_ref[1:2, :] * _shift_down(cx_prev, cx, 1)
                + cw_ref[2:3, :] * cx)
        e = gb * conv
        re = lax.rsqrt(_group_sum(e * e, gmat, STAT_TERMS) * inv_g + EPS)
        n_e = e * re
        e_n = n_e * cg_ref[...]
        zc = zc_ref[...]
        sig_c = _sigmoid(zc)
        sz_c = zc * sig_c
        y_c = e_n * sz_c
        mix = jnp.concatenate([y_a, y_c], axis=-1)
        mix_b = mix.astype(MXU_DTYPE)
        first = jnp.where(i == 0, mb[...], x_refs[0][...])
        h = jnp.concatenate([first] + [r[...] for r in x_refs[1:]], axis=0)
        out = h + _dot(mix_b, wo_ref[...])
        r2 = lax.rsqrt(jnp.mean(out * out, axis=-1, keepdims=True) + EPS)
        n_f = out * r2
        y = n_f * fg_ref[...]
        tgt = jnp.concatenate([r[...] for r in t_refs], axis=0)
        valid = (i * t + lax.broadcasted_iota(jnp.int32, (t, 1), 0)) >= FRONT
        diff = jnp.where(valid, y - tgt, 0.0)
        loss_ref[...] = loss_ref[...] + 0.5 * jnp.sum(jnp.sum(diff * diff, axis=-1, keepdims=True) * (1.0 / D_MODEL))
        dy = diff * (1.0 / D_MODEL)
        gf_ref[...] = gf_ref[...] + jnp.sum(dy * n_f, axis=0, keepdims=True)
        dn = dy * fg_ref[...]
        d_out = r2 * (dn - n_f * jnp.mean(dn * n_f, axis=-1, keepdims=True))
        dout_ref[...] = d_out
        d_out_b = d_out.astype(MXU_DTYPE)
        d_mix = _dot_nt(d_out_b, wo_ref[...])
        gwo_ref[...] = gwo_ref[...] + _dot(mix.T.astype(MXU_DTYPE), d_out_b)
        d_ya = d_mix[:, :D_ATTN]
        d_yc = d_mix[:, D_ATTN:]
        d_an = d_ya * sz_a
        dza_ref[...] = (d_ya * a_n * (sig_a * (1.0 + za * (1.0 - sig_a)))).astype(dza_ref.dtype)
        gag_ref[...] = gag_ref[...] + jnp.sum(d_an * n_a, axis=0, keepdims=True)
        dn_a = d_an * ag_ref[...]
        d_o = ra * (dn_a - n_a * (_group_sum(dn_a * n_a, gmat, STAT_TERMS) * inv_g))
        d_o_l = d_o / l_ref[...]
        d_o_b = d_o_l.astype(do_ref.dtype)
        do_ref[...] = d_o_b
        dot_ref[...] = d_o_l.T.astype(dot_ref.dtype)
        dl_ref[...] = _group_sum(d_o_b.astype(F32) * o_v, hr_ref[...])
        d_en = d_yc * sz_c
        dzc_ref[...] = (d_yc * e_n * (sig_c * (1.0 + zc * (1.0 - sig_c)))).astype(dzc_ref.dtype)
        gcg_ref[...] = gcg_ref[...] + jnp.sum(d_en * n_e, axis=0, keepdims=True)
        dn_e = d_en * cg_ref[...]
        d_e = re * (dn_e - n_e * (_group_sum(dn_e * n_e, gmat, STAT_TERMS) * inv_g))
        dgb_ref[...] = (d_e * conv).astype(dgb_ref.dtype)
        dcv_ref[...] = d_e * gb

    head_rep = jnp.where((lax.broadcasted_iota(jnp.int32, (D_ATTN, HEADS * LANE), 0) >> 6)
                         == (lax.broadcasted_iota(jnp.int32, (D_ATTN, HEADS * LANE), 1) >> 7), 1.0, 0.0).astype(MXU_DTYPE)
    row_blk = lambda cols: pl.BlockSpec((t, cols), lambda i: (i, 0))
    rest_blk = lambda s: pl.BlockSpec((t, 512), functools.partial(lambda i, s: (i, s), s=s))
    halo = lambda s: pl.BlockSpec((SUBLANE, 512), functools.partial(lambda i, s: (jnp.maximum(i * hb - 1, 0), s), s=s))
    const = lambda shape: pl.BlockSpec(shape, lambda i: (0, 0))
    acc = lambda shape: pl.BlockSpec(shape, lambda i: (0, 0))
    return pl.pallas_call(
        body, name="post_fwd_bwd", grid=(nt,),
        in_specs=[row_blk(D_ATTN), row_blk(D_ATTN)] + [rest_blk(s) for s in range(5)] + [halo(2), halo(3)]
                 + _x_block_specs(n_sub, LANE) + [const((LANE, D_MODEL))] + _x_block_specs(n_sub, LANE)
                 + [const((D_MODEL, D_MODEL)), const((1, D_ATTN)), const((1, D_CONV)), const((1, D_MODEL)),
                    const((SUBLANE, D_CONV)), const((D_ATTN, D_ATTN)), const((D_ATTN, HEADS * LANE))],
        out_specs=(row_blk(D_MODEL), row_blk(D_ATTN), pl.BlockSpec((None, D_ATTN, t), lambda i: (i, 0, 0)),
                   row_blk(HEADS * LANE), row_blk(D_ATTN), row_blk(D_CONV),
                   row_blk(D_CONV), row_blk(D_CONV),
                   acc((1, LANE)), acc((1, D_MODEL)), acc((1, D_ATTN)), acc((1, D_CONV)), acc((D_MODEL, D_MODEL))),
        out_shape=(jax.ShapeDtypeStruct((lp, D_MODEL), F32), jax.ShapeDtypeStruct((lp, D_ATTN), MXU_DTYPE),
                   jax.ShapeDtypeStruct((nt, D_ATTN, t), MXU_DTYPE), jax.ShapeDtypeStruct((lp, HEADS * LANE), F32), jax.ShapeDtypeStruct((lp, D_ATTN), MXU_DTYPE),
                   jax.ShapeDtypeStruct((lp, D_CONV), MXU_DTYPE), jax.ShapeDtypeStruct((lp, D_CONV), MXU_DTYPE),
                   jax.ShapeDtypeStruct((lp, D_CONV), F32),
                   jax.ShapeDtypeStruct((1, LANE), F32), jax.ShapeDtypeStruct((1, D_MODEL), F32),
                   jax.ShapeDtypeStruct((1, D_ATTN), F32), jax.ShapeDtypeStruct((1, D_CONV), F32),
                   jax.ShapeDtypeStruct((D_MODEL, D_MODEL), F32)),
        compiler_params=_params(("arbitrary",)),
    )(o, l_sum, *([rest] * 5), rest, rest, *([x2] * n_sub), meta_blk, *([tgt2] * n_sub),
      w_out, attn_g, conv_g, final_g, conv_w8, _group_matrix(), head_rep)


def _bwd_in(x2, meta_blk, norm_g, w_pad, bf_pad, fl, dc, dq, dk, dv, dza, dgb, dzc, dconv, rest, d_out, conv_w8):
    lp = fl.shape[0]
    t = ROW_TILE
    nt = lp // t
    n_sub = t // LANE
    hb = t // SUBLANE
    rev = lambda i: nt - 1 - i

    def body(*refs):
        x_refs = refs[:n_sub]
        (mb, g_ref, w_ref, bf_ref, fl_ref, dc_ref, dq_ref, dk_ref, dv_ref, dza_ref, dgb_ref, dzc_ref,
         dcv_ref, dcvn_ref, gc_ref, xc_ref, gch_ref, xch_ref, dout_ref, cw_ref, tri_ref) = refs[n_sub:n_sub + 21]
        dp_ref, gx_ref, front_ref, gn_ref, gbf_ref, gcw_ref, carry, dh_scr, gx_sems = refs[n_sub + 21:]
        step = pl.program_id(0)
        i = rev(step)

        @pl.when(step == 0)
        def _():
            for r in (gn_ref, gbf_ref, gcw_ref, carry):
                r[...] = jnp.zeros_like(r)

        dc8 = jnp.concatenate([dc_ref[...], jnp.zeros((LANE - HEADS, t), F32)], axis=0).T
        dlogf = _dot_exact(tri_ref[...], dc8) + carry[...]
        carry[...] = carry[...] + jnp.sum(dc8, axis=0, keepdims=True)
        z = fl_ref[...] + bf_ref[...]
        row = i * t + lax.broadcasted_iota(jnp.int32, (t, LANE), 0)
        d_f = jnp.where(row >= PAD_ROWS, dlogf * (1.0 / (1.0 + jnp.exp(z))), 0.0)
        gbf_ref[...] = gbf_ref[...] + jnp.sum(d_f, axis=0, keepdims=True)
        dcv = dcv_ref[...]
        dcv_next = jnp.where(i == nt - 1, 0.0, dcvn_ref[...])
        d_cx = (cw_ref[2:3, :] * dcv + cw_ref[1:2, :] * _shift_up(dcv, dcv_next, 1)
                + cw_ref[0:1, :] * _shift_up(dcv, dcv_next, 2))
        gc = gc_ref[...]
        xc = xc_ref[...]
        cx = gc * xc
        cx_prev = jnp.where(i == 0, 0.0, gch_ref[...] * xch_ref[...])
        rowi = lax.broadcasted_iota(jnp.int32, (SUBLANE, 1), 0)
        gcw = (jnp.where(rowi == 0, jnp.sum(dcv * _shift_down(cx_prev, cx, 2), axis=0, keepdims=True), 0.0)
               + jnp.where(rowi == 1, jnp.sum(dcv * _shift_down(cx_prev, cx, 1), axis=0, keepdims=True), 0.0)
               + jnp.where(rowi == 2, jnp.sum(dcv * cx, axis=0, keepdims=True), 0.0))
        gcw_ref[...] = gcw_ref[...] + gcw
        dp_ref[:, SEG_Q:SEG_Q + 512] = dq_ref[...]
        dp_ref[:, SEG_K:SEG_K + 512] = dk_ref[...]
        dp_ref[:, SEG_V:SEG_V + 512] = dv_ref[...]
        dp_ref[:, SEG_F:SEG_F + LANE] = d_f.astype(dp_ref.dtype)
        dp_ref[:, SEG_ZA:SEG_ZA + 512] = dza_ref[...]
        dp_ref[:, SEG_GB:SEG_GB + 512] = dgb_ref[...]
        dp_ref[:, SEG_GC:SEG_GC + 512] = (d_cx * xc).astype(dp_ref.dtype)
        dp_ref[:, SEG_XC:SEG_XC + 512] = (d_cx * gc).astype(dp_ref.dtype)
        dp_ref[:, SEG_ZC:SEG_ZC + 512] = dzc_ref[...]
        d_u = _dot(dp_ref[...], w_ref[...])
        first = jnp.where(i == 0, mb[...], x_refs[0][...])
        h = jnp.concatenate([first] + [r[...] for r in x_refs[1:]], axis=0)
        r1 = lax.rsqrt(jnp.mean(h * h, axis=-1, keepdims=True) + EPS)
        n_h = h * r1
        gn_ref[...] = gn_ref[...] + jnp.sum(d_u * n_h, axis=0, keepdims=True)
        dn = d_u * g_ref[...]
        d_h = dout_ref[...] + r1 * (dn - n_h * jnp.mean(dn * n_h, axis=-1, keepdims=True))
        slot = step % 2

        def to_grad_x(slot_, tile):
            return pltpu.make_async_copy(dh_scr.at[slot_], gx_ref.at[pl.ds(pl.multiple_of(tile * t - FRONT, SUBLANE), t)],
                                         gx_sems.at[slot_])

        @pl.when(step >= 2)
        def _():
            to_grad_x(slot, 1).wait()

        dh_scr[slot] = d_h

        @pl.when(i > 0)
        def _():
            to_grad_x(slot, i).start()

        @pl.when(i == 0)
        def _():
            front_ref[...] = d_h[:FRONT]
            rest_rows = pltpu.make_async_copy(dh_scr.at[slot, pl.ds(FRONT, t - FRONT)], gx_ref.at[pl.ds(0, t - FRONT)],
                                              gx_sems.at[slot])
            rest_rows.start()
            rest_rows.wait()
            if nt >= 2:
                to_grad_x(1 - slot, 1).wait()

    def x_specs():
        specs = [pl.BlockSpec((LANE, D_MODEL), lambda s: (jnp.maximum(n_sub * rev(s) - 1, 0), 0))]
        for b in range(1, n_sub):
            specs.append(pl.BlockSpec((LANE, D_MODEL), functools.partial(lambda s, b: (n_sub * rev(s) - 1 + b, 0), b=b)))
        return specs

    row_blk = lambda cols: pl.BlockSpec((t, cols), lambda s: (rev(s), 0))
    rest_blk = lambda k: pl.BlockSpec((t, 512), functools.partial(lambda s, k: (rev(s), k), k=k))
    halo_prev = lambda k: pl.BlockSpec(
        (SUBLANE, 512), functools.partial(lambda s, k: (jnp.maximum(rev(s) * hb - 1, 0), k), k=k))
    halo_next = pl.BlockSpec((SUBLANE, 512), lambda s: (jnp.minimum((rev(s) + 1) * hb, lp // SUBLANE - 1), 0))
    const = lambda shape: pl.BlockSpec(shape, lambda s: (0, 0))
    return pl.pallas_call(
        body, name="bwd_in", grid=(nt,),
        in_specs=x_specs() + [const((LANE, D_MODEL)), const((1, D_MODEL)),
                              pl.BlockSpec((D_IN_PAD, D_MODEL), lambda s: (0, 0), pipeline_mode=pl.Buffered(1)),
                              const((1, LANE)), row_blk(LANE),
                              pl.BlockSpec((HEADS, t), lambda s: (0, rev(s))),
                              row_blk(512), row_blk(512), row_blk(512), row_blk(512), row_blk(512), row_blk(512),
                              row_blk(512), halo_next, rest_blk(2), rest_blk(3), halo_prev(2), halo_prev(3),
                              row_blk(D_MODEL), const((SUBLANE, D_CONV)), const((t, t))],
        out_specs=(row_blk(D_IN_PAD), ANY, const((FRONT, D_MODEL)), const((1, D_MODEL)), const((1, LANE)),
                   const((SUBLANE, D_CONV))),
        out_shape=(jax.ShapeDtypeStruct((lp, D_IN_PAD), MXU_DTYPE), jax.ShapeDtypeStruct((lp - FRONT, D_MODEL), F32),
                   jax.ShapeDtypeStruct((FRONT, D_MODEL), F32),
                   jax.ShapeDtypeStruct((1, D_MODEL), F32), jax.ShapeDtypeStruct((1, LANE), F32),
                   jax.ShapeDtypeStruct((SUBLANE, D_CONV), F32)),
        scratch_shapes=[pltpu.VMEM((1, LANE), F32), pltpu.VMEM((2, t, D_MODEL), F32), pltpu.SemaphoreType.DMA((2,))],
        compiler_params=_params(("arbitrary",)),
    )(*([x2] * n_sub), meta_blk, norm_g, w_pad, bf_pad, fl, dc, dq, dk, dv, dza, dgb, dzc, dconv, dconv,
      rest, rest, rest, rest, d_out, conv_w8, _triangle(t, lower=False))


def _grad_w_in(u, dproj):
    lp = u.shape[0]
    tn = GW_COL_TILE
    tk = tn if lp % tn == 0 else ROW_TILE

    def body(d_ref, u_ref, o_ref, wire_ref):
        k = pl.program_id(1)

        @pl.when(k == 0)
        def _():
            o_ref[...] = jnp.zeros_like(o_ref)

        o_ref[...] = o_ref[...] + lax.dot_general(d_ref[...], u_ref[...], (((0,), (0,)), ((), ())),
                                                  preferred_element_type=F32)

        @pl.when(k == pl.num_programs(1) - 1)
        def _():
            wire_ref[...] = o_ref[...].astype(wire_ref.dtype)

    out_spec = pl.BlockSpec((tn, D_MODEL), lambda n, k: (n, 0))
    return pl.pallas_call(
        body, name="grad_w_in", grid=(D_IN_PAD // tn, lp // tk),
        in_specs=[pl.BlockSpec((tk, tn), lambda n, k: (k, n)), pl.BlockSpec((tk, D_MODEL), lambda n, k: (k, 0))],
        out_specs=(out_spec, out_spec),
        out_shape=(jax.ShapeDtypeStruct((D_IN_PAD, D_MODEL), F32), jax.ShapeDtypeStruct((D_IN_PAD, D_MODEL), WIRE_DTYPE)),
        compiler_params=_params(("parallel", "arbitrary")),
    )(dproj, u)


def _by_chip(own, others, me):
    by_mask = jnp.stack([own, others[1], others[0], others[2]])
    return [lax.dynamic_index_in_dim(by_mask, jnp.bitwise_xor(me, s), 0, keepdims=False) for s in range(N_CHIPS)]


def _both_halves(mine, other, c):
    return jnp.where(c == 0, jnp.concatenate([mine, other], axis=0), jnp.concatenate([other, mine], axis=0))


def _local_step(x2, tgt2, meta_full, norm_g, w_pad, b_f, conv_w_full, attn_g, conv_g, w_out_full, final_g):
    lp = x2.shape[0] + FRONT
    nt = lp // ROW_TILE
    meta_blk = jnp.concatenate([jnp.zeros((PAD_ROWS, D_MODEL), F32), meta_full], axis=0)
    bf_pad = jnp.pad(b_f, ((0, 0), (0, LANE - HEADS)))
    conv_w8 = jnp.pad(conv_w_full, ((0, SUBLANE - conv_w_full.shape[0]), (0, 0)))
    q, k, v, rest, fl, ct, u, q_t, v_t, cc = _in_proj(x2, meta_blk, norm_g, w_pad, bf_pad)
    ct4 = ct.reshape(SUBLANE, nt, 1, ROW_TILE)
    o, l_sum, m_max = _attn_fwd(q, k, v_t, cc)
    (d_out, d_o, do_t, delta, dza, dgb, dzc, dconv, loss, g_final, g_attn, g_convg, gw_out) = _post(
        o, l_sum, rest, x2, meta_blk, tgt2, w_out_full, attn_g, conv_g, final_g, conv_w8)
    dq, dk, dv, dc = _attn_bwd(q, k, v, d_o, q_t, do_t, m_max, delta, ct4)
    dproj, grad_x, d_front, g_norm, g_bf, g_cw = _bwd_in(x2, meta_blk, norm_g, w_pad, bf_pad, fl, dc.reshape(HEADS, lp), dq, dk, dv,
                                             dza, dgb, dzc, dconv, rest, d_out, conv_w8)
    gw_in, gw_in_wire = _grad_w_in(u, dproj)
    return dict(loss=loss, grad_x=grad_x, d_front=d_front, g_norm=g_norm, g_final=g_final, g_attn=g_attn, g_convg=g_convg, g_bf=g_bf,
                g_cw=g_cw, gw_out=gw_out, gw_in=gw_in, gw_in_wire=gw_in_wire)


def kernel(x, meta, norm_g, w_in, b_f, conv_w, attn_norm_g, conv_norm_g, w_out, final_norm_g, loss_target, m_meta, m_norm_g, m_w_in, m_b_f, m_conv_w, m_attn_norm_g, m_conv_norm_g, m_w_out, m_final_norm_g, v_meta, v_norm_g, v_w_in, v_b_f, v_conv_w, v_attn_norm_g, v_conv_norm_g, v_w_out, v_final_norm_g):
    cx_, cy_, cc_ = _position()
    chip = 2 * cx_ + cy_
    shard = w_in.shape[2]
    out_half = w_out.shape[1] // 2
    pick = lambda vals: jnp.where(chip == 0, vals[0], jnp.where(chip == 1, vals[1], jnp.where(chip == 2, vals[2], vals[3])))
    a_off, b_off = pick(A_OFF), pick(B_OFF)
    wt = jnp.transpose(w_in[0]).astype(MXU_DTYPE)
    wi = lax.dynamic_update_slice_in_dim(
        lax.dynamic_update_slice_in_dim(jnp.zeros((WIN_ROWS, D_MODEL), MXU_DTYPE), wt[:PIECE_A], a_off, 0),
        wt[PIECE_A:], b_off, 0)
    wo = w_out[0].astype(MXU_DTYPE)
    small = jnp.concatenate([meta, jnp.pad(conv_w[0], ((0, 8 - conv_w.shape[1]), (0, meta.shape[1] - conv_w.shape[2])))],
                            axis=0)
    gwi, gwo, gsm = _gather_weights(wi.reshape(2, WIN_HALF, D_MODEL), wo.reshape(2, out_half, D_MODEL), small)
    starts = jnp.stack([_window_start(jnp.bitwise_xor(chip, mask)) for mask in (0, 2, 1, 3)]).astype(jnp.int32)
    w_pad = _assemble_w(wi, gwi.reshape(3, WIN_ROWS, D_MODEL), starts)
    w_out_full = jnp.concatenate(_by_chip(wo, gwo.reshape(3, 2 * out_half, D_MODEL), chip), axis=0)
    small_full = jnp.concatenate(_by_chip(small, gsm, chip), axis=1)
    meta_full = small_full[:N_META]
    conv_w_full = jnp.concatenate([small_full[N_META:N_META + 3, 256 * s:256 * s + LANE] for s in range(N_CHIPS)], axis=1)
    final_g2 = final_norm_g.reshape(1, D_MODEL)
    r = _local_step(x[0], loss_target[0], meta_full, norm_g, w_pad, b_f, conv_w_full, attn_norm_g, conv_norm_g,
                    w_out_full, final_g2)
    grad_x = r["grad_x"][None]
    gb = r["gw_out"].reshape(N_CHIPS, 2, out_half, D_MODEL)
    wide = lambda a: jnp.pad(a, ((0, 0), (0, D_MODEL - a.shape[1])))
    pack = jnp.concatenate([
        r["g_norm"], r["g_final"], jnp.concatenate([r["g_attn"], r["g_convg"]], axis=1), wide(r["g_bf"]),
        wide(r["loss"]), jnp.zeros((3, D_MODEL), F32), r["d_front"][PAD_ROWS:], wide(r["g_cw"])], axis=0)
    ra, rb, packs = _pair_exchange(r["gw_in_wire"], gb, pack)
    c_idx = jnp.reshape(cc_, (1,)).astype(jnp.int32)
    chip_idx = jnp.reshape(chip, (1,)).astype(jnp.int32)
    pa, pa_wire = _pair_sum_windows(r["gw_in"], ra, c_idx)
    pb, pb_wire = _pair_sum(gb, rb, c_idx)
    xa, xb = _chip_exchange(pa_wire, pb_wire)
    ha = _chip_sum(pa, xa, chip_idx)
    hb = _chip_sum(pb, xb, chip_idx)
    oa, ob = _pair_share(ha, hb)
    g_window = _both_halves(ha, oa, cc_)
    g_w_in_t = jnp.concatenate([lax.dynamic_slice_in_dim(g_window, a_off, PIECE_A, 0),
                                lax.dynamic_slice_in_dim(g_window, b_off, shard - PIECE_A, 0)], axis=0)
    g_w_out = _both_halves(hb, ob, cc_)
    as_rows = lambda a: jnp.transpose(a, (2, 0, 1))
    g_w_in, d_w_in, nm_w_in, nv_w_in = (jnp.transpose(a, (1, 2, 0)) for a in _adamw_rows(
        as_rows(w_in), g_w_in_t, as_rows(m_w_in), as_rows(v_w_in)))
    d_w_out, nm_w_out, nv_w_out = (a[None] for a in _adamw_big(w_out[0], g_w_out, m_w_out[0], v_w_out[0], LANE))
    params = (norm_g, final_g2, attn_norm_g, conv_norm_g, b_f, meta, conv_w[0])
    ms = (m_norm_g, m_final_norm_g.reshape(1, D_MODEL), m_attn_norm_g, m_conv_norm_g, m_b_f, m_meta, m_conv_w[0])
    vs = (v_norm_g, v_final_norm_g.reshape(1, D_MODEL), v_attn_norm_g, v_conv_norm_g, v_b_f, v_meta, v_conv_w[0])
    loss, g_s, d_s, m_s, v_s = _small_update(pack, packs, params, ms, vs)

    def ordered(small_list, big_in, big_out):
        s_norm, s_final, s_attn, s_convg, s_bf, s_meta, s_cw = small_list
        return (s_meta, s_norm, big_in, s_bf, s_cw[None], s_attn, s_convg, big_out, s_final.reshape(D_MODEL))

    return (loss.reshape(()), grad_x,
            *ordered(g_s, g_w_in, g_w_out[None]), *ordered(d_s, d_w_in, d_w_out),
            *ordered(m_s, nm_w_in, nm_w_out), *ordered(v_s, nv_w_in, nv_w_out))
```

```python
import functools

import jax
import jax.numpy as jnp
from jax import lax
from jax.experimental import pallas as pl
from jax.experimental.pallas import tpu as pltpu

F32 = jnp.float32
MXU_DTYPE = jnp.bfloat16
WIRE_DTYPE = jnp.bfloat16

D_MODEL = 1024
N_META = 16
HEADS = 8
HEAD_DIM = 64
D_ATTN = HEADS * HEAD_DIM
D_CONV = 512
EPS = 1e-6
LANE = 128
SUBLANE = 8
ROW_TILE = 384
ATTN_UNROLL = 3
ATTN_BWD_QBLOCKS = 2
STAT_TERMS = 1
FRONT = LANE
PAD_ROWS = FRONT - N_META
NEG = -1e30
LOG2E = 1.4426950408889634
N_CHIPS = 4
N_DEV = 8
VMEM_LIMIT_BYTES = 60 * 1024 * 1024

SEG_Q, SEG_K, SEG_V, SEG_F, SEG_ZA, SEG_GB, SEG_GC, SEG_XC, SEG_ZC = (
    0, 512, 1024, 1536, 1664, 2176, 2688, 3200, 3712)
D_IN = 4104
D_IN_PAD = 4224
F_END = 1544
GW_COL_TILE = 1408
WIN_ROWS = 1152
WIN_HALF = WIN_ROWS // 2
WIN_START = (0, 1024, 2160, 3072)
PIECE_A = 518
A_OFF = (0, 2, 12, 126)
B_OFF = (518, 640, 530, 644)
ADAM_LR = 0.001
ADAM_B1 = 0.9
ADAM_B2 = 0.999
ADAM_EPS = 1e-08
ADAM_WD = 0.01
ADAM_STEP = 10

MESH = pl.DeviceIdType.MESH
ANY = pl.BlockSpec(memory_space=pl.ANY)

PACK_ROWS = 32
SLOT_NORM = (0, 1, 0, 1024)
SLOT_FINAL = (1, 2, 0, 1024)
SLOT_ATTN = (2, 3, 0, 512)
SLOT_CONVG = (2, 3, 512, 1024)
SLOT_BF = (3, 4, 0, 8)
SLOT_META = (8, 24, 0, 256)
SLOT_CONVW = (24, 27, 0, 128)
LOSS_ROW = 4


def _params(sem=None):
    return pltpu.CompilerParams(dimension_semantics=sem, vmem_limit_bytes=VMEM_LIMIT_BYTES)


def _sigmoid(z):
    return 1.0 / (1.0 + jnp.exp(-z))


def _dot(a, b):
    return jnp.dot(a, b, preferred_element_type=F32)


def _dot_nt(a, b):
    return lax.dot_general(a, b, (((1,), (1,)), ((), ())), preferred_element_type=F32)


def _dot_exact(ones, x):
    ones = ones.astype(MXU_DTYPE)
    total = None
    for _ in range(3):
        term = x.astype(MXU_DTYPE)
        x = x - term.astype(F32)
        total = _dot(ones, term) if total is None else total + _dot(ones, term)
    return total


def _group_matrix():
    r = lax.broadcasted_iota(jnp.int32, (D_ATTN, D_ATTN), 0) >> 6
    c = lax.broadcasted_iota(jnp.int32, (D_ATTN, D_ATTN), 1) >> 6
    return jnp.where(r == c, 1.0, 0.0).astype(MXU_DTYPE)


def _triangle(n, lower):
    r = lax.broadcasted_iota(jnp.int32, (n, n), 0)
    c = lax.broadcasted_iota(jnp.int32, (n, n), 1)
    return jnp.where((r >= c) if lower else (c >= r), 1.0, 0.0).astype(MXU_DTYPE)


def _group_sum(x, gmat, terms=2):
    hi = x.astype(MXU_DTYPE)
    if terms == 1:
        return _dot(hi, gmat)
    lo = (x - hi.astype(F32)).astype(MXU_DTYPE)
    return _dot(hi, gmat) + _dot(lo, gmat)


def _x_block_specs(n_sub, rows):
    specs = [pl.BlockSpec((rows, D_MODEL), lambda i: (jnp.maximum(n_sub * i - 1, 0), 0))]
    for b in range(1, n_sub):
        specs.append(pl.BlockSpec((rows, D_MODEL), functools.partial(lambda i, b: (n_sub * i - 1 + b, 0), b=b)))
    return specs


def _position():
    return lax.axis_index("x"), lax.axis_index("y"), lax.axis_index("c")


def _gather_weights(wi, wo, small):
    def body(wi_ref, wo_ref, sm_ref, gwi_ref, gwo_ref, gsm_ref, send_sems, recv_sems):
        x, y, c = _position()
        sibling = (x, y, 1 - c)
        chips = [(1 - x, y), (x, 1 - y), (1 - x, 1 - y)]

        def remote(k, src, dst, to):
            return pltpu.make_async_remote_copy(src_ref=src, dst_ref=dst, send_sem=send_sems.at[k],
                                                recv_sem=recv_sems.at[k], device_id=to, device_id_type=MESH)

        first, passed, landed = [], [], []
        for a, (src_ref, g_ref) in enumerate(((wi_ref, gwi_ref), (wo_ref, gwo_ref))):
            for j, (cx, cy) in enumerate(chips):
                slot = g_ref.at[j, c]
                first.append(remote(6 * a + j, src_ref.at[c], slot, (cx, cy, c)))
                landed.append(remote(6 * a + j, slot, slot, sibling))
                passed.append(remote(6 * a + 3 + j, slot, slot, sibling))
        for j, (cx, cy) in enumerate(chips):
            first.append(remote(12 + j, sm_ref, gsm_ref.at[j], (cx, cy, c)))
        for cp in first:
            cp.start()
        for arrived, onward in zip(landed, passed):
            arrived.wait_recv()
            onward.start()
        for a, g_ref in enumerate((gwi_ref, gwo_ref)):
            for j in range(3):
                remote(6 * a + 3 + j, g_ref.at[j, 1 - c], g_ref.at[j, 1 - c], sibling).wait_recv()
        for j in range(3):
            remote(12 + j, sm_ref, gsm_ref.at[j], sibling).wait_recv()
        for cp in first + passed:
            cp.wait_send()

    return pl.pallas_call(
        body, name="gather_weights",
        out_shape=(jax.ShapeDtypeStruct((3,) + wi.shape, wi.dtype), jax.ShapeDtypeStruct((3,) + wo.shape, wo.dtype),
                   jax.ShapeDtypeStruct((3,) + small.shape, small.dtype)),
        in_specs=[ANY, ANY, ANY], out_specs=(ANY, ANY, ANY),
        scratch_shapes=[pltpu.SemaphoreType.DMA((15,)), pltpu.SemaphoreType.DMA((15,))],
    )(wi, wo, small)


def _pair_exchange(gw, gb, pack):
    n_big = N_CHIPS + 1

    def body(gw_ref, gb_ref, p_ref, ra_ref, rb_ref, o_ref, send_sems, recv_sems):
        x, y, c = _position()
        sibling = (x, y, 1 - c)

        def remote(k, src, dst, to):
            return pltpu.make_async_remote_copy(src_ref=src, dst_ref=dst, send_sem=send_sems.at[k],
                                                recv_sem=recv_sems.at[k], device_id=to, device_id_type=MESH)

        copies = [remote(N_CHIPS, gb_ref.at[:, 1 - c], rb_ref, sibling)]
        for s, start in enumerate(WIN_START):
            rows = pl.ds(pl.multiple_of(start + WIN_HALF * (1 - c), 2 * SUBLANE), WIN_HALF)
            copies.append(remote(s, gw_ref.at[rows], ra_ref.at[s], sibling))
        for mask in range(1, N_DEV):
            peer = (1 - x if mask & 4 else x, 1 - y if mask & 2 else y, 1 - c if mask & 1 else c)
            copies.append(remote(n_big + mask - 1, p_ref, o_ref.at[mask - 1], peer))
        for cp in copies:
            cp.start()
        for cp in copies:
            cp.wait()

    n_sems = n_big + N_DEV - 1
    return pl.pallas_call(
        body, name="grad_pair_exchange",
        out_shape=(jax.ShapeDtypeStruct((N_CHIPS, WIN_HALF, D_MODEL), gw.dtype),
                   jax.ShapeDtypeStruct((N_CHIPS,) + gb.shape[2:], gb.dtype),
                   jax.ShapeDtypeStruct((N_DEV - 1,) + pack.shape, pack.dtype)),
        in_specs=[ANY, ANY, ANY], out_specs=(ANY, ANY, ANY),
        scratch_shapes=[pltpu.SemaphoreType.DMA((n_sems,)), pltpu.SemaphoreType.DMA((n_sems,))],
    )(gw, gb, pack)


def _chip_exchange(pa, pb):
    def body(pa_ref, pb_ref, ra_ref, rb_ref, send_sems, recv_sems):
        x, y, c = _position()
        chips = [(1 - x, y), (x, 1 - y), (1 - x, 1 - y)]
        copies = []
        for a, (src, dst) in enumerate(((pa_ref, ra_ref), (pb_ref, rb_ref))):
            for j, (cx, cy) in enumerate(chips):
                copies.append(pltpu.make_async_remote_copy(
                    src_ref=src.at[2 * cx + cy], dst_ref=dst.at[j], send_sem=send_sems.at[3 * a + j],
                    recv_sem=recv_sems.at[3 * a + j], device_id=(cx, cy, c), device_id_type=MESH))
        for cp in copies:
            cp.start()
        for cp in copies:
            cp.wait()

    return pl.pallas_call(
        body, name="grad_chip_exchange",
        out_shape=(jax.ShapeDtypeStruct((3,) + pa.shape[1:], pa.dtype),
                   jax.ShapeDtypeStruct((3,) + pb.shape[1:], pb.dtype)),
        in_specs=[ANY, ANY], out_specs=(ANY, ANY),
        scratch_shapes=[pltpu.SemaphoreType.DMA((6,)), pltpu.SemaphoreType.DMA((6,))],
    )(pa, pb)


def _pair_share(ha, hb):
    def body(ha_ref, hb_ref, oa_ref, ob_ref, send_sems, recv_sems):
        x, y, c = _position()
        copies = [pltpu.make_async_remote_copy(
            src_ref=src, dst_ref=dst, send_sem=send_sems.at[k], recv_sem=recv_sems.at[k],
            device_id=(x, y, 1 - c), device_id_type=MESH)
            for k, (src, dst) in enumerate(((ha_ref, oa_ref), (hb_ref, ob_ref)))]
        for cp in copies:
            cp.start()
        for cp in copies:
            cp.wait()

    return pl.pallas_call(
        body, name="grad_pair_share",
        out_shape=(jax.ShapeDtypeStruct(ha.shape, ha.dtype), jax.ShapeDtypeStruct(hb.shape, hb.dtype)),
        in_specs=[ANY, ANY], out_specs=(ANY, ANY),
        scratch_shapes=[pltpu.SemaphoreType.DMA((2,)), pltpu.SemaphoreType.DMA((2,))],
    )(ha, hb)


def _pair_sum(mine, recv, c_idx):
    rows, cols = mine.shape[2:]

    def body(c_ref, a_ref, b_ref, o_ref, send_ref):
        total = a_ref[...] + b_ref[...]
        o_ref[...] = total
        send_ref[...] = total.astype(send_ref.dtype)

    out_spec = pl.BlockSpec((None, rows, cols), lambda s, c_ref: (s, 0, 0))
    return pl.pallas_call(
        body, name="grad_pair_sum",
        grid_spec=pltpu.PrefetchScalarGridSpec(
            num_scalar_prefetch=1, grid=(N_CHIPS,),
            in_specs=[pl.BlockSpec((None, None, rows, cols), lambda s, c_ref: (s, c_ref[0], 0, 0)),
                      pl.BlockSpec((None, rows, cols), lambda s, c_ref: (s, 0, 0))],
            out_specs=(out_spec, out_spec)),
        out_shape=(jax.ShapeDtypeStruct(recv.shape, recv.dtype), jax.ShapeDtypeStruct(recv.shape, WIRE_DTYPE)),
        compiler_params=_params(("parallel",)),
    )(c_idx, mine, recv)


def _window_start(s):
    return jnp.where(s == 0, WIN_START[0], jnp.where(s == 1, WIN_START[1], jnp.where(s == 2, WIN_START[2], WIN_START[3])))


def _pair_sum_windows(gw, recv, c_idx):
    tr = WIN_HALF // 3

    def body(c_ref, a_ref, b_ref, o_ref, send_ref):
        total = a_ref[...] + b_ref[...].astype(F32)
        o_ref[...] = total
        send_ref[...] = total.astype(send_ref.dtype)

    out_spec = pl.BlockSpec((None, tr, D_MODEL), lambda s, i, c_ref: (s, i, 0))
    return pl.pallas_call(
        body, name="grad_pair_sum_windows",
        grid_spec=pltpu.PrefetchScalarGridSpec(
            num_scalar_prefetch=1, grid=(N_CHIPS, WIN_HALF // tr),
            in_specs=[pl.BlockSpec((pl.Element(tr), pl.Element(D_MODEL)),
                                   lambda s, i, c_ref: (pl.multiple_of(
                                       _window_start(s) + WIN_HALF * c_ref[0] + tr * i, SUBLANE), 0)),
                      pl.BlockSpec((None, tr, D_MODEL), lambda s, i, c_ref: (s, i, 0))],
            out_specs=(out_spec, out_spec)),
        out_shape=(jax.ShapeDtypeStruct(recv.shape, F32), jax.ShapeDtypeStruct(recv.shape, WIRE_DTYPE)),
        compiler_params=_params(("parallel", "parallel")),
    )(c_idx, gw, recv)


def _assemble_w(own, others, starts):
    def body(starts_ref, own_ref, oth_ref, o_ref):
        o_ref[...] = jnp.zeros_like(o_ref)
        for k in range(N_CHIPS):
            rows = pl.ds(pl.multiple_of(starts_ref[k], 2 * SUBLANE), WIN_ROWS)
            o_ref[rows, :] = o_ref[rows, :] + (own_ref[...] if k == 0 else oth_ref[k - 1])

    return pl.pallas_call(
        body, name="assemble_w",
        in_specs=[pl.BlockSpec(memory_space=pltpu.SMEM), pl.BlockSpec(memory_space=pltpu.VMEM),
                  pl.BlockSpec(memory_space=pltpu.VMEM)],
        out_specs=pl.BlockSpec(memory_space=pltpu.VMEM),
        out_shape=jax.ShapeDtypeStruct((D_IN_PAD, D_MODEL), own.dtype),
        compiler_params=_params(),
    )(starts, own, others)


def _chip_sum(psum, recv3, chip_idx):
    rows, cols = psum.shape[1:]
    tr = rows // 2

    def body(s_ref, p_ref, r0, r1, r2, o_ref):
        o_ref[...] = ((p_ref[...] + r0[...].astype(F32)) + r1[...].astype(F32)) + r2[...].astype(F32)

    return pl.pallas_call(
        body, name="grad_chip_sum",
        grid_spec=pltpu.PrefetchScalarGridSpec(
            num_scalar_prefetch=1, grid=(2,),
            in_specs=[pl.BlockSpec((None, tr, cols), lambda i, s_ref: (s_ref[0], i, 0))] +
                     [pl.BlockSpec((None, tr, cols), functools.partial(lambda i, s_ref, j: (j, i, 0), j=j))
                      for j in range(3)],
            out_specs=pl.BlockSpec((tr, cols), lambda i, s_ref: (i, 0))),
        out_shape=jax.ShapeDtypeStruct((rows, cols), psum.dtype),
        compiler_params=_params(("parallel",)),
    )(chip_idx, psum, recv3, recv3, recv3)


def _adamw_math(w, g, m, v):
    m = ADAM_B1 * m + (1.0 - ADAM_B1) * g
    v = ADAM_B2 * v + (1.0 - ADAM_B2) * (g * g)
    m_hat = m * (1.0 / (1.0 - ADAM_B1 ** ADAM_STEP))
    v_hat = v * (1.0 / (1.0 - ADAM_B2 ** ADAM_STEP))
    delta = -ADAM_LR * (m_hat / (jnp.sqrt(v_hat) + ADAM_EPS) + ADAM_WD * w)
    return delta, m, v


def _adamw_big(w, g, m, v, tr):
    rows, cols = w.shape
    assert rows % tr == 0 and g.shape[0] >= rows

    def body(w_ref, g_ref, m_ref, v_ref, d_out, m_out, v_out):
        d, m2, v2 = _adamw_math(w_ref[...], g_ref[...], m_ref[...], v_ref[...])
        d_out[...] = d
        m_out[...] = m2
        v_out[...] = v2

    spec = pl.BlockSpec((tr, cols), lambda i: (i, 0))
    sds = jax.ShapeDtypeStruct((rows, cols), F32)
    return pl.pallas_call(
        body, name="adamw_big", grid=(rows // tr,), in_specs=[spec] * 4, out_specs=(spec,) * 3,
        out_shape=(sds,) * 3, compiler_params=_params(("parallel",)),
    )(w, g, m, v)


def _adamw_rows(w3, g, m3, v3):
    rows, _, cols = w3.shape
    tc = 2 * LANE

    def body(w_ref, g_ref, m_ref, v_ref, g_out, d_out, m_out, v_out):
        g = g_ref[...]
        d, m2, v2 = _adamw_math(w_ref[:, 0, :], g, m_ref[:, 0, :], v_ref[:, 0, :])
        g_out[:, 0, :] = g
        d_out[:, 0, :] = d
        m_out[:, 0, :] = m2
        v_out[:, 0, :] = v2

    spec3 = pl.BlockSpec((rows, 1, tc), lambda i: (0, 0, i))
    sds = jax.ShapeDtypeStruct((rows, 1, cols), F32)
    return pl.pallas_call(
        body, name="adamw_rows", grid=(cols // tc,),
        in_specs=[spec3, pl.BlockSpec((rows, tc), lambda i: (0, i)), spec3, spec3], out_specs=(spec3,) * 4,
        out_shape=(sds,) * 4, compiler_params=_params(("parallel",)),
    )(w3, g, m3, v3)


def _small_update(own, others, params, ms, vs):
    slots = (SLOT_NORM, SLOT_FINAL, SLOT_ATTN, SLOT_CONVG, SLOT_BF, SLOT_META, SLOT_CONVW)
    n = len(slots)

    def body(*refs):
        own_ref, gp_ref = refs[:2]
        w_refs, m_refs, v_refs = refs[2:2 + n], refs[2 + n:2 + 2 * n], refs[2 + 2 * n:2 + 3 * n]
        outs = refs[2 + 3 * n:3 + 7 * n]
        loss_ref = outs[0]
        g_outs, d_outs, m_outs, v_outs = (outs[1 + k * n:1 + (k + 1) * n] for k in range(4))
        g_scr, w_scr, m_scr, v_scr = refs[3 + 7 * n:]
        x, y, c = _position()
        shard = 2 * x + y
        me = 4 * x + 2 * y + c
        tot = None
        for d in range(N_DEV):
            rel = jnp.bitwise_xor(me, d)
            term = jnp.where(rel == 0, own_ref[...], gp_ref[jnp.maximum(rel, 1) - 1])
            tot = term if tot is None else tot + term
        r0, r1, _, _ = SLOT_META
        meta_sel = tot[r0:r1, 0:256]
        cw_sel = tot[24:32, 0:128]
        for k in range(1, N_CHIPS):
            meta_sel = jnp.where(shard == k, tot[r0:r1, 256 * k:256 * (k + 1)], meta_sel)
            cw_sel = jnp.where(shard == k, tot[24:32, 128 * k:128 * (k + 1)], cw_sel)
        zeros = jnp.zeros((PACK_ROWS, D_MODEL), F32)
        for scr in (g_scr, w_scr, m_scr, v_scr):
            scr[...] = zeros
        g_scr[0:8, :] = tot[0:8, :]
        g_scr[r0:r1, 0:256] = meta_sel
        g_scr[24:32, 0:128] = cw_sel
        for (a, b, c0, c1), w_ref, m_ref, v_ref in zip(slots, w_refs, m_refs, v_refs):
            w_scr[a:b, c0:c1] = w_ref[...]
            m_scr[a:b, c0:c1] = m_ref[...]
            v_scr[a:b, c0:c1] = v_ref[...]
        loss_ref[...] = g_scr[LOSS_ROW:LOSS_ROW + 1, 0:1]
        d, m2, v2 = _adamw_math(w_scr[...], g_scr[...], m_scr[...], v_scr[...])
        w_scr[...] = d
        m_scr[...] = m2
        v_scr[...] = v2
        for (a, b, c0, c1), g_o, d_o, m_o, v_o in zip(slots, g_outs, d_outs, m_outs, v_outs):
            g_o[...] = g_scr[a:b, c0:c1]
            d_o[...] = w_scr[a:b, c0:c1]
            m_o[...] = m_scr[a:b, c0:c1]
            v_o[...] = v_scr[a:b, c0:c1]

    shapes = [jax.ShapeDtypeStruct(p.shape, F32) for p in params]
    out = pl.pallas_call(
        body, name="small_update",
        out_shape=[jax.ShapeDtypeStruct((1, 1), F32)] + shapes * 4,
        scratch_shapes=[pltpu.VMEM((PACK_ROWS, D_MODEL), F32)] * 4,
        compiler_params=_params(),
    )(own, others, *params, *ms, *vs)
    return out[0], out[1:1 + n], out[1 + n:1 + 2 * n], out[1 + 2 * n:1 + 3 * n], out[1 + 3 * n:1 + 4 * n]


def _in_proj(x2, meta_blk, norm_g, w_pad, bf_pad):
    seq = x2.shape[0]
    lp = seq + FRONT
    t = ROW_TILE
    nt = lp // t
    n_sub = t // LANE

    def body(*refs):
        x_refs = refs[:n_sub]
        mb, g_ref, w_ref, bf_ref, tri_ref = refs[n_sub:n_sub + 5]
        q_ref, k_ref, v_ref, rest_ref, fl_ref, ct_ref, u_ref, qt_ref, kt_ref, vt_ref, cc_ref, carry = refs[n_sub + 5:]
        i = pl.program_id(0)

        @pl.when(i == 0)
        def _():
            carry[...] = jnp.zeros_like(carry)

        first = jnp.where(i == 0, mb[...], x_refs[0][...])
        h = jnp.concatenate([first] + [r[...] for r in x_refs[1:]], axis=0)
        ms = jnp.mean(h * h, axis=-1, keepdims=True)
        u = ((h * lax.rsqrt(ms + EPS)) * g_ref[...]).astype(MXU_DTYPE)
        u_ref[...] = u

        def seg(a, width):
            return _dot_nt(u, w_ref[a:a + width, :])

        q_tile = seg(SEG_Q, D_ATTN) * (HEAD_DIM ** -0.5)
        q_ref[...] = q_tile.astype(MXU_DTYPE)
        qt_ref[...] = q_tile.T.astype(MXU_DTYPE)
        k_tile = seg(SEG_K, D_ATTN)
        k_ref[...] = k_tile.astype(MXU_DTYPE)
        kt_ref[...] = k_tile.T.astype(MXU_DTYPE)
        v_tile = seg(SEG_V, D_ATTN)
        v_ref[...] = v_tile.astype(MXU_DTYPE)
        vt_ref[...] = v_tile.T.astype(MXU_DTYPE)
        for s in range(5):
            rest_ref[:, 512 * s:512 * (s + 1)] = seg(SEG_ZA + 512 * s, 512)
        fl = seg(SEG_F, LANE)
        fl_ref[...] = fl
        z = fl + bf_ref[...]
        logf = jnp.minimum(z, 0.0) - jnp.log(1.0 + jnp.exp(-jnp.abs(z)))
        row = i * t + lax.broadcasted_iota(jnp.int32, (t, LANE), 0)
        logf = jnp.where(row >= PAD_ROWS, logf, 0.0)
        cs = _dot_exact(tri_ref[...], logf) + carry[...]
        carry[...] = carry[...] + jnp.sum(logf, axis=0, keepdims=True)
        col = i * t + lax.broadcasted_iota(jnp.int32, (SUBLANE, t), 1)
        ct_ref[...] = jnp.where(col >= PAD_ROWS, cs.T[0:SUBLANE, :], -NEG)
        cc_ref[...] = jnp.where(row >= PAD_ROWS, cs, -NEG)

    row_blk = lambda cols: pl.BlockSpec((t, cols), lambda i: (i, 0))
    tr_blk = pl.BlockSpec((None, D_ATTN, t), lambda i: (i, 0, 0))
    const = lambda shape: pl.BlockSpec(shape, lambda i: (0, 0))
    return pl.pallas_call(
        body, name="in_proj", grid=(nt,),
        in_specs=_x_block_specs(n_sub, LANE) + [const((LANE, D_MODEL)), const((1, D_MODEL)),
                                                pl.BlockSpec((D_IN_PAD, D_MODEL), lambda i: (0, 0),
                                                             pipeline_mode=pl.Buffered(1)),
                                                const((1, LANE)), const((t, t))],
        out_specs=(row_blk(D_ATTN), row_blk(D_ATTN), row_blk(D_ATTN), row_blk(5 * 512), row_blk(LANE),
                   pl.BlockSpec((SUBLANE, t), lambda i: (0, i)), row_blk(D_MODEL), tr_blk, tr_blk, tr_blk, row_blk(LANE)),
        out_shape=(jax.ShapeDtypeStruct((lp, D_ATTN), MXU_DTYPE), jax.ShapeDtypeStruct((lp, D_ATTN), MXU_DTYPE),
                   jax.ShapeDtypeStruct((lp, D_ATTN), MXU_DTYPE), jax.ShapeDtypeStruct((lp, 5 * 512), F32),
                   jax.ShapeDtypeStruct((lp, LANE), F32),
                   jax.ShapeDtypeStruct((SUBLANE, lp), F32), jax.ShapeDtypeStruct((lp, D_MODEL), MXU_DTYPE),
                   jax.ShapeDtypeStruct((nt, D_ATTN, t), MXU_DTYPE), jax.ShapeDtypeStruct((nt, D_ATTN, t), MXU_DTYPE),
                   jax.ShapeDtypeStruct((nt, D_ATTN, t), MXU_DTYPE), jax.ShapeDtypeStruct((lp, LANE), F32)),
        scratch_shapes=[pltpu.VMEM((1, LANE), F32)],
        compiler_params=_params(("arbitrary",)),
    )(*([x2] * n_sub), meta_blk, norm_g, w_pad, bf_pad, _triangle(t, lower=True))


def _head_masks():
    lane = lax.broadcasted_iota(jnp.int32, (1, LANE), 1)
    return lane < HEAD_DIM, lane >= HEAD_DIM


def _pair_specs(lp, nt, t):
    blk = pl.BlockSpec((lp, LANE), lambda g: (0, g))
    ct_a = pl.BlockSpec((None, nt, 1, t), lambda g: (2 * g, 0, 0, 0))
    ct_b = pl.BlockSpec((None, nt, 1, t), lambda g: (2 * g + 1, 0, 0, 0))
    return blk, ct_a, ct_b


def _sub_rows(s, col):
    return jnp.concatenate([s[:, a * LANE:(a + 1) * LANE] - col for a in range(s.shape[1] // LANE)], axis=1)


def _loop_unrolled(lo, hi, step, init, n):
    def group(jj, carry):
        for k in range(n):
            carry = step(lo + n * jj + k, carry)
        return carry

    groups = (hi - lo) // n
    carry = lax.fori_loop(0, groups, group, init)
    return lax.fori_loop(lo + n * groups, hi, step, carry)


def _attn_fwd(q, k, v_t, cc):
    lp = q.shape[0]
    t = ROW_TILE
    nt = lp // t
    ext = LANE + 2 * SUBLANE

    def body(q_ref, k_ref, vt_ref, cc_ref, o_ref, l_ref, m_ref, s_scr, last_scr, m_scr, mfin_scr, acc_scr, c_scr):
        masks = _head_masks()
        lane = lax.broadcasted_iota(jnp.int32, (1, LANE), 1)
        for hh in range(2):
            picked = jnp.where(lane == 2 * pl.program_id(0) + hh, cc_ref[...], 0.0)
            c_scr[hh] = jnp.broadcast_to(jnp.sum(picked, axis=-1, keepdims=True), (lp, LANE))
        visible = lax.broadcasted_iota(jnp.int32, (t, t), 0) <= lax.broadcasted_iota(jnp.int32, (t, t), 1)
        top = lax.broadcasted_iota(jnp.int32, (LANE, 1), 0) < HEAD_DIM
        second_head = (lax.broadcasted_iota(jnp.int32, (2 * SUBLANE, 2 * t), 1) >= t).astype(jnp.int32)
        ones_rows = jnp.where(lax.broadcasted_iota(jnp.int32, (2 * SUBLANE, 2 * t), 0) == second_head,
                              1.0, 0.0).astype(MXU_DTYPE)

        on_first_diagonal = jnp.concatenate([visible, jnp.ones((t, t), jnp.bool_)], axis=1)

        def scores(j, queries):
            kj = k_ref[pl.ds(pl.multiple_of(j * t, t), t), :]
            return _dot_nt(jnp.concatenate([jnp.where(hm, kj, 0).astype(MXU_DTYPE) for hm in masks], axis=0), queries)

        def biased(s2, j, hh):
            return _sub_rows(s2[hh * t:(hh + 1) * t, :], c_scr[hh, pl.ds(pl.multiple_of(j * t, t), t), :]) * LOG2E

        def track_max(hh, s, lo, hi):
            m = m_scr[hh, :, lo:hi]
            for a in range(t // SUBLANE):
                m = jnp.maximum(m, s[a * SUBLANE:(a + 1) * SUBLANE, :])
            m_scr[hh, :, lo:hi] = m

        def probabilities(scores_of, ms_cols):
            return jnp.concatenate([jnp.exp2(scores_of(hh) - ms_cols[hh]).astype(MXU_DTYPE) for hh in range(2)], axis=0)

        def values(j):
            vtj = vt_ref[j]
            v2 = jnp.concatenate([jnp.where(top, vtj, 0).astype(MXU_DTYPE),
                                  jnp.where(top, 0, vtj).astype(MXU_DTYPE)], axis=1)
            return jnp.concatenate([v2, ones_rows], axis=0)

        def stage(done, ahead):
            if ahead is not None:
                i_a, rows_a = ahead
                qa = q_ref[pl.ds(pl.multiple_of(i_a * t, t), rows_a), :]
                m_scr[...] = jnp.full(m_scr.shape, NEG, F32)

                def max_step(j, mask=None):
                    s2 = scores(j, qa)
                    for hh in range(2):
                        s = biased(s2, j, hh)
                        if mask is not None:
                            s = jnp.where(mask, s, NEG)
                        s_scr[j, hh * t:(hh + 1) * t, 0:rows_a] = s
                        track_max(hh, s, 0, rows_a)

            if done is not None:
                i_d, rows_d = done
                r0 = pl.multiple_of(i_d * t, t)
                ms = [mfin_scr[hh, 0:1, 0:rows_d] for hh in range(2)]
                acc_scr[...] = jnp.zeros(acc_scr.shape, F32)

                def key_step(j, carry):
                    p = probabilities(lambda hh: s_scr[j, hh * t:(hh + 1) * t, 0:rows_d], ms)
                    acc_scr[:, 0:rows_d] = acc_scr[:, 0:rows_d] + _dot(values(j), p)
                    if ahead is not None:
                        max_step(j)
                    return carry

                _loop_unrolled(0, i_d + 1, key_step, 0, ATTN_UNROLL)
                if rows_d == 2 * t:
                    p = probabilities(lambda hh: last_scr[hh * t:(hh + 1) * t, :], [m[:, t:] for m in ms])
                    acc_scr[:, t:rows_d] = acc_scr[:, t:rows_d] + _dot(values(i_d + 1), p)
                acc = acc_scr[:, 0:rows_d]
                l_pair = jnp.where(top, acc[LANE:LANE + 1], acc[LANE + 1:LANE + 2])
                o_ref[pl.ds(r0, rows_d), :] = (acc[:LANE] / l_pair).T
                l_ref[pl.ds(r0, rows_d), :] = l_pair.T
                for hh in range(2):
                    m_ref[pl.ds(r0, rows_d), hh * LANE:(hh + 1) * LANE] = jnp.broadcast_to(ms[hh], (LANE, rows_d)).T

            if ahead is not None:
                if done is not None:
                    max_step(i_a - 1)
                max_step(i_a, on_first_diagonal if rows_a == 2 * t else visible)
                if rows_a == 2 * t:
                    s2 = scores(i_a + 1, qa[t:])
                    for hh in range(2):
                        s = jnp.where(visible, biased(s2, i_a + 1, hh), NEG)
                        last_scr[hh * t:(hh + 1) * t, :] = s
                        track_max(hh, s, t, rows_a)
                for hh in range(2):
                    mfin_scr[hh, :, 0:rows_a] = jnp.broadcast_to(jnp.max(m_scr[hh, :, 0:rows_a], axis=0, keepdims=True),
                                                                 (SUBLANE, rows_a))

        pairs = nt // 2
        stage(None, (0, 2 * t))

        def pair_to_pair(u, _):
            stage((2 * u, 2 * t), (2 * u + 2, 2 * t))
            return 0

        lax.fori_loop(0, pairs - 1, pair_to_pair, 0)
        if nt % 2:
            stage((2 * pairs - 2, 2 * t), (nt - 1, t))
            stage((nt - 1, t), None)
        else:
            stage((2 * pairs - 2, 2 * t), None)

    blk = pl.BlockSpec((lp, LANE), lambda g: (0, g))
    return pl.pallas_call(
        body, name="attn_fwd", grid=(HEADS // 2,),
        in_specs=[blk, blk, pl.BlockSpec((nt, LANE, t), lambda g: (0, g, 0)),
                  pl.BlockSpec((lp, LANE), lambda g: (0, 0), pipeline_mode=pl.Buffered(1))],
        out_specs=(blk, blk, pl.BlockSpec((lp, 2 * LANE), lambda g: (0, g))),
        out_shape=(jax.ShapeDtypeStruct((lp, D_ATTN), F32), jax.ShapeDtypeStruct((lp, D_ATTN), F32),
                   jax.ShapeDtypeStruct((lp, HEADS * LANE), F32)),
        scratch_shapes=[pltpu.VMEM((nt, 2 * t, 2 * t), F32), pltpu.VMEM((2 * t, t), F32),
                        pltpu.VMEM((2, SUBLANE, 2 * t), F32), pltpu.VMEM((2, SUBLANE, 2 * t), F32),
                        pltpu.VMEM((ext, 2 * t), F32), pltpu.VMEM((2, lp, LANE), F32)],
        compiler_params=_params(("parallel",)),
    )(q, k, v_t, cc)


def _attn_bwd(q, k, v, do, q_t, k_t, do_t, m, delta, ct4):
    lp = q.shape[0]
    t = ROW_TILE
    nt = lp // t

    def body(q_ref, k_ref, v_ref, do_ref, qt_ref, kt_ref, dot_ref, ma_ref, mb_ref, dla_ref, dlb_ref, cta_ref, ctb_ref,
             dq_ref, dk_ref, dv_ref, dc_ref, dq_acc, dk_acc, dv_acc):
        masks = _head_masks()
        ct_refs, m_refs, dl_refs = (cta_ref, ctb_ref), (ma_ref, mb_ref), (dla_ref, dlb_ref)
        below = lax.broadcasted_iota(jnp.int32, (t, t), 1) <= lax.broadcasted_iota(jnp.int32, (t, t), 0)
        top = lax.broadcasted_iota(jnp.int32, (LANE, 1), 0) < HEAD_DIM
        dq_acc[...] = jnp.zeros_like(dq_acc)

        on_first_diagonal = jnp.concatenate([below, jnp.ones((t, t), jnp.bool_)], axis=0)

        def k_block(j, _, with_next=True):
            c0 = pl.multiple_of(j * t, t)
            kj = k_ref[pl.ds(c0, t), :]
            vj = v_ref[pl.ds(c0, t), :]
            k2 = jnp.concatenate([jnp.where(hm, kj, 0).astype(MXU_DTYPE) for hm in masks], axis=0)
            v2 = jnp.concatenate([jnp.where(hm, vj, 0).astype(MXU_DTYPE) for hm in masks], axis=0)
            ck = [r[j] for r in ct_refs]
            ktj = kt_ref[j]
            k2t = jnp.concatenate([jnp.where(top, ktj, 0).astype(MXU_DTYPE), jnp.where(top, 0, ktj).astype(MXU_DTYPE)],
                                  axis=1)
            dk_acc[...] = jnp.zeros_like(dk_acc)
            dv_acc[...] = jnp.zeros_like(dv_acc)

            def q_block(i, colsums, mask=None, rows=t):
                r0 = pl.multiple_of(i * t, t)
                qi = q_ref[pl.ds(r0, rows), :]
                doi = do_ref[pl.ds(r0, rows), :]
                qti = jnp.concatenate([qt_ref[i + b] for b in range(rows // t)], axis=1)
                doti = jnp.concatenate([dot_ref[i + b] for b in range(rows // t)], axis=1)
                s2 = _dot_nt(qi, k2)
                dp2 = _dot_nt(doi, v2)
                out, ps, dss = [], [], []
                for hh in range(2):
                    s = (s2[:, hh * t:(hh + 1) * t] - ck[hh]) * LOG2E
                    if mask is not None:
                        s = jnp.where(mask, s, NEG)
                    p = jnp.exp2(_sub_rows(s, m_refs[hh][pl.ds(r0, rows), :])).astype(MXU_DTYPE)
                    ds32 = p.astype(F32) * _sub_rows(dp2[:, hh * t:(hh + 1) * t], dl_refs[hh][pl.ds(r0, rows), :])
                    ps.append(p)
                    dss.append(ds32.astype(MXU_DTYPE))
                    out.append(colsums[hh] + jnp.sum(ds32, axis=0, keepdims=True))
                ds_cat = jnp.concatenate(dss, axis=1)
                dv_acc[...] = dv_acc[...] + _dot(doti, jnp.concatenate(ps, axis=1))
                dk_acc[...] = dk_acc[...] + _dot(qti, ds_cat)
                dq_t = _dot(k2t, ds_cat.T)
                for b in range(rows // t):
                    dq_acc[i + b] = dq_acc[i + b] + dq_t[:, b * t:(b + 1) * t]
                return tuple(out)

            nq = ATTN_BWD_QBLOCKS
            colsums = (jnp.zeros((1, t), F32), jnp.zeros((1, t), F32))
            if with_next:
                colsums = q_block(j, colsums, on_first_diagonal, nq * t)
            else:
                colsums = q_block(j, colsums, below)
            first = j + (nq if with_next else 1)
            groups = (nt - first) // nq
            colsums = lax.fori_loop(0, groups, lambda p, c: q_block(first + nq * p, c, None, nq * t), colsums)
            colsums = lax.fori_loop(first + nq * groups, nt, q_block, colsums)
            for hh in range(2):
                dc_ref[hh, j] = -colsums[hh]
            own = lambda acc: jnp.concatenate([acc[:HEAD_DIM, :t], acc[HEAD_DIM:, t:]], axis=0).T
            dk_ref[pl.ds(c0, t), :] = own(dk_acc[...]).astype(dk_ref.dtype)
            dv_ref[pl.ds(c0, t), :] = own(dv_acc[...]).astype(dv_ref.dtype)
            return 0

        lax.fori_loop(0, nt - 1, k_block, 0)
        k_block(nt - 1, 0, with_next=False)
        for i in range(nt):
            dq_ref[i * t:(i + 1) * t, :] = (dq_acc[i].T * (HEAD_DIM ** -0.5)).astype(dq_ref.dtype)

    blk, ct_a, ct_b = _pair_specs(lp, nt, t)
    rep_a = pl.BlockSpec((lp, LANE), lambda g: (0, 2 * g))
    rep_b = pl.BlockSpec((lp, LANE), lambda g: (0, 2 * g + 1))
    tr_blk = pl.BlockSpec((nt, LANE, t), lambda g: (0, g, 0))
    return pl.pallas_call(
        body, name="attn_bwd", grid=(HEADS // 2,),
        in_specs=[blk] * 4 + [tr_blk, tr_blk, tr_blk, rep_a, rep_b, rep_a, rep_b, ct_a, ct_b],
        out_specs=(blk, blk, blk, pl.BlockSpec((2, nt, 1, t), lambda g: (g, 0, 0, 0))),
        out_shape=(jax.ShapeDtypeStruct((lp, D_ATTN), MXU_DTYPE),) * 3
                  + (jax.ShapeDtypeStruct((HEADS, nt, 1, t), F32),),
        scratch_shapes=[pltpu.VMEM((nt, LANE, t), F32), pltpu.VMEM((LANE, 2 * t), F32), pltpu.VMEM((LANE, 2 * t), F32)],
        compiler_params=_params(("parallel",)),
    )(q, k, v, do, q_t, k_t, do_t, m, m, delta, delta, ct4, ct4)


def _shift_down(prev8, cur, k):
    ext = jnp.concatenate([prev8, cur], axis=0)
    return pltpu.roll(ext, k, 0)[SUBLANE:, :]


def _shift_up(cur, next8, k):
    ext = jnp.concatenate([cur, next8], axis=0)
    n = ext.shape[0]
    return pltpu.roll(ext, n - k, 0)[:cur.shape[0], :]


def _post(o, l_sum, rest, x2, meta_blk, tgt2, w_out, attn_g, conv_g, final_g, conv_w8):
    lp = o.shape[0]
    t = ROW_TILE
    nt = lp // t
    n_sub = t // LANE
    hb = t // SUBLANE

    def body(*refs):
        o_ref, l_ref, za_ref, gb_ref, gc_ref, xc_ref, zc_ref, gch_ref, xch_ref = refs[:9]
        x_refs = refs[9:9 + n_sub]
        mb = refs[9 + n_sub]
        t_refs = refs[10 + n_sub:10 + 2 * n_sub]
        wo_ref, ag_ref, cg_ref, fg_ref, cw_ref, gm_ref, hr_ref = refs[10 + 2 * n_sub:17 + 2 * n_sub]
        (dout_ref, do_ref, dot_ref, dl_ref, dza_ref, dgb_ref, dzc_ref, dcv_ref,
         loss_ref, gf_ref, gag_ref, gcg_ref, gwo_ref) = refs[17 + 2 * n_sub:]
        i = pl.program_id(0)

        @pl.when(i == 0)
        def _():
            for r in (loss_ref, gf_ref, gag_ref, gcg_ref, gwo_ref):
                r[...] = jnp.zeros_like(r)

        gmat = gm_ref[...]
        inv_g = 1.0 / HEAD_DIM
        o_v = o_ref[...]
        ra = lax.rsqrt(_group_sum(o_v * o_v, gmat, STAT_TERMS) * inv_g + EPS)
        n_a = o_v * ra
        a_n = n_a * ag_ref[...]
        za = za_ref[...]
        sig_a = _sigmoid(za)
        sz_a = za * sig_a
        y_a = a_n * sz_a
        gb = gb_ref[...]
        gc = gc_ref[...]
        xc = xc_ref[...]
        cx = gc * xc
        cx_prev = jnp.where(i == 0, 0.0, gch_ref[...] * xch_ref[...])
        conv = (cw_ref[0:1, :] * _shift_down(cx_prev, cx, 2) + cw_ref[1:2, :] * _shift_down(cx_prev, cx, 1)
                + cw_ref[2:3, :] * cx)
        e = gb * conv
        re = lax.rsqrt(_group_sum(e * e, gmat, STAT_TERMS) * inv_g + EPS)
        n_e = e * re
        e_n = n_e * cg_ref[...]
        zc = zc_ref[...]
        sig_c = _sigmoid(zc)
        sz_c = zc * sig_c
        y_c = e_n * sz_c
        mix = jnp.concatenate([y_a, y_c], axis=-1)
        mix_b = mix.astype(MXU_DTYPE)
        first = jnp.where(i == 0, mb[...], x_refs[0][...])
        h = jnp.concatenate([first] + [r[...] for r in x_refs[1:]], axis=0)
        out = h + _dot(mix_b, wo_ref[...])
        r2 = lax.rsqrt(jnp.mean(out * out, axis=-1, keepdims=True) + EPS)
        n_f = out * r2
        y = n_f * fg_ref[...]
        tgt = jnp.concatenate([r[...] for r in t_refs], axis=0)
        valid = (i * t + lax.broadcasted_iota(jnp.int32, (t, 1), 0)) >= FRONT
        diff = jnp.where(valid, y - tgt, 0.0)
        loss_ref[...] = loss_ref[...] + 0.5 * jnp.sum(jnp.sum(diff * diff, axis=-1, keepdims=True) * (1.0 / D_MODEL))
        dy = diff * (1.0 / D_MODEL)
        gf_ref[...] = gf_ref[...] + jnp.sum(dy * n_f, axis=0, keepdims=True)
        dn = dy * fg_ref[...]
        d_out = r2 * (dn - n_f * jnp.mean(dn * n_f, axis=-1, keepdims=True))
        dout_ref[...] = d_out
        d_out_b = d_out.astype(MXU_DTYPE)
        d_mix = _dot_nt(d_out_b, wo_ref[...])
        gwo_ref[...] = gwo_ref[...] + _dot(mix.T.astype(MXU_DTYPE), d_out_b)
        d_ya = d_mix[:, :D_ATTN]
        d_yc = d_mix[:, D_ATTN:]
        d_an = d_ya * sz_a
        dza_ref[...] = (d_ya * a_n * (sig_a * (1.0 + za * (1.0 - sig_a)))).astype(dza_ref.dtype)
        gag_ref[...] = gag_ref[...] + jnp.sum(d_an * n_a, axis=0, keepdims=True)
        dn_a = d_an * ag_ref[...]
        d_o = ra * (dn_a - n_a * (_group_sum(dn_a * n_a, gmat, STAT_TERMS) * inv_g))
        d_o_l = d_o / l_ref[...]
        d_o_b = d_o_l.astype(do_ref.dtype)
        do_ref[...] = d_o_b
        dot_ref[...] = d_o_l.T.astype(dot_ref.dtype)
        dl_ref[...] = _group_sum(d_o_b.astype(F32) * o_v, hr_ref[...])
        d_en = d_yc * sz_c
        dzc_ref[...] = (d_yc * e_n * (sig_c * (1.0 + zc * (1.0 - sig_c)))).astype(dzc_ref.dtype)
        gcg_ref[...] = gcg_ref[...] + jnp.sum(d_en * n_e, axis=0, keepdims=True)
        dn_e = d_en * cg_ref[...]
        d_e = re * (dn_e - n_e * (_group_sum(dn_e * n_e, gmat, STAT_TERMS) * inv_g))
        dgb_ref[...] = (d_e * conv).astype(dgb_ref.dtype)
        dcv_ref[...] = d_e * gb

    head_rep = jnp.where((lax.broadcasted_iota(jnp.int32, (D_ATTN, HEADS * LANE), 0) >> 6)
                         == (lax.broadcasted_iota(jnp.int32, (D_ATTN, HEADS * LANE), 1) >> 7), 1.0, 0.0).astype(MXU_DTYPE)
    row_blk = lambda cols: pl.BlockSpec((t, cols), lambda i: (i, 0))
    rest_blk = lambda s: pl.BlockSpec((t, 512), functools.partial(lambda i, s: (i, s), s=s))
    halo = lambda s: pl.BlockSpec((SUBLANE, 512), functools.partial(lambda i, s: (jnp.maximum(i * hb - 1, 0), s), s=s))
    const = lambda shape: pl.BlockSpec(shape, lambda i: (0, 0))
    acc = lambda shape: pl.BlockSpec(shape, lambda i: (0, 0))
    return pl.pallas_call(
        body, name="post_fwd_bwd", grid=(nt,),
        in_specs=[row_blk(D_ATTN), row_blk(D_ATTN)] + [rest_blk(s) for s in range(5)] + [halo(2), halo(3)]
                 + _x_block_specs(n_sub, LANE) + [const((LANE, D_MODEL))] + _x_block_specs(n_sub, LANE)
                 + [const((D_MODEL, D_MODEL)), const((1, D_ATTN)), const((1, D_CONV)), const((1, D_MODEL)),
                    const((SUBLANE, D_CONV)), const((D_ATTN, D_ATTN)), const((D_ATTN, HEADS * LANE))],
        out_specs=(row_blk(D_MODEL), row_blk(D_ATTN), pl.BlockSpec((None, D_ATTN, t), lambda i: (i, 0, 0)),
                   row_blk(HEADS * LANE), row_blk(D_ATTN), row_blk(D_CONV),
                   row_blk(D_CONV), row_blk(D_CONV),
                   acc((1, LANE)), acc((1, D_MODEL)), acc((1, D_ATTN)), acc((1, D_CONV)), acc((D_MODEL, D_MODEL))),
        out_shape=(jax.ShapeDtypeStruct((lp, D_MODEL), F32), jax.ShapeDtypeStruct((lp, D_ATTN), MXU_DTYPE),
                   jax.ShapeDtypeStruct((nt, D_ATTN, t), MXU_DTYPE), jax.ShapeDtypeStruct((lp, HEADS * LANE), F32), jax.ShapeDtypeStruct((lp, D_ATTN), MXU_DTYPE),
                   jax.ShapeDtypeStruct((lp, D_CONV), MXU_DTYPE), jax.ShapeDtypeStruct((lp, D_CONV), MXU_DTYPE),
                   jax.ShapeDtypeStruct((lp, D_CONV), F32),
                   jax.ShapeDtypeStruct((1, LANE), F32), jax.ShapeDtypeStruct((1, D_MODEL), F32),
                   jax.ShapeDtypeStruct((1, D_ATTN), F32), jax.ShapeDtypeStruct((1, D_CONV), F32),
                   jax.ShapeDtypeStruct((D_MODEL, D_MODEL), F32)),
        compiler_params=_params(("arbitrary",)),
    )(o, l_sum, *([rest] * 5), rest, rest, *([x2] * n_sub), meta_blk, *([tgt2] * n_sub),
      w_out, attn_g, conv_g, final_g, conv_w8, _group_matrix(), head_rep)


def _bwd_in(x2, meta_blk, norm_g, w_pad, bf_pad, fl, dc, dq, dk, dv, dza, dgb, dzc, dconv, rest, d_out, conv_w8):
    lp = fl.shape[0]
    t = ROW_TILE
    nt = lp // t
    n_sub = t // LANE
    hb = t // SUBLANE
    rev = lambda i: nt - 1 - i

    def body(*refs):
        x_refs = refs[:n_sub]
        (mb, g_ref, w_ref, bf_ref, fl_ref, dc_ref, dq_ref, dk_ref, dv_ref, dza_ref, dgb_ref, dzc_ref,
         dcv_ref, dcvn_ref, gc_ref, xc_ref, gch_ref, xch_ref, dout_ref, cw_ref, tri_ref) = refs[n_sub:n_sub + 21]
        dp_ref, gx_ref, front_ref, gn_ref, gbf_ref, gcw_ref, carry, dh_scr, gx_sems = refs[n_sub + 21:]
        step = pl.program_id(0)
        i = rev(step)

        @pl.when(step == 0)
        def _():
            for r in (gn_ref, gbf_ref, gcw_ref, carry):
                r[...] = jnp.zeros_like(r)

        dc8 = jnp.concatenate([dc_ref[...], jnp.zeros((LANE - HEADS, t), F32)], axis=0).T
        dlogf = _dot_exact(tri_ref[...], dc8) + carry[...]
        carry[...] = carry[...] + jnp.sum(dc8, axis=0, keepdims=True)
        z = fl_ref[...] + bf_ref[...]
        row = i * t + lax.broadcasted_iota(jnp.int32, (t, LANE), 0)
        d_f = jnp.where(row >= PAD_ROWS, dlogf * (1.0 / (1.0 + jnp.exp(z))), 0.0)
        gbf_ref[...] = gbf_ref[...] + jnp.sum(d_f, axis=0, keepdims=True)
        dcv = dcv_ref[...]
        dcv_next = jnp.where(i == nt - 1, 0.0, dcvn_ref[...])
        d_cx = (cw_ref[2:3, :] * dcv + cw_ref[1:2, :] * _shift_up(dcv, dcv_next, 1)
                + cw_ref[0:1, :] * _shift_up(dcv, dcv_next, 2))
        gc = gc_ref[...]
        xc = xc_ref[...]
        cx = gc * xc
        cx_prev = jnp.where(i == 0, 0.0, gch_ref[...] * xch_ref[...])
        rowi = lax.broadcasted_iota(jnp.int32, (SUBLANE, 1), 0)
        gcw = (jnp.where(rowi == 0, jnp.sum(dcv * _shift_down(cx_prev, cx, 2), axis=0, keepdims=True), 0.0)
               + jnp.where(rowi == 1, jnp.sum(dcv * _shift_down(cx_prev, cx, 1), axis=0, keepdims=True), 0.0)
               + jnp.where(rowi == 2, jnp.sum(dcv * cx, axis=0, keepdims=True), 0.0))
        gcw_ref[...] = gcw_ref[...] + gcw
        dp_ref[:, SEG_Q:SEG_Q + 512] = dq_ref[...]
        dp_ref[:, SEG_K:SEG_K + 512] = dk_ref[...]
        dp_ref[:, SEG_V:SEG_V + 512] = dv_ref[...]
        dp_ref[:, SEG_F:SEG_F + LANE] = d_f.astype(dp_ref.dtype)
        dp_ref[:, SEG_ZA:SEG_ZA + 512] = dza_ref[...]
        dp_ref[:, SEG_GB:SEG_GB + 512] = dgb_ref[...]
        dp_ref[:, SEG_GC:SEG_GC + 512] = (d_cx * xc).astype(dp_ref.dtype)
        dp_ref[:, SEG_XC:SEG_XC + 512] = (d_cx * gc).astype(dp_ref.dtype)
        dp_ref[:, SEG_ZC:SEG_ZC + 512] = dzc_ref[...]
        d_u = _dot(dp_ref[...], w_ref[...])
        first = jnp.where(i == 0, mb[...], x_refs[0][...])
        h = jnp.concatenate([first] + [r[...] for r in x_refs[1:]], axis=0)
        r1 = lax.rsqrt(jnp.mean(h * h, axis=-1, keepdims=True) + EPS)
        n_h = h * r1
        gn_ref[...] = gn_ref[...] + jnp.sum(d_u * n_h, axis=0, keepdims=True)
        dn = d_u * g_ref[...]
        d_h = dout_ref[...] + r1 * (dn - n_h * jnp.mean(dn * n_h, axis=-1, keepdims=True))
        slot = step % 2

        def to_grad_x(slot_, tile):
            return pltpu.make_async_copy(dh_scr.at[slot_], gx_ref.at[pl.ds(pl.multiple_of(tile * t - FRONT, SUBLANE), t)],
                                         gx_sems.at[slot_])

        @pl.when(step >= 2)
        def _():
            to_grad_x(slot, 1).wait()

        dh_scr[slot] = d_h

        @pl.when(i > 0)
        def _():
            to_grad_x(slot, i).start()

        @pl.when(i == 0)
        def _():
            front_ref[...] = d_h[:FRONT]
            rest_rows = pltpu.make_async_copy(dh_scr.at[slot, pl.ds(FRONT, t - FRONT)], gx_ref.at[pl.ds(0, t - FRONT)],
                                              gx_sems.at[slot])
            rest_rows.start()
            rest_rows.wait()
            if nt >= 2:
                to_grad_x(1 - slot, 1).wait()

    def x_specs():
        specs = [pl.BlockSpec((LANE, D_MODEL), lambda s: (jnp.maximum(n_sub * rev(s) - 1, 0), 0))]
        for b in range(1, n_sub):
            specs.append(pl.BlockSpec((LANE, D_MODEL), functools.partial(lambda s, b: (n_sub * rev(s) - 1 + b, 0), b=b)))
        return specs

    row_blk = lambda cols: pl.BlockSpec((t, cols), lambda s: (rev(s), 0))
    rest_blk = lambda k: pl.BlockSpec((t, 512), functools.partial(lambda s, k: (rev(s), k), k=k))
    halo_prev = lambda k: pl.BlockSpec(
        (SUBLANE, 512), functools.partial(lambda s, k: (jnp.maximum(rev(s) * hb - 1, 0), k), k=k))
    halo_next = pl.BlockSpec((SUBLANE, 512), lambda s: (jnp.minimum((rev(s) + 1) * hb, lp // SUBLANE - 1), 0))
    const = lambda shape: pl.BlockSpec(shape, lambda s: (0, 0))
    return pl.pallas_call(
        body, name="bwd_in", grid=(nt,),
        in_specs=x_specs() + [const((LANE, D_MODEL)), const((1, D_MODEL)),
                              pl.BlockSpec((D_IN_PAD, D_MODEL), lambda s: (0, 0), pipeline_mode=pl.Buffered(1)),
                              const((1, LANE)), row_blk(LANE),
                              pl.BlockSpec((HEADS, t), lambda s: (0, rev(s))),
                              row_blk(512), row_blk(512), row_blk(512), row_blk(512), row_blk(512), row_blk(512),
                              row_blk(512), halo_next, rest_blk(2), rest_blk(3), halo_prev(2), halo_prev(3),
                              row_blk(D_MODEL), const((SUBLANE, D_CONV)), const((t, t))],
        out_specs=(row_blk(D_IN_PAD), ANY, const((FRONT, D_MODEL)), const((1, D_MODEL)), const((1, LANE)),
                   const((SUBLANE, D_CONV))),
        out_shape=(jax.ShapeDtypeStruct((lp, D_IN_PAD), MXU_DTYPE), jax.ShapeDtypeStruct((lp - FRONT, D_MODEL), F32),
                   jax.ShapeDtypeStruct((FRONT, D_MODEL), F32),
                   jax.ShapeDtypeStruct((1, D_MODEL), F32), jax.ShapeDtypeStruct((1, LANE), F32),
                   jax.ShapeDtypeStruct((SUBLANE, D_CONV), F32)),
        scratch_shapes=[pltpu.VMEM((1, LANE), F32), pltpu.VMEM((2, t, D_MODEL), F32), pltpu.SemaphoreType.DMA((2,))],
        compiler_params=_params(("arbitrary",)),
    )(*([x2] * n_sub), meta_blk, norm_g, w_pad, bf_pad, fl, dc, dq, dk, dv, dza, dgb, dzc, dconv, dconv,
      rest, rest, rest, rest, d_out, conv_w8, _triangle(t, lower=False))


def _grad_w_in(u, dproj):
    lp = u.shape[0]
    tn = GW_COL_TILE
    tk = tn if lp % tn == 0 else ROW_TILE

    def body(d_ref, u_ref, o_ref, wire_ref):
        k = pl.program_id(1)

        @pl.when(k == 0)
        def _():
            o_ref[...] = jnp.zeros_like(o_ref)

        o_ref[...] = o_ref[...] + lax.dot_general(d_ref[...], u_ref[...], (((0,), (0,)), ((), ())),
                                                  preferred_element_type=F32)

        @pl.when(k == pl.num_programs(1) - 1)
        def _():
            wire_ref[...] = o_ref[...].astype(wire_ref.dtype)

    out_spec = pl.BlockSpec((tn, D_MODEL), lambda n, k: (n, 0))
    return pl.pallas_call(
        body, name="grad_w_in", grid=(D_IN_PAD // tn, lp // tk),
        in_specs=[pl.BlockSpec((tk, tn), lambda n, k: (k, n)), pl.BlockSpec((tk, D_MODEL), lambda n, k: (k, 0))],
        out_specs=(out_spec, out_spec),
        out_shape=(jax.ShapeDtypeStruct((D_IN_PAD, D_MODEL), F32), jax.ShapeDtypeStruct((D_IN_PAD, D_MODEL), WIRE_DTYPE)),
        compiler_params=_params(("parallel", "arbitrary")),
    )(dproj, u)


def _by_chip(own, others, me):
    by_mask = jnp.stack([own, others[1], others[0], others[2]])
    return [lax.dynamic_index_in_dim(by_mask, jnp.bitwise_xor(me, s), 0, keepdims=False) for s in range(N_CHIPS)]


def _both_halves(mine, other, c):
    return jnp.where(c == 0, jnp.concatenate([mine, other], axis=0), jnp.concatenate([other, mine], axis=0))


def _local_step(x2, tgt2, meta_full, norm_g, w_pad, b_f, conv_w_full, attn_g, conv_g, w_out_full, final_g):
    lp = x2.shape[0] + FRONT
    nt = lp // ROW_TILE
    meta_blk = jnp.concatenate([jnp.zeros((PAD_ROWS, D_MODEL), F32), meta_full], axis=0)
    bf_pad = jnp.pad(b_f, ((0, 0), (0, LANE - HEADS)))
    conv_w8 = jnp.pad(conv_w_full, ((0, SUBLANE - conv_w_full.shape[0]), (0, 0)))
    q, k, v, rest, fl, ct, u, q_t, k_t, v_t, cc = _in_proj(x2, meta_blk, norm_g, w_pad, bf_pad)
    ct4 = ct.reshape(SUBLANE, nt, 1, ROW_TILE)
    o, l_sum, m_max = _attn_fwd(q, k, v_t, cc)
    (d_out, d_o, do_t, delta, dza, dgb, dzc, dconv, loss, g_final, g_attn, g_convg, gw_out) = _post(
        o, l_sum, rest, x2, meta_blk, tgt2, w_out_full, attn_g, conv_g, final_g, conv_w8)
    dq, dk, dv, dc = _attn_bwd(q, k, v, d_o, q_t, k_t, do_t, m_max, delta, ct4)
    dproj, grad_x, d_front, g_norm, g_bf, g_cw = _bwd_in(x2, meta_blk, norm_g, w_pad, bf_pad, fl, dc.reshape(HEADS, lp), dq, dk, dv,
                                             dza, dgb, dzc, dconv, rest, d_out, conv_w8)
    gw_in, gw_in_wire = _grad_w_in(u, dproj)
    return dict(loss=loss, grad_x=grad_x, d_front=d_front, g_norm=g_norm, g_final=g_final, g_attn=g_attn, g_convg=g_convg, g_bf=g_bf,
                g_cw=g_cw, gw_out=gw_out, gw_in=gw_in, gw_in_wire=gw_in_wire)


def kernel(x, meta, norm_g, w_in, b_f, conv_w, attn_norm_g, conv_norm_g, w_out, final_norm_g, loss_target, m_meta, m_norm_g, m_w_in, m_b_f, m_conv_w, m_attn_norm_g, m_conv_norm_g, m_w_out, m_final_norm_g, v_meta, v_norm_g, v_w_in, v_b_f, v_conv_w, v_attn_norm_g, v_conv_norm_g, v_w_out, v_final_norm_g):
    cx_, cy_, cc_ = _position()
    chip = 2 * cx_ + cy_
    shard = w_in.shape[2]
    out_half = w_out.shape[1] // 2
    pick = lambda vals: jnp.where(chip == 0, vals[0], jnp.where(chip == 1, vals[1], jnp.where(chip == 2, vals[2], vals[3])))
    a_off, b_off = pick(A_OFF), pick(B_OFF)
    wt = jnp.transpose(w_in[0]).astype(MXU_DTYPE)
    wi = lax.dynamic_update_slice_in_dim(
        lax.dynamic_update_slice_in_dim(jnp.zeros((WIN_ROWS, D_MODEL), MXU_DTYPE), wt[:PIECE_A], a_off, 0),
        wt[PIECE_A:], b_off, 0)
    wo = w_out[0].astype(MXU_DTYPE)
    small = jnp.concatenate([meta, jnp.pad(conv_w[0], ((0, 8 - conv_w.shape[1]), (0, meta.shape[1] - conv_w.shape[2])))],
                            axis=0)
    gwi, gwo, gsm = _gather_weights(wi.reshape(2, WIN_HALF, D_MODEL), wo.reshape(2, out_half, D_MODEL), small)
    starts = jnp.stack([_window_start(jnp.bitwise_xor(chip, mask)) for mask in (0, 2, 1, 3)]).astype(jnp.int32)
    w_pad = _assemble_w(wi, gwi.reshape(3, WIN_ROWS, D_MODEL), starts)
    w_out_full = jnp.concatenate(_by_chip(wo, gwo.reshape(3, 2 * out_half, D_MODEL), chip), axis=0)
    small_full = jnp.concatenate(_by_chip(small, gsm, chip), axis=1)
    meta_full = small_full[:N_META]
    conv_w_full = jnp.concatenate([small_full[N_META:N_META + 3, 256 * s:256 * s + LANE] for s in range(N_CHIPS)], axis=1)
    final_g2 = final_norm_g.reshape(1, D_MODEL)
    r = _local_step(x[0], loss_target[0], meta_full, norm_g, w_pad, b_f, conv_w_full, attn_norm_g, conv_norm_g,
                    w_out_full, final_g2)
    grad_x = r["grad_x"][None]
    gb = r["gw_out"].reshape(N_CHIPS, 2, out_half, D_MODEL)
    wide = lambda a: jnp.pad(a, ((0, 0), (0, D_MODEL - a.shape[1])))
    pack = jnp.concatenate([
        r["g_norm"], r["g_final"], jnp.concatenate([r["g_attn"], r["g_convg"]], axis=1), wide(r["g_bf"]),
        wide(r["loss"]), jnp.zeros((3, D_MODEL), F32), r["d_front"][PAD_ROWS:], wide(r["g_cw"])], axis=0)
    ra, rb, packs = _pair_exchange(r["gw_in_wire"], gb, pack)
    c_idx = jnp.reshape(cc_, (1,)).astype(jnp.int32)
    chip_idx = jnp.reshape(chip, (1,)).astype(jnp.int32)
    pa, pa_wire = _pair_sum_windows(r["gw_in"], ra, c_idx)
    pb, pb_wire = _pair_sum(gb, rb, c_idx)
    xa, xb = _chip_exchange(pa_wire, pb_wire)
    ha = _chip_sum(pa, xa, chip_idx)
    hb = _chip_sum(pb, xb, chip_idx)
    oa, ob = _pair_share(ha, hb)
    g_window = _both_halves(ha, oa, cc_)
    g_w_in_t = jnp.concatenate([lax.dynamic_slice_in_dim(g_window, a_off, PIECE_A, 0),
                                lax.dynamic_slice_in_dim(g_window, b_off, shard - PIECE_A, 0)], axis=0)
    g_w_out = _both_halves(hb, ob, cc_)
    as_rows = lambda a: jnp.transpose(a, (2, 0, 1))
    g_w_in, d_w_in, nm_w_in, nv_w_in = (jnp.transpose(a, (1, 2, 0)) for a in _adamw_rows(
        as_rows(w_in), g_w_in_t, as_rows(m_w_in), as_rows(v_w_in)))
    d_w_out, nm_w_out, nv_w_out = (a[None] for a in _adamw_big(w_out[0], g_w_out, m_w_out[0], v_w_out[0], LANE))
    params = (norm_g, final_g2, attn_norm_g, conv_norm_g, b_f, meta, conv_w[0])
    ms = (m_norm_g, m_final_norm_g.reshape(1, D_MODEL), m_attn_norm_g, m_conv_norm_g, m_b_f, m_meta, m_conv_w[0])
    vs = (v_norm_g, v_final_norm_g.reshape(1, D_MODEL), v_attn_norm_g, v_conv_norm_g, v_b_f, v_meta, v_conv_w[0])
    loss, g_s, d_s, m_s, v_s = _small_update(pack, packs, params, ms, vs)

    def ordered(small_list, big_in, big_out):
        s_norm, s_final, s_attn, s_convg, s_bf, s_meta, s_cw = small_list
        return (s_meta, s_norm, big_in, s_bf, s_cw[None], s_attn, s_convg, big_out, s_final.reshape(D_MODEL))

    return (loss.reshape(()), grad_x,
            *ordered(g_s, g_w_in, g_w_out[None]), *ordered(d_s, d_w_in, d_w_out),
            *ordered(m_s, nm_w_in, nm_w_out), *ordered(v_s, nv_w_in, nv_w_out))
```

```python
import functools

import jax
import jax.numpy as jnp
from jax import lax
from jax.experimental import pallas as pl
from jax.experimental.pallas import tpu as pltpu

F32 = jnp.float32
MXU_DTYPE = jnp.bfloat16
WIRE_DTYPE = jnp.bfloat16

D_MODEL = 1024
N_META = 16
HEADS = 8
HEAD_DIM = 64
D_ATTN = HEADS * HEAD_DIM
D_CONV = 512
EPS = 1e-6
LANE = 128
SUBLANE = 8
ROW_TILE = 384
ATTN_UNROLL = 3
ATTN_BWD_QBLOCKS = 2
STAT_TERMS = 1
FRONT = LANE
PAD_ROWS = FRONT - N_META
NEG = -1e30
LOG2E = 1.4426950408889634
N_CHIPS = 4
N_DEV = 8
VMEM_LIMIT_BYTES = 60 * 1024 * 1024

SEG_Q, SEG_K, SEG_V, SEG_F, SEG_ZA, SEG_GB, SEG_GC, SEG_XC, SEG_ZC = (
    0, 512, 1024, 1536, 1664, 2176, 2688, 3200, 3712)
D_IN = 4104
D_IN_PAD = 4224
F_END = 1544
GW_COL_TILE = 1408
WIN_ROWS = 1152
WIN_HALF = WIN_ROWS // 2
WIN_START = (0, 1024, 2160, 3072)
PIECE_A = 518
A_OFF = (0, 2, 12, 126)
B_OFF = (518, 640, 530, 644)
ADAM_LR = 0.001
ADAM_B1 = 0.9
ADAM_B2 = 0.999
ADAM_EPS = 1e-08
ADAM_WD = 0.01
ADAM_STEP = 10

MESH = pl.DeviceIdType.MESH
ANY = pl.BlockSpec(memory_space=pl.ANY)

PACK_ROWS = 32
SLOT_NORM = (0, 1, 0, 1024)
SLOT_FINAL = (1, 2, 0, 1024)
SLOT_ATTN = (2, 3, 0, 512)
SLOT_CONVG = (2, 3, 512, 1024)
SLOT_BF = (3, 4, 0, 8)
SLOT_META = (8, 24, 0, 256)
SLOT_CONVW = (24, 27, 0, 128)
LOSS_ROW = 4


def _params(sem=None):
    return pltpu.CompilerParams(dimension_semantics=sem, vmem_limit_bytes=VMEM_LIMIT_BYTES)


def _sigmoid(z):
    return 1.0 / (1.0 + jnp.exp(-z))


def _dot(a, b):
    return jnp.dot(a, b, preferred_element_type=F32)


def _dot_nt(a, b):
    return lax.dot_general(a, b, (((1,), (1,)), ((), ())), preferred_element_type=F32)


def _dot_exact(ones, x):
    ones = ones.astype(MXU_DTYPE)
    total = None
    for _ in range(3):
        term = x.astype(MXU_DTYPE)
        x = x - term.astype(F32)
        total = _dot(ones, term) if total is None else total + _dot(ones, term)
    return total


def _group_matrix():
    r = lax.broadcasted_iota(jnp.int32, (D_ATTN, D_ATTN), 0) >> 6
    c = lax.broadcasted_iota(jnp.int32, (D_ATTN, D_ATTN), 1) >> 6
    return jnp.where(r == c, 1.0, 0.0).astype(MXU_DTYPE)


def _triangle(n, lower):
    r = lax.broadcasted_iota(jnp.int32, (n, n), 0)
    c = lax.broadcasted_iota(jnp.int32, (n, n), 1)
    return jnp.where((r >= c) if lower else (c >= r), 1.0, 0.0).astype(MXU_DTYPE)


def _group_sum(x, gmat, terms=2):
    hi = x.astype(MXU_DTYPE)
    if terms == 1:
        return _dot(hi, gmat)
    lo = (x - hi.astype(F32)).astype(MXU_DTYPE)
    return _dot(hi, gmat) + _dot(lo, gmat)


def _x_block_specs(n_sub, rows):
    specs = [pl.BlockSpec((rows, D_MODEL), lambda i: (jnp.maximum(n_sub * i - 1, 0), 0))]
    for b in range(1, n_sub):
        specs.append(pl.BlockSpec((rows, D_MODEL), functools.partial(lambda i, b: (n_sub * i - 1 + b, 0), b=b)))
    return specs


def _position():
    return lax.axis_index("x"), lax.axis_index("y"), lax.axis_index("c")


def _gather_weights(wi, wo, small):
    def body(wi_ref, wo_ref, sm_ref, gwi_ref, gwo_ref, gsm_ref, send_sems, recv_sems):
        x, y, c = _position()
        sibling = (x, y, 1 - c)
        chips = [(1 - x, y), (x, 1 - y), (1 - x, 1 - y)]

        def remote(k, src, dst, to):
            return pltpu.make_async_remote_copy(src_ref=src, dst_ref=dst, send_sem=send_sems.at[k],
                                                recv_sem=recv_sems.at[k], device_id=to, device_id_type=MESH)

        first, passed, landed = [], [], []
        for a, (src_ref, g_ref) in enumerate(((wi_ref, gwi_ref), (wo_ref, gwo_ref))):
            for j, (cx, cy) in enumerate(chips):
                slot = g_ref.at[j, c]
                first.append(remote(6 * a + j, src_ref.at[c], slot, (cx, cy, c)))
                landed.append(remote(6 * a + j, slot, slot, sibling))
                passed.append(remote(6 * a + 3 + j, slot, slot, sibling))
        for j, (cx, cy) in enumerate(chips):
            first.append(remote(12 + j, sm_ref, gsm_ref.at[j], (cx, cy, c)))
        for cp in first:
            cp.start()
        for arrived, onward in zip(landed, passed):
            arrived.wait_recv()
            onward.start()
        for a, g_ref in enumerate((gwi_ref, gwo_ref)):
            for j in range(3):
                remote(6 * a + 3 + j, g_ref.at[j, 1 - c], g_ref.at[j, 1 - c], sibling).wait_recv()
        for j in range(3):
            remote(12 + j, sm_ref, gsm_ref.at[j], sibling).wait_recv()
        for cp in first + passed:
            cp.wait_send()

    return pl.pallas_call(
        body, name="gather_weights",
        out_shape=(jax.ShapeDtypeStruct((3,) + wi.shape, wi.dtype), jax.ShapeDtypeStruct((3,) + wo.shape, wo.dtype),
                   jax.ShapeDtypeStruct((3,) + small.shape, small.dtype)),
        in_specs=[ANY, ANY, ANY], out_specs=(ANY, ANY, ANY),
        scratch_shapes=[pltpu.SemaphoreType.DMA((15,)), pltpu.SemaphoreType.DMA((15,))],
    )(wi, wo, small)


def _pair_exchange(gw, gb, pack):
    n_big = N_CHIPS + 1

    def body(gw_ref, gb_ref, p_ref, ra_ref, rb_ref, o_ref, send_sems, recv_sems):
        x, y, c = _position()
        sibling = (x, y, 1 - c)

        def remote(k, src, dst, to):
            return pltpu.make_async_remote_copy(src_ref=src, dst_ref=dst, send_sem=send_sems.at[k],
                                                recv_sem=recv_sems.at[k], device_id=to, device_id_type=MESH)

        copies = [remote(N_CHIPS, gb_ref.at[:, 1 - c], rb_ref, sibling)]
        for s, start in enumerate(WIN_START):
            rows = pl.ds(pl.multiple_of(start + WIN_HALF * (1 - c), 2 * SUBLANE), WIN_HALF)
            copies.append(remote(s, gw_ref.at[rows], ra_ref.at[s], sibling))
        for mask in range(1, N_DEV):
            peer = (1 - x if mask & 4 else x, 1 - y if mask & 2 else y, 1 - c if mask & 1 else c)
            copies.append(remote(n_big + mask - 1, p_ref, o_ref.at[mask - 1], peer))
        for cp in copies:
            cp.start()
        for cp in copies:
            cp.wait()

    n_sems = n_big + N_DEV - 1
    return pl.pallas_call(
        body, name="grad_pair_exchange",
        out_shape=(jax.ShapeDtypeStruct((N_CHIPS, WIN_HALF, D_MODEL), gw.dtype),
                   jax.ShapeDtypeStruct((N_CHIPS,) + gb.shape[2:], gb.dtype),
                   jax.ShapeDtypeStruct((N_DEV - 1,) + pack.shape, pack.dtype)),
        in_specs=[ANY, ANY, ANY], out_specs=(ANY, ANY, ANY),
        scratch_shapes=[pltpu.SemaphoreType.DMA((n_sems,)), pltpu.SemaphoreType.DMA((n_sems,))],
    )(gw, gb, pack)


def _chip_exchange(pa, pb):
    def body(pa_ref, pb_ref, ra_ref, rb_ref, send_sems, recv_sems):
        x, y, c = _position()
        chips = [(1 - x, y), (x, 1 - y), (1 - x, 1 - y)]
        copies = []
        for a, (src, dst) in enumerate(((pa_ref, ra_ref), (pb_ref, rb_ref))):
            for j, (cx, cy) in enumerate(chips):
                copies.append(pltpu.make_async_remote_copy(
                    src_ref=src.at[2 * cx + cy], dst_ref=dst.at[j], send_sem=send_sems.at[3 * a + j],
                    recv_sem=recv_sems.at[3 * a + j], device_id=(cx, cy, c), device_id_type=MESH))
        for cp in copies:
            cp.start()
        for cp in copies:
            cp.wait()

    return pl.pallas_call(
        body, name="grad_chip_exchange",
        out_shape=(jax.ShapeDtypeStruct((3,) + pa.shape[1:], pa.dtype),
                   jax.ShapeDtypeStruct((3,) + pb.shape[1:], pb.dtype)),
        in_specs=[ANY, ANY], out_specs=(ANY, ANY),
        scratch_shapes=[pltpu.SemaphoreType.DMA((6,)), pltpu.SemaphoreType.DMA((6,))],
    )(pa, pb)


def _pair_share(ha, hb):
    def body(ha_ref, hb_ref, oa_ref, ob_ref, send_sems, recv_sems):
        x, y, c = _position()
        copies = [pltpu.make_async_remote_copy(
            src_ref=src, dst_ref=dst, send_sem=send_sems.at[k], recv_sem=recv_sems.at[k],
            device_id=(x, y, 1 - c), device_id_type=MESH)
            for k, (src, dst) in enumerate(((ha_ref, oa_ref), (hb_ref, ob_ref)))]
        for cp in copies:
            cp.start()
        for cp in copies:
            cp.wait()

    return pl.pallas_call(
        body, name="grad_pair_share",
        out_shape=(jax.ShapeDtypeStruct(ha.shape, ha.dtype), jax.ShapeDtypeStruct(hb.shape, hb.dtype)),
        in_specs=[ANY, ANY], out_specs=(ANY, ANY),
        scratch_shapes=[pltpu.SemaphoreType.DMA((2,)), pltpu.SemaphoreType.DMA((2,))],
    )(ha, hb)


def _pair_sum(mine, recv, c_idx):
    rows, cols = mine.shape[2:]

    def body(c_ref, a_ref, b_ref, o_ref, send_ref):
        total = a_ref[...] + b_ref[...]
        o_ref[...] = total
        send_ref[...] = total.astype(send_ref.dtype)

    out_spec = pl.BlockSpec((None, rows, cols), lambda s, c_ref: (s, 0, 0))
    return pl.pallas_call(
        body, name="grad_pair_sum",
        grid_spec=pltpu.PrefetchScalarGridSpec(
            num_scalar_prefetch=1, grid=(N_CHIPS,),
            in_specs=[pl.BlockSpec((None, None, rows, cols), lambda s, c_ref: (s, c_ref[0], 0, 0)),
                      pl.BlockSpec((None, rows, cols), lambda s, c_ref: (s, 0, 0))],
            out_specs=(out_spec, out_spec)),
        out_shape=(jax.ShapeDtypeStruct(recv.shape, recv.dtype), jax.ShapeDtypeStruct(recv.shape, WIRE_DTYPE)),
        compiler_params=_params(("parallel",)),
    )(c_idx, mine, recv)


def _window_start(s):
    return jnp.where(s == 0, WIN_START[0], jnp.where(s == 1, WIN_START[1], jnp.where(s == 2, WIN_START[2], WIN_START[3])))


def _pair_sum_windows(gw, recv, c_idx):
    tr = WIN_HALF // 3

    def body(c_ref, a_ref, b_ref, o_ref, send_ref):
        total = a_ref[...] + b_ref[...].astype(F32)
        o_ref[...] = total
        send_ref[...] = total.astype(send_ref.dtype)

    out_spec = pl.BlockSpec((None, tr, D_MODEL), lambda s, i, c_ref: (s, i, 0))
    return pl.pallas_call(
        body, name="grad_pair_sum_windows",
        grid_spec=pltpu.PrefetchScalarGridSpec(
            num_scalar_prefetch=1, grid=(N_CHIPS, WIN_HALF // tr),
            in_specs=[pl.BlockSpec((pl.Element(tr), pl.Element(D_MODEL)),
                                   lambda s, i, c_ref: (pl.multiple_of(
                                       _window_start(s) + WIN_HALF * c_ref[0] + tr * i, SUBLANE), 0)),
                      pl.BlockSpec((None, tr, D_MODEL), lambda s, i, c_ref: (s, i, 0))],
            out_specs=(out_spec, out_spec)),
        out_shape=(jax.ShapeDtypeStruct(recv.shape, F32), jax.ShapeDtypeStruct(recv.shape, WIRE_DTYPE)),
        compiler_params=_params(("parallel", "parallel")),
    )(c_idx, gw, recv)


def _assemble_w(own, others, starts):
    def body(starts_ref, own_ref, oth_ref, o_ref):
        o_ref[...] = jnp.zeros_like(o_ref)
        for k in range(N_CHIPS):
            rows = pl.ds(pl.multiple_of(starts_ref[k], 2 * SUBLANE), WIN_ROWS)
            o_ref[rows, :] = o_ref[rows, :] + (own_ref[...] if k == 0 else oth_ref[k - 1])

    return pl.pallas_call(
        body, name="assemble_w",
        in_specs=[pl.BlockSpec(memory_space=pltpu.SMEM), pl.BlockSpec(memory_space=pltpu.VMEM),
                  pl.BlockSpec(memory_space=pltpu.VMEM)],
        out_specs=pl.BlockSpec(memory_space=pltpu.VMEM),
        out_shape=jax.ShapeDtypeStruct((D_IN_PAD, D_MODEL), own.dtype),
        compiler_params=_params(),
    )(starts, own, others)


def _chip_sum(psum, recv3, chip_idx):
    rows, cols = psum.shape[1:]
    tr = rows // 2

    def body(s_ref, p_ref, r0, r1, r2, o_ref):
        o_ref[...] = ((p_ref[...] + r0[...].astype(F32)) + r1[...].astype(F32)) + r2[...].astype(F32)

    return pl.pallas_call(
        body, name="grad_chip_sum",
        grid_spec=pltpu.PrefetchScalarGridSpec(
            num_scalar_prefetch=1, grid=(2,),
            in_specs=[pl.BlockSpec((None, tr, cols), lambda i, s_ref: (s_ref[0], i, 0))] +
                     [pl.BlockSpec((None, tr, cols), functools.partial(lambda i, s_ref, j: (j, i, 0), j=j))
                      for j in range(3)],
            out_specs=pl.BlockSpec((tr, cols), lambda i, s_ref: (i, 0))),
        out_shape=jax.ShapeDtypeStruct((rows, cols), psum.dtype),
        compiler_params=_params(("parallel",)),
    )(chip_idx, psum, recv3, recv3, recv3)


def _adamw_math(w, g, m, v):
    m = ADAM_B1 * m + (1.0 - ADAM_B1) * g
    v = ADAM_B2 * v + (1.0 - ADAM_B2) * (g * g)
    m_hat = m * (1.0 / (1.0 - ADAM_B1 ** ADAM_STEP))
    v_hat = v * (1.0 / (1.0 - ADAM_B2 ** ADAM_STEP))
    delta = -ADAM_LR * (m_hat / (jnp.sqrt(v_hat) + ADAM_EPS) + ADAM_WD * w)
    return delta, m, v


def _adamw_big(w, g, m, v, tr):
    rows, cols = w.shape
    assert rows % tr == 0 and g.shape[0] >= rows

    def body(w_ref, g_ref, m_ref, v_ref, d_out, m_out, v_out):
        d, m2, v2 = _adamw_math(w_ref[...], g_ref[...], m_ref[...], v_ref[...])
        d_out[...] = d
        m_out[...] = m2
        v_out[...] = v2

    spec = pl.BlockSpec((tr, cols), lambda i: (i, 0))
    sds = jax.ShapeDtypeStruct((rows, cols), F32)
    return pl.pallas_call(
        body, name="adamw_big", grid=(rows // tr,), in_specs=[spec] * 4, out_specs=(spec,) * 3,
        out_shape=(sds,) * 3, compiler_params=_params(("parallel",)),
    )(w, g, m, v)


def _adamw_rows(w3, g, m3, v3):
    rows, _, cols = w3.shape
    tc = 2 * LANE

    def body(w_ref, g_ref, m_ref, v_ref, g_out, d_out, m_out, v_out):
        g = g_ref[...]
        d, m2, v2 = _adamw_math(w_ref[:, 0, :], g, m_ref[:, 0, :], v_ref[:, 0, :])
        g_out[:, 0, :] = g
        d_out[:, 0, :] = d
        m_out[:, 0, :] = m2
        v_out[:, 0, :] = v2

    spec3 = pl.BlockSpec((rows, 1, tc), lambda i: (0, 0, i))
    sds = jax.ShapeDtypeStruct((rows, 1, cols), F32)
    return pl.pallas_call(
        body, name="adamw_rows", grid=(cols // tc,),
        in_specs=[spec3, pl.BlockSpec((rows, tc), lambda i: (0, i)), spec3, spec3], out_specs=(spec3,) * 4,
        out_shape=(sds,) * 4, compiler_params=_params(("parallel",)),
    )(w3, g, m3, v3)


def _small_update(own, others, params, ms, vs):
    slots = (SLOT_NORM, SLOT_FINAL, SLOT_ATTN, SLOT_CONVG, SLOT_BF, SLOT_META, SLOT_CONVW)
    n = len(slots)

    def body(*refs):
        own_ref, gp_ref = refs[:2]
        w_refs, m_refs, v_refs = refs[2:2 + n], refs[2 + n:2 + 2 * n], refs[2 + 2 * n:2 + 3 * n]
        outs = refs[2 + 3 * n:3 + 7 * n]
        loss_ref = outs[0]
        g_outs, d_outs, m_outs, v_outs = (outs[1 + k * n:1 + (k + 1) * n] for k in range(4))
        g_scr, w_scr, m_scr, v_scr = refs[3 + 7 * n:]
        x, y, c = _position()
        shard = 2 * x + y
        me = 4 * x + 2 * y + c
        tot = None
        for d in range(N_DEV):
            rel = jnp.bitwise_xor(me, d)
            term = jnp.where(rel == 0, own_ref[...], gp_ref[jnp.maximum(rel, 1) - 1])
            tot = term if tot is None else tot + term
        r0, r1, _, _ = SLOT_META
        meta_sel = tot[r0:r1, 0:256]
        cw_sel = tot[24:32, 0:128]
        for k in range(1, N_CHIPS):
            meta_sel = jnp.where(shard == k, tot[r0:r1, 256 * k:256 * (k + 1)], meta_sel)
            cw_sel = jnp.where(shard == k, tot[24:32, 128 * k:128 * (k + 1)], cw_sel)
        zeros = jnp.zeros((PACK_ROWS, D_MODEL), F32)
        for scr in (g_scr, w_scr, m_scr, v_scr):
            scr[...] = zeros
        g_scr[0:8, :] = tot[0:8, :]
        g_scr[r0:r1, 0:256] = meta_sel
        g_scr[24:32, 0:128] = cw_sel
        for (a, b, c0, c1), w_ref, m_ref, v_ref in zip(slots, w_refs, m_refs, v_refs):
            w_scr[a:b, c0:c1] = w_ref[...]
            m_scr[a:b, c0:c1] = m_ref[...]
            v_scr[a:b, c0:c1] = v_ref[...]
        loss_ref[...] = g_scr[LOSS_ROW:LOSS_ROW + 1, 0:1]
        d, m2, v2 = _adamw_math(w_scr[...], g_scr[...], m_scr[...], v_scr[...])
        w_scr[...] = d
        m_scr[...] = m2
        v_scr[...] = v2
        for (a, b, c0, c1), g_o, d_o, m_o, v_o in zip(slots, g_outs, d_outs, m_outs, v_outs):
            g_o[...] = g_scr[a:b, c0:c1]
            d_o[...] = w_scr[a:b, c0:c1]
            m_o[...] = m_scr[a:b, c0:c1]
            v_o[...] = v_scr[a:b, c0:c1]

    shapes = [jax.ShapeDtypeStruct(p.shape, F32) for p in params]
    out = pl.pallas_call(
        body, name="small_update",
        out_shape=[jax.ShapeDtypeStruct((1, 1), F32)] + shapes * 4,
        scratch_shapes=[pltpu.VMEM((PACK_ROWS, D_MODEL), F32)] * 4,
        compiler_params=_params(),
    )(own, others, *params, *ms, *vs)
    return out[0], out[1:1 + n], out[1 + n:1 + 2 * n], out[1 + 2 * n:1 + 3 * n], out[1 + 3 * n:1 + 4 * n]


def _in_proj(x2, meta_blk, norm_g, w_pad, bf_pad):
    seq = x2.shape[0]
    lp = seq + FRONT
    t = ROW_TILE
    nt = lp // t
    n_sub = t // LANE

    def body(*refs):
        x_refs = refs[:n_sub]
        mb, g_ref, w_ref, bf_ref, tri_ref = refs[n_sub:n_sub + 5]
        q_ref, k_ref, v_ref, rest_ref, fl_ref, ct_ref, u_ref, qt_ref, kt_ref, vt_ref, cc_ref, carry = refs[n_sub + 5:]
        i = pl.program_id(0)

        @pl.when(i == 0)
        def _():
            carry[...] = jnp.zeros_like(carry)

        first = jnp.where(i == 0, mb[...], x_refs[0][...])
        h = jnp.concatenate([first] + [r[...] for r in x_refs[1:]], axis=0)
        ms = jnp.mean(h * h, axis=-1, keepdims=True)
        u = ((h * lax.rsqrt(ms + EPS)) * g_ref[...]).astype(MXU_DTYPE)
        u_ref[...] = u

        def seg(a, width):
            return _dot_nt(u, w_ref[a:a + width, :])

        q_tile = seg(SEG_Q, D_ATTN) * (HEAD_DIM ** -0.5)
        q_ref[...] = q_tile.astype(MXU_DTYPE)
        qt_ref[...] = q_tile.T.astype(MXU_DTYPE)
        k_tile = seg(SEG_K, D_ATTN)
        k_ref[...] = k_tile.astype(MXU_DTYPE)
        kt_ref[...] = k_tile.T.astype(MXU_DTYPE)
        v_tile = seg(SEG_V, D_ATTN)
        v_ref[...] = v_tile.astype(MXU_DTYPE)
        vt_ref[...] = v_tile.T.astype(MXU_DTYPE)
        for s in range(5):
            rest_ref[:, 512 * s:512 * (s + 1)] = seg(SEG_ZA + 512 * s, 512)
        fl = seg(SEG_F, LANE)
        fl_ref[...] = fl
        z = fl + bf_ref[...]
        logf = jnp.minimum(z, 0.0) - jnp.log(1.0 + jnp.exp(-jnp.abs(z)))
        row = i * t + lax.broadcasted_iota(jnp.int32, (t, LANE), 0)
        logf = jnp.where(row >= PAD_ROWS, logf, 0.0)
        cs = _dot_exact(tri_ref[...], logf) + carry[...]
        carry[...] = carry[...] + jnp.sum(logf, axis=0, keepdims=True)
        col = i * t + lax.broadcasted_iota(jnp.int32, (SUBLANE, t), 1)
        ct_ref[...] = jnp.where(col >= PAD_ROWS, cs.T[0:SUBLANE, :], -NEG)
        cc_ref[...] = jnp.where(row >= PAD_ROWS, cs, -NEG)

    row_blk = lambda cols: pl.BlockSpec((t, cols), lambda i: (i, 0))
    tr_blk = pl.BlockSpec((None, D_ATTN, t), lambda i: (i, 0, 0))
    const = lambda shape: pl.BlockSpec(shape, lambda i: (0, 0))
    return pl.pallas_call(
        body, name="in_proj", grid=(nt,),
        in_specs=_x_block_specs(n_sub, LANE) + [const((LANE, D_MODEL)), const((1, D_MODEL)),
                                                pl.BlockSpec((D_IN_PAD, D_MODEL), lambda i: (0, 0),
                                                             pipeline_mode=pl.Buffered(1)),
                                                const((1, LANE)), const((t, t))],
        out_specs=(row_blk(D_ATTN), row_blk(D_ATTN), row_blk(D_ATTN), row_blk(5 * 512), row_blk(LANE),
                   pl.BlockSpec((SUBLANE, t), lambda i: (0, i)), row_blk(D_MODEL), tr_blk, tr_blk, tr_blk, row_blk(LANE)),
        out_shape=(jax.ShapeDtypeStruct((lp, D_ATTN), MXU_DTYPE), jax.ShapeDtypeStruct((lp, D_ATTN), MXU_DTYPE),
                   jax.ShapeDtypeStruct((lp, D_ATTN), MXU_DTYPE), jax.ShapeDtypeStruct((lp, 5 * 512), F32),
                   jax.ShapeDtypeStruct((lp, LANE), F32),
                   jax.ShapeDtypeStruct((SUBLANE, lp), F32), jax.ShapeDtypeStruct((lp, D_MODEL), MXU_DTYPE),
                   jax.ShapeDtypeStruct((nt, D_ATTN, t), MXU_DTYPE), jax.ShapeDtypeStruct((nt, D_ATTN, t), MXU_DTYPE),
                   jax.ShapeDtypeStruct((nt, D_ATTN, t), MXU_DTYPE), jax.ShapeDtypeStruct((lp, LANE), F32)),
        scratch_shapes=[pltpu.VMEM((1, LANE), F32)],
        compiler_params=_params(("arbitrary",)),
    )(*([x2] * n_sub), meta_blk, norm_g, w_pad, bf_pad, _triangle(t, lower=True))


def _head_masks():
    lane = lax.broadcasted_iota(jnp.int32, (1, LANE), 1)
    return lane < HEAD_DIM, lane >= HEAD_DIM


def _pair_specs(lp, nt, t):
    blk = pl.BlockSpec((lp, LANE), lambda g: (0, g))
    ct_a = pl.BlockSpec((None, nt, 1, t), lambda g: (2 * g, 0, 0, 0))
    ct_b = pl.BlockSpec((None, nt, 1, t), lambda g: (2 * g + 1, 0, 0, 0))
    return blk, ct_a, ct_b


def _sub_rows(s, col):
    return jnp.concatenate([s[:, a * LANE:(a + 1) * LANE] - col for a in range(s.shape[1] // LANE)], axis=1)


def _loop_unrolled(lo, hi, step, init, n):
    def group(jj, carry):
        for k in range(n):
            carry = step(lo + n * jj + k, carry)
        return carry

    groups = (hi - lo) // n
    carry = lax.fori_loop(0, groups, group, init)
    return lax.fori_loop(lo + n * groups, hi, step, carry)


def _attn_fwd(q, k, v_t, cc):
    lp = q.shape[0]
    t = ROW_TILE
    nt = lp // t
    ext = LANE + 2 * SUBLANE

    def body(q_ref, k_ref, vt_ref, cc_ref, o_ref, l_ref, m_ref, s_scr, last_scr, m_scr, mfin_scr, acc_scr, c_scr):
        masks = _head_masks()
        lane = lax.broadcasted_iota(jnp.int32, (1, LANE), 1)
        for hh in range(2):
            picked = jnp.where(lane == 2 * pl.program_id(0) + hh, cc_ref[...], 0.0)
            c_scr[hh] = jnp.broadcast_to(jnp.sum(picked, axis=-1, keepdims=True), (lp, LANE))
        visible = lax.broadcasted_iota(jnp.int32, (t, t), 0) <= lax.broadcasted_iota(jnp.int32, (t, t), 1)
        top = lax.broadcasted_iota(jnp.int32, (LANE, 1), 0) < HEAD_DIM
        second_head = (lax.broadcasted_iota(jnp.int32, (2 * SUBLANE, 2 * t), 1) >= t).astype(jnp.int32)
        ones_rows = jnp.where(lax.broadcasted_iota(jnp.int32, (2 * SUBLANE, 2 * t), 0) == second_head,
                              1.0, 0.0).astype(MXU_DTYPE)

        on_first_diagonal = jnp.concatenate([visible, jnp.ones((t, t), jnp.bool_)], axis=1)

        def scores(j, queries):
            kj = k_ref[pl.ds(pl.multiple_of(j * t, t), t), :]
            return _dot_nt(jnp.concatenate([jnp.where(hm, kj, 0).astype(MXU_DTYPE) for hm in masks], axis=0), queries)

        def biased(s2, j, hh):
            return _sub_rows(s2[hh * t:(hh + 1) * t, :], c_scr[hh, pl.ds(pl.multiple_of(j * t, t), t), :]) * LOG2E

        def track_max(hh, s, lo, hi):
            m = m_scr[hh, :, lo:hi]
            for a in range(t // SUBLANE):
                m = jnp.maximum(m, s[a * SUBLANE:(a + 1) * SUBLANE, :])
            m_scr[hh, :, lo:hi] = m

        def probabilities(scores_of, ms_cols):
            return jnp.concatenate([jnp.exp2(scores_of(hh) - ms_cols[hh]).astype(MXU_DTYPE) for hh in range(2)], axis=0)

        def values(j):
            vtj = vt_ref[j]
            v2 = jnp.concatenate([jnp.where(top, vtj, 0).astype(MXU_DTYPE),
                                  jnp.where(top, 0, vtj).astype(MXU_DTYPE)], axis=1)
            return jnp.concatenate([v2, ones_rows], axis=0)

        def stage(done, ahead):
            if ahead is not None:
                i_a, rows_a = ahead
                qa = q_ref[pl.ds(pl.multiple_of(i_a * t, t), rows_a), :]
                m_scr[...] = jnp.full(m_scr.shape, NEG, F32)

                def max_step(j, mask=None):
                    s2 = scores(j, qa)
                    for hh in range(2):
                        s = biased(s2, j, hh)
                        if mask is not None:
                            s = jnp.where(mask, s, NEG)
                        s_scr[j, hh * t:(hh + 1) * t, 0:rows_a] = s
                        track_max(hh, s, 0, rows_a)

            if done is not None:
                i_d, rows_d = done
                r0 = pl.multiple_of(i_d * t, t)
                ms = [mfin_scr[hh, 0:1, 0:rows_d] for hh in range(2)]
                acc_scr[...] = jnp.zeros(acc_scr.shape, F32)

                def key_step(j, carry):
                    p = probabilities(lambda hh: s_scr[j, hh * t:(hh + 1) * t, 0:rows_d], ms)
                    acc_scr[:, 0:rows_d] = acc_scr[:, 0:rows_d] + _dot(values(j), p)
                    if ahead is not None:
                        max_step(j)
                    return carry

                _loop_unrolled(0, i_d + 1, key_step, 0, ATTN_UNROLL)
                if rows_d == 2 * t:
                    p = probabilities(lambda hh: last_scr[hh * t:(hh + 1) * t, :], [m[:, t:] for m in ms])
                    acc_scr[:, t:rows_d] = acc_scr[:, t:rows_d] + _dot(values(i_d + 1), p)
                acc = acc_scr[:, 0:rows_d]
                l_pair = jnp.where(top, acc[LANE:LANE + 1], acc[LANE + 1:LANE + 2])
                o_ref[pl.ds(r0, rows_d), :] = (acc[:LANE] / l_pair).T
                l_ref[pl.ds(r0, rows_d), :] = l_pair.T
                for hh in range(2):
                    m_ref[pl.ds(r0, rows_d), hh * LANE:(hh + 1) * LANE] = jnp.broadcast_to(ms[hh], (LANE, rows_d)).T

            if ahead is not None:
                if done is not None:
                    max_step(i_a - 1)
                max_step(i_a, on_first_diagonal if rows_a == 2 * t else visible)
                if rows_a == 2 * t:
                    s2 = scores(i_a + 1, qa[t:])
                    for hh in range(2):
                        s = jnp.where(visible, biased(s2, i_a + 1, hh), NEG)
                        last_scr[hh * t:(hh + 1) * t, :] = s
                        track_max(hh, s, t, rows_a)
                for hh in range(2):
                    mfin_scr[hh, :, 0:rows_a] = jnp.broadcast_to(jnp.max(m_scr[hh, :, 0:rows_a], axis=0, keepdims=True),
                                                                 (SUBLANE, rows_a))

        pairs = nt // 2
        stage(None, (0, 2 * t))

        def pair_to_pair(u, _):
            stage((2 * u, 2 * t), (2 * u + 2, 2 * t))
            return 0

        lax.fori_loop(0, pairs - 1, pair_to_pair, 0)
        if nt % 2:
            stage((2 * pairs - 2, 2 * t), (nt - 1, t))
            stage((nt - 1, t), None)
        else:
            stage((2 * pairs - 2, 2 * t), None)

    blk = pl.BlockSpec((lp, LANE), lambda g: (0, g))
    return pl.pallas_call(
        body, name="attn_fwd", grid=(HEADS // 2,),
        in_specs=[blk, blk, pl.BlockSpec((nt, LANE, t), lambda g: (0, g, 0)),
                  pl.BlockSpec((lp, LANE), lambda g: (0, 0), pipeline_mode=pl.Buffered(1))],
        out_specs=(blk, blk, pl.BlockSpec((lp, 2 * LANE), lambda g: (0, g))),
        out_shape=(jax.ShapeDtypeStruct((lp, D_ATTN), F32), jax.ShapeDtypeStruct((lp, D_ATTN), F32),
                   jax.ShapeDtypeStruct((lp, HEADS * LANE), F32)),
        scratch_shapes=[pltpu.VMEM((nt, 2 * t, 2 * t), F32), pltpu.VMEM((2 * t, t), F32),
                        pltpu.VMEM((2, SUBLANE, 2 * t), F32), pltpu.VMEM((2, SUBLANE, 2 * t), F32),
                        pltpu.VMEM((ext, 2 * t), F32), pltpu.VMEM((2, lp, LANE), F32)],
        compiler_params=_params(("parallel",)),
    )(q, k, v_t, cc)


def _attn_bwd(q, k, v, do, q_t, k_t, do_t, m, delta, ct4):
    lp = q.shape[0]
    t = ROW_TILE
    nt = lp // t

    def body(q_ref, k_ref, v_ref, do_ref, qt_ref, kt_ref, dot_ref, ma_ref, mb_ref, dla_ref, dlb_ref, cta_ref, ctb_ref,
             dq_ref, dk_ref, dv_ref, dc_ref, dq_acc, dk_acc, dv_acc, p_scr, ds_scr):
        masks = _head_masks()
        ct_refs, m_refs, dl_refs = (cta_ref, ctb_ref), (ma_ref, mb_ref), (dla_ref, dlb_ref)
        below = lax.broadcasted_iota(jnp.int32, (t, t), 1) <= lax.broadcasted_iota(jnp.int32, (t, t), 0)
        top = lax.broadcasted_iota(jnp.int32, (LANE, 1), 0) < HEAD_DIM
        dq_acc[...] = jnp.zeros_like(dq_acc)

        on_first_diagonal = jnp.concatenate([below, jnp.ones((t, t), jnp.bool_)], axis=0)

        def k_block(j, _, with_next=True):
            c0 = pl.multiple_of(j * t, t)
            kj = k_ref[pl.ds(c0, t), :]
            vj = v_ref[pl.ds(c0, t), :]
            k2 = jnp.concatenate([jnp.where(hm, kj, 0).astype(MXU_DTYPE) for hm in masks], axis=0)
            v2 = jnp.concatenate([jnp.where(hm, vj, 0).astype(MXU_DTYPE) for hm in masks], axis=0)
            ck = [r[j] for r in ct_refs]
            ktj = kt_ref[j]
            k2t = jnp.concatenate([jnp.where(top, ktj, 0).astype(MXU_DTYPE), jnp.where(top, 0, ktj).astype(MXU_DTYPE)],
                                  axis=1)
            dk_acc[...] = jnp.zeros_like(dk_acc)
            dv_acc[...] = jnp.zeros_like(dv_acc)

            def scores_part(i, colsums, slot, mask=None, rows=t):
                r0 = pl.multiple_of(i * t, t)
                s2 = _dot_nt(q_ref[pl.ds(r0, rows), :], k2)
                dp2 = _dot_nt(do_ref[pl.ds(r0, rows), :], v2)
                out = []
                for hh in range(2):
                    s = (s2[:, hh * t:(hh + 1) * t] - ck[hh]) * LOG2E
                    if mask is not None:
                        s = jnp.where(mask, s, NEG)
                    p = jnp.exp2(_sub_rows(s, m_refs[hh][pl.ds(r0, rows), :])).astype(MXU_DTYPE)
                    ds32 = p.astype(F32) * _sub_rows(dp2[:, hh * t:(hh + 1) * t], dl_refs[hh][pl.ds(r0, rows), :])
                    p_scr[slot, 0:rows, hh * t:(hh + 1) * t] = p
                    ds_scr[slot, 0:rows, hh * t:(hh + 1) * t] = ds32.astype(MXU_DTYPE)
                    out.append(colsums[hh] + jnp.sum(ds32, axis=0, keepdims=True))
                return tuple(out)

            def grads_part(i, slot, rows=t):
                qti = jnp.concatenate([qt_ref[i + b] for b in range(rows // t)], axis=1)
                doti = jnp.concatenate([dot_ref[i + b] for b in range(rows // t)], axis=1)
                ds_cat = ds_scr[slot, 0:rows, :]
                dv_acc[...] = dv_acc[...] + _dot(doti, p_scr[slot, 0:rows, :])
                dk_acc[...] = dk_acc[...] + _dot(qti, ds_cat)
                dq_t = _dot(k2t, ds_cat.T)
                for b in range(rows // t):
                    dq_acc[i + b] = dq_acc[i + b] + dq_t[:, b * t:(b + 1) * t]

            def q_block(i, colsums, mask=None):
                colsums = scores_part(i, colsums, 0, mask)
                grads_part(i, 0)
                return colsums

            nq = ATTN_BWD_QBLOCKS
            colsums = (jnp.zeros((1, t), F32), jnp.zeros((1, t), F32))
            if with_next:
                wide = nq * t
                pairs = (nt - j) // nq

                def two_groups(u, c, mask=None):
                    i = j + 2 * nq * u
                    c = scores_part(i, c, 0, mask, wide)
                    c = scores_part(i + nq, c, 1, None, wide)
                    grads_part(i, 0, wide)
                    grads_part(i + nq, 1, wide)
                    return c

                def one_group(i, c, mask=None):
                    c = scores_part(i, c, 0, mask, wide)
                    grads_part(i, 0, wide)
                    return c

                steps = pairs // 2
                alone = (pairs == 1).astype(jnp.int32)
                left_over = lax.rem(pairs, 2) - alone
                colsums = lax.fori_loop(0, jnp.minimum(steps, 1), functools.partial(two_groups, mask=on_first_diagonal),
                                        colsums)
                colsums = lax.fori_loop(1, steps, two_groups, colsums)
                colsums = lax.fori_loop(0, alone, lambda u, c: one_group(j, c, on_first_diagonal), colsums)
                colsums = lax.fori_loop(0, left_over, lambda u, c: one_group(j + nq * (pairs - 1), c), colsums)
                colsums = lax.fori_loop(j + nq * pairs, nt, q_block, colsums)
            else:
                colsums = q_block(j, colsums, below)
            for hh in range(2):
                dc_ref[hh, j] = -colsums[hh]
            own = lambda acc: jnp.concatenate([acc[:HEAD_DIM, :t], acc[HEAD_DIM:, t:]], axis=0).T
            dk_ref[pl.ds(c0, t), :] = own(dk_acc[...]).astype(dk_ref.dtype)
            dv_ref[pl.ds(c0, t), :] = own(dv_acc[...]).astype(dv_ref.dtype)
            return 0

        lax.fori_loop(0, nt - 1, k_block, 0)
        k_block(nt - 1, 0, with_next=False)
        for i in range(nt):
            dq_ref[i * t:(i + 1) * t, :] = (dq_acc[i].T * (HEAD_DIM ** -0.5)).astype(dq_ref.dtype)

    blk, ct_a, ct_b = _pair_specs(lp, nt, t)
    rep_a = pl.BlockSpec((lp, LANE), lambda g: (0, 2 * g))
    rep_b = pl.BlockSpec((lp, LANE), lambda g: (0, 2 * g + 1))
    tr_blk = pl.BlockSpec((nt, LANE, t), lambda g: (0, g, 0))
    return pl.pallas_call(
        body, name="attn_bwd", grid=(HEADS // 2,),
        in_specs=[blk] * 4 + [tr_blk, tr_blk, tr_blk, rep_a, rep_b, rep_a, rep_b, ct_a, ct_b],
        out_specs=(blk, blk, blk, pl.BlockSpec((2, nt, 1, t), lambda g: (g, 0, 0, 0))),
        out_shape=(jax.ShapeDtypeStruct((lp, D_ATTN), MXU_DTYPE),) * 3
                  + (jax.ShapeDtypeStruct((HEADS, nt, 1, t), F32),),
        scratch_shapes=[pltpu.VMEM((nt, LANE, t), F32), pltpu.VMEM((LANE, 2 * t), F32), pltpu.VMEM((LANE, 2 * t), F32),
                        pltpu.VMEM((2, 2 * t, 2 * t), MXU_DTYPE), pltpu.VMEM((2, 2 * t, 2 * t), MXU_DTYPE)],
        compiler_params=_params(("parallel",)),
    )(q, k, v, do, q_t, k_t, do_t, m, m, delta, delta, ct4, ct4)


def _shift_down(prev8, cur, k):
    ext = jnp.concatenate([prev8, cur], axis=0)
    return pltpu.roll(ext, k, 0)[SUBLANE:, :]


def _shift_up(cur, next8, k):
    ext = jnp.concatenate([cur, next8], axis=0)
    n = ext.shape[0]
    return pltpu.roll(ext, n - k, 0)[:cur.shape[0], :]


def _post(o, l_sum, rest, x2, meta_blk, tgt2, w_out, attn_g, conv_g, final_g, conv_w8):
    lp = o.shape[0]
    t = ROW_TILE
    nt = lp // t
    n_sub = t // LANE
    hb = t // SUBLANE

    def body(*refs):
        o_ref, l_ref, za_ref, gb_ref, gc_ref, xc_ref, zc_ref, gch_ref, xch_ref = refs[:9]
        x_refs = refs[9:9 + n_sub]
        mb = refs[9 + n_sub]
        t_refs = refs[10 + n_sub:10 + 2 * n_sub]
        wo_ref, ag_ref, cg_ref, fg_ref, cw_ref, gm_ref, hr_ref = refs[10 + 2 * n_sub:17 + 2 * n_sub]
        (dout_ref, do_ref, dot_ref, dl_ref, dza_ref, dgb_ref, dzc_ref, dcv_ref,
         loss_ref, gf_ref, gag_ref, gcg_ref, gwo_ref) = refs[17 + 2 * n_sub:]
        i = pl.program_id(0)

        @pl.when(i == 0)
        def _():
            for r in (loss_ref, gf_ref, gag_ref, gcg_ref, gwo_ref):
                r[...] = jnp.zeros_like(r)

        gmat = gm_ref[...]
        inv_g = 1.0 / HEAD_DIM
        o_v = o_ref[...]
        ra = lax.rsqrt(_group_sum(o_v * o_v, gmat, STAT_TERMS) * inv_g + EPS)
        n_a = o_v * ra
        a_n = n_a * ag_ref[...]
        za = za_ref[...]
        sig_a = _sigmoid(za)
        sz_a = za * sig_a
        y_a = a_n * sz_a
        gb = gb_ref[...]
        gc = gc_ref[...]
        xc = xc_ref[...]
        cx = gc * xc
        cx_prev = jnp.where(i == 0, 0.0, gch_ref[...] * xch_ref[...])
        conv = (cw_ref[0:1, :] * _shift_down(cx_prev, cx, 2) + cw_ref[1:2, :] * _shift_down(cx_prev, cx, 1)
                + cw_ref[2:3, :] * cx)
        e = gb * conv
        re = lax.rsqrt(_group_sum(e * e, gmat, STAT_TERMS) * inv_g + EPS)
        n_e = e * re
        e_n = n_e * cg_ref[...]
        zc = zc_ref[...]
        sig_c = _sigmoid(zc)
        sz_c = zc * sig_c
        y_c = e_n * sz_c
        mix = jnp.concatenate([y_a, y_c], axis=-1)
        mix_b = mix.astype(MXU_DTYPE)
        first = jnp.where(i == 0, mb[...], x_refs[0][...])
        h = jnp.concatenate([first] + [r[...] for r in x_refs[1:]], axis=0)
        out = h + _dot(mix_b, wo_ref[...])
        r2 = lax.rsqrt(jnp.mean(out * out, axis=-1, keepdims=True) + EPS)
        n_f = out * r2
        y = n_f * fg_ref[...]
        tgt = jnp.concatenate([r[...] for r in t_refs], axis=0)
        valid = (i * t + lax.broadcasted_iota(jnp.int32, (t, 1), 0)) >= FRONT
        diff = jnp.where(valid, y - tgt, 0.0)
        loss_ref[...] = loss_ref[...] + 0.5 * jnp.sum(jnp.sum(diff * diff, axis=-1, keepdims=True) * (1.0 / D_MODEL))
        dy = diff * (1.0 / D_MODEL)
        gf_ref[...] = gf_ref[...] + jnp.sum(dy * n_f, axis=0, keepdims=True)
        dn = dy * fg_ref[...]
        d_out = r2 * (dn - n_f * jnp.mean(dn * n_f, axis=-1, keepdims=True))
        dout_ref[...] = d_out
        d_out_b = d_out.astype(MXU_DTYPE)
        d_mix = _dot_nt(d_out_b, wo_ref[...])
        gwo_ref[...] = gwo_ref[...] + _dot(mix.T.astype(MXU_DTYPE), d_out_b)
        d_ya = d_mix[:, :D_ATTN]
        d_yc = d_mix[:, D_ATTN:]
        d_an = d_ya * sz_a
        dza_ref[...] = (d_ya * a_n * (sig_a * (1.0 + za * (1.0 - sig_a)))).astype(dza_ref.dtype)
        gag_ref[...] = gag_ref[...] + jnp.sum(d_an * n_a, axis=0, keepdims=True)
        dn_a = d_an * ag_ref[...]
        d_o = ra * (dn_a - n_a * (_group_sum(dn_a * n_a, gmat, STAT_TERMS) * inv_g))
        d_o_l = d_o / l_ref[...]
        d_o_b = d_o_l.astype(do_ref.dtype)
        do_ref[...] = d_o_b
        dot_ref[...] = d_o_l.T.astype(dot_ref.dtype)
        dl_ref[...] = _group_sum(d_o_b.astype(F32) * o_v, hr_ref[...])
        d_en = d_yc * sz_c
        dzc_ref[...] = (d_yc * e_n * (sig_c * (1.0 + zc * (1.0 - sig_c)))).astype(dzc_ref.dtype)
        gcg_ref[...] = gcg_ref[...] + jnp.sum(d_en * n_e, axis=0, keepdims=True)
        dn_e = d_en * cg_ref[...]
        d_e = re * (dn_e - n_e * (_group_sum(dn_e * n_e, gmat, STAT_TERMS) * inv_g))
        dgb_ref[...] = (d_e * conv).astype(dgb_ref.dtype)
        dcv_ref[...] = d_e * gb

    head_rep = jnp.where((lax.broadcasted_iota(jnp.int32, (D_ATTN, HEADS * LANE), 0) >> 6)
                         == (lax.broadcasted_iota(jnp.int32, (D_ATTN, HEADS * LANE), 1) >> 7), 1.0, 0.0).astype(MXU_DTYPE)
    row_blk = lambda cols: pl.BlockSpec((t, cols), lambda i: (i, 0))
    rest_blk = lambda s: pl.BlockSpec((t, 512), functools.partial(lambda i, s: (i, s), s=s))
    halo = lambda s: pl.BlockSpec((SUBLANE, 512), functools.partial(lambda i, s: (jnp.maximum(i * hb - 1, 0), s), s=s))
    const = lambda shape: pl.BlockSpec(shape, lambda i: (0, 0))
    acc = lambda shape: pl.BlockSpec(shape, lambda i: (0, 0))
    return pl.pallas_call(
        body, name="post_fwd_bwd", grid=(nt,),
        in_specs=[row_blk(D_ATTN), row_blk(D_ATTN)] + [rest_blk(s) for s in range(5)] + [halo(2), halo(3)]
                 + _x_block_specs(n_sub, LANE) + [const((LANE, D_MODEL))] + _x_block_specs(n_sub, LANE)
                 + [const((D_MODEL, D_MODEL)), const((1, D_ATTN)), const((1, D_CONV)), const((1, D_MODEL)),
                    const((SUBLANE, D_CONV)), const((D_ATTN, D_ATTN)), const((D_ATTN, HEADS * LANE))],
        out_specs=(row_blk(D_MODEL), row_blk(D_ATTN), pl.BlockSpec((None, D_ATTN, t), lambda i: (i, 0, 0)),
                   row_blk(HEADS * LANE), row_blk(D_ATTN), row_blk(D_CONV),
                   row_blk(D_CONV), row_blk(D_CONV),
                   acc((1, LANE)), acc((1, D_MODEL)), acc((1, D_ATTN)), acc((1, D_CONV)), acc((D_MODEL, D_MODEL))),
        out_shape=(jax.ShapeDtypeStruct((lp, D_MODEL), F32), jax.ShapeDtypeStruct((lp, D_ATTN), MXU_DTYPE),
                   jax.ShapeDtypeStruct((nt, D_ATTN, t), MXU_DTYPE), jax.ShapeDtypeStruct((lp, HEADS * LANE), F32), jax.ShapeDtypeStruct((lp, D_ATTN), MXU_DTYPE),
                   jax.ShapeDtypeStruct((lp, D_CONV), MXU_DTYPE), jax.ShapeDtypeStruct((lp, D_CONV), MXU_DTYPE),
                   jax.ShapeDtypeStruct((lp, D_CONV), F32),
                   jax.ShapeDtypeStruct((1, LANE), F32), jax.ShapeDtypeStruct((1, D_MODEL), F32),
                   jax.ShapeDtypeStruct((1, D_ATTN), F32), jax.ShapeDtypeStruct((1, D_CONV), F32),
                   jax.ShapeDtypeStruct((D_MODEL, D_MODEL), F32)),
        compiler_params=_params(("arbitrary",)),
    )(o, l_sum, *([rest] * 5), rest, rest, *([x2] * n_sub), meta_blk, *([tgt2] * n_sub),
      w_out, attn_g, conv_g, final_g, conv_w8, _group_matrix(), head_rep)


def _bwd_in(x2, meta_blk, norm_g, w_pad, bf_pad, fl, dc, dq, dk, dv, dza, dgb, dzc, dconv, rest, d_out, conv_w8):
    lp = fl.shape[0]
    t = ROW_TILE
    nt = lp // t
    n_sub = t // LANE
    hb = t // SUBLANE
    rev = lambda i: nt - 1 - i

    def body(*refs):
        x_refs = refs[:n_sub]
        (mb, g_ref, w_ref, bf_ref, fl_ref, dc_ref, dq_ref, dk_ref, dv_ref, dza_ref, dgb_ref, dzc_ref,
         dcv_ref, dcvn_ref, gc_ref, xc_ref, gch_ref, xch_ref, dout_ref, cw_ref, tri_ref) = refs[n_sub:n_sub + 21]
        dp_ref, gx_ref, front_ref, gn_ref, gbf_ref, gcw_ref, carry, dh_scr, gx_sems = refs[n_sub + 21:]
        step = pl.program_id(0)
        i = rev(step)

        @pl.when(step == 0)
        def _():
            for r in (gn_ref, gbf_ref, gcw_ref, carry):
                r[...] = jnp.zeros_like(r)

        dc8 = jnp.concatenate([dc_ref[...], jnp.zeros((LANE - HEADS, t), F32)], axis=0).T
        dlogf = _dot_exact(tri_ref[...], dc8) + carry[...]
        carry[...] = carry[...] + jnp.sum(dc8, axis=0, keepdims=True)
        z = fl_ref[...] + bf_ref[...]
        row = i * t + lax.broadcasted_iota(jnp.int32, (t, LANE), 0)
        d_f = jnp.where(row >= PAD_ROWS, dlogf * (1.0 / (1.0 + jnp.exp(z))), 0.0)
        gbf_ref[...] = gbf_ref[...] + jnp.sum(d_f, axis=0, keepdims=True)
        dcv = dcv_ref[...]
        dcv_next = jnp.where(i == nt - 1, 0.0, dcvn_ref[...])
        d_cx = (cw_ref[2:3, :] * dcv + cw_ref[1:2, :] * _shift_up(dcv, dcv_next, 1)
                + cw_ref[0:1, :] * _shift_up(dcv, dcv_next, 2))
        gc = gc_ref[...]
        xc = xc_ref[...]
        cx = gc * xc
        cx_prev = jnp.where(i == 0, 0.0, gch_ref[...] * xch_ref[...])
        rowi = lax.broadcasted_iota(jnp.int32, (SUBLANE, 1), 0)
        gcw = (jnp.where(rowi == 0, jnp.sum(dcv * _shift_down(cx_prev, cx, 2), axis=0, keepdims=True), 0.0)
               + jnp.where(rowi == 1, jnp.sum(dcv * _shift_down(cx_prev, cx, 1), axis=0, keepdims=True), 0.0)
               + jnp.where(rowi == 2, jnp.sum(dcv * cx, axis=0, keepdims=True), 0.0))
        gcw_ref[...] = gcw_ref[...] + gcw
        dp_ref[:, SEG_Q:SEG_Q + 512] = dq_ref[...]
        dp_ref[:, SEG_K:SEG_K + 512] = dk_ref[...]
        dp_ref[:, SEG_V:SEG_V + 512] = dv_ref[...]
        dp_ref[:, SEG_F:SEG_F + LANE] = d_f.astype(dp_ref.dtype)
        dp_ref[:, SEG_ZA:SEG_ZA + 512] = dza_ref[...]
        dp_ref[:, SEG_GB:SEG_GB + 512] = dgb_ref[...]
        dp_ref[:, SEG_GC:SEG_GC + 512] = (d_cx * xc).astype(dp_ref.dtype)
        dp_ref[:, SEG_XC:SEG_XC + 512] = (d_cx * gc).astype(dp_ref.dtype)
        dp_ref[:, SEG_ZC:SEG_ZC + 512] = dzc_ref[...]
        d_u = _dot(dp_ref[...], w_ref[...])
        first = jnp.where(i == 0, mb[...], x_refs[0][...])
        h = jnp.concatenate([first] + [r[...] for r in x_refs[1:]], axis=0)
        r1 = lax.rsqrt(jnp.mean(h * h, axis=-1, keepdims=True) + EPS)
        n_h = h * r1
        gn_ref[...] = gn_ref[...] + jnp.sum(d_u * n_h, axis=0, keepdims=True)
        dn = d_u * g_ref[...]
        d_h = dout_ref[...] + r1 * (dn - n_h * jnp.mean(dn * n_h, axis=-1, keepdims=True))
        slot = step % 2

        def to_grad_x(slot_, tile):
            return pltpu.make_async_copy(dh_scr.at[slot_], gx_ref.at[pl.ds(pl.multiple_of(tile * t - FRONT, SUBLANE), t)],
                                         gx_sems.at[slot_])

        @pl.when(step >= 2)
        def _():
            to_grad_x(slot, 1).wait()

        dh_scr[slot] = d_h

        @pl.when(i > 0)
        def _():
            to_grad_x(slot, i).start()

        @pl.when(i == 0)
        def _():
            front_ref[...] = d_h[:FRONT]
            rest_rows = pltpu.make_async_copy(dh_scr.at[slot, pl.ds(FRONT, t - FRONT)], gx_ref.at[pl.ds(0, t - FRONT)],
                                              gx_sems.at[slot])
            rest_rows.start()
            rest_rows.wait()
            if nt >= 2:
                to_grad_x(1 - slot, 1).wait()

    def x_specs():
        specs = [pl.BlockSpec((LANE, D_MODEL), lambda s: (jnp.maximum(n_sub * rev(s) - 1, 0), 0))]
        for b in range(1, n_sub):
            specs.append(pl.BlockSpec((LANE, D_MODEL), functools.partial(lambda s, b: (n_sub * rev(s) - 1 + b, 0), b=b)))
        return specs

    row_blk = lambda cols: pl.BlockSpec((t, cols), lambda s: (rev(s), 0))
    rest_blk = lambda k: pl.BlockSpec((t, 512), functools.partial(lambda s, k: (rev(s), k), k=k))
    halo_prev = lambda k: pl.BlockSpec(
        (SUBLANE, 512), functools.partial(lambda s, k: (jnp.maximum(rev(s) * hb - 1, 0), k), k=k))
    halo_next = pl.BlockSpec((SUBLANE, 512), lambda s: (jnp.minimum((rev(s) + 1) * hb, lp // SUBLANE - 1), 0))
    const = lambda shape: pl.BlockSpec(shape, lambda s: (0, 0))
    return pl.pallas_call(
        body, name="bwd_in", grid=(nt,),
        in_specs=x_specs() + [const((LANE, D_MODEL)), const((1, D_MODEL)),
                              pl.BlockSpec((D_IN_PAD, D_MODEL), lambda s: (0, 0), pipeline_mode=pl.Buffered(1)),
                              const((1, LANE)), row_blk(LANE),
                              pl.BlockSpec((HEADS, t), lambda s: (0, rev(s))),
                              row_blk(512), row_blk(512), row_blk(512), row_blk(512), row_blk(512), row_blk(512),
                              row_blk(512), halo_next, rest_blk(2), rest_blk(3), halo_prev(2), halo_prev(3),
                              row_blk(D_MODEL), const((SUBLANE, D_CONV)), const((t, t))],
        out_specs=(row_blk(D_IN_PAD), ANY, const((FRONT, D_MODEL)), const((1, D_MODEL)), const((1, LANE)),
                   const((SUBLANE, D_CONV))),
        out_shape=(jax.ShapeDtypeStruct((lp, D_IN_PAD), MXU_DTYPE), jax.ShapeDtypeStruct((lp - FRONT, D_MODEL), F32),
                   jax.ShapeDtypeStruct((FRONT, D_MODEL), F32),
                   jax.ShapeDtypeStruct((1, D_MODEL), F32), jax.ShapeDtypeStruct((1, LANE), F32),
                   jax.ShapeDtypeStruct((SUBLANE, D_CONV), F32)),
        scratch_shapes=[pltpu.VMEM((1, LANE), F32), pltpu.VMEM((2, t, D_MODEL), F32), pltpu.SemaphoreType.DMA((2,))],
        compiler_params=_params(("arbitrary",)),
    )(*([x2] * n_sub), meta_blk, norm_g, w_pad, bf_pad, fl, dc, dq, dk, dv, dza, dgb, dzc, dconv, dconv,
      rest, rest, rest, rest, d_out, conv_w8, _triangle(t, lower=False))


def _grad_w_in(u, dproj):
    lp = u.shape[0]
    tn = GW_COL_TILE
    tk = tn if lp % tn == 0 else ROW_TILE

    def body(d_ref, u_ref, o_ref, wire_ref):
        k = pl.program_id(1)

        @pl.when(k == 0)
        def _():
            o_ref[...] = jnp.zeros_like(o_ref)

        o_ref[...] = o_ref[...] + lax.dot_general(d_ref[...], u_ref[...], (((0,), (0,)), ((), ())),
                                                  preferred_element_type=F32)

        @pl.when(k == pl.num_programs(1) - 1)
        def _():
            wire_ref[...] = o_ref[...].astype(wire_ref.dtype)

    out_spec = pl.BlockSpec((tn, D_MODEL), lambda n, k: (n, 0))
    return pl.pallas_call(
        body, name="grad_w_in", grid=(D_IN_PAD // tn, lp // tk),
        in_specs=[pl.BlockSpec((tk, tn), lambda n, k: (k, n)), pl.BlockSpec((tk, D_MODEL), lambda n, k: (k, 0))],
        out_specs=(out_spec, out_spec),
        out_shape=(jax.ShapeDtypeStruct((D_IN_PAD, D_MODEL), F32), jax.ShapeDtypeStruct((D_IN_PAD, D_MODEL), WIRE_DTYPE)),
        compiler_params=_params(("parallel", "arbitrary")),
    )(dproj, u)


def _by_chip(own, others, me):
    by_mask = jnp.stack([own, others[1], others[0], others[2]])
    return [lax.dynamic_index_in_dim(by_mask, jnp.bitwise_xor(me, s), 0, keepdims=False) for s in range(N_CHIPS)]


def _both_halves(mine, other, c):
    return jnp.where(c == 0, jnp.concatenate([mine, other], axis=0), jnp.concatenate([other, mine], axis=0))


def _local_step(x2, tgt2, meta_full, norm_g, w_pad, b_f, conv_w_full, attn_g, conv_g, w_out_full, final_g):
    lp = x2.shape[0] + FRONT
    nt = lp // ROW_TILE
    meta_blk = jnp.concatenate([jnp.zeros((PAD_ROWS, D_MODEL), F32), meta_full], axis=0)
    bf_pad = jnp.pad(b_f, ((0, 0), (0, LANE - HEADS)))
    conv_w8 = jnp.pad(conv_w_full, ((0, SUBLANE - conv_w_full.shape[0]), (0, 0)))
    q, k, v, rest, fl, ct, u, q_t, k_t, v_t, cc = _in_proj(x2, meta_blk, norm_g, w_pad, bf_pad)
    ct4 = ct.reshape(SUBLANE, nt, 1, ROW_TILE)
    o, l_sum, m_max = _attn_fwd(q, k, v_t, cc)
    (d_out, d_o, do_t, delta, dza, dgb, dzc, dconv, loss, g_final, g_attn, g_convg, gw_out) = _post(
        o, l_sum, rest, x2, meta_blk, tgt2, w_out_full, attn_g, conv_g, final_g, conv_w8)
    dq, dk, dv, dc = _attn_bwd(q, k, v, d_o, q_t, k_t, do_t, m_max, delta, ct4)
    dproj, grad_x, d_front, g_norm, g_bf, g_cw = _bwd_in(x2, meta_blk, norm_g, w_pad, bf_pad, fl, dc.reshape(HEADS, lp), dq, dk, dv,
                                             dza, dgb, dzc, dconv, rest, d_out, conv_w8)
    gw_in, gw_in_wire = _grad_w_in(u, dproj)
    return dict(loss=loss, grad_x=grad_x, d_front=d_front, g_norm=g_norm, g_final=g_final, g_attn=g_attn, g_convg=g_convg, g_bf=g_bf,
                g_cw=g_cw, gw_out=gw_out, gw_in=gw_in, gw_in_wire=gw_in_wire)


def kernel(x, meta, norm_g, w_in, b_f, conv_w, attn_norm_g, conv_norm_g, w_out, final_norm_g, loss_target, m_meta, m_norm_g, m_w_in, m_b_f, m_conv_w, m_attn_norm_g, m_conv_norm_g, m_w_out, m_final_norm_g, v_meta, v_norm_g, v_w_in, v_b_f, v_conv_w, v_attn_norm_g, v_conv_norm_g, v_w_out, v_final_norm_g):
    cx_, cy_, cc_ = _position()
    chip = 2 * cx_ + cy_
    shard = w_in.shape[2]
    out_half = w_out.shape[1] // 2
    pick = lambda vals: jnp.where(chip == 0, vals[0], jnp.where(chip == 1, vals[1], jnp.where(chip == 2, vals[2], vals[3])))
    a_off, b_off = pick(A_OFF), pick(B_OFF)
    wt = jnp.transpose(w_in[0]).astype(MXU_DTYPE)
    wi = lax.dynamic_update_slice_in_dim(
        lax.dynamic_update_slice_in_dim(jnp.zeros((WIN_ROWS, D_MODEL), MXU_DTYPE), wt[:PIECE_A], a_off, 0),
        wt[PIECE_A:], b_off, 0)
    wo = w_out[0].astype(MXU_DTYPE)
    small = jnp.concatenate([meta, jnp.pad(conv_w[0], ((0, 8 - conv_w.shape[1]), (0, meta.shape[1] - conv_w.shape[2])))],
                            axis=0)
    gwi, gwo, gsm = _gather_weights(wi.reshape(2, WIN_HALF, D_MODEL), wo.reshape(2, out_half, D_MODEL), small)
    starts = jnp.stack([_window_start(jnp.bitwise_xor(chip, mask)) for mask in (0, 2, 1, 3)]).astype(jnp.int32)
    w_pad = _assemble_w(wi, gwi.reshape(3, WIN_ROWS, D_MODEL), starts)
    w_out_full = jnp.concatenate(_by_chip(wo, gwo.reshape(3, 2 * out_half, D_MODEL), chip), axis=0)
    small_full = jnp.concatenate(_by_chip(small, gsm, chip), axis=1)
    meta_full = small_full[:N_META]
    conv_w_full = jnp.concatenate([small_full[N_META:N_META + 3, 256 * s:256 * s + LANE] for s in range(N_CHIPS)], axis=1)
    final_g2 = final_norm_g.reshape(1, D_MODEL)
    r = _local_step(x[0], loss_target[0], meta_full, norm_g, w_pad, b_f, conv_w_full, attn_norm_g, conv_norm_g,
                    w_out_full, final_g2)
    grad_x = r["grad_x"][None]
    gb = r["gw_out"].reshape(N_CHIPS, 2, out_half, D_MODEL)
    wide = lambda a: jnp.pad(a, ((0, 0), (0, D_MODEL - a.shape[1])))
    pack = jnp.concatenate([
        r["g_norm"], r["g_final"], jnp.concatenate([r["g_attn"], r["g_convg"]], axis=1), wide(r["g_bf"]),
        wide(r["loss"]), jnp.zeros((3, D_MODEL), F32), r["d_front"][PAD_ROWS:], wide(r["g_cw"])], axis=0)
    ra, rb, packs = _pair_exchange(r["gw_in_wire"], gb, pack)
    c_idx = jnp.reshape(cc_, (1,)).astype(jnp.int32)
    chip_idx = jnp.reshape(chip, (1,)).astype(jnp.int32)
    pa, pa_wire = _pair_sum_windows(r["gw_in"], ra, c_idx)
    pb, pb_wire = _pair_sum(gb, rb, c_idx)
    xa, xb = _chip_exchange(pa_wire, pb_wire)
    ha = _chip_sum(pa, xa, chip_idx)
    hb = _chip_sum(pb, xb, chip_idx)
    oa, ob = _pair_share(ha, hb)
    g_window = _both_halves(ha, oa, cc_)
    g_w_in_t = jnp.concatenate([lax.dynamic_slice_in_dim(g_window, a_off, PIECE_A, 0),
                                lax.dynamic_slice_in_dim(g_window, b_off, shard - PIECE_A, 0)], axis=0)
    g_w_out = _both_halves(hb, ob, cc_)
    as_rows = lambda a: jnp.transpose(a, (2, 0, 1))
    g_w_in, d_w_in, nm_w_in, nv_w_in = (jnp.transpose(a, (1, 2, 0)) for a in _adamw_rows(
        as_rows(w_in), g_w_in_t, as_rows(m_w_in), as_rows(v_w_in)))
    d_w_out, nm_w_out, nv_w_out = (a[None] for a in _adamw_big(w_out[0], g_w_out, m_w_out[0], v_w_out[0], LANE))
    params = (norm_g, final_g2, attn_norm_g, conv_norm_g, b_f, meta, conv_w[0])
    ms = (m_norm_g, m_final_norm_g.reshape(1, D_MODEL), m_attn_norm_g, m_conv_norm_g, m_b_f, m_meta, m_conv_w[0])
    vs = (v_norm_g, v_final_norm_g.reshape(1, D_MODEL), v_attn_norm_g, v_conv_norm_g, v_b_f, v_meta, v_conv_w[0])
    loss, g_s, d_s, m_s, v_s = _small_update(pack, packs, params, ms, vs)

    def ordered(small_list, big_in, big_out):
        s_norm, s_final, s_attn, s_convg, s_bf, s_meta, s_cw = small_list
        return (s_meta, s_norm, big_in, s_bf, s_cw[None], s_attn, s_convg, big_out, s_final.reshape(D_MODEL))

    return (loss.reshape(()), grad_x,
            *ordered(g_s, g_w_in, g_w_out[None]), *ordered(d_s, d_w_in, d_w_out),
            *ordered(m_s, nm_w_in, nm_w_out), *ordered(v_s, nv_w_in, nv_w_out))
```

```python
import functools

import jax
import jax.numpy as jnp
from jax import lax
from jax.experimental import pallas as pl
from jax.experimental.pallas import tpu as pltpu

F32 = jnp.float32
MXU_DTYPE = jnp.bfloat16
WIRE_DTYPE = jnp.bfloat16

D_MODEL = 1024
N_META = 16
HEADS = 8
HEAD_DIM = 64
D_ATTN = HEADS * HEAD_DIM
D_CONV = 512
EPS = 1e-6
LANE = 128
SUBLANE = 8
ROW_TILE = 384
ATTN_UNROLL = 3
ATTN_BWD_QBLOCKS = 2
DELTA_TERMS = 3
STAT_TERMS = 1
FRONT = LANE
PAD_ROWS = FRONT - N_META
NEG = -1e30
LOG2E = 1.4426950408889634
N_CHIPS = 4
N_DEV = 8
VMEM_LIMIT_BYTES = 60 * 1024 * 1024

SEG_Q, SEG_K, SEG_V, SEG_F, SEG_ZA, SEG_GB, SEG_GC, SEG_XC, SEG_ZC = (
    0, 512, 1024, 1536, 1664, 2176, 2688, 3200, 3712)
D_IN = 4104
D_IN_PAD = 4224
F_END = 1544
GW_COL_TILE = 1408
WIN_ROWS = 1152
WIN_HALF = WIN_ROWS // 2
WIN_START = (0, 1024, 2160, 3072)
PIECE_A = 518
A_OFF = (0, 2, 12, 126)
B_OFF = (518, 640, 530, 644)
ADAM_LR = 0.001
ADAM_B1 = 0.9
ADAM_B2 = 0.999
ADAM_EPS = 1e-08
ADAM_WD = 0.01
ADAM_STEP = 10

MESH = pl.DeviceIdType.MESH
ANY = pl.BlockSpec(memory_space=pl.ANY)

PACK_ROWS = 32
SLOT_NORM = (0, 1, 0, 1024)
SLOT_FINAL = (1, 2, 0, 1024)
SLOT_ATTN = (2, 3, 0, 512)
SLOT_CONVG = (2, 3, 512, 1024)
SLOT_BF = (3, 4, 0, 8)
SLOT_META = (8, 24, 0, 256)
SLOT_CONVW = (24, 27, 0, 128)
LOSS_ROW = 4


def _params(sem=None):
    return pltpu.CompilerParams(dimension_semantics=sem, vmem_limit_bytes=VMEM_LIMIT_BYTES)


def _sigmoid(z):
    return 1.0 / (1.0 + jnp.exp(-z))


def _dot(a, b):
    return jnp.dot(a, b, preferred_element_type=F32)


def _dot_nt(a, b):
    return lax.dot_general(a, b, (((1,), (1,)), ((), ())), preferred_element_type=F32)


def _dot_exact(ones, x):
    ones = ones.astype(MXU_DTYPE)
    total = None
    for _ in range(3):
        term = x.astype(MXU_DTYPE)
        x = x - term.astype(F32)
        total = _dot(ones, term) if total is None else total + _dot(ones, term)
    return total


def _group_matrix():
    r = lax.broadcasted_iota(jnp.int32, (D_ATTN, D_ATTN), 0) >> 6
    c = lax.broadcasted_iota(jnp.int32, (D_ATTN, D_ATTN), 1) >> 6
    return jnp.where(r == c, 1.0, 0.0).astype(MXU_DTYPE)


def _triangle(n, lower):
    r = lax.broadcasted_iota(jnp.int32, (n, n), 0)
    c = lax.broadcasted_iota(jnp.int32, (n, n), 1)
    return jnp.where((r >= c) if lower else (c >= r), 1.0, 0.0).astype(MXU_DTYPE)


def _group_sum(x, gmat, terms=2):
    hi = x.astype(MXU_DTYPE)
    if terms == 1:
        return _dot(hi, gmat)
    lo = (x - hi.astype(F32)).astype(MXU_DTYPE)
    return _dot(hi, gmat) + _dot(lo, gmat)


def _x_block_specs(n_sub, rows):
    specs = [pl.BlockSpec((rows, D_MODEL), lambda i: (jnp.maximum(n_sub * i - 1, 0), 0))]
    for b in range(1, n_sub):
        specs.append(pl.BlockSpec((rows, D_MODEL), functools.partial(lambda i, b: (n_sub * i - 1 + b, 0), b=b)))
    return specs


def _position():
    return lax.axis_index("x"), lax.axis_index("y"), lax.axis_index("c")


def _gather_weights(wi, wo, small):
    def body(wi_ref, wo_ref, sm_ref, gwi_ref, gwo_ref, gsm_ref, send_sems, recv_sems):
        x, y, c = _position()
        sibling = (x, y, 1 - c)
        chips = [(1 - x, y), (x, 1 - y), (1 - x, 1 - y)]

        def remote(k, src, dst, to):
            return pltpu.make_async_remote_copy(src_ref=src, dst_ref=dst, send_sem=send_sems.at[k],
                                                recv_sem=recv_sems.at[k], device_id=to, device_id_type=MESH)

        first, passed, landed = [], [], []
        for a, (src_ref, g_ref) in enumerate(((wi_ref, gwi_ref), (wo_ref, gwo_ref))):
            for j, (cx, cy) in enumerate(chips):
                slot = g_ref.at[j, c]
                first.append(remote(6 * a + j, src_ref.at[c], slot, (cx, cy, c)))
                landed.append(remote(6 * a + j, slot, slot, sibling))
                passed.append(remote(6 * a + 3 + j, slot, slot, sibling))
        for j, (cx, cy) in enumerate(chips):
            first.append(remote(12 + j, sm_ref, gsm_ref.at[j], (cx, cy, c)))
        for cp in first:
            cp.start()
        for arrived, onward in zip(landed, passed):
            arrived.wait_recv()
            onward.start()
        for a, g_ref in enumerate((gwi_ref, gwo_ref)):
            for j in range(3):
                remote(6 * a + 3 + j, g_ref.at[j, 1 - c], g_ref.at[j, 1 - c], sibling).wait_recv()
        for j in range(3):
            remote(12 + j, sm_ref, gsm_ref.at[j], sibling).wait_recv()
        for cp in first + passed:
            cp.wait_send()

    return pl.pallas_call(
        body, name="gather_weights",
        out_shape=(jax.ShapeDtypeStruct((3,) + wi.shape, wi.dtype), jax.ShapeDtypeStruct((3,) + wo.shape, wo.dtype),
                   jax.ShapeDtypeStruct((3,) + small.shape, small.dtype)),
        in_specs=[ANY, ANY, ANY], out_specs=(ANY, ANY, ANY),
        scratch_shapes=[pltpu.SemaphoreType.DMA((15,)), pltpu.SemaphoreType.DMA((15,))],
    )(wi, wo, small)


def _pair_exchange(gw, gb, pack):
    n_big = N_CHIPS + 1

    def body(gw_ref, gb_ref, p_ref, ra_ref, rb_ref, o_ref, send_sems, recv_sems):
        x, y, c = _position()
        sibling = (x, y, 1 - c)

        def remote(k, src, dst, to):
            return pltpu.make_async_remote_copy(src_ref=src, dst_ref=dst, send_sem=send_sems.at[k],
                                                recv_sem=recv_sems.at[k], device_id=to, device_id_type=MESH)

        copies = [remote(N_CHIPS, gb_ref.at[:, 1 - c], rb_ref, sibling)]
        for s, start in enumerate(WIN_START):
            rows = pl.ds(pl.multiple_of(start + WIN_HALF * (1 - c), 2 * SUBLANE), WIN_HALF)
            copies.append(remote(s, gw_ref.at[rows], ra_ref.at[s], sibling))
        for mask in range(1, N_DEV):
            peer = (1 - x if mask & 4 else x, 1 - y if mask & 2 else y, 1 - c if mask & 1 else c)
            copies.append(remote(n_big + mask - 1, p_ref, o_ref.at[mask - 1], peer))
        for cp in copies:
            cp.start()
        for cp in copies:
            cp.wait()

    n_sems = n_big + N_DEV - 1
    return pl.pallas_call(
        body, name="grad_pair_exchange",
        out_shape=(jax.ShapeDtypeStruct((N_CHIPS, WIN_HALF, D_MODEL), gw.dtype),
                   jax.ShapeDtypeStruct((N_CHIPS,) + gb.shape[2:], gb.dtype),
                   jax.ShapeDtypeStruct((N_DEV - 1,) + pack.shape, pack.dtype)),
        in_specs=[ANY, ANY, ANY], out_specs=(ANY, ANY, ANY),
        scratch_shapes=[pltpu.SemaphoreType.DMA((n_sems,)), pltpu.SemaphoreType.DMA((n_sems,))],
    )(gw, gb, pack)


def _chip_exchange(pa, pb):
    def body(pa_ref, pb_ref, ra_ref, rb_ref, send_sems, recv_sems):
        x, y, c = _position()
        chips = [(1 - x, y), (x, 1 - y), (1 - x, 1 - y)]
        copies = []
        for a, (src, dst) in enumerate(((pa_ref, ra_ref), (pb_ref, rb_ref))):
            for j, (cx, cy) in enumerate(chips):
                copies.append(pltpu.make_async_remote_copy(
                    src_ref=src.at[2 * cx + cy], dst_ref=dst.at[j], send_sem=send_sems.at[3 * a + j],
                    recv_sem=recv_sems.at[3 * a + j], device_id=(cx, cy, c), device_id_type=MESH))
        for cp in copies:
            cp.start()
        for cp in copies:
            cp.wait()

    return pl.pallas_call(
        body, name="grad_chip_exchange",
        out_shape=(jax.ShapeDtypeStruct((3,) + pa.shape[1:], pa.dtype),
                   jax.ShapeDtypeStruct((3,) + pb.shape[1:], pb.dtype)),
        in_specs=[ANY, ANY], out_specs=(ANY, ANY),
        scratch_shapes=[pltpu.SemaphoreType.DMA((6,)), pltpu.SemaphoreType.DMA((6,))],
    )(pa, pb)


def _pair_share(ha, hb):
    def body(ha_ref, hb_ref, oa_ref, ob_ref, send_sems, recv_sems):
        x, y, c = _position()
        copies = [pltpu.make_async_remote_copy(
            src_ref=src, dst_ref=dst, send_sem=send_sems.at[k], recv_sem=recv_sems.at[k],
            device_id=(x, y, 1 - c), device_id_type=MESH)
            for k, (src, dst) in enumerate(((ha_ref, oa_ref), (hb_ref, ob_ref)))]
        for cp in copies:
            cp.start()
        for cp in copies:
            cp.wait()

    return pl.pallas_call(
        body, name="grad_pair_share",
        out_shape=(jax.ShapeDtypeStruct(ha.shape, ha.dtype), jax.ShapeDtypeStruct(hb.shape, hb.dtype)),
        in_specs=[ANY, ANY], out_specs=(ANY, ANY),
        scratch_shapes=[pltpu.SemaphoreType.DMA((2,)), pltpu.SemaphoreType.DMA((2,))],
    )(ha, hb)


def _pair_sum(mine, recv, c_idx):
    rows, cols = mine.shape[2:]

    def body(c_ref, a_ref, b_ref, o_ref, send_ref):
        total = a_ref[...] + b_ref[...]
        o_ref[...] = total
        send_ref[...] = total.astype(send_ref.dtype)

    out_spec = pl.BlockSpec((None, rows, cols), lambda s, c_ref: (s, 0, 0))
    return pl.pallas_call(
        body, name="grad_pair_sum",
        grid_spec=pltpu.PrefetchScalarGridSpec(
            num_scalar_prefetch=1, grid=(N_CHIPS,),
            in_specs=[pl.BlockSpec((None, None, rows, cols), lambda s, c_ref: (s, c_ref[0], 0, 0)),
                      pl.BlockSpec((None, rows, cols), lambda s, c_ref: (s, 0, 0))],
            out_specs=(out_spec, out_spec)),
        out_shape=(jax.ShapeDtypeStruct(recv.shape, recv.dtype), jax.ShapeDtypeStruct(recv.shape, WIRE_DTYPE)),
        compiler_params=_params(("parallel",)),
    )(c_idx, mine, recv)


def _window_start(s):
    return jnp.where(s == 0, WIN_START[0], jnp.where(s == 1, WIN_START[1], jnp.where(s == 2, WIN_START[2], WIN_START[3])))


def _pair_sum_windows(gw, recv, c_idx):
    tr = WIN_HALF // 3

    def body(c_ref, a_ref, b_ref, o_ref, send_ref):
        total = a_ref[...] + b_ref[...].astype(F32)
        o_ref[...] = total
        send_ref[...] = total.astype(send_ref.dtype)

    out_spec = pl.BlockSpec((None, tr, D_MODEL), lambda s, i, c_ref: (s, i, 0))
    return pl.pallas_call(
        body, name="grad_pair_sum_windows",
        grid_spec=pltpu.PrefetchScalarGridSpec(
            num_scalar_prefetch=1, grid=(N_CHIPS, WIN_HALF // tr),
            in_specs=[pl.BlockSpec((pl.Element(tr), pl.Element(D_MODEL)),
                                   lambda s, i, c_ref: (pl.multiple_of(
                                       _window_start(s) + WIN_HALF * c_ref[0] + tr * i, SUBLANE), 0)),
                      pl.BlockSpec((None, tr, D_MODEL), lambda s, i, c_ref: (s, i, 0))],
            out_specs=(out_spec, out_spec)),
        out_shape=(jax.ShapeDtypeStruct(recv.shape, F32), jax.ShapeDtypeStruct(recv.shape, WIRE_DTYPE)),
        compiler_params=_params(("parallel", "parallel")),
    )(c_idx, gw, recv)


def _assemble_w(own, others, starts):
    def body(starts_ref, own_ref, oth_ref, o_ref):
        o_ref[...] = jnp.zeros_like(o_ref)
        for k in range(N_CHIPS):
            rows = pl.ds(pl.multiple_of(starts_ref[k], 2 * SUBLANE), WIN_ROWS)
            o_ref[rows, :] = o_ref[rows, :] + (own_ref[...] if k == 0 else oth_ref[k - 1])

    return pl.pallas_call(
        body, name="assemble_w",
        in_specs=[pl.BlockSpec(memory_space=pltpu.SMEM), pl.BlockSpec(memory_space=pltpu.VMEM),
                  pl.BlockSpec(memory_space=pltpu.VMEM)],
        out_specs=pl.BlockSpec(memory_space=pltpu.VMEM),
        out_shape=jax.ShapeDtypeStruct((D_IN_PAD, D_MODEL), own.dtype),
        compiler_params=_params(),
    )(starts, own, others)


def _chip_sum(psum, recv3, chip_idx):
    rows, cols = psum.shape[1:]
    tr = rows // 2

    def body(s_ref, p_ref, r0, r1, r2, o_ref):
        o_ref[...] = ((p_ref[...] + r0[...].astype(F32)) + r1[...].astype(F32)) + r2[...].astype(F32)

    return pl.pallas_call(
        body, name="grad_chip_sum",
        grid_spec=pltpu.PrefetchScalarGridSpec(
            num_scalar_prefetch=1, grid=(2,),
            in_specs=[pl.BlockSpec((None, tr, cols), lambda i, s_ref: (s_ref[0], i, 0))] +
                     [pl.BlockSpec((None, tr, cols), functools.partial(lambda i, s_ref, j: (j, i, 0), j=j))
                      for j in range(3)],
            out_specs=pl.BlockSpec((tr, cols), lambda i, s_ref: (i, 0))),
        out_shape=jax.ShapeDtypeStruct((rows, cols), psum.dtype),
        compiler_params=_params(("parallel",)),
    )(chip_idx, psum, recv3, recv3, recv3)


def _adamw_math(w, g, m, v):
    m = ADAM_B1 * m + (1.0 - ADAM_B1) * g
    v = ADAM_B2 * v + (1.0 - ADAM_B2) * (g * g)
    m_hat = m * (1.0 / (1.0 - ADAM_B1 ** ADAM_STEP))
    v_hat = v * (1.0 / (1.0 - ADAM_B2 ** ADAM_STEP))
    delta = -ADAM_LR * (m_hat / (jnp.sqrt(v_hat) + ADAM_EPS) + ADAM_WD * w)
    return delta, m, v


def _adamw_big(w, g, m, v, tr):
    rows, cols = w.shape
    assert rows % tr == 0 and g.shape[0] >= rows

    def body(w_ref, g_ref, m_ref, v_ref, d_out, m_out, v_out):
        d, m2, v2 = _adamw_math(w_ref[...], g_ref[...], m_ref[...], v_ref[...])
        d_out[...] = d
        m_out[...] = m2
        v_out[...] = v2

    spec = pl.BlockSpec((tr, cols), lambda i: (i, 0))
    sds = jax.ShapeDtypeStruct((rows, cols), F32)
    return pl.pallas_call(
        body, name="adamw_big", grid=(rows // tr,), in_specs=[spec] * 4, out_specs=(spec,) * 3,
        out_shape=(sds,) * 3, compiler_params=_params(("parallel",)),
    )(w, g, m, v)


def _adamw_rows(w3, g, m3, v3):
    rows, _, cols = w3.shape
    tc = 2 * LANE

    def body(w_ref, g_ref, m_ref, v_ref, g_out, d_out, m_out, v_out):
        g = g_ref[...]
        d, m2, v2 = _adamw_math(w_ref[:, 0, :], g, m_ref[:, 0, :], v_ref[:, 0, :])
        g_out[:, 0, :] = g
        d_out[:, 0, :] = d
        m_out[:, 0, :] = m2
        v_out[:, 0, :] = v2

    spec3 = pl.BlockSpec((rows, 1, tc), lambda i: (0, 0, i))
    sds = jax.ShapeDtypeStruct((rows, 1, cols), F32)
    return pl.pallas_call(
        body, name="adamw_rows", grid=(cols // tc,),
        in_specs=[spec3, pl.BlockSpec((rows, tc), lambda i: (0, i)), spec3, spec3], out_specs=(spec3,) * 4,
        out_shape=(sds,) * 4, compiler_params=_params(("parallel",)),
    )(w3, g, m3, v3)


def _small_update(own, others, params, ms, vs):
    slots = (SLOT_NORM, SLOT_FINAL, SLOT_ATTN, SLOT_CONVG, SLOT_BF, SLOT_META, SLOT_CONVW)
    n = len(slots)

    def body(*refs):
        own_ref, gp_ref = refs[:2]
        w_refs, m_refs, v_refs = refs[2:2 + n], refs[2 + n:2 + 2 * n], refs[2 + 2 * n:2 + 3 * n]
        outs = refs[2 + 3 * n:3 + 7 * n]
        loss_ref = outs[0]
        g_outs, d_outs, m_outs, v_outs = (outs[1 + k * n:1 + (k + 1) * n] for k in range(4))
        g_scr, w_scr, m_scr, v_scr = refs[3 + 7 * n:]
        x, y, c = _position()
        shard = 2 * x + y
        me = 4 * x + 2 * y + c
        tot = None
        for d in range(N_DEV):
            rel = jnp.bitwise_xor(me, d)
            term = jnp.where(rel == 0, own_ref[...], gp_ref[jnp.maximum(rel, 1) - 1])
            tot = term if tot is None else tot + term
        r0, r1, _, _ = SLOT_META
        meta_sel = tot[r0:r1, 0:256]
        cw_sel = tot[24:32, 0:128]
        for k in range(1, N_CHIPS):
            meta_sel = jnp.where(shard == k, tot[r0:r1, 256 * k:256 * (k + 1)], meta_sel)
            cw_sel = jnp.where(shard == k, tot[24:32, 128 * k:128 * (k + 1)], cw_sel)
        zeros = jnp.zeros((PACK_ROWS, D_MODEL), F32)
        for scr in (g_scr, w_scr, m_scr, v_scr):
            scr[...] = zeros
        g_scr[0:8, :] = tot[0:8, :]
        g_scr[r0:r1, 0:256] = meta_sel
        g_scr[24:32, 0:128] = cw_sel
        for (a, b, c0, c1), w_ref, m_ref, v_ref in zip(slots, w_refs, m_refs, v_refs):
            w_scr[a:b, c0:c1] = w_ref[...]
            m_scr[a:b, c0:c1] = m_ref[...]
            v_scr[a:b, c0:c1] = v_ref[...]
        loss_ref[...] = g_scr[LOSS_ROW:LOSS_ROW + 1, 0:1]
        d, m2, v2 = _adamw_math(w_scr[...], g_scr[...], m_scr[...], v_scr[...])
        w_scr[...] = d
        m_scr[...] = m2
        v_scr[...] = v2
        for (a, b, c0, c1), g_o, d_o, m_o, v_o in zip(slots, g_outs, d_outs, m_outs, v_outs):
            g_o[...] = g_scr[a:b, c0:c1]
            d_o[...] = w_scr[a:b, c0:c1]
            m_o[...] = m_scr[a:b, c0:c1]
            v_o[...] = v_scr[a:b, c0:c1]

    shapes = [jax.ShapeDtypeStruct(p.shape, F32) for p in params]
    out = pl.pallas_call(
        body, name="small_update",
        out_shape=[jax.ShapeDtypeStruct((1, 1), F32)] + shapes * 4,
        scratch_shapes=[pltpu.VMEM((PACK_ROWS, D_MODEL), F32)] * 4,
        compiler_params=_params(),
    )(own, others, *params, *ms, *vs)
    return out[0], out[1:1 + n], out[1 + n:1 + 2 * n], out[1 + 2 * n:1 + 3 * n], out[1 + 3 * n:1 + 4 * n]


def _in_proj(x2, meta_blk, norm_g, w_pad, bf_pad):
    seq = x2.shape[0]
    lp = seq + FRONT
    t = ROW_TILE
    nt = lp // t
    n_sub = t // LANE

    def body(*refs):
        x_refs = refs[:n_sub]
        mb, g_ref, w_ref, bf_ref, tri_ref = refs[n_sub:n_sub + 5]
        q_ref, k_ref, v_ref, rest_ref, fl_ref, ct_ref, u_ref, qt_ref, kt_ref, vt_ref, cc_ref, carry = refs[n_sub + 5:]
        i = pl.program_id(0)

        @pl.when(i == 0)
        def _():
            carry[...] = jnp.zeros_like(carry)

        first = jnp.where(i == 0, mb[...], x_refs[0][...])
        h = jnp.concatenate([first] + [r[...] for r in x_refs[1:]], axis=0)
        ms = jnp.mean(h * h, axis=-1, keepdims=True)
        u = ((h * lax.rsqrt(ms + EPS)) * g_ref[...]).astype(MXU_DTYPE)
        u_ref[...] = u

        def seg(a, width):
            return _dot_nt(u, w_ref[a:a + width, :])

        q_tile = seg(SEG_Q, D_ATTN) * (HEAD_DIM ** -0.5)
        q_ref[...] = q_tile.astype(MXU_DTYPE)
        qt_ref[...] = q_tile.T.astype(MXU_DTYPE)
        k_tile = seg(SEG_K, D_ATTN)
        k_ref[...] = k_tile.astype(MXU_DTYPE)
        kt_ref[...] = k_tile.T.astype(MXU_DTYPE)
        v_tile = seg(SEG_V, D_ATTN)
        v_ref[...] = v_tile.astype(MXU_DTYPE)
        vt_ref[...] = v_tile.T.astype(MXU_DTYPE)
        for s in range(5):
            rest_ref[:, 512 * s:512 * (s + 1)] = seg(SEG_ZA + 512 * s, 512)
        fl = seg(SEG_F, LANE)
        fl_ref[...] = fl
        z = fl + bf_ref[...]
        logf = jnp.minimum(z, 0.0) - jnp.log(1.0 + jnp.exp(-jnp.abs(z)))
        row = i * t + lax.broadcasted_iota(jnp.int32, (t, LANE), 0)
        logf = jnp.where(row >= PAD_ROWS, logf, 0.0)
        cs = _dot_exact(tri_ref[...], logf) + carry[...]
        carry[...] = carry[...] + jnp.sum(logf, axis=0, keepdims=True)
        col = i * t + lax.broadcasted_iota(jnp.int32, (SUBLANE, t), 1)
        ct_ref[...] = jnp.where(col >= PAD_ROWS, cs.T[0:SUBLANE, :], -NEG)
        cc_ref[...] = jnp.where(row >= PAD_ROWS, cs, -NEG)

    row_blk = lambda cols: pl.BlockSpec((t, cols), lambda i: (i, 0))
    tr_blk = pl.BlockSpec((None, D_ATTN, t), lambda i: (i, 0, 0))
    const = lambda shape: pl.BlockSpec(shape, lambda i: (0, 0))
    return pl.pallas_call(
        body, name="in_proj", grid=(nt,),
        in_specs=_x_block_specs(n_sub, LANE) + [const((LANE, D_MODEL)), const((1, D_MODEL)),
                                                pl.BlockSpec((D_IN_PAD, D_MODEL), lambda i: (0, 0),
                                                             pipeline_mode=pl.Buffered(1)),
                                                const((1, LANE)), const((t, t))],
        out_specs=(row_blk(D_ATTN), row_blk(D_ATTN), row_blk(D_ATTN), row_blk(5 * 512), row_blk(LANE),
                   pl.BlockSpec((SUBLANE, t), lambda i: (0, i)), row_blk(D_MODEL), tr_blk, tr_blk, tr_blk, row_blk(LANE)),
        out_shape=(jax.ShapeDtypeStruct((lp, D_ATTN), MXU_DTYPE), jax.ShapeDtypeStruct((lp, D_ATTN), MXU_DTYPE),
                   jax.ShapeDtypeStruct((lp, D_ATTN), MXU_DTYPE), jax.ShapeDtypeStruct((lp, 5 * 512), F32),
                   jax.ShapeDtypeStruct((lp, LANE), F32),
                   jax.ShapeDtypeStruct((SUBLANE, lp), F32), jax.ShapeDtypeStruct((lp, D_MODEL), MXU_DTYPE),
                   jax.ShapeDtypeStruct((nt, D_ATTN, t), MXU_DTYPE), jax.ShapeDtypeStruct((nt, D_ATTN, t), MXU_DTYPE),
                   jax.ShapeDtypeStruct((nt, D_ATTN, t), MXU_DTYPE), jax.ShapeDtypeStruct((lp, LANE), F32)),
        scratch_shapes=[pltpu.VMEM((1, LANE), F32)],
        compiler_params=_params(("arbitrary",)),
    )(*([x2] * n_sub), meta_blk, norm_g, w_pad, bf_pad, _triangle(t, lower=True))


def _head_masks():
    lane = lax.broadcasted_iota(jnp.int32, (1, LANE), 1)
    return lane < HEAD_DIM, lane >= HEAD_DIM


def _pair_specs(lp, nt, t):
    blk = pl.BlockSpec((lp, LANE), lambda g: (0, g))
    ct_a = pl.BlockSpec((None, nt, 1, t), lambda g: (2 * g, 0, 0, 0))
    ct_b = pl.BlockSpec((None, nt, 1, t), lambda g: (2 * g + 1, 0, 0, 0))
    return blk, ct_a, ct_b


def _sub_rows(s, col):
    return jnp.concatenate([s[:, a * LANE:(a + 1) * LANE] - col for a in range(s.shape[1] // LANE)], axis=1)


def _loop_unrolled(lo, hi, step, init, n):
    def group(jj, carry):
        for k in range(n):
            carry = step(lo + n * jj + k, carry)
        return carry

    groups = (hi - lo) // n
    carry = lax.fori_loop(0, groups, group, init)
    return lax.fori_loop(lo + n * groups, hi, step, carry)


def _attn_fwd(q, k, v_t, cc):
    lp = q.shape[0]
    t = ROW_TILE
    nt = lp // t
    ext = LANE + 2 * SUBLANE

    def body(q_ref, k_ref, vt_ref, cc_ref, o_ref, l_ref, m_ref, s_scr, last_scr, m_scr, mfin_scr, acc_scr, c_scr):
        masks = _head_masks()
        lane = lax.broadcasted_iota(jnp.int32, (1, LANE), 1)
        for hh in range(2):
            picked = jnp.where(lane == 2 * pl.program_id(0) + hh, cc_ref[...], 0.0)
            c_scr[hh] = jnp.broadcast_to(jnp.sum(picked, axis=-1, keepdims=True), (lp, LANE))
        visible = lax.broadcasted_iota(jnp.int32, (t, t), 0) <= lax.broadcasted_iota(jnp.int32, (t, t), 1)
        top = lax.broadcasted_iota(jnp.int32, (LANE, 1), 0) < HEAD_DIM
        second_head = (lax.broadcasted_iota(jnp.int32, (2 * SUBLANE, 2 * t), 1) >= t).astype(jnp.int32)
        ones_rows = jnp.where(lax.broadcasted_iota(jnp.int32, (2 * SUBLANE, 2 * t), 0) == second_head,
                              1.0, 0.0).astype(MXU_DTYPE)

        on_first_diagonal = jnp.concatenate([visible, jnp.ones((t, t), jnp.bool_)], axis=1)

        def scores(j, queries):
            kj = k_ref[pl.ds(pl.multiple_of(j * t, t), t), :]
            return _dot_nt(jnp.concatenate([jnp.where(hm, kj, 0).astype(MXU_DTYPE) for hm in masks], axis=0), queries)

        def biased(s2, j, hh):
            return _sub_rows(s2[hh * t:(hh + 1) * t, :], c_scr[hh, pl.ds(pl.multiple_of(j * t, t), t), :]) * LOG2E

        def track_max(hh, s, lo, hi):
            m = m_scr[hh, :, lo:hi]
            for a in range(t // SUBLANE):
                m = jnp.maximum(m, s[a * SUBLANE:(a + 1) * SUBLANE, :])
            m_scr[hh, :, lo:hi] = m

        def probabilities(scores_of, ms_cols):
            return jnp.concatenate([jnp.exp2(scores_of(hh) - ms_cols[hh]).astype(MXU_DTYPE) for hh in range(2)], axis=0)

        def values(j):
            vtj = vt_ref[j]
            v2 = jnp.concatenate([jnp.where(top, vtj, 0).astype(MXU_DTYPE),
                                  jnp.where(top, 0, vtj).astype(MXU_DTYPE)], axis=1)
            return jnp.concatenate([v2, ones_rows], axis=0)

        def stage(done, ahead):
            if ahead is not None:
                i_a, rows_a = ahead
                qa = q_ref[pl.ds(pl.multiple_of(i_a * t, t), rows_a), :]
                m_scr[...] = jnp.full(m_scr.shape, NEG, F32)

                def max_step(j, mask=None):
                    s2 = scores(j, qa)
                    for hh in range(2):
                        s = biased(s2, j, hh)
                        if mask is not None:
                            s = jnp.where(mask, s, NEG)
                        s_scr[j, hh * t:(hh + 1) * t, 0:rows_a] = s
                        track_max(hh, s, 0, rows_a)

            if done is not None:
                i_d, rows_d = done
                r0 = pl.multiple_of(i_d * t, t)
                ms = [mfin_scr[hh, 0:1, 0:rows_d] for hh in range(2)]
                acc_scr[...] = jnp.zeros(acc_scr.shape, F32)

                def key_step(j, carry):
                    p = probabilities(lambda hh: s_scr[j, hh * t:(hh + 1) * t, 0:rows_d], ms)
                    acc_scr[:, 0:rows_d] = acc_scr[:, 0:rows_d] + _dot(values(j), p)
                    if ahead is not None:
                        max_step(j)
                    return carry

                _loop_unrolled(0, i_d + 1, key_step, 0, ATTN_UNROLL)
                if rows_d == 2 * t:
                    p = probabilities(lambda hh: last_scr[hh * t:(hh + 1) * t, :], [m[:, t:] for m in ms])
                    acc_scr[:, t:rows_d] = acc_scr[:, t:rows_d] + _dot(values(i_d + 1), p)
                acc = acc_scr[:, 0:rows_d]
                l_pair = jnp.where(top, acc[LANE:LANE + 1], acc[LANE + 1:LANE + 2])
                o_ref[pl.ds(r0, rows_d), :] = (acc[:LANE] / l_pair).T
                l_ref[pl.ds(r0, rows_d), :] = l_pair.T
                for hh in range(2):
                    m_ref[pl.ds(r0, rows_d), hh * LANE:(hh + 1) * LANE] = jnp.broadcast_to(ms[hh], (LANE, rows_d)).T

            if ahead is not None:
                if done is not None:
                    max_step(i_a - 1)
                max_step(i_a, on_first_diagonal if rows_a == 2 * t else visible)
                if rows_a == 2 * t:
                    s2 = scores(i_a + 1, qa[t:])
                    for hh in range(2):
                        s = jnp.where(visible, biased(s2, i_a + 1, hh), NEG)
                        last_scr[hh * t:(hh + 1) * t, :] = s
                        track_max(hh, s, t, rows_a)
                for hh in range(2):
                    mfin_scr[hh, :, 0:rows_a] = jnp.broadcast_to(jnp.max(m_scr[hh, :, 0:rows_a], axis=0, keepdims=True),
                                                                 (SUBLANE, rows_a))

        pairs = nt // 2
        stage(None, (0, 2 * t))

        def pair_to_pair(u, _):
            stage((2 * u, 2 * t), (2 * u + 2, 2 * t))
            return 0

        lax.fori_loop(0, pairs - 1, pair_to_pair, 0)
        if nt % 2:
            stage((2 * pairs - 2, 2 * t), (nt - 1, t))
            stage((nt - 1, t), None)
        else:
            stage((2 * pairs - 2, 2 * t), None)

    blk = pl.BlockSpec((lp, LANE), lambda g: (0, g))
    return pl.pallas_call(
        body, name="attn_fwd", grid=(HEADS // 2,),
        in_specs=[blk, blk, pl.BlockSpec((nt, LANE, t), lambda g: (0, g, 0)),
                  pl.BlockSpec((lp, LANE), lambda g: (0, 0), pipeline_mode=pl.Buffered(1))],
        out_specs=(blk, blk, pl.BlockSpec((lp, 2 * LANE), lambda g: (0, g))),
        out_shape=(jax.ShapeDtypeStruct((lp, D_ATTN), F32), jax.ShapeDtypeStruct((lp, D_ATTN), F32),
                   jax.ShapeDtypeStruct((lp, HEADS * LANE), F32)),
        scratch_shapes=[pltpu.VMEM((nt, 2 * t, 2 * t), F32), pltpu.VMEM((2 * t, t), F32),
                        pltpu.VMEM((2, SUBLANE, 2 * t), F32), pltpu.VMEM((2, SUBLANE, 2 * t), F32),
                        pltpu.VMEM((ext, 2 * t), F32), pltpu.VMEM((2, lp, LANE), F32)],
        compiler_params=_params(("parallel",)),
    )(q, k, v_t, cc)


def _attn_bwd(q, k, v, do, q_t, k_t, do_t, m, neg_delta, ct4):
    lp = q.shape[0]
    t = ROW_TILE
    nt = lp // t

    def body(q_ref, k_ref, v_ref, do_ref, qt_ref, kt_ref, dot_ref, ma_ref, mb_ref, nd_ref, cta_ref, ctb_ref,
             dq_ref, dk_ref, dv_ref, dc_ref, dq_acc, dk_acc, dv_acc):
        masks = _head_masks()
        ct_refs, m_refs = (cta_ref, ctb_ref), (ma_ref, mb_ref)
        row_head = 2 * pl.program_id(0) + (lax.broadcasted_iota(jnp.int32, (2 * t, LANE), 0) >= t).astype(jnp.int32)
        col = lax.broadcasted_iota(jnp.int32, (2 * t, LANE), 1)
        delta_ones = jnp.where((col < HEADS * DELTA_TERMS) & (col % HEADS == row_head), 1.0, 0.0).astype(MXU_DTYPE)
        below = lax.broadcasted_iota(jnp.int32, (t, t), 1) <= lax.broadcasted_iota(jnp.int32, (t, t), 0)
        top = lax.broadcasted_iota(jnp.int32, (LANE, 1), 0) < HEAD_DIM
        dq_acc[...] = jnp.zeros_like(dq_acc)

        on_first_diagonal = jnp.concatenate([below, jnp.ones((t, t), jnp.bool_)], axis=0)

        def k_block(j, _, with_next=True):
            c0 = pl.multiple_of(j * t, t)
            kj = k_ref[pl.ds(c0, t), :]
            vj = v_ref[pl.ds(c0, t), :]
            k2 = jnp.concatenate([jnp.where(hm, kj, 0).astype(MXU_DTYPE) for hm in masks], axis=0)
            v2 = jnp.concatenate([jnp.where(hm, vj, 0).astype(MXU_DTYPE) for hm in masks], axis=0)
            v2 = jnp.concatenate([v2, delta_ones], axis=1)
            ck = [r[j] for r in ct_refs]
            ktj = kt_ref[j]
            k2t = jnp.concatenate([jnp.where(top, ktj, 0).astype(MXU_DTYPE), jnp.where(top, 0, ktj).astype(MXU_DTYPE)],
                                  axis=1)
            dk_acc[...] = jnp.zeros_like(dk_acc)
            dv_acc[...] = jnp.zeros_like(dv_acc)

            def q_block(i, colsums, mask=None, rows=t):
                r0 = pl.multiple_of(i * t, t)
                qi = q_ref[pl.ds(r0, rows), :]
                doi = jnp.concatenate([do_ref[pl.ds(r0, rows), :], nd_ref[pl.ds(r0, rows), :]], axis=1)
                qti = jnp.concatenate([qt_ref[i + b] for b in range(rows // t)], axis=1)
                doti = jnp.concatenate([dot_ref[i + b] for b in range(rows // t)], axis=1)
                s2 = _dot_nt(qi, k2)
                dp2 = _dot_nt(doi, v2)
                out, ps, dss = [], [], []
                for hh in range(2):
                    s = (s2[:, hh * t:(hh + 1) * t] - ck[hh]) * LOG2E
                    if mask is not None:
                        s = jnp.where(mask, s, NEG)
                    p = jnp.exp2(_sub_rows(s, m_refs[hh][pl.ds(r0, rows), :])).astype(MXU_DTYPE)
                    ds32 = p.astype(F32) * dp2[:, hh * t:(hh + 1) * t]
                    ps.append(p)
                    dss.append(ds32.astype(MXU_DTYPE))
                    out.append(colsums[hh] + jnp.sum(ds32, axis=0, keepdims=True))
                ds_cat = jnp.concatenate(dss, axis=1)
                dv_acc[...] = dv_acc[...] + _dot(doti, jnp.concatenate(ps, axis=1))
                dk_acc[...] = dk_acc[...] + _dot(qti, ds_cat)
                dq_t = _dot(k2t, ds_cat.T)
                for b in range(rows // t):
                    dq_acc[i + b] = dq_acc[i + b] + dq_t[:, b * t:(b + 1) * t]
                return tuple(out)

            nq = ATTN_BWD_QBLOCKS
            colsums = (jnp.zeros((1, t), F32), jnp.zeros((1, t), F32))
            if with_next:
                colsums = q_block(j, colsums, on_first_diagonal, nq * t)
            else:
                colsums = q_block(j, colsums, below)
            first = j + (nq if with_next else 1)
            groups = (nt - first) // nq
            colsums = lax.fori_loop(0, groups, lambda p, c: q_block(first + nq * p, c, None, nq * t), colsums)
            colsums = lax.fori_loop(first + nq * groups, nt, q_block, colsums)
            for hh in range(2):
                dc_ref[hh, j] = -colsums[hh]
            own = lambda acc: jnp.concatenate([acc[:HEAD_DIM, :t], acc[HEAD_DIM:, t:]], axis=0).T
            dk_ref[pl.ds(c0, t), :] = own(dk_acc[...]).astype(dk_ref.dtype)
            dv_ref[pl.ds(c0, t), :] = own(dv_acc[...]).astype(dv_ref.dtype)
            return 0

        lax.fori_loop(0, nt - 1, k_block, 0)
        k_block(nt - 1, 0, with_next=False)
        for i in range(nt):
            dq_ref[i * t:(i + 1) * t, :] = (dq_acc[i].T * (HEAD_DIM ** -0.5)).astype(dq_ref.dtype)

    blk, ct_a, ct_b = _pair_specs(lp, nt, t)
    rep_a = pl.BlockSpec((lp, LANE), lambda g: (0, 2 * g))
    rep_b = pl.BlockSpec((lp, LANE), lambda g: (0, 2 * g + 1))
    tr_blk = pl.BlockSpec((nt, LANE, t), lambda g: (0, g, 0))
    return pl.pallas_call(
        body, name="attn_bwd", grid=(HEADS // 2,),
        in_specs=[blk] * 4 + [tr_blk, tr_blk, tr_blk, rep_a, rep_b, pl.BlockSpec((lp, LANE), lambda g: (0, 0)), ct_a, ct_b],
        out_specs=(blk, blk, blk, pl.BlockSpec((2, nt, 1, t), lambda g: (g, 0, 0, 0))),
        out_shape=(jax.ShapeDtypeStruct((lp, D_ATTN), MXU_DTYPE),) * 3
                  + (jax.ShapeDtypeStruct((HEADS, nt, 1, t), F32),),
        scratch_shapes=[pltpu.VMEM((nt, LANE, t), F32), pltpu.VMEM((LANE, 2 * t), F32), pltpu.VMEM((LANE, 2 * t), F32)],
        compiler_params=_params(("parallel",)),
    )(q, k, v, do, q_t, k_t, do_t, m, m, neg_delta, ct4, ct4)


def _shift_down(prev8, cur, k):
    ext = jnp.concatenate([prev8, cur], axis=0)
    return pltpu.roll(ext, k, 0)[SUBLANE:, :]


def _shift_up(cur, next8, k):
    ext = jnp.concatenate([cur, next8], axis=0)
    n = ext.shape[0]
    return pltpu.roll(ext, n - k, 0)[:cur.shape[0], :]


def _post(o, l_sum, rest, x2, meta_blk, tgt2, w_out, attn_g, conv_g, final_g, conv_w8):
    lp = o.shape[0]
    t = ROW_TILE
    nt = lp // t
    n_sub = t // LANE
    hb = t // SUBLANE

    def body(*refs):
        o_ref, l_ref, za_ref, gb_ref, gc_ref, xc_ref, zc_ref, gch_ref, xch_ref = refs[:9]
        x_refs = refs[9:9 + n_sub]
        mb = refs[9 + n_sub]
        t_refs = refs[10 + n_sub:10 + 2 * n_sub]
        wo_ref, ag_ref, cg_ref, fg_ref, cw_ref, gm_ref, hr_ref = refs[10 + 2 * n_sub:17 + 2 * n_sub]
        (dout_ref, do_ref, dot_ref, dl_ref, dza_ref, dgb_ref, dzc_ref, dcv_ref,
         loss_ref, gf_ref, gag_ref, gcg_ref, gwo_ref) = refs[17 + 2 * n_sub:]
        i = pl.program_id(0)

        @pl.when(i == 0)
        def _():
            for r in (loss_ref, gf_ref, gag_ref, gcg_ref, gwo_ref):
                r[...] = jnp.zeros_like(r)

        gmat = gm_ref[...]
        inv_g = 1.0 / HEAD_DIM
        o_v = o_ref[...]
        ra = lax.rsqrt(_group_sum(o_v * o_v, gmat, STAT_TERMS) * inv_g + EPS)
        n_a = o_v * ra
        a_n = n_a * ag_ref[...]
        za = za_ref[...]
        sig_a = _sigmoid(za)
        sz_a = za * sig_a
        y_a = a_n * sz_a
        gb = gb_ref[...]
        gc = gc_ref[...]
        xc = xc_ref[...]
        cx = gc * xc
        cx_prev = jnp.where(i == 0, 0.0, gch_ref[...] * xch_ref[...])
        conv = (cw_ref[0:1, :] * _shift_down(cx_prev, cx, 2) + cw_ref[1:2, :] * _shift_down(cx_prev, cx, 1)
                + cw_ref[2:3, :] * cx)
        e = gb * conv
        re = lax.rsqrt(_group_sum(e * e, gmat, STAT_TERMS) * inv_g + EPS)
        n_e = e * re
        e_n = n_e * cg_ref[...]
        zc = zc_ref[...]
        sig_c = _sigmoid(zc)
        sz_c = zc * sig_c
        y_c = e_n * sz_c
        mix = jnp.concatenate([y_a, y_c], axis=-1)
        mix_b = mix.astype(MXU_DTYPE)
        first = jnp.where(i == 0, mb[...], x_refs[0][...])
        h = jnp.concatenate([first] + [r[...] for r in x_refs[1:]], axis=0)
        out = h + _dot(mix_b, wo_ref[...])
        r2 = lax.rsqrt(jnp.mean(out * out, axis=-1, keepdims=True) + EPS)
        n_f = out * r2
        y = n_f * fg_ref[...]
        tgt = jnp.concatenate([r[...] for r in t_refs], axis=0)
        valid = (i * t + lax.broadcasted_iota(jnp.int32, (t, 1), 0)) >= FRONT
        diff = jnp.where(valid, y - tgt, 0.0)
        loss_ref[...] = loss_ref[...] + 0.5 * jnp.sum(jnp.sum(diff * diff, axis=-1, keepdims=True) * (1.0 / D_MODEL))
        dy = diff * (1.0 / D_MODEL)
        gf_ref[...] = gf_ref[...] + jnp.sum(dy * n_f, axis=0, keepdims=True)
        dn = dy * fg_ref[...]
        d_out = r2 * (dn - n_f * jnp.mean(dn * n_f, axis=-1, keepdims=True))
        dout_ref[...] = d_out
        d_out_b = d_out.astype(MXU_DTYPE)
        d_mix = _dot_nt(d_out_b, wo_ref[...])
        gwo_ref[...] = gwo_ref[...] + _dot(mix.T.astype(MXU_DTYPE), d_out_b)
        d_ya = d_mix[:, :D_ATTN]
        d_yc = d_mix[:, D_ATTN:]
        d_an = d_ya * sz_a
        dza_ref[...] = (d_ya * a_n * (sig_a * (1.0 + za * (1.0 - sig_a)))).astype(dza_ref.dtype)
        gag_ref[...] = gag_ref[...] + jnp.sum(d_an * n_a, axis=0, keepdims=True)
        dn_a = d_an * ag_ref[...]
        d_o = ra * (dn_a - n_a * (_group_sum(dn_a * n_a, gmat, STAT_TERMS) * inv_g))
        d_o_l = d_o / l_ref[...]
        d_o_b = d_o_l.astype(do_ref.dtype)
        do_ref[...] = d_o_b
        dot_ref[...] = d_o_l.T.astype(dot_ref.dtype)
        delta = _group_sum(d_o_b.astype(F32) * o_v, hr_ref[...])
        terms, rest_of = [], delta
        for k in range(DELTA_TERMS):
            terms.append(rest_of.astype(MXU_DTYPE).astype(F32))
            rest_of = rest_of - terms[-1]
        dl_ref[...] = -sum(pltpu.roll(term, HEADS * k, 1) if k else term
                           for k, term in enumerate(terms)).astype(dl_ref.dtype)
        d_en = d_yc * sz_c
        dzc_ref[...] = (d_yc * e_n * (sig_c * (1.0 + zc * (1.0 - sig_c)))).astype(dzc_ref.dtype)
        gcg_ref[...] = gcg_ref[...] + jnp.sum(d_en * n_e, axis=0, keepdims=True)
        dn_e = d_en * cg_ref[...]
        d_e = re * (dn_e - n_e * (_group_sum(dn_e * n_e, gmat, STAT_TERMS) * inv_g))
        dgb_ref[...] = (d_e * conv).astype(dgb_ref.dtype)
        dcv_ref[...] = d_e * gb

    head_rep = jnp.where((lax.broadcasted_iota(jnp.int32, (D_ATTN, LANE), 0) >> 6)
                         == lax.broadcasted_iota(jnp.int32, (D_ATTN, LANE), 1), 1.0, 0.0).astype(MXU_DTYPE)
    row_blk = lambda cols: pl.BlockSpec((t, cols), lambda i: (i, 0))
    rest_blk = lambda s: pl.BlockSpec((t, 512), functools.partial(lambda i, s: (i, s), s=s))
    halo = lambda s: pl.BlockSpec((SUBLANE, 512), functools.partial(lambda i, s: (jnp.maximum(i * hb - 1, 0), s), s=s))
    const = lambda shape: pl.BlockSpec(shape, lambda i: (0, 0))
    acc = lambda shape: pl.BlockSpec(shape, lambda i: (0, 0))
    return pl.pallas_call(
        body, name="post_fwd_bwd", grid=(nt,),
        in_specs=[row_blk(D_ATTN), row_blk(D_ATTN)] + [rest_blk(s) for s in range(5)] + [halo(2), halo(3)]
                 + _x_block_specs(n_sub, LANE) + [const((LANE, D_MODEL))] + _x_block_specs(n_sub, LANE)
                 + [const((D_MODEL, D_MODEL)), const((1, D_ATTN)), const((1, D_CONV)), const((1, D_MODEL)),
                    const((SUBLANE, D_CONV)), const((D_ATTN, D_ATTN)), const((D_ATTN, LANE))],
        out_specs=(row_blk(D_MODEL), row_blk(D_ATTN), pl.BlockSpec((None, D_ATTN, t), lambda i: (i, 0, 0)),
                   row_blk(LANE), row_blk(D_ATTN), row_blk(D_CONV),
                   row_blk(D_CONV), row_blk(D_CONV),
                   acc((1, LANE)), acc((1, D_MODEL)), acc((1, D_ATTN)), acc((1, D_CONV)), acc((D_MODEL, D_MODEL))),
        out_shape=(jax.ShapeDtypeStruct((lp, D_MODEL), F32), jax.ShapeDtypeStruct((lp, D_ATTN), MXU_DTYPE),
                   jax.ShapeDtypeStruct((nt, D_ATTN, t), MXU_DTYPE), jax.ShapeDtypeStruct((lp, LANE), MXU_DTYPE),
                   jax.ShapeDtypeStruct((lp, D_ATTN), MXU_DTYPE),
                   jax.ShapeDtypeStruct((lp, D_CONV), MXU_DTYPE), jax.ShapeDtypeStruct((lp, D_CONV), MXU_DTYPE),
                   jax.ShapeDtypeStruct((lp, D_CONV), F32),
                   jax.ShapeDtypeStruct((1, LANE), F32), jax.ShapeDtypeStruct((1, D_MODEL), F32),
                   jax.ShapeDtypeStruct((1, D_ATTN), F32), jax.ShapeDtypeStruct((1, D_CONV), F32),
                   jax.ShapeDtypeStruct((D_MODEL, D_MODEL), F32)),
        compiler_params=_params(("arbitrary",)),
    )(o, l_sum, *([rest] * 5), rest, rest, *([x2] * n_sub), meta_blk, *([tgt2] * n_sub),
      w_out, attn_g, conv_g, final_g, conv_w8, _group_matrix(), head_rep)


def _bwd_in(x2, meta_blk, norm_g, w_pad, bf_pad, fl, dc, dq, dk, dv, dza, dgb, dzc, dconv, rest, d_out, conv_w8):
    lp = fl.shape[0]
    t = ROW_TILE
    nt = lp // t
    n_sub = t // LANE
    hb = t // SUBLANE
    rev = lambda i: nt - 1 - i

    def body(*refs):
        x_refs = refs[:n_sub]
        (mb, g_ref, w_ref, bf_ref, fl_ref, dc_ref, dq_ref, dk_ref, dv_ref, dza_ref, dgb_ref, dzc_ref,
         dcv_ref, dcvn_ref, gc_ref, xc_ref, gch_ref, xch_ref, dout_ref, cw_ref, tri_ref) = refs[n_sub:n_sub + 21]
        dp_ref, gx_ref, front_ref, gn_ref, gbf_ref, gcw_ref, carry, dh_scr, gx_sems = refs[n_sub + 21:]
        step = pl.program_id(0)
        i = rev(step)

        @pl.when(step == 0)
        def _():
            for r in (gn_ref, gbf_ref, gcw_ref, carry):
                r[...] = jnp.zeros_like(r)

        dc8 = jnp.concatenate([dc_ref[...], jnp.zeros((LANE - HEADS, t), F32)], axis=0).T
        dlogf = _dot_exact(tri_ref[...], dc8) + carry[...]
        carry[...] = carry[...] + jnp.sum(dc8, axis=0, keepdims=True)
        z = fl_ref[...] + bf_ref[...]
        row = i * t + lax.broadcasted_iota(jnp.int32, (t, LANE), 0)
        d_f = jnp.where(row >= PAD_ROWS, dlogf * (1.0 / (1.0 + jnp.exp(z))), 0.0)
        gbf_ref[...] = gbf_ref[...] + jnp.sum(d_f, axis=0, keepdims=True)
        dcv = dcv_ref[...]
        dcv_next = jnp.where(i == nt - 1, 0.0, dcvn_ref[...])
        d_cx = (cw_ref[2:3, :] * dcv + cw_ref[1:2, :] * _shift_up(dcv, dcv_next, 1)
                + cw_ref[0:1, :] * _shift_up(dcv, dcv_next, 2))
        gc = gc_ref[...]
        xc = xc_ref[...]
        cx = gc * xc
        cx_prev = jnp.where(i == 0, 0.0, gch_ref[...] * xch_ref[...])
        rowi = lax.broadcasted_iota(jnp.int32, (SUBLANE, 1), 0)
        gcw = (jnp.where(rowi == 0, jnp.sum(dcv * _shift_down(cx_prev, cx, 2), axis=0, keepdims=True), 0.0)
               + jnp.where(rowi == 1, jnp.sum(dcv * _shift_down(cx_prev, cx, 1), axis=0, keepdims=True), 0.0)
               + jnp.where(rowi == 2, jnp.sum(dcv * cx, axis=0, keepdims=True), 0.0))
        gcw_ref[...] = gcw_ref[...] + gcw
        dp_ref[:, SEG_Q:SEG_Q + 512] = dq_ref[...]
        dp_ref[:, SEG_K:SEG_K + 512] = dk_ref[...]
        dp_ref[:, SEG_V:SEG_V + 512] = dv_ref[...]
        dp_ref[:, SEG_F:SEG_F + LANE] = d_f.astype(dp_ref.dtype)
        dp_ref[:, SEG_ZA:SEG_ZA + 512] = dza_ref[...]
        dp_ref[:, SEG_GB:SEG_GB + 512] = dgb_ref[...]
        dp_ref[:, SEG_GC:SEG_GC + 512] = (d_cx * xc).astype(dp_ref.dtype)
        dp_ref[:, SEG_XC:SEG_XC + 512] = (d_cx * gc).astype(dp_ref.dtype)
        dp_ref[:, SEG_ZC:SEG_ZC + 512] = dzc_ref[...]
        d_u = _dot(dp_ref[...], w_ref[...])
        first = jnp.where(i == 0, mb[...], x_refs[0][...])
        h = jnp.concatenate([first] + [r[...] for r in x_refs[1:]], axis=0)
        r1 = lax.rsqrt(jnp.mean(h * h, axis=-1, keepdims=True) + EPS)
        n_h = h * r1
        gn_ref[...] = gn_ref[...] + jnp.sum(d_u * n_h, axis=0, keepdims=True)
        dn = d_u * g_ref[...]
        d_h = dout_ref[...] + r1 * (dn - n_h * jnp.mean(dn * n_h, axis=-1, keepdims=True))
        slot = step % 2

        def to_grad_x(slot_, tile):
            return pltpu.make_async_copy(dh_scr.at[slot_], gx_ref.at[pl.ds(pl.multiple_of(tile * t - FRONT, SUBLANE), t)],
                                         gx_sems.at[slot_])

        @pl.when(step >= 2)
        def _():
            to_grad_x(slot, 1).wait()

        dh_scr[slot] = d_h

        @pl.when(i > 0)
        def _():
            to_grad_x(slot, i).start()

        @pl.when(i == 0)
        def _():
            front_ref[...] = d_h[:FRONT]
            rest_rows = pltpu.make_async_copy(dh_scr.at[slot, pl.ds(FRONT, t - FRONT)], gx_ref.at[pl.ds(0, t - FRONT)],
                                              gx_sems.at[slot])
            rest_rows.start()
            rest_rows.wait()
            if nt >= 2:
                to_grad_x(1 - slot, 1).wait()

    def x_specs():
        specs = [pl.BlockSpec((LANE, D_MODEL), lambda s: (jnp.maximum(n_sub * rev(s) - 1, 0), 0))]
        for b in range(1, n_sub):
            specs.append(pl.BlockSpec((LANE, D_MODEL), functools.partial(lambda s, b: (n_sub * rev(s) - 1 + b, 0), b=b)))
        return specs

    row_blk = lambda cols: pl.BlockSpec((t, cols), lambda s: (rev(s), 0))
    rest_blk = lambda k: pl.BlockSpec((t, 512), functools.partial(lambda s, k: (rev(s), k), k=k))
    halo_prev = lambda k: pl.BlockSpec(
        (SUBLANE, 512), functools.partial(lambda s, k: (jnp.maximum(rev(s) * hb - 1, 0), k), k=k))
    halo_next = pl.BlockSpec((SUBLANE, 512), lambda s: (jnp.minimum((rev(s) + 1) * hb, lp // SUBLANE - 1), 0))
    const = lambda shape: pl.BlockSpec(shape, lambda s: (0, 0))
    return pl.pallas_call(
        body, name="bwd_in", grid=(nt,),
        in_specs=x_specs() + [const((LANE, D_MODEL)), const((1, D_MODEL)),
                              pl.BlockSpec((D_IN_PAD, D_MODEL), lambda s: (0, 0), pipeline_mode=pl.Buffered(1)),
                              const((1, LANE)), row_blk(LANE),
                              pl.BlockSpec((HEADS, t), lambda s: (0, rev(s))),
                              row_blk(512), row_blk(512), row_blk(512), row_blk(512), row_blk(512), row_blk(512),
                              row_blk(512), halo_next, rest_blk(2), rest_blk(3), halo_prev(2), halo_prev(3),
                              row_blk(D_MODEL), const((SUBLANE, D_CONV)), const((t, t))],
        out_specs=(row_blk(D_IN_PAD), ANY, const((FRONT, D_MODEL)), const((1, D_MODEL)), const((1, LANE)),
                   const((SUBLANE, D_CONV))),
        out_shape=(jax.ShapeDtypeStruct((lp, D_IN_PAD), MXU_DTYPE), jax.ShapeDtypeStruct((lp - FRONT, D_MODEL), F32),
                   jax.ShapeDtypeStruct((FRONT, D_MODEL), F32),
                   jax.ShapeDtypeStruct((1, D_MODEL), F32), jax.ShapeDtypeStruct((1, LANE), F32),
                   jax.ShapeDtypeStruct((SUBLANE, D_CONV), F32)),
        scratch_shapes=[pltpu.VMEM((1, LANE), F32), pltpu.VMEM((2, t, D_MODEL), F32), pltpu.SemaphoreType.DMA((2,))],
        compiler_params=_params(("arbitrary",)),
    )(*([x2] * n_sub), meta_blk, norm_g, w_pad, bf_pad, fl, dc, dq, dk, dv, dza, dgb, dzc, dconv, dconv,
      rest, rest, rest, rest, d_out, conv_w8, _triangle(t, lower=False))


def _grad_w_in(u, dproj):
    lp = u.shape[0]
    tn = GW_COL_TILE
    tk = tn if lp % tn == 0 else ROW_TILE

    def body(d_ref, u_ref, o_ref, wire_ref):
        k = pl.program_id(1)

        @pl.when(k == 0)
        def _():
            o_ref[...] = jnp.zeros_like(o_ref)

        o_ref[...] = o_ref[...] + lax.dot_general(d_ref[...], u_ref[...], (((0,), (0,)), ((), ())),
                                                  preferred_element_type=F32)

        @pl.when(k == pl.num_programs(1) - 1)
        def _():
            wire_ref[...] = o_ref[...].astype(wire_ref.dtype)

    out_spec = pl.BlockSpec((tn, D_MODEL), lambda n, k: (n, 0))
    return pl.pallas_call(
        body, name="grad_w_in", grid=(D_IN_PAD // tn, lp // tk),
        in_specs=[pl.BlockSpec((tk, tn), lambda n, k: (k, n)), pl.BlockSpec((tk, D_MODEL), lambda n, k: (k, 0))],
        out_specs=(out_spec, out_spec),
        out_shape=(jax.ShapeDtypeStruct((D_IN_PAD, D_MODEL), F32), jax.ShapeDtypeStruct((D_IN_PAD, D_MODEL), WIRE_DTYPE)),
        compiler_params=_params(("parallel", "arbitrary")),
    )(dproj, u)


def _by_chip(own, others, me):
    by_mask = jnp.stack([own, others[1], others[0], others[2]])
    return [lax.dynamic_index_in_dim(by_mask, jnp.bitwise_xor(me, s), 0, keepdims=False) for s in range(N_CHIPS)]


def _both_halves(mine, other, c):
    return jnp.where(c == 0, jnp.concatenate([mine, other], axis=0), jnp.concatenate([other, mine], axis=0))


def _local_step(x2, tgt2, meta_full, norm_g, w_pad, b_f, conv_w_full, attn_g, conv_g, w_out_full, final_g):
    lp = x2.shape[0] + FRONT
    nt = lp // ROW_TILE
    meta_blk = jnp.concatenate([jnp.zeros((PAD_ROWS, D_MODEL), F32), meta_full], axis=0)
    bf_pad = jnp.pad(b_f, ((0, 0), (0, LANE - HEADS)))
    conv_w8 = jnp.pad(conv_w_full, ((0, SUBLANE - conv_w_full.shape[0]), (0, 0)))
    q, k, v, rest, fl, ct, u, q_t, k_t, v_t, cc = _in_proj(x2, meta_blk, norm_g, w_pad, bf_pad)
    ct4 = ct.reshape(SUBLANE, nt, 1, ROW_TILE)
    o, l_sum, m_max = _attn_fwd(q, k, v_t, cc)
    (d_out, d_o, do_t, neg_delta, dza, dgb, dzc, dconv, loss, g_final, g_attn, g_convg, gw_out) = _post(
        o, l_sum, rest, x2, meta_blk, tgt2, w_out_full, attn_g, conv_g, final_g, conv_w8)
    dq, dk, dv, dc = _attn_bwd(q, k, v, d_o, q_t, k_t, do_t, m_max, neg_delta, ct4)
    dproj, grad_x, d_front, g_norm, g_bf, g_cw = _bwd_in(x2, meta_blk, norm_g, w_pad, bf_pad, fl, dc.reshape(HEADS, lp), dq, dk, dv,
                                             dza, dgb, dzc, dconv, rest, d_out, conv_w8)
    gw_in, gw_in_wire = _grad_w_in(u, dproj)
    return dict(loss=loss, grad_x=grad_x, d_front=d_front, g_norm=g_norm, g_final=g_final, g_attn=g_attn, g_convg=g_convg, g_bf=g_bf,
                g_cw=g_cw, gw_out=gw_out, gw_in=gw_in, gw_in_wire=gw_in_wire)


def kernel(x, meta, norm_g, w_in, b_f, conv_w, attn_norm_g, conv_norm_g, w_out, final_norm_g, loss_target, m_meta, m_norm_g, m_w_in, m_b_f, m_conv_w, m_attn_norm_g, m_conv_norm_g, m_w_out, m_final_norm_g, v_meta, v_norm_g, v_w_in, v_b_f, v_conv_w, v_attn_norm_g, v_conv_norm_g, v_w_out, v_final_norm_g):
    cx_, cy_, cc_ = _position()
    chip = 2 * cx_ + cy_
    shard = w_in.shape[2]
    out_half = w_out.shape[1] // 2
    pick = lambda vals: jnp.where(chip == 0, vals[0], jnp.where(chip == 1, vals[1], jnp.where(chip == 2, vals[2], vals[3])))
    a_off, b_off = pick(A_OFF), pick(B_OFF)
    wt = jnp.transpose(w_in[0]).astype(MXU_DTYPE)
    wi = lax.dynamic_update_slice_in_dim(
        lax.dynamic_update_slice_in_dim(jnp.zeros((WIN_ROWS, D_MODEL), MXU_DTYPE), wt[:PIECE_A], a_off, 0),
        wt[PIECE_A:], b_off, 0)
    wo = w_out[0].astype(MXU_DTYPE)
    small = jnp.concatenate([meta, jnp.pad(conv_w[0], ((0, 8 - conv_w.shape[1]), (0, meta.shape[1] - conv_w.shape[2])))],
                            axis=0)
    gwi, gwo, gsm = _gather_weights(wi.reshape(2, WIN_HALF, D_MODEL), wo.reshape(2, out_half, D_MODEL), small)
    starts = jnp.stack([_window_start(jnp.bitwise_xor(chip, mask)) for mask in (0, 2, 1, 3)]).astype(jnp.int32)
    w_pad = _assemble_w(wi, gwi.reshape(3, WIN_ROWS, D_MODEL), starts)
    w_out_full = jnp.concatenate(_by_chip(wo, gwo.reshape(3, 2 * out_half, D_MODEL), chip), axis=0)
    small_full = jnp.concatenate(_by_chip(small, gsm, chip), axis=1)
    meta_full = small_full[:N_META]
    conv_w_full = jnp.concatenate([small_full[N_META:N_META + 3, 256 * s:256 * s + LANE] for s in range(N_CHIPS)], axis=1)
    final_g2 = final_norm_g.reshape(1, D_MODEL)
    r = _local_step(x[0], loss_target[0], meta_full, norm_g, w_pad, b_f, conv_w_full, attn_norm_g, conv_norm_g,
                    w_out_full, final_g2)
    grad_x = r["grad_x"][None]
    gb = r["gw_out"].reshape(N_CHIPS, 2, out_half, D_MODEL)
    wide = lambda a: jnp.pad(a, ((0, 0), (0, D_MODEL - a.shape[1])))
    pack = jnp.concatenate([
        r["g_norm"], r["g_final"], jnp.concatenate([r["g_attn"], r["g_convg"]], axis=1), wide(r["g_bf"]),
        wide(r["loss"]), jnp.zeros((3, D_MODEL), F32), r["d_front"][PAD_ROWS:], wide(r["g_cw"])], axis=0)
    ra, rb, packs = _pair_exchange(r["gw_in_wire"], gb, pack)
    c_idx = jnp.reshape(cc_, (1,)).astype(jnp.int32)
    chip_idx = jnp.reshape(chip, (1,)).astype(jnp.int32)
    pa, pa_wire = _pair_sum_windows(r["gw_in"], ra, c_idx)
    pb, pb_wire = _pair_sum(gb, rb, c_idx)
    xa, xb = _chip_exchange(pa_wire, pb_wire)
    ha = _chip_sum(pa, xa, chip_idx)
    hb = _chip_sum(pb, xb, chip_idx)
    oa, ob = _pair_share(ha, hb)
    g_window = _both_halves(ha, oa, cc_)
    g_w_in_t = jnp.concatenate([lax.dynamic_slice_in_dim(g_window, a_off, PIECE_A, 0),
                                lax.dynamic_slice_in_dim(g_window, b_off, shard - PIECE_A, 0)], axis=0)
    g_w_out = _both_halves(hb, ob, cc_)
    as_rows = lambda a: jnp.transpose(a, (2, 0, 1))
    g_w_in, d_w_in, nm_w_in, nv_w_in = (jnp.transpose(a, (1, 2, 0)) for a in _adamw_rows(
        as_rows(w_in), g_w_in_t, as_rows(m_w_in), as_rows(v_w_in)))
    d_w_out, nm_w_out, nv_w_out = (a[None] for a in _adamw_big(w_out[0], g_w_out, m_w_out[0], v_w_out[0], LANE))
    params = (norm_g, final_g2, attn_norm_g, conv_norm_g, b_f, meta, conv_w[0])
    ms = (m_norm_g, m_final_norm_g.reshape(1, D_MODEL), m_attn_norm_g, m_conv_norm_g, m_b_f, m_meta, m_conv_w[0])
    vs = (v_norm_g, v_final_norm_g.reshape(1, D_MODEL), v_attn_norm_g, v_conv_norm_g, v_b_f, v_meta, v_conv_w[0])
    loss, g_s, d_s, m_s, v_s = _small_update(pack, packs, params, ms, vs)

    def ordered(small_list, big_in, big_out):
        s_norm, s_final, s_attn, s_convg, s_bf, s_meta, s_cw = small_list
        return (s_meta, s_norm, big_in, s_bf, s_cw[None], s_attn, s_convg, big_out, s_final.reshape(D_MODEL))

    return (loss.reshape(()), grad_x,
            *ordered(g_s, g_w_in, g_w_out[None]), *ordered(d_s, d_w_in, d_w_out),
            *ordered(m_s, nm_w_in, nm_w_out), *ordered(v_s, nv_w_in, nv_w_out))
```

```python
import functools

import jax
import jax.numpy as jnp
from jax import lax
from jax.experimental import pallas as pl
from jax.experimental.pallas import tpu as pltpu

F32 = jnp.float32
MXU_DTYPE = jnp.bfloat16
WIRE_DTYPE = jnp.bfloat16

D_MODEL = 1024
N_META = 16
HEADS = 8
HEAD_DIM = 64
D_ATTN = HEADS * HEAD_DIM
D_CONV = 512
EPS = 1e-6
LANE = 128
SUBLANE = 8
ROW_TILE = 384
ATTN_UNROLL = 3
ATTN_BWD_QBLOCKS = 2
DELTA_TERMS = 3
STAT_TERMS = 1
FRONT = LANE
PAD_ROWS = FRONT - N_META
NEG = -1e30
LOG2E = 1.4426950408889634
N_CHIPS = 4
N_DEV = 8
VMEM_LIMIT_BYTES = 60 * 1024 * 1024

SEG_Q, SEG_K, SEG_V, SEG_F, SEG_ZA, SEG_GB, SEG_GC, SEG_XC, SEG_ZC = (
    0, 512, 1024, 1536, 1664, 2176, 2688, 3200, 3712)
D_IN = 4104
D_IN_PAD = 4224
F_END = 1544
GW_COL_TILE = 1408
WIN_ROWS = 1152
WIN_HALF = WIN_ROWS // 2
WIN_START = (0, 1024, 2160, 3072)
PIECE_A = 518
A_OFF = (0, 2, 12, 126)
B_OFF = (518, 640, 530, 644)
ADAM_LR = 0.001
ADAM_B1 = 0.9
ADAM_B2 = 0.999
ADAM_EPS = 1e-08
ADAM_WD = 0.01
ADAM_STEP = 10

MESH = pl.DeviceIdType.MESH
ANY = pl.BlockSpec(memory_space=pl.ANY)

PACK_ROWS = 32
SLOT_NORM = (0, 1, 0, 1024)
SLOT_FINAL = (1, 2, 0, 1024)
SLOT_ATTN = (2, 3, 0, 512)
SLOT_CONVG = (2, 3, 512, 1024)
SLOT_BF = (3, 4, 0, 8)
SLOT_META = (8, 24, 0, 256)
SLOT_CONVW = (24, 27, 0, 128)
LOSS_ROW = 4


def _params(sem=None):
    return pltpu.CompilerParams(dimension_semantics=sem, vmem_limit_bytes=VMEM_LIMIT_BYTES)


def _sigmoid(z):
    return 1.0 / (1.0 + jnp.exp(-z))


def _dot(a, b):
    return jnp.dot(a, b, preferred_element_type=F32)


def _dot_nt(a, b):
    return lax.dot_general(a, b, (((1,), (1,)), ((), ())), preferred_element_type=F32)


def _dot_exact(ones, x):
    ones = ones.astype(MXU_DTYPE)
    total = None
    for _ in range(3):
        term = x.astype(MXU_DTYPE)
        x = x - term.astype(F32)
        total = _dot(ones, term) if total is None else total + _dot(ones, term)
    return total


def _group_matrix():
    r = lax.broadcasted_iota(jnp.int32, (D_ATTN, D_ATTN), 0) >> 6
    c = lax.broadcasted_iota(jnp.int32, (D_ATTN, D_ATTN), 1) >> 6
    return jnp.where(r == c, 1.0, 0.0).astype(MXU_DTYPE)


def _triangle(n, lower):
    r = lax.broadcasted_iota(jnp.int32, (n, n), 0)
    c = lax.broadcasted_iota(jnp.int32, (n, n), 1)
    return jnp.where((r >= c) if lower else (c >= r), 1.0, 0.0).astype(MXU_DTYPE)


def _group_sum(x, gmat, terms=2):
    hi = x.astype(MXU_DTYPE)
    if terms == 1:
        return _dot(hi, gmat)
    lo = (x - hi.astype(F32)).astype(MXU_DTYPE)
    return _dot(hi, gmat) + _dot(lo, gmat)


def _x_block_specs(n_sub, rows):
    specs = [pl.BlockSpec((rows, D_MODEL), lambda i: (jnp.maximum(n_sub * i - 1, 0), 0))]
    for b in range(1, n_sub):
        specs.append(pl.BlockSpec((rows, D_MODEL), functools.partial(lambda i, b: (n_sub * i - 1 + b, 0), b=b)))
    return specs


def _position():
    return lax.axis_index("x"), lax.axis_index("y"), lax.axis_index("c")


def _gather_weights(wi, wo, small):
    def body(wi_ref, wo_ref, sm_ref, gwi_ref, gwo_ref, gsm_ref, send_sems, recv_sems):
        x, y, c = _position()
        sibling = (x, y, 1 - c)
        chips = [(1 - x, y), (x, 1 - y), (1 - x, 1 - y)]

        def remote(k, src, dst, to):
            return pltpu.make_async_remote_copy(src_ref=src, dst_ref=dst, send_sem=send_sems.at[k],
                                                recv_sem=recv_sems.at[k], device_id=to, device_id_type=MESH)

        first, passed, landed = [], [], []
        for a, (src_ref, g_ref) in enumerate(((wi_ref, gwi_ref), (wo_ref, gwo_ref))):
            for j, (cx, cy) in enumerate(chips):
                slot = g_ref.at[j, c]
                first.append(remote(6 * a + j, src_ref.at[c], slot, (cx, cy, c)))
                landed.append(remote(6 * a + j, slot, slot, sibling))
                passed.append(remote(6 * a + 3 + j, slot, slot, sibling))
        for j, (cx, cy) in enumerate(chips):
            first.append(remote(12 + j, sm_ref, gsm_ref.at[j], (cx, cy, c)))
        for cp in first:
            cp.start()
        for arrived, onward in zip(landed, passed):
            arrived.wait_recv()
            onward.start()
        for a, g_ref in enumerate((gwi_ref, gwo_ref)):
            for j in range(3):
                remote(6 * a + 3 + j, g_ref.at[j, 1 - c], g_ref.at[j, 1 - c], sibling).wait_recv()
        for j in range(3):
            remote(12 + j, sm_ref, gsm_ref.at[j], sibling).wait_recv()
        for cp in first + passed:
            cp.wait_send()

    return pl.pallas_call(
        body, name="gather_weights",
        out_shape=(jax.ShapeDtypeStruct((3,) + wi.shape, wi.dtype), jax.ShapeDtypeStruct((3,) + wo.shape, wo.dtype),
                   jax.ShapeDtypeStruct((3,) + small.shape, small.dtype)),
        in_specs=[ANY, ANY, ANY], out_specs=(ANY, ANY, ANY),
        scratch_shapes=[pltpu.SemaphoreType.DMA((15,)), pltpu.SemaphoreType.DMA((15,))],
    )(wi, wo, small)


def _pair_exchange(gw, gb, pack):
    n_big = N_CHIPS + 1

    def body(gw_ref, gb_ref, p_ref, ra_ref, rb_ref, o_ref, send_sems, recv_sems):
        x, y, c = _position()
        sibling = (x, y, 1 - c)

        def remote(k, src, dst, to):
            return pltpu.make_async_remote_copy(src_ref=src, dst_ref=dst, send_sem=send_sems.at[k],
                                                recv_sem=recv_sems.at[k], device_id=to, device_id_type=MESH)

        copies = [remote(N_CHIPS, gb_ref.at[:, 1 - c], rb_ref, sibling)]
        for s, start in enumerate(WIN_START):
            rows = pl.ds(pl.multiple_of(start + WIN_HALF * (1 - c), 2 * SUBLANE), WIN_HALF)
            copies.append(remote(s, gw_ref.at[rows], ra_ref.at[s], sibling))
        for mask in range(1, N_DEV):
            peer = (1 - x if mask & 4 else x, 1 - y if mask & 2 else y, 1 - c if mask & 1 else c)
            copies.append(remote(n_big + mask - 1, p_ref, o_ref.at[mask - 1], peer))
        for cp in copies:
            cp.start()
        for cp in copies:
            cp.wait()

    n_sems = n_big + N_DEV - 1
    return pl.pallas_call(
        body, name="grad_pair_exchange",
        out_shape=(jax.ShapeDtypeStruct((N_CHIPS, WIN_HALF, D_MODEL), gw.dtype),
                   jax.ShapeDtypeStruct((N_CHIPS,) + gb.shape[2:], gb.dtype),
                   jax.ShapeDtypeStruct((N_DEV - 1,) + pack.shape, pack.dtype)),
        in_specs=[ANY, ANY, ANY], out_specs=(ANY, ANY, ANY),
        scratch_shapes=[pltpu.SemaphoreType.DMA((n_sems,)), pltpu.SemaphoreType.DMA((n_sems,))],
    )(gw, gb, pack)


def _chip_exchange(pa, pb):
    def body(pa_ref, pb_ref, ra_ref, rb_ref, send_sems, recv_sems):
        x, y, c = _position()
        chips = [(1 - x, y), (x, 1 - y), (1 - x, 1 - y)]
        copies = []
        for a, (src, dst) in enumerate(((pa_ref, ra_ref), (pb_ref, rb_ref))):
            for j, (cx, cy) in enumerate(chips):
                copies.append(pltpu.make_async_remote_copy(
                    src_ref=src.at[2 * cx + cy], dst_ref=dst.at[j], send_sem=send_sems.at[3 * a + j],
                    recv_sem=recv_sems.at[3 * a + j], device_id=(cx, cy, c), device_id_type=MESH))
        for cp in copies:
            cp.start()
        for cp in copies:
            cp.wait()

    return pl.pallas_call(
        body, name="grad_chip_exchange",
        out_shape=(jax.ShapeDtypeStruct((3,) + pa.shape[1:], pa.dtype),
                   jax.ShapeDtypeStruct((3,) + pb.shape[1:], pb.dtype)),
        in_specs=[ANY, ANY], out_specs=(ANY, ANY),
        scratch_shapes=[pltpu.SemaphoreType.DMA((6,)), pltpu.SemaphoreType.DMA((6,))],
    )(pa, pb)


def _pair_share(ha, hb):
    def body(ha_ref, hb_ref, oa_ref, ob_ref, send_sems, recv_sems):
        x, y, c = _position()
        copies = [pltpu.make_async_remote_copy(
            src_ref=src, dst_ref=dst, send_sem=send_sems.at[k], recv_sem=recv_sems.at[k],
            device_id=(x, y, 1 - c), device_id_type=MESH)
            for k, (src, dst) in enumerate(((ha_ref, oa_ref), (hb_ref, ob_ref)))]
        for cp in copies:
            cp.start()
        for cp in copies:
            cp.wait()

    return pl.pallas_call(
        body, name="grad_pair_share",
        out_shape=(jax.ShapeDtypeStruct(ha.shape, ha.dtype), jax.ShapeDtypeStruct(hb.shape, hb.dtype)),
        in_specs=[ANY, ANY], out_specs=(ANY, ANY),
        scratch_shapes=[pltpu.SemaphoreType.DMA((2,)), pltpu.SemaphoreType.DMA((2,))],
    )(ha, hb)


def _pair_sum(mine, recv, c_idx):
    rows, cols = mine.shape[2:]

    def body(c_ref, a_ref, b_ref, o_ref, send_ref):
        total = a_ref[...] + b_ref[...]
        o_ref[...] = total
        send_ref[...] = total.astype(send_ref.dtype)

    out_spec = pl.BlockSpec((None, rows, cols), lambda s, c_ref: (s, 0, 0))
    return pl.pallas_call(
        body, name="grad_pair_sum",
        grid_spec=pltpu.PrefetchScalarGridSpec(
            num_scalar_prefetch=1, grid=(N_CHIPS,),
            in_specs=[pl.BlockSpec((None, None, rows, cols), lambda s, c_ref: (s, c_ref[0], 0, 0)),
                      pl.BlockSpec((None, rows, cols), lambda s, c_ref: (s, 0, 0))],
            out_specs=(out_spec, out_spec)),
        out_shape=(jax.ShapeDtypeStruct(recv.shape, recv.dtype), jax.ShapeDtypeStruct(recv.shape, WIRE_DTYPE)),
        compiler_params=_params(("parallel",)),
    )(c_idx, mine, recv)


def _window_start(s):
    return jnp.where(s == 0, WIN_START[0], jnp.where(s == 1, WIN_START[1], jnp.where(s == 2, WIN_START[2], WIN_START[3])))


def _pair_sum_windows(gw, recv, c_idx):
    tr = WIN_HALF // 3

    def body(c_ref, a_ref, b_ref, o_ref, send_ref):
        total = a_ref[...] + b_ref[...].astype(F32)
        o_ref[...] = total
        send_ref[...] = total.astype(send_ref.dtype)

    out_spec = pl.BlockSpec((None, tr, D_MODEL), lambda s, i, c_ref: (s, i, 0))
    return pl.pallas_call(
        body, name="grad_pair_sum_windows",
        grid_spec=pltpu.PrefetchScalarGridSpec(
            num_scalar_prefetch=1, grid=(N_CHIPS, WIN_HALF // tr),
            in_specs=[pl.BlockSpec((pl.Element(tr), pl.Element(D_MODEL)),
                                   lambda s, i, c_ref: (pl.multiple_of(
                                       _window_start(s) + WIN_HALF * c_ref[0] + tr * i, SUBLANE), 0)),
                      pl.BlockSpec((None, tr, D_MODEL), lambda s, i, c_ref: (s, i, 0))],
            out_specs=(out_spec, out_spec)),
        out_shape=(jax.ShapeDtypeStruct(recv.shape, F32), jax.ShapeDtypeStruct(recv.shape, WIRE_DTYPE)),
        compiler_params=_params(("parallel", "parallel")),
    )(c_idx, gw, recv)


def _assemble_w(own, others, starts):
    def body(starts_ref, own_ref, oth_ref, o_ref):
        o_ref[...] = jnp.zeros_like(o_ref)
        for k in range(N_CHIPS):
            rows = pl.ds(pl.multiple_of(starts_ref[k], 2 * SUBLANE), WIN_ROWS)
            o_ref[rows, :] = o_ref[rows, :] + (own_ref[...] if k == 0 else oth_ref[k - 1])

    return pl.pallas_call(
        body, name="assemble_w",
        in_specs=[pl.BlockSpec(memory_space=pltpu.SMEM), pl.BlockSpec(memory_space=pltpu.VMEM),
                  pl.BlockSpec(memory_space=pltpu.VMEM)],
        out_specs=pl.BlockSpec(memory_space=pltpu.VMEM),
        out_shape=jax.ShapeDtypeStruct((D_IN_PAD, D_MODEL), own.dtype),
        compiler_params=_params(),
    )(starts, own, others)


def _chip_sum(psum, recv3, chip_idx):
    rows, cols = psum.shape[1:]
    tr = rows // 2

    def body(s_ref, p_ref, r0, r1, r2, o_ref):
        o_ref[...] = ((p_ref[...] + r0[...].astype(F32)) + r1[...].astype(F32)) + r2[...].astype(F32)

    return pl.pallas_call(
        body, name="grad_chip_sum",
        grid_spec=pltpu.PrefetchScalarGridSpec(
            num_scalar_prefetch=1, grid=(2,),
            in_specs=[pl.BlockSpec((None, tr, cols), lambda i, s_ref: (s_ref[0], i, 0))] +
                     [pl.BlockSpec((None, tr, cols), functools.partial(lambda i, s_ref, j: (j, i, 0), j=j))
                      for j in range(3)],
            out_specs=pl.BlockSpec((tr, cols), lambda i, s_ref: (i, 0))),
        out_shape=jax.ShapeDtypeStruct((rows, cols), psum.dtype),
        compiler_params=_params(("parallel",)),
    )(chip_idx, psum, recv3, recv3, recv3)


def _adamw_math(w, g, m, v):
    m = ADAM_B1 * m + (1.0 - ADAM_B1) * g
    v = ADAM_B2 * v + (1.0 - ADAM_B2) * (g * g)
    m_hat = m * (1.0 / (1.0 - ADAM_B1 ** ADAM_STEP))
    v_hat = v * (1.0 / (1.0 - ADAM_B2 ** ADAM_STEP))
    delta = -ADAM_LR * (m_hat / (jnp.sqrt(v_hat) + ADAM_EPS) + ADAM_WD * w)
    return delta, m, v


def _adamw_big(w, g, m, v, tr):
    rows, cols = w.shape
    assert rows % tr == 0 and g.shape[0] >= rows

    def body(w_ref, g_ref, m_ref, v_ref, d_out, m_out, v_out):
        d, m2, v2 = _adamw_math(w_ref[...], g_ref[...], m_ref[...], v_ref[...])
        d_out[...] = d
        m_out[...] = m2
        v_out[...] = v2

    spec = pl.BlockSpec((tr, cols), lambda i: (i, 0))
    sds = jax.ShapeDtypeStruct((rows, cols), F32)
    return pl.pallas_call(
        body, name="adamw_big", grid=(rows // tr,), in_specs=[spec] * 4, out_specs=(spec,) * 3,
        out_shape=(sds,) * 3, compiler_params=_params(("parallel",)),
    )(w, g, m, v)


def _adamw_rows(w3, g, m3, v3):
    rows, _, cols = w3.shape
    tc = 2 * LANE

    def body(w_ref, g_ref, m_ref, v_ref, g_out, d_out, m_out, v_out):
        g = g_ref[...]
        d, m2, v2 = _adamw_math(w_ref[:, 0, :], g, m_ref[:, 0, :], v_ref[:, 0, :])
        g_out[:, 0, :] = g
        d_out[:, 0, :] = d
        m_out[:, 0, :] = m2
        v_out[:, 0, :] = v2

    spec3 = pl.BlockSpec((rows, 1, tc), lambda i: (0, 0, i))
    sds = jax.ShapeDtypeStruct((rows, 1, cols), F32)
    return pl.pallas_call(
        body, name="adamw_rows", grid=(cols // tc,),
        in_specs=[spec3, pl.BlockSpec((rows, tc), lambda i: (0, i)), spec3, spec3], out_specs=(spec3,) * 4,
        out_shape=(sds,) * 4, compiler_params=_params(("parallel",)),
    )(w3, g, m3, v3)


def _small_update(own, others, params, ms, vs):
    slots = (SLOT_NORM, SLOT_FINAL, SLOT_ATTN, SLOT_CONVG, SLOT_BF, SLOT_META, SLOT_CONVW)
    n = len(slots)

    def body(*refs):
        own_ref, gp_ref = refs[:2]
        w_refs, m_refs, v_refs = refs[2:2 + n], refs[2 + n:2 + 2 * n], refs[2 + 2 * n:2 + 3 * n]
        outs = refs[2 + 3 * n:3 + 7 * n]
        loss_ref = outs[0]
        g_outs, d_outs, m_outs, v_outs = (outs[1 + k * n:1 + (k + 1) * n] for k in range(4))
        g_scr, w_scr, m_scr, v_scr = refs[3 + 7 * n:]
        x, y, c = _position()
        shard = 2 * x + y
        me = 4 * x + 2 * y + c
        tot = None
        for d in range(N_DEV):
            rel = jnp.bitwise_xor(me, d)
            term = jnp.where(rel == 0, own_ref[...], gp_ref[jnp.maximum(rel, 1) - 1])
            tot = term if tot is None else tot + term
        r0, r1, _, _ = SLOT_META
        meta_sel = tot[r0:r1, 0:256]
        cw_sel = tot[24:32, 0:128]
        for k in range(1, N_CHIPS):
            meta_sel = jnp.where(shard == k, tot[r0:r1, 256 * k:256 * (k + 1)], meta_sel)
            cw_sel = jnp.where(shard == k, tot[24:32, 128 * k:128 * (k + 1)], cw_sel)
        zeros = jnp.zeros((PACK_ROWS, D_MODEL), F32)
        for scr in (g_scr, w_scr, m_scr, v_scr):
            scr[...] = zeros
        g_scr[0:8, :] = tot[0:8, :]
        g_scr[r0:r1, 0:256] = meta_sel
        g_scr[24:32, 0:128] = cw_sel
        for (a, b, c0, c1), w_ref, m_ref, v_ref in zip(slots, w_refs, m_refs, v_refs):
            w_scr[a:b, c0:c1] = w_ref[...]
            m_scr[a:b, c0:c1] = m_ref[...]
            v_scr[a:b, c0:c1] = v_ref[...]
        loss_ref[...] = g_scr[LOSS_ROW:LOSS_ROW + 1, 0:1]
        d, m2, v2 = _adamw_math(w_scr[...], g_scr[...], m_scr[...], v_scr[...])
        w_scr[...] = d
        m_scr[...] = m2
        v_scr[...] = v2
        for (a, b, c0, c1), g_o, d_o, m_o, v_o in zip(slots, g_outs, d_outs, m_outs, v_outs):
            g_o[...] = g_scr[a:b, c0:c1]
            d_o[...] = w_scr[a:b, c0:c1]
            m_o[...] = m_scr[a:b, c0:c1]
            v_o[...] = v_scr[a:b, c0:c1]

    shapes = [jax.ShapeDtypeStruct(p.shape, F32) for p in params]
    out = pl.pallas_call(
        body, name="small_update",
        out_shape=[jax.ShapeDtypeStruct((1, 1), F32)] + shapes * 4,
        scratch_shapes=[pltpu.VMEM((PACK_ROWS, D_MODEL), F32)] * 4,
        compiler_params=_params(),
    )(own, others, *params, *ms, *vs)
    return out[0], out[1:1 + n], out[1 + n:1 + 2 * n], out[1 + 2 * n:1 + 3 * n], out[1 + 3 * n:1 + 4 * n]


def _in_proj(x2, meta_blk, norm_g, w_pad, bf_pad):
    seq = x2.shape[0]
    lp = seq + FRONT
    t = ROW_TILE
    nt = lp // t
    n_sub = t // LANE

    def body(*refs):
        x_refs = refs[:n_sub]
        mb, g_ref, w_ref, bf_ref, tri_ref = refs[n_sub:n_sub + 5]
        q_ref, k_ref, v_ref, rest_ref, fl_ref, ct_ref, u_ref, qt_ref, kt_ref, vt_ref, cc_ref, carry = refs[n_sub + 5:]
        i = pl.program_id(0)

        @pl.when(i == 0)
        def _():
            carry[...] = jnp.zeros_like(carry)

        first = jnp.where(i == 0, mb[...], x_refs[0][...])
        h = jnp.concatenate([first] + [r[...] for r in x_refs[1:]], axis=0)
        ms = jnp.mean(h * h, axis=-1, keepdims=True)
        u = ((h * lax.rsqrt(ms + EPS)) * g_ref[...]).astype(MXU_DTYPE)
        u_ref[...] = u

        def seg(a, width):
            return _dot_nt(u, w_ref[a:a + width, :])

        fl = seg(SEG_F, LANE)
        fl_ref[...] = fl
        q_tile = seg(SEG_Q, D_ATTN) * (HEAD_DIM ** -0.5)
        q_ref[...] = q_tile.astype(MXU_DTYPE)
        qt_ref[...] = q_tile.T.astype(MXU_DTYPE)
        z = fl + bf_ref[...]
        logf = jnp.minimum(z, 0.0) - jnp.log(1.0 + jnp.exp(-jnp.abs(z)))
        row = i * t + lax.broadcasted_iota(jnp.int32, (t, LANE), 0)
        logf = jnp.where(row >= PAD_ROWS, logf, 0.0)
        k_tile = seg(SEG_K, D_ATTN)
        k_ref[...] = k_tile.astype(MXU_DTYPE)
        kt_ref[...] = k_tile.T.astype(MXU_DTYPE)
        cs = _dot_exact(tri_ref[...], logf) + carry[...]
        carry[...] = carry[...] + jnp.sum(logf, axis=0, keepdims=True)
        v_tile = seg(SEG_V, D_ATTN)
        v_ref[...] = v_tile.astype(MXU_DTYPE)
        vt_ref[...] = v_tile.T.astype(MXU_DTYPE)
        col = i * t + lax.broadcasted_iota(jnp.int32, (SUBLANE, t), 1)
        ct_ref[...] = jnp.where(col >= PAD_ROWS, cs.T[0:SUBLANE, :], -NEG)
        cc_ref[...] = jnp.where(row >= PAD_ROWS, cs, -NEG)
        for s in range(5):
            rest_ref[:, 512 * s:512 * (s + 1)] = seg(SEG_ZA + 512 * s, 512)

    row_blk = lambda cols: pl.BlockSpec((t, cols), lambda i: (i, 0))
    tr_blk = pl.BlockSpec((None, D_ATTN, t), lambda i: (i, 0, 0))
    const = lambda shape: pl.BlockSpec(shape, lambda i: (0, 0))
    return pl.pallas_call(
        body, name="in_proj", grid=(nt,),
        in_specs=_x_block_specs(n_sub, LANE) + [const((LANE, D_MODEL)), const((1, D_MODEL)),
                                                pl.BlockSpec((D_IN_PAD, D_MODEL), lambda i: (0, 0),
                                                             pipeline_mode=pl.Buffered(1)),
                                                const((1, LANE)), const((t, t))],
        out_specs=(row_blk(D_ATTN), row_blk(D_ATTN), row_blk(D_ATTN), row_blk(5 * 512), row_blk(LANE),
                   pl.BlockSpec((SUBLANE, t), lambda i: (0, i)), row_blk(D_MODEL), tr_blk, tr_blk, tr_blk, row_blk(LANE)),
        out_shape=(jax.ShapeDtypeStruct((lp, D_ATTN), MXU_DTYPE), jax.ShapeDtypeStruct((lp, D_ATTN), MXU_DTYPE),
                   jax.ShapeDtypeStruct((lp, D_ATTN), MXU_DTYPE), jax.ShapeDtypeStruct((lp, 5 * 512), F32),
                   jax.ShapeDtypeStruct((lp, LANE), F32),
                   jax.ShapeDtypeStruct((SUBLANE, lp), F32), jax.ShapeDtypeStruct((lp, D_MODEL), MXU_DTYPE),
                   jax.ShapeDtypeStruct((nt, D_ATTN, t), MXU_DTYPE), jax.ShapeDtypeStruct((nt, D_ATTN, t), MXU_DTYPE),
                   jax.ShapeDtypeStruct((nt, D_ATTN, t), MXU_DTYPE), jax.ShapeDtypeStruct((lp, LANE), F32)),
        scratch_shapes=[pltpu.VMEM((1, LANE), F32)],
        compiler_params=_params(("arbitrary",)),
    )(*([x2] * n_sub), meta_blk, norm_g, w_pad, bf_pad, _triangle(t, lower=True))


def _head_masks():
    lane = lax.broadcasted_iota(jnp.int32, (1, LANE), 1)
    return lane < HEAD_DIM, lane >= HEAD_DIM


def _pair_specs(lp, nt, t):
    blk = pl.BlockSpec((lp, LANE), lambda g: (0, g))
    ct_a = pl.BlockSpec((None, nt, 1, t), lambda g: (2 * g, 0, 0, 0))
    ct_b = pl.BlockSpec((None, nt, 1, t), lambda g: (2 * g + 1, 0, 0, 0))
    return blk, ct_a, ct_b


def _sub_rows(s, col):
    return jnp.concatenate([s[:, a * LANE:(a + 1) * LANE] - col for a in range(s.shape[1] // LANE)], axis=1)


def _loop_unrolled(lo, hi, step, init, n):
    def group(jj, carry):
        for k in range(n):
            carry = step(lo + n * jj + k, carry)
        return carry

    groups = (hi - lo) // n
    carry = lax.fori_loop(0, groups, group, init)
    return lax.fori_loop(lo + n * groups, hi, step, carry)


def _attn_fwd(q, k, v_t, cc):
    lp = q.shape[0]
    t = ROW_TILE
    nt = lp // t
    ext = LANE + 2 * SUBLANE

    def body(q_ref, k_ref, vt_ref, cc_ref, o_ref, l_ref, m_ref, s_scr, last_scr, m_scr, mfin_scr, acc_scr, c_scr):
        masks = _head_masks()
        lane = lax.broadcasted_iota(jnp.int32, (1, LANE), 1)
        for hh in range(2):
            picked = jnp.where(lane == 2 * pl.program_id(0) + hh, cc_ref[...], 0.0)
            c_scr[hh] = jnp.broadcast_to(jnp.sum(picked, axis=-1, keepdims=True), (lp, LANE))
        visible = lax.broadcasted_iota(jnp.int32, (t, t), 0) <= lax.broadcasted_iota(jnp.int32, (t, t), 1)
        top = lax.broadcasted_iota(jnp.int32, (LANE, 1), 0) < HEAD_DIM
        second_head = (lax.broadcasted_iota(jnp.int32, (2 * SUBLANE, 2 * t), 1) >= t).astype(jnp.int32)
        ones_rows = jnp.where(lax.broadcasted_iota(jnp.int32, (2 * SUBLANE, 2 * t), 0) == second_head,
                              1.0, 0.0).astype(MXU_DTYPE)

        on_first_diagonal = jnp.concatenate([visible, jnp.ones((t, t), jnp.bool_)], axis=1)

        def scores(j, queries):
            kj = k_ref[pl.ds(pl.multiple_of(j * t, t), t), :]
            return _dot_nt(jnp.concatenate([jnp.where(hm, kj, 0).astype(MXU_DTYPE) for hm in masks], axis=0), queries)

        def biased(s2, j, hh):
            return _sub_rows(s2[hh * t:(hh + 1) * t, :], c_scr[hh, pl.ds(pl.multiple_of(j * t, t), t), :]) * LOG2E

        def track_max(hh, s, lo, hi):
            m = m_scr[hh, :, lo:hi]
            for a in range(t // SUBLANE):
                m = jnp.maximum(m, s[a * SUBLANE:(a + 1) * SUBLANE, :])
            m_scr[hh, :, lo:hi] = m

        def probabilities(scores_of, ms_cols):
            return jnp.concatenate([jnp.exp2(scores_of(hh) - ms_cols[hh]).astype(MXU_DTYPE) for hh in range(2)], axis=0)

        def values(j):
            vtj = vt_ref[j]
            v2 = jnp.concatenate([jnp.where(top, vtj, 0).astype(MXU_DTYPE),
                                  jnp.where(top, 0, vtj).astype(MXU_DTYPE)], axis=1)
            return jnp.concatenate([v2, ones_rows], axis=0)

        def stage(done, ahead):
            if ahead is not None:
                i_a, rows_a = ahead
                qa = q_ref[pl.ds(pl.multiple_of(i_a * t, t), rows_a), :]
                m_scr[...] = jnp.full(m_scr.shape, NEG, F32)

                def max_step(j, mask=None):
                    s2 = scores(j, qa)
                    for hh in range(2):
                        s = biased(s2, j, hh)
                        if mask is not None:
                            s = jnp.where(mask, s, NEG)
                        s_scr[j, hh * t:(hh + 1) * t, 0:rows_a] = s
                        track_max(hh, s, 0, rows_a)

            if done is not None:
                i_d, rows_d = done
                r0 = pl.multiple_of(i_d * t, t)
                ms = [mfin_scr[hh, 0:1, 0:rows_d] for hh in range(2)]
                acc_scr[...] = jnp.zeros(acc_scr.shape, F32)

                def key_step(j, carry):
                    p = probabilities(lambda hh: s_scr[j, hh * t:(hh + 1) * t, 0:rows_d], ms)
                    acc_scr[:, 0:rows_d] = acc_scr[:, 0:rows_d] + _dot(values(j), p)
                    if ahead is not None:
                        max_step(j)
                    return carry

                _loop_unrolled(0, i_d + 1, key_step, 0, ATTN_UNROLL)
                if rows_d == 2 * t:
                    p = probabilities(lambda hh: last_scr[hh * t:(hh + 1) * t, :], [m[:, t:] for m in ms])
                    acc_scr[:, t:rows_d] = acc_scr[:, t:rows_d] + _dot(values(i_d + 1), p)
                acc = acc_scr[:, 0:rows_d]
                l_pair = jnp.where(top, acc[LANE:LANE + 1], acc[LANE + 1:LANE + 2])
                o_ref[pl.ds(r0, rows_d), :] = (acc[:LANE] / l_pair).T
                l_ref[pl.ds(r0, rows_d), :] = l_pair.T
                for hh in range(2):
                    m_ref[pl.ds(r0, rows_d), hh * LANE:(hh + 1) * LANE] = jnp.broadcast_to(ms[hh], (LANE, rows_d)).T

            if ahead is not None:
                if done is not None:
                    max_step(i_a - 1)
                max_step(i_a, on_first_diagonal if rows_a == 2 * t else visible)
                if rows_a == 2 * t:
                    s2 = scores(i_a + 1, qa[t:])
                    for hh in range(2):
                        s = jnp.where(visible, biased(s2, i_a + 1, hh), NEG)
                        last_scr[hh * t:(hh + 1) * t, :] = s
                        track_max(hh, s, t, rows_a)
                for hh in range(2):
                    mfin_scr[hh, :, 0:rows_a] = jnp.broadcast_to(jnp.max(m_scr[hh, :, 0:rows_a], axis=0, keepdims=True),
                                                                 (SUBLANE, rows_a))

        pairs = nt // 2
        stage(None, (0, 2 * t))

        def pair_to_pair(u, _):
            stage((2 * u, 2 * t), (2 * u + 2, 2 * t))
            return 0

        lax.fori_loop(0, pairs - 1, pair_to_pair, 0)
        if nt % 2:
            stage((2 * pairs - 2, 2 * t), (nt - 1, t))
            stage((nt - 1, t), None)
        else:
            stage((2 * pairs - 2, 2 * t), None)

    blk = pl.BlockSpec((lp, LANE), lambda g: (0, g))
    return pl.pallas_call(
        body, name="attn_fwd", grid=(HEADS // 2,),
        in_specs=[blk, blk, pl.BlockSpec((nt, LANE, t), lambda g: (0, g, 0)),
                  pl.BlockSpec((lp, LANE), lambda g: (0, 0), pipeline_mode=pl.Buffered(1))],
        out_specs=(blk, blk, pl.BlockSpec((lp, 2 * LANE), lambda g: (0, g))),
        out_shape=(jax.ShapeDtypeStruct((lp, D_ATTN), F32), jax.ShapeDtypeStruct((lp, D_ATTN), F32),
                   jax.ShapeDtypeStruct((lp, HEADS * LANE), F32)),
        scratch_shapes=[pltpu.VMEM((nt, 2 * t, 2 * t), F32), pltpu.VMEM((2 * t, t), F32),
                        pltpu.VMEM((2, SUBLANE, 2 * t), F32), pltpu.VMEM((2, SUBLANE, 2 * t), F32),
                        pltpu.VMEM((ext, 2 * t), F32), pltpu.VMEM((2, lp, LANE), F32)],
        compiler_params=_params(("parallel",)),
    )(q, k, v_t, cc)


def _attn_bwd(q, k, v, do, q_t, k_t, do_t, m, neg_delta, ct4):
    lp = q.shape[0]
    t = ROW_TILE
    nt = lp // t

    def body(q_ref, k_ref, v_ref, do_ref, qt_ref, kt_ref, dot_ref, ma_ref, mb_ref, nd_ref, cta_ref, ctb_ref,
             dq_ref, dk_ref, dv_ref, dc_ref, dq_acc, dk_acc, dv_acc):
        masks = _head_masks()
        ct_refs, m_refs = (cta_ref, ctb_ref), (ma_ref, mb_ref)
        row_head = 2 * pl.program_id(0) + (lax.broadcasted_iota(jnp.int32, (2 * t, LANE), 0) >= t).astype(jnp.int32)
        col = lax.broadcasted_iota(jnp.int32, (2 * t, LANE), 1)
        delta_ones = jnp.where((col < HEADS * DELTA_TERMS) & (col % HEADS == row_head), 1.0, 0.0).astype(MXU_DTYPE)
        below = lax.broadcasted_iota(jnp.int32, (t, t), 1) <= lax.broadcasted_iota(jnp.int32, (t, t), 0)
        top = lax.broadcasted_iota(jnp.int32, (LANE, 1), 0) < HEAD_DIM
        dq_acc[...] = jnp.zeros_like(dq_acc)

        on_first_diagonal = jnp.concatenate([below, jnp.ones((t, t), jnp.bool_)], axis=0)

        def k_block(j, _, with_next=True):
            c0 = pl.multiple_of(j * t, t)
            kj = k_ref[pl.ds(c0, t), :]
            vj = v_ref[pl.ds(c0, t), :]
            k2 = jnp.concatenate([jnp.where(hm, kj, 0).astype(MXU_DTYPE) for hm in masks], axis=0)
            v2 = jnp.concatenate([jnp.where(hm, vj, 0).astype(MXU_DTYPE) for hm in masks], axis=0)
            v2 = jnp.concatenate([v2, delta_ones], axis=1)
            ck = [r[j] for r in ct_refs]
            ktj = kt_ref[j]
            k2t = jnp.concatenate([jnp.where(top, ktj, 0).astype(MXU_DTYPE), jnp.where(top, 0, ktj).astype(MXU_DTYPE)],
                                  axis=1)
            dk_acc[...] = jnp.zeros_like(dk_acc)
            dv_acc[...] = jnp.zeros_like(dv_acc)

            def q_block(i, colsums, mask=None, rows=t):
                r0 = pl.multiple_of(i * t, t)
                qi = q_ref[pl.ds(r0, rows), :]
                doi = jnp.concatenate([do_ref[pl.ds(r0, rows), :], nd_ref[pl.ds(r0, rows), :]], axis=1)
                qti = jnp.concatenate([qt_ref[i + b] for b in range(rows // t)], axis=1)
                doti = jnp.concatenate([dot_ref[i + b] for b in range(rows // t)], axis=1)
                s2 = _dot_nt(qi, k2)
                dp2 = _dot_nt(doi, v2)
                out, ps, dss = [], [], []
                for hh in range(2):
                    s = (s2[:, hh * t:(hh + 1) * t] - ck[hh]) * LOG2E
                    if mask is not None:
                        s = jnp.where(mask, s, NEG)
                    p = jnp.exp2(_sub_rows(s, m_refs[hh][pl.ds(r0, rows), :])).astype(MXU_DTYPE)
                    ds32 = p.astype(F32) * dp2[:, hh * t:(hh + 1) * t]
                    ps.append(p)
                    dss.append(ds32.astype(MXU_DTYPE))
                    out.append(colsums[hh] + jnp.sum(ds32, axis=0, keepdims=True))
                ds_cat = jnp.concatenate(dss, axis=1)
                dv_acc[...] = dv_acc[...] + _dot(doti, jnp.concatenate(ps, axis=1))
                dk_acc[...] = dk_acc[...] + _dot(qti, ds_cat)
                dq_t = _dot(k2t, ds_cat.T)
                for b in range(rows // t):
                    dq_acc[i + b] = dq_acc[i + b] + dq_t[:, b * t:(b + 1) * t]
                return tuple(out)

            nq = ATTN_BWD_QBLOCKS
            colsums = (jnp.zeros((1, t), F32), jnp.zeros((1, t), F32))
            if with_next:
                colsums = q_block(j, colsums, on_first_diagonal, nq * t)
            else:
                colsums = q_block(j, colsums, below)
            first = j + (nq if with_next else 1)
            groups = (nt - first) // nq
            colsums = lax.fori_loop(0, groups, lambda p, c: q_block(first + nq * p, c, None, nq * t), colsums)
            colsums = lax.fori_loop(first + nq * groups, nt, q_block, colsums)
            for hh in range(2):
                dc_ref[hh, j] = -colsums[hh]
            own = lambda acc: jnp.concatenate([acc[:HEAD_DIM, :t], acc[HEAD_DIM:, t:]], axis=0).T
            dk_ref[pl.ds(c0, t), :] = own(dk_acc[...]).astype(dk_ref.dtype)
            dv_ref[pl.ds(c0, t), :] = own(dv_acc[...]).astype(dv_ref.dtype)
            return 0

        lax.fori_loop(0, nt - 1, k_block, 0)
        k_block(nt - 1, 0, with_next=False)
        for i in range(nt):
            dq_ref[i * t:(i + 1) * t, :] = (dq_acc[i].T * (HEAD_DIM ** -0.5)).astype(dq_ref.dtype)

    blk, ct_a, ct_b = _pair_specs(lp, nt, t)
    rep_a = pl.BlockSpec((lp, LANE), lambda g: (0, 2 * g))
    rep_b = pl.BlockSpec((lp, LANE), lambda g: (0, 2 * g + 1))
    tr_blk = pl.BlockSpec((nt, LANE, t), lambda g: (0, g, 0))
    return pl.pallas_call(
        body, name="attn_bwd", grid=(HEADS // 2,),
        in_specs=[blk] * 4 + [tr_blk, tr_blk, tr_blk, rep_a, rep_b, pl.BlockSpec((lp, LANE), lambda g: (0, 0)), ct_a, ct_b],
        out_specs=(blk, blk, blk, pl.BlockSpec((2, nt, 1, t), lambda g: (g, 0, 0, 0))),
        out_shape=(jax.ShapeDtypeStruct((lp, D_ATTN), MXU_DTYPE),) * 3
                  + (jax.ShapeDtypeStruct((HEADS, nt, 1, t), F32),),
        scratch_shapes=[pltpu.VMEM((nt, LANE, t), F32), pltpu.VMEM((LANE, 2 * t), F32), pltpu.VMEM((LANE, 2 * t), F32)],
        compiler_params=_params(("parallel",)),
    )(q, k, v, do, q_t, k_t, do_t, m, m, neg_delta, ct4, ct4)


def _shift_down(prev8, cur, k):
    ext = jnp.concatenate([prev8, cur], axis=0)
    return pltpu.roll(ext, k, 0)[SUBLANE:, :]


def _shift_up(cur, next8, k):
    ext = jnp.concatenate([cur, next8], axis=0)
    n = ext.shape[0]
    return pltpu.roll(ext, n - k, 0)[:cur.shape[0], :]


def _post(o, l_sum, rest, x2, meta_blk, tgt2, w_out, attn_g, conv_g, final_g, conv_w8):
    lp = o.shape[0]
    t = ROW_TILE
    nt = lp // t
    n_sub = t // LANE
    hb = t // SUBLANE

    def body(*refs):
        o_ref, l_ref, za_ref, gb_ref, gc_ref, xc_ref, zc_ref, gch_ref, xch_ref = refs[:9]
        x_refs = refs[9:9 + n_sub]
        mb = refs[9 + n_sub]
        t_refs = refs[10 + n_sub:10 + 2 * n_sub]
        wo_ref, ag_ref, cg_ref, fg_ref, cw_ref, gm_ref, hr_ref = refs[10 + 2 * n_sub:17 + 2 * n_sub]
        (dout_ref, do_ref, dot_ref, dl_ref, dza_ref, dgb_ref, dzc_ref, dcv_ref,
         loss_ref, gf_ref, gag_ref, gcg_ref, gwo_ref) = refs[17 + 2 * n_sub:]
        i = pl.program_id(0)

        @pl.when(i == 0)
        def _():
            for r in (loss_ref, gf_ref, gag_ref, gcg_ref, gwo_ref):
                r[...] = jnp.zeros_like(r)

        gmat = gm_ref[...]
        inv_g = 1.0 / HEAD_DIM
        o_v = o_ref[...]
        ra = lax.rsqrt(_group_sum(o_v * o_v, gmat, STAT_TERMS) * inv_g + EPS)
        n_a = o_v * ra
        a_n = n_a * ag_ref[...]
        za = za_ref[...]
        sig_a = _sigmoid(za)
        sz_a = za * sig_a
        y_a = a_n * sz_a
        gb = gb_ref[...]
        gc = gc_ref[...]
        xc = xc_ref[...]
        cx = gc * xc
        cx_prev = jnp.where(i == 0, 0.0, gch_ref[...] * xch_ref[...])
        conv = (cw_ref[0:1, :] * _shift_down(cx_prev, cx, 2) + cw_ref[1:2, :] * _shift_down(cx_prev, cx, 1)
                + cw_ref[2:3, :] * cx)
        e = gb * conv
        re = lax.rsqrt(_group_sum(e * e, gmat, STAT_TERMS) * inv_g + EPS)
        n_e = e * re
        e_n = n_e * cg_ref[...]
        zc = zc_ref[...]
        sig_c = _sigmoid(zc)
        sz_c = zc * sig_c
        y_c = e_n * sz_c
        mix = jnp.concatenate([y_a, y_c], axis=-1)
        mix_b = mix.astype(MXU_DTYPE)
        first = jnp.where(i == 0, mb[...], x_refs[0][...])
        h = jnp.concatenate([first] + [r[...] for r in x_refs[1:]], axis=0)
        out = h + _dot(mix_b, wo_ref[...])
        r2 = lax.rsqrt(jnp.mean(out * out, axis=-1, keepdims=True) + EPS)
        n_f = out * r2
        y = n_f * fg_ref[...]
        tgt = jnp.concatenate([r[...] for r in t_refs], axis=0)
        valid = (i * t + lax.broadcasted_iota(jnp.int32, (t, 1), 0)) >= FRONT
        diff = jnp.where(valid, y - tgt, 0.0)
        loss_ref[...] = loss_ref[...] + 0.5 * jnp.sum(jnp.sum(diff * diff, axis=-1, keepdims=True) * (1.0 / D_MODEL))
        dy = diff * (1.0 / D_MODEL)
        gf_ref[...] = gf_ref[...] + jnp.sum(dy * n_f, axis=0, keepdims=True)
        dn = dy * fg_ref[...]
        d_out = r2 * (dn - n_f * jnp.mean(dn * n_f, axis=-1, keepdims=True))
        dout_ref[...] = d_out
        d_out_b = d_out.astype(MXU_DTYPE)
        d_mix = _dot_nt(d_out_b, wo_ref[...])
        gwo_ref[...] = gwo_ref[...] + _dot(mix.T.astype(MXU_DTYPE), d_out_b)
        d_ya = d_mix[:, :D_ATTN]
        d_yc = d_mix[:, D_ATTN:]
        d_an = d_ya * sz_a
        dza_ref[...] = (d_ya * a_n * (sig_a * (1.0 + za * (1.0 - sig_a)))).astype(dza_ref.dtype)
        gag_ref[...] = gag_ref[...] + jnp.sum(d_an * n_a, axis=0, keepdims=True)
        dn_a = d_an * ag_ref[...]
        d_o = ra * (dn_a - n_a * (_group_sum(dn_a * n_a, gmat, STAT_TERMS) * inv_g))
        d_o_l = d_o / l_ref[...]
        d_o_b = d_o_l.astype(do_ref.dtype)
        do_ref[...] = d_o_b
        dot_ref[...] = d_o_l.T.astype(dot_ref.dtype)
        delta = _group_sum(d_o_b.astype(F32) * o_v, hr_ref[...])
        terms, rest_of = [], delta
        for k in range(DELTA_TERMS):
            terms.append(rest_of.astype(MXU_DTYPE).astype(F32))
            rest_of = rest_of - terms[-1]
        dl_ref[...] = -sum(pltpu.roll(term, HEADS * k, 1) if k else term
                           for k, term in enumerate(terms)).astype(dl_ref.dtype)
        d_en = d_yc * sz_c
        dzc_ref[...] = (d_yc * e_n * (sig_c * (1.0 + zc * (1.0 - sig_c)))).astype(dzc_ref.dtype)
        gcg_ref[...] = gcg_ref[...] + jnp.sum(d_en * n_e, axis=0, keepdims=True)
        dn_e = d_en * cg_ref[...]
        d_e = re * (dn_e - n_e * (_group_sum(dn_e * n_e, gmat, STAT_TERMS) * inv_g))
        dgb_ref[...] = (d_e * conv).astype(dgb_ref.dtype)
        dcv_ref[...] = d_e * gb

    head_rep = jnp.where((lax.broadcasted_iota(jnp.int32, (D_ATTN, LANE), 0) >> 6)
                         == lax.broadcasted_iota(jnp.int32, (D_ATTN, LANE), 1), 1.0, 0.0).astype(MXU_DTYPE)
    row_blk = lambda cols: pl.BlockSpec((t, cols), lambda i: (i, 0))
    rest_blk = lambda s: pl.BlockSpec((t, 512), functools.partial(lambda i, s: (i, s), s=s))
    halo = lambda s: pl.BlockSpec((SUBLANE, 512), functools.partial(lambda i, s: (jnp.maximum(i * hb - 1, 0), s), s=s))
    const = lambda shape: pl.BlockSpec(shape, lambda i: (0, 0))
    acc = lambda shape: pl.BlockSpec(shape, lambda i: (0, 0))
    return pl.pallas_call(
        body, name="post_fwd_bwd", grid=(nt,),
        in_specs=[row_blk(D_ATTN), row_blk(D_ATTN)] + [rest_blk(s) for s in range(5)] + [halo(2), halo(3)]
                 + _x_block_specs(n_sub, LANE) + [const((LANE, D_MODEL))] + _x_block_specs(n_sub, LANE)
                 + [const((D_MODEL, D_MODEL)), const((1, D_ATTN)), const((1, D_CONV)), const((1, D_MODEL)),
                    const((SUBLANE, D_CONV)), const((D_ATTN, D_ATTN)), const((D_ATTN, LANE))],
        out_specs=(row_blk(D_MODEL), row_blk(D_ATTN), pl.BlockSpec((None, D_ATTN, t), lambda i: (i, 0, 0)),
                   row_blk(LANE), row_blk(D_ATTN), row_blk(D_CONV),
                   row_blk(D_CONV), row_blk(D_CONV),
                   acc((1, LANE)), acc((1, D_MODEL)), acc((1, D_ATTN)), acc((1, D_CONV)), acc((D_MODEL, D_MODEL))),
        out_shape=(jax.ShapeDtypeStruct((lp, D_MODEL), F32), jax.ShapeDtypeStruct((lp, D_ATTN), MXU_DTYPE),
                   jax.ShapeDtypeStruct((nt, D_ATTN, t), MXU_DTYPE), jax.ShapeDtypeStruct((lp, LANE), MXU_DTYPE),
                   jax.ShapeDtypeStruct((lp, D_ATTN), MXU_DTYPE),
                   jax.ShapeDtypeStruct((lp, D_CONV), MXU_DTYPE), jax.ShapeDtypeStruct((lp, D_CONV), MXU_DTYPE),
                   jax.ShapeDtypeStruct((lp, D_CONV), F32),
                   jax.ShapeDtypeStruct((1, LANE), F32), jax.ShapeDtypeStruct((1, D_MODEL), F32),
                   jax.ShapeDtypeStruct((1, D_ATTN), F32), jax.ShapeDtypeStruct((1, D_CONV), F32),
                   jax.ShapeDtypeStruct((D_MODEL, D_MODEL), F32)),
        compiler_params=_params(("arbitrary",)),
    )(o, l_sum, *([rest] * 5), rest, rest, *([x2] * n_sub), meta_blk, *([tgt2] * n_sub),
      w_out, attn_g, conv_g, final_g, conv_w8, _group_matrix(), head_rep)


def _bwd_in(x2, meta_blk, norm_g, w_pad, bf_pad, fl, dc, dq, dk, dv, dza, dgb, dzc, dconv, rest, d_out, conv_w8):
    lp = fl.shape[0]
    t = ROW_TILE
    nt = lp // t
    n_sub = t // LANE
    hb = t // SUBLANE
    rev = lambda i: nt - 1 - i

    def body(*refs):
        x_refs = refs[:n_sub]
        (mb, g_ref, w_ref, bf_ref, fl_ref, dc_ref, dq_ref, dk_ref, dv_ref, dza_ref, dgb_ref, dzc_ref,
         dcv_ref, dcvn_ref, gc_ref, xc_ref, gch_ref, xch_ref, dout_ref, cw_ref, tri_ref) = refs[n_sub:n_sub + 21]
        dp_ref, gx_ref, front_ref, gn_ref, gbf_ref, gcw_ref, carry, dh_scr, gx_sems = refs[n_sub + 21:]
        step = pl.program_id(0)
        i = rev(step)

        @pl.when(step == 0)
        def _():
            for r in (gn_ref, gbf_ref, gcw_ref, carry):
                r[...] = jnp.zeros_like(r)

        dc8 = jnp.concatenate([dc_ref[...], jnp.zeros((LANE - HEADS, t), F32)], axis=0).T
        dlogf = _dot_exact(tri_ref[...], dc8) + carry[...]
        carry[...] = carry[...] + jnp.sum(dc8, axis=0, keepdims=True)
        z = fl_ref[...] + bf_ref[...]
        row = i * t + lax.broadcasted_iota(jnp.int32, (t, LANE), 0)
        d_f = jnp.where(row >= PAD_ROWS, dlogf * (1.0 / (1.0 + jnp.exp(z))), 0.0)
        gbf_ref[...] = gbf_ref[...] + jnp.sum(d_f, axis=0, keepdims=True)
        dcv = dcv_ref[...]
        dcv_next = jnp.where(i == nt - 1, 0.0, dcvn_ref[...])
        d_cx = (cw_ref[2:3, :] * dcv + cw_ref[1:2, :] * _shift_up(dcv, dcv_next, 1)
                + cw_ref[0:1, :] * _shift_up(dcv, dcv_next, 2))
        gc = gc_ref[...]
        xc = xc_ref[...]
        cx = gc * xc
        cx_prev = jnp.where(i == 0, 0.0, gch_ref[...] * xch_ref[...])
        rowi = lax.broadcasted_iota(jnp.int32, (SUBLANE, 1), 0)
        gcw = (jnp.where(rowi == 0, jnp.sum(dcv * _shift_down(cx_prev, cx, 2), axis=0, keepdims=True), 0.0)
               + jnp.where(rowi == 1, jnp.sum(dcv * _shift_down(cx_prev, cx, 1), axis=0, keepdims=True), 0.0)
               + jnp.where(rowi == 2, jnp.sum(dcv * cx, axis=0, keepdims=True), 0.0))
        gcw_ref[...] = gcw_ref[...] + gcw
        dp_ref[:, SEG_Q:SEG_Q + 512] = dq_ref[...]
        dp_ref[:, SEG_K:SEG_K + 512] = dk_ref[...]
        dp_ref[:, SEG_V:SEG_V + 512] = dv_ref[...]
        dp_ref[:, SEG_F:SEG_F + LANE] = d_f.astype(dp_ref.dtype)
        dp_ref[:, SEG_ZA:SEG_ZA + 512] = dza_ref[...]
        dp_ref[:, SEG_GB:SEG_GB + 512] = dgb_ref[...]
        dp_ref[:, SEG_GC:SEG_GC + 512] = (d_cx * xc).astype(dp_ref.dtype)
        dp_ref[:, SEG_XC:SEG_XC + 512] = (d_cx * gc).astype(dp_ref.dtype)
        dp_ref[:, SEG_ZC:SEG_ZC + 512] = dzc_ref[...]
        d_u = _dot(dp_ref[...], w_ref[...])
        first = jnp.where(i == 0, mb[...], x_refs[0][...])
        h = jnp.concatenate([first] + [r[...] for r in x_refs[1:]], axis=0)
        r1 = lax.rsqrt(jnp.mean(h * h, axis=-1, keepdims=True) + EPS)
        n_h = h * r1
        gn_ref[...] = gn_ref[...] + jnp.sum(d_u * n_h, axis=0, keepdims=True)
        dn = d_u * g_ref[...]
        d_h = dout_ref[...] + r1 * (dn - n_h * jnp.mean(dn * n_h, axis=-1, keepdims=True))
        slot = step % 2

        def to_grad_x(slot_, tile):
            return pltpu.make_async_copy(dh_scr.at[slot_], gx_ref.at[pl.ds(pl.multiple_of(tile * t - FRONT, SUBLANE), t)],
                                         gx_sems.at[slot_])

        @pl.when(step >= 2)
        def _():
            to_grad_x(slot, 1).wait()

        dh_scr[slot] = d_h

        @pl.when(i > 0)
        def _():
            to_grad_x(slot, i).start()

        @pl.when(i == 0)
        def _():
            front_ref[...] = d_h[:FRONT]
            rest_rows = pltpu.make_async_copy(dh_scr.at[slot, pl.ds(FRONT, t - FRONT)], gx_ref.at[pl.ds(0, t - FRONT)],
                                              gx_sems.at[slot])
            rest_rows.start()
            rest_rows.wait()
            if nt >= 2:
                to_grad_x(1 - slot, 1).wait()

    def x_specs():
        specs = [pl.BlockSpec((LANE, D_MODEL), lambda s: (jnp.maximum(n_sub * rev(s) - 1, 0), 0))]
        for b in range(1, n_sub):
            specs.append(pl.BlockSpec((LANE, D_MODEL), functools.partial(lambda s, b: (n_sub * rev(s) - 1 + b, 0), b=b)))
        return specs

    row_blk = lambda cols: pl.BlockSpec((t, cols), lambda s: (rev(s), 0))
    rest_blk = lambda k: pl.BlockSpec((t, 512), functools.partial(lambda s, k: (rev(s), k), k=k))
    halo_prev = lambda k: pl.BlockSpec(
        (SUBLANE, 512), functools.partial(lambda s, k: (jnp.maximum(rev(s) * hb - 1, 0), k), k=k))
    halo_next = pl.BlockSpec((SUBLANE, 512), lambda s: (jnp.minimum((rev(s) + 1) * hb, lp // SUBLANE - 1), 0))
    const = lambda shape: pl.BlockSpec(shape, lambda s: (0, 0))
    return pl.pallas_call(
        body, name="bwd_in", grid=(nt,),
        in_specs=x_specs() + [const((LANE, D_MODEL)), const((1, D_MODEL)),
                              pl.BlockSpec((D_IN_PAD, D_MODEL), lambda s: (0, 0), pipeline_mode=pl.Buffered(1)),
                              const((1, LANE)), row_blk(LANE),
                              pl.BlockSpec((HEADS, t), lambda s: (0, rev(s))),
                              row_blk(512), row_blk(512), row_blk(512), row_blk(512), row_blk(512), row_blk(512),
                              row_blk(512), halo_next, rest_blk(2), rest_blk(3), halo_prev(2), halo_prev(3),
                              row_blk(D_MODEL), const((SUBLANE, D_CONV)), const((t, t))],
        out_specs=(row_blk(D_IN_PAD), ANY, const((FRONT, D_MODEL)), const((1, D_MODEL)), const((1, LANE)),
                   const((SUBLANE, D_CONV))),
        out_shape=(jax.ShapeDtypeStruct((lp, D_IN_PAD), MXU_DTYPE), jax.ShapeDtypeStruct((lp - FRONT, D_MODEL), F32),
                   jax.ShapeDtypeStruct((FRONT, D_MODEL), F32),
                   jax.ShapeDtypeStruct((1, D_MODEL), F32), jax.ShapeDtypeStruct((1, LANE), F32),
                   jax.ShapeDtypeStruct((SUBLANE, D_CONV), F32)),
        scratch_shapes=[pltpu.VMEM((1, LANE), F32), pltpu.VMEM((2, t, D_MODEL), F32), pltpu.SemaphoreType.DMA((2,))],
        compiler_params=_params(("arbitrary",)),
    )(*([x2] * n_sub), meta_blk, norm_g, w_pad, bf_pad, fl, dc, dq, dk, dv, dza, dgb, dzc, dconv, dconv,
      rest, rest, rest, rest, d_out, conv_w8, _triangle(t, lower=False))


def _grad_w_in(u, dproj):
    lp = u.shape[0]
    tn = GW_COL_TILE
    tk = tn if lp % tn == 0 else ROW_TILE

    def body(d_ref, u_ref, o_ref, wire_ref):
        k = pl.program_id(1)

        @pl.when(k == 0)
        def _():
            o_ref[...] = jnp.zeros_like(o_ref)

        o_ref[...] = o_ref[...] + lax.dot_general(d_ref[...], u_ref[...], (((0,), (0,)), ((), ())),
                                                  preferred_element_type=F32)

        @pl.when(k == pl.num_programs(1) - 1)
        def _():
            wire_ref[...] = o_ref[...].astype(wire_ref.dtype)

    out_spec = pl.BlockSpec((tn, D_MODEL), lambda n, k: (n, 0))
    return pl.pallas_call(
        body, name="grad_w_in", grid=(D_IN_PAD // tn, lp // tk),
        in_specs=[pl.BlockSpec((tk, tn), lambda n, k: (k, n)), pl.BlockSpec((tk, D_MODEL), lambda n, k: (k, 0))],
        out_specs=(out_spec, out_spec),
        out_shape=(jax.ShapeDtypeStruct((D_IN_PAD, D_MODEL), F32), jax.ShapeDtypeStruct((D_IN_PAD, D_MODEL), WIRE_DTYPE)),
        compiler_params=_params(("parallel", "arbitrary")),
    )(dproj, u)


def _by_chip(own, others, me):
    by_mask = jnp.stack([own, others[1], others[0], others[2]])
    return [lax.dynamic_index_in_dim(by_mask, jnp.bitwise_xor(me, s), 0, keepdims=False) for s in range(N_CHIPS)]


def _both_halves(mine, other, c):
    return jnp.where(c == 0, jnp.concatenate([mine, other], axis=0), jnp.concatenate([other, mine], axis=0))


def _local_step(x2, tgt2, meta_full, norm_g, w_pad, b_f, conv_w_full, attn_g, conv_g, w_out_full, final_g):
    lp = x2.shape[0] + FRONT
    nt = lp // ROW_TILE
    meta_blk = jnp.concatenate([jnp.zeros((PAD_ROWS, D_MODEL), F32), meta_full], axis=0)
    bf_pad = jnp.pad(b_f, ((0, 0), (0, LANE - HEADS)))
    conv_w8 = jnp.pad(conv_w_full, ((0, SUBLANE - conv_w_full.shape[0]), (0, 0)))
    q, k, v, rest, fl, ct, u, q_t, k_t, v_t, cc = _in_proj(x2, meta_blk, norm_g, w_pad, bf_pad)
    ct4 = ct.reshape(SUBLANE, nt, 1, ROW_TILE)
    o, l_sum, m_max = _attn_fwd(q, k, v_t, cc)
    (d_out, d_o, do_t, neg_delta, dza, dgb, dzc, dconv, loss, g_final, g_attn, g_convg, gw_out) = _post(
        o, l_sum, rest, x2, meta_blk, tgt2, w_out_full, attn_g, conv_g, final_g, conv_w8)
    dq, dk, dv, dc = _attn_bwd(q, k, v, d_o, q_t, k_t, do_t, m_max, neg_delta, ct4)
    dproj, grad_x, d_front, g_norm, g_bf, g_cw = _bwd_in(x2, meta_blk, norm_g, w_pad, bf_pad, fl, dc.reshape(HEADS, lp), dq, dk, dv,
                                             dza, dgb, dzc, dconv, rest, d_out, conv_w8)
    gw_in, gw_in_wire = _grad_w_in(u, dproj)
    return dict(loss=loss, grad_x=grad_x, d_front=d_front, g_norm=g_norm, g_final=g_final, g_attn=g_attn, g_convg=g_convg, g_bf=g_bf,
                g_cw=g_cw, gw_out=gw_out, gw_in=gw_in, gw_in_wire=gw_in_wire)


def kernel(x, meta, norm_g, w_in, b_f, conv_w, attn_norm_g, conv_norm_g, w_out, final_norm_g, loss_target, m_meta, m_norm_g, m_w_in, m_b_f, m_conv_w, m_attn_norm_g, m_conv_norm_g, m_w_out, m_final_norm_g, v_meta, v_norm_g, v_w_in, v_b_f, v_conv_w, v_attn_norm_g, v_conv_norm_g, v_w_out, v_final_norm_g):
    cx_, cy_, cc_ = _position()
    chip = 2 * cx_ + cy_
    shard = w_in.shape[2]
    out_half = w_out.shape[1] // 2
    pick = lambda vals: jnp.where(chip == 0, vals[0], jnp.where(chip == 1, vals[1], jnp.where(chip == 2, vals[2], vals[3])))
    a_off, b_off = pick(A_OFF), pick(B_OFF)
    wt = jnp.transpose(w_in[0]).astype(MXU_DTYPE)
    wi = lax.dynamic_update_slice_in_dim(
        lax.dynamic_update_slice_in_dim(jnp.zeros((WIN_ROWS, D_MODEL), MXU_DTYPE), wt[:PIECE_A], a_off, 0),
        wt[PIECE_A:], b_off, 0)
    wo = w_out[0].astype(MXU_DTYPE)
    small = jnp.concatenate([meta, jnp.pad(conv_w[0], ((0, 8 - conv_w.shape[1]), (0, meta.shape[1] - conv_w.shape[2])))],
                            axis=0)
    gwi, gwo, gsm = _gather_weights(wi.reshape(2, WIN_HALF, D_MODEL), wo.reshape(2, out_half, D_MODEL), small)
    starts = jnp.stack([_window_start(jnp.bitwise_xor(chip, mask)) for mask in (0, 2, 1, 3)]).astype(jnp.int32)
    w_pad = _assemble_w(wi, gwi.reshape(3, WIN_ROWS, D_MODEL), starts)
    w_out_full = jnp.concatenate(_by_chip(wo, gwo.reshape(3, 2 * out_half, D_MODEL), chip), axis=0)
    small_full = jnp.concatenate(_by_chip(small, gsm, chip), axis=1)
    meta_full = small_full[:N_META]
    conv_w_full = jnp.concatenate([small_full[N_META:N_META + 3, 256 * s:256 * s + LANE] for s in range(N_CHIPS)], axis=1)
    final_g2 = final_norm_g.reshape(1, D_MODEL)
    r = _local_step(x[0], loss_target[0], meta_full, norm_g, w_pad, b_f, conv_w_full, attn_norm_g, conv_norm_g,
                    w_out_full, final_g2)
    grad_x = r["grad_x"][None]
    gb = r["gw_out"].reshape(N_CHIPS, 2, out_half, D_MODEL)
    wide = lambda a: jnp.pad(a, ((0, 0), (0, D_MODEL - a.shape[1])))
    pack = jnp.concatenate([
        r["g_norm"], r["g_final"], jnp.concatenate([r["g_attn"], r["g_convg"]], axis=1), wide(r["g_bf"]),
        wide(r["loss"]), jnp.zeros((3, D_MODEL), F32), r["d_front"][PAD_ROWS:], wide(r["g_cw"])], axis=0)
    ra, rb, packs = _pair_exchange(r["gw_in_wire"], gb, pack)
    c_idx = jnp.reshape(cc_, (1,)).astype(jnp.int32)
    chip_idx = jnp.reshape(chip, (1,)).astype(jnp.int32)
    pa, pa_wire = _pair_sum_windows(r["gw_in"], ra, c_idx)
    pb, pb_wire = _pair_sum(gb, rb, c_idx)
    xa, xb = _chip_exchange(pa_wire, pb_wire)
    ha = _chip_sum(pa, xa, chip_idx)
    hb = _chip_sum(pb, xb, chip_idx)
    oa, ob = _pair_share(ha, hb)
    g_window = _both_halves(ha, oa, cc_)
    g_w_in_t = jnp.concatenate([lax.dynamic_slice_in_dim(g_window, a_off, PIECE_A, 0),
                                lax.dynamic_slice_in_dim(g_window, b_off, shard - PIECE_A, 0)], axis=0)
    g_w_out = _both_halves(hb, ob, cc_)
    as_rows = lambda a: jnp.transpose(a, (2, 0, 1))
    g_w_in, d_w_in, nm_w_in, nv_w_in = (jnp.transpose(a, (1, 2, 0)) for a in _adamw_rows(
        as_rows(w_in), g_w_in_t, as_rows(m_w_in), as_rows(v_w_in)))
    d_w_out, nm_w_out, nv_w_out = (a[None] for a in _adamw_big(w_out[0], g_w_out, m_w_out[0], v_w_out[0], LANE))
    params = (norm_g, final_g2, attn_norm_g, conv_norm_g, b_f, meta, conv_w[0])
    ms = (m_norm_g, m_final_norm_g.reshape(1, D_MODEL), m_attn_norm_g, m_conv_norm_g, m_b_f, m_meta, m_conv_w[0])
    vs = (v_norm_g, v_final_norm_g.reshape(1, D_MODEL), v_attn_norm_g, v_conv_norm_g, v_b_f, v_meta, v_conv_w[0])
    loss, g_s, d_s, m_s, v_s = _small_update(pack, packs, params, ms, vs)

    def ordered(small_list, big_in, big_out):
        s_norm, s_final, s_attn, s_convg, s_bf, s_meta, s_cw = small_list
        return (s_meta, s_norm, big_in, s_bf, s_cw[None], s_attn, s_convg, big_out, s_final.reshape(D_MODEL))

    return (loss.reshape(()), grad_x,
            *ordered(g_s, g_w_in, g_w_out[None]), *ordered(d_s, d_w_in, d_w_out),
            *ordered(m_s, nm_w_in, nm_w_out), *ordered(v_s, nv_w_in, nv_w_out))
```

```python
import functools

import jax
import jax.numpy as jnp
from jax import lax
from jax.experimental import pallas as pl
from jax.experimental.pallas import tpu as pltpu

F32 = jnp.float32
MXU_DTYPE = jnp.bfloat16
WIRE_DTYPE = jnp.bfloat16

D_MODEL = 1024
N_META = 16
HEADS = 8
HEAD_DIM = 64
D_ATTN = HEADS * HEAD_DIM
D_CONV = 512
EPS = 1e-6
LANE = 128
SUBLANE = 8
ROW_TILE = 384
ATTN_UNROLL = 3
ATTN_BWD_QBLOCKS = 2
DELTA_TERMS = 3
STAT_TERMS = 1
FRONT = LANE
PAD_ROWS = FRONT - N_META
NEG = -1e30
LOG2E = 1.4426950408889634
N_CHIPS = 4
N_DEV = 8
VMEM_LIMIT_BYTES = 60 * 1024 * 1024

SEG_Q, SEG_K, SEG_V, SEG_F, SEG_ZA, SEG_GB, SEG_GC, SEG_XC, SEG_ZC = (
    0, 512, 1024, 1536, 1664, 2176, 2688, 3200, 3712)
D_IN = 4104
D_IN_PAD = 4224
F_END = 1544
GW_COL_TILE = 1408
WIN_ROWS = 1152
WIN_HALF = WIN_ROWS // 2
WIN_START = (0, 1024, 2160, 3072)
PIECE_A = 518
A_OFF = (0, 2, 12, 126)
B_OFF = (518, 640, 530, 644)
ADAM_LR = 0.001
ADAM_B1 = 0.9
ADAM_B2 = 0.999
ADAM_EPS = 1e-08
ADAM_WD = 0.01
ADAM_STEP = 10

MESH = pl.DeviceIdType.MESH
ANY = pl.BlockSpec(memory_space=pl.ANY)

PACK_ROWS = 32
SLOT_NORM = (0, 1, 0, 1024)
SLOT_FINAL = (1, 2, 0, 1024)
SLOT_ATTN = (2, 3, 0, 512)
SLOT_CONVG = (2, 3, 512, 1024)
SLOT_BF = (3, 4, 0, 8)
SLOT_META = (8, 24, 0, 256)
SLOT_CONVW = (24, 27, 0, 128)
LOSS_ROW = 4


def _params(sem=None):
    return pltpu.CompilerParams(dimension_semantics=sem, vmem_limit_bytes=VMEM_LIMIT_BYTES)


def _sigmoid(z):
    return 1.0 / (1.0 + jnp.exp(-z))


def _dot(a, b):
    return jnp.dot(a, b, preferred_element_type=F32)


def _dot_nt(a, b):
    return lax.dot_general(a, b, (((1,), (1,)), ((), ())), preferred_element_type=F32)


def _dot_exact(ones, x):
    ones = ones.astype(MXU_DTYPE)
    total = None
    for _ in range(3):
        term = x.astype(MXU_DTYPE)
        x = x - term.astype(F32)
        total = _dot(ones, term) if total is None else total + _dot(ones, term)
    return total


def _group_matrix():
    r = lax.broadcasted_iota(jnp.int32, (D_ATTN, D_ATTN), 0) >> 6
    c = lax.broadcasted_iota(jnp.int32, (D_ATTN, D_ATTN), 1) >> 6
    return jnp.where(r == c, 1.0, 0.0).astype(MXU_DTYPE)


def _triangle(n, lower):
    r = lax.broadcasted_iota(jnp.int32, (n, n), 0)
    c = lax.broadcasted_iota(jnp.int32, (n, n), 1)
    return jnp.where((r >= c) if lower else (c >= r), 1.0, 0.0).astype(MXU_DTYPE)


def _group_sum(x, gmat, terms=2):
    hi = x.astype(MXU_DTYPE)
    if terms == 1:
        return _dot(hi, gmat)
    lo = (x - hi.astype(F32)).astype(MXU_DTYPE)
    return _dot(hi, gmat) + _dot(lo, gmat)


def _x_block_specs(n_sub, rows):
    specs = [pl.BlockSpec((rows, D_MODEL), lambda i: (jnp.maximum(n_sub * i - 1, 0), 0))]
    for b in range(1, n_sub):
        specs.append(pl.BlockSpec((rows, D_MODEL), functools.partial(lambda i, b: (n_sub * i - 1 + b, 0), b=b)))
    return specs


def _position():
    return lax.axis_index("x"), lax.axis_index("y"), lax.axis_index("c")


def _gather_weights(wi, wo, small):
    def body(wi_ref, wo_ref, sm_ref, gwi_ref, gwo_ref, gsm_ref, send_sems, recv_sems):
        x, y, c = _position()
        sibling = (x, y, 1 - c)
        chips = [(1 - x, y), (x, 1 - y), (1 - x, 1 - y)]

        def remote(k, src, dst, to):
            return pltpu.make_async_remote_copy(src_ref=src, dst_ref=dst, send_sem=send_sems.at[k],
                                                recv_sem=recv_sems.at[k], device_id=to, device_id_type=MESH)

        first, passed, landed = [], [], []
        for a, (src_ref, g_ref) in enumerate(((wi_ref, gwi_ref), (wo_ref, gwo_ref))):
            for j, (cx, cy) in enumerate(chips):
                slot = g_ref.at[j, c]
                first.append(remote(6 * a + j, src_ref.at[c], slot, (cx, cy, c)))
                landed.append(remote(6 * a + j, slot, slot, sibling))
                passed.append(remote(6 * a + 3 + j, slot, slot, sibling))
        for j, (cx, cy) in enumerate(chips):
            first.append(remote(12 + j, sm_ref, gsm_ref.at[j], (cx, cy, c)))
        for cp in first:
            cp.start()
        for arrived, onward in zip(landed, passed):
            arrived.wait_recv()
            onward.start()
        for a, g_ref in enumerate((gwi_ref, gwo_ref)):
            for j in range(3):
                remote(6 * a + 3 + j, g_ref.at[j, 1 - c], g_ref.at[j, 1 - c], sibling).wait_recv()
        for j in range(3):
            remote(12 + j, sm_ref, gsm_ref.at[j], sibling).wait_recv()
        for cp in first + passed:
            cp.wait_send()

    return pl.pallas_call(
        body, name="gather_weights",
        out_shape=(jax.ShapeDtypeStruct((3,) + wi.shape, wi.dtype), jax.ShapeDtypeStruct((3,) + wo.shape, wo.dtype),
                   jax.ShapeDtypeStruct((3,) + small.shape, small.dtype)),
        in_specs=[ANY, ANY, ANY], out_specs=(ANY, ANY, ANY),
        scratch_shapes=[pltpu.SemaphoreType.DMA((15,)), pltpu.SemaphoreType.DMA((15,))],
    )(wi, wo, small)


def _pair_exchange(gw, gb, pack):
    n_big = N_CHIPS + 1

    def body(gw_ref, gb_ref, p_ref, ra_ref, rb_ref, o_ref, send_sems, recv_sems):
        x, y, c = _position()
        sibling = (x, y, 1 - c)

        def remote(k, src, dst, to):
            return pltpu.make_async_remote_copy(src_ref=src, dst_ref=dst, send_sem=send_sems.at[k],
                                                recv_sem=recv_sems.at[k], device_id=to, device_id_type=MESH)

        copies = [remote(N_CHIPS, gb_ref.at[:, 1 - c], rb_ref, sibling)]
        for s, start in enumerate(WIN_START):
            rows = pl.ds(pl.multiple_of(start + WIN_HALF * (1 - c), 2 * SUBLANE), WIN_HALF)
            copies.append(remote(s, gw_ref.at[rows], ra_ref.at[s], sibling))
        for mask in range(1, N_DEV):
            peer = (1 - x if mask & 4 else x, 1 - y if mask & 2 else y, 1 - c if mask & 1 else c)
            copies.append(remote(n_big + mask - 1, p_ref, o_ref.at[mask - 1], peer))
        for cp in copies:
            cp.start()
        for cp in copies:
            cp.wait()

    n_sems = n_big + N_DEV - 1
    return pl.pallas_call(
        body, name="grad_pair_exchange",
        out_shape=(jax.ShapeDtypeStruct((N_CHIPS, WIN_HALF, D_MODEL), gw.dtype),
                   jax.ShapeDtypeStruct((N_CHIPS,) + gb.shape[2:], gb.dtype),
                   jax.ShapeDtypeStruct((N_DEV - 1,) + pack.shape, pack.dtype)),
        in_specs=[ANY, ANY, ANY], out_specs=(ANY, ANY, ANY),
        scratch_shapes=[pltpu.SemaphoreType.DMA((n_sems,)), pltpu.SemaphoreType.DMA((n_sems,))],
    )(gw, gb, pack)


def _chip_exchange(pa, pb):
    def body(pa_ref, pb_ref, ra_ref, rb_ref, send_sems, recv_sems):
        x, y, c = _position()
        chips = [(1 - x, y), (x, 1 - y), (1 - x, 1 - y)]
        copies = []
        for a, (src, dst) in enumerate(((pa_ref, ra_ref), (pb_ref, rb_ref))):
            for j, (cx, cy) in enumerate(chips):
                copies.append(pltpu.make_async_remote_copy(
                    src_ref=src.at[2 * cx + cy], dst_ref=dst.at[j], send_sem=send_sems.at[3 * a + j],
                    recv_sem=recv_sems.at[3 * a + j], device_id=(cx, cy, c), device_id_type=MESH))
        for cp in copies:
            cp.start()
        for cp in copies:
            cp.wait()

    return pl.pallas_call(
        body, name="grad_chip_exchange",
        out_shape=(jax.ShapeDtypeStruct((3,) + pa.shape[1:], pa.dtype),
                   jax.ShapeDtypeStruct((3,) + pb.shape[1:], pb.dtype)),
        in_specs=[ANY, ANY], out_specs=(ANY, ANY),
        scratch_shapes=[pltpu.SemaphoreType.DMA((6,)), pltpu.SemaphoreType.DMA((6,))],
    )(pa, pb)


def _pair_share(ha, hb):
    def body(ha_ref, hb_ref, oa_ref, ob_ref, send_sems, recv_sems):
        x, y, c = _position()
        copies = [pltpu.make_async_remote_copy(
            src_ref=src, dst_ref=dst, send_sem=send_sems.at[k], recv_sem=recv_sems.at[k],
            device_id=(x, y, 1 - c), device_id_type=MESH)
            for k, (src, dst) in enumerate(((ha_ref, oa_ref), (hb_ref, ob_ref)))]
        for cp in copies:
            cp.start()
        for cp in copies:
            cp.wait()

    return pl.pallas_call(
        body, name="grad_pair_share",
        out_shape=(jax.ShapeDtypeStruct(ha.shape, ha.dtype), jax.ShapeDtypeStruct(hb.shape, hb.dtype)),
        in_specs=[ANY, ANY], out_specs=(ANY, ANY),
        scratch_shapes=[pltpu.SemaphoreType.DMA((2,)), pltpu.SemaphoreType.DMA((2,))],
    )(ha, hb)


def _pair_sum(mine, recv, c_idx):
    rows, cols = mine.shape[2:]

    def body(c_ref, a_ref, b_ref, o_ref, send_ref):
        total = a_ref[...] + b_ref[...]
        o_ref[...] = total
        send_ref[...] = total.astype(send_ref.dtype)

    out_spec = pl.BlockSpec((None, rows, cols), lambda s, c_ref: (s, 0, 0))
    return pl.pallas_call(
        body, name="grad_pair_sum",
        grid_spec=pltpu.PrefetchScalarGridSpec(
            num_scalar_prefetch=1, grid=(N_CHIPS,),
            in_specs=[pl.BlockSpec((None, None, rows, cols), lambda s, c_ref: (s, c_ref[0], 0, 0)),
                      pl.BlockSpec((None, rows, cols), lambda s, c_ref: (s, 0, 0))],
            out_specs=(out_spec, out_spec)),
        out_shape=(jax.ShapeDtypeStruct(recv.shape, recv.dtype), jax.ShapeDtypeStruct(recv.shape, WIRE_DTYPE)),
        compiler_params=_params(("parallel",)),
    )(c_idx, mine, recv)


def _window_start(s):
    return jnp.where(s == 0, WIN_START[0], jnp.where(s == 1, WIN_START[1], jnp.where(s == 2, WIN_START[2], WIN_START[3])))


def _pair_sum_windows(gw, recv, c_idx):
    tr = WIN_HALF // 3

    def body(c_ref, a_ref, b_ref, o_ref, send_ref):
        total = a_ref[...] + b_ref[...].astype(F32)
        o_ref[...] = total
        send_ref[...] = total.astype(send_ref.dtype)

    out_spec = pl.BlockSpec((None, tr, D_MODEL), lambda s, i, c_ref: (s, i, 0))
    return pl.pallas_call(
        body, name="grad_pair_sum_windows",
        grid_spec=pltpu.PrefetchScalarGridSpec(
            num_scalar_prefetch=1, grid=(N_CHIPS, WIN_HALF // tr),
            in_specs=[pl.BlockSpec((pl.Element(tr), pl.Element(D_MODEL)),
                                   lambda s, i, c_ref: (pl.multiple_of(
                                       _window_start(s) + WIN_HALF * c_ref[0] + tr * i, SUBLANE), 0)),
                      pl.BlockSpec((None, tr, D_MODEL), lambda s, i, c_ref: (s, i, 0))],
            out_specs=(out_spec, out_spec)),
        out_shape=(jax.ShapeDtypeStruct(recv.shape, F32), jax.ShapeDtypeStruct(recv.shape, WIRE_DTYPE)),
        compiler_params=_params(("parallel", "parallel")),
    )(c_idx, gw, recv)


def _assemble_w(own, others, starts):
    def body(starts_ref, own_ref, oth_ref, o_ref):
        o_ref[...] = jnp.zeros_like(o_ref)
        for k in range(N_CHIPS):
            rows = pl.ds(pl.multiple_of(starts_ref[k], 2 * SUBLANE), WIN_ROWS)
            o_ref[rows, :] = o_ref[rows, :] + (own_ref[...] if k == 0 else oth_ref[k - 1])

    return pl.pallas_call(
        body, name="assemble_w",
        in_specs=[pl.BlockSpec(memory_space=pltpu.SMEM), pl.BlockSpec(memory_space=pltpu.VMEM),
                  pl.BlockSpec(memory_space=pltpu.VMEM)],
        out_specs=pl.BlockSpec(memory_space=pltpu.VMEM),
        out_shape=jax.ShapeDtypeStruct((D_IN_PAD, D_MODEL), own.dtype),
        compiler_params=_params(),
    )(starts, own, others)


def _chip_sum(psum, recv3, chip_idx):
    rows, cols = psum.shape[1:]
    tr = rows // 2

    def body(s_ref, p_ref, r0, r1, r2, o_ref):
        o_ref[...] = ((p_ref[...] + r0[...].astype(F32)) + r1[...].astype(F32)) + r2[...].astype(F32)

    return pl.pallas_call(
        body, name="grad_chip_sum",
        grid_spec=pltpu.PrefetchScalarGridSpec(
            num_scalar_prefetch=1, grid=(2,),
            in_specs=[pl.BlockSpec((None, tr, cols), lambda i, s_ref: (s_ref[0], i, 0))] +
                     [pl.BlockSpec((None, tr, cols), functools.partial(lambda i, s_ref, j: (j, i, 0), j=j))
                      for j in range(3)],
            out_specs=pl.BlockSpec((tr, cols), lambda i, s_ref: (i, 0))),
        out_shape=jax.ShapeDtypeStruct((rows, cols), psum.dtype),
        compiler_params=_params(("parallel",)),
    )(chip_idx, psum, recv3, recv3, recv3)


def _adamw_math(w, g, m, v):
    m = ADAM_B1 * m + (1.0 - ADAM_B1) * g
    v = ADAM_B2 * v + (1.0 - ADAM_B2) * (g * g)
    m_hat = m * (1.0 / (1.0 - ADAM_B1 ** ADAM_STEP))
    v_hat = v * (1.0 / (1.0 - ADAM_B2 ** ADAM_STEP))
    delta = -ADAM_LR * (m_hat / (jnp.sqrt(v_hat) + ADAM_EPS) + ADAM_WD * w)
    return delta, m, v


def _adamw_big(w, g, m, v, tr):
    rows, cols = w.shape
    assert rows % tr == 0 and g.shape[0] >= rows

    def body(w_ref, g_ref, m_ref, v_ref, d_out, m_out, v_out):
        d, m2, v2 = _adamw_math(w_ref[...], g_ref[...], m_ref[...], v_ref[...])
        d_out[...] = d
        m_out[...] = m2
        v_out[...] = v2

    spec = pl.BlockSpec((tr, cols), lambda i: (i, 0))
    sds = jax.ShapeDtypeStruct((rows, cols), F32)
    return pl.pallas_call(
        body, name="adamw_big", grid=(rows // tr,), in_specs=[spec] * 4, out_specs=(spec,) * 3,
        out_shape=(sds,) * 3, compiler_params=_params(("parallel",)),
    )(w, g, m, v)


def _adamw_rows(w3, g, m3, v3):
    rows, _, cols = w3.shape
    tc = 2 * LANE

    def body(w_ref, g_ref, m_ref, v_ref, g_out, d_out, m_out, v_out):
        g = g_ref[...]
        d, m2, v2 = _adamw_math(w_ref[:, 0, :], g, m_ref[:, 0, :], v_ref[:, 0, :])
        g_out[:, 0, :] = g
        d_out[:, 0, :] = d
        m_out[:, 0, :] = m2
        v_out[:, 0, :] = v2

    spec3 = pl.BlockSpec((rows, 1, tc), lambda i: (0, 0, i))
    sds = jax.ShapeDtypeStruct((rows, 1, cols), F32)
    return pl.pallas_call(
        body, name="adamw_rows", grid=(cols // tc,),
        in_specs=[spec3, pl.BlockSpec((rows, tc), lambda i: (0, i)), spec3, spec3], out_specs=(spec3,) * 4,
        out_shape=(sds,) * 4, compiler_params=_params(("parallel",)),
    )(w3, g, m3, v3)


def _small_update(own, others, params, ms, vs):
    slots = (SLOT_NORM, SLOT_FINAL, SLOT_ATTN, SLOT_CONVG, SLOT_BF, SLOT_META, SLOT_CONVW)
    n = len(slots)

    def body(*refs):
        own_ref, gp_ref = refs[:2]
        w_refs, m_refs, v_refs = refs[2:2 + n], refs[2 + n:2 + 2 * n], refs[2 + 2 * n:2 + 3 * n]
        outs = refs[2 + 3 * n:3 + 7 * n]
        loss_ref = outs[0]
        g_outs, d_outs, m_outs, v_outs = (outs[1 + k * n:1 + (k + 1) * n] for k in range(4))
        g_scr, w_scr, m_scr, v_scr = refs[3 + 7 * n:]
        x, y, c = _position()
        shard = 2 * x + y
        me = 4 * x + 2 * y + c
        tot = None
        for d in range(N_DEV):
            rel = jnp.bitwise_xor(me, d)
            term = jnp.where(rel == 0, own_ref[...], gp_ref[jnp.maximum(rel, 1) - 1])
            tot = term if tot is None else tot + term
        r0, r1, _, _ = SLOT_META
        meta_sel = tot[r0:r1, 0:256]
        cw_sel = tot[24:32, 0:128]
        for k in range(1, N_CHIPS):
            meta_sel = jnp.where(shard == k, tot[r0:r1, 256 * k:256 * (k + 1)], meta_sel)
            cw_sel = jnp.where(shard == k, tot[24:32, 128 * k:128 * (k + 1)], cw_sel)
        zeros = jnp.zeros((PACK_ROWS, D_MODEL), F32)
        for scr in (g_scr, w_scr, m_scr, v_scr):
            scr[...] = zeros
        g_scr[0:8, :] = tot[0:8, :]
        g_scr[r0:r1, 0:256] = meta_sel
        g_scr[24:32, 0:128] = cw_sel
        for (a, b, c0, c1), w_ref, m_ref, v_ref in zip(slots, w_refs, m_refs, v_refs):
            w_scr[a:b, c0:c1] = w_ref[...]
            m_scr[a:b, c0:c1] = m_ref[...]
            v_scr[a:b, c0:c1] = v_ref[...]
        loss_ref[...] = g_scr[LOSS_ROW:LOSS_ROW + 1, 0:1]
        d, m2, v2 = _adamw_math(w_scr[...], g_scr[...], m_scr[...], v_scr[...])
        w_scr[...] = d
        m_scr[...] = m2
        v_scr[...] = v2
        for (a, b, c0, c1), g_o, d_o, m_o, v_o in zip(slots, g_outs, d_outs, m_outs, v_outs):
            g_o[...] = g_scr[a:b, c0:c1]
            d_o[...] = w_scr[a:b, c0:c1]
            m_o[...] = m_scr[a:b, c0:c1]
            v_o[...] = v_scr[a:b, c0:c1]

    shapes = [jax.ShapeDtypeStruct(p.shape, F32) for p in params]
    out = pl.pallas_call(
        body, name="small_update",
        out_shape=[jax.ShapeDtypeStruct((1, 1), F32)] + shapes * 4,
        scratch_shapes=[pltpu.VMEM((PACK_ROWS, D_MODEL), F32)] * 4,
        compiler_params=_params(),
    )(own, others, *params, *ms, *vs)
    return out[0], out[1:1 + n], out[1 + n:1 + 2 * n], out[1 + 2 * n:1 + 3 * n], out[1 + 3 * n:1 + 4 * n]


def _in_proj(x2, meta_blk, norm_g, w_pad, bf_pad):
    seq = x2.shape[0]
    lp = seq + FRONT
    t = ROW_TILE
    nt = lp // t
    n_sub = t // LANE

    def body(*refs):
        x_refs = refs[:n_sub]
        mb, g_ref, w_ref, bf_ref, tri_ref = refs[n_sub:n_sub + 5]
        q_ref, k_ref, v_ref, rest_ref, fl_ref, ct_ref, u_ref, qt_ref, kt_ref, vt_ref, cc_ref, carry = refs[n_sub + 5:]
        i = pl.program_id(0)

        @pl.when(i == 0)
        def _():
            carry[...] = jnp.zeros_like(carry)

        first = jnp.where(i == 0, mb[...], x_refs[0][...])
        h = jnp.concatenate([first] + [r[...] for r in x_refs[1:]], axis=0)
        ms = jnp.mean(h * h, axis=-1, keepdims=True)
        u = ((h * lax.rsqrt(ms + EPS)) * g_ref[...]).astype(MXU_DTYPE)
        u_ref[...] = u

        def seg(a, width):
            return _dot_nt(u, w_ref[a:a + width, :])

        fl = seg(SEG_F, LANE)
        fl_ref[...] = fl
        q_tile = seg(SEG_Q, D_ATTN) * (HEAD_DIM ** -0.5)
        q_ref[...] = q_tile.astype(MXU_DTYPE)
        qt_ref[...] = q_tile.T.astype(MXU_DTYPE)
        z = fl + bf_ref[...]
        logf = jnp.minimum(z, 0.0) - jnp.log(1.0 + jnp.exp(-jnp.abs(z)))
        row = i * t + lax.broadcasted_iota(jnp.int32, (t, LANE), 0)
        logf = jnp.where(row >= PAD_ROWS, logf, 0.0)
        k_tile = seg(SEG_K, D_ATTN)
        k_ref[...] = k_tile.astype(MXU_DTYPE)
        kt_ref[...] = k_tile.T.astype(MXU_DTYPE)
        cs = _dot_exact(tri_ref[...], logf) + carry[...]
        carry[...] = carry[...] + jnp.sum(logf, axis=0, keepdims=True)
        v_tile = seg(SEG_V, D_ATTN)
        v_ref[...] = v_tile.astype(MXU_DTYPE)
        vt_ref[...] = v_tile.T.astype(MXU_DTYPE)
        col = i * t + lax.broadcasted_iota(jnp.int32, (SUBLANE, t), 1)
        ct_ref[...] = jnp.where(col >= PAD_ROWS, cs.T[0:SUBLANE, :], -NEG)
        cc_ref[...] = jnp.where(row >= PAD_ROWS, cs, -NEG)
        for s in range(5):
            rest_ref[:, 512 * s:512 * (s + 1)] = seg(SEG_ZA + 512 * s, 512)

    row_blk = lambda cols: pl.BlockSpec((t, cols), lambda i: (i, 0))
    tr_blk = pl.BlockSpec((None, D_ATTN, t), lambda i: (i, 0, 0))
    const = lambda shape: pl.BlockSpec(shape, lambda i: (0, 0))
    return pl.pallas_call(
        body, name="in_proj", grid=(nt,),
        in_specs=_x_block_specs(n_sub, LANE) + [const((LANE, D_MODEL)), const((1, D_MODEL)),
                                                pl.BlockSpec((D_IN_PAD, D_MODEL), lambda i: (0, 0),
                                                             pipeline_mode=pl.Buffered(1)),
                                                const((1, LANE)), const((t, t))],
        out_specs=(row_blk(D_ATTN), row_blk(D_ATTN), row_blk(D_ATTN), row_blk(5 * 512), row_blk(LANE),
                   pl.BlockSpec((SUBLANE, t), lambda i: (0, i)), row_blk(D_MODEL), tr_blk, tr_blk, tr_blk, row_blk(LANE)),
        out_shape=(jax.ShapeDtypeStruct((lp, D_ATTN), MXU_DTYPE), jax.ShapeDtypeStruct((lp, D_ATTN), MXU_DTYPE),
                   jax.ShapeDtypeStruct((lp, D_ATTN), MXU_DTYPE), jax.ShapeDtypeStruct((lp, 5 * 512), F32),
                   jax.ShapeDtypeStruct((lp, LANE), F32),
                   jax.ShapeDtypeStruct((SUBLANE, lp), F32), jax.ShapeDtypeStruct((lp, D_MODEL), MXU_DTYPE),
                   jax.ShapeDtypeStruct((nt, D_ATTN, t), MXU_DTYPE), jax.ShapeDtypeStruct((nt, D_ATTN, t), MXU_DTYPE),
                   jax.ShapeDtypeStruct((nt, D_ATTN, t), MXU_DTYPE), jax.ShapeDtypeStruct((lp, LANE), F32)),
        scratch_shapes=[pltpu.VMEM((1, LANE), F32)],
        compiler_params=_params(("arbitrary",)),
    )(*([x2] * n_sub), meta_blk, norm_g, w_pad, bf_pad, _triangle(t, lower=True))


def _head_masks():
    lane = lax.broadcasted_iota(jnp.int32, (1, LANE), 1)
    return lane < HEAD_DIM, lane >= HEAD_DIM


def _pair_specs(lp, nt, t):
    blk = pl.BlockSpec((lp, LANE), lambda g: (0, g))
    ct_a = pl.BlockSpec((None, nt, 1, t), lambda g: (2 * g, 0, 0, 0))
    ct_b = pl.BlockSpec((None, nt, 1, t), lambda g: (2 * g + 1, 0, 0, 0))
    return blk, ct_a, ct_b


def _sub_rows(s, col):
    return jnp.concatenate([s[:, a * LANE:(a + 1) * LANE] - col for a in range(s.shape[1] // LANE)], axis=1)


def _loop_unrolled(lo, hi, step, init, n):
    def group(jj, carry):
        for k in range(n):
            carry = step(lo + n * jj + k, carry)
        return carry

    groups = (hi - lo) // n
    carry = lax.fori_loop(0, groups, group, init)
    return lax.fori_loop(lo + n * groups, hi, step, carry)


def _attn_fwd(q, k, v_t, cc):
    lp = q.shape[0]
    t = ROW_TILE
    nt = lp // t
    ext = LANE + 2 * SUBLANE

    def body(q_ref, k_ref, vt_ref, cc_ref, o_ref, l_ref, m_ref, s_scr, last_scr, m_scr, mfin_scr, acc_scr, c_scr):
        masks = _head_masks()
        lane = lax.broadcasted_iota(jnp.int32, (1, LANE), 1)
        for hh in range(2):
            picked = jnp.where(lane == 2 * pl.program_id(0) + hh, cc_ref[...], 0.0)
            c_scr[hh] = jnp.broadcast_to(jnp.sum(picked, axis=-1, keepdims=True), (lp, LANE))
        visible = lax.broadcasted_iota(jnp.int32, (t, t), 0) <= lax.broadcasted_iota(jnp.int32, (t, t), 1)
        top = lax.broadcasted_iota(jnp.int32, (LANE, 1), 0) < HEAD_DIM
        second_head = (lax.broadcasted_iota(jnp.int32, (2 * SUBLANE, 2 * t), 1) >= t).astype(jnp.int32)
        ones_rows = jnp.where(lax.broadcasted_iota(jnp.int32, (2 * SUBLANE, 2 * t), 0) == second_head,
                              1.0, 0.0).astype(MXU_DTYPE)

        on_first_diagonal = jnp.concatenate([visible, jnp.ones((t, t), jnp.bool_)], axis=1)

        def scores(j, queries):
            kj = k_ref[pl.ds(pl.multiple_of(j * t, t), t), :]
            return _dot_nt(jnp.concatenate([jnp.where(hm, kj, 0).astype(MXU_DTYPE) for hm in masks], axis=0), queries)

        def biased(s2, j, hh):
            return _sub_rows(s2[hh * t:(hh + 1) * t, :], c_scr[hh, pl.ds(pl.multiple_of(j * t, t), t), :]) * LOG2E

        def track_max(hh, s, lo, hi):
            m = m_scr[hh, :, lo:hi]
            for a in range(t // SUBLANE):
                m = jnp.maximum(m, s[a * SUBLANE:(a + 1) * SUBLANE, :])
            m_scr[hh, :, lo:hi] = m

        def probabilities(scores_of, ms_cols):
            return jnp.concatenate([jnp.exp2(scores_of(hh) - ms_cols[hh]).astype(MXU_DTYPE) for hh in range(2)], axis=0)

        def values(j):
            vtj = vt_ref[j]
            v2 = jnp.concatenate([jnp.where(top, vtj, 0).astype(MXU_DTYPE),
                                  jnp.where(top, 0, vtj).astype(MXU_DTYPE)], axis=1)
            return jnp.concatenate([v2, ones_rows], axis=0)

        def stage(done, ahead):
            if ahead is not None:
                i_a, rows_a = ahead
                qa = q_ref[pl.ds(pl.multiple_of(i_a * t, t), rows_a), :]
                m_scr[...] = jnp.full(m_scr.shape, NEG, F32)

                def max_step(j, mask=None):
                    s2 = scores(j, qa)
                    for hh in range(2):
                        s = biased(s2, j, hh)
                        if mask is not None:
                            s = jnp.where(mask, s, NEG)
                        s_scr[j, hh * t:(hh + 1) * t, 0:rows_a] = s
                        track_max(hh, s, 0, rows_a)

            if done is not None:
                i_d, rows_d = done
                r0 = pl.multiple_of(i_d * t, t)
                ms = [mfin_scr[hh, 0:1, 0:rows_d] for hh in range(2)]
                acc_scr[...] = jnp.zeros(acc_scr.shape, F32)

                def key_step(j, carry):
                    p = probabilities(lambda hh: s_scr[j, hh * t:(hh + 1) * t, 0:rows_d], ms)
                    acc_scr[:, 0:rows_d] = acc_scr[:, 0:rows_d] + _dot(values(j), p)
                    if ahead is not None:
                        max_step(j)
                    return carry

                _loop_unrolled(0, i_d + 1, key_step, 0, ATTN_UNROLL)
                if rows_d == 2 * t:
                    p = probabilities(lambda hh: last_scr[hh * t:(hh + 1) * t, :], [m[:, t:] for m in ms])
                    acc_scr[:, t:rows_d] = acc_scr[:, t:rows_d] + _dot(values(i_d + 1), p)
                acc = acc_scr[:, 0:rows_d]
                l_pair = jnp.where(top, acc[LANE:LANE + 1], acc[LANE + 1:LANE + 2])
                o_ref[pl.ds(r0, rows_d), :] = (acc[:LANE] / l_pair).T
                l_ref[pl.ds(r0, rows_d), :] = l_pair.T
                for hh in range(2):
                    m_ref[pl.ds(r0, rows_d), hh * LANE:(hh + 1) * LANE] = jnp.broadcast_to(ms[hh], (LANE, rows_d)).T

            if ahead is not None:
                if done is not None:
                    max_step(i_a - 1)
                max_step(i_a, on_first_diagonal if rows_a == 2 * t else visible)
                if rows_a == 2 * t:
                    s2 = scores(i_a + 1, qa[t:])
                    for hh in range(2):
                        s = jnp.where(visible, biased(s2, i_a + 1, hh), NEG)
                        last_scr[hh * t:(hh + 1) * t, :] = s
                        track_max(hh, s, t, rows_a)
                for hh in range(2):
                    mfin_scr[hh, :, 0:rows_a] = jnp.broadcast_to(jnp.max(m_scr[hh, :, 0:rows_a], axis=0, keepdims=True),
                                                                 (SUBLANE, rows_a))

        pairs = nt // 2
        stage(None, (0, 2 * t))

        def pair_to_pair(u, _):
            stage((2 * u, 2 * t), (2 * u + 2, 2 * t))
            return 0

        lax.fori_loop(0, pairs - 1, pair_to_pair, 0)
        if nt % 2:
            stage((2 * pairs - 2, 2 * t), (nt - 1, t))
            stage((nt - 1, t), None)
        else:
            stage((2 * pairs - 2, 2 * t), None)

    blk = pl.BlockSpec((lp, LANE), lambda g: (0, g))
    return pl.pallas_call(
        body, name="attn_fwd", grid=(HEADS // 2,),
        in_specs=[blk, blk, pl.BlockSpec((nt, LANE, t), lambda g: (0, g, 0)),
                  pl.BlockSpec((lp, LANE), lambda g: (0, 0), pipeline_mode=pl.Buffered(1))],
        out_specs=(blk, blk, pl.BlockSpec((lp, 2 * LANE), lambda g: (0, g))),
        out_shape=(jax.ShapeDtypeStruct((lp, D_ATTN), F32), jax.ShapeDtypeStruct((lp, D_ATTN), F32),
                   jax.ShapeDtypeStruct((lp, HEADS * LANE), F32)),
        scratch_shapes=[pltpu.VMEM((nt, 2 * t, 2 * t), F32), pltpu.VMEM((2 * t, t), F32),
                        pltpu.VMEM((2, SUBLANE, 2 * t), F32), pltpu.VMEM((2, SUBLANE, 2 * t), F32),
                        pltpu.VMEM((ext, 2 * t), F32), pltpu.VMEM((2, lp, LANE), F32)],
        compiler_params=_params(("parallel",)),
    )(q, k, v_t, cc)


def _attn_bwd(q, k, v, do, q_t, k_t, do_t, m, neg_delta, ct4):
    lp = q.shape[0]
    t = ROW_TILE
    nt = lp // t

    def body(q_ref, k_ref, v_ref, do_ref, qt_ref, kt_ref, dot_ref, ma_ref, mb_ref, nd_ref, cta_ref, ctb_ref,
             dq_ref, dk_ref, dv_ref, dc_ref, dq_acc, dk_acc, dv_acc):
        masks = _head_masks()
        ct_refs, m_refs = (cta_ref, ctb_ref), (ma_ref, mb_ref)
        row_head = 2 * pl.program_id(0) + (lax.broadcasted_iota(jnp.int32, (2 * t, LANE), 0) >= t).astype(jnp.int32)
        col = lax.broadcasted_iota(jnp.int32, (2 * t, LANE), 1)
        delta_ones = jnp.where((col < HEADS * DELTA_TERMS) & (col % HEADS == row_head), 1.0, 0.0).astype(MXU_DTYPE)
        below = lax.broadcasted_iota(jnp.int32, (t, t), 1) <= lax.broadcasted_iota(jnp.int32, (t, t), 0)
        top = lax.broadcasted_iota(jnp.int32, (LANE, 1), 0) < HEAD_DIM
        dq_acc[...] = jnp.zeros_like(dq_acc)

        on_first_diagonal = jnp.concatenate([below, jnp.ones((t, t), jnp.bool_)], axis=0)

        def k_block(j, _, with_next=True):
            c0 = pl.multiple_of(j * t, t)
            kj = k_ref[pl.ds(c0, t), :]
            vj = v_ref[pl.ds(c0, t), :]
            k2 = jnp.concatenate([jnp.where(hm, kj, 0).astype(MXU_DTYPE) for hm in masks], axis=0)
            v2 = jnp.concatenate([jnp.where(hm, vj, 0).astype(MXU_DTYPE) for hm in masks], axis=0)
            v2 = jnp.concatenate([v2, delta_ones], axis=1)
            ck = [r[j] for r in ct_refs]
            ktj = kt_ref[j]
            k2t = jnp.concatenate([jnp.where(top, ktj, 0).astype(MXU_DTYPE), jnp.where(top, 0, ktj).astype(MXU_DTYPE)],
                                  axis=1)
            dk_acc[...] = jnp.zeros_like(dk_acc)
            dv_acc[...] = jnp.zeros_like(dv_acc)

            def q_block(i, colsums, mask=None, rows=t):
                r0 = pl.multiple_of(i * t, t)
                qi = q_ref[pl.ds(r0, rows), :]
                doi = jnp.concatenate([do_ref[pl.ds(r0, rows), :], nd_ref[pl.ds(r0, rows), :]], axis=1)
                qti = jnp.concatenate([qt_ref[i + b] for b in range(rows // t)], axis=1)
                doti = jnp.concatenate([dot_ref[i + b] for b in range(rows // t)], axis=1)
                s2 = _dot_nt(qi, k2)
                dp2 = _dot_nt(doi, v2)
                out, ps, dss = [], [], []
                for hh in range(2):
                    s = (s2[:, hh * t:(hh + 1) * t] - ck[hh]) * LOG2E
                    if mask is not None:
                        s = jnp.where(mask, s, NEG)
                    p = jnp.exp2(_sub_rows(s, m_refs[hh][pl.ds(r0, rows), :])).astype(MXU_DTYPE)
                    ds32 = p.astype(F32) * dp2[:, hh * t:(hh + 1) * t]
                    ps.append(p)
                    dss.append(ds32.astype(MXU_DTYPE))
                    out.append(colsums[hh] + jnp.sum(ds32, axis=0, keepdims=True))
                ds_cat = jnp.concatenate(dss, axis=1)
                dv_acc[...] = dv_acc[...] + _dot(doti, jnp.concatenate(ps, axis=1))
                dk_acc[...] = dk_acc[...] + _dot(qti, ds_cat)
                dq_t = _dot(k2t, ds_cat.T)
                for b in range(rows // t):
                    dq_acc[i + b] = dq_acc[i + b] + dq_t[:, b * t:(b + 1) * t]
                return tuple(out)

            nq = ATTN_BWD_QBLOCKS
            colsums = (jnp.zeros((1, t), F32), jnp.zeros((1, t), F32))
            if with_next:
                colsums = q_block(j, colsums, on_first_diagonal, nq * t)
            else:
                colsums = q_block(j, colsums, below)
            first = j + (nq if with_next else 1)
            groups = (nt - first) // nq
            colsums = lax.fori_loop(0, groups, lambda p, c: q_block(first + nq * p, c, None, nq * t), colsums)
            colsums = lax.fori_loop(first + nq * groups, nt, q_block, colsums)
            for hh in range(2):
                dc_ref[hh, j] = -colsums[hh]
            own = lambda acc: jnp.concatenate([acc[:HEAD_DIM, :t], acc[HEAD_DIM:, t:]], axis=0).T
            dk_ref[pl.ds(c0, t), :] = own(dk_acc[...]).astype(dk_ref.dtype)
            dv_ref[pl.ds(c0, t), :] = own(dv_acc[...]).astype(dv_ref.dtype)
            return 0

        lax.fori_loop(0, nt - 1, k_block, 0)
        k_block(nt - 1, 0, with_next=False)
        for i in range(nt):
            dq_ref[i * t:(i + 1) * t, :] = (dq_acc[i].T * (HEAD_DIM ** -0.5)).astype(dq_ref.dtype)

    blk, ct_a, ct_b = _pair_specs(lp, nt, t)
    rep_a = pl.BlockSpec((lp, LANE), lambda g: (0, 2 * g))
    rep_b = pl.BlockSpec((lp, LANE), lambda g: (0, 2 * g + 1))
    tr_blk = pl.BlockSpec((nt, LANE, t), lambda g: (0, g, 0))
    return pl.pallas_call(
        body, name="attn_bwd", grid=(HEADS // 2,),
        in_specs=[blk] * 4 + [tr_blk, tr_blk, tr_blk, rep_a, rep_b, pl.BlockSpec((lp, LANE), lambda g: (0, 0)), ct_a, ct_b],
        out_specs=(blk, blk, blk, pl.BlockSpec((2, nt, 1, t), lambda g: (g, 0, 0, 0))),
        out_shape=(jax.ShapeDtypeStruct((lp, D_ATTN), MXU_DTYPE),) * 3
                  + (jax.ShapeDtypeStruct((HEADS, nt, 1, t), F32),),
        scratch_shapes=[pltpu.VMEM((nt, LANE, t), F32), pltpu.VMEM((LANE, 2 * t), F32), pltpu.VMEM((LANE, 2 * t), F32)],
        compiler_params=_params(("parallel",)),
    )(q, k, v, do, q_t, k_t, do_t, m, m, neg_delta, ct4, ct4)


def _shift_down(prev8, cur, k):
    ext = jnp.concatenate([prev8, cur], axis=0)
    return pltpu.roll(ext, k, 0)[SUBLANE:, :]


def _shift_up(cur, next8, k):
    ext = jnp.concatenate([cur, next8], axis=0)
    n = ext.shape[0]
    return pltpu.roll(ext, n - k, 0)[:cur.shape[0], :]


def _post(o, l_sum, rest, x2, meta_blk, tgt2, w_out, attn_g, conv_g, final_g, conv_w8):
    lp = o.shape[0]
    t = ROW_TILE
    nt = lp // t
    n_sub = t // LANE
    hb = t // SUBLANE

    def body(*refs):
        o_ref, l_ref, za_ref, gb_ref, gc_ref, xc_ref, zc_ref, gch_ref, xch_ref = refs[:9]
        x_refs = refs[9:9 + n_sub]
        mb = refs[9 + n_sub]
        t_refs = refs[10 + n_sub:10 + 2 * n_sub]
        wo_ref, ag_ref, cg_ref, fg_ref, cw_ref, gm_ref, hr_ref = refs[10 + 2 * n_sub:17 + 2 * n_sub]
        (dout_ref, do_ref, dot_ref, dl_ref, dza_ref, dgb_ref, dzc_ref, dcv_ref,
         loss_ref, gf_ref, gag_ref, gcg_ref, gwo_ref) = refs[17 + 2 * n_sub:]
        i = pl.program_id(0)

        @pl.when(i == 0)
        def _():
            for r in (loss_ref, gf_ref, gag_ref, gcg_ref, gwo_ref):
                r[...] = jnp.zeros_like(r)

        gmat = gm_ref[...]
        inv_g = 1.0 / HEAD_DIM
        o_v = o_ref[...]
        ra = lax.rsqrt(_group_sum(o_v * o_v, gmat, STAT_TERMS) * inv_g + EPS)
        n_a = o_v * ra
        a_n = n_a * ag_ref[...]
        za = za_ref[...]
        sig_a = _sigmoid(za)
        sz_a = za * sig_a
        y_a = a_n * sz_a
        gb = gb_ref[...]
        gc = gc_ref[...]
        xc = xc_ref[...]
        cx = gc * xc
        cx_prev = jnp.where(i == 0, 0.0, gch_ref[...] * xch_ref[...])
        conv = (cw_ref[0:1, :] * _shift_down(cx_prev, cx, 2) + cw_ref[1:2, :] * _shift_down(cx_prev, cx, 1)
                + cw_ref[2:3, :] * cx)
        e = gb * conv
        re = lax.rsqrt(_group_sum(e * e, gmat, STAT_TERMS) * inv_g + EPS)
        n_e = e * re
        e_n = n_e * cg_ref[...]
        zc = zc_ref[...]
        sig_c = _sigmoid(zc)
        sz_c = zc * sig_c
        y_c = e_n * sz_c
        mix = jnp.concatenate([y_a, y_c], axis=-1)
        mix_b = mix.astype(MXU_DTYPE)
        first = jnp.where(i == 0, mb[...], x_refs[0][...])
        h = jnp.concatenate([first] + [r[...] for r in x_refs[1:]], axis=0)
        out = h + _dot(mix_b, wo_ref[...])
        r2 = lax.rsqrt(jnp.mean(out * out, axis=-1, keepdims=True) + EPS)
        n_f = out * r2
        y = n_f * fg_ref[...]
        tgt = jnp.concatenate([r[...] for r in t_refs], axis=0)
        valid = (i * t + lax.broadcasted_iota(jnp.int32, (t, 1), 0)) >= FRONT
        diff = jnp.where(valid, y - tgt, 0.0)
        loss_ref[...] = loss_ref[...] + 0.5 * jnp.sum(jnp.sum(diff * diff, axis=-1, keepdims=True) * (1.0 / D_MODEL))
        dy = diff * (1.0 / D_MODEL)
        gf_ref[...] = gf_ref[...] + jnp.sum(dy * n_f, axis=0, keepdims=True)
        dn = dy * fg_ref[...]
        d_out = r2 * (dn - n_f * jnp.mean(dn * n_f, axis=-1, keepdims=True))
        dout_ref[...] = d_out
        d_out_b = d_out.astype(MXU_DTYPE)
        d_mix = _dot_nt(d_out_b, wo_ref[...])
        gwo_ref[...] = gwo_ref[...] + _dot(mix.T.astype(MXU_DTYPE), d_out_b)
        d_ya = d_mix[:, :D_ATTN]
        d_yc = d_mix[:, D_ATTN:]
        d_an = d_ya * sz_a
        dza_ref[...] = (d_ya * a_n * (sig_a * (1.0 + za * (1.0 - sig_a)))).astype(dza_ref.dtype)
        gag_ref[...] = gag_ref[...] + jnp.sum(d_an * n_a, axis=0, keepdims=True)
        dn_a = d_an * ag_ref[...]
        d_o = ra * (dn_a - n_a * (_group_sum(dn_a * n_a, gmat, STAT_TERMS) * inv_g))
        d_o_l = d_o / l_ref[...]
        d_o_b = d_o_l.astype(do_ref.dtype)
        do_ref[...] = d_o_b
        dot_ref[...] = d_o_l.T.astype(dot_ref.dtype)
        delta = _group_sum(d_o_b.astype(F32) * o_v, hr_ref[...])
        terms, rest_of = [], delta
        for k in range(DELTA_TERMS):
            terms.append(rest_of.astype(MXU_DTYPE).astype(F32))
            rest_of = rest_of - terms[-1]
        dl_ref[...] = -sum(pltpu.roll(term, HEADS * k, 1) if k else term
                           for k, term in enumerate(terms)).astype(dl_ref.dtype)
        d_en = d_yc * sz_c
        dzc_ref[...] = (d_yc * e_n * (sig_c * (1.0 + zc * (1.0 - sig_c)))).astype(dzc_ref.dtype)
        gcg_ref[...] = gcg_ref[...] + jnp.sum(d_en * n_e, axis=0, keepdims=True)
        dn_e = d_en * cg_ref[...]
        d_e = re * (dn_e - n_e * (_group_sum(dn_e * n_e, gmat, STAT_TERMS) * inv_g))
        dgb_ref[...] = (d_e * conv).astype(dgb_ref.dtype)
        dcv_ref[...] = d_e * gb

    head_rep = jnp.where((lax.broadcasted_iota(jnp.int32, (D_ATTN, LANE), 0) >> 6)
                         == lax.broadcasted_iota(jnp.int32, (D_ATTN, LANE), 1), 1.0, 0.0).astype(MXU_DTYPE)
    row_blk = lambda cols: pl.BlockSpec((t, cols), lambda i: (i, 0))
    rest_blk = lambda s: pl.BlockSpec((t, 512), functools.partial(lambda i, s: (i, s), s=s))
    halo = lambda s: pl.BlockSpec((SUBLANE, 512), functools.partial(lambda i, s: (jnp.maximum(i * hb - 1, 0), s), s=s))
    const = lambda shape: pl.BlockSpec(shape, lambda i: (0, 0))
    acc = lambda shape: pl.BlockSpec(shape, lambda i: (0, 0))
    return pl.pallas_call(
        body, name="post_fwd_bwd", grid=(nt,),
        in_specs=[row_blk(D_ATTN), row_blk(D_ATTN)] + [rest_blk(s) for s in range(5)] + [halo(2), halo(3)]
                 + _x_block_specs(n_sub, LANE) + [const((LANE, D_MODEL))] + _x_block_specs(n_sub, LANE)
                 + [const((D_MODEL, D_MODEL)), const((1, D_ATTN)), const((1, D_CONV)), const((1, D_MODEL)),
                    const((SUBLANE, D_CONV)), const((D_ATTN, D_ATTN)), const((D_ATTN, LANE))],
        out_specs=(row_blk(D_MODEL), row_blk(D_ATTN), pl.BlockSpec((None, D_ATTN, t), lambda i: (i, 0, 0)),
                   row_blk(LANE), row_blk(D_ATTN), row_blk(D_CONV),
                   row_blk(D_CONV), row_blk(D_CONV),
                   acc((1, LANE)), acc((1, D_MODEL)), acc((1, D_ATTN)), acc((1, D_CONV)), acc((D_MODEL, D_MODEL))),
        out_shape=(jax.ShapeDtypeStruct((lp, D_MODEL), F32), jax.ShapeDtypeStruct((lp, D_ATTN), MXU_DTYPE),
                   jax.ShapeDtypeStruct((nt, D_ATTN, t), MXU_DTYPE), jax.ShapeDtypeStruct((lp, LANE), MXU_DTYPE),
                   jax.ShapeDtypeStruct((lp, D_ATTN), MXU_DTYPE),
                   jax.ShapeDtypeStruct((lp, D_CONV), MXU_DTYPE), jax.ShapeDtypeStruct((lp, D_CONV), MXU_DTYPE),
                   jax.ShapeDtypeStruct((lp, D_CONV), F32),
                   jax.ShapeDtypeStruct((1, LANE), F32), jax.ShapeDtypeStruct((1, D_MODEL), F32),
                   jax.ShapeDtypeStruct((1, D_ATTN), F32), jax.ShapeDtypeStruct((1, D_CONV), F32),
                   jax.ShapeDtypeStruct((D_MODEL, D_MODEL), F32)),
        compiler_params=_params(("arbitrary",)),
    )(o, l_sum, *([rest] * 5), rest, rest, *([x2] * n_sub), meta_blk, *([tgt2] * n_sub),
      w_out, attn_g, conv_g, final_g, conv_w8, _group_matrix(), head_rep)


def _bwd_in(x2, meta_blk, norm_g, w_pad, bf_pad, fl, dc, dq, dk, dv, dza, dgb, dzc, dconv, rest, d_out, conv_w8):
    lp = fl.shape[0]
    t = ROW_TILE
    nt = lp // t
    n_sub = t // LANE
    hb = t // SUBLANE
    rev = lambda i: nt - 1 - i

    def body(*refs):
        x_refs = refs[:n_sub]
        (mb, g_ref, w_ref, bf_ref, fl_ref, dc_ref, dq_ref, dk_ref, dv_ref, dza_ref, dgb_ref, dzc_ref,
         dcv_ref, dcvn_ref, gc_ref, xc_ref, gch_ref, xch_ref, dout_ref, cw_ref, tri_ref) = refs[n_sub:n_sub + 21]
        dp_ref, gx_ref, front_ref, gn_ref, gbf_ref, gcw_ref, carry, dh_scr, gx_sems = refs[n_sub + 21:]
        step = pl.program_id(0)
        i = rev(step)

        @pl.when(step == 0)
        def _():
            for r in (gn_ref, gbf_ref, gcw_ref, carry):
                r[...] = jnp.zeros_like(r)

        dc8 = jnp.concatenate([dc_ref[...], jnp.zeros((LANE - HEADS, t), F32)], axis=0).T
        dlogf = _dot_exact(tri_ref[...], dc8) + carry[...]
        carry[...] = carry[...] + jnp.sum(dc8, axis=0, keepdims=True)
        z = fl_ref[...] + bf_ref[...]
        row = i * t + lax.broadcasted_iota(jnp.int32, (t, LANE), 0)
        d_f = jnp.where(row >= PAD_ROWS, dlogf * (1.0 / (1.0 + jnp.exp(z))), 0.0)
        gbf_ref[...] = gbf_ref[...] + jnp.sum(d_f, axis=0, keepdims=True)
        dcv = dcv_ref[...]
        dcv_next = jnp.where(i == nt - 1, 0.0, dcvn_ref[...])
        d_cx = (cw_ref[2:3, :] * dcv + cw_ref[1:2, :] * _shift_up(dcv, dcv_next, 1)
                + cw_ref[0:1, :] * _shift_up(dcv, dcv_next, 2))
        gc = gc_ref[...]
        xc = xc_ref[...]
        cx = gc * xc
        cx_prev = jnp.where(i == 0, 0.0, gch_ref[...] * xch_ref[...])
        rowi = lax.broadcasted_iota(jnp.int32, (SUBLANE, 1), 0)
        gcw = (jnp.where(rowi == 0, jnp.sum(dcv * _shift_down(cx_prev, cx, 2), axis=0, keepdims=True), 0.0)
               + jnp.where(rowi == 1, jnp.sum(dcv * _shift_down(cx_prev, cx, 1), axis=0, keepdims=True), 0.0)
               + jnp.where(rowi == 2, jnp.sum(dcv * cx, axis=0, keepdims=True), 0.0))
        gcw_ref[...] = gcw_ref[...] + gcw
        dp_ref[:, SEG_Q:SEG_Q + 512] = dq_ref[...]
        dp_ref[:, SEG_K:SEG_K + 512] = dk_ref[...]
        dp_ref[:, SEG_V:SEG_V + 512] = dv_ref[...]
        dp_ref[:, SEG_F:SEG_F + LANE] = d_f.astype(dp_ref.dtype)
        dp_ref[:, SEG_ZA:SEG_ZA + 512] = dza_ref[...]
        dp_ref[:, SEG_GB:SEG_GB + 512] = dgb_ref[...]
        dp_ref[:, SEG_GC:SEG_GC + 512] = (d_cx * xc).astype(dp_ref.dtype)
        dp_ref[:, SEG_XC:SEG_XC + 512] = (d_cx * gc).astype(dp_ref.dtype)
        dp_ref[:, SEG_ZC:SEG_ZC + 512] = dzc_ref[...]
        d_u = _dot(dp_ref[...], w_ref[...])
        first = jnp.where(i == 0, mb[...], x_refs[0][...])
        h = jnp.concatenate([first] + [r[...] for r in x_refs[1:]], axis=0)
        r1 = lax.rsqrt(jnp.mean(h * h, axis=-1, keepdims=True) + EPS)
        n_h = h * r1
        gn_ref[...] = gn_ref[...] + jnp.sum(d_u * n_h, axis=0, keepdims=True)
        dn = d_u * g_ref[...]
        d_h = dout_ref[...] + r1 * (dn - n_h * jnp.mean(dn * n_h, axis=-1, keepdims=True))
        slot = step % 2

        def to_grad_x(slot_, tile):
            return pltpu.make_async_copy(dh_scr.at[slot_], gx_ref.at[pl.ds(pl.multiple_of(tile * t - FRONT, SUBLANE), t)],
                                         gx_sems.at[slot_])

        @pl.when(step >= 2)
        def _():
            to_grad_x(slot, 1).wait()

        dh_scr[slot] = d_h

        @pl.when(i > 0)
        def _():
            to_grad_x(slot, i).start()

        @pl.when(i == 0)
        def _():
            front_ref[...] = d_h[:FRONT]
            rest_rows = pltpu.make_async_copy(dh_scr.at[slot, pl.ds(FRONT, t - FRONT)], gx_ref.at[pl.ds(0, t - FRONT)],
                                              gx_sems.at[slot])
            rest_rows.start()
            rest_rows.wait()
            if nt >= 2:
                to_grad_x(1 - slot, 1).wait()

    def x_specs():
        specs = [pl.BlockSpec((LANE, D_MODEL), lambda s: (jnp.maximum(n_sub * rev(s) - 1, 0), 0))]
        for b in range(1, n_sub):
            specs.append(pl.BlockSpec((LANE, D_MODEL), functools.partial(lambda s, b: (n_sub * rev(s) - 1 + b, 0), b=b)))
        return specs

    row_blk = lambda cols: pl.BlockSpec((t, cols), lambda s: (rev(s), 0))
    rest_blk = lambda k: pl.BlockSpec((t, 512), functools.partial(lambda s, k: (rev(s), k), k=k))
    halo_prev = lambda k: pl.BlockSpec(
        (SUBLANE, 512), functools.partial(lambda s, k: (jnp.maximum(rev(s) * hb - 1, 0), k), k=k))
    halo_next = pl.BlockSpec((SUBLANE, 512), lambda s: (jnp.minimum((rev(s) + 1) * hb, lp // SUBLANE - 1), 0))
    const = lambda shape: pl.BlockSpec(shape, lambda s: (0, 0))
    return pl.pallas_call(
        body, name="bwd_in", grid=(nt,),
        in_specs=x_specs() + [const((LANE, D_MODEL)), const((1, D_MODEL)),
                              pl.BlockSpec((D_IN_PAD, D_MODEL), lambda s: (0, 0), pipeline_mode=pl.Buffered(1)),
                              const((1, LANE)), row_blk(LANE),
                              pl.BlockSpec((HEADS, t), lambda s: (0, rev(s))),
                              row_blk(512), row_blk(512), row_blk(512), row_blk(512), row_blk(512), row_blk(512),
                              row_blk(512), halo_next, rest_blk(2), rest_blk(3), halo_prev(2), halo_prev(3),
                              row_blk(D_MODEL), const((SUBLANE, D_CONV)), const((t, t))],
        out_specs=(row_blk(D_IN_PAD), ANY, const((FRONT, D_MODEL)), const((1, D_MODEL)), const((1, LANE)),
                   const((SUBLANE, D_CONV))),
        out_shape=(jax.ShapeDtypeStruct((lp, D_IN_PAD), MXU_DTYPE), jax.ShapeDtypeStruct((lp - FRONT, D_MODEL), F32),
                   jax.ShapeDtypeStruct((FRONT, D_MODEL), F32),
                   jax.ShapeDtypeStruct((1, D_MODEL), F32), jax.ShapeDtypeStruct((1, LANE), F32),
                   jax.ShapeDtypeStruct((SUBLANE, D_CONV), F32)),
        scratch_shapes=[pltpu.VMEM((1, LANE), F32), pltpu.VMEM((2, t, D_MODEL), F32), pltpu.SemaphoreType.DMA((2,))],
        compiler_params=_params(("arbitrary",)),
    )(*([x2] * n_sub), meta_blk, norm_g, w_pad, bf_pad, fl, dc, dq, dk, dv, dza, dgb, dzc, dconv, dconv,
      rest, rest, rest, rest, d_out, conv_w8, _triangle(t, lower=False))


def _grad_w_in(u, dproj):
    lp = u.shape[0]
    tn = GW_COL_TILE
    tk = tn if lp % tn == 0 else ROW_TILE

    def body(d_ref, u_ref, o_ref, wire_ref):
        k = pl.program_id(1)

        @pl.when(k == 0)
        def _():
            o_ref[...] = jnp.zeros_like(o_ref)

        o_ref[...] = o_ref[...] + lax.dot_general(d_ref[...], u_ref[...], (((0,), (0,)), ((), ())),
                                                  preferred_element_type=F32)

        @pl.when(k == pl.num_programs(1) - 1)
        def _():
            wire_ref[...] = o_ref[...].astype(wire_ref.dtype)

    out_spec = pl.BlockSpec((tn, D_MODEL), lambda n, k: (n, 0))
    return pl.pallas_call(
        body, name="grad_w_in", grid=(D_IN_PAD // tn, lp // tk),
        in_specs=[pl.BlockSpec((tk, tn), lambda n, k: (k, n)), pl.BlockSpec((tk, D_MODEL), lambda n, k: (k, 0))],
        out_specs=(out_spec, out_spec),
        out_shape=(jax.ShapeDtypeStruct((D_IN_PAD, D_MODEL), F32), jax.ShapeDtypeStruct((D_IN_PAD, D_MODEL), WIRE_DTYPE)),
        compiler_params=_params(("parallel", "arbitrary")),
    )(dproj, u)


def _by_chip(own, others, me):
    by_mask = jnp.stack([own, others[1], others[0], others[2]])
    return [lax.dynamic_index_in_dim(by_mask, jnp.bitwise_xor(me, s), 0, keepdims=False) for s in range(N_CHIPS)]


def _both_halves(mine, other, c):
    return jnp.where(c == 0, jnp.concatenate([mine, other], axis=0), jnp.concatenate([other, mine], axis=0))


def _local_step(x2, tgt2, meta_full, norm_g, w_pad, b_f, conv_w_full, attn_g, conv_g, w_out_full, final_g):
    lp = x2.shape[0] + FRONT
    nt = lp // ROW_TILE
    meta_blk = jnp.concatenate([jnp.zeros((PAD_ROWS, D_MODEL), F32), meta_full], axis=0)
    bf_pad = jnp.pad(b_f, ((0, 0), (0, LANE - HEADS)))
    conv_w8 = jnp.pad(conv_w_full, ((0, SUBLANE - conv_w_full.shape[0]), (0, 0)))
    q, k, v, rest, fl, ct, u, q_t, k_t, v_t, cc = _in_proj(x2, meta_blk, norm_g, w_pad, bf_pad)
    ct4 = ct.reshape(SUBLANE, nt, 1, ROW_TILE)
    o, l_sum, m_max = _attn_fwd(q, k, v_t, cc)
    (d_out, d_o, do_t, neg_delta, dza, dgb, dzc, dconv, loss, g_final, g_attn, g_convg, gw_out) = _post(
        o, l_sum, rest, x2, meta_blk, tgt2, w_out_full, attn_g, conv_g, final_g, conv_w8)
    dq, dk, dv, dc = _attn_bwd(q, k, v, d_o, q_t, k_t, do_t, m_max, neg_delta, ct4)
    dproj, grad_x, d_front, g_norm, g_bf, g_cw = _bwd_in(x2, meta_blk, norm_g, w_pad, bf_pad, fl, dc.reshape(HEADS, lp), dq, dk, dv,
                                             dza, dgb, dzc, dconv, rest, d_out, conv_w8)
    gw_in, gw_in_wire = _grad_w_in(u, dproj)
    return dict(loss=loss, grad_x=grad_x, d_front=d_front, g_norm=g_norm, g_final=g_final, g_attn=g_attn, g_convg=g_convg, g_bf=g_bf,
                g_cw=g_cw, gw_out=gw_out, gw_in=gw_in, gw_in_wire=gw_in_wire)


def kernel(x, meta, norm_g, w_in, b_f, conv_w, attn_norm_g, conv_norm_g, w_out, final_norm_g, loss_target, m_meta, m_norm_g, m_w_in, m_b_f, m_conv_w, m_attn_norm_g, m_conv_norm_g, m_w_out, m_final_norm_g, v_meta, v_norm_g, v_w_in, v_b_f, v_conv_w, v_attn_norm_g, v_conv_norm_g, v_w_out, v_final_norm_g):
    cx_, cy_, cc_ = _position()
    chip = 2 * cx_ + cy_
    shard = w_in.shape[2]
    out_half = w_out.shape[1] // 2
    pick = lambda vals: jnp.where(chip == 0, vals[0], jnp.where(chip == 1, vals[1], jnp.where(chip == 2, vals[2], vals[3])))
    a_off, b_off = pick(A_OFF), pick(B_OFF)
    wt = jnp.transpose(w_in[0]).astype(MXU_DTYPE)

    def placed(piece, off):
        return lax.dynamic_slice_in_dim(jnp.pad(piece, ((WIN_ROWS, WIN_ROWS), (0, 0))), WIN_ROWS - off, WIN_ROWS, 0)

    wi = placed(wt[:PIECE_A], a_off) + placed(wt[PIECE_A:], b_off)
    wo = w_out[0].astype(MXU_DTYPE)
    small = jnp.concatenate([meta, jnp.pad(conv_w[0], ((0, 8 - conv_w.shape[1]), (0, meta.shape[1] - conv_w.shape[2])))],
                            axis=0)
    gwi, gwo, gsm = _gather_weights(wi.reshape(2, WIN_HALF, D_MODEL), wo.reshape(2, out_half, D_MODEL), small)
    starts = jnp.stack([_window_start(jnp.bitwise_xor(chip, mask)) for mask in (0, 2, 1, 3)]).astype(jnp.int32)
    w_pad = _assemble_w(wi, gwi.reshape(3, WIN_ROWS, D_MODEL), starts)
    w_out_full = jnp.concatenate(_by_chip(wo, gwo.reshape(3, 2 * out_half, D_MODEL), chip), axis=0)
    small_full = jnp.concatenate(_by_chip(small, gsm, chip), axis=1)
    meta_full = small_full[:N_META]
    conv_w_full = jnp.concatenate([small_full[N_META:N_META + 3, 256 * s:256 * s + LANE] for s in range(N_CHIPS)], axis=1)
    final_g2 = final_norm_g.reshape(1, D_MODEL)
    r = _local_step(x[0], loss_target[0], meta_full, norm_g, w_pad, b_f, conv_w_full, attn_norm_g, conv_norm_g,
                    w_out_full, final_g2)
    grad_x = r["grad_x"][None]
    gb = r["gw_out"].reshape(N_CHIPS, 2, out_half, D_MODEL)
    wide = lambda a: jnp.pad(a, ((0, 0), (0, D_MODEL - a.shape[1])))
    pack = jnp.concatenate([
        r["g_norm"], r["g_final"], jnp.concatenate([r["g_attn"], r["g_convg"]], axis=1), wide(r["g_bf"]),
        wide(r["loss"]), jnp.zeros((3, D_MODEL), F32), r["d_front"][PAD_ROWS:], wide(r["g_cw"])], axis=0)
    ra, rb, packs = _pair_exchange(r["gw_in_wire"], gb, pack)
    c_idx = jnp.reshape(cc_, (1,)).astype(jnp.int32)
    chip_idx = jnp.reshape(chip, (1,)).astype(jnp.int32)
    pa, pa_wire = _pair_sum_windows(r["gw_in"], ra, c_idx)
    pb, pb_wire = _pair_sum(gb, rb, c_idx)
    xa, xb = _chip_exchange(pa_wire, pb_wire)
    ha = _chip_sum(pa, xa, chip_idx)
    hb = _chip_sum(pb, xb, chip_idx)
    oa, ob = _pair_share(ha, hb)
    g_window = _both_halves(ha, oa, cc_)
    g_w_in_t = jnp.concatenate([lax.dynamic_slice_in_dim(g_window, a_off, PIECE_A, 0),
                                lax.dynamic_slice_in_dim(g_window, b_off, shard - PIECE_A, 0)], axis=0)
    g_w_out = _both_halves(hb, ob, cc_)
    as_rows = lambda a: jnp.transpose(a, (2, 0, 1))
    g_w_in, d_w_in, nm_w_in, nv_w_in = (jnp.transpose(a, (1, 2, 0)) for a in _adamw_rows(
        as_rows(w_in), g_w_in_t, as_rows(m_w_in), as_rows(v_w_in)))
    d_w_out, nm_w_out, nv_w_out = (a[None] for a in _adamw_big(w_out[0], g_w_out, m_w_out[0], v_w_out[0], LANE))
    params = (norm_g, final_g2, attn_norm_g, conv_norm_g, b_f, meta, conv_w[0])
    ms = (m_norm_g, m_final_norm_g.reshape(1, D_MODEL), m_attn_norm_g, m_conv_norm_g, m_b_f, m_meta, m_conv_w[0])
    vs = (v_norm_g, v_final_norm_g.reshape(1, D_MODEL), v_attn_norm_g, v_conv_norm_g, v_b_f, v_meta, v_conv_w[0])
    loss, g_s, d_s, m_s, v_s = _small_update(pack, packs, params, ms, vs)

    def ordered(small_list, big_in, big_out):
        s_norm, s_final, s_attn, s_convg, s_bf, s_meta, s_cw = small_list
        return (s_meta, s_norm, big_in, s_bf, s_cw[None], s_attn, s_convg, big_out, s_final.reshape(D_MODEL))

    return (loss.reshape(()), grad_x,
            *ordered(g_s, g_w_in, g_w_out[None]), *ordered(d_s, d_w_in, d_w_out),
            *ordered(m_s, nm_w_in, nm_w_out), *ordered(v_s, nv_w_in, nv_w_out))
```

```python
import functools

import jax
import jax.numpy as jnp
from jax import lax
from jax.experimental import pallas as pl
from jax.experimental.pallas import tpu as pltpu

F32 = jnp.float32
MXU_DTYPE = jnp.bfloat16
WIRE_DTYPE = jnp.bfloat16

D_MODEL = 1024
N_META = 16
HEADS = 8
HEAD_DIM = 64
D_ATTN = HEADS * HEAD_DIM
D_CONV = 512
EPS = 1e-6
LANE = 128
SUBLANE = 8
ROW_TILE = 384
ATTN_UNROLL = 3
ATTN_BWD_QBLOCKS = 2
DELTA_TERMS = 3
ADAM_ROW_STEPS = 3
STAT_TERMS = 1
FRONT = LANE
PAD_ROWS = FRONT - N_META
NEG = -1e30
LOG2E = 1.4426950408889634
N_CHIPS = 4
N_DEV = 8
VMEM_LIMIT_BYTES = 60 * 1024 * 1024

SEG_Q, SEG_K, SEG_V, SEG_F, SEG_ZA, SEG_GB, SEG_GC, SEG_XC, SEG_ZC = (
    0, 512, 1024, 1536, 1664, 2176, 2688, 3200, 3712)
D_IN = 4104
D_IN_PAD = 4224
F_END = 1544
GW_COL_TILE = 1408
WIN_ROWS = 1152
WIN_HALF = WIN_ROWS // 2
WIN_START = (0, 1024, 2160, 3072)
PIECE_A = 518
A_OFF = (0, 2, 12, 126)
B_OFF = (518, 640, 530, 644)
ADAM_LR = 0.001
ADAM_B1 = 0.9
ADAM_B2 = 0.999
ADAM_EPS = 1e-08
ADAM_WD = 0.01
ADAM_STEP = 10

MESH = pl.DeviceIdType.MESH
ANY = pl.BlockSpec(memory_space=pl.ANY)

PACK_ROWS = 32
SLOT_NORM = (0, 1, 0, 1024)
SLOT_FINAL = (1, 2, 0, 1024)
SLOT_ATTN = (2, 3, 0, 512)
SLOT_CONVG = (2, 3, 512, 1024)
SLOT_BF = (3, 4, 0, 8)
SLOT_META = (8, 24, 0, 256)
SLOT_CONVW = (24, 27, 0, 128)
LOSS_ROW = 4


def _params(sem=None):
    return pltpu.CompilerParams(dimension_semantics=sem, vmem_limit_bytes=VMEM_LIMIT_BYTES)


def _sigmoid(z):
    return 1.0 / (1.0 + jnp.exp(-z))


def _dot(a, b):
    return jnp.dot(a, b, preferred_element_type=F32)


def _dot_nt(a, b):
    return lax.dot_general(a, b, (((1,), (1,)), ((), ())), preferred_element_type=F32)


def _dot_exact(ones, x):
    ones = ones.astype(MXU_DTYPE)
    total = None
    for _ in range(3):
        term = x.astype(MXU_DTYPE)
        x = x - term.astype(F32)
        total = _dot(ones, term) if total is None else total + _dot(ones, term)
    return total


def _group_matrix():
    r = lax.broadcasted_iota(jnp.int32, (D_ATTN, D_ATTN), 0) >> 6
    c = lax.broadcasted_iota(jnp.int32, (D_ATTN, D_ATTN), 1) >> 6
    return jnp.where(r == c, 1.0, 0.0).astype(MXU_DTYPE)


def _triangle(n, lower):
    r = lax.broadcasted_iota(jnp.int32, (n, n), 0)
    c = lax.broadcasted_iota(jnp.int32, (n, n), 1)
    return jnp.where((r >= c) if lower else (c >= r), 1.0, 0.0).astype(MXU_DTYPE)


def _group_sum(x, gmat, terms=2):
    hi = x.astype(MXU_DTYPE)
    if terms == 1:
        return _dot(hi, gmat)
    lo = (x - hi.astype(F32)).astype(MXU_DTYPE)
    return _dot(hi, gmat) + _dot(lo, gmat)


def _x_block_specs(n_sub, rows):
    specs = [pl.BlockSpec((rows, D_MODEL), lambda i: (jnp.maximum(n_sub * i - 1, 0), 0))]
    for b in range(1, n_sub):
        specs.append(pl.BlockSpec((rows, D_MODEL), functools.partial(lambda i, b: (n_sub * i - 1 + b, 0), b=b)))
    return specs


def _position():
    return lax.axis_index("x"), lax.axis_index("y"), lax.axis_index("c")


def _gather_weights(wi, wo, small):
    def body(wi_ref, wo_ref, sm_ref, gwi_ref, gwo_ref, gsm_ref, send_sems, recv_sems):
        x, y, c = _position()
        sibling = (x, y, 1 - c)
        chips = [(1 - x, y), (x, 1 - y), (1 - x, 1 - y)]

        def remote(k, src, dst, to):
            return pltpu.make_async_remote_copy(src_ref=src, dst_ref=dst, send_sem=send_sems.at[k],
                                                recv_sem=recv_sems.at[k], device_id=to, device_id_type=MESH)

        first, passed, landed = [], [], []
        for a, (src_ref, g_ref) in enumerate(((wi_ref, gwi_ref), (wo_ref, gwo_ref))):
            for j, (cx, cy) in enumerate(chips):
                slot = g_ref.at[j, c]
                first.append(remote(6 * a + j, src_ref.at[c], slot, (cx, cy, c)))
                landed.append(remote(6 * a + j, slot, slot, sibling))
                passed.append(remote(6 * a + 3 + j, slot, slot, sibling))
        for j, (cx, cy) in enumerate(chips):
            first.append(remote(12 + j, sm_ref, gsm_ref.at[j], (cx, cy, c)))
        for cp in first:
            cp.start()
        for arrived, onward in zip(landed, passed):
            arrived.wait_recv()
            onward.start()
        for a, g_ref in enumerate((gwi_ref, gwo_ref)):
            for j in range(3):
                remote(6 * a + 3 + j, g_ref.at[j, 1 - c], g_ref.at[j, 1 - c], sibling).wait_recv()
        for j in range(3):
            remote(12 + j, sm_ref, gsm_ref.at[j], sibling).wait_recv()
        for cp in first + passed:
            cp.wait_send()

    return pl.pallas_call(
        body, name="gather_weights",
        out_shape=(jax.ShapeDtypeStruct((3,) + wi.shape, wi.dtype), jax.ShapeDtypeStruct((3,) + wo.shape, wo.dtype),
                   jax.ShapeDtypeStruct((3,) + small.shape, small.dtype)),
        in_specs=[ANY, ANY, ANY], out_specs=(ANY, ANY, ANY),
        scratch_shapes=[pltpu.SemaphoreType.DMA((15,)), pltpu.SemaphoreType.DMA((15,))],
    )(wi, wo, small)


def _pair_exchange(gw, gb, pack):
    n_big = N_CHIPS + 1

    def body(gw_ref, gb_ref, p_ref, ra_ref, rb_ref, o_ref, send_sems, recv_sems):
        x, y, c = _position()
        sibling = (x, y, 1 - c)

        def remote(k, src, dst, to):
            return pltpu.make_async_remote_copy(src_ref=src, dst_ref=dst, send_sem=send_sems.at[k],
                                                recv_sem=recv_sems.at[k], device_id=to, device_id_type=MESH)

        copies = [remote(N_CHIPS, gb_ref.at[:, 1 - c], rb_ref, sibling)]
        for s, start in enumerate(WIN_START):
            rows = pl.ds(pl.multiple_of(start + WIN_HALF * (1 - c), 2 * SUBLANE), WIN_HALF)
            copies.append(remote(s, gw_ref.at[rows], ra_ref.at[s], sibling))
        for mask in range(1, N_DEV):
            peer = (1 - x if mask & 4 else x, 1 - y if mask & 2 else y, 1 - c if mask & 1 else c)
            copies.append(remote(n_big + mask - 1, p_ref, o_ref.at[mask - 1], peer))
        for cp in copies:
            cp.start()
        for cp in copies:
            cp.wait()

    n_sems = n_big + N_DEV - 1
    return pl.pallas_call(
        body, name="grad_pair_exchange",
        out_shape=(jax.ShapeDtypeStruct((N_CHIPS, WIN_HALF, D_MODEL), gw.dtype),
                   jax.ShapeDtypeStruct((N_CHIPS,) + gb.shape[2:], gb.dtype),
                   jax.ShapeDtypeStruct((N_DEV - 1,) + pack.shape, pack.dtype)),
        in_specs=[ANY, ANY, ANY], out_specs=(ANY, ANY, ANY),
        scratch_shapes=[pltpu.SemaphoreType.DMA((n_sems,)), pltpu.SemaphoreType.DMA((n_sems,))],
    )(gw, gb, pack)


def _chip_exchange(pa, pb):
    def body(pa_ref, pb_ref, ra_ref, rb_ref, send_sems, recv_sems):
        x, y, c = _position()
        chips = [(1 - x, y), (x, 1 - y), (1 - x, 1 - y)]
        copies = []
        for a, (src, dst) in enumerate(((pa_ref, ra_ref), (pb_ref, rb_ref))):
            for j, (cx, cy) in enumerate(chips):
                copies.append(pltpu.make_async_remote_copy(
                    src_ref=src.at[2 * cx + cy], dst_ref=dst.at[j], send_sem=send_sems.at[3 * a + j],
                    recv_sem=recv_sems.at[3 * a + j], device_id=(cx, cy, c), device_id_type=MESH))
        for cp in copies:
            cp.start()
        for cp in copies:
            cp.wait()

    return pl.pallas_call(
        body, name="grad_chip_exchange",
        out_shape=(jax.ShapeDtypeStruct((3,) + pa.shape[1:], pa.dtype),
                   jax.ShapeDtypeStruct((3,) + pb.shape[1:], pb.dtype)),
        in_specs=[ANY, ANY], out_specs=(ANY, ANY),
        scratch_shapes=[pltpu.SemaphoreType.DMA((6,)), pltpu.SemaphoreType.DMA((6,))],
    )(pa, pb)


def _pair_share(ha, hb):
    def body(ha_ref, hb_ref, oa_ref, ob_ref, send_sems, recv_sems):
        x, y, c = _position()
        copies = [pltpu.make_async_remote_copy(
            src_ref=src, dst_ref=dst, send_sem=send_sems.at[k], recv_sem=recv_sems.at[k],
            device_id=(x, y, 1 - c), device_id_type=MESH)
            for k, (src, dst) in enumerate(((ha_ref, oa_ref), (hb_ref, ob_ref)))]
        for cp in copies:
            cp.start()
        for cp in copies:
            cp.wait()

    return pl.pallas_call(
        body, name="grad_pair_share",
        out_shape=(jax.ShapeDtypeStruct(ha.shape, ha.dtype), jax.ShapeDtypeStruct(hb.shape, hb.dtype)),
        in_specs=[ANY, ANY], out_specs=(ANY, ANY),
        scratch_shapes=[pltpu.SemaphoreType.DMA((2,)), pltpu.SemaphoreType.DMA((2,))],
    )(ha, hb)


def _pair_sum(mine, recv, c_idx):
    rows, cols = mine.shape[2:]

    def body(c_ref, a_ref, b_ref, o_ref, send_ref):
        total = a_ref[...] + b_ref[...]
        o_ref[...] = total
        send_ref[...] = total.astype(send_ref.dtype)

    out_spec = pl.BlockSpec((None, rows, cols), lambda s, c_ref: (s, 0, 0))
    return pl.pallas_call(
        body, name="grad_pair_sum",
        grid_spec=pltpu.PrefetchScalarGridSpec(
            num_scalar_prefetch=1, grid=(N_CHIPS,),
            in_specs=[pl.BlockSpec((None, None, rows, cols), lambda s, c_ref: (s, c_ref[0], 0, 0)),
                      pl.BlockSpec((None, rows, cols), lambda s, c_ref: (s, 0, 0))],
            out_specs=(out_spec, out_spec)),
        out_shape=(jax.ShapeDtypeStruct(recv.shape, recv.dtype), jax.ShapeDtypeStruct(recv.shape, WIRE_DTYPE)),
        compiler_params=_params(("parallel",)),
    )(c_idx, mine, recv)


def _window_start(s):
    return jnp.where(s == 0, WIN_START[0], jnp.where(s == 1, WIN_START[1], jnp.where(s == 2, WIN_START[2], WIN_START[3])))


def _pair_sum_windows(gw, recv, c_idx):
    tr = WIN_HALF // 3

    def body(c_ref, a_ref, b_ref, o_ref, send_ref):
        total = a_ref[...] + b_ref[...].astype(F32)
        o_ref[...] = total
        send_ref[...] = total.astype(send_ref.dtype)

    out_spec = pl.BlockSpec((None, tr, D_MODEL), lambda s, i, c_ref: (s, i, 0))
    return pl.pallas_call(
        body, name="grad_pair_sum_windows",
        grid_spec=pltpu.PrefetchScalarGridSpec(
            num_scalar_prefetch=1, grid=(N_CHIPS, WIN_HALF // tr),
            in_specs=[pl.BlockSpec((pl.Element(tr), pl.Element(D_MODEL)),
                                   lambda s, i, c_ref: (pl.multiple_of(
                                       _window_start(s) + WIN_HALF * c_ref[0] + tr * i, SUBLANE), 0)),
                      pl.BlockSpec((None, tr, D_MODEL), lambda s, i, c_ref: (s, i, 0))],
            out_specs=(out_spec, out_spec)),
        out_shape=(jax.ShapeDtypeStruct(recv.shape, F32), jax.ShapeDtypeStruct(recv.shape, WIRE_DTYPE)),
        compiler_params=_params(("parallel", "parallel")),
    )(c_idx, gw, recv)


def _assemble_w(own, others, starts):
    def body(starts_ref, own_ref, oth_ref, o_ref):
        o_ref[...] = jnp.zeros_like(o_ref)
        for k in range(N_CHIPS):
            rows = pl.ds(pl.multiple_of(starts_ref[k], 2 * SUBLANE), WIN_ROWS)
            o_ref[rows, :] = o_ref[rows, :] + (own_ref[...] if k == 0 else oth_ref[k - 1])

    return pl.pallas_call(
        body, name="assemble_w",
        in_specs=[pl.BlockSpec(memory_space=pltpu.SMEM), pl.BlockSpec(memory_space=pltpu.VMEM),
                  pl.BlockSpec(memory_space=pltpu.VMEM)],
        out_specs=pl.BlockSpec(memory_space=pltpu.VMEM),
        out_shape=jax.ShapeDtypeStruct((D_IN_PAD, D_MODEL), own.dtype),
        compiler_params=_params(),
    )(starts, own, others)


def _chip_sum(psum, recv3, chip_idx):
    rows, cols = psum.shape[1:]
    tr = rows // 2

    def body(s_ref, p_ref, r0, r1, r2, o_ref):
        o_ref[...] = ((p_ref[...] + r0[...].astype(F32)) + r1[...].astype(F32)) + r2[...].astype(F32)

    return pl.pallas_call(
        body, name="grad_chip_sum",
        grid_spec=pltpu.PrefetchScalarGridSpec(
            num_scalar_prefetch=1, grid=(2,),
            in_specs=[pl.BlockSpec((None, tr, cols), lambda i, s_ref: (s_ref[0], i, 0))] +
                     [pl.BlockSpec((None, tr, cols), functools.partial(lambda i, s_ref, j: (j, i, 0), j=j))
                      for j in range(3)],
            out_specs=pl.BlockSpec((tr, cols), lambda i, s_ref: (i, 0))),
        out_shape=jax.ShapeDtypeStruct((rows, cols), psum.dtype),
        compiler_params=_params(("parallel",)),
    )(chip_idx, psum, recv3, recv3, recv3)


def _adamw_math(w, g, m, v):
    m = ADAM_B1 * m + (1.0 - ADAM_B1) * g
    v = ADAM_B2 * v + (1.0 - ADAM_B2) * (g * g)
    m_hat = m * (1.0 / (1.0 - ADAM_B1 ** ADAM_STEP))
    v_hat = v * (1.0 / (1.0 - ADAM_B2 ** ADAM_STEP))
    delta = -ADAM_LR * (m_hat / (jnp.sqrt(v_hat) + ADAM_EPS) + ADAM_WD * w)
    return delta, m, v


def _adamw_big(w, g, m, v, tr):
    rows, cols = w.shape
    assert rows % tr == 0 and g.shape[0] >= rows

    def body(w_ref, g_ref, m_ref, v_ref, d_out, m_out, v_out):
        d, m2, v2 = _adamw_math(w_ref[...], g_ref[...], m_ref[...], v_ref[...])
        d_out[...] = d
        m_out[...] = m2
        v_out[...] = v2

    spec = pl.BlockSpec((tr, cols), lambda i: (i, 0))
    sds = jax.ShapeDtypeStruct((rows, cols), F32)
    return pl.pallas_call(
        body, name="adamw_big", grid=(rows // tr,), in_specs=[spec] * 4, out_specs=(spec,) * 3,
        out_shape=(sds,) * 3, compiler_params=_params(("parallel",)),
    )(w, g, m, v)


def _adamw_rows(w3, g, m3, v3):
    rows, _, cols = w3.shape
    rb = rows // ADAM_ROW_STEPS
    assert rb * ADAM_ROW_STEPS == rows

    def body(w_ref, g_ref, m_ref, v_ref, g_out, d_out, m_out, v_out):
        for k in range(ADAM_ROW_STEPS):
            @pl.when(pl.program_id(0) == k)
            def _(k=k):
                g = g_ref[k * rb:(k + 1) * rb, :]
                d, m2, v2 = _adamw_math(w_ref[:, 0, :], g, m_ref[:, 0, :], v_ref[:, 0, :])
                g_out[:, 0, :] = g
                d_out[:, 0, :] = d
                m_out[:, 0, :] = m2
                v_out[:, 0, :] = v2

    spec3 = pl.BlockSpec((rb, 1, cols), lambda i: (i, 0, 0))
    sds = jax.ShapeDtypeStruct((rows, 1, cols), F32)
    return pl.pallas_call(
        body, name="adamw_rows", grid=(ADAM_ROW_STEPS,),
        in_specs=[spec3, pl.BlockSpec((rows, cols), lambda i: (0, 0), pipeline_mode=pl.Buffered(1)), spec3, spec3],
        out_specs=(spec3,) * 4, out_shape=(sds,) * 4, compiler_params=_params(("parallel",)),
    )(w3, g, m3, v3)


def _small_update(own, others, params, ms, vs):
    slots = (SLOT_NORM, SLOT_FINAL, SLOT_ATTN, SLOT_CONVG, SLOT_BF, SLOT_META, SLOT_CONVW)
    n = len(slots)

    def body(*refs):
        own_ref, gp_ref = refs[:2]
        w_refs, m_refs, v_refs = refs[2:2 + n], refs[2 + n:2 + 2 * n], refs[2 + 2 * n:2 + 3 * n]
        outs = refs[2 + 3 * n:3 + 7 * n]
        loss_ref = outs[0]
        g_outs, d_outs, m_outs, v_outs = (outs[1 + k * n:1 + (k + 1) * n] for k in range(4))
        g_scr, w_scr, m_scr, v_scr = refs[3 + 7 * n:]
        x, y, c = _position()
        shard = 2 * x + y
        me = 4 * x + 2 * y + c
        tot = None
        for d in range(N_DEV):
            rel = jnp.bitwise_xor(me, d)
            term = jnp.where(rel == 0, own_ref[...], gp_ref[jnp.maximum(rel, 1) - 1])
            tot = term if tot is None else tot + term
        r0, r1, _, _ = SLOT_META
        meta_sel = tot[r0:r1, 0:256]
        cw_sel = tot[24:32, 0:128]
        for k in range(1, N_CHIPS):
            meta_sel = jnp.where(shard == k, tot[r0:r1, 256 * k:256 * (k + 1)], meta_sel)
            cw_sel = jnp.where(shard == k, tot[24:32, 128 * k:128 * (k + 1)], cw_sel)
        zeros = jnp.zeros((PACK_ROWS, D_MODEL), F32)
        for scr in (g_scr, w_scr, m_scr, v_scr):
            scr[...] = zeros
        g_scr[0:8, :] = tot[0:8, :]
        g_scr[r0:r1, 0:256] = meta_sel
        g_scr[24:32, 0:128] = cw_sel
        for (a, b, c0, c1), w_ref, m_ref, v_ref in zip(slots, w_refs, m_refs, v_refs):
            w_scr[a:b, c0:c1] = w_ref[...]
            m_scr[a:b, c0:c1] = m_ref[...]
            v_scr[a:b, c0:c1] = v_ref[...]
        loss_ref[...] = g_scr[LOSS_ROW:LOSS_ROW + 1, 0:1]
        d, m2, v2 = _adamw_math(w_scr[...], g_scr[...], m_scr[...], v_scr[...])
        w_scr[...] = d
        m_scr[...] = m2
        v_scr[...] = v2
        for (a, b, c0, c1), g_o, d_o, m_o, v_o in zip(slots, g_outs, d_outs, m_outs, v_outs):
            g_o[...] = g_scr[a:b, c0:c1]
            d_o[...] = w_scr[a:b, c0:c1]
            m_o[...] = m_scr[a:b, c0:c1]
            v_o[...] = v_scr[a:b, c0:c1]

    shapes = [jax.ShapeDtypeStruct(p.shape, F32) for p in params]
    out = pl.pallas_call(
        body, name="small_update",
        out_shape=[jax.ShapeDtypeStruct((1, 1), F32)] + shapes * 4,
        scratch_shapes=[pltpu.VMEM((PACK_ROWS, D_MODEL), F32)] * 4,
        compiler_params=_params(),
    )(own, others, *params, *ms, *vs)
    return out[0], out[1:1 + n], out[1 + n:1 + 2 * n], out[1 + 2 * n:1 + 3 * n], out[1 + 3 * n:1 + 4 * n]


def _in_proj(x2, meta_blk, norm_g, w_pad, bf_pad):
    seq = x2.shape[0]
    lp = seq + FRONT
    t = ROW_TILE
    nt = lp // t
    n_sub = t // LANE

    def body(*refs):
        x_refs = refs[:n_sub]
        mb, g_ref, w_ref, bf_ref, tri_ref = refs[n_sub:n_sub + 5]
        q_ref, k_ref, v_ref, rest_ref, fl_ref, ct_ref, u_ref, qt_ref, kt_ref, vt_ref, cc_ref, carry = refs[n_sub + 5:]
        i = pl.program_id(0)

        @pl.when(i == 0)
        def _():
            carry[...] = jnp.zeros_like(carry)

        first = jnp.where(i == 0, mb[...], x_refs[0][...])
        h = jnp.concatenate([first] + [r[...] for r in x_refs[1:]], axis=0)
        ms = jnp.mean(h * h, axis=-1, keepdims=True)
        u = ((h * lax.rsqrt(ms + EPS)) * g_ref[...]).astype(MXU_DTYPE)
        u_ref[...] = u

        def seg(a, width):
            return _dot_nt(u, w_ref[a:a + width, :])

        fl = seg(SEG_F, LANE)
        fl_ref[...] = fl
        q_tile = seg(SEG_Q, D_ATTN) * (HEAD_DIM ** -0.5)
        q_ref[...] = q_tile.astype(MXU_DTYPE)
        qt_ref[...] = q_tile.T.astype(MXU_DTYPE)
        z = fl + bf_ref[...]
        logf = jnp.minimum(z, 0.0) - jnp.log(1.0 + jnp.exp(-jnp.abs(z)))
        row = i * t + lax.broadcasted_iota(jnp.int32, (t, LANE), 0)
        logf = jnp.where(row >= PAD_ROWS, logf, 0.0)
        k_tile = seg(SEG_K, D_ATTN)
        k_ref[...] = k_tile.astype(MXU_DTYPE)
        kt_ref[...] = k_tile.T.astype(MXU_DTYPE)
        cs = _dot_exact(tri_ref[...], logf) + carry[...]
        carry[...] = carry[...] + jnp.sum(logf, axis=0, keepdims=True)
        v_tile = seg(SEG_V, D_ATTN)
        v_ref[...] = v_tile.astype(MXU_DTYPE)
        vt_ref[...] = v_tile.T.astype(MXU_DTYPE)
        col = i * t + lax.broadcasted_iota(jnp.int32, (SUBLANE, t), 1)
        ct_ref[...] = jnp.where(col >= PAD_ROWS, cs.T[0:SUBLANE, :], -NEG)
        cc_ref[...] = jnp.where(row >= PAD_ROWS, cs, -NEG)
        for s in range(5):
            rest_ref[:, 512 * s:512 * (s + 1)] = seg(SEG_ZA + 512 * s, 512)

    row_blk = lambda cols: pl.BlockSpec((t, cols), lambda i: (i, 0))
    tr_blk = pl.BlockSpec((None, D_ATTN, t), lambda i: (i, 0, 0))
    const = lambda shape: pl.BlockSpec(shape, lambda i: (0, 0))
    return pl.pallas_call(
        body, name="in_proj", grid=(nt,),
        in_specs=_x_block_specs(n_sub, LANE) + [const((LANE, D_MODEL)), const((1, D_MODEL)),
                                                pl.BlockSpec((D_IN_PAD, D_MODEL), lambda i: (0, 0),
                                                             pipeline_mode=pl.Buffered(1)),
                                                const((1, LANE)), const((t, t))],
        out_specs=(row_blk(D_ATTN), row_blk(D_ATTN), row_blk(D_ATTN), row_blk(5 * 512), row_blk(LANE),
                   pl.BlockSpec((SUBLANE, t), lambda i: (0, i)), row_blk(D_MODEL), tr_blk, tr_blk, tr_blk, row_blk(LANE)),
        out_shape=(jax.ShapeDtypeStruct((lp, D_ATTN), MXU_DTYPE), jax.ShapeDtypeStruct((lp, D_ATTN), MXU_DTYPE),
                   jax.ShapeDtypeStruct((lp, D_ATTN), MXU_DTYPE), jax.ShapeDtypeStruct((lp, 5 * 512), F32),
                   jax.ShapeDtypeStruct((lp, LANE), F32),
                   jax.ShapeDtypeStruct((SUBLANE, lp), F32), jax.ShapeDtypeStruct((lp, D_MODEL), MXU_DTYPE),
                   jax.ShapeDtypeStruct((nt, D_ATTN, t), MXU_DTYPE), jax.ShapeDtypeStruct((nt, D_ATTN, t), MXU_DTYPE),
                   jax.ShapeDtypeStruct((nt, D_ATTN, t), MXU_DTYPE), jax.ShapeDtypeStruct((lp, LANE), F32)),
        scratch_shapes=[pltpu.VMEM((1, LANE), F32)],
        compiler_params=_params(("arbitrary",)),
    )(*([x2] * n_sub), meta_blk, norm_g, w_pad, bf_pad, _triangle(t, lower=True))


def _head_masks():
    lane = lax.broadcasted_iota(jnp.int32, (1, LANE), 1)
    return lane < HEAD_DIM, lane >= HEAD_DIM


def _pair_specs(lp, nt, t):
    blk = pl.BlockSpec((lp, LANE), lambda g: (0, g))
    ct_a = pl.BlockSpec((None, nt, 1, t), lambda g: (2 * g, 0, 0, 0))
    ct_b = pl.BlockSpec((None, nt, 1, t), lambda g: (2 * g + 1, 0, 0, 0))
    return blk, ct_a, ct_b


def _sub_rows(s, col):
    return jnp.concatenate([s[:, a * LANE:(a + 1) * LANE] - col for a in range(s.shape[1] // LANE)], axis=1)


def _loop_unrolled(lo, hi, step, init, n):
    def group(jj, carry):
        for k in range(n):
            carry = step(lo + n * jj + k, carry)
        return carry

    groups = (hi - lo) // n
    carry = lax.fori_loop(0, groups, group, init)
    return lax.fori_loop(lo + n * groups, hi, step, carry)


def _attn_fwd(q, k, v_t, cc):
    lp = q.shape[0]
    t = ROW_TILE
    nt = lp // t
    ext = LANE + 2 * SUBLANE

    def body(q_ref, k_ref, vt_ref, cc_ref, o_ref, l_ref, m_ref, s_scr, last_scr, m_scr, mfin_scr, acc_scr, c_scr):
        masks = _head_masks()
        lane = lax.broadcasted_iota(jnp.int32, (1, LANE), 1)
        for hh in range(2):
            picked = jnp.where(lane == 2 * pl.program_id(0) + hh, cc_ref[...], 0.0)
            c_scr[hh] = jnp.broadcast_to(jnp.sum(picked, axis=-1, keepdims=True), (lp, LANE))
        visible = lax.broadcasted_iota(jnp.int32, (t, t), 0) <= lax.broadcasted_iota(jnp.int32, (t, t), 1)
        top = lax.broadcasted_iota(jnp.int32, (LANE, 1), 0) < HEAD_DIM
        second_head = (lax.broadcasted_iota(jnp.int32, (2 * SUBLANE, 2 * t), 1) >= t).astype(jnp.int32)
        ones_rows = jnp.where(lax.broadcasted_iota(jnp.int32, (2 * SUBLANE, 2 * t), 0) == second_head,
                              1.0, 0.0).astype(MXU_DTYPE)

        on_first_diagonal = jnp.concatenate([visible, jnp.ones((t, t), jnp.bool_)], axis=1)

        def scores(j, queries):
            kj = k_ref[pl.ds(pl.multiple_of(j * t, t), t), :]
            return _dot_nt(jnp.concatenate([jnp.where(hm, kj, 0).astype(MXU_DTYPE) for hm in masks], axis=0), queries)

        def biased(s2, j, hh):
            return _sub_rows(s2[hh * t:(hh + 1) * t, :], c_scr[hh, pl.ds(pl.multiple_of(j * t, t), t), :]) * LOG2E

        def track_max(hh, s, lo, hi):
            m = m_scr[hh, :, lo:hi]
            for a in range(t // SUBLANE):
                m = jnp.maximum(m, s[a * SUBLANE:(a + 1) * SUBLANE, :])
            m_scr[hh, :, lo:hi] = m

        def probabilities(scores_of, ms_cols):
            return jnp.concatenate([jnp.exp2(scores_of(hh) - ms_cols[hh]).astype(MXU_DTYPE) for hh in range(2)], axis=0)

        def values(j):
            vtj = vt_ref[j]
            v2 = jnp.concatenate([jnp.where(top, vtj, 0).astype(MXU_DTYPE),
                                  jnp.where(top, 0, vtj).astype(MXU_DTYPE)], axis=1)
            return jnp.concatenate([v2, ones_rows], axis=0)

        def stage(done, ahead):
            if ahead is not None:
                i_a, rows_a = ahead
                qa = q_ref[pl.ds(pl.multiple_of(i_a * t, t), rows_a), :]
                m_scr[...] = jnp.full(m_scr.shape, NEG, F32)

                def max_step(j, mask=None):
                    s2 = scores(j, qa)
                    for hh in range(2):
                        s = biased(s2, j, hh)
                        if mask is not None:
                            s = jnp.where(mask, s, NEG)
                        s_scr[j, hh * t:(hh + 1) * t, 0:rows_a] = s
                        track_max(hh, s, 0, rows_a)

            if done is not None:
                i_d, rows_d = done
                r0 = pl.multiple_of(i_d * t, t)
                ms = [mfin_scr[hh, 0:1, 0:rows_d] for hh in range(2)]
                acc_scr[...] = jnp.zeros(acc_scr.shape, F32)

                def key_step(j, carry):
                    p = probabilities(lambda hh: s_scr[j, hh * t:(hh + 1) * t, 0:rows_d], ms)
                    acc_scr[:, 0:rows_d] = acc_scr[:, 0:rows_d] + _dot(values(j), p)
                    if ahead is not None:
                        max_step(j)
                    return carry

                _loop_unrolled(0, i_d + 1, key_step, 0, ATTN_UNROLL)
                if rows_d == 2 * t:
                    p = probabilities(lambda hh: last_scr[hh * t:(hh + 1) * t, :], [m[:, t:] for m in ms])
                    acc_scr[:, t:rows_d] = acc_scr[:, t:rows_d] + _dot(values(i_d + 1), p)
                acc = acc_scr[:, 0:rows_d]
                l_pair = jnp.where(top, acc[LANE:LANE + 1], acc[LANE + 1:LANE + 2])
                o_ref[pl.ds(r0, rows_d), :] = (acc[:LANE] / l_pair).T
                l_ref[pl.ds(r0, rows_d), :] = l_pair.T
                for hh in range(2):
                    m_ref[pl.ds(r0, rows_d), hh * LANE:(hh + 1) * LANE] = jnp.broadcast_to(ms[hh], (LANE, rows_d)).T

            if ahead is not None:
                if done is not None:
                    max_step(i_a - 1)
                max_step(i_a, on_first_diagonal if rows_a == 2 * t else visible)
                if rows_a == 2 * t:
                    s2 = scores(i_a + 1, qa[t:])
                    for hh in range(2):
                        s = jnp.where(visible, biased(s2, i_a + 1, hh), NEG)
                        last_scr[hh * t:(hh + 1) * t, :] = s
                        track_max(hh, s, t, rows_a)
                for hh in range(2):
                    mfin_scr[hh, :, 0:rows_a] = jnp.broadcast_to(jnp.max(m_scr[hh, :, 0:rows_a], axis=0, keepdims=True),
                                                                 (SUBLANE, rows_a))

        pairs = nt // 2
        stage(None, (0, 2 * t))

        def pair_to_pair(u, _):
            stage((2 * u, 2 * t), (2 * u + 2, 2 * t))
            return 0

        lax.fori_loop(0, pairs - 1, pair_to_pair, 0)
        if nt % 2:
            stage((2 * pairs - 2, 2 * t), (nt - 1, t))
            stage((nt - 1, t), None)
        else:
            stage((2 * pairs - 2, 2 * t), None)

    blk = pl.BlockSpec((lp, LANE), lambda g: (0, g))
    return pl.pallas_call(
        body, name="attn_fwd", grid=(HEADS // 2,),
        in_specs=[blk, blk, pl.BlockSpec((nt, LANE, t), lambda g: (0, g, 0)),
                  pl.BlockSpec((lp, LANE), lambda g: (0, 0), pipeline_mode=pl.Buffered(1))],
        out_specs=(blk, blk, pl.BlockSpec((lp, 2 * LANE), lambda g: (0, g))),
        out_shape=(jax.ShapeDtypeStruct((lp, D_ATTN), F32), jax.ShapeDtypeStruct((lp, D_ATTN), F32),
                   jax.ShapeDtypeStruct((lp, HEADS * LANE), F32)),
        scratch_shapes=[pltpu.VMEM((nt, 2 * t, 2 * t), F32), pltpu.VMEM((2 * t, t), F32),
                        pltpu.VMEM((2, SUBLANE, 2 * t), F32), pltpu.VMEM((2, SUBLANE, 2 * t), F32),
                        pltpu.VMEM((ext, 2 * t), F32), pltpu.VMEM((2, lp, LANE), F32)],
        compiler_params=_params(("parallel",)),
    )(q, k, v_t, cc)


def _attn_bwd(q, k, v, do, q_t, k_t, do_t, m, neg_delta, ct4):
    lp = q.shape[0]
    t = ROW_TILE
    nt = lp // t

    def body(q_ref, k_ref, v_ref, do_ref, qt_ref, kt_ref, dot_ref, ma_ref, mb_ref, nd_ref, cta_ref, ctb_ref,
             dq_ref, dk_ref, dv_ref, dc_ref, dq_acc, dk_acc, dv_acc):
        masks = _head_masks()
        ct_refs, m_refs = (cta_ref, ctb_ref), (ma_ref, mb_ref)
        row_head = 2 * pl.program_id(0) + (lax.broadcasted_iota(jnp.int32, (2 * t, LANE), 0) >= t).astype(jnp.int32)
        col = lax.broadcasted_iota(jnp.int32, (2 * t, LANE), 1)
        delta_ones = jnp.where((col < HEADS * DELTA_TERMS) & (col % HEADS == row_head), 1.0, 0.0).astype(MXU_DTYPE)
        below = lax.broadcasted_iota(jnp.int32, (t, t), 1) <= lax.broadcasted_iota(jnp.int32, (t, t), 0)
        top = lax.broadcasted_iota(jnp.int32, (LANE, 1), 0) < HEAD_DIM
        dq_acc[...] = jnp.zeros_like(dq_acc)

        on_first_diagonal = jnp.concatenate([below, jnp.ones((t, t), jnp.bool_)], axis=0)

        def k_block(j, _, with_next=True):
            c0 = pl.multiple_of(j * t, t)
            kj = k_ref[pl.ds(c0, t), :]
            vj = v_ref[pl.ds(c0, t), :]
            k2 = jnp.concatenate([jnp.where(hm, kj, 0).astype(MXU_DTYPE) for hm in masks], axis=0)
            v2 = jnp.concatenate([jnp.where(hm, vj, 0).astype(MXU_DTYPE) for hm in masks], axis=0)
            v2 = jnp.concatenate([v2, delta_ones], axis=1)
            ck = [r[j] for r in ct_refs]
            ktj = kt_ref[j]
            k2t = jnp.concatenate([jnp.where(top, ktj, 0).astype(MXU_DTYPE), jnp.where(top, 0, ktj).astype(MXU_DTYPE)],
                                  axis=1)
            dk_acc[...] = jnp.zeros_like(dk_acc)
            dv_acc[...] = jnp.zeros_like(dv_acc)

            def q_block(i, colsums, mask=None, rows=t):
                r0 = pl.multiple_of(i * t, t)
                qi = q_ref[pl.ds(r0, rows), :]
                doi = jnp.concatenate([do_ref[pl.ds(r0, rows), :], nd_ref[pl.ds(r0, rows), :]], axis=1)
                qti = jnp.concatenate([qt_ref[i + b] for b in range(rows // t)], axis=1)
                doti = jnp.concatenate([dot_ref[i + b] for b in range(rows // t)], axis=1)
                s2 = _dot_nt(qi, k2)
                dp2 = _dot_nt(doi, v2)
                out, ps, dss = [], [], []
                for hh in range(2):
                    s = (s2[:, hh * t:(hh + 1) * t] - ck[hh]) * LOG2E
                    if mask is not None:
                        s = jnp.where(mask, s, NEG)
                    p = jnp.exp2(_sub_rows(s, m_refs[hh][pl.ds(r0, rows), :])).astype(MXU_DTYPE)
                    ds32 = p.astype(F32) * dp2[:, hh * t:(hh + 1) * t]
                    ps.append(p)
                    dss.append(ds32.astype(MXU_DTYPE))
                    out.append(colsums[hh] + jnp.sum(ds32, axis=0, keepdims=True))
                ds_cat = jnp.concatenate(dss, axis=1)
                dv_acc[...] = dv_acc[...] + _dot(doti, jnp.concatenate(ps, axis=1))
                dk_acc[...] = dk_acc[...] + _dot(qti, ds_cat)
                dq_t = _dot(k2t, ds_cat.T)
                for b in range(rows // t):
                    dq_acc[i + b] = dq_acc[i + b] + dq_t[:, b * t:(b + 1) * t]
                return tuple(out)

            nq = ATTN_BWD_QBLOCKS
            colsums = (jnp.zeros((1, t), F32), jnp.zeros((1, t), F32))
            if with_next:
                colsums = q_block(j, colsums, on_first_diagonal, nq * t)
            else:
                colsums = q_block(j, colsums, below)
            first = j + (nq if with_next else 1)
            groups = (nt - first) // nq
            colsums = lax.fori_loop(0, groups, lambda p, c: q_block(first + nq * p, c, None, nq * t), colsums)
            colsums = lax.fori_loop(first + nq * groups, nt, q_block, colsums)
            for hh in range(2):
                dc_ref[hh, j] = -colsums[hh]
            own = lambda acc: jnp.concatenate([acc[:HEAD_DIM, :t], acc[HEAD_DIM:, t:]], axis=0).T
            dk_ref[pl.ds(c0, t), :] = own(dk_acc[...]).astype(dk_ref.dtype)
            dv_ref[pl.ds(c0, t), :] = own(dv_acc[...]).astype(dv_ref.dtype)
            return 0

        lax.fori_loop(0, nt - 1, k_block, 0)
        k_block(nt - 1, 0, with_next=False)
        for i in range(nt):
            dq_ref[i * t:(i + 1) * t, :] = (dq_acc[i].T * (HEAD_DIM ** -0.5)).astype(dq_ref.dtype)

    blk, ct_a, ct_b = _pair_specs(lp, nt, t)
    rep_a = pl.BlockSpec((lp, LANE), lambda g: (0, 2 * g))
    rep_b = pl.BlockSpec((lp, LANE), lambda g: (0, 2 * g + 1))
    tr_blk = pl.BlockSpec((nt, LANE, t), lambda g: (0, g, 0))
    return pl.pallas_call(
        body, name="attn_bwd", grid=(HEADS // 2,),
        in_specs=[blk] * 4 + [tr_blk, tr_blk, tr_blk, rep_a, rep_b, pl.BlockSpec((lp, LANE), lambda g: (0, 0)), ct_a, ct_b],
        out_specs=(blk, blk, blk, pl.BlockSpec((2, nt, 1, t), lambda g: (g, 0, 0, 0))),
        out_shape=(jax.ShapeDtypeStruct((lp, D_ATTN), MXU_DTYPE),) * 3
                  + (jax.ShapeDtypeStruct((HEADS, nt, 1, t), F32),),
        scratch_shapes=[pltpu.VMEM((nt, LANE, t), F32), pltpu.VMEM((LANE, 2 * t), F32), pltpu.VMEM((LANE, 2 * t), F32)],
        compiler_params=_params(("parallel",)),
    )(q, k, v, do, q_t, k_t, do_t, m, m, neg_delta, ct4, ct4)


def _shift_down(prev8, cur, k):
    ext = jnp.concatenate([prev8, cur], axis=0)
    return pltpu.roll(ext, k, 0)[SUBLANE:, :]


def _shift_up(cur, next8, k):
    ext = jnp.concatenate([cur, next8], axis=0)
    n = ext.shape[0]
    return pltpu.roll(ext, n - k, 0)[:cur.shape[0], :]


def _post(o, l_sum, rest, x2, meta_blk, tgt2, w_out, attn_g, conv_g, final_g, conv_w8):
    lp = o.shape[0]
    t = ROW_TILE
    nt = lp // t
    n_sub = t // LANE
    hb = t // SUBLANE

    def body(*refs):
        o_ref, l_ref, za_ref, gb_ref, gc_ref, xc_ref, zc_ref, gch_ref, xch_ref = refs[:9]
        x_refs = refs[9:9 + n_sub]
        mb = refs[9 + n_sub]
        t_refs = refs[10 + n_sub:10 + 2 * n_sub]
        wo_ref, ag_ref, cg_ref, fg_ref, cw_ref, gm_ref, hr_ref = refs[10 + 2 * n_sub:17 + 2 * n_sub]
        (dout_ref, do_ref, dot_ref, dl_ref, dza_ref, dgb_ref, dzc_ref, dcv_ref,
         loss_ref, gf_ref, gag_ref, gcg_ref, gwo_ref) = refs[17 + 2 * n_sub:]
        i = pl.program_id(0)

        @pl.when(i == 0)
        def _():
            for r in (loss_ref, gf_ref, gag_ref, gcg_ref, gwo_ref):
                r[...] = jnp.zeros_like(r)

        gmat = gm_ref[...]
        inv_g = 1.0 / HEAD_DIM
        o_v = o_ref[...]
        ra = lax.rsqrt(_group_sum(o_v * o_v, gmat, STAT_TERMS) * inv_g + EPS)
        n_a = o_v * ra
        a_n = n_a * ag_ref[...]
        za = za_ref[...]
        sig_a = _sigmoid(za)
        sz_a = za * sig_a
        y_a = a_n * sz_a
        gb = gb_ref[...]
        gc = gc_ref[...]
        xc = xc_ref[...]
        cx = gc * xc
        cx_prev = jnp.where(i == 0, 0.0, gch_ref[...] * xch_ref[...])
        conv = (cw_ref[0:1, :] * _shift_down(cx_prev, cx, 2) + cw_ref[1:2, :] * _shift_down(cx_prev, cx, 1)
                + cw_ref[2:3, :] * cx)
        e = gb * conv
        re = lax.rsqrt(_group_sum(e * e, gmat, STAT_TERMS) * inv_g + EPS)
        n_e = e * re
        e_n = n_e * cg_ref[...]
        zc = zc_ref[...]
        sig_c = _sigmoid(zc)
        sz_c = zc * sig_c
        y_c = e_n * sz_c
        mix = jnp.concatenate([y_a, y_c], axis=-1)
        mix_b = mix.astype(MXU_DTYPE)
        first = jnp.where(i == 0, mb[...], x_refs[0][...])
        h = jnp.concatenate([first] + [r[...] for r in x_refs[1:]], axis=0)
        out = h + _dot(mix_b, wo_ref[...])
        r2 = lax.rsqrt(jnp.mean(out * out, axis=-1, keepdims=True) + EPS)
        n_f = out * r2
        y = n_f * fg_ref[...]
        tgt = jnp.concatenate([r[...] for r in t_refs], axis=0)
        valid = (i * t + lax.broadcasted_iota(jnp.int32, (t, 1), 0)) >= FRONT
        diff = jnp.where(valid, y - tgt, 0.0)
        loss_ref[...] = loss_ref[...] + 0.5 * jnp.sum(jnp.sum(diff * diff, axis=-1, keepdims=True) * (1.0 / D_MODEL))
        dy = diff * (1.0 / D_MODEL)
        gf_ref[...] = gf_ref[...] + jnp.sum(dy * n_f, axis=0, keepdims=True)
        dn = dy * fg_ref[...]
        d_out = r2 * (dn - n_f * jnp.mean(dn * n_f, axis=-1, keepdims=True))
        dout_ref[...] = d_out
        d_out_b = d_out.astype(MXU_DTYPE)
        d_mix = _dot_nt(d_out_b, wo_ref[...])
        gwo_ref[...] = gwo_ref[...] + _dot(mix.T.astype(MXU_DTYPE), d_out_b)
        d_ya = d_mix[:, :D_ATTN]
        d_yc = d_mix[:, D_ATTN:]
        d_an = d_ya * sz_a
        dza_ref[...] = (d_ya * a_n * (sig_a * (1.0 + za * (1.0 - sig_a)))).astype(dza_ref.dtype)
        gag_ref[...] = gag_ref[...] + jnp.sum(d_an * n_a, axis=0, keepdims=True)
        dn_a = d_an * ag_ref[...]
        d_o = ra * (dn_a - n_a * (_group_sum(dn_a * n_a, gmat, STAT_TERMS) * inv_g))
        d_o_l = d_o / l_ref[...]
        d_o_b = d_o_l.astype(do_ref.dtype)
        do_ref[...] = d_o_b
        dot_ref[...] = d_o_l.T.astype(dot_ref.dtype)
        delta = _group_sum(d_o_b.astype(F32) * o_v, hr_ref[...])
        terms, rest_of = [], delta
        for k in range(DELTA_TERMS):
            terms.append(rest_of.astype(MXU_DTYPE).astype(F32))
            rest_of = rest_of - terms[-1]
        dl_ref[...] = -sum(pltpu.roll(term, HEADS * k, 1) if k else term
                           for k, term in enumerate(terms)).astype(dl_ref.dtype)
        d_en = d_yc * sz_c
        dzc_ref[...] = (d_yc * e_n * (sig_c * (1.0 + zc * (1.0 - sig_c)))).astype(dzc_ref.dtype)
        gcg_ref[...] = gcg_ref[...] + jnp.sum(d_en * n_e, axis=0, keepdims=True)
        dn_e = d_en * cg_ref[...]
        d_e = re * (dn_e - n_e * (_group_sum(dn_e * n_e, gmat, STAT_TERMS) * inv_g))
        dgb_ref[...] = (d_e * conv).astype(dgb_ref.dtype)
        dcv_ref[...] = d_e * gb

    head_rep = jnp.where((lax.broadcasted_iota(jnp.int32, (D_ATTN, LANE), 0) >> 6)
                         == lax.broadcasted_iota(jnp.int32, (D_ATTN, LANE), 1), 1.0, 0.0).astype(MXU_DTYPE)
    row_blk = lambda cols: pl.BlockSpec((t, cols), lambda i: (i, 0))
    rest_blk = lambda s: pl.BlockSpec((t, 512), functools.partial(lambda i, s: (i, s), s=s))
    halo = lambda s: pl.BlockSpec((SUBLANE, 512), functools.partial(lambda i, s: (jnp.maximum(i * hb - 1, 0), s), s=s))
    const = lambda shape: pl.BlockSpec(shape, lambda i: (0, 0))
    acc = lambda shape: pl.BlockSpec(shape, lambda i: (0, 0))
    return pl.pallas_call(
        body, name="post_fwd_bwd", grid=(nt,),
        in_specs=[row_blk(D_ATTN), row_blk(D_ATTN)] + [rest_blk(s) for s in range(5)] + [halo(2), halo(3)]
                 + _x_block_specs(n_sub, LANE) + [const((LANE, D_MODEL))] + _x_block_specs(n_sub, LANE)
                 + [const((D_MODEL, D_MODEL)), const((1, D_ATTN)), const((1, D_CONV)), const((1, D_MODEL)),
                    const((SUBLANE, D_CONV)), const((D_ATTN, D_ATTN)), const((D_ATTN, LANE))],
        out_specs=(row_blk(D_MODEL), row_blk(D_ATTN), pl.BlockSpec((None, D_ATTN, t), lambda i: (i, 0, 0)),
                   row_blk(LANE), row_blk(D_ATTN), row_blk(D_CONV),
                   row_blk(D_CONV), row_blk(D_CONV),
                   acc((1, LANE)), acc((1, D_MODEL)), acc((1, D_ATTN)), acc((1, D_CONV)), acc((D_MODEL, D_MODEL))),
        out_shape=(jax.ShapeDtypeStruct((lp, D_MODEL), F32), jax.ShapeDtypeStruct((lp, D_ATTN), MXU_DTYPE),
                   jax.ShapeDtypeStruct((nt, D_ATTN, t), MXU_DTYPE), jax.ShapeDtypeStruct((lp, LANE), MXU_DTYPE),
                   jax.ShapeDtypeStruct((lp, D_ATTN), MXU_DTYPE),
                   jax.ShapeDtypeStruct((lp, D_CONV), MXU_DTYPE), jax.ShapeDtypeStruct((lp, D_CONV), MXU_DTYPE),
                   jax.ShapeDtypeStruct((lp, D_CONV), F32),
                   jax.ShapeDtypeStruct((1, LANE), F32), jax.ShapeDtypeStruct((1, D_MODEL), F32),
                   jax.ShapeDtypeStruct((1, D_ATTN), F32), jax.ShapeDtypeStruct((1, D_CONV), F32),
                   jax.ShapeDtypeStruct((D_MODEL, D_MODEL), F32)),
        compiler_params=_params(("arbitrary",)),
    )(o, l_sum, *([rest] * 5), rest, rest, *([x2] * n_sub), meta_blk, *([tgt2] * n_sub),
      w_out, attn_g, conv_g, final_g, conv_w8, _group_matrix(), head_rep)


def _bwd_in(x2, meta_blk, norm_g, w_pad, bf_pad, fl, dc, dq, dk, dv, dza, dgb, dzc, dconv, rest, d_out, conv_w8):
    lp = fl.shape[0]
    t = ROW_TILE
    nt = lp // t
    n_sub = t // LANE
    hb = t // SUBLANE
    rev = lambda i: nt - 1 - i

    def body(*refs):
        x_refs = refs[:n_sub]
        (mb, g_ref, w_ref, bf_ref, fl_ref, dc_ref, dq_ref, dk_ref, dv_ref, dza_ref, dgb_ref, dzc_ref,
         dcv_ref, dcvn_ref, gc_ref, xc_ref, gch_ref, xch_ref, dout_ref, cw_ref, tri_ref) = refs[n_sub:n_sub + 21]
        dp_ref, gx_ref, front_ref, gn_ref, gbf_ref, gcw_ref, carry, dh_scr, gx_sems = refs[n_sub + 21:]
        step = pl.program_id(0)
        i = rev(step)

        @pl.when(step == 0)
        def _():
            for r in (gn_ref, gbf_ref, gcw_ref, carry):
                r[...] = jnp.zeros_like(r)

        dc8 = jnp.concatenate([dc_ref[...], jnp.zeros((LANE - HEADS, t), F32)], axis=0).T
        dlogf = _dot_exact(tri_ref[...], dc8) + carry[...]
        carry[...] = carry[...] + jnp.sum(dc8, axis=0, keepdims=True)
        z = fl_ref[...] + bf_ref[...]
        row = i * t + lax.broadcasted_iota(jnp.int32, (t, LANE), 0)
        d_f = jnp.where(row >= PAD_ROWS, dlogf * (1.0 / (1.0 + jnp.exp(z))), 0.0)
        gbf_ref[...] = gbf_ref[...] + jnp.sum(d_f, axis=0, keepdims=True)
        dcv = dcv_ref[...]
        dcv_next = jnp.where(i == nt - 1, 0.0, dcvn_ref[...])
        d_cx = (cw_ref[2:3, :] * dcv + cw_ref[1:2, :] * _shift_up(dcv, dcv_next, 1)
                + cw_ref[0:1, :] * _shift_up(dcv, dcv_next, 2))
        gc = gc_ref[...]
        xc = xc_ref[...]
        cx = gc * xc
        cx_prev = jnp.where(i == 0, 0.0, gch_ref[...] * xch_ref[...])
        rowi = lax.broadcasted_iota(jnp.int32, (SUBLANE, 1), 0)
        gcw = (jnp.where(rowi == 0, jnp.sum(dcv * _shift_down(cx_prev, cx, 2), axis=0, keepdims=True), 0.0)
               + jnp.where(rowi == 1, jnp.sum(dcv * _shift_down(cx_prev, cx, 1), axis=0, keepdims=True), 0.0)
               + jnp.where(rowi == 2, jnp.sum(dcv * cx, axis=0, keepdims=True), 0.0))
        gcw_ref[...] = gcw_ref[...] + gcw
        dp_ref[:, SEG_Q:SEG_Q + 512] = dq_ref[...]
        dp_ref[:, SEG_K:SEG_K + 512] = dk_ref[...]
        dp_ref[:, SEG_V:SEG_V + 512] = dv_ref[...]
        dp_ref[:, SEG_F:SEG_F + LANE] = d_f.astype(dp_ref.dtype)
        dp_ref[:, SEG_ZA:SEG_ZA + 512] = dza_ref[...]
        dp_ref[:, SEG_GB:SEG_GB + 512] = dgb_ref[...]
        dp_ref[:, SEG_GC:SEG_GC + 512] = (d_cx * xc).astype(dp_ref.dtype)
        dp_ref[:, SEG_XC:SEG_XC + 512] = (d_cx * gc).astype(dp_ref.dtype)
        dp_ref[:, SEG_ZC:SEG_ZC + 512] = dzc_ref[...]
        d_u = _dot(dp_ref[...], w_ref[...])
        first = jnp.where(i == 0, mb[...], x_refs[0][...])
        h = jnp.concatenate([first] + [r[...] for r in x_refs[1:]], axis=0)
        r1 = lax.rsqrt(jnp.mean(h * h, axis=-1, keepdims=True) + EPS)
        n_h = h * r1
        gn_ref[...] = gn_ref[...] + jnp.sum(d_u * n_h, axis=0, keepdims=True)
        dn = d_u * g_ref[...]
        d_h = dout_ref[...] + r1 * (dn - n_h * jnp.mean(dn * n_h, axis=-1, keepdims=True))
        slot = step % 2

        def to_grad_x(slot_, tile):
            return pltpu.make_async_copy(dh_scr.at[slot_], gx_ref.at[pl.ds(pl.multiple_of(tile * t - FRONT, SUBLANE), t)],
                                         gx_sems.at[slot_])

        @pl.when(step >= 2)
        def _():
            to_grad_x(slot, 1).wait()

        dh_scr[slot] = d_h

        @pl.when(i > 0)
        def _():
            to_grad_x(slot, i).start()

        @pl.when(i == 0)
        def _():
            front_ref[...] = d_h[:FRONT]
            rest_rows = pltpu.make_async_copy(dh_scr.at[slot, pl.ds(FRONT, t - FRONT)], gx_ref.at[pl.ds(0, t - FRONT)],
                                              gx_sems.at[slot])
            rest_rows.start()
            rest_rows.wait()
            if nt >= 2:
                to_grad_x(1 - slot, 1).wait()

    def x_specs():
        specs = [pl.BlockSpec((LANE, D_MODEL), lambda s: (jnp.maximum(n_sub * rev(s) - 1, 0), 0))]
        for b in range(1, n_sub):
            specs.append(pl.BlockSpec((LANE, D_MODEL), functools.partial(lambda s, b: (n_sub * rev(s) - 1 + b, 0), b=b)))
        return specs

    row_blk = lambda cols: pl.BlockSpec((t, cols), lambda s: (rev(s), 0))
    rest_blk = lambda k: pl.BlockSpec((t, 512), functools.partial(lambda s, k: (rev(s), k), k=k))
    halo_prev = lambda k: pl.BlockSpec(
        (SUBLANE, 512), functools.partial(lambda s, k: (jnp.maximum(rev(s) * hb - 1, 0), k), k=k))
    halo_next = pl.BlockSpec((SUBLANE, 512), lambda s: (jnp.minimum((rev(s) + 1) * hb, lp // SUBLANE - 1), 0))
    const = lambda shape: pl.BlockSpec(shape, lambda s: (0, 0))
    return pl.pallas_call(
        body, name="bwd_in", grid=(nt,),
        in_specs=x_specs() + [const((LANE, D_MODEL)), const((1, D_MODEL)),
                              pl.BlockSpec((D_IN_PAD, D_MODEL), lambda s: (0, 0), pipeline_mode=pl.Buffered(1)),
                              const((1, LANE)), row_blk(LANE),
                              pl.BlockSpec((HEADS, t), lambda s: (0, rev(s))),
                              row_blk(512), row_blk(512), row_blk(512), row_blk(512), row_blk(512), row_blk(512),
                              row_blk(512), halo_next, rest_blk(2), rest_blk(3), halo_prev(2), halo_prev(3),
                              row_blk(D_MODEL), const((SUBLANE, D_CONV)), const((t, t))],
        out_specs=(row_blk(D_IN_PAD), ANY, const((FRONT, D_MODEL)), const((1, D_MODEL)), const((1, LANE)),
                   const((SUBLANE, D_CONV))),
        out_shape=(jax.ShapeDtypeStruct((lp, D_IN_PAD), MXU_DTYPE), jax.ShapeDtypeStruct((lp - FRONT, D_MODEL), F32),
                   jax.ShapeDtypeStruct((FRONT, D_MODEL), F32),
                   jax.ShapeDtypeStruct((1, D_MODEL), F32), jax.ShapeDtypeStruct((1, LANE), F32),
                   jax.ShapeDtypeStruct((SUBLANE, D_CONV), F32)),
        scratch_shapes=[pltpu.VMEM((1, LANE), F32), pltpu.VMEM((2, t, D_MODEL), F32), pltpu.SemaphoreType.DMA((2,))],
        compiler_params=_params(("arbitrary",)),
    )(*([x2] * n_sub), meta_blk, norm_g, w_pad, bf_pad, fl, dc, dq, dk, dv, dza, dgb, dzc, dconv, dconv,
      rest, rest, rest, rest, d_out, conv_w8, _triangle(t, lower=False))


def _grad_w_in(u, dproj):
    lp = u.shape[0]
    tn = GW_COL_TILE
    tk = tn if lp % tn == 0 else ROW_TILE

    def body(d_ref, u_ref, o_ref, wire_ref):
        k = pl.program_id(1)

        @pl.when(k == 0)
        def _():
            o_ref[...] = jnp.zeros_like(o_ref)

        o_ref[...] = o_ref[...] + lax.dot_general(d_ref[...], u_ref[...], (((0,), (0,)), ((), ())),
                                                  preferred_element_type=F32)

        @pl.when(k == pl.num_programs(1) - 1)
        def _():
            wire_ref[...] = o_ref[...].astype(wire_ref.dtype)

    out_spec = pl.BlockSpec((tn, D_MODEL), lambda n, k: (n, 0))
    return pl.pallas_call(
        body, name="grad_w_in", grid=(D_IN_PAD // tn, lp // tk),
        in_specs=[pl.BlockSpec((tk, tn), lambda n, k: (k, n)), pl.BlockSpec((tk, D_MODEL), lambda n, k: (k, 0))],
        out_specs=(out_spec, out_spec),
        out_shape=(jax.ShapeDtypeStruct((D_IN_PAD, D_MODEL), F32), jax.ShapeDtypeStruct((D_IN_PAD, D_MODEL), WIRE_DTYPE)),
        compiler_params=_params(("parallel", "arbitrary")),
    )(dproj, u)


def _by_chip(own, others, me):
    by_mask = jnp.stack([own, others[1], others[0], others[2]])
    return [lax.dynamic_index_in_dim(by_mask, jnp.bitwise_xor(me, s), 0, keepdims=False) for s in range(N_CHIPS)]


def _both_halves(mine, other, c):
    return jnp.where(c == 0, jnp.concatenate([mine, other], axis=0), jnp.concatenate([other, mine], axis=0))


def _local_step(x2, tgt2, meta_full, norm_g, w_pad, b_f, conv_w_full, attn_g, conv_g, w_out_full, final_g):
    lp = x2.shape[0] + FRONT
    nt = lp // ROW_TILE
    meta_blk = jnp.concatenate([jnp.zeros((PAD_ROWS, D_MODEL), F32), meta_full], axis=0)
    bf_pad = jnp.pad(b_f, ((0, 0), (0, LANE - HEADS)))
    conv_w8 = jnp.pad(conv_w_full, ((0, SUBLANE - conv_w_full.shape[0]), (0, 0)))
    q, k, v, rest, fl, ct, u, q_t, k_t, v_t, cc = _in_proj(x2, meta_blk, norm_g, w_pad, bf_pad)
    ct4 = ct.reshape(SUBLANE, nt, 1, ROW_TILE)
    o, l_sum, m_max = _attn_fwd(q, k, v_t, cc)
    (d_out, d_o, do_t, neg_delta, dza, dgb, dzc, dconv, loss, g_final, g_attn, g_convg, gw_out) = _post(
        o, l_sum, rest, x2, meta_blk, tgt2, w_out_full, attn_g, conv_g, final_g, conv_w8)
    dq, dk, dv, dc = _attn_bwd(q, k, v, d_o, q_t, k_t, do_t, m_max, neg_delta, ct4)
    dproj, grad_x, d_front, g_norm, g_bf, g_cw = _bwd_in(x2, meta_blk, norm_g, w_pad, bf_pad, fl, dc.reshape(HEADS, lp), dq, dk, dv,
                                             dza, dgb, dzc, dconv, rest, d_out, conv_w8)
    gw_in, gw_in_wire = _grad_w_in(u, dproj)
    return dict(loss=loss, grad_x=grad_x, d_front=d_front, g_norm=g_norm, g_final=g_final, g_attn=g_attn, g_convg=g_convg, g_bf=g_bf,
                g_cw=g_cw, gw_out=gw_out, gw_in=gw_in, gw_in_wire=gw_in_wire)


def kernel(x, meta, norm_g, w_in, b_f, conv_w, attn_norm_g, conv_norm_g, w_out, final_norm_g, loss_target, m_meta, m_norm_g, m_w_in, m_b_f, m_conv_w, m_attn_norm_g, m_conv_norm_g, m_w_out, m_final_norm_g, v_meta, v_norm_g, v_w_in, v_b_f, v_conv_w, v_attn_norm_g, v_conv_norm_g, v_w_out, v_final_norm_g):
    cx_, cy_, cc_ = _position()
    chip = 2 * cx_ + cy_
    shard = w_in.shape[2]
    out_half = w_out.shape[1] // 2
    pick = lambda vals: jnp.where(chip == 0, vals[0], jnp.where(chip == 1, vals[1], jnp.where(chip == 2, vals[2], vals[3])))
    a_off, b_off = pick(A_OFF), pick(B_OFF)
    wt = jnp.transpose(w_in[0]).astype(MXU_DTYPE)

    def placed(piece, off):
        return lax.dynamic_slice_in_dim(jnp.pad(piece, ((WIN_ROWS, WIN_ROWS), (0, 0))), WIN_ROWS - off, WIN_ROWS, 0)

    wi = placed(wt[:PIECE_A], a_off) + placed(wt[PIECE_A:], b_off)
    wo = w_out[0].astype(MXU_DTYPE)
    small = jnp.concatenate([meta, jnp.pad(conv_w[0], ((0, 8 - conv_w.shape[1]), (0, meta.shape[1] - conv_w.shape[2])))],
                            axis=0)
    gwi, gwo, gsm = _gather_weights(wi.reshape(2, WIN_HALF, D_MODEL), wo.reshape(2, out_half, D_MODEL), small)
    starts = jnp.stack([_window_start(jnp.bitwise_xor(chip, mask)) for mask in (0, 2, 1, 3)]).astype(jnp.int32)
    w_pad = _assemble_w(wi, gwi.reshape(3, WIN_ROWS, D_MODEL), starts)
    w_out_full = jnp.concatenate(_by_chip(wo, gwo.reshape(3, 2 * out_half, D_MODEL), chip), axis=0)
    small_full = jnp.concatenate(_by_chip(small, gsm, chip), axis=1)
    meta_full = small_full[:N_META]
    conv_w_full = jnp.concatenate([small_full[N_META:N_META + 3, 256 * s:256 * s + LANE] for s in range(N_CHIPS)], axis=1)
    final_g2 = final_norm_g.reshape(1, D_MODEL)
    r = _local_step(x[0], loss_target[0], meta_full, norm_g, w_pad, b_f, conv_w_full, attn_norm_g, conv_norm_g,
                    w_out_full, final_g2)
    grad_x = r["grad_x"][None]
    gb = r["gw_out"].reshape(N_CHIPS, 2, out_half, D_MODEL)
    wide = lambda a: jnp.pad(a, ((0, 0), (0, D_MODEL - a.shape[1])))
    pack = jnp.concatenate([
        r["g_norm"], r["g_final"], jnp.concatenate([r["g_attn"], r["g_convg"]], axis=1), wide(r["g_bf"]),
        wide(r["loss"]), jnp.zeros((3, D_MODEL), F32), r["d_front"][PAD_ROWS:], wide(r["g_cw"])], axis=0)
    ra, rb, packs = _pair_exchange(r["gw_in_wire"], gb, pack)
    c_idx = jnp.reshape(cc_, (1,)).astype(jnp.int32)
    chip_idx = jnp.reshape(chip, (1,)).astype(jnp.int32)
    pa, pa_wire = _pair_sum_windows(r["gw_in"], ra, c_idx)
    pb, pb_wire = _pair_sum(gb, rb, c_idx)
    xa, xb = _chip_exchange(pa_wire, pb_wire)
    ha = _chip_sum(pa, xa, chip_idx)
    hb = _chip_sum(pb, xb, chip_idx)
    oa, ob = _pair_share(ha, hb)
    g_window = _both_halves(ha, oa, cc_)
    g_w_in_t = jnp.concatenate([lax.dynamic_slice_in_dim(g_window, a_off, PIECE_A, 0),
                                lax.dynamic_slice_in_dim(g_window, b_off, shard - PIECE_A, 0)], axis=0)
    g_w_out = _both_halves(hb, ob, cc_)
    as_rows = lambda a: jnp.transpose(a, (2, 0, 1))
    g_w_in, d_w_in, nm_w_in, nv_w_in = (jnp.transpose(a, (1, 2, 0)) for a in _adamw_rows(
        as_rows(w_in), g_w_in_t, as_rows(m_w_in), as_rows(v_w_in)))
    d_w_out, nm_w_out, nv_w_out = (a[None] for a in _adamw_big(w_out[0], g_w_out, m_w_out[0], v_w_out[0], LANE))
    params = (norm_g, final_g2, attn_norm_g, conv_norm_g, b_f, meta, conv_w[0])
    ms = (m_norm_g, m_final_norm_g.reshape(1, D_MODEL), m_attn_norm_g, m_conv_norm_g, m_b_f, m_meta, m_conv_w[0])
    vs = (v_norm_g, v_final_norm_g.reshape(1, D_MODEL), v_attn_norm_g, v_conv_norm_g, v_b_f, v_meta, v_conv_w[0])
    loss, g_s, d_s, m_s, v_s = _small_update(pack, packs, params, ms, vs)

    def ordered(small_list, big_in, big_out):
        s_norm, s_final, s_attn, s_convg, s_bf, s_meta, s_cw = small_list
        return (s_meta, s_norm, big_in, s_bf, s_cw[None], s_attn, s_convg, big_out, s_final.reshape(D_MODEL))

    return (loss.reshape(()), grad_x,
            *ordered(g_s, g_w_in, g_w_out[None]), *ordered(d_s, d_w_in, d_w_out),
            *ordered(m_s, nm_w_in, nm_w_out), *ordered(v_s, nv_w_in, nv_w_out))
```

```python
import functools

import jax
import jax.numpy as jnp
from jax import lax
from jax.experimental import pallas as pl
from jax.experimental.pallas import tpu as pltpu

F32 = jnp.float32
MXU_DTYPE = jnp.bfloat16
WIRE_DTYPE = jnp.bfloat16

D_MODEL = 1024
N_META = 16
HEADS = 8
HEAD_DIM = 64
D_ATTN = HEADS * HEAD_DIM
D_CONV = 512
EPS = 1e-6
LANE = 128
SUBLANE = 8
ROW_TILE = 384
ATTN_UNROLL = 3
ATTN_BWD_QBLOCKS = 2
DELTA_TERMS = 3
ADAM_ROW_STEPS = 3
STAT_TERMS = 1
FRONT = LANE
PAD_ROWS = FRONT - N_META
NEG = -1e30
LOG2E = 1.4426950408889634
N_CHIPS = 4
N_DEV = 8
VMEM_LIMIT_BYTES = 60 * 1024 * 1024

SEG_Q, SEG_K, SEG_V, SEG_F, SEG_ZA, SEG_GB, SEG_GC, SEG_XC, SEG_ZC = (
    0, 512, 1024, 1536, 1664, 2176, 2688, 3200, 3712)
D_IN = 4104
D_IN_PAD = 4224
F_END = 1544
GW_COL_TILE = 1408
WIN_ROWS = 1152
WIN_HALF = WIN_ROWS // 2
WIN_START = (0, 1024, 2160, 3072)
PIECE_A = 518
A_OFF = (0, 2, 12, 126)
B_OFF = (518, 640, 530, 644)
ADAM_LR = 0.001
ADAM_B1 = 0.9
ADAM_B2 = 0.999
ADAM_EPS = 1e-08
ADAM_WD = 0.01
ADAM_STEP = 10

MESH = pl.DeviceIdType.MESH
ANY = pl.BlockSpec(memory_space=pl.ANY)

PACK_ROWS = 32
SLOT_NORM = (0, 1, 0, 1024)
SLOT_FINAL = (1, 2, 0, 1024)
SLOT_ATTN = (2, 3, 0, 512)
SLOT_CONVG = (2, 3, 512, 1024)
SLOT_BF = (3, 4, 0, 8)
SLOT_META = (8, 24, 0, 256)
SLOT_CONVW = (24, 27, 0, 128)
LOSS_ROW = 4


def _params(sem=None):
    return pltpu.CompilerParams(dimension_semantics=sem, vmem_limit_bytes=VMEM_LIMIT_BYTES)


def _sigmoid(z):
    return 1.0 / (1.0 + jnp.exp(-z))


def _dot(a, b):
    return jnp.dot(a, b, preferred_element_type=F32)


def _dot_nt(a, b):
    return lax.dot_general(a, b, (((1,), (1,)), ((), ())), preferred_element_type=F32)


def _dot_exact(ones, x):
    ones = ones.astype(MXU_DTYPE)
    total = None
    for _ in range(3):
        term = x.astype(MXU_DTYPE)
        x = x - term.astype(F32)
        total = _dot(ones, term) if total is None else total + _dot(ones, term)
    return total


def _group_matrix():
    r = lax.broadcasted_iota(jnp.int32, (D_ATTN, D_ATTN), 0) >> 6
    c = lax.broadcasted_iota(jnp.int32, (D_ATTN, D_ATTN), 1) >> 6
    return jnp.where(r == c, 1.0, 0.0).astype(MXU_DTYPE)


def _triangle(n, lower):
    r = lax.broadcasted_iota(jnp.int32, (n, n), 0)
    c = lax.broadcasted_iota(jnp.int32, (n, n), 1)
    return jnp.where((r >= c) if lower else (c >= r), 1.0, 0.0).astype(MXU_DTYPE)


def _group_sum(x, gmat, terms=2):
    hi = x.astype(MXU_DTYPE)
    if terms == 1:
        return _dot(hi, gmat)
    lo = (x - hi.astype(F32)).astype(MXU_DTYPE)
    return _dot(hi, gmat) + _dot(lo, gmat)


def _x_block_specs(n_sub, rows):
    specs = [pl.BlockSpec((rows, D_MODEL), lambda i: (jnp.maximum(n_sub * i - 1, 0), 0))]
    for b in range(1, n_sub):
        specs.append(pl.BlockSpec((rows, D_MODEL), functools.partial(lambda i, b: (n_sub * i - 1 + b, 0), b=b)))
    return specs


def _position():
    return lax.axis_index("x"), lax.axis_index("y"), lax.axis_index("c")


def _gather_weights(wi, wo, small):
    def body(wi_ref, wo_ref, sm_ref, gwi_ref, gwo_ref, gsm_ref, send_sems, recv_sems):
        x, y, c = _position()
        sibling = (x, y, 1 - c)
        chips = [(1 - x, y), (x, 1 - y), (1 - x, 1 - y)]

        def remote(k, src, dst, to):
            return pltpu.make_async_remote_copy(src_ref=src, dst_ref=dst, send_sem=send_sems.at[k],
                                                recv_sem=recv_sems.at[k], device_id=to, device_id_type=MESH)

        first, passed, landed = [], [], []
        for a, (src_ref, g_ref) in enumerate(((wi_ref, gwi_ref), (wo_ref, gwo_ref))):
            for j, (cx, cy) in enumerate(chips):
                slot = g_ref.at[j, c]
                first.append(remote(6 * a + j, src_ref.at[c], slot, (cx, cy, c)))
                landed.append(remote(6 * a + j, slot, slot, sibling))
                passed.append(remote(6 * a + 3 + j, slot, slot, sibling))
        for j, (cx, cy) in enumerate(chips):
            first.append(remote(12 + j, sm_ref, gsm_ref.at[j], (cx, cy, c)))
        for cp in first:
            cp.start()
        for arrived, onward in zip(landed, passed):
            arrived.wait_recv()
            onward.start()
        for a, g_ref in enumerate((gwi_ref, gwo_ref)):
            for j in range(3):
                remote(6 * a + 3 + j, g_ref.at[j, 1 - c], g_ref.at[j, 1 - c], sibling).wait_recv()
        for j in range(3):
            remote(12 + j, sm_ref, gsm_ref.at[j], sibling).wait_recv()
        for cp in first + passed:
            cp.wait_send()

    return pl.pallas_call(
        body, name="gather_weights",
        out_shape=(jax.ShapeDtypeStruct((3,) + wi.shape, wi.dtype), jax.ShapeDtypeStruct((3,) + wo.shape, wo.dtype),
                   jax.ShapeDtypeStruct((3,) + small.shape, small.dtype)),
        in_specs=[ANY, ANY, ANY], out_specs=(ANY, ANY, ANY),
        scratch_shapes=[pltpu.SemaphoreType.DMA((15,)), pltpu.SemaphoreType.DMA((15,))],
    )(wi, wo, small)


def _pair_exchange(gw, gb, pack):
    n_big = N_CHIPS + 1

    def body(gw_ref, gb_ref, p_ref, ra_ref, rb_ref, o_ref, send_sems, recv_sems):
        x, y, c = _position()
        sibling = (x, y, 1 - c)

        def remote(k, src, dst, to):
            return pltpu.make_async_remote_copy(src_ref=src, dst_ref=dst, send_sem=send_sems.at[k],
                                                recv_sem=recv_sems.at[k], device_id=to, device_id_type=MESH)

        copies = [remote(N_CHIPS, gb_ref.at[:, 1 - c], rb_ref, sibling)]
        for s, start in enumerate(WIN_START):
            rows = pl.ds(pl.multiple_of(start + WIN_HALF * (1 - c), 2 * SUBLANE), WIN_HALF)
            copies.append(remote(s, gw_ref.at[rows], ra_ref.at[s], sibling))
        for mask in range(1, N_DEV):
            peer = (1 - x if mask & 4 else x, 1 - y if mask & 2 else y, 1 - c if mask & 1 else c)
            copies.append(remote(n_big + mask - 1, p_ref, o_ref.at[mask - 1], peer))
        for cp in copies:
            cp.start()
        for cp in copies:
            cp.wait()

    n_sems = n_big + N_DEV - 1
    return pl.pallas_call(
        body, name="grad_pair_exchange",
        out_shape=(jax.ShapeDtypeStruct((N_CHIPS, WIN_HALF, D_MODEL), gw.dtype),
                   jax.ShapeDtypeStruct((N_CHIPS,) + gb.shape[2:], gb.dtype),
                   jax.ShapeDtypeStruct((N_DEV - 1,) + pack.shape, pack.dtype)),
        in_specs=[ANY, ANY, ANY], out_specs=(ANY, ANY, ANY),
        scratch_shapes=[pltpu.SemaphoreType.DMA((n_sems,)), pltpu.SemaphoreType.DMA((n_sems,))],
    )(gw, gb, pack)


def _chip_exchange(gw, recv, pb):
    half, cols = recv.shape[1:]

    def body(gw_ref, rv_ref, pb_ref, own_ref, ra_ref, rb_ref, g_buf, r_buf, wire_buf, own_buf,
             in_sems, out_sem, send_sems, recv_sems):
        x, y, c = _position()
        chips = [(1 - x, y), (x, 1 - y), (1 - x, 1 - y)]
        windows = [2 * cx + cy for cx, cy in chips] + [2 * x + y]

        def fetch(k):
            slot = k % 2
            rows = pl.ds(pl.multiple_of(_window_start(windows[k]) + half * c, SUBLANE), half)
            return (pltpu.make_async_copy(gw_ref.at[rows], g_buf.at[slot], in_sems.at[0, slot]),
                    pltpu.make_async_copy(rv_ref.at[windows[k]], r_buf.at[slot], in_sems.at[1, slot]))

        def to_owner(src, dst, j, sem):
            cx, cy = chips[j]
            return pltpu.make_async_remote_copy(src_ref=src, dst_ref=dst.at[j], send_sem=send_sems.at[sem],
                                                recv_sem=recv_sems.at[sem], device_id=(cx, cy, c), device_id_type=MESH)

        for cp in fetch(0):
            cp.start()
        remote = [to_owner(pb_ref.at[windows[j]], rb_ref, j, 3 + j) for j in range(3)]
        for cp in remote:
            cp.start()
        for k in range(4):
            if k + 1 < 4:
                for cp in fetch(k + 1):
                    cp.start()
            for cp in fetch(k):
                cp.wait()
            total = g_buf[k % 2] + r_buf[k % 2].astype(F32)
            if k < 3:
                wire_buf[k] = total.astype(wire_buf.dtype)
                remote.append(to_owner(wire_buf.at[k], ra_ref, k, k))
                remote[-1].start()
            else:
                own_buf[...] = total
        keep = pltpu.make_async_copy(own_buf, own_ref, out_sem.at[0])
        keep.start()
        for cp in remote:
            cp.wait()
        keep.wait()

    return pl.pallas_call(
        body, name="grad_chip_exchange",
        out_shape=(jax.ShapeDtypeStruct((half, cols), F32), jax.ShapeDtypeStruct((3, half, cols), recv.dtype),
                   jax.ShapeDtypeStruct((3,) + pb.shape[1:], pb.dtype)),
        in_specs=[ANY, ANY, ANY], out_specs=(ANY, ANY, ANY),
        scratch_shapes=[pltpu.VMEM((2, half, cols), F32), pltpu.VMEM((2, half, cols), recv.dtype),
                        pltpu.VMEM((3, half, cols), recv.dtype), pltpu.VMEM((half, cols), F32),
                        pltpu.SemaphoreType.DMA((2, 2)), pltpu.SemaphoreType.DMA((1,)),
                        pltpu.SemaphoreType.DMA((6,)), pltpu.SemaphoreType.DMA((6,))],
        compiler_params=pltpu.CompilerParams(vmem_limit_bytes=VMEM_LIMIT_BYTES),
    )(gw, recv, pb)


def _pair_share(ha, hb):
    def body(ha_ref, hb_ref, oa_ref, ob_ref, send_sems, recv_sems):
        x, y, c = _position()
        copies = [pltpu.make_async_remote_copy(
            src_ref=src, dst_ref=dst, send_sem=send_sems.at[k], recv_sem=recv_sems.at[k],
            device_id=(x, y, 1 - c), device_id_type=MESH)
            for k, (src, dst) in enumerate(((ha_ref, oa_ref), (hb_ref, ob_ref)))]
        for cp in copies:
            cp.start()
        for cp in copies:
            cp.wait()

    return pl.pallas_call(
        body, name="grad_pair_share",
        out_shape=(jax.ShapeDtypeStruct(ha.shape, ha.dtype), jax.ShapeDtypeStruct(hb.shape, hb.dtype)),
        in_specs=[ANY, ANY], out_specs=(ANY, ANY),
        scratch_shapes=[pltpu.SemaphoreType.DMA((2,)), pltpu.SemaphoreType.DMA((2,))],
    )(ha, hb)


def _pair_sum(mine, recv, c_idx):
    rows, cols = mine.shape[2:]

    def body(c_ref, a_ref, b_ref, o_ref, send_ref):
        total = a_ref[...] + b_ref[...]
        o_ref[...] = total
        send_ref[...] = total.astype(send_ref.dtype)

    out_spec = pl.BlockSpec((None, rows, cols), lambda s, c_ref: (s, 0, 0))
    return pl.pallas_call(
        body, name="grad_pair_sum",
        grid_spec=pltpu.PrefetchScalarGridSpec(
            num_scalar_prefetch=1, grid=(N_CHIPS,),
            in_specs=[pl.BlockSpec((None, None, rows, cols), lambda s, c_ref: (s, c_ref[0], 0, 0)),
                      pl.BlockSpec((None, rows, cols), lambda s, c_ref: (s, 0, 0))],
            out_specs=(out_spec, out_spec)),
        out_shape=(jax.ShapeDtypeStruct(recv.shape, recv.dtype), jax.ShapeDtypeStruct(recv.shape, WIRE_DTYPE)),
        compiler_params=_params(("parallel",)),
    )(c_idx, mine, recv)


def _window_start(s):
    return jnp.where(s == 0, WIN_START[0], jnp.where(s == 1, WIN_START[1], jnp.where(s == 2, WIN_START[2], WIN_START[3])))


def _assemble_w(own, others, starts):
    def body(starts_ref, own_ref, oth_ref, o_ref):
        o_ref[...] = jnp.zeros_like(o_ref)
        for k in range(N_CHIPS):
            rows = pl.ds(pl.multiple_of(starts_ref[k], 2 * SUBLANE), WIN_ROWS)
            o_ref[rows, :] = o_ref[rows, :] + (own_ref[...] if k == 0 else oth_ref[k - 1])

    return pl.pallas_call(
        body, name="assemble_w",
        in_specs=[pl.BlockSpec(memory_space=pltpu.SMEM), pl.BlockSpec(memory_space=pltpu.VMEM),
                  pl.BlockSpec(memory_space=pltpu.VMEM)],
        out_specs=pl.BlockSpec(memory_space=pltpu.VMEM),
        out_shape=jax.ShapeDtypeStruct((D_IN_PAD, D_MODEL), own.dtype),
        compiler_params=_params(),
    )(starts, own, others)


def _chip_sum(psum, recv3, chip_idx):
    rows, cols = psum.shape[1:]
    tr = rows // 2

    def body(s_ref, p_ref, r0, r1, r2, o_ref):
        o_ref[...] = ((p_ref[...] + r0[...].astype(F32)) + r1[...].astype(F32)) + r2[...].astype(F32)

    return pl.pallas_call(
        body, name="grad_chip_sum",
        grid_spec=pltpu.PrefetchScalarGridSpec(
            num_scalar_prefetch=1, grid=(2,),
            in_specs=[pl.BlockSpec((None, tr, cols), lambda i, s_ref: (s_ref[0], i, 0))] +
                     [pl.BlockSpec((None, tr, cols), functools.partial(lambda i, s_ref, j: (j, i, 0), j=j))
                      for j in range(3)],
            out_specs=pl.BlockSpec((tr, cols), lambda i, s_ref: (i, 0))),
        out_shape=jax.ShapeDtypeStruct((rows, cols), psum.dtype),
        compiler_params=_params(("parallel",)),
    )(chip_idx, psum, recv3, recv3, recv3)


def _adamw_math(w, g, m, v):
    m = ADAM_B1 * m + (1.0 - ADAM_B1) * g
    v = ADAM_B2 * v + (1.0 - ADAM_B2) * (g * g)
    m_hat = m * (1.0 / (1.0 - ADAM_B1 ** ADAM_STEP))
    v_hat = v * (1.0 / (1.0 - ADAM_B2 ** ADAM_STEP))
    delta = -ADAM_LR * (m_hat / (jnp.sqrt(v_hat) + ADAM_EPS) + ADAM_WD * w)
    return delta, m, v


def _adamw_big(w, g, m, v, tr):
    rows, cols = w.shape
    assert rows % tr == 0 and g.shape[0] >= rows

    def body(w_ref, g_ref, m_ref, v_ref, d_out, m_out, v_out):
        d, m2, v2 = _adamw_math(w_ref[...], g_ref[...], m_ref[...], v_ref[...])
        d_out[...] = d
        m_out[...] = m2
        v_out[...] = v2

    spec = pl.BlockSpec((tr, cols), lambda i: (i, 0))
    sds = jax.ShapeDtypeStruct((rows, cols), F32)
    return pl.pallas_call(
        body, name="adamw_big", grid=(rows // tr,), in_specs=[spec] * 4, out_specs=(spec,) * 3,
        out_shape=(sds,) * 3, compiler_params=_params(("parallel",)),
    )(w, g, m, v)


def _adamw_rows(w3, g, m3, v3):
    rows, _, cols = w3.shape
    rb = rows // ADAM_ROW_STEPS
    assert rb * ADAM_ROW_STEPS == rows

    def body(w_ref, g_ref, m_ref, v_ref, g_out, d_out, m_out, v_out):
        for k in range(ADAM_ROW_STEPS):
            @pl.when(pl.program_id(0) == k)
            def _(k=k):
                g = g_ref[k * rb:(k + 1) * rb, :]
                d, m2, v2 = _adamw_math(w_ref[:, 0, :], g, m_ref[:, 0, :], v_ref[:, 0, :])
                g_out[:, 0, :] = g
                d_out[:, 0, :] = d
                m_out[:, 0, :] = m2
                v_out[:, 0, :] = v2

    spec3 = pl.BlockSpec((rb, 1, cols), lambda i: (i, 0, 0))
    sds = jax.ShapeDtypeStruct((rows, 1, cols), F32)
    return pl.pallas_call(
        body, name="adamw_rows", grid=(ADAM_ROW_STEPS,),
        in_specs=[spec3, pl.BlockSpec((rows, cols), lambda i: (0, 0), pipeline_mode=pl.Buffered(1)), spec3, spec3],
        out_specs=(spec3,) * 4, out_shape=(sds,) * 4, compiler_params=_params(("parallel",)),
    )(w3, g, m3, v3)


def _small_update(own, others, params, ms, vs):
    slots = (SLOT_NORM, SLOT_FINAL, SLOT_ATTN, SLOT_CONVG, SLOT_BF, SLOT_META, SLOT_CONVW)
    n = len(slots)

    def body(*refs):
        own_ref, gp_ref = refs[:2]
        w_refs, m_refs, v_refs = refs[2:2 + n], refs[2 + n:2 + 2 * n], refs[2 + 2 * n:2 + 3 * n]
        outs = refs[2 + 3 * n:3 + 7 * n]
        loss_ref = outs[0]
        g_outs, d_outs, m_outs, v_outs = (outs[1 + k * n:1 + (k + 1) * n] for k in range(4))
        g_scr, w_scr, m_scr, v_scr = refs[3 + 7 * n:]
        x, y, c = _position()
        shard = 2 * x + y
        me = 4 * x + 2 * y + c
        tot = None
        for d in range(N_DEV):
            rel = jnp.bitwise_xor(me, d)
            term = jnp.where(rel == 0, own_ref[...], gp_ref[jnp.maximum(rel, 1) - 1])
            tot = term if tot is None else tot + term
        r0, r1, _, _ = SLOT_META
        meta_sel = tot[r0:r1, 0:256]
        cw_sel = tot[24:32, 0:128]
        for k in range(1, N_CHIPS):
            meta_sel = jnp.where(shard == k, tot[r0:r1, 256 * k:256 * (k + 1)], meta_sel)
            cw_sel = jnp.where(shard == k, tot[24:32, 128 * k:128 * (k + 1)], cw_sel)
        zeros = jnp.zeros((PACK_ROWS, D_MODEL), F32)
        for scr in (g_scr, w_scr, m_scr, v_scr):
            scr[...] = zeros
        g_scr[0:8, :] = tot[0:8, :]
        g_scr[r0:r1, 0:256] = meta_sel
        g_scr[24:32, 0:128] = cw_sel
        for (a, b, c0, c1), w_ref, m_ref, v_ref in zip(slots, w_refs, m_refs, v_refs):
            w_scr[a:b, c0:c1] = w_ref[...]
            m_scr[a:b, c0:c1] = m_ref[...]
            v_scr[a:b, c0:c1] = v_ref[...]
        loss_ref[...] = g_scr[LOSS_ROW:LOSS_ROW + 1, 0:1]
        d, m2, v2 = _adamw_math(w_scr[...], g_scr[...], m_scr[...], v_scr[...])
        w_scr[...] = d
        m_scr[...] = m2
        v_scr[...] = v2
        for (a, b, c0, c1), g_o, d_o, m_o, v_o in zip(slots, g_outs, d_outs, m_outs, v_outs):
            g_o[...] = g_scr[a:b, c0:c1]
            d_o[...] = w_scr[a:b, c0:c1]
            m_o[...] = m_scr[a:b, c0:c1]
            v_o[...] = v_scr[a:b, c0:c1]

    shapes = [jax.ShapeDtypeStruct(p.shape, F32) for p in params]
    out = pl.pallas_call(
        body, name="small_update",
        out_shape=[jax.ShapeDtypeStruct((1, 1), F32)] + shapes * 4,
        scratch_shapes=[pltpu.VMEM((PACK_ROWS, D_MODEL), F32)] * 4,
        compiler_params=_params(),
    )(own, others, *params, *ms, *vs)
    return out[0], out[1:1 + n], out[1 + n:1 + 2 * n], out[1 + 2 * n:1 + 3 * n], out[1 + 3 * n:1 + 4 * n]


def _in_proj(x2, meta_blk, norm_g, w_pad, bf_pad):
    seq = x2.shape[0]
    lp = seq + FRONT
    t = ROW_TILE
    nt = lp // t
    n_sub = t // LANE

    def body(*refs):
        x_refs = refs[:n_sub]
        mb, g_ref, w_ref, bf_ref, tri_ref = refs[n_sub:n_sub + 5]
        q_ref, k_ref, v_ref, rest_ref, fl_ref, ct_ref, u_ref, qt_ref, kt_ref, vt_ref, cc_ref, carry = refs[n_sub + 5:]
        i = pl.program_id(0)

        @pl.when(i == 0)
        def _():
            carry[...] = jnp.zeros_like(carry)

        first = jnp.where(i == 0, mb[...], x_refs[0][...])
        h = jnp.concatenate([first] + [r[...] for r in x_refs[1:]], axis=0)
        ms = jnp.mean(h * h, axis=-1, keepdims=True)
        u = ((h * lax.rsqrt(ms + EPS)) * g_ref[...]).astype(MXU_DTYPE)
        u_ref[...] = u

        def seg(a, width):
            return _dot_nt(u, w_ref[a:a + width, :])

        fl = seg(SEG_F, LANE)
        fl_ref[...] = fl
        q_tile = seg(SEG_Q, D_ATTN) * (HEAD_DIM ** -0.5)
        q_ref[...] = q_tile.astype(MXU_DTYPE)
        qt_ref[...] = q_tile.T.astype(MXU_DTYPE)
        z = fl + bf_ref[...]
        logf = jnp.minimum(z, 0.0) - jnp.log(1.0 + jnp.exp(-jnp.abs(z)))
        row = i * t + lax.broadcasted_iota(jnp.int32, (t, LANE), 0)
        logf = jnp.where(row >= PAD_ROWS, logf, 0.0)
        k_tile = seg(SEG_K, D_ATTN)
        k_ref[...] = k_tile.astype(MXU_DTYPE)
        kt_ref[...] = k_tile.T.astype(MXU_DTYPE)
        cs = _dot_exact(tri_ref[...], logf) + carry[...]
        carry[...] = carry[...] + jnp.sum(logf, axis=0, keepdims=True)
        v_tile = seg(SEG_V, D_ATTN)
        v_ref[...] = v_tile.astype(MXU_DTYPE)
        vt_ref[...] = v_tile.T.astype(MXU_DTYPE)
        col = i * t + lax.broadcasted_iota(jnp.int32, (SUBLANE, t), 1)
        ct_ref[...] = jnp.where(col >= PAD_ROWS, cs.T[0:SUBLANE, :], -NEG)
        cc_ref[...] = jnp.where(row >= PAD_ROWS, cs, -NEG)
        for s in range(5):
            rest_ref[:, 512 * s:512 * (s + 1)] = seg(SEG_ZA + 512 * s, 512)

    row_blk = lambda cols: pl.BlockSpec((t, cols), lambda i: (i, 0))
    tr_blk = pl.BlockSpec((None, D_ATTN, t), lambda i: (i, 0, 0))
    const = lambda shape: pl.BlockSpec(shape, lambda i: (0, 0))
    return pl.pallas_call(
        body, name="in_proj", grid=(nt,),
        in_specs=_x_block_specs(n_sub, LANE) + [const((LANE, D_MODEL)), const((1, D_MODEL)),
                                                pl.BlockSpec((D_IN_PAD, D_MODEL), lambda i: (0, 0),
                                                             pipeline_mode=pl.Buffered(1)),
                                                const((1, LANE)), const((t, t))],
        out_specs=(row_blk(D_ATTN), row_blk(D_ATTN), row_blk(D_ATTN), row_blk(5 * 512), row_blk(LANE),
                   pl.BlockSpec((SUBLANE, t), lambda i: (0, i)), row_blk(D_MODEL), tr_blk, tr_blk, tr_blk, row_blk(LANE)),
        out_shape=(jax.ShapeDtypeStruct((lp, D_ATTN), MXU_DTYPE), jax.ShapeDtypeStruct((lp, D_ATTN), MXU_DTYPE),
                   jax.ShapeDtypeStruct((lp, D_ATTN), MXU_DTYPE), jax.ShapeDtypeStruct((lp, 5 * 512), F32),
                   jax.ShapeDtypeStruct((lp, LANE), F32),
                   jax.ShapeDtypeStruct((SUBLANE, lp), F32), jax.ShapeDtypeStruct((lp, D_MODEL), MXU_DTYPE),
                   jax.ShapeDtypeStruct((nt, D_ATTN, t), MXU_DTYPE), jax.ShapeDtypeStruct((nt, D_ATTN, t), MXU_DTYPE),
                   jax.ShapeDtypeStruct((nt, D_ATTN, t), MXU_DTYPE), jax.ShapeDtypeStruct((lp, LANE), F32)),
        scratch_shapes=[pltpu.VMEM((1, LANE), F32)],
        compiler_params=_params(("arbitrary",)),
    )(*([x2] * n_sub), meta_blk, norm_g, w_pad, bf_pad, _triangle(t, lower=True))


def _head_masks():
    lane = lax.broadcasted_iota(jnp.int32, (1, LANE), 1)
    return lane < HEAD_DIM, lane >= HEAD_DIM


def _pair_specs(lp, nt, t):
    blk = pl.BlockSpec((lp, LANE), lambda g: (0, g))
    ct_a = pl.BlockSpec((None, nt, 1, t), lambda g: (2 * g, 0, 0, 0))
    ct_b = pl.BlockSpec((None, nt, 1, t), lambda g: (2 * g + 1, 0, 0, 0))
    return blk, ct_a, ct_b


def _sub_rows(s, col):
    return jnp.concatenate([s[:, a * LANE:(a + 1) * LANE] - col for a in range(s.shape[1] // LANE)], axis=1)


def _loop_unrolled(lo, hi, step, init, n):
    def group(jj, carry):
        for k in range(n):
            carry = step(lo + n * jj + k, carry)
        return carry

    groups = (hi - lo) // n
    carry = lax.fori_loop(0, groups, group, init)
    return lax.fori_loop(lo + n * groups, hi, step, carry)


def _attn_fwd(q, k, v_t, cc):
    lp = q.shape[0]
    t = ROW_TILE
    nt = lp // t
    ext = LANE + 2 * SUBLANE

    def body(q_ref, k_ref, vt_ref, cc_ref, o_ref, l_ref, m_ref, s_scr, last_scr, m_scr, mfin_scr, acc_scr, c_scr):
        masks = _head_masks()
        lane = lax.broadcasted_iota(jnp.int32, (1, LANE), 1)
        for hh in range(2):
            picked = jnp.where(lane == 2 * pl.program_id(0) + hh, cc_ref[...], 0.0)
            c_scr[hh] = jnp.broadcast_to(jnp.sum(picked, axis=-1, keepdims=True), (lp, LANE))
        visible = lax.broadcasted_iota(jnp.int32, (t, t), 0) <= lax.broadcasted_iota(jnp.int32, (t, t), 1)
        top = lax.broadcasted_iota(jnp.int32, (LANE, 1), 0) < HEAD_DIM
        second_head = (lax.broadcasted_iota(jnp.int32, (2 * SUBLANE, 2 * t), 1) >= t).astype(jnp.int32)
        ones_rows = jnp.where(lax.broadcasted_iota(jnp.int32, (2 * SUBLANE, 2 * t), 0) == second_head,
                              1.0, 0.0).astype(MXU_DTYPE)

        on_first_diagonal = jnp.concatenate([visible, jnp.ones((t, t), jnp.bool_)], axis=1)

        def scores(j, queries):
            kj = k_ref[pl.ds(pl.multiple_of(j * t, t), t), :]
            return _dot_nt(jnp.concatenate([jnp.where(hm, kj, 0).astype(MXU_DTYPE) for hm in masks], axis=0), queries)

        def biased(s2, j, hh):
            return _sub_rows(s2[hh * t:(hh + 1) * t, :], c_scr[hh, pl.ds(pl.multiple_of(j * t, t), t), :]) * LOG2E

        def track_max(hh, s, lo, hi):
            m = m_scr[hh, :, lo:hi]
            for a in range(t // SUBLANE):
                m = jnp.maximum(m, s[a * SUBLANE:(a + 1) * SUBLANE, :])
            m_scr[hh, :, lo:hi] = m

        def probabilities(scores_of, ms_cols):
            return jnp.concatenate([jnp.exp2(scores_of(hh) - ms_cols[hh]).astype(MXU_DTYPE) for hh in range(2)], axis=0)

        def values(j):
            vtj = vt_ref[j]
            v2 = jnp.concatenate([jnp.where(top, vtj, 0).astype(MXU_DTYPE),
                                  jnp.where(top, 0, vtj).astype(MXU_DTYPE)], axis=1)
            return jnp.concatenate([v2, ones_rows], axis=0)

        def stage(done, ahead):
            if ahead is not None:
                i_a, rows_a = ahead
                qa = q_ref[pl.ds(pl.multiple_of(i_a * t, t), rows_a), :]
                m_scr[...] = jnp.full(m_scr.shape, NEG, F32)

                def max_step(j, mask=None):
                    s2 = scores(j, qa)
                    for hh in range(2):
                        s = biased(s2, j, hh)
                        if mask is not None:
                            s = jnp.where(mask, s, NEG)
                        s_scr[j, hh * t:(hh + 1) * t, 0:rows_a] = s
                        track_max(hh, s, 0, rows_a)

            if done is not None:
                i_d, rows_d = done
                r0 = pl.multiple_of(i_d * t, t)
                ms = [mfin_scr[hh, 0:1, 0:rows_d] for hh in range(2)]
                acc_scr[...] = jnp.zeros(acc_scr.shape, F32)

                def key_step(j, carry):
                    p = probabilities(lambda hh: s_scr[j, hh * t:(hh + 1) * t, 0:rows_d], ms)
                    acc_scr[:, 0:rows_d] = acc_scr[:, 0:rows_d] + _dot(values(j), p)
                    if ahead is not None:
                        max_step(j)
                    return carry

                _loop_unrolled(0, i_d + 1, key_step, 0, ATTN_UNROLL)
                if rows_d == 2 * t:
                    p = probabilities(lambda hh: last_scr[hh * t:(hh + 1) * t, :], [m[:, t:] for m in ms])
                    acc_scr[:, t:rows_d] = acc_scr[:, t:rows_d] + _dot(values(i_d + 1), p)
                acc = acc_scr[:, 0:rows_d]
                l_pair = jnp.where(top, acc[LANE:LANE + 1], acc[LANE + 1:LANE + 2])
                o_ref[pl.ds(r0, rows_d), :] = (acc[:LANE] / l_pair).T
                l_ref[pl.ds(r0, rows_d), :] = l_pair.T
                for hh in range(2):
                    m_ref[pl.ds(r0, rows_d), hh * LANE:(hh + 1) * LANE] = jnp.broadcast_to(ms[hh], (LANE, rows_d)).T

            if ahead is not None:
                if done is not None:
                    max_step(i_a - 1)
                max_step(i_a, on_first_diagonal if rows_a == 2 * t else visible)
                if rows_a == 2 * t:
                    s2 = scores(i_a + 1, qa[t:])
                    for hh in range(2):
                        s = jnp.where(visible, biased(s2, i_a + 1, hh), NEG)
                        last_scr[hh * t:(hh + 1) * t, :] = s
                        track_max(hh, s, t, rows_a)
                for hh in range(2):
                    mfin_scr[hh, :, 0:rows_a] = jnp.broadcast_to(jnp.max(m_scr[hh, :, 0:rows_a], axis=0, keepdims=True),
                                                                 (SUBLANE, rows_a))

        pairs = nt // 2
        stage(None, (0, 2 * t))

        def pair_to_pair(u, _):
            stage((2 * u, 2 * t), (2 * u + 2, 2 * t))
            return 0

        lax.fori_loop(0, pairs - 1, pair_to_pair, 0)
        if nt % 2:
            stage((2 * pairs - 2, 2 * t), (nt - 1, t))
            stage((nt - 1, t), None)
        else:
            stage((2 * pairs - 2, 2 * t), None)

    blk = pl.BlockSpec((lp, LANE), lambda g: (0, g))
    return pl.pallas_call(
        body, name="attn_fwd", grid=(HEADS // 2,),
        in_specs=[blk, blk, pl.BlockSpec((nt, LANE, t), lambda g: (0, g, 0)),
                  pl.BlockSpec((lp, LANE), lambda g: (0, 0), pipeline_mode=pl.Buffered(1))],
        out_specs=(blk, blk, pl.BlockSpec((lp, 2 * LANE), lambda g: (0, g))),
        out_shape=(jax.ShapeDtypeStruct((lp, D_ATTN), F32), jax.ShapeDtypeStruct((lp, D_ATTN), F32),
                   jax.ShapeDtypeStruct((lp, HEADS * LANE), F32)),
        scratch_shapes=[pltpu.VMEM((nt, 2 * t, 2 * t), F32), pltpu.VMEM((2 * t, t), F32),
                        pltpu.VMEM((2, SUBLANE, 2 * t), F32), pltpu.VMEM((2, SUBLANE, 2 * t), F32),
                        pltpu.VMEM((ext, 2 * t), F32), pltpu.VMEM((2, lp, LANE), F32)],
        compiler_params=_params(("parallel",)),
    )(q, k, v_t, cc)


def _attn_bwd(q, k, v, do, q_t, k_t, do_t, m, neg_delta, ct4):
    lp = q.shape[0]
    t = ROW_TILE
    nt = lp // t

    def body(q_ref, k_ref, v_ref, do_ref, qt_ref, kt_ref, dot_ref, ma_ref, mb_ref, nd_ref, cta_ref, ctb_ref,
             dq_ref, dk_ref, dv_ref, dc_ref, dq_acc, dk_acc, dv_acc):
        masks = _head_masks()
        ct_refs, m_refs = (cta_ref, ctb_ref), (ma_ref, mb_ref)
        row_head = 2 * pl.program_id(0) + (lax.broadcasted_iota(jnp.int32, (2 * t, LANE), 0) >= t).astype(jnp.int32)
        col = lax.broadcasted_iota(jnp.int32, (2 * t, LANE), 1)
        delta_ones = jnp.where((col < HEADS * DELTA_TERMS) & (col % HEADS == row_head), 1.0, 0.0).astype(MXU_DTYPE)
        below = lax.broadcasted_iota(jnp.int32, (t, t), 1) <= lax.broadcasted_iota(jnp.int32, (t, t), 0)
        top = lax.broadcasted_iota(jnp.int32, (LANE, 1), 0) < HEAD_DIM
        dq_acc[...] = jnp.zeros_like(dq_acc)

        on_first_diagonal = jnp.concatenate([below, jnp.ones((t, t), jnp.bool_)], axis=0)

        def k_block(j, _, with_next=True):
            c0 = pl.multiple_of(j * t, t)
            kj = k_ref[pl.ds(c0, t), :]
            vj = v_ref[pl.ds(c0, t), :]
            k2 = jnp.concatenate([jnp.where(hm, kj, 0).astype(MXU_DTYPE) for hm in masks], axis=0)
            v2 = jnp.concatenate([jnp.where(hm, vj, 0).astype(MXU_DTYPE) for hm in masks], axis=0)
            v2 = jnp.concatenate([v2, delta_ones], axis=1)
            ck = [r[j] for r in ct_refs]
            ktj = kt_ref[j]
            k2t = jnp.concatenate([jnp.where(top, ktj, 0).astype(MXU_DTYPE), jnp.where(top, 0, ktj).astype(MXU_DTYPE)],
                                  axis=1)
            dk_acc[...] = jnp.zeros_like(dk_acc)
            dv_acc[...] = jnp.zeros_like(dv_acc)

            def q_block(i, colsums, mask=None, rows=t):
                r0 = pl.multiple_of(i * t, t)
                qi = q_ref[pl.ds(r0, rows), :]
                doi = jnp.concatenate([do_ref[pl.ds(r0, rows), :], nd_ref[pl.ds(r0, rows), :]], axis=1)
                qti = jnp.concatenate([qt_ref[i + b] for b in range(rows // t)], axis=1)
                doti = jnp.concatenate([dot_ref[i + b] for b in range(rows // t)], axis=1)
                s2 = _dot_nt(qi, k2)
                dp2 = _dot_nt(doi, v2)
                out, ps, dss = [], [], []
                for hh in range(2):
                    s = (s2[:, hh * t:(hh + 1) * t] - ck[hh]) * LOG2E
                    if mask is not None:
                        s = jnp.where(mask, s, NEG)
                    p = jnp.exp2(_sub_rows(s, m_refs[hh][pl.ds(r0, rows), :])).astype(MXU_DTYPE)
                    ds32 = p.astype(F32) * dp2[:, hh * t:(hh + 1) * t]
                    ps.append(p)
                    dss.append(ds32.astype(MXU_DTYPE))
                    out.append(colsums[hh] + jnp.sum(ds32, axis=0, keepdims=True))
                ds_cat = jnp.concatenate(dss, axis=1)
                dv_acc[...] = dv_acc[...] + _dot(doti, jnp.concatenate(ps, axis=1))
                dk_acc[...] = dk_acc[...] + _dot(qti, ds_cat)
                dq_t = _dot(k2t, ds_cat.T)
                for b in range(rows // t):
                    dq_acc[i + b] = dq_acc[i + b] + dq_t[:, b * t:(b + 1) * t]
                return tuple(out)

            nq = ATTN_BWD_QBLOCKS
            colsums = (jnp.zeros((1, t), F32), jnp.zeros((1, t), F32))
            if with_next:
                colsums = q_block(j, colsums, on_first_diagonal, nq * t)
            else:
                colsums = q_block(j, colsums, below)
            first = j + (nq if with_next else 1)
            groups = (nt - first) // nq
            colsums = lax.fori_loop(0, groups, lambda p, c: q_block(first + nq * p, c, None, nq * t), colsums)
            colsums = lax.fori_loop(first + nq * groups, nt, q_block, colsums)
            for hh in range(2):
                dc_ref[hh, j] = -colsums[hh]
            own = lambda acc: jnp.concatenate([acc[:HEAD_DIM, :t], acc[HEAD_DIM:, t:]], axis=0).T
            dk_ref[pl.ds(c0, t), :] = own(dk_acc[...]).astype(dk_ref.dtype)
            dv_ref[pl.ds(c0, t), :] = own(dv_acc[...]).astype(dv_ref.dtype)
            return 0

        lax.fori_loop(0, nt - 1, k_block, 0)
        k_block(nt - 1, 0, with_next=False)
        for i in range(nt):
            dq_ref[i * t:(i + 1) * t, :] = (dq_acc[i].T * (HEAD_DIM ** -0.5)).astype(dq_ref.dtype)

    blk, ct_a, ct_b = _pair_specs(lp, nt, t)
    rep_a = pl.BlockSpec((lp, LANE), lambda g: (0, 2 * g))
    rep_b = pl.BlockSpec((lp, LANE), lambda g: (0, 2 * g + 1))
    tr_blk = pl.BlockSpec((nt, LANE, t), lambda g: (0, g, 0))
    return pl.pallas_call(
        body, name="attn_bwd", grid=(HEADS // 2,),
        in_specs=[blk] * 4 + [tr_blk, tr_blk, tr_blk, rep_a, rep_b, pl.BlockSpec((lp, LANE), lambda g: (0, 0)), ct_a, ct_b],
        out_specs=(blk, blk, blk, pl.BlockSpec((2, nt, 1, t), lambda g: (g, 0, 0, 0))),
        out_shape=(jax.ShapeDtypeStruct((lp, D_ATTN), MXU_DTYPE),) * 3
                  + (jax.ShapeDtypeStruct((HEADS, nt, 1, t), F32),),
        scratch_shapes=[pltpu.VMEM((nt, LANE, t), F32), pltpu.VMEM((LANE, 2 * t), F32), pltpu.VMEM((LANE, 2 * t), F32)],
        compiler_params=_params(("parallel",)),
    )(q, k, v, do, q_t, k_t, do_t, m, m, neg_delta, ct4, ct4)


def _shift_down(prev8, cur, k):
    ext = jnp.concatenate([prev8, cur], axis=0)
    return pltpu.roll(ext, k, 0)[SUBLANE:, :]


def _shift_up(cur, next8, k):
    ext = jnp.concatenate([cur, next8], axis=0)
    n = ext.shape[0]
    return pltpu.roll(ext, n - k, 0)[:cur.shape[0], :]


def _post(o, l_sum, rest, x2, meta_blk, tgt2, w_out, attn_g, conv_g, final_g, conv_w8):
    lp = o.shape[0]
    t = ROW_TILE
    nt = lp // t
    n_sub = t // LANE
    hb = t // SUBLANE

    def body(*refs):
        o_ref, l_ref, za_ref, gb_ref, gc_ref, xc_ref, zc_ref, gch_ref, xch_ref = refs[:9]
        x_refs = refs[9:9 + n_sub]
        mb = refs[9 + n_sub]
        t_refs = refs[10 + n_sub:10 + 2 * n_sub]
        wo_ref, ag_ref, cg_ref, fg_ref, cw_ref, gm_ref, hr_ref = refs[10 + 2 * n_sub:17 + 2 * n_sub]
        (dout_ref, do_ref, dot_ref, dl_ref, dza_ref, dgb_ref, dzc_ref, dcv_ref,
         loss_ref, gf_ref, gag_ref, gcg_ref, gwo_ref) = refs[17 + 2 * n_sub:]
        i = pl.program_id(0)

        @pl.when(i == 0)
        def _():
            for r in (loss_ref, gf_ref, gag_ref, gcg_ref, gwo_ref):
                r[...] = jnp.zeros_like(r)

        gmat = gm_ref[...]
        inv_g = 1.0 / HEAD_DIM
        o_v = o_ref[...]
        ra = lax.rsqrt(_group_sum(o_v * o_v, gmat, STAT_TERMS) * inv_g + EPS)
        n_a = o_v * ra
        a_n = n_a * ag_ref[...]
        za = za_ref[...]
        sig_a = _sigmoid(za)
        sz_a = za * sig_a
        y_a = a_n * sz_a
        gb = gb_ref[...]
        gc = gc_ref[...]
        xc = xc_ref[...]
        cx = gc * xc
        cx_prev = jnp.where(i == 0, 0.0, gch_ref[...] * xch_ref[...])
        conv = (cw_ref[0:1, :] * _shift_down(cx_prev, cx, 2) + cw_ref[1:2, :] * _shift_down(cx_prev, cx, 1)
                + cw_ref[2:3, :] * cx)
        e = gb * conv
        re = lax.rsqrt(_group_sum(e * e, gmat, STAT_TERMS) * inv_g + EPS)
        n_e = e * re
        e_n = n_e * cg_ref[...]
        zc = zc_ref[...]
        sig_c = _sigmoid(zc)
        sz_c = zc * sig_c
        y_c = e_n * sz_c
        mix = jnp.concatenate([y_a, y_c], axis=-1)
        mix_b = mix.astype(MXU_DTYPE)
        first = jnp.where(i == 0, mb[...], x_refs[0][...])
        h = jnp.concatenate([first] + [r[...] for r in x_refs[1:]], axis=0)
        out = h + _dot(mix_b, wo_ref[...])
        r2 = lax.rsqrt(jnp.mean(out * out, axis=-1, keepdims=True) + EPS)
        n_f = out * r2
        y = n_f * fg_ref[...]
        tgt = jnp.concatenate([r[...] for r in t_refs], axis=0)
        valid = (i * t + lax.broadcasted_iota(jnp.int32, (t, 1), 0)) >= FRONT
        diff = jnp.where(valid, y - tgt, 0.0)
        loss_ref[...] = loss_ref[...] + 0.5 * jnp.sum(jnp.sum(diff * diff, axis=-1, keepdims=True) * (1.0 / D_MODEL))
        dy = diff * (1.0 / D_MODEL)
        gf_ref[...] = gf_ref[...] + jnp.sum(dy * n_f, axis=0, keepdims=True)
        dn = dy * fg_ref[...]
        d_out = r2 * (dn - n_f * jnp.mean(dn * n_f, axis=-1, keepdims=True))
        dout_ref[...] = d_out
        d_out_b = d_out.astype(MXU_DTYPE)
        d_mix = _dot_nt(d_out_b, wo_ref[...])
        gwo_ref[...] = gwo_ref[...] + _dot(mix.T.astype(MXU_DTYPE), d_out_b)
        d_ya = d_mix[:, :D_ATTN]
        d_yc = d_mix[:, D_ATTN:]
        d_an = d_ya * sz_a
        dza_ref[...] = (d_ya * a_n * (sig_a * (1.0 + za * (1.0 - sig_a)))).astype(dza_ref.dtype)
        gag_ref[...] = gag_ref[...] + jnp.sum(d_an * n_a, axis=0, keepdims=True)
        dn_a = d_an * ag_ref[...]
        d_o = ra * (dn_a - n_a * (_group_sum(dn_a * n_a, gmat, STAT_TERMS) * inv_g))
        d_o_l = d_o / l_ref[...]
        d_o_b = d_o_l.astype(do_ref.dtype)
        do_ref[...] = d_o_b
        dot_ref[...] = d_o_l.T.astype(dot_ref.dtype)
        delta = _group_sum(d_o_b.astype(F32) * o_v, hr_ref[...])
        terms, rest_of = [], delta
        for k in range(DELTA_TERMS):
            terms.append(rest_of.astype(MXU_DTYPE).astype(F32))
            rest_of = rest_of - terms[-1]
        dl_ref[...] = -sum(pltpu.roll(term, HEADS * k, 1) if k else term
                           for k, term in enumerate(terms)).astype(dl_ref.dtype)
        d_en = d_yc * sz_c
        dzc_ref[...] = (d_yc * e_n * (sig_c * (1.0 + zc * (1.0 - sig_c)))).astype(dzc_ref.dtype)
        gcg_ref[...] = gcg_ref[...] + jnp.sum(d_en * n_e, axis=0, keepdims=True)
        dn_e = d_en * cg_ref[...]
        d_e = re * (dn_e - n_e * (_group_sum(dn_e * n_e, gmat, STAT_TERMS) * inv_g))
        dgb_ref[...] = (d_e * conv).astype(dgb_ref.dtype)
        dcv_ref[...] = d_e * gb

    head_rep = jnp.where((lax.broadcasted_iota(jnp.int32, (D_ATTN, LANE), 0) >> 6)
                         == lax.broadcasted_iota(jnp.int32, (D_ATTN, LANE), 1), 1.0, 0.0).astype(MXU_DTYPE)
    row_blk = lambda cols: pl.BlockSpec((t, cols), lambda i: (i, 0))
    rest_blk = lambda s: pl.BlockSpec((t, 512), functools.partial(lambda i, s: (i, s), s=s))
    halo = lambda s: pl.BlockSpec((SUBLANE, 512), functools.partial(lambda i, s: (jnp.maximum(i * hb - 1, 0), s), s=s))
    const = lambda shape: pl.BlockSpec(shape, lambda i: (0, 0))
    acc = lambda shape: pl.BlockSpec(shape, lambda i: (0, 0))
    return pl.pallas_call(
        body, name="post_fwd_bwd", grid=(nt,),
        in_specs=[row_blk(D_ATTN), row_blk(D_ATTN)] + [rest_blk(s) for s in range(5)] + [halo(2), halo(3)]
                 + _x_block_specs(n_sub, LANE) + [const((LANE, D_MODEL))] + _x_block_specs(n_sub, LANE)
                 + [const((D_MODEL, D_MODEL)), const((1, D_ATTN)), const((1, D_CONV)), const((1, D_MODEL)),
                    const((SUBLANE, D_CONV)), const((D_ATTN, D_ATTN)), const((D_ATTN, LANE))],
        out_specs=(row_blk(D_MODEL), row_blk(D_ATTN), pl.BlockSpec((None, D_ATTN, t), lambda i: (i, 0, 0)),
                   row_blk(LANE), row_blk(D_ATTN), row_blk(D_CONV),
                   row_blk(D_CONV), row_blk(D_CONV),
                   acc((1, LANE)), acc((1, D_MODEL)), acc((1, D_ATTN)), acc((1, D_CONV)), acc((D_MODEL, D_MODEL))),
        out_shape=(jax.ShapeDtypeStruct((lp, D_MODEL), F32), jax.ShapeDtypeStruct((lp, D_ATTN), MXU_DTYPE),
                   jax.ShapeDtypeStruct((nt, D_ATTN, t), MXU_DTYPE), jax.ShapeDtypeStruct((lp, LANE), MXU_DTYPE),
                   jax.ShapeDtypeStruct((lp, D_ATTN), MXU_DTYPE),
                   jax.ShapeDtypeStruct((lp, D_CONV), MXU_DTYPE), jax.ShapeDtypeStruct((lp, D_CONV), MXU_DTYPE),
                   jax.ShapeDtypeStruct((lp, D_CONV), F32),
                   jax.ShapeDtypeStruct((1, LANE), F32), jax.ShapeDtypeStruct((1, D_MODEL), F32),
                   jax.ShapeDtypeStruct((1, D_ATTN), F32), jax.ShapeDtypeStruct((1, D_CONV), F32),
                   jax.ShapeDtypeStruct((D_MODEL, D_MODEL), F32)),
        compiler_params=_params(("arbitrary",)),
    )(o, l_sum, *([rest] * 5), rest, rest, *([x2] * n_sub), meta_blk, *([tgt2] * n_sub),
      w_out, attn_g, conv_g, final_g, conv_w8, _group_matrix(), head_rep)


def _bwd_in(x2, meta_blk, norm_g, w_pad, bf_pad, fl, dc, dq, dk, dv, dza, dgb, dzc, dconv, rest, d_out, conv_w8):
    lp = fl.shape[0]
    t = ROW_TILE
    nt = lp // t
    n_sub = t // LANE
    hb = t // SUBLANE
    rev = lambda i: nt - 1 - i

    def body(*refs):
        x_refs = refs[:n_sub]
        (mb, g_ref, w_ref, bf_ref, fl_ref, dc_ref, dq_ref, dk_ref, dv_ref, dza_ref, dgb_ref, dzc_ref,
         dcv_ref, dcvn_ref, gc_ref, xc_ref, gch_ref, xch_ref, dout_ref, cw_ref, tri_ref) = refs[n_sub:n_sub + 21]
        dp_ref, gx_ref, front_ref, gn_ref, gbf_ref, gcw_ref, carry, dh_scr, gx_sems = refs[n_sub + 21:]
        step = pl.program_id(0)
        i = rev(step)

        @pl.when(step == 0)
        def _():
            for r in (gn_ref, gbf_ref, gcw_ref, carry):
                r[...] = jnp.zeros_like(r)

        dc8 = jnp.concatenate([dc_ref[...], jnp.zeros((LANE - HEADS, t), F32)], axis=0).T
        dlogf = _dot_exact(tri_ref[...], dc8) + carry[...]
        carry[...] = carry[...] + jnp.sum(dc8, axis=0, keepdims=True)
        z = fl_ref[...] + bf_ref[...]
        row = i * t + lax.broadcasted_iota(jnp.int32, (t, LANE), 0)
        d_f = jnp.where(row >= PAD_ROWS, dlogf * (1.0 / (1.0 + jnp.exp(z))), 0.0)
        gbf_ref[...] = gbf_ref[...] + jnp.sum(d_f, axis=0, keepdims=True)
        dcv = dcv_ref[...]
        dcv_next = jnp.where(i == nt - 1, 0.0, dcvn_ref[...])
        d_cx = (cw_ref[2:3, :] * dcv + cw_ref[1:2, :] * _shift_up(dcv, dcv_next, 1)
                + cw_ref[0:1, :] * _shift_up(dcv, dcv_next, 2))
        gc = gc_ref[...]
        xc = xc_ref[...]
        cx = gc * xc
        cx_prev = jnp.where(i == 0, 0.0, gch_ref[...] * xch_ref[...])
        rowi = lax.broadcasted_iota(jnp.int32, (SUBLANE, 1), 0)
        gcw = (jnp.where(rowi == 0, jnp.sum(dcv * _shift_down(cx_prev, cx, 2), axis=0, keepdims=True), 0.0)
               + jnp.where(rowi == 1, jnp.sum(dcv * _shift_down(cx_prev, cx, 1), axis=0, keepdims=True), 0.0)
               + jnp.where(rowi == 2, jnp.sum(dcv * cx, axis=0, keepdims=True), 0.0))
        gcw_ref[...] = gcw_ref[...] + gcw
        dp_ref[:, SEG_Q:SEG_Q + 512] = dq_ref[...]
        dp_ref[:, SEG_K:SEG_K + 512] = dk_ref[...]
        dp_ref[:, SEG_V:SEG_V + 512] = dv_ref[...]
        dp_ref[:, SEG_F:SEG_F + LANE] = d_f.astype(dp_ref.dtype)
        dp_ref[:, SEG_ZA:SEG_ZA + 512] = dza_ref[...]
        dp_ref[:, SEG_GB:SEG_GB + 512] = dgb_ref[...]
        dp_ref[:, SEG_GC:SEG_GC + 512] = (d_cx * xc).astype(dp_ref.dtype)
        dp_ref[:, SEG_XC:SEG_XC + 512] = (d_cx * gc).astype(dp_ref.dtype)
        dp_ref[:, SEG_ZC:SEG_ZC + 512] = dzc_ref[...]
        d_u = _dot(dp_ref[...], w_ref[...])
        first = jnp.where(i == 0, mb[...], x_refs[0][...])
        h = jnp.concatenate([first] + [r[...] for r in x_refs[1:]], axis=0)
        r1 = lax.rsqrt(jnp.mean(h * h, axis=-1, keepdims=True) + EPS)
        n_h = h * r1
        gn_ref[...] = gn_ref[...] + jnp.sum(d_u * n_h, axis=0, keepdims=True)
        dn = d_u * g_ref[...]
        d_h = dout_ref[...] + r1 * (dn - n_h * jnp.mean(dn * n_h, axis=-1, keepdims=True))
        slot = step % 2

        def to_grad_x(slot_, tile):
            return pltpu.make_async_copy(dh_scr.at[slot_], gx_ref.at[pl.ds(pl.multiple_of(tile * t - FRONT, SUBLANE), t)],
                                         gx_sems.at[slot_])

        @pl.when(step >= 2)
        def _():
            to_grad_x(slot, 1).wait()

        dh_scr[slot] = d_h

        @pl.when(i > 0)
        def _():
            to_grad_x(slot, i).start()

        @pl.when(i == 0)
        def _():
            front_ref[...] = d_h[:FRONT]
            rest_rows = pltpu.make_async_copy(dh_scr.at[slot, pl.ds(FRONT, t - FRONT)], gx_ref.at[pl.ds(0, t - FRONT)],
                                              gx_sems.at[slot])
            rest_rows.start()
            rest_rows.wait()
            if nt >= 2:
                to_grad_x(1 - slot, 1).wait()

    def x_specs():
        specs = [pl.BlockSpec((LANE, D_MODEL), lambda s: (jnp.maximum(n_sub * rev(s) - 1, 0), 0))]
        for b in range(1, n_sub):
            specs.append(pl.BlockSpec((LANE, D_MODEL), functools.partial(lambda s, b: (n_sub * rev(s) - 1 + b, 0), b=b)))
        return specs

    row_blk = lambda cols: pl.BlockSpec((t, cols), lambda s: (rev(s), 0))
    rest_blk = lambda k: pl.BlockSpec((t, 512), functools.partial(lambda s, k: (rev(s), k), k=k))
    halo_prev = lambda k: pl.BlockSpec(
        (SUBLANE, 512), functools.partial(lambda s, k: (jnp.maximum(rev(s) * hb - 1, 0), k), k=k))
    halo_next = pl.BlockSpec((SUBLANE, 512), lambda s: (jnp.minimum((rev(s) + 1) * hb, lp // SUBLANE - 1), 0))
    const = lambda shape: pl.BlockSpec(shape, lambda s: (0, 0))
    return pl.pallas_call(
        body, name="bwd_in", grid=(nt,),
        in_specs=x_specs() + [const((LANE, D_MODEL)), const((1, D_MODEL)),
                              pl.BlockSpec((D_IN_PAD, D_MODEL), lambda s: (0, 0), pipeline_mode=pl.Buffered(1)),
                              const((1, LANE)), row_blk(LANE),
                              pl.BlockSpec((HEADS, t), lambda s: (0, rev(s))),
                              row_blk(512), row_blk(512), row_blk(512), row_blk(512), row_blk(512), row_blk(512),
                              row_blk(512), halo_next, rest_blk(2), rest_blk(3), halo_prev(2), halo_prev(3),
                              row_blk(D_MODEL), const((SUBLANE, D_CONV)), const((t, t))],
        out_specs=(row_blk(D_IN_PAD), ANY, const((FRONT, D_MODEL)), const((1, D_MODEL)), const((1, LANE)),
                   const((SUBLANE, D_CONV))),
        out_shape=(jax.ShapeDtypeStruct((lp, D_IN_PAD), MXU_DTYPE), jax.ShapeDtypeStruct((lp - FRONT, D_MODEL), F32),
                   jax.ShapeDtypeStruct((FRONT, D_MODEL), F32),
                   jax.ShapeDtypeStruct((1, D_MODEL), F32), jax.ShapeDtypeStruct((1, LANE), F32),
                   jax.ShapeDtypeStruct((SUBLANE, D_CONV), F32)),
        scratch_shapes=[pltpu.VMEM((1, LANE), F32), pltpu.VMEM((2, t, D_MODEL), F32), pltpu.SemaphoreType.DMA((2,))],
        compiler_params=_params(("arbitrary",)),
    )(*([x2] * n_sub), meta_blk, norm_g, w_pad, bf_pad, fl, dc, dq, dk, dv, dza, dgb, dzc, dconv, dconv,
      rest, rest, rest, rest, d_out, conv_w8, _triangle(t, lower=False))


def _grad_w_in(u, dproj):
    lp = u.shape[0]
    tn = GW_COL_TILE
    tk = tn if lp % tn == 0 else ROW_TILE

    def body(d_ref, u_ref, o_ref, wire_ref):
        k = pl.program_id(1)

        @pl.when(k == 0)
        def _():
            o_ref[...] = jnp.zeros_like(o_ref)

        o_ref[...] = o_ref[...] + lax.dot_general(d_ref[...], u_ref[...], (((0,), (0,)), ((), ())),
                                                  preferred_element_type=F32)

        @pl.when(k == pl.num_programs(1) - 1)
        def _():
            wire_ref[...] = o_ref[...].astype(wire_ref.dtype)

    out_spec = pl.BlockSpec((tn, D_MODEL), lambda n, k: (n, 0))
    return pl.pallas_call(
        body, name="grad_w_in", grid=(D_IN_PAD // tn, lp // tk),
        in_specs=[pl.BlockSpec((tk, tn), lambda n, k: (k, n)), pl.BlockSpec((tk, D_MODEL), lambda n, k: (k, 0))],
        out_specs=(out_spec, out_spec),
        out_shape=(jax.ShapeDtypeStruct((D_IN_PAD, D_MODEL), F32), jax.ShapeDtypeStruct((D_IN_PAD, D_MODEL), WIRE_DTYPE)),
        compiler_params=_params(("parallel", "arbitrary")),
    )(dproj, u)


def _by_chip(own, others, me):
    by_mask = jnp.stack([own, others[1], others[0], others[2]])
    return [lax.dynamic_index_in_dim(by_mask, jnp.bitwise_xor(me, s), 0, keepdims=False) for s in range(N_CHIPS)]


def _both_halves(mine, other, c):
    return jnp.where(c == 0, jnp.concatenate([mine, other], axis=0), jnp.concatenate([other, mine], axis=0))


def _local_step(x2, tgt2, meta_full, norm_g, w_pad, b_f, conv_w_full, attn_g, conv_g, w_out_full, final_g):
    lp = x2.shape[0] + FRONT
    nt = lp // ROW_TILE
    meta_blk = jnp.concatenate([jnp.zeros((PAD_ROWS, D_MODEL), F32), meta_full], axis=0)
    bf_pad = jnp.pad(b_f, ((0, 0), (0, LANE - HEADS)))
    conv_w8 = jnp.pad(conv_w_full, ((0, SUBLANE - conv_w_full.shape[0]), (0, 0)))
    q, k, v, rest, fl, ct, u, q_t, k_t, v_t, cc = _in_proj(x2, meta_blk, norm_g, w_pad, bf_pad)
    ct4 = ct.reshape(SUBLANE, nt, 1, ROW_TILE)
    o, l_sum, m_max = _attn_fwd(q, k, v_t, cc)
    (d_out, d_o, do_t, neg_delta, dza, dgb, dzc, dconv, loss, g_final, g_attn, g_convg, gw_out) = _post(
        o, l_sum, rest, x2, meta_blk, tgt2, w_out_full, attn_g, conv_g, final_g, conv_w8)
    dq, dk, dv, dc = _attn_bwd(q, k, v, d_o, q_t, k_t, do_t, m_max, neg_delta, ct4)
    dproj, grad_x, d_front, g_norm, g_bf, g_cw = _bwd_in(x2, meta_blk, norm_g, w_pad, bf_pad, fl, dc.reshape(HEADS, lp), dq, dk, dv,
                                             dza, dgb, dzc, dconv, rest, d_out, conv_w8)
    gw_in, gw_in_wire = _grad_w_in(u, dproj)
    return dict(loss=loss, grad_x=grad_x, d_front=d_front, g_norm=g_norm, g_final=g_final, g_attn=g_attn, g_convg=g_convg, g_bf=g_bf,
                g_cw=g_cw, gw_out=gw_out, gw_in=gw_in, gw_in_wire=gw_in_wire)


def kernel(x, meta, norm_g, w_in, b_f, conv_w, attn_norm_g, conv_norm_g, w_out, final_norm_g, loss_target, m_meta, m_norm_g, m_w_in, m_b_f, m_conv_w, m_attn_norm_g, m_conv_norm_g, m_w_out, m_final_norm_g, v_meta, v_norm_g, v_w_in, v_b_f, v_conv_w, v_attn_norm_g, v_conv_norm_g, v_w_out, v_final_norm_g):
    cx_, cy_, cc_ = _position()
    chip = 2 * cx_ + cy_
    shard = w_in.shape[2]
    out_half = w_out.shape[1] // 2
    pick = lambda vals: jnp.where(chip == 0, vals[0], jnp.where(chip == 1, vals[1], jnp.where(chip == 2, vals[2], vals[3])))
    a_off, b_off = pick(A_OFF), pick(B_OFF)
    wt = jnp.transpose(w_in[0]).astype(MXU_DTYPE)

    def placed(piece, off):
        return lax.dynamic_slice_in_dim(jnp.pad(piece, ((WIN_ROWS, WIN_ROWS), (0, 0))), WIN_ROWS - off, WIN_ROWS, 0)

    wi = placed(wt[:PIECE_A], a_off) + placed(wt[PIECE_A:], b_off)
    wo = w_out[0].astype(MXU_DTYPE)
    small = jnp.concatenate([meta, jnp.pad(conv_w[0], ((0, 8 - conv_w.shape[1]), (0, meta.shape[1] - conv_w.shape[2])))],
                            axis=0)
    gwi, gwo, gsm = _gather_weights(wi.reshape(2, WIN_HALF, D_MODEL), wo.reshape(2, out_half, D_MODEL), small)
    starts = jnp.stack([_window_start(jnp.bitwise_xor(chip, mask)) for mask in (0, 2, 1, 3)]).astype(jnp.int32)
    w_pad = _assemble_w(wi, gwi.reshape(3, WIN_ROWS, D_MODEL), starts)
    w_out_full = jnp.concatenate(_by_chip(wo, gwo.reshape(3, 2 * out_half, D_MODEL), chip), axis=0)
    small_full = jnp.concatenate(_by_chip(small, gsm, chip), axis=1)
    meta_full = small_full[:N_META]
    conv_w_full = jnp.concatenate([small_full[N_META:N_META + 3, 256 * s:256 * s + LANE] for s in range(N_CHIPS)], axis=1)
    final_g2 = final_norm_g.reshape(1, D_MODEL)
    r = _local_step(x[0], loss_target[0], meta_full, norm_g, w_pad, b_f, conv_w_full, attn_norm_g, conv_norm_g,
                    w_out_full, final_g2)
    grad_x = r["grad_x"][None]
    gb = r["gw_out"].reshape(N_CHIPS, 2, out_half, D_MODEL)
    wide = lambda a: jnp.pad(a, ((0, 0), (0, D_MODEL - a.shape[1])))
    pack = jnp.concatenate([
        r["g_norm"], r["g_final"], jnp.concatenate([r["g_attn"], r["g_convg"]], axis=1), wide(r["g_bf"]),
        wide(r["loss"]), jnp.zeros((3, D_MODEL), F32), r["d_front"][PAD_ROWS:], wide(r["g_cw"])], axis=0)
    ra, rb, packs = _pair_exchange(r["gw_in_wire"], gb, pack)
    c_idx = jnp.reshape(cc_, (1,)).astype(jnp.int32)
    chip_idx = jnp.reshape(chip, (1,)).astype(jnp.int32)
    pb, pb_wire = _pair_sum(gb, rb, c_idx)
    pa_own, xa, xb = _chip_exchange(r["gw_in"], ra, pb_wire)
    ha = _chip_sum(pa_own[None], xa, jnp.zeros((1,), jnp.int32))
    hb = _chip_sum(pb, xb, chip_idx)
    oa, ob = _pair_share(ha, hb)
    g_window = _both_halves(ha, oa, cc_)
    g_w_in_t = jnp.concatenate([lax.dynamic_slice_in_dim(g_window, a_off, PIECE_A, 0),
                                lax.dynamic_slice_in_dim(g_window, b_off, shard - PIECE_A, 0)], axis=0)
    g_w_out = _both_halves(hb, ob, cc_)
    as_rows = lambda a: jnp.transpose(a, (2, 0, 1))
    g_w_in, d_w_in, nm_w_in, nv_w_in = (jnp.transpose(a, (1, 2, 0)) for a in _adamw_rows(
        as_rows(w_in), g_w_in_t, as_rows(m_w_in), as_rows(v_w_in)))
    d_w_out, nm_w_out, nv_w_out = (a[None] for a in _adamw_big(w_out[0], g_w_out, m_w_out[0], v_w_out[0], LANE))
    params = (norm_g, final_g2, attn_norm_g, conv_norm_g, b_f, meta, conv_w[0])
    ms = (m_norm_g, m_final_norm_g.reshape(1, D_MODEL), m_attn_norm_g, m_conv_norm_g, m_b_f, m_meta, m_conv_w[0])
    vs = (v_norm_g, v_final_norm_g.reshape(1, D_MODEL), v_attn_norm_g, v_conv_norm_g, v_b_f, v_meta, v_conv_w[0])
    loss, g_s, d_s, m_s, v_s = _small_update(pack, packs, params, ms, vs)

    def ordered(small_list, big_in, big_out):
        s_norm, s_final, s_attn, s_convg, s_bf, s_meta, s_cw = small_list
        return (s_meta, s_norm, big_in, s_bf, s_cw[None], s_attn, s_convg, big_out, s_final.reshape(D_MODEL))

    return (loss.reshape(()), grad_x,
            *ordered(g_s, g_w_in, g_w_out[None]), *ordered(d_s, d_w_in, d_w_out),
            *ordered(m_s, nm_w_in, nm_w_out), *ordered(v_s, nv_w_in, nv_w_out))
```

```python
import functools

import jax
import jax.numpy as jnp
from jax import lax
from jax.experimental import pallas as pl
from jax.experimental.pallas import tpu as pltpu

F32 = jnp.float32
MXU_DTYPE = jnp.bfloat16
WIRE_DTYPE = jnp.bfloat16

D_MODEL = 1024
N_META = 16
HEADS = 8
HEAD_DIM = 64
D_ATTN = HEADS * HEAD_DIM
D_CONV = 512
EPS = 1e-6
LANE = 128
SUBLANE = 8
ROW_TILE = 384
ATTN_UNROLL = 3
ATTN_BWD_QBLOCKS = 2
DELTA_TERMS = 3
ADAM_ROW_STEPS = 3
STAT_TERMS = 1
FRONT = LANE
PAD_ROWS = FRONT - N_META
NEG = -1e30
LOG2E = 1.4426950408889634
N_CHIPS = 4
N_DEV = 8
VMEM_LIMIT_BYTES = 60 * 1024 * 1024

SEG_Q, SEG_K, SEG_V, SEG_F, SEG_ZA, SEG_GB, SEG_GC, SEG_XC, SEG_ZC = (
    0, 512, 1024, 1536, 1664, 2176, 2688, 3200, 3712)
D_IN = 4104
D_IN_PAD = 4224
F_END = 1544
GW_COL_TILE = 1408
WIN_ROWS = 1152
WIN_HALF = WIN_ROWS // 2
WIN_START = (0, 1024, 2160, 3072)
PIECE_A = 518
A_OFF = (0, 2, 12, 126)
B_OFF = (518, 640, 530, 644)
ADAM_LR = 0.001
ADAM_B1 = 0.9
ADAM_B2 = 0.999
ADAM_EPS = 1e-08
ADAM_WD = 0.01
ADAM_STEP = 10

MESH = pl.DeviceIdType.MESH
ANY = pl.BlockSpec(memory_space=pl.ANY)

PACK_ROWS = 32
SLOT_NORM = (0, 1, 0, 1024)
SLOT_FINAL = (1, 2, 0, 1024)
SLOT_ATTN = (2, 3, 0, 512)
SLOT_CONVG = (2, 3, 512, 1024)
SLOT_BF = (3, 4, 0, 8)
SLOT_META = (8, 24, 0, 256)
SLOT_CONVW = (24, 27, 0, 128)
LOSS_ROW = 4


def _params(sem=None):
    return pltpu.CompilerParams(dimension_semantics=sem, vmem_limit_bytes=VMEM_LIMIT_BYTES)


def _sigmoid(z):
    return 1.0 / (1.0 + jnp.exp(-z))


def _dot(a, b):
    return jnp.dot(a, b, preferred_element_type=F32)


def _dot_nt(a, b):
    return lax.dot_general(a, b, (((1,), (1,)), ((), ())), preferred_element_type=F32)


def _dot_exact(ones, x):
    ones = ones.astype(MXU_DTYPE)
    total = None
    for _ in range(3):
        term = x.astype(MXU_DTYPE)
        x = x - term.astype(F32)
        total = _dot(ones, term) if total is None else total + _dot(ones, term)
    return total


def _group_matrix():
    r = lax.broadcasted_iota(jnp.int32, (D_ATTN, D_ATTN), 0) >> 6
    c = lax.broadcasted_iota(jnp.int32, (D_ATTN, D_ATTN), 1) >> 6
    return jnp.where(r == c, 1.0, 0.0).astype(MXU_DTYPE)


def _triangle(n, lower):
    r = lax.broadcasted_iota(jnp.int32, (n, n), 0)
    c = lax.broadcasted_iota(jnp.int32, (n, n), 1)
    return jnp.where((r >= c) if lower else (c >= r), 1.0, 0.0).astype(MXU_DTYPE)


def _group_sum(x, gmat, terms=2):
    hi = x.astype(MXU_DTYPE)
    if terms == 1:
        return _dot(hi, gmat)
    lo = (x - hi.astype(F32)).astype(MXU_DTYPE)
    return _dot(hi, gmat) + _dot(lo, gmat)


def _x_block_specs(n_sub, rows):
    specs = [pl.BlockSpec((rows, D_MODEL), lambda i: (jnp.maximum(n_sub * i - 1, 0), 0))]
    for b in range(1, n_sub):
        specs.append(pl.BlockSpec((rows, D_MODEL), functools.partial(lambda i, b: (n_sub * i - 1 + b, 0), b=b)))
    return specs


def _position():
    return lax.axis_index("x"), lax.axis_index("y"), lax.axis_index("c")


def _gather_weights(wi, wo, small):
    def body(wi_ref, wo_ref, sm_ref, gwi_ref, gwo_ref, gsm_ref, send_sems, recv_sems):
        x, y, c = _position()
        sibling = (x, y, 1 - c)
        chips = [(1 - x, y), (x, 1 - y), (1 - x, 1 - y)]

        def remote(k, src, dst, to):
            return pltpu.make_async_remote_copy(src_ref=src, dst_ref=dst, send_sem=send_sems.at[k],
                                                recv_sem=recv_sems.at[k], device_id=to, device_id_type=MESH)

        first, passed, landed = [], [], []
        for a, (src_ref, g_ref) in enumerate(((wi_ref, gwi_ref), (wo_ref, gwo_ref))):
            for j, (cx, cy) in enumerate(chips):
                slot = g_ref.at[j, c]
                first.append(remote(6 * a + j, src_ref.at[c], slot, (cx, cy, c)))
                landed.append(remote(6 * a + j, slot, slot, sibling))
                passed.append(remote(6 * a + 3 + j, slot, slot, sibling))
        for j, (cx, cy) in enumerate(chips):
            first.append(remote(12 + j, sm_ref, gsm_ref.at[j], (cx, cy, c)))
        for cp in first:
            cp.start()
        for arrived, onward in zip(landed, passed):
            arrived.wait_recv()
            onward.start()
        for a, g_ref in enumerate((gwi_ref, gwo_ref)):
            for j in range(3):
                remote(6 * a + 3 + j, g_ref.at[j, 1 - c], g_ref.at[j, 1 - c], sibling).wait_recv()
        for j in range(3):
            remote(12 + j, sm_ref, gsm_ref.at[j], sibling).wait_recv()
        for cp in first + passed:
            cp.wait_send()

    return pl.pallas_call(
        body, name="gather_weights",
        out_shape=(jax.ShapeDtypeStruct((3,) + wi.shape, wi.dtype), jax.ShapeDtypeStruct((3,) + wo.shape, wo.dtype),
                   jax.ShapeDtypeStruct((3,) + small.shape, small.dtype)),
        in_specs=[ANY, ANY, ANY], out_specs=(ANY, ANY, ANY),
        scratch_shapes=[pltpu.SemaphoreType.DMA((15,)), pltpu.SemaphoreType.DMA((15,))],
    )(wi, wo, small)


def _pair_exchange(gw, gb, pack):
    n_big = N_CHIPS + 1

    def body(gw_ref, gb_ref, p_ref, ra_ref, rb_ref, o_ref, send_sems, recv_sems):
        x, y, c = _position()
        sibling = (x, y, 1 - c)

        def remote(k, src, dst, to):
            return pltpu.make_async_remote_copy(src_ref=src, dst_ref=dst, send_sem=send_sems.at[k],
                                                recv_sem=recv_sems.at[k], device_id=to, device_id_type=MESH)

        copies = [remote(N_CHIPS, gb_ref.at[:, 1 - c], rb_ref, sibling)]
        for s, start in enumerate(WIN_START):
            rows = pl.ds(pl.multiple_of(start + WIN_HALF * (1 - c), 2 * SUBLANE), WIN_HALF)
            copies.append(remote(s, gw_ref.at[rows], ra_ref.at[s], sibling))
        for mask in range(1, N_DEV):
            peer = (1 - x if mask & 4 else x, 1 - y if mask & 2 else y, 1 - c if mask & 1 else c)
            copies.append(remote(n_big + mask - 1, p_ref, o_ref.at[mask - 1], peer))
        for cp in copies:
            cp.start()
        for cp in copies:
            cp.wait()

    n_sems = n_big + N_DEV - 1
    return pl.pallas_call(
        body, name="grad_pair_exchange",
        out_shape=(jax.ShapeDtypeStruct((N_CHIPS, WIN_HALF, D_MODEL), gw.dtype),
                   jax.ShapeDtypeStruct((N_CHIPS,) + gb.shape[2:], gb.dtype),
                   jax.ShapeDtypeStruct((N_DEV - 1,) + pack.shape, pack.dtype)),
        in_specs=[ANY, ANY, ANY], out_specs=(ANY, ANY, ANY),
        scratch_shapes=[pltpu.SemaphoreType.DMA((n_sems,)), pltpu.SemaphoreType.DMA((n_sems,))],
    )(gw, gb, pack)


def _chip_exchange(gw, recv_a, gb, recv_b):
    half, cols = recv_a.shape[1:]
    out_half = recv_b.shape[1]

    def body(gw_ref, ra_in, gb_ref, rb_in, own_a, own_b, ra_ref, rb_ref, ga_buf, pa_buf, wa_buf, oa_buf,
             gb_buf, pb_buf, wb_buf, ob_buf, in_sems, out_sems, send_sems, recv_sems):
        x, y, c = _position()
        chips = [(1 - x, y), (x, 1 - y), (1 - x, 1 - y)]
        windows = [2 * cx + cy for cx, cy in chips] + [2 * x + y]
        w_in_rows = lambda s: gw_ref.at[pl.ds(pl.multiple_of(_window_start(s) + half * c, SUBLANE), half)]
        parts = ((lambda s: gb_ref.at[s, c], rb_in, gb_buf, pb_buf, wb_buf, ob_buf, own_b, rb_ref),
                 (w_in_rows, ra_in, ga_buf, pa_buf, wa_buf, oa_buf, own_a, ra_ref))
        remote, kept = [], []
        for n, (mine, theirs, g_buf, p_buf, wire_buf, own_buf, own_ref, dst) in enumerate(parts):

            def fetch(k):
                slot = k % 2
                return (pltpu.make_async_copy(mine(windows[k]), g_buf.at[slot], in_sems.at[n, 0, slot]),
                        pltpu.make_async_copy(theirs.at[windows[k]], p_buf.at[slot], in_sems.at[n, 1, slot]))

            for cp in fetch(0):
                cp.start()
            for k in range(4):
                if k + 1 < 4:
                    for cp in fetch(k + 1):
                        cp.start()
                for cp in fetch(k):
                    cp.wait()
                total = g_buf[k % 2] + p_buf[k % 2].astype(F32)
                if k < 3:
                    wire_buf[k] = total.astype(wire_buf.dtype)
                    cx, cy = chips[k]
                    remote.append(pltpu.make_async_remote_copy(
                        src_ref=wire_buf.at[k], dst_ref=dst.at[k], send_sem=send_sems.at[3 * n + k],
                        recv_sem=recv_sems.at[3 * n + k], device_id=(cx, cy, c), device_id_type=MESH))
                    remote[-1].start()
                else:
                    own_buf[...] = total
                    kept.append(pltpu.make_async_copy(own_buf, own_ref, out_sems.at[n]))
                    kept[-1].start()
        for cp in remote + kept:
            cp.wait()

    vmem = lambda shape, dtype: pltpu.VMEM(shape, dtype)
    return pl.pallas_call(
        body, name="grad_chip_exchange",
        out_shape=(jax.ShapeDtypeStruct((half, cols), F32), jax.ShapeDtypeStruct((out_half, cols), F32),
                   jax.ShapeDtypeStruct((3, half, cols), WIRE_DTYPE), jax.ShapeDtypeStruct((3, out_half, cols), WIRE_DTYPE)),
        in_specs=[ANY] * 4, out_specs=(ANY,) * 4,
        scratch_shapes=[vmem((2, half, cols), F32), vmem((2, half, cols), recv_a.dtype), vmem((3, half, cols), WIRE_DTYPE),
                        vmem((half, cols), F32),
                        vmem((2, out_half, cols), F32), vmem((2, out_half, cols), recv_b.dtype),
                        vmem((3, out_half, cols), WIRE_DTYPE), vmem((out_half, cols), F32),
                        pltpu.SemaphoreType.DMA((2, 2, 2)), pltpu.SemaphoreType.DMA((2,)),
                        pltpu.SemaphoreType.DMA((6,)), pltpu.SemaphoreType.DMA((6,))],
        compiler_params=pltpu.CompilerParams(vmem_limit_bytes=VMEM_LIMIT_BYTES),
    )(gw, recv_a, gb, recv_b)


def _pair_share(ha, hb):
    def body(ha_ref, hb_ref, oa_ref, ob_ref, send_sems, recv_sems):
        x, y, c = _position()
        copies = [pltpu.make_async_remote_copy(
            src_ref=src, dst_ref=dst, send_sem=send_sems.at[k], recv_sem=recv_sems.at[k],
            device_id=(x, y, 1 - c), device_id_type=MESH)
            for k, (src, dst) in enumerate(((ha_ref, oa_ref), (hb_ref, ob_ref)))]
        for cp in copies:
            cp.start()
        for cp in copies:
            cp.wait()

    return pl.pallas_call(
        body, name="grad_pair_share",
        out_shape=(jax.ShapeDtypeStruct(ha.shape, ha.dtype), jax.ShapeDtypeStruct(hb.shape, hb.dtype)),
        in_specs=[ANY, ANY], out_specs=(ANY, ANY),
        scratch_shapes=[pltpu.SemaphoreType.DMA((2,)), pltpu.SemaphoreType.DMA((2,))],
    )(ha, hb)


def _window_start(s):
    return jnp.where(s == 0, WIN_START[0], jnp.where(s == 1, WIN_START[1], jnp.where(s == 2, WIN_START[2], WIN_START[3])))


def _assemble_w(own, others, starts):
    def body(starts_ref, own_ref, oth_ref, o_ref):
        o_ref[...] = jnp.zeros_like(o_ref)
        for k in range(N_CHIPS):
            rows = pl.ds(pl.multiple_of(starts_ref[k], 2 * SUBLANE), WIN_ROWS)
            o_ref[rows, :] = o_ref[rows, :] + (own_ref[...] if k == 0 else oth_ref[k - 1])

    return pl.pallas_call(
        body, name="assemble_w",
        in_specs=[pl.BlockSpec(memory_space=pltpu.SMEM), pl.BlockSpec(memory_space=pltpu.VMEM),
                  pl.BlockSpec(memory_space=pltpu.VMEM)],
        out_specs=pl.BlockSpec(memory_space=pltpu.VMEM),
        out_shape=jax.ShapeDtypeStruct((D_IN_PAD, D_MODEL), own.dtype),
        compiler_params=_params(),
    )(starts, own, others)


def _chip_sum(psum, recv3, chip_idx):
    rows, cols = psum.shape[1:]
    tr = rows // 2

    def body(s_ref, p_ref, r0, r1, r2, o_ref):
        o_ref[...] = ((p_ref[...] + r0[...].astype(F32)) + r1[...].astype(F32)) + r2[...].astype(F32)

    return pl.pallas_call(
        body, name="grad_chip_sum",
        grid_spec=pltpu.PrefetchScalarGridSpec(
            num_scalar_prefetch=1, grid=(2,),
            in_specs=[pl.BlockSpec((None, tr, cols), lambda i, s_ref: (s_ref[0], i, 0))] +
                     [pl.BlockSpec((None, tr, cols), functools.partial(lambda i, s_ref, j: (j, i, 0), j=j))
                      for j in range(3)],
            out_specs=pl.BlockSpec((tr, cols), lambda i, s_ref: (i, 0))),
        out_shape=jax.ShapeDtypeStruct((rows, cols), psum.dtype),
        compiler_params=_params(("parallel",)),
    )(chip_idx, psum, recv3, recv3, recv3)


def _adamw_math(w, g, m, v):
    m = ADAM_B1 * m + (1.0 - ADAM_B1) * g
    v = ADAM_B2 * v + (1.0 - ADAM_B2) * (g * g)
    m_hat = m * (1.0 / (1.0 - ADAM_B1 ** ADAM_STEP))
    v_hat = v * (1.0 / (1.0 - ADAM_B2 ** ADAM_STEP))
    delta = -ADAM_LR * (m_hat / (jnp.sqrt(v_hat) + ADAM_EPS) + ADAM_WD * w)
    return delta, m, v


def _adamw_big(w, g, m, v, tr):
    rows, cols = w.shape
    assert rows % tr == 0 and g.shape[0] >= rows

    def body(w_ref, g_ref, m_ref, v_ref, d_out, m_out, v_out):
        d, m2, v2 = _adamw_math(w_ref[...], g_ref[...], m_ref[...], v_ref[...])
        d_out[...] = d
        m_out[...] = m2
        v_out[...] = v2

    spec = pl.BlockSpec((tr, cols), lambda i: (i, 0))
    sds = jax.ShapeDtypeStruct((rows, cols), F32)
    return pl.pallas_call(
        body, name="adamw_big", grid=(rows // tr,), in_specs=[spec] * 4, out_specs=(spec,) * 3,
        out_shape=(sds,) * 3, compiler_params=_params(("parallel",)),
    )(w, g, m, v)


def _adamw_rows(w3, g, m3, v3):
    rows, _, cols = w3.shape
    rb = rows // ADAM_ROW_STEPS
    assert rb * ADAM_ROW_STEPS == rows

    def body(w_ref, g_ref, m_ref, v_ref, g_out, d_out, m_out, v_out):
        for k in range(ADAM_ROW_STEPS):
            @pl.when(pl.program_id(0) == k)
            def _(k=k):
                g = g_ref[k * rb:(k + 1) * rb, :]
                d, m2, v2 = _adamw_math(w_ref[:, 0, :], g, m_ref[:, 0, :], v_ref[:, 0, :])
                g_out[:, 0, :] = g
                d_out[:, 0, :] = d
                m_out[:, 0, :] = m2
                v_out[:, 0, :] = v2

    spec3 = pl.BlockSpec((rb, 1, cols), lambda i: (i, 0, 0))
    sds = jax.ShapeDtypeStruct((rows, 1, cols), F32)
    return pl.pallas_call(
        body, name="adamw_rows", grid=(ADAM_ROW_STEPS,),
        in_specs=[spec3, pl.BlockSpec((rows, cols), lambda i: (0, 0), pipeline_mode=pl.Buffered(1)), spec3, spec3],
        out_specs=(spec3,) * 4, out_shape=(sds,) * 4, compiler_params=_params(("parallel",)),
    )(w3, g, m3, v3)


def _small_update(own, others, params, ms, vs):
    slots = (SLOT_NORM, SLOT_FINAL, SLOT_ATTN, SLOT_CONVG, SLOT_BF, SLOT_META, SLOT_CONVW)
    n = len(slots)

    def body(*refs):
        own_ref, gp_ref = refs[:2]
        w_refs, m_refs, v_refs = refs[2:2 + n], refs[2 + n:2 + 2 * n], refs[2 + 2 * n:2 + 3 * n]
        outs = refs[2 + 3 * n:3 + 7 * n]
        loss_ref = outs[0]
        g_outs, d_outs, m_outs, v_outs = (outs[1 + k * n:1 + (k + 1) * n] for k in range(4))
        g_scr, w_scr, m_scr, v_scr = refs[3 + 7 * n:]
        x, y, c = _position()
        shard = 2 * x + y
        me = 4 * x + 2 * y + c
        tot = None
        for d in range(N_DEV):
            rel = jnp.bitwise_xor(me, d)
            term = jnp.where(rel == 0, own_ref[...], gp_ref[jnp.maximum(rel, 1) - 1])
            tot = term if tot is None else tot + term
        r0, r1, _, _ = SLOT_META
        meta_sel = tot[r0:r1, 0:256]
        cw_sel = tot[24:32, 0:128]
        for k in range(1, N_CHIPS):
            meta_sel = jnp.where(shard == k, tot[r0:r1, 256 * k:256 * (k + 1)], meta_sel)
            cw_sel = jnp.where(shard == k, tot[24:32, 128 * k:128 * (k + 1)], cw_sel)
        zeros = jnp.zeros((PACK_ROWS, D_MODEL), F32)
        for scr in (g_scr, w_scr, m_scr, v_scr):
            scr[...] = zeros
        g_scr[0:8, :] = tot[0:8, :]
        g_scr[r0:r1, 0:256] = meta_sel
        g_scr[24:32, 0:128] = cw_sel
        for (a, b, c0, c1), w_ref, m_ref, v_ref in zip(slots, w_refs, m_refs, v_refs):
            w_scr[a:b, c0:c1] = w_ref[...]
            m_scr[a:b, c0:c1] = m_ref[...]
            v_scr[a:b, c0:c1] = v_ref[...]
        loss_ref[...] = g_scr[LOSS_ROW:LOSS_ROW + 1, 0:1]
        d, m2, v2 = _adamw_math(w_scr[...], g_scr[...], m_scr[...], v_scr[...])
        w_scr[...] = d
        m_scr[...] = m2
        v_scr[...] = v2
        for (a, b, c0, c1), g_o, d_o, m_o, v_o in zip(slots, g_outs, d_outs, m_outs, v_outs):
            g_o[...] = g_scr[a:b, c0:c1]
            d_o[...] = w_scr[a:b, c0:c1]
            m_o[...] = m_scr[a:b, c0:c1]
            v_o[...] = v_scr[a:b, c0:c1]

    shapes = [jax.ShapeDtypeStruct(p.shape, F32) for p in params]
    out = pl.pallas_call(
        body, name="small_update",
        out_shape=[jax.ShapeDtypeStruct((1, 1), F32)] + shapes * 4,
        scratch_shapes=[pltpu.VMEM((PACK_ROWS, D_MODEL), F32)] * 4,
        compiler_params=_params(),
    )(own, others, *params, *ms, *vs)
    return out[0], out[1:1 + n], out[1 + n:1 + 2 * n], out[1 + 2 * n:1 + 3 * n], out[1 + 3 * n:1 + 4 * n]


def _in_proj(x2, meta_blk, norm_g, w_pad, bf_pad):
    seq = x2.shape[0]
    lp = seq + FRONT
    t = ROW_TILE
    nt = lp // t
    n_sub = t // LANE

    def body(*refs):
        x_refs = refs[:n_sub]
        mb, g_ref, w_ref, bf_ref, tri_ref = refs[n_sub:n_sub + 5]
        q_ref, k_ref, v_ref, rest_ref, fl_ref, ct_ref, u_ref, qt_ref, kt_ref, vt_ref, cc_ref, carry = refs[n_sub + 5:]
        i = pl.program_id(0)

        @pl.when(i == 0)
        def _():
            carry[...] = jnp.zeros_like(carry)

        first = jnp.where(i == 0, mb[...], x_refs[0][...])
        h = jnp.concatenate([first] + [r[...] for r in x_refs[1:]], axis=0)
        ms = jnp.mean(h * h, axis=-1, keepdims=True)
        u = ((h * lax.rsqrt(ms + EPS)) * g_ref[...]).astype(MXU_DTYPE)
        u_ref[...] = u

        def seg(a, width):
            return _dot_nt(u, w_ref[a:a + width, :])

        fl = seg(SEG_F, LANE)
        fl_ref[...] = fl
        q_tile = seg(SEG_Q, D_ATTN) * (HEAD_DIM ** -0.5)
        q_ref[...] = q_tile.astype(MXU_DTYPE)
        qt_ref[...] = q_tile.T.astype(MXU_DTYPE)
        z = fl + bf_ref[...]
        logf = jnp.minimum(z, 0.0) - jnp.log(1.0 + jnp.exp(-jnp.abs(z)))
        row = i * t + lax.broadcasted_iota(jnp.int32, (t, LANE), 0)
        logf = jnp.where(row >= PAD_ROWS, logf, 0.0)
        k_tile = seg(SEG_K, D_ATTN)
        k_ref[...] = k_tile.astype(MXU_DTYPE)
        kt_ref[...] = k_tile.T.astype(MXU_DTYPE)
        cs = _dot_exact(tri_ref[...], logf) + carry[...]
        carry[...] = carry[...] + jnp.sum(logf, axis=0, keepdims=True)
        v_tile = seg(SEG_V, D_ATTN)
        v_ref[...] = v_tile.astype(MXU_DTYPE)
        vt_ref[...] = v_tile.T.astype(MXU_DTYPE)
        col = i * t + lax.broadcasted_iota(jnp.int32, (SUBLANE, t), 1)
        ct_ref[...] = jnp.where(col >= PAD_ROWS, cs.T[0:SUBLANE, :], -NEG)
        cc_ref[...] = jnp.where(row >= PAD_ROWS, cs, -NEG)
        for s in range(5):
            rest_ref[:, 512 * s:512 * (s + 1)] = seg(SEG_ZA + 512 * s, 512)

    row_blk = lambda cols: pl.BlockSpec((t, cols), lambda i: (i, 0))
    tr_blk = pl.BlockSpec((None, D_ATTN, t), lambda i: (i, 0, 0))
    const = lambda shape: pl.BlockSpec(shape, lambda i: (0, 0))
    return pl.pallas_call(
        body, name="in_proj", grid=(nt,),
        in_specs=_x_block_specs(n_sub, LANE) + [const((LANE, D_MODEL)), const((1, D_MODEL)),
                                                pl.BlockSpec((D_IN_PAD, D_MODEL), lambda i: (0, 0),
                                                             pipeline_mode=pl.Buffered(1)),
                                                const((1, LANE)), const((t, t))],
        out_specs=(row_blk(D_ATTN), row_blk(D_ATTN), row_blk(D_ATTN), row_blk(5 * 512), row_blk(LANE),
                   pl.BlockSpec((SUBLANE, t), lambda i: (0, i)), row_blk(D_MODEL), tr_blk, tr_blk, tr_blk, row_blk(LANE)),
        out_shape=(jax.ShapeDtypeStruct((lp, D_ATTN), MXU_DTYPE), jax.ShapeDtypeStruct((lp, D_ATTN), MXU_DTYPE),
                   jax.ShapeDtypeStruct((lp, D_ATTN), MXU_DTYPE), jax.ShapeDtypeStruct((lp, 5 * 512), F32),
                   jax.ShapeDtypeStruct((lp, LANE), F32),
                   jax.ShapeDtypeStruct((SUBLANE, lp), F32), jax.ShapeDtypeStruct((lp, D_MODEL), MXU_DTYPE),
                   jax.ShapeDtypeStruct((nt, D_ATTN, t), MXU_DTYPE), jax.ShapeDtypeStruct((nt, D_ATTN, t), MXU_DTYPE),
                   jax.ShapeDtypeStruct((nt, D_ATTN, t), MXU_DTYPE), jax.ShapeDtypeStruct((lp, LANE), F32)),
        scratch_shapes=[pltpu.VMEM((1, LANE), F32)],
        compiler_params=_params(("arbitrary",)),
    )(*([x2] * n_sub), meta_blk, norm_g, w_pad, bf_pad, _triangle(t, lower=True))


def _head_masks():
    lane = lax.broadcasted_iota(jnp.int32, (1, LANE), 1)
    return lane < HEAD_DIM, lane >= HEAD_DIM


def _pair_specs(lp, nt, t):
    blk = pl.BlockSpec((lp, LANE), lambda g: (0, g))
    ct_a = pl.BlockSpec((None, nt, 1, t), lambda g: (2 * g, 0, 0, 0))
    ct_b = pl.BlockSpec((None, nt, 1, t), lambda g: (2 * g + 1, 0, 0, 0))
    return blk, ct_a, ct_b


def _sub_rows(s, col):
    return jnp.concatenate([s[:, a * LANE:(a + 1) * LANE] - col for a in range(s.shape[1] // LANE)], axis=1)


def _loop_unrolled(lo, hi, step, init, n):
    def group(jj, carry):
        for k in range(n):
            carry = step(lo + n * jj + k, carry)
        return carry

    groups = (hi - lo) // n
    carry = lax.fori_loop(0, groups, group, init)
    return lax.fori_loop(lo + n * groups, hi, step, carry)


def _attn_fwd(q, k, v_t, cc):
    lp = q.shape[0]
    t = ROW_TILE
    nt = lp // t
    ext = LANE + 2 * SUBLANE

    def body(q_ref, k_ref, vt_ref, cc_ref, o_ref, l_ref, m_ref, s_scr, last_scr, m_scr, mfin_scr, acc_scr, c_scr):
        masks = _head_masks()
        lane = lax.broadcasted_iota(jnp.int32, (1, LANE), 1)
        for hh in range(2):
            picked = jnp.where(lane == 2 * pl.program_id(0) + hh, cc_ref[...], 0.0)
            c_scr[hh] = jnp.broadcast_to(jnp.sum(picked, axis=-1, keepdims=True), (lp, LANE))
        visible = lax.broadcasted_iota(jnp.int32, (t, t), 0) <= lax.broadcasted_iota(jnp.int32, (t, t), 1)
        top = lax.broadcasted_iota(jnp.int32, (LANE, 1), 0) < HEAD_DIM
        second_head = (lax.broadcasted_iota(jnp.int32, (2 * SUBLANE, 2 * t), 1) >= t).astype(jnp.int32)
        ones_rows = jnp.where(lax.broadcasted_iota(jnp.int32, (2 * SUBLANE, 2 * t), 0) == second_head,
                              1.0, 0.0).astype(MXU_DTYPE)

        on_first_diagonal = jnp.concatenate([visible, jnp.ones((t, t), jnp.bool_)], axis=1)

        def scores(j, queries):
            kj = k_ref[pl.ds(pl.multiple_of(j * t, t), t), :]
            return _dot_nt(jnp.concatenate([jnp.where(hm, kj, 0).astype(MXU_DTYPE) for hm in masks], axis=0), queries)

        def biased(s2, j, hh):
            return _sub_rows(s2[hh * t:(hh + 1) * t, :], c_scr[hh, pl.ds(pl.multiple_of(j * t, t), t), :]) * LOG2E

        def track_max(hh, s, lo, hi):
            m = m_scr[hh, :, lo:hi]
            for a in range(t // SUBLANE):
                m = jnp.maximum(m, s[a * SUBLANE:(a + 1) * SUBLANE, :])
            m_scr[hh, :, lo:hi] = m

        def probabilities(scores_of, ms_cols):
            return jnp.concatenate([jnp.exp2(scores_of(hh) - ms_cols[hh]).astype(MXU_DTYPE) for hh in range(2)], axis=0)

        def values(j):
            vtj = vt_ref[j]
            v2 = jnp.concatenate([jnp.where(top, vtj, 0).astype(MXU_DTYPE),
                                  jnp.where(top, 0, vtj).astype(MXU_DTYPE)], axis=1)
            return jnp.concatenate([v2, ones_rows], axis=0)

        def stage(done, ahead):
            if ahead is not None:
                i_a, rows_a = ahead
                qa = q_ref[pl.ds(pl.multiple_of(i_a * t, t), rows_a), :]
                m_scr[...] = jnp.full(m_scr.shape, NEG, F32)

                def max_step(j, mask=None):
                    s2 = scores(j, qa)
                    for hh in range(2):
                        s = biased(s2, j, hh)
                        if mask is not None:
                            s = jnp.where(mask, s, NEG)
                        s_scr[j, hh * t:(hh + 1) * t, 0:rows_a] = s
                        track_max(hh, s, 0, rows_a)

            if done is not None:
                i_d, rows_d = done
                r0 = pl.multiple_of(i_d * t, t)
                ms = [mfin_scr[hh, 0:1, 0:rows_d] for hh in range(2)]
                acc_scr[...] = jnp.zeros(acc_scr.shape, F32)

                def key_step(j, carry):
                    p = probabilities(lambda hh: s_scr[j, hh * t:(hh + 1) * t, 0:rows_d], ms)
                    acc_scr[:, 0:rows_d] = acc_scr[:, 0:rows_d] + _dot(values(j), p)
                    if ahead is not None:
                        max_step(j)
                    return carry

                _loop_unrolled(0, i_d + 1, key_step, 0, ATTN_UNROLL)
                if rows_d == 2 * t:
                    p = probabilities(lambda hh: last_scr[hh * t:(hh + 1) * t, :], [m[:, t:] for m in ms])
                    acc_scr[:, t:rows_d] = acc_scr[:, t:rows_d] + _dot(values(i_d + 1), p)
                acc = acc_scr[:, 0:rows_d]
                l_pair = jnp.where(top, acc[LANE:LANE + 1], acc[LANE + 1:LANE + 2])
                o_ref[pl.ds(r0, rows_d), :] = (acc[:LANE] / l_pair).T
                l_ref[pl.ds(r0, rows_d), :] = l_pair.T
                for hh in range(2):
                    m_ref[pl.ds(r0, rows_d), hh * LANE:(hh + 1) * LANE] = jnp.broadcast_to(ms[hh], (LANE, rows_d)).T

            if ahead is not None:
                if done is not None:
                    max_step(i_a - 1)
                max_step(i_a, on_first_diagonal if rows_a == 2 * t else visible)
                if rows_a == 2 * t:
                    s2 = scores(i_a + 1, qa[t:])
                    for hh in range(2):
                        s = jnp.where(visible, biased(s2, i_a + 1, hh), NEG)
                        last_scr[hh * t:(hh + 1) * t, :] = s
                        track_max(hh, s, t, rows_a)
                for hh in range(2):
                    mfin_scr[hh, :, 0:rows_a] = jnp.broadcast_to(jnp.max(m_scr[hh, :, 0:rows_a], axis=0, keepdims=True),
                                                                 (SUBLANE, rows_a))

        pairs = nt // 2
        stage(None, (0, 2 * t))

        def pair_to_pair(u, _):
            stage((2 * u, 2 * t), (2 * u + 2, 2 * t))
            return 0

        lax.fori_loop(0, pairs - 1, pair_to_pair, 0)
        if nt % 2:
            stage((2 * pairs - 2, 2 * t), (nt - 1, t))
            stage((nt - 1, t), None)
        else:
            stage((2 * pairs - 2, 2 * t), None)

    blk = pl.BlockSpec((lp, LANE), lambda g: (0, g))
    return pl.pallas_call(
        body, name="attn_fwd", grid=(HEADS // 2,),
        in_specs=[blk, blk, pl.BlockSpec((nt, LANE, t), lambda g: (0, g, 0)),
                  pl.BlockSpec((lp, LANE), lambda g: (0, 0), pipeline_mode=pl.Buffered(1))],
        out_specs=(blk, blk, pl.BlockSpec((lp, 2 * LANE), lambda g: (0, g))),
        out_shape=(jax.ShapeDtypeStruct((lp, D_ATTN), F32), jax.ShapeDtypeStruct((lp, D_ATTN), F32),
                   jax.ShapeDtypeStruct((lp, HEADS * LANE), F32)),
        scratch_shapes=[pltpu.VMEM((nt, 2 * t, 2 * t), F32), pltpu.VMEM((2 * t, t), F32),
                        pltpu.VMEM((2, SUBLANE, 2 * t), F32), pltpu.VMEM((2, SUBLANE, 2 * t), F32),
                        pltpu.VMEM((ext, 2 * t), F32), pltpu.VMEM((2, lp, LANE), F32)],
        compiler_params=_params(("parallel",)),
    )(q, k, v_t, cc)


def _attn_bwd(q, k, v, do, q_t, k_t, do_t, m, neg_delta, ct4):
    lp = q.shape[0]
    t = ROW_TILE
    nt = lp // t

    def body(q_ref, k_ref, v_ref, do_ref, qt_ref, kt_ref, dot_ref, ma_ref, mb_ref, nd_ref, cta_ref, ctb_ref,
             dq_ref, dk_ref, dv_ref, dc_ref, dq_acc, dk_acc, dv_acc):
        masks = _head_masks()
        ct_refs, m_refs = (cta_ref, ctb_ref), (ma_ref, mb_ref)
        row_head = 2 * pl.program_id(0) + (lax.broadcasted_iota(jnp.int32, (2 * t, LANE), 0) >= t).astype(jnp.int32)
        col = lax.broadcasted_iota(jnp.int32, (2 * t, LANE), 1)
        delta_ones = jnp.where((col < HEADS * DELTA_TERMS) & (col % HEADS == row_head), 1.0, 0.0).astype(MXU_DTYPE)
        below = lax.broadcasted_iota(jnp.int32, (t, t), 1) <= lax.broadcasted_iota(jnp.int32, (t, t), 0)
        top = lax.broadcasted_iota(jnp.int32, (LANE, 1), 0) < HEAD_DIM
        dq_acc[...] = jnp.zeros_like(dq_acc)

        on_first_diagonal = jnp.concatenate([below, jnp.ones((t, t), jnp.bool_)], axis=0)

        def k_block(j, _, with_next=True):
            c0 = pl.multiple_of(j * t, t)
            kj = k_ref[pl.ds(c0, t), :]
            vj = v_ref[pl.ds(c0, t), :]
            k2 = jnp.concatenate([jnp.where(hm, kj, 0).astype(MXU_DTYPE) for hm in masks], axis=0)
            v2 = jnp.concatenate([jnp.where(hm, vj, 0).astype(MXU_DTYPE) for hm in masks], axis=0)
            v2 = jnp.concatenate([v2, delta_ones], axis=1)
            ck = [r[j] for r in ct_refs]
            ktj = kt_ref[j]
            k2t = jnp.concatenate([jnp.where(top, ktj, 0).astype(MXU_DTYPE), jnp.where(top, 0, ktj).astype(MXU_DTYPE)],
                                  axis=1)
            dk_acc[...] = jnp.zeros_like(dk_acc)
            dv_acc[...] = jnp.zeros_like(dv_acc)

            def q_block(i, colsums, mask=None, rows=t):
                r0 = pl.multiple_of(i * t, t)
                qi = q_ref[pl.ds(r0, rows), :]
                doi = jnp.concatenate([do_ref[pl.ds(r0, rows), :], nd_ref[pl.ds(r0, rows), :]], axis=1)
                qti = jnp.concatenate([qt_ref[i + b] for b in range(rows // t)], axis=1)
                doti = jnp.concatenate([dot_ref[i + b] for b in range(rows // t)], axis=1)
                s2 = _dot_nt(qi, k2)
                dp2 = _dot_nt(doi, v2)
                out, ps, dss = [], [], []
                for hh in range(2):
                    s = (s2[:, hh * t:(hh + 1) * t] - ck[hh]) * LOG2E
                    if mask is not None:
                        s = jnp.where(mask, s, NEG)
                    p = jnp.exp2(_sub_rows(s, m_refs[hh][pl.ds(r0, rows), :])).astype(MXU_DTYPE)
                    ds32 = p.astype(F32) * dp2[:, hh * t:(hh + 1) * t]
                    ps.append(p)
                    dss.append(ds32.astype(MXU_DTYPE))
                    out.append(colsums[hh] + jnp.sum(ds32, axis=0, keepdims=True))
                ds_cat = jnp.concatenate(dss, axis=1)
                dv_acc[...] = dv_acc[...] + _dot(doti, jnp.concatenate(ps, axis=1))
                dk_acc[...] = dk_acc[...] + _dot(qti, ds_cat)
                dq_t = _dot(k2t, ds_cat.T)
                for b in range(rows // t):
                    dq_acc[i + b] = dq_acc[i + b] + dq_t[:, b * t:(b + 1) * t]
                return tuple(out)

            nq = ATTN_BWD_QBLOCKS
            colsums = (jnp.zeros((1, t), F32), jnp.zeros((1, t), F32))
            if with_next:
                colsums = q_block(j, colsums, on_first_diagonal, nq * t)
            else:
                colsums = q_block(j, colsums, below)
            first = j + (nq if with_next else 1)
            groups = (nt - first) // nq
            colsums = lax.fori_loop(0, groups, lambda p, c: q_block(first + nq * p, c, None, nq * t), colsums)
            colsums = lax.fori_loop(first + nq * groups, nt, q_block, colsums)
            for hh in range(2):
                dc_ref[hh, j] = -colsums[hh]
            own = lambda acc: jnp.concatenate([acc[:HEAD_DIM, :t], acc[HEAD_DIM:, t:]], axis=0).T
            dk_ref[pl.ds(c0, t), :] = own(dk_acc[...]).astype(dk_ref.dtype)
            dv_ref[pl.ds(c0, t), :] = own(dv_acc[...]).astype(dv_ref.dtype)
            return 0

        lax.fori_loop(0, nt - 1, k_block, 0)
        k_block(nt - 1, 0, with_next=False)
        for i in range(nt):
            dq_ref[i * t:(i + 1) * t, :] = (dq_acc[i].T * (HEAD_DIM ** -0.5)).astype(dq_ref.dtype)

    blk, ct_a, ct_b = _pair_specs(lp, nt, t)
    rep_a = pl.BlockSpec((lp, LANE), lambda g: (0, 2 * g))
    rep_b = pl.BlockSpec((lp, LANE), lambda g: (0, 2 * g + 1))
    tr_blk = pl.BlockSpec((nt, LANE, t), lambda g: (0, g, 0))
    return pl.pallas_call(
        body, name="attn_bwd", grid=(HEADS // 2,),
        in_specs=[blk] * 4 + [tr_blk, tr_blk, tr_blk, rep_a, rep_b, pl.BlockSpec((lp, LANE), lambda g: (0, 0)), ct_a, ct_b],
        out_specs=(blk, blk, blk, pl.BlockSpec((2, nt, 1, t), lambda g: (g, 0, 0, 0))),
        out_shape=(jax.ShapeDtypeStruct((lp, D_ATTN), MXU_DTYPE),) * 3
                  + (jax.ShapeDtypeStruct((HEADS, nt, 1, t), F32),),
        scratch_shapes=[pltpu.VMEM((nt, LANE, t), F32), pltpu.VMEM((LANE, 2 * t), F32), pltpu.VMEM((LANE, 2 * t), F32)],
        compiler_params=_params(("parallel",)),
    )(q, k, v, do, q_t, k_t, do_t, m, m, neg_delta, ct4, ct4)


def _shift_down(prev8, cur, k):
    ext = jnp.concatenate([prev8, cur], axis=0)
    return pltpu.roll(ext, k, 0)[SUBLANE:, :]


def _shift_up(cur, next8, k):
    ext = jnp.concatenate([cur, next8], axis=0)
    n = ext.shape[0]
    return pltpu.roll(ext, n - k, 0)[:cur.shape[0], :]


def _post(o, l_sum, rest, x2, meta_blk, tgt2, w_out, attn_g, conv_g, final_g, conv_w8):
    lp = o.shape[0]
    t = ROW_TILE
    nt = lp // t
    n_sub = t // LANE
    hb = t // SUBLANE

    def body(*refs):
        o_ref, l_ref, za_ref, gb_ref, gc_ref, xc_ref, zc_ref, gch_ref, xch_ref = refs[:9]
        x_refs = refs[9:9 + n_sub]
        mb = refs[9 + n_sub]
        t_refs = refs[10 + n_sub:10 + 2 * n_sub]
        wo_ref, ag_ref, cg_ref, fg_ref, cw_ref, gm_ref, hr_ref = refs[10 + 2 * n_sub:17 + 2 * n_sub]
        (dout_ref, do_ref, dot_ref, dl_ref, dza_ref, dgb_ref, dzc_ref, dcv_ref,
         loss_ref, gf_ref, gag_ref, gcg_ref, gwo_ref) = refs[17 + 2 * n_sub:]
        i = pl.program_id(0)

        @pl.when(i == 0)
        def _():
            for r in (loss_ref, gf_ref, gag_ref, gcg_ref, gwo_ref):
                r[...] = jnp.zeros_like(r)

        gmat = gm_ref[...]
        inv_g = 1.0 / HEAD_DIM
        o_v = o_ref[...]
        ra = lax.rsqrt(_group_sum(o_v * o_v, gmat, STAT_TERMS) * inv_g + EPS)
        n_a = o_v * ra
        a_n = n_a * ag_ref[...]
        za = za_ref[...]
        sig_a = _sigmoid(za)
        sz_a = za * sig_a
        y_a = a_n * sz_a
        gb = gb_ref[...]
        gc = gc_ref[...]
        xc = xc_ref[...]
        cx = gc * xc
        cx_prev = jnp.where(i == 0, 0.0, gch_ref[...] * xch_ref[...])
        conv = (cw_ref[0:1, :] * _shift_down(cx_prev, cx, 2) + cw_ref[1:2, :] * _shift_down(cx_prev, cx, 1)
                + cw_ref[2:3, :] * cx)
        e = gb * conv
        re = lax.rsqrt(_group_sum(e * e, gmat, STAT_TERMS) * inv_g + EPS)
        n_e = e * re
        e_n = n_e * cg_ref[...]
        zc = zc_ref[...]
        sig_c = _sigmoid(zc)
        sz_c = zc * sig_c
        y_c = e_n * sz_c
        mix = jnp.concatenate([y_a, y_c], axis=-1)
        mix_b = mix.astype(MXU_DTYPE)
        first = jnp.where(i == 0, mb[...], x_refs[0][...])
        h = jnp.concatenate([first] + [r[...] for r in x_refs[1:]], axis=0)
        out = h + _dot(mix_b, wo_ref[...])
        r2 = lax.rsqrt(jnp.mean(out * out, axis=-1, keepdims=True) + EPS)
        n_f = out * r2
        y = n_f * fg_ref[...]
        tgt = jnp.concatenate([r[...] for r in t_refs], axis=0)
        valid = (i * t + lax.broadcasted_iota(jnp.int32, (t, 1), 0)) >= FRONT
        diff = jnp.where(valid, y - tgt, 0.0)
        loss_ref[...] = loss_ref[...] + 0.5 * jnp.sum(jnp.sum(diff * diff, axis=-1, keepdims=True) * (1.0 / D_MODEL))
        dy = diff * (1.0 / D_MODEL)
        gf_ref[...] = gf_ref[...] + jnp.sum(dy * n_f, axis=0, keepdims=True)
        dn = dy * fg_ref[...]
        d_out = r2 * (dn - n_f * jnp.mean(dn * n_f, axis=-1, keepdims=True))
        dout_ref[...] = d_out
        d_out_b = d_out.astype(MXU_DTYPE)
        d_mix = _dot_nt(d_out_b, wo_ref[...])
        gwo_ref[...] = gwo_ref[...] + _dot(mix.T.astype(MXU_DTYPE), d_out_b)
        d_ya = d_mix[:, :D_ATTN]
        d_yc = d_mix[:, D_ATTN:]
        d_an = d_ya * sz_a
        dza_ref[...] = (d_ya * a_n * (sig_a * (1.0 + za * (1.0 - sig_a)))).astype(dza_ref.dtype)
        gag_ref[...] = gag_ref[...] + jnp.sum(d_an * n_a, axis=0, keepdims=True)
        dn_a = d_an * ag_ref[...]
        d_o = ra * (dn_a - n_a * (_group_sum(dn_a * n_a, gmat, STAT_TERMS) * inv_g))
        d_o_l = d_o / l_ref[...]
        d_o_b = d_o_l.astype(do_ref.dtype)
        do_ref[...] = d_o_b
        dot_ref[...] = d_o_l.T.astype(dot_ref.dtype)
        delta = _group_sum(d_o_b.astype(F32) * o_v, hr_ref[...])
        terms, rest_of = [], delta
        for k in range(DELTA_TERMS):
            terms.append(rest_of.astype(MXU_DTYPE).astype(F32))
            rest_of = rest_of - terms[-1]
        dl_ref[...] = -sum(pltpu.roll(term, HEADS * k, 1) if k else term
                           for k, term in enumerate(terms)).astype(dl_ref.dtype)
        d_en = d_yc * sz_c
        dzc_ref[...] = (d_yc * e_n * (sig_c * (1.0 + zc * (1.0 - sig_c)))).astype(dzc_ref.dtype)
        gcg_ref[...] = gcg_ref[...] + jnp.sum(d_en * n_e, axis=0, keepdims=True)
        dn_e = d_en * cg_ref[...]
        d_e = re * (dn_e - n_e * (_group_sum(dn_e * n_e, gmat, STAT_TERMS) * inv_g))
        dgb_ref[...] = (d_e * conv).astype(dgb_ref.dtype)
        dcv_ref[...] = d_e * gb

    head_rep = jnp.where((lax.broadcasted_iota(jnp.int32, (D_ATTN, LANE), 0) >> 6)
                         == lax.broadcasted_iota(jnp.int32, (D_ATTN, LANE), 1), 1.0, 0.0).astype(MXU_DTYPE)
    row_blk = lambda cols: pl.BlockSpec((t, cols), lambda i: (i, 0))
    rest_blk = lambda s: pl.BlockSpec((t, 512), functools.partial(lambda i, s: (i, s), s=s))
    halo = lambda s: pl.BlockSpec((SUBLANE, 512), functools.partial(lambda i, s: (jnp.maximum(i * hb - 1, 0), s), s=s))
    const = lambda shape: pl.BlockSpec(shape, lambda i: (0, 0))
    acc = lambda shape: pl.BlockSpec(shape, lambda i: (0, 0))
    return pl.pallas_call(
        body, name="post_fwd_bwd", grid=(nt,),
        in_specs=[row_blk(D_ATTN), row_blk(D_ATTN)] + [rest_blk(s) for s in range(5)] + [halo(2), halo(3)]
                 + _x_block_specs(n_sub, LANE) + [const((LANE, D_MODEL))] + _x_block_specs(n_sub, LANE)
                 + [const((D_MODEL, D_MODEL)), const((1, D_ATTN)), const((1, D_CONV)), const((1, D_MODEL)),
                    const((SUBLANE, D_CONV)), const((D_ATTN, D_ATTN)), const((D_ATTN, LANE))],
        out_specs=(row_blk(D_MODEL), row_blk(D_ATTN), pl.BlockSpec((None, D_ATTN, t), lambda i: (i, 0, 0)),
                   row_blk(LANE), row_blk(D_ATTN), row_blk(D_CONV),
                   row_blk(D_CONV), row_blk(D_CONV),
                   acc((1, LANE)), acc((1, D_MODEL)), acc((1, D_ATTN)), acc((1, D_CONV)), acc((D_MODEL, D_MODEL))),
        out_shape=(jax.ShapeDtypeStruct((lp, D_MODEL), F32), jax.ShapeDtypeStruct((lp, D_ATTN), MXU_DTYPE),
                   jax.ShapeDtypeStruct((nt, D_ATTN, t), MXU_DTYPE), jax.ShapeDtypeStruct((lp, LANE), MXU_DTYPE),
                   jax.ShapeDtypeStruct((lp, D_ATTN), MXU_DTYPE),
                   jax.ShapeDtypeStruct((lp, D_CONV), MXU_DTYPE), jax.ShapeDtypeStruct((lp, D_CONV), MXU_DTYPE),
                   jax.ShapeDtypeStruct((lp, D_CONV), F32),
                   jax.ShapeDtypeStruct((1, LANE), F32), jax.ShapeDtypeStruct((1, D_MODEL), F32),
                   jax.ShapeDtypeStruct((1, D_ATTN), F32), jax.ShapeDtypeStruct((1, D_CONV), F32),
                   jax.ShapeDtypeStruct((D_MODEL, D_MODEL), F32)),
        compiler_params=_params(("arbitrary",)),
    )(o, l_sum, *([rest] * 5), rest, rest, *([x2] * n_sub), meta_blk, *([tgt2] * n_sub),
      w_out, attn_g, conv_g, final_g, conv_w8, _group_matrix(), head_rep)


def _bwd_in(x2, meta_blk, norm_g, w_pad, bf_pad, fl, dc, dq, dk, dv, dza, dgb, dzc, dconv, rest, d_out, conv_w8):
    lp = fl.shape[0]
    t = ROW_TILE
    nt = lp // t
    n_sub = t // LANE
    hb = t // SUBLANE
    rev = lambda i: nt - 1 - i

    def body(*refs):
        x_refs = refs[:n_sub]
        (mb, g_ref, w_ref, bf_ref, fl_ref, dc_ref, dq_ref, dk_ref, dv_ref, dza_ref, dgb_ref, dzc_ref,
         dcv_ref, dcvn_ref, gc_ref, xc_ref, gch_ref, xch_ref, dout_ref, cw_ref, tri_ref) = refs[n_sub:n_sub + 21]
        dp_ref, gx_ref, front_ref, gn_ref, gbf_ref, gcw_ref, carry, dh_scr, gx_sems = refs[n_sub + 21:]
        step = pl.program_id(0)
        i = rev(step)

        @pl.when(step == 0)
        def _():
            for r in (gn_ref, gbf_ref, gcw_ref, carry):
                r[...] = jnp.zeros_like(r)

        dc8 = jnp.concatenate([dc_ref[...], jnp.zeros((LANE - HEADS, t), F32)], axis=0).T
        dlogf = _dot_exact(tri_ref[...], dc8) + carry[...]
        carry[...] = carry[...] + jnp.sum(dc8, axis=0, keepdims=True)
        z = fl_ref[...] + bf_ref[...]
        row = i * t + lax.broadcasted_iota(jnp.int32, (t, LANE), 0)
        d_f = jnp.where(row >= PAD_ROWS, dlogf * (1.0 / (1.0 + jnp.exp(z))), 0.0)
        gbf_ref[...] = gbf_ref[...] + jnp.sum(d_f, axis=0, keepdims=True)
        dcv = dcv_ref[...]
        dcv_next = jnp.where(i == nt - 1, 0.0, dcvn_ref[...])
        d_cx = (cw_ref[2:3, :] * dcv + cw_ref[1:2, :] * _shift_up(dcv, dcv_next, 1)
                + cw_ref[0:1, :] * _shift_up(dcv, dcv_next, 2))
        gc = gc_ref[...]
        xc = xc_ref[...]
        cx = gc * xc
        cx_prev = jnp.where(i == 0, 0.0, gch_ref[...] * xch_ref[...])
        rowi = lax.broadcasted_iota(jnp.int32, (SUBLANE, 1), 0)
        gcw = (jnp.where(rowi == 0, jnp.sum(dcv * _shift_down(cx_prev, cx, 2), axis=0, keepdims=True), 0.0)
               + jnp.where(rowi == 1, jnp.sum(dcv * _shift_down(cx_prev, cx, 1), axis=0, keepdims=True), 0.0)
               + jnp.where(rowi == 2, jnp.sum(dcv * cx, axis=0, keepdims=True), 0.0))
        gcw_ref[...] = gcw_ref[...] + gcw
        dp_ref[:, SEG_Q:SEG_Q + 512] = dq_ref[...]
        dp_ref[:, SEG_K:SEG_K + 512] = dk_ref[...]
        dp_ref[:, SEG_V:SEG_V + 512] = dv_ref[...]
        dp_ref[:, SEG_F:SEG_F + LANE] = d_f.astype(dp_ref.dtype)
        dp_ref[:, SEG_ZA:SEG_ZA + 512] = dza_ref[...]
        dp_ref[:, SEG_GB:SEG_GB + 512] = dgb_ref[...]
        dp_ref[:, SEG_GC:SEG_GC + 512] = (d_cx * xc).astype(dp_ref.dtype)
        dp_ref[:, SEG_XC:SEG_XC + 512] = (d_cx * gc).astype(dp_ref.dtype)
        dp_ref[:, SEG_ZC:SEG_ZC + 512] = dzc_ref[...]
        d_u = _dot(dp_ref[...], w_ref[...])
        first = jnp.where(i == 0, mb[...], x_refs[0][...])
        h = jnp.concatenate([first] + [r[...] for r in x_refs[1:]], axis=0)
        r1 = lax.rsqrt(jnp.mean(h * h, axis=-1, keepdims=True) + EPS)
        n_h = h * r1
        gn_ref[...] = gn_ref[...] + jnp.sum(d_u * n_h, axis=0, keepdims=True)
        dn = d_u * g_ref[...]
        d_h = dout_ref[...] + r1 * (dn - n_h * jnp.mean(dn * n_h, axis=-1, keepdims=True))
        slot = step % 2

        def to_grad_x(slot_, tile):
            return pltpu.make_async_copy(dh_scr.at[slot_], gx_ref.at[pl.ds(pl.multiple_of(tile * t - FRONT, SUBLANE), t)],
                                         gx_sems.at[slot_])

        @pl.when(step >= 2)
        def _():
            to_grad_x(slot, 1).wait()

        dh_scr[slot] = d_h

        @pl.when(i > 0)
        def _():
            to_grad_x(slot, i).start()

        @pl.when(i == 0)
        def _():
            front_ref[...] = d_h[:FRONT]
            rest_rows = pltpu.make_async_copy(dh_scr.at[slot, pl.ds(FRONT, t - FRONT)], gx_ref.at[pl.ds(0, t - FRONT)],
                                              gx_sems.at[slot])
            rest_rows.start()
            rest_rows.wait()
            if nt >= 2:
                to_grad_x(1 - slot, 1).wait()

    def x_specs():
        specs = [pl.BlockSpec((LANE, D_MODEL), lambda s: (jnp.maximum(n_sub * rev(s) - 1, 0), 0))]
        for b in range(1, n_sub):
            specs.append(pl.BlockSpec((LANE, D_MODEL), functools.partial(lambda s, b: (n_sub * rev(s) - 1 + b, 0), b=b)))
        return specs

    row_blk = lambda cols: pl.BlockSpec((t, cols), lambda s: (rev(s), 0))
    rest_blk = lambda k: pl.BlockSpec((t, 512), functools.partial(lambda s, k: (rev(s), k), k=k))
    halo_prev = lambda k: pl.BlockSpec(
        (SUBLANE, 512), functools.partial(lambda s, k: (jnp.maximum(rev(s) * hb - 1, 0), k), k=k))
    halo_next = pl.BlockSpec((SUBLANE, 512), lambda s: (jnp.minimum((rev(s) + 1) * hb, lp // SUBLANE - 1), 0))
    const = lambda shape: pl.BlockSpec(shape, lambda s: (0, 0))
    return pl.pallas_call(
        body, name="bwd_in", grid=(nt,),
        in_specs=x_specs() + [const((LANE, D_MODEL)), const((1, D_MODEL)),
                              pl.BlockSpec((D_IN_PAD, D_MODEL), lambda s: (0, 0), pipeline_mode=pl.Buffered(1)),
                              const((1, LANE)), row_blk(LANE),
                              pl.BlockSpec((HEADS, t), lambda s: (0, rev(s))),
                              row_blk(512), row_blk(512), row_blk(512), row_blk(512), row_blk(512), row_blk(512),
                              row_blk(512), halo_next, rest_blk(2), rest_blk(3), halo_prev(2), halo_prev(3),
                              row_blk(D_MODEL), const((SUBLANE, D_CONV)), const((t, t))],
        out_specs=(row_blk(D_IN_PAD), ANY, const((FRONT, D_MODEL)), const((1, D_MODEL)), const((1, LANE)),
                   const((SUBLANE, D_CONV))),
        out_shape=(jax.ShapeDtypeStruct((lp, D_IN_PAD), MXU_DTYPE), jax.ShapeDtypeStruct((lp - FRONT, D_MODEL), F32),
                   jax.ShapeDtypeStruct((FRONT, D_MODEL), F32),
                   jax.ShapeDtypeStruct((1, D_MODEL), F32), jax.ShapeDtypeStruct((1, LANE), F32),
                   jax.ShapeDtypeStruct((SUBLANE, D_CONV), F32)),
        scratch_shapes=[pltpu.VMEM((1, LANE), F32), pltpu.VMEM((2, t, D_MODEL), F32), pltpu.SemaphoreType.DMA((2,))],
        compiler_params=_params(("arbitrary",)),
    )(*([x2] * n_sub), meta_blk, norm_g, w_pad, bf_pad, fl, dc, dq, dk, dv, dza, dgb, dzc, dconv, dconv,
      rest, rest, rest, rest, d_out, conv_w8, _triangle(t, lower=False))


def _grad_w_in(u, dproj):
    lp = u.shape[0]
    tn = GW_COL_TILE
    tk = tn if lp % tn == 0 else ROW_TILE

    def body(d_ref, u_ref, o_ref, wire_ref):
        k = pl.program_id(1)

        @pl.when(k == 0)
        def _():
            o_ref[...] = jnp.zeros_like(o_ref)

        o_ref[...] = o_ref[...] + lax.dot_general(d_ref[...], u_ref[...], (((0,), (0,)), ((), ())),
                                                  preferred_element_type=F32)

        @pl.when(k == pl.num_programs(1) - 1)
        def _():
            wire_ref[...] = o_ref[...].astype(wire_ref.dtype)

    out_spec = pl.BlockSpec((tn, D_MODEL), lambda n, k: (n, 0))
    return pl.pallas_call(
        body, name="grad_w_in", grid=(D_IN_PAD // tn, lp // tk),
        in_specs=[pl.BlockSpec((tk, tn), lambda n, k: (k, n)), pl.BlockSpec((tk, D_MODEL), lambda n, k: (k, 0))],
        out_specs=(out_spec, out_spec),
        out_shape=(jax.ShapeDtypeStruct((D_IN_PAD, D_MODEL), F32), jax.ShapeDtypeStruct((D_IN_PAD, D_MODEL), WIRE_DTYPE)),
        compiler_params=_params(("parallel", "arbitrary")),
    )(dproj, u)


def _by_chip(own, others, me):
    by_mask = jnp.stack([own, others[1], others[0], others[2]])
    return [lax.dynamic_index_in_dim(by_mask, jnp.bitwise_xor(me, s), 0, keepdims=False) for s in range(N_CHIPS)]


def _both_halves(mine, other, c):
    return jnp.where(c == 0, jnp.concatenate([mine, other], axis=0), jnp.concatenate([other, mine], axis=0))


def _local_step(x2, tgt2, meta_full, norm_g, w_pad, b_f, conv_w_full, attn_g, conv_g, w_out_full, final_g):
    lp = x2.shape[0] + FRONT
    nt = lp // ROW_TILE
    meta_blk = jnp.concatenate([jnp.zeros((PAD_ROWS, D_MODEL), F32), meta_full], axis=0)
    bf_pad = jnp.pad(b_f, ((0, 0), (0, LANE - HEADS)))
    conv_w8 = jnp.pad(conv_w_full, ((0, SUBLANE - conv_w_full.shape[0]), (0, 0)))
    q, k, v, rest, fl, ct, u, q_t, k_t, v_t, cc = _in_proj(x2, meta_blk, norm_g, w_pad, bf_pad)
    ct4 = ct.reshape(SUBLANE, nt, 1, ROW_TILE)
    o, l_sum, m_max = _attn_fwd(q, k, v_t, cc)
    (d_out, d_o, do_t, neg_delta, dza, dgb, dzc, dconv, loss, g_final, g_attn, g_convg, gw_out) = _post(
        o, l_sum, rest, x2, meta_blk, tgt2, w_out_full, attn_g, conv_g, final_g, conv_w8)
    dq, dk, dv, dc = _attn_bwd(q, k, v, d_o, q_t, k_t, do_t, m_max, neg_delta, ct4)
    dproj, grad_x, d_front, g_norm, g_bf, g_cw = _bwd_in(x2, meta_blk, norm_g, w_pad, bf_pad, fl, dc.reshape(HEADS, lp), dq, dk, dv,
                                             dza, dgb, dzc, dconv, rest, d_out, conv_w8)
    gw_in, gw_in_wire = _grad_w_in(u, dproj)
    return dict(loss=loss, grad_x=grad_x, d_front=d_front, g_norm=g_norm, g_final=g_final, g_attn=g_attn, g_convg=g_convg, g_bf=g_bf,
                g_cw=g_cw, gw_out=gw_out, gw_in=gw_in, gw_in_wire=gw_in_wire)


def kernel(x, meta, norm_g, w_in, b_f, conv_w, attn_norm_g, conv_norm_g, w_out, final_norm_g, loss_target, m_meta, m_norm_g, m_w_in, m_b_f, m_conv_w, m_attn_norm_g, m_conv_norm_g, m_w_out, m_final_norm_g, v_meta, v_norm_g, v_w_in, v_b_f, v_conv_w, v_attn_norm_g, v_conv_norm_g, v_w_out, v_final_norm_g):
    cx_, cy_, cc_ = _position()
    chip = 2 * cx_ + cy_
    shard = w_in.shape[2]
    out_half = w_out.shape[1] // 2
    pick = lambda vals: jnp.where(chip == 0, vals[0], jnp.where(chip == 1, vals[1], jnp.where(chip == 2, vals[2], vals[3])))
    a_off, b_off = pick(A_OFF), pick(B_OFF)
    wt = jnp.transpose(w_in[0]).astype(MXU_DTYPE)

    def placed(piece, off):
        return lax.dynamic_slice_in_dim(jnp.pad(piece, ((WIN_ROWS, WIN_ROWS), (0, 0))), WIN_ROWS - off, WIN_ROWS, 0)

    wi = placed(wt[:PIECE_A], a_off) + placed(wt[PIECE_A:], b_off)
    wo = w_out[0].astype(MXU_DTYPE)
    small = jnp.concatenate([meta, jnp.pad(conv_w[0], ((0, 8 - conv_w.shape[1]), (0, meta.shape[1] - conv_w.shape[2])))],
                            axis=0)
    gwi, gwo, gsm = _gather_weights(wi.reshape(2, WIN_HALF, D_MODEL), wo.reshape(2, out_half, D_MODEL), small)
    starts = jnp.stack([_window_start(jnp.bitwise_xor(chip, mask)) for mask in (0, 2, 1, 3)]).astype(jnp.int32)
    w_pad = _assemble_w(wi, gwi.reshape(3, WIN_ROWS, D_MODEL), starts)
    w_out_full = jnp.concatenate(_by_chip(wo, gwo.reshape(3, 2 * out_half, D_MODEL), chip), axis=0)
    small_full = jnp.concatenate(_by_chip(small, gsm, chip), axis=1)
    meta_full = small_full[:N_META]
    conv_w_full = jnp.concatenate([small_full[N_META:N_META + 3, 256 * s:256 * s + LANE] for s in range(N_CHIPS)], axis=1)
    final_g2 = final_norm_g.reshape(1, D_MODEL)
    r = _local_step(x[0], loss_target[0], meta_full, norm_g, w_pad, b_f, conv_w_full, attn_norm_g, conv_norm_g,
                    w_out_full, final_g2)
    grad_x = r["grad_x"][None]
    gb = r["gw_out"].reshape(N_CHIPS, 2, out_half, D_MODEL)
    wide = lambda a: jnp.pad(a, ((0, 0), (0, D_MODEL - a.shape[1])))
    pack = jnp.concatenate([
        r["g_norm"], r["g_final"], jnp.concatenate([r["g_attn"], r["g_convg"]], axis=1), wide(r["g_bf"]),
        wide(r["loss"]), jnp.zeros((3, D_MODEL), F32), r["d_front"][PAD_ROWS:], wide(r["g_cw"])], axis=0)
    ra, rb, packs = _pair_exchange(r["gw_in_wire"], gb, pack)
    pa_own, pb_own, xa, xb = _chip_exchange(r["gw_in"], ra, gb, rb)
    first = jnp.zeros((1,), jnp.int32)
    ha = _chip_sum(pa_own[None], xa, first)
    hb = _chip_sum(pb_own[None], xb, first)
    oa, ob = _pair_share(ha, hb)
    g_window = _both_halves(ha, oa, cc_)
    g_w_in_t = jnp.concatenate([lax.dynamic_slice_in_dim(g_window, a_off, PIECE_A, 0),
                                lax.dynamic_slice_in_dim(g_window, b_off, shard - PIECE_A, 0)], axis=0)
    g_w_out = _both_halves(hb, ob, cc_)
    as_rows = lambda a: jnp.transpose(a, (2, 0, 1))
    g_w_in, d_w_in, nm_w_in, nv_w_in = (jnp.transpose(a, (1, 2, 0)) for a in _adamw_rows(
        as_rows(w_in), g_w_in_t, as_rows(m_w_in), as_rows(v_w_in)))
    d_w_out, nm_w_out, nv_w_out = (a[None] for a in _adamw_big(w_out[0], g_w_out, m_w_out[0], v_w_out[0], LANE))
    params = (norm_g, final_g2, attn_norm_g, conv_norm_g, b_f, meta, conv_w[0])
    ms = (m_norm_g, m_final_norm_g.reshape(1, D_MODEL), m_attn_norm_g, m_conv_norm_g, m_b_f, m_meta, m_conv_w[0])
    vs = (v_norm_g, v_final_norm_g.reshape(1, D_MODEL), v_attn_norm_g, v_conv_norm_g, v_b_f, v_meta, v_conv_w[0])
    loss, g_s, d_s, m_s, v_s = _small_update(pack, packs, params, ms, vs)

    def ordered(small_list, big_in, big_out):
        s_norm, s_final, s_attn, s_convg, s_bf, s_meta, s_cw = small_list
        return (s_meta, s_norm, big_in, s_bf, s_cw[None], s_attn, s_convg, big_out, s_final.reshape(D_MODEL))

    return (loss.reshape(()), grad_x,
            *ordered(g_s, g_w_in, g_w_out[None]), *ordered(d_s, d_w_in, d_w_out),
            *ordered(m_s, nm_w_in, nm_w_out), *ordered(v_s, nv_w_in, nv_w_out))
```

```python
import functools

import jax
import jax.numpy as jnp
from jax import lax
from jax.experimental import pallas as pl
from jax.experimental.pallas import tpu as pltpu

F32 = jnp.float32
MXU_DTYPE = jnp.bfloat16
WIRE_DTYPE = jnp.bfloat16

D_MODEL = 1024
N_META = 16
HEADS = 8
HEAD_DIM = 64
D_ATTN = HEADS * HEAD_DIM
D_CONV = 512
EPS = 1e-6
LANE = 128
SUBLANE = 8
ROW_TILE = 384
ATTN_UNROLL = 3
ATTN_BWD_QBLOCKS = 2
DELTA_TERMS = 3
ADAM_ROW_STEPS = 3
STAT_TERMS = 1
FRONT = LANE
PAD_ROWS = FRONT - N_META
NEG = -1e30
LOG2E = 1.4426950408889634
N_CHIPS = 4
N_DEV = 8
VMEM_LIMIT_BYTES = 60 * 1024 * 1024

SEG_Q, SEG_K, SEG_V, SEG_F, SEG_ZA, SEG_GB, SEG_GC, SEG_XC, SEG_ZC = (
    0, 512, 1024, 1536, 1664, 2176, 2688, 3200, 3712)
D_IN = 4104
D_IN_PAD = 4224
F_END = 1544
GW_COL_TILE = 1408
WIN_ROWS = 1152
WIN_HALF = WIN_ROWS // 2
WIN_START = (0, 1024, 2160, 3072)
PIECE_A = 518
A_OFF = (0, 2, 12, 126)
B_OFF = (518, 640, 530, 644)
ADAM_LR = 0.001
ADAM_B1 = 0.9
ADAM_B2 = 0.999
ADAM_EPS = 1e-08
ADAM_WD = 0.01
ADAM_STEP = 10

MESH = pl.DeviceIdType.MESH
ANY = pl.BlockSpec(memory_space=pl.ANY)

PACK_ROWS = 32
SLOT_NORM = (0, 1, 0, 1024)
SLOT_FINAL = (1, 2, 0, 1024)
SLOT_ATTN = (2, 3, 0, 512)
SLOT_CONVG = (2, 3, 512, 1024)
SLOT_BF = (3, 4, 0, 8)
SLOT_META = (8, 24, 0, 256)
SLOT_CONVW = (24, 27, 0, 128)
LOSS_ROW = 4


def _params(sem=None):
    return pltpu.CompilerParams(dimension_semantics=sem, vmem_limit_bytes=VMEM_LIMIT_BYTES)


def _sigmoid(z):
    return 1.0 / (1.0 + jnp.exp(-z))


def _dot(a, b):
    return jnp.dot(a, b, preferred_element_type=F32)


def _dot_nt(a, b):
    return lax.dot_general(a, b, (((1,), (1,)), ((), ())), preferred_element_type=F32)


def _dot_exact(ones, x):
    ones = ones.astype(MXU_DTYPE)
    total = None
    for _ in range(3):
        term = x.astype(MXU_DTYPE)
        x = x - term.astype(F32)
        total = _dot(ones, term) if total is None else total + _dot(ones, term)
    return total


def _group_matrix():
    r = lax.broadcasted_iota(jnp.int32, (D_ATTN, D_ATTN), 0) >> 6
    c = lax.broadcasted_iota(jnp.int32, (D_ATTN, D_ATTN), 1) >> 6
    return jnp.where(r == c, 1.0, 0.0).astype(MXU_DTYPE)


def _triangle(n, lower):
    r = lax.broadcasted_iota(jnp.int32, (n, n), 0)
    c = lax.broadcasted_iota(jnp.int32, (n, n), 1)
    return jnp.where((r >= c) if lower else (c >= r), 1.0, 0.0).astype(MXU_DTYPE)


def _group_sum(x, gmat, terms=2):
    hi = x.astype(MXU_DTYPE)
    if terms == 1:
        return _dot(hi, gmat)
    lo = (x - hi.astype(F32)).astype(MXU_DTYPE)
    return _dot(hi, gmat) + _dot(lo, gmat)


def _x_block_specs(n_sub, rows):
    specs = [pl.BlockSpec((rows, D_MODEL), lambda i: (jnp.maximum(n_sub * i - 1, 0), 0))]
    for b in range(1, n_sub):
        specs.append(pl.BlockSpec((rows, D_MODEL), functools.partial(lambda i, b: (n_sub * i - 1 + b, 0), b=b)))
    return specs


def _position():
    return lax.axis_index("x"), lax.axis_index("y"), lax.axis_index("c")


def _gather_weights(wi, wo, small):
    def body(wi_ref, wo_ref, sm_ref, gwi_ref, gwo_ref, gsm_ref, send_sems, recv_sems):
        x, y, c = _position()
        sibling = (x, y, 1 - c)
        chips = [(1 - x, y), (x, 1 - y), (1 - x, 1 - y)]

        def remote(k, src, dst, to):
            return pltpu.make_async_remote_copy(src_ref=src, dst_ref=dst, send_sem=send_sems.at[k],
                                                recv_sem=recv_sems.at[k], device_id=to, device_id_type=MESH)

        first, passed, landed = [], [], []
        for a, (src_ref, g_ref) in enumerate(((wi_ref, gwi_ref), (wo_ref, gwo_ref))):
            for j, (cx, cy) in enumerate(chips):
                slot = g_ref.at[j, c]
                first.append(remote(6 * a + j, src_ref.at[c], slot, (cx, cy, c)))
                landed.append(remote(6 * a + j, slot, slot, sibling))
                passed.append(remote(6 * a + 3 + j, slot, slot, sibling))
        for j, (cx, cy) in enumerate(chips):
            first.append(remote(12 + j, sm_ref, gsm_ref.at[j], (cx, cy, c)))
        for cp in first:
            cp.start()
        for arrived, onward in zip(landed, passed):
            arrived.wait_recv()
            onward.start()
        for a, g_ref in enumerate((gwi_ref, gwo_ref)):
            for j in range(3):
                remote(6 * a + 3 + j, g_ref.at[j, 1 - c], g_ref.at[j, 1 - c], sibling).wait_recv()
        for j in range(3):
            remote(12 + j, sm_ref, gsm_ref.at[j], sibling).wait_recv()
        for cp in first + passed:
            cp.wait_send()

    return pl.pallas_call(
        body, name="gather_weights",
        out_shape=(jax.ShapeDtypeStruct((3,) + wi.shape, wi.dtype), jax.ShapeDtypeStruct((3,) + wo.shape, wo.dtype),
                   jax.ShapeDtypeStruct((3,) + small.shape, small.dtype)),
        in_specs=[ANY, ANY, ANY], out_specs=(ANY, ANY, ANY),
        scratch_shapes=[pltpu.SemaphoreType.DMA((15,)), pltpu.SemaphoreType.DMA((15,))],
    )(wi, wo, small)


def _pair_exchange(gw, gb, pack):
    n_big = N_CHIPS + 1

    def body(gw_ref, gb_ref, p_ref, ra_ref, rb_ref, o_ref, send_sems, recv_sems):
        x, y, c = _position()
        sibling = (x, y, 1 - c)

        def remote(k, src, dst, to):
            return pltpu.make_async_remote_copy(src_ref=src, dst_ref=dst, send_sem=send_sems.at[k],
                                                recv_sem=recv_sems.at[k], device_id=to, device_id_type=MESH)

        copies = [remote(N_CHIPS, gb_ref.at[:, 1 - c], rb_ref, sibling)]
        for s, start in enumerate(WIN_START):
            rows = pl.ds(pl.multiple_of(start + WIN_HALF * (1 - c), 2 * SUBLANE), WIN_HALF)
            copies.append(remote(s, gw_ref.at[rows], ra_ref.at[s], sibling))
        for mask in range(1, N_DEV):
            peer = (1 - x if mask & 4 else x, 1 - y if mask & 2 else y, 1 - c if mask & 1 else c)
            copies.append(remote(n_big + mask - 1, p_ref, o_ref.at[mask - 1], peer))
        for cp in copies:
            cp.start()
        for cp in copies:
            cp.wait()

    n_sems = n_big + N_DEV - 1
    return pl.pallas_call(
        body, name="grad_pair_exchange",
        out_shape=(jax.ShapeDtypeStruct((N_CHIPS, WIN_HALF, D_MODEL), gw.dtype),
                   jax.ShapeDtypeStruct((N_CHIPS,) + gb.shape[2:], gb.dtype),
                   jax.ShapeDtypeStruct((N_DEV - 1,) + pack.shape, pack.dtype)),
        in_specs=[ANY, ANY, ANY], out_specs=(ANY, ANY, ANY),
        scratch_shapes=[pltpu.SemaphoreType.DMA((n_sems,)), pltpu.SemaphoreType.DMA((n_sems,))],
    )(gw, gb, pack)


def _chip_exchange(gw, recv_a, gb, recv_b):
    half, cols = recv_a.shape[1:]
    out_half = recv_b.shape[1]

    def body(gw_ref, ra_in, gb_ref, rb_in, own_a, own_b, ga_buf, pa_buf, wa_buf, oa_buf, xa_buf,
             gb_buf, pb_buf, wb_buf, ob_buf, xb_buf, in_sems, out_sems, send_sems, recv_sems):
        x, y, c = _position()
        chips = [(1 - x, y), (x, 1 - y), (1 - x, 1 - y)]
        windows = [2 * cx + cy for cx, cy in chips] + [2 * x + y]
        w_in_rows = lambda s: gw_ref.at[pl.ds(pl.multiple_of(_window_start(s) + half * c, SUBLANE), half)]
        parts = ((lambda s: gb_ref.at[s, c], rb_in, gb_buf, pb_buf, wb_buf, ob_buf, own_b, xb_buf),
                 (w_in_rows, ra_in, ga_buf, pa_buf, wa_buf, oa_buf, own_a, xa_buf))
        remote = []
        for n, (mine, theirs, g_buf, p_buf, wire_buf, own_buf, own_ref, dst) in enumerate(parts):

            def fetch(k):
                slot = k % 2
                return (pltpu.make_async_copy(mine(windows[k]), g_buf.at[slot], in_sems.at[n, 0, slot]),
                        pltpu.make_async_copy(theirs.at[windows[k]], p_buf.at[slot], in_sems.at[n, 1, slot]))

            for cp in fetch(0):
                cp.start()
            for k in range(4):
                if k + 1 < 4:
                    for cp in fetch(k + 1):
                        cp.start()
                for cp in fetch(k):
                    cp.wait()
                total = g_buf[k % 2] + p_buf[k % 2].astype(F32)
                if k < 3:
                    wire_buf[k] = total.astype(wire_buf.dtype)
                    cx, cy = chips[k]
                    remote.append(pltpu.make_async_remote_copy(
                        src_ref=wire_buf.at[k], dst_ref=dst.at[k], send_sem=send_sems.at[3 * n + k],
                        recv_sem=recv_sems.at[3 * n + k], device_id=(cx, cy, c), device_id_type=MESH))
                    remote[-1].start()
                else:
                    own_buf[...] = total
        for cp in remote:
            cp.wait()
        kept = []
        for n, (_, _, _, _, _, own_buf, own_ref, arrived) in enumerate(parts):
            own_buf[...] = ((own_buf[...] + arrived[0].astype(F32)) + arrived[1].astype(F32)) + arrived[2].astype(F32)
            kept.append(pltpu.make_async_copy(own_buf, own_ref, out_sems.at[n]))
            kept[-1].start()
        for cp in kept:
            cp.wait()

    vmem = lambda shape, dtype: pltpu.VMEM(shape, dtype)
    return pl.pallas_call(
        body, name="grad_chip_exchange",
        out_shape=(jax.ShapeDtypeStruct((half, cols), F32), jax.ShapeDtypeStruct((out_half, cols), F32)),
        in_specs=[ANY] * 4, out_specs=(ANY,) * 2,
        scratch_shapes=[vmem((2, half, cols), F32), vmem((2, half, cols), recv_a.dtype), vmem((3, half, cols), WIRE_DTYPE),
                        vmem((half, cols), F32), vmem((3, half, cols), WIRE_DTYPE),
                        vmem((2, out_half, cols), F32), vmem((2, out_half, cols), recv_b.dtype),
                        vmem((3, out_half, cols), WIRE_DTYPE), vmem((out_half, cols), F32),
                        vmem((3, out_half, cols), WIRE_DTYPE),
                        pltpu.SemaphoreType.DMA((2, 2, 2)), pltpu.SemaphoreType.DMA((2,)),
                        pltpu.SemaphoreType.DMA((6,)), pltpu.SemaphoreType.DMA((6,))],
        compiler_params=pltpu.CompilerParams(vmem_limit_bytes=VMEM_LIMIT_BYTES),
    )(gw, recv_a, gb, recv_b)


def _pair_share(ha, hb):
    def body(ha_ref, hb_ref, oa_ref, ob_ref, send_sems, recv_sems):
        x, y, c = _position()
        copies = [pltpu.make_async_remote_copy(
            src_ref=src, dst_ref=dst, send_sem=send_sems.at[k], recv_sem=recv_sems.at[k],
            device_id=(x, y, 1 - c), device_id_type=MESH)
            for k, (src, dst) in enumerate(((ha_ref, oa_ref), (hb_ref, ob_ref)))]
        for cp in copies:
            cp.start()
        for cp in copies:
            cp.wait()

    return pl.pallas_call(
        body, name="grad_pair_share",
        out_shape=(jax.ShapeDtypeStruct(ha.shape, ha.dtype), jax.ShapeDtypeStruct(hb.shape, hb.dtype)),
        in_specs=[ANY, ANY], out_specs=(ANY, ANY),
        scratch_shapes=[pltpu.SemaphoreType.DMA((2,)), pltpu.SemaphoreType.DMA((2,))],
    )(ha, hb)


def _window_start(s):
    return jnp.where(s == 0, WIN_START[0], jnp.where(s == 1, WIN_START[1], jnp.where(s == 2, WIN_START[2], WIN_START[3])))


def _assemble_w(own, others, starts):
    def body(starts_ref, own_ref, oth_ref, o_ref):
        o_ref[...] = jnp.zeros_like(o_ref)
        for k in range(N_CHIPS):
            rows = pl.ds(pl.multiple_of(starts_ref[k], 2 * SUBLANE), WIN_ROWS)
            o_ref[rows, :] = o_ref[rows, :] + (own_ref[...] if k == 0 else oth_ref[k - 1])

    return pl.pallas_call(
        body, name="assemble_w",
        in_specs=[pl.BlockSpec(memory_space=pltpu.SMEM), pl.BlockSpec(memory_space=pltpu.VMEM),
                  pl.BlockSpec(memory_space=pltpu.VMEM)],
        out_specs=pl.BlockSpec(memory_space=pltpu.VMEM),
        out_shape=jax.ShapeDtypeStruct((D_IN_PAD, D_MODEL), own.dtype),
        compiler_params=_params(),
    )(starts, own, others)


def _adamw_math(w, g, m, v):
    m = ADAM_B1 * m + (1.0 - ADAM_B1) * g
    v = ADAM_B2 * v + (1.0 - ADAM_B2) * (g * g)
    m_hat = m * (1.0 / (1.0 - ADAM_B1 ** ADAM_STEP))
    v_hat = v * (1.0 / (1.0 - ADAM_B2 ** ADAM_STEP))
    delta = -ADAM_LR * (m_hat / (jnp.sqrt(v_hat) + ADAM_EPS) + ADAM_WD * w)
    return delta, m, v


def _adamw_big(w, g, m, v, tr):
    rows, cols = w.shape
    assert rows % tr == 0 and g.shape[0] >= rows

    def body(w_ref, g_ref, m_ref, v_ref, d_out, m_out, v_out):
        d, m2, v2 = _adamw_math(w_ref[...], g_ref[...], m_ref[...], v_ref[...])
        d_out[...] = d
        m_out[...] = m2
        v_out[...] = v2

    spec = pl.BlockSpec((tr, cols), lambda i: (i, 0))
    sds = jax.ShapeDtypeStruct((rows, cols), F32)
    return pl.pallas_call(
        body, name="adamw_big", grid=(rows // tr,), in_specs=[spec] * 4, out_specs=(spec,) * 3,
        out_shape=(sds,) * 3, compiler_params=_params(("parallel",)),
    )(w, g, m, v)


def _adamw_rows(w3, g, m3, v3):
    rows, _, cols = w3.shape
    rb = rows // ADAM_ROW_STEPS
    assert rb * ADAM_ROW_STEPS == rows

    def body(w_ref, g_ref, m_ref, v_ref, g_out, d_out, m_out, v_out):
        for k in range(ADAM_ROW_STEPS):
            @pl.when(pl.program_id(0) == k)
            def _(k=k):
                g = g_ref[k * rb:(k + 1) * rb, :]
                d, m2, v2 = _adamw_math(w_ref[:, 0, :], g, m_ref[:, 0, :], v_ref[:, 0, :])
                g_out[:, 0, :] = g
                d_out[:, 0, :] = d
                m_out[:, 0, :] = m2
                v_out[:, 0, :] = v2

    spec3 = pl.BlockSpec((rb, 1, cols), lambda i: (i, 0, 0))
    sds = jax.ShapeDtypeStruct((rows, 1, cols), F32)
    return pl.pallas_call(
        body, name="adamw_rows", grid=(ADAM_ROW_STEPS,),
        in_specs=[spec3, pl.BlockSpec((rows, cols), lambda i: (0, 0), pipeline_mode=pl.Buffered(1)), spec3, spec3],
        out_specs=(spec3,) * 4, out_shape=(sds,) * 4, compiler_params=_params(("parallel",)),
    )(w3, g, m3, v3)


def _small_update(own, others, params, ms, vs):
    slots = (SLOT_NORM, SLOT_FINAL, SLOT_ATTN, SLOT_CONVG, SLOT_BF, SLOT_META, SLOT_CONVW)
    n = len(slots)

    def body(*refs):
        own_ref, gp_ref = refs[:2]
        w_refs, m_refs, v_refs = refs[2:2 + n], refs[2 + n:2 + 2 * n], refs[2 + 2 * n:2 + 3 * n]
        outs = refs[2 + 3 * n:3 + 7 * n]
        loss_ref = outs[0]
        g_outs, d_outs, m_outs, v_outs = (outs[1 + k * n:1 + (k + 1) * n] for k in range(4))
        g_scr, w_scr, m_scr, v_scr = refs[3 + 7 * n:]
        x, y, c = _position()
        shard = 2 * x + y
        me = 4 * x + 2 * y + c
        tot = None
        for d in range(N_DEV):
            rel = jnp.bitwise_xor(me, d)
            term = jnp.where(rel == 0, own_ref[...], gp_ref[jnp.maximum(rel, 1) - 1])
            tot = term if tot is None else tot + term
        r0, r1, _, _ = SLOT_META
        meta_sel = tot[r0:r1, 0:256]
        cw_sel = tot[24:32, 0:128]
        for k in range(1, N_CHIPS):
            meta_sel = jnp.where(shard == k, tot[r0:r1, 256 * k:256 * (k + 1)], meta_sel)
            cw_sel = jnp.where(shard == k, tot[24:32, 128 * k:128 * (k + 1)], cw_sel)
        zeros = jnp.zeros((PACK_ROWS, D_MODEL), F32)
        for scr in (g_scr, w_scr, m_scr, v_scr):
            scr[...] = zeros
        g_scr[0:8, :] = tot[0:8, :]
        g_scr[r0:r1, 0:256] = meta_sel
        g_scr[24:32, 0:128] = cw_sel
        for (a, b, c0, c1), w_ref, m_ref, v_ref in zip(slots, w_refs, m_refs, v_refs):
            w_scr[a:b, c0:c1] = w_ref[...]
            m_scr[a:b, c0:c1] = m_ref[...]
            v_scr[a:b, c0:c1] = v_ref[...]
        loss_ref[...] = g_scr[LOSS_ROW:LOSS_ROW + 1, 0:1]
        d, m2, v2 = _adamw_math(w_scr[...], g_scr[...], m_scr[...], v_scr[...])
        w_scr[...] = d
        m_scr[...] = m2
        v_scr[...] = v2
        for (a, b, c0, c1), g_o, d_o, m_o, v_o in zip(slots, g_outs, d_outs, m_outs, v_outs):
            g_o[...] = g_scr[a:b, c0:c1]
            d_o[...] = w_scr[a:b, c0:c1]
            m_o[...] = m_scr[a:b, c0:c1]
            v_o[...] = v_scr[a:b, c0:c1]

    shapes = [jax.ShapeDtypeStruct(p.shape, F32) for p in params]
    out = pl.pallas_call(
        body, name="small_update",
        out_shape=[jax.ShapeDtypeStruct((1, 1), F32)] + shapes * 4,
        scratch_shapes=[pltpu.VMEM((PACK_ROWS, D_MODEL), F32)] * 4,
        compiler_params=_params(),
    )(own, others, *params, *ms, *vs)
    return out[0], out[1:1 + n], out[1 + n:1 + 2 * n], out[1 + 2 * n:1 + 3 * n], out[1 + 3 * n:1 + 4 * n]


def _in_proj(x2, meta_blk, norm_g, w_pad, bf_pad):
    seq = x2.shape[0]
    lp = seq + FRONT
    t = ROW_TILE
    nt = lp // t
    n_sub = t // LANE

    def body(*refs):
        x_refs = refs[:n_sub]
        mb, g_ref, w_ref, bf_ref, tri_ref = refs[n_sub:n_sub + 5]
        q_ref, k_ref, v_ref, rest_ref, fl_ref, ct_ref, u_ref, qt_ref, kt_ref, vt_ref, cc_ref, carry = refs[n_sub + 5:]
        i = pl.program_id(0)

        @pl.when(i == 0)
        def _():
            carry[...] = jnp.zeros_like(carry)

        first = jnp.where(i == 0, mb[...], x_refs[0][...])
        h = jnp.concatenate([first] + [r[...] for r in x_refs[1:]], axis=0)
        ms = jnp.mean(h * h, axis=-1, keepdims=True)
        u = ((h * lax.rsqrt(ms + EPS)) * g_ref[...]).astype(MXU_DTYPE)
        u_ref[...] = u

        def seg(a, width):
            return _dot_nt(u, w_ref[a:a + width, :])

        fl = seg(SEG_F, LANE)
        fl_ref[...] = fl
        q_tile = seg(SEG_Q, D_ATTN) * (HEAD_DIM ** -0.5)
        q_ref[...] = q_tile.astype(MXU_DTYPE)
        qt_ref[...] = q_tile.T.astype(MXU_DTYPE)
        z = fl + bf_ref[...]
        logf = jnp.minimum(z, 0.0) - jnp.log(1.0 + jnp.exp(-jnp.abs(z)))
        row = i * t + lax.broadcasted_iota(jnp.int32, (t, LANE), 0)
        logf = jnp.where(row >= PAD_ROWS, logf, 0.0)
        k_tile = seg(SEG_K, D_ATTN)
        k_ref[...] = k_tile.astype(MXU_DTYPE)
        kt_ref[...] = k_tile.T.astype(MXU_DTYPE)
        cs = _dot_exact(tri_ref[...], logf) + carry[...]
        carry[...] = carry[...] + jnp.sum(logf, axis=0, keepdims=True)
        v_tile = seg(SEG_V, D_ATTN)
        v_ref[...] = v_tile.astype(MXU_DTYPE)
        vt_ref[...] = v_tile.T.astype(MXU_DTYPE)
        col = i * t + lax.broadcasted_iota(jnp.int32, (SUBLANE, t), 1)
        ct_ref[...] = jnp.where(col >= PAD_ROWS, cs.T[0:SUBLANE, :], -NEG)
        cc_ref[...] = jnp.where(row >= PAD_ROWS, cs, -NEG)
        for s in range(5):
            rest_ref[:, 512 * s:512 * (s + 1)] = seg(SEG_ZA + 512 * s, 512)

    row_blk = lambda cols: pl.BlockSpec((t, cols), lambda i: (i, 0))
    tr_blk = pl.BlockSpec((None, D_ATTN, t), lambda i: (i, 0, 0))
    const = lambda shape: pl.BlockSpec(shape, lambda i: (0, 0))
    return pl.pallas_call(
        body, name="in_proj", grid=(nt,),
        in_specs=_x_block_specs(n_sub, LANE) + [const((LANE, D_MODEL)), const((1, D_MODEL)),
                                                pl.BlockSpec((D_IN_PAD, D_MODEL), lambda i: (0, 0),
                                                             pipeline_mode=pl.Buffered(1)),
                                                const((1, LANE)), const((t, t))],
        out_specs=(row_blk(D_ATTN), row_blk(D_ATTN), row_blk(D_ATTN), row_blk(5 * 512), row_blk(LANE),
                   pl.BlockSpec((SUBLANE, t), lambda i: (0, i)), row_blk(D_MODEL), tr_blk, tr_blk, tr_blk, row_blk(LANE)),
        out_shape=(jax.ShapeDtypeStruct((lp, D_ATTN), MXU_DTYPE), jax.ShapeDtypeStruct((lp, D_ATTN), MXU_DTYPE),
                   jax.ShapeDtypeStruct((lp, D_ATTN), MXU_DTYPE), jax.ShapeDtypeStruct((lp, 5 * 512), F32),
                   jax.ShapeDtypeStruct((lp, LANE), F32),
                   jax.ShapeDtypeStruct((SUBLANE, lp), F32), jax.ShapeDtypeStruct((lp, D_MODEL), MXU_DTYPE),
                   jax.ShapeDtypeStruct((nt, D_ATTN, t), MXU_DTYPE), jax.ShapeDtypeStruct((nt, D_ATTN, t), MXU_DTYPE),
                   jax.ShapeDtypeStruct((nt, D_ATTN, t), MXU_DTYPE), jax.ShapeDtypeStruct((lp, LANE), F32)),
        scratch_shapes=[pltpu.VMEM((1, LANE), F32)],
        compiler_params=_params(("arbitrary",)),
    )(*([x2] * n_sub), meta_blk, norm_g, w_pad, bf_pad, _triangle(t, lower=True))


def _head_masks():
    lane = lax.broadcasted_iota(jnp.int32, (1, LANE), 1)
    return lane < HEAD_DIM, lane >= HEAD_DIM


def _pair_specs(lp, nt, t):
    blk = pl.BlockSpec((lp, LANE), lambda g: (0, g))
    ct_a = pl.BlockSpec((None, nt, 1, t), lambda g: (2 * g, 0, 0, 0))
    ct_b = pl.BlockSpec((None, nt, 1, t), lambda g: (2 * g + 1, 0, 0, 0))
    return blk, ct_a, ct_b


def _sub_rows(s, col):
    return jnp.concatenate([s[:, a * LANE:(a + 1) * LANE] - col for a in range(s.shape[1] // LANE)], axis=1)


def _loop_unrolled(lo, hi, step, init, n):
    def group(jj, carry):
        for k in range(n):
            carry = step(lo + n * jj + k, carry)
        return carry

    groups = (hi - lo) // n
    carry = lax.fori_loop(0, groups, group, init)
    return lax.fori_loop(lo + n * groups, hi, step, carry)


def _attn_fwd(q, k, v_t, cc):
    lp = q.shape[0]
    t = ROW_TILE
    nt = lp // t
    ext = LANE + 2 * SUBLANE

    def body(q_ref, k_ref, vt_ref, cc_ref, o_ref, l_ref, m_ref, s_scr, last_scr, m_scr, mfin_scr, acc_scr, c_scr):
        masks = _head_masks()
        lane = lax.broadcasted_iota(jnp.int32, (1, LANE), 1)
        for hh in range(2):
            picked = jnp.where(lane == 2 * pl.program_id(0) + hh, cc_ref[...], 0.0)
            c_scr[hh] = jnp.broadcast_to(jnp.sum(picked, axis=-1, keepdims=True), (lp, LANE))
        visible = lax.broadcasted_iota(jnp.int32, (t, t), 0) <= lax.broadcasted_iota(jnp.int32, (t, t), 1)
        top = lax.broadcasted_iota(jnp.int32, (LANE, 1), 0) < HEAD_DIM
        second_head = (lax.broadcasted_iota(jnp.int32, (2 * SUBLANE, 2 * t), 1) >= t).astype(jnp.int32)
        ones_rows = jnp.where(lax.broadcasted_iota(jnp.int32, (2 * SUBLANE, 2 * t), 0) == second_head,
                              1.0, 0.0).astype(MXU_DTYPE)

        on_first_diagonal = jnp.concatenate([visible, jnp.ones((t, t), jnp.bool_)], axis=1)

        def scores(j, queries):
            kj = k_ref[pl.ds(pl.multiple_of(j * t, t), t), :]
            return _dot_nt(jnp.concatenate([jnp.where(hm, kj, 0).astype(MXU_DTYPE) for hm in masks], axis=0), queries)

        def biased(s2, j, hh):
            return _sub_rows(s2[hh * t:(hh + 1) * t, :], c_scr[hh, pl.ds(pl.multiple_of(j * t, t), t), :]) * LOG2E

        def track_max(hh, s, lo, hi):
            m = m_scr[hh, :, lo:hi]
            for a in range(t // SUBLANE):
                m = jnp.maximum(m, s[a * SUBLANE:(a + 1) * SUBLANE, :])
            m_scr[hh, :, lo:hi] = m

        def probabilities(scores_of, ms_cols):
            return jnp.concatenate([jnp.exp2(scores_of(hh) - ms_cols[hh]).astype(MXU_DTYPE) for hh in range(2)], axis=0)

        def values(j):
            vtj = vt_ref[j]
            v2 = jnp.concatenate([jnp.where(top, vtj, 0).astype(MXU_DTYPE),
                                  jnp.where(top, 0, vtj).astype(MXU_DTYPE)], axis=1)
            return jnp.concatenate([v2, ones_rows], axis=0)

        def stage(done, ahead):
            if ahead is not None:
                i_a, rows_a = ahead
                qa = q_ref[pl.ds(pl.multiple_of(i_a * t, t), rows_a), :]
                m_scr[...] = jnp.full(m_scr.shape, NEG, F32)

                def max_step(j, mask=None):
                    s2 = scores(j, qa)
                    for hh in range(2):
                        s = biased(s2, j, hh)
                        if mask is not None:
                            s = jnp.where(mask, s, NEG)
                        s_scr[j, hh * t:(hh + 1) * t, 0:rows_a] = s
                        track_max(hh, s, 0, rows_a)

            if done is not None:
                i_d, rows_d = done
                r0 = pl.multiple_of(i_d * t, t)
                ms = [mfin_scr[hh, 0:1, 0:rows_d] for hh in range(2)]
                acc_scr[...] = jnp.zeros(acc_scr.shape, F32)

                def key_step(j, carry):
                    p = probabilities(lambda hh: s_scr[j, hh * t:(hh + 1) * t, 0:rows_d], ms)
                    acc_scr[:, 0:rows_d] = acc_scr[:, 0:rows_d] + _dot(values(j), p)
                    if ahead is not None:
                        max_step(j)
                    return carry

                _loop_unrolled(0, i_d + 1, key_step, 0, ATTN_UNROLL)
                if rows_d == 2 * t:
                    p = probabilities(lambda hh: last_scr[hh * t:(hh + 1) * t, :], [m[:, t:] for m in ms])
                    acc_scr[:, t:rows_d] = acc_scr[:, t:rows_d] + _dot(values(i_d + 1), p)
                acc = acc_scr[:, 0:rows_d]
                l_pair = jnp.where(top, acc[LANE:LANE + 1], acc[LANE + 1:LANE + 2])
                o_ref[pl.ds(r0, rows_d), :] = (acc[:LANE] / l_pair).T
                l_ref[pl.ds(r0, rows_d), :] = l_pair.T
                for hh in range(2):
                    m_ref[pl.ds(r0, rows_d), hh * LANE:(hh + 1) * LANE] = jnp.broadcast_to(ms[hh], (LANE, rows_d)).T

            if ahead is not None:
                if done is not None:
                    max_step(i_a - 1)
                max_step(i_a, on_first_diagonal if rows_a == 2 * t else visible)
                if rows_a == 2 * t:
                    s2 = scores(i_a + 1, qa[t:])
                    for hh in range(2):
                        s = jnp.where(visible, biased(s2, i_a + 1, hh), NEG)
                        last_scr[hh * t:(hh + 1) * t, :] = s
                        track_max(hh, s, t, rows_a)
                for hh in range(2):
                    mfin_scr[hh, :, 0:rows_a] = jnp.broadcast_to(jnp.max(m_scr[hh, :, 0:rows_a], axis=0, keepdims=True),
                                                                 (SUBLANE, rows_a))

        pairs = nt // 2
        stage(None, (0, 2 * t))

        def pair_to_pair(u, _):
            stage((2 * u, 2 * t), (2 * u + 2, 2 * t))
            return 0

        lax.fori_loop(0, pairs - 1, pair_to_pair, 0)
        if nt % 2:
            stage((2 * pairs - 2, 2 * t), (nt - 1, t))
            stage((nt - 1, t), None)
        else:
            stage((2 * pairs - 2, 2 * t), None)

    blk = pl.BlockSpec((lp, LANE), lambda g: (0, g))
    return pl.pallas_call(
        body, name="attn_fwd", grid=(HEADS // 2,),
        in_specs=[blk, blk, pl.BlockSpec((nt, LANE, t), lambda g: (0, g, 0)),
                  pl.BlockSpec((lp, LANE), lambda g: (0, 0), pipeline_mode=pl.Buffered(1))],
        out_specs=(blk, blk, pl.BlockSpec((lp, 2 * LANE), lambda g: (0, g))),
        out_shape=(jax.ShapeDtypeStruct((lp, D_ATTN), F32), jax.ShapeDtypeStruct((lp, D_ATTN), F32),
                   jax.ShapeDtypeStruct((lp, HEADS * LANE), F32)),
        scratch_shapes=[pltpu.VMEM((nt, 2 * t, 2 * t), F32), pltpu.VMEM((2 * t, t), F32),
                        pltpu.VMEM((2, SUBLANE, 2 * t), F32), pltpu.VMEM((2, SUBLANE, 2 * t), F32),
                        pltpu.VMEM((ext, 2 * t), F32), pltpu.VMEM((2, lp, LANE), F32)],
        compiler_params=_params(("parallel",)),
    )(q, k, v_t, cc)


def _attn_bwd(q, k, v, do, q_t, k_t, do_t, m, neg_delta, ct4):
    lp = q.shape[0]
    t = ROW_TILE
    nt = lp // t

    def body(q_ref, k_ref, v_ref, do_ref, qt_ref, kt_ref, dot_ref, ma_ref, mb_ref, nd_ref, cta_ref, ctb_ref,
             dq_ref, dk_ref, dv_ref, dc_ref, dq_acc, dk_acc, dv_acc):
        masks = _head_masks()
        ct_refs, m_refs = (cta_ref, ctb_ref), (ma_ref, mb_ref)
        row_head = 2 * pl.program_id(0) + (lax.broadcasted_iota(jnp.int32, (2 * t, LANE), 0) >= t).astype(jnp.int32)
        col = lax.broadcasted_iota(jnp.int32, (2 * t, LANE), 1)
        delta_ones = jnp.where((col < HEADS * DELTA_TERMS) & (col % HEADS == row_head), 1.0, 0.0).astype(MXU_DTYPE)
        below = lax.broadcasted_iota(jnp.int32, (t, t), 1) <= lax.broadcasted_iota(jnp.int32, (t, t), 0)
        top = lax.broadcasted_iota(jnp.int32, (LANE, 1), 0) < HEAD_DIM
        dq_acc[...] = jnp.zeros_like(dq_acc)

        on_first_diagonal = jnp.concatenate([below, jnp.ones((t, t), jnp.bool_)], axis=0)

        def k_block(j, _, with_next=True):
            c0 = pl.multiple_of(j * t, t)
            kj = k_ref[pl.ds(c0, t), :]
            vj = v_ref[pl.ds(c0, t), :]
            k2 = jnp.concatenate([jnp.where(hm, kj, 0).astype(MXU_DTYPE) for hm in masks], axis=0)
            v2 = jnp.concatenate([jnp.where(hm, vj, 0).astype(MXU_DTYPE) for hm in masks], axis=0)
            v2 = jnp.concatenate([v2, delta_ones], axis=1)
            ck = [r[j] for r in ct_refs]
            ktj = kt_ref[j]
            k2t = jnp.concatenate([jnp.where(top, ktj, 0).astype(MXU_DTYPE), jnp.where(top, 0, ktj).astype(MXU_DTYPE)],
                                  axis=1)
            dk_acc[...] = jnp.zeros_like(dk_acc)
            dv_acc[...] = jnp.zeros_like(dv_acc)

            def q_block(i, colsums, mask=None, rows=t):
                r0 = pl.multiple_of(i * t, t)
                qi = q_ref[pl.ds(r0, rows), :]
                doi = jnp.concatenate([do_ref[pl.ds(r0, rows), :], nd_ref[pl.ds(r0, rows), :]], axis=1)
                qti = jnp.concatenate([qt_ref[i + b] for b in range(rows // t)], axis=1)
                doti = jnp.concatenate([dot_ref[i + b] for b in range(rows // t)], axis=1)
                s2 = _dot_nt(qi, k2)
                dp2 = _dot_nt(doi, v2)
                out, ps, dss = [], [], []
                for hh in range(2):
                    s = (s2[:, hh * t:(hh + 1) * t] - ck[hh]) * LOG2E
                    if mask is not None:
                        s = jnp.where(mask, s, NEG)
                    p = jnp.exp2(_sub_rows(s, m_refs[hh][pl.ds(r0, rows), :])).astype(MXU_DTYPE)
                    ds32 = p.astype(F32) * dp2[:, hh * t:(hh + 1) * t]
                    ps.append(p)
                    dss.append(ds32.astype(MXU_DTYPE))
                    out.append(colsums[hh] + jnp.sum(ds32, axis=0, keepdims=True))
                ds_cat = jnp.concatenate(dss, axis=1)
                dv_acc[...] = dv_acc[...] + _dot(doti, jnp.concatenate(ps, axis=1))
                dk_acc[...] = dk_acc[...] + _dot(qti, ds_cat)
                dq_t = _dot(k2t, ds_cat.T)
                for b in range(rows // t):
                    dq_acc[i + b] = dq_acc[i + b] + dq_t[:, b * t:(b + 1) * t]
                return tuple(out)

            nq = ATTN_BWD_QBLOCKS
            colsums = (jnp.zeros((1, t), F32), jnp.zeros((1, t), F32))
            if with_next:
                colsums = q_block(j, colsums, on_first_diagonal, nq * t)
            else:
                colsums = q_block(j, colsums, below)
            first = j + (nq if with_next else 1)
            groups = (nt - first) // nq
            colsums = lax.fori_loop(0, groups, lambda p, c: q_block(first + nq * p, c, None, nq * t), colsums)
            colsums = lax.fori_loop(first + nq * groups, nt, q_block, colsums)
            for hh in range(2):
                dc_ref[hh, j] = -colsums[hh]
            own = lambda acc: jnp.concatenate([acc[:HEAD_DIM, :t], acc[HEAD_DIM:, t:]], axis=0).T
            dk_ref[pl.ds(c0, t), :] = own(dk_acc[...]).astype(dk_ref.dtype)
            dv_ref[pl.ds(c0, t), :] = own(dv_acc[...]).astype(dv_ref.dtype)
            return 0

        lax.fori_loop(0, nt - 1, k_block, 0)
        k_block(nt - 1, 0, with_next=False)
        for i in range(nt):
            dq_ref[i * t:(i + 1) * t, :] = (dq_acc[i].T * (HEAD_DIM ** -0.5)).astype(dq_ref.dtype)

    blk, ct_a, ct_b = _pair_specs(lp, nt, t)
    rep_a = pl.BlockSpec((lp, LANE), lambda g: (0, 2 * g))
    rep_b = pl.BlockSpec((lp, LANE), lambda g: (0, 2 * g + 1))
    tr_blk = pl.BlockSpec((nt, LANE, t), lambda g: (0, g, 0))
    return pl.pallas_call(
        body, name="attn_bwd", grid=(HEADS // 2,),
        in_specs=[blk] * 4 + [tr_blk, tr_blk, tr_blk, rep_a, rep_b, pl.BlockSpec((lp, LANE), lambda g: (0, 0)), ct_a, ct_b],
        out_specs=(blk, blk, blk, pl.BlockSpec((2, nt, 1, t), lambda g: (g, 0, 0, 0))),
        out_shape=(jax.ShapeDtypeStruct((lp, D_ATTN), MXU_DTYPE),) * 3
                  + (jax.ShapeDtypeStruct((HEADS, nt, 1, t), F32),),
        scratch_shapes=[pltpu.VMEM((nt, LANE, t), F32), pltpu.VMEM((LANE, 2 * t), F32), pltpu.VMEM((LANE, 2 * t), F32)],
        compiler_params=_params(("parallel",)),
    )(q, k, v, do, q_t, k_t, do_t, m, m, neg_delta, ct4, ct4)


def _shift_down(prev8, cur, k):
    ext = jnp.concatenate([prev8, cur], axis=0)
    return pltpu.roll(ext, k, 0)[SUBLANE:, :]


def _shift_up(cur, next8, k):
    ext = jnp.concatenate([cur, next8], axis=0)
    n = ext.shape[0]
    return pltpu.roll(ext, n - k, 0)[:cur.shape[0], :]


def _post(o, l_sum, rest, x2, meta_blk, tgt2, w_out, attn_g, conv_g, final_g, conv_w8):
    lp = o.shape[0]
    t = ROW_TILE
    nt = lp // t
    n_sub = t // LANE
    hb = t // SUBLANE

    def body(*refs):
        o_ref, l_ref, za_ref, gb_ref, gc_ref, xc_ref, zc_ref, gch_ref, xch_ref = refs[:9]
        x_refs = refs[9:9 + n_sub]
        mb = refs[9 + n_sub]
        t_refs = refs[10 + n_sub:10 + 2 * n_sub]
        wo_ref, ag_ref, cg_ref, fg_ref, cw_ref, gm_ref, hr_ref = refs[10 + 2 * n_sub:17 + 2 * n_sub]
        (dout_ref, do_ref, dot_ref, dl_ref, dza_ref, dgb_ref, dzc_ref, dcv_ref,
         loss_ref, gf_ref, gag_ref, gcg_ref, gwo_ref) = refs[17 + 2 * n_sub:]
        i = pl.program_id(0)

        @pl.when(i == 0)
        def _():
            for r in (loss_ref, gf_ref, gag_ref, gcg_ref, gwo_ref):
                r[...] = jnp.zeros_like(r)

        gmat = gm_ref[...]
        inv_g = 1.0 / HEAD_DIM
        o_v = o_ref[...]
        ra = lax.rsqrt(_group_sum(o_v * o_v, gmat, STAT_TERMS) * inv_g + EPS)
        n_a = o_v * ra
        a_n = n_a * ag_ref[...]
        za = za_ref[...]
        sig_a = _sigmoid(za)
        sz_a = za * sig_a
        y_a = a_n * sz_a
        gb = gb_ref[...]
        gc = gc_ref[...]
        xc = xc_ref[...]
        cx = gc * xc
        cx_prev = jnp.where(i == 0, 0.0, gch_ref[...] * xch_ref[...])
        conv = (cw_ref[0:1, :] * _shift_down(cx_prev, cx, 2) + cw_ref[1:2, :] * _shift_down(cx_prev, cx, 1)
                + cw_ref[2:3, :] * cx)
        e = gb * conv
        re = lax.rsqrt(_group_sum(e * e, gmat, STAT_TERMS) * inv_g + EPS)
        n_e = e * re
        e_n = n_e * cg_ref[...]
        zc = zc_ref[...]
        sig_c = _sigmoid(zc)
        sz_c = zc * sig_c
        y_c = e_n * sz_c
        mix = jnp.concatenate([y_a, y_c], axis=-1)
        mix_b = mix.astype(MXU_DTYPE)
        first = jnp.where(i == 0, mb[...], x_refs[0][...])
        h = jnp.concatenate([first] + [r[...] for r in x_refs[1:]], axis=0)
        out = h + _dot(mix_b, wo_ref[...])
        r2 = lax.rsqrt(jnp.mean(out * out, axis=-1, keepdims=True) + EPS)
        n_f = out * r2
        y = n_f * fg_ref[...]
        tgt = jnp.concatenate([r[...] for r in t_refs], axis=0)
        valid = (i * t + lax.broadcasted_iota(jnp.int32, (t, 1), 0)) >= FRONT
        diff = jnp.where(valid, y - tgt, 0.0)
        loss_ref[...] = loss_ref[...] + 0.5 * jnp.sum(jnp.sum(diff * diff, axis=-1, keepdims=True) * (1.0 / D_MODEL))
        dy = diff * (1.0 / D_MODEL)
        gf_ref[...] = gf_ref[...] + jnp.sum(dy * n_f, axis=0, keepdims=True)
        dn = dy * fg_ref[...]
        d_out = r2 * (dn - n_f * jnp.mean(dn * n_f, axis=-1, keepdims=True))
        dout_ref[...] = d_out
        d_out_b = d_out.astype(MXU_DTYPE)
        d_mix = _dot_nt(d_out_b, wo_ref[...])
        gwo_ref[...] = gwo_ref[...] + _dot(mix.T.astype(MXU_DTYPE), d_out_b)
        d_ya = d_mix[:, :D_ATTN]
        d_yc = d_mix[:, D_ATTN:]
        d_an = d_ya * sz_a
        dza_ref[...] = (d_ya * a_n * (sig_a * (1.0 + za * (1.0 - sig_a)))).astype(dza_ref.dtype)
        gag_ref[...] = gag_ref[...] + jnp.sum(d_an * n_a, axis=0, keepdims=True)
        dn_a = d_an * ag_ref[...]
        d_o = ra * (dn_a - n_a * (_group_sum(dn_a * n_a, gmat, STAT_TERMS) * inv_g))
        d_o_l = d_o / l_ref[...]
        d_o_b = d_o_l.astype(do_ref.dtype)
        do_ref[...] = d_o_b
        dot_ref[...] = d_o_l.T.astype(dot_ref.dtype)
        delta = _group_sum(d_o_b.astype(F32) * o_v, hr_ref[...])
        terms, rest_of = [], delta
        for k in range(DELTA_TERMS):
            terms.append(rest_of.astype(MXU_DTYPE).astype(F32))
            rest_of = rest_of - terms[-1]
        dl_ref[...] = -sum(pltpu.roll(term, HEADS * k, 1) if k else term
                           for k, term in enumerate(terms)).astype(dl_ref.dtype)
        d_en = d_yc * sz_c
        dzc_ref[...] = (d_yc * e_n * (sig_c * (1.0 + zc * (1.0 - sig_c)))).astype(dzc_ref.dtype)
        gcg_ref[...] = gcg_ref[...] + jnp.sum(d_en * n_e, axis=0, keepdims=True)
        dn_e = d_en * cg_ref[...]
        d_e = re * (dn_e - n_e * (_group_sum(dn_e * n_e, gmat, STAT_TERMS) * inv_g))
        dgb_ref[...] = (d_e * conv).astype(dgb_ref.dtype)
        dcv_ref[...] = d_e * gb

    head_rep = jnp.where((lax.broadcasted_iota(jnp.int32, (D_ATTN, LANE), 0) >> 6)
                         == lax.broadcasted_iota(jnp.int32, (D_ATTN, LANE), 1), 1.0, 0.0).astype(MXU_DTYPE)
    row_blk = lambda cols: pl.BlockSpec((t, cols), lambda i: (i, 0))
    rest_blk = lambda s: pl.BlockSpec((t, 512), functools.partial(lambda i, s: (i, s), s=s))
    halo = lambda s: pl.BlockSpec((SUBLANE, 512), functools.partial(lambda i, s: (jnp.maximum(i * hb - 1, 0), s), s=s))
    const = lambda shape: pl.BlockSpec(shape, lambda i: (0, 0))
    acc = lambda shape: pl.BlockSpec(shape, lambda i: (0, 0))
    return pl.pallas_call(
        body, name="post_fwd_bwd", grid=(nt,),
        in_specs=[row_blk(D_ATTN), row_blk(D_ATTN)] + [rest_blk(s) for s in range(5)] + [halo(2), halo(3)]
                 + _x_block_specs(n_sub, LANE) + [const((LANE, D_MODEL))] + _x_block_specs(n_sub, LANE)
                 + [const((D_MODEL, D_MODEL)), const((1, D_ATTN)), const((1, D_CONV)), const((1, D_MODEL)),
                    const((SUBLANE, D_CONV)), const((D_ATTN, D_ATTN)), const((D_ATTN, LANE))],
        out_specs=(row_blk(D_MODEL), row_blk(D_ATTN), pl.BlockSpec((None, D_ATTN, t), lambda i: (i, 0, 0)),
                   row_blk(LANE), row_blk(D_ATTN), row_blk(D_CONV),
                   row_blk(D_CONV), row_blk(D_CONV),
                   acc((1, LANE)), acc((1, D_MODEL)), acc((1, D_ATTN)), acc((1, D_CONV)), acc((D_MODEL, D_MODEL))),
        out_shape=(jax.ShapeDtypeStruct((lp, D_MODEL), F32), jax.ShapeDtypeStruct((lp, D_ATTN), MXU_DTYPE),
                   jax.ShapeDtypeStruct((nt, D_ATTN, t), MXU_DTYPE), jax.ShapeDtypeStruct((lp, LANE), MXU_DTYPE),
                   jax.ShapeDtypeStruct((lp, D_ATTN), MXU_DTYPE),
                   jax.ShapeDtypeStruct((lp, D_CONV), MXU_DTYPE), jax.ShapeDtypeStruct((lp, D_CONV), MXU_DTYPE),
                   jax.ShapeDtypeStruct((lp, D_CONV), F32),
                   jax.ShapeDtypeStruct((1, LANE), F32), jax.ShapeDtypeStruct((1, D_MODEL), F32),
                   jax.ShapeDtypeStruct((1, D_ATTN), F32), jax.ShapeDtypeStruct((1, D_CONV), F32),
                   jax.ShapeDtypeStruct((D_MODEL, D_MODEL), F32)),
        compiler_params=_params(("arbitrary",)),
    )(o, l_sum, *([rest] * 5), rest, rest, *([x2] * n_sub), meta_blk, *([tgt2] * n_sub),
      w_out, attn_g, conv_g, final_g, conv_w8, _group_matrix(), head_rep)


def _bwd_in(x2, meta_blk, norm_g, w_pad, bf_pad, fl, dc, dq, dk, dv, dza, dgb, dzc, dconv, rest, d_out, conv_w8):
    lp = fl.shape[0]
    t = ROW_TILE
    nt = lp // t
    n_sub = t // LANE
    hb = t // SUBLANE
    rev = lambda i: nt - 1 - i

    def body(*refs):
        x_refs = refs[:n_sub]
        (mb, g_ref, w_ref, bf_ref, fl_ref, dc_ref, dq_ref, dk_ref, dv_ref, dza_ref, dgb_ref, dzc_ref,
         dcv_ref, dcvn_ref, gc_ref, xc_ref, gch_ref, xch_ref, dout_ref, cw_ref, tri_ref) = refs[n_sub:n_sub + 21]
        dp_ref, gx_ref, front_ref, gn_ref, gbf_ref, gcw_ref, carry, dh_scr, gx_sems = refs[n_sub + 21:]
        step = pl.program_id(0)
        i = rev(step)

        @pl.when(step == 0)
        def _():
            for r in (gn_ref, gbf_ref, gcw_ref, carry):
                r[...] = jnp.zeros_like(r)

        dc8 = jnp.concatenate([dc_ref[...], jnp.zeros((LANE - HEADS, t), F32)], axis=0).T
        dlogf = _dot_exact(tri_ref[...], dc8) + carry[...]
        carry[...] = carry[...] + jnp.sum(dc8, axis=0, keepdims=True)
        z = fl_ref[...] + bf_ref[...]
        row = i * t + lax.broadcasted_iota(jnp.int32, (t, LANE), 0)
        d_f = jnp.where(row >= PAD_ROWS, dlogf * (1.0 / (1.0 + jnp.exp(z))), 0.0)
        gbf_ref[...] = gbf_ref[...] + jnp.sum(d_f, axis=0, keepdims=True)
        dcv = dcv_ref[...]
        dcv_next = jnp.where(i == nt - 1, 0.0, dcvn_ref[...])
        d_cx = (cw_ref[2:3, :] * dcv + cw_ref[1:2, :] * _shift_up(dcv, dcv_next, 1)
                + cw_ref[0:1, :] * _shift_up(dcv, dcv_next, 2))
        gc = gc_ref[...]
        xc = xc_ref[...]
        cx = gc * xc
        cx_prev = jnp.where(i == 0, 0.0, gch_ref[...] * xch_ref[...])
        rowi = lax.broadcasted_iota(jnp.int32, (SUBLANE, 1), 0)
        gcw = (jnp.where(rowi == 0, jnp.sum(dcv * _shift_down(cx_prev, cx, 2), axis=0, keepdims=True), 0.0)
               + jnp.where(rowi == 1, jnp.sum(dcv * _shift_down(cx_prev, cx, 1), axis=0, keepdims=True), 0.0)
               + jnp.where(rowi == 2, jnp.sum(dcv * cx, axis=0, keepdims=True), 0.0))
        gcw_ref[...] = gcw_ref[...] + gcw
        dp_ref[:, SEG_Q:SEG_Q + 512] = dq_ref[...]
        dp_ref[:, SEG_K:SEG_K + 512] = dk_ref[...]
        dp_ref[:, SEG_V:SEG_V + 512] = dv_ref[...]
        dp_ref[:, SEG_F:SEG_F + LANE] = d_f.astype(dp_ref.dtype)
        dp_ref[:, SEG_ZA:SEG_ZA + 512] = dza_ref[...]
        dp_ref[:, SEG_GB:SEG_GB + 512] = dgb_ref[...]
        dp_ref[:, SEG_GC:SEG_GC + 512] = (d_cx * xc).astype(dp_ref.dtype)
        dp_ref[:, SEG_XC:SEG_XC + 512] = (d_cx * gc).astype(dp_ref.dtype)
        dp_ref[:, SEG_ZC:SEG_ZC + 512] = dzc_ref[...]
        d_u = _dot(dp_ref[...], w_ref[...])
        first = jnp.where(i == 0, mb[...], x_refs[0][...])
        h = jnp.concatenate([first] + [r[...] for r in x_refs[1:]], axis=0)
        r1 = lax.rsqrt(jnp.mean(h * h, axis=-1, keepdims=True) + EPS)
        n_h = h * r1
        gn_ref[...] = gn_ref[...] + jnp.sum(d_u * n_h, axis=0, keepdims=True)
        dn = d_u * g_ref[...]
        d_h = dout_ref[...] + r1 * (dn - n_h * jnp.mean(dn * n_h, axis=-1, keepdims=True))
        slot = step % 2

        def to_grad_x(slot_, tile):
            return pltpu.make_async_copy(dh_scr.at[slot_], gx_ref.at[pl.ds(pl.multiple_of(tile * t - FRONT, SUBLANE), t)],
                                         gx_sems.at[slot_])

        @pl.when(step >= 2)
        def _():
            to_grad_x(slot, 1).wait()

        dh_scr[slot] = d_h

        @pl.when(i > 0)
        def _():
            to_grad_x(slot, i).start()

        @pl.when(i == 0)
        def _():
            front_ref[...] = d_h[:FRONT]
            rest_rows = pltpu.make_async_copy(dh_scr.at[slot, pl.ds(FRONT, t - FRONT)], gx_ref.at[pl.ds(0, t - FRONT)],
                                              gx_sems.at[slot])
            rest_rows.start()
            rest_rows.wait()
            if nt >= 2:
                to_grad_x(1 - slot, 1).wait()

    def x_specs():
        specs = [pl.BlockSpec((LANE, D_MODEL), lambda s: (jnp.maximum(n_sub * rev(s) - 1, 0), 0))]
        for b in range(1, n_sub):
            specs.append(pl.BlockSpec((LANE, D_MODEL), functools.partial(lambda s, b: (n_sub * rev(s) - 1 + b, 0), b=b)))
        return specs

    row_blk = lambda cols: pl.BlockSpec((t, cols), lambda s: (rev(s), 0))
    rest_blk = lambda k: pl.BlockSpec((t, 512), functools.partial(lambda s, k: (rev(s), k), k=k))
    halo_prev = lambda k: pl.BlockSpec(
        (SUBLANE, 512), functools.partial(lambda s, k: (jnp.maximum(rev(s) * hb - 1, 0), k), k=k))
    halo_next = pl.BlockSpec((SUBLANE, 512), lambda s: (jnp.minimum((rev(s) + 1) * hb, lp // SUBLANE - 1), 0))
    const = lambda shape: pl.BlockSpec(shape, lambda s: (0, 0))
    return pl.pallas_call(
        body, name="bwd_in", grid=(nt,),
        in_specs=x_specs() + [const((LANE, D_MODEL)), const((1, D_MODEL)),
                              pl.BlockSpec((D_IN_PAD, D_MODEL), lambda s: (0, 0), pipeline_mode=pl.Buffered(1)),
                              const((1, LANE)), row_blk(LANE),
                              pl.BlockSpec((HEADS, t), lambda s: (0, rev(s))),
                              row_blk(512), row_blk(512), row_blk(512), row_blk(512), row_blk(512), row_blk(512),
                              row_blk(512), halo_next, rest_blk(2), rest_blk(3), halo_prev(2), halo_prev(3),
                              row_blk(D_MODEL), const((SUBLANE, D_CONV)), const((t, t))],
        out_specs=(row_blk(D_IN_PAD), ANY, const((FRONT, D_MODEL)), const((1, D_MODEL)), const((1, LANE)),
                   const((SUBLANE, D_CONV))),
        out_shape=(jax.ShapeDtypeStruct((lp, D_IN_PAD), MXU_DTYPE), jax.ShapeDtypeStruct((lp - FRONT, D_MODEL), F32),
                   jax.ShapeDtypeStruct((FRONT, D_MODEL), F32),
                   jax.ShapeDtypeStruct((1, D_MODEL), F32), jax.ShapeDtypeStruct((1, LANE), F32),
                   jax.ShapeDtypeStruct((SUBLANE, D_CONV), F32)),
        scratch_shapes=[pltpu.VMEM((1, LANE), F32), pltpu.VMEM((2, t, D_MODEL), F32), pltpu.SemaphoreType.DMA((2,))],
        compiler_params=_params(("arbitrary",)),
    )(*([x2] * n_sub), meta_blk, norm_g, w_pad, bf_pad, fl, dc, dq, dk, dv, dza, dgb, dzc, dconv, dconv,
      rest, rest, rest, rest, d_out, conv_w8, _triangle(t, lower=False))


def _grad_w_in(u, dproj):
    lp = u.shape[0]
    tn = GW_COL_TILE
    tk = tn if lp % tn == 0 else ROW_TILE

    def body(d_ref, u_ref, o_ref, wire_ref):
        k = pl.program_id(1)

        @pl.when(k == 0)
        def _():
            o_ref[...] = jnp.zeros_like(o_ref)

        o_ref[...] = o_ref[...] + lax.dot_general(d_ref[...], u_ref[...], (((0,), (0,)), ((), ())),
                                                  preferred_element_type=F32)

        @pl.when(k == pl.num_programs(1) - 1)
        def _():
            wire_ref[...] = o_ref[...].astype(wire_ref.dtype)

    out_spec = pl.BlockSpec((tn, D_MODEL), lambda n, k: (n, 0))
    return pl.pallas_call(
        body, name="grad_w_in", grid=(D_IN_PAD // tn, lp // tk),
        in_specs=[pl.BlockSpec((tk, tn), lambda n, k: (k, n)), pl.BlockSpec((tk, D_MODEL), lambda n, k: (k, 0))],
        out_specs=(out_spec, out_spec),
        out_shape=(jax.ShapeDtypeStruct((D_IN_PAD, D_MODEL), F32), jax.ShapeDtypeStruct((D_IN_PAD, D_MODEL), WIRE_DTYPE)),
        compiler_params=_params(("parallel", "arbitrary")),
    )(dproj, u)


def _by_chip(own, others, me):
    by_mask = jnp.stack([own, others[1], others[0], others[2]])
    return [lax.dynamic_index_in_dim(by_mask, jnp.bitwise_xor(me, s), 0, keepdims=False) for s in range(N_CHIPS)]


def _both_halves(mine, other, c):
    return jnp.where(c == 0, jnp.concatenate([mine, other], axis=0), jnp.concatenate([other, mine], axis=0))


def _local_step(x2, tgt2, meta_full, norm_g, w_pad, b_f, conv_w_full, attn_g, conv_g, w_out_full, final_g):
    lp = x2.shape[0] + FRONT
    nt = lp // ROW_TILE
    meta_blk = jnp.concatenate([jnp.zeros((PAD_ROWS, D_MODEL), F32), meta_full], axis=0)
    bf_pad = jnp.pad(b_f, ((0, 0), (0, LANE - HEADS)))
    conv_w8 = jnp.pad(conv_w_full, ((0, SUBLANE - conv_w_full.shape[0]), (0, 0)))
    q, k, v, rest, fl, ct, u, q_t, k_t, v_t, cc = _in_proj(x2, meta_blk, norm_g, w_pad, bf_pad)
    ct4 = ct.reshape(SUBLANE, nt, 1, ROW_TILE)
    o, l_sum, m_max = _attn_fwd(q, k, v_t, cc)
    (d_out, d_o, do_t, neg_delta, dza, dgb, dzc, dconv, loss, g_final, g_attn, g_convg, gw_out) = _post(
        o, l_sum, rest, x2, meta_blk, tgt2, w_out_full, attn_g, conv_g, final_g, conv_w8)
    dq, dk, dv, dc = _attn_bwd(q, k, v, d_o, q_t, k_t, do_t, m_max, neg_delta, ct4)
    dproj, grad_x, d_front, g_norm, g_bf, g_cw = _bwd_in(x2, meta_blk, norm_g, w_pad, bf_pad, fl, dc.reshape(HEADS, lp), dq, dk, dv,
                                             dza, dgb, dzc, dconv, rest, d_out, conv_w8)
    gw_in, gw_in_wire = _grad_w_in(u, dproj)
    return dict(loss=loss, grad_x=grad_x, d_front=d_front, g_norm=g_norm, g_final=g_final, g_attn=g_attn, g_convg=g_convg, g_bf=g_bf,
                g_cw=g_cw, gw_out=gw_out, gw_in=gw_in, gw_in_wire=gw_in_wire)


def kernel(x, meta, norm_g, w_in, b_f, conv_w, attn_norm_g, conv_norm_g, w_out, final_norm_g, loss_target, m_meta, m_norm_g, m_w_in, m_b_f, m_conv_w, m_attn_norm_g, m_conv_norm_g, m_w_out, m_final_norm_g, v_meta, v_norm_g, v_w_in, v_b_f, v_conv_w, v_attn_norm_g, v_conv_norm_g, v_w_out, v_final_norm_g):
    cx_, cy_, cc_ = _position()
    chip = 2 * cx_ + cy_
    shard = w_in.shape[2]
    out_half = w_out.shape[1] // 2
    pick = lambda vals: jnp.where(chip == 0, vals[0], jnp.where(chip == 1, vals[1], jnp.where(chip == 2, vals[2], vals[3])))
    a_off, b_off = pick(A_OFF), pick(B_OFF)
    wt = jnp.transpose(w_in[0]).astype(MXU_DTYPE)

    def placed(piece, off):
        return lax.dynamic_slice_in_dim(jnp.pad(piece, ((WIN_ROWS, WIN_ROWS), (0, 0))), WIN_ROWS - off, WIN_ROWS, 0)

    wi = placed(wt[:PIECE_A], a_off) + placed(wt[PIECE_A:], b_off)
    wo = w_out[0].astype(MXU_DTYPE)
    small = jnp.concatenate([meta, jnp.pad(conv_w[0], ((0, 8 - conv_w.shape[1]), (0, meta.shape[1] - conv_w.shape[2])))],
                            axis=0)
    gwi, gwo, gsm = _gather_weights(wi.reshape(2, WIN_HALF, D_MODEL), wo.reshape(2, out_half, D_MODEL), small)
    starts = jnp.stack([_window_start(jnp.bitwise_xor(chip, mask)) for mask in (0, 2, 1, 3)]).astype(jnp.int32)
    w_pad = _assemble_w(wi, gwi.reshape(3, WIN_ROWS, D_MODEL), starts)
    w_out_full = jnp.concatenate(_by_chip(wo, gwo.reshape(3, 2 * out_half, D_MODEL), chip), axis=0)
    small_full = jnp.concatenate(_by_chip(small, gsm, chip), axis=1)
    meta_full = small_full[:N_META]
    conv_w_full = jnp.concatenate([small_full[N_META:N_META + 3, 256 * s:256 * s + LANE] for s in range(N_CHIPS)], axis=1)
    final_g2 = final_norm_g.reshape(1, D_MODEL)
    r = _local_step(x[0], loss_target[0], meta_full, norm_g, w_pad, b_f, conv_w_full, attn_norm_g, conv_norm_g,
                    w_out_full, final_g2)
    grad_x = r["grad_x"][None]
    gb = r["gw_out"].reshape(N_CHIPS, 2, out_half, D_MODEL)
    wide = lambda a: jnp.pad(a, ((0, 0), (0, D_MODEL - a.shape[1])))
    pack = jnp.concatenate([
        r["g_norm"], r["g_final"], jnp.concatenate([r["g_attn"], r["g_convg"]], axis=1), wide(r["g_bf"]),
        wide(r["loss"]), jnp.zeros((3, D_MODEL), F32), r["d_front"][PAD_ROWS:], wide(r["g_cw"])], axis=0)
    ra, rb, packs = _pair_exchange(r["gw_in_wire"], gb, pack)
    ha, hb = _chip_exchange(r["gw_in"], ra, gb, rb)
    oa, ob = _pair_share(ha, hb)
    g_window = _both_halves(ha, oa, cc_)
    g_w_in_t = jnp.concatenate([lax.dynamic_slice_in_dim(g_window, a_off, PIECE_A, 0),
                                lax.dynamic_slice_in_dim(g_window, b_off, shard - PIECE_A, 0)], axis=0)
    g_w_out = _both_halves(hb, ob, cc_)
    as_rows = lambda a: jnp.transpose(a, (2, 0, 1))
    g_w_in, d_w_in, nm_w_in, nv_w_in = (jnp.transpose(a, (1, 2, 0)) for a in _adamw_rows(
        as_rows(w_in), g_w_in_t, as_rows(m_w_in), as_rows(v_w_in)))
    d_w_out, nm_w_out, nv_w_out = (a[None] for a in _adamw_big(w_out[0], g_w_out, m_w_out[0], v_w_out[0], LANE))
    params = (norm_g, final_g2, attn_norm_g, conv_norm_g, b_f, meta, conv_w[0])
    ms = (m_norm_g, m_final_norm_g.reshape(1, D_MODEL), m_attn_norm_g, m_conv_norm_g, m_b_f, m_meta, m_conv_w[0])
    vs = (v_norm_g, v_final_norm_g.reshape(1, D_MODEL), v_attn_norm_g, v_conv_norm_g, v_b_f, v_meta, v_conv_w[0])
    loss, g_s, d_s, m_s, v_s = _small_update(pack, packs, params, ms, vs)

    def ordered(small_list, big_in, big_out):
        s_norm, s_final, s_attn, s_convg, s_bf, s_meta, s_cw = small_list
        return (s_meta, s_norm, big_in, s_bf, s_cw[None], s_attn, s_convg, big_out, s_final.reshape(D_MODEL))

    return (loss.reshape(()), grad_x,
            *ordered(g_s, g_w_in, g_w_out[None]), *ordered(d_s, d_w_in, d_w_out),
            *ordered(m_s, nm_w_in, nm_w_out), *ordered(v_s, nv_w_in, nv_w_out))
```

```python
import functools

import jax
import jax.numpy as jnp
from jax import lax
from jax.experimental import pallas as pl
from jax.experimental.pallas import tpu as pltpu

F32 = jnp.float32
MXU_DTYPE = jnp.bfloat16
WIRE_DTYPE = jnp.bfloat16

D_MODEL = 1024
N_META = 16
HEADS = 8
HEAD_DIM = 64
D_ATTN = HEADS * HEAD_DIM
D_CONV = 512
EPS = 1e-6
LANE = 128
SUBLANE = 8
ROW_TILE = 384
ATTN_UNROLL = 3
ATTN_BWD_QBLOCKS = 2
DELTA_TERMS = 3
ADAM_ROW_STEPS = 3
STAT_TERMS = 1
FRONT = LANE
PAD_ROWS = FRONT - N_META
NEG = -1e30
LOG2E = 1.4426950408889634
N_CHIPS = 4
N_DEV = 8
VMEM_LIMIT_BYTES = 60 * 1024 * 1024

SEG_Q, SEG_K, SEG_V, SEG_F, SEG_ZA, SEG_GB, SEG_GC, SEG_XC, SEG_ZC = (
    0, 512, 1024, 1536, 1664, 2176, 2688, 3200, 3712)
D_IN = 4104
D_IN_PAD = 4224
F_END = 1544
GW_COL_TILE = 1408
WIN_ROWS = 1152
WIN_HALF = WIN_ROWS // 2
WIN_START = (0, 1024, 2160, 3072)
PIECE_A = 518
A_OFF = (0, 2, 12, 126)
B_OFF = (518, 640, 530, 644)
ADAM_LR = 0.001
ADAM_B1 = 0.9
ADAM_B2 = 0.999
ADAM_EPS = 1e-08
ADAM_WD = 0.01
ADAM_STEP = 10

MESH = pl.DeviceIdType.MESH
ANY = pl.BlockSpec(memory_space=pl.ANY)

PACK_ROWS = 32
SLOT_NORM = (0, 1, 0, 1024)
SLOT_FINAL = (1, 2, 0, 1024)
SLOT_ATTN = (2, 3, 0, 512)
SLOT_CONVG = (2, 3, 512, 1024)
SLOT_BF = (3, 4, 0, 8)
SLOT_META = (8, 24, 0, 256)
SLOT_CONVW = (24, 27, 0, 128)
LOSS_ROW = 4


def _params(sem=None):
    return pltpu.CompilerParams(dimension_semantics=sem, vmem_limit_bytes=VMEM_LIMIT_BYTES)


def _sigmoid(z):
    return 1.0 / (1.0 + jnp.exp(-z))


def _dot(a, b):
    return jnp.dot(a, b, preferred_element_type=F32)


def _dot_nt(a, b):
    return lax.dot_general(a, b, (((1,), (1,)), ((), ())), preferred_element_type=F32)


def _dot_exact(ones, x):
    ones = ones.astype(MXU_DTYPE)
    total = None
    for _ in range(3):
        term = x.astype(MXU_DTYPE)
        x = x - term.astype(F32)
        total = _dot(ones, term) if total is None else total + _dot(ones, term)
    return total


def _group_matrix():
    r = lax.broadcasted_iota(jnp.int32, (D_ATTN, D_ATTN), 0) >> 6
    c = lax.broadcasted_iota(jnp.int32, (D_ATTN, D_ATTN), 1) >> 6
    return jnp.where(r == c, 1.0, 0.0).astype(MXU_DTYPE)


def _triangle(n, lower):
    r = lax.broadcasted_iota(jnp.int32, (n, n), 0)
    c = lax.broadcasted_iota(jnp.int32, (n, n), 1)
    return jnp.where((r >= c) if lower else (c >= r), 1.0, 0.0).astype(MXU_DTYPE)


def _group_sum(x, gmat, terms=2):
    hi = x.astype(MXU_DTYPE)
    if terms == 1:
        return _dot(hi, gmat)
    lo = (x - hi.astype(F32)).astype(MXU_DTYPE)
    return _dot(hi, gmat) + _dot(lo, gmat)


def _x_block_specs(n_sub, rows):
    specs = [pl.BlockSpec((rows, D_MODEL), lambda i: (jnp.maximum(n_sub * i - 1, 0), 0))]
    for b in range(1, n_sub):
        specs.append(pl.BlockSpec((rows, D_MODEL), functools.partial(lambda i, b: (n_sub * i - 1 + b, 0), b=b)))
    return specs


def _position():
    return lax.axis_index("x"), lax.axis_index("y"), lax.axis_index("c")


def _gather_weights(wi, wo, small):
    def body(wi_ref, wo_ref, sm_ref, gwi_ref, gwo_ref, gsm_ref, send_sems, recv_sems):
        x, y, c = _position()
        sibling = (x, y, 1 - c)
        chips = [(1 - x, y), (x, 1 - y), (1 - x, 1 - y)]

        def remote(k, src, dst, to):
            return pltpu.make_async_remote_copy(src_ref=src, dst_ref=dst, send_sem=send_sems.at[k],
                                                recv_sem=recv_sems.at[k], device_id=to, device_id_type=MESH)

        first, passed, landed = [], [], []
        for a, (src_ref, g_ref) in enumerate(((wi_ref, gwi_ref), (wo_ref, gwo_ref))):
            for j, (cx, cy) in enumerate(chips):
                slot = g_ref.at[j, c]
                first.append(remote(6 * a + j, src_ref.at[c], slot, (cx, cy, c)))
                landed.append(remote(6 * a + j, slot, slot, sibling))
                passed.append(remote(6 * a + 3 + j, slot, slot, sibling))
        for j, (cx, cy) in enumerate(chips):
            first.append(remote(12 + j, sm_ref, gsm_ref.at[j], (cx, cy, c)))
        for cp in first:
            cp.start()
        for arrived, onward in zip(landed, passed):
            arrived.wait_recv()
            onward.start()
        for a, g_ref in enumerate((gwi_ref, gwo_ref)):
            for j in range(3):
                remote(6 * a + 3 + j, g_ref.at[j, 1 - c], g_ref.at[j, 1 - c], sibling).wait_recv()
        for j in range(3):
            remote(12 + j, sm_ref, gsm_ref.at[j], sibling).wait_recv()
        for cp in first + passed:
            cp.wait_send()

    return pl.pallas_call(
        body, name="gather_weights",
        out_shape=(jax.ShapeDtypeStruct((3,) + wi.shape, wi.dtype), jax.ShapeDtypeStruct((3,) + wo.shape, wo.dtype),
                   jax.ShapeDtypeStruct((3,) + small.shape, small.dtype)),
        in_specs=[ANY, ANY, ANY], out_specs=(ANY, ANY, ANY),
        scratch_shapes=[pltpu.SemaphoreType.DMA((15,)), pltpu.SemaphoreType.DMA((15,))],
    )(wi, wo, small)


def _pair_exchange(gw, gb, pack):
    n_big = N_CHIPS + 1

    def body(gw_ref, gb_ref, p_ref, ra_ref, rb_ref, o_ref, send_sems, recv_sems):
        x, y, c = _position()
        sibling = (x, y, 1 - c)

        def remote(k, src, dst, to):
            return pltpu.make_async_remote_copy(src_ref=src, dst_ref=dst, send_sem=send_sems.at[k],
                                                recv_sem=recv_sems.at[k], device_id=to, device_id_type=MESH)

        copies = [remote(N_CHIPS, gb_ref.at[:, 1 - c], rb_ref, sibling)]
        for s, start in enumerate(WIN_START):
            rows = pl.ds(pl.multiple_of(start + WIN_HALF * (1 - c), 2 * SUBLANE), WIN_HALF)
            copies.append(remote(s, gw_ref.at[rows], ra_ref.at[s], sibling))
        for mask in range(1, N_DEV):
            peer = (1 - x if mask & 4 else x, 1 - y if mask & 2 else y, 1 - c if mask & 1 else c)
            copies.append(remote(n_big + mask - 1, p_ref, o_ref.at[mask - 1], peer))
        for cp in copies:
            cp.start()
        for cp in copies:
            cp.wait()

    n_sems = n_big + N_DEV - 1
    return pl.pallas_call(
        body, name="grad_pair_exchange",
        out_shape=(jax.ShapeDtypeStruct((N_CHIPS, WIN_HALF, D_MODEL), gw.dtype),
                   jax.ShapeDtypeStruct((N_CHIPS,) + gb.shape[2:], gb.dtype),
                   jax.ShapeDtypeStruct((N_DEV - 1,) + pack.shape, pack.dtype)),
        in_specs=[ANY, ANY, ANY], out_specs=(ANY, ANY, ANY),
        scratch_shapes=[pltpu.SemaphoreType.DMA((n_sems,)), pltpu.SemaphoreType.DMA((n_sems,))],
    )(gw, gb, pack)


def _chip_exchange(gw, recv_a, gb, recv_b):
    half, cols = recv_a.shape[1:]
    out_half = recv_b.shape[1]

    def body(gw_ref, ra_in, gb_ref, rb_in, own_a, own_b, sib_a, sib_b, ga_buf, pa_buf, wa_buf, oa_buf, xa_buf,
             gb_buf, pb_buf, wb_buf, ob_buf, xb_buf, in_sems, out_sems, send_sems, recv_sems):
        x, y, c = _position()
        chips = [(1 - x, y), (x, 1 - y), (1 - x, 1 - y)]
        windows = [2 * cx + cy for cx, cy in chips] + [2 * x + y]
        w_in_rows = lambda s: gw_ref.at[pl.ds(pl.multiple_of(_window_start(s) + half * c, SUBLANE), half)]
        parts = ((lambda s: gb_ref.at[s, c], rb_in, gb_buf, pb_buf, wb_buf, ob_buf, own_b, xb_buf, sib_b),
                 (w_in_rows, ra_in, ga_buf, pa_buf, wa_buf, oa_buf, own_a, xa_buf, sib_a))
        remote = []
        for n, (mine, theirs, g_buf, p_buf, wire_buf, own_buf, own_ref, dst, _) in enumerate(parts):

            def fetch(k):
                slot = k % 2
                return (pltpu.make_async_copy(mine(windows[k]), g_buf.at[slot], in_sems.at[n, 0, slot]),
                        pltpu.make_async_copy(theirs.at[windows[k]], p_buf.at[slot], in_sems.at[n, 1, slot]))

            for cp in fetch(0):
                cp.start()
            for k in range(4):
                if k + 1 < 4:
                    for cp in fetch(k + 1):
                        cp.start()
                for cp in fetch(k):
                    cp.wait()
                total = g_buf[k % 2] + p_buf[k % 2].astype(F32)
                if k < 3:
                    wire_buf[k] = total.astype(wire_buf.dtype)
                    cx, cy = chips[k]
                    remote.append(pltpu.make_async_remote_copy(
                        src_ref=wire_buf.at[k], dst_ref=dst.at[k], send_sem=send_sems.at[3 * n + k],
                        recv_sem=recv_sems.at[3 * n + k], device_id=(cx, cy, c), device_id_type=MESH))
                    remote[-1].start()
                else:
                    own_buf[...] = total
        for cp in remote:
            cp.wait()
        kept = []
        for n, (_, _, _, _, _, own_buf, own_ref, arrived, sib_ref) in enumerate(parts):
            own_buf[...] = ((own_buf[...] + arrived[0].astype(F32)) + arrived[1].astype(F32)) + arrived[2].astype(F32)
            kept.append(pltpu.make_async_copy(own_buf, own_ref, out_sems.at[n]))
            kept.append(pltpu.make_async_remote_copy(
                src_ref=own_buf, dst_ref=sib_ref, send_sem=send_sems.at[6 + n], recv_sem=recv_sems.at[6 + n],
                device_id=(x, y, 1 - c), device_id_type=MESH))
            for cp in kept[-2:]:
                cp.start()
        for cp in kept:
            cp.wait()

    vmem = lambda shape, dtype: pltpu.VMEM(shape, dtype)
    return pl.pallas_call(
        body, name="grad_chip_exchange",
        out_shape=(jax.ShapeDtypeStruct((half, cols), F32), jax.ShapeDtypeStruct((out_half, cols), F32)) * 2,
        in_specs=[ANY] * 4, out_specs=(ANY,) * 4,
        scratch_shapes=[vmem((2, half, cols), F32), vmem((2, half, cols), recv_a.dtype), vmem((3, half, cols), WIRE_DTYPE),
                        vmem((half, cols), F32), vmem((3, half, cols), WIRE_DTYPE),
                        vmem((2, out_half, cols), F32), vmem((2, out_half, cols), recv_b.dtype),
                        vmem((3, out_half, cols), WIRE_DTYPE), vmem((out_half, cols), F32),
                        vmem((3, out_half, cols), WIRE_DTYPE),
                        pltpu.SemaphoreType.DMA((2, 2, 2)), pltpu.SemaphoreType.DMA((2,)),
                        pltpu.SemaphoreType.DMA((8,)), pltpu.SemaphoreType.DMA((8,))],
        compiler_params=pltpu.CompilerParams(vmem_limit_bytes=VMEM_LIMIT_BYTES),
    )(gw, recv_a, gb, recv_b)


def _window_start(s):
    return jnp.where(s == 0, WIN_START[0], jnp.where(s == 1, WIN_START[1], jnp.where(s == 2, WIN_START[2], WIN_START[3])))


def _assemble_w(own, others, starts):
    def body(starts_ref, own_ref, oth_ref, o_ref):
        o_ref[...] = jnp.zeros_like(o_ref)
        for k in range(N_CHIPS):
            rows = pl.ds(pl.multiple_of(starts_ref[k], 2 * SUBLANE), WIN_ROWS)
            o_ref[rows, :] = o_ref[rows, :] + (own_ref[...] if k == 0 else oth_ref[k - 1])

    return pl.pallas_call(
        body, name="assemble_w",
        in_specs=[pl.BlockSpec(memory_space=pltpu.SMEM), pl.BlockSpec(memory_space=pltpu.VMEM),
                  pl.BlockSpec(memory_space=pltpu.VMEM)],
        out_specs=pl.BlockSpec(memory_space=pltpu.VMEM),
        out_shape=jax.ShapeDtypeStruct((D_IN_PAD, D_MODEL), own.dtype),
        compiler_params=_params(),
    )(starts, own, others)


def _adamw_math(w, g, m, v):
    m = ADAM_B1 * m + (1.0 - ADAM_B1) * g
    v = ADAM_B2 * v + (1.0 - ADAM_B2) * (g * g)
    m_hat = m * (1.0 / (1.0 - ADAM_B1 ** ADAM_STEP))
    v_hat = v * (1.0 / (1.0 - ADAM_B2 ** ADAM_STEP))
    delta = -ADAM_LR * (m_hat / (jnp.sqrt(v_hat) + ADAM_EPS) + ADAM_WD * w)
    return delta, m, v


def _adamw_big(w, g, m, v, tr):
    rows, cols = w.shape
    assert rows % tr == 0 and g.shape[0] >= rows

    def body(w_ref, g_ref, m_ref, v_ref, d_out, m_out, v_out):
        d, m2, v2 = _adamw_math(w_ref[...], g_ref[...], m_ref[...], v_ref[...])
        d_out[...] = d
        m_out[...] = m2
        v_out[...] = v2

    spec = pl.BlockSpec((tr, cols), lambda i: (i, 0))
    sds = jax.ShapeDtypeStruct((rows, cols), F32)
    return pl.pallas_call(
        body, name="adamw_big", grid=(rows // tr,), in_specs=[spec] * 4, out_specs=(spec,) * 3,
        out_shape=(sds,) * 3, compiler_params=_params(("parallel",)),
    )(w, g, m, v)


def _adamw_rows(w3, g, m3, v3):
    rows, _, cols = w3.shape
    rb = rows // ADAM_ROW_STEPS
    assert rb * ADAM_ROW_STEPS == rows

    def body(w_ref, g_ref, m_ref, v_ref, g_out, d_out, m_out, v_out):
        for k in range(ADAM_ROW_STEPS):
            @pl.when(pl.program_id(0) == k)
            def _(k=k):
                g = g_ref[k * rb:(k + 1) * rb, :]
                d, m2, v2 = _adamw_math(w_ref[:, 0, :], g, m_ref[:, 0, :], v_ref[:, 0, :])
                g_out[:, 0, :] = g
                d_out[:, 0, :] = d
                m_out[:, 0, :] = m2
                v_out[:, 0, :] = v2

    spec3 = pl.BlockSpec((rb, 1, cols), lambda i: (i, 0, 0))
    sds = jax.ShapeDtypeStruct((rows, 1, cols), F32)
    return pl.pallas_call(
        body, name="adamw_rows", grid=(ADAM_ROW_STEPS,),
        in_specs=[spec3, pl.BlockSpec((rows, cols), lambda i: (0, 0), pipeline_mode=pl.Buffered(1)), spec3, spec3],
        out_specs=(spec3,) * 4, out_shape=(sds,) * 4, compiler_params=_params(("parallel",)),
    )(w3, g, m3, v3)


def _small_update(own, others, params, ms, vs):
    slots = (SLOT_NORM, SLOT_FINAL, SLOT_ATTN, SLOT_CONVG, SLOT_BF, SLOT_META, SLOT_CONVW)
    n = len(slots)

    def body(*refs):
        own_ref, gp_ref = refs[:2]
        w_refs, m_refs, v_refs = refs[2:2 + n], refs[2 + n:2 + 2 * n], refs[2 + 2 * n:2 + 3 * n]
        outs = refs[2 + 3 * n:3 + 7 * n]
        loss_ref = outs[0]
        g_outs, d_outs, m_outs, v_outs = (outs[1 + k * n:1 + (k + 1) * n] for k in range(4))
        g_scr, w_scr, m_scr, v_scr = refs[3 + 7 * n:]
        x, y, c = _position()
        shard = 2 * x + y
        me = 4 * x + 2 * y + c
        tot = None
        for d in range(N_DEV):
            rel = jnp.bitwise_xor(me, d)
            term = jnp.where(rel == 0, own_ref[...], gp_ref[jnp.maximum(rel, 1) - 1])
            tot = term if tot is None else tot + term
        r0, r1, _, _ = SLOT_META
        meta_sel = tot[r0:r1, 0:256]
        cw_sel = tot[24:32, 0:128]
        for k in range(1, N_CHIPS):
            meta_sel = jnp.where(shard == k, tot[r0:r1, 256 * k:256 * (k + 1)], meta_sel)
            cw_sel = jnp.where(shard == k, tot[24:32, 128 * k:128 * (k + 1)], cw_sel)
        zeros = jnp.zeros((PACK_ROWS, D_MODEL), F32)
        for scr in (g_scr, w_scr, m_scr, v_scr):
            scr[...] = zeros
        g_scr[0:8, :] = tot[0:8, :]
        g_scr[r0:r1, 0:256] = meta_sel
        g_scr[24:32, 0:128] = cw_sel
        for (a, b, c0, c1), w_ref, m_ref, v_ref in zip(slots, w_refs, m_refs, v_refs):
            w_scr[a:b, c0:c1] = w_ref[...]
            m_scr[a:b, c0:c1] = m_ref[...]
            v_scr[a:b, c0:c1] = v_ref[...]
        loss_ref[...] = g_scr[LOSS_ROW:LOSS_ROW + 1, 0:1]
        d, m2, v2 = _adamw_math(w_scr[...], g_scr[...], m_scr[...], v_scr[...])
        w_scr[...] = d
        m_scr[...] = m2
        v_scr[...] = v2
        for (a, b, c0, c1), g_o, d_o, m_o, v_o in zip(slots, g_outs, d_outs, m_outs, v_outs):
            g_o[...] = g_scr[a:b, c0:c1]
            d_o[...] = w_scr[a:b, c0:c1]
            m_o[...] = m_scr[a:b, c0:c1]
            v_o[...] = v_scr[a:b, c0:c1]

    shapes = [jax.ShapeDtypeStruct(p.shape, F32) for p in params]
    out = pl.pallas_call(
        body, name="small_update",
        out_shape=[jax.ShapeDtypeStruct((1, 1), F32)] + shapes * 4,
        scratch_shapes=[pltpu.VMEM((PACK_ROWS, D_MODEL), F32)] * 4,
        compiler_params=_params(),
    )(own, others, *params, *ms, *vs)
    return out[0], out[1:1 + n], out[1 + n:1 + 2 * n], out[1 + 2 * n:1 + 3 * n], out[1 + 3 * n:1 + 4 * n]


def _in_proj(x2, meta_blk, norm_g, w_pad, bf_pad):
    seq = x2.shape[0]
    lp = seq + FRONT
    t = ROW_TILE
    nt = lp // t
    n_sub = t // LANE

    def body(*refs):
        x_refs = refs[:n_sub]
        mb, g_ref, w_ref, bf_ref, tri_ref = refs[n_sub:n_sub + 5]
        q_ref, k_ref, v_ref, rest_ref, fl_ref, ct_ref, u_ref, qt_ref, kt_ref, vt_ref, cc_ref, carry = refs[n_sub + 5:]
        i = pl.program_id(0)

        @pl.when(i == 0)
        def _():
            carry[...] = jnp.zeros_like(carry)

        first = jnp.where(i == 0, mb[...], x_refs[0][...])
        h = jnp.concatenate([first] + [r[...] for r in x_refs[1:]], axis=0)
        ms = jnp.mean(h * h, axis=-1, keepdims=True)
        u = ((h * lax.rsqrt(ms + EPS)) * g_ref[...]).astype(MXU_DTYPE)
        u_ref[...] = u

        def seg(a, width):
            return _dot_nt(u, w_ref[a:a + width, :])

        fl = seg(SEG_F, LANE)
        fl_ref[...] = fl
        q_tile = seg(SEG_Q, D_ATTN) * (HEAD_DIM ** -0.5)
        q_ref[...] = q_tile.astype(MXU_DTYPE)
        qt_ref[...] = q_tile.T.astype(MXU_DTYPE)
        z = fl + bf_ref[...]
        logf = jnp.minimum(z, 0.0) - jnp.log(1.0 + jnp.exp(-jnp.abs(z)))
        row = i * t + lax.broadcasted_iota(jnp.int32, (t, LANE), 0)
        logf = jnp.where(row >= PAD_ROWS, logf, 0.0)
        k_tile = seg(SEG_K, D_ATTN)
        k_ref[...] = k_tile.astype(MXU_DTYPE)
        kt_ref[...] = k_tile.T.astype(MXU_DTYPE)
        cs = _dot_exact(tri_ref[...], logf) + carry[...]
        carry[...] = carry[...] + jnp.sum(logf, axis=0, keepdims=True)
        v_tile = seg(SEG_V, D_ATTN)
        v_ref[...] = v_tile.astype(MXU_DTYPE)
        vt_ref[...] = v_tile.T.astype(MXU_DTYPE)
        col = i * t + lax.broadcasted_iota(jnp.int32, (SUBLANE, t), 1)
        ct_ref[...] = jnp.where(col >= PAD_ROWS, cs.T[0:SUBLANE, :], -NEG)
        cc_ref[...] = jnp.where(row >= PAD_ROWS, cs, -NEG)
        for s in range(5):
            rest_ref[:, 512 * s:512 * (s + 1)] = seg(SEG_ZA + 512 * s, 512)

    row_blk = lambda cols: pl.BlockSpec((t, cols), lambda i: (i, 0))
    tr_blk = pl.BlockSpec((None, D_ATTN, t), lambda i: (i, 0, 0))
    const = lambda shape: pl.BlockSpec(shape, lambda i: (0, 0))
    return pl.pallas_call(
        body, name="in_proj", grid=(nt,),
        in_specs=_x_block_specs(n_sub, LANE) + [const((LANE, D_MODEL)), const((1, D_MODEL)),
                                                pl.BlockSpec((D_IN_PAD, D_MODEL), lambda i: (0, 0),
                                                             pipeline_mode=pl.Buffered(1)),
                                                const((1, LANE)), const((t, t))],
        out_specs=(row_blk(D_ATTN), row_blk(D_ATTN), row_blk(D_ATTN), row_blk(5 * 512), row_blk(LANE),
                   pl.BlockSpec((SUBLANE, t), lambda i: (0, i)), row_blk(D_MODEL), tr_blk, tr_blk, tr_blk, row_blk(LANE)),
        out_shape=(jax.ShapeDtypeStruct((lp, D_ATTN), MXU_DTYPE), jax.ShapeDtypeStruct((lp, D_ATTN), MXU_DTYPE),
                   jax.ShapeDtypeStruct((lp, D_ATTN), MXU_DTYPE), jax.ShapeDtypeStruct((lp, 5 * 512), F32),
                   jax.ShapeDtypeStruct((lp, LANE), F32),
                   jax.ShapeDtypeStruct((SUBLANE, lp), F32), jax.ShapeDtypeStruct((lp, D_MODEL), MXU_DTYPE),
                   jax.ShapeDtypeStruct((nt, D_ATTN, t), MXU_DTYPE), jax.ShapeDtypeStruct((nt, D_ATTN, t), MXU_DTYPE),
                   jax.ShapeDtypeStruct((nt, D_ATTN, t), MXU_DTYPE), jax.ShapeDtypeStruct((lp, LANE), F32)),
        scratch_shapes=[pltpu.VMEM((1, LANE), F32)],
        compiler_params=_params(("arbitrary",)),
    )(*([x2] * n_sub), meta_blk, norm_g, w_pad, bf_pad, _triangle(t, lower=True))


def _head_masks():
    lane = lax.broadcasted_iota(jnp.int32, (1, LANE), 1)
    return lane < HEAD_DIM, lane >= HEAD_DIM


def _pair_specs(lp, nt, t):
    blk = pl.BlockSpec((lp, LANE), lambda g: (0, g))
    ct_a = pl.BlockSpec((None, nt, 1, t), lambda g: (2 * g, 0, 0, 0))
    ct_b = pl.BlockSpec((None, nt, 1, t), lambda g: (2 * g + 1, 0, 0, 0))
    return blk, ct_a, ct_b


def _sub_rows(s, col):
    return jnp.concatenate([s[:, a * LANE:(a + 1) * LANE] - col for a in range(s.shape[1] // LANE)], axis=1)


def _loop_unrolled(lo, hi, step, init, n):
    def group(jj, carry):
        for k in range(n):
            carry = step(lo + n * jj + k, carry)
        return carry

    groups = (hi - lo) // n
    carry = lax.fori_loop(0, groups, group, init)
    return lax.fori_loop(lo + n * groups, hi, step, carry)


def _attn_fwd(q, k, v_t, cc):
    lp = q.shape[0]
    t = ROW_TILE
    nt = lp // t
    ext = LANE + 2 * SUBLANE

    def body(q_ref, k_ref, vt_ref, cc_ref, o_ref, l_ref, m_ref, s_scr, last_scr, m_scr, mfin_scr, acc_scr, c_scr):
        masks = _head_masks()
        lane = lax.broadcasted_iota(jnp.int32, (1, LANE), 1)
        for hh in range(2):
            picked = jnp.where(lane == 2 * pl.program_id(0) + hh, cc_ref[...], 0.0)
            c_scr[hh] = jnp.broadcast_to(jnp.sum(picked, axis=-1, keepdims=True), (lp, LANE))
        visible = lax.broadcasted_iota(jnp.int32, (t, t), 0) <= lax.broadcasted_iota(jnp.int32, (t, t), 1)
        top = lax.broadcasted_iota(jnp.int32, (LANE, 1), 0) < HEAD_DIM
        second_head = (lax.broadcasted_iota(jnp.int32, (2 * SUBLANE, 2 * t), 1) >= t).astype(jnp.int32)
        ones_rows = jnp.where(lax.broadcasted_iota(jnp.int32, (2 * SUBLANE, 2 * t), 0) == second_head,
                              1.0, 0.0).astype(MXU_DTYPE)

        on_first_diagonal = jnp.concatenate([visible, jnp.ones((t, t), jnp.bool_)], axis=1)

        def scores(j, queries):
            kj = k_ref[pl.ds(pl.multiple_of(j * t, t), t), :]
            return _dot_nt(jnp.concatenate([jnp.where(hm, kj, 0).astype(MXU_DTYPE) for hm in masks], axis=0), queries)

        def biased(s2, j, hh):
            return _sub_rows(s2[hh * t:(hh + 1) * t, :], c_scr[hh, pl.ds(pl.multiple_of(j * t, t), t), :]) * LOG2E

        def track_max(hh, s, lo, hi):
            m = m_scr[hh, :, lo:hi]
            for a in range(t // SUBLANE):
                m = jnp.maximum(m, s[a * SUBLANE:(a + 1) * SUBLANE, :])
            m_scr[hh, :, lo:hi] = m

        def probabilities(scores_of, ms_cols):
            return jnp.concatenate([jnp.exp2(scores_of(hh) - ms_cols[hh]).astype(MXU_DTYPE) for hh in range(2)], axis=0)

        def values(j):
            vtj = vt_ref[j]
            v2 = jnp.concatenate([jnp.where(top, vtj, 0).astype(MXU_DTYPE),
                                  jnp.where(top, 0, vtj).astype(MXU_DTYPE)], axis=1)
            return jnp.concatenate([v2, ones_rows], axis=0)

        def stage(done, ahead):
            if ahead is not None:
                i_a, rows_a = ahead
                qa = q_ref[pl.ds(pl.multiple_of(i_a * t, t), rows_a), :]
                m_scr[...] = jnp.full(m_scr.shape, NEG, F32)

                def max_step(j, mask=None):
                    s2 = scores(j, qa)
                    for hh in range(2):
                        s = biased(s2, j, hh)
                        if mask is not None:
                            s = jnp.where(mask, s, NEG)
                        s_scr[j, hh * t:(hh + 1) * t, 0:rows_a] = s
                        track_max(hh, s, 0, rows_a)

            if done is not None:
                i_d, rows_d = done
                r0 = pl.multiple_of(i_d * t, t)
                ms = [mfin_scr[hh, 0:1, 0:rows_d] for hh in range(2)]
                acc_scr[...] = jnp.zeros(acc_scr.shape, F32)

                def key_step(j, carry):
                    p = probabilities(lambda hh: s_scr[j, hh * t:(hh + 1) * t, 0:rows_d], ms)
                    acc_scr[:, 0:rows_d] = acc_scr[:, 0:rows_d] + _dot(values(j), p)
                    if ahead is not None:
                        max_step(j)
                    return carry

                _loop_unrolled(0, i_d + 1, key_step, 0, ATTN_UNROLL)
                if rows_d == 2 * t:
                    p = probabilities(lambda hh: last_scr[hh * t:(hh + 1) * t, :], [m[:, t:] for m in ms])
                    acc_scr[:, t:rows_d] = acc_scr[:, t:rows_d] + _dot(values(i_d + 1), p)
                acc = acc_scr[:, 0:rows_d]
                l_pair = jnp.where(top, acc[LANE:LANE + 1], acc[LANE + 1:LANE + 2])
                o_ref[pl.ds(r0, rows_d), :] = (acc[:LANE] / l_pair).T
                l_ref[pl.ds(r0, rows_d), :] = l_pair.T
                for hh in range(2):
                    m_ref[pl.ds(r0, rows_d), hh * LANE:(hh + 1) * LANE] = jnp.broadcast_to(ms[hh], (LANE, rows_d)).T

            if ahead is not None:
                if done is not None:
                    max_step(i_a - 1)
                max_step(i_a, on_first_diagonal if rows_a == 2 * t else visible)
                if rows_a == 2 * t:
                    s2 = scores(i_a + 1, qa[t:])
                    for hh in range(2):
                        s = jnp.where(visible, biased(s2, i_a + 1, hh), NEG)
                        last_scr[hh * t:(hh + 1) * t, :] = s
                        track_max(hh, s, t, rows_a)
                for hh in range(2):
                    mfin_scr[hh, :, 0:rows_a] = jnp.broadcast_to(jnp.max(m_scr[hh, :, 0:rows_a], axis=0, keepdims=True),
                                                                 (SUBLANE, rows_a))

        pairs = nt // 2
        stage(None, (0, 2 * t))

        def pair_to_pair(u, _):
            stage((2 * u, 2 * t), (2 * u + 2, 2 * t))
            return 0

        lax.fori_loop(0, pairs - 1, pair_to_pair, 0)
        if nt % 2:
            stage((2 * pairs - 2, 2 * t), (nt - 1, t))
            stage((nt - 1, t), None)
        else:
            stage((2 * pairs - 2, 2 * t), None)

    blk = pl.BlockSpec((lp, LANE), lambda g: (0, g))
    return pl.pallas_call(
        body, name="attn_fwd", grid=(HEADS // 2,),
        in_specs=[blk, blk, pl.BlockSpec((nt, LANE, t), lambda g: (0, g, 0)),
                  pl.BlockSpec((lp, LANE), lambda g: (0, 0), pipeline_mode=pl.Buffered(1))],
        out_specs=(blk, blk, pl.BlockSpec((lp, 2 * LANE), lambda g: (0, g))),
        out_shape=(jax.ShapeDtypeStruct((lp, D_ATTN), F32), jax.ShapeDtypeStruct((lp, D_ATTN), F32),
                   jax.ShapeDtypeStruct((lp, HEADS * LANE), F32)),
        scratch_shapes=[pltpu.VMEM((nt, 2 * t, 2 * t), F32), pltpu.VMEM((2 * t, t), F32),
                        pltpu.VMEM((2, SUBLANE, 2 * t), F32), pltpu.VMEM((2, SUBLANE, 2 * t), F32),
                        pltpu.VMEM((ext, 2 * t), F32), pltpu.VMEM((2, lp, LANE), F32)],
        compiler_params=_params(("parallel",)),
    )(q, k, v_t, cc)


def _attn_bwd(q, k, v, do, q_t, k_t, do_t, m, neg_delta, ct4):
    lp = q.shape[0]
    t = ROW_TILE
    nt = lp // t

    def body(q_ref, k_ref, v_ref, do_ref, qt_ref, kt_ref, dot_ref, ma_ref, mb_ref, nd_ref, cta_ref, ctb_ref,
             dq_ref, dk_ref, dv_ref, dc_ref, dq_acc, dk_acc, dv_acc):
        masks = _head_masks()
        ct_refs, m_refs = (cta_ref, ctb_ref), (ma_ref, mb_ref)
        row_head = 2 * pl.program_id(0) + (lax.broadcasted_iota(jnp.int32, (2 * t, LANE), 0) >= t).astype(jnp.int32)
        col = lax.broadcasted_iota(jnp.int32, (2 * t, LANE), 1)
        delta_ones = jnp.where((col < HEADS * DELTA_TERMS) & (col % HEADS == row_head), 1.0, 0.0).astype(MXU_DTYPE)
        below = lax.broadcasted_iota(jnp.int32, (t, t), 1) <= lax.broadcasted_iota(jnp.int32, (t, t), 0)
        top = lax.broadcasted_iota(jnp.int32, (LANE, 1), 0) < HEAD_DIM
        dq_acc[...] = jnp.zeros_like(dq_acc)

        on_first_diagonal = jnp.concatenate([below, jnp.ones((t, t), jnp.bool_)], axis=0)

        def k_block(j, _, with_next=True):
            c0 = pl.multiple_of(j * t, t)
            kj = k_ref[pl.ds(c0, t), :]
            vj = v_ref[pl.ds(c0, t), :]
            k2 = jnp.concatenate([jnp.where(hm, kj, 0).astype(MXU_DTYPE) for hm in masks], axis=0)
            v2 = jnp.concatenate([jnp.where(hm, vj, 0).astype(MXU_DTYPE) for hm in masks], axis=0)
            v2 = jnp.concatenate([v2, delta_ones], axis=1)
            ck = [r[j] for r in ct_refs]
            ktj = kt_ref[j]
            k2t = jnp.concatenate([jnp.where(top, ktj, 0).astype(MXU_DTYPE), jnp.where(top, 0, ktj).astype(MXU_DTYPE)],
                                  axis=1)
            dk_acc[...] = jnp.zeros_like(dk_acc)
            dv_acc[...] = jnp.zeros_like(dv_acc)

            def q_block(i, colsums, mask=None, rows=t):
                r0 = pl.multiple_of(i * t, t)
                qi = q_ref[pl.ds(r0, rows), :]
                doi = jnp.concatenate([do_ref[pl.ds(r0, rows), :], nd_ref[pl.ds(r0, rows), :]], axis=1)
                qti = jnp.concatenate([qt_ref[i + b] for b in range(rows // t)], axis=1)
                doti = jnp.concatenate([dot_ref[i + b] for b in range(rows // t)], axis=1)
                s2 = _dot_nt(qi, k2)
                dp2 = _dot_nt(doi, v2)
                out, ps, dss = [], [], []
                for hh in range(2):
                    s = (s2[:, hh * t:(hh + 1) * t] - ck[hh]) * LOG2E
                    if mask is not None:
                        s = jnp.where(mask, s, NEG)
                    p = jnp.exp2(_sub_rows(s, m_refs[hh][pl.ds(r0, rows), :])).astype(MXU_DTYPE)
                    ds32 = p.astype(F32) * dp2[:, hh * t:(hh + 1) * t]
                    ps.append(p)
                    dss.append(ds32.astype(MXU_DTYPE))
                    out.append(colsums[hh] + jnp.sum(ds32, axis=0, keepdims=True))
                ds_cat = jnp.concatenate(dss, axis=1)
                dv_acc[...] = dv_acc[...] + _dot(doti, jnp.concatenate(ps, axis=1))
                dk_acc[...] = dk_acc[...] + _dot(qti, ds_cat)
                dq_t = _dot(k2t, ds_cat.T)
                for b in range(rows // t):
                    dq_acc[i + b] = dq_acc[i + b] + dq_t[:, b * t:(b + 1) * t]
                return tuple(out)

            nq = ATTN_BWD_QBLOCKS
            colsums = (jnp.zeros((1, t), F32), jnp.zeros((1, t), F32))
            if with_next:
                colsums = q_block(j, colsums, on_first_diagonal, nq * t)
            else:
                colsums = q_block(j, colsums, below)
            first = j + (nq if with_next else 1)
            groups = (nt - first) // nq
            colsums = lax.fori_loop(0, groups, lambda p, c: q_block(first + nq * p, c, None, nq * t), colsums)
            colsums = lax.fori_loop(first + nq * groups, nt, q_block, colsums)
            for hh in range(2):
                dc_ref[hh, j] = -colsums[hh]
            own = lambda acc: jnp.concatenate([acc[:HEAD_DIM, :t], acc[HEAD_DIM:, t:]], axis=0).T
            dk_ref[pl.ds(c0, t), :] = own(dk_acc[...]).astype(dk_ref.dtype)
            dv_ref[pl.ds(c0, t), :] = own(dv_acc[...]).astype(dv_ref.dtype)
            return 0

        lax.fori_loop(0, nt - 1, k_block, 0)
        k_block(nt - 1, 0, with_next=False)
        for i in range(nt):
            dq_ref[i * t:(i + 1) * t, :] = (dq_acc[i].T * (HEAD_DIM ** -0.5)).astype(dq_ref.dtype)

    blk, ct_a, ct_b = _pair_specs(lp, nt, t)
    rep_a = pl.BlockSpec((lp, LANE), lambda g: (0, 2 * g))
    rep_b = pl.BlockSpec((lp, LANE), lambda g: (0, 2 * g + 1))
    tr_blk = pl.BlockSpec((nt, LANE, t), lambda g: (0, g, 0))
    return pl.pallas_call(
        body, name="attn_bwd", grid=(HEADS // 2,),
        in_specs=[blk] * 4 + [tr_blk, tr_blk, tr_blk, rep_a, rep_b, pl.BlockSpec((lp, LANE), lambda g: (0, 0)), ct_a, ct_b],
        out_specs=(blk, blk, blk, pl.BlockSpec((2, nt, 1, t), lambda g: (g, 0, 0, 0))),
        out_shape=(jax.ShapeDtypeStruct((lp, D_ATTN), MXU_DTYPE),) * 3
                  + (jax.ShapeDtypeStruct((HEADS, nt, 1, t), F32),),
        scratch_shapes=[pltpu.VMEM((nt, LANE, t), F32), pltpu.VMEM((LANE, 2 * t), F32), pltpu.VMEM((LANE, 2 * t), F32)],
        compiler_params=_params(("parallel",)),
    )(q, k, v, do, q_t, k_t, do_t, m, m, neg_delta, ct4, ct4)


def _shift_down(prev8, cur, k):
    ext = jnp.concatenate([prev8, cur], axis=0)
    return pltpu.roll(ext, k, 0)[SUBLANE:, :]


def _shift_up(cur, next8, k):
    ext = jnp.concatenate([cur, next8], axis=0)
    n = ext.shape[0]
    return pltpu.roll(ext, n - k, 0)[:cur.shape[0], :]


def _post(o, l_sum, rest, x2, meta_blk, tgt2, w_out, attn_g, conv_g, final_g, conv_w8):
    lp = o.shape[0]
    t = ROW_TILE
    nt = lp // t
    n_sub = t // LANE
    hb = t // SUBLANE

    def body(*refs):
        o_ref, l_ref, za_ref, gb_ref, gc_ref, xc_ref, zc_ref, gch_ref, xch_ref = refs[:9]
        x_refs = refs[9:9 + n_sub]
        mb = refs[9 + n_sub]
        t_refs = refs[10 + n_sub:10 + 2 * n_sub]
        wo_ref, ag_ref, cg_ref, fg_ref, cw_ref, gm_ref, hr_ref = refs[10 + 2 * n_sub:17 + 2 * n_sub]
        (dout_ref, do_ref, dot_ref, dl_ref, dza_ref, dgb_ref, dzc_ref, dcv_ref,
         loss_ref, gf_ref, gag_ref, gcg_ref, gwo_ref) = refs[17 + 2 * n_sub:]
        i = pl.program_id(0)

        @pl.when(i == 0)
        def _():
            for r in (loss_ref, gf_ref, gag_ref, gcg_ref, gwo_ref):
                r[...] = jnp.zeros_like(r)

        gmat = gm_ref[...]
        inv_g = 1.0 / HEAD_DIM
        o_v = o_ref[...]
        ra = lax.rsqrt(_group_sum(o_v * o_v, gmat, STAT_TERMS) * inv_g + EPS)
        n_a = o_v * ra
        a_n = n_a * ag_ref[...]
        za = za_ref[...]
        sig_a = _sigmoid(za)
        sz_a = za * sig_a
        y_a = a_n * sz_a
        gb = gb_ref[...]
        gc = gc_ref[...]
        xc = xc_ref[...]
        cx = gc * xc
        cx_prev = jnp.where(i == 0, 0.0, gch_ref[...] * xch_ref[...])
        conv = (cw_ref[0:1, :] * _shift_down(cx_prev, cx, 2) + cw_ref[1:2, :] * _shift_down(cx_prev, cx, 1)
                + cw_ref[2:3, :] * cx)
        e = gb * conv
        re = lax.rsqrt(_group_sum(e * e, gmat, STAT_TERMS) * inv_g + EPS)
        n_e = e * re
        e_n = n_e * cg_ref[...]
        zc = zc_ref[...]
        sig_c = _sigmoid(zc)
        sz_c = zc * sig_c
        y_c = e_n * sz_c
        mix = jnp.concatenate([y_a, y_c], axis=-1)
        mix_b = mix.astype(MXU_DTYPE)
        first = jnp.where(i == 0, mb[...], x_refs[0][...])
        h = jnp.concatenate([first] + [r[...] for r in x_refs[1:]], axis=0)
        out = h + _dot(mix_b, wo_ref[...])
        r2 = lax.rsqrt(jnp.mean(out * out, axis=-1, keepdims=True) + EPS)
        n_f = out * r2
        y = n_f * fg_ref[...]
        tgt = jnp.concatenate([r[...] for r in t_refs], axis=0)
        valid = (i * t + lax.broadcasted_iota(jnp.int32, (t, 1), 0)) >= FRONT
        diff = jnp.where(valid, y - tgt, 0.0)
        loss_ref[...] = loss_ref[...] + 0.5 * jnp.sum(jnp.sum(diff * diff, axis=-1, keepdims=True) * (1.0 / D_MODEL))
        dy = diff * (1.0 / D_MODEL)
        gf_ref[...] = gf_ref[...] + jnp.sum(dy * n_f, axis=0, keepdims=True)
        dn = dy * fg_ref[...]
        d_out = r2 * (dn - n_f * jnp.mean(dn * n_f, axis=-1, keepdims=True))
        dout_ref[...] = d_out
        d_out_b = d_out.astype(MXU_DTYPE)
        d_mix = _dot_nt(d_out_b, wo_ref[...])
        gwo_ref[...] = gwo_ref[...] + _dot(mix.T.astype(MXU_DTYPE), d_out_b)
        d_ya = d_mix[:, :D_ATTN]
        d_yc = d_mix[:, D_ATTN:]
        d_an = d_ya * sz_a
        dza_ref[...] = (d_ya * a_n * (sig_a * (1.0 + za * (1.0 - sig_a)))).astype(dza_ref.dtype)
        gag_ref[...] = gag_ref[...] + jnp.sum(d_an * n_a, axis=0, keepdims=True)
        dn_a = d_an * ag_ref[...]
        d_o = ra * (dn_a - n_a * (_group_sum(dn_a * n_a, gmat, STAT_TERMS) * inv_g))
        d_o_l = d_o / l_ref[...]
        d_o_b = d_o_l.astype(do_ref.dtype)
        do_ref[...] = d_o_b
        dot_ref[...] = d_o_l.T.astype(dot_ref.dtype)
        delta = _group_sum(d_o_b.astype(F32) * o_v, hr_ref[...])
        terms, rest_of = [], delta
        for k in range(DELTA_TERMS):
            terms.append(rest_of.astype(MXU_DTYPE).astype(F32))
            rest_of = rest_of - terms[-1]
        dl_ref[...] = -sum(pltpu.roll(term, HEADS * k, 1) if k else term
                           for k, term in enumerate(terms)).astype(dl_ref.dtype)
        d_en = d_yc * sz_c
        dzc_ref[...] = (d_yc * e_n * (sig_c * (1.0 + zc * (1.0 - sig_c)))).astype(dzc_ref.dtype)
        gcg_ref[...] = gcg_ref[...] + jnp.sum(d_en * n_e, axis=0, keepdims=True)
        dn_e = d_en * cg_ref[...]
        d_e = re * (dn_e - n_e * (_group_sum(dn_e * n_e, gmat, STAT_TERMS) * inv_g))
        dgb_ref[...] = (d_e * conv).astype(dgb_ref.dtype)
        dcv_ref[...] = d_e * gb

    head_rep = jnp.where((lax.broadcasted_iota(jnp.int32, (D_ATTN, LANE), 0) >> 6)
                         == lax.broadcasted_iota(jnp.int32, (D_ATTN, LANE), 1), 1.0, 0.0).astype(MXU_DTYPE)
    row_blk = lambda cols: pl.BlockSpec((t, cols), lambda i: (i, 0))
    rest_blk = lambda s: pl.BlockSpec((t, 512), functools.partial(lambda i, s: (i, s), s=s))
    halo = lambda s: pl.BlockSpec((SUBLANE, 512), functools.partial(lambda i, s: (jnp.maximum(i * hb - 1, 0), s), s=s))
    const = lambda shape: pl.BlockSpec(shape, lambda i: (0, 0))
    acc = lambda shape: pl.BlockSpec(shape, lambda i: (0, 0))
    return pl.pallas_call(
        body, name="post_fwd_bwd", grid=(nt,),
        in_specs=[row_blk(D_ATTN), row_blk(D_ATTN)] + [rest_blk(s) for s in range(5)] + [halo(2), halo(3)]
                 + _x_block_specs(n_sub, LANE) + [const((LANE, D_MODEL))] + _x_block_specs(n_sub, LANE)
                 + [const((D_MODEL, D_MODEL)), const((1, D_ATTN)), const((1, D_CONV)), const((1, D_MODEL)),
                    const((SUBLANE, D_CONV)), const((D_ATTN, D_ATTN)), const((D_ATTN, LANE))],
        out_specs=(row_blk(D_MODEL), row_blk(D_ATTN), pl.BlockSpec((None, D_ATTN, t), lambda i: (i, 0, 0)),
                   row_blk(LANE), row_blk(D_ATTN), row_blk(D_CONV),
                   row_blk(D_CONV), row_blk(D_CONV),
                   acc((1, LANE)), acc((1, D_MODEL)), acc((1, D_ATTN)), acc((1, D_CONV)), acc((D_MODEL, D_MODEL))),
        out_shape=(jax.ShapeDtypeStruct((lp, D_MODEL), F32), jax.ShapeDtypeStruct((lp, D_ATTN), MXU_DTYPE),
                   jax.ShapeDtypeStruct((nt, D_ATTN, t), MXU_DTYPE), jax.ShapeDtypeStruct((lp, LANE), MXU_DTYPE),
                   jax.ShapeDtypeStruct((lp, D_ATTN), MXU_DTYPE),
                   jax.ShapeDtypeStruct((lp, D_CONV), MXU_DTYPE), jax.ShapeDtypeStruct((lp, D_CONV), MXU_DTYPE),
                   jax.ShapeDtypeStruct((lp, D_CONV), F32),
                   jax.ShapeDtypeStruct((1, LANE), F32), jax.ShapeDtypeStruct((1, D_MODEL), F32),
                   jax.ShapeDtypeStruct((1, D_ATTN), F32), jax.ShapeDtypeStruct((1, D_CONV), F32),
                   jax.ShapeDtypeStruct((D_MODEL, D_MODEL), F32)),
        compiler_params=_params(("arbitrary",)),
    )(o, l_sum, *([rest] * 5), rest, rest, *([x2] * n_sub), meta_blk, *([tgt2] * n_sub),
      w_out, attn_g, conv_g, final_g, conv_w8, _group_matrix(), head_rep)


def _bwd_in(x2, meta_blk, norm_g, w_pad, bf_pad, fl, dc, dq, dk, dv, dza, dgb, dzc, dconv, rest, d_out, conv_w8):
    lp = fl.shape[0]
    t = ROW_TILE
    nt = lp // t
    n_sub = t // LANE
    hb = t // SUBLANE
    rev = lambda i: nt - 1 - i

    def body(*refs):
        x_refs = refs[:n_sub]
        (mb, g_ref, w_ref, bf_ref, fl_ref, dc_ref, dq_ref, dk_ref, dv_ref, dza_ref, dgb_ref, dzc_ref,
         dcv_ref, dcvn_ref, gc_ref, xc_ref, gch_ref, xch_ref, dout_ref, cw_ref, tri_ref) = refs[n_sub:n_sub + 21]
        dp_ref, gx_ref, front_ref, gn_ref, gbf_ref, gcw_ref, carry, dh_scr, gx_sems = refs[n_sub + 21:]
        step = pl.program_id(0)
        i = rev(step)

        @pl.when(step == 0)
        def _():
            for r in (gn_ref, gbf_ref, gcw_ref, carry):
                r[...] = jnp.zeros_like(r)

        dc8 = jnp.concatenate([dc_ref[...], jnp.zeros((LANE - HEADS, t), F32)], axis=0).T
        dlogf = _dot_exact(tri_ref[...], dc8) + carry[...]
        carry[...] = carry[...] + jnp.sum(dc8, axis=0, keepdims=True)
        z = fl_ref[...] + bf_ref[...]
        row = i * t + lax.broadcasted_iota(jnp.int32, (t, LANE), 0)
        d_f = jnp.where(row >= PAD_ROWS, dlogf * (1.0 / (1.0 + jnp.exp(z))), 0.0)
        gbf_ref[...] = gbf_ref[...] + jnp.sum(d_f, axis=0, keepdims=True)
        dcv = dcv_ref[...]
        dcv_next = jnp.where(i == nt - 1, 0.0, dcvn_ref[...])
        d_cx = (cw_ref[2:3, :] * dcv + cw_ref[1:2, :] * _shift_up(dcv, dcv_next, 1)
                + cw_ref[0:1, :] * _shift_up(dcv, dcv_next, 2))
        gc = gc_ref[...]
        xc = xc_ref[...]
        cx = gc * xc
        cx_prev = jnp.where(i == 0, 0.0, gch_ref[...] * xch_ref[...])
        rowi = lax.broadcasted_iota(jnp.int32, (SUBLANE, 1), 0)
        gcw = (jnp.where(rowi == 0, jnp.sum(dcv * _shift_down(cx_prev, cx, 2), axis=0, keepdims=True), 0.0)
               + jnp.where(rowi == 1, jnp.sum(dcv * _shift_down(cx_prev, cx, 1), axis=0, keepdims=True), 0.0)
               + jnp.where(rowi == 2, jnp.sum(dcv * cx, axis=0, keepdims=True), 0.0))
        gcw_ref[...] = gcw_ref[...] + gcw
        dp_ref[:, SEG_Q:SEG_Q + 512] = dq_ref[...]
        dp_ref[:, SEG_K:SEG_K + 512] = dk_ref[...]
        dp_ref[:, SEG_V:SEG_V + 512] = dv_ref[...]
        dp_ref[:, SEG_F:SEG_F + LANE] = d_f.astype(dp_ref.dtype)
        dp_ref[:, SEG_ZA:SEG_ZA + 512] = dza_ref[...]
        dp_ref[:, SEG_GB:SEG_GB + 512] = dgb_ref[...]
        dp_ref[:, SEG_GC:SEG_GC + 512] = (d_cx * xc).astype(dp_ref.dtype)
        dp_ref[:, SEG_XC:SEG_XC + 512] = (d_cx * gc).astype(dp_ref.dtype)
        dp_ref[:, SEG_ZC:SEG_ZC + 512] = dzc_ref[...]
        d_u = _dot(dp_ref[...], w_ref[...])
        first = jnp.where(i == 0, mb[...], x_refs[0][...])
        h = jnp.concatenate([first] + [r[...] for r in x_refs[1:]], axis=0)
        r1 = lax.rsqrt(jnp.mean(h * h, axis=-1, keepdims=True) + EPS)
        n_h = h * r1
        gn_ref[...] = gn_ref[...] + jnp.sum(d_u * n_h, axis=0, keepdims=True)
        dn = d_u * g_ref[...]
        d_h = dout_ref[...] + r1 * (dn - n_h * jnp.mean(dn * n_h, axis=-1, keepdims=True))
        slot = step % 2

        def to_grad_x(slot_, tile):
            return pltpu.make_async_copy(dh_scr.at[slot_], gx_ref.at[pl.ds(pl.multiple_of(tile * t - FRONT, SUBLANE), t)],
                                         gx_sems.at[slot_])

        @pl.when(step >= 2)
        def _():
            to_grad_x(slot, 1).wait()

        dh_scr[slot] = d_h

        @pl.when(i > 0)
        def _():
            to_grad_x(slot, i).start()

        @pl.when(i == 0)
        def _():
            front_ref[...] = d_h[:FRONT]
            rest_rows = pltpu.make_async_copy(dh_scr.at[slot, pl.ds(FRONT, t - FRONT)], gx_ref.at[pl.ds(0, t - FRONT)],
                                              gx_sems.at[slot])
            rest_rows.start()
            rest_rows.wait()
            if nt >= 2:
                to_grad_x(1 - slot, 1).wait()

    def x_specs():
        specs = [pl.BlockSpec((LANE, D_MODEL), lambda s: (jnp.maximum(n_sub * rev(s) - 1, 0), 0))]
        for b in range(1, n_sub):
            specs.append(pl.BlockSpec((LANE, D_MODEL), functools.partial(lambda s, b: (n_sub * rev(s) - 1 + b, 0), b=b)))
        return specs

    row_blk = lambda cols: pl.BlockSpec((t, cols), lambda s: (rev(s), 0))
    rest_blk = lambda k: pl.BlockSpec((t, 512), functools.partial(lambda s, k: (rev(s), k), k=k))
    halo_prev = lambda k: pl.BlockSpec(
        (SUBLANE, 512), functools.partial(lambda s, k: (jnp.maximum(rev(s) * hb - 1, 0), k), k=k))
    halo_next = pl.BlockSpec((SUBLANE, 512), lambda s: (jnp.minimum((rev(s) + 1) * hb, lp // SUBLANE - 1), 0))
    const = lambda shape: pl.BlockSpec(shape, lambda s: (0, 0))
    return pl.pallas_call(
        body, name="bwd_in", grid=(nt,),
        in_specs=x_specs() + [const((LANE, D_MODEL)), const((1, D_MODEL)),
                              pl.BlockSpec((D_IN_PAD, D_MODEL), lambda s: (0, 0), pipeline_mode=pl.Buffered(1)),
                              const((1, LANE)), row_blk(LANE),
                              pl.BlockSpec((HEADS, t), lambda s: (0, rev(s))),
                              row_blk(512), row_blk(512), row_blk(512), row_blk(512), row_blk(512), row_blk(512),
                              row_blk(512), halo_next, rest_blk(2), rest_blk(3), halo_prev(2), halo_prev(3),
                              row_blk(D_MODEL), const((SUBLANE, D_CONV)), const((t, t))],
        out_specs=(row_blk(D_IN_PAD), ANY, const((FRONT, D_MODEL)), const((1, D_MODEL)), const((1, LANE)),
                   const((SUBLANE, D_CONV))),
        out_shape=(jax.ShapeDtypeStruct((lp, D_IN_PAD), MXU_DTYPE), jax.ShapeDtypeStruct((lp - FRONT, D_MODEL), F32),
                   jax.ShapeDtypeStruct((FRONT, D_MODEL), F32),
                   jax.ShapeDtypeStruct((1, D_MODEL), F32), jax.ShapeDtypeStruct((1, LANE), F32),
                   jax.ShapeDtypeStruct((SUBLANE, D_CONV), F32)),
        scratch_shapes=[pltpu.VMEM((1, LANE), F32), pltpu.VMEM((2, t, D_MODEL), F32), pltpu.SemaphoreType.DMA((2,))],
        compiler_params=_params(("arbitrary",)),
    )(*([x2] * n_sub), meta_blk, norm_g, w_pad, bf_pad, fl, dc, dq, dk, dv, dza, dgb, dzc, dconv, dconv,
      rest, rest, rest, rest, d_out, conv_w8, _triangle(t, lower=False))


def _grad_w_in(u, dproj):
    lp = u.shape[0]
    tn = GW_COL_TILE
    tk = tn if lp % tn == 0 else ROW_TILE

    def body(d_ref, u_ref, o_ref, wire_ref):
        k = pl.program_id(1)

        @pl.when(k == 0)
        def _():
            o_ref[...] = jnp.zeros_like(o_ref)

        o_ref[...] = o_ref[...] + lax.dot_general(d_ref[...], u_ref[...], (((0,), (0,)), ((), ())),
                                                  preferred_element_type=F32)

        @pl.when(k == pl.num_programs(1) - 1)
        def _():
            wire_ref[...] = o_ref[...].astype(wire_ref.dtype)

    out_spec = pl.BlockSpec((tn, D_MODEL), lambda n, k: (n, 0))
    return pl.pallas_call(
        body, name="grad_w_in", grid=(D_IN_PAD // tn, lp // tk),
        in_specs=[pl.BlockSpec((tk, tn), lambda n, k: (k, n)), pl.BlockSpec((tk, D_MODEL), lambda n, k: (k, 0))],
        out_specs=(out_spec, out_spec),
        out_shape=(jax.ShapeDtypeStruct((D_IN_PAD, D_MODEL), F32), jax.ShapeDtypeStruct((D_IN_PAD, D_MODEL), WIRE_DTYPE)),
        compiler_params=_params(("parallel", "arbitrary")),
    )(dproj, u)


def _by_chip(own, others, me):
    by_mask = jnp.stack([own, others[1], others[0], others[2]])
    return [lax.dynamic_index_in_dim(by_mask, jnp.bitwise_xor(me, s), 0, keepdims=False) for s in range(N_CHIPS)]


def _both_halves(mine, other, c):
    return jnp.where(c == 0, jnp.concatenate([mine, other], axis=0), jnp.concatenate([other, mine], axis=0))


def _local_step(x2, tgt2, meta_full, norm_g, w_pad, b_f, conv_w_full, attn_g, conv_g, w_out_full, final_g):
    lp = x2.shape[0] + FRONT
    nt = lp // ROW_TILE
    meta_blk = jnp.concatenate([jnp.zeros((PAD_ROWS, D_MODEL), F32), meta_full], axis=0)
    bf_pad = jnp.pad(b_f, ((0, 0), (0, LANE - HEADS)))
    conv_w8 = jnp.pad(conv_w_full, ((0, SUBLANE - conv_w_full.shape[0]), (0, 0)))
    q, k, v, rest, fl, ct, u, q_t, k_t, v_t, cc = _in_proj(x2, meta_blk, norm_g, w_pad, bf_pad)
    ct4 = ct.reshape(SUBLANE, nt, 1, ROW_TILE)
    o, l_sum, m_max = _attn_fwd(q, k, v_t, cc)
    (d_out, d_o, do_t, neg_delta, dza, dgb, dzc, dconv, loss, g_final, g_attn, g_convg, gw_out) = _post(
        o, l_sum, rest, x2, meta_blk, tgt2, w_out_full, attn_g, conv_g, final_g, conv_w8)
    dq, dk, dv, dc = _attn_bwd(q, k, v, d_o, q_t, k_t, do_t, m_max, neg_delta, ct4)
    dproj, grad_x, d_front, g_norm, g_bf, g_cw = _bwd_in(x2, meta_blk, norm_g, w_pad, bf_pad, fl, dc.reshape(HEADS, lp), dq, dk, dv,
                                             dza, dgb, dzc, dconv, rest, d_out, conv_w8)
    gw_in, gw_in_wire = _grad_w_in(u, dproj)
    return dict(loss=loss, grad_x=grad_x, d_front=d_front, g_norm=g_norm, g_final=g_final, g_attn=g_attn, g_convg=g_convg, g_bf=g_bf,
                g_cw=g_cw, gw_out=gw_out, gw_in=gw_in, gw_in_wire=gw_in_wire)


def kernel(x, meta, norm_g, w_in, b_f, conv_w, attn_norm_g, conv_norm_g, w_out, final_norm_g, loss_target, m_meta, m_norm_g, m_w_in, m_b_f, m_conv_w, m_attn_norm_g, m_conv_norm_g, m_w_out, m_final_norm_g, v_meta, v_norm_g, v_w_in, v_b_f, v_conv_w, v_attn_norm_g, v_conv_norm_g, v_w_out, v_final_norm_g):
    cx_, cy_, cc_ = _position()
    chip = 2 * cx_ + cy_
    shard = w_in.shape[2]
    out_half = w_out.shape[1] // 2
    pick = lambda vals: jnp.where(chip == 0, vals[0], jnp.where(chip == 1, vals[1], jnp.where(chip == 2, vals[2], vals[3])))
    a_off, b_off = pick(A_OFF), pick(B_OFF)
    wt = jnp.transpose(w_in[0]).astype(MXU_DTYPE)

    def placed(piece, off):
        return lax.dynamic_slice_in_dim(jnp.pad(piece, ((WIN_ROWS, WIN_ROWS), (0, 0))), WIN_ROWS - off, WIN_ROWS, 0)

    wi = placed(wt[:PIECE_A], a_off) + placed(wt[PIECE_A:], b_off)
    wo = w_out[0].astype(MXU_DTYPE)
    small = jnp.concatenate([meta, jnp.pad(conv_w[0], ((0, 8 - conv_w.shape[1]), (0, meta.shape[1] - conv_w.shape[2])))],
                            axis=0)
    gwi, gwo, gsm = _gather_weights(wi.reshape(2, WIN_HALF, D_MODEL), wo.reshape(2, out_half, D_MODEL), small)
    starts = jnp.stack([_window_start(jnp.bitwise_xor(chip, mask)) for mask in (0, 2, 1, 3)]).astype(jnp.int32)
    w_pad = _assemble_w(wi, gwi.reshape(3, WIN_ROWS, D_MODEL), starts)
    w_out_full = jnp.concatenate(_by_chip(wo, gwo.reshape(3, 2 * out_half, D_MODEL), chip), axis=0)
    small_full = jnp.concatenate(_by_chip(small, gsm, chip), axis=1)
    meta_full = small_full[:N_META]
    conv_w_full = jnp.concatenate([small_full[N_META:N_META + 3, 256 * s:256 * s + LANE] for s in range(N_CHIPS)], axis=1)
    final_g2 = final_norm_g.reshape(1, D_MODEL)
    r = _local_step(x[0], loss_target[0], meta_full, norm_g, w_pad, b_f, conv_w_full, attn_norm_g, conv_norm_g,
                    w_out_full, final_g2)
    grad_x = r["grad_x"][None]
    gb = r["gw_out"].reshape(N_CHIPS, 2, out_half, D_MODEL)
    wide = lambda a: jnp.pad(a, ((0, 0), (0, D_MODEL - a.shape[1])))
    pack = jnp.concatenate([
        r["g_norm"], r["g_final"], jnp.concatenate([r["g_attn"], r["g_convg"]], axis=1), wide(r["g_bf"]),
        wide(r["loss"]), jnp.zeros((3, D_MODEL), F32), r["d_front"][PAD_ROWS:], wide(r["g_cw"])], axis=0)
    ra, rb, packs = _pair_exchange(r["gw_in_wire"], gb, pack)
    ha, hb, oa, ob = _chip_exchange(r["gw_in"], ra, gb, rb)
    g_window = _both_halves(ha, oa, cc_)
    g_w_in_t = jnp.concatenate([lax.dynamic_slice_in_dim(g_window, a_off, PIECE_A, 0),
                                lax.dynamic_slice_in_dim(g_window, b_off, shard - PIECE_A, 0)], axis=0)
    g_w_out = _both_halves(hb, ob, cc_)
    as_rows = lambda a: jnp.transpose(a, (2, 0, 1))
    g_w_in, d_w_in, nm_w_in, nv_w_in = (jnp.transpose(a, (1, 2, 0)) for a in _adamw_rows(
        as_rows(w_in), g_w_in_t, as_rows(m_w_in), as_rows(v_w_in)))
    d_w_out, nm_w_out, nv_w_out = (a[None] for a in _adamw_big(w_out[0], g_w_out, m_w_out[0], v_w_out[0], LANE))
    params = (norm_g, final_g2, attn_norm_g, conv_norm_g, b_f, meta, conv_w[0])
    ms = (m_norm_g, m_final_norm_g.reshape(1, D_MODEL), m_attn_norm_g, m_conv_norm_g, m_b_f, m_meta, m_conv_w[0])
    vs = (v_norm_g, v_final_norm_g.reshape(1, D_MODEL), v_attn_norm_g, v_conv_norm_g, v_b_f, v_meta, v_conv_w[0])
    loss, g_s, d_s, m_s, v_s = _small_update(pack, packs, params, ms, vs)

    def ordered(small_list, big_in, big_out):
        s_norm, s_final, s_attn, s_convg, s_bf, s_meta, s_cw = small_list
        return (s_meta, s_norm, big_in, s_bf, s_cw[None], s_attn, s_convg, big_out, s_final.reshape(D_MODEL))

    return (loss.reshape(()), grad_x,
            *ordered(g_s, g_w_in, g_w_out[None]), *ordered(d_s, d_w_in, d_w_out),
            *ordered(m_s, nm_w_in, nm_w_out), *ordered(v_s, nv_w_in, nv_w_out))
```
